```python
import math
import jax, jax.numpy as jnp
from jax import lax
import numpy as np

D_MODEL = 1024
BATCH = 8
SEQ = 4096
DEPTH = 1

HEAD_DIM = 64
N_Q_HEADS = 8
N_KV_HEADS = 2
GQA_GROUP = N_Q_HEADS // N_KV_HEADS
ATTN_WIDTH = N_Q_HEADS * HEAD_DIM
KV_WIDTH = N_KV_HEADS * HEAD_DIM
WINDOW = 128
BLOCK = 128
POOL_WIDTH = D_MODEL - ATTN_WIDTH
POOL_SIZES = (2, 4, 8, 16)
N_POOL_GROUPS = len(POOL_SIZES)
POOL_GROUP_DIM = POOL_WIDTH // N_POOL_GROUPS
IN_WIDTH = ATTN_WIDTH + 2 * KV_WIDTH + POOL_WIDTH
MIX_WIDTH = ATTN_WIDTH + POOL_WIDTH
N_BUCKETS = 32
MAX_DISTANCE = 128
D_FF = -(-8 * D_MODEL // (3 * 256)) * 256
PLE_DIM = 256
EPS = 1e-6

kernel_name = "hymba_swa_sink_pool_hybrid"


def rmsnorm(x, g):
    xf = x.astype(jnp.float32)
    y = xf * lax.rsqrt(jnp.mean(xf * xf, axis=-1, keepdims=True) + EPS)
    return (y * g.astype(jnp.float32)).astype(x.dtype)


def t5_causal_bucket(dist):
    n = np.maximum(dist, 0)
    max_exact = N_BUCKETS // 2
    nf = np.maximum(n, 1).astype(np.float64)
    large = max_exact + (np.log(nf / max_exact) / math.log(MAX_DISTANCE / max_exact)
                         * (N_BUCKETS - max_exact)).astype(np.int64)
    large = np.minimum(large, N_BUCKETS - 1)
    return np.where(n < max_exact, n, large).astype(np.int32)


def sliding_window_sink_attention(q, k, v, sinks, rel_bias):
    B, S = q.shape[0], q.shape[1]
    nb = S // BLOCK
    qb = q.reshape(B, nb, BLOCK, N_KV_HEADS, GQA_GROUP, HEAD_DIM)
    kb = k.reshape(B, nb, BLOCK, N_KV_HEADS, HEAD_DIM)
    vb = v.reshape(B, nb, BLOCK, N_KV_HEADS, HEAD_DIM)
    pad = ((0, 0), (1, 0), (0, 0), (0, 0), (0, 0))
    k2 = jnp.concatenate([jnp.pad(kb, pad)[:, :-1], kb], axis=2)
    v2 = jnp.concatenate([jnp.pad(vb, pad)[:, :-1], vb], axis=2)

    logits = jnp.einsum('bnqhgd,bnkhd->bnhgqk', qb, k2).astype(jnp.float32) * (HEAD_DIM ** -0.5)

    i_idx = np.arange(BLOCK)[:, None]
    j_idx = np.arange(2 * BLOCK)[None, :]
    d = BLOCK + i_idx - j_idx
    bucket = t5_causal_bucket(d)
    bias = rel_bias.astype(jnp.float32)[bucket]
    bias = jnp.transpose(bias, (2, 0, 1)).reshape(N_KV_HEADS, GQA_GROUP, BLOCK, 2 * BLOCK)

    rel_ok = (d >= 0) & (d < WINDOW)
    blk_ok = (np.arange(nb)[:, None, None] > 0) | (j_idx[None] >= BLOCK)
    mask = jnp.asarray(rel_ok[None] & blk_ok)[None, :, None, None]

    logits = jnp.where(mask, logits + bias, -jnp.inf)
    sink = sinks.astype(jnp.float32).reshape(N_KV_HEADS, GQA_GROUP)[:, :, None, None]
    m = jnp.maximum(jnp.max(logits, axis=-1, keepdims=True), sink)
    pexp = jnp.exp(logits - m)
    probs = pexp / (jnp.sum(pexp, axis=-1, keepdims=True) + jnp.exp(sink - m))
    out = jnp.einsum('bnhgqk,bnkhd->bnqhgd', probs.astype(v.dtype), v2)
    return out.reshape(B, S, N_Q_HEADS * HEAD_DIM)


def multiscale_pool_mixer(u, w_pool, pool_scale):
    B, S = u.shape[0], u.shape[1]
    ug = u.reshape(B, S, N_POOL_GROUPS, POOL_GROUP_DIM).astype(jnp.float32)
    cs = jnp.cumsum(ug, axis=1)
    t1 = jnp.arange(1, S + 1, dtype=jnp.float32)
    outs = []
    for g, w in enumerate(POOL_SIZES):
        c = cs[:, :, g]
        lag = jnp.pad(c, ((0, 0), (w, 0), (0, 0)))[:, :S]
        cnt = jnp.minimum(t1, float(w))[None, :, None]
        outs.append((c - lag) / cnt - ug[:, :, g])
    pooled = jnp.stack(outs, axis=2)
    y = jnp.einsum('bsgc,gcd->bsgd', pooled, w_pool.astype(jnp.float32))
    y = y.reshape(B, S, POOL_WIDTH) * pool_scale.astype(jnp.float32)
    return y.astype(u.dtype)


def _fwd_setup_inputs(seed: int = 0) -> dict:
    key = jax.random.key(seed)
    ks = jax.random.split(key, 20)
    f32 = jnp.float32
    nrm = lambda k, shape, s: jax.random.normal(k, shape, f32) * s
    return {
        "x": nrm(ks[0], (BATCH, SEQ, D_MODEL), 1.0),
        "p": nrm(ks[1], (DEPTH, BATCH, SEQ, PLE_DIM), 1.0),
        "w_in": nrm(ks[2], (DEPTH, D_MODEL, IN_WIDTH), D_MODEL ** -0.5),
        "w_out": nrm(ks[3], (DEPTH, MIX_WIDTH, D_MODEL), MIX_WIDTH ** -0.5),
        "g_attn_norm": 1.0 + nrm(ks[4], (DEPTH, D_MODEL), 0.05),
        "g_q": 1.0 + nrm(ks[5], (DEPTH, HEAD_DIM), 0.05),
        "g_k": 1.0 + nrm(ks[6], (DEPTH, HEAD_DIM), 0.05),
        "attn_sinks": nrm(ks[7], (DEPTH, N_Q_HEADS), 1.0),
        "rel_bias": nrm(ks[8], (N_BUCKETS, N_Q_HEADS), 0.3),
        "w_pool": nrm(ks[9], (DEPTH, N_POOL_GROUPS, POOL_GROUP_DIM, POOL_GROUP_DIM), POOL_GROUP_DIM ** -0.5),
        "pool_scale": 1.0 + nrm(ks[10], (DEPTH, POOL_WIDTH), 0.1),
        "g_ffn_norm": 1.0 + nrm(ks[11], (DEPTH, D_MODEL), 0.05),
        "w_gate": nrm(ks[12], (DEPTH, D_MODEL, D_FF), D_MODEL ** -0.5),
        "w_up": nrm(ks[13], (DEPTH, D_MODEL, D_FF), D_MODEL ** -0.5),
        "w_down": nrm(ks[14], (DEPTH, D_FF, D_MODEL), D_FF ** -0.5),
        "g_ple_norm": 1.0 + nrm(ks[15], (DEPTH, D_MODEL), 0.05),
        "w_ple_gate": nrm(ks[16], (DEPTH, D_MODEL, D_MODEL), D_MODEL ** -0.5),
        "w_ple_proj": nrm(ks[17], (DEPTH, PLE_DIM, D_MODEL), PLE_DIM ** -0.5),
    }


def _fwd_reference(x, p, w_in, w_out, g_attn_norm, g_q, g_k, attn_sinks, rel_bias, w_pool, pool_scale,
              g_ffn_norm, w_gate, w_up, w_down, g_ple_norm, w_ple_gate, w_ple_proj):
    B, S = x.shape[0], x.shape[1]
    h = x
    for i in range(DEPTH):
        hn = rmsnorm(h, g_attn_norm[i])
        z = hn @ w_in[i]
        q = z[..., :ATTN_WIDTH]
        k = z[..., ATTN_WIDTH:ATTN_WIDTH + KV_WIDTH]
        v = z[..., ATTN_WIDTH + KV_WIDTH:ATTN_WIDTH + 2 * KV_WIDTH]
        u = z[..., ATTN_WIDTH + 2 * KV_WIDTH:]
        q = rmsnorm(q.reshape(B, S, N_Q_HEADS, HEAD_DIM), g_q[i])
        k = rmsnorm(k.reshape(B, S, N_KV_HEADS, HEAD_DIM), g_k[i])
        v = v.reshape(B, S, N_KV_HEADS, HEAD_DIM)
        a = sliding_window_sink_attention(q, k, v, attn_sinks[i], rel_bias)
        m = multiscale_pool_mixer(u, w_pool[i], pool_scale[i])
        h = h + jnp.concatenate([a, m], axis=-1) @ w_out[i]
        hn = rmsnorm(h, g_ffn_norm[i])
        h = h + (jax.nn.silu(hn @ w_gate[i]) * (hn @ w_up[i])) @ w_down[i]
        gate = jax.nn.sigmoid(rmsnorm(h, g_ple_norm[i]) @ w_ple_gate[i])
        h = h + gate * (p[i] @ w_ple_proj[i])
    return h


import jax as _jax
import jax.numpy as _jnp

TWIN_FORMAT = 'train_step'
FWD_PARAMS = ['x', 'p', 'w_in', 'w_out', 'g_attn_norm', 'g_q', 'g_k', 'attn_sinks', 'rel_bias', 'w_pool', 'pool_scale', 'g_ffn_norm', 'w_gate', 'w_up', 'w_down', 'g_ple_norm', 'w_ple_gate', 'w_ple_proj']
TWIN_WEIGHTS = ['w_in', 'w_out', 'g_attn_norm', 'g_q', 'g_k', 'attn_sinks', 'rel_bias', 'w_pool', 'pool_scale', 'g_ffn_norm', 'w_gate', 'w_up', 'w_down', 'g_ple_norm', 'w_ple_gate', 'w_ple_proj']
TWIN_DIFF_INPUT = 'x'
TWIN_INPUTS = ['x', 'p', 'w_in', 'w_out', 'g_attn_norm', 'g_q', 'g_k', 'attn_sinks', 'rel_bias', 'w_pool', 'pool_scale', 'g_ffn_norm', 'w_gate', 'w_up', 'w_down', 'g_ple_norm', 'w_ple_gate', 'w_ple_proj', 'loss_target', 'm_w_in', 'm_w_out', 'm_g_attn_norm', 'm_g_q', 'm_g_k', 'm_attn_sinks', 'm_rel_bias', 'm_w_pool', 'm_pool_scale', 'm_g_ffn_norm', 'm_w_gate', 'm_w_up', 'm_w_down', 'm_g_ple_norm', 'm_w_ple_gate', 'm_w_ple_proj', 'v_w_in', 'v_w_out', 'v_g_attn_norm', 'v_g_q', 'v_g_k', 'v_attn_sinks', 'v_rel_bias', 'v_w_pool', 'v_pool_scale', 'v_g_ffn_norm', 'v_w_gate', 'v_w_up', 'v_w_down', 'v_g_ple_norm', 'v_w_ple_gate', 'v_w_ple_proj']
TWIN_OUTPUTS = ['loss', 'grad_x', 'grad_w_in', 'grad_w_out', 'grad_g_attn_norm', 'grad_g_q', 'grad_g_k', 'grad_attn_sinks', 'grad_rel_bias', 'grad_w_pool', 'grad_pool_scale', 'grad_g_ffn_norm', 'grad_w_gate', 'grad_w_up', 'grad_w_down', 'grad_g_ple_norm', 'grad_w_ple_gate', 'grad_w_ple_proj', 'delta_w_in', 'delta_w_out', 'delta_g_attn_norm', 'delta_g_q', 'delta_g_k', 'delta_attn_sinks', 'delta_rel_bias', 'delta_w_pool', 'delta_pool_scale', 'delta_g_ffn_norm', 'delta_w_gate', 'delta_w_up', 'delta_w_down', 'delta_g_ple_norm', 'delta_w_ple_gate', 'delta_w_ple_proj', 'new_m_w_in', 'new_m_w_out', 'new_m_g_attn_norm', 'new_m_g_q', 'new_m_g_k', 'new_m_attn_sinks', 'new_m_rel_bias', 'new_m_w_pool', 'new_m_pool_scale', 'new_m_g_ffn_norm', 'new_m_w_gate', 'new_m_w_up', 'new_m_w_down', 'new_m_g_ple_norm', 'new_m_w_ple_gate', 'new_m_w_ple_proj', 'new_v_w_in', 'new_v_w_out', 'new_v_g_attn_norm', 'new_v_g_q', 'new_v_g_k', 'new_v_attn_sinks', 'new_v_rel_bias', 'new_v_w_pool', 'new_v_pool_scale', 'new_v_g_ffn_norm', 'new_v_w_gate', 'new_v_w_up', 'new_v_w_down', 'new_v_g_ple_norm', 'new_v_w_ple_gate', 'new_v_w_ple_proj']
TWIN_LEAF_KINDS = {'loss': 'loss', 'grad_x': 'grad_x', 'grad_w_in': 'grad_w', 'grad_w_out': 'grad_w', 'grad_g_attn_norm': 'grad_w', 'grad_g_q': 'grad_w', 'grad_g_k': 'grad_w', 'grad_attn_sinks': 'grad_w', 'grad_rel_bias': 'grad_w', 'grad_w_pool': 'grad_w', 'grad_pool_scale': 'grad_w', 'grad_g_ffn_norm': 'grad_w', 'grad_w_gate': 'grad_w', 'grad_w_up': 'grad_w', 'grad_w_down': 'grad_w', 'grad_g_ple_norm': 'grad_w', 'grad_w_ple_gate': 'grad_w', 'grad_w_ple_proj': 'grad_w', 'delta_w_in': 'delta_w', 'delta_w_out': 'delta_w', 'delta_g_attn_norm': 'delta_w', 'delta_g_q': 'delta_w', 'delta_g_k': 'delta_w', 'delta_attn_sinks': 'delta_w', 'delta_rel_bias': 'delta_w', 'delta_w_pool': 'delta_w', 'delta_pool_scale': 'delta_w', 'delta_g_ffn_norm': 'delta_w', 'delta_w_gate': 'delta_w', 'delta_w_up': 'delta_w', 'delta_w_down': 'delta_w', 'delta_g_ple_norm': 'delta_w', 'delta_w_ple_gate': 'delta_w', 'delta_w_ple_proj': 'delta_w', 'new_m_w_in': 'new_m', 'new_m_w_out': 'new_m', 'new_m_g_attn_norm': 'new_m', 'new_m_g_q': 'new_m', 'new_m_g_k': 'new_m', 'new_m_attn_sinks': 'new_m', 'new_m_rel_bias': 'new_m', 'new_m_w_pool': 'new_m', 'new_m_pool_scale': 'new_m', 'new_m_g_ffn_norm': 'new_m', 'new_m_w_gate': 'new_m', 'new_m_w_up': 'new_m', 'new_m_w_down': 'new_m', 'new_m_g_ple_norm': 'new_m', 'new_m_w_ple_gate': 'new_m', 'new_m_w_ple_proj': 'new_m', 'new_v_w_in': 'new_v', 'new_v_w_out': 'new_v', 'new_v_g_attn_norm': 'new_v', 'new_v_g_q': 'new_v', 'new_v_g_k': 'new_v', 'new_v_attn_sinks': 'new_v', 'new_v_rel_bias': 'new_v', 'new_v_w_pool': 'new_v', 'new_v_pool_scale': 'new_v', 'new_v_g_ffn_norm': 'new_v', 'new_v_w_gate': 'new_v', 'new_v_w_up': 'new_v', 'new_v_w_down': 'new_v', 'new_v_g_ple_norm': 'new_v', 'new_v_w_ple_gate': 'new_v', 'new_v_w_ple_proj': 'new_v'}


def _forward(args):
    return _fwd_reference(*[args[k] for k in FWD_PARAMS])


def _output_shape():
    def fwd():
        inp = _fwd_setup_inputs(0)
        return _fwd_reference(*[inp[k] for k in FWD_PARAMS])
    out = _jax.eval_shape(fwd)
    return out.shape, out.dtype

N_MICROBATCH = 1
ADAM_LR = 0.001
ADAM_B1 = 0.9
ADAM_B2 = 0.999
ADAM_EPS = 1e-08
ADAM_WD = 0.01
ADAM_STEP = 10
PER_EXAMPLE_BATCH_AXIS = {'x': 0, 'p': 1, 'loss_target': 0}
SHARED_INPUTS = []
_WEIGHT_DTYPES = {'w_in': _jnp.float32, 'w_out': _jnp.float32, 'g_attn_norm': _jnp.float32, 'g_q': _jnp.float32, 'g_k': _jnp.float32, 'attn_sinks': _jnp.float32, 'rel_bias': _jnp.float32, 'w_pool': _jnp.float32, 'pool_scale': _jnp.float32, 'g_ffn_norm': _jnp.float32, 'w_gate': _jnp.float32, 'w_up': _jnp.float32, 'w_down': _jnp.float32, 'g_ple_norm': _jnp.float32, 'w_ple_gate': _jnp.float32, 'w_ple_proj': _jnp.float32}
MOMENT_SCALE = {'w_in': 8.240830e-01, 'w_out': 9.703458e-01, 'g_attn_norm': 1.259949e+01, 'g_q': 3.754239e+00, 'g_k': 3.797104e+00, 'attn_sinks': 8.725786e-01, 'rel_bias': 4.067900e-01, 'w_pool': 2.588794e+00, 'pool_scale': 2.522482e+01, 'g_ffn_norm': 2.493619e+01, 'w_gate': 1.993139e-01, 'w_up': 2.355658e-01, 'w_down': 3.511054e-01, 'g_ple_norm': 9.664116e-01, 'w_ple_gate': 8.341017e-02, 'w_ple_proj': 5.116382e-01}


def _to_microbatches(a, axis):
    t = _jnp.moveaxis(a, axis, 0)
    t = t.reshape((N_MICROBATCH, t.shape[0] // N_MICROBATCH) + t.shape[1:])
    return _jnp.moveaxis(t, 1, axis + 1)


def setup_inputs(seed: int = 0) -> dict:
    inp = _fwd_setup_inputs(seed)
    key = _jax.random.fold_in(_jax.random.key(seed), 7919)
    shape, _ = _output_shape()
    out = dict(inp)
    out["loss_target"] = _jax.random.normal(_jax.random.fold_in(key, 0), shape, _jnp.float32)
    for i, name in enumerate(TWIN_WEIGHTS):
        w = inp[name].astype(_jnp.float32)
        if MOMENT_SCALE is None:
            s = _jnp.sqrt(_jnp.mean(_jnp.square(w)) + 1e-30)
        else:
            s = MOMENT_SCALE[name]
        km, kv = _jax.random.split(_jax.random.fold_in(key, i + 1))
        out[name] = w
        out["m_" + name] = s * _jax.random.normal(km, w.shape, _jnp.float32)
        out["v_" + name] = (s * s) * _jax.random.uniform(kv, w.shape, _jnp.float32, 0.5, 1.5)
    if N_MICROBATCH > 1:
        for name, axis in PER_EXAMPLE_BATCH_AXIS.items():
            out[name] = _to_microbatches(out[name], axis)
    return {'x': out['x'], 'p': out['p'], 'w_in': out['w_in'], 'w_out': out['w_out'], 'g_attn_norm': out['g_attn_norm'], 'g_q': out['g_q'], 'g_k': out['g_k'], 'attn_sinks': out['attn_sinks'], 'rel_bias': out['rel_bias'], 'w_pool': out['w_pool'], 'pool_scale': out['pool_scale'], 'g_ffn_norm': out['g_ffn_norm'], 'w_gate': out['w_gate'], 'w_up': out['w_up'], 'w_down': out['w_down'], 'g_ple_norm': out['g_ple_norm'], 'w_ple_gate': out['w_ple_gate'], 'w_ple_proj': out['w_ple_proj'], 'loss_target': out['loss_target'], 'm_w_in': out['m_w_in'], 'm_w_out': out['m_w_out'], 'm_g_attn_norm': out['m_g_attn_norm'], 'm_g_q': out['m_g_q'], 'm_g_k': out['m_g_k'], 'm_attn_sinks': out['m_attn_sinks'], 'm_rel_bias': out['m_rel_bias'], 'm_w_pool': out['m_w_pool'], 'm_pool_scale': out['m_pool_scale'], 'm_g_ffn_norm': out['m_g_ffn_norm'], 'm_w_gate': out['m_w_gate'], 'm_w_up': out['m_w_up'], 'm_w_down': out['m_w_down'], 'm_g_ple_norm': out['m_g_ple_norm'], 'm_w_ple_gate': out['m_w_ple_gate'], 'm_w_ple_proj': out['m_w_ple_proj'], 'v_w_in': out['v_w_in'], 'v_w_out': out['v_w_out'], 'v_g_attn_norm': out['v_g_attn_norm'], 'v_g_q': out['v_g_q'], 'v_g_k': out['v_g_k'], 'v_attn_sinks': out['v_attn_sinks'], 'v_rel_bias': out['v_rel_bias'], 'v_w_pool': out['v_w_pool'], 'v_pool_scale': out['v_pool_scale'], 'v_g_ffn_norm': out['v_g_ffn_norm'], 'v_w_gate': out['v_w_gate'], 'v_w_up': out['v_w_up'], 'v_w_down': out['v_w_down'], 'v_g_ple_norm': out['v_g_ple_norm'], 'v_w_ple_gate': out['v_w_ple_gate'], 'v_w_ple_proj': out['v_w_ple_proj']}


def _loss(weights, diff, rest, loss_target):
    with _jax.named_scope("forward"):
        args = {**rest, TWIN_DIFF_INPUT: diff, **{k: w.astype(_WEIGHT_DTYPES[k]) for k, w in weights.items()}}
        y = _forward(args)
    with _jax.named_scope("loss_head"):
        err = _jnp.square(y.astype(_jnp.float32) - loss_target)
        return 0.5 * _jnp.sum(_jnp.mean(err, axis=-1)) if err.ndim else 0.5 * err


def _adamw(w, g, m, v):
    m = ADAM_B1 * m + (1.0 - ADAM_B1) * g
    v = ADAM_B2 * v + (1.0 - ADAM_B2) * _jnp.square(g)
    m_hat = m / (1.0 - ADAM_B1 ** ADAM_STEP)
    v_hat = v / (1.0 - ADAM_B2 ** ADAM_STEP)
    delta = -ADAM_LR * (m_hat / (_jnp.sqrt(v_hat) + ADAM_EPS) + ADAM_WD * w)
    return delta, m, v


def reference(x, p, w_in, w_out, g_attn_norm, g_q, g_k, attn_sinks, rel_bias, w_pool, pool_scale, g_ffn_norm, w_gate, w_up, w_down, g_ple_norm, w_ple_gate, w_ple_proj, loss_target, m_w_in, m_w_out, m_g_attn_norm, m_g_q, m_g_k, m_attn_sinks, m_rel_bias, m_w_pool, m_pool_scale, m_g_ffn_norm, m_w_gate, m_w_up, m_w_down, m_g_ple_norm, m_w_ple_gate, m_w_ple_proj, v_w_in, v_w_out, v_g_attn_norm, v_g_q, v_g_k, v_attn_sinks, v_rel_bias, v_w_pool, v_pool_scale, v_g_ffn_norm, v_w_gate, v_w_up, v_w_down, v_g_ple_norm, v_w_ple_gate, v_w_ple_proj):
    given = dict(x=x, p=p, w_in=w_in, w_out=w_out, g_attn_norm=g_attn_norm, g_q=g_q, g_k=g_k, attn_sinks=attn_sinks, rel_bias=rel_bias, w_pool=w_pool, pool_scale=pool_scale, g_ffn_norm=g_ffn_norm, w_gate=w_gate, w_up=w_up, w_down=w_down, g_ple_norm=g_ple_norm, w_ple_gate=w_ple_gate, w_ple_proj=w_ple_proj, loss_target=loss_target, m_w_in=m_w_in, m_w_out=m_w_out, m_g_attn_norm=m_g_attn_norm, m_g_q=m_g_q, m_g_k=m_g_k, m_attn_sinks=m_attn_sinks, m_rel_bias=m_rel_bias, m_w_pool=m_w_pool, m_pool_scale=m_pool_scale, m_g_ffn_norm=m_g_ffn_norm, m_w_gate=m_w_gate, m_w_up=m_w_up, m_w_down=m_w_down, m_g_ple_norm=m_g_ple_norm, m_w_ple_gate=m_w_ple_gate, m_w_ple_proj=m_w_ple_proj, v_w_in=v_w_in, v_w_out=v_w_out, v_g_attn_norm=v_g_attn_norm, v_g_q=v_g_q, v_g_k=v_g_k, v_attn_sinks=v_attn_sinks, v_rel_bias=v_rel_bias, v_w_pool=v_w_pool, v_pool_scale=v_pool_scale, v_g_ffn_norm=v_g_ffn_norm, v_w_gate=v_w_gate, v_w_up=v_w_up, v_w_down=v_w_down, v_g_ple_norm=v_g_ple_norm, v_w_ple_gate=v_w_ple_gate, v_w_ple_proj=v_w_ple_proj)
    weights = {n: given[n] for n in TWIN_WEIGHTS}
    shared = {n: given[n] for n in SHARED_INPUTS}
    per_example = {n: given[n] for n in ['x', 'p']}
    grad_fn = _jax.value_and_grad(_loss, argnums=(0, 1))

    def one_microbatch(ex, loss_target):
        ex = dict(ex)
        diff = ex.pop(TWIN_DIFF_INPUT)
        return grad_fn(weights, diff, {**shared, **ex}, loss_target)

    if N_MICROBATCH == 1:
        loss, (grad_w, grad_x) = one_microbatch(per_example, given["loss_target"])
    else:
        def body(carry, xs):
            loss_sum, grad_sum = carry
            l_k, (gw_k, gx_k) = one_microbatch(xs[0], xs[1])
            with _jax.named_scope("update"):
                return (loss_sum + l_k, _jax.tree.map(_jnp.add, grad_sum, gw_k)), gx_k

        init = (_jnp.zeros((), _jnp.float32), _jax.tree.map(_jnp.zeros_like, weights))
        (loss, grad_w), grad_x = _jax.lax.scan(body, init, (per_example, given["loss_target"]))
    with _jax.named_scope("update"):
        delta_w, new_m, new_v = {}, {}, {}
        for n in TWIN_WEIGHTS:
            delta_w[n], new_m[n], new_v[n] = _adamw(weights[n], grad_w[n], given["m_" + n], given["v_" + n])
    return (loss, grad_x, *[grad_w[n] for n in TWIN_WEIGHTS], *[delta_w[n] for n in TWIN_WEIGHTS],
            *[new_m[n] for n in TWIN_WEIGHTS], *[new_v[n] for n in TWIN_WEIGHTS])
```

```python
import functools
import math

import jax
import jax.numpy as jnp
import numpy as np
from jax import lax
from jax.experimental import pallas as pl
from jax.experimental.pallas import tpu as pltpu

F32 = jnp.float32
BF16 = jnp.bfloat16
MESH = pl.DeviceIdType.MESH

D_MODEL = 1024
HEAD_DIM = 64
ATTN_WIDTH = 512
KV_WIDTH = 128
POOL_WIDTH = 512
POOL_SIZES = (2, 4, 8, 16)
POOL_GROUP = 128
POOL_HALO = 16
IN_WIDTH = 1280
D_FF = 2816
PLE_DIM = 256
BLOCK = 128
N_BUCKETS = 32
MAX_DISTANCE = 128
EPS = 1e-6
N_DEV = 8
N_CHIPS = 4

ADAM_LR = 0.001
ADAM_B1 = 0.9
ADAM_B2 = 0.999
ADAM_EPS = 1e-08
ADAM_WD = 0.01
ADAM_STEP = 10

TOKEN_TILE = 512
FFN_TOKEN_TILE = 256
FF_CHUNK = 256
HEADS_A = (0, 2, 5, 7)
HEADS_B = (1, 3, 4, 6)
SMALL_LANES = 128


def _nn(a, b):
    return jnp.dot(a, b, preferred_element_type=F32)


def _nt(a, b):
    return lax.dot_general(a, b, (((1,), (1,)), ((), ())), preferred_element_type=F32)


def _tn(a, b):
    return lax.dot_general(a, b, (((0,), (0,)), ((), ())), preferred_element_type=F32)


def _resident(shape):
    nd = len(shape)
    return pl.BlockSpec(shape, lambda i, _nd=nd: (0,) * _nd, pipeline_mode=pl.Buffered(1))


def _rows(tile, width):
    return pl.BlockSpec((tile, width), lambda i: (i, 0))


def _acc(shape):
    nd = len(shape)
    return pl.BlockSpec(shape, lambda i, _nd=nd: (0,) * _nd)


def _head_mean_matrix(width):
    idx = np.arange(width) // HEAD_DIM
    return jnp.asarray((idx[:, None] == idx[None, :]).astype(np.float32) / HEAD_DIM, dtype=BF16)


def _seg_mean(v, bmat):
    hi = v.astype(BF16)
    lo = (v - hi.astype(F32)).astype(BF16)
    return _nn(hi, bmat) + _nn(lo, bmat)


def _rms(x):
    return lax.rsqrt(jnp.mean(x * x, axis=-1, keepdims=True) + EPS)


def _rms_bwd(d_y, x, r, g):
    gy = d_y * g
    d_x = r * gy - x * (r * r * r) * jnp.mean(gy * x, axis=-1, keepdims=True)
    d_g = jnp.sum(d_y * (x * r), axis=0, keepdims=True)
    return d_x, d_g


def _lane_lo(shape):
    return lax.broadcasted_iota(jnp.int32, shape, 1) < HEAD_DIM


def _in_proj(x, g_attn, w_in_t, gq_t, gk_t):
    s = x.shape[0]
    ts = min(TOKEN_TILE, s)

    def body(x_ref, g_ref, w_ref, gq_ref, gk_ref, bq_ref, bk_ref, zqk_ref, qn_ref, kn_ref, v_ref, u_ref):
        xf = x_ref[...]
        hn = ((xf * _rms(xf)) * g_ref[...]).astype(BF16)
        z = _nt(hn, w_ref[...])
        q = z[:, :ATTN_WIDTH]
        k = z[:, ATTN_WIDTH:ATTN_WIDTH + KV_WIDTH]
        zqk_ref[...] = z[:, :ATTN_WIDTH + KV_WIDTH]
        rq = lax.rsqrt(_seg_mean(q * q, bq_ref[...]) + EPS)
        qn_ref[...] = ((q * rq) * gq_ref[...]).astype(BF16)
        rk = lax.rsqrt(_seg_mean(k * k, bk_ref[...]) + EPS)
        kn_ref[...] = ((k * rk) * gk_ref[...]).astype(BF16)
        v_ref[...] = z[:, ATTN_WIDTH + KV_WIDTH:ATTN_WIDTH + 2 * KV_WIDTH].astype(BF16)
        u_ref[...] = z[:, ATTN_WIDTH + 2 * KV_WIDTH:]

    return pl.pallas_call(
        body,
        name="in_proj",
        grid=(s // ts,),
        in_specs=[
            _rows(ts, D_MODEL),
            _resident((1, D_MODEL)),
            _resident((IN_WIDTH, D_MODEL)),
            _resident((1, ATTN_WIDTH)),
            _resident((1, KV_WIDTH)),
            _resident((ATTN_WIDTH, ATTN_WIDTH)),
            _resident((KV_WIDTH, KV_WIDTH)),
        ],
        out_specs=[
            _rows(ts, ATTN_WIDTH + KV_WIDTH),
            _rows(ts, ATTN_WIDTH),
            _rows(ts, KV_WIDTH),
            _rows(ts, KV_WIDTH),
            _rows(ts, POOL_WIDTH),
        ],
        out_shape=[
            jax.ShapeDtypeStruct((s, ATTN_WIDTH + KV_WIDTH), F32),
            jax.ShapeDtypeStruct((s, ATTN_WIDTH), BF16),
            jax.ShapeDtypeStruct((s, KV_WIDTH), BF16),
            jax.ShapeDtypeStruct((s, KV_WIDTH), BF16),
            jax.ShapeDtypeStruct((s, POOL_WIDTH), F32),
        ],
    )(x, g_attn, w_in_t, gq_t, gk_t, _head_mean_matrix(ATTN_WIDTH), _head_mean_matrix(KV_WIDTH))


def _bucket_ranges():
    n = np.arange(MAX_DISTANCE)
    max_exact = N_BUCKETS // 2
    nf = np.maximum(n, 1).astype(np.float64)
    large = max_exact + (np.log(nf / max_exact) / math.log(MAX_DISTANCE / max_exact) * (N_BUCKETS - max_exact)).astype(np.int64)
    bucket = np.where(n < max_exact, n, np.minimum(large, N_BUCKETS - 1))
    out = []
    for b in range(N_BUCKETS):
        idx = np.nonzero(bucket == b)[0]
        out.append((int(idx.min()), int(idx.max()) + 1))
    return out


def _band_distance():
    i = lax.broadcasted_iota(jnp.int32, (BLOCK, 2 * BLOCK), 0)
    j = lax.broadcasted_iota(jnp.int32, (BLOCK, 2 * BLOCK), 1)
    return BLOCK + i - j


def _bias_table(rel_bias):
    ranges = _bucket_ranges()

    def body(rb_ref, tab_ref):
        d = _band_distance()
        for half, heads in enumerate((HEADS_A, HEADS_B)):
            for slot, h in enumerate(heads):
                t = jnp.full((BLOCK, 2 * BLOCK), -jnp.inf, F32)
                for b, (lo, hi) in enumerate(ranges):
                    t = jnp.where((d >= lo) & (d < hi), rb_ref[b, h], t)
                tab_ref[half, slot * BLOCK:(slot + 1) * BLOCK, :] = t

    return pl.pallas_call(
        body,
        name="bias_table",
        in_specs=[pl.BlockSpec(memory_space=pltpu.SMEM)],
        out_shape=jax.ShapeDtypeStruct((2, 4 * BLOCK, 2 * BLOCK), F32),
    )(rel_bias)


def _bias_table_bwd(dl_acc):
    ranges = _bucket_ranges()

    def body(dl_ref, out_ref):
        d = _band_distance()
        row = lax.broadcasted_iota(jnp.int32, (N_BUCKETS, SMALL_LANES), 0)
        lane = lax.broadcasted_iota(jnp.int32, (N_BUCKETS, SMALL_LANES), 1)
        out = jnp.zeros((N_BUCKETS, SMALL_LANES), F32)
        for half, heads in enumerate((HEADS_A, HEADS_B)):
            for slot, h in enumerate(heads):
                g = dl_ref[half, slot * BLOCK:(slot + 1) * BLOCK, :]
                for b, (lo, hi) in enumerate(ranges):
                    part = jnp.sum(jnp.where((d >= lo) & (d < hi), g, 0.0), axis=1, keepdims=True)
                    tot = jnp.sum(part, axis=0, keepdims=True)
                    out = jnp.where((row == b) & (lane == h), tot, out)
        out_ref[...] = out

    return pl.pallas_call(
        body,
        name="bias_table_bwd",
        out_shape=jax.ShapeDtypeStruct((N_BUCKETS, SMALL_LANES), F32),
    )(dl_acc)


def _stack_heads(pairs, lo_mask):
    zero = jnp.zeros_like(pairs[0])
    lo = [jnp.where(lo_mask, t, zero) for t in pairs]
    hi = [jnp.where(lo_mask, zero, t) for t in pairs]
    return (jnp.concatenate([lo[0], lo[1], hi[2], hi[3]], axis=0),
            jnp.concatenate([hi[0], hi[1], lo[2], lo[3]], axis=0))


def _unstack_heads(out_a, out_b, lo_mask):
    t = lambda x, r: x[r * BLOCK:(r + 1) * BLOCK, :]
    return [
        jnp.where(lo_mask, t(out_a, 0), t(out_b, 0)),
        jnp.where(lo_mask, t(out_a, 1), t(out_b, 1)),
        jnp.where(lo_mask, t(out_b, 2), t(out_a, 2)),
        jnp.where(lo_mask, t(out_b, 3), t(out_a, 3)),
    ]


def _sink_column(sink_ref, heads):
    row = lax.broadcasted_iota(jnp.int32, (4 * BLOCK, 1), 0)
    col = jnp.full((4 * BLOCK, 1), sink_ref[0, heads[3]], F32)
    for slot in (2, 1, 0):
        col = jnp.where(row < (slot + 1) * BLOCK, sink_ref[0, heads[slot]], col)
    return col


def _band_probs(q_stack, keys, tab, sink, first_block):
    s = _nt(q_stack, keys) * (HEAD_DIM ** -0.5) + tab
    col = lax.broadcasted_iota(jnp.int32, s.shape, 1)
    s = jnp.where(jnp.logical_and(first_block, col < BLOCK), -jnp.inf, s)
    m = jnp.maximum(jnp.max(s, axis=-1, keepdims=True), sink)
    e = jnp.exp(s - m)
    e_sink = jnp.exp(sink - m)
    den = jnp.sum(e, axis=-1, keepdims=True) + e_sink
    return e / den, e_sink / den


def _attn_specs(nb):
    cur = lambda n: (jnp.minimum(n, nb - 1), 0)
    prev = lambda n: (jnp.maximum(jnp.minimum(n, nb - 1) - 1, 0), 0)
    return cur, prev


def _attn_fwd(qn, kn, v, tab, sinks):
    s = qn.shape[0]
    nb = s // BLOCK
    cur, prev = _attn_specs(nb)

    def body(sink_ref, q_ref, kc_ref, kp_ref, vc_ref, vp_ref, tab_ref, o_ref):
        first = pl.program_id(0) == 0
        lo_mask = _lane_lo((BLOCK, BLOCK))
        kk = jnp.concatenate([kp_ref[...], kc_ref[...]], axis=0)
        vv = jnp.concatenate([vp_ref[...], vc_ref[...]], axis=0)
        kk_sw = pltpu.roll(kk, HEAD_DIM, 1)
        vv_sw = pltpu.roll(vv, HEAD_DIM, 1)
        q_a, q_b = _stack_heads([q_ref[:, p * BLOCK:(p + 1) * BLOCK] for p in range(4)], lo_mask)
        p_a, _ = _band_probs(q_a, kk, tab_ref[0], _sink_column(sink_ref, HEADS_A), first)
        p_b, _ = _band_probs(q_b, kk_sw, tab_ref[1], _sink_column(sink_ref, HEADS_B), first)
        out = _unstack_heads(_nn(p_a.astype(BF16), vv), _nn(p_b.astype(BF16), vv_sw), lo_mask)
        for p in range(4):
            o_ref[:, p * BLOCK:(p + 1) * BLOCK] = out[p].astype(BF16)

    return pl.pallas_call(
        body,
        name="attn_fwd",
        grid=(nb,),
        in_specs=[
            pl.BlockSpec(memory_space=pltpu.SMEM),
            pl.BlockSpec((BLOCK, ATTN_WIDTH), cur),
            pl.BlockSpec((BLOCK, KV_WIDTH), cur),
            pl.BlockSpec((BLOCK, KV_WIDTH), prev),
            pl.BlockSpec((BLOCK, KV_WIDTH), cur),
            pl.BlockSpec((BLOCK, KV_WIDTH), prev),
            _resident((2, 4 * BLOCK, 2 * BLOCK)),
        ],
        out_specs=pl.BlockSpec((BLOCK, ATTN_WIDTH), cur),
        out_shape=jax.ShapeDtypeStruct((s, ATTN_WIDTH), BF16),
    )(sinks, qn, kn, kn, v, v, tab)


def _pooled(u_tile, u_halo, tile_index, tile_rows):
    halo = jnp.where(tile_index > 0, u_halo, 0.0)
    ext = jnp.concatenate([halo, u_tile], axis=0)
    sums = []
    acc = ext
    for shift in (1, 2, 4, 8):
        acc = acc + pltpu.roll(acc, shift, 0)
        sums.append(acc)
    t = tile_index * tile_rows + lax.broadcasted_iota(jnp.int32, (tile_rows, 1), 0)
    out = []
    for g, w in enumerate(POOL_SIZES):
        lanes = slice(g * POOL_GROUP, (g + 1) * POOL_GROUP)
        cnt = jnp.minimum(t + 1, w).astype(F32)
        out.append(sums[g][POOL_HALO:, lanes] / cnt - u_tile[:, lanes])
    return out


def _halo_before(tile):
    return lambda i: (jnp.maximum(i * (tile // POOL_HALO) - 1, 0), 0)


def _mix_out(u, a, x, w_out, w_pool, pool_scale, g_ffn):
    s = x.shape[0]
    ts = min(TOKEN_TILE, s)

    def body(u_ref, uh_ref, a_ref, x_ref, wo_ref, wp_ref, sc_ref, g_ref, h1_ref, hn_ref, m_ref):
        i = pl.program_id(0)
        pooled = _pooled(u_ref[...], uh_ref[...], i, ts)
        for g in range(len(POOL_SIZES)):
            lanes = slice(g * POOL_GROUP, (g + 1) * POOL_GROUP)
            y = _nn(pooled[g].astype(BF16), wp_ref[g].astype(BF16))
            m_ref[:, lanes] = (y * sc_ref[:, lanes]).astype(BF16)
        h1 = x_ref[...] + _nn(a_ref[...], wo_ref[:ATTN_WIDTH, :]) + _nn(m_ref[...], wo_ref[ATTN_WIDTH:, :])
        h1_ref[...] = h1
        hn_ref[...] = ((h1 * _rms(h1)) * g_ref[...]).astype(BF16)

    return pl.pallas_call(
        body,
        name="mix_out",
        grid=(s // ts,),
        in_specs=[
            _rows(ts, POOL_WIDTH),
            pl.BlockSpec((POOL_HALO, POOL_WIDTH), _halo_before(ts)),
            _rows(ts, ATTN_WIDTH),
            _rows(ts, D_MODEL),
            _resident((D_MODEL, D_MODEL)),
            _resident((len(POOL_SIZES), POOL_GROUP, POOL_GROUP)),
            _resident((1, POOL_WIDTH)),
            _resident((1, D_MODEL)),
        ],
        out_specs=[_rows(ts, D_MODEL), _rows(ts, D_MODEL), _rows(ts, POOL_WIDTH)],
        out_shape=[
            jax.ShapeDtypeStruct((s, D_MODEL), F32),
            jax.ShapeDtypeStruct((s, D_MODEL), BF16),
            jax.ShapeDtypeStruct((s, POOL_WIDTH), BF16),
        ],
    )(u, u, a, x, w_out, w_pool, pool_scale, g_ffn)


def _ffn_fwd(hn2, h1, wg_t, wu_t, w_down):
    s = h1.shape[0]
    ts = min(FFN_TOKEN_TILE, s)

    def body(hn_ref, h1_ref, wg_ref, wu_ref, wd_ref, h2_ref, gt_ref, up_ref, act_ref):
        hn = hn_ref[...]
        for c in range(D_FF // FF_CHUNK):
            cols = slice(c * FF_CHUNK, (c + 1) * FF_CHUNK)
            gt = _nt(hn, wg_ref[cols, :])
            up = _nt(hn, wu_ref[cols, :])
            gt_ref[:, cols] = gt.astype(BF16)
            up_ref[:, cols] = up.astype(BF16)
            act_ref[:, cols] = ((gt * jax.nn.sigmoid(gt)) * up).astype(BF16)
        h2_ref[...] = h1_ref[...] + _nn(act_ref[...], wd_ref[...])

    return pl.pallas_call(
        body,
        name="ffn_fwd",
        grid=(s // ts,),
        in_specs=[
            _rows(ts, D_MODEL),
            _rows(ts, D_MODEL),
            _resident((D_FF, D_MODEL)),
            _resident((D_FF, D_MODEL)),
            _resident((D_FF, D_MODEL)),
        ],
        out_specs=[_rows(ts, D_MODEL), _rows(ts, D_FF), _rows(ts, D_FF)],
        out_shape=[
            jax.ShapeDtypeStruct((s, D_MODEL), F32),
            jax.ShapeDtypeStruct((s, D_FF), BF16),
            jax.ShapeDtypeStruct((s, D_FF), BF16),
        ],
        scratch_shapes=[pltpu.VMEM((ts, D_FF), BF16)],
    )(hn2, h1, wg_t, wu_t, w_down)


def _ple_fwd_bwd(h2, p, target, g_ple, w_pg, w_pp_t):
    s = h2.shape[0]
    ts = min(TOKEN_TILE, s)

    def body(h2_ref, p_ref, t_ref, g_ref, wpg_ref, wpp_ref, loss_ref, dh_ref, dhb_ref, dwpg_ref, dwpp_ref, dg_ref):
        @pl.when(pl.program_id(0) == 0)
        def _():
            loss_ref[...] = jnp.zeros_like(loss_ref)
            dwpg_ref[...] = jnp.zeros_like(dwpg_ref)
            dwpp_ref[...] = jnp.zeros_like(dwpp_ref)
            dg_ref[...] = jnp.zeros_like(dg_ref)

        h2v = h2_ref[...]
        r = _rms(h2v)
        hn = ((h2v * r) * g_ref[...]).astype(BF16)
        gate = jax.nn.sigmoid(_nn(hn, wpg_ref[...]))
        pb = p_ref[...].astype(BF16)
        pp = _nt(pb, wpp_ref[...])
        diff = (h2v + gate * pp) - t_ref[...]
        loss_ref[...] += jnp.sum(jnp.sum(diff * diff, axis=0, keepdims=True), axis=1, keepdims=True) * (0.5 / D_MODEL)
        dy = diff * (1.0 / D_MODEL)
        d_pp = (dy * gate).astype(BF16)
        d_pre = ((dy * pp) * (gate * (1.0 - gate))).astype(BF16)
        dwpp_ref[...] += _tn(d_pp, pb)
        dwpg_ref[...] += _tn(hn, d_pre)
        d_x, d_g = _rms_bwd(_nt(d_pre, wpg_ref[...]), h2v, r, g_ref[...])
        dg_ref[...] += d_g
        dh = dy + d_x
        dh_ref[...] = dh
        dhb_ref[...] = dh.astype(BF16)

    return pl.pallas_call(
        body,
        name="ple_fwd_bwd",
        grid=(s // ts,),
        in_specs=[
            _rows(ts, D_MODEL),
            _rows(ts, PLE_DIM),
            _rows(ts, D_MODEL),
            _resident((1, D_MODEL)),
            _resident((D_MODEL, D_MODEL)),
            _resident((D_MODEL, PLE_DIM)),
        ],
        out_specs=[
            _acc((1, SMALL_LANES)),
            _rows(ts, D_MODEL),
            _rows(ts, D_MODEL),
            _acc((D_MODEL, D_MODEL)),
            _acc((D_MODEL, PLE_DIM)),
            _acc((1, D_MODEL)),
        ],
        out_shape=[
            jax.ShapeDtypeStruct((1, SMALL_LANES), F32),
            jax.ShapeDtypeStruct((s, D_MODEL), F32),
            jax.ShapeDtypeStruct((s, D_MODEL), BF16),
            jax.ShapeDtypeStruct((D_MODEL, D_MODEL), F32),
            jax.ShapeDtypeStruct((D_MODEL, PLE_DIM), F32),
            jax.ShapeDtypeStruct((1, D_MODEL), F32),
        ],
    )(h2, p, target, g_ple, w_pg, w_pp_t)


def _ffn_bwd_act(dh2, dh2b, h1, gt, up, g_ffn, wg_t, wu_t, w_down):
    s = h1.shape[0]
    ts = min(FFN_TOKEN_TILE, s)

    def body(dh_ref, dhb_ref, h1_ref, gt_ref, up_ref, g_ref, wg_ref, wu_ref, wd_ref,
             act_ref, dgt_ref, dup_ref, dh1_ref, dh1b_ref, dg_ref):
        @pl.when(pl.program_id(0) == 0)
        def _():
            dg_ref[...] = jnp.zeros_like(dg_ref)

        dhb = dhb_ref[...]
        for c in range(D_FF // FF_CHUNK):
            cols = slice(c * FF_CHUNK, (c + 1) * FF_CHUNK)
            d_act = _nt(dhb, wd_ref[cols, :])
            gtv = gt_ref[:, cols].astype(F32)
            upv = up_ref[:, cols].astype(F32)
            sg = jax.nn.sigmoid(gtv)
            silu = gtv * sg
            act_ref[:, cols] = (silu * upv).astype(BF16)
            dup_ref[:, cols] = (d_act * silu).astype(BF16)
            dgt_ref[:, cols] = ((d_act * upv) * (sg * (1.0 + gtv * (1.0 - sg)))).astype(BF16)
        d_hn = _nn(dgt_ref[...], wg_ref[...]) + _nn(dup_ref[...], wu_ref[...])
        h1v = h1_ref[...]
        d_x, d_g = _rms_bwd(d_hn, h1v, _rms(h1v), g_ref[...])
        dg_ref[...] += d_g
        dh1 = dh_ref[...] + d_x
        dh1_ref[...] = dh1
        dh1b_ref[...] = dh1.astype(BF16)

    return pl.pallas_call(
        body,
        name="ffn_bwd_act",
        grid=(s // ts,),
        in_specs=[
            _rows(ts, D_MODEL),
            _rows(ts, D_MODEL),
            _rows(ts, D_MODEL),
            _rows(ts, D_FF),
            _rows(ts, D_FF),
            _resident((1, D_MODEL)),
            _resident((D_FF, D_MODEL)),
            _resident((D_FF, D_MODEL)),
            _resident((D_FF, D_MODEL)),
        ],
        out_specs=[
            _rows(ts, D_FF), _rows(ts, D_FF), _rows(ts, D_FF),
            _rows(ts, D_MODEL), _rows(ts, D_MODEL), _acc((1, D_MODEL)),
        ],
        out_shape=[
            jax.ShapeDtypeStruct((s, D_FF), BF16),
            jax.ShapeDtypeStruct((s, D_FF), BF16),
            jax.ShapeDtypeStruct((s, D_FF), BF16),
            jax.ShapeDtypeStruct((s, D_MODEL), F32),
            jax.ShapeDtypeStruct((s, D_MODEL), BF16),
            jax.ShapeDtypeStruct((1, D_MODEL), F32),
        ],
    )(dh2, dh2b, h1, gt, up, g_ffn, wg_t, wu_t, w_down)


def _ffn_bwd_w(act, dgt, dup, hn2, dh2b):
    s = hn2.shape[0]
    cols = lambda i: (0, i)

    def body(act_ref, dgt_ref, dup_ref, hn_ref, dh_ref, dwg_ref, dwu_ref, dwd_ref):
        dwg_ref[...] = _tn(dgt_ref[...], hn_ref[...])
        dwu_ref[...] = _tn(dup_ref[...], hn_ref[...])
        dwd_ref[...] = _tn(act_ref[...], dh_ref[...])

    return pl.pallas_call(
        body,
        name="ffn_bwd_w",
        grid=(D_FF // FF_CHUNK,),
        in_specs=[
            pl.BlockSpec((s, FF_CHUNK), cols),
            pl.BlockSpec((s, FF_CHUNK), cols),
            pl.BlockSpec((s, FF_CHUNK), cols),
            _resident((s, D_MODEL)),
            _resident((s, D_MODEL)),
        ],
        out_specs=[_rows(FF_CHUNK, D_MODEL)] * 3,
        out_shape=[jax.ShapeDtypeStruct((D_FF, D_MODEL), F32)] * 3,
    )(act, dgt, dup, hn2, dh2b)


def _mix_bwd(dh1b, a, m, u, w_out, w_pool, pool_scale):
    s = u.shape[0]
    ts = min(TOKEN_TILE, s)
    nt = s // ts
    halo_after = lambda i: (jnp.minimum((i + 1) * (ts // POOL_HALO), s // POOL_HALO - 1), 0)
    n_groups = len(POOL_SIZES)

    def body(dh_ref, dhn_ref, a_ref, m_ref, u_ref, uh_ref, wo_ref, wp_ref, sc_ref,
             da_ref, du_ref, dwo_ref, dwp_ref, dsc_ref):
        i = pl.program_id(0)

        @pl.when(i == 0)
        def _():
            dwo_ref[...] = jnp.zeros_like(dwo_ref)
            dwp_ref[...] = jnp.zeros_like(dwp_ref)
            dsc_ref[...] = jnp.zeros_like(dsc_ref)

        dh = dh_ref[...]
        da_ref[...] = _nt(dh, wo_ref[:ATTN_WIDTH, :])
        dwo_ref[:ATTN_WIDTH, :] += _tn(a_ref[...], dh)
        dwo_ref[ATTN_WIDTH:, :] += _tn(m_ref[...], dh)
        dh_next = jnp.where(i < nt - 1, dhn_ref[...], jnp.zeros_like(dhn_ref))
        dm_ext = _nt(jnp.concatenate([dh, dh_next], axis=0), wo_ref[ATTN_WIDTH:, :])
        pooled = _pooled(u_ref[...], uh_ref[...], i, ts)
        t_ext = i * ts + lax.broadcasted_iota(jnp.int32, (ts + POOL_HALO, 1), 0)
        for g, w in enumerate(POOL_SIZES):
            lanes = slice(g * POOL_GROUP, (g + 1) * POOL_GROUP)
            wp = wp_ref[g].astype(BF16)
            pg = pooled[g].astype(BF16)
            dm_g = dm_ext[:, lanes]
            dsc_ref[:, lanes] += jnp.sum(dm_g[:ts, :] * _nn(pg, wp), axis=0, keepdims=True)
            dy = (dm_g * sc_ref[:, lanes]).astype(BF16)
            dwp_ref[g] += _tn(pg, dy[:ts, :])
            d_pool = _nt(dy, wp)
            acc = d_pool / jnp.minimum(t_ext + 1, w).astype(F32)
            shift = 1
            while shift < w:
                acc = acc + pltpu.roll(acc, ts + POOL_HALO - shift, 0)
                shift *= 2
            du_ref[:, lanes] = acc[:ts, :] - d_pool[:ts, :]

    return pl.pallas_call(
        body,
        name="mix_bwd",
        grid=(nt,),
        in_specs=[
            _rows(ts, D_MODEL),
            pl.BlockSpec((POOL_HALO, D_MODEL), halo_after),
            _rows(ts, ATTN_WIDTH),
            _rows(ts, POOL_WIDTH),
            _rows(ts, POOL_WIDTH),
            pl.BlockSpec((POOL_HALO, POOL_WIDTH), _halo_before(ts)),
            _resident((D_MODEL, D_MODEL)),
            _resident((n_groups, POOL_GROUP, POOL_GROUP)),
            _resident((1, POOL_WIDTH)),
        ],
        out_specs=[
            _rows(ts, ATTN_WIDTH),
            _rows(ts, POOL_WIDTH),
            _acc((D_MODEL, D_MODEL)),
            _acc((n_groups, POOL_GROUP, POOL_GROUP)),
            _acc((1, POOL_WIDTH)),
        ],
        out_shape=[
            jax.ShapeDtypeStruct((s, ATTN_WIDTH), F32),
            jax.ShapeDtypeStruct((s, POOL_WIDTH), F32),
            jax.ShapeDtypeStruct((D_MODEL, D_MODEL), F32),
            jax.ShapeDtypeStruct((n_groups, POOL_GROUP, POOL_GROUP), F32),
            jax.ShapeDtypeStruct((1, POOL_WIDTH), F32),
        ],
    )(dh1b, dh1b, a, m, u, u, w_out, w_pool, pool_scale)


def _attn_bwd(qn, kn, v, a, da, tab, sinks):
    s = qn.shape[0]
    nb = s // BLOCK
    cur, prev = _attn_specs(nb)
    done = lambda n: (jnp.maximum(n - 1, 0), 0)

    def body(sink_ref, q_ref, kc_ref, kp_ref, vc_ref, vp_ref, o_ref, do_ref, tab_ref,
             dq_ref, dk_ref, dv_ref, dl_ref, ds_ref, k_carry, v_carry, sink_acc):
        n = pl.program_id(0)

        @pl.when(n == 0)
        def _():
            dl_ref[...] = jnp.zeros_like(dl_ref)
            k_carry[...] = jnp.zeros_like(k_carry)
            v_carry[...] = jnp.zeros_like(v_carry)
            sink_acc[...] = jnp.zeros_like(sink_acc)

        @pl.when(n < nb)
        def _():
            first = n == 0
            lo_mask = _lane_lo((BLOCK, BLOCK))
            kk = jnp.concatenate([kp_ref[...], kc_ref[...]], axis=0)
            vv = jnp.concatenate([vp_ref[...], vc_ref[...]], axis=0)
            keys = (kk, pltpu.roll(kk, HEAD_DIM, 1))
            vals = (vv, pltpu.roll(vv, HEAD_DIM, 1))
            q_st = _stack_heads([q_ref[:, p * BLOCK:(p + 1) * BLOCK] for p in range(4)], lo_mask)
            do_pairs = [do_ref[:, p * BLOCK:(p + 1) * BLOCK] for p in range(4)]
            do_st = _stack_heads(do_pairs, lo_mask)
            o_st = _stack_heads([o_ref[:, p * BLOCK:(p + 1) * BLOCK].astype(F32) for p in range(4)], lo_mask)
            dq_st, dk_parts, dv_parts = [], [], []
            for half, heads in enumerate((HEADS_A, HEADS_B)):
                probs, p_sink = _band_probs(q_st[half], keys[half], tab_ref[half], _sink_column(sink_ref, heads), first)
                delta = jnp.sum(do_st[half] * o_st[half], axis=-1, keepdims=True)
                dob = do_st[half].astype(BF16)
                dl = probs * (_nt(dob, vals[half]) - delta)
                dl_ref[half] += dl
                sink_acc[half] += p_sink * delta
                dsb = (dl * (HEAD_DIM ** -0.5)).astype(BF16)
                dq_st.append(_nn(dsb, keys[half]))
                dk_parts.append(_tn(dsb, q_st[half]))
                dv_parts.append(_tn(probs.astype(BF16), dob))
            dq = _unstack_heads(dq_st[0], dq_st[1], lo_mask)
            for p in range(4):
                dq_ref[:, p * BLOCK:(p + 1) * BLOCK] = dq[p]
            dk = dk_parts[0] + pltpu.roll(dk_parts[1], HEAD_DIM, 1)
            dv = dv_parts[0] + pltpu.roll(dv_parts[1], HEAD_DIM, 1)
            dk_ref[...] = k_carry[...] + dk[:BLOCK, :]
            dv_ref[...] = v_carry[...] + dv[:BLOCK, :]
            k_carry[...] = dk[BLOCK:, :]
            v_carry[...] = dv[BLOCK:, :]

        @pl.when(n == nb)
        def _():
            dk_ref[...] = k_carry[...]
            dv_ref[...] = v_carry[...]
            for half, heads in enumerate((HEADS_A, HEADS_B)):
                for slot, h in enumerate(heads):
                    tot = jnp.sum(sink_acc[half, slot * BLOCK:(slot + 1) * BLOCK, :], axis=0, keepdims=True)
                    ds_ref[h:h + 1, :] = jnp.broadcast_to(-tot, (1, SMALL_LANES))

    return pl.pallas_call(
        body,
        name="attn_bwd",
        grid=(nb + 1,),
        in_specs=[
            pl.BlockSpec(memory_space=pltpu.SMEM),
            pl.BlockSpec((BLOCK, ATTN_WIDTH), cur),
            pl.BlockSpec((BLOCK, KV_WIDTH), cur),
            pl.BlockSpec((BLOCK, KV_WIDTH), prev),
            pl.BlockSpec((BLOCK, KV_WIDTH), cur),
            pl.BlockSpec((BLOCK, KV_WIDTH), prev),
            pl.BlockSpec((BLOCK, ATTN_WIDTH), cur),
            pl.BlockSpec((BLOCK, ATTN_WIDTH), cur),
            _resident((2, 4 * BLOCK, 2 * BLOCK)),
        ],
        out_specs=[
            pl.BlockSpec((BLOCK, ATTN_WIDTH), cur),
            pl.BlockSpec((BLOCK, KV_WIDTH), done),
            pl.BlockSpec((BLOCK, KV_WIDTH), done),
            _acc((2, 4 * BLOCK, 2 * BLOCK)),
            _acc((N_DEV, SMALL_LANES)),
        ],
        out_shape=[
            jax.ShapeDtypeStruct((s, ATTN_WIDTH), F32),
            jax.ShapeDtypeStruct((s, KV_WIDTH), F32),
            jax.ShapeDtypeStruct((s, KV_WIDTH), F32),
            jax.ShapeDtypeStruct((2, 4 * BLOCK, 2 * BLOCK), F32),
            jax.ShapeDtypeStruct((N_DEV, SMALL_LANES), F32),
        ],
        scratch_shapes=[
            pltpu.VMEM((BLOCK, KV_WIDTH), F32),
            pltpu.VMEM((BLOCK, KV_WIDTH), F32),
            pltpu.VMEM((2, 4 * BLOCK, 1), F32),
        ],
    )(sinks, qn, kn, kn, v, v, a, da, tab)


def _fold_heads(acc):
    t = acc + pltpu.roll(acc, HEAD_DIM, 1)
    out = t[:, :SMALL_LANES]
    for g in range(1, acc.shape[1] // SMALL_LANES):
        out = out + t[:, g * SMALL_LANES:(g + 1) * SMALL_LANES]
    return out


def _in_proj_bwd(dqn, dkn, dv, du, zqk, x, dh1, g_attn, gq_t, gk_t, w_in_t):
    s = x.shape[0]
    ts = min(TOKEN_TILE, s)
    nt = s // ts

    def head_norm_bwd(d_n, raw, g_t, bmat):
        r = lax.rsqrt(_seg_mean(raw * raw, bmat) + EPS)
        gy = d_n * g_t
        d_raw = r * gy - raw * (r * r * r) * _seg_mean(gy * raw, bmat)
        return d_raw, jnp.sum(d_n * (raw * r), axis=0, keepdims=True)

    def body(dqn_ref, dkn_ref, dv_ref, du_ref, zqk_ref, x_ref, dh1_ref, g_ref, gq_ref, gk_ref, w_ref, bq_ref, bk_ref,
             gx_ref, dw_ref, dg_ref, dgq_ref, dgk_ref, dz_ref, gq_acc, gk_acc):
        i = pl.program_id(0)

        @pl.when(i == 0)
        def _():
            dw_ref[...] = jnp.zeros_like(dw_ref)
            dg_ref[...] = jnp.zeros_like(dg_ref)
            gq_acc[...] = jnp.zeros_like(gq_acc)
            gk_acc[...] = jnp.zeros_like(gk_acc)

        d_q, d_gq = head_norm_bwd(dqn_ref[...], zqk_ref[:, :ATTN_WIDTH], gq_ref[...], bq_ref[...])
        d_k, d_gk = head_norm_bwd(dkn_ref[...], zqk_ref[:, ATTN_WIDTH:], gk_ref[...], bk_ref[...])
        gq_acc[...] += d_gq
        gk_acc[...] += d_gk
        dz_ref[:, :ATTN_WIDTH] = d_q.astype(BF16)
        dz_ref[:, ATTN_WIDTH:ATTN_WIDTH + KV_WIDTH] = d_k.astype(BF16)
        dz_ref[:, ATTN_WIDTH + KV_WIDTH:ATTN_WIDTH + 2 * KV_WIDTH] = dv_ref[...].astype(BF16)
        dz_ref[:, ATTN_WIDTH + 2 * KV_WIDTH:] = du_ref[...].astype(BF16)
        dz = dz_ref[...]
        xf = x_ref[...]
        r = _rms(xf)
        hn = ((xf * r) * g_ref[...]).astype(BF16)
        dw_ref[...] += _tn(dz, hn)
        d_x, d_g = _rms_bwd(_nn(dz, w_ref[...]), xf, r, g_ref[...])
        dg_ref[...] += d_g
        gx_ref[...] = dh1_ref[...] + d_x

        @pl.when(i == nt - 1)
        def _():
            dgq_ref[...] = _fold_heads(gq_acc[...])
            dgk_ref[...] = _fold_heads(gk_acc[...])

    return pl.pallas_call(
        body,
        name="in_proj_bwd",
        grid=(nt,),
        in_specs=[
            _rows(ts, ATTN_WIDTH),
            _rows(ts, KV_WIDTH),
            _rows(ts, KV_WIDTH),
            _rows(ts, POOL_WIDTH),
            _rows(ts, ATTN_WIDTH + KV_WIDTH),
            _rows(ts, D_MODEL),
            _rows(ts, D_MODEL),
            _resident((1, D_MODEL)),
            _resident((1, ATTN_WIDTH)),
            _resident((1, KV_WIDTH)),
            _resident((IN_WIDTH, D_MODEL)),
            _resident((ATTN_WIDTH, ATTN_WIDTH)),
            _resident((KV_WIDTH, KV_WIDTH)),
        ],
        out_specs=[
            _rows(ts, D_MODEL),
            _acc((IN_WIDTH, D_MODEL)),
            _acc((1, D_MODEL)),
            _acc((1, SMALL_LANES)),
            _acc((1, SMALL_LANES)),
        ],
        out_shape=[
            jax.ShapeDtypeStruct((s, D_MODEL), F32),
            jax.ShapeDtypeStruct((IN_WIDTH, D_MODEL), F32),
            jax.ShapeDtypeStruct((1, D_MODEL), F32),
            jax.ShapeDtypeStruct((1, SMALL_LANES), F32),
            jax.ShapeDtypeStruct((1, SMALL_LANES), F32),
        ],
        scratch_shapes=[
            pltpu.VMEM((ts, IN_WIDTH), BF16),
            pltpu.VMEM((1, ATTN_WIDTH), F32),
            pltpu.VMEM((1, KV_WIDTH), F32),
        ],
    )(dqn, dkn, dv, du, zqk, x, dh1, g_attn, gq_t, gk_t, w_in_t,
      _head_mean_matrix(ATTN_WIDTH), _head_mean_matrix(KV_WIDTH))


BIG_WEIGHTS = (
    ("w_in", True, IN_WIDTH // N_DEV, D_MODEL),
    ("w_out", False, D_MODEL // N_DEV, D_MODEL),
    ("w_gate", True, D_FF // N_DEV, D_MODEL),
    ("w_up", True, D_FF // N_DEV, D_MODEL),
    ("w_down", False, D_FF // N_DEV, D_MODEL),
    ("w_ple_gate", False, D_MODEL // N_DEV, D_MODEL),
    ("w_ple_proj", True, D_MODEL // N_DEV, PLE_DIM),
)
N_BIG = len(BIG_WEIGHTS)


def _place():
    x, y, c = lax.axis_index("x"), lax.axis_index("y"), lax.axis_index("c")
    chips = [(1 - x, y), (x, 1 - y), (1 - x, 1 - y)]
    return x, y, c, chips


def _all_gather_weights(shards):
    any_spec = pl.BlockSpec(memory_space=pl.ANY)

    def body(*refs):
        ins = refs[:N_BIG]
        outs = refs[N_BIG:2 * N_BIG]
        mine = refs[2 * N_BIG:3 * N_BIG]
        send_sems, recv_sems, local_sems = refs[3 * N_BIG:]
        x, y, c, chips = _place()
        me, sibling = (x, y, c), (x, y, 1 - c)

        def block(k, px, py, pc):
            return outs[k].at[4 * px + 2 * py + pc]

        def copy(k, idx, owner, to, src=None):
            return pltpu.make_async_remote_copy(
                src_ref=block(k, *owner) if src is None else src, dst_ref=block(k, *owner),
                send_sem=send_sems.at[k, idx], recv_sem=recv_sems.at[k, idx], device_id=to, device_id_type=MESH)

        local, first, passed = [], [], []
        for k, (_, transposed, _, _) in enumerate(BIG_WEIGHTS):
            w = ins[k][...]
            mine[k][...] = (w.T if transposed else w).astype(BF16)
            local.append(pltpu.make_async_copy(mine[k], block(k, *me), local_sems.at[k]))
            local[-1].start()
            first.append(copy(k, 0, me, sibling, src=mine[k]))
            first += [copy(k, 1 + j, me, (*chip, c), src=mine[k]) for j, chip in enumerate(chips)]
        for cp in first:
            cp.start()
        for j, chip in enumerate(chips):
            for k in range(N_BIG):
                copy(k, 1 + j, (*chip, c), me).wait_recv()
                passed.append(copy(k, 4 + j, (*chip, c), sibling))
                passed[-1].start()
        for k in range(N_BIG):
            copy(k, 0, sibling, me).wait_recv()
            for j, chip in enumerate(chips):
                copy(k, 4 + j, (*chip, 1 - c), me).wait_recv()
        for cp in first + passed:
            cp.wait_send()
        for cp in local:
            cp.wait()

    return pl.pallas_call(
        body,
        name="all_gather_weights",
        in_specs=[pl.BlockSpec(memory_space=pltpu.VMEM)] * N_BIG,
        out_specs=[any_spec] * N_BIG,
        out_shape=[jax.ShapeDtypeStruct((N_DEV, r, c), BF16) for _, _, r, c in BIG_WEIGHTS],
        scratch_shapes=[pltpu.VMEM((r, c), BF16) for _, _, r, c in BIG_WEIGHTS] + [
            pltpu.SemaphoreType.DMA((N_BIG, 7)),
            pltpu.SemaphoreType.DMA((N_BIG, 7)),
            pltpu.SemaphoreType.DMA((N_BIG,)),
        ],
    )(*shards)


def _exchange_with_sibling(grads):
    any_spec = pl.BlockSpec(memory_space=pl.ANY)

    def body(*refs):
        gs = refs[:N_BIG]
        lands = refs[N_BIG:2 * N_BIG]
        send_sems, recv_sems = refs[2 * N_BIG:]
        x, y, c, _ = _place()
        copies = [
            pltpu.make_async_remote_copy(
                src_ref=gs[k].at[:, 1 - c], dst_ref=lands[k], send_sem=send_sems.at[k], recv_sem=recv_sems.at[k],
                device_id=(x, y, 1 - c), device_id_type=MESH)
            for k in range(N_BIG)
        ]
        for cp in copies:
            cp.start()
        for cp in copies:
            cp.wait()

    return pl.pallas_call(
        body,
        name="exchange_with_sibling",
        in_specs=[any_spec] * N_BIG,
        out_specs=[any_spec] * N_BIG,
        out_shape=[jax.ShapeDtypeStruct((N_CHIPS, r, c), F32) for _, _, r, c in BIG_WEIGHTS],
        scratch_shapes=[pltpu.SemaphoreType.DMA((N_BIG,)), pltpu.SemaphoreType.DMA((N_BIG,))],
    )(*grads)


def _chip_sum(k, place, grad, from_sibling):
    _, _, r, c = BIG_WEIGHTS[k]

    def body(place_ref, g_ref, l_ref, own_ref, send_ref):
        q = pl.program_id(0)
        tot = g_ref[0, 0] + l_ref[0]
        mine = q == 2 * place_ref[0] + place_ref[1]

        @pl.when(mine)
        def _():
            own_ref[...] = tot

        send_ref[0] = jnp.where(mine, 0.0, tot).astype(BF16)

    return pl.pallas_call(
        body,
        name=f"chip_sum_{BIG_WEIGHTS[k][0]}",
        grid_spec=pltpu.PrefetchScalarGridSpec(
            num_scalar_prefetch=1,
            grid=(N_CHIPS,),
            in_specs=[
                pl.BlockSpec((1, 1, r, c), lambda q, place: (q, place[2], 0, 0)),
                pl.BlockSpec((1, r, c), lambda q, place: (q, 0, 0)),
            ],
            out_specs=[
                pl.BlockSpec((r, c), lambda q, place: (0, 0)),
                pl.BlockSpec((1, r, c), lambda q, place: (q, 0, 0)),
            ],
        ),
        out_shape=[jax.ShapeDtypeStruct((r, c), F32), jax.ShapeDtypeStruct((N_CHIPS, r, c), BF16)],
    )(place, grad, from_sibling)


def _exchange_between_chips(to_send, small):
    any_spec = pl.BlockSpec(memory_space=pl.ANY)
    rows_small = small.shape[0]

    def body(*refs):
        sends = refs[:N_BIG]
        small_ref = refs[N_BIG]
        lands = refs[N_BIG + 1:2 * N_BIG + 1]
        small_land = refs[2 * N_BIG + 1]
        send_sems, recv_sems, small_send, small_recv, local_sem = refs[2 * N_BIG + 2:]
        x, y, c, chips = _place()
        me = 4 * x + 2 * y + c
        copies = []
        for k in range(N_BIG):
            for j, (px, py) in enumerate(chips):
                copies.append(pltpu.make_async_remote_copy(
                    src_ref=sends[k].at[2 * px + py], dst_ref=lands[k].at[j],
                    send_sem=send_sems.at[k, j], recv_sem=recv_sems.at[k, j],
                    device_id=(px, py, c), device_id_type=MESH))
        own = pltpu.make_async_copy(small_ref, small_land.at[me], local_sem)
        own.start()
        rel = 0
        for fx in (0, 1):
            for fy in (0, 1):
                for fc in (0, 1):
                    if (fx, fy, fc) == (0, 0, 0):
                        continue
                    copies.append(pltpu.make_async_remote_copy(
                        src_ref=small_ref, dst_ref=small_land.at[me],
                        send_sem=small_send.at[rel], recv_sem=small_recv.at[rel],
                        device_id=(x ^ fx, y ^ fy, c ^ fc), device_id_type=MESH))
                    rel += 1
        for cp in copies:
            cp.start()
        for cp in copies:
            cp.wait()
        own.wait()

    return pl.pallas_call(
        body,
        name="exchange_between_chips",
        in_specs=[any_spec] * (N_BIG + 1),
        out_specs=[any_spec] * (N_BIG + 1),
        out_shape=[jax.ShapeDtypeStruct((3, r, c), BF16) for _, _, r, c in BIG_WEIGHTS]
        + [jax.ShapeDtypeStruct((N_DEV, rows_small, SMALL_LANES), F32)],
        scratch_shapes=[
            pltpu.SemaphoreType.DMA((N_BIG, 3)),
            pltpu.SemaphoreType.DMA((N_BIG, 3)),
            pltpu.SemaphoreType.DMA((7,)),
            pltpu.SemaphoreType.DMA((7,)),
            pltpu.SemaphoreType.DMA,
        ],
    )(*to_send, small)


def _adamw(w, g, m, v):
    m = ADAM_B1 * m + (1.0 - ADAM_B1) * g
    v = ADAM_B2 * v + (1.0 - ADAM_B2) * jnp.square(g)
    m_hat = m / (1.0 - ADAM_B1 ** ADAM_STEP)
    v_hat = v / (1.0 - ADAM_B2 ** ADAM_STEP)
    delta = -ADAM_LR * (m_hat / (jnp.sqrt(v_hat) + ADAM_EPS) + ADAM_WD * w)
    return delta, m, v


def _adamw_big(k, own, landed, w, m, v):
    name, transposed, _, _ = BIG_WEIGHTS[k]

    def body(own_ref, land_ref, w_ref, m_ref, v_ref, g_ref, d_ref, nm_ref, nv_ref):
        g = ((own_ref[...] + land_ref[0].astype(F32)) + land_ref[1].astype(F32)) + land_ref[2].astype(F32)
        if transposed:
            g = g.T
        g_ref[...] = g
        d_ref[...], nm_ref[...], nv_ref[...] = _adamw(w_ref[...], g, m_ref[...], v_ref[...])

    return pl.pallas_call(
        body,
        name=f"adamw_{name}",
        out_shape=[jax.ShapeDtypeStruct(w.shape, F32)] * 4,
    )(own, landed, w, m, v)


def _sum_small(parts):
    def body(p_ref, out_ref):
        tot = p_ref[0]
        for j in range(1, N_DEV):
            tot = tot + p_ref[j]
        out_ref[...] = tot

    return pl.pallas_call(body, name="sum_small", out_shape=jax.ShapeDtypeStruct(parts.shape[1:], F32))(parts)


def _adamw_small(grads, ws, ms, vs):
    n = len(grads)

    def body(*refs):
        g_refs, w_refs, m_refs, v_refs = refs[:n], refs[n:2 * n], refs[2 * n:3 * n], refs[3 * n:4 * n]
        outs = refs[4 * n:]
        for i in range(n):
            d, nm, nv = _adamw(w_refs[i][...], g_refs[i][...], m_refs[i][...], v_refs[i][...])
            outs[i][...] = d
            outs[n + i][...] = nm
            outs[2 * n + i][...] = nv

    shapes = [jax.ShapeDtypeStruct(w.shape, F32) for w in ws]
    return pl.pallas_call(body, name="adamw_small", out_shape=shapes * 3)(*grads, *ws, *ms, *vs)


SMALL_NAMES = ("g_attn_norm", "g_q", "g_k", "attn_sinks", "rel_bias", "w_pool", "pool_scale", "g_ffn_norm", "g_ple_norm")


def _pack_small(arrays):
    rows, offsets = [], []
    at = 0
    for a in arrays:
        flat = a.reshape(-1)
        n_rows = -(-flat.shape[0] // (8 * SMALL_LANES)) * 8
        flat = jnp.pad(flat, (0, n_rows * SMALL_LANES - flat.shape[0]))
        rows.append(flat.reshape(n_rows, SMALL_LANES))
        offsets.append(at)
        at += n_rows
    return jnp.concatenate(rows, axis=0), offsets


def kernel(x, p, w_in, w_out, g_attn_norm, g_q, g_k, attn_sinks, rel_bias, w_pool, pool_scale, g_ffn_norm, w_gate, w_up, w_down, g_ple_norm, w_ple_gate, w_ple_proj, loss_target, m_w_in, m_w_out, m_g_attn_norm, m_g_q, m_g_k, m_attn_sinks, m_rel_bias, m_w_pool, m_pool_scale, m_g_ffn_norm, m_w_gate, m_w_up, m_w_down, m_g_ple_norm, m_w_ple_gate, m_w_ple_proj, v_w_in, v_w_out, v_g_attn_norm, v_g_q, v_g_k, v_attn_sinks, v_rel_bias, v_w_pool, v_pool_scale, v_g_ffn_norm, v_w_gate, v_w_up, v_w_down, v_g_ple_norm, v_w_ple_gate, v_w_ple_proj):
    weights = dict(w_in=w_in, w_out=w_out, g_attn_norm=g_attn_norm, g_q=g_q, g_k=g_k, attn_sinks=attn_sinks,
                   rel_bias=rel_bias, w_pool=w_pool, pool_scale=pool_scale, g_ffn_norm=g_ffn_norm, w_gate=w_gate,
                   w_up=w_up, w_down=w_down, g_ple_norm=g_ple_norm, w_ple_gate=w_ple_gate, w_ple_proj=w_ple_proj)
    m_in = dict(w_in=m_w_in, w_out=m_w_out, g_attn_norm=m_g_attn_norm, g_q=m_g_q, g_k=m_g_k, attn_sinks=m_attn_sinks,
                rel_bias=m_rel_bias, w_pool=m_w_pool, pool_scale=m_pool_scale, g_ffn_norm=m_g_ffn_norm, w_gate=m_w_gate,
                w_up=m_w_up, w_down=m_w_down, g_ple_norm=m_g_ple_norm, w_ple_gate=m_w_ple_gate, w_ple_proj=m_w_ple_proj)
    v_in = dict(w_in=v_w_in, w_out=v_w_out, g_attn_norm=v_g_attn_norm, g_q=v_g_q, g_k=v_g_k, attn_sinks=v_attn_sinks,
                rel_bias=v_rel_bias, w_pool=v_w_pool, pool_scale=v_pool_scale, g_ffn_norm=v_g_ffn_norm, w_gate=v_w_gate,
                w_up=v_w_up, w_down=v_w_down, g_ple_norm=v_g_ple_norm, w_ple_gate=v_w_ple_gate, w_ple_proj=v_w_ple_proj)

    xs = x[0]
    ps = p[0, 0]
    target = loss_target[0]
    wp = w_pool[0]
    gq_t = jnp.tile(g_q, (1, ATTN_WIDTH // HEAD_DIM))
    gk_t = jnp.tile(g_k, (1, KV_WIDTH // HEAD_DIM))

    gathered = _all_gather_weights([weights[name][0] for name, _, _, _ in BIG_WEIGHTS])
    w_in_t, w_out_f, wg_t, wu_t, w_down_f, w_pg, w_pp_t = [
        g.reshape(N_DEV * r, c) for g, (_, _, r, c) in zip(gathered, BIG_WEIGHTS)]

    tab = _bias_table(rel_bias)
    zqk, qn, kn, v, u = _in_proj(xs, g_attn_norm, w_in_t, gq_t, gk_t)
    a = _attn_fwd(qn, kn, v, tab, attn_sinks)
    h1, hn2, m_out = _mix_out(u, a, xs, w_out_f, wp, pool_scale, g_ffn_norm)
    h2, gt, up = _ffn_fwd(hn2, h1, wg_t, wu_t, w_down_f)

    loss_part, dh2, dh2b, d_wpg, d_wpp_t, d_g_ple = _ple_fwd_bwd(h2, ps, target, g_ple_norm, w_pg, w_pp_t)
    act, dgt, dup, dh1, dh1b, d_g_ffn = _ffn_bwd_act(dh2, dh2b, h1, gt, up, g_ffn_norm, wg_t, wu_t, w_down_f)
    d_wg_t, d_wu_t, d_wd = _ffn_bwd_w(act, dgt, dup, hn2, dh2b)
    da, du, d_wo, d_wpool, d_scale = _mix_bwd(dh1b, a, m_out, u, w_out_f, wp, pool_scale)
    dqn, dkn, dv, dl_acc, d_sinks = _attn_bwd(qn, kn, v, a, da, tab, attn_sinks)
    d_rel = _bias_table_bwd(dl_acc)
    grad_x, d_win_t, d_g_attn, d_gq, d_gk = _in_proj_bwd(dqn, dkn, dv, du, zqk, xs, dh1, g_attn_norm, gq_t, gk_t, w_in_t)

    big = [d_win_t, d_wo, d_wg_t, d_wu_t, d_wd, d_wpg, d_wpp_t]
    big = [g.reshape(N_CHIPS, 2, r, c) for g, (_, _, r, c) in zip(big, BIG_WEIGHTS)]
    place = jnp.stack([lax.axis_index("x"), lax.axis_index("y"), lax.axis_index("c")]).astype(jnp.int32)
    from_sibling = _exchange_with_sibling(big)
    sums = [_chip_sum(k, place, big[k], from_sibling[k]) for k in range(N_BIG)]
    small_parts = [d_g_attn, d_gq[:, :HEAD_DIM], d_gk[:, :HEAD_DIM], d_sinks[:, 0], d_rel[:, :N_DEV], d_wpool, d_scale,
                   d_g_ffn, d_g_ple, loss_part[:, :1]]
    packed, offsets = _pack_small(small_parts)
    *landed, small_all = _exchange_between_chips([s[1] for s in sums], packed)
    small_sum = _sum_small(small_all)

    out = {"grad": {}, "delta": {}, "new_m": {}, "new_v": {}}
    for k, (name, _, _, _) in enumerate(BIG_WEIGHTS):
        g, d, nm, nv = _adamw_big(k, sums[k][0], landed[k], weights[name][0], m_in[name][0], v_in[name][0])
        out["grad"][name], out["delta"][name], out["new_m"][name], out["new_v"][name] = g[None], d[None], nm[None], nv[None]
    small_grads = []
    for name, off in zip(SMALL_NAMES, offsets):
        n = weights[name].size
        rows = -(-n // SMALL_LANES)
        small_grads.append(small_sum[off:off + rows].reshape(-1)[:n].reshape(weights[name].shape))
    loss = small_sum[offsets[-1], 0]
    updates = _adamw_small(small_grads, [weights[n] for n in SMALL_NAMES], [m_in[n] for n in SMALL_NAMES],
                           [v_in[n] for n in SMALL_NAMES])
    n_small = len(SMALL_NAMES)
    for i, name in enumerate(SMALL_NAMES):
        out["grad"][name] = small_grads[i]
        out["delta"][name] = updates[i]
        out["new_m"][name] = updates[n_small + i]
        out["new_v"][name] = updates[2 * n_small + i]

    order = ("w_in", "w_out", "g_attn_norm", "g_q", "g_k", "attn_sinks", "rel_bias", "w_pool", "pool_scale",
             "g_ffn_norm", "w_gate", "w_up", "w_down", "g_ple_norm", "w_ple_gate", "w_ple_proj")
    return (loss, grad_x[None], *[out["grad"][n] for n in order], *[out["delta"][n] for n in order],
            *[out["new_m"][n] for n in order], *[out["new_v"][n] for n in order])
```

```python
import functools
import math

import jax
import jax.numpy as jnp
import numpy as np
from jax import lax
from jax.experimental import pallas as pl
from jax.experimental.pallas import tpu as pltpu

F32 = jnp.float32
BF16 = jnp.bfloat16
MESH = pl.DeviceIdType.MESH

D_MODEL = 1024
HEAD_DIM = 64
ATTN_WIDTH = 512
KV_WIDTH = 128
POOL_WIDTH = 512
POOL_SIZES = (2, 4, 8, 16)
POOL_GROUP = 128
POOL_HALO = 16
IN_WIDTH = 1280
D_FF = 2816
PLE_DIM = 256
BLOCK = 128
N_BUCKETS = 32
MAX_DISTANCE = 128
EPS = 1e-6
N_DEV = 8
N_CHIPS = 4

ADAM_LR = 0.001
ADAM_B1 = 0.9
ADAM_B2 = 0.999
ADAM_EPS = 1e-08
ADAM_WD = 0.01
ADAM_STEP = 10

TOKEN_TILE = 512
FFN_TOKEN_TILE = 256
FF_CHUNK = 256
HEADS_A = (0, 2, 5, 7)
HEADS_B = (1, 3, 4, 6)
SMALL_LANES = 128


def _nn(a, b):
    return jnp.dot(a, b, preferred_element_type=F32)


def _nt(a, b):
    return lax.dot_general(a, b, (((1,), (1,)), ((), ())), preferred_element_type=F32)


def _tn(a, b):
    return lax.dot_general(a, b, (((0,), (0,)), ((), ())), preferred_element_type=F32)


def _resident(shape):
    nd = len(shape)
    return pl.BlockSpec(shape, lambda i, _nd=nd: (0,) * _nd, pipeline_mode=pl.Buffered(1))


def _rows(tile, width):
    return pl.BlockSpec((tile, width), lambda i: (i, 0))


def _acc(shape):
    nd = len(shape)
    return pl.BlockSpec(shape, lambda i, _nd=nd: (0,) * _nd)


def _head_mean_matrix(width):
    idx = np.arange(width) // HEAD_DIM
    return jnp.asarray((idx[:, None] == idx[None, :]).astype(np.float32) / HEAD_DIM, dtype=BF16)


def _seg_mean(v, bmat):
    hi = v.astype(BF16)
    lo = (v - hi.astype(F32)).astype(BF16)
    return _nn(hi, bmat) + _nn(lo, bmat)


def _rms(x):
    return lax.rsqrt(jnp.mean(x * x, axis=-1, keepdims=True) + EPS)


def _rms_bwd(d_y, x, r, g):
    gy = d_y * g
    d_x = r * gy - x * (r * r * r) * jnp.mean(gy * x, axis=-1, keepdims=True)
    d_g = jnp.sum(d_y * (x * r), axis=0, keepdims=True)
    return d_x, d_g


def _lane_lo(shape):
    return lax.broadcasted_iota(jnp.int32, shape, 1) < HEAD_DIM


class _Rider:
    def __init__(self, inputs, out_shapes, sems, begin, end, middle=None):
        self.inputs, self.out_shapes, self.sems = list(inputs), list(out_shapes), list(sems)
        self.begin, self.middle, self.end = begin, middle, end


def _call(body, args, *, name, grid, in_specs, out_specs, out_shape, scratch_shapes=(), rider=None):
    in_specs, out_specs, out_shape, scratch_shapes = list(in_specs), list(out_specs), list(out_shape), list(scratch_shapes)
    if rider is None:
        outs = pl.pallas_call(body, name=name, grid=grid, in_specs=in_specs, out_specs=out_specs, out_shape=out_shape,
                              scratch_shapes=scratch_shapes)(*args)
        return list(outs), []
    n_in, n_out, n_scr = len(in_specs), len(out_shape), len(scratch_shapes)
    r_in, r_out = len(rider.inputs), len(rider.out_shapes)
    n_steps = grid[0]

    def hosted(*refs):
        ins, refs = refs[:n_in], refs[n_in:]
        r_ins, refs = refs[:r_in], refs[r_in:]
        outs, refs = refs[:n_out], refs[n_out:]
        r_outs, refs = refs[:r_out], refs[r_out:]
        scratch, r_sems = refs[:n_scr], refs[n_scr:]
        step = pl.program_id(0)

        @pl.when(step == 0)
        def _():
            rider.begin(r_ins, r_outs, r_sems)

        if rider.middle is not None:
            @pl.when(step == n_steps // 2)
            def _():
                rider.middle(r_ins, r_outs, r_sems)

        body(*ins, *outs, *scratch)

        @pl.when(step == n_steps - 1)
        def _():
            rider.end(r_ins, r_outs, r_sems)

    any_spec = pl.BlockSpec(memory_space=pl.ANY)
    outs = pl.pallas_call(
        hosted, name=name, grid=grid,
        in_specs=in_specs + [any_spec] * r_in,
        out_specs=out_specs + [any_spec] * r_out,
        out_shape=out_shape + rider.out_shapes,
        scratch_shapes=scratch_shapes + rider.sems,
    )(*args, *rider.inputs)
    return list(outs[:n_out]), list(outs[n_out:])


def _in_proj(x, g_attn, w_in_t, gq_t, gk_t, rider=None):
    s = x.shape[0]
    ts = min(TOKEN_TILE, s)

    def body(x_ref, g_ref, w_ref, gq_ref, gk_ref, bq_ref, bk_ref, zqk_ref, qn_ref, kn_ref, v_ref, u_ref):
        xf = x_ref[...]
        hn = ((xf * _rms(xf)) * g_ref[...]).astype(BF16)
        z = _nt(hn, w_ref[...])
        q = z[:, :ATTN_WIDTH]
        k = z[:, ATTN_WIDTH:ATTN_WIDTH + KV_WIDTH]
        zqk_ref[...] = z[:, :ATTN_WIDTH + KV_WIDTH]
        rq = lax.rsqrt(_seg_mean(q * q, bq_ref[...]) + EPS)
        qn_ref[...] = ((q * rq) * gq_ref[...]).astype(BF16)
        rk = lax.rsqrt(_seg_mean(k * k, bk_ref[...]) + EPS)
        kn_ref[...] = ((k * rk) * gk_ref[...]).astype(BF16)
        v_ref[...] = z[:, ATTN_WIDTH + KV_WIDTH:ATTN_WIDTH + 2 * KV_WIDTH].astype(BF16)
        u_ref[...] = z[:, ATTN_WIDTH + 2 * KV_WIDTH:]

    return _call(
        body,
        (x, g_attn, w_in_t, gq_t, gk_t, _head_mean_matrix(ATTN_WIDTH), _head_mean_matrix(KV_WIDTH)),
        name="in_proj",
        grid=(s // ts,),
        in_specs=[
            _rows(ts, D_MODEL),
            _resident((1, D_MODEL)),
            _resident((IN_WIDTH, D_MODEL)),
            _resident((1, ATTN_WIDTH)),
            _resident((1, KV_WIDTH)),
            _resident((ATTN_WIDTH, ATTN_WIDTH)),
            _resident((KV_WIDTH, KV_WIDTH)),
        ],
        out_specs=[
            _rows(ts, ATTN_WIDTH + KV_WIDTH),
            _rows(ts, ATTN_WIDTH),
            _rows(ts, KV_WIDTH),
            _rows(ts, KV_WIDTH),
            _rows(ts, POOL_WIDTH),
        ],
        out_shape=[
            jax.ShapeDtypeStruct((s, ATTN_WIDTH + KV_WIDTH), F32),
            jax.ShapeDtypeStruct((s, ATTN_WIDTH), BF16),
            jax.ShapeDtypeStruct((s, KV_WIDTH), BF16),
            jax.ShapeDtypeStruct((s, KV_WIDTH), BF16),
            jax.ShapeDtypeStruct((s, POOL_WIDTH), F32),
        ],
        rider=rider,
    )


def _bucket_ranges():
    n = np.arange(MAX_DISTANCE)
    max_exact = N_BUCKETS // 2
    nf = np.maximum(n, 1).astype(np.float64)
    large = max_exact + (np.log(nf / max_exact) / math.log(MAX_DISTANCE / max_exact) * (N_BUCKETS - max_exact)).astype(np.int64)
    bucket = np.where(n < max_exact, n, np.minimum(large, N_BUCKETS - 1))
    out = []
    for b in range(N_BUCKETS):
        idx = np.nonzero(bucket == b)[0]
        out.append((int(idx.min()), int(idx.max()) + 1))
    return out


def _band_distance():
    i = lax.broadcasted_iota(jnp.int32, (BLOCK, 2 * BLOCK), 0)
    j = lax.broadcasted_iota(jnp.int32, (BLOCK, 2 * BLOCK), 1)
    return BLOCK + i - j


def _bias_table(rel_bias_t):
    ranges = _bucket_ranges()

    def body(rb_ref, tab_ref):
        d = _band_distance()
        for half, heads in enumerate((HEADS_A, HEADS_B)):
            for slot, h in enumerate(heads):
                t = jnp.full((BLOCK, 2 * BLOCK), -jnp.inf, F32)
                for b, (lo, hi) in enumerate(ranges):
                    t = jnp.where((d >= lo) & (d < hi), rb_ref[h, b], t)
                tab_ref[half, slot * BLOCK:(slot + 1) * BLOCK, :] = t

    return pl.pallas_call(
        body,
        name="bias_table",
        in_specs=[pl.BlockSpec(memory_space=pltpu.SMEM)],
        out_shape=jax.ShapeDtypeStruct((2, 4 * BLOCK, 2 * BLOCK), F32),
    )(rel_bias_t)


def _bias_table_bwd(dl_acc, rider=None):
    ranges = _bucket_ranges()
    n_heads = len(HEADS_A) + len(HEADS_B)

    def body(dl_ref, out_ref):
        d = _band_distance()
        row = lax.broadcasted_iota(jnp.int32, (n_heads, SMALL_LANES), 0)
        lane = lax.broadcasted_iota(jnp.int32, (n_heads, SMALL_LANES), 1)
        out = jnp.zeros((n_heads, SMALL_LANES), F32)
        for half, heads in enumerate((HEADS_A, HEADS_B)):
            for slot, h in enumerate(heads):
                g = dl_ref[half, slot * BLOCK:(slot + 1) * BLOCK, :]
                for b, (lo, hi) in enumerate(ranges):
                    part = jnp.sum(jnp.where((d >= lo) & (d < hi), g, 0.0), axis=1, keepdims=True)
                    tot = jnp.sum(part, axis=0, keepdims=True)
                    out = jnp.where((row == h) & (lane == b), tot, out)
        out_ref[...] = out

    return _call(
        body,
        (dl_acc,),
        name="bias_table_bwd",
        grid=(1,),
        in_specs=[_acc((2, 4 * BLOCK, 2 * BLOCK))],
        out_specs=[_acc((n_heads, SMALL_LANES))],
        out_shape=[jax.ShapeDtypeStruct((n_heads, SMALL_LANES), F32)],
        rider=rider,
    )


def _stack_heads(pairs, lo_mask):
    zero = jnp.zeros_like(pairs[0])
    lo = [jnp.where(lo_mask, t, zero) for t in pairs]
    hi = [jnp.where(lo_mask, zero, t) for t in pairs]
    return (jnp.concatenate([lo[0], lo[1], hi[2], hi[3]], axis=0),
            jnp.concatenate([hi[0], hi[1], lo[2], lo[3]], axis=0))


def _unstack_heads(out_a, out_b, lo_mask):
    t = lambda x, r: x[r * BLOCK:(r + 1) * BLOCK, :]
    return [
        jnp.where(lo_mask, t(out_a, 0), t(out_b, 0)),
        jnp.where(lo_mask, t(out_a, 1), t(out_b, 1)),
        jnp.where(lo_mask, t(out_b, 2), t(out_a, 2)),
        jnp.where(lo_mask, t(out_b, 3), t(out_a, 3)),
    ]


def _sink_column(sink_ref, heads):
    row = lax.broadcasted_iota(jnp.int32, (4 * BLOCK, 1), 0)
    col = jnp.full((4 * BLOCK, 1), sink_ref[0, heads[3]], F32)
    for slot in (2, 1, 0):
        col = jnp.where(row < (slot + 1) * BLOCK, sink_ref[0, heads[slot]], col)
    return col


def _band_probs(q_stack, keys, tab, sink, first_block):
    s = _nt(q_stack, keys) * (HEAD_DIM ** -0.5) + tab
    col = lax.broadcasted_iota(jnp.int32, s.shape, 1)
    s = jnp.where(jnp.logical_and(first_block, col < BLOCK), -jnp.inf, s)
    m = jnp.maximum(jnp.max(s, axis=-1, keepdims=True), sink)
    e = jnp.exp(s - m)
    e_sink = jnp.exp(sink - m)
    den = jnp.sum(e, axis=-1, keepdims=True) + e_sink
    return e / den, e_sink / den


def _attn_specs(nb):
    cur = lambda n: (jnp.minimum(n, nb - 1), 0)
    prev = lambda n: (jnp.maximum(jnp.minimum(n, nb - 1) - 1, 0), 0)
    return cur, prev


def _attn_fwd(qn, kn, v, tab, sinks, rider=None):
    s = qn.shape[0]
    nb = s // BLOCK
    cur, prev = _attn_specs(nb)

    def body(sink_ref, q_ref, kc_ref, kp_ref, vc_ref, vp_ref, tab_ref, o_ref):
        first = pl.program_id(0) == 0
        lo_mask = _lane_lo((BLOCK, BLOCK))
        kk = jnp.concatenate([kp_ref[...], kc_ref[...]], axis=0)
        vv = jnp.concatenate([vp_ref[...], vc_ref[...]], axis=0)
        kk_sw = pltpu.roll(kk, HEAD_DIM, 1)
        vv_sw = pltpu.roll(vv, HEAD_DIM, 1)
        q_a, q_b = _stack_heads([q_ref[:, p * BLOCK:(p + 1) * BLOCK] for p in range(4)], lo_mask)
        p_a, _ = _band_probs(q_a, kk, tab_ref[0], _sink_column(sink_ref, HEADS_A), first)
        p_b, _ = _band_probs(q_b, kk_sw, tab_ref[1], _sink_column(sink_ref, HEADS_B), first)
        out = _unstack_heads(_nn(p_a.astype(BF16), vv), _nn(p_b.astype(BF16), vv_sw), lo_mask)
        for p in range(4):
            o_ref[:, p * BLOCK:(p + 1) * BLOCK] = out[p].astype(BF16)

    return _call(
        body,
        (sinks, qn, kn, kn, v, v, tab),
        name="attn_fwd",
        grid=(nb,),
        in_specs=[
            pl.BlockSpec(memory_space=pltpu.SMEM),
            pl.BlockSpec((BLOCK, ATTN_WIDTH), cur),
            pl.BlockSpec((BLOCK, KV_WIDTH), cur),
            pl.BlockSpec((BLOCK, KV_WIDTH), prev),
            pl.BlockSpec((BLOCK, KV_WIDTH), cur),
            pl.BlockSpec((BLOCK, KV_WIDTH), prev),
            _resident((2, 4 * BLOCK, 2 * BLOCK)),
        ],
        out_specs=[pl.BlockSpec((BLOCK, ATTN_WIDTH), cur)],
        out_shape=[jax.ShapeDtypeStruct((s, ATTN_WIDTH), BF16)],
        rider=rider,
    )


def _pooled(u_tile, u_halo, tile_index, tile_rows):
    halo = jnp.where(tile_index > 0, u_halo, 0.0)
    ext = jnp.concatenate([halo, u_tile], axis=0)
    sums = []
    acc = ext
    for shift in (1, 2, 4, 8):
        acc = acc + pltpu.roll(acc, shift, 0)
        sums.append(acc)
    t = tile_index * tile_rows + lax.broadcasted_iota(jnp.int32, (tile_rows, 1), 0)
    out = []
    for g, w in enumerate(POOL_SIZES):
        lanes = slice(g * POOL_GROUP, (g + 1) * POOL_GROUP)
        cnt = jnp.minimum(t + 1, w).astype(F32)
        out.append(sums[g][POOL_HALO:, lanes] / cnt - u_tile[:, lanes])
    return out


def _halo_before(tile):
    return lambda i: (jnp.maximum(i * (tile // POOL_HALO) - 1, 0), 0)


def _mix_out(u, a, x, w_out, w_pool, pool_scale, g_ffn, rider=None):
    s = x.shape[0]
    ts = min(TOKEN_TILE, s)

    def body(u_ref, uh_ref, a_ref, x_ref, wo_ref, wp_ref, sc_ref, g_ref, h1_ref, hn_ref, m_ref):
        i = pl.program_id(0)
        pooled = _pooled(u_ref[...], uh_ref[...], i, ts)
        for g in range(len(POOL_SIZES)):
            lanes = slice(g * POOL_GROUP, (g + 1) * POOL_GROUP)
            y = _nn(pooled[g].astype(BF16), wp_ref[g].astype(BF16))
            m_ref[:, lanes] = (y * sc_ref[:, lanes]).astype(BF16)
        h1 = x_ref[...] + _nn(a_ref[...], wo_ref[:ATTN_WIDTH, :]) + _nn(m_ref[...], wo_ref[ATTN_WIDTH:, :])
        h1_ref[...] = h1
        hn_ref[...] = ((h1 * _rms(h1)) * g_ref[...]).astype(BF16)

    return _call(
        body,
        (u, u, a, x, w_out, w_pool, pool_scale, g_ffn),
        name="mix_out",
        grid=(s // ts,),
        in_specs=[
            _rows(ts, POOL_WIDTH),
            pl.BlockSpec((POOL_HALO, POOL_WIDTH), _halo_before(ts)),
            _rows(ts, ATTN_WIDTH),
            _rows(ts, D_MODEL),
            _resident((D_MODEL, D_MODEL)),
            _resident((len(POOL_SIZES), POOL_GROUP, POOL_GROUP)),
            _resident((1, POOL_WIDTH)),
            _resident((1, D_MODEL)),
        ],
        out_specs=[_rows(ts, D_MODEL), _rows(ts, D_MODEL), _rows(ts, POOL_WIDTH)],
        out_shape=[
            jax.ShapeDtypeStruct((s, D_MODEL), F32),
            jax.ShapeDtypeStruct((s, D_MODEL), BF16),
            jax.ShapeDtypeStruct((s, POOL_WIDTH), BF16),
        ],
        rider=rider,
    )


def _ffn_up(hn2, wg_t, wu_t, rider=None):
    s = hn2.shape[0]
    ts = min(TOKEN_TILE, s)

    def body(hn_ref, wg_ref, wu_ref, gt_ref, up_ref):
        hn = hn_ref[...]
        for c in range(D_FF // FF_CHUNK):
            cols = slice(c * FF_CHUNK, (c + 1) * FF_CHUNK)
            gt_ref[:, cols] = _nt(hn, wg_ref[cols, :]).astype(BF16)
            up_ref[:, cols] = _nt(hn, wu_ref[cols, :]).astype(BF16)

    return _call(
        body,
        (hn2, wg_t, wu_t),
        name="ffn_up",
        grid=(s // ts,),
        in_specs=[_rows(ts, D_MODEL), _resident((D_FF, D_MODEL)), _resident((D_FF, D_MODEL))],
        out_specs=[_rows(ts, D_FF), _rows(ts, D_FF)],
        out_shape=[jax.ShapeDtypeStruct((s, D_FF), BF16), jax.ShapeDtypeStruct((s, D_FF), BF16)],
        rider=rider,
    )


def _silu_mul(gt, up):
    return (gt * jax.nn.sigmoid(gt)) * up


def _ffn_down(gt, up, h1, w_down, rider=None):
    s = h1.shape[0]
    ts = min(TOKEN_TILE, s)

    def body(gt_ref, up_ref, h1_ref, wd_ref, h2_ref, act_ref):
        for c in range(D_FF // FF_CHUNK):
            cols = slice(c * FF_CHUNK, (c + 1) * FF_CHUNK)
            act_ref[:, cols] = _silu_mul(gt_ref[:, cols].astype(F32), up_ref[:, cols].astype(F32)).astype(BF16)
        h2_ref[...] = h1_ref[...] + _nn(act_ref[...], wd_ref[...])

    return _call(
        body,
        (gt, up, h1, w_down),
        name="ffn_down",
        grid=(s // ts,),
        in_specs=[_rows(ts, D_FF), _rows(ts, D_FF), _rows(ts, D_MODEL), _resident((D_FF, D_MODEL))],
        out_specs=[_rows(ts, D_MODEL)],
        out_shape=[jax.ShapeDtypeStruct((s, D_MODEL), F32)],
        scratch_shapes=[pltpu.VMEM((ts, D_FF), BF16)],
        rider=rider,
    )


def _ple_fwd_bwd(h2, p, target, g_ple, w_pg, w_pp, rider=None):
    s = h2.shape[0]
    ts = min(TOKEN_TILE, s)
    blk = D_MODEL // N_DEV

    def body(h2_ref, p_ref, t_ref, g_ref, wpg_ref, wpp_ref, loss_ref, dh_ref, dhb_ref, dwpg_ref, dwpp_ref, dg_ref, pp_ref):
        @pl.when(pl.program_id(0) == 0)
        def _():
            loss_ref[...] = jnp.zeros_like(loss_ref)
            dwpg_ref[...] = jnp.zeros_like(dwpg_ref)
            dwpp_ref[...] = jnp.zeros_like(dwpp_ref)
            dg_ref[...] = jnp.zeros_like(dg_ref)

        h2v = h2_ref[...]
        r = _rms(h2v)
        hn = ((h2v * r) * g_ref[...]).astype(BF16)
        gate = jax.nn.sigmoid(_nn(hn, wpg_ref[...]))
        pb = p_ref[...].astype(BF16)
        for j in range(N_DEV):
            pp_ref[:, j * blk:(j + 1) * blk] = _nn(pb, wpp_ref[j])
        pp = pp_ref[...]
        diff = (h2v + gate * pp) - t_ref[...]
        loss_ref[...] += jnp.sum(jnp.sum(diff * diff, axis=0, keepdims=True), axis=1, keepdims=True) * (0.5 / D_MODEL)
        dy = diff * (1.0 / D_MODEL)
        d_pp = (dy * gate).astype(BF16)
        d_pre = ((dy * pp) * (gate * (1.0 - gate))).astype(BF16)
        for j in range(N_DEV):
            dwpp_ref[j] += _tn(pb, d_pp[:, j * blk:(j + 1) * blk])
        dwpg_ref[...] += _tn(hn, d_pre)
        d_x, d_g = _rms_bwd(_nt(d_pre, wpg_ref[...]), h2v, r, g_ref[...])
        dg_ref[...] += d_g
        dh = dy + d_x
        dh_ref[...] = dh
        dhb_ref[...] = dh.astype(BF16)

    return _call(
        body,
        (h2, p, target, g_ple, w_pg, w_pp),
        name="ple_fwd_bwd",
        grid=(s // ts,),
        in_specs=[
            _rows(ts, D_MODEL),
            _rows(ts, PLE_DIM),
            _rows(ts, D_MODEL),
            _resident((1, D_MODEL)),
            _resident((D_MODEL, D_MODEL)),
            _resident((N_DEV, PLE_DIM, blk)),
        ],
        out_specs=[
            _acc((1, SMALL_LANES)),
            _rows(ts, D_MODEL),
            _rows(ts, D_MODEL),
            _acc((D_MODEL, D_MODEL)),
            _acc((N_DEV, PLE_DIM, blk)),
            _acc((1, D_MODEL)),
        ],
        out_shape=[
            jax.ShapeDtypeStruct((1, SMALL_LANES), F32),
            jax.ShapeDtypeStruct((s, D_MODEL), F32),
            jax.ShapeDtypeStruct((s, D_MODEL), BF16),
            jax.ShapeDtypeStruct((D_MODEL, D_MODEL), F32),
            jax.ShapeDtypeStruct((N_DEV, PLE_DIM, blk), F32),
            jax.ShapeDtypeStruct((1, D_MODEL), F32),
        ],
        scratch_shapes=[pltpu.VMEM((ts, D_MODEL), F32)],
        rider=rider,
    )


def _ffn_bwd_act(dh2, dh2b, h1, gt, up, g_ffn, wg_t, wu_t, w_down, rider=None):
    s = h1.shape[0]
    ts = min(FFN_TOKEN_TILE, s)

    def body(dh_ref, dhb_ref, h1_ref, gt_ref, up_ref, g_ref, wg_ref, wu_ref, wd_ref,
             act_ref, dgt_ref, dup_ref, dh1_ref, dh1b_ref, dg_ref):
        @pl.when(pl.program_id(0) == 0)
        def _():
            dg_ref[...] = jnp.zeros_like(dg_ref)

        dhb = dhb_ref[...]
        for c in range(D_FF // FF_CHUNK):
            cols = slice(c * FF_CHUNK, (c + 1) * FF_CHUNK)
            d_act = _nt(dhb, wd_ref[cols, :])
            gtv = gt_ref[:, cols].astype(F32)
            upv = up_ref[:, cols].astype(F32)
            sg = jax.nn.sigmoid(gtv)
            silu = gtv * sg
            act_ref[:, cols] = (silu * upv).astype(BF16)
            dup_ref[:, cols] = (d_act * silu).astype(BF16)
            dgt_ref[:, cols] = ((d_act * upv) * (sg * (1.0 + gtv * (1.0 - sg)))).astype(BF16)
        d_hn = _nn(dgt_ref[...], wg_ref[...]) + _nn(dup_ref[...], wu_ref[...])
        h1v = h1_ref[...]
        d_x, d_g = _rms_bwd(d_hn, h1v, _rms(h1v), g_ref[...])
        dg_ref[...] += d_g
        dh1 = dh_ref[...] + d_x
        dh1_ref[...] = dh1
        dh1b_ref[...] = dh1.astype(BF16)

    return _call(
        body,
        (dh2, dh2b, h1, gt, up, g_ffn, wg_t, wu_t, w_down),
        name="ffn_bwd_act",
        grid=(s // ts,),
        in_specs=[
            _rows(ts, D_MODEL),
            _rows(ts, D_MODEL),
            _rows(ts, D_MODEL),
            _rows(ts, D_FF),
            _rows(ts, D_FF),
            _resident((1, D_MODEL)),
            _resident((D_FF, D_MODEL)),
            _resident((D_FF, D_MODEL)),
            _resident((D_FF, D_MODEL)),
        ],
        out_specs=[
            _rows(ts, D_FF), _rows(ts, D_FF), _rows(ts, D_FF),
            _rows(ts, D_MODEL), _rows(ts, D_MODEL), _acc((1, D_MODEL)),
        ],
        out_shape=[
            jax.ShapeDtypeStruct((s, D_FF), BF16),
            jax.ShapeDtypeStruct((s, D_FF), BF16),
            jax.ShapeDtypeStruct((s, D_FF), BF16),
            jax.ShapeDtypeStruct((s, D_MODEL), F32),
            jax.ShapeDtypeStruct((s, D_MODEL), BF16),
            jax.ShapeDtypeStruct((1, D_MODEL), F32),
        ],
        rider=rider,
    )


def _ffn_bwd_w(which, lhs, rhs, rider=None):
    s = rhs.shape[0]

    def body(lhs_ref, rhs_ref, dw_ref):
        dw_ref[...] = _tn(lhs_ref[...], rhs_ref[...])

    return _call(
        body,
        (lhs, rhs),
        name=f"ffn_bwd_{which}",
        grid=(D_FF // FF_CHUNK,),
        in_specs=[pl.BlockSpec((s, FF_CHUNK), lambda i: (0, i)), _resident((s, D_MODEL))],
        out_specs=[_rows(FF_CHUNK, D_MODEL)],
        out_shape=[jax.ShapeDtypeStruct((D_FF, D_MODEL), F32)],
        rider=rider,
    )


def _mix_bwd(dh1b, u, w_out, w_pool, pool_scale, rider=None):
    s = u.shape[0]
    ts = min(TOKEN_TILE, s)
    nt = s // ts
    halo_after = lambda i: (jnp.minimum((i + 1) * (ts // POOL_HALO), s // POOL_HALO - 1), 0)
    n_groups = len(POOL_SIZES)

    def body(dh_ref, dhn_ref, u_ref, uh_ref, wo_ref, wp_ref, sc_ref, da_ref, du_ref, dwp_ref, dsc_ref):
        i = pl.program_id(0)

        @pl.when(i == 0)
        def _():
            dwp_ref[...] = jnp.zeros_like(dwp_ref)
            dsc_ref[...] = jnp.zeros_like(dsc_ref)

        dh = dh_ref[...]
        da_ref[...] = _nt(dh, wo_ref[:ATTN_WIDTH, :])
        dh_next = jnp.where(i < nt - 1, dhn_ref[...], jnp.zeros_like(dhn_ref))
        dm_ext = _nt(jnp.concatenate([dh, dh_next], axis=0), wo_ref[ATTN_WIDTH:, :])
        pooled = _pooled(u_ref[...], uh_ref[...], i, ts)
        t_ext = i * ts + lax.broadcasted_iota(jnp.int32, (ts + POOL_HALO, 1), 0)
        for g, w in enumerate(POOL_SIZES):
            lanes = slice(g * POOL_GROUP, (g + 1) * POOL_GROUP)
            wp = wp_ref[g].astype(BF16)
            pg = pooled[g].astype(BF16)
            dm_g = dm_ext[:, lanes]
            dsc_ref[:, lanes] += jnp.sum(dm_g[:ts, :] * _nn(pg, wp), axis=0, keepdims=True)
            dy = (dm_g * sc_ref[:, lanes]).astype(BF16)
            dwp_ref[g] += _tn(pg, dy[:ts, :])
            d_pool = _nt(dy, wp)
            acc = d_pool / jnp.minimum(t_ext + 1, w).astype(F32)
            shift = 1
            while shift < w:
                acc = acc + pltpu.roll(acc, ts + POOL_HALO - shift, 0)
                shift *= 2
            du_ref[:, lanes] = acc[:ts, :] - d_pool[:ts, :]

    return _call(
        body,
        (dh1b, dh1b, u, u, w_out, w_pool, pool_scale),
        name="mix_bwd",
        grid=(nt,),
        in_specs=[
            _rows(ts, D_MODEL),
            pl.BlockSpec((POOL_HALO, D_MODEL), halo_after),
            _rows(ts, POOL_WIDTH),
            pl.BlockSpec((POOL_HALO, POOL_WIDTH), _halo_before(ts)),
            _resident((D_MODEL, D_MODEL)),
            _resident((n_groups, POOL_GROUP, POOL_GROUP)),
            _resident((1, POOL_WIDTH)),
        ],
        out_specs=[
            _rows(ts, ATTN_WIDTH),
            _rows(ts, POOL_WIDTH),
            _acc((n_groups, POOL_GROUP, POOL_GROUP)),
            _acc((1, POOL_WIDTH)),
        ],
        out_shape=[
            jax.ShapeDtypeStruct((s, ATTN_WIDTH), F32),
            jax.ShapeDtypeStruct((s, POOL_WIDTH), F32),
            jax.ShapeDtypeStruct((n_groups, POOL_GROUP, POOL_GROUP), F32),
            jax.ShapeDtypeStruct((1, POOL_WIDTH), F32),
        ],
        rider=rider,
    )


def _out_w_bwd(a, m, dh1b, rider=None):
    s = dh1b.shape[0]

    def body(a_ref, m_ref, dh_ref, dw_ref):
        @pl.when(pl.program_id(0) == 0)
        def _():
            dw_ref[...] = _tn(a_ref[...], dh_ref[...])

        @pl.when(pl.program_id(0) == 1)
        def _():
            dw_ref[...] = _tn(m_ref[...], dh_ref[...])

    return _call(
        body,
        (a, m, dh1b),
        name="out_w_bwd",
        grid=(2,),
        in_specs=[_resident((s, ATTN_WIDTH)), _resident((s, POOL_WIDTH)), _resident((s, D_MODEL))],
        out_specs=[_rows(ATTN_WIDTH, D_MODEL)],
        out_shape=[jax.ShapeDtypeStruct((D_MODEL, D_MODEL), F32)],
        rider=rider,
    )


def _attn_bwd(qn, kn, v, a, da, tab, sinks, rider=None):
    s = qn.shape[0]
    nb = s // BLOCK
    cur, prev = _attn_specs(nb)
    done = lambda n: (jnp.maximum(n - 1, 0), 0)

    def body(sink_ref, q_ref, kc_ref, kp_ref, vc_ref, vp_ref, o_ref, do_ref, tab_ref,
             dq_ref, dk_ref, dv_ref, dl_ref, ds_ref, k_carry, v_carry, sink_acc):
        n = pl.program_id(0)

        @pl.when(n == 0)
        def _():
            dl_ref[...] = jnp.zeros_like(dl_ref)
            k_carry[...] = jnp.zeros_like(k_carry)
            v_carry[...] = jnp.zeros_like(v_carry)
            sink_acc[...] = jnp.zeros_like(sink_acc)

        @pl.when(n < nb)
        def _():
            first = n == 0
            lo_mask = _lane_lo((BLOCK, BLOCK))
            kk = jnp.concatenate([kp_ref[...], kc_ref[...]], axis=0)
            vv = jnp.concatenate([vp_ref[...], vc_ref[...]], axis=0)
            keys = (kk, pltpu.roll(kk, HEAD_DIM, 1))
            vals = (vv, pltpu.roll(vv, HEAD_DIM, 1))
            q_st = _stack_heads([q_ref[:, p * BLOCK:(p + 1) * BLOCK] for p in range(4)], lo_mask)
            do_pairs = [do_ref[:, p * BLOCK:(p + 1) * BLOCK] for p in range(4)]
            do_st = _stack_heads(do_pairs, lo_mask)
            o_st = _stack_heads([o_ref[:, p * BLOCK:(p + 1) * BLOCK].astype(F32) for p in range(4)], lo_mask)
            dq_st, dk_parts, dv_parts = [], [], []
            for half, heads in enumerate((HEADS_A, HEADS_B)):
                probs, p_sink = _band_probs(q_st[half], keys[half], tab_ref[half], _sink_column(sink_ref, heads), first)
                delta = jnp.sum(do_st[half] * o_st[half], axis=-1, keepdims=True)
                dob = do_st[half].astype(BF16)
                dl = probs * (_nt(dob, vals[half]) - delta)
                dl_ref[half] += dl
                sink_acc[half] += p_sink * delta
                dsb = (dl * (HEAD_DIM ** -0.5)).astype(BF16)
                dq_st.append(_nn(dsb, keys[half]))
                dk_parts.append(_tn(dsb, q_st[half]))
                dv_parts.append(_tn(probs.astype(BF16), dob))
            dq = _unstack_heads(dq_st[0], dq_st[1], lo_mask)
            for p in range(4):
                dq_ref[:, p * BLOCK:(p + 1) * BLOCK] = dq[p]
            dk = dk_parts[0] + pltpu.roll(dk_parts[1], HEAD_DIM, 1)
            dv = dv_parts[0] + pltpu.roll(dv_parts[1], HEAD_DIM, 1)
            dk_ref[...] = k_carry[...] + dk[:BLOCK, :]
            dv_ref[...] = v_carry[...] + dv[:BLOCK, :]
            k_carry[...] = dk[BLOCK:, :]
            v_carry[...] = dv[BLOCK:, :]

        @pl.when(n == nb)
        def _():
            dk_ref[...] = k_carry[...]
            dv_ref[...] = v_carry[...]
            for half, heads in enumerate((HEADS_A, HEADS_B)):
                for slot, h in enumerate(heads):
                    tot = jnp.sum(sink_acc[half, slot * BLOCK:(slot + 1) * BLOCK, :], axis=0, keepdims=True)
                    ds_ref[h:h + 1, :] = jnp.broadcast_to(-tot, (1, SMALL_LANES))

    return _call(
        body,
        (sinks, qn, kn, kn, v, v, a, da, tab),
        name="attn_bwd",
        grid=(nb + 1,),
        in_specs=[
            pl.BlockSpec(memory_space=pltpu.SMEM),
            pl.BlockSpec((BLOCK, ATTN_WIDTH), cur),
            pl.BlockSpec((BLOCK, KV_WIDTH), cur),
            pl.BlockSpec((BLOCK, KV_WIDTH), prev),
            pl.BlockSpec((BLOCK, KV_WIDTH), cur),
            pl.BlockSpec((BLOCK, KV_WIDTH), prev),
            pl.BlockSpec((BLOCK, ATTN_WIDTH), cur),
            pl.BlockSpec((BLOCK, ATTN_WIDTH), cur),
            _resident((2, 4 * BLOCK, 2 * BLOCK)),
        ],
        out_specs=[
            pl.BlockSpec((BLOCK, ATTN_WIDTH), cur),
            pl.BlockSpec((BLOCK, KV_WIDTH), done),
            pl.BlockSpec((BLOCK, KV_WIDTH), done),
            _acc((2, 4 * BLOCK, 2 * BLOCK)),
            _acc((N_DEV, SMALL_LANES)),
        ],
        out_shape=[
            jax.ShapeDtypeStruct((s, ATTN_WIDTH), F32),
            jax.ShapeDtypeStruct((s, KV_WIDTH), F32),
            jax.ShapeDtypeStruct((s, KV_WIDTH), F32),
            jax.ShapeDtypeStruct((2, 4 * BLOCK, 2 * BLOCK), F32),
            jax.ShapeDtypeStruct((N_DEV, SMALL_LANES), F32),
        ],
        scratch_shapes=[
            pltpu.VMEM((BLOCK, KV_WIDTH), F32),
            pltpu.VMEM((BLOCK, KV_WIDTH), F32),
            pltpu.VMEM((2, 4 * BLOCK, 1), F32),
        ],
        rider=rider,
    )


def _fold_heads(acc):
    t = acc + pltpu.roll(acc, HEAD_DIM, 1)
    out = t[:, :SMALL_LANES]
    for g in range(1, acc.shape[1] // SMALL_LANES):
        out = out + t[:, g * SMALL_LANES:(g + 1) * SMALL_LANES]
    return out


def _in_proj_bwd(dqn, dkn, dv, du, zqk, x, dh1, g_attn, gq_t, gk_t, w_in_t, rider=None):
    s = x.shape[0]
    ts = min(TOKEN_TILE, s)
    nt = s // ts

    def head_norm_bwd(d_n, raw, g_t, bmat):
        r = lax.rsqrt(_seg_mean(raw * raw, bmat) + EPS)
        gy = d_n * g_t
        d_raw = r * gy - raw * (r * r * r) * _seg_mean(gy * raw, bmat)
        return d_raw, jnp.sum(d_n * (raw * r), axis=0, keepdims=True)

    def body(dqn_ref, dkn_ref, dv_ref, du_ref, zqk_ref, x_ref, dh1_ref, g_ref, gq_ref, gk_ref, w_ref, bq_ref, bk_ref,
             gx_ref, dw_ref, dg_ref, dgq_ref, dgk_ref, dz_ref, gq_acc, gk_acc):
        i = pl.program_id(0)

        @pl.when(i == 0)
        def _():
            dw_ref[...] = jnp.zeros_like(dw_ref)
            dg_ref[...] = jnp.zeros_like(dg_ref)
            gq_acc[...] = jnp.zeros_like(gq_acc)
            gk_acc[...] = jnp.zeros_like(gk_acc)

        d_q, d_gq = head_norm_bwd(dqn_ref[...], zqk_ref[:, :ATTN_WIDTH], gq_ref[...], bq_ref[...])
        d_k, d_gk = head_norm_bwd(dkn_ref[...], zqk_ref[:, ATTN_WIDTH:], gk_ref[...], bk_ref[...])
        gq_acc[...] += d_gq
        gk_acc[...] += d_gk
        dz_ref[:, :ATTN_WIDTH] = d_q.astype(BF16)
        dz_ref[:, ATTN_WIDTH:ATTN_WIDTH + KV_WIDTH] = d_k.astype(BF16)
        dz_ref[:, ATTN_WIDTH + KV_WIDTH:ATTN_WIDTH + 2 * KV_WIDTH] = dv_ref[...].astype(BF16)
        dz_ref[:, ATTN_WIDTH + 2 * KV_WIDTH:] = du_ref[...].astype(BF16)
        dz = dz_ref[...]
        xf = x_ref[...]
        r = _rms(xf)
        hn = ((xf * r) * g_ref[...]).astype(BF16)
        dw_ref[...] += _tn(dz, hn)
        d_x, d_g = _rms_bwd(_nn(dz, w_ref[...]), xf, r, g_ref[...])
        dg_ref[...] += d_g
        gx_ref[...] = dh1_ref[...] + d_x

        @pl.when(i == nt - 1)
        def _():
            dgq_ref[...] = _fold_heads(gq_acc[...])
            dgk_ref[...] = _fold_heads(gk_acc[...])

    return _call(
        body,
        (dqn, dkn, dv, du, zqk, x, dh1, g_attn, gq_t, gk_t, w_in_t,
      _head_mean_matrix(ATTN_WIDTH), _head_mean_matrix(KV_WIDTH)),
        name="in_proj_bwd",
        grid=(nt,),
        in_specs=[
            _rows(ts, ATTN_WIDTH),
            _rows(ts, KV_WIDTH),
            _rows(ts, KV_WIDTH),
            _rows(ts, POOL_WIDTH),
            _rows(ts, ATTN_WIDTH + KV_WIDTH),
            _rows(ts, D_MODEL),
            _rows(ts, D_MODEL),
            _resident((1, D_MODEL)),
            _resident((1, ATTN_WIDTH)),
            _resident((1, KV_WIDTH)),
            _resident((IN_WIDTH, D_MODEL)),
            _resident((ATTN_WIDTH, ATTN_WIDTH)),
            _resident((KV_WIDTH, KV_WIDTH)),
        ],
        out_specs=[
            _rows(ts, D_MODEL),
            _acc((IN_WIDTH, D_MODEL)),
            _acc((1, D_MODEL)),
            _acc((1, SMALL_LANES)),
            _acc((1, SMALL_LANES)),
        ],
        out_shape=[
            jax.ShapeDtypeStruct((s, D_MODEL), F32),
            jax.ShapeDtypeStruct((IN_WIDTH, D_MODEL), F32),
            jax.ShapeDtypeStruct((1, D_MODEL), F32),
            jax.ShapeDtypeStruct((1, SMALL_LANES), F32),
            jax.ShapeDtypeStruct((1, SMALL_LANES), F32),
        ],
        scratch_shapes=[
            pltpu.VMEM((ts, IN_WIDTH), BF16),
            pltpu.VMEM((1, ATTN_WIDTH), F32),
            pltpu.VMEM((1, KV_WIDTH), F32),
        ],
        rider=rider,
    )


BIG_WEIGHTS = (
    ("w_in", True, IN_WIDTH // N_DEV, D_MODEL),
    ("w_out", False, D_MODEL // N_DEV, D_MODEL),
    ("w_gate", True, D_FF // N_DEV, D_MODEL),
    ("w_up", True, D_FF // N_DEV, D_MODEL),
    ("w_down", False, D_FF // N_DEV, D_MODEL),
    ("w_ple_gate", False, D_MODEL // N_DEV, D_MODEL),
    ("w_ple_proj", False, PLE_DIM, D_MODEL // N_DEV),
)
N_BIG = len(BIG_WEIGHTS)


def _place():
    x, y, c = lax.axis_index("x"), lax.axis_index("y"), lax.axis_index("c")
    chips = [(1 - x, y), (x, 1 - y), (1 - x, 1 - y)]
    return x, y, c, chips


class _Gather:
    def __init__(self, n):
        self.n = n
        self.sems = [pltpu.SemaphoreType.DMA((n, 7)), pltpu.SemaphoreType.DMA((n, 7)), pltpu.SemaphoreType.DMA((n,))]

    def _ctx(self, srcs, outs, sems):
        send_sems, recv_sems, local_sems = sems
        x, y, c, chips = _place()
        me, sibling = (x, y, c), (x, y, 1 - c)

        def block(k, owner):
            px, py, pc = owner
            return outs[k].at[4 * px + 2 * py + pc]

        def copy(k, idx, owner, to, src=None):
            return pltpu.make_async_remote_copy(
                src_ref=block(k, owner) if src is None else src, dst_ref=block(k, owner),
                send_sem=send_sems.at[k, idx], recv_sem=recv_sems.at[k, idx], device_id=to, device_id_type=MESH)

        def local(k):
            return pltpu.make_async_copy(srcs[k], block(k, me), local_sems.at[k])

        return c, chips, me, sibling, copy, local

    def begin(self, srcs, outs, sems):
        c, chips, me, sibling, copy, local = self._ctx(srcs, outs, sems)
        for k in range(self.n):
            local(k).start()
            copy(k, 0, me, sibling, src=srcs[k]).start()
            for j, chip in enumerate(chips):
                copy(k, 1 + j, me, (*chip, c), src=srcs[k]).start()

    def middle(self, srcs, outs, sems):
        c, chips, me, sibling, copy, local = self._ctx(srcs, outs, sems)
        for j, chip in enumerate(chips):
            for k in range(self.n):
                copy(k, 1 + j, (*chip, c), me).wait_recv()
                copy(k, 4 + j, (*chip, c), sibling).start()

    def end(self, srcs, outs, sems):
        c, chips, me, sibling, copy, local = self._ctx(srcs, outs, sems)
        for k in range(self.n):
            copy(k, 0, sibling, me).wait_recv()
            for j, chip in enumerate(chips):
                copy(k, 4 + j, (*chip, 1 - c), me).wait_recv()
        for k in range(self.n):
            copy(k, 0, me, sibling, src=srcs[k]).wait_send()
            for j, chip in enumerate(chips):
                copy(k, 1 + j, me, (*chip, c), src=srcs[k]).wait_send()
                copy(k, 4 + j, (*chip, c), sibling).wait_send()
            local(k).wait()


def _gather_rider(blocks):
    g = _Gather(len(blocks))
    shapes = [jax.ShapeDtypeStruct((N_DEV, *b.shape), b.dtype) for b in blocks]
    return _Rider(blocks, shapes, g.sems, g.begin, g.end, g.middle)


def _cast_and_gather_first(shards):
    g = _Gather(1)
    any_spec = pl.BlockSpec(memory_space=pl.ANY)
    vmem = pl.BlockSpec(memory_space=pltpu.VMEM)

    def body(*refs):
        ins, outs, gathered, sems = refs[:N_BIG], refs[N_BIG:2 * N_BIG], refs[2 * N_BIG], refs[2 * N_BIG + 1:]
        outs[0][...] = ins[0][...].astype(BF16)
        g.begin(outs[:1], [gathered], sems)
        for k in range(1, N_BIG):
            outs[k][...] = ins[k][...].astype(BF16)
        g.middle(outs[:1], [gathered], sems)
        g.end(outs[:1], [gathered], sems)

    res = pl.pallas_call(
        body,
        name="cast_and_gather_first",
        in_specs=[vmem] * N_BIG,
        out_specs=[vmem] * N_BIG + [any_spec],
        out_shape=[jax.ShapeDtypeStruct((r, c), BF16) for _, _, r, c in BIG_WEIGHTS]
        + [jax.ShapeDtypeStruct((N_DEV, *BIG_WEIGHTS[0][2:]), BF16)],
        scratch_shapes=g.sems,
    )(*shards)
    return list(res[:N_BIG]), res[N_BIG]


def _sibling_rider(grads):
    n = len(grads)

    def copies(gs, lands, sems):
        send_sems, recv_sems = sems
        x, y, c, _ = _place()
        return [
            pltpu.make_async_remote_copy(
                src_ref=gs[k].at[:, 1 - c], dst_ref=lands[k], send_sem=send_sems.at[k], recv_sem=recv_sems.at[k],
                device_id=(x, y, 1 - c), device_id_type=MESH)
            for k in range(n)
        ]

    def begin(gs, lands, sems):
        for cp in copies(gs, lands, sems):
            cp.start()

    def end(gs, lands, sems):
        for cp in copies(gs, lands, sems):
            cp.wait()

    shapes = [jax.ShapeDtypeStruct((N_CHIPS, *g.shape[2:]), F32) for g in grads]
    return _Rider(grads, shapes, [pltpu.SemaphoreType.DMA((n,)), pltpu.SemaphoreType.DMA((n,))], begin, end)


def _chip_sum(k, place, grad, from_sibling):
    _, _, r, c = BIG_WEIGHTS[k]

    def body(place_ref, g_ref, l_ref, own_ref, send_ref):
        q = pl.program_id(0)
        tot = g_ref[0, 0] + l_ref[0]
        mine = q == 2 * place_ref[0] + place_ref[1]

        @pl.when(mine)
        def _():
            own_ref[...] = tot

        send_ref[0] = jnp.where(mine, 0.0, tot).astype(BF16)

    return pl.pallas_call(
        body,
        name=f"chip_sum_{BIG_WEIGHTS[k][0]}",
        grid_spec=pltpu.PrefetchScalarGridSpec(
            num_scalar_prefetch=1,
            grid=(N_CHIPS,),
            in_specs=[
                pl.BlockSpec((1, 1, r, c), lambda q, place: (q, place[2], 0, 0)),
                pl.BlockSpec((1, r, c), lambda q, place: (q, 0, 0)),
            ],
            out_specs=[
                pl.BlockSpec((r, c), lambda q, place: (0, 0)),
                pl.BlockSpec((1, r, c), lambda q, place: (q, 0, 0)),
            ],
        ),
        out_shape=[jax.ShapeDtypeStruct((r, c), F32), jax.ShapeDtypeStruct((N_CHIPS, r, c), BF16)],
    )(place, grad, from_sibling)


def _chips_rider(to_send, small=None):
    n = len(to_send)
    inputs = list(to_send) + ([] if small is None else [small])
    shapes = [jax.ShapeDtypeStruct((3, *t.shape[1:]), BF16) for t in to_send]
    sems = [pltpu.SemaphoreType.DMA((max(n, 1), 3)), pltpu.SemaphoreType.DMA((max(n, 1), 3))]
    if small is not None:
        shapes.append(jax.ShapeDtypeStruct((N_DEV, *small.shape), F32))
        sems += [pltpu.SemaphoreType.DMA((7,)), pltpu.SemaphoreType.DMA((7,)), pltpu.SemaphoreType.DMA]

    def copies(ins, outs, sem_refs):
        x, y, c, chips = _place()
        out = []
        for k in range(n):
            for j, (px, py) in enumerate(chips):
                out.append(pltpu.make_async_remote_copy(
                    src_ref=ins[k].at[2 * px + py], dst_ref=outs[k].at[j],
                    send_sem=sem_refs[0].at[k, j], recv_sem=sem_refs[1].at[k, j],
                    device_id=(px, py, c), device_id_type=MESH))
        local = None
        if small is not None:
            me = 4 * x + 2 * y + c
            local = pltpu.make_async_copy(ins[n], outs[n].at[me], sem_refs[4])
            rel = 0
            for fx in (0, 1):
                for fy in (0, 1):
                    for fc in (0, 1):
                        if (fx, fy, fc) != (0, 0, 0):
                            out.append(pltpu.make_async_remote_copy(
                                src_ref=ins[n], dst_ref=outs[n].at[me],
                                send_sem=sem_refs[2].at[rel], recv_sem=sem_refs[3].at[rel],
                                device_id=(x ^ fx, y ^ fy, c ^ fc), device_id_type=MESH))
                            rel += 1
        return out, local

    def begin(ins, outs, sem_refs):
        remote, local = copies(ins, outs, sem_refs)
        if local is not None:
            local.start()
        for cp in remote:
            cp.start()

    def end(ins, outs, sem_refs):
        remote, local = copies(ins, outs, sem_refs)
        for cp in remote:
            cp.wait()
        if local is not None:
            local.wait()

    return _Rider(inputs, shapes, sems, begin, end)


def _exchange(name, rider):
    return _call(lambda: None, (), name=name, grid=(1,), in_specs=[], out_specs=[], out_shape=[], rider=rider)[1]


def _merge_riders(*riders):
    riders = [r for r in riders if r is not None]
    if len(riders) == 1:
        return riders[0]

    def split(refs, counts):
        out, at = [], 0
        for n in counts:
            out.append(refs[at:at + n])
            at += n
        return out

    def run(which):
        def fn(ins, outs, sems):
            parts = zip(riders, split(ins, [len(r.inputs) for r in riders]),
                        split(outs, [len(r.out_shapes) for r in riders]), split(sems, [len(r.sems) for r in riders]))
            for r, i, o, s in parts:
                hook = getattr(r, which)
                if hook is not None:
                    hook(i, o, s)
        return fn

    middle = run("middle") if any(r.middle is not None for r in riders) else None
    return _Rider(sum((r.inputs for r in riders), []), sum((r.out_shapes for r in riders), []),
                  sum((r.sems for r in riders), []), run("begin"), run("end"), middle)


def _split_outputs(outs, *riders):
    res, at = [], 0
    for r in riders:
        res.append(outs[at:at + len(r.out_shapes)])
        at += len(r.out_shapes)
    return res


def _adamw(w, g, m, v):
    m = ADAM_B1 * m + (1.0 - ADAM_B1) * g
    v = ADAM_B2 * v + (1.0 - ADAM_B2) * jnp.square(g)
    m_hat = m / (1.0 - ADAM_B1 ** ADAM_STEP)
    v_hat = v / (1.0 - ADAM_B2 ** ADAM_STEP)
    delta = -ADAM_LR * (m_hat / (jnp.sqrt(v_hat) + ADAM_EPS) + ADAM_WD * w)
    return delta, m, v


def _adamw_big(k, own, landed, w, m, v, rider=None):
    name, _, r, c = BIG_WEIGHTS[k]
    tile = r // 2
    tiles = lambda i: (i, 0)

    def body(own_ref, land_ref, w_ref, m_ref, v_ref, g_ref, d_ref, nm_ref, nv_ref):
        g = ((own_ref[...] + land_ref[0].astype(F32)) + land_ref[1].astype(F32)) + land_ref[2].astype(F32)
        g_ref[...] = g
        d_ref[...], nm_ref[...], nv_ref[...] = _adamw(w_ref[...], g, m_ref[...], v_ref[...])

    return _call(
        body,
        (own, landed, w, m, v),
        name=f"adamw_{name}",
        grid=(r // tile,),
        in_specs=[pl.BlockSpec((tile, c), tiles), pl.BlockSpec((3, tile, c), lambda i: (0, i, 0))]
        + [pl.BlockSpec((tile, c), tiles)] * 3,
        out_specs=[pl.BlockSpec((tile, c), tiles)] * 4,
        out_shape=[jax.ShapeDtypeStruct((r, c), F32)] * 4,
        rider=rider,
    )


def _sum_small(parts_list):
    n = len(parts_list)

    def body(*refs):
        for p_ref, out_ref in zip(refs[:n], refs[n:]):
            tot = p_ref[0]
            for j in range(1, N_DEV):
                tot = tot + p_ref[j]
            out_ref[...] = tot

    return pl.pallas_call(body, name="sum_small",
                          out_shape=[jax.ShapeDtypeStruct(p.shape[1:], F32) for p in parts_list])(*parts_list)


def _adamw_small(grads, ws, ms, vs):
    n = len(grads)

    def body(*refs):
        g_refs, w_refs, m_refs, v_refs = refs[:n], refs[n:2 * n], refs[2 * n:3 * n], refs[3 * n:4 * n]
        outs = refs[4 * n:]
        for i in range(n):
            d, nm, nv = _adamw(w_refs[i][...], g_refs[i][...], m_refs[i][...], v_refs[i][...])
            outs[i][...] = d
            outs[n + i][...] = nm
            outs[2 * n + i][...] = nv

    shapes = [jax.ShapeDtypeStruct(w.shape, F32) for w in ws]
    return pl.pallas_call(body, name="adamw_small", out_shape=shapes * 3)(*grads, *ws, *ms, *vs)


SMALL_NAMES = ("g_attn_norm", "g_q", "g_k", "attn_sinks", "rel_bias", "w_pool", "pool_scale", "g_ffn_norm", "g_ple_norm")


def _pack_small(arrays):
    rows, offsets = [], []
    at = 0
    for a in arrays:
        flat = a.reshape(-1)
        n_rows = -(-flat.shape[0] // (8 * SMALL_LANES)) * 8
        flat = jnp.pad(flat, (0, n_rows * SMALL_LANES - flat.shape[0]))
        rows.append(flat.reshape(n_rows, SMALL_LANES))
        offsets.append(at)
        at += n_rows
    return jnp.concatenate(rows, axis=0), offsets


def kernel(x, p, w_in, w_out, g_attn_norm, g_q, g_k, attn_sinks, rel_bias, w_pool, pool_scale, g_ffn_norm, w_gate, w_up, w_down, g_ple_norm, w_ple_gate, w_ple_proj, loss_target, m_w_in, m_w_out, m_g_attn_norm, m_g_q, m_g_k, m_attn_sinks, m_rel_bias, m_w_pool, m_pool_scale, m_g_ffn_norm, m_w_gate, m_w_up, m_w_down, m_g_ple_norm, m_w_ple_gate, m_w_ple_proj, v_w_in, v_w_out, v_g_attn_norm, v_g_q, v_g_k, v_attn_sinks, v_rel_bias, v_w_pool, v_pool_scale, v_g_ffn_norm, v_w_gate, v_w_up, v_w_down, v_g_ple_norm, v_w_ple_gate, v_w_ple_proj):
    weights = dict(w_in=w_in, w_out=w_out, g_attn_norm=g_attn_norm, g_q=g_q, g_k=g_k, attn_sinks=attn_sinks,
                   rel_bias=rel_bias, w_pool=w_pool, pool_scale=pool_scale, g_ffn_norm=g_ffn_norm, w_gate=w_gate,
                   w_up=w_up, w_down=w_down, g_ple_norm=g_ple_norm, w_ple_gate=w_ple_gate, w_ple_proj=w_ple_proj)
    m_in = dict(w_in=m_w_in, w_out=m_w_out, g_attn_norm=m_g_attn_norm, g_q=m_g_q, g_k=m_g_k, attn_sinks=m_attn_sinks,
                rel_bias=m_rel_bias, w_pool=m_w_pool, pool_scale=m_pool_scale, g_ffn_norm=m_g_ffn_norm, w_gate=m_w_gate,
                w_up=m_w_up, w_down=m_w_down, g_ple_norm=m_g_ple_norm, w_ple_gate=m_w_ple_gate, w_ple_proj=m_w_ple_proj)
    v_in = dict(w_in=v_w_in, w_out=v_w_out, g_attn_norm=v_g_attn_norm, g_q=v_g_q, g_k=v_g_k, attn_sinks=v_attn_sinks,
                rel_bias=v_rel_bias, w_pool=v_w_pool, pool_scale=v_pool_scale, g_ffn_norm=v_g_ffn_norm, w_gate=v_w_gate,
                w_up=v_w_up, w_down=v_w_down, g_ple_norm=v_g_ple_norm, w_ple_gate=v_w_ple_gate, w_ple_proj=v_w_ple_proj)

    xs = x[0]
    ps = p[0, 0]
    target = loss_target[0]
    wp = w_pool[0]
    gq_t = jnp.tile(g_q, (1, ATTN_WIDTH // HEAD_DIM))
    gk_t = jnp.tile(g_k, (1, KV_WIDTH // HEAD_DIM))

    def to_blocks(k, arr):
        return jnp.swapaxes(arr[0], 0, 1) if BIG_WEIGHTS[k][1] else arr[0]

    def from_blocks(k, arr):
        return (jnp.swapaxes(arr, 0, 1) if BIG_WEIGHTS[k][1] else arr)[None]

    IN, OUT, GATE, UP, DOWN, PG, PP = range(N_BIG)
    full = lambda g: g.reshape(N_DEV * g.shape[1], g.shape[2])
    halves = lambda k, g: g.reshape(N_CHIPS, 2, *BIG_WEIGHTS[k][2:])
    place = jnp.stack([lax.axis_index("x"), lax.axis_index("y"), lax.axis_index("c")]).astype(jnp.int32)

    sh, w_in_g = _cast_and_gather_first([to_blocks(k, weights[name]) for k, (name, _, _, _) in enumerate(BIG_WEIGHTS)])
    w_in_t = full(w_in_g)

    tab = _bias_table(rel_bias.T)
    (zqk, qn, kn, v, u), (w_out_g, w_pp_g) = _in_proj(xs, g_attn_norm, w_in_t, gq_t, gk_t,
                                                    rider=_gather_rider([sh[OUT], sh[PP]]))
    (a,), (wg_g,) = _attn_fwd(qn, kn, v, tab, attn_sinks, rider=_gather_rider([sh[GATE]]))
    w_out_f = full(w_out_g)
    (h1, hn2, m_out), (wu_g,) = _mix_out(u, a, xs, w_out_f, wp, pool_scale, g_ffn_norm, rider=_gather_rider([sh[UP]]))
    wg_t, wu_t = full(wg_g), full(wu_g)
    (gt, up), (wd_g, w_pg_g) = _ffn_up(hn2, wg_t, wu_t, rider=_gather_rider([sh[DOWN], sh[PG]]))
    w_down_f = full(wd_g)
    (h2,), _ = _ffn_down(gt, up, h1, w_down_f)

    sums, landed = [None] * N_BIG, [None] * N_BIG

    def chip_sum(k, grad, from_sibling):
        sums[k] = _chip_sum(k, place, halves(k, grad), from_sibling)

    (loss_part, dh2, dh2b, d_wpg, d_wpp, d_g_ple), _ = _ple_fwd_bwd(h2, ps, target, g_ple_norm, full(w_pg_g), w_pp_g)
    (act, dgt, dup, dh1, dh1b, d_g_ffn), sib = _ffn_bwd_act(
        dh2, dh2b, h1, gt, up, g_ffn_norm, wg_t, wu_t, w_down_f,
        rider=_sibling_rider([halves(PG, d_wpg), halves(PP, d_wpp)]))
    chip_sum(PG, d_wpg, sib[0])
    chip_sum(PP, d_wpp, sib[1])
    (d_wd,), (landed[PG], landed[PP]) = _ffn_bwd_w("down", act, dh2b, rider=_chips_rider([sums[PG][1], sums[PP][1]]))
    (d_wo,), sib = _out_w_bwd(a, m_out, dh1b, rider=_sibling_rider([halves(DOWN, d_wd)]))
    chip_sum(DOWN, d_wd, sib[0])
    r_sib, r_chips = _sibling_rider([halves(OUT, d_wo)]), _chips_rider([sums[DOWN][1]])
    (d_wg_t,), outs = _ffn_bwd_w("gate", dgt, hn2, rider=_merge_riders(r_sib, r_chips))
    sib, (landed[DOWN],) = _split_outputs(outs, r_sib, r_chips)
    chip_sum(OUT, d_wo, sib[0])
    r_sib, r_chips = _sibling_rider([halves(GATE, d_wg_t)]), _chips_rider([sums[OUT][1]])
    (d_wu_t,), outs = _ffn_bwd_w("up", dup, hn2, rider=_merge_riders(r_sib, r_chips))
    sib, (landed[OUT],) = _split_outputs(outs, r_sib, r_chips)
    chip_sum(GATE, d_wg_t, sib[0])
    r_sib, r_chips = _sibling_rider([halves(UP, d_wu_t)]), _chips_rider([sums[GATE][1]])
    (da, du, d_wpool, d_scale), outs = _mix_bwd(dh1b, u, w_out_f, wp, pool_scale, rider=_merge_riders(r_sib, r_chips))
    sib, (landed[GATE],) = _split_outputs(outs, r_sib, r_chips)
    chip_sum(UP, d_wu_t, sib[0])
    early, early_at = _pack_small([d_wpool, d_scale, d_g_ffn, d_g_ple, loss_part[:, :1]])
    (dqn, dkn, dv, dl_acc, d_sinks), (landed[UP], early_all) = _attn_bwd(
        qn, kn, v, a, da, tab, attn_sinks, rider=_chips_rider([sums[UP][1]], early))
    (grad_x, d_win_t, d_g_attn, d_gq, d_gk), _ = _in_proj_bwd(dqn, dkn, dv, du, zqk, xs, dh1, g_attn_norm, gq_t, gk_t, w_in_t)
    (d_rel_t,), sib = _bias_table_bwd(dl_acc, rider=_sibling_rider([halves(IN, d_win_t)]))
    chip_sum(IN, d_win_t, sib[0])
    late, late_at = _pack_small([d_g_attn, d_gq[:, :HEAD_DIM], d_gk[:, :HEAD_DIM], d_sinks[:, 0], d_rel_t[:, :N_BUCKETS]])
    landed[IN], late_all = _exchange("last_exchange", _chips_rider([sums[IN][1]], late))

    out = {"grad": {}, "delta": {}, "new_m": {}, "new_v": {}}
    for k, (name, _, _, _) in enumerate(BIG_WEIGHTS):
        res, _ = _adamw_big(k, sums[k][0], landed[k], to_blocks(k, weights[name]), to_blocks(k, m_in[name]),
                            to_blocks(k, v_in[name]))
        for kind, r in zip(("grad", "delta", "new_m", "new_v"), res):
            out[kind][name] = from_blocks(k, r)
    early_sum, late_sum = _sum_small([early_all, late_all])

    def unpack(packed, at, shape):
        n = math.prod(shape)
        return packed[at:at + -(-n // SMALL_LANES)].reshape(-1)[:n].reshape(shape)

    small_grads = dict(
        w_pool=unpack(early_sum, early_at[0], w_pool.shape), pool_scale=unpack(early_sum, early_at[1], pool_scale.shape),
        g_ffn_norm=unpack(early_sum, early_at[2], g_ffn_norm.shape), g_ple_norm=unpack(early_sum, early_at[3], g_ple_norm.shape),
        g_attn_norm=unpack(late_sum, late_at[0], g_attn_norm.shape), g_q=unpack(late_sum, late_at[1], g_q.shape),
        g_k=unpack(late_sum, late_at[2], g_k.shape), attn_sinks=unpack(late_sum, late_at[3], attn_sinks.shape),
        rel_bias=unpack(late_sum, late_at[4], rel_bias.T.shape))
    loss = early_sum[early_at[4], 0]
    flip = lambda name, arr: arr.T if name == "rel_bias" else arr
    updates = _adamw_small([small_grads[n] for n in SMALL_NAMES], [flip(n, weights[n]) for n in SMALL_NAMES],
                           [flip(n, m_in[n]) for n in SMALL_NAMES], [flip(n, v_in[n]) for n in SMALL_NAMES])
    n_small = len(SMALL_NAMES)
    for i, name in enumerate(SMALL_NAMES):
        out["grad"][name] = flip(name, small_grads[name])
        out["delta"][name] = flip(name, updates[i])
        out["new_m"][name] = flip(name, updates[n_small + i])
        out["new_v"][name] = flip(name, updates[2 * n_small + i])

    order = ("w_in", "w_out", "g_attn_norm", "g_q", "g_k", "attn_sinks", "rel_bias", "w_pool", "pool_scale",
             "g_ffn_norm", "w_gate", "w_up", "w_down", "g_ple_norm", "w_ple_gate", "w_ple_proj")
    return (loss, grad_x[None], *[out["grad"][n] for n in order], *[out["delta"][n] for n in order],
            *[out["new_m"][n] for n in order], *[out["new_v"][n] for n in order])
```

```python
import functools
import math

import jax
import jax.numpy as jnp
import numpy as np
from jax import lax
from jax.experimental import pallas as pl
from jax.experimental.pallas import tpu as pltpu

F32 = jnp.float32
BF16 = jnp.bfloat16
MESH = pl.DeviceIdType.MESH

D_MODEL = 1024
HEAD_DIM = 64
ATTN_WIDTH = 512
KV_WIDTH = 128
POOL_WIDTH = 512
POOL_SIZES = (2, 4, 8, 16)
POOL_GROUP = 128
POOL_HALO = 16
IN_WIDTH = 1280
D_FF = 2816
PLE_DIM = 256
BLOCK = 128
N_BUCKETS = 32
MAX_DISTANCE = 128
EPS = 1e-6
N_DEV = 8
N_CHIPS = 4

ADAM_LR = 0.001
ADAM_B1 = 0.9
ADAM_B2 = 0.999
ADAM_EPS = 1e-08
ADAM_WD = 0.01
ADAM_STEP = 10

TOKEN_TILE = 512
FFN_TOKEN_TILE = 256
FF_CHUNK = 256
HEADS_A = (0, 2, 5, 7)
HEADS_B = (1, 3, 4, 6)
SMALL_LANES = 128


def _nn(a, b):
    return jnp.dot(a, b, preferred_element_type=F32)


def _nt(a, b):
    return lax.dot_general(a, b, (((1,), (1,)), ((), ())), preferred_element_type=F32)


def _tn(a, b):
    return lax.dot_general(a, b, (((0,), (0,)), ((), ())), preferred_element_type=F32)


def _resident(shape):
    nd = len(shape)
    return pl.BlockSpec(shape, lambda i, _nd=nd: (0,) * _nd, pipeline_mode=pl.Buffered(1))


def _rows(tile, width):
    return pl.BlockSpec((tile, width), lambda i: (i, 0))


def _acc(shape):
    nd = len(shape)
    return pl.BlockSpec(shape, lambda i, _nd=nd: (0,) * _nd)


def _head_mean_matrix(width):
    idx = np.arange(width) // HEAD_DIM
    return jnp.asarray((idx[:, None] == idx[None, :]).astype(np.float32) / HEAD_DIM, dtype=BF16)


def _seg_mean(v, bmat):
    hi = v.astype(BF16)
    lo = (v - hi.astype(F32)).astype(BF16)
    return _nn(hi, bmat) + _nn(lo, bmat)


def _rms(x):
    return lax.rsqrt(jnp.mean(x * x, axis=-1, keepdims=True) + EPS)


def _rms_bwd(d_y, x, r, g):
    gy = d_y * g
    d_x = r * gy - x * (r * r * r) * jnp.mean(gy * x, axis=-1, keepdims=True)
    d_g = jnp.sum(d_y * (x * r), axis=0, keepdims=True)
    return d_x, d_g


def _lane_lo(shape):
    return lax.broadcasted_iota(jnp.int32, shape, 1) < HEAD_DIM


class _Rider:
    def __init__(self, inputs, out_shapes, sems, begin, end, middle=None):
        self.inputs, self.out_shapes, self.sems = list(inputs), list(out_shapes), list(sems)
        self.begin, self.middle, self.end = begin, middle, end


def _call(body, args, *, name, grid, in_specs, out_specs, out_shape, scratch_shapes=(), rider=None):
    in_specs, out_specs, out_shape, scratch_shapes = list(in_specs), list(out_specs), list(out_shape), list(scratch_shapes)
    if rider is None:
        outs = pl.pallas_call(body, name=name, grid=grid, in_specs=in_specs, out_specs=out_specs, out_shape=out_shape,
                              scratch_shapes=scratch_shapes)(*args)
        return list(outs), []
    n_in, n_out, n_scr = len(in_specs), len(out_shape), len(scratch_shapes)
    r_in, r_out = len(rider.inputs), len(rider.out_shapes)
    n_steps = grid[0]

    def hosted(*refs):
        ins, refs = refs[:n_in], refs[n_in:]
        r_ins, refs = refs[:r_in], refs[r_in:]
        outs, refs = refs[:n_out], refs[n_out:]
        r_outs, refs = refs[:r_out], refs[r_out:]
        scratch, r_sems = refs[:n_scr], refs[n_scr:]
        step = pl.program_id(0)

        @pl.when(step == 0)
        def _():
            rider.begin(r_ins, r_outs, r_sems)

        if rider.middle is not None:
            @pl.when(step == n_steps - 1)
            def _():
                rider.middle(r_ins, r_outs, r_sems)

        body(*ins, *outs, *scratch)

        @pl.when(step == n_steps - 1)
        def _():
            rider.end(r_ins, r_outs, r_sems)

    any_spec = pl.BlockSpec(memory_space=pl.ANY)
    outs = pl.pallas_call(
        hosted, name=name, grid=grid,
        in_specs=in_specs + [any_spec] * r_in,
        out_specs=out_specs + [any_spec] * r_out,
        out_shape=out_shape + rider.out_shapes,
        scratch_shapes=scratch_shapes + rider.sems,
    )(*args, *rider.inputs)
    return list(outs[:n_out]), list(outs[n_out:])


def _in_proj(x, g_attn, w_in_t, gq_t, gk_t, rider=None):
    s = x.shape[0]
    ts = min(TOKEN_TILE, s)

    def body(x_ref, g_ref, w_ref, gq_ref, gk_ref, bq_ref, bk_ref, zqk_ref, qn_ref, kn_ref, v_ref, u_ref):
        xf = x_ref[...]
        hn = ((xf * _rms(xf)) * g_ref[...]).astype(BF16)
        z = _nt(hn, w_ref[...])
        q = z[:, :ATTN_WIDTH]
        k = z[:, ATTN_WIDTH:ATTN_WIDTH + KV_WIDTH]
        zqk_ref[...] = z[:, :ATTN_WIDTH + KV_WIDTH]
        rq = lax.rsqrt(_seg_mean(q * q, bq_ref[...]) + EPS)
        qn_ref[...] = ((q * rq) * gq_ref[...]).astype(BF16)
        rk = lax.rsqrt(_seg_mean(k * k, bk_ref[...]) + EPS)
        kn_ref[...] = ((k * rk) * gk_ref[...]).astype(BF16)
        v_ref[...] = z[:, ATTN_WIDTH + KV_WIDTH:ATTN_WIDTH + 2 * KV_WIDTH].astype(BF16)
        u_ref[...] = z[:, ATTN_WIDTH + 2 * KV_WIDTH:]

    return _call(
        body,
        (x, g_attn, w_in_t, gq_t, gk_t, _head_mean_matrix(ATTN_WIDTH), _head_mean_matrix(KV_WIDTH)),
        name="in_proj",
        grid=(s // ts,),
        in_specs=[
            _rows(ts, D_MODEL),
            _resident((1, D_MODEL)),
            _resident((IN_WIDTH, D_MODEL)),
            _resident((1, ATTN_WIDTH)),
            _resident((1, KV_WIDTH)),
            _resident((ATTN_WIDTH, ATTN_WIDTH)),
            _resident((KV_WIDTH, KV_WIDTH)),
        ],
        out_specs=[
            _rows(ts, ATTN_WIDTH + KV_WIDTH),
            _rows(ts, ATTN_WIDTH),
            _rows(ts, KV_WIDTH),
            _rows(ts, KV_WIDTH),
            _rows(ts, POOL_WIDTH),
        ],
        out_shape=[
            jax.ShapeDtypeStruct((s, ATTN_WIDTH + KV_WIDTH), F32),
            jax.ShapeDtypeStruct((s, ATTN_WIDTH), BF16),
            jax.ShapeDtypeStruct((s, KV_WIDTH), BF16),
            jax.ShapeDtypeStruct((s, KV_WIDTH), BF16),
            jax.ShapeDtypeStruct((s, POOL_WIDTH), F32),
        ],
        rider=rider,
    )


def _bucket_ranges():
    n = np.arange(MAX_DISTANCE)
    max_exact = N_BUCKETS // 2
    nf = np.maximum(n, 1).astype(np.float64)
    large = max_exact + (np.log(nf / max_exact) / math.log(MAX_DISTANCE / max_exact) * (N_BUCKETS - max_exact)).astype(np.int64)
    bucket = np.where(n < max_exact, n, np.minimum(large, N_BUCKETS - 1))
    out = []
    for b in range(N_BUCKETS):
        idx = np.nonzero(bucket == b)[0]
        out.append((int(idx.min()), int(idx.max()) + 1))
    return out


def _band_distance():
    i = lax.broadcasted_iota(jnp.int32, (BLOCK, 2 * BLOCK), 0)
    j = lax.broadcasted_iota(jnp.int32, (BLOCK, 2 * BLOCK), 1)
    return BLOCK + i - j


def _bias_table(rel_bias_t):
    ranges = _bucket_ranges()

    def body(rb_ref, tab_ref):
        d = _band_distance()
        for half, heads in enumerate((HEADS_A, HEADS_B)):
            for slot, h in enumerate(heads):
                t = jnp.full((BLOCK, 2 * BLOCK), -jnp.inf, F32)
                for b, (lo, hi) in enumerate(ranges):
                    t = jnp.where((d >= lo) & (d < hi), rb_ref[h, b], t)
                tab_ref[half, slot * BLOCK:(slot + 1) * BLOCK, :] = t

    return pl.pallas_call(
        body,
        name="bias_table",
        in_specs=[pl.BlockSpec(memory_space=pltpu.SMEM)],
        out_shape=jax.ShapeDtypeStruct((2, 4 * BLOCK, 2 * BLOCK), F32),
    )(rel_bias_t)


def _bias_table_bwd(dl_acc, rider=None):
    ranges = _bucket_ranges()
    n_heads = len(HEADS_A) + len(HEADS_B)

    def body(dl_ref, out_ref):
        d = _band_distance()
        row = lax.broadcasted_iota(jnp.int32, (n_heads, SMALL_LANES), 0)
        lane = lax.broadcasted_iota(jnp.int32, (n_heads, SMALL_LANES), 1)
        out = jnp.zeros((n_heads, SMALL_LANES), F32)
        for b, (lo, hi) in enumerate(ranges):
            in_bucket = (d >= lo) & (d < hi)
            for half, heads in enumerate((HEADS_A, HEADS_B)):
                for slot, h in enumerate(heads):
                    g = dl_ref[half, slot * BLOCK:(slot + 1) * BLOCK, :]
                    part = jnp.sum(jnp.where(in_bucket, g, 0.0), axis=0, keepdims=True)
                    tot = jnp.sum(part, axis=1, keepdims=True)
                    out = jnp.where((row == h) & (lane == b), tot, out)
        out_ref[...] = out

    return _call(
        body,
        (dl_acc,),
        name="bias_table_bwd",
        grid=(1,),
        in_specs=[_acc((2, 4 * BLOCK, 2 * BLOCK))],
        out_specs=[_acc((n_heads, SMALL_LANES))],
        out_shape=[jax.ShapeDtypeStruct((n_heads, SMALL_LANES), F32)],
        rider=rider,
    )


def _stack_heads(pairs, lo_mask):
    zero = jnp.zeros_like(pairs[0])
    lo = [jnp.where(lo_mask, t, zero) for t in pairs]
    hi = [jnp.where(lo_mask, zero, t) for t in pairs]
    return (jnp.concatenate([lo[0], lo[1], hi[2], hi[3]], axis=0),
            jnp.concatenate([hi[0], hi[1], lo[2], lo[3]], axis=0))


def _unstack_heads(out_a, out_b, lo_mask):
    t = lambda x, r: x[r * BLOCK:(r + 1) * BLOCK, :]
    return [
        jnp.where(lo_mask, t(out_a, 0), t(out_b, 0)),
        jnp.where(lo_mask, t(out_a, 1), t(out_b, 1)),
        jnp.where(lo_mask, t(out_b, 2), t(out_a, 2)),
        jnp.where(lo_mask, t(out_b, 3), t(out_a, 3)),
    ]


def _sink_column(sink_ref, heads):
    row = lax.broadcasted_iota(jnp.int32, (4 * BLOCK, 1), 0)
    col = jnp.full((4 * BLOCK, 1), sink_ref[0, heads[3]], F32)
    for slot in (2, 1, 0):
        col = jnp.where(row < (slot + 1) * BLOCK, sink_ref[0, heads[slot]], col)
    return col


def _band_probs(q_stack, keys, tab, sink, first_block):
    s = _nt(q_stack, keys) * (HEAD_DIM ** -0.5) + tab
    col = lax.broadcasted_iota(jnp.int32, s.shape, 1)
    s = jnp.where(jnp.logical_and(first_block, col < BLOCK), -jnp.inf, s)
    m = jnp.maximum(jnp.max(s, axis=-1, keepdims=True), sink)
    e = jnp.exp(s - m)
    e_sink = jnp.exp(sink - m)
    den = jnp.sum(e, axis=-1, keepdims=True) + e_sink
    return e / den, e_sink / den


def _attn_specs(nb):
    cur = lambda n: (jnp.minimum(n, nb - 1), 0)
    prev = lambda n: (jnp.maximum(jnp.minimum(n, nb - 1) - 1, 0), 0)
    return cur, prev


def _attn_fwd(qn, kn, v, tab, sinks, rider=None):
    s = qn.shape[0]
    nb = s // BLOCK
    cur, prev = _attn_specs(nb)

    def body(sink_ref, q_ref, kc_ref, kp_ref, vc_ref, vp_ref, tab_ref, o_ref):
        first = pl.program_id(0) == 0
        lo_mask = _lane_lo((BLOCK, BLOCK))
        kk = jnp.concatenate([kp_ref[...], kc_ref[...]], axis=0)
        vv = jnp.concatenate([vp_ref[...], vc_ref[...]], axis=0)
        kk_sw = pltpu.roll(kk, HEAD_DIM, 1)
        vv_sw = pltpu.roll(vv, HEAD_DIM, 1)
        q_a, q_b = _stack_heads([q_ref[:, p * BLOCK:(p + 1) * BLOCK] for p in range(4)], lo_mask)
        p_a, _ = _band_probs(q_a, kk, tab_ref[0], _sink_column(sink_ref, HEADS_A), first)
        p_b, _ = _band_probs(q_b, kk_sw, tab_ref[1], _sink_column(sink_ref, HEADS_B), first)
        out = _unstack_heads(_nn(p_a.astype(BF16), vv), _nn(p_b.astype(BF16), vv_sw), lo_mask)
        for p in range(4):
            o_ref[:, p * BLOCK:(p + 1) * BLOCK] = out[p].astype(BF16)

    return _call(
        body,
        (sinks, qn, kn, kn, v, v, tab),
        name="attn_fwd",
        grid=(nb,),
        in_specs=[
            pl.BlockSpec(memory_space=pltpu.SMEM),
            pl.BlockSpec((BLOCK, ATTN_WIDTH), cur),
            pl.BlockSpec((BLOCK, KV_WIDTH), cur),
            pl.BlockSpec((BLOCK, KV_WIDTH), prev),
            pl.BlockSpec((BLOCK, KV_WIDTH), cur),
            pl.BlockSpec((BLOCK, KV_WIDTH), prev),
            _resident((2, 4 * BLOCK, 2 * BLOCK)),
        ],
        out_specs=[pl.BlockSpec((BLOCK, ATTN_WIDTH), cur)],
        out_shape=[jax.ShapeDtypeStruct((s, ATTN_WIDTH), BF16)],
        rider=rider,
    )


def _pooled(u_tile, u_halo, tile_index, tile_rows):
    halo = jnp.where(tile_index > 0, u_halo, 0.0)
    ext = jnp.concatenate([halo, u_tile], axis=0)
    sums = []
    acc = ext
    for shift in (1, 2, 4, 8):
        acc = acc + pltpu.roll(acc, shift, 0)
        sums.append(acc)
    t = tile_index * tile_rows + lax.broadcasted_iota(jnp.int32, (tile_rows, 1), 0)
    out = []
    for g, w in enumerate(POOL_SIZES):
        lanes = slice(g * POOL_GROUP, (g + 1) * POOL_GROUP)
        cnt = jnp.minimum(t + 1, w).astype(F32)
        out.append(sums[g][POOL_HALO:, lanes] / cnt - u_tile[:, lanes])
    return out


def _halo_before(tile):
    return lambda i: (jnp.maximum(i * (tile // POOL_HALO) - 1, 0), 0)


def _mix_out(u, a, x, w_out, w_pool, pool_scale, g_ffn, rider=None):
    s = x.shape[0]
    ts = min(TOKEN_TILE, s)

    def body(u_ref, uh_ref, a_ref, x_ref, wo_ref, wp_ref, sc_ref, g_ref, h1_ref, hn_ref, m_ref):
        i = pl.program_id(0)
        pooled = _pooled(u_ref[...], uh_ref[...], i, ts)
        for g in range(len(POOL_SIZES)):
            lanes = slice(g * POOL_GROUP, (g + 1) * POOL_GROUP)
            y = _nn(pooled[g].astype(BF16), wp_ref[g].astype(BF16))
            m_ref[:, lanes] = (y * sc_ref[:, lanes]).astype(BF16)
        h1 = x_ref[...] + _nn(a_ref[...], wo_ref[:ATTN_WIDTH, :]) + _nn(m_ref[...], wo_ref[ATTN_WIDTH:, :])
        h1_ref[...] = h1
        hn_ref[...] = ((h1 * _rms(h1)) * g_ref[...]).astype(BF16)

    return _call(
        body,
        (u, u, a, x, w_out, w_pool, pool_scale, g_ffn),
        name="mix_out",
        grid=(s // ts,),
        in_specs=[
            _rows(ts, POOL_WIDTH),
            pl.BlockSpec((POOL_HALO, POOL_WIDTH), _halo_before(ts)),
            _rows(ts, ATTN_WIDTH),
            _rows(ts, D_MODEL),
            _resident((D_MODEL, D_MODEL)),
            _resident((len(POOL_SIZES), POOL_GROUP, POOL_GROUP)),
            _resident((1, POOL_WIDTH)),
            _resident((1, D_MODEL)),
        ],
        out_specs=[_rows(ts, D_MODEL), _rows(ts, D_MODEL), _rows(ts, POOL_WIDTH)],
        out_shape=[
            jax.ShapeDtypeStruct((s, D_MODEL), F32),
            jax.ShapeDtypeStruct((s, D_MODEL), BF16),
            jax.ShapeDtypeStruct((s, POOL_WIDTH), BF16),
        ],
        rider=rider,
    )


def _ffn_up(hn2, wg_t, wu_t, rider=None):
    s = hn2.shape[0]
    ts = min(TOKEN_TILE, s)

    def body(hn_ref, wg_ref, wu_ref, gt_ref, up_ref):
        hn = hn_ref[...]
        for c in range(D_FF // FF_CHUNK):
            cols = slice(c * FF_CHUNK, (c + 1) * FF_CHUNK)
            gt_ref[:, cols] = _nt(hn, wg_ref[cols, :]).astype(BF16)
            up_ref[:, cols] = _nt(hn, wu_ref[cols, :]).astype(BF16)

    return _call(
        body,
        (hn2, wg_t, wu_t),
        name="ffn_up",
        grid=(s // ts,),
        in_specs=[_rows(ts, D_MODEL), _resident((D_FF, D_MODEL)), _resident((D_FF, D_MODEL))],
        out_specs=[_rows(ts, D_FF), _rows(ts, D_FF)],
        out_shape=[jax.ShapeDtypeStruct((s, D_FF), BF16), jax.ShapeDtypeStruct((s, D_FF), BF16)],
        rider=rider,
    )


def _silu_mul(gt, up):
    return (gt * jax.nn.sigmoid(gt)) * up


def _ffn_down(gt, up, h1, w_down, rider=None):
    s = h1.shape[0]
    ts = min(TOKEN_TILE, s)

    def body(gt_ref, up_ref, h1_ref, wd_ref, h2_ref, act_ref):
        for c in range(D_FF // FF_CHUNK):
            cols = slice(c * FF_CHUNK, (c + 1) * FF_CHUNK)
            act_ref[:, cols] = _silu_mul(gt_ref[:, cols].astype(F32), up_ref[:, cols].astype(F32)).astype(BF16)
        h2_ref[...] = h1_ref[...] + _nn(act_ref[...], wd_ref[...])

    return _call(
        body,
        (gt, up, h1, w_down),
        name="ffn_down",
        grid=(s // ts,),
        in_specs=[_rows(ts, D_FF), _rows(ts, D_FF), _rows(ts, D_MODEL), _resident((D_FF, D_MODEL))],
        out_specs=[_rows(ts, D_MODEL)],
        out_shape=[jax.ShapeDtypeStruct((s, D_MODEL), F32)],
        scratch_shapes=[pltpu.VMEM((ts, D_FF), BF16)],
        rider=rider,
    )


def _ple_fwd_bwd(h2, p, target, g_ple, w_pg, w_pp, rider=None):
    s = h2.shape[0]
    ts = min(TOKEN_TILE, s)
    blk = D_MODEL // N_DEV

    def body(h2_ref, p_ref, t_ref, g_ref, wpg_ref, wpp_ref, loss_ref, dh_ref, dhb_ref, dwpg_ref, dwpp_ref, dg_ref, pp_ref):
        @pl.when(pl.program_id(0) == 0)
        def _():
            loss_ref[...] = jnp.zeros_like(loss_ref)
            dwpg_ref[...] = jnp.zeros_like(dwpg_ref)
            dwpp_ref[...] = jnp.zeros_like(dwpp_ref)
            dg_ref[...] = jnp.zeros_like(dg_ref)

        h2v = h2_ref[...]
        r = _rms(h2v)
        hn = ((h2v * r) * g_ref[...]).astype(BF16)
        gate = jax.nn.sigmoid(_nn(hn, wpg_ref[...]))
        pb = p_ref[...].astype(BF16)
        for j in range(N_DEV):
            pp_ref[:, j * blk:(j + 1) * blk] = _nn(pb, wpp_ref[j])
        pp = pp_ref[...]
        diff = (h2v + gate * pp) - t_ref[...]
        loss_ref[...] += jnp.sum(jnp.sum(diff * diff, axis=0, keepdims=True), axis=1, keepdims=True) * (0.5 / D_MODEL)
        dy = diff * (1.0 / D_MODEL)
        d_pp = (dy * gate).astype(BF16)
        d_pre = ((dy * pp) * (gate * (1.0 - gate))).astype(BF16)
        for j in range(N_DEV):
            dwpp_ref[j] += _tn(pb, d_pp[:, j * blk:(j + 1) * blk])
        dwpg_ref[...] += _tn(hn, d_pre)
        d_x, d_g = _rms_bwd(_nt(d_pre, wpg_ref[...]), h2v, r, g_ref[...])
        dg_ref[...] += d_g
        dh = dy + d_x
        dh_ref[...] = dh
        dhb_ref[...] = dh.astype(BF16)

    return _call(
        body,
        (h2, p, target, g_ple, w_pg, w_pp),
        name="ple_fwd_bwd",
        grid=(s // ts,),
        in_specs=[
            _rows(ts, D_MODEL),
            _rows(ts, PLE_DIM),
            _rows(ts, D_MODEL),
            _resident((1, D_MODEL)),
            _resident((D_MODEL, D_MODEL)),
            _resident((N_DEV, PLE_DIM, blk)),
        ],
        out_specs=[
            _acc((1, SMALL_LANES)),
            _rows(ts, D_MODEL),
            _rows(ts, D_MODEL),
            _acc((D_MODEL, D_MODEL)),
            _acc((N_DEV, PLE_DIM, blk)),
            _acc((1, D_MODEL)),
        ],
        out_shape=[
            jax.ShapeDtypeStruct((1, SMALL_LANES), F32),
            jax.ShapeDtypeStruct((s, D_MODEL), F32),
            jax.ShapeDtypeStruct((s, D_MODEL), BF16),
            jax.ShapeDtypeStruct((D_MODEL, D_MODEL), F32),
            jax.ShapeDtypeStruct((N_DEV, PLE_DIM, blk), F32),
            jax.ShapeDtypeStruct((1, D_MODEL), F32),
        ],
        scratch_shapes=[pltpu.VMEM((ts, D_MODEL), F32)],
        rider=rider,
    )


def _ffn_bwd_act(dh2, dh2b, h1, gt, up, g_ffn, wg_t, wu_t, w_down, rider=None):
    s = h1.shape[0]
    ts = min(FFN_TOKEN_TILE, s)

    def body(dh_ref, dhb_ref, h1_ref, gt_ref, up_ref, g_ref, wg_ref, wu_ref, wd_ref,
             act_ref, dgt_ref, dup_ref, dh1_ref, dh1b_ref, dg_ref):
        @pl.when(pl.program_id(0) == 0)
        def _():
            dg_ref[...] = jnp.zeros_like(dg_ref)

        dhb = dhb_ref[...]
        for c in range(D_FF // FF_CHUNK):
            cols = slice(c * FF_CHUNK, (c + 1) * FF_CHUNK)
            d_act = _nt(dhb, wd_ref[cols, :])
            gtv = gt_ref[:, cols].astype(F32)
            upv = up_ref[:, cols].astype(F32)
            sg = jax.nn.sigmoid(gtv)
            silu = gtv * sg
            act_ref[:, cols] = (silu * upv).astype(BF16)
            dup_ref[:, cols] = (d_act * silu).astype(BF16)
            dgt_ref[:, cols] = ((d_act * upv) * (sg * (1.0 + gtv * (1.0 - sg)))).astype(BF16)
        d_hn = _nn(dgt_ref[...], wg_ref[...]) + _nn(dup_ref[...], wu_ref[...])
        h1v = h1_ref[...]
        d_x, d_g = _rms_bwd(d_hn, h1v, _rms(h1v), g_ref[...])
        dg_ref[...] += d_g
        dh1 = dh_ref[...] + d_x
        dh1_ref[...] = dh1
        dh1b_ref[...] = dh1.astype(BF16)

    return _call(
        body,
        (dh2, dh2b, h1, gt, up, g_ffn, wg_t, wu_t, w_down),
        name="ffn_bwd_act",
        grid=(s // ts,),
        in_specs=[
            _rows(ts, D_MODEL),
            _rows(ts, D_MODEL),
            _rows(ts, D_MODEL),
            _rows(ts, D_FF),
            _rows(ts, D_FF),
            _resident((1, D_MODEL)),
            _resident((D_FF, D_MODEL)),
            _resident((D_FF, D_MODEL)),
            _resident((D_FF, D_MODEL)),
        ],
        out_specs=[
            _rows(ts, D_FF), _rows(ts, D_FF), _rows(ts, D_FF),
            _rows(ts, D_MODEL), _rows(ts, D_MODEL), _acc((1, D_MODEL)),
        ],
        out_shape=[
            jax.ShapeDtypeStruct((s, D_FF), BF16),
            jax.ShapeDtypeStruct((s, D_FF), BF16),
            jax.ShapeDtypeStruct((s, D_FF), BF16),
            jax.ShapeDtypeStruct((s, D_MODEL), F32),
            jax.ShapeDtypeStruct((s, D_MODEL), BF16),
            jax.ShapeDtypeStruct((1, D_MODEL), F32),
        ],
        rider=rider,
    )


def _ffn_bwd_w(which, lhs, rhs, rider=None):
    s = rhs.shape[0]

    def body(lhs_ref, rhs_ref, dw_ref):
        dw_ref[...] = _tn(lhs_ref[...], rhs_ref[...])

    return _call(
        body,
        (lhs, rhs),
        name=f"ffn_bwd_{which}",
        grid=(D_FF // FF_CHUNK,),
        in_specs=[pl.BlockSpec((s, FF_CHUNK), lambda i: (0, i)), _resident((s, D_MODEL))],
        out_specs=[_rows(FF_CHUNK, D_MODEL)],
        out_shape=[jax.ShapeDtypeStruct((D_FF, D_MODEL), F32)],
        rider=rider,
    )


def _mix_bwd(dh1b, u, w_out, w_pool, pool_scale, rider=None):
    s = u.shape[0]
    ts = min(TOKEN_TILE, s)
    nt = s // ts
    halo_after = lambda i: (jnp.minimum((i + 1) * (ts // POOL_HALO), s // POOL_HALO - 1), 0)
    n_groups = len(POOL_SIZES)

    def body(dh_ref, dhn_ref, u_ref, uh_ref, wo_ref, wp_ref, sc_ref, da_ref, du_ref, dwp_ref, dsc_ref):
        i = pl.program_id(0)

        @pl.when(i == 0)
        def _():
            dwp_ref[...] = jnp.zeros_like(dwp_ref)
            dsc_ref[...] = jnp.zeros_like(dsc_ref)

        dh = dh_ref[...]
        da_ref[...] = _nt(dh, wo_ref[:ATTN_WIDTH, :])
        dh_next = jnp.where(i < nt - 1, dhn_ref[...], jnp.zeros_like(dhn_ref))
        dm_ext = _nt(jnp.concatenate([dh, dh_next], axis=0), wo_ref[ATTN_WIDTH:, :])
        pooled = _pooled(u_ref[...], uh_ref[...], i, ts)
        t_ext = i * ts + lax.broadcasted_iota(jnp.int32, (ts + POOL_HALO, 1), 0)
        for g, w in enumerate(POOL_SIZES):
            lanes = slice(g * POOL_GROUP, (g + 1) * POOL_GROUP)
            wp = wp_ref[g].astype(BF16)
            pg = pooled[g].astype(BF16)
            dm_g = dm_ext[:, lanes]
            dsc_ref[:, lanes] += jnp.sum(dm_g[:ts, :] * _nn(pg, wp), axis=0, keepdims=True)
            dy = (dm_g * sc_ref[:, lanes]).astype(BF16)
            dwp_ref[g] += _tn(pg, dy[:ts, :])
            d_pool = _nt(dy, wp)
            acc = d_pool / jnp.minimum(t_ext + 1, w).astype(F32)
            shift = 1
            while shift < w:
                acc = acc + pltpu.roll(acc, ts + POOL_HALO - shift, 0)
                shift *= 2
            du_ref[:, lanes] = acc[:ts, :] - d_pool[:ts, :]

    return _call(
        body,
        (dh1b, dh1b, u, u, w_out, w_pool, pool_scale),
        name="mix_bwd",
        grid=(nt,),
        in_specs=[
            _rows(ts, D_MODEL),
            pl.BlockSpec((POOL_HALO, D_MODEL), halo_after),
            _rows(ts, POOL_WIDTH),
            pl.BlockSpec((POOL_HALO, POOL_WIDTH), _halo_before(ts)),
            _resident((D_MODEL, D_MODEL)),
            _resident((n_groups, POOL_GROUP, POOL_GROUP)),
            _resident((1, POOL_WIDTH)),
        ],
        out_specs=[
            _rows(ts, ATTN_WIDTH),
            _rows(ts, POOL_WIDTH),
            _acc((n_groups, POOL_GROUP, POOL_GROUP)),
            _acc((1, POOL_WIDTH)),
        ],
        out_shape=[
            jax.ShapeDtypeStruct((s, ATTN_WIDTH), F32),
            jax.ShapeDtypeStruct((s, POOL_WIDTH), F32),
            jax.ShapeDtypeStruct((n_groups, POOL_GROUP, POOL_GROUP), F32),
            jax.ShapeDtypeStruct((1, POOL_WIDTH), F32),
        ],
        rider=rider,
    )


def _out_w_bwd(a, m, dh1b, rider=None):
    s = dh1b.shape[0]

    def body(a_ref, m_ref, dh_ref, dw_ref):
        @pl.when(pl.program_id(0) == 0)
        def _():
            dw_ref[...] = _tn(a_ref[...], dh_ref[...])

        @pl.when(pl.program_id(0) == 1)
        def _():
            dw_ref[...] = _tn(m_ref[...], dh_ref[...])

    return _call(
        body,
        (a, m, dh1b),
        name="out_w_bwd",
        grid=(2,),
        in_specs=[_resident((s, ATTN_WIDTH)), _resident((s, POOL_WIDTH)), _resident((s, D_MODEL))],
        out_specs=[_rows(ATTN_WIDTH, D_MODEL)],
        out_shape=[jax.ShapeDtypeStruct((D_MODEL, D_MODEL), F32)],
        rider=rider,
    )


def _attn_bwd(qn, kn, v, a, da, tab, sinks, rider=None):
    s = qn.shape[0]
    nb = s // BLOCK
    cur, prev = _attn_specs(nb)
    done = lambda n: (jnp.maximum(n - 1, 0), 0)

    def body(sink_ref, q_ref, kc_ref, kp_ref, vc_ref, vp_ref, o_ref, do_ref, tab_ref,
             dq_ref, dk_ref, dv_ref, dl_ref, ds_ref, k_carry, v_carry, sink_acc):
        n = pl.program_id(0)

        @pl.when(n == 0)
        def _():
            dl_ref[...] = jnp.zeros_like(dl_ref)
            k_carry[...] = jnp.zeros_like(k_carry)
            v_carry[...] = jnp.zeros_like(v_carry)
            sink_acc[...] = jnp.zeros_like(sink_acc)

        @pl.when(n < nb)
        def _():
            first = n == 0
            lo_mask = _lane_lo((BLOCK, BLOCK))
            kk = jnp.concatenate([kp_ref[...], kc_ref[...]], axis=0)
            vv = jnp.concatenate([vp_ref[...], vc_ref[...]], axis=0)
            keys = (kk, pltpu.roll(kk, HEAD_DIM, 1))
            vals = (vv, pltpu.roll(vv, HEAD_DIM, 1))
            q_st = _stack_heads([q_ref[:, p * BLOCK:(p + 1) * BLOCK] for p in range(4)], lo_mask)
            do_pairs = [do_ref[:, p * BLOCK:(p + 1) * BLOCK] for p in range(4)]
            do_st = _stack_heads(do_pairs, lo_mask)
            o_st = _stack_heads([o_ref[:, p * BLOCK:(p + 1) * BLOCK].astype(F32) for p in range(4)], lo_mask)
            dq_st, dk_parts, dv_parts = [], [], []
            for half, heads in enumerate((HEADS_A, HEADS_B)):
                probs, p_sink = _band_probs(q_st[half], keys[half], tab_ref[half], _sink_column(sink_ref, heads), first)
                delta = jnp.sum(do_st[half] * o_st[half], axis=-1, keepdims=True)
                dob = do_st[half].astype(BF16)
                dl = probs * (_nt(dob, vals[half]) - delta)
                dl_ref[half] += dl
                sink_acc[half] += p_sink * delta
                dsb = (dl * (HEAD_DIM ** -0.5)).astype(BF16)
                dq_st.append(_nn(dsb, keys[half]))
                dk_parts.append(_tn(dsb, q_st[half]))
                dv_parts.append(_tn(probs.astype(BF16), dob))
            dq = _unstack_heads(dq_st[0], dq_st[1], lo_mask)
            for p in range(4):
                dq_ref[:, p * BLOCK:(p + 1) * BLOCK] = dq[p]
            dk = dk_parts[0] + pltpu.roll(dk_parts[1], HEAD_DIM, 1)
            dv = dv_parts[0] + pltpu.roll(dv_parts[1], HEAD_DIM, 1)
            dk_ref[...] = k_carry[...] + dk[:BLOCK, :]
            dv_ref[...] = v_carry[...] + dv[:BLOCK, :]
            k_carry[...] = dk[BLOCK:, :]
            v_carry[...] = dv[BLOCK:, :]

        @pl.when(n == nb)
        def _():
            dk_ref[...] = k_carry[...]
            dv_ref[...] = v_carry[...]
            for half, heads in enumerate((HEADS_A, HEADS_B)):
                for slot, h in enumerate(heads):
                    tot = jnp.sum(sink_acc[half, slot * BLOCK:(slot + 1) * BLOCK, :], axis=0, keepdims=True)
                    ds_ref[h:h + 1, :] = jnp.broadcast_to(-tot, (1, SMALL_LANES))

    return _call(
        body,
        (sinks, qn, kn, kn, v, v, a, da, tab),
        name="attn_bwd",
        grid=(nb + 1,),
        in_specs=[
            pl.BlockSpec(memory_space=pltpu.SMEM),
            pl.BlockSpec((BLOCK, ATTN_WIDTH), cur),
            pl.BlockSpec((BLOCK, KV_WIDTH), cur),
            pl.BlockSpec((BLOCK, KV_WIDTH), prev),
            pl.BlockSpec((BLOCK, KV_WIDTH), cur),
            pl.BlockSpec((BLOCK, KV_WIDTH), prev),
            pl.BlockSpec((BLOCK, ATTN_WIDTH), cur),
            pl.BlockSpec((BLOCK, ATTN_WIDTH), cur),
            _resident((2, 4 * BLOCK, 2 * BLOCK)),
        ],
        out_specs=[
            pl.BlockSpec((BLOCK, ATTN_WIDTH), cur),
            pl.BlockSpec((BLOCK, KV_WIDTH), done),
            pl.BlockSpec((BLOCK, KV_WIDTH), done),
            _acc((2, 4 * BLOCK, 2 * BLOCK)),
            _acc((N_DEV, SMALL_LANES)),
        ],
        out_shape=[
            jax.ShapeDtypeStruct((s, ATTN_WIDTH), F32),
            jax.ShapeDtypeStruct((s, KV_WIDTH), F32),
            jax.ShapeDtypeStruct((s, KV_WIDTH), F32),
            jax.ShapeDtypeStruct((2, 4 * BLOCK, 2 * BLOCK), F32),
            jax.ShapeDtypeStruct((N_DEV, SMALL_LANES), F32),
        ],
        scratch_shapes=[
            pltpu.VMEM((BLOCK, KV_WIDTH), F32),
            pltpu.VMEM((BLOCK, KV_WIDTH), F32),
            pltpu.VMEM((2, 4 * BLOCK, 1), F32),
        ],
        rider=rider,
    )


def _fold_heads(acc):
    t = acc + pltpu.roll(acc, HEAD_DIM, 1)
    out = t[:, :SMALL_LANES]
    for g in range(1, acc.shape[1] // SMALL_LANES):
        out = out + t[:, g * SMALL_LANES:(g + 1) * SMALL_LANES]
    return out


def _in_proj_bwd(dqn, dkn, dv, du, zqk, x, dh1, g_attn, gq_t, gk_t, w_in_t, rider=None):
    s = x.shape[0]
    ts = min(TOKEN_TILE, s)
    nt = s // ts

    def head_norm_bwd(d_n, raw, g_t, bmat):
        r = lax.rsqrt(_seg_mean(raw * raw, bmat) + EPS)
        gy = d_n * g_t
        d_raw = r * gy - raw * (r * r * r) * _seg_mean(gy * raw, bmat)
        return d_raw, jnp.sum(d_n * (raw * r), axis=0, keepdims=True)

    def body(dqn_ref, dkn_ref, dv_ref, du_ref, zqk_ref, x_ref, dh1_ref, g_ref, gq_ref, gk_ref, w_ref, bq_ref, bk_ref,
             gx_ref, dw_ref, dg_ref, dgq_ref, dgk_ref, dz_ref, gq_acc, gk_acc):
        i = pl.program_id(0)

        @pl.when(i == 0)
        def _():
            dw_ref[...] = jnp.zeros_like(dw_ref)
            dg_ref[...] = jnp.zeros_like(dg_ref)
            gq_acc[...] = jnp.zeros_like(gq_acc)
            gk_acc[...] = jnp.zeros_like(gk_acc)

        d_q, d_gq = head_norm_bwd(dqn_ref[...], zqk_ref[:, :ATTN_WIDTH], gq_ref[...], bq_ref[...])
        d_k, d_gk = head_norm_bwd(dkn_ref[...], zqk_ref[:, ATTN_WIDTH:], gk_ref[...], bk_ref[...])
        gq_acc[...] += d_gq
        gk_acc[...] += d_gk
        dz_ref[:, :ATTN_WIDTH] = d_q.astype(BF16)
        dz_ref[:, ATTN_WIDTH:ATTN_WIDTH + KV_WIDTH] = d_k.astype(BF16)
        dz_ref[:, ATTN_WIDTH + KV_WIDTH:ATTN_WIDTH + 2 * KV_WIDTH] = dv_ref[...].astype(BF16)
        dz_ref[:, ATTN_WIDTH + 2 * KV_WIDTH:] = du_ref[...].astype(BF16)
        dz = dz_ref[...]
        xf = x_ref[...]
        r = _rms(xf)
        hn = ((xf * r) * g_ref[...]).astype(BF16)
        dw_ref[...] += _tn(dz, hn)
        d_x, d_g = _rms_bwd(_nn(dz, w_ref[...]), xf, r, g_ref[...])
        dg_ref[...] += d_g
        gx_ref[...] = dh1_ref[...] + d_x

        @pl.when(i == nt - 1)
        def _():
            dgq_ref[...] = _fold_heads(gq_acc[...])
            dgk_ref[...] = _fold_heads(gk_acc[...])

    return _call(
        body,
        (dqn, dkn, dv, du, zqk, x, dh1, g_attn, gq_t, gk_t, w_in_t,
      _head_mean_matrix(ATTN_WIDTH), _head_mean_matrix(KV_WIDTH)),
        name="in_proj_bwd",
        grid=(nt,),
        in_specs=[
            _rows(ts, ATTN_WIDTH),
            _rows(ts, KV_WIDTH),
            _rows(ts, KV_WIDTH),
            _rows(ts, POOL_WIDTH),
            _rows(ts, ATTN_WIDTH + KV_WIDTH),
            _rows(ts, D_MODEL),
            _rows(ts, D_MODEL),
            _resident((1, D_MODEL)),
            _resident((1, ATTN_WIDTH)),
            _resident((1, KV_WIDTH)),
            _resident((IN_WIDTH, D_MODEL)),
            _resident((ATTN_WIDTH, ATTN_WIDTH)),
            _resident((KV_WIDTH, KV_WIDTH)),
        ],
        out_specs=[
            _rows(ts, D_MODEL),
            _acc((IN_WIDTH, D_MODEL)),
            _acc((1, D_MODEL)),
            _acc((1, SMALL_LANES)),
            _acc((1, SMALL_LANES)),
        ],
        out_shape=[
            jax.ShapeDtypeStruct((s, D_MODEL), F32),
            jax.ShapeDtypeStruct((IN_WIDTH, D_MODEL), F32),
            jax.ShapeDtypeStruct((1, D_MODEL), F32),
            jax.ShapeDtypeStruct((1, SMALL_LANES), F32),
            jax.ShapeDtypeStruct((1, SMALL_LANES), F32),
        ],
        scratch_shapes=[
            pltpu.VMEM((ts, IN_WIDTH), BF16),
            pltpu.VMEM((1, ATTN_WIDTH), F32),
            pltpu.VMEM((1, KV_WIDTH), F32),
        ],
        rider=rider,
    )


BIG_WEIGHTS = (
    ("w_in", True, IN_WIDTH // N_DEV, D_MODEL),
    ("w_out", False, D_MODEL // N_DEV, D_MODEL),
    ("w_gate", True, D_FF // N_DEV, D_MODEL),
    ("w_up", True, D_FF // N_DEV, D_MODEL),
    ("w_down", False, D_FF // N_DEV, D_MODEL),
    ("w_ple_gate", False, D_MODEL // N_DEV, D_MODEL),
    ("w_ple_proj", False, PLE_DIM, D_MODEL // N_DEV),
)
N_BIG = len(BIG_WEIGHTS)


def _place():
    x, y, c = lax.axis_index("x"), lax.axis_index("y"), lax.axis_index("c")
    chips = [(1 - x, y), (x, 1 - y), (1 - x, 1 - y)]
    return x, y, c, chips


class _Gather:
    def __init__(self, n):
        self.n = n
        self.sems = [pltpu.SemaphoreType.DMA((n, 7)), pltpu.SemaphoreType.DMA((n, 7)), pltpu.SemaphoreType.DMA((n,))]

    def _ctx(self, srcs, outs, sems):
        send_sems, recv_sems, local_sems = sems
        x, y, c, chips = _place()
        me, sibling = (x, y, c), (x, y, 1 - c)

        def block(k, owner):
            px, py, pc = owner
            return outs[k].at[4 * px + 2 * py + pc]

        def copy(k, idx, owner, to, src=None):
            return pltpu.make_async_remote_copy(
                src_ref=block(k, owner) if src is None else src, dst_ref=block(k, owner),
                send_sem=send_sems.at[k, idx], recv_sem=recv_sems.at[k, idx], device_id=to, device_id_type=MESH)

        def local(k):
            return pltpu.make_async_copy(srcs[k], block(k, me), local_sems.at[k])

        return c, chips, me, sibling, copy, local

    def begin(self, srcs, outs, sems):
        c, chips, me, sibling, copy, local = self._ctx(srcs, outs, sems)
        for k in range(self.n):
            local(k).start()
            copy(k, 0, me, sibling, src=srcs[k]).start()
            for j, chip in enumerate(chips):
                copy(k, 1 + j, me, (*chip, c), src=srcs[k]).start()

    def middle(self, srcs, outs, sems):
        c, chips, me, sibling, copy, local = self._ctx(srcs, outs, sems)
        for j, chip in enumerate(chips):
            for k in range(self.n):
                copy(k, 1 + j, (*chip, c), me).wait_recv()
                copy(k, 4 + j, (*chip, c), sibling).start()

    def end(self, srcs, outs, sems):
        c, chips, me, sibling, copy, local = self._ctx(srcs, outs, sems)
        for k in range(self.n):
            copy(k, 0, sibling, me).wait_recv()
            for j, chip in enumerate(chips):
                copy(k, 4 + j, (*chip, 1 - c), me).wait_recv()
        for k in range(self.n):
            copy(k, 0, me, sibling, src=srcs[k]).wait_send()
            for j, chip in enumerate(chips):
                copy(k, 1 + j, me, (*chip, c), src=srcs[k]).wait_send()
                copy(k, 4 + j, (*chip, c), sibling).wait_send()
            local(k).wait()


def _gather_rider(blocks):
    g = _Gather(len(blocks))
    shapes = [jax.ShapeDtypeStruct((N_DEV, *b.shape), b.dtype) for b in blocks]
    return _Rider(blocks, shapes, g.sems, g.begin, g.end, g.middle)


def _cast_and_gather_first(shards):
    g = _Gather(1)
    any_spec = pl.BlockSpec(memory_space=pl.ANY)
    vmem = pl.BlockSpec(memory_space=pltpu.VMEM)

    def body(*refs):
        ins, outs, gathered, sems = refs[:N_BIG], refs[N_BIG:2 * N_BIG], refs[2 * N_BIG], refs[2 * N_BIG + 1:]
        outs[0][...] = ins[0][...].astype(BF16)
        g.begin(outs[:1], [gathered], sems)
        for k in range(1, N_BIG):
            outs[k][...] = ins[k][...].astype(BF16)
        g.middle(outs[:1], [gathered], sems)
        g.end(outs[:1], [gathered], sems)

    res = pl.pallas_call(
        body,
        name="cast_and_gather_first",
        in_specs=[vmem] * N_BIG,
        out_specs=[vmem] * N_BIG + [any_spec],
        out_shape=[jax.ShapeDtypeStruct((r, c), BF16) for _, _, r, c in BIG_WEIGHTS]
        + [jax.ShapeDtypeStruct((N_DEV, *BIG_WEIGHTS[0][2:]), BF16)],
        scratch_shapes=g.sems,
    )(*shards)
    return list(res[:N_BIG]), res[N_BIG]


def _sibling_rider(grads):
    n = len(grads)

    def copies(gs, lands, sems):
        send_sems, recv_sems = sems
        x, y, c, _ = _place()
        return [
            pltpu.make_async_remote_copy(
                src_ref=gs[k].at[:, 1 - c], dst_ref=lands[k], send_sem=send_sems.at[k], recv_sem=recv_sems.at[k],
                device_id=(x, y, 1 - c), device_id_type=MESH)
            for k in range(n)
        ]

    def begin(gs, lands, sems):
        for cp in copies(gs, lands, sems):
            cp.start()

    def end(gs, lands, sems):
        for cp in copies(gs, lands, sems):
            cp.wait()

    shapes = [jax.ShapeDtypeStruct((N_CHIPS, *g.shape[2:]), F32) for g in grads]
    return _Rider(grads, shapes, [pltpu.SemaphoreType.DMA((n,)), pltpu.SemaphoreType.DMA((n,))], begin, end)


def _chip_sum(k, place, grad, from_sibling):
    _, _, r, c = BIG_WEIGHTS[k]

    def body(place_ref, g_ref, l_ref, own_ref, send_ref):
        q = pl.program_id(0)
        tot = g_ref[0, 0] + l_ref[0]
        mine = q == 2 * place_ref[0] + place_ref[1]

        @pl.when(mine)
        def _():
            own_ref[...] = tot

        send_ref[0] = jnp.where(mine, 0.0, tot).astype(BF16)

    return pl.pallas_call(
        body,
        name=f"chip_sum_{BIG_WEIGHTS[k][0]}",
        grid_spec=pltpu.PrefetchScalarGridSpec(
            num_scalar_prefetch=1,
            grid=(N_CHIPS,),
            in_specs=[
                pl.BlockSpec((1, 1, r, c), lambda q, place: (q, place[2], 0, 0)),
                pl.BlockSpec((1, r, c), lambda q, place: (q, 0, 0)),
            ],
            out_specs=[
                pl.BlockSpec((r, c), lambda q, place: (0, 0)),
                pl.BlockSpec((1, r, c), lambda q, place: (q, 0, 0)),
            ],
        ),
        out_shape=[jax.ShapeDtypeStruct((r, c), F32), jax.ShapeDtypeStruct((N_CHIPS, r, c), BF16)],
    )(place, grad, from_sibling)


def _chips_rider(to_send, small=None):
    n = len(to_send)
    inputs = list(to_send) + ([] if small is None else [small])
    shapes = [jax.ShapeDtypeStruct((3, *t.shape[1:]), BF16) for t in to_send]
    sems = [pltpu.SemaphoreType.DMA((max(n, 1), 3)), pltpu.SemaphoreType.DMA((max(n, 1), 3))]
    if small is not None:
        shapes.append(jax.ShapeDtypeStruct((N_DEV, *small.shape), F32))
        sems += [pltpu.SemaphoreType.DMA((7,)), pltpu.SemaphoreType.DMA((7,)), pltpu.SemaphoreType.DMA]

    def copies(ins, outs, sem_refs):
        x, y, c, chips = _place()
        out = []
        for k in range(n):
            for j, (px, py) in enumerate(chips):
                out.append(pltpu.make_async_remote_copy(
                    src_ref=ins[k].at[2 * px + py], dst_ref=outs[k].at[j],
                    send_sem=sem_refs[0].at[k, j], recv_sem=sem_refs[1].at[k, j],
                    device_id=(px, py, c), device_id_type=MESH))
        local = None
        if small is not None:
            me = 4 * x + 2 * y + c
            local = pltpu.make_async_copy(ins[n], outs[n].at[me], sem_refs[4])
            rel = 0
            for fx in (0, 1):
                for fy in (0, 1):
                    for fc in (0, 1):
                        if (fx, fy, fc) != (0, 0, 0):
                            out.append(pltpu.make_async_remote_copy(
                                src_ref=ins[n], dst_ref=outs[n].at[me],
                                send_sem=sem_refs[2].at[rel], recv_sem=sem_refs[3].at[rel],
                                device_id=(x ^ fx, y ^ fy, c ^ fc), device_id_type=MESH))
                            rel += 1
        return out, local

    def begin(ins, outs, sem_refs):
        remote, local = copies(ins, outs, sem_refs)
        if local is not None:
            local.start()
        for cp in remote:
            cp.start()

    def end(ins, outs, sem_refs):
        remote, local = copies(ins, outs, sem_refs)
        for cp in remote:
            cp.wait()
        if local is not None:
            local.wait()

    return _Rider(inputs, shapes, sems, begin, end)


def _exchange(name, rider):
    return _call(lambda: None, (), name=name, grid=(1,), in_specs=[], out_specs=[], out_shape=[], rider=rider)[1]


def _merge_riders(*riders):
    riders = [r for r in riders if r is not None]
    if len(riders) == 1:
        return riders[0]

    def split(refs, counts):
        out, at = [], 0
        for n in counts:
            out.append(refs[at:at + n])
            at += n
        return out

    def run(which):
        def fn(ins, outs, sems):
            parts = zip(riders, split(ins, [len(r.inputs) for r in riders]),
                        split(outs, [len(r.out_shapes) for r in riders]), split(sems, [len(r.sems) for r in riders]))
            for r, i, o, s in parts:
                hook = getattr(r, which)
                if hook is not None:
                    hook(i, o, s)
        return fn

    middle = run("middle") if any(r.middle is not None for r in riders) else None
    return _Rider(sum((r.inputs for r in riders), []), sum((r.out_shapes for r in riders), []),
                  sum((r.sems for r in riders), []), run("begin"), run("end"), middle)


def _split_outputs(outs, *riders):
    res, at = [], 0
    for r in riders:
        res.append(outs[at:at + len(r.out_shapes)])
        at += len(r.out_shapes)
    return res


def _adamw(w, g, m, v):
    m = ADAM_B1 * m + (1.0 - ADAM_B1) * g
    v = ADAM_B2 * v + (1.0 - ADAM_B2) * jnp.square(g)
    m_hat = m / (1.0 - ADAM_B1 ** ADAM_STEP)
    v_hat = v / (1.0 - ADAM_B2 ** ADAM_STEP)
    delta = -ADAM_LR * (m_hat / (jnp.sqrt(v_hat) + ADAM_EPS) + ADAM_WD * w)
    return delta, m, v


def _adamw_big(k, own, landed, w, m, v, rider=None):
    name, _, r, c = BIG_WEIGHTS[k]
    tile = r // 2
    tiles = lambda i: (i, 0)

    def body(own_ref, land_ref, w_ref, m_ref, v_ref, g_ref, d_ref, nm_ref, nv_ref):
        g = ((own_ref[...] + land_ref[0].astype(F32)) + land_ref[1].astype(F32)) + land_ref[2].astype(F32)
        g_ref[...] = g
        d_ref[...], nm_ref[...], nv_ref[...] = _adamw(w_ref[...], g, m_ref[...], v_ref[...])

    return _call(
        body,
        (own, landed, w, m, v),
        name=f"adamw_{name}",
        grid=(r // tile,),
        in_specs=[pl.BlockSpec((tile, c), tiles), pl.BlockSpec((3, tile, c), lambda i: (0, i, 0))]
        + [pl.BlockSpec((tile, c), tiles)] * 3,
        out_specs=[pl.BlockSpec((tile, c), tiles)] * 4,
        out_shape=[jax.ShapeDtypeStruct((r, c), F32)] * 4,
        rider=rider,
    )


def _sum_small(parts_list):
    n = len(parts_list)

    def body(*refs):
        for p_ref, out_ref in zip(refs[:n], refs[n:]):
            tot = p_ref[0]
            for j in range(1, N_DEV):
                tot = tot + p_ref[j]
            out_ref[...] = tot

    return pl.pallas_call(body, name="sum_small",
                          out_shape=[jax.ShapeDtypeStruct(p.shape[1:], F32) for p in parts_list])(*parts_list)


def _adamw_small(grads, ws, ms, vs):
    n = len(grads)

    def body(*refs):
        g_refs, w_refs, m_refs, v_refs = refs[:n], refs[n:2 * n], refs[2 * n:3 * n], refs[3 * n:4 * n]
        outs = refs[4 * n:]
        for i in range(n):
            d, nm, nv = _adamw(w_refs[i][...], g_refs[i][...], m_refs[i][...], v_refs[i][...])
            outs[i][...] = d
            outs[n + i][...] = nm
            outs[2 * n + i][...] = nv

    shapes = [jax.ShapeDtypeStruct(w.shape, F32) for w in ws]
    return pl.pallas_call(body, name="adamw_small", out_shape=shapes * 3)(*grads, *ws, *ms, *vs)


SMALL_NAMES = ("g_attn_norm", "g_q", "g_k", "attn_sinks", "rel_bias", "w_pool", "pool_scale", "g_ffn_norm", "g_ple_norm")


def _pack_small(arrays):
    rows, offsets = [], []
    at = 0
    for a in arrays:
        flat = a.reshape(-1)
        n_rows = -(-flat.shape[0] // (8 * SMALL_LANES)) * 8
        flat = jnp.pad(flat, (0, n_rows * SMALL_LANES - flat.shape[0]))
        rows.append(flat.reshape(n_rows, SMALL_LANES))
        offsets.append(at)
        at += n_rows
    return jnp.concatenate(rows, axis=0), offsets


def kernel(x, p, w_in, w_out, g_attn_norm, g_q, g_k, attn_sinks, rel_bias, w_pool, pool_scale, g_ffn_norm, w_gate, w_up, w_down, g_ple_norm, w_ple_gate, w_ple_proj, loss_target, m_w_in, m_w_out, m_g_attn_norm, m_g_q, m_g_k, m_attn_sinks, m_rel_bias, m_w_pool, m_pool_scale, m_g_ffn_norm, m_w_gate, m_w_up, m_w_down, m_g_ple_norm, m_w_ple_gate, m_w_ple_proj, v_w_in, v_w_out, v_g_attn_norm, v_g_q, v_g_k, v_attn_sinks, v_rel_bias, v_w_pool, v_pool_scale, v_g_ffn_norm, v_w_gate, v_w_up, v_w_down, v_g_ple_norm, v_w_ple_gate, v_w_ple_proj):
    weights = dict(w_in=w_in, w_out=w_out, g_attn_norm=g_attn_norm, g_q=g_q, g_k=g_k, attn_sinks=attn_sinks,
                   rel_bias=rel_bias, w_pool=w_pool, pool_scale=pool_scale, g_ffn_norm=g_ffn_norm, w_gate=w_gate,
                   w_up=w_up, w_down=w_down, g_ple_norm=g_ple_norm, w_ple_gate=w_ple_gate, w_ple_proj=w_ple_proj)
    m_in = dict(w_in=m_w_in, w_out=m_w_out, g_attn_norm=m_g_attn_norm, g_q=m_g_q, g_k=m_g_k, attn_sinks=m_attn_sinks,
                rel_bias=m_rel_bias, w_pool=m_w_pool, pool_scale=m_pool_scale, g_ffn_norm=m_g_ffn_norm, w_gate=m_w_gate,
                w_up=m_w_up, w_down=m_w_down, g_ple_norm=m_g_ple_norm, w_ple_gate=m_w_ple_gate, w_ple_proj=m_w_ple_proj)
    v_in = dict(w_in=v_w_in, w_out=v_w_out, g_attn_norm=v_g_attn_norm, g_q=v_g_q, g_k=v_g_k, attn_sinks=v_attn_sinks,
                rel_bias=v_rel_bias, w_pool=v_w_pool, pool_scale=v_pool_scale, g_ffn_norm=v_g_ffn_norm, w_gate=v_w_gate,
                w_up=v_w_up, w_down=v_w_down, g_ple_norm=v_g_ple_norm, w_ple_gate=v_w_ple_gate, w_ple_proj=v_w_ple_proj)

    xs = x[0]
    ps = p[0, 0]
    target = loss_target[0]
    wp = w_pool[0]
    gq_t = jnp.tile(g_q, (1, ATTN_WIDTH // HEAD_DIM))
    gk_t = jnp.tile(g_k, (1, KV_WIDTH // HEAD_DIM))

    def to_blocks(k, arr):
        return jnp.swapaxes(arr[0], 0, 1) if BIG_WEIGHTS[k][1] else arr[0]

    def from_blocks(k, arr):
        return (jnp.swapaxes(arr, 0, 1) if BIG_WEIGHTS[k][1] else arr)[None]

    IN, OUT, GATE, UP, DOWN, PG, PP = range(N_BIG)
    full = lambda g: g.reshape(N_DEV * g.shape[1], g.shape[2])
    halves = lambda k, g: g.reshape(N_CHIPS, 2, *BIG_WEIGHTS[k][2:])
    place = jnp.stack([lax.axis_index("x"), lax.axis_index("y"), lax.axis_index("c")]).astype(jnp.int32)

    sh, w_in_g = _cast_and_gather_first([to_blocks(k, weights[name]) for k, (name, _, _, _) in enumerate(BIG_WEIGHTS)])
    w_in_t = full(w_in_g)

    tab = _bias_table(rel_bias.T)
    (zqk, qn, kn, v, u), (w_out_g, w_pp_g) = _in_proj(xs, g_attn_norm, w_in_t, gq_t, gk_t,
                                                    rider=_gather_rider([sh[OUT], sh[PP]]))
    (a,), (wg_g,) = _attn_fwd(qn, kn, v, tab, attn_sinks, rider=_gather_rider([sh[GATE]]))
    w_out_f = full(w_out_g)
    (h1, hn2, m_out), (wu_g,) = _mix_out(u, a, xs, w_out_f, wp, pool_scale, g_ffn_norm, rider=_gather_rider([sh[UP]]))
    wg_t, wu_t = full(wg_g), full(wu_g)
    (gt, up), (wd_g, w_pg_g) = _ffn_up(hn2, wg_t, wu_t, rider=_gather_rider([sh[DOWN], sh[PG]]))
    w_down_f = full(wd_g)
    (h2,), _ = _ffn_down(gt, up, h1, w_down_f)

    sums, landed = [None] * N_BIG, [None] * N_BIG

    def chip_sum(k, grad, from_sibling):
        sums[k] = _chip_sum(k, place, halves(k, grad), from_sibling)

    (loss_part, dh2, dh2b, d_wpg, d_wpp, d_g_ple), _ = _ple_fwd_bwd(h2, ps, target, g_ple_norm, full(w_pg_g), w_pp_g)
    (act, dgt, dup, dh1, dh1b, d_g_ffn), sib = _ffn_bwd_act(
        dh2, dh2b, h1, gt, up, g_ffn_norm, wg_t, wu_t, w_down_f,
        rider=_sibling_rider([halves(PG, d_wpg), halves(PP, d_wpp)]))
    chip_sum(PG, d_wpg, sib[0])
    chip_sum(PP, d_wpp, sib[1])
    (d_wd,), (landed[PG], landed[PP]) = _ffn_bwd_w("down", act, dh2b, rider=_chips_rider([sums[PG][1], sums[PP][1]]))
    (d_wo,), sib = _out_w_bwd(a, m_out, dh1b, rider=_sibling_rider([halves(DOWN, d_wd)]))
    chip_sum(DOWN, d_wd, sib[0])
    r_sib, r_chips = _sibling_rider([halves(OUT, d_wo)]), _chips_rider([sums[DOWN][1]])
    (d_wg_t,), outs = _ffn_bwd_w("gate", dgt, hn2, rider=_merge_riders(r_sib, r_chips))
    sib, (landed[DOWN],) = _split_outputs(outs, r_sib, r_chips)
    chip_sum(OUT, d_wo, sib[0])
    r_sib, r_chips = _sibling_rider([halves(GATE, d_wg_t)]), _chips_rider([sums[OUT][1]])
    (d_wu_t,), outs = _ffn_bwd_w("up", dup, hn2, rider=_merge_riders(r_sib, r_chips))
    sib, (landed[OUT],) = _split_outputs(outs, r_sib, r_chips)
    chip_sum(GATE, d_wg_t, sib[0])
    r_sib, r_chips = _sibling_rider([halves(UP, d_wu_t)]), _chips_rider([sums[GATE][1]])
    (da, du, d_wpool, d_scale), outs = _mix_bwd(dh1b, u, w_out_f, wp, pool_scale, rider=_merge_riders(r_sib, r_chips))
    sib, (landed[GATE],) = _split_outputs(outs, r_sib, r_chips)
    chip_sum(UP, d_wu_t, sib[0])
    early, early_at = _pack_small([d_wpool, d_scale, d_g_ffn, d_g_ple, loss_part[:, :1]])
    (dqn, dkn, dv, dl_acc, d_sinks), (landed[UP], early_all) = _attn_bwd(
        qn, kn, v, a, da, tab, attn_sinks, rider=_chips_rider([sums[UP][1]], early))
    (grad_x, d_win_t, d_g_attn, d_gq, d_gk), _ = _in_proj_bwd(dqn, dkn, dv, du, zqk, xs, dh1, g_attn_norm, gq_t, gk_t, w_in_t)
    (d_rel_t,), sib = _bias_table_bwd(dl_acc, rider=_sibling_rider([halves(IN, d_win_t)]))
    chip_sum(IN, d_win_t, sib[0])
    late, late_at = _pack_small([d_g_attn, d_gq[:, :HEAD_DIM], d_gk[:, :HEAD_DIM], d_sinks[:, 0], d_rel_t[:, :N_BUCKETS]])
    landed[IN], late_all = _exchange("last_exchange", _chips_rider([sums[IN][1]], late))

    out = {"grad": {}, "delta": {}, "new_m": {}, "new_v": {}}
    for k, (name, _, _, _) in enumerate(BIG_WEIGHTS):
        res, _ = _adamw_big(k, sums[k][0], landed[k], to_blocks(k, weights[name]), to_blocks(k, m_in[name]),
                            to_blocks(k, v_in[name]))
        for kind, r in zip(("grad", "delta", "new_m", "new_v"), res):
            out[kind][name] = from_blocks(k, r)
    early_sum, late_sum = _sum_small([early_all, late_all])

    def unpack(packed, at, shape):
        n = math.prod(shape)
        return packed[at:at + -(-n // SMALL_LANES)].reshape(-1)[:n].reshape(shape)

    small_grads = dict(
        w_pool=unpack(early_sum, early_at[0], w_pool.shape), pool_scale=unpack(early_sum, early_at[1], pool_scale.shape),
        g_ffn_norm=unpack(early_sum, early_at[2], g_ffn_norm.shape), g_ple_norm=unpack(early_sum, early_at[3], g_ple_norm.shape),
        g_attn_norm=unpack(late_sum, late_at[0], g_attn_norm.shape), g_q=unpack(late_sum, late_at[1], g_q.shape),
        g_k=unpack(late_sum, late_at[2], g_k.shape), attn_sinks=unpack(late_sum, late_at[3], attn_sinks.shape),
        rel_bias=unpack(late_sum, late_at[4], rel_bias.T.shape))
    loss = early_sum[early_at[4], 0]
    flip = lambda name, arr: arr.T if name == "rel_bias" else arr
    updates = _adamw_small([small_grads[n] for n in SMALL_NAMES], [flip(n, weights[n]) for n in SMALL_NAMES],
                           [flip(n, m_in[n]) for n in SMALL_NAMES], [flip(n, v_in[n]) for n in SMALL_NAMES])
    n_small = len(SMALL_NAMES)
    for i, name in enumerate(SMALL_NAMES):
        out["grad"][name] = flip(name, small_grads[name])
        out["delta"][name] = flip(name, updates[i])
        out["new_m"][name] = flip(name, updates[n_small + i])
        out["new_v"][name] = flip(name, updates[2 * n_small + i])

    order = ("w_in", "w_out", "g_attn_norm", "g_q", "g_k", "attn_sinks", "rel_bias", "w_pool", "pool_scale",
             "g_ffn_norm", "w_gate", "w_up", "w_down", "g_ple_norm", "w_ple_gate", "w_ple_proj")
    return (loss, grad_x[None], *[out["grad"][n] for n in order], *[out["delta"][n] for n in order],
            *[out["new_m"][n] for n in order], *[out["new_v"][n] for n in order])
```

```python
import functools
import math

import jax
import jax.numpy as jnp
import numpy as np
from jax import lax
from jax.experimental import pallas as pl
from jax.experimental.pallas import tpu as pltpu

F32 = jnp.float32
BF16 = jnp.bfloat16
MESH = pl.DeviceIdType.MESH

D_MODEL = 1024
HEAD_DIM = 64
ATTN_WIDTH = 512
KV_WIDTH = 128
POOL_WIDTH = 512
POOL_SIZES = (2, 4, 8, 16)
POOL_GROUP = 128
POOL_HALO = 16
IN_WIDTH = 1280
D_FF = 2816
PLE_DIM = 256
BLOCK = 128
N_BUCKETS = 32
MAX_DISTANCE = 128
EPS = 1e-6
N_DEV = 8
N_CHIPS = 4

ADAM_LR = 0.001
ADAM_B1 = 0.9
ADAM_B2 = 0.999
ADAM_EPS = 1e-08
ADAM_WD = 0.01
ADAM_STEP = 10

TOKEN_TILE = 512
FF_CHUNK = 256
ATTN_STEP_BLOCKS = 2
HEADS_A = (0, 2, 5, 7)
HEADS_B = (1, 3, 4, 6)
SMALL_LANES = 128


def _nn(a, b):
    return jnp.dot(a, b, preferred_element_type=F32)


def _nt(a, b):
    return lax.dot_general(a, b, (((1,), (1,)), ((), ())), preferred_element_type=F32)


def _tn(a, b):
    return lax.dot_general(a, b, (((0,), (0,)), ((), ())), preferred_element_type=F32)


def _resident(shape):
    nd = len(shape)
    return pl.BlockSpec(shape, lambda i, _nd=nd: (0,) * _nd, pipeline_mode=pl.Buffered(1))


def _rows(tile, width):
    return pl.BlockSpec((tile, width), lambda i: (i, 0))


def _acc(shape):
    nd = len(shape)
    return pl.BlockSpec(shape, lambda i, _nd=nd: (0,) * _nd)


def _head_mean_matrix(width):
    idx = np.arange(width) // HEAD_DIM
    return jnp.asarray((idx[:, None] == idx[None, :]).astype(np.float32) / HEAD_DIM, dtype=BF16)


def _seg_mean(v, bmat):
    hi = v.astype(BF16)
    lo = (v - hi.astype(F32)).astype(BF16)
    return _nn(hi, bmat) + _nn(lo, bmat)


def _rms(x):
    return lax.rsqrt(jnp.mean(x * x, axis=-1, keepdims=True) + EPS)


def _rms_bwd(d_y, x, r, g):
    gy = d_y * g
    d_x = r * gy - x * (r * r * r) * jnp.mean(gy * x, axis=-1, keepdims=True)
    d_g = jnp.sum(d_y * (x * r), axis=0, keepdims=True)
    return d_x, d_g


def _lane_lo(shape):
    return lax.broadcasted_iota(jnp.int32, shape, 1) < HEAD_DIM


class _Rider:
    def __init__(self, inputs, out_shapes, sems, begin, end, middle=None):
        self.inputs, self.out_shapes, self.sems = list(inputs), list(out_shapes), list(sems)
        self.begin, self.middle, self.end = begin, middle, end


def _call(body, args, *, name, grid, in_specs, out_specs, out_shape, scratch_shapes=(), rider=None):
    in_specs, out_specs, out_shape, scratch_shapes = list(in_specs), list(out_specs), list(out_shape), list(scratch_shapes)
    if rider is None:
        outs = pl.pallas_call(body, name=name, grid=grid, in_specs=in_specs, out_specs=out_specs, out_shape=out_shape,
                              scratch_shapes=scratch_shapes)(*args)
        return list(outs), []
    n_in, n_out, n_scr = len(in_specs), len(out_shape), len(scratch_shapes)
    r_in, r_out = len(rider.inputs), len(rider.out_shapes)
    n_steps = grid[0]

    def hosted(*refs):
        ins, refs = refs[:n_in], refs[n_in:]
        r_ins, refs = refs[:r_in], refs[r_in:]
        outs, refs = refs[:n_out], refs[n_out:]
        r_outs, refs = refs[:r_out], refs[r_out:]
        scratch, r_sems = refs[:n_scr], refs[n_scr:]
        step = pl.program_id(0)

        @pl.when(step == 0)
        def _():
            rider.begin(r_ins, r_outs, r_sems)

        if rider.middle is not None:
            @pl.when(step == n_steps - 1)
            def _():
                rider.middle(r_ins, r_outs, r_sems)

        body(*ins, *outs, *scratch)

        @pl.when(step == n_steps - 1)
        def _():
            rider.end(r_ins, r_outs, r_sems)

    any_spec = pl.BlockSpec(memory_space=pl.ANY)
    outs = pl.pallas_call(
        hosted, name=name, grid=grid,
        in_specs=in_specs + [any_spec] * r_in,
        out_specs=out_specs + [any_spec] * r_out,
        out_shape=out_shape + rider.out_shapes,
        scratch_shapes=scratch_shapes + rider.sems,
    )(*args, *rider.inputs)
    return list(outs[:n_out]), list(outs[n_out:])


def _in_proj(x, g_attn, w_in_t, gq_t, gk_t, rider=None):
    s = x.shape[0]
    ts = min(TOKEN_TILE, s)

    def body(x_ref, g_ref, w_ref, gq_ref, gk_ref, bq_ref, bk_ref, zqk_ref, qn_ref, kn_ref, v_ref, u_ref):
        xf = x_ref[...]
        hn = ((xf * _rms(xf)) * g_ref[...]).astype(BF16)
        z = _nt(hn, w_ref[...])
        q = z[:, :ATTN_WIDTH]
        k = z[:, ATTN_WIDTH:ATTN_WIDTH + KV_WIDTH]
        zqk_ref[...] = z[:, :ATTN_WIDTH + KV_WIDTH]
        rq = lax.rsqrt(_seg_mean(q * q, bq_ref[...]) + EPS)
        qn_ref[...] = ((q * rq) * gq_ref[...]).astype(BF16)
        rk = lax.rsqrt(_seg_mean(k * k, bk_ref[...]) + EPS)
        kn_ref[...] = ((k * rk) * gk_ref[...]).astype(BF16)
        v_ref[...] = z[:, ATTN_WIDTH + KV_WIDTH:ATTN_WIDTH + 2 * KV_WIDTH].astype(BF16)
        u_ref[...] = z[:, ATTN_WIDTH + 2 * KV_WIDTH:]

    return _call(
        body,
        (x, g_attn, w_in_t, gq_t, gk_t, _head_mean_matrix(ATTN_WIDTH), _head_mean_matrix(KV_WIDTH)),
        name="in_proj",
        grid=(s // ts,),
        in_specs=[
            _rows(ts, D_MODEL),
            _resident((1, D_MODEL)),
            _resident((IN_WIDTH, D_MODEL)),
            _resident((1, ATTN_WIDTH)),
            _resident((1, KV_WIDTH)),
            _resident((ATTN_WIDTH, ATTN_WIDTH)),
            _resident((KV_WIDTH, KV_WIDTH)),
        ],
        out_specs=[
            _rows(ts, ATTN_WIDTH + KV_WIDTH),
            _rows(ts, ATTN_WIDTH),
            _rows(ts, KV_WIDTH),
            _rows(ts, KV_WIDTH),
            _rows(ts, POOL_WIDTH),
        ],
        out_shape=[
            jax.ShapeDtypeStruct((s, ATTN_WIDTH + KV_WIDTH), F32),
            jax.ShapeDtypeStruct((s, ATTN_WIDTH), BF16),
            jax.ShapeDtypeStruct((s, KV_WIDTH), BF16),
            jax.ShapeDtypeStruct((s, KV_WIDTH), BF16),
            jax.ShapeDtypeStruct((s, POOL_WIDTH), F32),
        ],
        rider=rider,
    )


def _bucket_ranges():
    n = np.arange(MAX_DISTANCE)
    max_exact = N_BUCKETS // 2
    nf = np.maximum(n, 1).astype(np.float64)
    large = max_exact + (np.log(nf / max_exact) / math.log(MAX_DISTANCE / max_exact) * (N_BUCKETS - max_exact)).astype(np.int64)
    bucket = np.where(n < max_exact, n, np.minimum(large, N_BUCKETS - 1))
    out = []
    for b in range(N_BUCKETS):
        idx = np.nonzero(bucket == b)[0]
        out.append((int(idx.min()), int(idx.max()) + 1))
    return out


def _band_distance():
    i = lax.broadcasted_iota(jnp.int32, (BLOCK, 2 * BLOCK), 0)
    j = lax.broadcasted_iota(jnp.int32, (BLOCK, 2 * BLOCK), 1)
    return BLOCK + i - j


def _bias_table(rel_bias_t):
    ranges = _bucket_ranges()

    def body(rb_ref, tab_ref):
        d = _band_distance()
        for half, heads in enumerate((HEADS_A, HEADS_B)):
            for slot, h in enumerate(heads):
                t = jnp.full((BLOCK, 2 * BLOCK), -jnp.inf, F32)
                for b, (lo, hi) in enumerate(ranges):
                    t = jnp.where((d >= lo) & (d < hi), rb_ref[h, b], t)
                tab_ref[half, slot * BLOCK:(slot + 1) * BLOCK, :] = t

    return pl.pallas_call(
        body,
        name="bias_table",
        in_specs=[pl.BlockSpec(memory_space=pltpu.SMEM)],
        out_shape=jax.ShapeDtypeStruct((2, 4 * BLOCK, 2 * BLOCK), F32),
    )(rel_bias_t)


def _bias_table_bwd(dl_acc, rider=None):
    ranges = _bucket_ranges()
    n_heads = len(HEADS_A) + len(HEADS_B)

    def body(dl_ref, out_ref):
        d = _band_distance()
        row = lax.broadcasted_iota(jnp.int32, (n_heads, SMALL_LANES), 0)
        lane = lax.broadcasted_iota(jnp.int32, (n_heads, SMALL_LANES), 1)
        out = jnp.zeros((n_heads, SMALL_LANES), F32)
        for b, (lo, hi) in enumerate(ranges):
            in_bucket = (d >= lo) & (d < hi)
            for half, heads in enumerate((HEADS_A, HEADS_B)):
                for slot, h in enumerate(heads):
                    g = dl_ref[half, slot * BLOCK:(slot + 1) * BLOCK, :]
                    part = jnp.sum(jnp.where(in_bucket, g, 0.0), axis=0, keepdims=True)
                    tot = jnp.sum(part, axis=1, keepdims=True)
                    out = jnp.where((row == h) & (lane == b), tot, out)
        out_ref[...] = out

    return _call(
        body,
        (dl_acc,),
        name="bias_table_bwd",
        grid=(1,),
        in_specs=[_acc((2, 4 * BLOCK, 2 * BLOCK))],
        out_specs=[_acc((n_heads, SMALL_LANES))],
        out_shape=[jax.ShapeDtypeStruct((n_heads, SMALL_LANES), F32)],
        rider=rider,
    )


def _stack_heads(pairs, lo_mask):
    zero = jnp.zeros_like(pairs[0])
    lo = [jnp.where(lo_mask, t, zero) for t in pairs]
    hi = [jnp.where(lo_mask, zero, t) for t in pairs]
    return (jnp.concatenate([lo[0], lo[1], hi[2], hi[3]], axis=0),
            jnp.concatenate([hi[0], hi[1], lo[2], lo[3]], axis=0))


def _unstack_heads(out_a, out_b, lo_mask):
    t = lambda x, r: x[r * BLOCK:(r + 1) * BLOCK, :]
    return [
        jnp.where(lo_mask, t(out_a, 0), t(out_b, 0)),
        jnp.where(lo_mask, t(out_a, 1), t(out_b, 1)),
        jnp.where(lo_mask, t(out_b, 2), t(out_a, 2)),
        jnp.where(lo_mask, t(out_b, 3), t(out_a, 3)),
    ]


def _sink_column(sink_ref, heads):
    row = lax.broadcasted_iota(jnp.int32, (4 * BLOCK, 1), 0)
    col = jnp.full((4 * BLOCK, 1), sink_ref[0, heads[3]], F32)
    for slot in (2, 1, 0):
        col = jnp.where(row < (slot + 1) * BLOCK, sink_ref[0, heads[slot]], col)
    return col


def _band_probs(q_stack, keys, tab, sink, first_block):
    s = _nt(q_stack, keys) * (HEAD_DIM ** -0.5) + tab
    if first_block is not None:
        col = lax.broadcasted_iota(jnp.int32, s.shape, 1)
        s = jnp.where(jnp.logical_and(first_block, col < BLOCK), -jnp.inf, s)
    m = jnp.maximum(jnp.max(s, axis=-1, keepdims=True), sink)
    e = jnp.exp(s - m)
    e_sink = jnp.exp(sink - m)
    den = jnp.sum(e, axis=-1, keepdims=True) + e_sink
    return e / den, e_sink / den


def _attn_specs(n_groups):
    group = lambda n: (jnp.minimum(n, n_groups - 1), 0)
    prev = lambda n: (jnp.maximum(jnp.minimum(n, n_groups - 1) * ATTN_STEP_BLOCKS - 1, 0), 0)
    return group, prev


def _band(prev_ref, group_ref, b):
    rows = lambda i: group_ref[i * BLOCK:(i + 1) * BLOCK, :]
    band = jnp.concatenate([prev_ref[...] if b == 0 else rows(b - 1), rows(b)], axis=0)
    return band, pltpu.roll(band, HEAD_DIM, 1)


def _attn_fwd(qn, kn, v, tab, sinks, rider=None):
    s = qn.shape[0]
    n_groups = s // (ATTN_STEP_BLOCKS * BLOCK)
    group, prev = _attn_specs(n_groups)
    rows = ATTN_STEP_BLOCKS * BLOCK

    def body(sink_ref, q_ref, kc_ref, kp_ref, vc_ref, vp_ref, tab_ref, o_ref):
        first = pl.program_id(0) == 0
        lo_mask = _lane_lo((BLOCK, BLOCK))
        for b in range(ATTN_STEP_BLOCKS):
            at = slice(b * BLOCK, (b + 1) * BLOCK)
            kk, kk_sw = _band(kp_ref, kc_ref, b)
            vv, vv_sw = _band(vp_ref, vc_ref, b)
            q_a, q_b = _stack_heads([q_ref[at, p * BLOCK:(p + 1) * BLOCK] for p in range(4)], lo_mask)
            no_prev = first if b == 0 else None
            p_a, _ = _band_probs(q_a, kk, tab_ref[0], _sink_column(sink_ref, HEADS_A), no_prev)
            p_b, _ = _band_probs(q_b, kk_sw, tab_ref[1], _sink_column(sink_ref, HEADS_B), no_prev)
            out = _unstack_heads(_nn(p_a.astype(BF16), vv), _nn(p_b.astype(BF16), vv_sw), lo_mask)
            for p in range(4):
                o_ref[at, p * BLOCK:(p + 1) * BLOCK] = out[p].astype(BF16)

    return _call(
        body,
        (sinks, qn, kn, kn, v, v, tab),
        name="attn_fwd",
        grid=(n_groups,),
        in_specs=[
            pl.BlockSpec(memory_space=pltpu.SMEM),
            pl.BlockSpec((rows, ATTN_WIDTH), group),
            pl.BlockSpec((rows, KV_WIDTH), group),
            pl.BlockSpec((BLOCK, KV_WIDTH), prev),
            pl.BlockSpec((rows, KV_WIDTH), group),
            pl.BlockSpec((BLOCK, KV_WIDTH), prev),
            _resident((2, 4 * BLOCK, 2 * BLOCK)),
        ],
        out_specs=[pl.BlockSpec((rows, ATTN_WIDTH), group)],
        out_shape=[jax.ShapeDtypeStruct((s, ATTN_WIDTH), BF16)],
        rider=rider,
    )


def _pooled(u_tile, u_halo, tile_index, tile_rows):
    halo = jnp.where(tile_index > 0, u_halo, 0.0)
    ext = jnp.concatenate([halo, u_tile], axis=0)
    sums = []
    acc = ext
    for shift in (1, 2, 4, 8):
        acc = acc + pltpu.roll(acc, shift, 0)
        sums.append(acc)
    t = tile_index * tile_rows + lax.broadcasted_iota(jnp.int32, (tile_rows, 1), 0)
    out = []
    for g, w in enumerate(POOL_SIZES):
        lanes = slice(g * POOL_GROUP, (g + 1) * POOL_GROUP)
        cnt = jnp.minimum(t + 1, w).astype(F32)
        out.append(sums[g][POOL_HALO:, lanes] / cnt - u_tile[:, lanes])
    return out


def _halo_before(tile):
    return lambda i: (jnp.maximum(i * (tile // POOL_HALO) - 1, 0), 0)


def _mix_out(u, a, x, w_out, w_pool, pool_scale, g_ffn, rider=None):
    s = x.shape[0]
    ts = min(TOKEN_TILE, s)

    def body(u_ref, uh_ref, a_ref, x_ref, wo_ref, wp_ref, sc_ref, g_ref, h1_ref, hn_ref, m_ref):
        i = pl.program_id(0)
        pooled = _pooled(u_ref[...], uh_ref[...], i, ts)
        for g in range(len(POOL_SIZES)):
            lanes = slice(g * POOL_GROUP, (g + 1) * POOL_GROUP)
            y = _nn(pooled[g].astype(BF16), wp_ref[g].astype(BF16))
            m_ref[:, lanes] = (y * sc_ref[:, lanes]).astype(BF16)
        h1 = x_ref[...] + _nn(a_ref[...], wo_ref[:ATTN_WIDTH, :]) + _nn(m_ref[...], wo_ref[ATTN_WIDTH:, :])
        h1_ref[...] = h1
        hn_ref[...] = ((h1 * _rms(h1)) * g_ref[...]).astype(BF16)

    return _call(
        body,
        (u, u, a, x, w_out, w_pool, pool_scale, g_ffn),
        name="mix_out",
        grid=(s // ts,),
        in_specs=[
            _rows(ts, POOL_WIDTH),
            pl.BlockSpec((POOL_HALO, POOL_WIDTH), _halo_before(ts)),
            _rows(ts, ATTN_WIDTH),
            _rows(ts, D_MODEL),
            _resident((D_MODEL, D_MODEL)),
            _resident((len(POOL_SIZES), POOL_GROUP, POOL_GROUP)),
            _resident((1, POOL_WIDTH)),
            _resident((1, D_MODEL)),
        ],
        out_specs=[_rows(ts, D_MODEL), _rows(ts, D_MODEL), _rows(ts, POOL_WIDTH)],
        out_shape=[
            jax.ShapeDtypeStruct((s, D_MODEL), F32),
            jax.ShapeDtypeStruct((s, D_MODEL), BF16),
            jax.ShapeDtypeStruct((s, POOL_WIDTH), BF16),
        ],
        rider=rider,
    )


def _ffn_up(hn2, wg_t, wu_t, rider=None):
    s = hn2.shape[0]
    ts = min(2 * TOKEN_TILE, s)

    def body(hn_ref, wg_ref, wu_ref, gt_ref, up_ref):
        hn = hn_ref[...]
        for c in range(D_FF // FF_CHUNK):
            cols = slice(c * FF_CHUNK, (c + 1) * FF_CHUNK)
            gt_ref[:, cols] = _nt(hn, wg_ref[cols, :]).astype(BF16)
            up_ref[:, cols] = _nt(hn, wu_ref[cols, :]).astype(BF16)

    return _call(
        body,
        (hn2, wg_t, wu_t),
        name="ffn_up",
        grid=(s // ts,),
        in_specs=[_rows(ts, D_MODEL), _resident((D_FF, D_MODEL)), _resident((D_FF, D_MODEL))],
        out_specs=[_rows(ts, D_FF), _rows(ts, D_FF)],
        out_shape=[jax.ShapeDtypeStruct((s, D_FF), BF16), jax.ShapeDtypeStruct((s, D_FF), BF16)],
        rider=rider,
    )


def _silu_mul(gt, up):
    return (gt * jax.nn.sigmoid(gt)) * up


def _ffn_down(gt, up, h1, w_down, rider=None):
    s = h1.shape[0]
    ts = min(TOKEN_TILE, s)

    def body(gt_ref, up_ref, h1_ref, wd_ref, h2_ref, act_ref):
        for c in range(D_FF // FF_CHUNK):
            cols = slice(c * FF_CHUNK, (c + 1) * FF_CHUNK)
            act_ref[:, cols] = _silu_mul(gt_ref[:, cols].astype(F32), up_ref[:, cols].astype(F32)).astype(BF16)
        h2_ref[...] = h1_ref[...] + _nn(act_ref[...], wd_ref[...])

    return _call(
        body,
        (gt, up, h1, w_down),
        name="ffn_down",
        grid=(s // ts,),
        in_specs=[_rows(ts, D_FF), _rows(ts, D_FF), _rows(ts, D_MODEL), _resident((D_FF, D_MODEL))],
        out_specs=[_rows(ts, D_MODEL)],
        out_shape=[jax.ShapeDtypeStruct((s, D_MODEL), F32)],
        scratch_shapes=[pltpu.VMEM((ts, D_FF), BF16)],
        rider=rider,
    )


def _ple_fwd_bwd(h2, p, target, g_ple, w_pg, w_pp, rider=None):
    s = h2.shape[0]
    ts = min(TOKEN_TILE, s)
    blk = D_MODEL // N_DEV

    def body(h2_ref, p_ref, t_ref, g_ref, wpg_ref, wpp_ref, loss_ref, dh_ref, dhb_ref, dwpg_ref, dwpp_ref, dg_ref, pp_ref):
        @pl.when(pl.program_id(0) == 0)
        def _():
            loss_ref[...] = jnp.zeros_like(loss_ref)
            dwpg_ref[...] = jnp.zeros_like(dwpg_ref)
            dwpp_ref[...] = jnp.zeros_like(dwpp_ref)
            dg_ref[...] = jnp.zeros_like(dg_ref)

        h2v = h2_ref[...]
        r = _rms(h2v)
        hn = ((h2v * r) * g_ref[...]).astype(BF16)
        gate = jax.nn.sigmoid(_nn(hn, wpg_ref[...]))
        pb = p_ref[...].astype(BF16)
        for j in range(N_DEV):
            pp_ref[:, j * blk:(j + 1) * blk] = _nn(pb, wpp_ref[j])
        pp = pp_ref[...]
        diff = (h2v + gate * pp) - t_ref[...]
        loss_ref[...] += jnp.sum(jnp.sum(diff * diff, axis=0, keepdims=True), axis=1, keepdims=True) * (0.5 / D_MODEL)
        dy = diff * (1.0 / D_MODEL)
        d_pp = (dy * gate).astype(BF16)
        d_pre = ((dy * pp) * (gate * (1.0 - gate))).astype(BF16)
        for j in range(N_DEV):
            dwpp_ref[j] += _tn(pb, d_pp[:, j * blk:(j + 1) * blk])
        dwpg_ref[...] += _tn(hn, d_pre)
        d_x, d_g = _rms_bwd(_nt(d_pre, wpg_ref[...]), h2v, r, g_ref[...])
        dg_ref[...] += d_g
        dh = dy + d_x
        dh_ref[...] = dh
        dhb_ref[...] = dh.astype(BF16)

    return _call(
        body,
        (h2, p, target, g_ple, w_pg, w_pp),
        name="ple_fwd_bwd",
        grid=(s // ts,),
        in_specs=[
            _rows(ts, D_MODEL),
            _rows(ts, PLE_DIM),
            _rows(ts, D_MODEL),
            _resident((1, D_MODEL)),
            _resident((D_MODEL, D_MODEL)),
            _resident((N_DEV, PLE_DIM, blk)),
        ],
        out_specs=[
            _acc((1, SMALL_LANES)),
            _rows(ts, D_MODEL),
            _rows(ts, D_MODEL),
            _acc((D_MODEL, D_MODEL)),
            _acc((N_DEV, PLE_DIM, blk)),
            _acc((1, D_MODEL)),
        ],
        out_shape=[
            jax.ShapeDtypeStruct((1, SMALL_LANES), F32),
            jax.ShapeDtypeStruct((s, D_MODEL), F32),
            jax.ShapeDtypeStruct((s, D_MODEL), BF16),
            jax.ShapeDtypeStruct((D_MODEL, D_MODEL), F32),
            jax.ShapeDtypeStruct((N_DEV, PLE_DIM, blk), F32),
            jax.ShapeDtypeStruct((1, D_MODEL), F32),
        ],
        scratch_shapes=[pltpu.VMEM((ts, D_MODEL), F32)],
        rider=rider,
    )


def _ffn_bwd_hidden(dh2b, gt, up, w_down, rider=None):
    s = dh2b.shape[0]
    ts = min(TOKEN_TILE, s)

    def body(dhb_ref, gt_ref, up_ref, wd_ref, dgt_ref, dup_ref):
        dhb = dhb_ref[...]
        for c in range(D_FF // FF_CHUNK):
            cols = slice(c * FF_CHUNK, (c + 1) * FF_CHUNK)
            d_act = _nt(dhb, wd_ref[cols, :])
            gtv = gt_ref[:, cols].astype(F32)
            upv = up_ref[:, cols].astype(F32)
            sg = jax.nn.sigmoid(gtv)
            dup_ref[:, cols] = (d_act * (gtv * sg)).astype(BF16)
            dgt_ref[:, cols] = ((d_act * upv) * (sg * (1.0 + gtv * (1.0 - sg)))).astype(BF16)

    return _call(
        body,
        (dh2b, gt, up, w_down),
        name="ffn_bwd_hidden",
        grid=(s // ts,),
        in_specs=[_rows(ts, D_MODEL), _rows(ts, D_FF), _rows(ts, D_FF), _resident((D_FF, D_MODEL))],
        out_specs=[_rows(ts, D_FF), _rows(ts, D_FF)],
        out_shape=[jax.ShapeDtypeStruct((s, D_FF), BF16), jax.ShapeDtypeStruct((s, D_FF), BF16)],
        rider=rider,
    )


def _ffn_bwd_in(dgt, dup, dh2, h1, g_ffn, wg_t, wu_t, rider=None):
    s = h1.shape[0]
    ts = min(TOKEN_TILE, s)

    def body(dgt_ref, dup_ref, dh_ref, h1_ref, g_ref, wg_ref, wu_ref, dh1_ref, dh1b_ref, dg_ref):
        @pl.when(pl.program_id(0) == 0)
        def _():
            dg_ref[...] = jnp.zeros_like(dg_ref)

        d_hn = _nn(dgt_ref[...], wg_ref[...]) + _nn(dup_ref[...], wu_ref[...])
        h1v = h1_ref[...]
        d_x, d_g = _rms_bwd(d_hn, h1v, _rms(h1v), g_ref[...])
        dg_ref[...] += d_g
        dh1 = dh_ref[...] + d_x
        dh1_ref[...] = dh1
        dh1b_ref[...] = dh1.astype(BF16)

    return _call(
        body,
        (dgt, dup, dh2, h1, g_ffn, wg_t, wu_t),
        name="ffn_bwd_in",
        grid=(s // ts,),
        in_specs=[
            _rows(ts, D_FF), _rows(ts, D_FF), _rows(ts, D_MODEL), _rows(ts, D_MODEL),
            _resident((1, D_MODEL)), _resident((D_FF, D_MODEL)), _resident((D_FF, D_MODEL)),
        ],
        out_specs=[_rows(ts, D_MODEL), _rows(ts, D_MODEL), _acc((1, D_MODEL))],
        out_shape=[
            jax.ShapeDtypeStruct((s, D_MODEL), F32),
            jax.ShapeDtypeStruct((s, D_MODEL), BF16),
            jax.ShapeDtypeStruct((1, D_MODEL), F32),
        ],
        rider=rider,
    )


def _ffn_bwd_w(which, lhs, rhs, rider=None):
    s = rhs.shape[0]
    pair = isinstance(lhs, tuple)
    lhs = lhs if pair else (lhs,)

    def body(*refs):
        rhs_ref, dw_ref = refs[-2:]
        if pair:
            lhs_v = _silu_mul(refs[0][...].astype(F32), refs[1][...].astype(F32)).astype(BF16)
        else:
            lhs_v = refs[0][...]
        dw_ref[...] = _tn(lhs_v, rhs_ref[...])

    return _call(
        body,
        (*lhs, rhs),
        name=f"ffn_bwd_{which}",
        grid=(D_FF // FF_CHUNK,),
        in_specs=[pl.BlockSpec((s, FF_CHUNK), lambda i: (0, i))] * len(lhs) + [_resident((s, D_MODEL))],
        out_specs=[_rows(FF_CHUNK, D_MODEL)],
        out_shape=[jax.ShapeDtypeStruct((D_FF, D_MODEL), F32)],
        rider=rider,
    )


def _mix_bwd(dh1b, u, w_out, w_pool, pool_scale, rider=None):
    s = u.shape[0]
    ts = min(TOKEN_TILE, s)
    nt = s // ts
    halo_after = lambda i: (jnp.minimum((i + 1) * (ts // POOL_HALO), s // POOL_HALO - 1), 0)
    n_groups = len(POOL_SIZES)

    def body(dh_ref, dhn_ref, u_ref, uh_ref, wo_ref, wp_ref, sc_ref, da_ref, du_ref, dwp_ref, dsc_ref):
        i = pl.program_id(0)

        @pl.when(i == 0)
        def _():
            dwp_ref[...] = jnp.zeros_like(dwp_ref)
            dsc_ref[...] = jnp.zeros_like(dsc_ref)

        dh = dh_ref[...]
        da_ref[...] = _nt(dh, wo_ref[:ATTN_WIDTH, :])
        dh_next = jnp.where(i < nt - 1, dhn_ref[...], jnp.zeros_like(dhn_ref))
        dm_ext = _nt(jnp.concatenate([dh, dh_next], axis=0), wo_ref[ATTN_WIDTH:, :])
        pooled = _pooled(u_ref[...], uh_ref[...], i, ts)
        t_ext = i * ts + lax.broadcasted_iota(jnp.int32, (ts + POOL_HALO, 1), 0)
        for g, w in enumerate(POOL_SIZES):
            lanes = slice(g * POOL_GROUP, (g + 1) * POOL_GROUP)
            wp = wp_ref[g].astype(BF16)
            pg = pooled[g].astype(BF16)
            dm_g = dm_ext[:, lanes]
            dsc_ref[:, lanes] += jnp.sum(dm_g[:ts, :] * _nn(pg, wp), axis=0, keepdims=True)
            dy = (dm_g * sc_ref[:, lanes]).astype(BF16)
            dwp_ref[g] += _tn(pg, dy[:ts, :])
            d_pool = _nt(dy, wp)
            acc = d_pool / jnp.minimum(t_ext + 1, w).astype(F32)
            shift = 1
            while shift < w:
                acc = acc + pltpu.roll(acc, ts + POOL_HALO - shift, 0)
                shift *= 2
            du_ref[:, lanes] = acc[:ts, :] - d_pool[:ts, :]

    return _call(
        body,
        (dh1b, dh1b, u, u, w_out, w_pool, pool_scale),
        name="mix_bwd",
        grid=(nt,),
        in_specs=[
            _rows(ts, D_MODEL),
            pl.BlockSpec((POOL_HALO, D_MODEL), halo_after),
            _rows(ts, POOL_WIDTH),
            pl.BlockSpec((POOL_HALO, POOL_WIDTH), _halo_before(ts)),
            _resident((D_MODEL, D_MODEL)),
            _resident((n_groups, POOL_GROUP, POOL_GROUP)),
            _resident((1, POOL_WIDTH)),
        ],
        out_specs=[
            _rows(ts, ATTN_WIDTH),
            _rows(ts, POOL_WIDTH),
            _acc((n_groups, POOL_GROUP, POOL_GROUP)),
            _acc((1, POOL_WIDTH)),
        ],
        out_shape=[
            jax.ShapeDtypeStruct((s, ATTN_WIDTH), F32),
            jax.ShapeDtypeStruct((s, POOL_WIDTH), F32),
            jax.ShapeDtypeStruct((n_groups, POOL_GROUP, POOL_GROUP), F32),
            jax.ShapeDtypeStruct((1, POOL_WIDTH), F32),
        ],
        rider=rider,
    )


def _out_w_bwd(a, m, dh1b, rider=None):
    s = dh1b.shape[0]

    def body(a_ref, m_ref, dh_ref, dw_ref):
        @pl.when(pl.program_id(0) == 0)
        def _():
            dw_ref[...] = _tn(a_ref[...], dh_ref[...])

        @pl.when(pl.program_id(0) == 1)
        def _():
            dw_ref[...] = _tn(m_ref[...], dh_ref[...])

    return _call(
        body,
        (a, m, dh1b),
        name="out_w_bwd",
        grid=(2,),
        in_specs=[_resident((s, ATTN_WIDTH)), _resident((s, POOL_WIDTH)), _resident((s, D_MODEL))],
        out_specs=[_rows(ATTN_WIDTH, D_MODEL)],
        out_shape=[jax.ShapeDtypeStruct((D_MODEL, D_MODEL), F32)],
        rider=rider,
    )


def _attn_bwd(qn, kn, v, a, da, tab, sinks, rider=None):
    s = qn.shape[0]
    qb = ATTN_STEP_BLOCKS
    rows = qb * BLOCK
    n_groups = s // rows
    group, prev = _attn_specs(n_groups)
    done = lambda n: (jnp.maximum(n - 1, 0), 0)

    def body(sink_ref, q_ref, kc_ref, kp_ref, vc_ref, vp_ref, o_ref, do_ref, tab_ref,
             dq_ref, dk_ref, dv_ref, dl_ref, ds_ref, k_carry, v_carry, sink_acc):
        n = pl.program_id(0)

        @pl.when(n == 0)
        def _():
            dl_ref[...] = jnp.zeros_like(dl_ref)
            k_carry[...] = jnp.zeros_like(k_carry)
            v_carry[...] = jnp.zeros_like(v_carry)
            sink_acc[...] = jnp.zeros_like(sink_acc)

        @pl.when(n < n_groups)
        def _():
            first = n == 0
            lo_mask = _lane_lo((BLOCK, BLOCK))
            dks, dvs = [], []
            for b in range(qb):
                at = slice(b * BLOCK, (b + 1) * BLOCK)
                keys = _band(kp_ref, kc_ref, b)
                vals = _band(vp_ref, vc_ref, b)
                q_st = _stack_heads([q_ref[at, p * BLOCK:(p + 1) * BLOCK] for p in range(4)], lo_mask)
                do_st = _stack_heads([do_ref[at, p * BLOCK:(p + 1) * BLOCK] for p in range(4)], lo_mask)
                o_st = _stack_heads([o_ref[at, p * BLOCK:(p + 1) * BLOCK].astype(F32) for p in range(4)], lo_mask)
                dq_st, dk_parts, dv_parts = [], [], []
                for half, heads in enumerate((HEADS_A, HEADS_B)):
                    probs, p_sink = _band_probs(q_st[half], keys[half], tab_ref[half], _sink_column(sink_ref, heads),
                                                first if b == 0 else None)
                    delta = jnp.sum(do_st[half] * o_st[half], axis=-1, keepdims=True)
                    dob = do_st[half].astype(BF16)
                    dl = probs * (_nt(dob, vals[half]) - delta)
                    dl_ref[half] += dl
                    sink_acc[half] += p_sink * delta
                    dsb = (dl * (HEAD_DIM ** -0.5)).astype(BF16)
                    dq_st.append(_nn(dsb, keys[half]))
                    dk_parts.append(_tn(dsb, q_st[half]))
                    dv_parts.append(_tn(probs.astype(BF16), dob))
                dq = _unstack_heads(dq_st[0], dq_st[1], lo_mask)
                for p in range(4):
                    dq_ref[at, p * BLOCK:(p + 1) * BLOCK] = dq[p]
                dks.append(dk_parts[0] + pltpu.roll(dk_parts[1], HEAD_DIM, 1))
                dvs.append(dv_parts[0] + pltpu.roll(dv_parts[1], HEAD_DIM, 1))
            last = slice((qb - 1) * BLOCK, qb * BLOCK)
            for parts, out_ref, carry in ((dks, dk_ref, k_carry), (dvs, dv_ref, v_carry)):
                out_ref[...] = carry[...]
                out_ref[last, :] += parts[0][:BLOCK, :]
                for b in range(qb):
                    own = parts[b][BLOCK:, :]
                    carry[b * BLOCK:(b + 1) * BLOCK, :] = own + parts[b + 1][:BLOCK, :] if b + 1 < qb else own

        @pl.when(n == n_groups)
        def _():
            dk_ref[...] = k_carry[...]
            dv_ref[...] = v_carry[...]
            for half, heads in enumerate((HEADS_A, HEADS_B)):
                for slot, h in enumerate(heads):
                    tot = jnp.sum(sink_acc[half, slot * BLOCK:(slot + 1) * BLOCK, :], axis=0, keepdims=True)
                    ds_ref[h:h + 1, :] = jnp.broadcast_to(-tot, (1, SMALL_LANES))

    return _call(
        body,
        (sinks, qn, kn, kn, v, v, a, da, tab),
        name="attn_bwd",
        grid=(n_groups + 1,),
        in_specs=[
            pl.BlockSpec(memory_space=pltpu.SMEM),
            pl.BlockSpec((rows, ATTN_WIDTH), group),
            pl.BlockSpec((rows, KV_WIDTH), group),
            pl.BlockSpec((BLOCK, KV_WIDTH), prev),
            pl.BlockSpec((rows, KV_WIDTH), group),
            pl.BlockSpec((BLOCK, KV_WIDTH), prev),
            pl.BlockSpec((rows, ATTN_WIDTH), group),
            pl.BlockSpec((rows, ATTN_WIDTH), group),
            _resident((2, 4 * BLOCK, 2 * BLOCK)),
        ],
        out_specs=[
            pl.BlockSpec((rows, ATTN_WIDTH), group),
            pl.BlockSpec((rows, KV_WIDTH), done),
            pl.BlockSpec((rows, KV_WIDTH), done),
            _acc((2, 4 * BLOCK, 2 * BLOCK)),
            _acc((N_DEV, SMALL_LANES)),
        ],
        out_shape=[
            jax.ShapeDtypeStruct((s, ATTN_WIDTH), F32),
            jax.ShapeDtypeStruct((s, KV_WIDTH), F32),
            jax.ShapeDtypeStruct((s, KV_WIDTH), F32),
            jax.ShapeDtypeStruct((2, 4 * BLOCK, 2 * BLOCK), F32),
            jax.ShapeDtypeStruct((N_DEV, SMALL_LANES), F32),
        ],
        scratch_shapes=[
            pltpu.VMEM((rows, KV_WIDTH), F32),
            pltpu.VMEM((rows, KV_WIDTH), F32),
            pltpu.VMEM((2, 4 * BLOCK, 1), F32),
        ],
        rider=rider,
    )


def _fold_heads(acc):
    t = acc + pltpu.roll(acc, HEAD_DIM, 1)
    out = t[:, :SMALL_LANES]
    for g in range(1, acc.shape[1] // SMALL_LANES):
        out = out + t[:, g * SMALL_LANES:(g + 1) * SMALL_LANES]
    return out


def _in_proj_bwd(dqn, dkn, dv, du, zqk, x, dh1, g_attn, gq_t, gk_t, w_in_t, rider=None):
    s = x.shape[0]
    ts = min(TOKEN_TILE, s)
    nt = s // ts

    def head_norm_bwd(d_n, raw, g_t, bmat):
        r = lax.rsqrt(_seg_mean(raw * raw, bmat) + EPS)
        gy = d_n * g_t
        d_raw = r * gy - raw * (r * r * r) * _seg_mean(gy * raw, bmat)
        return d_raw, jnp.sum(d_n * (raw * r), axis=0, keepdims=True)

    def body(dqn_ref, dkn_ref, dv_ref, du_ref, zqk_ref, x_ref, dh1_ref, g_ref, gq_ref, gk_ref, w_ref, bq_ref, bk_ref,
             gx_ref, dw_ref, dg_ref, dgq_ref, dgk_ref, dz_ref, gq_acc, gk_acc):
        i = pl.program_id(0)

        @pl.when(i == 0)
        def _():
            dw_ref[...] = jnp.zeros_like(dw_ref)
            dg_ref[...] = jnp.zeros_like(dg_ref)
            gq_acc[...] = jnp.zeros_like(gq_acc)
            gk_acc[...] = jnp.zeros_like(gk_acc)

        d_q, d_gq = head_norm_bwd(dqn_ref[...], zqk_ref[:, :ATTN_WIDTH], gq_ref[...], bq_ref[...])
        d_k, d_gk = head_norm_bwd(dkn_ref[...], zqk_ref[:, ATTN_WIDTH:], gk_ref[...], bk_ref[...])
        gq_acc[...] += d_gq
        gk_acc[...] += d_gk
        dz_ref[:, :ATTN_WIDTH] = d_q.astype(BF16)
        dz_ref[:, ATTN_WIDTH:ATTN_WIDTH + KV_WIDTH] = d_k.astype(BF16)
        dz_ref[:, ATTN_WIDTH + KV_WIDTH:ATTN_WIDTH + 2 * KV_WIDTH] = dv_ref[...].astype(BF16)
        dz_ref[:, ATTN_WIDTH + 2 * KV_WIDTH:] = du_ref[...].astype(BF16)
        dz = dz_ref[...]
        xf = x_ref[...]
        r = _rms(xf)
        hn = ((xf * r) * g_ref[...]).astype(BF16)
        dw_ref[...] += _tn(dz, hn)
        d_x, d_g = _rms_bwd(_nn(dz, w_ref[...]), xf, r, g_ref[...])
        dg_ref[...] += d_g
        gx_ref[...] = dh1_ref[...] + d_x

        @pl.when(i == nt - 1)
        def _():
            dgq_ref[...] = _fold_heads(gq_acc[...])
            dgk_ref[...] = _fold_heads(gk_acc[...])

    return _call(
        body,
        (dqn, dkn, dv, du, zqk, x, dh1, g_attn, gq_t, gk_t, w_in_t,
      _head_mean_matrix(ATTN_WIDTH), _head_mean_matrix(KV_WIDTH)),
        name="in_proj_bwd",
        grid=(nt,),
        in_specs=[
            _rows(ts, ATTN_WIDTH),
            _rows(ts, KV_WIDTH),
            _rows(ts, KV_WIDTH),
            _rows(ts, POOL_WIDTH),
            _rows(ts, ATTN_WIDTH + KV_WIDTH),
            _rows(ts, D_MODEL),
            _rows(ts, D_MODEL),
            _resident((1, D_MODEL)),
            _resident((1, ATTN_WIDTH)),
            _resident((1, KV_WIDTH)),
            _resident((IN_WIDTH, D_MODEL)),
            _resident((ATTN_WIDTH, ATTN_WIDTH)),
            _resident((KV_WIDTH, KV_WIDTH)),
        ],
        out_specs=[
            _rows(ts, D_MODEL),
            _acc((IN_WIDTH, D_MODEL)),
            _acc((1, D_MODEL)),
            _acc((1, SMALL_LANES)),
            _acc((1, SMALL_LANES)),
        ],
        out_shape=[
            jax.ShapeDtypeStruct((s, D_MODEL), F32),
            jax.ShapeDtypeStruct((IN_WIDTH, D_MODEL), F32),
            jax.ShapeDtypeStruct((1, D_MODEL), F32),
            jax.ShapeDtypeStruct((1, SMALL_LANES), F32),
            jax.ShapeDtypeStruct((1, SMALL_LANES), F32),
        ],
        scratch_shapes=[
            pltpu.VMEM((ts, IN_WIDTH), BF16),
            pltpu.VMEM((1, ATTN_WIDTH), F32),
            pltpu.VMEM((1, KV_WIDTH), F32),
        ],
        rider=rider,
    )


BIG_WEIGHTS = (
    ("w_in", True, IN_WIDTH // N_DEV, D_MODEL),
    ("w_out", False, D_MODEL // N_DEV, D_MODEL),
    ("w_gate", True, D_FF // N_DEV, D_MODEL),
    ("w_up", True, D_FF // N_DEV, D_MODEL),
    ("w_down", False, D_FF // N_DEV, D_MODEL),
    ("w_ple_gate", False, D_MODEL // N_DEV, D_MODEL),
    ("w_ple_proj", False, PLE_DIM, D_MODEL // N_DEV),
)
N_BIG = len(BIG_WEIGHTS)


def _place():
    x, y, c = lax.axis_index("x"), lax.axis_index("y"), lax.axis_index("c")
    chips = [(1 - x, y), (x, 1 - y), (1 - x, 1 - y)]
    return x, y, c, chips


class _Gather:
    def __init__(self, n):
        self.n = n
        self.sems = [pltpu.SemaphoreType.DMA((n, 7)), pltpu.SemaphoreType.DMA((n, 7)), pltpu.SemaphoreType.DMA((n,))]

    def _ctx(self, srcs, outs, sems):
        send_sems, recv_sems, local_sems = sems
        x, y, c, chips = _place()
        me, sibling = (x, y, c), (x, y, 1 - c)

        def block(k, owner):
            px, py, pc = owner
            return outs[k].at[4 * px + 2 * py + pc]

        def copy(k, idx, owner, to, src=None):
            return pltpu.make_async_remote_copy(
                src_ref=block(k, owner) if src is None else src, dst_ref=block(k, owner),
                send_sem=send_sems.at[k, idx], recv_sem=recv_sems.at[k, idx], device_id=to, device_id_type=MESH)

        def local(k):
            return pltpu.make_async_copy(srcs[k], block(k, me), local_sems.at[k])

        return c, chips, me, sibling, copy, local

    def begin(self, srcs, outs, sems):
        c, chips, me, sibling, copy, local = self._ctx(srcs, outs, sems)
        for k in range(self.n):
            local(k).start()
            copy(k, 0, me, sibling, src=srcs[k]).start()
            for j, chip in enumerate(chips):
                copy(k, 1 + j, me, (*chip, c), src=srcs[k]).start()

    def middle(self, srcs, outs, sems):
        c, chips, me, sibling, copy, local = self._ctx(srcs, outs, sems)
        for j, chip in enumerate(chips):
            for k in range(self.n):
                copy(k, 1 + j, (*chip, c), me).wait_recv()
                copy(k, 4 + j, (*chip, c), sibling).start()

    def end(self, srcs, outs, sems):
        c, chips, me, sibling, copy, local = self._ctx(srcs, outs, sems)
        for k in range(self.n):
            copy(k, 0, sibling, me).wait_recv()
            for j, chip in enumerate(chips):
                copy(k, 4 + j, (*chip, 1 - c), me).wait_recv()
        for k in range(self.n):
            copy(k, 0, me, sibling, src=srcs[k]).wait_send()
            for j, chip in enumerate(chips):
                copy(k, 1 + j, me, (*chip, c), src=srcs[k]).wait_send()
                copy(k, 4 + j, (*chip, c), sibling).wait_send()
            local(k).wait()


def _gather_rider(blocks):
    g = _Gather(len(blocks))
    shapes = [jax.ShapeDtypeStruct((N_DEV, *b.shape), b.dtype) for b in blocks]
    return _Rider(blocks, shapes, g.sems, g.begin, g.end, g.middle)


def _cast_and_gather_first(shards):
    g = _Gather(1)
    any_spec = pl.BlockSpec(memory_space=pl.ANY)
    vmem = pl.BlockSpec(memory_space=pltpu.VMEM)

    def body(*refs):
        ins, outs, gathered, sems = refs[:N_BIG], refs[N_BIG:2 * N_BIG], refs[2 * N_BIG], refs[2 * N_BIG + 1:]
        outs[0][...] = ins[0][...].astype(BF16)
        g.begin(outs[:1], [gathered], sems)
        for k in range(1, N_BIG):
            outs[k][...] = ins[k][...].astype(BF16)
        g.middle(outs[:1], [gathered], sems)
        g.end(outs[:1], [gathered], sems)

    res = pl.pallas_call(
        body,
        name="cast_and_gather_first",
        in_specs=[vmem] * N_BIG,
        out_specs=[vmem] * N_BIG + [any_spec],
        out_shape=[jax.ShapeDtypeStruct((r, c), BF16) for _, _, r, c in BIG_WEIGHTS]
        + [jax.ShapeDtypeStruct((N_DEV, *BIG_WEIGHTS[0][2:]), BF16)],
        scratch_shapes=g.sems,
    )(*shards)
    return list(res[:N_BIG]), res[N_BIG]


def _sibling_rider(grads):
    n = len(grads)

    def copies(gs, lands, sems):
        send_sems, recv_sems = sems
        x, y, c, _ = _place()
        return [
            pltpu.make_async_remote_copy(
                src_ref=gs[k].at[:, 1 - c], dst_ref=lands[k], send_sem=send_sems.at[k], recv_sem=recv_sems.at[k],
                device_id=(x, y, 1 - c), device_id_type=MESH)
            for k in range(n)
        ]

    def begin(gs, lands, sems):
        for cp in copies(gs, lands, sems):
            cp.start()

    def end(gs, lands, sems):
        for cp in copies(gs, lands, sems):
            cp.wait()

    shapes = [jax.ShapeDtypeStruct((N_CHIPS, *g.shape[2:]), F32) for g in grads]
    return _Rider(grads, shapes, [pltpu.SemaphoreType.DMA((n,)), pltpu.SemaphoreType.DMA((n,))], begin, end)


def _chip_sum(k, place, grad, from_sibling):
    _, _, r, c = BIG_WEIGHTS[k]

    def body(place_ref, g_ref, l_ref, own_ref, send_ref):
        q = pl.program_id(0)
        tot = g_ref[0, 0] + l_ref[0]
        mine = q == 2 * place_ref[0] + place_ref[1]

        @pl.when(mine)
        def _():
            own_ref[...] = tot

        send_ref[0] = jnp.where(mine, 0.0, tot).astype(BF16)

    return pl.pallas_call(
        body,
        name=f"chip_sum_{BIG_WEIGHTS[k][0]}",
        grid_spec=pltpu.PrefetchScalarGridSpec(
            num_scalar_prefetch=1,
            grid=(N_CHIPS,),
            in_specs=[
                pl.BlockSpec((1, 1, r, c), lambda q, place: (q, place[2], 0, 0)),
                pl.BlockSpec((1, r, c), lambda q, place: (q, 0, 0)),
            ],
            out_specs=[
                pl.BlockSpec((r, c), lambda q, place: (0, 0)),
                pl.BlockSpec((1, r, c), lambda q, place: (q, 0, 0)),
            ],
        ),
        out_shape=[jax.ShapeDtypeStruct((r, c), F32), jax.ShapeDtypeStruct((N_CHIPS, r, c), BF16)],
    )(place, grad, from_sibling)


def _chips_rider(to_send, small=None):
    n = len(to_send)
    inputs = list(to_send) + ([] if small is None else [small])
    shapes = [jax.ShapeDtypeStruct((3, *t.shape[1:]), BF16) for t in to_send]
    sems = [pltpu.SemaphoreType.DMA((max(n, 1), 3)), pltpu.SemaphoreType.DMA((max(n, 1), 3))]
    if small is not None:
        shapes.append(jax.ShapeDtypeStruct((N_DEV, *small.shape), F32))
        sems += [pltpu.SemaphoreType.DMA((7,)), pltpu.SemaphoreType.DMA((7,)), pltpu.SemaphoreType.DMA]

    def copies(ins, outs, sem_refs):
        x, y, c, chips = _place()
        out = []
        for k in range(n):
            for j, (px, py) in enumerate(chips):
                out.append(pltpu.make_async_remote_copy(
                    src_ref=ins[k].at[2 * px + py], dst_ref=outs[k].at[j],
                    send_sem=sem_refs[0].at[k, j], recv_sem=sem_refs[1].at[k, j],
                    device_id=(px, py, c), device_id_type=MESH))
        local = None
        if small is not None:
            me = 4 * x + 2 * y + c
            local = pltpu.make_async_copy(ins[n], outs[n].at[me], sem_refs[4])
            rel = 0
            for fx in (0, 1):
                for fy in (0, 1):
                    for fc in (0, 1):
                        if (fx, fy, fc) != (0, 0, 0):
                            out.append(pltpu.make_async_remote_copy(
                                src_ref=ins[n], dst_ref=outs[n].at[me],
                                send_sem=sem_refs[2].at[rel], recv_sem=sem_refs[3].at[rel],
                                device_id=(x ^ fx, y ^ fy, c ^ fc), device_id_type=MESH))
                            rel += 1
        return out, local

    def begin(ins, outs, sem_refs):
        remote, local = copies(ins, outs, sem_refs)
        if local is not None:
            local.start()
        for cp in remote:
            cp.start()

    def end(ins, outs, sem_refs):
        remote, local = copies(ins, outs, sem_refs)
        for cp in remote:
            cp.wait()
        if local is not None:
            local.wait()

    return _Rider(inputs, shapes, sems, begin, end)


def _exchange(name, rider):
    return _call(lambda: None, (), name=name, grid=(1,), in_specs=[], out_specs=[], out_shape=[], rider=rider)[1]


def _merge_riders(*riders):
    riders = [r for r in riders if r is not None]
    if len(riders) == 1:
        return riders[0]

    def split(refs, counts):
        out, at = [], 0
        for n in counts:
            out.append(refs[at:at + n])
            at += n
        return out

    def run(which):
        def fn(ins, outs, sems):
            parts = zip(riders, split(ins, [len(r.inputs) for r in riders]),
                        split(outs, [len(r.out_shapes) for r in riders]), split(sems, [len(r.sems) for r in riders]))
            for r, i, o, s in parts:
                hook = getattr(r, which)
                if hook is not None:
                    hook(i, o, s)
        return fn

    middle = run("middle") if any(r.middle is not None for r in riders) else None
    return _Rider(sum((r.inputs for r in riders), []), sum((r.out_shapes for r in riders), []),
                  sum((r.sems for r in riders), []), run("begin"), run("end"), middle)


def _split_outputs(outs, *riders):
    res, at = [], 0
    for r in riders:
        res.append(outs[at:at + len(r.out_shapes)])
        at += len(r.out_shapes)
    return res


def _adamw(w, g, m, v):
    m = ADAM_B1 * m + (1.0 - ADAM_B1) * g
    v = ADAM_B2 * v + (1.0 - ADAM_B2) * jnp.square(g)
    m_hat = m / (1.0 - ADAM_B1 ** ADAM_STEP)
    v_hat = v / (1.0 - ADAM_B2 ** ADAM_STEP)
    delta = -ADAM_LR * (m_hat / (jnp.sqrt(v_hat) + ADAM_EPS) + ADAM_WD * w)
    return delta, m, v


def _adamw_big(k, own, landed, w, m, v, rider=None):
    name, _, r, c = BIG_WEIGHTS[k]
    tile = r // 2
    tiles = lambda i: (i, 0)

    def body(own_ref, land_ref, w_ref, m_ref, v_ref, g_ref, d_ref, nm_ref, nv_ref):
        g = ((own_ref[...] + land_ref[0].astype(F32)) + land_ref[1].astype(F32)) + land_ref[2].astype(F32)
        g_ref[...] = g
        d_ref[...], nm_ref[...], nv_ref[...] = _adamw(w_ref[...], g, m_ref[...], v_ref[...])

    return _call(
        body,
        (own, landed, w, m, v),
        name=f"adamw_{name}",
        grid=(r // tile,),
        in_specs=[pl.BlockSpec((tile, c), tiles), pl.BlockSpec((3, tile, c), lambda i: (0, i, 0))]
        + [pl.BlockSpec((tile, c), tiles)] * 3,
        out_specs=[pl.BlockSpec((tile, c), tiles)] * 4,
        out_shape=[jax.ShapeDtypeStruct((r, c), F32)] * 4,
        rider=rider,
    )


def _sum_small(parts_list):
    n = len(parts_list)

    def body(*refs):
        for p_ref, out_ref in zip(refs[:n], refs[n:]):
            tot = p_ref[0]
            for j in range(1, N_DEV):
                tot = tot + p_ref[j]
            out_ref[...] = tot

    return pl.pallas_call(body, name="sum_small",
                          out_shape=[jax.ShapeDtypeStruct(p.shape[1:], F32) for p in parts_list])(*parts_list)


def _adamw_small(grads, ws, ms, vs):
    n = len(grads)

    def body(*refs):
        g_refs, w_refs, m_refs, v_refs = refs[:n], refs[n:2 * n], refs[2 * n:3 * n], refs[3 * n:4 * n]
        outs = refs[4 * n:]
        for i in range(n):
            d, nm, nv = _adamw(w_refs[i][...], g_refs[i][...], m_refs[i][...], v_refs[i][...])
            outs[i][...] = d
            outs[n + i][...] = nm
            outs[2 * n + i][...] = nv

    shapes = [jax.ShapeDtypeStruct(w.shape, F32) for w in ws]
    return pl.pallas_call(body, name="adamw_small", out_shape=shapes * 3)(*grads, *ws, *ms, *vs)


SMALL_NAMES = ("g_attn_norm", "g_q", "g_k", "attn_sinks", "rel_bias", "w_pool", "pool_scale", "g_ffn_norm", "g_ple_norm")


def _pack_small(arrays):
    rows, offsets = [], []
    at = 0
    for a in arrays:
        flat = a.reshape(-1)
        n_rows = -(-flat.shape[0] // (8 * SMALL_LANES)) * 8
        flat = jnp.pad(flat, (0, n_rows * SMALL_LANES - flat.shape[0]))
        rows.append(flat.reshape(n_rows, SMALL_LANES))
        offsets.append(at)
        at += n_rows
    return jnp.concatenate(rows, axis=0), offsets


def kernel(x, p, w_in, w_out, g_attn_norm, g_q, g_k, attn_sinks, rel_bias, w_pool, pool_scale, g_ffn_norm, w_gate, w_up, w_down, g_ple_norm, w_ple_gate, w_ple_proj, loss_target, m_w_in, m_w_out, m_g_attn_norm, m_g_q, m_g_k, m_attn_sinks, m_rel_bias, m_w_pool, m_pool_scale, m_g_ffn_norm, m_w_gate, m_w_up, m_w_down, m_g_ple_norm, m_w_ple_gate, m_w_ple_proj, v_w_in, v_w_out, v_g_attn_norm, v_g_q, v_g_k, v_attn_sinks, v_rel_bias, v_w_pool, v_pool_scale, v_g_ffn_norm, v_w_gate, v_w_up, v_w_down, v_g_ple_norm, v_w_ple_gate, v_w_ple_proj):
    weights = dict(w_in=w_in, w_out=w_out, g_attn_norm=g_attn_norm, g_q=g_q, g_k=g_k, attn_sinks=attn_sinks,
                   rel_bias=rel_bias, w_pool=w_pool, pool_scale=pool_scale, g_ffn_norm=g_ffn_norm, w_gate=w_gate,
                   w_up=w_up, w_down=w_down, g_ple_norm=g_ple_norm, w_ple_gate=w_ple_gate, w_ple_proj=w_ple_proj)
    m_in = dict(w_in=m_w_in, w_out=m_w_out, g_attn_norm=m_g_attn_norm, g_q=m_g_q, g_k=m_g_k, attn_sinks=m_attn_sinks,
                rel_bias=m_rel_bias, w_pool=m_w_pool, pool_scale=m_pool_scale, g_ffn_norm=m_g_ffn_norm, w_gate=m_w_gate,
                w_up=m_w_up, w_down=m_w_down, g_ple_norm=m_g_ple_norm, w_ple_gate=m_w_ple_gate, w_ple_proj=m_w_ple_proj)
    v_in = dict(w_in=v_w_in, w_out=v_w_out, g_attn_norm=v_g_attn_norm, g_q=v_g_q, g_k=v_g_k, attn_sinks=v_attn_sinks,
                rel_bias=v_rel_bias, w_pool=v_w_pool, pool_scale=v_pool_scale, g_ffn_norm=v_g_ffn_norm, w_gate=v_w_gate,
                w_up=v_w_up, w_down=v_w_down, g_ple_norm=v_g_ple_norm, w_ple_gate=v_w_ple_gate, w_ple_proj=v_w_ple_proj)

    xs = x[0]
    ps = p[0, 0]
    target = loss_target[0]
    wp = w_pool[0]
    gq_t = jnp.tile(g_q, (1, ATTN_WIDTH // HEAD_DIM))
    gk_t = jnp.tile(g_k, (1, KV_WIDTH // HEAD_DIM))

    def to_blocks(k, arr):
        return jnp.swapaxes(arr[0], 0, 1) if BIG_WEIGHTS[k][1] else arr[0]

    def from_blocks(k, arr):
        return (jnp.swapaxes(arr, 0, 1) if BIG_WEIGHTS[k][1] else arr)[None]

    IN, OUT, GATE, UP, DOWN, PG, PP = range(N_BIG)
    full = lambda g: g.reshape(N_DEV * g.shape[1], g.shape[2])
    halves = lambda k, g: g.reshape(N_CHIPS, 2, *BIG_WEIGHTS[k][2:])
    place = jnp.stack([lax.axis_index("x"), lax.axis_index("y"), lax.axis_index("c")]).astype(jnp.int32)

    sh, w_in_g = _cast_and_gather_first([to_blocks(k, weights[name]) for k, (name, _, _, _) in enumerate(BIG_WEIGHTS)])
    w_in_t = full(w_in_g)

    tab = _bias_table(rel_bias.T)
    (zqk, qn, kn, v, u), (w_out_g, w_pp_g) = _in_proj(xs, g_attn_norm, w_in_t, gq_t, gk_t,
                                                    rider=_gather_rider([sh[OUT], sh[PP]]))
    (a,), (wg_g,) = _attn_fwd(qn, kn, v, tab, attn_sinks, rider=_gather_rider([sh[GATE]]))
    w_out_f = full(w_out_g)
    (h1, hn2, m_out), (wu_g,) = _mix_out(u, a, xs, w_out_f, wp, pool_scale, g_ffn_norm, rider=_gather_rider([sh[UP]]))
    wg_t, wu_t = full(wg_g), full(wu_g)
    (gt, up), (wd_g, w_pg_g) = _ffn_up(hn2, wg_t, wu_t, rider=_gather_rider([sh[DOWN], sh[PG]]))
    w_down_f = full(wd_g)
    (h2,), _ = _ffn_down(gt, up, h1, w_down_f)

    sums, landed = [None] * N_BIG, [None] * N_BIG

    def chip_sum(k, grad, from_sibling):
        sums[k] = _chip_sum(k, place, halves(k, grad), from_sibling)

    (loss_part, dh2, dh2b, d_wpg, d_wpp, d_g_ple), _ = _ple_fwd_bwd(h2, ps, target, g_ple_norm, full(w_pg_g), w_pp_g)
    (dgt, dup), sib = _ffn_bwd_hidden(dh2b, gt, up, w_down_f, rider=_sibling_rider([halves(PG, d_wpg), halves(PP, d_wpp)]))
    chip_sum(PG, d_wpg, sib[0])
    chip_sum(PP, d_wpp, sib[1])
    (d_wd,), (landed[PG], landed[PP]) = _ffn_bwd_w("down", (gt, up), dh2b, rider=_chips_rider([sums[PG][1], sums[PP][1]]))
    (d_wg_t,), sib = _ffn_bwd_w("gate", dgt, hn2, rider=_sibling_rider([halves(DOWN, d_wd)]))
    chip_sum(DOWN, d_wd, sib[0])
    r_sib, r_chips = _sibling_rider([halves(GATE, d_wg_t)]), _chips_rider([sums[DOWN][1]])
    (d_wu_t,), outs = _ffn_bwd_w("up", dup, hn2, rider=_merge_riders(r_sib, r_chips))
    sib, (landed[DOWN],) = _split_outputs(outs, r_sib, r_chips)
    chip_sum(GATE, d_wg_t, sib[0])
    r_sib, r_chips = _sibling_rider([halves(UP, d_wu_t)]), _chips_rider([sums[GATE][1]])
    (dh1, dh1b, d_g_ffn), outs = _ffn_bwd_in(dgt, dup, dh2, h1, g_ffn_norm, wg_t, wu_t, rider=_merge_riders(r_sib, r_chips))
    sib, (landed[GATE],) = _split_outputs(outs, r_sib, r_chips)
    chip_sum(UP, d_wu_t, sib[0])
    (d_wo,), _ = _out_w_bwd(a, m_out, dh1b)
    r_sib, r_chips = _sibling_rider([halves(OUT, d_wo)]), _chips_rider([sums[UP][1]])
    (da, du, d_wpool, d_scale), outs = _mix_bwd(dh1b, u, w_out_f, wp, pool_scale, rider=_merge_riders(r_sib, r_chips))
    sib, (landed[UP],) = _split_outputs(outs, r_sib, r_chips)
    chip_sum(OUT, d_wo, sib[0])
    early, early_at = _pack_small([d_wpool, d_scale, d_g_ffn, d_g_ple, loss_part[:, :1]])
    (dqn, dkn, dv, dl_acc, d_sinks), (landed[OUT], early_all) = _attn_bwd(
        qn, kn, v, a, da, tab, attn_sinks, rider=_chips_rider([sums[OUT][1]], early))
    (grad_x, d_win_t, d_g_attn, d_gq, d_gk), _ = _in_proj_bwd(dqn, dkn, dv, du, zqk, xs, dh1, g_attn_norm, gq_t, gk_t, w_in_t)
    (d_rel_t,), sib = _bias_table_bwd(dl_acc, rider=_sibling_rider([halves(IN, d_win_t)]))
    chip_sum(IN, d_win_t, sib[0])
    late, late_at = _pack_small([d_g_attn, d_gq[:, :HEAD_DIM], d_gk[:, :HEAD_DIM], d_sinks[:, 0], d_rel_t[:, :N_BUCKETS]])
    landed[IN], late_all = _exchange("last_exchange", _chips_rider([sums[IN][1]], late))

    out = {"grad": {}, "delta": {}, "new_m": {}, "new_v": {}}
    for k, (name, _, _, _) in enumerate(BIG_WEIGHTS):
        res, _ = _adamw_big(k, sums[k][0], landed[k], to_blocks(k, weights[name]), to_blocks(k, m_in[name]),
                            to_blocks(k, v_in[name]))
        for kind, r in zip(("grad", "delta", "new_m", "new_v"), res):
            out[kind][name] = from_blocks(k, r)
    early_sum, late_sum = _sum_small([early_all, late_all])

    def unpack(packed, at, shape):
        n = math.prod(shape)
        return packed[at:at + -(-n // SMALL_LANES)].reshape(-1)[:n].reshape(shape)

    small_grads = dict(
        w_pool=unpack(early_sum, early_at[0], w_pool.shape), pool_scale=unpack(early_sum, early_at[1], pool_scale.shape),
        g_ffn_norm=unpack(early_sum, early_at[2], g_ffn_norm.shape), g_ple_norm=unpack(early_sum, early_at[3], g_ple_norm.shape),
        g_attn_norm=unpack(late_sum, late_at[0], g_attn_norm.shape), g_q=unpack(late_sum, late_at[1], g_q.shape),
        g_k=unpack(late_sum, late_at[2], g_k.shape), attn_sinks=unpack(late_sum, late_at[3], attn_sinks.shape),
        rel_bias=unpack(late_sum, late_at[4], rel_bias.T.shape))
    loss = early_sum[early_at[4], 0]
    flip = lambda name, arr: arr.T if name == "rel_bias" else arr
    updates = _adamw_small([small_grads[n] for n in SMALL_NAMES], [flip(n, weights[n]) for n in SMALL_NAMES],
                           [flip(n, m_in[n]) for n in SMALL_NAMES], [flip(n, v_in[n]) for n in SMALL_NAMES])
    n_small = len(SMALL_NAMES)
    for i, name in enumerate(SMALL_NAMES):
        out["grad"][name] = flip(name, small_grads[name])
        out["delta"][name] = flip(name, updates[i])
        out["new_m"][name] = flip(name, updates[n_small + i])
        out["new_v"][name] = flip(name, updates[2 * n_small + i])

    order = ("w_in", "w_out", "g_attn_norm", "g_q", "g_k", "attn_sinks", "rel_bias", "w_pool", "pool_scale",
             "g_ffn_norm", "w_gate", "w_up", "w_down", "g_ple_norm", "w_ple_gate", "w_ple_proj")
    return (loss, grad_x[None], *[out["grad"][n] for n in order], *[out["delta"][n] for n in order],
            *[out["new_m"][n] for n in order], *[out["new_v"][n] for n in order])
```

```python
import functools
import math

import jax
import jax.numpy as jnp
import numpy as np
from jax import lax
from jax.experimental import pallas as pl
from jax.experimental.pallas import tpu as pltpu

F32 = jnp.float32
BF16 = jnp.bfloat16
MESH = pl.DeviceIdType.MESH

D_MODEL = 1024
HEAD_DIM = 64
ATTN_WIDTH = 512
KV_WIDTH = 128
POOL_WIDTH = 512
POOL_SIZES = (2, 4, 8, 16)
POOL_GROUP = 128
POOL_HALO = 16
IN_WIDTH = 1280
D_FF = 2816
PLE_DIM = 256
BLOCK = 128
N_BUCKETS = 32
MAX_DISTANCE = 128
EPS = 1e-6
N_DEV = 8
N_CHIPS = 4

ADAM_LR = 0.001
ADAM_B1 = 0.9
ADAM_B2 = 0.999
ADAM_EPS = 1e-08
ADAM_WD = 0.01
ADAM_STEP = 10

TOKEN_TILE = 512
FFN_BWD_TILE = 256
FF_CHUNK = 256
ATTN_STEP_BLOCKS = 2
GATE_ROWS_EARLY = 96
UP_ROWS_EARLY = 64
HEADS_A = (0, 2, 5, 7)
HEADS_B = (1, 3, 4, 6)
SMALL_LANES = 128


def _nn(a, b):
    return jnp.dot(a, b, preferred_element_type=F32)


def _nt(a, b):
    return lax.dot_general(a, b, (((1,), (1,)), ((), ())), preferred_element_type=F32)


def _tn(a, b):
    return lax.dot_general(a, b, (((0,), (0,)), ((), ())), preferred_element_type=F32)


def _resident(shape):
    nd = len(shape)
    return pl.BlockSpec(shape, lambda i, _nd=nd: (0,) * _nd, pipeline_mode=pl.Buffered(1))


def _rows(tile, width):
    return pl.BlockSpec((tile, width), lambda i: (i, 0))


def _acc(shape):
    nd = len(shape)
    return pl.BlockSpec(shape, lambda i, _nd=nd: (0,) * _nd)


def _head_mean_matrix(width):
    idx = np.arange(width) // HEAD_DIM
    return jnp.asarray((idx[:, None] == idx[None, :]).astype(np.float32) / HEAD_DIM, dtype=BF16)


def _seg_mean(v, bmat):
    hi = v.astype(BF16)
    lo = (v - hi.astype(F32)).astype(BF16)
    return _nn(hi, bmat) + _nn(lo, bmat)


def _rms(x):
    return lax.rsqrt(jnp.mean(x * x, axis=-1, keepdims=True) + EPS)


def _rms_bwd(d_y, x, r, g):
    gy = d_y * g
    d_x = r * gy - x * (r * r * r) * jnp.mean(gy * x, axis=-1, keepdims=True)
    d_g = jnp.sum(d_y * (x * r), axis=0, keepdims=True)
    return d_x, d_g


def _lane_lo(shape):
    return lax.broadcasted_iota(jnp.int32, shape, 1) < HEAD_DIM


class _Rider:
    def __init__(self, inputs, out_shapes, sems, begin, end, middle=None, aliases=None):
        self.inputs, self.out_shapes, self.sems = list(inputs), list(out_shapes), list(sems)
        self.begin, self.middle, self.end = begin, middle, end
        self.aliases = dict(aliases or {})


def _call(body, args, *, name, grid, in_specs, out_specs, out_shape, scratch_shapes=(), rider=None):
    in_specs, out_specs, out_shape, scratch_shapes = list(in_specs), list(out_specs), list(out_shape), list(scratch_shapes)
    if rider is None:
        outs = pl.pallas_call(body, name=name, grid=grid, in_specs=in_specs, out_specs=out_specs, out_shape=out_shape,
                              scratch_shapes=scratch_shapes)(*args)
        return list(outs), []
    n_in, n_out, n_scr = len(in_specs), len(out_shape), len(scratch_shapes)
    r_in, r_out = len(rider.inputs), len(rider.out_shapes)
    n_steps = grid[0]

    def hosted(*refs):
        ins, refs = refs[:n_in], refs[n_in:]
        r_ins, refs = refs[:r_in], refs[r_in:]
        outs, refs = refs[:n_out], refs[n_out:]
        r_outs, refs = refs[:r_out], refs[r_out:]
        scratch, r_sems = refs[:n_scr], refs[n_scr:]
        step = pl.program_id(0)

        @pl.when(step == 0)
        def _():
            rider.begin(r_ins, r_outs, r_sems)

        if rider.middle is not None:
            @pl.when(step == n_steps - 1)
            def _():
                rider.middle(r_ins, r_outs, r_sems)

        body(*ins, *outs, *scratch)

        @pl.when(step == n_steps - 1)
        def _():
            rider.end(r_ins, r_outs, r_sems)

    any_spec = pl.BlockSpec(memory_space=pl.ANY)
    outs = pl.pallas_call(
        hosted, name=name, grid=grid,
        in_specs=in_specs + [any_spec] * r_in,
        out_specs=out_specs + [any_spec] * r_out,
        out_shape=out_shape + rider.out_shapes,
        scratch_shapes=scratch_shapes + rider.sems,
        input_output_aliases={n_in + i: n_out + o for i, o in rider.aliases.items()},
    )(*args, *rider.inputs)
    return list(outs[:n_out]), list(outs[n_out:])


def _in_proj(x, g_attn, w_in_t, gq_t, gk_t, rider=None):
    s = x.shape[0]
    ts = min(TOKEN_TILE, s)

    def body(x_ref, g_ref, w_ref, gq_ref, gk_ref, bq_ref, bk_ref, zqk_ref, qn_ref, kn_ref, v_ref, u_ref):
        xf = x_ref[...]
        hn = ((xf * _rms(xf)) * g_ref[...]).astype(BF16)
        z = _nt(hn, w_ref[...])
        q = z[:, :ATTN_WIDTH]
        k = z[:, ATTN_WIDTH:ATTN_WIDTH + KV_WIDTH]
        zqk_ref[...] = z[:, :ATTN_WIDTH + KV_WIDTH]
        rq = lax.rsqrt(_seg_mean(q * q, bq_ref[...]) + EPS)
        qn_ref[...] = ((q * rq) * gq_ref[...]).astype(BF16)
        rk = lax.rsqrt(_seg_mean(k * k, bk_ref[...]) + EPS)
        kn_ref[...] = ((k * rk) * gk_ref[...]).astype(BF16)
        v_ref[...] = z[:, ATTN_WIDTH + KV_WIDTH:ATTN_WIDTH + 2 * KV_WIDTH].astype(BF16)
        u_ref[...] = z[:, ATTN_WIDTH + 2 * KV_WIDTH:]

    return _call(
        body,
        (x, g_attn, w_in_t, gq_t, gk_t, _head_mean_matrix(ATTN_WIDTH), _head_mean_matrix(KV_WIDTH)),
        name="in_proj",
        grid=(s // ts,),
        in_specs=[
            _rows(ts, D_MODEL),
            _resident((1, D_MODEL)),
            _resident((IN_WIDTH, D_MODEL)),
            _resident((1, ATTN_WIDTH)),
            _resident((1, KV_WIDTH)),
            _resident((ATTN_WIDTH, ATTN_WIDTH)),
            _resident((KV_WIDTH, KV_WIDTH)),
        ],
        out_specs=[
            _rows(ts, ATTN_WIDTH + KV_WIDTH),
            _rows(ts, ATTN_WIDTH),
            _rows(ts, KV_WIDTH),
            _rows(ts, KV_WIDTH),
            _rows(ts, POOL_WIDTH),
        ],
        out_shape=[
            jax.ShapeDtypeStruct((s, ATTN_WIDTH + KV_WIDTH), F32),
            jax.ShapeDtypeStruct((s, ATTN_WIDTH), BF16),
            jax.ShapeDtypeStruct((s, KV_WIDTH), BF16),
            jax.ShapeDtypeStruct((s, KV_WIDTH), BF16),
            jax.ShapeDtypeStruct((s, POOL_WIDTH), F32),
        ],
        rider=rider,
    )


def _bucket_ranges():
    n = np.arange(MAX_DISTANCE)
    max_exact = N_BUCKETS // 2
    nf = np.maximum(n, 1).astype(np.float64)
    large = max_exact + (np.log(nf / max_exact) / math.log(MAX_DISTANCE / max_exact) * (N_BUCKETS - max_exact)).astype(np.int64)
    bucket = np.where(n < max_exact, n, np.minimum(large, N_BUCKETS - 1))
    out = []
    for b in range(N_BUCKETS):
        idx = np.nonzero(bucket == b)[0]
        out.append((int(idx.min()), int(idx.max()) + 1))
    return out


def _band_distance():
    i = lax.broadcasted_iota(jnp.int32, (BLOCK, 2 * BLOCK), 0)
    j = lax.broadcasted_iota(jnp.int32, (BLOCK, 2 * BLOCK), 1)
    return BLOCK + i - j


def _bias_table(rel_bias_t):
    ranges = _bucket_ranges()

    def body(rb_ref, tab_ref):
        d = _band_distance()
        for half, heads in enumerate((HEADS_A, HEADS_B)):
            for slot, h in enumerate(heads):
                t = jnp.full((BLOCK, 2 * BLOCK), -jnp.inf, F32)
                for b, (lo, hi) in enumerate(ranges):
                    t = jnp.where((d >= lo) & (d < hi), rb_ref[h, b], t)
                tab_ref[half, slot * BLOCK:(slot + 1) * BLOCK, :] = t

    return pl.pallas_call(
        body,
        name="bias_table",
        in_specs=[pl.BlockSpec(memory_space=pltpu.SMEM)],
        out_shape=jax.ShapeDtypeStruct((2, 4 * BLOCK, 2 * BLOCK), F32),
    )(rel_bias_t)


def _bias_table_bwd(dl_acc, rider=None):
    ranges = _bucket_ranges()
    n_heads = len(HEADS_A) + len(HEADS_B)

    def body(dl_ref, out_ref):
        d = _band_distance()
        row = lax.broadcasted_iota(jnp.int32, (n_heads, SMALL_LANES), 0)
        lane = lax.broadcasted_iota(jnp.int32, (n_heads, SMALL_LANES), 1)
        out = jnp.zeros((n_heads, SMALL_LANES), F32)
        for b, (lo, hi) in enumerate(ranges):
            in_bucket = (d >= lo) & (d < hi)
            for half, heads in enumerate((HEADS_A, HEADS_B)):
                for slot, h in enumerate(heads):
                    g = dl_ref[half, slot * BLOCK:(slot + 1) * BLOCK, :]
                    part = jnp.sum(jnp.where(in_bucket, g, 0.0), axis=0, keepdims=True)
                    tot = jnp.sum(part, axis=1, keepdims=True)
                    out = jnp.where((row == h) & (lane == b), tot, out)
        out_ref[...] = out

    return _call(
        body,
        (dl_acc,),
        name="bias_table_bwd",
        grid=(1,),
        in_specs=[_acc((2, 4 * BLOCK, 2 * BLOCK))],
        out_specs=[_acc((n_heads, SMALL_LANES))],
        out_shape=[jax.ShapeDtypeStruct((n_heads, SMALL_LANES), F32)],
        rider=rider,
    )


def _stack_heads(pairs, lo_mask):
    zero = jnp.zeros_like(pairs[0])
    lo = [jnp.where(lo_mask, t, zero) for t in pairs]
    hi = [jnp.where(lo_mask, zero, t) for t in pairs]
    return (jnp.concatenate([lo[0], lo[1], hi[2], hi[3]], axis=0),
            jnp.concatenate([hi[0], hi[1], lo[2], lo[3]], axis=0))


def _unstack_heads(out_a, out_b, lo_mask):
    t = lambda x, r: x[r * BLOCK:(r + 1) * BLOCK, :]
    return [
        jnp.where(lo_mask, t(out_a, 0), t(out_b, 0)),
        jnp.where(lo_mask, t(out_a, 1), t(out_b, 1)),
        jnp.where(lo_mask, t(out_b, 2), t(out_a, 2)),
        jnp.where(lo_mask, t(out_b, 3), t(out_a, 3)),
    ]


def _sink_column(sink_ref, heads):
    row = lax.broadcasted_iota(jnp.int32, (4 * BLOCK, 1), 0)
    col = jnp.full((4 * BLOCK, 1), sink_ref[0, heads[3]], F32)
    for slot in (2, 1, 0):
        col = jnp.where(row < (slot + 1) * BLOCK, sink_ref[0, heads[slot]], col)
    return col


def _band_probs(q_stack, keys, tab, sink, first_block):
    s = _nt(q_stack, keys) * (HEAD_DIM ** -0.5) + tab
    if first_block is not None:
        col = lax.broadcasted_iota(jnp.int32, s.shape, 1)
        s = jnp.where(jnp.logical_and(first_block, col < BLOCK), -jnp.inf, s)
    m = jnp.maximum(jnp.max(s, axis=-1, keepdims=True), sink)
    e = jnp.exp(s - m)
    e_sink = jnp.exp(sink - m)
    den = jnp.sum(e, axis=-1, keepdims=True) + e_sink
    return e / den, e_sink / den


def _attn_specs(n_groups):
    group = lambda n: (jnp.minimum(n, n_groups - 1), 0)
    prev = lambda n: (jnp.maximum(jnp.minimum(n, n_groups - 1) * ATTN_STEP_BLOCKS - 1, 0), 0)
    return group, prev


def _band(prev_ref, group_ref, b):
    rows = lambda i: group_ref[i * BLOCK:(i + 1) * BLOCK, :]
    band = jnp.concatenate([prev_ref[...] if b == 0 else rows(b - 1), rows(b)], axis=0)
    return band, pltpu.roll(band, HEAD_DIM, 1)


def _attn_fwd(qn, kn, v, tab, sinks, rider=None):
    s = qn.shape[0]
    n_groups = s // (ATTN_STEP_BLOCKS * BLOCK)
    group, prev = _attn_specs(n_groups)
    rows = ATTN_STEP_BLOCKS * BLOCK

    def body(sink_ref, q_ref, kc_ref, kp_ref, vc_ref, vp_ref, tab_ref, o_ref):
        first = pl.program_id(0) == 0
        lo_mask = _lane_lo((BLOCK, BLOCK))
        for b in range(ATTN_STEP_BLOCKS):
            at = slice(b * BLOCK, (b + 1) * BLOCK)
            kk, kk_sw = _band(kp_ref, kc_ref, b)
            vv, vv_sw = _band(vp_ref, vc_ref, b)
            q_a, q_b = _stack_heads([q_ref[at, p * BLOCK:(p + 1) * BLOCK] for p in range(4)], lo_mask)
            no_prev = first if b == 0 else None
            p_a, _ = _band_probs(q_a, kk, tab_ref[0], _sink_column(sink_ref, HEADS_A), no_prev)
            p_b, _ = _band_probs(q_b, kk_sw, tab_ref[1], _sink_column(sink_ref, HEADS_B), no_prev)
            out = _unstack_heads(_nn(p_a.astype(BF16), vv), _nn(p_b.astype(BF16), vv_sw), lo_mask)
            for p in range(4):
                o_ref[at, p * BLOCK:(p + 1) * BLOCK] = out[p].astype(BF16)

    return _call(
        body,
        (sinks, qn, kn, kn, v, v, tab),
        name="attn_fwd",
        grid=(n_groups,),
        in_specs=[
            pl.BlockSpec(memory_space=pltpu.SMEM),
            pl.BlockSpec((rows, ATTN_WIDTH), group),
            pl.BlockSpec((rows, KV_WIDTH), group),
            pl.BlockSpec((BLOCK, KV_WIDTH), prev),
            pl.BlockSpec((rows, KV_WIDTH), group),
            pl.BlockSpec((BLOCK, KV_WIDTH), prev),
            _resident((2, 4 * BLOCK, 2 * BLOCK)),
        ],
        out_specs=[pl.BlockSpec((rows, ATTN_WIDTH), group)],
        out_shape=[jax.ShapeDtypeStruct((s, ATTN_WIDTH), BF16)],
        rider=rider,
    )


def _pooled(u_tile, u_halo, tile_index, tile_rows):
    halo = jnp.where(tile_index > 0, u_halo, 0.0)
    ext = jnp.concatenate([halo, u_tile], axis=0)
    sums = []
    acc = ext
    for shift in (1, 2, 4, 8):
        acc = acc + pltpu.roll(acc, shift, 0)
        sums.append(acc)
    t = tile_index * tile_rows + lax.broadcasted_iota(jnp.int32, (tile_rows, 1), 0)
    out = []
    for g, w in enumerate(POOL_SIZES):
        lanes = slice(g * POOL_GROUP, (g + 1) * POOL_GROUP)
        cnt = jnp.minimum(t + 1, w).astype(F32)
        out.append(sums[g][POOL_HALO:, lanes] / cnt - u_tile[:, lanes])
    return out


def _halo_before(tile):
    return lambda i: (jnp.maximum(i * (tile // POOL_HALO) - 1, 0), 0)


def _mix_out(u, a, x, w_out, w_pool, pool_scale, g_ffn, rider=None):
    s = x.shape[0]
    ts = min(TOKEN_TILE, s)

    def body(u_ref, uh_ref, a_ref, x_ref, wo_ref, wp_ref, sc_ref, g_ref, h1_ref, hn_ref, m_ref):
        i = pl.program_id(0)
        pooled = _pooled(u_ref[...], uh_ref[...], i, ts)
        for g in range(len(POOL_SIZES)):
            lanes = slice(g * POOL_GROUP, (g + 1) * POOL_GROUP)
            y = _nn(pooled[g].astype(BF16), wp_ref[g].astype(BF16))
            m_ref[:, lanes] = (y * sc_ref[:, lanes]).astype(BF16)
        h1 = x_ref[...] + _nn(a_ref[...], wo_ref[:ATTN_WIDTH, :]) + _nn(m_ref[...], wo_ref[ATTN_WIDTH:, :])
        h1_ref[...] = h1
        hn_ref[...] = ((h1 * _rms(h1)) * g_ref[...]).astype(BF16)

    return _call(
        body,
        (u, u, a, x, w_out, w_pool, pool_scale, g_ffn),
        name="mix_out",
        grid=(s // ts,),
        in_specs=[
            _rows(ts, POOL_WIDTH),
            pl.BlockSpec((POOL_HALO, POOL_WIDTH), _halo_before(ts)),
            _rows(ts, ATTN_WIDTH),
            _rows(ts, D_MODEL),
            _resident((D_MODEL, D_MODEL)),
            _resident((len(POOL_SIZES), POOL_GROUP, POOL_GROUP)),
            _resident((1, POOL_WIDTH)),
            _resident((1, D_MODEL)),
        ],
        out_specs=[_rows(ts, D_MODEL), _rows(ts, D_MODEL), _rows(ts, POOL_WIDTH)],
        out_shape=[
            jax.ShapeDtypeStruct((s, D_MODEL), F32),
            jax.ShapeDtypeStruct((s, D_MODEL), BF16),
            jax.ShapeDtypeStruct((s, POOL_WIDTH), BF16),
        ],
        rider=rider,
    )


def _ffn_up(hn2, wg_t, wu_t, rider=None):
    s = hn2.shape[0]
    ts = min(TOKEN_TILE, s)

    def body(hn_ref, wg_ref, wu_ref, gt_ref, up_ref):
        hn = hn_ref[...]
        for c in range(D_FF // FF_CHUNK):
            cols = slice(c * FF_CHUNK, (c + 1) * FF_CHUNK)
            gt_ref[:, cols] = _nt(hn, wg_ref[cols, :]).astype(BF16)
            up_ref[:, cols] = _nt(hn, wu_ref[cols, :]).astype(BF16)

    return _call(
        body,
        (hn2, wg_t, wu_t),
        name="ffn_up",
        grid=(s // ts,),
        in_specs=[_rows(ts, D_MODEL), _resident((D_FF, D_MODEL)), _resident((D_FF, D_MODEL))],
        out_specs=[_rows(ts, D_FF), _rows(ts, D_FF)],
        out_shape=[jax.ShapeDtypeStruct((s, D_FF), BF16), jax.ShapeDtypeStruct((s, D_FF), BF16)],
        rider=rider,
    )


def _silu_mul(gt, up):
    return (gt * jax.nn.sigmoid(gt)) * up


def _ffn_down(gt, up, h1, w_down, rider=None):
    s = h1.shape[0]
    ts = min(TOKEN_TILE, s)

    def body(gt_ref, up_ref, h1_ref, wd_ref, h2_ref, act_ref):
        for c in range(D_FF // FF_CHUNK):
            cols = slice(c * FF_CHUNK, (c + 1) * FF_CHUNK)
            act_ref[:, cols] = _silu_mul(gt_ref[:, cols].astype(F32), up_ref[:, cols].astype(F32)).astype(BF16)
        h2_ref[...] = h1_ref[...] + _nn(act_ref[...], wd_ref[...])

    return _call(
        body,
        (gt, up, h1, w_down),
        name="ffn_down",
        grid=(s // ts,),
        in_specs=[_rows(ts, D_FF), _rows(ts, D_FF), _rows(ts, D_MODEL), _resident((D_FF, D_MODEL))],
        out_specs=[_rows(ts, D_MODEL)],
        out_shape=[jax.ShapeDtypeStruct((s, D_MODEL), F32)],
        scratch_shapes=[pltpu.VMEM((ts, D_FF), BF16)],
        rider=rider,
    )


def _ple_fwd_bwd(h2, p, target, g_ple, w_pg, w_pp, rider=None):
    s = h2.shape[0]
    ts = min(TOKEN_TILE, s)
    blk = D_MODEL // N_DEV

    def body(h2_ref, p_ref, t_ref, g_ref, wpg_ref, wpp_ref, loss_ref, dh_ref, dhb_ref, dwpg_ref, dwpp_ref, dg_ref, pp_ref):
        @pl.when(pl.program_id(0) == 0)
        def _():
            loss_ref[...] = jnp.zeros_like(loss_ref)
            dwpg_ref[...] = jnp.zeros_like(dwpg_ref)
            dwpp_ref[...] = jnp.zeros_like(dwpp_ref)
            dg_ref[...] = jnp.zeros_like(dg_ref)

        h2v = h2_ref[...]
        r = _rms(h2v)
        hn = ((h2v * r) * g_ref[...]).astype(BF16)
        gate = jax.nn.sigmoid(_nn(hn, wpg_ref[...]))
        pb = p_ref[...].astype(BF16)
        for j in range(N_DEV):
            pp_ref[:, j * blk:(j + 1) * blk] = _nn(pb, wpp_ref[j])
        pp = pp_ref[...]
        diff = (h2v + gate * pp) - t_ref[...]
        loss_ref[...] += jnp.sum(jnp.sum(diff * diff, axis=0, keepdims=True), axis=1, keepdims=True) * (0.5 / D_MODEL)
        dy = diff * (1.0 / D_MODEL)
        d_pp = (dy * gate).astype(BF16)
        d_pre = ((dy * pp) * (gate * (1.0 - gate))).astype(BF16)
        for j in range(N_DEV):
            dwpp_ref[j] += _tn(pb, d_pp[:, j * blk:(j + 1) * blk])
        dwpg_ref[...] += _tn(hn, d_pre)
        d_x, d_g = _rms_bwd(_nt(d_pre, wpg_ref[...]), h2v, r, g_ref[...])
        dg_ref[...] += d_g
        dh = dy + d_x
        dh_ref[...] = dh
        dhb_ref[...] = dh.astype(BF16)

    return _call(
        body,
        (h2, p, target, g_ple, w_pg, w_pp),
        name="ple_fwd_bwd",
        grid=(s // ts,),
        in_specs=[
            _rows(ts, D_MODEL),
            _rows(ts, PLE_DIM),
            _rows(ts, D_MODEL),
            _resident((1, D_MODEL)),
            _resident((D_MODEL, D_MODEL)),
            _resident((N_DEV, PLE_DIM, blk)),
        ],
        out_specs=[
            _acc((1, SMALL_LANES)),
            _rows(ts, D_MODEL),
            _rows(ts, D_MODEL),
            _acc((D_MODEL, D_MODEL)),
            _acc((N_DEV, PLE_DIM, blk)),
            _acc((1, D_MODEL)),
        ],
        out_shape=[
            jax.ShapeDtypeStruct((1, SMALL_LANES), F32),
            jax.ShapeDtypeStruct((s, D_MODEL), F32),
            jax.ShapeDtypeStruct((s, D_MODEL), BF16),
            jax.ShapeDtypeStruct((D_MODEL, D_MODEL), F32),
            jax.ShapeDtypeStruct((N_DEV, PLE_DIM, blk), F32),
            jax.ShapeDtypeStruct((1, D_MODEL), F32),
        ],
        scratch_shapes=[pltpu.VMEM((ts, D_MODEL), F32)],
        rider=rider,
    )


def _ffn_bwd_act(dh2, h1, gt, up, g_ffn, wg_t, wu_t, w_down, rider=None):
    s = h1.shape[0]
    ts = min(FFN_BWD_TILE, s)

    def body(dh_ref, h1_ref, gt_ref, up_ref, g_ref, wg_ref, wu_ref, wd_ref,
             act_ref, dgt_ref, dup_ref, dh1_ref, dh1b_ref, dg_ref):
        @pl.when(pl.program_id(0) == 0)
        def _():
            dg_ref[...] = jnp.zeros_like(dg_ref)

        dhb = dh_ref[...].astype(BF16)
        for c in range(D_FF // FF_CHUNK):
            cols = slice(c * FF_CHUNK, (c + 1) * FF_CHUNK)
            d_act = _nt(dhb, wd_ref[cols, :])
            gtv = gt_ref[:, cols].astype(F32)
            upv = up_ref[:, cols].astype(F32)
            sg = jax.nn.sigmoid(gtv)
            silu = gtv * sg
            act_ref[:, cols] = (silu * upv).astype(BF16)
            dup_ref[:, cols] = (d_act * silu).astype(BF16)
            dgt_ref[:, cols] = ((d_act * upv) * (sg * (1.0 + gtv * (1.0 - sg)))).astype(BF16)
        d_hn = _nn(dgt_ref[...], wg_ref[...]) + _nn(dup_ref[...], wu_ref[...])
        h1v = h1_ref[...]
        d_x, d_g = _rms_bwd(d_hn, h1v, _rms(h1v), g_ref[...])
        dg_ref[...] += d_g
        dh1 = dh_ref[...] + d_x
        dh1_ref[...] = dh1
        dh1b_ref[...] = dh1.astype(BF16)

    return _call(
        body,
        (dh2, h1, gt, up, g_ffn, wg_t, wu_t, w_down),
        name="ffn_bwd_act",
        grid=(s // ts,),
        in_specs=[
            _rows(ts, D_MODEL),
            _rows(ts, D_MODEL),
            _rows(ts, D_FF),
            _rows(ts, D_FF),
            _resident((1, D_MODEL)),
            _resident((D_FF, D_MODEL)),
            _resident((D_FF, D_MODEL)),
            _resident((D_FF, D_MODEL)),
        ],
        out_specs=[
            _rows(ts, D_FF), _rows(ts, D_FF), _rows(ts, D_FF),
            _rows(ts, D_MODEL), _rows(ts, D_MODEL), _acc((1, D_MODEL)),
        ],
        out_shape=[
            jax.ShapeDtypeStruct((s, D_FF), BF16),
            jax.ShapeDtypeStruct((s, D_FF), BF16),
            jax.ShapeDtypeStruct((s, D_FF), BF16),
            jax.ShapeDtypeStruct((s, D_MODEL), F32),
            jax.ShapeDtypeStruct((s, D_MODEL), BF16),
            jax.ShapeDtypeStruct((1, D_MODEL), F32),
        ],
        rider=rider,
    )


def _ffn_bwd_w(which, lhs, rhs, rider=None):
    s = rhs.shape[0]

    def body(lhs_ref, rhs_ref, dw_ref):
        dw_ref[...] = _tn(lhs_ref[...], rhs_ref[...])

    return _call(
        body,
        (lhs, rhs),
        name=f"ffn_bwd_{which}",
        grid=(D_FF // FF_CHUNK,),
        in_specs=[pl.BlockSpec((s, FF_CHUNK), lambda i: (0, i)), _resident((s, D_MODEL))],
        out_specs=[_rows(FF_CHUNK, D_MODEL)],
        out_shape=[jax.ShapeDtypeStruct((D_FF, D_MODEL), F32)],
        rider=rider,
    )


def _mix_bwd(dh1b, u, w_out, w_pool, pool_scale, rider=None):
    s = u.shape[0]
    ts = min(TOKEN_TILE, s)
    nt = s // ts
    halo_after = lambda i: (jnp.minimum((i + 1) * (ts // POOL_HALO), s // POOL_HALO - 1), 0)
    n_groups = len(POOL_SIZES)

    def body(dh_ref, dhn_ref, u_ref, uh_ref, wo_ref, wp_ref, sc_ref, da_ref, du_ref, dwp_ref, dsc_ref):
        i = pl.program_id(0)

        @pl.when(i == 0)
        def _():
            dwp_ref[...] = jnp.zeros_like(dwp_ref)
            dsc_ref[...] = jnp.zeros_like(dsc_ref)

        dh = dh_ref[...]
        da_ref[...] = _nt(dh, wo_ref[:ATTN_WIDTH, :])
        dh_next = jnp.where(i < nt - 1, dhn_ref[...], jnp.zeros_like(dhn_ref))
        dm_ext = _nt(jnp.concatenate([dh, dh_next], axis=0), wo_ref[ATTN_WIDTH:, :])
        pooled = _pooled(u_ref[...], uh_ref[...], i, ts)
        t_ext = i * ts + lax.broadcasted_iota(jnp.int32, (ts + POOL_HALO, 1), 0)
        for g, w in enumerate(POOL_SIZES):
            lanes = slice(g * POOL_GROUP, (g + 1) * POOL_GROUP)
            wp = wp_ref[g].astype(BF16)
            pg = pooled[g].astype(BF16)
            dm_g = dm_ext[:, lanes]
            dsc_ref[:, lanes] += jnp.sum(dm_g[:ts, :] * _nn(pg, wp), axis=0, keepdims=True)
            dy = (dm_g * sc_ref[:, lanes]).astype(BF16)
            dwp_ref[g] += _tn(pg, dy[:ts, :])
            d_pool = _nt(dy, wp)
            acc = d_pool / jnp.minimum(t_ext + 1, w).astype(F32)
            shift = 1
            while shift < w:
                acc = acc + pltpu.roll(acc, ts + POOL_HALO - shift, 0)
                shift *= 2
            du_ref[:, lanes] = acc[:ts, :] - d_pool[:ts, :]

    return _call(
        body,
        (dh1b, dh1b, u, u, w_out, w_pool, pool_scale),
        name="mix_bwd",
        grid=(nt,),
        in_specs=[
            _rows(ts, D_MODEL),
            pl.BlockSpec((POOL_HALO, D_MODEL), halo_after),
            _rows(ts, POOL_WIDTH),
            pl.BlockSpec((POOL_HALO, POOL_WIDTH), _halo_before(ts)),
            _resident((D_MODEL, D_MODEL)),
            _resident((n_groups, POOL_GROUP, POOL_GROUP)),
            _resident((1, POOL_WIDTH)),
        ],
        out_specs=[
            _rows(ts, ATTN_WIDTH),
            _rows(ts, POOL_WIDTH),
            _acc((n_groups, POOL_GROUP, POOL_GROUP)),
            _acc((1, POOL_WIDTH)),
        ],
        out_shape=[
            jax.ShapeDtypeStruct((s, ATTN_WIDTH), F32),
            jax.ShapeDtypeStruct((s, POOL_WIDTH), F32),
            jax.ShapeDtypeStruct((n_groups, POOL_GROUP, POOL_GROUP), F32),
            jax.ShapeDtypeStruct((1, POOL_WIDTH), F32),
        ],
        rider=rider,
    )


def _out_w_bwd(a, m, dh1b, rider=None):
    s = dh1b.shape[0]

    def body(a_ref, m_ref, dh_ref, dw_ref):
        @pl.when(pl.program_id(0) == 0)
        def _():
            dw_ref[...] = _tn(a_ref[...], dh_ref[...])

        @pl.when(pl.program_id(0) == 1)
        def _():
            dw_ref[...] = _tn(m_ref[...], dh_ref[...])

    return _call(
        body,
        (a, m, dh1b),
        name="out_w_bwd",
        grid=(2,),
        in_specs=[_resident((s, ATTN_WIDTH)), _resident((s, POOL_WIDTH)), _resident((s, D_MODEL))],
        out_specs=[_rows(ATTN_WIDTH, D_MODEL)],
        out_shape=[jax.ShapeDtypeStruct((D_MODEL, D_MODEL), F32)],
        rider=rider,
    )


def _attn_bwd(qn, kn, v, a, da, tab, sinks, rider=None):
    s = qn.shape[0]
    qb = ATTN_STEP_BLOCKS
    rows = qb * BLOCK
    n_groups = s // rows
    group, prev = _attn_specs(n_groups)
    done = lambda n: (jnp.maximum(n - 1, 0), 0)

    def body(sink_ref, q_ref, kc_ref, kp_ref, vc_ref, vp_ref, o_ref, do_ref, tab_ref,
             dq_ref, dk_ref, dv_ref, dl_ref, ds_ref, k_carry, v_carry, sink_acc):
        n = pl.program_id(0)

        @pl.when(n == 0)
        def _():
            dl_ref[...] = jnp.zeros_like(dl_ref)
            k_carry[...] = jnp.zeros_like(k_carry)
            v_carry[...] = jnp.zeros_like(v_carry)
            sink_acc[...] = jnp.zeros_like(sink_acc)

        @pl.when(n < n_groups)
        def _():
            first = n == 0
            lo_mask = _lane_lo((BLOCK, BLOCK))
            dks, dvs = [], []
            for b in range(qb):
                at = slice(b * BLOCK, (b + 1) * BLOCK)
                keys = _band(kp_ref, kc_ref, b)
                vals = _band(vp_ref, vc_ref, b)
                q_st = _stack_heads([q_ref[at, p * BLOCK:(p + 1) * BLOCK] for p in range(4)], lo_mask)
                do_st = _stack_heads([do_ref[at, p * BLOCK:(p + 1) * BLOCK] for p in range(4)], lo_mask)
                o_st = _stack_heads([o_ref[at, p * BLOCK:(p + 1) * BLOCK].astype(F32) for p in range(4)], lo_mask)
                dq_st, dk_parts, dv_parts = [], [], []
                for half, heads in enumerate((HEADS_A, HEADS_B)):
                    probs, p_sink = _band_probs(q_st[half], keys[half], tab_ref[half], _sink_column(sink_ref, heads),
                                                first if b == 0 else None)
                    delta = jnp.sum(do_st[half] * o_st[half], axis=-1, keepdims=True)
                    dob = do_st[half].astype(BF16)
                    dl = probs * (_nt(dob, vals[half]) - delta)
                    dl_ref[half] += dl
                    sink_acc[half] += p_sink * delta
                    dsb = (dl * (HEAD_DIM ** -0.5)).astype(BF16)
                    dq_st.append(_nn(dsb, keys[half]))
                    dk_parts.append(_tn(dsb, q_st[half]))
                    dv_parts.append(_tn(probs.astype(BF16), dob))
                dq = _unstack_heads(dq_st[0], dq_st[1], lo_mask)
                for p in range(4):
                    dq_ref[at, p * BLOCK:(p + 1) * BLOCK] = dq[p]
                dks.append(dk_parts[0] + pltpu.roll(dk_parts[1], HEAD_DIM, 1))
                dvs.append(dv_parts[0] + pltpu.roll(dv_parts[1], HEAD_DIM, 1))
            last = slice((qb - 1) * BLOCK, qb * BLOCK)
            for parts, out_ref, carry in ((dks, dk_ref, k_carry), (dvs, dv_ref, v_carry)):
                out_ref[...] = carry[...]
                out_ref[last, :] += parts[0][:BLOCK, :]
                for b in range(qb):
                    own = parts[b][BLOCK:, :]
                    carry[b * BLOCK:(b + 1) * BLOCK, :] = own + parts[b + 1][:BLOCK, :] if b + 1 < qb else own

        @pl.when(n == n_groups)
        def _():
            dk_ref[...] = k_carry[...]
            dv_ref[...] = v_carry[...]
            for half, heads in enumerate((HEADS_A, HEADS_B)):
                for slot, h in enumerate(heads):
                    tot = jnp.sum(sink_acc[half, slot * BLOCK:(slot + 1) * BLOCK, :], axis=0, keepdims=True)
                    ds_ref[h:h + 1, :] = jnp.broadcast_to(-tot, (1, SMALL_LANES))

    return _call(
        body,
        (sinks, qn, kn, kn, v, v, a, da, tab),
        name="attn_bwd",
        grid=(n_groups + 1,),
        in_specs=[
            pl.BlockSpec(memory_space=pltpu.SMEM),
            pl.BlockSpec((rows, ATTN_WIDTH), group),
            pl.BlockSpec((rows, KV_WIDTH), group),
            pl.BlockSpec((BLOCK, KV_WIDTH), prev),
            pl.BlockSpec((rows, KV_WIDTH), group),
            pl.BlockSpec((BLOCK, KV_WIDTH), prev),
            pl.BlockSpec((rows, ATTN_WIDTH), group),
            pl.BlockSpec((rows, ATTN_WIDTH), group),
            _resident((2, 4 * BLOCK, 2 * BLOCK)),
        ],
        out_specs=[
            pl.BlockSpec((rows, ATTN_WIDTH), group),
            pl.BlockSpec((rows, KV_WIDTH), done),
            pl.BlockSpec((rows, KV_WIDTH), done),
            _acc((2, 4 * BLOCK, 2 * BLOCK)),
            _acc((N_DEV, SMALL_LANES)),
        ],
        out_shape=[
            jax.ShapeDtypeStruct((s, ATTN_WIDTH), F32),
            jax.ShapeDtypeStruct((s, KV_WIDTH), F32),
            jax.ShapeDtypeStruct((s, KV_WIDTH), F32),
            jax.ShapeDtypeStruct((2, 4 * BLOCK, 2 * BLOCK), F32),
            jax.ShapeDtypeStruct((N_DEV, SMALL_LANES), F32),
        ],
        scratch_shapes=[
            pltpu.VMEM((rows, KV_WIDTH), F32),
            pltpu.VMEM((rows, KV_WIDTH), F32),
            pltpu.VMEM((2, 4 * BLOCK, 1), F32),
        ],
        rider=rider,
    )


def _fold_heads(acc):
    t = acc + pltpu.roll(acc, HEAD_DIM, 1)
    out = t[:, :SMALL_LANES]
    for g in range(1, acc.shape[1] // SMALL_LANES):
        out = out + t[:, g * SMALL_LANES:(g + 1) * SMALL_LANES]
    return out


def _in_proj_bwd(dqn, dkn, dv, du, zqk, x, dh1, g_attn, gq_t, gk_t, w_in_t, rider=None):
    s = x.shape[0]
    ts = min(TOKEN_TILE, s)
    nt = s // ts

    def head_norm_bwd(d_n, raw, g_t, bmat):
        r = lax.rsqrt(_seg_mean(raw * raw, bmat) + EPS)
        gy = d_n * g_t
        d_raw = r * gy - raw * (r * r * r) * _seg_mean(gy * raw, bmat)
        return d_raw, jnp.sum(d_n * (raw * r), axis=0, keepdims=True)

    def body(dqn_ref, dkn_ref, dv_ref, du_ref, zqk_ref, x_ref, dh1_ref, g_ref, gq_ref, gk_ref, w_ref, bq_ref, bk_ref,
             gx_ref, dw_ref, dg_ref, dgq_ref, dgk_ref, dz_ref, gq_acc, gk_acc):
        i = pl.program_id(0)

        @pl.when(i == 0)
        def _():
            dw_ref[...] = jnp.zeros_like(dw_ref)
            dg_ref[...] = jnp.zeros_like(dg_ref)
            gq_acc[...] = jnp.zeros_like(gq_acc)
            gk_acc[...] = jnp.zeros_like(gk_acc)

        d_q, d_gq = head_norm_bwd(dqn_ref[...], zqk_ref[:, :ATTN_WIDTH], gq_ref[...], bq_ref[...])
        d_k, d_gk = head_norm_bwd(dkn_ref[...], zqk_ref[:, ATTN_WIDTH:], gk_ref[...], bk_ref[...])
        gq_acc[...] += d_gq
        gk_acc[...] += d_gk
        dz_ref[:, :ATTN_WIDTH] = d_q.astype(BF16)
        dz_ref[:, ATTN_WIDTH:ATTN_WIDTH + KV_WIDTH] = d_k.astype(BF16)
        dz_ref[:, ATTN_WIDTH + KV_WIDTH:ATTN_WIDTH + 2 * KV_WIDTH] = dv_ref[...].astype(BF16)
        dz_ref[:, ATTN_WIDTH + 2 * KV_WIDTH:] = du_ref[...].astype(BF16)
        dz = dz_ref[...]
        xf = x_ref[...]
        r = _rms(xf)
        hn = ((xf * r) * g_ref[...]).astype(BF16)
        dw_ref[...] += _tn(dz, hn)
        d_x, d_g = _rms_bwd(_nn(dz, w_ref[...]), xf, r, g_ref[...])
        dg_ref[...] += d_g
        gx_ref[...] = dh1_ref[...] + d_x

        @pl.when(i == nt - 1)
        def _():
            dgq_ref[...] = _fold_heads(gq_acc[...])
            dgk_ref[...] = _fold_heads(gk_acc[...])

    return _call(
        body,
        (dqn, dkn, dv, du, zqk, x, dh1, g_attn, gq_t, gk_t, w_in_t,
      _head_mean_matrix(ATTN_WIDTH), _head_mean_matrix(KV_WIDTH)),
        name="in_proj_bwd",
        grid=(nt,),
        in_specs=[
            _rows(ts, ATTN_WIDTH),
            _rows(ts, KV_WIDTH),
            _rows(ts, KV_WIDTH),
            _rows(ts, POOL_WIDTH),
            _rows(ts, ATTN_WIDTH + KV_WIDTH),
            _rows(ts, D_MODEL),
            _rows(ts, D_MODEL),
            _resident((1, D_MODEL)),
            _resident((1, ATTN_WIDTH)),
            _resident((1, KV_WIDTH)),
            _resident((IN_WIDTH, D_MODEL)),
            _resident((ATTN_WIDTH, ATTN_WIDTH)),
            _resident((KV_WIDTH, KV_WIDTH)),
        ],
        out_specs=[
            _rows(ts, D_MODEL),
            _acc((IN_WIDTH, D_MODEL)),
            _acc((1, D_MODEL)),
            _acc((1, SMALL_LANES)),
            _acc((1, SMALL_LANES)),
        ],
        out_shape=[
            jax.ShapeDtypeStruct((s, D_MODEL), F32),
            jax.ShapeDtypeStruct((IN_WIDTH, D_MODEL), F32),
            jax.ShapeDtypeStruct((1, D_MODEL), F32),
            jax.ShapeDtypeStruct((1, SMALL_LANES), F32),
            jax.ShapeDtypeStruct((1, SMALL_LANES), F32),
        ],
        scratch_shapes=[
            pltpu.VMEM((ts, IN_WIDTH), BF16),
            pltpu.VMEM((1, ATTN_WIDTH), F32),
            pltpu.VMEM((1, KV_WIDTH), F32),
        ],
        rider=rider,
    )


BIG_WEIGHTS = (
    ("w_in", True, IN_WIDTH // N_DEV, D_MODEL),
    ("w_out", False, D_MODEL // N_DEV, D_MODEL),
    ("w_gate", True, D_FF // N_DEV, D_MODEL),
    ("w_up", True, D_FF // N_DEV, D_MODEL),
    ("w_down", False, D_FF // N_DEV, D_MODEL),
    ("w_ple_gate", False, D_MODEL // N_DEV, D_MODEL),
    ("w_ple_proj", False, PLE_DIM, D_MODEL // N_DEV),
)
N_BIG = len(BIG_WEIGHTS)


def _place():
    x, y, c = lax.axis_index("x"), lax.axis_index("y"), lax.axis_index("c")
    chips = [(1 - x, y), (x, 1 - y), (1 - x, 1 - y)]
    return x, y, c, chips


class _Gather:
    def __init__(self, n, rows=None):
        self.n = n
        self.rows = rows or [None] * n
        self.sems = [pltpu.SemaphoreType.DMA((n, 7)), pltpu.SemaphoreType.DMA((n, 7)), pltpu.SemaphoreType.DMA((n,))]

    def _ctx(self, srcs, outs, sems):
        send_sems, recv_sems, local_sems = sems
        x, y, c, chips = _place()
        me, sibling = (x, y, c), (x, y, 1 - c)

        def part(k, ref):
            return ref if self.rows[k] is None else ref.at[pl.ds(*self.rows[k]), :]

        def block(k, owner):
            px, py, pc = owner
            return part(k, outs[k].at[4 * px + 2 * py + pc])

        def copy(k, idx, owner, to, mine=False):
            return pltpu.make_async_remote_copy(
                src_ref=part(k, srcs[k]) if mine else block(k, owner), dst_ref=block(k, owner),
                send_sem=send_sems.at[k, idx], recv_sem=recv_sems.at[k, idx], device_id=to, device_id_type=MESH)

        def local(k):
            return pltpu.make_async_copy(part(k, srcs[k]), block(k, me), local_sems.at[k])

        return c, chips, me, sibling, copy, local

    def begin(self, srcs, outs, sems):
        c, chips, me, sibling, copy, local = self._ctx(srcs, outs, sems)
        for k in range(self.n):
            local(k).start()
            copy(k, 0, me, sibling, mine=True).start()
            for j, chip in enumerate(chips):
                copy(k, 1 + j, me, (*chip, c), mine=True).start()

    def middle(self, srcs, outs, sems):
        c, chips, me, sibling, copy, local = self._ctx(srcs, outs, sems)
        for j, chip in enumerate(chips):
            for k in range(self.n):
                copy(k, 1 + j, (*chip, c), me).wait_recv()
                copy(k, 4 + j, (*chip, c), sibling).start()

    def end(self, srcs, outs, sems):
        c, chips, me, sibling, copy, local = self._ctx(srcs, outs, sems)
        for k in range(self.n):
            copy(k, 0, sibling, me).wait_recv()
            for j, chip in enumerate(chips):
                copy(k, 4 + j, (*chip, 1 - c), me).wait_recv()
        for k in range(self.n):
            copy(k, 0, me, sibling, mine=True).wait_send()
            for j, chip in enumerate(chips):
                copy(k, 1 + j, me, (*chip, c), mine=True).wait_send()
                copy(k, 4 + j, (*chip, c), sibling).wait_send()
            local(k).wait()


def _gather_rider(items):
    items = [it if isinstance(it, tuple) else (it, None, None, None) for it in items]
    n = len(items)
    g = _Gather(n, [None if r0 is None else (r0, nr) for _, r0, nr, _ in items])
    shapes = [jax.ShapeDtypeStruct((N_DEV, *sh.shape), sh.dtype) for sh, _, _, _ in items]
    stacks = [(k, st) for k, (_, _, _, st) in enumerate(items) if st is not None]
    aliases = {n + i: k for i, (k, _) in enumerate(stacks)}
    return _Rider([sh for sh, _, _, _ in items] + [st for _, st in stacks], shapes, g.sems, g.begin, g.end, g.middle,
                  aliases=aliases)


def _cast_and_gather_first(shards):
    g = _Gather(1)
    any_spec = pl.BlockSpec(memory_space=pl.ANY)
    vmem = pl.BlockSpec(memory_space=pltpu.VMEM)

    def body(*refs):
        ins, outs, gathered, sems = refs[:N_BIG], refs[N_BIG:2 * N_BIG], refs[2 * N_BIG], refs[2 * N_BIG + 1:]
        outs[0][...] = ins[0][...].astype(BF16)
        g.begin(outs[:1], [gathered], sems)
        for k in range(1, N_BIG):
            outs[k][...] = ins[k][...].astype(BF16)
        g.middle(outs[:1], [gathered], sems)
        g.end(outs[:1], [gathered], sems)

    res = pl.pallas_call(
        body,
        name="cast_and_gather_first",
        in_specs=[vmem] * N_BIG,
        out_specs=[vmem] * N_BIG + [any_spec],
        out_shape=[jax.ShapeDtypeStruct((r, c), BF16) for _, _, r, c in BIG_WEIGHTS]
        + [jax.ShapeDtypeStruct((N_DEV, *BIG_WEIGHTS[0][2:]), BF16)],
        scratch_shapes=g.sems,
    )(*shards)
    return list(res[:N_BIG]), res[N_BIG]


def _sibling_rider(grads):
    n = len(grads)

    def copies(gs, lands, sems):
        send_sems, recv_sems = sems
        x, y, c, _ = _place()
        return [
            pltpu.make_async_remote_copy(
                src_ref=gs[k].at[:, 1 - c], dst_ref=lands[k], send_sem=send_sems.at[k], recv_sem=recv_sems.at[k],
                device_id=(x, y, 1 - c), device_id_type=MESH)
            for k in range(n)
        ]

    def begin(gs, lands, sems):
        for cp in copies(gs, lands, sems):
            cp.start()

    def end(gs, lands, sems):
        for cp in copies(gs, lands, sems):
            cp.wait()

    shapes = [jax.ShapeDtypeStruct((N_CHIPS, *g.shape[2:]), F32) for g in grads]
    return _Rider(grads, shapes, [pltpu.SemaphoreType.DMA((n,)), pltpu.SemaphoreType.DMA((n,))], begin, end)


def _chip_sum(k, place, grad, from_sibling):
    _, _, r, c = BIG_WEIGHTS[k]

    def body(place_ref, g_ref, l_ref, own_ref, send_ref):
        q = pl.program_id(0)
        tot = g_ref[0, 0] + l_ref[0]
        mine = q == 2 * place_ref[0] + place_ref[1]

        @pl.when(mine)
        def _():
            own_ref[...] = tot

        send_ref[0] = jnp.where(mine, 0.0, tot).astype(BF16)

    return pl.pallas_call(
        body,
        name=f"chip_sum_{BIG_WEIGHTS[k][0]}",
        grid_spec=pltpu.PrefetchScalarGridSpec(
            num_scalar_prefetch=1,
            grid=(N_CHIPS,),
            in_specs=[
                pl.BlockSpec((1, 1, r, c), lambda q, place: (q, place[2], 0, 0)),
                pl.BlockSpec((1, r, c), lambda q, place: (q, 0, 0)),
            ],
            out_specs=[
                pl.BlockSpec((r, c), lambda q, place: (0, 0)),
                pl.BlockSpec((1, r, c), lambda q, place: (q, 0, 0)),
            ],
        ),
        out_shape=[jax.ShapeDtypeStruct((r, c), F32), jax.ShapeDtypeStruct((N_CHIPS, r, c), BF16)],
    )(place, grad, from_sibling)


def _chips_rider(to_send, small=None):
    n = len(to_send)
    inputs = list(to_send) + ([] if small is None else [small])
    shapes = [jax.ShapeDtypeStruct((3, *t.shape[1:]), BF16) for t in to_send]
    sems = [pltpu.SemaphoreType.DMA((max(n, 1), 3)), pltpu.SemaphoreType.DMA((max(n, 1), 3))]
    if small is not None:
        shapes.append(jax.ShapeDtypeStruct((N_DEV, *small.shape), F32))
        sems += [pltpu.SemaphoreType.DMA((7,)), pltpu.SemaphoreType.DMA((7,)), pltpu.SemaphoreType.DMA]

    def copies(ins, outs, sem_refs):
        x, y, c, chips = _place()
        out = []
        for k in range(n):
            for j, (px, py) in enumerate(chips):
                out.append(pltpu.make_async_remote_copy(
                    src_ref=ins[k].at[2 * px + py], dst_ref=outs[k].at[j],
                    send_sem=sem_refs[0].at[k, j], recv_sem=sem_refs[1].at[k, j],
                    device_id=(px, py, c), device_id_type=MESH))
        local = None
        if small is not None:
            me = 4 * x + 2 * y + c
            local = pltpu.make_async_copy(ins[n], outs[n].at[me], sem_refs[4])
            rel = 0
            for fx in (0, 1):
                for fy in (0, 1):
                    for fc in (0, 1):
                        if (fx, fy, fc) != (0, 0, 0):
                            out.append(pltpu.make_async_remote_copy(
                                src_ref=ins[n], dst_ref=outs[n].at[me],
                                send_sem=sem_refs[2].at[rel], recv_sem=sem_refs[3].at[rel],
                                device_id=(x ^ fx, y ^ fy, c ^ fc), device_id_type=MESH))
                            rel += 1
        return out, local

    def begin(ins, outs, sem_refs):
        remote, local = copies(ins, outs, sem_refs)
        if local is not None:
            local.start()
        for cp in remote:
            cp.start()

    def end(ins, outs, sem_refs):
        remote, local = copies(ins, outs, sem_refs)
        for cp in remote:
            cp.wait()
        if local is not None:
            local.wait()

    return _Rider(inputs, shapes, sems, begin, end)


def _exchange(name, rider):
    return _call(lambda: None, (), name=name, grid=(1,), in_specs=[], out_specs=[], out_shape=[], rider=rider)[1]


def _merge_riders(*riders):
    riders = [r for r in riders if r is not None]
    if len(riders) == 1:
        return riders[0]
    assert not any(r.aliases for r in riders)

    def split(refs, counts):
        out, at = [], 0
        for n in counts:
            out.append(refs[at:at + n])
            at += n
        return out

    def run(which):
        def fn(ins, outs, sems):
            parts = zip(riders, split(ins, [len(r.inputs) for r in riders]),
                        split(outs, [len(r.out_shapes) for r in riders]), split(sems, [len(r.sems) for r in riders]))
            for r, i, o, s in parts:
                hook = getattr(r, which)
                if hook is not None:
                    hook(i, o, s)
        return fn

    middle = run("middle") if any(r.middle is not None for r in riders) else None
    return _Rider(sum((r.inputs for r in riders), []), sum((r.out_shapes for r in riders), []),
                  sum((r.sems for r in riders), []), run("begin"), run("end"), middle)


def _split_outputs(outs, *riders):
    res, at = [], 0
    for r in riders:
        res.append(outs[at:at + len(r.out_shapes)])
        at += len(r.out_shapes)
    return res


def _adamw(w, g, m, v):
    m = ADAM_B1 * m + (1.0 - ADAM_B1) * g
    v = ADAM_B2 * v + (1.0 - ADAM_B2) * jnp.square(g)
    m_hat = m / (1.0 - ADAM_B1 ** ADAM_STEP)
    v_hat = v / (1.0 - ADAM_B2 ** ADAM_STEP)
    delta = -ADAM_LR * (m_hat / (jnp.sqrt(v_hat) + ADAM_EPS) + ADAM_WD * w)
    return delta, m, v


def _adamw_big(k, own, landed, w, m, v, rider=None):
    name, _, r, c = BIG_WEIGHTS[k]
    tile = r // 2
    tiles = lambda i: (i, 0)

    def body(own_ref, land_ref, w_ref, m_ref, v_ref, g_ref, d_ref, nm_ref, nv_ref):
        g = ((own_ref[...] + land_ref[0].astype(F32)) + land_ref[1].astype(F32)) + land_ref[2].astype(F32)
        g_ref[...] = g
        d_ref[...], nm_ref[...], nv_ref[...] = _adamw(w_ref[...], g, m_ref[...], v_ref[...])

    return _call(
        body,
        (own, landed, w, m, v),
        name=f"adamw_{name}",
        grid=(r // tile,),
        in_specs=[pl.BlockSpec((tile, c), tiles), pl.BlockSpec((3, tile, c), lambda i: (0, i, 0))]
        + [pl.BlockSpec((tile, c), tiles)] * 3,
        out_specs=[pl.BlockSpec((tile, c), tiles)] * 4,
        out_shape=[jax.ShapeDtypeStruct((r, c), F32)] * 4,
        rider=rider,
    )


def _sum_small(parts_list):
    n = len(parts_list)

    def body(*refs):
        for p_ref, out_ref in zip(refs[:n], refs[n:]):
            tot = p_ref[0]
            for j in range(1, N_DEV):
                tot = tot + p_ref[j]
            out_ref[...] = tot

    return pl.pallas_call(body, name="sum_small",
                          out_shape=[jax.ShapeDtypeStruct(p.shape[1:], F32) for p in parts_list])(*parts_list)


def _adamw_small(grads, ws, ms, vs):
    n = len(grads)

    def body(*refs):
        g_refs, w_refs, m_refs, v_refs = refs[:n], refs[n:2 * n], refs[2 * n:3 * n], refs[3 * n:4 * n]
        outs = refs[4 * n:]
        for i in range(n):
            d, nm, nv = _adamw(w_refs[i][...], g_refs[i][...], m_refs[i][...], v_refs[i][...])
            outs[i][...] = d
            outs[n + i][...] = nm
            outs[2 * n + i][...] = nv

    shapes = [jax.ShapeDtypeStruct(w.shape, F32) for w in ws]
    return pl.pallas_call(body, name="adamw_small", out_shape=shapes * 3)(*grads, *ws, *ms, *vs)


SMALL_NAMES = ("g_attn_norm", "g_q", "g_k", "attn_sinks", "rel_bias", "w_pool", "pool_scale", "g_ffn_norm", "g_ple_norm")


def _pack_small(arrays):
    rows, offsets = [], []
    at = 0
    for a in arrays:
        flat = a.reshape(-1)
        n_rows = -(-flat.shape[0] // (8 * SMALL_LANES)) * 8
        flat = jnp.pad(flat, (0, n_rows * SMALL_LANES - flat.shape[0]))
        rows.append(flat.reshape(n_rows, SMALL_LANES))
        offsets.append(at)
        at += n_rows
    return jnp.concatenate(rows, axis=0), offsets


def kernel(x, p, w_in, w_out, g_attn_norm, g_q, g_k, attn_sinks, rel_bias, w_pool, pool_scale, g_ffn_norm, w_gate, w_up, w_down, g_ple_norm, w_ple_gate, w_ple_proj, loss_target, m_w_in, m_w_out, m_g_attn_norm, m_g_q, m_g_k, m_attn_sinks, m_rel_bias, m_w_pool, m_pool_scale, m_g_ffn_norm, m_w_gate, m_w_up, m_w_down, m_g_ple_norm, m_w_ple_gate, m_w_ple_proj, v_w_in, v_w_out, v_g_attn_norm, v_g_q, v_g_k, v_attn_sinks, v_rel_bias, v_w_pool, v_pool_scale, v_g_ffn_norm, v_w_gate, v_w_up, v_w_down, v_g_ple_norm, v_w_ple_gate, v_w_ple_proj):
    weights = dict(w_in=w_in, w_out=w_out, g_attn_norm=g_attn_norm, g_q=g_q, g_k=g_k, attn_sinks=attn_sinks,
                   rel_bias=rel_bias, w_pool=w_pool, pool_scale=pool_scale, g_ffn_norm=g_ffn_norm, w_gate=w_gate,
                   w_up=w_up, w_down=w_down, g_ple_norm=g_ple_norm, w_ple_gate=w_ple_gate, w_ple_proj=w_ple_proj)
    m_in = dict(w_in=m_w_in, w_out=m_w_out, g_attn_norm=m_g_attn_norm, g_q=m_g_q, g_k=m_g_k, attn_sinks=m_attn_sinks,
                rel_bias=m_rel_bias, w_pool=m_w_pool, pool_scale=m_pool_scale, g_ffn_norm=m_g_ffn_norm, w_gate=m_w_gate,
                w_up=m_w_up, w_down=m_w_down, g_ple_norm=m_g_ple_norm, w_ple_gate=m_w_ple_gate, w_ple_proj=m_w_ple_proj)
    v_in = dict(w_in=v_w_in, w_out=v_w_out, g_attn_norm=v_g_attn_norm, g_q=v_g_q, g_k=v_g_k, attn_sinks=v_attn_sinks,
                rel_bias=v_rel_bias, w_pool=v_w_pool, pool_scale=v_pool_scale, g_ffn_norm=v_g_ffn_norm, w_gate=v_w_gate,
                w_up=v_w_up, w_down=v_w_down, g_ple_norm=v_g_ple_norm, w_ple_gate=v_w_ple_gate, w_ple_proj=v_w_ple_proj)

    xs = x[0]
    ps = p[0, 0]
    target = loss_target[0]
    wp = w_pool[0]
    gq_t = jnp.tile(g_q, (1, ATTN_WIDTH // HEAD_DIM))
    gk_t = jnp.tile(g_k, (1, KV_WIDTH // HEAD_DIM))

    def to_blocks(k, arr):
        return jnp.swapaxes(arr[0], 0, 1) if BIG_WEIGHTS[k][1] else arr[0]

    def from_blocks(k, arr):
        return (jnp.swapaxes(arr, 0, 1) if BIG_WEIGHTS[k][1] else arr)[None]

    IN, OUT, GATE, UP, DOWN, PG, PP = range(N_BIG)
    full = lambda g: g.reshape(N_DEV * g.shape[1], g.shape[2])
    halves = lambda k, g: g.reshape(N_CHIPS, 2, *BIG_WEIGHTS[k][2:])
    place = jnp.stack([lax.axis_index("x"), lax.axis_index("y"), lax.axis_index("c")]).astype(jnp.int32)

    sh, w_in_g = _cast_and_gather_first([to_blocks(k, weights[name]) for k, (name, _, _, _) in enumerate(BIG_WEIGHTS)])
    w_in_t = full(w_in_g)

    ffn_rows = BIG_WEIGHTS[GATE][2]
    tab = _bias_table(rel_bias.T)
    (zqk, qn, kn, v, u), (w_out_g, wg_g) = _in_proj(
        xs, g_attn_norm, w_in_t, gq_t, gk_t, rider=_gather_rider([sh[OUT], (sh[GATE], 0, GATE_ROWS_EARLY, None)]))
    (a,), (wg_g, wu_g) = _attn_fwd(qn, kn, v, tab, attn_sinks, rider=_gather_rider([
        (sh[GATE], GATE_ROWS_EARLY, ffn_rows - GATE_ROWS_EARLY, wg_g), (sh[UP], 0, UP_ROWS_EARLY, None)]))
    w_out_f = full(w_out_g)
    (h1, hn2, m_out), (wu_g,) = _mix_out(u, a, xs, w_out_f, wp, pool_scale, g_ffn_norm, rider=_gather_rider([
        (sh[UP], UP_ROWS_EARLY, ffn_rows - UP_ROWS_EARLY, wu_g)]))
    wg_t, wu_t = full(wg_g), full(wu_g)
    (gt, up), (wd_g,) = _ffn_up(hn2, wg_t, wu_t, rider=_gather_rider([sh[DOWN]]))
    w_down_f = full(wd_g)
    (h2,), (w_pg_g, w_pp_g) = _ffn_down(gt, up, h1, w_down_f, rider=_gather_rider([sh[PG], sh[PP]]))

    sums, landed = [None] * N_BIG, [None] * N_BIG

    def chip_sum(k, grad, from_sibling):
        sums[k] = _chip_sum(k, place, halves(k, grad), from_sibling)

    (loss_part, dh2, dh2b, d_wpg, d_wpp, d_g_ple), _ = _ple_fwd_bwd(h2, ps, target, g_ple_norm, full(w_pg_g), w_pp_g)
    (act, dgt, dup, dh1, dh1b, d_g_ffn), sib = _ffn_bwd_act(
        dh2, h1, gt, up, g_ffn_norm, wg_t, wu_t, w_down_f, rider=_sibling_rider([halves(PG, d_wpg), halves(PP, d_wpp)]))
    chip_sum(PG, d_wpg, sib[0])
    chip_sum(PP, d_wpp, sib[1])
    (d_wd,), (landed[PG], landed[PP]) = _ffn_bwd_w("down", act, dh2b, rider=_chips_rider([sums[PG][1], sums[PP][1]]))
    (d_wo,), sib = _out_w_bwd(a, m_out, dh1b, rider=_sibling_rider([halves(DOWN, d_wd)]))
    chip_sum(DOWN, d_wd, sib[0])
    r_sib, r_chips = _sibling_rider([halves(OUT, d_wo)]), _chips_rider([sums[DOWN][1]])
    (d_wg_t,), outs = _ffn_bwd_w("gate", dgt, hn2, rider=_merge_riders(r_sib, r_chips))
    sib, (landed[DOWN],) = _split_outputs(outs, r_sib, r_chips)
    chip_sum(OUT, d_wo, sib[0])
    r_sib, r_chips = _sibling_rider([halves(GATE, d_wg_t)]), _chips_rider([sums[OUT][1]])
    (d_wu_t,), outs = _ffn_bwd_w("up", dup, hn2, rider=_merge_riders(r_sib, r_chips))
    sib, (landed[OUT],) = _split_outputs(outs, r_sib, r_chips)
    chip_sum(GATE, d_wg_t, sib[0])
    r_sib, r_chips = _sibling_rider([halves(UP, d_wu_t)]), _chips_rider([sums[GATE][1]])
    (da, du, d_wpool, d_scale), outs = _mix_bwd(dh1b, u, w_out_f, wp, pool_scale, rider=_merge_riders(r_sib, r_chips))
    sib, (landed[GATE],) = _split_outputs(outs, r_sib, r_chips)
    chip_sum(UP, d_wu_t, sib[0])
    early, early_at = _pack_small([d_wpool, d_scale, d_g_ffn, d_g_ple, loss_part[:, :1]])
    (dqn, dkn, dv, dl_acc, d_sinks), (landed[UP], early_all) = _attn_bwd(
        qn, kn, v, a, da, tab, attn_sinks, rider=_chips_rider([sums[UP][1]], early))
    (grad_x, d_win_t, d_g_attn, d_gq, d_gk), _ = _in_proj_bwd(dqn, dkn, dv, du, zqk, xs, dh1, g_attn_norm, gq_t, gk_t, w_in_t)
    (d_rel_t,), sib = _bias_table_bwd(dl_acc, rider=_sibling_rider([halves(IN, d_win_t)]))
    chip_sum(IN, d_win_t, sib[0])
    late, late_at = _pack_small([d_g_attn, d_gq[:, :HEAD_DIM], d_gk[:, :HEAD_DIM], d_sinks[:, 0], d_rel_t[:, :N_BUCKETS]])
    landed[IN], late_all = _exchange("last_exchange", _chips_rider([sums[IN][1]], late))

    out = {"grad": {}, "delta": {}, "new_m": {}, "new_v": {}}
    for k, (name, _, _, _) in enumerate(BIG_WEIGHTS):
        res, _ = _adamw_big(k, sums[k][0], landed[k], to_blocks(k, weights[name]), to_blocks(k, m_in[name]),
                            to_blocks(k, v_in[name]))
        for kind, r in zip(("grad", "delta", "new_m", "new_v"), res):
            out[kind][name] = from_blocks(k, r)
    early_sum, late_sum = _sum_small([early_all, late_all])

    def unpack(packed, at, shape):
        n = math.prod(shape)
        return packed[at:at + -(-n // SMALL_LANES)].reshape(-1)[:n].reshape(shape)

    small_grads = dict(
        w_pool=unpack(early_sum, early_at[0], w_pool.shape), pool_scale=unpack(early_sum, early_at[1], pool_scale.shape),
        g_ffn_norm=unpack(early_sum, early_at[2], g_ffn_norm.shape), g_ple_norm=unpack(early_sum, early_at[3], g_ple_norm.shape),
        g_attn_norm=unpack(late_sum, late_at[0], g_attn_norm.shape), g_q=unpack(late_sum, late_at[1], g_q.shape),
        g_k=unpack(late_sum, late_at[2], g_k.shape), attn_sinks=unpack(late_sum, late_at[3], attn_sinks.shape),
        rel_bias=unpack(late_sum, late_at[4], rel_bias.T.shape))
    loss = early_sum[early_at[4], 0]
    flip = lambda name, arr: arr.T if name == "rel_bias" else arr
    updates = _adamw_small([small_grads[n] for n in SMALL_NAMES], [flip(n, weights[n]) for n in SMALL_NAMES],
                           [flip(n, m_in[n]) for n in SMALL_NAMES], [flip(n, v_in[n]) for n in SMALL_NAMES])
    n_small = len(SMALL_NAMES)
    for i, name in enumerate(SMALL_NAMES):
        out["grad"][name] = flip(name, small_grads[name])
        out["delta"][name] = flip(name, updates[i])
        out["new_m"][name] = flip(name, updates[n_small + i])
        out["new_v"][name] = flip(name, updates[2 * n_small + i])

    order = ("w_in", "w_out", "g_attn_norm", "g_q", "g_k", "attn_sinks", "rel_bias", "w_pool", "pool_scale",
             "g_ffn_norm", "w_gate", "w_up", "w_down", "g_ple_norm", "w_ple_gate", "w_ple_proj")
    return (loss, grad_x[None], *[out["grad"][n] for n in order], *[out["delta"][n] for n in order],
            *[out["new_m"][n] for n in order], *[out["new_v"][n] for n in order])
```

```python
import functools
import math

import jax
import jax.numpy as jnp
import numpy as np
from jax import lax
from jax.experimental import pallas as pl
from jax.experimental.pallas import tpu as pltpu

F32 = jnp.float32
BF16 = jnp.bfloat16
MESH = pl.DeviceIdType.MESH

D_MODEL = 1024
HEAD_DIM = 64
ATTN_WIDTH = 512
KV_WIDTH = 128
POOL_WIDTH = 512
POOL_SIZES = (2, 4, 8, 16)
POOL_GROUP = 128
POOL_HALO = 16
IN_WIDTH = 1280
D_FF = 2816
PLE_DIM = 256
BLOCK = 128
N_BUCKETS = 32
MAX_DISTANCE = 128
EPS = 1e-6
N_DEV = 8
N_CHIPS = 4

ADAM_LR = 0.001
ADAM_B1 = 0.9
ADAM_B2 = 0.999
ADAM_EPS = 1e-08
ADAM_WD = 0.01
ADAM_STEP = 10

TOKEN_TILE = 512
FFN_BWD_TILE = 256
FF_CHUNK = 256
ATTN_STEP_BLOCKS = 2
GATE_ROWS_EARLY = 96
UP_ROWS_EARLY = 64
HEADS_A = (0, 2, 5, 7)
HEADS_B = (1, 3, 4, 6)
SMALL_LANES = 128


def _nn(a, b):
    return jnp.dot(a, b, preferred_element_type=F32)


def _nt(a, b):
    return lax.dot_general(a, b, (((1,), (1,)), ((), ())), preferred_element_type=F32)


def _tn(a, b):
    return lax.dot_general(a, b, (((0,), (0,)), ((), ())), preferred_element_type=F32)


def _resident(shape):
    nd = len(shape)
    return pl.BlockSpec(shape, lambda i, _nd=nd: (0,) * _nd, pipeline_mode=pl.Buffered(1))


def _rows(tile, width):
    return pl.BlockSpec((tile, width), lambda i: (i, 0))


def _acc(shape):
    nd = len(shape)
    return pl.BlockSpec(shape, lambda i, _nd=nd: (0,) * _nd)


def _head_mean_matrix(width):
    idx = np.arange(width) // HEAD_DIM
    return jnp.asarray((idx[:, None] == idx[None, :]).astype(np.float32) / HEAD_DIM, dtype=BF16)


def _seg_mean(v, bmat):
    hi = v.astype(BF16)
    lo = (v - hi.astype(F32)).astype(BF16)
    return _nn(hi, bmat) + _nn(lo, bmat)


def _rms(x):
    return lax.rsqrt(jnp.mean(x * x, axis=-1, keepdims=True) + EPS)


def _rms_bwd(d_y, x, r, g):
    gy = d_y * g
    d_x = r * gy - x * (r * r * r) * jnp.mean(gy * x, axis=-1, keepdims=True)
    d_g = jnp.sum(d_y * (x * r), axis=0, keepdims=True)
    return d_x, d_g


def _lane_lo(shape):
    return lax.broadcasted_iota(jnp.int32, shape, 1) < HEAD_DIM


class _Rider:
    def __init__(self, inputs, out_shapes, sems, begin, end, middle=None, aliases=None):
        self.inputs, self.out_shapes, self.sems = list(inputs), list(out_shapes), list(sems)
        self.begin, self.middle, self.end = begin, middle, end
        self.aliases = dict(aliases or {})


def _call(body, args, *, name, grid, in_specs, out_specs, out_shape, scratch_shapes=(), rider=None):
    in_specs, out_specs, out_shape, scratch_shapes = list(in_specs), list(out_specs), list(out_shape), list(scratch_shapes)
    if rider is None:
        outs = pl.pallas_call(body, name=name, grid=grid, in_specs=in_specs, out_specs=out_specs, out_shape=out_shape,
                              scratch_shapes=scratch_shapes)(*args)
        return list(outs), []
    n_in, n_out, n_scr = len(in_specs), len(out_shape), len(scratch_shapes)
    r_in, r_out = len(rider.inputs), len(rider.out_shapes)
    n_steps = grid[0]

    def hosted(*refs):
        ins, refs = refs[:n_in], refs[n_in:]
        r_ins, refs = refs[:r_in], refs[r_in:]
        outs, refs = refs[:n_out], refs[n_out:]
        r_outs, refs = refs[:r_out], refs[r_out:]
        scratch, r_sems = refs[:n_scr], refs[n_scr:]
        step = pl.program_id(0)

        @pl.when(step == 0)
        def _():
            rider.begin(r_ins, r_outs, r_sems)

        if rider.middle is not None:
            @pl.when(step == n_steps - 1)
            def _():
                rider.middle(r_ins, r_outs, r_sems)

        body(*ins, *outs, *scratch)

        @pl.when(step == n_steps - 1)
        def _():
            rider.end(r_ins, r_outs, r_sems)

    any_spec = pl.BlockSpec(memory_space=pl.ANY)
    outs = pl.pallas_call(
        hosted, name=name, grid=grid,
        in_specs=in_specs + [any_spec] * r_in,
        out_specs=out_specs + [any_spec] * r_out,
        out_shape=out_shape + rider.out_shapes,
        scratch_shapes=scratch_shapes + rider.sems,
        input_output_aliases={n_in + i: n_out + o for i, o in rider.aliases.items()},
    )(*args, *rider.inputs)
    return list(outs[:n_out]), list(outs[n_out:])


def _in_proj(x, g_attn, w_in_t, gq_t, gk_t, rider=None):
    s = x.shape[0]
    ts = min(TOKEN_TILE, s)

    def body(x_ref, g_ref, w_ref, gq_ref, gk_ref, bq_ref, bk_ref, zqk_ref, qn_ref, kn_ref, v_ref, u_ref):
        xf = x_ref[...]
        hn = ((xf * _rms(xf)) * g_ref[...]).astype(BF16)
        z = _nt(hn, w_ref[...])
        q = z[:, :ATTN_WIDTH]
        k = z[:, ATTN_WIDTH:ATTN_WIDTH + KV_WIDTH]
        zqk_ref[...] = z[:, :ATTN_WIDTH + KV_WIDTH]
        rq = lax.rsqrt(_seg_mean(q * q, bq_ref[...]) + EPS)
        qn_ref[...] = ((q * rq) * gq_ref[...]).astype(BF16)
        rk = lax.rsqrt(_seg_mean(k * k, bk_ref[...]) + EPS)
        kn_ref[...] = ((k * rk) * gk_ref[...]).astype(BF16)
        v_ref[...] = z[:, ATTN_WIDTH + KV_WIDTH:ATTN_WIDTH + 2 * KV_WIDTH].astype(BF16)
        u_ref[...] = z[:, ATTN_WIDTH + 2 * KV_WIDTH:]

    return _call(
        body,
        (x, g_attn, w_in_t, gq_t, gk_t, _head_mean_matrix(ATTN_WIDTH), _head_mean_matrix(KV_WIDTH)),
        name="in_proj",
        grid=(s // ts,),
        in_specs=[
            _rows(ts, D_MODEL),
            _resident((1, D_MODEL)),
            _resident((IN_WIDTH, D_MODEL)),
            _resident((1, ATTN_WIDTH)),
            _resident((1, KV_WIDTH)),
            _resident((ATTN_WIDTH, ATTN_WIDTH)),
            _resident((KV_WIDTH, KV_WIDTH)),
        ],
        out_specs=[
            _rows(ts, ATTN_WIDTH + KV_WIDTH),
            _rows(ts, ATTN_WIDTH),
            _rows(ts, KV_WIDTH),
            _rows(ts, KV_WIDTH),
            _rows(ts, POOL_WIDTH),
        ],
        out_shape=[
            jax.ShapeDtypeStruct((s, ATTN_WIDTH + KV_WIDTH), F32),
            jax.ShapeDtypeStruct((s, ATTN_WIDTH), BF16),
            jax.ShapeDtypeStruct((s, KV_WIDTH), BF16),
            jax.ShapeDtypeStruct((s, KV_WIDTH), BF16),
            jax.ShapeDtypeStruct((s, POOL_WIDTH), F32),
        ],
        rider=rider,
    )


def _bucket_ranges():
    n = np.arange(MAX_DISTANCE)
    max_exact = N_BUCKETS // 2
    nf = np.maximum(n, 1).astype(np.float64)
    large = max_exact + (np.log(nf / max_exact) / math.log(MAX_DISTANCE / max_exact) * (N_BUCKETS - max_exact)).astype(np.int64)
    bucket = np.where(n < max_exact, n, np.minimum(large, N_BUCKETS - 1))
    out = []
    for b in range(N_BUCKETS):
        idx = np.nonzero(bucket == b)[0]
        out.append((int(idx.min()), int(idx.max()) + 1))
    return out


def _band_distance():
    i = lax.broadcasted_iota(jnp.int32, (BLOCK, 2 * BLOCK), 0)
    j = lax.broadcasted_iota(jnp.int32, (BLOCK, 2 * BLOCK), 1)
    return BLOCK + i - j


def _bias_table(rel_bias_t):
    ranges = _bucket_ranges()

    def body(rb_ref, tab_ref):
        d = _band_distance()
        for half, heads in enumerate((HEADS_A, HEADS_B)):
            for slot, h in enumerate(heads):
                t = jnp.full((BLOCK, 2 * BLOCK), -jnp.inf, F32)
                for b, (lo, hi) in enumerate(ranges):
                    t = jnp.where((d >= lo) & (d < hi), rb_ref[h, b], t)
                tab_ref[half, slot * BLOCK:(slot + 1) * BLOCK, :] = t

    return pl.pallas_call(
        body,
        name="bias_table",
        in_specs=[pl.BlockSpec(memory_space=pltpu.SMEM)],
        out_shape=jax.ShapeDtypeStruct((2, 4 * BLOCK, 2 * BLOCK), F32),
    )(rel_bias_t)


def _bias_table_bwd(dl_acc, rider=None):
    ranges = _bucket_ranges()
    n_heads = len(HEADS_A) + len(HEADS_B)

    def body(dl_ref, out_ref):
        d = _band_distance()
        row = lax.broadcasted_iota(jnp.int32, (n_heads, SMALL_LANES), 0)
        lane = lax.broadcasted_iota(jnp.int32, (n_heads, SMALL_LANES), 1)
        out = jnp.zeros((n_heads, SMALL_LANES), F32)
        for b, (lo, hi) in enumerate(ranges):
            in_bucket = (d >= lo) & (d < hi)
            for half, heads in enumerate((HEADS_A, HEADS_B)):
                for slot, h in enumerate(heads):
                    g = dl_ref[half, slot * BLOCK:(slot + 1) * BLOCK, :]
                    part = jnp.sum(jnp.where(in_bucket, g, 0.0), axis=0, keepdims=True)
                    tot = jnp.sum(part, axis=1, keepdims=True)
                    out = jnp.where((row == h) & (lane == b), tot, out)
        out_ref[...] = out

    return _call(
        body,
        (dl_acc,),
        name="bias_table_bwd",
        grid=(1,),
        in_specs=[_acc((2, 4 * BLOCK, 2 * BLOCK))],
        out_specs=[_acc((n_heads, SMALL_LANES))],
        out_shape=[jax.ShapeDtypeStruct((n_heads, SMALL_LANES), F32)],
        rider=rider,
    )


def _stack_heads(pairs, lo_mask):
    zero = jnp.zeros_like(pairs[0])
    lo = [jnp.where(lo_mask, t, zero) for t in pairs]
    hi = [jnp.where(lo_mask, zero, t) for t in pairs]
    return (jnp.concatenate([lo[0], lo[1], hi[2], hi[3]], axis=0),
            jnp.concatenate([hi[0], hi[1], lo[2], lo[3]], axis=0))


def _unstack_heads(out_a, out_b, lo_mask):
    t = lambda x, r: x[r * BLOCK:(r + 1) * BLOCK, :]
    return [
        jnp.where(lo_mask, t(out_a, 0), t(out_b, 0)),
        jnp.where(lo_mask, t(out_a, 1), t(out_b, 1)),
        jnp.where(lo_mask, t(out_b, 2), t(out_a, 2)),
        jnp.where(lo_mask, t(out_b, 3), t(out_a, 3)),
    ]


def _sink_column(sink_ref, heads):
    row = lax.broadcasted_iota(jnp.int32, (4 * BLOCK, 1), 0)
    col = jnp.full((4 * BLOCK, 1), sink_ref[0, heads[3]], F32)
    for slot in (2, 1, 0):
        col = jnp.where(row < (slot + 1) * BLOCK, sink_ref[0, heads[slot]], col)
    return col


def _band_probs(q_stack, keys, tab, sink, first_block):
    s = _nt(q_stack, keys) * (HEAD_DIM ** -0.5) + tab
    if first_block is not None:
        col = lax.broadcasted_iota(jnp.int32, s.shape, 1)
        s = jnp.where(jnp.logical_and(first_block, col < BLOCK), -jnp.inf, s)
    m = jnp.maximum(jnp.max(s, axis=-1, keepdims=True), sink)
    e = jnp.exp(s - m)
    e_sink = jnp.exp(sink - m)
    den = jnp.sum(e, axis=-1, keepdims=True) + e_sink
    return e / den, e_sink / den


def _attn_specs(n_groups):
    group = lambda n: (jnp.minimum(n, n_groups - 1), 0)
    prev = lambda n: (jnp.maximum(jnp.minimum(n, n_groups - 1) * ATTN_STEP_BLOCKS - 1, 0), 0)
    return group, prev


def _band(prev_ref, group_ref, b):
    rows = lambda i: group_ref[i * BLOCK:(i + 1) * BLOCK, :]
    band = jnp.concatenate([prev_ref[...] if b == 0 else rows(b - 1), rows(b)], axis=0)
    return band, pltpu.roll(band, HEAD_DIM, 1)


def _attn_fwd(qn, kn, v, tab, sinks, rider=None):
    s = qn.shape[0]
    n_groups = s // (ATTN_STEP_BLOCKS * BLOCK)
    group, prev = _attn_specs(n_groups)
    rows = ATTN_STEP_BLOCKS * BLOCK

    def body(sink_ref, q_ref, kc_ref, kp_ref, vc_ref, vp_ref, tab_ref, o_ref):
        first = pl.program_id(0) == 0
        lo_mask = _lane_lo((BLOCK, BLOCK))
        for b in range(ATTN_STEP_BLOCKS):
            at = slice(b * BLOCK, (b + 1) * BLOCK)
            kk, kk_sw = _band(kp_ref, kc_ref, b)
            vv, vv_sw = _band(vp_ref, vc_ref, b)
            q_a, q_b = _stack_heads([q_ref[at, p * BLOCK:(p + 1) * BLOCK] for p in range(4)], lo_mask)
            no_prev = first if b == 0 else None
            p_a, _ = _band_probs(q_a, kk, tab_ref[0], _sink_column(sink_ref, HEADS_A), no_prev)
            p_b, _ = _band_probs(q_b, kk_sw, tab_ref[1], _sink_column(sink_ref, HEADS_B), no_prev)
            out = _unstack_heads(_nn(p_a.astype(BF16), vv), _nn(p_b.astype(BF16), vv_sw), lo_mask)
            for p in range(4):
                o_ref[at, p * BLOCK:(p + 1) * BLOCK] = out[p].astype(BF16)

    return _call(
        body,
        (sinks, qn, kn, kn, v, v, tab),
        name="attn_fwd",
        grid=(n_groups,),
        in_specs=[
            pl.BlockSpec(memory_space=pltpu.SMEM),
            pl.BlockSpec((rows, ATTN_WIDTH), group),
            pl.BlockSpec((rows, KV_WIDTH), group),
            pl.BlockSpec((BLOCK, KV_WIDTH), prev),
            pl.BlockSpec((rows, KV_WIDTH), group),
            pl.BlockSpec((BLOCK, KV_WIDTH), prev),
            _resident((2, 4 * BLOCK, 2 * BLOCK)),
        ],
        out_specs=[pl.BlockSpec((rows, ATTN_WIDTH), group)],
        out_shape=[jax.ShapeDtypeStruct((s, ATTN_WIDTH), BF16)],
        rider=rider,
    )


def _pooled(u_tile, u_halo, tile_index, tile_rows):
    halo = jnp.where(tile_index > 0, u_halo, 0.0)
    ext = jnp.concatenate([halo, u_tile], axis=0)
    sums = []
    acc = ext
    for shift in (1, 2, 4, 8):
        acc = acc + pltpu.roll(acc, shift, 0)
        sums.append(acc)
    t = tile_index * tile_rows + lax.broadcasted_iota(jnp.int32, (tile_rows, 1), 0)
    out = []
    for g, w in enumerate(POOL_SIZES):
        lanes = slice(g * POOL_GROUP, (g + 1) * POOL_GROUP)
        cnt = jnp.minimum(t + 1, w).astype(F32)
        out.append(sums[g][POOL_HALO:, lanes] / cnt - u_tile[:, lanes])
    return out


def _halo_before(tile):
    return lambda i: (jnp.maximum(i * (tile // POOL_HALO) - 1, 0), 0)


def _mix_out(u, a, x, w_out, w_pool, pool_scale, g_ffn, rider=None):
    s = x.shape[0]
    ts = min(TOKEN_TILE, s)

    def body(u_ref, uh_ref, a_ref, x_ref, wo_ref, wp_ref, sc_ref, g_ref, h1_ref, hn_ref, m_ref):
        i = pl.program_id(0)
        pooled = _pooled(u_ref[...], uh_ref[...], i, ts)
        for g in range(len(POOL_SIZES)):
            lanes = slice(g * POOL_GROUP, (g + 1) * POOL_GROUP)
            y = _nn(pooled[g].astype(BF16), wp_ref[g].astype(BF16))
            m_ref[:, lanes] = (y * sc_ref[:, lanes]).astype(BF16)
        h1 = x_ref[...] + _nn(a_ref[...], wo_ref[:ATTN_WIDTH, :]) + _nn(m_ref[...], wo_ref[ATTN_WIDTH:, :])
        h1_ref[...] = h1
        hn_ref[...] = ((h1 * _rms(h1)) * g_ref[...]).astype(BF16)

    return _call(
        body,
        (u, u, a, x, w_out, w_pool, pool_scale, g_ffn),
        name="mix_out",
        grid=(s // ts,),
        in_specs=[
            _rows(ts, POOL_WIDTH),
            pl.BlockSpec((POOL_HALO, POOL_WIDTH), _halo_before(ts)),
            _rows(ts, ATTN_WIDTH),
            _rows(ts, D_MODEL),
            _resident((D_MODEL, D_MODEL)),
            _resident((len(POOL_SIZES), POOL_GROUP, POOL_GROUP)),
            _resident((1, POOL_WIDTH)),
            _resident((1, D_MODEL)),
        ],
        out_specs=[_rows(ts, D_MODEL), _rows(ts, D_MODEL), _rows(ts, POOL_WIDTH)],
        out_shape=[
            jax.ShapeDtypeStruct((s, D_MODEL), F32),
            jax.ShapeDtypeStruct((s, D_MODEL), BF16),
            jax.ShapeDtypeStruct((s, POOL_WIDTH), BF16),
        ],
        rider=rider,
    )


def _ffn_up(hn2, wg_t, wu_t, rider=None):
    s = hn2.shape[0]
    ts = min(TOKEN_TILE, s)

    def body(hn_ref, wg_ref, wu_ref, gt_ref, up_ref):
        hn = hn_ref[...]
        for c in range(D_FF // FF_CHUNK):
            cols = slice(c * FF_CHUNK, (c + 1) * FF_CHUNK)
            gt_ref[:, cols] = _nt(hn, wg_ref[cols, :]).astype(BF16)
            up_ref[:, cols] = _nt(hn, wu_ref[cols, :]).astype(BF16)

    return _call(
        body,
        (hn2, wg_t, wu_t),
        name="ffn_up",
        grid=(s // ts,),
        in_specs=[_rows(ts, D_MODEL), _resident((D_FF, D_MODEL)), _resident((D_FF, D_MODEL))],
        out_specs=[_rows(ts, D_FF), _rows(ts, D_FF)],
        out_shape=[jax.ShapeDtypeStruct((s, D_FF), BF16), jax.ShapeDtypeStruct((s, D_FF), BF16)],
        rider=rider,
    )


def _silu_mul(gt, up):
    return (gt * jax.nn.sigmoid(gt)) * up


def _ffn_down_ple(gt, up, h1, w_down, p, target, g_ple, w_pg, w_pp, rider=None):
    s = h1.shape[0]
    ts = min(TOKEN_TILE, s)
    blk = D_MODEL // N_DEV

    def body(gt_ref, up_ref, h1_ref, wd_ref, p_ref, t_ref, g_ref, wpg_ref, wpp_ref,
             loss_ref, dh_ref, dhb_ref, dwpg_ref, dwpp_ref, dg_ref, act_ref, pp_ref):
        @pl.when(pl.program_id(0) == 0)
        def _():
            loss_ref[...] = jnp.zeros_like(loss_ref)
            dwpg_ref[...] = jnp.zeros_like(dwpg_ref)
            dwpp_ref[...] = jnp.zeros_like(dwpp_ref)
            dg_ref[...] = jnp.zeros_like(dg_ref)

        for c in range(D_FF // FF_CHUNK):
            cols = slice(c * FF_CHUNK, (c + 1) * FF_CHUNK)
            act_ref[:, cols] = _silu_mul(gt_ref[:, cols].astype(F32), up_ref[:, cols].astype(F32)).astype(BF16)
        h2v = h1_ref[...] + _nn(act_ref[...], wd_ref[...])
        r = _rms(h2v)
        hn = ((h2v * r) * g_ref[...]).astype(BF16)
        gate = jax.nn.sigmoid(_nn(hn, wpg_ref[...]))
        pb = p_ref[...].astype(BF16)
        for j in range(N_DEV):
            pp_ref[:, j * blk:(j + 1) * blk] = _nn(pb, wpp_ref[j])
        pp = pp_ref[...]
        diff = (h2v + gate * pp) - t_ref[...]
        loss_ref[...] += jnp.sum(jnp.sum(diff * diff, axis=0, keepdims=True), axis=1, keepdims=True) * (0.5 / D_MODEL)
        dy = diff * (1.0 / D_MODEL)
        d_pp = (dy * gate).astype(BF16)
        d_pre = ((dy * pp) * (gate * (1.0 - gate))).astype(BF16)
        for j in range(N_DEV):
            dwpp_ref[j] += _tn(pb, d_pp[:, j * blk:(j + 1) * blk])
        dwpg_ref[...] += _tn(hn, d_pre)
        d_x, d_g = _rms_bwd(_nt(d_pre, wpg_ref[...]), h2v, r, g_ref[...])
        dg_ref[...] += d_g
        dh = dy + d_x
        dh_ref[...] = dh
        dhb_ref[...] = dh.astype(BF16)

    return _call(
        body,
        (gt, up, h1, w_down, p, target, g_ple, w_pg, w_pp),
        name="ffn_down_ple",
        grid=(s // ts,),
        in_specs=[
            _rows(ts, D_FF),
            _rows(ts, D_FF),
            _rows(ts, D_MODEL),
            _resident((D_FF, D_MODEL)),
            _rows(ts, PLE_DIM),
            _rows(ts, D_MODEL),
            _resident((1, D_MODEL)),
            _resident((D_MODEL, D_MODEL)),
            _resident((N_DEV, PLE_DIM, blk)),
        ],
        out_specs=[
            _acc((1, SMALL_LANES)),
            _rows(ts, D_MODEL),
            _rows(ts, D_MODEL),
            _acc((D_MODEL, D_MODEL)),
            _acc((N_DEV, PLE_DIM, blk)),
            _acc((1, D_MODEL)),
        ],
        out_shape=[
            jax.ShapeDtypeStruct((1, SMALL_LANES), F32),
            jax.ShapeDtypeStruct((s, D_MODEL), F32),
            jax.ShapeDtypeStruct((s, D_MODEL), BF16),
            jax.ShapeDtypeStruct((D_MODEL, D_MODEL), F32),
            jax.ShapeDtypeStruct((N_DEV, PLE_DIM, blk), F32),
            jax.ShapeDtypeStruct((1, D_MODEL), F32),
        ],
        scratch_shapes=[pltpu.VMEM((ts, D_FF), BF16), pltpu.VMEM((ts, D_MODEL), F32)],
        rider=rider,
    )


def _ffn_bwd_act(dh2, h1, gt, up, g_ffn, wg_t, wu_t, w_down, rider=None):
    s = h1.shape[0]
    ts = min(FFN_BWD_TILE, s)

    def body(dh_ref, h1_ref, gt_ref, up_ref, g_ref, wg_ref, wu_ref, wd_ref,
             act_ref, dgt_ref, dup_ref, dh1_ref, dh1b_ref, dg_ref):
        @pl.when(pl.program_id(0) == 0)
        def _():
            dg_ref[...] = jnp.zeros_like(dg_ref)

        dhb = dh_ref[...].astype(BF16)
        for c in range(D_FF // FF_CHUNK):
            cols = slice(c * FF_CHUNK, (c + 1) * FF_CHUNK)
            d_act = _nt(dhb, wd_ref[cols, :])
            gtv = gt_ref[:, cols].astype(F32)
            upv = up_ref[:, cols].astype(F32)
            sg = jax.nn.sigmoid(gtv)
            silu = gtv * sg
            act_ref[:, cols] = (silu * upv).astype(BF16)
            dup_ref[:, cols] = (d_act * silu).astype(BF16)
            dgt_ref[:, cols] = ((d_act * upv) * (sg * (1.0 + gtv * (1.0 - sg)))).astype(BF16)
        d_hn = _nn(dgt_ref[...], wg_ref[...]) + _nn(dup_ref[...], wu_ref[...])
        h1v = h1_ref[...]
        d_x, d_g = _rms_bwd(d_hn, h1v, _rms(h1v), g_ref[...])
        dg_ref[...] += d_g
        dh1 = dh_ref[...] + d_x
        dh1_ref[...] = dh1
        dh1b_ref[...] = dh1.astype(BF16)

    return _call(
        body,
        (dh2, h1, gt, up, g_ffn, wg_t, wu_t, w_down),
        name="ffn_bwd_act",
        grid=(s // ts,),
        in_specs=[
            _rows(ts, D_MODEL),
            _rows(ts, D_MODEL),
            _rows(ts, D_FF),
            _rows(ts, D_FF),
            _resident((1, D_MODEL)),
            _resident((D_FF, D_MODEL)),
            _resident((D_FF, D_MODEL)),
            _resident((D_FF, D_MODEL)),
        ],
        out_specs=[
            _rows(ts, D_FF), _rows(ts, D_FF), _rows(ts, D_FF),
            _rows(ts, D_MODEL), _rows(ts, D_MODEL), _acc((1, D_MODEL)),
        ],
        out_shape=[
            jax.ShapeDtypeStruct((s, D_FF), BF16),
            jax.ShapeDtypeStruct((s, D_FF), BF16),
            jax.ShapeDtypeStruct((s, D_FF), BF16),
            jax.ShapeDtypeStruct((s, D_MODEL), F32),
            jax.ShapeDtypeStruct((s, D_MODEL), BF16),
            jax.ShapeDtypeStruct((1, D_MODEL), F32),
        ],
        rider=rider,
    )


def _ffn_bwd_w(which, lhs, rhs, rider=None):
    s = rhs.shape[0]

    def body(lhs_ref, rhs_ref, dw_ref):
        dw_ref[...] = _tn(lhs_ref[...], rhs_ref[...])

    return _call(
        body,
        (lhs, rhs),
        name=f"ffn_bwd_{which}",
        grid=(D_FF // FF_CHUNK,),
        in_specs=[pl.BlockSpec((s, FF_CHUNK), lambda i: (0, i)), _resident((s, D_MODEL))],
        out_specs=[_rows(FF_CHUNK, D_MODEL)],
        out_shape=[jax.ShapeDtypeStruct((D_FF, D_MODEL), F32)],
        rider=rider,
    )


def _mix_bwd(dh1b, u, w_out, w_pool, pool_scale, rider=None):
    s = u.shape[0]
    ts = min(TOKEN_TILE, s)
    nt = s // ts
    halo_after = lambda i: (jnp.minimum((i + 1) * (ts // POOL_HALO), s // POOL_HALO - 1), 0)
    n_groups = len(POOL_SIZES)

    def body(dh_ref, dhn_ref, u_ref, uh_ref, wo_ref, wp_ref, sc_ref, da_ref, du_ref, dwp_ref, dsc_ref):
        i = pl.program_id(0)

        @pl.when(i == 0)
        def _():
            dwp_ref[...] = jnp.zeros_like(dwp_ref)
            dsc_ref[...] = jnp.zeros_like(dsc_ref)

        dh = dh_ref[...]
        da_ref[...] = _nt(dh, wo_ref[:ATTN_WIDTH, :])
        dh_next = jnp.where(i < nt - 1, dhn_ref[...], jnp.zeros_like(dhn_ref))
        dm_ext = _nt(jnp.concatenate([dh, dh_next], axis=0), wo_ref[ATTN_WIDTH:, :])
        pooled = _pooled(u_ref[...], uh_ref[...], i, ts)
        t_ext = i * ts + lax.broadcasted_iota(jnp.int32, (ts + POOL_HALO, 1), 0)
        for g, w in enumerate(POOL_SIZES):
            lanes = slice(g * POOL_GROUP, (g + 1) * POOL_GROUP)
            wp = wp_ref[g].astype(BF16)
            pg = pooled[g].astype(BF16)
            dm_g = dm_ext[:, lanes]
            dsc_ref[:, lanes] += jnp.sum(dm_g[:ts, :] * _nn(pg, wp), axis=0, keepdims=True)
            dy = (dm_g * sc_ref[:, lanes]).astype(BF16)
            dwp_ref[g] += _tn(pg, dy[:ts, :])
            d_pool = _nt(dy, wp)
            acc = d_pool / jnp.minimum(t_ext + 1, w).astype(F32)
            shift = 1
            while shift < w:
                acc = acc + pltpu.roll(acc, ts + POOL_HALO - shift, 0)
                shift *= 2
            du_ref[:, lanes] = acc[:ts, :] - d_pool[:ts, :]

    return _call(
        body,
        (dh1b, dh1b, u, u, w_out, w_pool, pool_scale),
        name="mix_bwd",
        grid=(nt,),
        in_specs=[
            _rows(ts, D_MODEL),
            pl.BlockSpec((POOL_HALO, D_MODEL), halo_after),
            _rows(ts, POOL_WIDTH),
            pl.BlockSpec((POOL_HALO, POOL_WIDTH), _halo_before(ts)),
            _resident((D_MODEL, D_MODEL)),
            _resident((n_groups, POOL_GROUP, POOL_GROUP)),
            _resident((1, POOL_WIDTH)),
        ],
        out_specs=[
            _rows(ts, ATTN_WIDTH),
            _rows(ts, POOL_WIDTH),
            _acc((n_groups, POOL_GROUP, POOL_GROUP)),
            _acc((1, POOL_WIDTH)),
        ],
        out_shape=[
            jax.ShapeDtypeStruct((s, ATTN_WIDTH), F32),
            jax.ShapeDtypeStruct((s, POOL_WIDTH), F32),
            jax.ShapeDtypeStruct((n_groups, POOL_GROUP, POOL_GROUP), F32),
            jax.ShapeDtypeStruct((1, POOL_WIDTH), F32),
        ],
        rider=rider,
    )


def _out_w_bwd(a, m, dh1b, rider=None):
    s = dh1b.shape[0]

    def body(a_ref, m_ref, dh_ref, dw_ref):
        @pl.when(pl.program_id(0) == 0)
        def _():
            dw_ref[...] = _tn(a_ref[...], dh_ref[...])

        @pl.when(pl.program_id(0) == 1)
        def _():
            dw_ref[...] = _tn(m_ref[...], dh_ref[...])

    return _call(
        body,
        (a, m, dh1b),
        name="out_w_bwd",
        grid=(2,),
        in_specs=[_resident((s, ATTN_WIDTH)), _resident((s, POOL_WIDTH)), _resident((s, D_MODEL))],
        out_specs=[_rows(ATTN_WIDTH, D_MODEL)],
        out_shape=[jax.ShapeDtypeStruct((D_MODEL, D_MODEL), F32)],
        rider=rider,
    )


def _attn_bwd(qn, kn, v, a, da, tab, sinks, rider=None):
    s = qn.shape[0]
    qb = ATTN_STEP_BLOCKS
    rows = qb * BLOCK
    n_groups = s // rows
    group, prev = _attn_specs(n_groups)
    done = lambda n: (jnp.maximum(n - 1, 0), 0)

    def body(sink_ref, q_ref, kc_ref, kp_ref, vc_ref, vp_ref, o_ref, do_ref, tab_ref,
             dq_ref, dk_ref, dv_ref, dl_ref, ds_ref, k_carry, v_carry, sink_acc):
        n = pl.program_id(0)

        @pl.when(n == 0)
        def _():
            dl_ref[...] = jnp.zeros_like(dl_ref)
            k_carry[...] = jnp.zeros_like(k_carry)
            v_carry[...] = jnp.zeros_like(v_carry)
            sink_acc[...] = jnp.zeros_like(sink_acc)

        @pl.when(n < n_groups)
        def _():
            first = n == 0
            lo_mask = _lane_lo((BLOCK, BLOCK))
            dks, dvs = [], []
            for b in range(qb):
                at = slice(b * BLOCK, (b + 1) * BLOCK)
                keys = _band(kp_ref, kc_ref, b)
                vals = _band(vp_ref, vc_ref, b)
                q_st = _stack_heads([q_ref[at, p * BLOCK:(p + 1) * BLOCK] for p in range(4)], lo_mask)
                do_st = _stack_heads([do_ref[at, p * BLOCK:(p + 1) * BLOCK] for p in range(4)], lo_mask)
                o_st = _stack_heads([o_ref[at, p * BLOCK:(p + 1) * BLOCK].astype(F32) for p in range(4)], lo_mask)
                dq_st, dk_parts, dv_parts = [], [], []
                for half, heads in enumerate((HEADS_A, HEADS_B)):
                    probs, p_sink = _band_probs(q_st[half], keys[half], tab_ref[half], _sink_column(sink_ref, heads),
                                                first if b == 0 else None)
                    delta = jnp.sum(do_st[half] * o_st[half], axis=-1, keepdims=True)
                    dob = do_st[half].astype(BF16)
                    dl = probs * (_nt(dob, vals[half]) - delta)
                    dl_ref[half] += dl
                    sink_acc[half] += p_sink * delta
                    dsb = (dl * (HEAD_DIM ** -0.5)).astype(BF16)
                    dq_st.append(_nn(dsb, keys[half]))
                    dk_parts.append(_tn(dsb, q_st[half]))
                    dv_parts.append(_tn(probs.astype(BF16), dob))
                dq = _unstack_heads(dq_st[0], dq_st[1], lo_mask)
                for p in range(4):
                    dq_ref[at, p * BLOCK:(p + 1) * BLOCK] = dq[p]
                dks.append(dk_parts[0] + pltpu.roll(dk_parts[1], HEAD_DIM, 1))
                dvs.append(dv_parts[0] + pltpu.roll(dv_parts[1], HEAD_DIM, 1))
            last = slice((qb - 1) * BLOCK, qb * BLOCK)
            for parts, out_ref, carry in ((dks, dk_ref, k_carry), (dvs, dv_ref, v_carry)):
                out_ref[...] = carry[...]
                out_ref[last, :] += parts[0][:BLOCK, :]
                for b in range(qb):
                    own = parts[b][BLOCK:, :]
                    carry[b * BLOCK:(b + 1) * BLOCK, :] = own + parts[b + 1][:BLOCK, :] if b + 1 < qb else own

        @pl.when(n == n_groups)
        def _():
            dk_ref[...] = k_carry[...]
            dv_ref[...] = v_carry[...]
            for half, heads in enumerate((HEADS_A, HEADS_B)):
                for slot, h in enumerate(heads):
                    tot = jnp.sum(sink_acc[half, slot * BLOCK:(slot + 1) * BLOCK, :], axis=0, keepdims=True)
                    ds_ref[h:h + 1, :] = jnp.broadcast_to(-tot, (1, SMALL_LANES))

    return _call(
        body,
        (sinks, qn, kn, kn, v, v, a, da, tab),
        name="attn_bwd",
        grid=(n_groups + 1,),
        in_specs=[
            pl.BlockSpec(memory_space=pltpu.SMEM),
            pl.BlockSpec((rows, ATTN_WIDTH), group),
            pl.BlockSpec((rows, KV_WIDTH), group),
            pl.BlockSpec((BLOCK, KV_WIDTH), prev),
            pl.BlockSpec((rows, KV_WIDTH), group),
            pl.BlockSpec((BLOCK, KV_WIDTH), prev),
            pl.BlockSpec((rows, ATTN_WIDTH), group),
            pl.BlockSpec((rows, ATTN_WIDTH), group),
            _resident((2, 4 * BLOCK, 2 * BLOCK)),
        ],
        out_specs=[
            pl.BlockSpec((rows, ATTN_WIDTH), group),
            pl.BlockSpec((rows, KV_WIDTH), done),
            pl.BlockSpec((rows, KV_WIDTH), done),
            _acc((2, 4 * BLOCK, 2 * BLOCK)),
            _acc((N_DEV, SMALL_LANES)),
        ],
        out_shape=[
            jax.ShapeDtypeStruct((s, ATTN_WIDTH), F32),
            jax.ShapeDtypeStruct((s, KV_WIDTH), F32),
            jax.ShapeDtypeStruct((s, KV_WIDTH), F32),
            jax.ShapeDtypeStruct((2, 4 * BLOCK, 2 * BLOCK), F32),
            jax.ShapeDtypeStruct((N_DEV, SMALL_LANES), F32),
        ],
        scratch_shapes=[
            pltpu.VMEM((rows, KV_WIDTH), F32),
            pltpu.VMEM((rows, KV_WIDTH), F32),
            pltpu.VMEM((2, 4 * BLOCK, 1), F32),
        ],
        rider=rider,
    )


def _fold_heads(acc):
    t = acc + pltpu.roll(acc, HEAD_DIM, 1)
    out = t[:, :SMALL_LANES]
    for g in range(1, acc.shape[1] // SMALL_LANES):
        out = out + t[:, g * SMALL_LANES:(g + 1) * SMALL_LANES]
    return out


def _in_proj_bwd(dqn, dkn, dv, du, zqk, x, dh1, g_attn, gq_t, gk_t, w_in_t, rider=None):
    s = x.shape[0]
    ts = min(TOKEN_TILE, s)
    nt = s // ts

    def head_norm_bwd(d_n, raw, g_t, bmat):
        r = lax.rsqrt(_seg_mean(raw * raw, bmat) + EPS)
        gy = d_n * g_t
        d_raw = r * gy - raw * (r * r * r) * _seg_mean(gy * raw, bmat)
        return d_raw, jnp.sum(d_n * (raw * r), axis=0, keepdims=True)

    def body(dqn_ref, dkn_ref, dv_ref, du_ref, zqk_ref, x_ref, dh1_ref, g_ref, gq_ref, gk_ref, w_ref, bq_ref, bk_ref,
             gx_ref, dw_ref, dg_ref, dgq_ref, dgk_ref, dz_ref, gq_acc, gk_acc):
        i = pl.program_id(0)

        @pl.when(i == 0)
        def _():
            dw_ref[...] = jnp.zeros_like(dw_ref)
            dg_ref[...] = jnp.zeros_like(dg_ref)
            gq_acc[...] = jnp.zeros_like(gq_acc)
            gk_acc[...] = jnp.zeros_like(gk_acc)

        d_q, d_gq = head_norm_bwd(dqn_ref[...], zqk_ref[:, :ATTN_WIDTH], gq_ref[...], bq_ref[...])
        d_k, d_gk = head_norm_bwd(dkn_ref[...], zqk_ref[:, ATTN_WIDTH:], gk_ref[...], bk_ref[...])
        gq_acc[...] += d_gq
        gk_acc[...] += d_gk
        dz_ref[:, :ATTN_WIDTH] = d_q.astype(BF16)
        dz_ref[:, ATTN_WIDTH:ATTN_WIDTH + KV_WIDTH] = d_k.astype(BF16)
        dz_ref[:, ATTN_WIDTH + KV_WIDTH:ATTN_WIDTH + 2 * KV_WIDTH] = dv_ref[...].astype(BF16)
        dz_ref[:, ATTN_WIDTH + 2 * KV_WIDTH:] = du_ref[...].astype(BF16)
        dz = dz_ref[...]
        xf = x_ref[...]
        r = _rms(xf)
        hn = ((xf * r) * g_ref[...]).astype(BF16)
        dw_ref[...] += _tn(dz, hn)
        d_x, d_g = _rms_bwd(_nn(dz, w_ref[...]), xf, r, g_ref[...])
        dg_ref[...] += d_g
        gx_ref[...] = dh1_ref[...] + d_x

        @pl.when(i == nt - 1)
        def _():
            dgq_ref[...] = _fold_heads(gq_acc[...])
            dgk_ref[...] = _fold_heads(gk_acc[...])

    return _call(
        body,
        (dqn, dkn, dv, du, zqk, x, dh1, g_attn, gq_t, gk_t, w_in_t,
      _head_mean_matrix(ATTN_WIDTH), _head_mean_matrix(KV_WIDTH)),
        name="in_proj_bwd",
        grid=(nt,),
        in_specs=[
            _rows(ts, ATTN_WIDTH),
            _rows(ts, KV_WIDTH),
            _rows(ts, KV_WIDTH),
            _rows(ts, POOL_WIDTH),
            _rows(ts, ATTN_WIDTH + KV_WIDTH),
            _rows(ts, D_MODEL),
            _rows(ts, D_MODEL),
            _resident((1, D_MODEL)),
            _resident((1, ATTN_WIDTH)),
            _resident((1, KV_WIDTH)),
            _resident((IN_WIDTH, D_MODEL)),
            _resident((ATTN_WIDTH, ATTN_WIDTH)),
            _resident((KV_WIDTH, KV_WIDTH)),
        ],
        out_specs=[
            _rows(ts, D_MODEL),
            _acc((IN_WIDTH, D_MODEL)),
            _acc((1, D_MODEL)),
            _acc((1, SMALL_LANES)),
            _acc((1, SMALL_LANES)),
        ],
        out_shape=[
            jax.ShapeDtypeStruct((s, D_MODEL), F32),
            jax.ShapeDtypeStruct((IN_WIDTH, D_MODEL), F32),
            jax.ShapeDtypeStruct((1, D_MODEL), F32),
            jax.ShapeDtypeStruct((1, SMALL_LANES), F32),
            jax.ShapeDtypeStruct((1, SMALL_LANES), F32),
        ],
        scratch_shapes=[
            pltpu.VMEM((ts, IN_WIDTH), BF16),
            pltpu.VMEM((1, ATTN_WIDTH), F32),
            pltpu.VMEM((1, KV_WIDTH), F32),
        ],
        rider=rider,
    )


BIG_WEIGHTS = (
    ("w_in", True, IN_WIDTH // N_DEV, D_MODEL),
    ("w_out", False, D_MODEL // N_DEV, D_MODEL),
    ("w_gate", True, D_FF // N_DEV, D_MODEL),
    ("w_up", True, D_FF // N_DEV, D_MODEL),
    ("w_down", False, D_FF // N_DEV, D_MODEL),
    ("w_ple_gate", False, D_MODEL // N_DEV, D_MODEL),
    ("w_ple_proj", False, PLE_DIM, D_MODEL // N_DEV),
)
N_BIG = len(BIG_WEIGHTS)


def _place():
    x, y, c = lax.axis_index("x"), lax.axis_index("y"), lax.axis_index("c")
    chips = [(1 - x, y), (x, 1 - y), (1 - x, 1 - y)]
    return x, y, c, chips


class _Gather:
    def __init__(self, n, rows=None):
        self.n = n
        self.rows = rows or [None] * n
        self.sems = [pltpu.SemaphoreType.DMA((n, 7)), pltpu.SemaphoreType.DMA((n, 7)), pltpu.SemaphoreType.DMA((n,))]

    def _ctx(self, srcs, outs, sems):
        send_sems, recv_sems, local_sems = sems
        x, y, c, chips = _place()
        me, sibling = (x, y, c), (x, y, 1 - c)

        def part(k, ref):
            return ref if self.rows[k] is None else ref.at[pl.ds(*self.rows[k]), :]

        def block(k, owner):
            px, py, pc = owner
            return part(k, outs[k].at[4 * px + 2 * py + pc])

        def copy(k, idx, owner, to, mine=False):
            return pltpu.make_async_remote_copy(
                src_ref=part(k, srcs[k]) if mine else block(k, owner), dst_ref=block(k, owner),
                send_sem=send_sems.at[k, idx], recv_sem=recv_sems.at[k, idx], device_id=to, device_id_type=MESH)

        def local(k):
            return pltpu.make_async_copy(part(k, srcs[k]), block(k, me), local_sems.at[k])

        return c, chips, me, sibling, copy, local

    def begin(self, srcs, outs, sems):
        c, chips, me, sibling, copy, local = self._ctx(srcs, outs, sems)
        for k in range(self.n):
            local(k).start()
            copy(k, 0, me, sibling, mine=True).start()
            for j, chip in enumerate(chips):
                copy(k, 1 + j, me, (*chip, c), mine=True).start()

    def middle(self, srcs, outs, sems):
        c, chips, me, sibling, copy, local = self._ctx(srcs, outs, sems)
        for j, chip in enumerate(chips):
            for k in range(self.n):
                copy(k, 1 + j, (*chip, c), me).wait_recv()
                copy(k, 4 + j, (*chip, c), sibling).start()

    def end(self, srcs, outs, sems):
        c, chips, me, sibling, copy, local = self._ctx(srcs, outs, sems)
        for k in range(self.n):
            copy(k, 0, sibling, me).wait_recv()
            for j, chip in enumerate(chips):
                copy(k, 4 + j, (*chip, 1 - c), me).wait_recv()
        for k in range(self.n):
            copy(k, 0, me, sibling, mine=True).wait_send()
            for j, chip in enumerate(chips):
                copy(k, 1 + j, me, (*chip, c), mine=True).wait_send()
                copy(k, 4 + j, (*chip, c), sibling).wait_send()
            local(k).wait()


def _gather_rider(items):
    items = [it if isinstance(it, tuple) else (it, None, None, None) for it in items]
    n = len(items)
    g = _Gather(n, [None if r0 is None else (r0, nr) for _, r0, nr, _ in items])
    shapes = [jax.ShapeDtypeStruct((N_DEV, *sh.shape), sh.dtype) for sh, _, _, _ in items]
    stacks = [(k, st) for k, (_, _, _, st) in enumerate(items) if st is not None]
    aliases = {n + i: k for i, (k, _) in enumerate(stacks)}
    return _Rider([sh for sh, _, _, _ in items] + [st for _, st in stacks], shapes, g.sems, g.begin, g.end, g.middle,
                  aliases=aliases)


def _cast_and_gather_first(shards):
    g = _Gather(1)
    any_spec = pl.BlockSpec(memory_space=pl.ANY)
    vmem = pl.BlockSpec(memory_space=pltpu.VMEM)

    def body(*refs):
        ins, outs, gathered, sems = refs[:N_BIG], refs[N_BIG:2 * N_BIG], refs[2 * N_BIG], refs[2 * N_BIG + 1:]
        outs[0][...] = ins[0][...].astype(BF16)
        g.begin(outs[:1], [gathered], sems)
        for k in range(1, N_BIG):
            outs[k][...] = ins[k][...].astype(BF16)
        g.middle(outs[:1], [gathered], sems)
        g.end(outs[:1], [gathered], sems)

    res = pl.pallas_call(
        body,
        name="cast_and_gather_first",
        in_specs=[vmem] * N_BIG,
        out_specs=[vmem] * N_BIG + [any_spec],
        out_shape=[jax.ShapeDtypeStruct((r, c), BF16) for _, _, r, c in BIG_WEIGHTS]
        + [jax.ShapeDtypeStruct((N_DEV, *BIG_WEIGHTS[0][2:]), BF16)],
        scratch_shapes=g.sems,
    )(*shards)
    return list(res[:N_BIG]), res[N_BIG]


def _sibling_rider(grads):
    n = len(grads)

    def copies(gs, lands, sems):
        send_sems, recv_sems = sems
        x, y, c, _ = _place()
        return [
            pltpu.make_async_remote_copy(
                src_ref=gs[k].at[:, 1 - c], dst_ref=lands[k], send_sem=send_sems.at[k], recv_sem=recv_sems.at[k],
                device_id=(x, y, 1 - c), device_id_type=MESH)
            for k in range(n)
        ]

    def begin(gs, lands, sems):
        for cp in copies(gs, lands, sems):
            cp.start()

    def end(gs, lands, sems):
        for cp in copies(gs, lands, sems):
            cp.wait()

    shapes = [jax.ShapeDtypeStruct((N_CHIPS, *g.shape[2:]), F32) for g in grads]
    return _Rider(grads, shapes, [pltpu.SemaphoreType.DMA((n,)), pltpu.SemaphoreType.DMA((n,))], begin, end)


def _chip_sum(k, place, grad, from_sibling):
    _, _, r, c = BIG_WEIGHTS[k]

    def body(place_ref, g_ref, l_ref, own_ref, send_ref):
        q = pl.program_id(0)
        tot = g_ref[0, 0] + l_ref[0]
        mine = q == 2 * place_ref[0] + place_ref[1]

        @pl.when(mine)
        def _():
            own_ref[...] = tot

        send_ref[0] = jnp.where(mine, 0.0, tot).astype(BF16)

    return pl.pallas_call(
        body,
        name=f"chip_sum_{BIG_WEIGHTS[k][0]}",
        grid_spec=pltpu.PrefetchScalarGridSpec(
            num_scalar_prefetch=1,
            grid=(N_CHIPS,),
            in_specs=[
                pl.BlockSpec((1, 1, r, c), lambda q, place: (q, place[2], 0, 0)),
                pl.BlockSpec((1, r, c), lambda q, place: (q, 0, 0)),
            ],
            out_specs=[
                pl.BlockSpec((r, c), lambda q, place: (0, 0)),
                pl.BlockSpec((1, r, c), lambda q, place: (q, 0, 0)),
            ],
        ),
        out_shape=[jax.ShapeDtypeStruct((r, c), F32), jax.ShapeDtypeStruct((N_CHIPS, r, c), BF16)],
    )(place, grad, from_sibling)


def _chips_rider(to_send, small=None):
    n = len(to_send)
    inputs = list(to_send) + ([] if small is None else [small])
    shapes = [jax.ShapeDtypeStruct((3, *t.shape[1:]), BF16) for t in to_send]
    sems = [pltpu.SemaphoreType.DMA((max(n, 1), 3)), pltpu.SemaphoreType.DMA((max(n, 1), 3))]
    if small is not None:
        shapes.append(jax.ShapeDtypeStruct((N_DEV, *small.shape), F32))
        sems += [pltpu.SemaphoreType.DMA((7,)), pltpu.SemaphoreType.DMA((7,)), pltpu.SemaphoreType.DMA]

    def copies(ins, outs, sem_refs):
        x, y, c, chips = _place()
        out = []
        for k in range(n):
            for j, (px, py) in enumerate(chips):
                out.append(pltpu.make_async_remote_copy(
                    src_ref=ins[k].at[2 * px + py], dst_ref=outs[k].at[j],
                    send_sem=sem_refs[0].at[k, j], recv_sem=sem_refs[1].at[k, j],
                    device_id=(px, py, c), device_id_type=MESH))
        local = None
        if small is not None:
            me = 4 * x + 2 * y + c
            local = pltpu.make_async_copy(ins[n], outs[n].at[me], sem_refs[4])
            rel = 0
            for fx in (0, 1):
                for fy in (0, 1):
                    for fc in (0, 1):
                        if (fx, fy, fc) != (0, 0, 0):
                            out.append(pltpu.make_async_remote_copy(
                                src_ref=ins[n], dst_ref=outs[n].at[me],
                                send_sem=sem_refs[2].at[rel], recv_sem=sem_refs[3].at[rel],
                                device_id=(x ^ fx, y ^ fy, c ^ fc), device_id_type=MESH))
                            rel += 1
        return out, local

    def begin(ins, outs, sem_refs):
        remote, local = copies(ins, outs, sem_refs)
        if local is not None:
            local.start()
        for cp in remote:
            cp.start()

    def end(ins, outs, sem_refs):
        remote, local = copies(ins, outs, sem_refs)
        for cp in remote:
            cp.wait()
        if local is not None:
            local.wait()

    return _Rider(inputs, shapes, sems, begin, end)


def _exchange(name, rider):
    return _call(lambda: None, (), name=name, grid=(1,), in_specs=[], out_specs=[], out_shape=[], rider=rider)[1]


def _merge_riders(*riders):
    riders = [r for r in riders if r is not None]
    if len(riders) == 1:
        return riders[0]
    assert not any(r.aliases for r in riders)

    def split(refs, counts):
        out, at = [], 0
        for n in counts:
            out.append(refs[at:at + n])
            at += n
        return out

    def run(which):
        def fn(ins, outs, sems):
            parts = zip(riders, split(ins, [len(r.inputs) for r in riders]),
                        split(outs, [len(r.out_shapes) for r in riders]), split(sems, [len(r.sems) for r in riders]))
            for r, i, o, s in parts:
                hook = getattr(r, which)
                if hook is not None:
                    hook(i, o, s)
        return fn

    middle = run("middle") if any(r.middle is not None for r in riders) else None
    return _Rider(sum((r.inputs for r in riders), []), sum((r.out_shapes for r in riders), []),
                  sum((r.sems for r in riders), []), run("begin"), run("end"), middle)


def _split_outputs(outs, *riders):
    res, at = [], 0
    for r in riders:
        res.append(outs[at:at + len(r.out_shapes)])
        at += len(r.out_shapes)
    return res


def _adamw(w, g, m, v):
    m = ADAM_B1 * m + (1.0 - ADAM_B1) * g
    v = ADAM_B2 * v + (1.0 - ADAM_B2) * jnp.square(g)
    m_hat = m / (1.0 - ADAM_B1 ** ADAM_STEP)
    v_hat = v / (1.0 - ADAM_B2 ** ADAM_STEP)
    delta = -ADAM_LR * (m_hat / (jnp.sqrt(v_hat) + ADAM_EPS) + ADAM_WD * w)
    return delta, m, v


def _adamw_big(k, own, landed, w, m, v, rider=None):
    name, _, r, c = BIG_WEIGHTS[k]
    tile = r // 2
    tiles = lambda i: (i, 0)

    def body(own_ref, land_ref, w_ref, m_ref, v_ref, g_ref, d_ref, nm_ref, nv_ref):
        g = ((own_ref[...] + land_ref[0].astype(F32)) + land_ref[1].astype(F32)) + land_ref[2].astype(F32)
        g_ref[...] = g
        d_ref[...], nm_ref[...], nv_ref[...] = _adamw(w_ref[...], g, m_ref[...], v_ref[...])

    return _call(
        body,
        (own, landed, w, m, v),
        name=f"adamw_{name}",
        grid=(r // tile,),
        in_specs=[pl.BlockSpec((tile, c), tiles), pl.BlockSpec((3, tile, c), lambda i: (0, i, 0))]
        + [pl.BlockSpec((tile, c), tiles)] * 3,
        out_specs=[pl.BlockSpec((tile, c), tiles)] * 4,
        out_shape=[jax.ShapeDtypeStruct((r, c), F32)] * 4,
        rider=rider,
    )


def _sum_small(parts_list):
    n = len(parts_list)

    def body(*refs):
        for p_ref, out_ref in zip(refs[:n], refs[n:]):
            tot = p_ref[0]
            for j in range(1, N_DEV):
                tot = tot + p_ref[j]
            out_ref[...] = tot

    return pl.pallas_call(body, name="sum_small",
                          out_shape=[jax.ShapeDtypeStruct(p.shape[1:], F32) for p in parts_list])(*parts_list)


def _adamw_small(grads, ws, ms, vs):
    n = len(grads)

    def body(*refs):
        g_refs, w_refs, m_refs, v_refs = refs[:n], refs[n:2 * n], refs[2 * n:3 * n], refs[3 * n:4 * n]
        outs = refs[4 * n:]
        for i in range(n):
            d, nm, nv = _adamw(w_refs[i][...], g_refs[i][...], m_refs[i][...], v_refs[i][...])
            outs[i][...] = d
            outs[n + i][...] = nm
            outs[2 * n + i][...] = nv

    shapes = [jax.ShapeDtypeStruct(w.shape, F32) for w in ws]
    return pl.pallas_call(body, name="adamw_small", out_shape=shapes * 3)(*grads, *ws, *ms, *vs)


SMALL_NAMES = ("g_attn_norm", "g_q", "g_k", "attn_sinks", "rel_bias", "w_pool", "pool_scale", "g_ffn_norm", "g_ple_norm")


def _pack_small(arrays):
    rows, offsets = [], []
    at = 0
    for a in arrays:
        flat = a.reshape(-1)
        n_rows = -(-flat.shape[0] // (8 * SMALL_LANES)) * 8
        flat = jnp.pad(flat, (0, n_rows * SMALL_LANES - flat.shape[0]))
        rows.append(flat.reshape(n_rows, SMALL_LANES))
        offsets.append(at)
        at += n_rows
    return jnp.concatenate(rows, axis=0), offsets


def kernel(x, p, w_in, w_out, g_attn_norm, g_q, g_k, attn_sinks, rel_bias, w_pool, pool_scale, g_ffn_norm, w_gate, w_up, w_down, g_ple_norm, w_ple_gate, w_ple_proj, loss_target, m_w_in, m_w_out, m_g_attn_norm, m_g_q, m_g_k, m_attn_sinks, m_rel_bias, m_w_pool, m_pool_scale, m_g_ffn_norm, m_w_gate, m_w_up, m_w_down, m_g_ple_norm, m_w_ple_gate, m_w_ple_proj, v_w_in, v_w_out, v_g_attn_norm, v_g_q, v_g_k, v_attn_sinks, v_rel_bias, v_w_pool, v_pool_scale, v_g_ffn_norm, v_w_gate, v_w_up, v_w_down, v_g_ple_norm, v_w_ple_gate, v_w_ple_proj):
    weights = dict(w_in=w_in, w_out=w_out, g_attn_norm=g_attn_norm, g_q=g_q, g_k=g_k, attn_sinks=attn_sinks,
                   rel_bias=rel_bias, w_pool=w_pool, pool_scale=pool_scale, g_ffn_norm=g_ffn_norm, w_gate=w_gate,
                   w_up=w_up, w_down=w_down, g_ple_norm=g_ple_norm, w_ple_gate=w_ple_gate, w_ple_proj=w_ple_proj)
    m_in = dict(w_in=m_w_in, w_out=m_w_out, g_attn_norm=m_g_attn_norm, g_q=m_g_q, g_k=m_g_k, attn_sinks=m_attn_sinks,
                rel_bias=m_rel_bias, w_pool=m_w_pool, pool_scale=m_pool_scale, g_ffn_norm=m_g_ffn_norm, w_gate=m_w_gate,
                w_up=m_w_up, w_down=m_w_down, g_ple_norm=m_g_ple_norm, w_ple_gate=m_w_ple_gate, w_ple_proj=m_w_ple_proj)
    v_in = dict(w_in=v_w_in, w_out=v_w_out, g_attn_norm=v_g_attn_norm, g_q=v_g_q, g_k=v_g_k, attn_sinks=v_attn_sinks,
                rel_bias=v_rel_bias, w_pool=v_w_pool, pool_scale=v_pool_scale, g_ffn_norm=v_g_ffn_norm, w_gate=v_w_gate,
                w_up=v_w_up, w_down=v_w_down, g_ple_norm=v_g_ple_norm, w_ple_gate=v_w_ple_gate, w_ple_proj=v_w_ple_proj)

    xs = x[0]
    ps = p[0, 0]
    target = loss_target[0]
    wp = w_pool[0]
    gq_t = jnp.tile(g_q, (1, ATTN_WIDTH // HEAD_DIM))
    gk_t = jnp.tile(g_k, (1, KV_WIDTH // HEAD_DIM))

    def to_blocks(k, arr):
        return jnp.swapaxes(arr[0], 0, 1) if BIG_WEIGHTS[k][1] else arr[0]

    def from_blocks(k, arr):
        return (jnp.swapaxes(arr, 0, 1) if BIG_WEIGHTS[k][1] else arr)[None]

    IN, OUT, GATE, UP, DOWN, PG, PP = range(N_BIG)
    full = lambda g: g.reshape(N_DEV * g.shape[1], g.shape[2])
    halves = lambda k, g: g.reshape(N_CHIPS, 2, *BIG_WEIGHTS[k][2:])
    place = jnp.stack([lax.axis_index("x"), lax.axis_index("y"), lax.axis_index("c")]).astype(jnp.int32)

    sh, w_in_g = _cast_and_gather_first([to_blocks(k, weights[name]) for k, (name, _, _, _) in enumerate(BIG_WEIGHTS)])
    w_in_t = full(w_in_g)

    ffn_rows = BIG_WEIGHTS[GATE][2]
    tab = _bias_table(rel_bias.T)
    (zqk, qn, kn, v, u), (w_out_g, wg_g) = _in_proj(
        xs, g_attn_norm, w_in_t, gq_t, gk_t, rider=_gather_rider([sh[OUT], (sh[GATE], 0, GATE_ROWS_EARLY, None)]))
    (a,), (wg_g, wu_g) = _attn_fwd(qn, kn, v, tab, attn_sinks, rider=_gather_rider([
        (sh[GATE], GATE_ROWS_EARLY, ffn_rows - GATE_ROWS_EARLY, wg_g), (sh[UP], 0, UP_ROWS_EARLY, None)]))
    w_out_f = full(w_out_g)
    (h1, hn2, m_out), (wu_g,) = _mix_out(u, a, xs, w_out_f, wp, pool_scale, g_ffn_norm, rider=_gather_rider([
        (sh[UP], UP_ROWS_EARLY, ffn_rows - UP_ROWS_EARLY, wu_g)]))
    wg_t, wu_t = full(wg_g), full(wu_g)
    (gt, up), (wd_g, w_pg_g, w_pp_g) = _ffn_up(hn2, wg_t, wu_t, rider=_gather_rider([sh[DOWN], sh[PG], sh[PP]]))
    w_down_f = full(wd_g)

    sums, landed = [None] * N_BIG, [None] * N_BIG

    def chip_sum(k, grad, from_sibling):
        sums[k] = _chip_sum(k, place, halves(k, grad), from_sibling)

    (loss_part, dh2, dh2b, d_wpg, d_wpp, d_g_ple), _ = _ffn_down_ple(
        gt, up, h1, w_down_f, ps, target, g_ple_norm, full(w_pg_g), w_pp_g)
    (act, dgt, dup, dh1, dh1b, d_g_ffn), sib = _ffn_bwd_act(
        dh2, h1, gt, up, g_ffn_norm, wg_t, wu_t, w_down_f, rider=_sibling_rider([halves(PG, d_wpg), halves(PP, d_wpp)]))
    chip_sum(PG, d_wpg, sib[0])
    chip_sum(PP, d_wpp, sib[1])
    (d_wd,), (landed[PG], landed[PP]) = _ffn_bwd_w("down", act, dh2b, rider=_chips_rider([sums[PG][1], sums[PP][1]]))
    (d_wo,), sib = _out_w_bwd(a, m_out, dh1b, rider=_sibling_rider([halves(DOWN, d_wd)]))
    chip_sum(DOWN, d_wd, sib[0])
    r_sib, r_chips = _sibling_rider([halves(OUT, d_wo)]), _chips_rider([sums[DOWN][1]])
    (d_wg_t,), outs = _ffn_bwd_w("gate", dgt, hn2, rider=_merge_riders(r_sib, r_chips))
    sib, (landed[DOWN],) = _split_outputs(outs, r_sib, r_chips)
    chip_sum(OUT, d_wo, sib[0])
    r_sib, r_chips = _sibling_rider([halves(GATE, d_wg_t)]), _chips_rider([sums[OUT][1]])
    (d_wu_t,), outs = _ffn_bwd_w("up", dup, hn2, rider=_merge_riders(r_sib, r_chips))
    sib, (landed[OUT],) = _split_outputs(outs, r_sib, r_chips)
    chip_sum(GATE, d_wg_t, sib[0])
    r_sib, r_chips = _sibling_rider([halves(UP, d_wu_t)]), _chips_rider([sums[GATE][1]])
    (da, du, d_wpool, d_scale), outs = _mix_bwd(dh1b, u, w_out_f, wp, pool_scale, rider=_merge_riders(r_sib, r_chips))
    sib, (landed[GATE],) = _split_outputs(outs, r_sib, r_chips)
    chip_sum(UP, d_wu_t, sib[0])
    early, early_at = _pack_small([d_wpool, d_scale, d_g_ffn, d_g_ple, loss_part[:, :1]])
    (dqn, dkn, dv, dl_acc, d_sinks), (landed[UP], early_all) = _attn_bwd(
        qn, kn, v, a, da, tab, attn_sinks, rider=_chips_rider([sums[UP][1]], early))
    (grad_x, d_win_t, d_g_attn, d_gq, d_gk), _ = _in_proj_bwd(dqn, dkn, dv, du, zqk, xs, dh1, g_attn_norm, gq_t, gk_t, w_in_t)
    (d_rel_t,), sib = _bias_table_bwd(dl_acc, rider=_sibling_rider([halves(IN, d_win_t)]))
    chip_sum(IN, d_win_t, sib[0])
    late, late_at = _pack_small([d_g_attn, d_gq[:, :HEAD_DIM], d_gk[:, :HEAD_DIM], d_sinks[:, 0], d_rel_t[:, :N_BUCKETS]])
    landed[IN], late_all = _exchange("last_exchange", _chips_rider([sums[IN][1]], late))

    out = {"grad": {}, "delta": {}, "new_m": {}, "new_v": {}}
    for k, (name, _, _, _) in enumerate(BIG_WEIGHTS):
        res, _ = _adamw_big(k, sums[k][0], landed[k], to_blocks(k, weights[name]), to_blocks(k, m_in[name]),
                            to_blocks(k, v_in[name]))
        for kind, r in zip(("grad", "delta", "new_m", "new_v"), res):
            out[kind][name] = from_blocks(k, r)
    early_sum, late_sum = _sum_small([early_all, late_all])

    def unpack(packed, at, shape):
        n = math.prod(shape)
        return packed[at:at + -(-n // SMALL_LANES)].reshape(-1)[:n].reshape(shape)

    small_grads = dict(
        w_pool=unpack(early_sum, early_at[0], w_pool.shape), pool_scale=unpack(early_sum, early_at[1], pool_scale.shape),
        g_ffn_norm=unpack(early_sum, early_at[2], g_ffn_norm.shape), g_ple_norm=unpack(early_sum, early_at[3], g_ple_norm.shape),
        g_attn_norm=unpack(late_sum, late_at[0], g_attn_norm.shape), g_q=unpack(late_sum, late_at[1], g_q.shape),
        g_k=unpack(late_sum, late_at[2], g_k.shape), attn_sinks=unpack(late_sum, late_at[3], attn_sinks.shape),
        rel_bias=unpack(late_sum, late_at[4], rel_bias.T.shape))
    loss = early_sum[early_at[4], 0]
    flip = lambda name, arr: arr.T if name == "rel_bias" else arr
    updates = _adamw_small([small_grads[n] for n in SMALL_NAMES], [flip(n, weights[n]) for n in SMALL_NAMES],
                           [flip(n, m_in[n]) for n in SMALL_NAMES], [flip(n, v_in[n]) for n in SMALL_NAMES])
    n_small = len(SMALL_NAMES)
    for i, name in enumerate(SMALL_NAMES):
        out["grad"][name] = flip(name, small_grads[name])
        out["delta"][name] = flip(name, updates[i])
        out["new_m"][name] = flip(name, updates[n_small + i])
        out["new_v"][name] = flip(name, updates[2 * n_small + i])

    order = ("w_in", "w_out", "g_attn_norm", "g_q", "g_k", "attn_sinks", "rel_bias", "w_pool", "pool_scale",
             "g_ffn_norm", "w_gate", "w_up", "w_down", "g_ple_norm", "w_ple_gate", "w_ple_proj")
    return (loss, grad_x[None], *[out["grad"][n] for n in order], *[out["delta"][n] for n in order],
            *[out["new_m"][n] for n in order], *[out["new_v"][n] for n in order])
```

```python
import functools
import math

import jax
import jax.numpy as jnp
import numpy as np
from jax import lax
from jax.experimental import pallas as pl
from jax.experimental.pallas import tpu as pltpu

F32 = jnp.float32
BF16 = jnp.bfloat16
MESH = pl.DeviceIdType.MESH

D_MODEL = 1024
HEAD_DIM = 64
ATTN_WIDTH = 512
KV_WIDTH = 128
POOL_WIDTH = 512
POOL_SIZES = (2, 4, 8, 16)
POOL_GROUP = 128
POOL_HALO = 16
IN_WIDTH = 1280
D_FF = 2816
PLE_DIM = 256
BLOCK = 128
N_BUCKETS = 32
MAX_DISTANCE = 128
EPS = 1e-6
N_DEV = 8
N_CHIPS = 4

ADAM_LR = 0.001
ADAM_B1 = 0.9
ADAM_B2 = 0.999
ADAM_EPS = 1e-08
ADAM_WD = 0.01
ADAM_STEP = 10

TOKEN_TILE = 512
FFN_BWD_TILE = 256
FF_CHUNK = 256
ATTN_STEP_BLOCKS = 2
GATE_ROWS_EARLY = 96
UP_ROWS_EARLY = 64
HEADS_A = (0, 2, 5, 7)
HEADS_B = (1, 3, 4, 6)
SMALL_LANES = 128


def _nn(a, b):
    return jnp.dot(a, b, preferred_element_type=F32)


def _nt(a, b):
    return lax.dot_general(a, b, (((1,), (1,)), ((), ())), preferred_element_type=F32)


def _tn(a, b):
    return lax.dot_general(a, b, (((0,), (0,)), ((), ())), preferred_element_type=F32)


def _resident(shape):
    nd = len(shape)
    return pl.BlockSpec(shape, lambda i, _nd=nd: (0,) * _nd, pipeline_mode=pl.Buffered(1))


def _rows(tile, width):
    return pl.BlockSpec((tile, width), lambda i: (i, 0))


def _acc(shape):
    nd = len(shape)
    return pl.BlockSpec(shape, lambda i, _nd=nd: (0,) * _nd)


def _head_mean_matrix(width):
    idx = np.arange(width) // HEAD_DIM
    return jnp.asarray((idx[:, None] == idx[None, :]).astype(np.float32) / HEAD_DIM, dtype=BF16)


def _seg_mean(v, bmat):
    hi = v.astype(BF16)
    lo = (v - hi.astype(F32)).astype(BF16)
    return _nn(hi, bmat) + _nn(lo, bmat)


def _rms(x):
    return lax.rsqrt(jnp.mean(x * x, axis=-1, keepdims=True) + EPS)


def _rms_bwd(d_y, x, r, g):
    gy = d_y * g
    d_x = r * gy - x * (r * r * r) * jnp.mean(gy * x, axis=-1, keepdims=True)
    d_g = jnp.sum(d_y * (x * r), axis=0, keepdims=True)
    return d_x, d_g


def _lane_lo(shape):
    return lax.broadcasted_iota(jnp.int32, shape, 1) < HEAD_DIM


class _Rider:
    def __init__(self, inputs, out_shapes, sems, begin, end, middle=None, aliases=None):
        self.inputs, self.out_shapes, self.sems = list(inputs), list(out_shapes), list(sems)
        self.begin, self.middle, self.end = begin, middle, end
        self.aliases = dict(aliases or {})


def _call(body, args, *, name, grid, in_specs, out_specs, out_shape, scratch_shapes=(), rider=None):
    in_specs, out_specs, out_shape, scratch_shapes = list(in_specs), list(out_specs), list(out_shape), list(scratch_shapes)
    if rider is None:
        outs = pl.pallas_call(body, name=name, grid=grid, in_specs=in_specs, out_specs=out_specs, out_shape=out_shape,
                              scratch_shapes=scratch_shapes)(*args)
        return list(outs), []
    n_in, n_out, n_scr = len(in_specs), len(out_shape), len(scratch_shapes)
    r_in, r_out = len(rider.inputs), len(rider.out_shapes)
    n_steps = grid[0]

    def hosted(*refs):
        ins, refs = refs[:n_in], refs[n_in:]
        r_ins, refs = refs[:r_in], refs[r_in:]
        outs, refs = refs[:n_out], refs[n_out:]
        r_outs, refs = refs[:r_out], refs[r_out:]
        scratch, r_sems = refs[:n_scr], refs[n_scr:]
        step = pl.program_id(0)

        @pl.when(step == 0)
        def _():
            rider.begin(r_ins, r_outs, r_sems)

        if rider.middle is not None:
            @pl.when(step == n_steps - 1)
            def _():
                rider.middle(r_ins, r_outs, r_sems)

        body(*ins, *outs, *scratch)

        @pl.when(step == n_steps - 1)
        def _():
            rider.end(r_ins, r_outs, r_sems)

    any_spec = pl.BlockSpec(memory_space=pl.ANY)
    outs = pl.pallas_call(
        hosted, name=name, grid=grid,
        in_specs=in_specs + [any_spec] * r_in,
        out_specs=out_specs + [any_spec] * r_out,
        out_shape=out_shape + rider.out_shapes,
        scratch_shapes=scratch_shapes + rider.sems,
        input_output_aliases={n_in + i: n_out + o for i, o in rider.aliases.items()},
    )(*args, *rider.inputs)
    return list(outs[:n_out]), list(outs[n_out:])


def _in_proj(x, g_attn, w_in_t, gq_t, gk_t, rider=None):
    s = x.shape[0]
    ts = min(TOKEN_TILE, s)

    def body(x_ref, g_ref, w_ref, gq_ref, gk_ref, bq_ref, bk_ref, zqk_ref, qn_ref, kn_ref, v_ref, u_ref):
        xf = x_ref[...]
        hn = ((xf * _rms(xf)) * g_ref[...]).astype(BF16)
        z = _nt(hn, w_ref[...])
        q = z[:, :ATTN_WIDTH]
        k = z[:, ATTN_WIDTH:ATTN_WIDTH + KV_WIDTH]
        zqk_ref[...] = z[:, :ATTN_WIDTH + KV_WIDTH]
        rq = lax.rsqrt(_seg_mean(q * q, bq_ref[...]) + EPS)
        qn_ref[...] = ((q * rq) * gq_ref[...]).astype(BF16)
        rk = lax.rsqrt(_seg_mean(k * k, bk_ref[...]) + EPS)
        kn_ref[...] = ((k * rk) * gk_ref[...]).astype(BF16)
        v_ref[...] = z[:, ATTN_WIDTH + KV_WIDTH:ATTN_WIDTH + 2 * KV_WIDTH].astype(BF16)
        u_ref[...] = z[:, ATTN_WIDTH + 2 * KV_WIDTH:]

    return _call(
        body,
        (x, g_attn, w_in_t, gq_t, gk_t, _head_mean_matrix(ATTN_WIDTH), _head_mean_matrix(KV_WIDTH)),
        name="in_proj",
        grid=(s // ts,),
        in_specs=[
            _rows(ts, D_MODEL),
            _resident((1, D_MODEL)),
            _resident((IN_WIDTH, D_MODEL)),
            _resident((1, ATTN_WIDTH)),
            _resident((1, KV_WIDTH)),
            _resident((ATTN_WIDTH, ATTN_WIDTH)),
            _resident((KV_WIDTH, KV_WIDTH)),
        ],
        out_specs=[
            _rows(ts, ATTN_WIDTH + KV_WIDTH),
            _rows(ts, ATTN_WIDTH),
            _rows(ts, KV_WIDTH),
            _rows(ts, KV_WIDTH),
            _rows(ts, POOL_WIDTH),
        ],
        out_shape=[
            jax.ShapeDtypeStruct((s, ATTN_WIDTH + KV_WIDTH), F32),
            jax.ShapeDtypeStruct((s, ATTN_WIDTH), BF16),
            jax.ShapeDtypeStruct((s, KV_WIDTH), BF16),
            jax.ShapeDtypeStruct((s, KV_WIDTH), BF16),
            jax.ShapeDtypeStruct((s, POOL_WIDTH), F32),
        ],
        rider=rider,
    )


def _bucket_ranges():
    n = np.arange(MAX_DISTANCE)
    max_exact = N_BUCKETS // 2
    nf = np.maximum(n, 1).astype(np.float64)
    large = max_exact + (np.log(nf / max_exact) / math.log(MAX_DISTANCE / max_exact) * (N_BUCKETS - max_exact)).astype(np.int64)
    bucket = np.where(n < max_exact, n, np.minimum(large, N_BUCKETS - 1))
    out = []
    for b in range(N_BUCKETS):
        idx = np.nonzero(bucket == b)[0]
        out.append((int(idx.min()), int(idx.max()) + 1))
    return out


def _band_distance():
    i = lax.broadcasted_iota(jnp.int32, (BLOCK, 2 * BLOCK), 0)
    j = lax.broadcasted_iota(jnp.int32, (BLOCK, 2 * BLOCK), 1)
    return BLOCK + i - j


def _bias_table(rel_bias_t):
    ranges = _bucket_ranges()

    def body(rb_ref, tab_ref):
        d = _band_distance()
        for half, heads in enumerate((HEADS_A, HEADS_B)):
            for slot, h in enumerate(heads):
                t = jnp.full((BLOCK, 2 * BLOCK), -jnp.inf, F32)
                for b, (lo, hi) in enumerate(ranges):
                    t = jnp.where((d >= lo) & (d < hi), rb_ref[h, b], t)
                tab_ref[half, slot * BLOCK:(slot + 1) * BLOCK, :] = t

    return pl.pallas_call(
        body,
        name="bias_table",
        in_specs=[pl.BlockSpec(memory_space=pltpu.SMEM)],
        out_shape=jax.ShapeDtypeStruct((2, 4 * BLOCK, 2 * BLOCK), F32),
    )(rel_bias_t)


def _bias_table_bwd(dl_acc, rider=None):
    ranges = _bucket_ranges()
    n_heads = len(HEADS_A) + len(HEADS_B)

    def body(dl_ref, out_ref):
        d = _band_distance()
        row = lax.broadcasted_iota(jnp.int32, (n_heads, SMALL_LANES), 0)
        lane = lax.broadcasted_iota(jnp.int32, (n_heads, SMALL_LANES), 1)
        out = jnp.zeros((n_heads, SMALL_LANES), F32)
        for b, (lo, hi) in enumerate(ranges):
            in_bucket = (d >= lo) & (d < hi)
            for half, heads in enumerate((HEADS_A, HEADS_B)):
                for slot, h in enumerate(heads):
                    g = dl_ref[half, slot * BLOCK:(slot + 1) * BLOCK, :]
                    part = jnp.sum(jnp.where(in_bucket, g, 0.0), axis=0, keepdims=True)
                    tot = jnp.sum(part, axis=1, keepdims=True)
                    out = jnp.where((row == h) & (lane == b), tot, out)
        out_ref[...] = out

    return _call(
        body,
        (dl_acc,),
        name="bias_table_bwd",
        grid=(1,),
        in_specs=[_acc((2, 4 * BLOCK, 2 * BLOCK))],
        out_specs=[_acc((n_heads, SMALL_LANES))],
        out_shape=[jax.ShapeDtypeStruct((n_heads, SMALL_LANES), F32)],
        rider=rider,
    )


def _stack_heads(pairs, lo_mask):
    zero = jnp.zeros_like(pairs[0])
    lo = [jnp.where(lo_mask, t, zero) for t in pairs]
    hi = [jnp.where(lo_mask, zero, t) for t in pairs]
    return (jnp.concatenate([lo[0], lo[1], hi[2], hi[3]], axis=0),
            jnp.concatenate([hi[0], hi[1], lo[2], lo[3]], axis=0))


def _unstack_heads(out_a, out_b, lo_mask):
    t = lambda x, r: x[r * BLOCK:(r + 1) * BLOCK, :]
    return [
        jnp.where(lo_mask, t(out_a, 0), t(out_b, 0)),
        jnp.where(lo_mask, t(out_a, 1), t(out_b, 1)),
        jnp.where(lo_mask, t(out_b, 2), t(out_a, 2)),
        jnp.where(lo_mask, t(out_b, 3), t(out_a, 3)),
    ]


def _sink_column(sink_ref, heads):
    row = lax.broadcasted_iota(jnp.int32, (4 * BLOCK, 1), 0)
    col = jnp.full((4 * BLOCK, 1), sink_ref[0, heads[3]], F32)
    for slot in (2, 1, 0):
        col = jnp.where(row < (slot + 1) * BLOCK, sink_ref[0, heads[slot]], col)
    return col


def _band_probs(q_stack, keys, tab, sink, first_block):
    s = _nt(q_stack, keys) * (HEAD_DIM ** -0.5) + tab
    if first_block is not None:
        col = lax.broadcasted_iota(jnp.int32, s.shape, 1)
        s = jnp.where(jnp.logical_and(first_block, col < BLOCK), -jnp.inf, s)
    m = jnp.maximum(jnp.max(s, axis=-1, keepdims=True), sink)
    e = jnp.exp(s - m)
    e_sink = jnp.exp(sink - m)
    den = jnp.sum(e, axis=-1, keepdims=True) + e_sink
    return e / den, e_sink / den


def _attn_specs(n_groups):
    group = lambda n: (jnp.minimum(n, n_groups - 1), 0)
    prev = lambda n: (jnp.maximum(jnp.minimum(n, n_groups - 1) * ATTN_STEP_BLOCKS - 1, 0), 0)
    return group, prev


def _band(prev_ref, group_ref, b):
    rows = lambda i: group_ref[i * BLOCK:(i + 1) * BLOCK, :]
    band = jnp.concatenate([prev_ref[...] if b == 0 else rows(b - 1), rows(b)], axis=0)
    return band, pltpu.roll(band, HEAD_DIM, 1)


def _attn_fwd(qn, kn, v, tab, sinks, rider=None):
    s = qn.shape[0]
    n_groups = s // (ATTN_STEP_BLOCKS * BLOCK)
    group, prev = _attn_specs(n_groups)
    rows = ATTN_STEP_BLOCKS * BLOCK

    def body(sink_ref, q_ref, kc_ref, kp_ref, vc_ref, vp_ref, tab_ref, o_ref):
        first = pl.program_id(0) == 0
        lo_mask = _lane_lo((BLOCK, BLOCK))
        for b in range(ATTN_STEP_BLOCKS):
            at = slice(b * BLOCK, (b + 1) * BLOCK)
            kk, kk_sw = _band(kp_ref, kc_ref, b)
            vv, vv_sw = _band(vp_ref, vc_ref, b)
            q_a, q_b = _stack_heads([q_ref[at, p * BLOCK:(p + 1) * BLOCK] for p in range(4)], lo_mask)
            no_prev = first if b == 0 else None
            p_a, _ = _band_probs(q_a, kk, tab_ref[0], _sink_column(sink_ref, HEADS_A), no_prev)
            p_b, _ = _band_probs(q_b, kk_sw, tab_ref[1], _sink_column(sink_ref, HEADS_B), no_prev)
            out = _unstack_heads(_nn(p_a.astype(BF16), vv), _nn(p_b.astype(BF16), vv_sw), lo_mask)
            for p in range(4):
                o_ref[at, p * BLOCK:(p + 1) * BLOCK] = out[p].astype(BF16)

    return _call(
        body,
        (sinks, qn, kn, kn, v, v, tab),
        name="attn_fwd",
        grid=(n_groups,),
        in_specs=[
            pl.BlockSpec(memory_space=pltpu.SMEM),
            pl.BlockSpec((rows, ATTN_WIDTH), group),
            pl.BlockSpec((rows, KV_WIDTH), group),
            pl.BlockSpec((BLOCK, KV_WIDTH), prev),
            pl.BlockSpec((rows, KV_WIDTH), group),
            pl.BlockSpec((BLOCK, KV_WIDTH), prev),
            _resident((2, 4 * BLOCK, 2 * BLOCK)),
        ],
        out_specs=[pl.BlockSpec((rows, ATTN_WIDTH), group)],
        out_shape=[jax.ShapeDtypeStruct((s, ATTN_WIDTH), BF16)],
        rider=rider,
    )


def _pooled(u_tile, u_halo, tile_index, tile_rows):
    halo = jnp.where(tile_index > 0, u_halo, 0.0)
    ext = jnp.concatenate([halo, u_tile], axis=0)
    sums = []
    acc = ext
    for shift in (1, 2, 4, 8):
        acc = acc + pltpu.roll(acc, shift, 0)
        sums.append(acc)
    t = tile_index * tile_rows + lax.broadcasted_iota(jnp.int32, (tile_rows, 1), 0)
    out = []
    for g, w in enumerate(POOL_SIZES):
        lanes = slice(g * POOL_GROUP, (g + 1) * POOL_GROUP)
        cnt = jnp.minimum(t + 1, w).astype(F32)
        out.append(sums[g][POOL_HALO:, lanes] / cnt - u_tile[:, lanes])
    return out


def _halo_before(tile):
    return lambda i: (jnp.maximum(i * (tile // POOL_HALO) - 1, 0), 0)


def _mix_out(u, a, x, w_out, w_pool, pool_scale, g_ffn, rider=None):
    s = x.shape[0]
    ts = min(TOKEN_TILE, s)

    def body(u_ref, uh_ref, a_ref, x_ref, wo_ref, wp_ref, sc_ref, g_ref, h1_ref, hn_ref, m_ref):
        i = pl.program_id(0)
        pooled = _pooled(u_ref[...], uh_ref[...], i, ts)
        for g in range(len(POOL_SIZES)):
            lanes = slice(g * POOL_GROUP, (g + 1) * POOL_GROUP)
            y = _nn(pooled[g].astype(BF16), wp_ref[g].astype(BF16))
            m_ref[:, lanes] = (y * sc_ref[:, lanes]).astype(BF16)
        h1 = x_ref[...] + _nn(a_ref[...], wo_ref[:ATTN_WIDTH, :]) + _nn(m_ref[...], wo_ref[ATTN_WIDTH:, :])
        h1_ref[...] = h1
        hn_ref[...] = ((h1 * _rms(h1)) * g_ref[...]).astype(BF16)

    return _call(
        body,
        (u, u, a, x, w_out, w_pool, pool_scale, g_ffn),
        name="mix_out",
        grid=(s // ts,),
        in_specs=[
            _rows(ts, POOL_WIDTH),
            pl.BlockSpec((POOL_HALO, POOL_WIDTH), _halo_before(ts)),
            _rows(ts, ATTN_WIDTH),
            _rows(ts, D_MODEL),
            _resident((D_MODEL, D_MODEL)),
            _resident((len(POOL_SIZES), POOL_GROUP, POOL_GROUP)),
            _resident((1, POOL_WIDTH)),
            _resident((1, D_MODEL)),
        ],
        out_specs=[_rows(ts, D_MODEL), _rows(ts, D_MODEL), _rows(ts, POOL_WIDTH)],
        out_shape=[
            jax.ShapeDtypeStruct((s, D_MODEL), F32),
            jax.ShapeDtypeStruct((s, D_MODEL), BF16),
            jax.ShapeDtypeStruct((s, POOL_WIDTH), BF16),
        ],
        rider=rider,
    )


def _ffn_up(hn2, wg_t, wu_t, rider=None):
    s = hn2.shape[0]
    ts = min(TOKEN_TILE, s)

    def body(hn_ref, wg_ref, wu_ref, gt_ref, up_ref):
        hn = hn_ref[...]
        for c in range(D_FF // FF_CHUNK):
            cols = slice(c * FF_CHUNK, (c + 1) * FF_CHUNK)
            gt_ref[:, cols] = _nt(hn, wg_ref[cols, :]).astype(BF16)
            up_ref[:, cols] = _nt(hn, wu_ref[cols, :]).astype(BF16)

    return _call(
        body,
        (hn2, wg_t, wu_t),
        name="ffn_up",
        grid=(s // ts,),
        in_specs=[_rows(ts, D_MODEL), _resident((D_FF, D_MODEL)), _resident((D_FF, D_MODEL))],
        out_specs=[_rows(ts, D_FF), _rows(ts, D_FF)],
        out_shape=[jax.ShapeDtypeStruct((s, D_FF), BF16), jax.ShapeDtypeStruct((s, D_FF), BF16)],
        rider=rider,
    )


def _silu_mul(gt, up):
    return (gt * jax.nn.sigmoid(gt)) * up


def _ffn_down_ple(gt, up, h1, w_down, p, target, g_ple, w_pg, w_pp, rider=None):
    s = h1.shape[0]
    ts = min(TOKEN_TILE, s)
    blk = D_MODEL // N_DEV

    def body(gt_ref, up_ref, h1_ref, wd_ref, p_ref, t_ref, g_ref, wpg_ref, wpp_ref,
             loss_ref, dh_ref, dwpg_ref, dwpp_ref, dg_ref, act_ref, pp_ref):
        @pl.when(pl.program_id(0) == 0)
        def _():
            loss_ref[...] = jnp.zeros_like(loss_ref)
            dwpg_ref[...] = jnp.zeros_like(dwpg_ref)
            dwpp_ref[...] = jnp.zeros_like(dwpp_ref)
            dg_ref[...] = jnp.zeros_like(dg_ref)

        for c in range(D_FF // FF_CHUNK):
            cols = slice(c * FF_CHUNK, (c + 1) * FF_CHUNK)
            act_ref[:, cols] = _silu_mul(gt_ref[:, cols].astype(F32), up_ref[:, cols].astype(F32)).astype(BF16)
        h2v = h1_ref[...] + _nn(act_ref[...], wd_ref[...])
        r = _rms(h2v)
        hn = ((h2v * r) * g_ref[...]).astype(BF16)
        gate = jax.nn.sigmoid(_nn(hn, wpg_ref[...]))
        pb = p_ref[...].astype(BF16)
        for j in range(N_DEV):
            pp_ref[:, j * blk:(j + 1) * blk] = _nn(pb, wpp_ref[j])
        pp = pp_ref[...]
        diff = (h2v + gate * pp) - t_ref[...]
        loss_ref[...] += jnp.sum(jnp.sum(diff * diff, axis=0, keepdims=True), axis=1, keepdims=True) * (0.5 / D_MODEL)
        dy = diff * (1.0 / D_MODEL)
        d_pp = (dy * gate).astype(BF16)
        d_pre = ((dy * pp) * (gate * (1.0 - gate))).astype(BF16)
        for j in range(N_DEV):
            dwpp_ref[j] += _tn(pb, d_pp[:, j * blk:(j + 1) * blk])
        dwpg_ref[...] += _tn(hn, d_pre)
        d_x, d_g = _rms_bwd(_nt(d_pre, wpg_ref[...]), h2v, r, g_ref[...])
        dg_ref[...] += d_g
        dh_ref[...] = dy + d_x

    return _call(
        body,
        (gt, up, h1, w_down, p, target, g_ple, w_pg, w_pp),
        name="ffn_down_ple",
        grid=(s // ts,),
        in_specs=[
            _rows(ts, D_FF),
            _rows(ts, D_FF),
            _rows(ts, D_MODEL),
            _resident((D_FF, D_MODEL)),
            _rows(ts, PLE_DIM),
            _rows(ts, D_MODEL),
            _resident((1, D_MODEL)),
            _resident((D_MODEL, D_MODEL)),
            _resident((N_DEV, PLE_DIM, blk)),
        ],
        out_specs=[
            _acc((1, SMALL_LANES)),
            _rows(ts, D_MODEL),
            _acc((D_MODEL, D_MODEL)),
            _acc((N_DEV, PLE_DIM, blk)),
            _acc((1, D_MODEL)),
        ],
        out_shape=[
            jax.ShapeDtypeStruct((1, SMALL_LANES), F32),
            jax.ShapeDtypeStruct((s, D_MODEL), F32),
            jax.ShapeDtypeStruct((D_MODEL, D_MODEL), F32),
            jax.ShapeDtypeStruct((N_DEV, PLE_DIM, blk), F32),
            jax.ShapeDtypeStruct((1, D_MODEL), F32),
        ],
        scratch_shapes=[pltpu.VMEM((ts, D_FF), BF16), pltpu.VMEM((ts, D_MODEL), F32)],
        rider=rider,
    )


def _ffn_bwd_act(dh2, h1, gt, up, g_ffn, wg_t, wu_t, w_down, rider=None):
    s = h1.shape[0]
    ts = min(FFN_BWD_TILE, s)

    def body(dh_ref, h1_ref, gt_ref, up_ref, g_ref, wg_ref, wu_ref, wd_ref,
             dgt_ref, dup_ref, dh1_ref, dh1b_ref, dg_ref, dwd_ref, act_ref):
        @pl.when(pl.program_id(0) == 0)
        def _():
            dg_ref[...] = jnp.zeros_like(dg_ref)
            dwd_ref[...] = jnp.zeros_like(dwd_ref)

        dhb = dh_ref[...].astype(BF16)
        for c in range(D_FF // FF_CHUNK):
            cols = slice(c * FF_CHUNK, (c + 1) * FF_CHUNK)
            d_act = _nt(dhb, wd_ref[cols, :])
            gtv = gt_ref[:, cols].astype(F32)
            upv = up_ref[:, cols].astype(F32)
            sg = jax.nn.sigmoid(gtv)
            silu = gtv * sg
            act_ref[:, cols] = (silu * upv).astype(BF16)
            dup_ref[:, cols] = (d_act * silu).astype(BF16)
            dgt_ref[:, cols] = ((d_act * upv) * (sg * (1.0 + gtv * (1.0 - sg)))).astype(BF16)
        dwd_ref[...] += _tn(act_ref[...], dhb)
        d_hn = _nn(dgt_ref[...], wg_ref[...]) + _nn(dup_ref[...], wu_ref[...])
        h1v = h1_ref[...]
        d_x, d_g = _rms_bwd(d_hn, h1v, _rms(h1v), g_ref[...])
        dg_ref[...] += d_g
        dh1 = dh_ref[...] + d_x
        dh1_ref[...] = dh1
        dh1b_ref[...] = dh1.astype(BF16)

    return _call(
        body,
        (dh2, h1, gt, up, g_ffn, wg_t, wu_t, w_down),
        name="ffn_bwd_act",
        grid=(s // ts,),
        in_specs=[
            _rows(ts, D_MODEL),
            _rows(ts, D_MODEL),
            _rows(ts, D_FF),
            _rows(ts, D_FF),
            _resident((1, D_MODEL)),
            _resident((D_FF, D_MODEL)),
            _resident((D_FF, D_MODEL)),
            _resident((D_FF, D_MODEL)),
        ],
        out_specs=[
            _rows(ts, D_FF), _rows(ts, D_FF),
            _rows(ts, D_MODEL), _rows(ts, D_MODEL), _acc((1, D_MODEL)), _acc((D_FF, D_MODEL)),
        ],
        out_shape=[
            jax.ShapeDtypeStruct((s, D_FF), BF16),
            jax.ShapeDtypeStruct((s, D_FF), BF16),
            jax.ShapeDtypeStruct((s, D_MODEL), F32),
            jax.ShapeDtypeStruct((s, D_MODEL), BF16),
            jax.ShapeDtypeStruct((1, D_MODEL), F32),
            jax.ShapeDtypeStruct((D_FF, D_MODEL), F32),
        ],
        scratch_shapes=[pltpu.VMEM((ts, D_FF), BF16)],
        rider=rider,
    )


def _ffn_bwd_w(which, lhs, rhs, rider=None):
    s = rhs.shape[0]

    def body(lhs_ref, rhs_ref, dw_ref):
        dw_ref[...] = _tn(lhs_ref[...], rhs_ref[...])

    return _call(
        body,
        (lhs, rhs),
        name=f"ffn_bwd_{which}",
        grid=(D_FF // FF_CHUNK,),
        in_specs=[pl.BlockSpec((s, FF_CHUNK), lambda i: (0, i)), _resident((s, D_MODEL))],
        out_specs=[_rows(FF_CHUNK, D_MODEL)],
        out_shape=[jax.ShapeDtypeStruct((D_FF, D_MODEL), F32)],
        rider=rider,
    )


def _mix_bwd(dh1b, u, w_out, w_pool, pool_scale, rider=None):
    s = u.shape[0]
    ts = min(TOKEN_TILE, s)
    nt = s // ts
    halo_after = lambda i: (jnp.minimum((i + 1) * (ts // POOL_HALO), s // POOL_HALO - 1), 0)
    n_groups = len(POOL_SIZES)

    def body(dh_ref, dhn_ref, u_ref, uh_ref, wo_ref, wp_ref, sc_ref, da_ref, du_ref, dwp_ref, dsc_ref):
        i = pl.program_id(0)

        @pl.when(i == 0)
        def _():
            dwp_ref[...] = jnp.zeros_like(dwp_ref)
            dsc_ref[...] = jnp.zeros_like(dsc_ref)

        dh = dh_ref[...]
        da_ref[...] = _nt(dh, wo_ref[:ATTN_WIDTH, :])
        dh_next = jnp.where(i < nt - 1, dhn_ref[...], jnp.zeros_like(dhn_ref))
        dm_ext = _nt(jnp.concatenate([dh, dh_next], axis=0), wo_ref[ATTN_WIDTH:, :])
        pooled = _pooled(u_ref[...], uh_ref[...], i, ts)
        t_ext = i * ts + lax.broadcasted_iota(jnp.int32, (ts + POOL_HALO, 1), 0)
        for g, w in enumerate(POOL_SIZES):
            lanes = slice(g * POOL_GROUP, (g + 1) * POOL_GROUP)
            wp = wp_ref[g].astype(BF16)
            pg = pooled[g].astype(BF16)
            dm_g = dm_ext[:, lanes]
            dsc_ref[:, lanes] += jnp.sum(dm_g[:ts, :] * _nn(pg, wp), axis=0, keepdims=True)
            dy = (dm_g * sc_ref[:, lanes]).astype(BF16)
            dwp_ref[g] += _tn(pg, dy[:ts, :])
            d_pool = _nt(dy, wp)
            acc = d_pool / jnp.minimum(t_ext + 1, w).astype(F32)
            shift = 1
            while shift < w:
                acc = acc + pltpu.roll(acc, ts + POOL_HALO - shift, 0)
                shift *= 2
            du_ref[:, lanes] = acc[:ts, :] - d_pool[:ts, :]

    return _call(
        body,
        (dh1b, dh1b, u, u, w_out, w_pool, pool_scale),
        name="mix_bwd",
        grid=(nt,),
        in_specs=[
            _rows(ts, D_MODEL),
            pl.BlockSpec((POOL_HALO, D_MODEL), halo_after),
            _rows(ts, POOL_WIDTH),
            pl.BlockSpec((POOL_HALO, POOL_WIDTH), _halo_before(ts)),
            _resident((D_MODEL, D_MODEL)),
            _resident((n_groups, POOL_GROUP, POOL_GROUP)),
            _resident((1, POOL_WIDTH)),
        ],
        out_specs=[
            _rows(ts, ATTN_WIDTH),
            _rows(ts, POOL_WIDTH),
            _acc((n_groups, POOL_GROUP, POOL_GROUP)),
            _acc((1, POOL_WIDTH)),
        ],
        out_shape=[
            jax.ShapeDtypeStruct((s, ATTN_WIDTH), F32),
            jax.ShapeDtypeStruct((s, POOL_WIDTH), F32),
            jax.ShapeDtypeStruct((n_groups, POOL_GROUP, POOL_GROUP), F32),
            jax.ShapeDtypeStruct((1, POOL_WIDTH), F32),
        ],
        rider=rider,
    )


def _out_w_bwd(a, m, dh1b, rider=None):
    s = dh1b.shape[0]

    def body(a_ref, m_ref, dh_ref, dw_ref):
        @pl.when(pl.program_id(0) == 0)
        def _():
            dw_ref[...] = _tn(a_ref[...], dh_ref[...])

        @pl.when(pl.program_id(0) == 1)
        def _():
            dw_ref[...] = _tn(m_ref[...], dh_ref[...])

    return _call(
        body,
        (a, m, dh1b),
        name="out_w_bwd",
        grid=(2,),
        in_specs=[_resident((s, ATTN_WIDTH)), _resident((s, POOL_WIDTH)), _resident((s, D_MODEL))],
        out_specs=[_rows(ATTN_WIDTH, D_MODEL)],
        out_shape=[jax.ShapeDtypeStruct((D_MODEL, D_MODEL), F32)],
        rider=rider,
    )


def _attn_bwd(qn, kn, v, a, da, tab, sinks, rider=None):
    s = qn.shape[0]
    qb = ATTN_STEP_BLOCKS
    rows = qb * BLOCK
    n_groups = s // rows
    group, prev = _attn_specs(n_groups)
    done = lambda n: (jnp.maximum(n - 1, 0), 0)

    def body(sink_ref, q_ref, kc_ref, kp_ref, vc_ref, vp_ref, o_ref, do_ref, tab_ref,
             dq_ref, dk_ref, dv_ref, dl_ref, ds_ref, k_carry, v_carry, sink_acc):
        n = pl.program_id(0)

        @pl.when(n == 0)
        def _():
            dl_ref[...] = jnp.zeros_like(dl_ref)
            k_carry[...] = jnp.zeros_like(k_carry)
            v_carry[...] = jnp.zeros_like(v_carry)
            sink_acc[...] = jnp.zeros_like(sink_acc)

        @pl.when(n < n_groups)
        def _():
            first = n == 0
            lo_mask = _lane_lo((BLOCK, BLOCK))
            dks, dvs = [], []
            for b in range(qb):
                at = slice(b * BLOCK, (b + 1) * BLOCK)
                keys = _band(kp_ref, kc_ref, b)
                vals = _band(vp_ref, vc_ref, b)
                q_st = _stack_heads([q_ref[at, p * BLOCK:(p + 1) * BLOCK] for p in range(4)], lo_mask)
                do_st = _stack_heads([do_ref[at, p * BLOCK:(p + 1) * BLOCK] for p in range(4)], lo_mask)
                o_st = _stack_heads([o_ref[at, p * BLOCK:(p + 1) * BLOCK].astype(F32) for p in range(4)], lo_mask)
                dq_st, dk_parts, dv_parts = [], [], []
                for half, heads in enumerate((HEADS_A, HEADS_B)):
                    probs, p_sink = _band_probs(q_st[half], keys[half], tab_ref[half], _sink_column(sink_ref, heads),
                                                first if b == 0 else None)
                    delta = jnp.sum(do_st[half] * o_st[half], axis=-1, keepdims=True)
                    dob = do_st[half].astype(BF16)
                    dl = probs * (_nt(dob, vals[half]) - delta)
                    dl_ref[half] += dl
                    sink_acc[half] += p_sink * delta
                    dsb = (dl * (HEAD_DIM ** -0.5)).astype(BF16)
                    dq_st.append(_nn(dsb, keys[half]))
                    dk_parts.append(_tn(dsb, q_st[half]))
                    dv_parts.append(_tn(probs.astype(BF16), dob))
                dq = _unstack_heads(dq_st[0], dq_st[1], lo_mask)
                for p in range(4):
                    dq_ref[at, p * BLOCK:(p + 1) * BLOCK] = dq[p]
                dks.append(dk_parts[0] + pltpu.roll(dk_parts[1], HEAD_DIM, 1))
                dvs.append(dv_parts[0] + pltpu.roll(dv_parts[1], HEAD_DIM, 1))
            last = slice((qb - 1) * BLOCK, qb * BLOCK)
            for parts, out_ref, carry in ((dks, dk_ref, k_carry), (dvs, dv_ref, v_carry)):
                out_ref[...] = carry[...]
                out_ref[last, :] += parts[0][:BLOCK, :]
                for b in range(qb):
                    own = parts[b][BLOCK:, :]
                    carry[b * BLOCK:(b + 1) * BLOCK, :] = own + parts[b + 1][:BLOCK, :] if b + 1 < qb else own

        @pl.when(n == n_groups)
        def _():
            dk_ref[...] = k_carry[...]
            dv_ref[...] = v_carry[...]
            for half, heads in enumerate((HEADS_A, HEADS_B)):
                for slot, h in enumerate(heads):
                    tot = jnp.sum(sink_acc[half, slot * BLOCK:(slot + 1) * BLOCK, :], axis=0, keepdims=True)
                    ds_ref[h:h + 1, :] = jnp.broadcast_to(-tot, (1, SMALL_LANES))

    return _call(
        body,
        (sinks, qn, kn, kn, v, v, a, da, tab),
        name="attn_bwd",
        grid=(n_groups + 1,),
        in_specs=[
            pl.BlockSpec(memory_space=pltpu.SMEM),
            pl.BlockSpec((rows, ATTN_WIDTH), group),
            pl.BlockSpec((rows, KV_WIDTH), group),
            pl.BlockSpec((BLOCK, KV_WIDTH), prev),
            pl.BlockSpec((rows, KV_WIDTH), group),
            pl.BlockSpec((BLOCK, KV_WIDTH), prev),
            pl.BlockSpec((rows, ATTN_WIDTH), group),
            pl.BlockSpec((rows, ATTN_WIDTH), group),
            _resident((2, 4 * BLOCK, 2 * BLOCK)),
        ],
        out_specs=[
            pl.BlockSpec((rows, ATTN_WIDTH), group),
            pl.BlockSpec((rows, KV_WIDTH), done),
            pl.BlockSpec((rows, KV_WIDTH), done),
            _acc((2, 4 * BLOCK, 2 * BLOCK)),
            _acc((N_DEV, SMALL_LANES)),
        ],
        out_shape=[
            jax.ShapeDtypeStruct((s, ATTN_WIDTH), F32),
            jax.ShapeDtypeStruct((s, KV_WIDTH), F32),
            jax.ShapeDtypeStruct((s, KV_WIDTH), F32),
            jax.ShapeDtypeStruct((2, 4 * BLOCK, 2 * BLOCK), F32),
            jax.ShapeDtypeStruct((N_DEV, SMALL_LANES), F32),
        ],
        scratch_shapes=[
            pltpu.VMEM((rows, KV_WIDTH), F32),
            pltpu.VMEM((rows, KV_WIDTH), F32),
            pltpu.VMEM((2, 4 * BLOCK, 1), F32),
        ],
        rider=rider,
    )


def _fold_heads(acc):
    t = acc + pltpu.roll(acc, HEAD_DIM, 1)
    out = t[:, :SMALL_LANES]
    for g in range(1, acc.shape[1] // SMALL_LANES):
        out = out + t[:, g * SMALL_LANES:(g + 1) * SMALL_LANES]
    return out


def _in_proj_bwd(dqn, dkn, dv, du, zqk, x, dh1, g_attn, gq_t, gk_t, w_in_t, rider=None):
    s = x.shape[0]
    ts = min(TOKEN_TILE, s)
    nt = s // ts

    def head_norm_bwd(d_n, raw, g_t, bmat):
        r = lax.rsqrt(_seg_mean(raw * raw, bmat) + EPS)
        gy = d_n * g_t
        d_raw = r * gy - raw * (r * r * r) * _seg_mean(gy * raw, bmat)
        return d_raw, jnp.sum(d_n * (raw * r), axis=0, keepdims=True)

    def body(dqn_ref, dkn_ref, dv_ref, du_ref, zqk_ref, x_ref, dh1_ref, g_ref, gq_ref, gk_ref, w_ref, bq_ref, bk_ref,
             gx_ref, dw_ref, dg_ref, dgq_ref, dgk_ref, dz_ref, gq_acc, gk_acc):
        i = pl.program_id(0)

        @pl.when(i == 0)
        def _():
            dw_ref[...] = jnp.zeros_like(dw_ref)
            dg_ref[...] = jnp.zeros_like(dg_ref)
            gq_acc[...] = jnp.zeros_like(gq_acc)
            gk_acc[...] = jnp.zeros_like(gk_acc)

        d_q, d_gq = head_norm_bwd(dqn_ref[...], zqk_ref[:, :ATTN_WIDTH], gq_ref[...], bq_ref[...])
        d_k, d_gk = head_norm_bwd(dkn_ref[...], zqk_ref[:, ATTN_WIDTH:], gk_ref[...], bk_ref[...])
        gq_acc[...] += d_gq
        gk_acc[...] += d_gk
        dz_ref[:, :ATTN_WIDTH] = d_q.astype(BF16)
        dz_ref[:, ATTN_WIDTH:ATTN_WIDTH + KV_WIDTH] = d_k.astype(BF16)
        dz_ref[:, ATTN_WIDTH + KV_WIDTH:ATTN_WIDTH + 2 * KV_WIDTH] = dv_ref[...].astype(BF16)
        dz_ref[:, ATTN_WIDTH + 2 * KV_WIDTH:] = du_ref[...].astype(BF16)
        dz = dz_ref[...]
        xf = x_ref[...]
        r = _rms(xf)
        hn = ((xf * r) * g_ref[...]).astype(BF16)
        dw_ref[...] += _tn(dz, hn)
        d_x, d_g = _rms_bwd(_nn(dz, w_ref[...]), xf, r, g_ref[...])
        dg_ref[...] += d_g
        gx_ref[...] = dh1_ref[...] + d_x

        @pl.when(i == nt - 1)
        def _():
            dgq_ref[...] = _fold_heads(gq_acc[...])
            dgk_ref[...] = _fold_heads(gk_acc[...])

    return _call(
        body,
        (dqn, dkn, dv, du, zqk, x, dh1, g_attn, gq_t, gk_t, w_in_t,
      _head_mean_matrix(ATTN_WIDTH), _head_mean_matrix(KV_WIDTH)),
        name="in_proj_bwd",
        grid=(nt,),
        in_specs=[
            _rows(ts, ATTN_WIDTH),
            _rows(ts, KV_WIDTH),
            _rows(ts, KV_WIDTH),
            _rows(ts, POOL_WIDTH),
            _rows(ts, ATTN_WIDTH + KV_WIDTH),
            _rows(ts, D_MODEL),
            _rows(ts, D_MODEL),
            _resident((1, D_MODEL)),
            _resident((1, ATTN_WIDTH)),
            _resident((1, KV_WIDTH)),
            _resident((IN_WIDTH, D_MODEL)),
            _resident((ATTN_WIDTH, ATTN_WIDTH)),
            _resident((KV_WIDTH, KV_WIDTH)),
        ],
        out_specs=[
            _rows(ts, D_MODEL),
            _acc((IN_WIDTH, D_MODEL)),
            _acc((1, D_MODEL)),
            _acc((1, SMALL_LANES)),
            _acc((1, SMALL_LANES)),
        ],
        out_shape=[
            jax.ShapeDtypeStruct((s, D_MODEL), F32),
            jax.ShapeDtypeStruct((IN_WIDTH, D_MODEL), F32),
            jax.ShapeDtypeStruct((1, D_MODEL), F32),
            jax.ShapeDtypeStruct((1, SMALL_LANES), F32),
            jax.ShapeDtypeStruct((1, SMALL_LANES), F32),
        ],
        scratch_shapes=[
            pltpu.VMEM((ts, IN_WIDTH), BF16),
            pltpu.VMEM((1, ATTN_WIDTH), F32),
            pltpu.VMEM((1, KV_WIDTH), F32),
        ],
        rider=rider,
    )


BIG_WEIGHTS = (
    ("w_in", True, IN_WIDTH // N_DEV, D_MODEL),
    ("w_out", False, D_MODEL // N_DEV, D_MODEL),
    ("w_gate", True, D_FF // N_DEV, D_MODEL),
    ("w_up", True, D_FF // N_DEV, D_MODEL),
    ("w_down", False, D_FF // N_DEV, D_MODEL),
    ("w_ple_gate", False, D_MODEL // N_DEV, D_MODEL),
    ("w_ple_proj", False, PLE_DIM, D_MODEL // N_DEV),
)
N_BIG = len(BIG_WEIGHTS)


def _place():
    x, y, c = lax.axis_index("x"), lax.axis_index("y"), lax.axis_index("c")
    chips = [(1 - x, y), (x, 1 - y), (1 - x, 1 - y)]
    return x, y, c, chips


class _Gather:
    def __init__(self, n, rows=None):
        self.n = n
        self.rows = rows or [None] * n
        self.sems = [pltpu.SemaphoreType.DMA((n, 7)), pltpu.SemaphoreType.DMA((n, 7)), pltpu.SemaphoreType.DMA((n,))]

    def _ctx(self, srcs, outs, sems):
        send_sems, recv_sems, local_sems = sems
        x, y, c, chips = _place()
        me, sibling = (x, y, c), (x, y, 1 - c)

        def part(k, ref):
            return ref if self.rows[k] is None else ref.at[pl.ds(*self.rows[k]), :]

        def block(k, owner):
            px, py, pc = owner
            return part(k, outs[k].at[4 * px + 2 * py + pc])

        def copy(k, idx, owner, to, mine=False):
            return pltpu.make_async_remote_copy(
                src_ref=part(k, srcs[k]) if mine else block(k, owner), dst_ref=block(k, owner),
                send_sem=send_sems.at[k, idx], recv_sem=recv_sems.at[k, idx], device_id=to, device_id_type=MESH)

        def local(k):
            return pltpu.make_async_copy(part(k, srcs[k]), block(k, me), local_sems.at[k])

        return c, chips, me, sibling, copy, local

    def begin(self, srcs, outs, sems):
        c, chips, me, sibling, copy, local = self._ctx(srcs, outs, sems)
        for k in range(self.n):
            local(k).start()
            copy(k, 0, me, sibling, mine=True).start()
            for j, chip in enumerate(chips):
                copy(k, 1 + j, me, (*chip, c), mine=True).start()

    def middle(self, srcs, outs, sems):
        c, chips, me, sibling, copy, local = self._ctx(srcs, outs, sems)
        for j, chip in enumerate(chips):
            for k in range(self.n):
                copy(k, 1 + j, (*chip, c), me).wait_recv()
                copy(k, 4 + j, (*chip, c), sibling).start()

    def end(self, srcs, outs, sems):
        c, chips, me, sibling, copy, local = self._ctx(srcs, outs, sems)
        for k in range(self.n):
            copy(k, 0, sibling, me).wait_recv()
            for j, chip in enumerate(chips):
                copy(k, 4 + j, (*chip, 1 - c), me).wait_recv()
        for k in range(self.n):
            copy(k, 0, me, sibling, mine=True).wait_send()
            for j, chip in enumerate(chips):
                copy(k, 1 + j, me, (*chip, c), mine=True).wait_send()
                copy(k, 4 + j, (*chip, c), sibling).wait_send()
            local(k).wait()


def _gather_rider(items):
    items = [it if isinstance(it, tuple) else (it, None, None, None) for it in items]
    n = len(items)
    g = _Gather(n, [None if r0 is None else (r0, nr) for _, r0, nr, _ in items])
    shapes = [jax.ShapeDtypeStruct((N_DEV, *sh.shape), sh.dtype) for sh, _, _, _ in items]
    stacks = [(k, st) for k, (_, _, _, st) in enumerate(items) if st is not None]
    aliases = {n + i: k for i, (k, _) in enumerate(stacks)}
    return _Rider([sh for sh, _, _, _ in items] + [st for _, st in stacks], shapes, g.sems, g.begin, g.end, g.middle,
                  aliases=aliases)


def _cast_and_gather_first(shards):
    g = _Gather(1)
    any_spec = pl.BlockSpec(memory_space=pl.ANY)
    vmem = pl.BlockSpec(memory_space=pltpu.VMEM)

    def body(*refs):
        ins, outs, gathered, sems = refs[:N_BIG], refs[N_BIG:2 * N_BIG], refs[2 * N_BIG], refs[2 * N_BIG + 1:]
        outs[0][...] = ins[0][...].astype(BF16)
        g.begin(outs[:1], [gathered], sems)
        for k in range(1, N_BIG):
            outs[k][...] = ins[k][...].astype(BF16)
        g.middle(outs[:1], [gathered], sems)
        g.end(outs[:1], [gathered], sems)

    res = pl.pallas_call(
        body,
        name="cast_and_gather_first",
        in_specs=[vmem] * N_BIG,
        out_specs=[vmem] * N_BIG + [any_spec],
        out_shape=[jax.ShapeDtypeStruct((r, c), BF16) for _, _, r, c in BIG_WEIGHTS]
        + [jax.ShapeDtypeStruct((N_DEV, *BIG_WEIGHTS[0][2:]), BF16)],
        scratch_shapes=g.sems,
    )(*shards)
    return list(res[:N_BIG]), res[N_BIG]


def _sibling_rider(grads):
    n = len(grads)

    def copies(gs, lands, sems):
        send_sems, recv_sems = sems
        x, y, c, _ = _place()
        return [
            pltpu.make_async_remote_copy(
                src_ref=gs[k].at[:, 1 - c], dst_ref=lands[k], send_sem=send_sems.at[k], recv_sem=recv_sems.at[k],
                device_id=(x, y, 1 - c), device_id_type=MESH)
            for k in range(n)
        ]

    def begin(gs, lands, sems):
        for cp in copies(gs, lands, sems):
            cp.start()

    def end(gs, lands, sems):
        for cp in copies(gs, lands, sems):
            cp.wait()

    shapes = [jax.ShapeDtypeStruct((N_CHIPS, *g.shape[2:]), F32) for g in grads]
    return _Rider(grads, shapes, [pltpu.SemaphoreType.DMA((n,)), pltpu.SemaphoreType.DMA((n,))], begin, end)


def _chip_sum(k, place, grad, from_sibling):
    _, _, r, c = BIG_WEIGHTS[k]

    def body(place_ref, g_ref, l_ref, own_ref, send_ref):
        q = pl.program_id(0)
        tot = g_ref[0, 0] + l_ref[0]
        mine = q == 2 * place_ref[0] + place_ref[1]

        @pl.when(mine)
        def _():
            own_ref[...] = tot

        send_ref[0] = jnp.where(mine, 0.0, tot).astype(BF16)

    return pl.pallas_call(
        body,
        name=f"chip_sum_{BIG_WEIGHTS[k][0]}",
        grid_spec=pltpu.PrefetchScalarGridSpec(
            num_scalar_prefetch=1,
            grid=(N_CHIPS,),
            in_specs=[
                pl.BlockSpec((1, 1, r, c), lambda q, place: (q, place[2], 0, 0)),
                pl.BlockSpec((1, r, c), lambda q, place: (q, 0, 0)),
            ],
            out_specs=[
                pl.BlockSpec((r, c), lambda q, place: (0, 0)),
                pl.BlockSpec((1, r, c), lambda q, place: (q, 0, 0)),
            ],
        ),
        out_shape=[jax.ShapeDtypeStruct((r, c), F32), jax.ShapeDtypeStruct((N_CHIPS, r, c), BF16)],
    )(place, grad, from_sibling)


def _chips_rider(to_send, small=None):
    n = len(to_send)
    inputs = list(to_send) + ([] if small is None else [small])
    shapes = [jax.ShapeDtypeStruct((3, *t.shape[1:]), BF16) for t in to_send]
    sems = [pltpu.SemaphoreType.DMA((max(n, 1), 3)), pltpu.SemaphoreType.DMA((max(n, 1), 3))]
    if small is not None:
        shapes.append(jax.ShapeDtypeStruct((N_DEV, *small.shape), F32))
        sems += [pltpu.SemaphoreType.DMA((7,)), pltpu.SemaphoreType.DMA((7,)), pltpu.SemaphoreType.DMA]

    def copies(ins, outs, sem_refs):
        x, y, c, chips = _place()
        out = []
        for k in range(n):
            for j, (px, py) in enumerate(chips):
                out.append(pltpu.make_async_remote_copy(
                    src_ref=ins[k].at[2 * px + py], dst_ref=outs[k].at[j],
                    send_sem=sem_refs[0].at[k, j], recv_sem=sem_refs[1].at[k, j],
                    device_id=(px, py, c), device_id_type=MESH))
        local = None
        if small is not None:
            me = 4 * x + 2 * y + c
            local = pltpu.make_async_copy(ins[n], outs[n].at[me], sem_refs[4])
            rel = 0
            for fx in (0, 1):
                for fy in (0, 1):
                    for fc in (0, 1):
                        if (fx, fy, fc) != (0, 0, 0):
                            out.append(pltpu.make_async_remote_copy(
                                src_ref=ins[n], dst_ref=outs[n].at[me],
                                send_sem=sem_refs[2].at[rel], recv_sem=sem_refs[3].at[rel],
                                device_id=(x ^ fx, y ^ fy, c ^ fc), device_id_type=MESH))
                            rel += 1
        return out, local

    def begin(ins, outs, sem_refs):
        remote, local = copies(ins, outs, sem_refs)
        if local is not None:
            local.start()
        for cp in remote:
            cp.start()

    def end(ins, outs, sem_refs):
        remote, local = copies(ins, outs, sem_refs)
        for cp in remote:
            cp.wait()
        if local is not None:
            local.wait()

    return _Rider(inputs, shapes, sems, begin, end)


def _exchange(name, rider):
    return _call(lambda: None, (), name=name, grid=(1,), in_specs=[], out_specs=[], out_shape=[], rider=rider)[1]


def _merge_riders(*riders):
    riders = [r for r in riders if r is not None]
    if len(riders) == 1:
        return riders[0]
    assert not any(r.aliases for r in riders)

    def split(refs, counts):
        out, at = [], 0
        for n in counts:
            out.append(refs[at:at + n])
            at += n
        return out

    def run(which):
        def fn(ins, outs, sems):
            parts = zip(riders, split(ins, [len(r.inputs) for r in riders]),
                        split(outs, [len(r.out_shapes) for r in riders]), split(sems, [len(r.sems) for r in riders]))
            for r, i, o, s in parts:
                hook = getattr(r, which)
                if hook is not None:
                    hook(i, o, s)
        return fn

    middle = run("middle") if any(r.middle is not None for r in riders) else None
    return _Rider(sum((r.inputs for r in riders), []), sum((r.out_shapes for r in riders), []),
                  sum((r.sems for r in riders), []), run("begin"), run("end"), middle)


def _split_outputs(outs, *riders):
    res, at = [], 0
    for r in riders:
        res.append(outs[at:at + len(r.out_shapes)])
        at += len(r.out_shapes)
    return res


def _adamw(w, g, m, v):
    m = ADAM_B1 * m + (1.0 - ADAM_B1) * g
    v = ADAM_B2 * v + (1.0 - ADAM_B2) * jnp.square(g)
    m_hat = m / (1.0 - ADAM_B1 ** ADAM_STEP)
    v_hat = v / (1.0 - ADAM_B2 ** ADAM_STEP)
    delta = -ADAM_LR * (m_hat / (jnp.sqrt(v_hat) + ADAM_EPS) + ADAM_WD * w)
    return delta, m, v


def _adamw_big(k, own, landed, w, m, v, rider=None):
    name, _, r, c = BIG_WEIGHTS[k]
    tile = r // 2
    tiles = lambda i: (i, 0)

    def body(own_ref, land_ref, w_ref, m_ref, v_ref, g_ref, d_ref, nm_ref, nv_ref):
        g = ((own_ref[...] + land_ref[0].astype(F32)) + land_ref[1].astype(F32)) + land_ref[2].astype(F32)
        g_ref[...] = g
        d_ref[...], nm_ref[...], nv_ref[...] = _adamw(w_ref[...], g, m_ref[...], v_ref[...])

    return _call(
        body,
        (own, landed, w, m, v),
        name=f"adamw_{name}",
        grid=(r // tile,),
        in_specs=[pl.BlockSpec((tile, c), tiles), pl.BlockSpec((3, tile, c), lambda i: (0, i, 0))]
        + [pl.BlockSpec((tile, c), tiles)] * 3,
        out_specs=[pl.BlockSpec((tile, c), tiles)] * 4,
        out_shape=[jax.ShapeDtypeStruct((r, c), F32)] * 4,
        rider=rider,
    )


def _sum_small(parts_list):
    n = len(parts_list)

    def body(*refs):
        for p_ref, out_ref in zip(refs[:n], refs[n:]):
            tot = p_ref[0]
            for j in range(1, N_DEV):
                tot = tot + p_ref[j]
            out_ref[...] = tot

    return pl.pallas_call(body, name="sum_small",
                          out_shape=[jax.ShapeDtypeStruct(p.shape[1:], F32) for p in parts_list])(*parts_list)


def _adamw_small(grads, ws, ms, vs):
    n = len(grads)

    def body(*refs):
        g_refs, w_refs, m_refs, v_refs = refs[:n], refs[n:2 * n], refs[2 * n:3 * n], refs[3 * n:4 * n]
        outs = refs[4 * n:]
        for i in range(n):
            d, nm, nv = _adamw(w_refs[i][...], g_refs[i][...], m_refs[i][...], v_refs[i][...])
            outs[i][...] = d
            outs[n + i][...] = nm
            outs[2 * n + i][...] = nv

    shapes = [jax.ShapeDtypeStruct(w.shape, F32) for w in ws]
    return pl.pallas_call(body, name="adamw_small", out_shape=shapes * 3)(*grads, *ws, *ms, *vs)


SMALL_NAMES = ("g_attn_norm", "g_q", "g_k", "attn_sinks", "rel_bias", "w_pool", "pool_scale", "g_ffn_norm", "g_ple_norm")


def _pack_small(arrays):
    rows, offsets = [], []
    at = 0
    for a in arrays:
        flat = a.reshape(-1)
        n_rows = -(-flat.shape[0] // (8 * SMALL_LANES)) * 8
        flat = jnp.pad(flat, (0, n_rows * SMALL_LANES - flat.shape[0]))
        rows.append(flat.reshape(n_rows, SMALL_LANES))
        offsets.append(at)
        at += n_rows
    return jnp.concatenate(rows, axis=0), offsets


def kernel(x, p, w_in, w_out, g_attn_norm, g_q, g_k, attn_sinks, rel_bias, w_pool, pool_scale, g_ffn_norm, w_gate, w_up, w_down, g_ple_norm, w_ple_gate, w_ple_proj, loss_target, m_w_in, m_w_out, m_g_attn_norm, m_g_q, m_g_k, m_attn_sinks, m_rel_bias, m_w_pool, m_pool_scale, m_g_ffn_norm, m_w_gate, m_w_up, m_w_down, m_g_ple_norm, m_w_ple_gate, m_w_ple_proj, v_w_in, v_w_out, v_g_attn_norm, v_g_q, v_g_k, v_attn_sinks, v_rel_bias, v_w_pool, v_pool_scale, v_g_ffn_norm, v_w_gate, v_w_up, v_w_down, v_g_ple_norm, v_w_ple_gate, v_w_ple_proj):
    weights = dict(w_in=w_in, w_out=w_out, g_attn_norm=g_attn_norm, g_q=g_q, g_k=g_k, attn_sinks=attn_sinks,
                   rel_bias=rel_bias, w_pool=w_pool, pool_scale=pool_scale, g_ffn_norm=g_ffn_norm, w_gate=w_gate,
                   w_up=w_up, w_down=w_down, g_ple_norm=g_ple_norm, w_ple_gate=w_ple_gate, w_ple_proj=w_ple_proj)
    m_in = dict(w_in=m_w_in, w_out=m_w_out, g_attn_norm=m_g_attn_norm, g_q=m_g_q, g_k=m_g_k, attn_sinks=m_attn_sinks,
                rel_bias=m_rel_bias, w_pool=m_w_pool, pool_scale=m_pool_scale, g_ffn_norm=m_g_ffn_norm, w_gate=m_w_gate,
                w_up=m_w_up, w_down=m_w_down, g_ple_norm=m_g_ple_norm, w_ple_gate=m_w_ple_gate, w_ple_proj=m_w_ple_proj)
    v_in = dict(w_in=v_w_in, w_out=v_w_out, g_attn_norm=v_g_attn_norm, g_q=v_g_q, g_k=v_g_k, attn_sinks=v_attn_sinks,
                rel_bias=v_rel_bias, w_pool=v_w_pool, pool_scale=v_pool_scale, g_ffn_norm=v_g_ffn_norm, w_gate=v_w_gate,
                w_up=v_w_up, w_down=v_w_down, g_ple_norm=v_g_ple_norm, w_ple_gate=v_w_ple_gate, w_ple_proj=v_w_ple_proj)

    xs = x[0]
    ps = p[0, 0]
    target = loss_target[0]
    wp = w_pool[0]
    gq_t = jnp.tile(g_q, (1, ATTN_WIDTH // HEAD_DIM))
    gk_t = jnp.tile(g_k, (1, KV_WIDTH // HEAD_DIM))

    def to_blocks(k, arr):
        return jnp.swapaxes(arr[0], 0, 1) if BIG_WEIGHTS[k][1] else arr[0]

    def from_blocks(k, arr):
        return (jnp.swapaxes(arr, 0, 1) if BIG_WEIGHTS[k][1] else arr)[None]

    IN, OUT, GATE, UP, DOWN, PG, PP = range(N_BIG)
    full = lambda g: g.reshape(N_DEV * g.shape[1], g.shape[2])
    halves = lambda k, g: g.reshape(N_CHIPS, 2, *BIG_WEIGHTS[k][2:])
    place = jnp.stack([lax.axis_index("x"), lax.axis_index("y"), lax.axis_index("c")]).astype(jnp.int32)

    sh, w_in_g = _cast_and_gather_first([to_blocks(k, weights[name]) for k, (name, _, _, _) in enumerate(BIG_WEIGHTS)])
    w_in_t = full(w_in_g)

    ffn_rows = BIG_WEIGHTS[GATE][2]
    tab = _bias_table(rel_bias.T)
    (zqk, qn, kn, v, u), (w_out_g, wg_g) = _in_proj(
        xs, g_attn_norm, w_in_t, gq_t, gk_t, rider=_gather_rider([sh[OUT], (sh[GATE], 0, GATE_ROWS_EARLY, None)]))
    (a,), (wg_g, wu_g) = _attn_fwd(qn, kn, v, tab, attn_sinks, rider=_gather_rider([
        (sh[GATE], GATE_ROWS_EARLY, ffn_rows - GATE_ROWS_EARLY, wg_g), (sh[UP], 0, UP_ROWS_EARLY, None)]))
    w_out_f = full(w_out_g)
    (h1, hn2, m_out), (wu_g,) = _mix_out(u, a, xs, w_out_f, wp, pool_scale, g_ffn_norm, rider=_gather_rider([
        (sh[UP], UP_ROWS_EARLY, ffn_rows - UP_ROWS_EARLY, wu_g)]))
    wg_t, wu_t = full(wg_g), full(wu_g)
    (gt, up), (wd_g, w_pg_g, w_pp_g) = _ffn_up(hn2, wg_t, wu_t, rider=_gather_rider([sh[DOWN], sh[PG], sh[PP]]))
    w_down_f = full(wd_g)

    sums, landed = [None] * N_BIG, [None] * N_BIG

    def chip_sum(k, grad, from_sibling):
        sums[k] = _chip_sum(k, place, halves(k, grad), from_sibling)

    (loss_part, dh2, d_wpg, d_wpp, d_g_ple), _ = _ffn_down_ple(
        gt, up, h1, w_down_f, ps, target, g_ple_norm, full(w_pg_g), w_pp_g)
    (dgt, dup, dh1, dh1b, d_g_ffn, d_wd), sib = _ffn_bwd_act(
        dh2, h1, gt, up, g_ffn_norm, wg_t, wu_t, w_down_f, rider=_sibling_rider([halves(PG, d_wpg), halves(PP, d_wpp)]))
    chip_sum(PG, d_wpg, sib[0])
    chip_sum(PP, d_wpp, sib[1])
    r_sib, r_chips = _sibling_rider([halves(DOWN, d_wd)]), _chips_rider([sums[PG][1], sums[PP][1]])
    (d_wo,), outs = _out_w_bwd(a, m_out, dh1b, rider=_merge_riders(r_sib, r_chips))
    sib, (landed[PG], landed[PP]) = _split_outputs(outs, r_sib, r_chips)
    chip_sum(DOWN, d_wd, sib[0])
    r_sib, r_chips = _sibling_rider([halves(OUT, d_wo)]), _chips_rider([sums[DOWN][1]])
    (d_wg_t,), outs = _ffn_bwd_w("gate", dgt, hn2, rider=_merge_riders(r_sib, r_chips))
    sib, (landed[DOWN],) = _split_outputs(outs, r_sib, r_chips)
    chip_sum(OUT, d_wo, sib[0])
    r_sib, r_chips = _sibling_rider([halves(GATE, d_wg_t)]), _chips_rider([sums[OUT][1]])
    (d_wu_t,), outs = _ffn_bwd_w("up", dup, hn2, rider=_merge_riders(r_sib, r_chips))
    sib, (landed[OUT],) = _split_outputs(outs, r_sib, r_chips)
    chip_sum(GATE, d_wg_t, sib[0])
    r_sib, r_chips = _sibling_rider([halves(UP, d_wu_t)]), _chips_rider([sums[GATE][1]])
    (da, du, d_wpool, d_scale), outs = _mix_bwd(dh1b, u, w_out_f, wp, pool_scale, rider=_merge_riders(r_sib, r_chips))
    sib, (landed[GATE],) = _split_outputs(outs, r_sib, r_chips)
    chip_sum(UP, d_wu_t, sib[0])
    early, early_at = _pack_small([d_wpool, d_scale, d_g_ffn, d_g_ple, loss_part[:, :1]])
    (dqn, dkn, dv, dl_acc, d_sinks), (landed[UP], early_all) = _attn_bwd(
        qn, kn, v, a, da, tab, attn_sinks, rider=_chips_rider([sums[UP][1]], early))
    (grad_x, d_win_t, d_g_attn, d_gq, d_gk), _ = _in_proj_bwd(dqn, dkn, dv, du, zqk, xs, dh1, g_attn_norm, gq_t, gk_t, w_in_t)
    (d_rel_t,), sib = _bias_table_bwd(dl_acc, rider=_sibling_rider([halves(IN, d_win_t)]))
    chip_sum(IN, d_win_t, sib[0])
    late, late_at = _pack_small([d_g_attn, d_gq[:, :HEAD_DIM], d_gk[:, :HEAD_DIM], d_sinks[:, 0], d_rel_t[:, :N_BUCKETS]])
    landed[IN], late_all = _exchange("last_exchange", _chips_rider([sums[IN][1]], late))

    out = {"grad": {}, "delta": {}, "new_m": {}, "new_v": {}}
    for k, (name, _, _, _) in enumerate(BIG_WEIGHTS):
        res, _ = _adamw_big(k, sums[k][0], landed[k], to_blocks(k, weights[name]), to_blocks(k, m_in[name]),
                            to_blocks(k, v_in[name]))
        for kind, r in zip(("grad", "delta", "new_m", "new_v"), res):
            out[kind][name] = from_blocks(k, r)
    early_sum, late_sum = _sum_small([early_all, late_all])

    def unpack(packed, at, shape):
        n = math.prod(shape)
        return packed[at:at + -(-n // SMALL_LANES)].reshape(-1)[:n].reshape(shape)

    small_grads = dict(
        w_pool=unpack(early_sum, early_at[0], w_pool.shape), pool_scale=unpack(early_sum, early_at[1], pool_scale.shape),
        g_ffn_norm=unpack(early_sum, early_at[2], g_ffn_norm.shape), g_ple_norm=unpack(early_sum, early_at[3], g_ple_norm.shape),
        g_attn_norm=unpack(late_sum, late_at[0], g_attn_norm.shape), g_q=unpack(late_sum, late_at[1], g_q.shape),
        g_k=unpack(late_sum, late_at[2], g_k.shape), attn_sinks=unpack(late_sum, late_at[3], attn_sinks.shape),
        rel_bias=unpack(late_sum, late_at[4], rel_bias.T.shape))
    loss = early_sum[early_at[4], 0]
    flip = lambda name, arr: arr.T if name == "rel_bias" else arr
    updates = _adamw_small([small_grads[n] for n in SMALL_NAMES], [flip(n, weights[n]) for n in SMALL_NAMES],
                           [flip(n, m_in[n]) for n in SMALL_NAMES], [flip(n, v_in[n]) for n in SMALL_NAMES])
    n_small = len(SMALL_NAMES)
    for i, name in enumerate(SMALL_NAMES):
        out["grad"][name] = flip(name, small_grads[name])
        out["delta"][name] = flip(name, updates[i])
        out["new_m"][name] = flip(name, updates[n_small + i])
        out["new_v"][name] = flip(name, updates[2 * n_small + i])

    order = ("w_in", "w_out", "g_attn_norm", "g_q", "g_k", "attn_sinks", "rel_bias", "w_pool", "pool_scale",
             "g_ffn_norm", "w_gate", "w_up", "w_down", "g_ple_norm", "w_ple_gate", "w_ple_proj")
    return (loss, grad_x[None], *[out["grad"][n] for n in order], *[out["delta"][n] for n in order],
            *[out["new_m"][n] for n in order], *[out["new_v"][n] for n in order])
```

```python
import functools
import math

import jax
import jax.numpy as jnp
import numpy as np
from jax import lax
from jax.experimental import pallas as pl
from jax.experimental.pallas import tpu as pltpu
from jax.experimental.pallas import tpu_sc as plsc

F32 = jnp.float32
BF16 = jnp.bfloat16
MESH = pl.DeviceIdType.MESH

D_MODEL = 1024
HEAD_DIM = 64
ATTN_WIDTH = 512
KV_WIDTH = 128
POOL_WIDTH = 512
POOL_SIZES = (2, 4, 8, 16)
POOL_GROUP = 128
POOL_HALO = 16
IN_WIDTH = 1280
D_FF = 2816
PLE_DIM = 256
BLOCK = 128
N_BUCKETS = 32
MAX_DISTANCE = 128
EPS = 1e-6
N_DEV = 8
N_CHIPS = 4

ADAM_LR = 0.001
ADAM_B1 = 0.9
ADAM_B2 = 0.999
ADAM_EPS = 1e-08
ADAM_WD = 0.01
ADAM_STEP = 10

TOKEN_TILE = 512
FFN_BWD_TILE = 256
FF_CHUNK = 256
ATTN_STEP_BLOCKS = 4
GATE_ROWS_EARLY = 96
UP_ROWS_EARLY = 64
HEADS_A = (0, 2, 5, 7)
HEADS_B = (1, 3, 4, 6)
SMALL_LANES = 128


def _nn(a, b):
    return jnp.dot(a, b, preferred_element_type=F32)


def _nt(a, b):
    return lax.dot_general(a, b, (((1,), (1,)), ((), ())), preferred_element_type=F32)


def _tn(a, b):
    return lax.dot_general(a, b, (((0,), (0,)), ((), ())), preferred_element_type=F32)


def _resident(shape):
    nd = len(shape)
    return pl.BlockSpec(shape, lambda i, _nd=nd: (0,) * _nd, pipeline_mode=pl.Buffered(1))


def _rows(tile, width):
    return pl.BlockSpec((tile, width), lambda i: (i, 0))


def _acc(shape):
    nd = len(shape)
    return pl.BlockSpec(shape, lambda i, _nd=nd: (0,) * _nd)


def _head_mean_matrix(width):
    idx = np.arange(width) // HEAD_DIM
    return jnp.asarray((idx[:, None] == idx[None, :]).astype(np.float32) / HEAD_DIM, dtype=BF16)


def _seg_mean(v, bmat):
    hi = v.astype(BF16)
    lo = (v - hi.astype(F32)).astype(BF16)
    return _nn(hi, bmat) + _nn(lo, bmat)


def _rms(x):
    return lax.rsqrt(jnp.mean(x * x, axis=-1, keepdims=True) + EPS)


def _rms_bwd(d_y, x, r, g):
    gy = d_y * g
    d_x = r * gy - x * (r * r * r) * jnp.mean(gy * x, axis=-1, keepdims=True)
    d_g = jnp.sum(d_y * (x * r), axis=0, keepdims=True)
    return d_x, d_g


def _lane_lo(shape):
    return lax.broadcasted_iota(jnp.int32, shape, 1) < HEAD_DIM


class _Rider:
    def __init__(self, inputs, out_shapes, sems, begin, end, middle=None, aliases=None):
        self.inputs, self.out_shapes, self.sems = list(inputs), list(out_shapes), list(sems)
        self.begin, self.middle, self.end = begin, middle, end
        self.aliases = dict(aliases or {})


_issued = []


def _after_last(args, in_specs):
    extra = list(_issued)
    return list(args) + extra, list(in_specs) + [pl.BlockSpec(memory_space=pl.ANY)] * len(extra), len(extra)


def _mark_issued(out):
    _issued[:] = [out]


def _complete_before_next(arrays):
    _issued.extend(arrays)


def _call(body, args, *, name, grid, in_specs, out_specs, out_shape, scratch_shapes=(), rider=None):
    in_specs, out_specs, out_shape, scratch_shapes = list(in_specs), list(out_specs), list(out_shape), list(scratch_shapes)
    if rider is None:
        n_args = len(args)
        args, in_specs, _ = _after_last(args, in_specs)

        def ordered(*refs):
            body(*refs[:n_args], *refs[len(args):])

        outs = pl.pallas_call(ordered, name=name, grid=grid, in_specs=in_specs, out_specs=out_specs, out_shape=out_shape,
                              scratch_shapes=scratch_shapes)(*args)
        _mark_issued(outs[0])
        return list(outs), []
    n_in, n_out, n_scr = len(in_specs), len(out_shape), len(scratch_shapes)
    r_in, r_out = len(rider.inputs), len(rider.out_shapes)
    n_steps = grid[0]

    def hosted(*refs):
        ins, refs = refs[:n_in], refs[n_in:]
        r_ins, refs = refs[:r_in], refs[r_in:]
        outs, refs = refs[:n_out], refs[n_out:]
        r_outs, refs = refs[:r_out], refs[r_out:]
        scratch, r_sems = refs[:n_scr], refs[n_scr:]
        step = pl.program_id(0)

        @pl.when(step == 0)
        def _():
            rider.begin(r_ins, r_outs, r_sems)

        if rider.middle is not None:
            @pl.when(step == n_steps - 1)
            def _():
                rider.middle(r_ins, r_outs, r_sems)

        body(*ins, *outs, *scratch)

        @pl.when(step == n_steps - 1)
        def _():
            rider.end(r_ins, r_outs, r_sems)

    any_spec = pl.BlockSpec(memory_space=pl.ANY)
    outs = pl.pallas_call(
        hosted, name=name, grid=grid,
        in_specs=in_specs + [any_spec] * r_in,
        out_specs=out_specs + [any_spec] * r_out,
        out_shape=out_shape + rider.out_shapes,
        scratch_shapes=scratch_shapes + rider.sems,
        input_output_aliases={n_in + i: n_out + o for i, o in rider.aliases.items()},
    )(*args, *rider.inputs)
    return list(outs[:n_out]), list(outs[n_out:])


def _in_proj(x, g_attn, w_in_t, gq_t, gk_t, rider=None):
    s = x.shape[0]
    ts = min(TOKEN_TILE, s)

    def body(x_ref, g_ref, w_ref, gq_ref, gk_ref, bq_ref, bk_ref, zqk_ref, qn_ref, kn_ref, v_ref, u_ref):
        xf = x_ref[...]
        hn = ((xf * _rms(xf)) * g_ref[...]).astype(BF16)
        z = _nt(hn, w_ref[...])
        q = z[:, :ATTN_WIDTH]
        k = z[:, ATTN_WIDTH:ATTN_WIDTH + KV_WIDTH]
        zqk_ref[...] = z[:, :ATTN_WIDTH + KV_WIDTH]
        rq = lax.rsqrt(_seg_mean(q * q, bq_ref[...]) + EPS)
        qn_ref[...] = ((q * rq) * gq_ref[...]).astype(BF16)
        rk = lax.rsqrt(_seg_mean(k * k, bk_ref[...]) + EPS)
        kn_ref[...] = ((k * rk) * gk_ref[...]).astype(BF16)
        v_ref[...] = z[:, ATTN_WIDTH + KV_WIDTH:ATTN_WIDTH + 2 * KV_WIDTH].astype(BF16)
        u_ref[...] = z[:, ATTN_WIDTH + 2 * KV_WIDTH:]

    return _call(
        body,
        (x, g_attn, w_in_t, gq_t, gk_t, _head_mean_matrix(ATTN_WIDTH), _head_mean_matrix(KV_WIDTH)),
        name="in_proj",
        grid=(s // ts,),
        in_specs=[
            _rows(ts, D_MODEL),
            _resident((1, D_MODEL)),
            _resident((IN_WIDTH, D_MODEL)),
            _resident((1, ATTN_WIDTH)),
            _resident((1, KV_WIDTH)),
            _resident((ATTN_WIDTH, ATTN_WIDTH)),
            _resident((KV_WIDTH, KV_WIDTH)),
        ],
        out_specs=[
            _rows(ts, ATTN_WIDTH + KV_WIDTH),
            _rows(ts, ATTN_WIDTH),
            _rows(ts, KV_WIDTH),
            _rows(ts, KV_WIDTH),
            _rows(ts, POOL_WIDTH),
        ],
        out_shape=[
            jax.ShapeDtypeStruct((s, ATTN_WIDTH + KV_WIDTH), F32),
            jax.ShapeDtypeStruct((s, ATTN_WIDTH), BF16),
            jax.ShapeDtypeStruct((s, KV_WIDTH), BF16),
            jax.ShapeDtypeStruct((s, KV_WIDTH), BF16),
            jax.ShapeDtypeStruct((s, POOL_WIDTH), F32),
        ],
        rider=rider,
    )


def _bucket_ranges():
    n = np.arange(MAX_DISTANCE)
    max_exact = N_BUCKETS // 2
    nf = np.maximum(n, 1).astype(np.float64)
    large = max_exact + (np.log(nf / max_exact) / math.log(MAX_DISTANCE / max_exact) * (N_BUCKETS - max_exact)).astype(np.int64)
    bucket = np.where(n < max_exact, n, np.minimum(large, N_BUCKETS - 1))
    out = []
    for b in range(N_BUCKETS):
        idx = np.nonzero(bucket == b)[0]
        out.append((int(idx.min()), int(idx.max()) + 1))
    return out


def _band_distance():
    i = lax.broadcasted_iota(jnp.int32, (BLOCK, 2 * BLOCK), 0)
    j = lax.broadcasted_iota(jnp.int32, (BLOCK, 2 * BLOCK), 1)
    return BLOCK + i - j


def _bias_table(rel_bias_t):
    ranges = _bucket_ranges()

    def body(rb_ref, tab_ref):
        d = _band_distance()
        for half, heads in enumerate((HEADS_A, HEADS_B)):
            for slot, h in enumerate(heads):
                t = jnp.full((BLOCK, 2 * BLOCK), -jnp.inf, F32)
                for b, (lo, hi) in enumerate(ranges):
                    t = jnp.where((d >= lo) & (d < hi), rb_ref[h, b], t)
                tab_ref[half, slot * BLOCK:(slot + 1) * BLOCK, :] = t

    return pl.pallas_call(
        body,
        name="bias_table",
        in_specs=[pl.BlockSpec(memory_space=pltpu.SMEM)],
        out_shape=jax.ShapeDtypeStruct((2, 4 * BLOCK, 2 * BLOCK), F32),
    )(rel_bias_t)


def _bias_table_bwd(dl_acc, rider=None):
    ranges = _bucket_ranges()
    n_heads = len(HEADS_A) + len(HEADS_B)

    def body(dl_ref, out_ref):
        d = _band_distance()
        row = lax.broadcasted_iota(jnp.int32, (n_heads, SMALL_LANES), 0)
        lane = lax.broadcasted_iota(jnp.int32, (n_heads, SMALL_LANES), 1)
        out = jnp.zeros((n_heads, SMALL_LANES), F32)
        for b, (lo, hi) in enumerate(ranges):
            in_bucket = (d >= lo) & (d < hi)
            for half, heads in enumerate((HEADS_A, HEADS_B)):
                for slot, h in enumerate(heads):
                    g = dl_ref[half, slot * BLOCK:(slot + 1) * BLOCK, :]
                    part = jnp.sum(jnp.where(in_bucket, g, 0.0), axis=0, keepdims=True)
                    tot = jnp.sum(part, axis=1, keepdims=True)
                    out = jnp.where((row == h) & (lane == b), tot, out)
        out_ref[...] = out

    return _call(
        body,
        (dl_acc,),
        name="bias_table_bwd",
        grid=(1,),
        in_specs=[_acc((2, 4 * BLOCK, 2 * BLOCK))],
        out_specs=[_acc((n_heads, SMALL_LANES))],
        out_shape=[jax.ShapeDtypeStruct((n_heads, SMALL_LANES), F32)],
        rider=rider,
    )


def _stack_heads(pairs, lo_mask):
    zero = jnp.zeros_like(pairs[0])
    lo = [jnp.where(lo_mask, t, zero) for t in pairs]
    hi = [jnp.where(lo_mask, zero, t) for t in pairs]
    return (jnp.concatenate([lo[0], lo[1], hi[2], hi[3]], axis=0),
            jnp.concatenate([hi[0], hi[1], lo[2], lo[3]], axis=0))


def _unstack_heads(out_a, out_b, lo_mask):
    t = lambda x, r: x[r * BLOCK:(r + 1) * BLOCK, :]
    return [
        jnp.where(lo_mask, t(out_a, 0), t(out_b, 0)),
        jnp.where(lo_mask, t(out_a, 1), t(out_b, 1)),
        jnp.where(lo_mask, t(out_b, 2), t(out_a, 2)),
        jnp.where(lo_mask, t(out_b, 3), t(out_a, 3)),
    ]


def _sink_column(sink_ref, heads):
    row = lax.broadcasted_iota(jnp.int32, (4 * BLOCK, 1), 0)
    col = jnp.full((4 * BLOCK, 1), sink_ref[0, heads[3]], F32)
    for slot in (2, 1, 0):
        col = jnp.where(row < (slot + 1) * BLOCK, sink_ref[0, heads[slot]], col)
    return col


def _band_probs(q_stack, keys, tab, sink, first_block):
    s = _nt(q_stack, keys) * (HEAD_DIM ** -0.5) + tab
    if first_block is not None:
        col = lax.broadcasted_iota(jnp.int32, s.shape, 1)
        s = jnp.where(jnp.logical_and(first_block, col < BLOCK), -jnp.inf, s)
    m = jnp.maximum(jnp.max(s, axis=-1, keepdims=True), sink)
    e = jnp.exp(s - m)
    e_sink = jnp.exp(sink - m)
    den = jnp.sum(e, axis=-1, keepdims=True) + e_sink
    return e / den, e_sink / den


def _attn_specs(n_groups):
    group = lambda n: (jnp.minimum(n, n_groups - 1), 0)
    prev = lambda n: (jnp.maximum(jnp.minimum(n, n_groups - 1) * ATTN_STEP_BLOCKS - 1, 0), 0)
    return group, prev


def _band(prev_ref, group_ref, b):
    rows = lambda i: group_ref[i * BLOCK:(i + 1) * BLOCK, :]
    band = jnp.concatenate([prev_ref[...] if b == 0 else rows(b - 1), rows(b)], axis=0)
    return band, pltpu.roll(band, HEAD_DIM, 1)


def _attn_fwd(qn, kn, v, tab, sinks, rider=None):
    s = qn.shape[0]
    n_groups = s // (ATTN_STEP_BLOCKS * BLOCK)
    group, prev = _attn_specs(n_groups)
    rows = ATTN_STEP_BLOCKS * BLOCK

    def body(sink_ref, q_ref, kc_ref, kp_ref, vc_ref, vp_ref, tab_ref, o_ref):
        first = pl.program_id(0) == 0
        lo_mask = _lane_lo((BLOCK, BLOCK))
        for b in range(ATTN_STEP_BLOCKS):
            at = slice(b * BLOCK, (b + 1) * BLOCK)
            kk, kk_sw = _band(kp_ref, kc_ref, b)
            vv, vv_sw = _band(vp_ref, vc_ref, b)
            q_a, q_b = _stack_heads([q_ref[at, p * BLOCK:(p + 1) * BLOCK] for p in range(4)], lo_mask)
            no_prev = first if b == 0 else None
            p_a, _ = _band_probs(q_a, kk, tab_ref[0], _sink_column(sink_ref, HEADS_A), no_prev)
            p_b, _ = _band_probs(q_b, kk_sw, tab_ref[1], _sink_column(sink_ref, HEADS_B), no_prev)
            out = _unstack_heads(_nn(p_a.astype(BF16), vv), _nn(p_b.astype(BF16), vv_sw), lo_mask)
            for p in range(4):
                o_ref[at, p * BLOCK:(p + 1) * BLOCK] = out[p].astype(BF16)

    return _call(
        body,
        (sinks, qn, kn, kn, v, v, tab),
        name="attn_fwd",
        grid=(n_groups,),
        in_specs=[
            pl.BlockSpec(memory_space=pltpu.SMEM),
            pl.BlockSpec((rows, ATTN_WIDTH), group),
            pl.BlockSpec((rows, KV_WIDTH), group),
            pl.BlockSpec((BLOCK, KV_WIDTH), prev),
            pl.BlockSpec((rows, KV_WIDTH), group),
            pl.BlockSpec((BLOCK, KV_WIDTH), prev),
            _resident((2, 4 * BLOCK, 2 * BLOCK)),
        ],
        out_specs=[pl.BlockSpec((rows, ATTN_WIDTH), group)],
        out_shape=[jax.ShapeDtypeStruct((s, ATTN_WIDTH), BF16)],
        rider=rider,
    )


def _pooled(u_tile, u_halo, tile_index, tile_rows):
    halo = jnp.where(tile_index > 0, u_halo, 0.0)
    ext = jnp.concatenate([halo, u_tile], axis=0)
    sums = []
    acc = ext
    for shift in (1, 2, 4, 8):
        acc = acc + pltpu.roll(acc, shift, 0)
        sums.append(acc)
    t = tile_index * tile_rows + lax.broadcasted_iota(jnp.int32, (tile_rows, 1), 0)
    out = []
    for g, w in enumerate(POOL_SIZES):
        lanes = slice(g * POOL_GROUP, (g + 1) * POOL_GROUP)
        cnt = jnp.minimum(t + 1, w).astype(F32)
        out.append(sums[g][POOL_HALO:, lanes] / cnt - u_tile[:, lanes])
    return out


def _halo_before(tile):
    return lambda i: (jnp.maximum(i * (tile // POOL_HALO) - 1, 0), 0)


def _mix_out(u, a, x, w_out, w_pool, pool_scale, g_ffn, rider=None):
    s = x.shape[0]
    ts = min(TOKEN_TILE, s)

    def body(u_ref, uh_ref, a_ref, x_ref, wo_ref, wp_ref, sc_ref, g_ref, h1_ref, hn_ref, m_ref):
        i = pl.program_id(0)
        pooled = _pooled(u_ref[...], uh_ref[...], i, ts)
        for g in range(len(POOL_SIZES)):
            lanes = slice(g * POOL_GROUP, (g + 1) * POOL_GROUP)
            y = _nn(pooled[g].astype(BF16), wp_ref[g].astype(BF16))
            m_ref[:, lanes] = (y * sc_ref[:, lanes]).astype(BF16)
        h1 = x_ref[...] + _nn(a_ref[...], wo_ref[:ATTN_WIDTH, :]) + _nn(m_ref[...], wo_ref[ATTN_WIDTH:, :])
        h1_ref[...] = h1
        hn_ref[...] = ((h1 * _rms(h1)) * g_ref[...]).astype(BF16)

    return _call(
        body,
        (u, u, a, x, w_out, w_pool, pool_scale, g_ffn),
        name="mix_out",
        grid=(s // ts,),
        in_specs=[
            _rows(ts, POOL_WIDTH),
            pl.BlockSpec((POOL_HALO, POOL_WIDTH), _halo_before(ts)),
            _rows(ts, ATTN_WIDTH),
            _rows(ts, D_MODEL),
            _resident((D_MODEL, D_MODEL)),
            _resident((len(POOL_SIZES), POOL_GROUP, POOL_GROUP)),
            _resident((1, POOL_WIDTH)),
            _resident((1, D_MODEL)),
        ],
        out_specs=[_rows(ts, D_MODEL), _rows(ts, D_MODEL), _rows(ts, POOL_WIDTH)],
        out_shape=[
            jax.ShapeDtypeStruct((s, D_MODEL), F32),
            jax.ShapeDtypeStruct((s, D_MODEL), BF16),
            jax.ShapeDtypeStruct((s, POOL_WIDTH), BF16),
        ],
        rider=rider,
    )


def _ffn_up(hn2, wg_t, wu_t, rider=None):
    s = hn2.shape[0]
    ts = min(TOKEN_TILE, s)

    def body(hn_ref, wg_ref, wu_ref, gt_ref, up_ref):
        hn = hn_ref[...]
        for c in range(D_FF // FF_CHUNK):
            cols = slice(c * FF_CHUNK, (c + 1) * FF_CHUNK)
            gt_ref[:, cols] = _nt(hn, wg_ref[cols, :]).astype(BF16)
            up_ref[:, cols] = _nt(hn, wu_ref[cols, :]).astype(BF16)

    return _call(
        body,
        (hn2, wg_t, wu_t),
        name="ffn_up",
        grid=(s // ts,),
        in_specs=[_rows(ts, D_MODEL), _resident((D_FF, D_MODEL)), _resident((D_FF, D_MODEL))],
        out_specs=[_rows(ts, D_FF), _rows(ts, D_FF)],
        out_shape=[jax.ShapeDtypeStruct((s, D_FF), BF16), jax.ShapeDtypeStruct((s, D_FF), BF16)],
        rider=rider,
    )


def _silu_mul(gt, up):
    return (gt * jax.nn.sigmoid(gt)) * up


def _ffn_down_ple(gt, up, h1, w_down, p, target, g_ple, w_pg, w_pp, rider=None):
    s = h1.shape[0]
    ts = min(TOKEN_TILE, s)
    blk = D_MODEL // N_DEV

    def body(gt_ref, up_ref, h1_ref, wd_ref, p_ref, t_ref, g_ref, wpg_ref, wpp_ref,
             loss_ref, dh_ref, dwpg_ref, dwpp_ref, dg_ref, act_ref, pp_ref):
        @pl.when(pl.program_id(0) == 0)
        def _():
            loss_ref[...] = jnp.zeros_like(loss_ref)
            dwpg_ref[...] = jnp.zeros_like(dwpg_ref)
            dwpp_ref[...] = jnp.zeros_like(dwpp_ref)
            dg_ref[...] = jnp.zeros_like(dg_ref)

        for c in range(D_FF // FF_CHUNK):
            cols = slice(c * FF_CHUNK, (c + 1) * FF_CHUNK)
            act_ref[:, cols] = _silu_mul(gt_ref[:, cols].astype(F32), up_ref[:, cols].astype(F32)).astype(BF16)
        h2v = h1_ref[...] + _nn(act_ref[...], wd_ref[...])
        r = _rms(h2v)
        hn = ((h2v * r) * g_ref[...]).astype(BF16)
        gate = jax.nn.sigmoid(_nn(hn, wpg_ref[...]))
        pb = p_ref[...].astype(BF16)
        for j in range(N_DEV):
            pp_ref[:, j * blk:(j + 1) * blk] = _nn(pb, wpp_ref[j])
        pp = pp_ref[...]
        diff = (h2v + gate * pp) - t_ref[...]
        loss_ref[...] += jnp.sum(jnp.sum(diff * diff, axis=0, keepdims=True), axis=1, keepdims=True) * (0.5 / D_MODEL)
        dy = diff * (1.0 / D_MODEL)
        d_pp = (dy * gate).astype(BF16)
        d_pre = ((dy * pp) * (gate * (1.0 - gate))).astype(BF16)
        for j in range(N_DEV):
            dwpp_ref[j] += _tn(pb, d_pp[:, j * blk:(j + 1) * blk])
        dwpg_ref[...] += _tn(hn, d_pre)
        d_x, d_g = _rms_bwd(_nt(d_pre, wpg_ref[...]), h2v, r, g_ref[...])
        dg_ref[...] += d_g
        dh_ref[...] = dy + d_x

    return _call(
        body,
        (gt, up, h1, w_down, p, target, g_ple, w_pg, w_pp),
        name="ffn_down_ple",
        grid=(s // ts,),
        in_specs=[
            _rows(ts, D_FF),
            _rows(ts, D_FF),
            _rows(ts, D_MODEL),
            _resident((D_FF, D_MODEL)),
            _rows(ts, PLE_DIM),
            _rows(ts, D_MODEL),
            _resident((1, D_MODEL)),
            _resident((D_MODEL, D_MODEL)),
            _resident((N_DEV, PLE_DIM, blk)),
        ],
        out_specs=[
            _acc((1, SMALL_LANES)),
            _rows(ts, D_MODEL),
            _acc((D_MODEL, D_MODEL)),
            _acc((N_DEV, PLE_DIM, blk)),
            _acc((1, D_MODEL)),
        ],
        out_shape=[
            jax.ShapeDtypeStruct((1, SMALL_LANES), F32),
            jax.ShapeDtypeStruct((s, D_MODEL), F32),
            jax.ShapeDtypeStruct((D_MODEL, D_MODEL), F32),
            jax.ShapeDtypeStruct((N_DEV, PLE_DIM, blk), F32),
            jax.ShapeDtypeStruct((1, D_MODEL), F32),
        ],
        scratch_shapes=[pltpu.VMEM((ts, D_FF), BF16), pltpu.VMEM((ts, D_MODEL), F32)],
        rider=rider,
    )


def _ffn_bwd_act(dh2, h1, gt, up, g_ffn, wg_t, wu_t, w_down, rider=None):
    s = h1.shape[0]
    ts = min(FFN_BWD_TILE, s)

    def body(dh_ref, h1_ref, gt_ref, up_ref, g_ref, wg_ref, wu_ref, wd_ref,
             dgt_ref, dup_ref, dh1_ref, dh1b_ref, dg_ref, dwd_ref, act_ref):
        @pl.when(pl.program_id(0) == 0)
        def _():
            dg_ref[...] = jnp.zeros_like(dg_ref)
            dwd_ref[...] = jnp.zeros_like(dwd_ref)

        dhb = dh_ref[...].astype(BF16)
        for c in range(D_FF // FF_CHUNK):
            cols = slice(c * FF_CHUNK, (c + 1) * FF_CHUNK)
            d_act = _nt(dhb, wd_ref[cols, :])
            gtv = gt_ref[:, cols].astype(F32)
            upv = up_ref[:, cols].astype(F32)
            sg = jax.nn.sigmoid(gtv)
            silu = gtv * sg
            act_ref[:, cols] = (silu * upv).astype(BF16)
            dup_ref[:, cols] = (d_act * silu).astype(BF16)
            dgt_ref[:, cols] = ((d_act * upv) * (sg * (1.0 + gtv * (1.0 - sg)))).astype(BF16)
        dwd_ref[...] += _tn(act_ref[...], dhb)
        d_hn = _nn(dgt_ref[...], wg_ref[...]) + _nn(dup_ref[...], wu_ref[...])
        h1v = h1_ref[...]
        d_x, d_g = _rms_bwd(d_hn, h1v, _rms(h1v), g_ref[...])
        dg_ref[...] += d_g
        dh1 = dh_ref[...] + d_x
        dh1_ref[...] = dh1
        dh1b_ref[...] = dh1.astype(BF16)

    return _call(
        body,
        (dh2, h1, gt, up, g_ffn, wg_t, wu_t, w_down),
        name="ffn_bwd_act",
        grid=(s // ts,),
        in_specs=[
            _rows(ts, D_MODEL),
            _rows(ts, D_MODEL),
            _rows(ts, D_FF),
            _rows(ts, D_FF),
            _resident((1, D_MODEL)),
            _resident((D_FF, D_MODEL)),
            _resident((D_FF, D_MODEL)),
            _resident((D_FF, D_MODEL)),
        ],
        out_specs=[
            _rows(ts, D_FF), _rows(ts, D_FF),
            _rows(ts, D_MODEL), _rows(ts, D_MODEL), _acc((1, D_MODEL)), _acc((D_FF, D_MODEL)),
        ],
        out_shape=[
            jax.ShapeDtypeStruct((s, D_FF), BF16),
            jax.ShapeDtypeStruct((s, D_FF), BF16),
            jax.ShapeDtypeStruct((s, D_MODEL), F32),
            jax.ShapeDtypeStruct((s, D_MODEL), BF16),
            jax.ShapeDtypeStruct((1, D_MODEL), F32),
            jax.ShapeDtypeStruct((D_FF, D_MODEL), F32),
        ],
        scratch_shapes=[pltpu.VMEM((ts, D_FF), BF16)],
        rider=rider,
    )


def _ffn_bwd_w(which, lhs, rhs, rider=None):
    s = rhs.shape[0]

    def body(lhs_ref, rhs_ref, dw_ref):
        dw_ref[...] = _tn(lhs_ref[...], rhs_ref[...])

    return _call(
        body,
        (lhs, rhs),
        name=f"ffn_bwd_{which}",
        grid=(D_FF // FF_CHUNK,),
        in_specs=[pl.BlockSpec((s, FF_CHUNK), lambda i: (0, i)), _resident((s, D_MODEL))],
        out_specs=[_rows(FF_CHUNK, D_MODEL)],
        out_shape=[jax.ShapeDtypeStruct((D_FF, D_MODEL), F32)],
        rider=rider,
    )


def _mix_bwd(dh1b, u, w_out, w_pool, pool_scale, rider=None):
    s = u.shape[0]
    ts = min(TOKEN_TILE, s)
    nt = s // ts
    halo_after = lambda i: (jnp.minimum((i + 1) * (ts // POOL_HALO), s // POOL_HALO - 1), 0)
    n_groups = len(POOL_SIZES)

    def body(dh_ref, dhn_ref, u_ref, uh_ref, wo_ref, wp_ref, sc_ref, da_ref, du_ref, dwp_ref, dsc_ref):
        i = pl.program_id(0)

        @pl.when(i == 0)
        def _():
            dwp_ref[...] = jnp.zeros_like(dwp_ref)
            dsc_ref[...] = jnp.zeros_like(dsc_ref)

        dh = dh_ref[...]
        da_ref[...] = _nt(dh, wo_ref[:ATTN_WIDTH, :])
        dh_next = jnp.where(i < nt - 1, dhn_ref[...], jnp.zeros_like(dhn_ref))
        dm_ext = _nt(jnp.concatenate([dh, dh_next], axis=0), wo_ref[ATTN_WIDTH:, :])
        pooled = _pooled(u_ref[...], uh_ref[...], i, ts)
        t_ext = i * ts + lax.broadcasted_iota(jnp.int32, (ts + POOL_HALO, 1), 0)
        for g, w in enumerate(POOL_SIZES):
            lanes = slice(g * POOL_GROUP, (g + 1) * POOL_GROUP)
            wp = wp_ref[g].astype(BF16)
            pg = pooled[g].astype(BF16)
            dm_g = dm_ext[:, lanes]
            dsc_ref[:, lanes] += jnp.sum(dm_g[:ts, :] * _nn(pg, wp), axis=0, keepdims=True)
            dy = (dm_g * sc_ref[:, lanes]).astype(BF16)
            dwp_ref[g] += _tn(pg, dy[:ts, :])
            d_pool = _nt(dy, wp)
            acc = d_pool / jnp.minimum(t_ext + 1, w).astype(F32)
            shift = 1
            while shift < w:
                acc = acc + pltpu.roll(acc, ts + POOL_HALO - shift, 0)
                shift *= 2
            du_ref[:, lanes] = acc[:ts, :] - d_pool[:ts, :]

    return _call(
        body,
        (dh1b, dh1b, u, u, w_out, w_pool, pool_scale),
        name="mix_bwd",
        grid=(nt,),
        in_specs=[
            _rows(ts, D_MODEL),
            pl.BlockSpec((POOL_HALO, D_MODEL), halo_after),
            _rows(ts, POOL_WIDTH),
            pl.BlockSpec((POOL_HALO, POOL_WIDTH), _halo_before(ts)),
            _resident((D_MODEL, D_MODEL)),
            _resident((n_groups, POOL_GROUP, POOL_GROUP)),
            _resident((1, POOL_WIDTH)),
        ],
        out_specs=[
            _rows(ts, ATTN_WIDTH),
            _rows(ts, POOL_WIDTH),
            _acc((n_groups, POOL_GROUP, POOL_GROUP)),
            _acc((1, POOL_WIDTH)),
        ],
        out_shape=[
            jax.ShapeDtypeStruct((s, ATTN_WIDTH), F32),
            jax.ShapeDtypeStruct((s, POOL_WIDTH), F32),
            jax.ShapeDtypeStruct((n_groups, POOL_GROUP, POOL_GROUP), F32),
            jax.ShapeDtypeStruct((1, POOL_WIDTH), F32),
        ],
        rider=rider,
    )


def _out_w_bwd(a, m, dh1b, rider=None):
    s = dh1b.shape[0]

    def body(a_ref, m_ref, dh_ref, dw_ref):
        @pl.when(pl.program_id(0) == 0)
        def _():
            dw_ref[...] = _tn(a_ref[...], dh_ref[...])

        @pl.when(pl.program_id(0) == 1)
        def _():
            dw_ref[...] = _tn(m_ref[...], dh_ref[...])

    return _call(
        body,
        (a, m, dh1b),
        name="out_w_bwd",
        grid=(2,),
        in_specs=[_resident((s, ATTN_WIDTH)), _resident((s, POOL_WIDTH)), _resident((s, D_MODEL))],
        out_specs=[_rows(ATTN_WIDTH, D_MODEL)],
        out_shape=[jax.ShapeDtypeStruct((D_MODEL, D_MODEL), F32)],
        rider=rider,
    )


def _attn_bwd(qn, kn, v, a, da, tab, sinks, rider=None):
    s = qn.shape[0]
    qb = ATTN_STEP_BLOCKS
    rows = qb * BLOCK
    n_groups = s // rows
    group, prev = _attn_specs(n_groups)
    done = lambda n: (jnp.maximum(n - 1, 0), 0)

    def body(sink_ref, q_ref, kc_ref, kp_ref, vc_ref, vp_ref, o_ref, do_ref, tab_ref,
             dq_ref, dk_ref, dv_ref, dl_ref, ds_ref, k_carry, v_carry, sink_acc):
        n = pl.program_id(0)

        @pl.when(n == 0)
        def _():
            dl_ref[...] = jnp.zeros_like(dl_ref)
            k_carry[...] = jnp.zeros_like(k_carry)
            v_carry[...] = jnp.zeros_like(v_carry)
            sink_acc[...] = jnp.zeros_like(sink_acc)

        @pl.when(n < n_groups)
        def _():
            first = n == 0
            lo_mask = _lane_lo((BLOCK, BLOCK))
            dks, dvs = [], []
            for b in range(qb):
                at = slice(b * BLOCK, (b + 1) * BLOCK)
                keys = _band(kp_ref, kc_ref, b)
                vals = _band(vp_ref, vc_ref, b)
                q_st = _stack_heads([q_ref[at, p * BLOCK:(p + 1) * BLOCK] for p in range(4)], lo_mask)
                do_st = _stack_heads([do_ref[at, p * BLOCK:(p + 1) * BLOCK] for p in range(4)], lo_mask)
                o_st = _stack_heads([o_ref[at, p * BLOCK:(p + 1) * BLOCK].astype(F32) for p in range(4)], lo_mask)
                dq_st, dk_parts, dv_parts = [], [], []
                for half, heads in enumerate((HEADS_A, HEADS_B)):
                    probs, p_sink = _band_probs(q_st[half], keys[half], tab_ref[half], _sink_column(sink_ref, heads),
                                                first if b == 0 else None)
                    delta = jnp.sum(do_st[half] * o_st[half], axis=-1, keepdims=True)
                    dob = do_st[half].astype(BF16)
                    dl = probs * (_nt(dob, vals[half]) - delta)
                    dl_ref[half] += dl
                    sink_acc[half] += p_sink * delta
                    dsb = (dl * (HEAD_DIM ** -0.5)).astype(BF16)
                    dq_st.append(_nn(dsb, keys[half]))
                    dk_parts.append(_tn(dsb, q_st[half]))
                    dv_parts.append(_tn(probs.astype(BF16), dob))
                dq = _unstack_heads(dq_st[0], dq_st[1], lo_mask)
                for p in range(4):
                    dq_ref[at, p * BLOCK:(p + 1) * BLOCK] = dq[p]
                dks.append(dk_parts[0] + pltpu.roll(dk_parts[1], HEAD_DIM, 1))
                dvs.append(dv_parts[0] + pltpu.roll(dv_parts[1], HEAD_DIM, 1))
            last = slice((qb - 1) * BLOCK, qb * BLOCK)
            for parts, out_ref, carry in ((dks, dk_ref, k_carry), (dvs, dv_ref, v_carry)):
                out_ref[...] = carry[...]
                out_ref[last, :] += parts[0][:BLOCK, :]
                for b in range(qb):
                    own = parts[b][BLOCK:, :]
                    carry[b * BLOCK:(b + 1) * BLOCK, :] = own + parts[b + 1][:BLOCK, :] if b + 1 < qb else own

        @pl.when(n == n_groups)
        def _():
            dk_ref[...] = k_carry[...]
            dv_ref[...] = v_carry[...]
            for half, heads in enumerate((HEADS_A, HEADS_B)):
                for slot, h in enumerate(heads):
                    tot = jnp.sum(sink_acc[half, slot * BLOCK:(slot + 1) * BLOCK, :], axis=0, keepdims=True)
                    ds_ref[h:h + 1, :] = jnp.broadcast_to(-tot, (1, SMALL_LANES))

    return _call(
        body,
        (sinks, qn, kn, kn, v, v, a, da, tab),
        name="attn_bwd",
        grid=(n_groups + 1,),
        in_specs=[
            pl.BlockSpec(memory_space=pltpu.SMEM),
            pl.BlockSpec((rows, ATTN_WIDTH), group),
            pl.BlockSpec((rows, KV_WIDTH), group),
            pl.BlockSpec((BLOCK, KV_WIDTH), prev),
            pl.BlockSpec((rows, KV_WIDTH), group),
            pl.BlockSpec((BLOCK, KV_WIDTH), prev),
            pl.BlockSpec((rows, ATTN_WIDTH), group),
            pl.BlockSpec((rows, ATTN_WIDTH), group),
            _resident((2, 4 * BLOCK, 2 * BLOCK)),
        ],
        out_specs=[
            pl.BlockSpec((rows, ATTN_WIDTH), group),
            pl.BlockSpec((rows, KV_WIDTH), done),
            pl.BlockSpec((rows, KV_WIDTH), done),
            _acc((2, 4 * BLOCK, 2 * BLOCK)),
            _acc((N_DEV, SMALL_LANES)),
        ],
        out_shape=[
            jax.ShapeDtypeStruct((s, ATTN_WIDTH), F32),
            jax.ShapeDtypeStruct((s, KV_WIDTH), F32),
            jax.ShapeDtypeStruct((s, KV_WIDTH), F32),
            jax.ShapeDtypeStruct((2, 4 * BLOCK, 2 * BLOCK), F32),
            jax.ShapeDtypeStruct((N_DEV, SMALL_LANES), F32),
        ],
        scratch_shapes=[
            pltpu.VMEM((rows, KV_WIDTH), F32),
            pltpu.VMEM((rows, KV_WIDTH), F32),
            pltpu.VMEM((2, 4 * BLOCK, 1), F32),
        ],
        rider=rider,
    )


def _fold_heads(acc):
    t = acc + pltpu.roll(acc, HEAD_DIM, 1)
    out = t[:, :SMALL_LANES]
    for g in range(1, acc.shape[1] // SMALL_LANES):
        out = out + t[:, g * SMALL_LANES:(g + 1) * SMALL_LANES]
    return out


def _in_proj_bwd(dqn, dkn, dv, du, zqk, x, dh1, g_attn, gq_t, gk_t, w_in_t, rider=None):
    s = x.shape[0]
    ts = min(TOKEN_TILE, s)
    nt = s // ts

    def head_norm_bwd(d_n, raw, g_t, bmat):
        r = lax.rsqrt(_seg_mean(raw * raw, bmat) + EPS)
        gy = d_n * g_t
        d_raw = r * gy - raw * (r * r * r) * _seg_mean(gy * raw, bmat)
        return d_raw, jnp.sum(d_n * (raw * r), axis=0, keepdims=True)

    def body(dqn_ref, dkn_ref, dv_ref, du_ref, zqk_ref, x_ref, dh1_ref, g_ref, gq_ref, gk_ref, w_ref, bq_ref, bk_ref,
             gx_ref, dw_ref, dg_ref, dgq_ref, dgk_ref, dz_ref, gq_acc, gk_acc):
        i = pl.program_id(0)

        @pl.when(i == 0)
        def _():
            dw_ref[...] = jnp.zeros_like(dw_ref)
            dg_ref[...] = jnp.zeros_like(dg_ref)
            gq_acc[...] = jnp.zeros_like(gq_acc)
            gk_acc[...] = jnp.zeros_like(gk_acc)

        d_q, d_gq = head_norm_bwd(dqn_ref[...], zqk_ref[:, :ATTN_WIDTH], gq_ref[...], bq_ref[...])
        d_k, d_gk = head_norm_bwd(dkn_ref[...], zqk_ref[:, ATTN_WIDTH:], gk_ref[...], bk_ref[...])
        gq_acc[...] += d_gq
        gk_acc[...] += d_gk
        dz_ref[:, :ATTN_WIDTH] = d_q.astype(BF16)
        dz_ref[:, ATTN_WIDTH:ATTN_WIDTH + KV_WIDTH] = d_k.astype(BF16)
        dz_ref[:, ATTN_WIDTH + KV_WIDTH:ATTN_WIDTH + 2 * KV_WIDTH] = dv_ref[...].astype(BF16)
        dz_ref[:, ATTN_WIDTH + 2 * KV_WIDTH:] = du_ref[...].astype(BF16)
        dz = dz_ref[...]
        xf = x_ref[...]
        r = _rms(xf)
        hn = ((xf * r) * g_ref[...]).astype(BF16)
        dw_ref[...] += _tn(dz, hn)
        d_x, d_g = _rms_bwd(_nn(dz, w_ref[...]), xf, r, g_ref[...])
        dg_ref[...] += d_g
        gx_ref[...] = dh1_ref[...] + d_x

        @pl.when(i == nt - 1)
        def _():
            dgq_ref[...] = _fold_heads(gq_acc[...])
            dgk_ref[...] = _fold_heads(gk_acc[...])

    return _call(
        body,
        (dqn, dkn, dv, du, zqk, x, dh1, g_attn, gq_t, gk_t, w_in_t,
      _head_mean_matrix(ATTN_WIDTH), _head_mean_matrix(KV_WIDTH)),
        name="in_proj_bwd",
        grid=(nt,),
        in_specs=[
            _rows(ts, ATTN_WIDTH),
            _rows(ts, KV_WIDTH),
            _rows(ts, KV_WIDTH),
            _rows(ts, POOL_WIDTH),
            _rows(ts, ATTN_WIDTH + KV_WIDTH),
            _rows(ts, D_MODEL),
            _rows(ts, D_MODEL),
            _resident((1, D_MODEL)),
            _resident((1, ATTN_WIDTH)),
            _resident((1, KV_WIDTH)),
            _resident((IN_WIDTH, D_MODEL)),
            _resident((ATTN_WIDTH, ATTN_WIDTH)),
            _resident((KV_WIDTH, KV_WIDTH)),
        ],
        out_specs=[
            _rows(ts, D_MODEL),
            _acc((IN_WIDTH, D_MODEL)),
            _acc((1, D_MODEL)),
            _acc((1, SMALL_LANES)),
            _acc((1, SMALL_LANES)),
        ],
        out_shape=[
            jax.ShapeDtypeStruct((s, D_MODEL), F32),
            jax.ShapeDtypeStruct((IN_WIDTH, D_MODEL), F32),
            jax.ShapeDtypeStruct((1, D_MODEL), F32),
            jax.ShapeDtypeStruct((1, SMALL_LANES), F32),
            jax.ShapeDtypeStruct((1, SMALL_LANES), F32),
        ],
        scratch_shapes=[
            pltpu.VMEM((ts, IN_WIDTH), BF16),
            pltpu.VMEM((1, ATTN_WIDTH), F32),
            pltpu.VMEM((1, KV_WIDTH), F32),
        ],
        rider=rider,
    )


BIG_WEIGHTS = (
    ("w_in", True, IN_WIDTH // N_DEV, D_MODEL),
    ("w_out", False, D_MODEL // N_DEV, D_MODEL),
    ("w_gate", True, D_FF // N_DEV, D_MODEL),
    ("w_up", True, D_FF // N_DEV, D_MODEL),
    ("w_down", False, D_FF // N_DEV, D_MODEL),
    ("w_ple_gate", False, D_MODEL // N_DEV, D_MODEL),
    ("w_ple_proj", False, PLE_DIM, D_MODEL // N_DEV),
)
N_BIG = len(BIG_WEIGHTS)


def _place():
    x, y, c = lax.axis_index("x"), lax.axis_index("y"), lax.axis_index("c")
    chips = [(1 - x, y), (x, 1 - y), (1 - x, 1 - y)]
    return x, y, c, chips


class _Gather:
    def __init__(self, n, rows=None):
        self.n = n
        self.rows = rows or [None] * n
        self.sems = [pltpu.SemaphoreType.DMA((n, 7)), pltpu.SemaphoreType.DMA((n, 7)), pltpu.SemaphoreType.DMA((n,))]

    def _ctx(self, srcs, outs, sems):
        send_sems, recv_sems, local_sems = sems
        x, y, c, chips = _place()
        me, sibling = (x, y, c), (x, y, 1 - c)

        def part(k, ref):
            return ref if self.rows[k] is None else ref.at[pl.ds(*self.rows[k]), :]

        def block(k, owner):
            px, py, pc = owner
            return part(k, outs[k].at[4 * px + 2 * py + pc])

        def copy(k, idx, owner, to, mine=False):
            return pltpu.make_async_remote_copy(
                src_ref=part(k, srcs[k]) if mine else block(k, owner), dst_ref=block(k, owner),
                send_sem=send_sems.at[k, idx], recv_sem=recv_sems.at[k, idx], device_id=to, device_id_type=MESH)

        def local(k):
            return pltpu.make_async_copy(part(k, srcs[k]), block(k, me), local_sems.at[k])

        return c, chips, me, sibling, copy, local

    def begin(self, srcs, outs, sems):
        c, chips, me, sibling, copy, local = self._ctx(srcs, outs, sems)
        for k in range(self.n):
            local(k).start()
            copy(k, 0, me, sibling, mine=True).start()
            for j, chip in enumerate(chips):
                copy(k, 1 + j, me, (*chip, c), mine=True).start()

    def middle(self, srcs, outs, sems):
        c, chips, me, sibling, copy, local = self._ctx(srcs, outs, sems)
        for j, chip in enumerate(chips):
            for k in range(self.n):
                copy(k, 1 + j, (*chip, c), me).wait_recv()
                copy(k, 4 + j, (*chip, c), sibling).start()

    def end(self, srcs, outs, sems):
        c, chips, me, sibling, copy, local = self._ctx(srcs, outs, sems)
        for k in range(self.n):
            copy(k, 0, sibling, me).wait_recv()
            for j, chip in enumerate(chips):
                copy(k, 4 + j, (*chip, 1 - c), me).wait_recv()
        for k in range(self.n):
            copy(k, 0, me, sibling, mine=True).wait_send()
            for j, chip in enumerate(chips):
                copy(k, 1 + j, me, (*chip, c), mine=True).wait_send()
                copy(k, 4 + j, (*chip, c), sibling).wait_send()
            local(k).wait()


def _gather_rider(items):
    items = [it if isinstance(it, tuple) else (it, None, None, None) for it in items]
    n = len(items)
    g = _Gather(n, [None if r0 is None else (r0, nr) for _, r0, nr, _ in items])
    shapes = [jax.ShapeDtypeStruct((N_DEV, *sh.shape), sh.dtype) for sh, _, _, _ in items]
    stacks = [(k, st) for k, (_, _, _, st) in enumerate(items) if st is not None]
    aliases = {n + i: k for i, (k, _) in enumerate(stacks)}
    return _Rider([sh for sh, _, _, _ in items] + [st for _, st in stacks], shapes, g.sems, g.begin, g.end, g.middle,
                  aliases=aliases)


def _cast_and_gather_first(shards):
    g = _Gather(1)
    any_spec = pl.BlockSpec(memory_space=pl.ANY)
    vmem = pl.BlockSpec(memory_space=pltpu.VMEM)

    def body(*refs):
        ins, outs, gathered, sems = refs[:N_BIG], refs[N_BIG:2 * N_BIG], refs[2 * N_BIG], refs[2 * N_BIG + 1:]
        outs[0][...] = ins[0][...].astype(BF16)
        g.begin(outs[:1], [gathered], sems)
        for k in range(1, N_BIG):
            outs[k][...] = ins[k][...].astype(BF16)
        g.middle(outs[:1], [gathered], sems)
        g.end(outs[:1], [gathered], sems)

    res = pl.pallas_call(
        body,
        name="cast_and_gather_first",
        in_specs=[vmem] * N_BIG,
        out_specs=[vmem] * N_BIG + [any_spec],
        out_shape=[jax.ShapeDtypeStruct((r, c), BF16) for _, _, r, c in BIG_WEIGHTS]
        + [jax.ShapeDtypeStruct((N_DEV, *BIG_WEIGHTS[0][2:]), BF16)],
        scratch_shapes=g.sems,
    )(*shards)
    return list(res[:N_BIG]), res[N_BIG]


def _sibling_rider(grads):
    n = len(grads)

    def copies(gs, lands, sems):
        send_sems, recv_sems = sems
        x, y, c, _ = _place()
        return [
            pltpu.make_async_remote_copy(
                src_ref=gs[k].at[:, 1 - c], dst_ref=lands[k], send_sem=send_sems.at[k], recv_sem=recv_sems.at[k],
                device_id=(x, y, 1 - c), device_id_type=MESH)
            for k in range(n)
        ]

    def begin(gs, lands, sems):
        for cp in copies(gs, lands, sems):
            cp.start()

    def end(gs, lands, sems):
        for cp in copies(gs, lands, sems):
            cp.wait()

    shapes = [jax.ShapeDtypeStruct((N_CHIPS, *g.shape[2:]), F32) for g in grads]
    return _Rider(grads, shapes, [pltpu.SemaphoreType.DMA((n,)), pltpu.SemaphoreType.DMA((n,))], begin, end)


def _chip_sum(k, place, grad, from_sibling):
    _, _, r, c = BIG_WEIGHTS[k]

    args, in_specs, _ = _after_last(
        [grad, from_sibling],
        [pl.BlockSpec((1, 1, r, c), lambda q, place: (q, place[2], 0, 0)),
         pl.BlockSpec((1, r, c), lambda q, place: (q, 0, 0))])

    def body(place_ref, g_ref, l_ref, *refs):
        own_ref, send_ref = refs[-2:]
        q = pl.program_id(0)
        tot = g_ref[0, 0] + l_ref[0]
        mine = q == 2 * place_ref[0] + place_ref[1]

        @pl.when(mine)
        def _():
            own_ref[...] = tot

        send_ref[0] = jnp.where(mine, 0.0, tot).astype(BF16)

    outs = pl.pallas_call(
        body,
        name=f"chip_sum_{BIG_WEIGHTS[k][0]}",
        grid_spec=pltpu.PrefetchScalarGridSpec(
            num_scalar_prefetch=1,
            grid=(N_CHIPS,),
            in_specs=in_specs,
            out_specs=[
                pl.BlockSpec((r, c), lambda q, place: (0, 0)),
                pl.BlockSpec((1, r, c), lambda q, place: (q, 0, 0)),
            ],
        ),
        out_shape=[jax.ShapeDtypeStruct((r, c), F32), jax.ShapeDtypeStruct((N_CHIPS, r, c), BF16)],
    )(place, *args)
    _mark_issued(outs[0])
    return outs


def _chips_rider(to_send, small=None):
    n = len(to_send)
    inputs = list(to_send) + ([] if small is None else [small])
    shapes = [jax.ShapeDtypeStruct((3, *t.shape[1:]), BF16) for t in to_send]
    sems = [pltpu.SemaphoreType.DMA((max(n, 1), 3)), pltpu.SemaphoreType.DMA((max(n, 1), 3))]
    if small is not None:
        shapes.append(jax.ShapeDtypeStruct((N_DEV, *small.shape), F32))
        sems += [pltpu.SemaphoreType.DMA((7,)), pltpu.SemaphoreType.DMA((7,)), pltpu.SemaphoreType.DMA]

    def copies(ins, outs, sem_refs):
        x, y, c, chips = _place()
        out = []
        for k in range(n):
            for j, (px, py) in enumerate(chips):
                out.append(pltpu.make_async_remote_copy(
                    src_ref=ins[k].at[2 * px + py], dst_ref=outs[k].at[j],
                    send_sem=sem_refs[0].at[k, j], recv_sem=sem_refs[1].at[k, j],
                    device_id=(px, py, c), device_id_type=MESH))
        local = None
        if small is not None:
            me = 4 * x + 2 * y + c
            local = pltpu.make_async_copy(ins[n], outs[n].at[me], sem_refs[4])
            rel = 0
            for fx in (0, 1):
                for fy in (0, 1):
                    for fc in (0, 1):
                        if (fx, fy, fc) != (0, 0, 0):
                            out.append(pltpu.make_async_remote_copy(
                                src_ref=ins[n], dst_ref=outs[n].at[me],
                                send_sem=sem_refs[2].at[rel], recv_sem=sem_refs[3].at[rel],
                                device_id=(x ^ fx, y ^ fy, c ^ fc), device_id_type=MESH))
                            rel += 1
        return out, local

    def begin(ins, outs, sem_refs):
        remote, local = copies(ins, outs, sem_refs)
        if local is not None:
            local.start()
        for cp in remote:
            cp.start()

    def end(ins, outs, sem_refs):
        remote, local = copies(ins, outs, sem_refs)
        for cp in remote:
            cp.wait()
        if local is not None:
            local.wait()

    return _Rider(inputs, shapes, sems, begin, end)


def _exchange(name, rider):
    return _call(lambda: None, (), name=name, grid=(1,), in_specs=[], out_specs=[], out_shape=[], rider=rider)[1]


PEER_SETS = {"sibling": 1, "chips": 2, "sibling+chips": 3, "all": 4}


def _peers(pattern):
    x, y, c, chips = _place()
    sibling, others = [(x, y, 1 - c)], [(*chip, c) for chip in chips]
    if pattern == "all":
        return sibling + others + [(*chip, 1 - c) for chip in chips]
    return {"sibling": sibling, "chips": others, "sibling+chips": sibling + others}[pattern]


def _on_sequencer(name, pattern, rider):
    assert not rider.aliases
    n_in, n_out = len(rider.inputs), len(rider.out_shapes)

    def body(*refs):
        ins, outs, sems = refs[:n_in], refs[n_in:n_in + n_out], refs[n_in + n_out:]
        peers = _peers(pattern)
        barrier = pltpu.get_barrier_semaphore()
        for peer in peers:
            pl.semaphore_signal(barrier, inc=1, device_id=peer, device_id_type=MESH)
        pl.semaphore_wait(barrier, len(peers))
        rider.begin(ins, outs, sems)
        if rider.middle is not None:
            rider.middle(ins, outs, sems)
        rider.end(ins, outs, sems)

    outs = pl.kernel(
        body,
        name=name,
        out_type=tuple(rider.out_shapes),
        mesh=plsc.ScalarSubcoreMesh(axis_name="sequencer", num_cores=1),
        scratch_types=tuple(rider.sems),
        compiler_params=pltpu.CompilerParams(collective_id=PEER_SETS[pattern]),
    )(*rider.inputs)
    return list(outs)


def _merge_riders(*riders):
    riders = [r for r in riders if r is not None]
    if len(riders) == 1:
        return riders[0]
    assert not any(r.aliases for r in riders)

    def split(refs, counts):
        out, at = [], 0
        for n in counts:
            out.append(refs[at:at + n])
            at += n
        return out

    def run(which):
        def fn(ins, outs, sems):
            parts = zip(riders, split(ins, [len(r.inputs) for r in riders]),
                        split(outs, [len(r.out_shapes) for r in riders]), split(sems, [len(r.sems) for r in riders]))
            for r, i, o, s in parts:
                hook = getattr(r, which)
                if hook is not None:
                    hook(i, o, s)
        return fn

    middle = run("middle") if any(r.middle is not None for r in riders) else None
    return _Rider(sum((r.inputs for r in riders), []), sum((r.out_shapes for r in riders), []),
                  sum((r.sems for r in riders), []), run("begin"), run("end"), middle)


def _split_outputs(outs, *riders):
    res, at = [], 0
    for r in riders:
        res.append(outs[at:at + len(r.out_shapes)])
        at += len(r.out_shapes)
    return res


def _adamw(w, g, m, v):
    m = ADAM_B1 * m + (1.0 - ADAM_B1) * g
    v = ADAM_B2 * v + (1.0 - ADAM_B2) * jnp.square(g)
    m_hat = m / (1.0 - ADAM_B1 ** ADAM_STEP)
    v_hat = v / (1.0 - ADAM_B2 ** ADAM_STEP)
    delta = -ADAM_LR * (m_hat / (jnp.sqrt(v_hat) + ADAM_EPS) + ADAM_WD * w)
    return delta, m, v


def _adamw_big(k, own, landed, w, m, v, rider=None):
    name, _, r, c = BIG_WEIGHTS[k]
    tile = r // 2
    tiles = lambda i: (i, 0)

    def body(own_ref, land_ref, w_ref, m_ref, v_ref, g_ref, d_ref, nm_ref, nv_ref):
        g = ((own_ref[...] + land_ref[0].astype(F32)) + land_ref[1].astype(F32)) + land_ref[2].astype(F32)
        g_ref[...] = g
        d_ref[...], nm_ref[...], nv_ref[...] = _adamw(w_ref[...], g, m_ref[...], v_ref[...])

    return _call(
        body,
        (own, landed, w, m, v),
        name=f"adamw_{name}",
        grid=(r // tile,),
        in_specs=[pl.BlockSpec((tile, c), tiles), pl.BlockSpec((3, tile, c), lambda i: (0, i, 0))]
        + [pl.BlockSpec((tile, c), tiles)] * 3,
        out_specs=[pl.BlockSpec((tile, c), tiles)] * 4,
        out_shape=[jax.ShapeDtypeStruct((r, c), F32)] * 4,
        rider=rider,
    )


def _sum_small(parts_list):
    n = len(parts_list)

    def body(*refs):
        for p_ref, out_ref in zip(refs[:n], refs[n:]):
            tot = p_ref[0]
            for j in range(1, N_DEV):
                tot = tot + p_ref[j]
            out_ref[...] = tot

    return pl.pallas_call(body, name="sum_small",
                          out_shape=[jax.ShapeDtypeStruct(p.shape[1:], F32) for p in parts_list])(*parts_list)


def _adamw_small(grads, ws, ms, vs):
    n = len(grads)

    def body(*refs):
        g_refs, w_refs, m_refs, v_refs = refs[:n], refs[n:2 * n], refs[2 * n:3 * n], refs[3 * n:4 * n]
        outs = refs[4 * n:]
        for i in range(n):
            d, nm, nv = _adamw(w_refs[i][...], g_refs[i][...], m_refs[i][...], v_refs[i][...])
            outs[i][...] = d
            outs[n + i][...] = nm
            outs[2 * n + i][...] = nv

    shapes = [jax.ShapeDtypeStruct(w.shape, F32) for w in ws]
    return pl.pallas_call(body, name="adamw_small", out_shape=shapes * 3)(*grads, *ws, *ms, *vs)


SMALL_NAMES = ("g_attn_norm", "g_q", "g_k", "attn_sinks", "rel_bias", "w_pool", "pool_scale", "g_ffn_norm", "g_ple_norm")


def _pack_small(arrays):
    rows, offsets = [], []
    at = 0
    for a in arrays:
        flat = a.reshape(-1)
        n_rows = -(-flat.shape[0] // (8 * SMALL_LANES)) * 8
        flat = jnp.pad(flat, (0, n_rows * SMALL_LANES - flat.shape[0]))
        rows.append(flat.reshape(n_rows, SMALL_LANES))
        offsets.append(at)
        at += n_rows
    return jnp.concatenate(rows, axis=0), offsets


def kernel(x, p, w_in, w_out, g_attn_norm, g_q, g_k, attn_sinks, rel_bias, w_pool, pool_scale, g_ffn_norm, w_gate, w_up, w_down, g_ple_norm, w_ple_gate, w_ple_proj, loss_target, m_w_in, m_w_out, m_g_attn_norm, m_g_q, m_g_k, m_attn_sinks, m_rel_bias, m_w_pool, m_pool_scale, m_g_ffn_norm, m_w_gate, m_w_up, m_w_down, m_g_ple_norm, m_w_ple_gate, m_w_ple_proj, v_w_in, v_w_out, v_g_attn_norm, v_g_q, v_g_k, v_attn_sinks, v_rel_bias, v_w_pool, v_pool_scale, v_g_ffn_norm, v_w_gate, v_w_up, v_w_down, v_g_ple_norm, v_w_ple_gate, v_w_ple_proj):
    weights = dict(w_in=w_in, w_out=w_out, g_attn_norm=g_attn_norm, g_q=g_q, g_k=g_k, attn_sinks=attn_sinks,
                   rel_bias=rel_bias, w_pool=w_pool, pool_scale=pool_scale, g_ffn_norm=g_ffn_norm, w_gate=w_gate,
                   w_up=w_up, w_down=w_down, g_ple_norm=g_ple_norm, w_ple_gate=w_ple_gate, w_ple_proj=w_ple_proj)
    m_in = dict(w_in=m_w_in, w_out=m_w_out, g_attn_norm=m_g_attn_norm, g_q=m_g_q, g_k=m_g_k, attn_sinks=m_attn_sinks,
                rel_bias=m_rel_bias, w_pool=m_w_pool, pool_scale=m_pool_scale, g_ffn_norm=m_g_ffn_norm, w_gate=m_w_gate,
                w_up=m_w_up, w_down=m_w_down, g_ple_norm=m_g_ple_norm, w_ple_gate=m_w_ple_gate, w_ple_proj=m_w_ple_proj)
    v_in = dict(w_in=v_w_in, w_out=v_w_out, g_attn_norm=v_g_attn_norm, g_q=v_g_q, g_k=v_g_k, attn_sinks=v_attn_sinks,
                rel_bias=v_rel_bias, w_pool=v_w_pool, pool_scale=v_pool_scale, g_ffn_norm=v_g_ffn_norm, w_gate=v_w_gate,
                w_up=v_w_up, w_down=v_w_down, g_ple_norm=v_g_ple_norm, w_ple_gate=v_w_ple_gate, w_ple_proj=v_w_ple_proj)

    _issued.clear()
    xs = x[0]
    ps = p[0, 0]
    target = loss_target[0]
    wp = w_pool[0]
    gq_t = jnp.tile(g_q, (1, ATTN_WIDTH // HEAD_DIM))
    gk_t = jnp.tile(g_k, (1, KV_WIDTH // HEAD_DIM))

    def to_blocks(k, arr):
        return jnp.swapaxes(arr[0], 0, 1) if BIG_WEIGHTS[k][1] else arr[0]

    def from_blocks(k, arr):
        return (jnp.swapaxes(arr, 0, 1) if BIG_WEIGHTS[k][1] else arr)[None]

    IN, OUT, GATE, UP, DOWN, PG, PP = range(N_BIG)
    full = lambda g: g.reshape(N_DEV * g.shape[1], g.shape[2])
    halves = lambda k, g: g.reshape(N_CHIPS, 2, *BIG_WEIGHTS[k][2:])
    place = jnp.stack([lax.axis_index("x"), lax.axis_index("y"), lax.axis_index("c")]).astype(jnp.int32)

    sh, w_in_g = _cast_and_gather_first([to_blocks(k, weights[name]) for k, (name, _, _, _) in enumerate(BIG_WEIGHTS)])
    w_in_t = full(w_in_g)

    (w_out_g,) = _on_sequencer("gather_out", "sibling+chips", _gather_rider([sh[OUT]]))
    wg_g, wu_g = _on_sequencer("gather_gate_up", "sibling+chips", _gather_rider([sh[GATE], sh[UP]]))
    wd_g, w_pg_g, w_pp_g = _on_sequencer("gather_down_ple", "sibling+chips", _gather_rider([sh[DOWN], sh[PG], sh[PP]]))
    tab = _bias_table(rel_bias.T)
    (zqk, qn, kn, v, u), _ = _in_proj(xs, g_attn_norm, w_in_t, gq_t, gk_t)
    (a,), _ = _attn_fwd(qn, kn, v, tab, attn_sinks)
    w_out_f = full(w_out_g)
    (h1, hn2, m_out), _ = _mix_out(u, a, xs, w_out_f, wp, pool_scale, g_ffn_norm)
    wg_t, wu_t = full(wg_g), full(wu_g)
    (gt, up), _ = _ffn_up(hn2, wg_t, wu_t)
    w_down_f = full(wd_g)

    sums, landed = [None] * N_BIG, [None] * N_BIG

    def to_sibling(name, ks, grads):
        return _on_sequencer(name, "sibling", _sibling_rider([halves(k, g) for k, g in zip(ks, grads)]))

    def chip_sum(k, grad, from_sibling):
        sums[k] = _chip_sum(k, place, halves(k, grad), from_sibling)

    def to_chips(name, ks, small=None):
        got = _on_sequencer(name, "chips" if small is None else "all", _chips_rider([sums[k][1] for k in ks], small))
        for k, g in zip(ks, got):
            landed[k] = g
        return got[len(ks):]

    (loss_part, dh2, d_wpg, d_wpp, d_g_ple), _ = _ffn_down_ple(
        gt, up, h1, w_down_f, ps, target, g_ple_norm, full(w_pg_g), w_pp_g)
    sib_ple = to_sibling("sibling_ple", (PG, PP), (d_wpg, d_wpp))
    (dgt, dup, dh1, dh1b, d_g_ffn, d_wd), _ = _ffn_bwd_act(dh2, h1, gt, up, g_ffn_norm, wg_t, wu_t, w_down_f)
    sib_down = to_sibling("sibling_down", (DOWN,), (d_wd,))
    chip_sum(PG, d_wpg, sib_ple[0])
    chip_sum(PP, d_wpp, sib_ple[1])
    to_chips("chips_ple", (PG, PP))
    (d_wo,), _ = _out_w_bwd(a, m_out, dh1b)
    sib_out = to_sibling("sibling_out", (OUT,), (d_wo,))
    (d_wg_t,), _ = _ffn_bwd_w("gate", dgt, hn2)
    sib_gate = to_sibling("sibling_gate", (GATE,), (d_wg_t,))
    chip_sum(DOWN, d_wd, sib_down[0])
    to_chips("chips_down", (DOWN,))
    _complete_before_next([landed[PG], landed[PP]])
    (d_wu_t,), _ = _ffn_bwd_w("up", dup, hn2)
    sib_up = to_sibling("sibling_up", (UP,), (d_wu_t,))
    chip_sum(OUT, d_wo, sib_out[0])
    chip_sum(GATE, d_wg_t, sib_gate[0])
    to_chips("chips_out_gate", (OUT, GATE))
    _complete_before_next([landed[DOWN]])
    (da, du, d_wpool, d_scale), _ = _mix_bwd(dh1b, u, w_out_f, wp, pool_scale)
    chip_sum(UP, d_wu_t, sib_up[0])
    early, early_at = _pack_small([d_wpool, d_scale, d_g_ffn, d_g_ple, loss_part[:, :1]])
    (early_all,) = to_chips("chips_up", (UP,), early)
    (dqn, dkn, dv, dl_acc, d_sinks), _ = _attn_bwd(qn, kn, v, a, da, tab, attn_sinks)
    _complete_before_next([landed[OUT], landed[GATE]])
    (grad_x, d_win_t, d_g_attn, d_gq, d_gk), _ = _in_proj_bwd(dqn, dkn, dv, du, zqk, xs, dh1, g_attn_norm, gq_t, gk_t, w_in_t)
    sib_in = to_sibling("sibling_in", (IN,), (d_win_t,))
    _complete_before_next([landed[UP], early_all])
    (d_rel_t,), _ = _bias_table_bwd(dl_acc)
    chip_sum(IN, d_win_t, sib_in[0])
    late, late_at = _pack_small([d_g_attn, d_gq[:, :HEAD_DIM], d_gk[:, :HEAD_DIM], d_sinks[:, 0], d_rel_t[:, :N_BUCKETS]])
    (late_all,) = to_chips("chips_in", (IN,), late)

    out = {"grad": {}, "delta": {}, "new_m": {}, "new_v": {}}
    for k in (PG, PP, DOWN, OUT, GATE, UP, IN):
        name = BIG_WEIGHTS[k][0]
        res, _ = _adamw_big(k, sums[k][0], landed[k], to_blocks(k, weights[name]), to_blocks(k, m_in[name]),
                            to_blocks(k, v_in[name]))
        for kind, r in zip(("grad", "delta", "new_m", "new_v"), res):
            out[kind][name] = from_blocks(k, r)
    early_sum, late_sum = _sum_small([early_all, late_all])

    def unpack(packed, at, shape):
        n = math.prod(shape)
        return packed[at:at + -(-n // SMALL_LANES)].reshape(-1)[:n].reshape(shape)

    small_grads = dict(
        w_pool=unpack(early_sum, early_at[0], w_pool.shape), pool_scale=unpack(early_sum, early_at[1], pool_scale.shape),
        g_ffn_norm=unpack(early_sum, early_at[2], g_ffn_norm.shape), g_ple_norm=unpack(early_sum, early_at[3], g_ple_norm.shape),
        g_attn_norm=unpack(late_sum, late_at[0], g_attn_norm.shape), g_q=unpack(late_sum, late_at[1], g_q.shape),
        g_k=unpack(late_sum, late_at[2], g_k.shape), attn_sinks=unpack(late_sum, late_at[3], attn_sinks.shape),
        rel_bias=unpack(late_sum, late_at[4], rel_bias.T.shape))
    loss = early_sum[early_at[4], 0]
    flip = lambda name, arr: arr.T if name == "rel_bias" else arr
    updates = _adamw_small([small_grads[n] for n in SMALL_NAMES], [flip(n, weights[n]) for n in SMALL_NAMES],
                           [flip(n, m_in[n]) for n in SMALL_NAMES], [flip(n, v_in[n]) for n in SMALL_NAMES])
    n_small = len(SMALL_NAMES)
    for i, name in enumerate(SMALL_NAMES):
        out["grad"][name] = flip(name, small_grads[name])
        out["delta"][name] = flip(name, updates[i])
        out["new_m"][name] = flip(name, updates[n_small + i])
        out["new_v"][name] = flip(name, updates[2 * n_small + i])

    _issued.clear()
    order = ("w_in", "w_out", "g_attn_norm", "g_q", "g_k", "attn_sinks", "rel_bias", "w_pool", "pool_scale",
             "g_ffn_norm", "w_gate", "w_up", "w_down", "g_ple_norm", "w_ple_gate", "w_ple_proj")
    return (loss, grad_x[None], *[out["grad"][n] for n in order], *[out["delta"][n] for n in order],
            *[out["new_m"][n] for n in order], *[out["new_v"][n] for n in order])
```

```python
import functools
import math

import jax
import jax.numpy as jnp
import numpy as np
from jax import lax
from jax.experimental import pallas as pl
from jax.experimental.pallas import tpu as pltpu
from jax.experimental.pallas import tpu_sc as plsc

F32 = jnp.float32
BF16 = jnp.bfloat16
MESH = pl.DeviceIdType.MESH

D_MODEL = 1024
HEAD_DIM = 64
ATTN_WIDTH = 512
KV_WIDTH = 128
POOL_WIDTH = 512
POOL_SIZES = (2, 4, 8, 16)
POOL_GROUP = 128
POOL_HALO = 16
IN_WIDTH = 1280
D_FF = 2816
PLE_DIM = 256
BLOCK = 128
N_BUCKETS = 32
MAX_DISTANCE = 128
EPS = 1e-6
N_DEV = 8
N_CHIPS = 4

ADAM_LR = 0.001
ADAM_B1 = 0.9
ADAM_B2 = 0.999
ADAM_EPS = 1e-08
ADAM_WD = 0.01
ADAM_STEP = 10

TOKEN_TILE = 512
FFN_BWD_TILE = 256
FF_CHUNK = 256
ATTN_STEP_BLOCKS = 4
GATE_ROWS_EARLY = 96
UP_ROWS_EARLY = 64
HEADS_A = (0, 2, 5, 7)
HEADS_B = (1, 3, 4, 6)
SMALL_LANES = 128


def _nn(a, b):
    return jnp.dot(a, b, preferred_element_type=F32)


def _nt(a, b):
    return lax.dot_general(a, b, (((1,), (1,)), ((), ())), preferred_element_type=F32)


def _tn(a, b):
    return lax.dot_general(a, b, (((0,), (0,)), ((), ())), preferred_element_type=F32)


def _resident(shape):
    nd = len(shape)
    return pl.BlockSpec(shape, lambda i, _nd=nd: (0,) * _nd, pipeline_mode=pl.Buffered(1))


def _rows(tile, width):
    return pl.BlockSpec((tile, width), lambda i: (i, 0))


def _acc(shape):
    nd = len(shape)
    return pl.BlockSpec(shape, lambda i, _nd=nd: (0,) * _nd)


def _head_mean_matrix(width):
    idx = np.arange(width) // HEAD_DIM
    return jnp.asarray((idx[:, None] == idx[None, :]).astype(np.float32) / HEAD_DIM, dtype=BF16)


def _seg_mean(v, bmat):
    hi = v.astype(BF16)
    lo = (v - hi.astype(F32)).astype(BF16)
    return _nn(hi, bmat) + _nn(lo, bmat)


def _rms(x):
    return lax.rsqrt(jnp.mean(x * x, axis=-1, keepdims=True) + EPS)


def _rms_bwd(d_y, x, r, g):
    gy = d_y * g
    d_x = r * gy - x * (r * r * r) * jnp.mean(gy * x, axis=-1, keepdims=True)
    d_g = jnp.sum(d_y * (x * r), axis=0, keepdims=True)
    return d_x, d_g


def _lane_lo(shape):
    return lax.broadcasted_iota(jnp.int32, shape, 1) < HEAD_DIM


class _Rider:
    def __init__(self, inputs, out_shapes, sems, begin, end, middle=None, aliases=None):
        self.inputs, self.out_shapes, self.sems = list(inputs), list(out_shapes), list(sems)
        self.begin, self.middle, self.end = begin, middle, end
        self.aliases = dict(aliases or {})


_issued = []


def _after_last(args, in_specs):
    extra = list(_issued)
    return list(args) + extra, list(in_specs) + [pl.BlockSpec(memory_space=pl.ANY)] * len(extra), len(extra)


def _mark_issued(out):
    _issued[:] = [out]


def _complete_before_next(arrays):
    _issued.extend(arrays)


def _call(body, args, *, name, grid, in_specs, out_specs, out_shape, scratch_shapes=(), rider=None, streamed=()):
    in_specs, out_specs, out_shape, scratch_shapes = list(in_specs), list(out_specs), list(out_shape), list(scratch_shapes)
    if rider is None:
        n_args, n_w, n_out, n_scr = len(args), len(streamed), len(out_shape), len(scratch_shapes)
        args, in_specs, n_extra = _after_last(list(args) + list(streamed),
                                              in_specs + [pl.BlockSpec(memory_space=pl.ANY)] * n_w)
        chunks = [w.shape[0] // FF_CHUNK for w in streamed]

        def ordered(*refs):
            ins, w_hbm = refs[:n_args], refs[n_args:n_args + n_w]
            rest = refs[n_args + n_w + n_extra:]
            outs, scratch, w_vmem, sems = rest[:n_out], rest[n_out:n_out + n_scr], rest[n_out + n_scr:n_out + n_scr + n_w], rest[n_out + n_scr + n_w:]
            first = pl.program_id(0) == 0

            def piece(i, c):
                rows = pl.ds(c * FF_CHUNK, FF_CHUNK)
                return pltpu.make_async_copy(w_hbm[i].at[rows, :], w_vmem[i].at[rows, :], sems[i].at[c])

            @pl.when(first)
            def _():
                for c in range(max(chunks, default=0)):
                    for i in range(n_w):
                        if c < chunks[i]:
                            piece(i, c).start()

            def arrived(i, first_chunk=0, end_chunk=None):
                @pl.when(first)
                def _():
                    for c in range(first_chunk, chunks[i] if end_chunk is None else end_chunk):
                        piece(i, c).wait()

            body(*ins, *w_vmem, *outs, *scratch, *((arrived,) if n_w else ()))

        outs = pl.pallas_call(
            ordered, name=name, grid=grid, in_specs=in_specs, out_specs=out_specs, out_shape=out_shape,
            scratch_shapes=scratch_shapes + [pltpu.VMEM(w.shape, w.dtype) for w in streamed]
            + [pltpu.SemaphoreType.DMA((n,)) for n in chunks])(*args)
        _mark_issued(outs[0])
        return list(outs), []
    n_in, n_out, n_scr = len(in_specs), len(out_shape), len(scratch_shapes)
    r_in, r_out = len(rider.inputs), len(rider.out_shapes)
    n_steps = grid[0]

    def hosted(*refs):
        ins, refs = refs[:n_in], refs[n_in:]
        r_ins, refs = refs[:r_in], refs[r_in:]
        outs, refs = refs[:n_out], refs[n_out:]
        r_outs, refs = refs[:r_out], refs[r_out:]
        scratch, r_sems = refs[:n_scr], refs[n_scr:]
        step = pl.program_id(0)

        @pl.when(step == 0)
        def _():
            rider.begin(r_ins, r_outs, r_sems)

        if rider.middle is not None:
            @pl.when(step == n_steps - 1)
            def _():
                rider.middle(r_ins, r_outs, r_sems)

        body(*ins, *outs, *scratch)

        @pl.when(step == n_steps - 1)
        def _():
            rider.end(r_ins, r_outs, r_sems)

    any_spec = pl.BlockSpec(memory_space=pl.ANY)
    outs = pl.pallas_call(
        hosted, name=name, grid=grid,
        in_specs=in_specs + [any_spec] * r_in,
        out_specs=out_specs + [any_spec] * r_out,
        out_shape=out_shape + rider.out_shapes,
        scratch_shapes=scratch_shapes + rider.sems,
        input_output_aliases={n_in + i: n_out + o for i, o in rider.aliases.items()},
    )(*args, *rider.inputs)
    return list(outs[:n_out]), list(outs[n_out:])


def _in_proj(x, g_attn, w_in_t, gq_t, gk_t, rider=None):
    s = x.shape[0]
    ts = min(TOKEN_TILE, s)

    def body(x_ref, g_ref, w_ref, gq_ref, gk_ref, bq_ref, bk_ref, zqk_ref, qn_ref, kn_ref, v_ref, u_ref):
        xf = x_ref[...]
        hn = ((xf * _rms(xf)) * g_ref[...]).astype(BF16)
        z = _nt(hn, w_ref[...])
        q = z[:, :ATTN_WIDTH]
        k = z[:, ATTN_WIDTH:ATTN_WIDTH + KV_WIDTH]
        zqk_ref[...] = z[:, :ATTN_WIDTH + KV_WIDTH]
        rq = lax.rsqrt(_seg_mean(q * q, bq_ref[...]) + EPS)
        qn_ref[...] = ((q * rq) * gq_ref[...]).astype(BF16)
        rk = lax.rsqrt(_seg_mean(k * k, bk_ref[...]) + EPS)
        kn_ref[...] = ((k * rk) * gk_ref[...]).astype(BF16)
        v_ref[...] = z[:, ATTN_WIDTH + KV_WIDTH:ATTN_WIDTH + 2 * KV_WIDTH].astype(BF16)
        u_ref[...] = z[:, ATTN_WIDTH + 2 * KV_WIDTH:]

    return _call(
        body,
        (x, g_attn, w_in_t, gq_t, gk_t, _head_mean_matrix(ATTN_WIDTH), _head_mean_matrix(KV_WIDTH)),
        name="in_proj",
        grid=(s // ts,),
        in_specs=[
            _rows(ts, D_MODEL),
            _resident((1, D_MODEL)),
            _resident((IN_WIDTH, D_MODEL)),
            _resident((1, ATTN_WIDTH)),
            _resident((1, KV_WIDTH)),
            _resident((ATTN_WIDTH, ATTN_WIDTH)),
            _resident((KV_WIDTH, KV_WIDTH)),
        ],
        out_specs=[
            _rows(ts, ATTN_WIDTH + KV_WIDTH),
            _rows(ts, ATTN_WIDTH),
            _rows(ts, KV_WIDTH),
            _rows(ts, KV_WIDTH),
            _rows(ts, POOL_WIDTH),
        ],
        out_shape=[
            jax.ShapeDtypeStruct((s, ATTN_WIDTH + KV_WIDTH), F32),
            jax.ShapeDtypeStruct((s, ATTN_WIDTH), BF16),
            jax.ShapeDtypeStruct((s, KV_WIDTH), BF16),
            jax.ShapeDtypeStruct((s, KV_WIDTH), BF16),
            jax.ShapeDtypeStruct((s, POOL_WIDTH), F32),
        ],
        rider=rider,
    )


def _bucket_ranges():
    n = np.arange(MAX_DISTANCE)
    max_exact = N_BUCKETS // 2
    nf = np.maximum(n, 1).astype(np.float64)
    large = max_exact + (np.log(nf / max_exact) / math.log(MAX_DISTANCE / max_exact) * (N_BUCKETS - max_exact)).astype(np.int64)
    bucket = np.where(n < max_exact, n, np.minimum(large, N_BUCKETS - 1))
    out = []
    for b in range(N_BUCKETS):
        idx = np.nonzero(bucket == b)[0]
        out.append((int(idx.min()), int(idx.max()) + 1))
    return out


def _band_distance():
    i = lax.broadcasted_iota(jnp.int32, (BLOCK, 2 * BLOCK), 0)
    j = lax.broadcasted_iota(jnp.int32, (BLOCK, 2 * BLOCK), 1)
    return BLOCK + i - j


def _bias_table(rel_bias_t):
    ranges = _bucket_ranges()

    def body(rb_ref, tab_ref):
        d = _band_distance()
        for half, heads in enumerate((HEADS_A, HEADS_B)):
            for slot, h in enumerate(heads):
                t = jnp.full((BLOCK, 2 * BLOCK), -jnp.inf, F32)
                for b, (lo, hi) in enumerate(ranges):
                    t = jnp.where((d >= lo) & (d < hi), rb_ref[h, b], t)
                tab_ref[half, slot * BLOCK:(slot + 1) * BLOCK, :] = t

    return pl.pallas_call(
        body,
        name="bias_table",
        in_specs=[pl.BlockSpec(memory_space=pltpu.SMEM)],
        out_shape=jax.ShapeDtypeStruct((2, 4 * BLOCK, 2 * BLOCK), F32),
    )(rel_bias_t)


def _bias_table_bwd(dl_acc, rider=None):
    ranges = _bucket_ranges()
    n_heads = len(HEADS_A) + len(HEADS_B)

    def body(dl_ref, out_ref):
        d = _band_distance()
        row = lax.broadcasted_iota(jnp.int32, (n_heads, SMALL_LANES), 0)
        lane = lax.broadcasted_iota(jnp.int32, (n_heads, SMALL_LANES), 1)
        out = jnp.zeros((n_heads, SMALL_LANES), F32)
        for b, (lo, hi) in enumerate(ranges):
            in_bucket = (d >= lo) & (d < hi)
            for half, heads in enumerate((HEADS_A, HEADS_B)):
                for slot, h in enumerate(heads):
                    g = dl_ref[half, slot * BLOCK:(slot + 1) * BLOCK, :]
                    part = jnp.sum(jnp.where(in_bucket, g, 0.0), axis=0, keepdims=True)
                    tot = jnp.sum(part, axis=1, keepdims=True)
                    out = jnp.where((row == h) & (lane == b), tot, out)
        out_ref[...] = out

    return _call(
        body,
        (dl_acc,),
        name="bias_table_bwd",
        grid=(1,),
        in_specs=[_acc((2, 4 * BLOCK, 2 * BLOCK))],
        out_specs=[_acc((n_heads, SMALL_LANES))],
        out_shape=[jax.ShapeDtypeStruct((n_heads, SMALL_LANES), F32)],
        rider=rider,
    )


def _stack_heads(pairs, lo_mask):
    zero = jnp.zeros_like(pairs[0])
    lo = [jnp.where(lo_mask, t, zero) for t in pairs]
    hi = [jnp.where(lo_mask, zero, t) for t in pairs]
    return (jnp.concatenate([lo[0], lo[1], hi[2], hi[3]], axis=0),
            jnp.concatenate([hi[0], hi[1], lo[2], lo[3]], axis=0))


def _unstack_heads(out_a, out_b, lo_mask):
    t = lambda x, r: x[r * BLOCK:(r + 1) * BLOCK, :]
    return [
        jnp.where(lo_mask, t(out_a, 0), t(out_b, 0)),
        jnp.where(lo_mask, t(out_a, 1), t(out_b, 1)),
        jnp.where(lo_mask, t(out_b, 2), t(out_a, 2)),
        jnp.where(lo_mask, t(out_b, 3), t(out_a, 3)),
    ]


def _sink_column(sink_ref, heads):
    row = lax.broadcasted_iota(jnp.int32, (4 * BLOCK, 1), 0)
    col = jnp.full((4 * BLOCK, 1), sink_ref[0, heads[3]], F32)
    for slot in (2, 1, 0):
        col = jnp.where(row < (slot + 1) * BLOCK, sink_ref[0, heads[slot]], col)
    return col


def _band_probs(q_stack, keys, tab, sink, first_block):
    s = _nt(q_stack, keys) * (HEAD_DIM ** -0.5) + tab
    if first_block is not None:
        col = lax.broadcasted_iota(jnp.int32, s.shape, 1)
        s = jnp.where(jnp.logical_and(first_block, col < BLOCK), -jnp.inf, s)
    m = jnp.maximum(jnp.max(s, axis=-1, keepdims=True), sink)
    e = jnp.exp(s - m)
    e_sink = jnp.exp(sink - m)
    den = jnp.sum(e, axis=-1, keepdims=True) + e_sink
    return e / den, e_sink / den


def _attn_specs(n_groups):
    group = lambda n: (jnp.minimum(n, n_groups - 1), 0)
    prev = lambda n: (jnp.maximum(jnp.minimum(n, n_groups - 1) * ATTN_STEP_BLOCKS - 1, 0), 0)
    return group, prev


def _band(prev_ref, group_ref, b):
    rows = lambda i: group_ref[i * BLOCK:(i + 1) * BLOCK, :]
    band = jnp.concatenate([prev_ref[...] if b == 0 else rows(b - 1), rows(b)], axis=0)
    return band, pltpu.roll(band, HEAD_DIM, 1)


def _attn_fwd(qn, kn, v, tab, sinks, rider=None):
    s = qn.shape[0]
    n_groups = s // (ATTN_STEP_BLOCKS * BLOCK)
    group, prev = _attn_specs(n_groups)
    rows = ATTN_STEP_BLOCKS * BLOCK

    def body(sink_ref, q_ref, kc_ref, kp_ref, vc_ref, vp_ref, tab_ref, o_ref):
        first = pl.program_id(0) == 0
        lo_mask = _lane_lo((BLOCK, BLOCK))
        for b in range(ATTN_STEP_BLOCKS):
            at = slice(b * BLOCK, (b + 1) * BLOCK)
            kk, kk_sw = _band(kp_ref, kc_ref, b)
            vv, vv_sw = _band(vp_ref, vc_ref, b)
            q_a, q_b = _stack_heads([q_ref[at, p * BLOCK:(p + 1) * BLOCK] for p in range(4)], lo_mask)
            no_prev = first if b == 0 else None
            p_a, _ = _band_probs(q_a, kk, tab_ref[0], _sink_column(sink_ref, HEADS_A), no_prev)
            p_b, _ = _band_probs(q_b, kk_sw, tab_ref[1], _sink_column(sink_ref, HEADS_B), no_prev)
            out = _unstack_heads(_nn(p_a.astype(BF16), vv), _nn(p_b.astype(BF16), vv_sw), lo_mask)
            for p in range(4):
                o_ref[at, p * BLOCK:(p + 1) * BLOCK] = out[p].astype(BF16)

    return _call(
        body,
        (sinks, qn, kn, kn, v, v, tab),
        name="attn_fwd",
        grid=(n_groups,),
        in_specs=[
            pl.BlockSpec(memory_space=pltpu.SMEM),
            pl.BlockSpec((rows, ATTN_WIDTH), group),
            pl.BlockSpec((rows, KV_WIDTH), group),
            pl.BlockSpec((BLOCK, KV_WIDTH), prev),
            pl.BlockSpec((rows, KV_WIDTH), group),
            pl.BlockSpec((BLOCK, KV_WIDTH), prev),
            _resident((2, 4 * BLOCK, 2 * BLOCK)),
        ],
        out_specs=[pl.BlockSpec((rows, ATTN_WIDTH), group)],
        out_shape=[jax.ShapeDtypeStruct((s, ATTN_WIDTH), BF16)],
        rider=rider,
    )


def _pooled(u_tile, u_halo, tile_index, tile_rows):
    halo = jnp.where(tile_index > 0, u_halo, 0.0)
    ext = jnp.concatenate([halo, u_tile], axis=0)
    sums = []
    acc = ext
    for shift in (1, 2, 4, 8):
        acc = acc + pltpu.roll(acc, shift, 0)
        sums.append(acc)
    t = tile_index * tile_rows + lax.broadcasted_iota(jnp.int32, (tile_rows, 1), 0)
    out = []
    for g, w in enumerate(POOL_SIZES):
        lanes = slice(g * POOL_GROUP, (g + 1) * POOL_GROUP)
        cnt = jnp.minimum(t + 1, w).astype(F32)
        out.append(sums[g][POOL_HALO:, lanes] / cnt - u_tile[:, lanes])
    return out


def _halo_before(tile):
    return lambda i: (jnp.maximum(i * (tile // POOL_HALO) - 1, 0), 0)


def _mix_out(u, a, x, w_out, w_pool, pool_scale, g_ffn, rider=None):
    s = x.shape[0]
    ts = min(TOKEN_TILE, s)

    def body(u_ref, uh_ref, a_ref, x_ref, wo_ref, wp_ref, sc_ref, g_ref, h1_ref, hn_ref, m_ref):
        i = pl.program_id(0)
        pooled = _pooled(u_ref[...], uh_ref[...], i, ts)
        for g in range(len(POOL_SIZES)):
            lanes = slice(g * POOL_GROUP, (g + 1) * POOL_GROUP)
            y = _nn(pooled[g].astype(BF16), wp_ref[g].astype(BF16))
            m_ref[:, lanes] = (y * sc_ref[:, lanes]).astype(BF16)
        h1 = x_ref[...] + _nn(a_ref[...], wo_ref[:ATTN_WIDTH, :]) + _nn(m_ref[...], wo_ref[ATTN_WIDTH:, :])
        h1_ref[...] = h1
        hn_ref[...] = ((h1 * _rms(h1)) * g_ref[...]).astype(BF16)

    return _call(
        body,
        (u, u, a, x, w_out, w_pool, pool_scale, g_ffn),
        name="mix_out",
        grid=(s // ts,),
        in_specs=[
            _rows(ts, POOL_WIDTH),
            pl.BlockSpec((POOL_HALO, POOL_WIDTH), _halo_before(ts)),
            _rows(ts, ATTN_WIDTH),
            _rows(ts, D_MODEL),
            _resident((D_MODEL, D_MODEL)),
            _resident((len(POOL_SIZES), POOL_GROUP, POOL_GROUP)),
            _resident((1, POOL_WIDTH)),
            _resident((1, D_MODEL)),
        ],
        out_specs=[_rows(ts, D_MODEL), _rows(ts, D_MODEL), _rows(ts, POOL_WIDTH)],
        out_shape=[
            jax.ShapeDtypeStruct((s, D_MODEL), F32),
            jax.ShapeDtypeStruct((s, D_MODEL), BF16),
            jax.ShapeDtypeStruct((s, POOL_WIDTH), BF16),
        ],
        rider=rider,
    )


def _ffn_up(hn2, wg_t, wu_t, rider=None):
    s = hn2.shape[0]
    ts = min(TOKEN_TILE, s)

    def body(hn_ref, wg_ref, wu_ref, gt_ref, up_ref, arrived):
        hn = hn_ref[...]
        for c in range(D_FF // FF_CHUNK):
            cols = slice(c * FF_CHUNK, (c + 1) * FF_CHUNK)
            arrived(0, c, c + 1)
            gt_ref[:, cols] = _nt(hn, wg_ref[cols, :]).astype(BF16)
            arrived(1, c, c + 1)
            up_ref[:, cols] = _nt(hn, wu_ref[cols, :]).astype(BF16)

    return _call(
        body,
        (hn2,),
        name="ffn_up",
        grid=(s // ts,),
        in_specs=[_rows(ts, D_MODEL)],
        out_specs=[_rows(ts, D_FF), _rows(ts, D_FF)],
        out_shape=[jax.ShapeDtypeStruct((s, D_FF), BF16), jax.ShapeDtypeStruct((s, D_FF), BF16)],
        rider=rider,
        streamed=(wg_t, wu_t),
    )


def _silu_mul(gt, up):
    return (gt * jax.nn.sigmoid(gt)) * up


def _ffn_down_ple(gt, up, h1, w_down, p, target, g_ple, w_pg, w_pp, rider=None):
    s = h1.shape[0]
    ts = min(TOKEN_TILE, s)
    blk = D_MODEL // N_DEV

    def body(gt_ref, up_ref, h1_ref, p_ref, t_ref, g_ref, wpp_ref, wd_ref, wpg_ref,
             loss_ref, dh_ref, dwpg_ref, dwpp_ref, dg_ref, act_ref, pp_ref, arrived):
        @pl.when(pl.program_id(0) == 0)
        def _():
            loss_ref[...] = jnp.zeros_like(loss_ref)
            dwpg_ref[...] = jnp.zeros_like(dwpg_ref)
            dwpp_ref[...] = jnp.zeros_like(dwpp_ref)
            dg_ref[...] = jnp.zeros_like(dg_ref)

        for c in range(D_FF // FF_CHUNK):
            cols = slice(c * FF_CHUNK, (c + 1) * FF_CHUNK)
            act_ref[:, cols] = _silu_mul(gt_ref[:, cols].astype(F32), up_ref[:, cols].astype(F32)).astype(BF16)
        arrived(0)
        h2v = h1_ref[...] + _nn(act_ref[...], wd_ref[...])
        r = _rms(h2v)
        hn = ((h2v * r) * g_ref[...]).astype(BF16)
        arrived(1)
        gate = jax.nn.sigmoid(_nn(hn, wpg_ref[...]))
        pb = p_ref[...].astype(BF16)
        for j in range(N_DEV):
            pp_ref[:, j * blk:(j + 1) * blk] = _nn(pb, wpp_ref[j])
        pp = pp_ref[...]
        diff = (h2v + gate * pp) - t_ref[...]
        loss_ref[...] += jnp.sum(jnp.sum(diff * diff, axis=0, keepdims=True), axis=1, keepdims=True) * (0.5 / D_MODEL)
        dy = diff * (1.0 / D_MODEL)
        d_pp = (dy * gate).astype(BF16)
        d_pre = ((dy * pp) * (gate * (1.0 - gate))).astype(BF16)
        for j in range(N_DEV):
            dwpp_ref[j] += _tn(pb, d_pp[:, j * blk:(j + 1) * blk])
        dwpg_ref[...] += _tn(hn, d_pre)
        d_x, d_g = _rms_bwd(_nt(d_pre, wpg_ref[...]), h2v, r, g_ref[...])
        dg_ref[...] += d_g
        dh_ref[...] = dy + d_x

    return _call(
        body,
        (gt, up, h1, p, target, g_ple, w_pp),
        name="ffn_down_ple",
        grid=(s // ts,),
        in_specs=[
            _rows(ts, D_FF),
            _rows(ts, D_FF),
            _rows(ts, D_MODEL),
            _rows(ts, PLE_DIM),
            _rows(ts, D_MODEL),
            _resident((1, D_MODEL)),
            _resident((N_DEV, PLE_DIM, blk)),
        ],
        out_specs=[
            _acc((1, SMALL_LANES)),
            _rows(ts, D_MODEL),
            _acc((D_MODEL, D_MODEL)),
            _acc((N_DEV, PLE_DIM, blk)),
            _acc((1, D_MODEL)),
        ],
        out_shape=[
            jax.ShapeDtypeStruct((1, SMALL_LANES), F32),
            jax.ShapeDtypeStruct((s, D_MODEL), F32),
            jax.ShapeDtypeStruct((D_MODEL, D_MODEL), F32),
            jax.ShapeDtypeStruct((N_DEV, PLE_DIM, blk), F32),
            jax.ShapeDtypeStruct((1, D_MODEL), F32),
        ],
        scratch_shapes=[pltpu.VMEM((ts, D_FF), BF16), pltpu.VMEM((ts, D_MODEL), F32)],
        rider=rider,
        streamed=(w_down, w_pg),
    )


def _ffn_bwd_act(dh2, h1, gt, up, g_ffn, wg_t, wu_t, w_down, rider=None):
    s = h1.shape[0]
    ts = min(FFN_BWD_TILE, s)

    def body(dh_ref, h1_ref, gt_ref, up_ref, g_ref, wd_ref, wg_ref, wu_ref,
             dgt_ref, dup_ref, dh1_ref, dh1b_ref, dg_ref, dwd_ref, act_ref, arrived):
        @pl.when(pl.program_id(0) == 0)
        def _():
            dg_ref[...] = jnp.zeros_like(dg_ref)
            dwd_ref[...] = jnp.zeros_like(dwd_ref)

        dhb = dh_ref[...].astype(BF16)
        for c in range(D_FF // FF_CHUNK):
            cols = slice(c * FF_CHUNK, (c + 1) * FF_CHUNK)
            arrived(0, c, c + 1)
            d_act = _nt(dhb, wd_ref[cols, :])
            gtv = gt_ref[:, cols].astype(F32)
            upv = up_ref[:, cols].astype(F32)
            sg = jax.nn.sigmoid(gtv)
            silu = gtv * sg
            act_ref[:, cols] = (silu * upv).astype(BF16)
            dup_ref[:, cols] = (d_act * silu).astype(BF16)
            dgt_ref[:, cols] = ((d_act * upv) * (sg * (1.0 + gtv * (1.0 - sg)))).astype(BF16)
        dwd_ref[...] += _tn(act_ref[...], dhb)
        arrived(1)
        arrived(2)
        d_hn = _nn(dgt_ref[...], wg_ref[...]) + _nn(dup_ref[...], wu_ref[...])
        h1v = h1_ref[...]
        d_x, d_g = _rms_bwd(d_hn, h1v, _rms(h1v), g_ref[...])
        dg_ref[...] += d_g
        dh1 = dh_ref[...] + d_x
        dh1_ref[...] = dh1
        dh1b_ref[...] = dh1.astype(BF16)

    return _call(
        body,
        (dh2, h1, gt, up, g_ffn),
        name="ffn_bwd_act",
        grid=(s // ts,),
        in_specs=[
            _rows(ts, D_MODEL),
            _rows(ts, D_MODEL),
            _rows(ts, D_FF),
            _rows(ts, D_FF),
            _resident((1, D_MODEL)),
        ],
        out_specs=[
            _rows(ts, D_FF), _rows(ts, D_FF),
            _rows(ts, D_MODEL), _rows(ts, D_MODEL), _acc((1, D_MODEL)), _acc((D_FF, D_MODEL)),
        ],
        out_shape=[
            jax.ShapeDtypeStruct((s, D_FF), BF16),
            jax.ShapeDtypeStruct((s, D_FF), BF16),
            jax.ShapeDtypeStruct((s, D_MODEL), F32),
            jax.ShapeDtypeStruct((s, D_MODEL), BF16),
            jax.ShapeDtypeStruct((1, D_MODEL), F32),
            jax.ShapeDtypeStruct((D_FF, D_MODEL), F32),
        ],
        scratch_shapes=[pltpu.VMEM((ts, D_FF), BF16)],
        rider=rider,
        streamed=(w_down, wg_t, wu_t),
    )


def _ffn_bwd_w(dgt, dup, hn2, rider=None):
    s = hn2.shape[0]
    slab = pl.BlockSpec((s, FF_CHUNK), lambda i: (0, i))

    def body(dgt_ref, dup_ref, hn_ref, dwg_ref, dwu_ref):
        hn = hn_ref[...]
        dwg_ref[...] = _tn(dgt_ref[...], hn)
        dwu_ref[...] = _tn(dup_ref[...], hn)

    return _call(
        body,
        (dgt, dup, hn2),
        name="ffn_bwd_w",
        grid=(D_FF // FF_CHUNK,),
        in_specs=[slab, slab, _resident((s, D_MODEL))],
        out_specs=[_rows(FF_CHUNK, D_MODEL)] * 2,
        out_shape=[jax.ShapeDtypeStruct((D_FF, D_MODEL), F32)] * 2,
        rider=rider,
    )


def _mix_bwd(dh1b, u, w_out, w_pool, pool_scale, rider=None):
    s = u.shape[0]
    ts = min(TOKEN_TILE, s)
    nt = s // ts
    halo_after = lambda i: (jnp.minimum((i + 1) * (ts // POOL_HALO), s // POOL_HALO - 1), 0)
    n_groups = len(POOL_SIZES)

    def body(dh_ref, dhn_ref, u_ref, uh_ref, wo_ref, wp_ref, sc_ref, da_ref, du_ref, dwp_ref, dsc_ref):
        i = pl.program_id(0)

        @pl.when(i == 0)
        def _():
            dwp_ref[...] = jnp.zeros_like(dwp_ref)
            dsc_ref[...] = jnp.zeros_like(dsc_ref)

        dh = dh_ref[...]
        da_ref[...] = _nt(dh, wo_ref[:ATTN_WIDTH, :])
        dh_next = jnp.where(i < nt - 1, dhn_ref[...], jnp.zeros_like(dhn_ref))
        dm_ext = _nt(jnp.concatenate([dh, dh_next], axis=0), wo_ref[ATTN_WIDTH:, :])
        pooled = _pooled(u_ref[...], uh_ref[...], i, ts)
        t_ext = i * ts + lax.broadcasted_iota(jnp.int32, (ts + POOL_HALO, 1), 0)
        for g, w in enumerate(POOL_SIZES):
            lanes = slice(g * POOL_GROUP, (g + 1) * POOL_GROUP)
            wp = wp_ref[g].astype(BF16)
            pg = pooled[g].astype(BF16)
            dm_g = dm_ext[:, lanes]
            dsc_ref[:, lanes] += jnp.sum(dm_g[:ts, :] * _nn(pg, wp), axis=0, keepdims=True)
            dy = (dm_g * sc_ref[:, lanes]).astype(BF16)
            dwp_ref[g] += _tn(pg, dy[:ts, :])
            d_pool = _nt(dy, wp)
            acc = d_pool / jnp.minimum(t_ext + 1, w).astype(F32)
            shift = 1
            while shift < w:
                acc = acc + pltpu.roll(acc, ts + POOL_HALO - shift, 0)
                shift *= 2
            du_ref[:, lanes] = acc[:ts, :] - d_pool[:ts, :]

    return _call(
        body,
        (dh1b, dh1b, u, u, w_out, w_pool, pool_scale),
        name="mix_bwd",
        grid=(nt,),
        in_specs=[
            _rows(ts, D_MODEL),
            pl.BlockSpec((POOL_HALO, D_MODEL), halo_after),
            _rows(ts, POOL_WIDTH),
            pl.BlockSpec((POOL_HALO, POOL_WIDTH), _halo_before(ts)),
            _resident((D_MODEL, D_MODEL)),
            _resident((n_groups, POOL_GROUP, POOL_GROUP)),
            _resident((1, POOL_WIDTH)),
        ],
        out_specs=[
            _rows(ts, ATTN_WIDTH),
            _rows(ts, POOL_WIDTH),
            _acc((n_groups, POOL_GROUP, POOL_GROUP)),
            _acc((1, POOL_WIDTH)),
        ],
        out_shape=[
            jax.ShapeDtypeStruct((s, ATTN_WIDTH), F32),
            jax.ShapeDtypeStruct((s, POOL_WIDTH), F32),
            jax.ShapeDtypeStruct((n_groups, POOL_GROUP, POOL_GROUP), F32),
            jax.ShapeDtypeStruct((1, POOL_WIDTH), F32),
        ],
        rider=rider,
    )


def _out_w_bwd(a, m, dh1b, rider=None):
    s = dh1b.shape[0]

    def body(a_ref, m_ref, dh_ref, dw_ref):
        @pl.when(pl.program_id(0) == 0)
        def _():
            dw_ref[...] = _tn(a_ref[...], dh_ref[...])

        @pl.when(pl.program_id(0) == 1)
        def _():
            dw_ref[...] = _tn(m_ref[...], dh_ref[...])

    return _call(
        body,
        (a, m, dh1b),
        name="out_w_bwd",
        grid=(2,),
        in_specs=[_resident((s, ATTN_WIDTH)), _resident((s, POOL_WIDTH)), _resident((s, D_MODEL))],
        out_specs=[_rows(ATTN_WIDTH, D_MODEL)],
        out_shape=[jax.ShapeDtypeStruct((D_MODEL, D_MODEL), F32)],
        rider=rider,
    )


def _attn_bwd(qn, kn, v, a, da, tab, sinks, rider=None):
    s = qn.shape[0]
    qb = ATTN_STEP_BLOCKS
    rows = qb * BLOCK
    n_groups = s // rows
    group, prev = _attn_specs(n_groups)
    done = lambda n: (jnp.maximum(n - 1, 0), 0)

    def body(sink_ref, q_ref, kc_ref, kp_ref, vc_ref, vp_ref, o_ref, do_ref, tab_ref,
             dq_ref, dk_ref, dv_ref, dl_ref, ds_ref, k_carry, v_carry, sink_acc):
        n = pl.program_id(0)

        @pl.when(n == 0)
        def _():
            dl_ref[...] = jnp.zeros_like(dl_ref)
            k_carry[...] = jnp.zeros_like(k_carry)
            v_carry[...] = jnp.zeros_like(v_carry)
            sink_acc[...] = jnp.zeros_like(sink_acc)

        @pl.when(n < n_groups)
        def _():
            first = n == 0
            lo_mask = _lane_lo((BLOCK, BLOCK))
            dks, dvs = [], []
            for b in range(qb):
                at = slice(b * BLOCK, (b + 1) * BLOCK)
                keys = _band(kp_ref, kc_ref, b)
                vals = _band(vp_ref, vc_ref, b)
                q_st = _stack_heads([q_ref[at, p * BLOCK:(p + 1) * BLOCK] for p in range(4)], lo_mask)
                do_st = _stack_heads([do_ref[at, p * BLOCK:(p + 1) * BLOCK] for p in range(4)], lo_mask)
                o_st = _stack_heads([o_ref[at, p * BLOCK:(p + 1) * BLOCK].astype(F32) for p in range(4)], lo_mask)
                dq_st, dk_parts, dv_parts = [], [], []
                for half, heads in enumerate((HEADS_A, HEADS_B)):
                    probs, p_sink = _band_probs(q_st[half], keys[half], tab_ref[half], _sink_column(sink_ref, heads),
                                                first if b == 0 else None)
                    delta = jnp.sum(do_st[half] * o_st[half], axis=-1, keepdims=True)
                    dob = do_st[half].astype(BF16)
                    dl = probs * (_nt(dob, vals[half]) - delta)
                    dl_ref[half] += dl
                    sink_acc[half] += p_sink * delta
                    dsb = (dl * (HEAD_DIM ** -0.5)).astype(BF16)
                    dq_st.append(_nn(dsb, keys[half]))
                    dk_parts.append(_tn(dsb, q_st[half]))
                    dv_parts.append(_tn(probs.astype(BF16), dob))
                dq = _unstack_heads(dq_st[0], dq_st[1], lo_mask)
                for p in range(4):
                    dq_ref[at, p * BLOCK:(p + 1) * BLOCK] = dq[p]
                dks.append(dk_parts[0] + pltpu.roll(dk_parts[1], HEAD_DIM, 1))
                dvs.append(dv_parts[0] + pltpu.roll(dv_parts[1], HEAD_DIM, 1))
            last = slice((qb - 1) * BLOCK, qb * BLOCK)
            for parts, out_ref, carry in ((dks, dk_ref, k_carry), (dvs, dv_ref, v_carry)):
                out_ref[...] = carry[...]
                out_ref[last, :] += parts[0][:BLOCK, :]
                for b in range(qb):
                    own = parts[b][BLOCK:, :]
                    carry[b * BLOCK:(b + 1) * BLOCK, :] = own + parts[b + 1][:BLOCK, :] if b + 1 < qb else own

        @pl.when(n == n_groups)
        def _():
            dk_ref[...] = k_carry[...]
            dv_ref[...] = v_carry[...]
            for half, heads in enumerate((HEADS_A, HEADS_B)):
                for slot, h in enumerate(heads):
                    tot = jnp.sum(sink_acc[half, slot * BLOCK:(slot + 1) * BLOCK, :], axis=0, keepdims=True)
                    ds_ref[h:h + 1, :] = jnp.broadcast_to(-tot, (1, SMALL_LANES))

    return _call(
        body,
        (sinks, qn, kn, kn, v, v, a, da, tab),
        name="attn_bwd",
        grid=(n_groups + 1,),
        in_specs=[
            pl.BlockSpec(memory_space=pltpu.SMEM),
            pl.BlockSpec((rows, ATTN_WIDTH), group),
            pl.BlockSpec((rows, KV_WIDTH), group),
            pl.BlockSpec((BLOCK, KV_WIDTH), prev),
            pl.BlockSpec((rows, KV_WIDTH), group),
            pl.BlockSpec((BLOCK, KV_WIDTH), prev),
            pl.BlockSpec((rows, ATTN_WIDTH), group),
            pl.BlockSpec((rows, ATTN_WIDTH), group),
            _resident((2, 4 * BLOCK, 2 * BLOCK)),
        ],
        out_specs=[
            pl.BlockSpec((rows, ATTN_WIDTH), group),
            pl.BlockSpec((rows, KV_WIDTH), done),
            pl.BlockSpec((rows, KV_WIDTH), done),
            _acc((2, 4 * BLOCK, 2 * BLOCK)),
            _acc((N_DEV, SMALL_LANES)),
        ],
        out_shape=[
            jax.ShapeDtypeStruct((s, ATTN_WIDTH), F32),
            jax.ShapeDtypeStruct((s, KV_WIDTH), F32),
            jax.ShapeDtypeStruct((s, KV_WIDTH), F32),
            jax.ShapeDtypeStruct((2, 4 * BLOCK, 2 * BLOCK), F32),
            jax.ShapeDtypeStruct((N_DEV, SMALL_LANES), F32),
        ],
        scratch_shapes=[
            pltpu.VMEM((rows, KV_WIDTH), F32),
            pltpu.VMEM((rows, KV_WIDTH), F32),
            pltpu.VMEM((2, 4 * BLOCK, 1), F32),
        ],
        rider=rider,
    )


def _fold_heads(acc):
    t = acc + pltpu.roll(acc, HEAD_DIM, 1)
    out = t[:, :SMALL_LANES]
    for g in range(1, acc.shape[1] // SMALL_LANES):
        out = out + t[:, g * SMALL_LANES:(g + 1) * SMALL_LANES]
    return out


def _in_proj_bwd(dqn, dkn, dv, du, zqk, x, dh1, g_attn, gq_t, gk_t, w_in_t, rider=None):
    s = x.shape[0]
    ts = min(TOKEN_TILE, s)
    nt = s // ts

    def head_norm_bwd(d_n, raw, g_t, bmat):
        r = lax.rsqrt(_seg_mean(raw * raw, bmat) + EPS)
        gy = d_n * g_t
        d_raw = r * gy - raw * (r * r * r) * _seg_mean(gy * raw, bmat)
        return d_raw, jnp.sum(d_n * (raw * r), axis=0, keepdims=True)

    def body(dqn_ref, dkn_ref, dv_ref, du_ref, zqk_ref, x_ref, dh1_ref, g_ref, gq_ref, gk_ref, w_ref, bq_ref, bk_ref,
             gx_ref, dw_ref, dg_ref, dgq_ref, dgk_ref, dz_ref, gq_acc, gk_acc):
        i = pl.program_id(0)

        @pl.when(i == 0)
        def _():
            dw_ref[...] = jnp.zeros_like(dw_ref)
            dg_ref[...] = jnp.zeros_like(dg_ref)
            gq_acc[...] = jnp.zeros_like(gq_acc)
            gk_acc[...] = jnp.zeros_like(gk_acc)

        d_q, d_gq = head_norm_bwd(dqn_ref[...], zqk_ref[:, :ATTN_WIDTH], gq_ref[...], bq_ref[...])
        d_k, d_gk = head_norm_bwd(dkn_ref[...], zqk_ref[:, ATTN_WIDTH:], gk_ref[...], bk_ref[...])
        gq_acc[...] += d_gq
        gk_acc[...] += d_gk
        dz_ref[:, :ATTN_WIDTH] = d_q.astype(BF16)
        dz_ref[:, ATTN_WIDTH:ATTN_WIDTH + KV_WIDTH] = d_k.astype(BF16)
        dz_ref[:, ATTN_WIDTH + KV_WIDTH:ATTN_WIDTH + 2 * KV_WIDTH] = dv_ref[...].astype(BF16)
        dz_ref[:, ATTN_WIDTH + 2 * KV_WIDTH:] = du_ref[...].astype(BF16)
        dz = dz_ref[...]
        xf = x_ref[...]
        r = _rms(xf)
        hn = ((xf * r) * g_ref[...]).astype(BF16)
        dw_ref[...] += _tn(dz, hn)
        d_x, d_g = _rms_bwd(_nn(dz, w_ref[...]), xf, r, g_ref[...])
        dg_ref[...] += d_g
        gx_ref[...] = dh1_ref[...] + d_x

        @pl.when(i == nt - 1)
        def _():
            dgq_ref[...] = _fold_heads(gq_acc[...])
            dgk_ref[...] = _fold_heads(gk_acc[...])

    return _call(
        body,
        (dqn, dkn, dv, du, zqk, x, dh1, g_attn, gq_t, gk_t, w_in_t,
      _head_mean_matrix(ATTN_WIDTH), _head_mean_matrix(KV_WIDTH)),
        name="in_proj_bwd",
        grid=(nt,),
        in_specs=[
            _rows(ts, ATTN_WIDTH),
            _rows(ts, KV_WIDTH),
            _rows(ts, KV_WIDTH),
            _rows(ts, POOL_WIDTH),
            _rows(ts, ATTN_WIDTH + KV_WIDTH),
            _rows(ts, D_MODEL),
            _rows(ts, D_MODEL),
            _resident((1, D_MODEL)),
            _resident((1, ATTN_WIDTH)),
            _resident((1, KV_WIDTH)),
            _resident((IN_WIDTH, D_MODEL)),
            _resident((ATTN_WIDTH, ATTN_WIDTH)),
            _resident((KV_WIDTH, KV_WIDTH)),
        ],
        out_specs=[
            _rows(ts, D_MODEL),
            _acc((IN_WIDTH, D_MODEL)),
            _acc((1, D_MODEL)),
            _acc((1, SMALL_LANES)),
            _acc((1, SMALL_LANES)),
        ],
        out_shape=[
            jax.ShapeDtypeStruct((s, D_MODEL), F32),
            jax.ShapeDtypeStruct((IN_WIDTH, D_MODEL), F32),
            jax.ShapeDtypeStruct((1, D_MODEL), F32),
            jax.ShapeDtypeStruct((1, SMALL_LANES), F32),
            jax.ShapeDtypeStruct((1, SMALL_LANES), F32),
        ],
        scratch_shapes=[
            pltpu.VMEM((ts, IN_WIDTH), BF16),
            pltpu.VMEM((1, ATTN_WIDTH), F32),
            pltpu.VMEM((1, KV_WIDTH), F32),
        ],
        rider=rider,
    )


BIG_WEIGHTS = (
    ("w_in", True, IN_WIDTH // N_DEV, D_MODEL),
    ("w_out", False, D_MODEL // N_DEV, D_MODEL),
    ("w_gate", True, D_FF // N_DEV, D_MODEL),
    ("w_up", True, D_FF // N_DEV, D_MODEL),
    ("w_down", False, D_FF // N_DEV, D_MODEL),
    ("w_ple_gate", False, D_MODEL // N_DEV, D_MODEL),
    ("w_ple_proj", False, PLE_DIM, D_MODEL // N_DEV),
)
N_BIG = len(BIG_WEIGHTS)


def _place():
    x, y, c = lax.axis_index("x"), lax.axis_index("y"), lax.axis_index("c")
    chips = [(1 - x, y), (x, 1 - y), (1 - x, 1 - y)]
    return x, y, c, chips


class _Gather:
    def __init__(self, n, rows=None):
        self.n = n
        self.rows = rows or [None] * n
        self.sems = [pltpu.SemaphoreType.DMA((n, 7)), pltpu.SemaphoreType.DMA((n, 7)), pltpu.SemaphoreType.DMA((n,))]

    def _ctx(self, srcs, outs, sems):
        send_sems, recv_sems, local_sems = sems
        x, y, c, chips = _place()
        me, sibling = (x, y, c), (x, y, 1 - c)

        def part(k, ref):
            return ref if self.rows[k] is None else ref.at[pl.ds(*self.rows[k]), :]

        def block(k, owner):
            px, py, pc = owner
            return part(k, outs[k].at[4 * px + 2 * py + pc])

        def copy(k, idx, owner, to, mine=False):
            return pltpu.make_async_remote_copy(
                src_ref=part(k, srcs[k]) if mine else block(k, owner), dst_ref=block(k, owner),
                send_sem=send_sems.at[k, idx], recv_sem=recv_sems.at[k, idx], device_id=to, device_id_type=MESH)

        def local(k):
            return pltpu.make_async_copy(part(k, srcs[k]), block(k, me), local_sems.at[k])

        return c, chips, me, sibling, copy, local

    def begin(self, srcs, outs, sems):
        c, chips, me, sibling, copy, local = self._ctx(srcs, outs, sems)
        for k in range(self.n):
            local(k).start()
            copy(k, 0, me, sibling, mine=True).start()
            for j, chip in enumerate(chips):
                copy(k, 1 + j, me, (*chip, c), mine=True).start()

    def middle(self, srcs, outs, sems):
        c, chips, me, sibling, copy, local = self._ctx(srcs, outs, sems)
        for j, chip in enumerate(chips):
            for k in range(self.n):
                copy(k, 1 + j, (*chip, c), me).wait_recv()
                copy(k, 4 + j, (*chip, c), sibling).start()

    def end(self, srcs, outs, sems):
        c, chips, me, sibling, copy, local = self._ctx(srcs, outs, sems)
        for k in range(self.n):
            copy(k, 0, sibling, me).wait_recv()
            for j, chip in enumerate(chips):
                copy(k, 4 + j, (*chip, 1 - c), me).wait_recv()
        for k in range(self.n):
            copy(k, 0, me, sibling, mine=True).wait_send()
            for j, chip in enumerate(chips):
                copy(k, 1 + j, me, (*chip, c), mine=True).wait_send()
                copy(k, 4 + j, (*chip, c), sibling).wait_send()
            local(k).wait()


def _gather_rider(items):
    items = [it if isinstance(it, tuple) else (it, None, None, None) for it in items]
    n = len(items)
    g = _Gather(n, [None if r0 is None else (r0, nr) for _, r0, nr, _ in items])
    shapes = [jax.ShapeDtypeStruct((N_DEV, *sh.shape), sh.dtype) for sh, _, _, _ in items]
    stacks = [(k, st) for k, (_, _, _, st) in enumerate(items) if st is not None]
    aliases = {n + i: k for i, (k, _) in enumerate(stacks)}
    return _Rider([sh for sh, _, _, _ in items] + [st for _, st in stacks], shapes, g.sems, g.begin, g.end, g.middle,
                  aliases=aliases)


def _cast_and_gather_first(shards):
    g = _Gather(1)
    any_spec = pl.BlockSpec(memory_space=pl.ANY)
    vmem = pl.BlockSpec(memory_space=pltpu.VMEM)

    def body(*refs):
        ins, outs, gathered, sems = refs[:N_BIG], refs[N_BIG:2 * N_BIG], refs[2 * N_BIG], refs[2 * N_BIG + 1:]
        outs[0][...] = ins[0][...].astype(BF16)
        g.begin(outs[:1], [gathered], sems)
        for k in range(1, N_BIG):
            outs[k][...] = ins[k][...].astype(BF16)
        g.middle(outs[:1], [gathered], sems)
        g.end(outs[:1], [gathered], sems)

    res = pl.pallas_call(
        body,
        name="cast_and_gather_first",
        in_specs=[vmem] * N_BIG,
        out_specs=[vmem] * N_BIG + [any_spec],
        out_shape=[jax.ShapeDtypeStruct((r, c), BF16) for _, _, r, c in BIG_WEIGHTS]
        + [jax.ShapeDtypeStruct((N_DEV, *BIG_WEIGHTS[0][2:]), BF16)],
        scratch_shapes=g.sems,
    )(*shards)
    return list(res[:N_BIG]), res[N_BIG]


def _sibling_rider(grads):
    n = len(grads)

    def copies(gs, lands, sems):
        send_sems, recv_sems = sems
        x, y, c, _ = _place()
        return [
            pltpu.make_async_remote_copy(
                src_ref=gs[k].at[:, 1 - c], dst_ref=lands[k], send_sem=send_sems.at[k], recv_sem=recv_sems.at[k],
                device_id=(x, y, 1 - c), device_id_type=MESH)
            for k in range(n)
        ]

    def begin(gs, lands, sems):
        for cp in copies(gs, lands, sems):
            cp.start()

    def end(gs, lands, sems):
        for cp in copies(gs, lands, sems):
            cp.wait()

    shapes = [jax.ShapeDtypeStruct((N_CHIPS, *g.shape[2:]), F32) for g in grads]
    return _Rider(grads, shapes, [pltpu.SemaphoreType.DMA((n,)), pltpu.SemaphoreType.DMA((n,))], begin, end)


def _chip_sum(k, place, grad, from_sibling):
    _, _, r, c = BIG_WEIGHTS[k]

    args, in_specs, _ = _after_last(
        [grad, from_sibling],
        [pl.BlockSpec((1, 1, r, c), lambda q, place: (q, place[2], 0, 0)),
         pl.BlockSpec((1, r, c), lambda q, place: (q, 0, 0))])

    def body(place_ref, g_ref, l_ref, *refs):
        own_ref, send_ref = refs[-2:]
        q = pl.program_id(0)
        tot = g_ref[0, 0] + l_ref[0]
        mine = q == 2 * place_ref[0] + place_ref[1]

        @pl.when(mine)
        def _():
            own_ref[...] = tot

        send_ref[0] = jnp.where(mine, 0.0, tot).astype(BF16)

    outs = pl.pallas_call(
        body,
        name=f"chip_sum_{BIG_WEIGHTS[k][0]}",
        grid_spec=pltpu.PrefetchScalarGridSpec(
            num_scalar_prefetch=1,
            grid=(N_CHIPS,),
            in_specs=in_specs,
            out_specs=[
                pl.BlockSpec((r, c), lambda q, place: (0, 0)),
                pl.BlockSpec((1, r, c), lambda q, place: (q, 0, 0)),
            ],
        ),
        out_shape=[jax.ShapeDtypeStruct((r, c), F32), jax.ShapeDtypeStruct((N_CHIPS, r, c), BF16)],
    )(place, *args)
    _mark_issued(outs[0])
    return outs


def _chips_rider(to_send, small=None):
    n = len(to_send)
    inputs = list(to_send) + ([] if small is None else [small])
    shapes = [jax.ShapeDtypeStruct((3, *t.shape[1:]), BF16) for t in to_send]
    sems = [pltpu.SemaphoreType.DMA((max(n, 1), 3)), pltpu.SemaphoreType.DMA((max(n, 1), 3))]
    if small is not None:
        shapes.append(jax.ShapeDtypeStruct((N_DEV, *small.shape), F32))
        sems += [pltpu.SemaphoreType.DMA((7,)), pltpu.SemaphoreType.DMA((7,)), pltpu.SemaphoreType.DMA]

    def copies(ins, outs, sem_refs):
        x, y, c, chips = _place()
        out = []
        for k in range(n):
            for j, (px, py) in enumerate(chips):
                out.append(pltpu.make_async_remote_copy(
                    src_ref=ins[k].at[2 * px + py], dst_ref=outs[k].at[j],
                    send_sem=sem_refs[0].at[k, j], recv_sem=sem_refs[1].at[k, j],
                    device_id=(px, py, c), device_id_type=MESH))
        local = None
        if small is not None:
            me = 4 * x + 2 * y + c
            local = pltpu.make_async_copy(ins[n], outs[n].at[me], sem_refs[4])
            rel = 0
            for fx in (0, 1):
                for fy in (0, 1):
                    for fc in (0, 1):
                        if (fx, fy, fc) != (0, 0, 0):
                            out.append(pltpu.make_async_remote_copy(
                                src_ref=ins[n], dst_ref=outs[n].at[me],
                                send_sem=sem_refs[2].at[rel], recv_sem=sem_refs[3].at[rel],
                                device_id=(x ^ fx, y ^ fy, c ^ fc), device_id_type=MESH))
                            rel += 1
        return out, local

    def begin(ins, outs, sem_refs):
        remote, local = copies(ins, outs, sem_refs)
        if local is not None:
            local.start()
        for cp in remote:
            cp.start()

    def end(ins, outs, sem_refs):
        remote, local = copies(ins, outs, sem_refs)
        for cp in remote:
            cp.wait()
        if local is not None:
            local.wait()

    return _Rider(inputs, shapes, sems, begin, end)


def _exchange(name, rider):
    return _call(lambda: None, (), name=name, grid=(1,), in_specs=[], out_specs=[], out_shape=[], rider=rider)[1]


PEER_SETS = {"sibling": 1, "chips": 2, "sibling+chips": 3, "all": 4}


def _peers(pattern):
    x, y, c, chips = _place()
    sibling, others = [(x, y, 1 - c)], [(*chip, c) for chip in chips]
    if pattern == "all":
        return sibling + others + [(*chip, 1 - c) for chip in chips]
    return {"sibling": sibling, "chips": others, "sibling+chips": sibling + others}[pattern]


def _on_sequencer(name, pattern, rider):
    assert not rider.aliases
    n_in, n_out = len(rider.inputs), len(rider.out_shapes)

    def body(*refs):
        ins, outs, sems = refs[:n_in], refs[n_in:n_in + n_out], refs[n_in + n_out:]
        peers = _peers(pattern)
        barrier = pltpu.get_barrier_semaphore()
        for peer in peers:
            pl.semaphore_signal(barrier, inc=1, device_id=peer, device_id_type=MESH)
        pl.semaphore_wait(barrier, len(peers))
        rider.begin(ins, outs, sems)
        if rider.middle is not None:
            rider.middle(ins, outs, sems)
        rider.end(ins, outs, sems)

    outs = pl.kernel(
        body,
        name=name,
        out_type=tuple(rider.out_shapes),
        mesh=plsc.ScalarSubcoreMesh(axis_name="sequencer", num_cores=1),
        scratch_types=tuple(rider.sems),
        compiler_params=pltpu.CompilerParams(collective_id=PEER_SETS[pattern]),
    )(*rider.inputs)
    return list(outs)


def _merge_riders(*riders):
    riders = [r for r in riders if r is not None]
    if len(riders) == 1:
        return riders[0]
    assert not any(r.aliases for r in riders)

    def split(refs, counts):
        out, at = [], 0
        for n in counts:
            out.append(refs[at:at + n])
            at += n
        return out

    def run(which):
        def fn(ins, outs, sems):
            parts = zip(riders, split(ins, [len(r.inputs) for r in riders]),
                        split(outs, [len(r.out_shapes) for r in riders]), split(sems, [len(r.sems) for r in riders]))
            for r, i, o, s in parts:
                hook = getattr(r, which)
                if hook is not None:
                    hook(i, o, s)
        return fn

    middle = run("middle") if any(r.middle is not None for r in riders) else None
    return _Rider(sum((r.inputs for r in riders), []), sum((r.out_shapes for r in riders), []),
                  sum((r.sems for r in riders), []), run("begin"), run("end"), middle)


def _split_outputs(outs, *riders):
    res, at = [], 0
    for r in riders:
        res.append(outs[at:at + len(r.out_shapes)])
        at += len(r.out_shapes)
    return res


def _adamw(w, g, m, v):
    m = ADAM_B1 * m + (1.0 - ADAM_B1) * g
    v = ADAM_B2 * v + (1.0 - ADAM_B2) * jnp.square(g)
    m_hat = m / (1.0 - ADAM_B1 ** ADAM_STEP)
    v_hat = v / (1.0 - ADAM_B2 ** ADAM_STEP)
    delta = -ADAM_LR * (m_hat / (jnp.sqrt(v_hat) + ADAM_EPS) + ADAM_WD * w)
    return delta, m, v


def _adamw_big(k, own, landed, w, m, v, rider=None):
    name, _, r, c = BIG_WEIGHTS[k]
    tile = r // 2
    tiles = lambda i: (i, 0)

    def body(own_ref, land_ref, w_ref, m_ref, v_ref, g_ref, d_ref, nm_ref, nv_ref):
        g = ((own_ref[...] + land_ref[0].astype(F32)) + land_ref[1].astype(F32)) + land_ref[2].astype(F32)
        g_ref[...] = g
        d_ref[...], nm_ref[...], nv_ref[...] = _adamw(w_ref[...], g, m_ref[...], v_ref[...])

    return _call(
        body,
        (own, landed, w, m, v),
        name=f"adamw_{name}",
        grid=(r // tile,),
        in_specs=[pl.BlockSpec((tile, c), tiles), pl.BlockSpec((3, tile, c), lambda i: (0, i, 0))]
        + [pl.BlockSpec((tile, c), tiles)] * 3,
        out_specs=[pl.BlockSpec((tile, c), tiles)] * 4,
        out_shape=[jax.ShapeDtypeStruct((r, c), F32)] * 4,
        rider=rider,
    )


def _sum_small(parts_list):
    n = len(parts_list)

    def body(*refs):
        for p_ref, out_ref in zip(refs[:n], refs[n:]):
            tot = p_ref[0]
            for j in range(1, N_DEV):
                tot = tot + p_ref[j]
            out_ref[...] = tot

    return pl.pallas_call(body, name="sum_small",
                          out_shape=[jax.ShapeDtypeStruct(p.shape[1:], F32) for p in parts_list])(*parts_list)


def _adamw_small(grads, ws, ms, vs):
    n = len(grads)

    def body(*refs):
        g_refs, w_refs, m_refs, v_refs = refs[:n], refs[n:2 * n], refs[2 * n:3 * n], refs[3 * n:4 * n]
        outs = refs[4 * n:]
        for i in range(n):
            d, nm, nv = _adamw(w_refs[i][...], g_refs[i][...], m_refs[i][...], v_refs[i][...])
            outs[i][...] = d
            outs[n + i][...] = nm
            outs[2 * n + i][...] = nv

    shapes = [jax.ShapeDtypeStruct(w.shape, F32) for w in ws]
    return pl.pallas_call(body, name="adamw_small", out_shape=shapes * 3)(*grads, *ws, *ms, *vs)


SMALL_NAMES = ("g_attn_norm", "g_q", "g_k", "attn_sinks", "rel_bias", "w_pool", "pool_scale", "g_ffn_norm", "g_ple_norm")


def _pack_small(arrays):
    rows, offsets = [], []
    at = 0
    for a in arrays:
        flat = a.reshape(-1)
        n_rows = -(-flat.shape[0] // (8 * SMALL_LANES)) * 8
        flat = jnp.pad(flat, (0, n_rows * SMALL_LANES - flat.shape[0]))
        rows.append(flat.reshape(n_rows, SMALL_LANES))
        offsets.append(at)
        at += n_rows
    return jnp.concatenate(rows, axis=0), offsets


def kernel(x, p, w_in, w_out, g_attn_norm, g_q, g_k, attn_sinks, rel_bias, w_pool, pool_scale, g_ffn_norm, w_gate, w_up, w_down, g_ple_norm, w_ple_gate, w_ple_proj, loss_target, m_w_in, m_w_out, m_g_attn_norm, m_g_q, m_g_k, m_attn_sinks, m_rel_bias, m_w_pool, m_pool_scale, m_g_ffn_norm, m_w_gate, m_w_up, m_w_down, m_g_ple_norm, m_w_ple_gate, m_w_ple_proj, v_w_in, v_w_out, v_g_attn_norm, v_g_q, v_g_k, v_attn_sinks, v_rel_bias, v_w_pool, v_pool_scale, v_g_ffn_norm, v_w_gate, v_w_up, v_w_down, v_g_ple_norm, v_w_ple_gate, v_w_ple_proj):
    weights = dict(w_in=w_in, w_out=w_out, g_attn_norm=g_attn_norm, g_q=g_q, g_k=g_k, attn_sinks=attn_sinks,
                   rel_bias=rel_bias, w_pool=w_pool, pool_scale=pool_scale, g_ffn_norm=g_ffn_norm, w_gate=w_gate,
                   w_up=w_up, w_down=w_down, g_ple_norm=g_ple_norm, w_ple_gate=w_ple_gate, w_ple_proj=w_ple_proj)
    m_in = dict(w_in=m_w_in, w_out=m_w_out, g_attn_norm=m_g_attn_norm, g_q=m_g_q, g_k=m_g_k, attn_sinks=m_attn_sinks,
                rel_bias=m_rel_bias, w_pool=m_w_pool, pool_scale=m_pool_scale, g_ffn_norm=m_g_ffn_norm, w_gate=m_w_gate,
                w_up=m_w_up, w_down=m_w_down, g_ple_norm=m_g_ple_norm, w_ple_gate=m_w_ple_gate, w_ple_proj=m_w_ple_proj)
    v_in = dict(w_in=v_w_in, w_out=v_w_out, g_attn_norm=v_g_attn_norm, g_q=v_g_q, g_k=v_g_k, attn_sinks=v_attn_sinks,
                rel_bias=v_rel_bias, w_pool=v_w_pool, pool_scale=v_pool_scale, g_ffn_norm=v_g_ffn_norm, w_gate=v_w_gate,
                w_up=v_w_up, w_down=v_w_down, g_ple_norm=v_g_ple_norm, w_ple_gate=v_w_ple_gate, w_ple_proj=v_w_ple_proj)

    _issued.clear()
    xs = x[0]
    ps = p[0, 0]
    target = loss_target[0]
    wp = w_pool[0]
    gq_t = jnp.tile(g_q, (1, ATTN_WIDTH // HEAD_DIM))
    gk_t = jnp.tile(g_k, (1, KV_WIDTH // HEAD_DIM))

    def to_blocks(k, arr):
        return jnp.swapaxes(arr[0], 0, 1) if BIG_WEIGHTS[k][1] else arr[0]

    def from_blocks(k, arr):
        return (jnp.swapaxes(arr, 0, 1) if BIG_WEIGHTS[k][1] else arr)[None]

    IN, OUT, GATE, UP, DOWN, PG, PP = range(N_BIG)
    full = lambda g: g.reshape(N_DEV * g.shape[1], g.shape[2])
    halves = lambda k, g: g.reshape(N_CHIPS, 2, *BIG_WEIGHTS[k][2:])
    place = jnp.stack([lax.axis_index("x"), lax.axis_index("y"), lax.axis_index("c")]).astype(jnp.int32)

    sh, w_in_g = _cast_and_gather_first([to_blocks(k, weights[name]) for k, (name, _, _, _) in enumerate(BIG_WEIGHTS)])
    w_in_t = full(w_in_g)

    (w_out_g,) = _on_sequencer("gather_out", "sibling+chips", _gather_rider([sh[OUT]]))
    wg_g, wu_g = _on_sequencer("gather_gate_up", "sibling+chips", _gather_rider([sh[GATE], sh[UP]]))
    wd_g, w_pg_g, w_pp_g = _on_sequencer("gather_down_ple", "sibling+chips", _gather_rider([sh[DOWN], sh[PG], sh[PP]]))
    tab = _bias_table(rel_bias.T)
    (zqk, qn, kn, v, u), _ = _in_proj(xs, g_attn_norm, w_in_t, gq_t, gk_t)
    (a,), _ = _attn_fwd(qn, kn, v, tab, attn_sinks)
    w_out_f = full(w_out_g)
    (h1, hn2, m_out), _ = _mix_out(u, a, xs, w_out_f, wp, pool_scale, g_ffn_norm)
    wg_t, wu_t = full(wg_g), full(wu_g)
    (gt, up), _ = _ffn_up(hn2, wg_t, wu_t)
    w_down_f = full(wd_g)

    sums, landed = [None] * N_BIG, [None] * N_BIG

    def to_sibling(name, ks, grads):
        return _on_sequencer(name, "sibling", _sibling_rider([halves(k, g) for k, g in zip(ks, grads)]))

    def chip_sum(k, grad, from_sibling):
        sums[k] = _chip_sum(k, place, halves(k, grad), from_sibling)

    def to_chips(name, ks, small=None):
        got = _on_sequencer(name, "chips" if small is None else "all", _chips_rider([sums[k][1] for k in ks], small))
        for k, g in zip(ks, got):
            landed[k] = g
        return got[len(ks):]

    (loss_part, dh2, d_wpg, d_wpp, d_g_ple), _ = _ffn_down_ple(
        gt, up, h1, w_down_f, ps, target, g_ple_norm, full(w_pg_g), w_pp_g)
    sib_ple = to_sibling("sibling_ple", (PG, PP), (d_wpg, d_wpp))
    (dgt, dup, dh1, dh1b, d_g_ffn, d_wd), _ = _ffn_bwd_act(dh2, h1, gt, up, g_ffn_norm, wg_t, wu_t, w_down_f)
    sib_down = to_sibling("sibling_down", (DOWN,), (d_wd,))
    chip_sum(PG, d_wpg, sib_ple[0])
    chip_sum(PP, d_wpp, sib_ple[1])
    to_chips("chips_ple", (PG, PP))
    (d_wo,), _ = _out_w_bwd(a, m_out, dh1b)
    sib_out = to_sibling("sibling_out", (OUT,), (d_wo,))
    (d_wg_t, d_wu_t), _ = _ffn_bwd_w(dgt, dup, hn2)
    sib_gate_up = to_sibling("sibling_gate_up", (GATE, UP), (d_wg_t, d_wu_t))
    _complete_before_next([landed[PG], landed[PP]])
    chip_sum(DOWN, d_wd, sib_down[0])
    to_chips("chips_down", (DOWN,))
    chip_sum(OUT, d_wo, sib_out[0])
    to_chips("chips_out", (OUT,))
    (da, du, d_wpool, d_scale), _ = _mix_bwd(dh1b, u, w_out_f, wp, pool_scale)
    chip_sum(GATE, d_wg_t, sib_gate_up[0])
    chip_sum(UP, d_wu_t, sib_gate_up[1])
    early, early_at = _pack_small([d_wpool, d_scale, d_g_ffn, d_g_ple, loss_part[:, :1]])
    (early_all,) = to_chips("chips_gate_up", (GATE, UP), early)
    (dqn, dkn, dv, dl_acc, d_sinks), _ = _attn_bwd(qn, kn, v, a, da, tab, attn_sinks)
    _complete_before_next([landed[DOWN], landed[OUT]])
    (grad_x, d_win_t, d_g_attn, d_gq, d_gk), _ = _in_proj_bwd(dqn, dkn, dv, du, zqk, xs, dh1, g_attn_norm, gq_t, gk_t, w_in_t)
    sib_in = to_sibling("sibling_in", (IN,), (d_win_t,))
    _complete_before_next([landed[GATE], landed[UP], early_all])
    (d_rel_t,), _ = _bias_table_bwd(dl_acc)
    chip_sum(IN, d_win_t, sib_in[0])
    late, late_at = _pack_small([d_g_attn, d_gq[:, :HEAD_DIM], d_gk[:, :HEAD_DIM], d_sinks[:, 0], d_rel_t[:, :N_BUCKETS]])
    (late_all,) = to_chips("chips_in", (IN,), late)

    out = {"grad": {}, "delta": {}, "new_m": {}, "new_v": {}}
    for k in (PG, PP, DOWN, OUT, GATE, UP, IN):
        name = BIG_WEIGHTS[k][0]
        res, _ = _adamw_big(k, sums[k][0], landed[k], to_blocks(k, weights[name]), to_blocks(k, m_in[name]),
                            to_blocks(k, v_in[name]))
        for kind, r in zip(("grad", "delta", "new_m", "new_v"), res):
            out[kind][name] = from_blocks(k, r)
    early_sum, late_sum = _sum_small([early_all, late_all])

    def unpack(packed, at, shape):
        n = math.prod(shape)
        return packed[at:at + -(-n // SMALL_LANES)].reshape(-1)[:n].reshape(shape)

    small_grads = dict(
        w_pool=unpack(early_sum, early_at[0], w_pool.shape), pool_scale=unpack(early_sum, early_at[1], pool_scale.shape),
        g_ffn_norm=unpack(early_sum, early_at[2], g_ffn_norm.shape), g_ple_norm=unpack(early_sum, early_at[3], g_ple_norm.shape),
        g_attn_norm=unpack(late_sum, late_at[0], g_attn_norm.shape), g_q=unpack(late_sum, late_at[1], g_q.shape),
        g_k=unpack(late_sum, late_at[2], g_k.shape), attn_sinks=unpack(late_sum, late_at[3], attn_sinks.shape),
        rel_bias=unpack(late_sum, late_at[4], rel_bias.T.shape))
    loss = early_sum[early_at[4], 0]
    flip = lambda name, arr: arr.T if name == "rel_bias" else arr
    updates = _adamw_small([small_grads[n] for n in SMALL_NAMES], [flip(n, weights[n]) for n in SMALL_NAMES],
                           [flip(n, m_in[n]) for n in SMALL_NAMES], [flip(n, v_in[n]) for n in SMALL_NAMES])
    n_small = len(SMALL_NAMES)
    for i, name in enumerate(SMALL_NAMES):
        out["grad"][name] = flip(name, small_grads[name])
        out["delta"][name] = flip(name, updates[i])
        out["new_m"][name] = flip(name, updates[n_small + i])
        out["new_v"][name] = flip(name, updates[2 * n_small + i])

    _issued.clear()
    order = ("w_in", "w_out", "g_attn_norm", "g_q", "g_k", "attn_sinks", "rel_bias", "w_pool", "pool_scale",
             "g_ffn_norm", "w_gate", "w_up", "w_down", "g_ple_norm", "w_ple_gate", "w_ple_proj")
    return (loss, grad_x[None], *[out["grad"][n] for n in order], *[out["delta"][n] for n in order],
            *[out["new_m"][n] for n in order], *[out["new_v"][n] for n in order])
```

```python
import functools
import math

import jax
import jax.numpy as jnp
import numpy as np
from jax import lax
from jax.experimental import pallas as pl
from jax.experimental.pallas import tpu as pltpu
from jax.experimental.pallas import tpu_sc as plsc

F32 = jnp.float32
BF16 = jnp.bfloat16
MESH = pl.DeviceIdType.MESH

D_MODEL = 1024
HEAD_DIM = 64
ATTN_WIDTH = 512
KV_WIDTH = 128
POOL_WIDTH = 512
POOL_SIZES = (2, 4, 8, 16)
POOL_GROUP = 128
POOL_HALO = 16
IN_WIDTH = 1280
D_FF = 2816
PLE_DIM = 256
BLOCK = 128
N_BUCKETS = 32
MAX_DISTANCE = 128
EPS = 1e-6
N_DEV = 8
N_CHIPS = 4

ADAM_LR = 0.001
ADAM_B1 = 0.9
ADAM_B2 = 0.999
ADAM_EPS = 1e-08
ADAM_WD = 0.01
ADAM_STEP = 10

TOKEN_TILE = 512
FFN_BWD_TILE = 256
FF_CHUNK = 256
ATTN_STEP_BLOCKS = 4
GATE_ROWS_EARLY = 96
UP_ROWS_EARLY = 64
HEADS_A = (0, 2, 5, 7)
HEADS_B = (1, 3, 4, 6)
SMALL_LANES = 128


def _nn(a, b):
    return jnp.dot(a, b, preferred_element_type=F32)


def _nt(a, b):
    return lax.dot_general(a, b, (((1,), (1,)), ((), ())), preferred_element_type=F32)


def _tn(a, b):
    return lax.dot_general(a, b, (((0,), (0,)), ((), ())), preferred_element_type=F32)


def _resident(shape):
    nd = len(shape)
    return pl.BlockSpec(shape, lambda i, _nd=nd: (0,) * _nd, pipeline_mode=pl.Buffered(1))


def _rows(tile, width):
    return pl.BlockSpec((tile, width), lambda i: (i, 0))


def _acc(shape):
    nd = len(shape)
    return pl.BlockSpec(shape, lambda i, _nd=nd: (0,) * _nd)


def _head_mean_matrix(width):
    idx = np.arange(width) // HEAD_DIM
    return jnp.asarray((idx[:, None] == idx[None, :]).astype(np.float32) / HEAD_DIM, dtype=BF16)


def _seg_mean(v, bmat):
    hi = v.astype(BF16)
    lo = (v - hi.astype(F32)).astype(BF16)
    return _nn(hi, bmat) + _nn(lo, bmat)


def _rms(x):
    return lax.rsqrt(jnp.mean(x * x, axis=-1, keepdims=True) + EPS)


def _rms_bwd(d_y, x, r, g):
    gy = d_y * g
    d_x = r * gy - x * (r * r * r) * jnp.mean(gy * x, axis=-1, keepdims=True)
    d_g = jnp.sum(d_y * (x * r), axis=0, keepdims=True)
    return d_x, d_g


def _lane_lo(shape):
    return lax.broadcasted_iota(jnp.int32, shape, 1) < HEAD_DIM


class _Rider:
    def __init__(self, inputs, out_shapes, sems, begin, end, middle=None, aliases=None):
        self.inputs, self.out_shapes, self.sems = list(inputs), list(out_shapes), list(sems)
        self.begin, self.middle, self.end = begin, middle, end
        self.aliases = dict(aliases or {})


_issued = []


def _after_last(args, in_specs):
    extra = list(_issued)
    return list(args) + extra, list(in_specs) + [pl.BlockSpec(memory_space=pl.ANY)] * len(extra), len(extra)


def _mark_issued(out):
    _issued[:] = [out]


def _complete_before_next(arrays):
    _issued.extend(arrays)


def _call(body, args, *, name, grid, in_specs, out_specs, out_shape, scratch_shapes=(), rider=None):
    in_specs, out_specs, out_shape, scratch_shapes = list(in_specs), list(out_specs), list(out_shape), list(scratch_shapes)
    if rider is None:
        n_args = len(args)
        args, in_specs, _ = _after_last(args, in_specs)

        def ordered(*refs):
            body(*refs[:n_args], *refs[len(args):])

        outs = pl.pallas_call(ordered, name=name, grid=grid, in_specs=in_specs, out_specs=out_specs, out_shape=out_shape,
                              scratch_shapes=scratch_shapes)(*args)
        _mark_issued(outs[0])
        return list(outs), []
    n_in, n_out, n_scr = len(in_specs), len(out_shape), len(scratch_shapes)
    r_in, r_out = len(rider.inputs), len(rider.out_shapes)
    n_steps = grid[0]

    def hosted(*refs):
        ins, refs = refs[:n_in], refs[n_in:]
        r_ins, refs = refs[:r_in], refs[r_in:]
        outs, refs = refs[:n_out], refs[n_out:]
        r_outs, refs = refs[:r_out], refs[r_out:]
        scratch, r_sems = refs[:n_scr], refs[n_scr:]
        step = pl.program_id(0)

        @pl.when(step == 0)
        def _():
            rider.begin(r_ins, r_outs, r_sems)

        if rider.middle is not None:
            @pl.when(step == n_steps - 1)
            def _():
                rider.middle(r_ins, r_outs, r_sems)

        body(*ins, *outs, *scratch)

        @pl.when(step == n_steps - 1)
        def _():
            rider.end(r_ins, r_outs, r_sems)

    any_spec = pl.BlockSpec(memory_space=pl.ANY)
    outs = pl.pallas_call(
        hosted, name=name, grid=grid,
        in_specs=in_specs + [any_spec] * r_in,
        out_specs=out_specs + [any_spec] * r_out,
        out_shape=out_shape + rider.out_shapes,
        scratch_shapes=scratch_shapes + rider.sems,
        input_output_aliases={n_in + i: n_out + o for i, o in rider.aliases.items()},
    )(*args, *rider.inputs)
    return list(outs[:n_out]), list(outs[n_out:])


def _in_proj(x, g_attn, w_in_t, gq_t, gk_t, rider=None):
    s = x.shape[0]
    ts = min(TOKEN_TILE, s)

    def body(x_ref, g_ref, w_ref, gq_ref, gk_ref, bq_ref, bk_ref, zqk_ref, qn_ref, kn_ref, v_ref, u_ref):
        xf = x_ref[...]
        hn = ((xf * _rms(xf)) * g_ref[...]).astype(BF16)
        z = _nt(hn, w_ref[...])
        q = z[:, :ATTN_WIDTH]
        k = z[:, ATTN_WIDTH:ATTN_WIDTH + KV_WIDTH]
        zqk_ref[...] = z[:, :ATTN_WIDTH + KV_WIDTH]
        rq = lax.rsqrt(_seg_mean(q * q, bq_ref[...]) + EPS)
        qn_ref[...] = ((q * rq) * gq_ref[...]).astype(BF16)
        rk = lax.rsqrt(_seg_mean(k * k, bk_ref[...]) + EPS)
        kn_ref[...] = ((k * rk) * gk_ref[...]).astype(BF16)
        v_ref[...] = z[:, ATTN_WIDTH + KV_WIDTH:ATTN_WIDTH + 2 * KV_WIDTH].astype(BF16)
        u_ref[...] = z[:, ATTN_WIDTH + 2 * KV_WIDTH:]

    return _call(
        body,
        (x, g_attn, w_in_t, gq_t, gk_t, _head_mean_matrix(ATTN_WIDTH), _head_mean_matrix(KV_WIDTH)),
        name="in_proj",
        grid=(s // ts,),
        in_specs=[
            _rows(ts, D_MODEL),
            _resident((1, D_MODEL)),
            _resident((IN_WIDTH, D_MODEL)),
            _resident((1, ATTN_WIDTH)),
            _resident((1, KV_WIDTH)),
            _resident((ATTN_WIDTH, ATTN_WIDTH)),
            _resident((KV_WIDTH, KV_WIDTH)),
        ],
        out_specs=[
            _rows(ts, ATTN_WIDTH + KV_WIDTH),
            _rows(ts, ATTN_WIDTH),
            _rows(ts, KV_WIDTH),
            _rows(ts, KV_WIDTH),
            _rows(ts, POOL_WIDTH),
        ],
        out_shape=[
            jax.ShapeDtypeStruct((s, ATTN_WIDTH + KV_WIDTH), F32),
            jax.ShapeDtypeStruct((s, ATTN_WIDTH), BF16),
            jax.ShapeDtypeStruct((s, KV_WIDTH), BF16),
            jax.ShapeDtypeStruct((s, KV_WIDTH), BF16),
            jax.ShapeDtypeStruct((s, POOL_WIDTH), F32),
        ],
        rider=rider,
    )


def _bucket_ranges():
    n = np.arange(MAX_DISTANCE)
    max_exact = N_BUCKETS // 2
    nf = np.maximum(n, 1).astype(np.float64)
    large = max_exact + (np.log(nf / max_exact) / math.log(MAX_DISTANCE / max_exact) * (N_BUCKETS - max_exact)).astype(np.int64)
    bucket = np.where(n < max_exact, n, np.minimum(large, N_BUCKETS - 1))
    out = []
    for b in range(N_BUCKETS):
        idx = np.nonzero(bucket == b)[0]
        out.append((int(idx.min()), int(idx.max()) + 1))
    return out


def _band_distance():
    i = lax.broadcasted_iota(jnp.int32, (BLOCK, 2 * BLOCK), 0)
    j = lax.broadcasted_iota(jnp.int32, (BLOCK, 2 * BLOCK), 1)
    return BLOCK + i - j


def _bias_table(rel_bias_t):
    ranges = _bucket_ranges()

    def body(rb_ref, tab_ref):
        d = _band_distance()
        for half, heads in enumerate((HEADS_A, HEADS_B)):
            for slot, h in enumerate(heads):
                t = jnp.full((BLOCK, 2 * BLOCK), -jnp.inf, F32)
                for b, (lo, hi) in enumerate(ranges):
                    t = jnp.where((d >= lo) & (d < hi), rb_ref[h, b], t)
                tab_ref[half, slot * BLOCK:(slot + 1) * BLOCK, :] = t

    return pl.pallas_call(
        body,
        name="bias_table",
        in_specs=[pl.BlockSpec(memory_space=pltpu.SMEM)],
        out_shape=jax.ShapeDtypeStruct((2, 4 * BLOCK, 2 * BLOCK), F32),
    )(rel_bias_t)


def _bias_table_bwd(dl_acc, rider=None):
    ranges = _bucket_ranges()
    n_heads = len(HEADS_A) + len(HEADS_B)

    def body(dl_ref, out_ref):
        d = _band_distance()
        row = lax.broadcasted_iota(jnp.int32, (n_heads, SMALL_LANES), 0)
        lane = lax.broadcasted_iota(jnp.int32, (n_heads, SMALL_LANES), 1)
        out = jnp.zeros((n_heads, SMALL_LANES), F32)
        for b, (lo, hi) in enumerate(ranges):
            in_bucket = (d >= lo) & (d < hi)
            for half, heads in enumerate((HEADS_A, HEADS_B)):
                for slot, h in enumerate(heads):
                    g = dl_ref[half, slot * BLOCK:(slot + 1) * BLOCK, :]
                    part = jnp.sum(jnp.where(in_bucket, g, 0.0), axis=0, keepdims=True)
                    tot = jnp.sum(part, axis=1, keepdims=True)
                    out = jnp.where((row == h) & (lane == b), tot, out)
        out_ref[...] = out

    return _call(
        body,
        (dl_acc,),
        name="bias_table_bwd",
        grid=(1,),
        in_specs=[_acc((2, 4 * BLOCK, 2 * BLOCK))],
        out_specs=[_acc((n_heads, SMALL_LANES))],
        out_shape=[jax.ShapeDtypeStruct((n_heads, SMALL_LANES), F32)],
        rider=rider,
    )


def _stack_heads(pairs, lo_mask):
    zero = jnp.zeros_like(pairs[0])
    lo = [jnp.where(lo_mask, t, zero) for t in pairs]
    hi = [jnp.where(lo_mask, zero, t) for t in pairs]
    return (jnp.concatenate([lo[0], lo[1], hi[2], hi[3]], axis=0),
            jnp.concatenate([hi[0], hi[1], lo[2], lo[3]], axis=0))


def _unstack_heads(out_a, out_b, lo_mask):
    t = lambda x, r: x[r * BLOCK:(r + 1) * BLOCK, :]
    return [
        jnp.where(lo_mask, t(out_a, 0), t(out_b, 0)),
        jnp.where(lo_mask, t(out_a, 1), t(out_b, 1)),
        jnp.where(lo_mask, t(out_b, 2), t(out_a, 2)),
        jnp.where(lo_mask, t(out_b, 3), t(out_a, 3)),
    ]


def _sink_column(sink_ref, heads):
    row = lax.broadcasted_iota(jnp.int32, (4 * BLOCK, 1), 0)
    col = jnp.full((4 * BLOCK, 1), sink_ref[0, heads[3]], F32)
    for slot in (2, 1, 0):
        col = jnp.where(row < (slot + 1) * BLOCK, sink_ref[0, heads[slot]], col)
    return col


def _band_probs(q_stack, keys, tab, sink, first_block):
    s = _nt(q_stack, keys) * (HEAD_DIM ** -0.5) + tab
    if first_block is not None:
        col = lax.broadcasted_iota(jnp.int32, s.shape, 1)
        s = jnp.where(jnp.logical_and(first_block, col < BLOCK), -jnp.inf, s)
    m = jnp.maximum(jnp.max(s, axis=-1, keepdims=True), sink)
    e = jnp.exp(s - m)
    e_sink = jnp.exp(sink - m)
    den = jnp.sum(e, axis=-1, keepdims=True) + e_sink
    return e / den, e_sink / den


def _attn_specs(n_groups):
    group = lambda n: (jnp.minimum(n, n_groups - 1), 0)
    prev = lambda n: (jnp.maximum(jnp.minimum(n, n_groups - 1) * ATTN_STEP_BLOCKS - 1, 0), 0)
    return group, prev


def _band(prev_ref, group_ref, b):
    rows = lambda i: group_ref[i * BLOCK:(i + 1) * BLOCK, :]
    band = jnp.concatenate([prev_ref[...] if b == 0 else rows(b - 1), rows(b)], axis=0)
    return band, pltpu.roll(band, HEAD_DIM, 1)


def _attn_fwd(qn, kn, v, tab, sinks, rider=None):
    s = qn.shape[0]
    n_groups = s // (ATTN_STEP_BLOCKS * BLOCK)
    group, prev = _attn_specs(n_groups)
    rows = ATTN_STEP_BLOCKS * BLOCK

    def body(sink_ref, q_ref, kc_ref, kp_ref, vc_ref, vp_ref, tab_ref, o_ref):
        first = pl.program_id(0) == 0
        lo_mask = _lane_lo((BLOCK, BLOCK))
        for b in range(ATTN_STEP_BLOCKS):
            at = slice(b * BLOCK, (b + 1) * BLOCK)
            kk, kk_sw = _band(kp_ref, kc_ref, b)
            vv, vv_sw = _band(vp_ref, vc_ref, b)
            q_a, q_b = _stack_heads([q_ref[at, p * BLOCK:(p + 1) * BLOCK] for p in range(4)], lo_mask)
            no_prev = first if b == 0 else None
            p_a, _ = _band_probs(q_a, kk, tab_ref[0], _sink_column(sink_ref, HEADS_A), no_prev)
            p_b, _ = _band_probs(q_b, kk_sw, tab_ref[1], _sink_column(sink_ref, HEADS_B), no_prev)
            out = _unstack_heads(_nn(p_a.astype(BF16), vv), _nn(p_b.astype(BF16), vv_sw), lo_mask)
            for p in range(4):
                o_ref[at, p * BLOCK:(p + 1) * BLOCK] = out[p].astype(BF16)

    return _call(
        body,
        (sinks, qn, kn, kn, v, v, tab),
        name="attn_fwd",
        grid=(n_groups,),
        in_specs=[
            pl.BlockSpec(memory_space=pltpu.SMEM),
            pl.BlockSpec((rows, ATTN_WIDTH), group),
            pl.BlockSpec((rows, KV_WIDTH), group),
            pl.BlockSpec((BLOCK, KV_WIDTH), prev),
            pl.BlockSpec((rows, KV_WIDTH), group),
            pl.BlockSpec((BLOCK, KV_WIDTH), prev),
            _resident((2, 4 * BLOCK, 2 * BLOCK)),
        ],
        out_specs=[pl.BlockSpec((rows, ATTN_WIDTH), group)],
        out_shape=[jax.ShapeDtypeStruct((s, ATTN_WIDTH), BF16)],
        rider=rider,
    )


def _pooled(u_tile, u_halo, tile_index, tile_rows):
    halo = jnp.where(tile_index > 0, u_halo, 0.0)
    ext = jnp.concatenate([halo, u_tile], axis=0)
    sums = []
    acc = ext
    for shift in (1, 2, 4, 8):
        acc = acc + pltpu.roll(acc, shift, 0)
        sums.append(acc)
    t = tile_index * tile_rows + lax.broadcasted_iota(jnp.int32, (tile_rows, 1), 0)
    out = []
    for g, w in enumerate(POOL_SIZES):
        lanes = slice(g * POOL_GROUP, (g + 1) * POOL_GROUP)
        cnt = jnp.minimum(t + 1, w).astype(F32)
        out.append(sums[g][POOL_HALO:, lanes] / cnt - u_tile[:, lanes])
    return out


def _halo_before(tile):
    return lambda i: (jnp.maximum(i * (tile // POOL_HALO) - 1, 0), 0)


def _mix_out(u, a, x, w_out, w_pool, pool_scale, g_ffn, rider=None):
    s = x.shape[0]
    ts = min(TOKEN_TILE, s)

    def body(u_ref, uh_ref, a_ref, x_ref, wo_ref, wp_ref, sc_ref, g_ref, h1_ref, hn_ref, m_ref):
        i = pl.program_id(0)
        pooled = _pooled(u_ref[...], uh_ref[...], i, ts)
        for g in range(len(POOL_SIZES)):
            lanes = slice(g * POOL_GROUP, (g + 1) * POOL_GROUP)
            y = _nn(pooled[g].astype(BF16), wp_ref[g].astype(BF16))
            m_ref[:, lanes] = (y * sc_ref[:, lanes]).astype(BF16)
        h1 = x_ref[...] + _nn(a_ref[...], wo_ref[:ATTN_WIDTH, :]) + _nn(m_ref[...], wo_ref[ATTN_WIDTH:, :])
        h1_ref[...] = h1
        hn_ref[...] = ((h1 * _rms(h1)) * g_ref[...]).astype(BF16)

    return _call(
        body,
        (u, u, a, x, w_out, w_pool, pool_scale, g_ffn),
        name="mix_out",
        grid=(s // ts,),
        in_specs=[
            _rows(ts, POOL_WIDTH),
            pl.BlockSpec((POOL_HALO, POOL_WIDTH), _halo_before(ts)),
            _rows(ts, ATTN_WIDTH),
            _rows(ts, D_MODEL),
            _resident((D_MODEL, D_MODEL)),
            _resident((len(POOL_SIZES), POOL_GROUP, POOL_GROUP)),
            _resident((1, POOL_WIDTH)),
            _resident((1, D_MODEL)),
        ],
        out_specs=[_rows(ts, D_MODEL), _rows(ts, D_MODEL), _rows(ts, POOL_WIDTH)],
        out_shape=[
            jax.ShapeDtypeStruct((s, D_MODEL), F32),
            jax.ShapeDtypeStruct((s, D_MODEL), BF16),
            jax.ShapeDtypeStruct((s, POOL_WIDTH), BF16),
        ],
        rider=rider,
    )


def _ffn_up(hn2, wg_t, wu_t, rider=None):
    s = hn2.shape[0]
    ts = min(TOKEN_TILE, s)

    def body(hn_ref, wg_ref, wu_ref, gt_ref, up_ref):
        hn = hn_ref[...]
        for c in range(D_FF // FF_CHUNK):
            cols = slice(c * FF_CHUNK, (c + 1) * FF_CHUNK)
            gt_ref[:, cols] = _nt(hn, wg_ref[cols, :]).astype(BF16)
            up_ref[:, cols] = _nt(hn, wu_ref[cols, :]).astype(BF16)

    return _call(
        body,
        (hn2, wg_t, wu_t),
        name="ffn_up",
        grid=(s // ts,),
        in_specs=[_rows(ts, D_MODEL), _resident((D_FF, D_MODEL)), _resident((D_FF, D_MODEL))],
        out_specs=[_rows(ts, D_FF), _rows(ts, D_FF)],
        out_shape=[jax.ShapeDtypeStruct((s, D_FF), BF16), jax.ShapeDtypeStruct((s, D_FF), BF16)],
        rider=rider,
    )


def _silu_mul(gt, up):
    return (gt * jax.nn.sigmoid(gt)) * up


def _ffn_down_ple(gt, up, h1, w_down, p, target, g_ple, w_pg, w_pp, rider=None):
    s = h1.shape[0]
    ts = min(TOKEN_TILE, s)
    blk = D_MODEL // N_DEV

    def body(gt_ref, up_ref, h1_ref, wd_ref, p_ref, t_ref, g_ref, wpg_ref, wpp_ref,
             loss_ref, dh_ref, dwpg_ref, dwpp_ref, dg_ref, act_ref, pp_ref):
        @pl.when(pl.program_id(0) == 0)
        def _():
            loss_ref[...] = jnp.zeros_like(loss_ref)
            dwpg_ref[...] = jnp.zeros_like(dwpg_ref)
            dwpp_ref[...] = jnp.zeros_like(dwpp_ref)
            dg_ref[...] = jnp.zeros_like(dg_ref)

        for c in range(D_FF // FF_CHUNK):
            cols = slice(c * FF_CHUNK, (c + 1) * FF_CHUNK)
            act_ref[:, cols] = _silu_mul(gt_ref[:, cols].astype(F32), up_ref[:, cols].astype(F32)).astype(BF16)
        h2v = h1_ref[...] + _nn(act_ref[...], wd_ref[...])
        r = _rms(h2v)
        hn = ((h2v * r) * g_ref[...]).astype(BF16)
        gate = jax.nn.sigmoid(_nn(hn, wpg_ref[...]))
        pb = p_ref[...].astype(BF16)
        for j in range(N_DEV):
            pp_ref[:, j * blk:(j + 1) * blk] = _nn(pb, wpp_ref[j])
        pp = pp_ref[...]
        diff = (h2v + gate * pp) - t_ref[...]
        loss_ref[...] += jnp.sum(jnp.sum(diff * diff, axis=0, keepdims=True), axis=1, keepdims=True) * (0.5 / D_MODEL)
        dy = diff * (1.0 / D_MODEL)
        d_pp = (dy * gate).astype(BF16)
        d_pre = ((dy * pp) * (gate * (1.0 - gate))).astype(BF16)
        for j in range(N_DEV):
            dwpp_ref[j] += _tn(pb, d_pp[:, j * blk:(j + 1) * blk])
        dwpg_ref[...] += _tn(hn, d_pre)
        d_x, d_g = _rms_bwd(_nt(d_pre, wpg_ref[...]), h2v, r, g_ref[...])
        dg_ref[...] += d_g
        dh_ref[...] = dy + d_x

    return _call(
        body,
        (gt, up, h1, w_down, p, target, g_ple, w_pg, w_pp),
        name="ffn_down_ple",
        grid=(s // ts,),
        in_specs=[
            _rows(ts, D_FF),
            _rows(ts, D_FF),
            _rows(ts, D_MODEL),
            _resident((D_FF, D_MODEL)),
            _rows(ts, PLE_DIM),
            _rows(ts, D_MODEL),
            _resident((1, D_MODEL)),
            _resident((D_MODEL, D_MODEL)),
            _resident((N_DEV, PLE_DIM, blk)),
        ],
        out_specs=[
            _acc((1, SMALL_LANES)),
            _rows(ts, D_MODEL),
            _acc((D_MODEL, D_MODEL)),
            _acc((N_DEV, PLE_DIM, blk)),
            _acc((1, D_MODEL)),
        ],
        out_shape=[
            jax.ShapeDtypeStruct((1, SMALL_LANES), F32),
            jax.ShapeDtypeStruct((s, D_MODEL), F32),
            jax.ShapeDtypeStruct((D_MODEL, D_MODEL), F32),
            jax.ShapeDtypeStruct((N_DEV, PLE_DIM, blk), F32),
            jax.ShapeDtypeStruct((1, D_MODEL), F32),
        ],
        scratch_shapes=[pltpu.VMEM((ts, D_FF), BF16), pltpu.VMEM((ts, D_MODEL), F32)],
        rider=rider,
    )


def _ffn_bwd_act(dh2, h1, gt, up, g_ffn, wg_t, wu_t, w_down, rider=None):
    s = h1.shape[0]
    ts = min(FFN_BWD_TILE, s)

    def body(dh_ref, h1_ref, gt_ref, up_ref, g_ref, wg_ref, wu_ref, wd_ref,
             dgt_ref, dup_ref, dh1_ref, dh1b_ref, dg_ref, dwd_ref, act_ref):
        @pl.when(pl.program_id(0) == 0)
        def _():
            dg_ref[...] = jnp.zeros_like(dg_ref)
            dwd_ref[...] = jnp.zeros_like(dwd_ref)

        dhb = dh_ref[...].astype(BF16)
        for c in range(D_FF // FF_CHUNK):
            cols = slice(c * FF_CHUNK, (c + 1) * FF_CHUNK)
            d_act = _nt(dhb, wd_ref[cols, :])
            gtv = gt_ref[:, cols].astype(F32)
            upv = up_ref[:, cols].astype(F32)
            sg = jax.nn.sigmoid(gtv)
            silu = gtv * sg
            act_ref[:, cols] = (silu * upv).astype(BF16)
            dup_ref[:, cols] = (d_act * silu).astype(BF16)
            dgt_ref[:, cols] = ((d_act * upv) * (sg * (1.0 + gtv * (1.0 - sg)))).astype(BF16)
        dwd_ref[...] += _tn(act_ref[...], dhb)
        d_hn =_nn(dgt_ref[...], wg_ref[...]) + _nn(dup_ref[...], wu_ref[...])
        h1v = h1_ref[...]
        d_x, d_g = _rms_bwd(d_hn, h1v, _rms(h1v), g_ref[...])
        dg_ref[...] += d_g
        dh1 = dh_ref[...] + d_x
        dh1_ref[...] = dh1
        dh1b_ref[...] = dh1.astype(BF16)

    return _call(
        body,
        (dh2, h1, gt, up, g_ffn, wg_t, wu_t, w_down),
        name="ffn_bwd_act",
        grid=(s // ts,),
        in_specs=[
            _rows(ts, D_MODEL),
            _rows(ts, D_MODEL),
            _rows(ts, D_FF),
            _rows(ts, D_FF),
            _resident((1, D_MODEL)),
            _resident((D_FF, D_MODEL)),
            _resident((D_FF, D_MODEL)),
            _resident((D_FF, D_MODEL)),
        ],
        out_specs=[
            _rows(ts, D_FF), _rows(ts, D_FF),
            _rows(ts, D_MODEL), _rows(ts, D_MODEL), _acc((1, D_MODEL)), _acc((D_FF, D_MODEL)),
        ],
        out_shape=[
            jax.ShapeDtypeStruct((s, D_FF), BF16),
            jax.ShapeDtypeStruct((s, D_FF), BF16),
            jax.ShapeDtypeStruct((s, D_MODEL), F32),
            jax.ShapeDtypeStruct((s, D_MODEL), BF16),
            jax.ShapeDtypeStruct((1, D_MODEL), F32),
            jax.ShapeDtypeStruct((D_FF, D_MODEL), F32),
        ],
        scratch_shapes=[pltpu.VMEM((ts, D_FF), BF16)],
        rider=rider,
    )


def _ffn_bwd_w(dgt, dup, hn2, rider=None):
    s = hn2.shape[0]
    slab = pl.BlockSpec((s, FF_CHUNK), lambda i: (0, i))

    def body(dgt_ref, dup_ref, hn_ref, dwg_ref, dwu_ref):
        hn = hn_ref[...]
        dwg_ref[...] = _tn(dgt_ref[...], hn)
        dwu_ref[...] = _tn(dup_ref[...], hn)

    return _call(
        body,
        (dgt, dup, hn2),
        name="ffn_bwd_w",
        grid=(D_FF // FF_CHUNK,),
        in_specs=[slab, slab, _resident((s, D_MODEL))],
        out_specs=[_rows(FF_CHUNK, D_MODEL)] * 2,
        out_shape=[jax.ShapeDtypeStruct((D_FF, D_MODEL), F32)] * 2,
        rider=rider,
    )


def _mix_bwd(dh1b, u, w_out, w_pool, pool_scale, rider=None):
    s = u.shape[0]
    ts = min(TOKEN_TILE, s)
    nt = s // ts
    halo_after = lambda i: (jnp.minimum((i + 1) * (ts // POOL_HALO), s // POOL_HALO - 1), 0)
    n_groups = len(POOL_SIZES)

    def body(dh_ref, dhn_ref, u_ref, uh_ref, wo_ref, wp_ref, sc_ref, da_ref, du_ref, dwp_ref, dsc_ref):
        i = pl.program_id(0)

        @pl.when(i == 0)
        def _():
            dwp_ref[...] = jnp.zeros_like(dwp_ref)
            dsc_ref[...] = jnp.zeros_like(dsc_ref)

        dh = dh_ref[...]
        da_ref[...] = _nt(dh, wo_ref[:ATTN_WIDTH, :])
        dh_next = jnp.where(i < nt - 1, dhn_ref[...], jnp.zeros_like(dhn_ref))
        dm_ext = _nt(jnp.concatenate([dh, dh_next], axis=0), wo_ref[ATTN_WIDTH:, :])
        pooled = _pooled(u_ref[...], uh_ref[...], i, ts)
        t_ext = i * ts + lax.broadcasted_iota(jnp.int32, (ts + POOL_HALO, 1), 0)
        for g, w in enumerate(POOL_SIZES):
            lanes = slice(g * POOL_GROUP, (g + 1) * POOL_GROUP)
            wp = wp_ref[g].astype(BF16)
            pg = pooled[g].astype(BF16)
            dm_g = dm_ext[:, lanes]
            dsc_ref[:, lanes] += jnp.sum(dm_g[:ts, :] * _nn(pg, wp), axis=0, keepdims=True)
            dy = (dm_g * sc_ref[:, lanes]).astype(BF16)
            dwp_ref[g] += _tn(pg, dy[:ts, :])
            d_pool = _nt(dy, wp)
            acc = d_pool / jnp.minimum(t_ext + 1, w).astype(F32)
            shift = 1
            while shift < w:
                acc = acc + pltpu.roll(acc, ts + POOL_HALO - shift, 0)
                shift *= 2
            du_ref[:, lanes] = acc[:ts, :] - d_pool[:ts, :]

    return _call(
        body,
        (dh1b, dh1b, u, u, w_out, w_pool, pool_scale),
        name="mix_bwd",
        grid=(nt,),
        in_specs=[
            _rows(ts, D_MODEL),
            pl.BlockSpec((POOL_HALO, D_MODEL), halo_after),
            _rows(ts, POOL_WIDTH),
            pl.BlockSpec((POOL_HALO, POOL_WIDTH), _halo_before(ts)),
            _resident((D_MODEL, D_MODEL)),
            _resident((n_groups, POOL_GROUP, POOL_GROUP)),
            _resident((1, POOL_WIDTH)),
        ],
        out_specs=[
            _rows(ts, ATTN_WIDTH),
            _rows(ts, POOL_WIDTH),
            _acc((n_groups, POOL_GROUP, POOL_GROUP)),
            _acc((1, POOL_WIDTH)),
        ],
        out_shape=[
            jax.ShapeDtypeStruct((s, ATTN_WIDTH), F32),
            jax.ShapeDtypeStruct((s, POOL_WIDTH), F32),
            jax.ShapeDtypeStruct((n_groups, POOL_GROUP, POOL_GROUP), F32),
            jax.ShapeDtypeStruct((1, POOL_WIDTH), F32),
        ],
        rider=rider,
    )


def _out_w_bwd(a, m, dh1b, rider=None):
    s = dh1b.shape[0]

    def body(a_ref, m_ref, dh_ref, dw_ref):
        @pl.when(pl.program_id(0) == 0)
        def _():
            dw_ref[...] = _tn(a_ref[...], dh_ref[...])

        @pl.when(pl.program_id(0) == 1)
        def _():
            dw_ref[...] = _tn(m_ref[...], dh_ref[...])

    return _call(
        body,
        (a, m, dh1b),
        name="out_w_bwd",
        grid=(2,),
        in_specs=[_resident((s, ATTN_WIDTH)), _resident((s, POOL_WIDTH)), _resident((s, D_MODEL))],
        out_specs=[_rows(ATTN_WIDTH, D_MODEL)],
        out_shape=[jax.ShapeDtypeStruct((D_MODEL, D_MODEL), F32)],
        rider=rider,
    )


def _attn_bwd(qn, kn, v, a, da, tab, sinks, rider=None):
    s = qn.shape[0]
    qb = ATTN_STEP_BLOCKS
    rows = qb * BLOCK
    n_groups = s // rows
    group, prev = _attn_specs(n_groups)
    done = lambda n: (jnp.maximum(n - 1, 0), 0)

    def body(sink_ref, q_ref, kc_ref, kp_ref, vc_ref, vp_ref, o_ref, do_ref, tab_ref,
             dq_ref, dk_ref, dv_ref, dl_ref, ds_ref, k_carry, v_carry, sink_acc):
        n = pl.program_id(0)

        @pl.when(n == 0)
        def _():
            dl_ref[...] = jnp.zeros_like(dl_ref)
            k_carry[...] = jnp.zeros_like(k_carry)
            v_carry[...] = jnp.zeros_like(v_carry)
            sink_acc[...] = jnp.zeros_like(sink_acc)

        @pl.when(n < n_groups)
        def _():
            first = n == 0
            lo_mask = _lane_lo((BLOCK, BLOCK))
            dks, dvs = [], []
            for b in range(qb):
                at = slice(b * BLOCK, (b + 1) * BLOCK)
                keys = _band(kp_ref, kc_ref, b)
                vals = _band(vp_ref, vc_ref, b)
                q_st = _stack_heads([q_ref[at, p * BLOCK:(p + 1) * BLOCK] for p in range(4)], lo_mask)
                do_st = _stack_heads([do_ref[at, p * BLOCK:(p + 1) * BLOCK] for p in range(4)], lo_mask)
                o_st = _stack_heads([o_ref[at, p * BLOCK:(p + 1) * BLOCK].astype(F32) for p in range(4)], lo_mask)
                dq_st, dk_parts, dv_parts = [], [], []
                for half, heads in enumerate((HEADS_A, HEADS_B)):
                    probs, p_sink = _band_probs(q_st[half], keys[half], tab_ref[half], _sink_column(sink_ref, heads),
                                                first if b == 0 else None)
                    delta = jnp.sum(do_st[half] * o_st[half], axis=-1, keepdims=True)
                    dob = do_st[half].astype(BF16)
                    dl = probs * (_nt(dob, vals[half]) - delta)
                    dl_ref[half] += dl
                    sink_acc[half] += p_sink * delta
                    dsb = (dl * (HEAD_DIM ** -0.5)).astype(BF16)
                    dq_st.append(_nn(dsb, keys[half]))
                    dk_parts.append(_tn(dsb, q_st[half]))
                    dv_parts.append(_tn(probs.astype(BF16), dob))
                dq = _unstack_heads(dq_st[0], dq_st[1], lo_mask)
                for p in range(4):
                    dq_ref[at, p * BLOCK:(p + 1) * BLOCK] = dq[p]
                dks.append(dk_parts[0] + pltpu.roll(dk_parts[1], HEAD_DIM, 1))
                dvs.append(dv_parts[0] + pltpu.roll(dv_parts[1], HEAD_DIM, 1))
            last = slice((qb - 1) * BLOCK, qb * BLOCK)
            for parts, out_ref, carry in ((dks, dk_ref, k_carry), (dvs, dv_ref, v_carry)):
                out_ref[...] = carry[...]
                out_ref[last, :] += parts[0][:BLOCK, :]
                for b in range(qb):
                    own = parts[b][BLOCK:, :]
                    carry[b * BLOCK:(b + 1) * BLOCK, :] = own + parts[b + 1][:BLOCK, :] if b + 1 < qb else own

        @pl.when(n == n_groups)
        def _():
            dk_ref[...] = k_carry[...]
            dv_ref[...] = v_carry[...]
            for half, heads in enumerate((HEADS_A, HEADS_B)):
                for slot, h in enumerate(heads):
                    tot = jnp.sum(sink_acc[half, slot * BLOCK:(slot + 1) * BLOCK, :], axis=0, keepdims=True)
                    ds_ref[h:h + 1, :] = jnp.broadcast_to(-tot, (1, SMALL_LANES))

    return _call(
        body,
        (sinks, qn, kn, kn, v, v, a, da, tab),
        name="attn_bwd",
        grid=(n_groups + 1,),
        in_specs=[
            pl.BlockSpec(memory_space=pltpu.SMEM),
            pl.BlockSpec((rows, ATTN_WIDTH), group),
            pl.BlockSpec((rows, KV_WIDTH), group),
            pl.BlockSpec((BLOCK, KV_WIDTH), prev),
            pl.BlockSpec((rows, KV_WIDTH), group),
            pl.BlockSpec((BLOCK, KV_WIDTH), prev),
            pl.BlockSpec((rows, ATTN_WIDTH), group),
            pl.BlockSpec((rows, ATTN_WIDTH), group),
            _resident((2, 4 * BLOCK, 2 * BLOCK)),
        ],
        out_specs=[
            pl.BlockSpec((rows, ATTN_WIDTH), group),
            pl.BlockSpec((rows, KV_WIDTH), done),
            pl.BlockSpec((rows, KV_WIDTH), done),
            _acc((2, 4 * BLOCK, 2 * BLOCK)),
            _acc((N_DEV, SMALL_LANES)),
        ],
        out_shape=[
            jax.ShapeDtypeStruct((s, ATTN_WIDTH), F32),
            jax.ShapeDtypeStruct((s, KV_WIDTH), F32),
            jax.ShapeDtypeStruct((s, KV_WIDTH), F32),
            jax.ShapeDtypeStruct((2, 4 * BLOCK, 2 * BLOCK), F32),
            jax.ShapeDtypeStruct((N_DEV, SMALL_LANES), F32),
        ],
        scratch_shapes=[
            pltpu.VMEM((rows, KV_WIDTH), F32),
            pltpu.VMEM((rows, KV_WIDTH), F32),
            pltpu.VMEM((2, 4 * BLOCK, 1), F32),
        ],
        rider=rider,
    )


def _fold_heads(acc):
    t = acc + pltpu.roll(acc, HEAD_DIM, 1)
    out = t[:, :SMALL_LANES]
    for g in range(1, acc.shape[1] // SMALL_LANES):
        out = out + t[:, g * SMALL_LANES:(g + 1) * SMALL_LANES]
    return out


def _in_proj_bwd(dqn, dkn, dv, du, zqk, x, dh1, g_attn, gq_t, gk_t, w_in_t, rider=None):
    s = x.shape[0]
    ts = min(TOKEN_TILE, s)
    nt = s // ts

    def head_norm_bwd(d_n, raw, g_t, bmat):
        r = lax.rsqrt(_seg_mean(raw * raw, bmat) + EPS)
        gy = d_n * g_t
        d_raw = r * gy - raw * (r * r * r) * _seg_mean(gy * raw, bmat)
        return d_raw, jnp.sum(d_n * (raw * r), axis=0, keepdims=True)

    def body(dqn_ref, dkn_ref, dv_ref, du_ref, zqk_ref, x_ref, dh1_ref, g_ref, gq_ref, gk_ref, w_ref, bq_ref, bk_ref,
             gx_ref, dw_ref, dg_ref, dgq_ref, dgk_ref, dz_ref, gq_acc, gk_acc):
        i = pl.program_id(0)

        @pl.when(i == 0)
        def _():
            dw_ref[...] = jnp.zeros_like(dw_ref)
            dg_ref[...] = jnp.zeros_like(dg_ref)
            gq_acc[...] = jnp.zeros_like(gq_acc)
            gk_acc[...] = jnp.zeros_like(gk_acc)

        d_q, d_gq = head_norm_bwd(dqn_ref[...], zqk_ref[:, :ATTN_WIDTH], gq_ref[...], bq_ref[...])
        d_k, d_gk = head_norm_bwd(dkn_ref[...], zqk_ref[:, ATTN_WIDTH:], gk_ref[...], bk_ref[...])
        gq_acc[...] += d_gq
        gk_acc[...] += d_gk
        dz_ref[:, :ATTN_WIDTH] = d_q.astype(BF16)
        dz_ref[:, ATTN_WIDTH:ATTN_WIDTH + KV_WIDTH] = d_k.astype(BF16)
        dz_ref[:, ATTN_WIDTH + KV_WIDTH:ATTN_WIDTH + 2 * KV_WIDTH] = dv_ref[...].astype(BF16)
        dz_ref[:, ATTN_WIDTH + 2 * KV_WIDTH:] = du_ref[...].astype(BF16)
        dz = dz_ref[...]
        xf = x_ref[...]
        r = _rms(xf)
        hn = ((xf * r) * g_ref[...]).astype(BF16)
        dw_ref[...] += _tn(dz, hn)
        d_x, d_g = _rms_bwd(_nn(dz, w_ref[...]), xf, r, g_ref[...])
        dg_ref[...] += d_g
        gx_ref[...] = dh1_ref[...] + d_x

        @pl.when(i == nt - 1)
        def _():
            dgq_ref[...] = _fold_heads(gq_acc[...])
            dgk_ref[...] = _fold_heads(gk_acc[...])

    return _call(
        body,
        (dqn, dkn, dv, du, zqk, x, dh1, g_attn, gq_t, gk_t, w_in_t,
      _head_mean_matrix(ATTN_WIDTH), _head_mean_matrix(KV_WIDTH)),
        name="in_proj_bwd",
        grid=(nt,),
        in_specs=[
            _rows(ts, ATTN_WIDTH),
            _rows(ts, KV_WIDTH),
            _rows(ts, KV_WIDTH),
            _rows(ts, POOL_WIDTH),
            _rows(ts, ATTN_WIDTH + KV_WIDTH),
            _rows(ts, D_MODEL),
            _rows(ts, D_MODEL),
            _resident((1, D_MODEL)),
            _resident((1, ATTN_WIDTH)),
            _resident((1, KV_WIDTH)),
            _resident((IN_WIDTH, D_MODEL)),
            _resident((ATTN_WIDTH, ATTN_WIDTH)),
            _resident((KV_WIDTH, KV_WIDTH)),
        ],
        out_specs=[
            _rows(ts, D_MODEL),
            _acc((IN_WIDTH, D_MODEL)),
            _acc((1, D_MODEL)),
            _acc((1, SMALL_LANES)),
            _acc((1, SMALL_LANES)),
        ],
        out_shape=[
            jax.ShapeDtypeStruct((s, D_MODEL), F32),
            jax.ShapeDtypeStruct((IN_WIDTH, D_MODEL), F32),
            jax.ShapeDtypeStruct((1, D_MODEL), F32),
            jax.ShapeDtypeStruct((1, SMALL_LANES), F32),
            jax.ShapeDtypeStruct((1, SMALL_LANES), F32),
        ],
        scratch_shapes=[
            pltpu.VMEM((ts, IN_WIDTH), BF16),
            pltpu.VMEM((1, ATTN_WIDTH), F32),
            pltpu.VMEM((1, KV_WIDTH), F32),
        ],
        rider=rider,
    )


BIG_WEIGHTS = (
    ("w_in", True, IN_WIDTH // N_DEV, D_MODEL),
    ("w_out", False, D_MODEL // N_DEV, D_MODEL),
    ("w_gate", True, D_FF // N_DEV, D_MODEL),
    ("w_up", True, D_FF // N_DEV, D_MODEL),
    ("w_down", False, D_FF // N_DEV, D_MODEL),
    ("w_ple_gate", False, D_MODEL // N_DEV, D_MODEL),
    ("w_ple_proj", False, PLE_DIM, D_MODEL // N_DEV),
)
N_BIG = len(BIG_WEIGHTS)


def _place():
    x, y, c = lax.axis_index("x"), lax.axis_index("y"), lax.axis_index("c")
    chips = [(1 - x, y), (x, 1 - y), (1 - x, 1 - y)]
    return x, y, c, chips


class _Gather:
    def __init__(self, n, rows=None):
        self.n = n
        self.rows = rows or [None] * n
        self.sems = [pltpu.SemaphoreType.DMA((n, 7)), pltpu.SemaphoreType.DMA((n, 7)), pltpu.SemaphoreType.DMA((n,))]

    def _ctx(self, srcs, outs, sems):
        send_sems, recv_sems, local_sems = sems
        x, y, c, chips = _place()
        me, sibling = (x, y, c), (x, y, 1 - c)

        def part(k, ref):
            return ref if self.rows[k] is None else ref.at[pl.ds(*self.rows[k]), :]

        def block(k, owner):
            px, py, pc = owner
            return part(k, outs[k].at[4 * px + 2 * py + pc])

        def copy(k, idx, owner, to, mine=False):
            return pltpu.make_async_remote_copy(
                src_ref=part(k, srcs[k]) if mine else block(k, owner), dst_ref=block(k, owner),
                send_sem=send_sems.at[k, idx], recv_sem=recv_sems.at[k, idx], device_id=to, device_id_type=MESH)

        def local(k):
            return pltpu.make_async_copy(part(k, srcs[k]), block(k, me), local_sems.at[k])

        return c, chips, me, sibling, copy, local

    def begin(self, srcs, outs, sems):
        c, chips, me, sibling, copy, local = self._ctx(srcs, outs, sems)
        for k in range(self.n):
            local(k).start()
            copy(k, 0, me, sibling, mine=True).start()
            for j, chip in enumerate(chips):
                copy(k, 1 + j, me, (*chip, c), mine=True).start()

    def middle(self, srcs, outs, sems):
        c, chips, me, sibling, copy, local = self._ctx(srcs, outs, sems)
        for j, chip in enumerate(chips):
            for k in range(self.n):
                copy(k, 1 + j, (*chip, c), me).wait_recv()
                copy(k, 4 + j, (*chip, c), sibling).start()

    def end(self, srcs, outs, sems):
        c, chips, me, sibling, copy, local = self._ctx(srcs, outs, sems)
        for k in range(self.n):
            copy(k, 0, sibling, me).wait_recv()
            for j, chip in enumerate(chips):
                copy(k, 4 + j, (*chip, 1 - c), me).wait_recv()
        for k in range(self.n):
            copy(k, 0, me, sibling, mine=True).wait_send()
            for j, chip in enumerate(chips):
                copy(k, 1 + j, me, (*chip, c), mine=True).wait_send()
                copy(k, 4 + j, (*chip, c), sibling).wait_send()
            local(k).wait()


def _gather_rider(items):
    items = [it if isinstance(it, tuple) else (it, None, None, None) for it in items]
    n = len(items)
    g = _Gather(n, [None if r0 is None else (r0, nr) for _, r0, nr, _ in items])
    shapes = [jax.ShapeDtypeStruct((N_DEV, *sh.shape), sh.dtype) for sh, _, _, _ in items]
    stacks = [(k, st) for k, (_, _, _, st) in enumerate(items) if st is not None]
    aliases = {n + i: k for i, (k, _) in enumerate(stacks)}
    return _Rider([sh for sh, _, _, _ in items] + [st for _, st in stacks], shapes, g.sems, g.begin, g.end, g.middle,
                  aliases=aliases)


def _cast_and_gather_first(shards):
    g = _Gather(1)
    any_spec = pl.BlockSpec(memory_space=pl.ANY)
    vmem = pl.BlockSpec(memory_space=pltpu.VMEM)

    def body(*refs):
        ins, outs, gathered, sems = refs[:N_BIG], refs[N_BIG:2 * N_BIG], refs[2 * N_BIG], refs[2 * N_BIG + 1:]
        outs[0][...] = ins[0][...].astype(BF16)
        g.begin(outs[:1], [gathered], sems)
        for k in range(1, N_BIG):
            outs[k][...] = ins[k][...].astype(BF16)
        g.middle(outs[:1], [gathered], sems)
        g.end(outs[:1], [gathered], sems)

    res = pl.pallas_call(
        body,
        name="cast_and_gather_first",
        in_specs=[vmem] * N_BIG,
        out_specs=[vmem] * N_BIG + [any_spec],
        out_shape=[jax.ShapeDtypeStruct((r, c), BF16) for _, _, r, c in BIG_WEIGHTS]
        + [jax.ShapeDtypeStruct((N_DEV, *BIG_WEIGHTS[0][2:]), BF16)],
        scratch_shapes=g.sems,
    )(*shards)
    return list(res[:N_BIG]), res[N_BIG]


def _sibling_rider(grads):
    n = len(grads)

    def copies(gs, lands, sems):
        send_sems, recv_sems = sems
        x, y, c, _ = _place()
        return [
            pltpu.make_async_remote_copy(
                src_ref=gs[k].at[:, 1 - c], dst_ref=lands[k], send_sem=send_sems.at[k], recv_sem=recv_sems.at[k],
                device_id=(x, y, 1 - c), device_id_type=MESH)
            for k in range(n)
        ]

    def begin(gs, lands, sems):
        for cp in copies(gs, lands, sems):
            cp.start()

    def end(gs, lands, sems):
        for cp in copies(gs, lands, sems):
            cp.wait()

    shapes = [jax.ShapeDtypeStruct((N_CHIPS, *g.shape[2:]), F32) for g in grads]
    return _Rider(grads, shapes, [pltpu.SemaphoreType.DMA((n,)), pltpu.SemaphoreType.DMA((n,))], begin, end)


def _chip_sum(ks, place, grads, from_sibling):
    n = len(ks)
    shapes = [BIG_WEIGHTS[k][2:] for k in ks]
    operands, specs = [], []
    for (r, c), g, l in zip(shapes, grads, from_sibling):
        operands += [g, l]
        specs += [pl.BlockSpec((1, 1, r, c), lambda q, place: (q, place[2], 0, 0)),
                  pl.BlockSpec((1, r, c), lambda q, place: (q, 0, 0))]
    args, in_specs, _ = _after_last(operands, specs)

    def body(place_ref, *refs):
        ins, outs = refs[:2 * n], refs[len(args):]
        mine = pl.program_id(0) == 2 * place_ref[0] + place_ref[1]
        for i in range(n):
            tot = ins[2 * i][0, 0] + ins[2 * i + 1][0]

            @pl.when(mine)
            def _(own_ref=outs[2 * i], tot=tot):
                own_ref[...] = tot

            outs[2 * i + 1][0] = jnp.where(mine, 0.0, tot).astype(BF16)

    outs = pl.pallas_call(
        body,
        name="chip_sum_" + "_".join(BIG_WEIGHTS[k][0] for k in ks),
        grid_spec=pltpu.PrefetchScalarGridSpec(
            num_scalar_prefetch=1,
            grid=(N_CHIPS,),
            in_specs=in_specs,
            out_specs=sum(([pl.BlockSpec((r, c), lambda q, place: (0, 0)),
                            pl.BlockSpec((1, r, c), lambda q, place: (q, 0, 0))] for r, c in shapes), []),
        ),
        out_shape=sum(([jax.ShapeDtypeStruct((r, c), F32), jax.ShapeDtypeStruct((N_CHIPS, r, c), BF16)]
                       for r, c in shapes), []),
    )(place, *args)
    _mark_issued(outs[0])
    return [(outs[2 * i], outs[2 * i + 1]) for i in range(n)]


def _chips_rider(to_send, small=None):
    n = len(to_send)
    inputs = list(to_send) + ([] if small is None else [small])
    shapes = [jax.ShapeDtypeStruct((3, *t.shape[1:]), BF16) for t in to_send]
    sems = [pltpu.SemaphoreType.DMA((max(n, 1), 3)), pltpu.SemaphoreType.DMA((max(n, 1), 3))]
    if small is not None:
        shapes.append(jax.ShapeDtypeStruct((N_DEV, *small.shape), F32))
        sems += [pltpu.SemaphoreType.DMA((7,)), pltpu.SemaphoreType.DMA((7,)), pltpu.SemaphoreType.DMA]

    def copies(ins, outs, sem_refs):
        x, y, c, chips = _place()
        out = []
        for k in range(n):
            for j, (px, py) in enumerate(chips):
                out.append(pltpu.make_async_remote_copy(
                    src_ref=ins[k].at[2 * px + py], dst_ref=outs[k].at[j],
                    send_sem=sem_refs[0].at[k, j], recv_sem=sem_refs[1].at[k, j],
                    device_id=(px, py, c), device_id_type=MESH))
        local = None
        if small is not None:
            me = 4 * x + 2 * y + c
            local = pltpu.make_async_copy(ins[n], outs[n].at[me], sem_refs[4])
            rel = 0
            for fx in (0, 1):
                for fy in (0, 1):
                    for fc in (0, 1):
                        if (fx, fy, fc) != (0, 0, 0):
                            out.append(pltpu.make_async_remote_copy(
                                src_ref=ins[n], dst_ref=outs[n].at[me],
                                send_sem=sem_refs[2].at[rel], recv_sem=sem_refs[3].at[rel],
                                device_id=(x ^ fx, y ^ fy, c ^ fc), device_id_type=MESH))
                            rel += 1
        return out, local

    def begin(ins, outs, sem_refs):
        remote, local = copies(ins, outs, sem_refs)
        if local is not None:
            local.start()
        for cp in remote:
            cp.start()

    def end(ins, outs, sem_refs):
        remote, local = copies(ins, outs, sem_refs)
        for cp in remote:
            cp.wait()
        if local is not None:
            local.wait()

    return _Rider(inputs, shapes, sems, begin, end)


def _exchange(name, rider):
    return _call(lambda: None, (), name=name, grid=(1,), in_specs=[], out_specs=[], out_shape=[], rider=rider)[1]


PEER_SETS = {"sibling": 1, "chips": 2, "sibling+chips": 3, "all": 4}


def _peers(pattern):
    x, y, c, chips = _place()
    sibling, others = [(x, y, 1 - c)], [(*chip, c) for chip in chips]
    if pattern == "all":
        return sibling + others + [(*chip, 1 - c) for chip in chips]
    return {"sibling": sibling, "chips": others, "sibling+chips": sibling + others}[pattern]


def _on_sequencer(name, pattern, rider):
    assert not rider.aliases
    n_in, n_out = len(rider.inputs), len(rider.out_shapes)

    def body(*refs):
        ins, outs, sems = refs[:n_in], refs[n_in:n_in + n_out], refs[n_in + n_out:]
        peers = _peers(pattern)
        barrier = pltpu.get_barrier_semaphore()
        for peer in peers:
            pl.semaphore_signal(barrier, inc=1, device_id=peer, device_id_type=MESH)
        pl.semaphore_wait(barrier, len(peers))
        rider.begin(ins, outs, sems)
        if rider.middle is not None:
            rider.middle(ins, outs, sems)
        rider.end(ins, outs, sems)

    outs = pl.kernel(
        body,
        name=name,
        out_type=tuple(rider.out_shapes),
        mesh=plsc.ScalarSubcoreMesh(axis_name="sequencer", num_cores=1),
        scratch_types=tuple(rider.sems),
        compiler_params=pltpu.CompilerParams(collective_id=PEER_SETS[pattern]),
    )(*rider.inputs)
    return list(outs)


def _merge_riders(*riders):
    riders = [r for r in riders if r is not None]
    if len(riders) == 1:
        return riders[0]
    assert not any(r.aliases for r in riders)

    def split(refs, counts):
        out, at = [], 0
        for n in counts:
            out.append(refs[at:at + n])
            at += n
        return out

    def run(which):
        def fn(ins, outs, sems):
            parts = zip(riders, split(ins, [len(r.inputs) for r in riders]),
                        split(outs, [len(r.out_shapes) for r in riders]), split(sems, [len(r.sems) for r in riders]))
            for r, i, o, s in parts:
                hook = getattr(r, which)
                if hook is not None:
                    hook(i, o, s)
        return fn

    middle = run("middle") if any(r.middle is not None for r in riders) else None
    return _Rider(sum((r.inputs for r in riders), []), sum((r.out_shapes for r in riders), []),
                  sum((r.sems for r in riders), []), run("begin"), run("end"), middle)


def _split_outputs(outs, *riders):
    res, at = [], 0
    for r in riders:
        res.append(outs[at:at + len(r.out_shapes)])
        at += len(r.out_shapes)
    return res


def _adamw(w, g, m, v):
    m = ADAM_B1 * m + (1.0 - ADAM_B1) * g
    v = ADAM_B2 * v + (1.0 - ADAM_B2) * jnp.square(g)
    m_hat = m / (1.0 - ADAM_B1 ** ADAM_STEP)
    v_hat = v / (1.0 - ADAM_B2 ** ADAM_STEP)
    delta = -ADAM_LR * (m_hat / (jnp.sqrt(v_hat) + ADAM_EPS) + ADAM_WD * w)
    return delta, m, v


def _adamw_big(ks, operands):
    n = len(ks)
    tiles = lambda i: (i, 0)
    in_specs, out_specs, out_shape = [], [], []
    for k in ks:
        _, _, r, c = BIG_WEIGHTS[k]
        tile = r // 2
        in_specs += [pl.BlockSpec((tile, c), tiles), pl.BlockSpec((3, tile, c), lambda i: (0, i, 0))]
        in_specs += [pl.BlockSpec((tile, c), tiles)] * 3
        out_specs += [pl.BlockSpec((tile, c), tiles)] * 4
        out_shape += [jax.ShapeDtypeStruct((r, c), F32)] * 4

    def body(*refs):
        ins, outs = refs[:5 * n], refs[5 * n:]
        for i in range(n):
            own_ref, land_ref, w_ref, m_ref, v_ref = ins[5 * i:5 * i + 5]
            g_ref, d_ref, nm_ref, nv_ref = outs[4 * i:4 * i + 4]
            g = ((own_ref[...] + land_ref[0].astype(F32)) + land_ref[1].astype(F32)) + land_ref[2].astype(F32)
            g_ref[...] = g
            d_ref[...], nm_ref[...], nv_ref[...] = _adamw(w_ref[...], g, m_ref[...], v_ref[...])

    outs, _ = _call(
        body,
        sum((list(ops) for ops in operands), []),
        name="adamw_" + "_".join(BIG_WEIGHTS[k][0] for k in ks),
        grid=(2,),
        in_specs=in_specs,
        out_specs=out_specs,
        out_shape=out_shape,
    )
    return [outs[4 * i:4 * i + 4] for i in range(n)]


def _sum_small(parts_list):
    n = len(parts_list)

    def body(*refs):
        for p_ref, out_ref in zip(refs[:n], refs[n:]):
            tot = p_ref[0]
            for j in range(1, N_DEV):
                tot = tot + p_ref[j]
            out_ref[...] = tot

    return pl.pallas_call(body, name="sum_small",
                          out_shape=[jax.ShapeDtypeStruct(p.shape[1:], F32) for p in parts_list])(*parts_list)


def _adamw_small(grads, ws, ms, vs):
    n = len(grads)

    def body(*refs):
        g_refs, w_refs, m_refs, v_refs = refs[:n], refs[n:2 * n], refs[2 * n:3 * n], refs[3 * n:4 * n]
        outs = refs[4 * n:]
        for i in range(n):
            d, nm, nv = _adamw(w_refs[i][...], g_refs[i][...], m_refs[i][...], v_refs[i][...])
            outs[i][...] = d
            outs[n + i][...] = nm
            outs[2 * n + i][...] = nv

    shapes = [jax.ShapeDtypeStruct(w.shape, F32) for w in ws]
    return pl.pallas_call(body, name="adamw_small", out_shape=shapes * 3)(*grads, *ws, *ms, *vs)


SMALL_NAMES = ("g_attn_norm", "g_q", "g_k", "attn_sinks", "rel_bias", "w_pool", "pool_scale", "g_ffn_norm", "g_ple_norm")


def _pack_small(arrays):
    rows, offsets = [], []
    at = 0
    for a in arrays:
        flat = a.reshape(-1)
        n_rows = -(-flat.shape[0] // (8 * SMALL_LANES)) * 8
        flat = jnp.pad(flat, (0, n_rows * SMALL_LANES - flat.shape[0]))
        rows.append(flat.reshape(n_rows, SMALL_LANES))
        offsets.append(at)
        at += n_rows
    return jnp.concatenate(rows, axis=0), offsets


def kernel(x, p, w_in, w_out, g_attn_norm, g_q, g_k, attn_sinks, rel_bias, w_pool, pool_scale, g_ffn_norm, w_gate, w_up, w_down, g_ple_norm, w_ple_gate, w_ple_proj, loss_target, m_w_in, m_w_out, m_g_attn_norm, m_g_q, m_g_k, m_attn_sinks, m_rel_bias, m_w_pool, m_pool_scale, m_g_ffn_norm, m_w_gate, m_w_up, m_w_down, m_g_ple_norm, m_w_ple_gate, m_w_ple_proj, v_w_in, v_w_out, v_g_attn_norm, v_g_q, v_g_k, v_attn_sinks, v_rel_bias, v_w_pool, v_pool_scale, v_g_ffn_norm, v_w_gate, v_w_up, v_w_down, v_g_ple_norm, v_w_ple_gate, v_w_ple_proj):
    weights = dict(w_in=w_in, w_out=w_out, g_attn_norm=g_attn_norm, g_q=g_q, g_k=g_k, attn_sinks=attn_sinks,
                   rel_bias=rel_bias, w_pool=w_pool, pool_scale=pool_scale, g_ffn_norm=g_ffn_norm, w_gate=w_gate,
                   w_up=w_up, w_down=w_down, g_ple_norm=g_ple_norm, w_ple_gate=w_ple_gate, w_ple_proj=w_ple_proj)
    m_in = dict(w_in=m_w_in, w_out=m_w_out, g_attn_norm=m_g_attn_norm, g_q=m_g_q, g_k=m_g_k, attn_sinks=m_attn_sinks,
                rel_bias=m_rel_bias, w_pool=m_w_pool, pool_scale=m_pool_scale, g_ffn_norm=m_g_ffn_norm, w_gate=m_w_gate,
                w_up=m_w_up, w_down=m_w_down, g_ple_norm=m_g_ple_norm, w_ple_gate=m_w_ple_gate, w_ple_proj=m_w_ple_proj)
    v_in = dict(w_in=v_w_in, w_out=v_w_out, g_attn_norm=v_g_attn_norm, g_q=v_g_q, g_k=v_g_k, attn_sinks=v_attn_sinks,
                rel_bias=v_rel_bias, w_pool=v_w_pool, pool_scale=v_pool_scale, g_ffn_norm=v_g_ffn_norm, w_gate=v_w_gate,
                w_up=v_w_up, w_down=v_w_down, g_ple_norm=v_g_ple_norm, w_ple_gate=v_w_ple_gate, w_ple_proj=v_w_ple_proj)

    _issued.clear()
    xs = x[0]
    ps = p[0, 0]
    target = loss_target[0]
    wp = w_pool[0]
    gq_t = jnp.tile(g_q, (1, ATTN_WIDTH // HEAD_DIM))
    gk_t = jnp.tile(g_k, (1, KV_WIDTH // HEAD_DIM))

    def to_blocks(k, arr):
        return jnp.swapaxes(arr[0], 0, 1) if BIG_WEIGHTS[k][1] else arr[0]

    def from_blocks(k, arr):
        return (jnp.swapaxes(arr, 0, 1) if BIG_WEIGHTS[k][1] else arr)[None]

    IN, OUT, GATE, UP, DOWN, PG, PP = range(N_BIG)
    full = lambda g: g.reshape(N_DEV * g.shape[1], g.shape[2])
    halves = lambda k, g: g.reshape(N_CHIPS, 2, *BIG_WEIGHTS[k][2:])
    place = jnp.stack([lax.axis_index("x"), lax.axis_index("y"), lax.axis_index("c")]).astype(jnp.int32)

    sh, w_in_g = _cast_and_gather_first([to_blocks(k, weights[name]) for k, (name, _, _, _) in enumerate(BIG_WEIGHTS)])
    w_in_t = full(w_in_g)

    (w_out_g,) = _on_sequencer("gather_out", "sibling+chips", _gather_rider([sh[OUT]]))
    wg_g, wu_g = _on_sequencer("gather_gate_up", "sibling+chips", _gather_rider([sh[GATE], sh[UP]]))
    wd_g, w_pg_g, w_pp_g = _on_sequencer("gather_down_ple", "sibling+chips", _gather_rider([sh[DOWN], sh[PG], sh[PP]]))
    tab = _bias_table(rel_bias.T)
    (zqk, qn, kn, v, u), _ = _in_proj(xs, g_attn_norm, w_in_t, gq_t, gk_t)
    (a,), _ = _attn_fwd(qn, kn, v, tab, attn_sinks)
    w_out_f = full(w_out_g)
    (h1, hn2, m_out), _ = _mix_out(u, a, xs, w_out_f, wp, pool_scale, g_ffn_norm)
    wg_t, wu_t = full(wg_g), full(wu_g)
    (gt, up), _ = _ffn_up(hn2, wg_t, wu_t)
    w_down_f = full(wd_g)

    sums, landed = [None] * N_BIG, [None] * N_BIG

    def to_sibling(name, ks, grads):
        return _on_sequencer(name, "sibling", _sibling_rider([halves(k, g) for k, g in zip(ks, grads)]))

    def chip_sum(ks, grads, from_sibling):
        for k, s in zip(ks, _chip_sum(ks, place, [halves(k, g) for k, g in zip(ks, grads)], from_sibling)):
            sums[k] = s

    def to_chips(name, ks, small=None):
        got = _on_sequencer(name, "chips" if small is None else "all", _chips_rider([sums[k][1] for k in ks], small))
        for k, g in zip(ks, got):
            landed[k] = g
        return got[len(ks):]

    (loss_part, dh2, d_wpg, d_wpp, d_g_ple), _ = _ffn_down_ple(
        gt, up, h1, w_down_f, ps, target, g_ple_norm, full(w_pg_g), w_pp_g)
    sib_ple = to_sibling("sibling_ple", (PG, PP), (d_wpg, d_wpp))
    (dgt, dup, dh1, dh1b, d_g_ffn, d_wd), _ = _ffn_bwd_act(dh2, h1, gt, up, g_ffn_norm, wg_t, wu_t, w_down_f)
    sib_down = to_sibling("sibling_down", (DOWN,), (d_wd,))
    chip_sum((PG, PP), (d_wpg, d_wpp), sib_ple)
    to_chips("chips_ple", (PG, PP))
    (d_wo,), _ = _out_w_bwd(a, m_out, dh1b)
    sib_out = to_sibling("sibling_out", (OUT,), (d_wo,))
    (d_wg_t, d_wu_t), _ = _ffn_bwd_w(dgt, dup, hn2)
    sib_gate_up = to_sibling("sibling_gate_up", (GATE, UP), (d_wg_t, d_wu_t))
    _complete_before_next([landed[PG], landed[PP]])
    chip_sum((DOWN, OUT), (d_wd, d_wo), sib_down + sib_out)
    to_chips("chips_down_out", (DOWN, OUT))
    (da, du, d_wpool, d_scale), _ = _mix_bwd(dh1b, u, w_out_f, wp, pool_scale)
    chip_sum((GATE, UP), (d_wg_t, d_wu_t), sib_gate_up)
    early, early_at = _pack_small([d_wpool, d_scale, d_g_ffn, d_g_ple, loss_part[:, :1]])
    (early_all,) = to_chips("chips_gate_up", (GATE, UP), early)
    (dqn, dkn, dv, dl_acc, d_sinks), _ = _attn_bwd(qn, kn, v, a, da, tab, attn_sinks)
    _complete_before_next([landed[DOWN], landed[OUT]])
    (grad_x, d_win_t, d_g_attn, d_gq, d_gk), _ = _in_proj_bwd(dqn, dkn, dv, du, zqk, xs, dh1, g_attn_norm, gq_t, gk_t, w_in_t)
    sib_in = to_sibling("sibling_in", (IN,), (d_win_t,))
    _complete_before_next([landed[GATE], landed[UP], early_all])
    (d_rel_t,), _ = _bias_table_bwd(dl_acc)
    chip_sum((IN,), (d_win_t,), sib_in)
    late, late_at = _pack_small([d_g_attn, d_gq[:, :HEAD_DIM], d_gk[:, :HEAD_DIM], d_sinks[:, 0], d_rel_t[:, :N_BUCKETS]])
    (late_all,) = to_chips("chips_in", (IN,), late)

    out = {"grad": {}, "delta": {}, "new_m": {}, "new_v": {}}
    for ks in ((PG, PP, OUT), (DOWN, GATE, UP), (IN,)):
        names = [BIG_WEIGHTS[k][0] for k in ks]
        results = _adamw_big(ks, [(sums[k][0], landed[k], to_blocks(k, weights[n]), to_blocks(k, m_in[n]),
                                   to_blocks(k, v_in[n])) for k, n in zip(ks, names)])
        for k, name, res in zip(ks, names, results):
            for kind, r in zip(("grad", "delta", "new_m", "new_v"), res):
                out[kind][name] = from_blocks(k, r)
    early_sum, late_sum = _sum_small([early_all, late_all])

    def unpack(packed, at, shape):
        n = math.prod(shape)
        return packed[at:at + -(-n // SMALL_LANES)].reshape(-1)[:n].reshape(shape)

    small_grads = dict(
        w_pool=unpack(early_sum, early_at[0], w_pool.shape), pool_scale=unpack(early_sum, early_at[1], pool_scale.shape),
        g_ffn_norm=unpack(early_sum, early_at[2], g_ffn_norm.shape), g_ple_norm=unpack(early_sum, early_at[3], g_ple_norm.shape),
        g_attn_norm=unpack(late_sum, late_at[0], g_attn_norm.shape), g_q=unpack(late_sum, late_at[1], g_q.shape),
        g_k=unpack(late_sum, late_at[2], g_k.shape), attn_sinks=unpack(late_sum, late_at[3], attn_sinks.shape),
        rel_bias=unpack(late_sum, late_at[4], rel_bias.T.shape))
    loss = early_sum[early_at[4], 0]
    flip = lambda name, arr: arr.T if name == "rel_bias" else arr
    updates = _adamw_small([small_grads[n] for n in SMALL_NAMES], [flip(n, weights[n]) for n in SMALL_NAMES],
                           [flip(n, m_in[n]) for n in SMALL_NAMES], [flip(n, v_in[n]) for n in SMALL_NAMES])
    n_small = len(SMALL_NAMES)
    for i, name in enumerate(SMALL_NAMES):
        out["grad"][name] = flip(name, small_grads[name])
        out["delta"][name] = flip(name, updates[i])
        out["new_m"][name] = flip(name, updates[n_small + i])
        out["new_v"][name] = flip(name, updates[2 * n_small + i])

    _issued.clear()
    order = ("w_in", "w_out", "g_attn_norm", "g_q", "g_k", "attn_sinks", "rel_bias", "w_pool", "pool_scale",
             "g_ffn_norm", "w_gate", "w_up", "w_down", "g_ple_norm", "w_ple_gate", "w_ple_proj")
    return (loss, grad_x[None], *[out["grad"][n] for n in order], *[out["delta"][n] for n in order],
            *[out["new_m"][n] for n in order], *[out["new_v"][n] for n in order])
```

```python
import functools
import math

import jax
import jax.numpy as jnp
import numpy as np
from jax import lax
from jax.experimental import pallas as pl
from jax.experimental.pallas import tpu as pltpu
from jax.experimental.pallas import tpu_sc as plsc

F32 = jnp.float32
BF16 = jnp.bfloat16
MESH = pl.DeviceIdType.MESH

D_MODEL = 1024
HEAD_DIM = 64
ATTN_WIDTH = 512
KV_WIDTH = 128
POOL_WIDTH = 512
POOL_SIZES = (2, 4, 8, 16)
POOL_GROUP = 128
POOL_HALO = 16
IN_WIDTH = 1280
D_FF = 2816
PLE_DIM = 256
BLOCK = 128
N_BUCKETS = 32
MAX_DISTANCE = 128
EPS = 1e-6
N_DEV = 8
N_CHIPS = 4

ADAM_LR = 0.001
ADAM_B1 = 0.9
ADAM_B2 = 0.999
ADAM_EPS = 1e-08
ADAM_WD = 0.01
ADAM_STEP = 10

TOKEN_TILE = 512
FFN_BWD_TILE = 256
FF_CHUNK = 256
ATTN_STEP_BLOCKS = 4
GATE_ROWS_EARLY = 96
UP_ROWS_EARLY = 64
HEADS_A = (0, 2, 5, 7)
HEADS_B = (1, 3, 4, 6)
SMALL_LANES = 128


def _nn(a, b):
    return jnp.dot(a, b, preferred_element_type=F32)


def _nt(a, b):
    return lax.dot_general(a, b, (((1,), (1,)), ((), ())), preferred_element_type=F32)


def _tn(a, b):
    return lax.dot_general(a, b, (((0,), (0,)), ((), ())), preferred_element_type=F32)


def _resident(shape):
    nd = len(shape)
    return pl.BlockSpec(shape, lambda i, _nd=nd: (0,) * _nd, pipeline_mode=pl.Buffered(1))


def _rows(tile, width):
    return pl.BlockSpec((tile, width), lambda i: (i, 0))


def _acc(shape):
    nd = len(shape)
    return pl.BlockSpec(shape, lambda i, _nd=nd: (0,) * _nd)


def _head_mean_matrix(width):
    idx = np.arange(width) // HEAD_DIM
    return jnp.asarray((idx[:, None] == idx[None, :]).astype(np.float32) / HEAD_DIM, dtype=BF16)


def _seg_mean(v, bmat):
    hi = v.astype(BF16)
    lo = (v - hi.astype(F32)).astype(BF16)
    return _nn(hi, bmat) + _nn(lo, bmat)


def _rms(x):
    return lax.rsqrt(jnp.mean(x * x, axis=-1, keepdims=True) + EPS)


def _rms_bwd(d_y, x, r, g):
    gy = d_y * g
    d_x = r * gy - x * (r * r * r) * jnp.mean(gy * x, axis=-1, keepdims=True)
    d_g = jnp.sum(d_y * (x * r), axis=0, keepdims=True)
    return d_x, d_g


def _lane_lo(shape):
    return lax.broadcasted_iota(jnp.int32, shape, 1) < HEAD_DIM


class _Rider:
    def __init__(self, inputs, out_shapes, sems, begin, end, middle=None, aliases=None):
        self.inputs, self.out_shapes, self.sems = list(inputs), list(out_shapes), list(sems)
        self.begin, self.middle, self.end = begin, middle, end
        self.aliases = dict(aliases or {})


_issued = []


def _after_last(args, in_specs):
    extra = list(_issued)
    return list(args) + extra, list(in_specs) + [pl.BlockSpec(memory_space=pl.ANY)] * len(extra), len(extra)


def _mark_issued(out):
    _issued[:] = [out]


def _complete_before_next(arrays):
    _issued.extend(arrays)


def _call(body, args, *, name, grid, in_specs, out_specs, out_shape, scratch_shapes=(), rider=None):
    in_specs, out_specs, out_shape, scratch_shapes = list(in_specs), list(out_specs), list(out_shape), list(scratch_shapes)
    if rider is None:
        n_args = len(args)
        args, in_specs, _ = _after_last(args, in_specs)

        def ordered(*refs):
            body(*refs[:n_args], *refs[len(args):])

        outs = pl.pallas_call(ordered, name=name, grid=grid, in_specs=in_specs, out_specs=out_specs, out_shape=out_shape,
                              scratch_shapes=scratch_shapes)(*args)
        _mark_issued(outs[0])
        return list(outs), []
    n_in, n_out, n_scr = len(in_specs), len(out_shape), len(scratch_shapes)
    r_in, r_out = len(rider.inputs), len(rider.out_shapes)
    n_steps = grid[0]

    def hosted(*refs):
        ins, refs = refs[:n_in], refs[n_in:]
        r_ins, refs = refs[:r_in], refs[r_in:]
        outs, refs = refs[:n_out], refs[n_out:]
        r_outs, refs = refs[:r_out], refs[r_out:]
        scratch, r_sems = refs[:n_scr], refs[n_scr:]
        step = pl.program_id(0)

        @pl.when(step == 0)
        def _():
            rider.begin(r_ins, r_outs, r_sems)

        if rider.middle is not None:
            @pl.when(step == n_steps - 1)
            def _():
                rider.middle(r_ins, r_outs, r_sems)

        body(*ins, *outs, *scratch)

        @pl.when(step == n_steps - 1)
        def _():
            rider.end(r_ins, r_outs, r_sems)

    any_spec = pl.BlockSpec(memory_space=pl.ANY)
    outs = pl.pallas_call(
        hosted, name=name, grid=grid,
        in_specs=in_specs + [any_spec] * r_in,
        out_specs=out_specs + [any_spec] * r_out,
        out_shape=out_shape + rider.out_shapes,
        scratch_shapes=scratch_shapes + rider.sems,
        input_output_aliases={n_in + i: n_out + o for i, o in rider.aliases.items()},
    )(*args, *rider.inputs)
    return list(outs[:n_out]), list(outs[n_out:])


def _in_proj(x, g_attn, w_in_t, gq_t, gk_t, rider=None):
    s = x.shape[0]
    ts = min(TOKEN_TILE, s)

    def body(x_ref, g_ref, w_ref, gq_ref, gk_ref, bq_ref, bk_ref, zqk_ref, qn_ref, kn_ref, v_ref, u_ref):
        xf = x_ref[...]
        hn = ((xf * _rms(xf)) * g_ref[...]).astype(BF16)
        z = _nt(hn, w_ref[...])
        q = z[:, :ATTN_WIDTH]
        k = z[:, ATTN_WIDTH:ATTN_WIDTH + KV_WIDTH]
        zqk_ref[...] = z[:, :ATTN_WIDTH + KV_WIDTH]
        rq = lax.rsqrt(_seg_mean(q * q, bq_ref[...]) + EPS)
        qn_ref[...] = ((q * rq) * gq_ref[...]).astype(BF16)
        rk = lax.rsqrt(_seg_mean(k * k, bk_ref[...]) + EPS)
        kn_ref[...] = ((k * rk) * gk_ref[...]).astype(BF16)
        v_ref[...] = z[:, ATTN_WIDTH + KV_WIDTH:ATTN_WIDTH + 2 * KV_WIDTH].astype(BF16)
        u_ref[...] = z[:, ATTN_WIDTH + 2 * KV_WIDTH:]

    return _call(
        body,
        (x, g_attn, w_in_t, gq_t, gk_t, _head_mean_matrix(ATTN_WIDTH), _head_mean_matrix(KV_WIDTH)),
        name="in_proj",
        grid=(s // ts,),
        in_specs=[
            _rows(ts, D_MODEL),
            _resident((1, D_MODEL)),
            _resident((IN_WIDTH, D_MODEL)),
            _resident((1, ATTN_WIDTH)),
            _resident((1, KV_WIDTH)),
            _resident((ATTN_WIDTH, ATTN_WIDTH)),
            _resident((KV_WIDTH, KV_WIDTH)),
        ],
        out_specs=[
            _rows(ts, ATTN_WIDTH + KV_WIDTH),
            _rows(ts, ATTN_WIDTH),
            _rows(ts, KV_WIDTH),
            _rows(ts, KV_WIDTH),
            _rows(ts, POOL_WIDTH),
        ],
        out_shape=[
            jax.ShapeDtypeStruct((s, ATTN_WIDTH + KV_WIDTH), F32),
            jax.ShapeDtypeStruct((s, ATTN_WIDTH), BF16),
            jax.ShapeDtypeStruct((s, KV_WIDTH), BF16),
            jax.ShapeDtypeStruct((s, KV_WIDTH), BF16),
            jax.ShapeDtypeStruct((s, POOL_WIDTH), F32),
        ],
        rider=rider,
    )


def _bucket_ranges():
    n = np.arange(MAX_DISTANCE)
    max_exact = N_BUCKETS // 2
    nf = np.maximum(n, 1).astype(np.float64)
    large = max_exact + (np.log(nf / max_exact) / math.log(MAX_DISTANCE / max_exact) * (N_BUCKETS - max_exact)).astype(np.int64)
    bucket = np.where(n < max_exact, n, np.minimum(large, N_BUCKETS - 1))
    out = []
    for b in range(N_BUCKETS):
        idx = np.nonzero(bucket == b)[0]
        out.append((int(idx.min()), int(idx.max()) + 1))
    return out


def _band_distance():
    i = lax.broadcasted_iota(jnp.int32, (BLOCK, 2 * BLOCK), 0)
    j = lax.broadcasted_iota(jnp.int32, (BLOCK, 2 * BLOCK), 1)
    return BLOCK + i - j


def _bias_table(rel_bias_t):
    ranges = _bucket_ranges()

    def body(rb_ref, tab_ref):
        d = _band_distance()
        for half, heads in enumerate((HEADS_A, HEADS_B)):
            for slot, h in enumerate(heads):
                t = jnp.full((BLOCK, 2 * BLOCK), -jnp.inf, F32)
                for b, (lo, hi) in enumerate(ranges):
                    t = jnp.where((d >= lo) & (d < hi), rb_ref[h, b], t)
                tab_ref[half, slot * BLOCK:(slot + 1) * BLOCK, :] = t

    return pl.pallas_call(
        body,
        name="bias_table",
        in_specs=[pl.BlockSpec(memory_space=pltpu.SMEM)],
        out_shape=jax.ShapeDtypeStruct((2, 4 * BLOCK, 2 * BLOCK), F32),
    )(rel_bias_t)


def _bias_table_bwd(dl_acc, rider=None):
    ranges = _bucket_ranges()
    n_heads = len(HEADS_A) + len(HEADS_B)

    def body(dl_ref, out_ref):
        d = _band_distance()
        row = lax.broadcasted_iota(jnp.int32, (n_heads, SMALL_LANES), 0)
        lane = lax.broadcasted_iota(jnp.int32, (n_heads, SMALL_LANES), 1)
        out = jnp.zeros((n_heads, SMALL_LANES), F32)
        for b, (lo, hi) in enumerate(ranges):
            in_bucket = (d >= lo) & (d < hi)
            for half, heads in enumerate((HEADS_A, HEADS_B)):
                for slot, h in enumerate(heads):
                    g = dl_ref[half, slot * BLOCK:(slot + 1) * BLOCK, :]
                    part = jnp.sum(jnp.where(in_bucket, g, 0.0), axis=0, keepdims=True)
                    tot = jnp.sum(part, axis=1, keepdims=True)
                    out = jnp.where((row == h) & (lane == b), tot, out)
        out_ref[...] = out

    return _call(
        body,
        (dl_acc,),
        name="bias_table_bwd",
        grid=(1,),
        in_specs=[_acc((2, 4 * BLOCK, 2 * BLOCK))],
        out_specs=[_acc((n_heads, SMALL_LANES))],
        out_shape=[jax.ShapeDtypeStruct((n_heads, SMALL_LANES), F32)],
        rider=rider,
    )


def _stack_heads(pairs, lo_mask):
    zero = jnp.zeros_like(pairs[0])
    lo = [jnp.where(lo_mask, t, zero) for t in pairs]
    hi = [jnp.where(lo_mask, zero, t) for t in pairs]
    return (jnp.concatenate([lo[0], lo[1], hi[2], hi[3]], axis=0),
            jnp.concatenate([hi[0], hi[1], lo[2], lo[3]], axis=0))


def _unstack_heads(out_a, out_b, lo_mask):
    t = lambda x, r: x[r * BLOCK:(r + 1) * BLOCK, :]
    return [
        jnp.where(lo_mask, t(out_a, 0), t(out_b, 0)),
        jnp.where(lo_mask, t(out_a, 1), t(out_b, 1)),
        jnp.where(lo_mask, t(out_b, 2), t(out_a, 2)),
        jnp.where(lo_mask, t(out_b, 3), t(out_a, 3)),
    ]


def _sink_column(sink_ref, heads):
    row = lax.broadcasted_iota(jnp.int32, (4 * BLOCK, 1), 0)
    col = jnp.full((4 * BLOCK, 1), sink_ref[0, heads[3]], F32)
    for slot in (2, 1, 0):
        col = jnp.where(row < (slot + 1) * BLOCK, sink_ref[0, heads[slot]], col)
    return col


def _band_probs(q_stack, keys, tab, sink, first_block):
    s = _nt(q_stack, keys) * (HEAD_DIM ** -0.5) + tab
    if first_block is not None:
        col = lax.broadcasted_iota(jnp.int32, s.shape, 1)
        s = jnp.where(jnp.logical_and(first_block, col < BLOCK), -jnp.inf, s)
    m = jnp.maximum(jnp.max(s, axis=-1, keepdims=True), sink)
    e = jnp.exp(s - m)
    e_sink = jnp.exp(sink - m)
    den = jnp.sum(e, axis=-1, keepdims=True) + e_sink
    return e / den, e_sink / den


def _attn_specs(n_groups):
    group = lambda n: (jnp.minimum(n, n_groups - 1), 0)
    prev = lambda n: (jnp.maximum(jnp.minimum(n, n_groups - 1) * ATTN_STEP_BLOCKS - 1, 0), 0)
    return group, prev


def _band(prev_ref, group_ref, b):
    rows = lambda i: group_ref[i * BLOCK:(i + 1) * BLOCK, :]
    band = jnp.concatenate([prev_ref[...] if b == 0 else rows(b - 1), rows(b)], axis=0)
    return band, pltpu.roll(band, HEAD_DIM, 1)


def _attn_fwd(qn, kn, v, tab, sinks, rider=None):
    s = qn.shape[0]
    n_groups = s // (ATTN_STEP_BLOCKS * BLOCK)
    group, prev = _attn_specs(n_groups)
    rows = ATTN_STEP_BLOCKS * BLOCK

    def body(sink_ref, q_ref, kc_ref, kp_ref, vc_ref, vp_ref, tab_ref, o_ref):
        first = pl.program_id(0) == 0
        lo_mask = _lane_lo((BLOCK, BLOCK))
        for b in range(ATTN_STEP_BLOCKS):
            at = slice(b * BLOCK, (b + 1) * BLOCK)
            kk, kk_sw = _band(kp_ref, kc_ref, b)
            vv, vv_sw = _band(vp_ref, vc_ref, b)
            q_a, q_b = _stack_heads([q_ref[at, p * BLOCK:(p + 1) * BLOCK] for p in range(4)], lo_mask)
            no_prev = first if b == 0 else None
            p_a, _ = _band_probs(q_a, kk, tab_ref[0], _sink_column(sink_ref, HEADS_A), no_prev)
            p_b, _ = _band_probs(q_b, kk_sw, tab_ref[1], _sink_column(sink_ref, HEADS_B), no_prev)
            out = _unstack_heads(_nn(p_a.astype(BF16), vv), _nn(p_b.astype(BF16), vv_sw), lo_mask)
            for p in range(4):
                o_ref[at, p * BLOCK:(p + 1) * BLOCK] = out[p].astype(BF16)

    return _call(
        body,
        (sinks, qn, kn, kn, v, v, tab),
        name="attn_fwd",
        grid=(n_groups,),
        in_specs=[
            pl.BlockSpec(memory_space=pltpu.SMEM),
            pl.BlockSpec((rows, ATTN_WIDTH), group),
            pl.BlockSpec((rows, KV_WIDTH), group),
            pl.BlockSpec((BLOCK, KV_WIDTH), prev),
            pl.BlockSpec((rows, KV_WIDTH), group),
            pl.BlockSpec((BLOCK, KV_WIDTH), prev),
            _resident((2, 4 * BLOCK, 2 * BLOCK)),
        ],
        out_specs=[pl.BlockSpec((rows, ATTN_WIDTH), group)],
        out_shape=[jax.ShapeDtypeStruct((s, ATTN_WIDTH), BF16)],
        rider=rider,
    )


def _pooled(u_tile, u_halo, tile_index, tile_rows):
    halo = jnp.where(tile_index > 0, u_halo, 0.0)
    ext = jnp.concatenate([halo, u_tile], axis=0)
    sums = []
    acc = ext
    for shift in (1, 2, 4, 8):
        acc = acc + pltpu.roll(acc, shift, 0)
        sums.append(acc)
    t = tile_index * tile_rows + lax.broadcasted_iota(jnp.int32, (tile_rows, 1), 0)
    out = []
    for g, w in enumerate(POOL_SIZES):
        lanes = slice(g * POOL_GROUP, (g + 1) * POOL_GROUP)
        cnt = jnp.minimum(t + 1, w).astype(F32)
        out.append(sums[g][POOL_HALO:, lanes] / cnt - u_tile[:, lanes])
    return out


def _halo_before(tile):
    return lambda i: (jnp.maximum(i * (tile // POOL_HALO) - 1, 0), 0)


def _mix_out(u, a, x, w_out, w_pool, pool_scale, g_ffn, rider=None):
    s = x.shape[0]
    ts = min(TOKEN_TILE, s)

    def body(u_ref, uh_ref, a_ref, x_ref, wo_ref, wp_ref, sc_ref, g_ref, h1_ref, hn_ref, m_ref):
        i = pl.program_id(0)
        pooled = _pooled(u_ref[...], uh_ref[...], i, ts)
        for g in range(len(POOL_SIZES)):
            lanes = slice(g * POOL_GROUP, (g + 1) * POOL_GROUP)
            y = _nn(pooled[g].astype(BF16), wp_ref[g].astype(BF16))
            m_ref[:, lanes] = (y * sc_ref[:, lanes]).astype(BF16)
        h1 = x_ref[...] + _nn(a_ref[...], wo_ref[:ATTN_WIDTH, :]) + _nn(m_ref[...], wo_ref[ATTN_WIDTH:, :])
        h1_ref[...] = h1
        hn_ref[...] = ((h1 * _rms(h1)) * g_ref[...]).astype(BF16)

    return _call(
        body,
        (u, u, a, x, w_out, w_pool, pool_scale, g_ffn),
        name="mix_out",
        grid=(s // ts,),
        in_specs=[
            _rows(ts, POOL_WIDTH),
            pl.BlockSpec((POOL_HALO, POOL_WIDTH), _halo_before(ts)),
            _rows(ts, ATTN_WIDTH),
            _rows(ts, D_MODEL),
            _resident((D_MODEL, D_MODEL)),
            _resident((len(POOL_SIZES), POOL_GROUP, POOL_GROUP)),
            _resident((1, POOL_WIDTH)),
            _resident((1, D_MODEL)),
        ],
        out_specs=[_rows(ts, D_MODEL), _rows(ts, D_MODEL), _rows(ts, POOL_WIDTH)],
        out_shape=[
            jax.ShapeDtypeStruct((s, D_MODEL), F32),
            jax.ShapeDtypeStruct((s, D_MODEL), BF16),
            jax.ShapeDtypeStruct((s, POOL_WIDTH), BF16),
        ],
        rider=rider,
    )


def _ffn_up(hn2, wg_t, wu_t, rider=None):
    s = hn2.shape[0]
    ts = min(TOKEN_TILE, s)

    def body(hn_ref, wg_ref, wu_ref, gt_ref, up_ref):
        hn = hn_ref[...]
        for c in range(D_FF // FF_CHUNK):
            cols = slice(c * FF_CHUNK, (c + 1) * FF_CHUNK)
            gt_ref[:, cols] = _nt(hn, wg_ref[cols, :]).astype(BF16)
            up_ref[:, cols] = _nt(hn, wu_ref[cols, :]).astype(BF16)

    return _call(
        body,
        (hn2, wg_t, wu_t),
        name="ffn_up",
        grid=(s // ts,),
        in_specs=[_rows(ts, D_MODEL), _resident((D_FF, D_MODEL)), _resident((D_FF, D_MODEL))],
        out_specs=[_rows(ts, D_FF), _rows(ts, D_FF)],
        out_shape=[jax.ShapeDtypeStruct((s, D_FF), BF16), jax.ShapeDtypeStruct((s, D_FF), BF16)],
        rider=rider,
    )


def _silu_mul(gt, up):
    return (gt * jax.nn.sigmoid(gt)) * up


def _ffn_down_ple(gt, up, h1, w_down, p, target, g_ple, w_pg, w_pp, rider=None):
    s = h1.shape[0]
    ts = min(TOKEN_TILE, s)
    blk = D_MODEL // N_DEV

    def body(gt_ref, up_ref, h1_ref, wd_ref, p_ref, t_ref, g_ref, wpg_ref, wpp_ref,
             loss_ref, dh_ref, dwpg_ref, dwpp_ref, dg_ref, act_ref, pp_ref):
        @pl.when(pl.program_id(0) == 0)
        def _():
            loss_ref[...] = jnp.zeros_like(loss_ref)
            dwpg_ref[...] = jnp.zeros_like(dwpg_ref)
            dwpp_ref[...] = jnp.zeros_like(dwpp_ref)
            dg_ref[...] = jnp.zeros_like(dg_ref)

        h2v = h1_ref[...]
        for c in range(D_FF // FF_CHUNK):
            cols = slice(c * FF_CHUNK, (c + 1) * FF_CHUNK)
            act = _silu_mul(gt_ref[:, cols].astype(F32), up_ref[:, cols].astype(F32)).astype(BF16)
            h2v = _nn(act, wd_ref[cols, :]) + h2v
        r = _rms(h2v)
        hn = ((h2v * r) * g_ref[...]).astype(BF16)
        gate = jax.nn.sigmoid(_nn(hn, wpg_ref[...]))
        pb = p_ref[...].astype(BF16)
        for j in range(N_DEV):
            pp_ref[:, j * blk:(j + 1) * blk] = _nn(pb, wpp_ref[j])
        pp = pp_ref[...]
        diff = (h2v + gate * pp) - t_ref[...]
        loss_ref[...] += jnp.sum(jnp.sum(diff * diff, axis=0, keepdims=True), axis=1, keepdims=True) * (0.5 / D_MODEL)
        dy = diff * (1.0 / D_MODEL)
        d_pp = (dy * gate).astype(BF16)
        d_pre = ((dy * pp) * (gate * (1.0 - gate))).astype(BF16)
        for j in range(N_DEV):
            dwpp_ref[j] += _tn(pb, d_pp[:, j * blk:(j + 1) * blk])
        dwpg_ref[...] += _tn(hn, d_pre)
        d_x, d_g = _rms_bwd(_nt(d_pre, wpg_ref[...]), h2v, r, g_ref[...])
        dg_ref[...] += d_g
        dh_ref[...] = dy + d_x

    return _call(
        body,
        (gt, up, h1, w_down, p, target, g_ple, w_pg, w_pp),
        name="ffn_down_ple",
        grid=(s // ts,),
        in_specs=[
            _rows(ts, D_FF),
            _rows(ts, D_FF),
            _rows(ts, D_MODEL),
            _resident((D_FF, D_MODEL)),
            _rows(ts, PLE_DIM),
            _rows(ts, D_MODEL),
            _resident((1, D_MODEL)),
            _resident((D_MODEL, D_MODEL)),
            _resident((N_DEV, PLE_DIM, blk)),
        ],
        out_specs=[
            _acc((1, SMALL_LANES)),
            _rows(ts, D_MODEL),
            _acc((D_MODEL, D_MODEL)),
            _acc((N_DEV, PLE_DIM, blk)),
            _acc((1, D_MODEL)),
        ],
        out_shape=[
            jax.ShapeDtypeStruct((1, SMALL_LANES), F32),
            jax.ShapeDtypeStruct((s, D_MODEL), F32),
            jax.ShapeDtypeStruct((D_MODEL, D_MODEL), F32),
            jax.ShapeDtypeStruct((N_DEV, PLE_DIM, blk), F32),
            jax.ShapeDtypeStruct((1, D_MODEL), F32),
        ],
        scratch_shapes=[pltpu.VMEM((ts, D_FF), BF16), pltpu.VMEM((ts, D_MODEL), F32)],
        rider=rider,
    )


def _ffn_bwd_act(dh2, h1, gt, up, g_ffn, wg_t, wu_t, w_down, rider=None):
    s = h1.shape[0]
    ts = min(FFN_BWD_TILE, s)

    def body(dh_ref, h1_ref, gt_ref, up_ref, g_ref, wg_ref, wu_ref, wd_ref,
             dgt_ref, dup_ref, dh1_ref, dh1b_ref, dg_ref, dwd_ref, act_ref):
        @pl.when(pl.program_id(0) == 0)
        def _():
            dg_ref[...] = jnp.zeros_like(dg_ref)
            dwd_ref[...] = jnp.zeros_like(dwd_ref)

        dhb = dh_ref[...].astype(BF16)
        d_hn = jnp.zeros((ts, D_MODEL), F32)
        for c in range(D_FF // FF_CHUNK):
            cols = slice(c * FF_CHUNK, (c + 1) * FF_CHUNK)
            d_act = _nt(dhb, wd_ref[cols, :])
            gtv = gt_ref[:, cols].astype(F32)
            upv = up_ref[:, cols].astype(F32)
            sg = jax.nn.sigmoid(gtv)
            silu = gtv * sg
            act_ref[:, cols] = (silu * upv).astype(BF16)
            d_up = (d_act * silu).astype(BF16)
            d_gt = ((d_act * upv) * (sg * (1.0 + gtv * (1.0 - sg)))).astype(BF16)
            dup_ref[:, cols] = d_up
            dgt_ref[:, cols] = d_gt
            d_hn = (_nn(d_gt, wg_ref[cols, :]) + _nn(d_up, wu_ref[cols, :])) + d_hn
        dwd_ref[...] += _tn(act_ref[...], dhb)
        h1v = h1_ref[...]
        d_x, d_g = _rms_bwd(d_hn, h1v, _rms(h1v), g_ref[...])
        dg_ref[...] += d_g
        dh1 = dh_ref[...] + d_x
        dh1_ref[...] = dh1
        dh1b_ref[...] = dh1.astype(BF16)

    return _call(
        body,
        (dh2, h1, gt, up, g_ffn, wg_t, wu_t, w_down),
        name="ffn_bwd_act",
        grid=(s // ts,),
        in_specs=[
            _rows(ts, D_MODEL),
            _rows(ts, D_MODEL),
            _rows(ts, D_FF),
            _rows(ts, D_FF),
            _resident((1, D_MODEL)),
            _resident((D_FF, D_MODEL)),
            _resident((D_FF, D_MODEL)),
            _resident((D_FF, D_MODEL)),
        ],
        out_specs=[
            _rows(ts, D_FF), _rows(ts, D_FF),
            _rows(ts, D_MODEL), _rows(ts, D_MODEL), _acc((1, D_MODEL)), _acc((D_FF, D_MODEL)),
        ],
        out_shape=[
            jax.ShapeDtypeStruct((s, D_FF), BF16),
            jax.ShapeDtypeStruct((s, D_FF), BF16),
            jax.ShapeDtypeStruct((s, D_MODEL), F32),
            jax.ShapeDtypeStruct((s, D_MODEL), BF16),
            jax.ShapeDtypeStruct((1, D_MODEL), F32),
            jax.ShapeDtypeStruct((D_FF, D_MODEL), F32),
        ],
        scratch_shapes=[pltpu.VMEM((ts, D_FF), BF16)],
        rider=rider,
    )


def _ffn_bwd_w(dgt, dup, hn2, rider=None):
    s = hn2.shape[0]
    slab = pl.BlockSpec((s, FF_CHUNK), lambda i: (0, i))

    def body(dgt_ref, dup_ref, hn_ref, dwg_ref, dwu_ref):
        hn = hn_ref[...]
        dwg_ref[...] = _tn(dgt_ref[...], hn)
        dwu_ref[...] = _tn(dup_ref[...], hn)

    return _call(
        body,
        (dgt, dup, hn2),
        name="ffn_bwd_w",
        grid=(D_FF // FF_CHUNK,),
        in_specs=[slab, slab, _resident((s, D_MODEL))],
        out_specs=[_rows(FF_CHUNK, D_MODEL)] * 2,
        out_shape=[jax.ShapeDtypeStruct((D_FF, D_MODEL), F32)] * 2,
        rider=rider,
    )


def _mix_bwd(dh1b, u, w_out, w_pool, pool_scale, rider=None):
    s = u.shape[0]
    ts = min(TOKEN_TILE, s)
    nt = s // ts
    halo_after = lambda i: (jnp.minimum((i + 1) * (ts // POOL_HALO), s // POOL_HALO - 1), 0)
    n_groups = len(POOL_SIZES)

    def body(dh_ref, dhn_ref, u_ref, uh_ref, wo_ref, wp_ref, sc_ref, da_ref, du_ref, dwp_ref, dsc_ref):
        i = pl.program_id(0)

        @pl.when(i == 0)
        def _():
            dwp_ref[...] = jnp.zeros_like(dwp_ref)
            dsc_ref[...] = jnp.zeros_like(dsc_ref)

        dh = dh_ref[...]
        da_ref[...] = _nt(dh, wo_ref[:ATTN_WIDTH, :])
        dh_next = jnp.where(i < nt - 1, dhn_ref[...], jnp.zeros_like(dhn_ref))
        dm_ext = _nt(jnp.concatenate([dh, dh_next], axis=0), wo_ref[ATTN_WIDTH:, :])
        pooled = _pooled(u_ref[...], uh_ref[...], i, ts)
        t_ext = i * ts + lax.broadcasted_iota(jnp.int32, (ts + POOL_HALO, 1), 0)
        for g, w in enumerate(POOL_SIZES):
            lanes = slice(g * POOL_GROUP, (g + 1) * POOL_GROUP)
            wp = wp_ref[g].astype(BF16)
            pg = pooled[g].astype(BF16)
            dm_g = dm_ext[:, lanes]
            dsc_ref[:, lanes] += jnp.sum(dm_g[:ts, :] * _nn(pg, wp), axis=0, keepdims=True)
            dy = (dm_g * sc_ref[:, lanes]).astype(BF16)
            dwp_ref[g] += _tn(pg, dy[:ts, :])
            d_pool = _nt(dy, wp)
            acc = d_pool / jnp.minimum(t_ext + 1, w).astype(F32)
            shift = 1
            while shift < w:
                acc = acc + pltpu.roll(acc, ts + POOL_HALO - shift, 0)
                shift *= 2
            du_ref[:, lanes] = acc[:ts, :] - d_pool[:ts, :]

    return _call(
        body,
        (dh1b, dh1b, u, u, w_out, w_pool, pool_scale),
        name="mix_bwd",
        grid=(nt,),
        in_specs=[
            _rows(ts, D_MODEL),
            pl.BlockSpec((POOL_HALO, D_MODEL), halo_after),
            _rows(ts, POOL_WIDTH),
            pl.BlockSpec((POOL_HALO, POOL_WIDTH), _halo_before(ts)),
            _resident((D_MODEL, D_MODEL)),
            _resident((n_groups, POOL_GROUP, POOL_GROUP)),
            _resident((1, POOL_WIDTH)),
        ],
        out_specs=[
            _rows(ts, ATTN_WIDTH),
            _rows(ts, POOL_WIDTH),
            _acc((n_groups, POOL_GROUP, POOL_GROUP)),
            _acc((1, POOL_WIDTH)),
        ],
        out_shape=[
            jax.ShapeDtypeStruct((s, ATTN_WIDTH), F32),
            jax.ShapeDtypeStruct((s, POOL_WIDTH), F32),
            jax.ShapeDtypeStruct((n_groups, POOL_GROUP, POOL_GROUP), F32),
            jax.ShapeDtypeStruct((1, POOL_WIDTH), F32),
        ],
        rider=rider,
    )


def _out_w_bwd(a, m, dh1b, rider=None):
    s = dh1b.shape[0]

    def body(a_ref, m_ref, dh_ref, dw_ref):
        @pl.when(pl.program_id(0) == 0)
        def _():
            dw_ref[...] = _tn(a_ref[...], dh_ref[...])

        @pl.when(pl.program_id(0) == 1)
        def _():
            dw_ref[...] = _tn(m_ref[...], dh_ref[...])

    return _call(
        body,
        (a, m, dh1b),
        name="out_w_bwd",
        grid=(2,),
        in_specs=[_resident((s, ATTN_WIDTH)), _resident((s, POOL_WIDTH)), _resident((s, D_MODEL))],
        out_specs=[_rows(ATTN_WIDTH, D_MODEL)],
        out_shape=[jax.ShapeDtypeStruct((D_MODEL, D_MODEL), F32)],
        rider=rider,
    )


def _attn_bwd(qn, kn, v, a, da, tab, sinks, rider=None):
    s = qn.shape[0]
    qb = ATTN_STEP_BLOCKS
    rows = qb * BLOCK
    n_groups = s // rows
    group, prev = _attn_specs(n_groups)
    done = lambda n: (jnp.maximum(n - 1, 0), 0)

    def body(sink_ref, q_ref, kc_ref, kp_ref, vc_ref, vp_ref, o_ref, do_ref, tab_ref,
             dq_ref, dk_ref, dv_ref, dl_ref, ds_ref, k_carry, v_carry, sink_acc):
        n = pl.program_id(0)

        @pl.when(n == 0)
        def _():
            dl_ref[...] = jnp.zeros_like(dl_ref)
            k_carry[...] = jnp.zeros_like(k_carry)
            v_carry[...] = jnp.zeros_like(v_carry)
            sink_acc[...] = jnp.zeros_like(sink_acc)

        @pl.when(n < n_groups)
        def _():
            first = n == 0
            lo_mask = _lane_lo((BLOCK, BLOCK))
            dks, dvs = [], []
            for b in range(qb):
                at = slice(b * BLOCK, (b + 1) * BLOCK)
                keys = _band(kp_ref, kc_ref, b)
                vals = _band(vp_ref, vc_ref, b)
                q_st = _stack_heads([q_ref[at, p * BLOCK:(p + 1) * BLOCK] for p in range(4)], lo_mask)
                do_st = _stack_heads([do_ref[at, p * BLOCK:(p + 1) * BLOCK] for p in range(4)], lo_mask)
                o_st = _stack_heads([o_ref[at, p * BLOCK:(p + 1) * BLOCK].astype(F32) for p in range(4)], lo_mask)
                dq_st, dk_parts, dv_parts = [], [], []
                for half, heads in enumerate((HEADS_A, HEADS_B)):
                    probs, p_sink = _band_probs(q_st[half], keys[half], tab_ref[half], _sink_column(sink_ref, heads),
                                                first if b == 0 else None)
                    delta = jnp.sum(do_st[half] * o_st[half], axis=-1, keepdims=True)
                    dob = do_st[half].astype(BF16)
                    dl = probs * (_nt(dob, vals[half]) - delta)
                    dl_ref[half] += dl
                    sink_acc[half] += p_sink * delta
                    dsb = (dl * (HEAD_DIM ** -0.5)).astype(BF16)
                    dq_st.append(_nn(dsb, keys[half]))
                    dk_parts.append(_tn(dsb, q_st[half]))
                    dv_parts.append(_tn(probs.astype(BF16), dob))
                dq = _unstack_heads(dq_st[0], dq_st[1], lo_mask)
                for p in range(4):
                    dq_ref[at, p * BLOCK:(p + 1) * BLOCK] = dq[p]
                dks.append(dk_parts[0] + pltpu.roll(dk_parts[1], HEAD_DIM, 1))
                dvs.append(dv_parts[0] + pltpu.roll(dv_parts[1], HEAD_DIM, 1))
            last = slice((qb - 1) * BLOCK, qb * BLOCK)
            for parts, out_ref, carry in ((dks, dk_ref, k_carry), (dvs, dv_ref, v_carry)):
                out_ref[...] = carry[...]
                out_ref[last, :] += parts[0][:BLOCK, :]
                for b in range(qb):
                    own = parts[b][BLOCK:, :]
                    carry[b * BLOCK:(b + 1) * BLOCK, :] = own + parts[b + 1][:BLOCK, :] if b + 1 < qb else own

        @pl.when(n == n_groups)
        def _():
            dk_ref[...] = k_carry[...]
            dv_ref[...] = v_carry[...]
            for half, heads in enumerate((HEADS_A, HEADS_B)):
                for slot, h in enumerate(heads):
                    tot = jnp.sum(sink_acc[half, slot * BLOCK:(slot + 1) * BLOCK, :], axis=0, keepdims=True)
                    ds_ref[h:h + 1, :] = jnp.broadcast_to(-tot, (1, SMALL_LANES))

    return _call(
        body,
        (sinks, qn, kn, kn, v, v, a, da, tab),
        name="attn_bwd",
        grid=(n_groups + 1,),
        in_specs=[
            pl.BlockSpec(memory_space=pltpu.SMEM),
            pl.BlockSpec((rows, ATTN_WIDTH), group),
            pl.BlockSpec((rows, KV_WIDTH), group),
            pl.BlockSpec((BLOCK, KV_WIDTH), prev),
            pl.BlockSpec((rows, KV_WIDTH), group),
            pl.BlockSpec((BLOCK, KV_WIDTH), prev),
            pl.BlockSpec((rows, ATTN_WIDTH), group),
            pl.BlockSpec((rows, ATTN_WIDTH), group),
            _resident((2, 4 * BLOCK, 2 * BLOCK)),
        ],
        out_specs=[
            pl.BlockSpec((rows, ATTN_WIDTH), group),
            pl.BlockSpec((rows, KV_WIDTH), done),
            pl.BlockSpec((rows, KV_WIDTH), done),
            _acc((2, 4 * BLOCK, 2 * BLOCK)),
            _acc((N_DEV, SMALL_LANES)),
        ],
        out_shape=[
            jax.ShapeDtypeStruct((s, ATTN_WIDTH), F32),
            jax.ShapeDtypeStruct((s, KV_WIDTH), F32),
            jax.ShapeDtypeStruct((s, KV_WIDTH), F32),
            jax.ShapeDtypeStruct((2, 4 * BLOCK, 2 * BLOCK), F32),
            jax.ShapeDtypeStruct((N_DEV, SMALL_LANES), F32),
        ],
        scratch_shapes=[
            pltpu.VMEM((rows, KV_WIDTH), F32),
            pltpu.VMEM((rows, KV_WIDTH), F32),
            pltpu.VMEM((2, 4 * BLOCK, 1), F32),
        ],
        rider=rider,
    )


def _fold_heads(acc):
    t = acc + pltpu.roll(acc, HEAD_DIM, 1)
    out = t[:, :SMALL_LANES]
    for g in range(1, acc.shape[1] // SMALL_LANES):
        out = out + t[:, g * SMALL_LANES:(g + 1) * SMALL_LANES]
    return out


def _in_proj_bwd(dqn, dkn, dv, du, zqk, x, dh1, g_attn, gq_t, gk_t, w_in_t, rider=None):
    s = x.shape[0]
    ts = min(TOKEN_TILE, s)
    nt = s // ts

    def head_norm_bwd(d_n, raw, g_t, bmat):
        r = lax.rsqrt(_seg_mean(raw * raw, bmat) + EPS)
        gy = d_n * g_t
        d_raw = r * gy - raw * (r * r * r) * _seg_mean(gy * raw, bmat)
        return d_raw, jnp.sum(d_n * (raw * r), axis=0, keepdims=True)

    def body(dqn_ref, dkn_ref, dv_ref, du_ref, zqk_ref, x_ref, dh1_ref, g_ref, gq_ref, gk_ref, w_ref, bq_ref, bk_ref,
             gx_ref, dw_ref, dg_ref, dgq_ref, dgk_ref, dz_ref, gq_acc, gk_acc):
        i = pl.program_id(0)

        @pl.when(i == 0)
        def _():
            dw_ref[...] = jnp.zeros_like(dw_ref)
            dg_ref[...] = jnp.zeros_like(dg_ref)
            gq_acc[...] = jnp.zeros_like(gq_acc)
            gk_acc[...] = jnp.zeros_like(gk_acc)

        d_q, d_gq = head_norm_bwd(dqn_ref[...], zqk_ref[:, :ATTN_WIDTH], gq_ref[...], bq_ref[...])
        d_k, d_gk = head_norm_bwd(dkn_ref[...], zqk_ref[:, ATTN_WIDTH:], gk_ref[...], bk_ref[...])
        gq_acc[...] += d_gq
        gk_acc[...] += d_gk
        dz_ref[:, :ATTN_WIDTH] = d_q.astype(BF16)
        dz_ref[:, ATTN_WIDTH:ATTN_WIDTH + KV_WIDTH] = d_k.astype(BF16)
        dz_ref[:, ATTN_WIDTH + KV_WIDTH:ATTN_WIDTH + 2 * KV_WIDTH] = dv_ref[...].astype(BF16)
        dz_ref[:, ATTN_WIDTH + 2 * KV_WIDTH:] = du_ref[...].astype(BF16)
        dz = dz_ref[...]
        xf = x_ref[...]
        r = _rms(xf)
        hn = ((xf * r) * g_ref[...]).astype(BF16)
        dw_ref[...] += _tn(dz, hn)
        d_x, d_g = _rms_bwd(_nn(dz, w_ref[...]), xf, r, g_ref[...])
        dg_ref[...] += d_g
        gx_ref[...] = dh1_ref[...] + d_x

        @pl.when(i == nt - 1)
        def _():
            dgq_ref[...] = _fold_heads(gq_acc[...])
            dgk_ref[...] = _fold_heads(gk_acc[...])

    return _call(
        body,
        (dqn, dkn, dv, du, zqk, x, dh1, g_attn, gq_t, gk_t, w_in_t,
      _head_mean_matrix(ATTN_WIDTH), _head_mean_matrix(KV_WIDTH)),
        name="in_proj_bwd",
        grid=(nt,),
        in_specs=[
            _rows(ts, ATTN_WIDTH),
            _rows(ts, KV_WIDTH),
            _rows(ts, KV_WIDTH),
            _rows(ts, POOL_WIDTH),
            _rows(ts, ATTN_WIDTH + KV_WIDTH),
            _rows(ts, D_MODEL),
            _rows(ts, D_MODEL),
            _resident((1, D_MODEL)),
            _resident((1, ATTN_WIDTH)),
            _resident((1, KV_WIDTH)),
            _resident((IN_WIDTH, D_MODEL)),
            _resident((ATTN_WIDTH, ATTN_WIDTH)),
            _resident((KV_WIDTH, KV_WIDTH)),
        ],
        out_specs=[
            _rows(ts, D_MODEL),
            _acc((IN_WIDTH, D_MODEL)),
            _acc((1, D_MODEL)),
            _acc((1, SMALL_LANES)),
            _acc((1, SMALL_LANES)),
        ],
        out_shape=[
            jax.ShapeDtypeStruct((s, D_MODEL), F32),
            jax.ShapeDtypeStruct((IN_WIDTH, D_MODEL), F32),
            jax.ShapeDtypeStruct((1, D_MODEL), F32),
            jax.ShapeDtypeStruct((1, SMALL_LANES), F32),
            jax.ShapeDtypeStruct((1, SMALL_LANES), F32),
        ],
        scratch_shapes=[
            pltpu.VMEM((ts, IN_WIDTH), BF16),
            pltpu.VMEM((1, ATTN_WIDTH), F32),
            pltpu.VMEM((1, KV_WIDTH), F32),
        ],
        rider=rider,
    )


BIG_WEIGHTS = (
    ("w_in", True, IN_WIDTH // N_DEV, D_MODEL),
    ("w_out", False, D_MODEL // N_DEV, D_MODEL),
    ("w_gate", True, D_FF // N_DEV, D_MODEL),
    ("w_up", True, D_FF // N_DEV, D_MODEL),
    ("w_down", False, D_FF // N_DEV, D_MODEL),
    ("w_ple_gate", False, D_MODEL // N_DEV, D_MODEL),
    ("w_ple_proj", False, PLE_DIM, D_MODEL // N_DEV),
)
N_BIG = len(BIG_WEIGHTS)


def _place():
    x, y, c = lax.axis_index("x"), lax.axis_index("y"), lax.axis_index("c")
    chips = [(1 - x, y), (x, 1 - y), (1 - x, 1 - y)]
    return x, y, c, chips


class _Gather:
    def __init__(self, n, rows=None):
        self.n = n
        self.rows = rows or [None] * n
        self.sems = [pltpu.SemaphoreType.DMA((n, 7)), pltpu.SemaphoreType.DMA((n, 7)), pltpu.SemaphoreType.DMA((n,))]

    def _ctx(self, srcs, outs, sems):
        send_sems, recv_sems, local_sems = sems
        x, y, c, chips = _place()
        me, sibling = (x, y, c), (x, y, 1 - c)

        def part(k, ref):
            return ref if self.rows[k] is None else ref.at[pl.ds(*self.rows[k]), :]

        def block(k, owner):
            px, py, pc = owner
            return part(k, outs[k].at[4 * px + 2 * py + pc])

        def copy(k, idx, owner, to, mine=False):
            return pltpu.make_async_remote_copy(
                src_ref=part(k, srcs[k]) if mine else block(k, owner), dst_ref=block(k, owner),
                send_sem=send_sems.at[k, idx], recv_sem=recv_sems.at[k, idx], device_id=to, device_id_type=MESH)

        def local(k):
            return pltpu.make_async_copy(part(k, srcs[k]), block(k, me), local_sems.at[k])

        return c, chips, me, sibling, copy, local

    def begin(self, srcs, outs, sems):
        c, chips, me, sibling, copy, local = self._ctx(srcs, outs, sems)
        for k in range(self.n):
            local(k).start()
            copy(k, 0, me, sibling, mine=True).start()
            for j, chip in enumerate(chips):
                copy(k, 1 + j, me, (*chip, c), mine=True).start()

    def middle(self, srcs, outs, sems):
        c, chips, me, sibling, copy, local = self._ctx(srcs, outs, sems)
        for j, chip in enumerate(chips):
            for k in range(self.n):
                copy(k, 1 + j, (*chip, c), me).wait_recv()
                copy(k, 4 + j, (*chip, c), sibling).start()

    def end(self, srcs, outs, sems):
        c, chips, me, sibling, copy, local = self._ctx(srcs, outs, sems)
        for k in range(self.n):
            copy(k, 0, sibling, me).wait_recv()
            for j, chip in enumerate(chips):
                copy(k, 4 + j, (*chip, 1 - c), me).wait_recv()
        for k in range(self.n):
            copy(k, 0, me, sibling, mine=True).wait_send()
            for j, chip in enumerate(chips):
                copy(k, 1 + j, me, (*chip, c), mine=True).wait_send()
                copy(k, 4 + j, (*chip, c), sibling).wait_send()
            local(k).wait()


def _gather_rider(items):
    items = [it if isinstance(it, tuple) else (it, None, None, None) for it in items]
    n = len(items)
    g = _Gather(n, [None if r0 is None else (r0, nr) for _, r0, nr, _ in items])
    shapes = [jax.ShapeDtypeStruct((N_DEV, *sh.shape), sh.dtype) for sh, _, _, _ in items]
    stacks = [(k, st) for k, (_, _, _, st) in enumerate(items) if st is not None]
    aliases = {n + i: k for i, (k, _) in enumerate(stacks)}
    return _Rider([sh for sh, _, _, _ in items] + [st for _, st in stacks], shapes, g.sems, g.begin, g.end, g.middle,
                  aliases=aliases)


def _cast_and_gather_first(shards):
    g = _Gather(1)
    any_spec = pl.BlockSpec(memory_space=pl.ANY)
    vmem = pl.BlockSpec(memory_space=pltpu.VMEM)

    def body(*refs):
        ins, outs, gathered, sems = refs[:N_BIG], refs[N_BIG:2 * N_BIG], refs[2 * N_BIG], refs[2 * N_BIG + 1:]
        outs[0][...] = ins[0][...].astype(BF16)
        g.begin(outs[:1], [gathered], sems)
        for k in range(1, N_BIG):
            outs[k][...] = ins[k][...].astype(BF16)
        g.middle(outs[:1], [gathered], sems)
        g.end(outs[:1], [gathered], sems)

    res = pl.pallas_call(
        body,
        name="cast_and_gather_first",
        in_specs=[vmem] * N_BIG,
        out_specs=[vmem] * N_BIG + [any_spec],
        out_shape=[jax.ShapeDtypeStruct((r, c), BF16) for _, _, r, c in BIG_WEIGHTS]
        + [jax.ShapeDtypeStruct((N_DEV, *BIG_WEIGHTS[0][2:]), BF16)],
        scratch_shapes=g.sems,
    )(*shards)
    return list(res[:N_BIG]), res[N_BIG]


def _sibling_rider(grads):
    n = len(grads)

    def copies(gs, lands, sems):
        send_sems, recv_sems = sems
        x, y, c, _ = _place()
        return [
            pltpu.make_async_remote_copy(
                src_ref=gs[k].at[:, 1 - c], dst_ref=lands[k], send_sem=send_sems.at[k], recv_sem=recv_sems.at[k],
                device_id=(x, y, 1 - c), device_id_type=MESH)
            for k in range(n)
        ]

    def begin(gs, lands, sems):
        for cp in copies(gs, lands, sems):
            cp.start()

    def end(gs, lands, sems):
        for cp in copies(gs, lands, sems):
            cp.wait()

    shapes = [jax.ShapeDtypeStruct((N_CHIPS, *g.shape[2:]), F32) for g in grads]
    return _Rider(grads, shapes, [pltpu.SemaphoreType.DMA((n,)), pltpu.SemaphoreType.DMA((n,))], begin, end)


def _chip_of_relation(j, place):
    x, y = place[0], place[1]
    return jnp.where(j == 0, 2 * (1 - x) + y, jnp.where(j == 1, 2 * x + 1 - y, 2 * (1 - x) + 1 - y))


def _chip_sum(ks, place, grads, from_sibling):
    n = len(ks)
    shapes = [BIG_WEIGHTS[k][2:] for k in ks]
    operands, specs = [], []
    for (r, c), g, l in zip(shapes, grads, from_sibling):
        operands += [g, l]
        specs += [pl.BlockSpec((1, 1, r, c), lambda j, place: (_chip_of_relation(j, place), place[2], 0, 0)),
                  pl.BlockSpec((1, r, c), lambda j, place: (_chip_of_relation(j, place), 0, 0))]
    args, in_specs, _ = _after_last(operands, specs)

    def body(place_ref, *refs):
        ins, outs = refs[:2 * n], refs[len(args):]
        for i in range(n):
            outs[i][0] = (ins[2 * i][0, 0] + ins[2 * i + 1][0]).astype(BF16)

    outs = pl.pallas_call(
        body,
        name="chip_sum_" + "_".join(BIG_WEIGHTS[k][0] for k in ks),
        grid_spec=pltpu.PrefetchScalarGridSpec(
            num_scalar_prefetch=1,
            grid=(N_CHIPS - 1,),
            in_specs=in_specs,
            out_specs=[pl.BlockSpec((1, r, c), lambda j, place: (j, 0, 0)) for r, c in shapes],
        ),
        out_shape=[jax.ShapeDtypeStruct((N_CHIPS - 1, r, c), BF16) for r, c in shapes],
    )(place, *args)
    _mark_issued(outs[0])
    return list(outs)


def _chips_rider(to_send, small=None):
    n = len(to_send)
    inputs = list(to_send) + ([] if small is None else [small])
    shapes = [jax.ShapeDtypeStruct((3, *t.shape[1:]), BF16) for t in to_send]
    sems = [pltpu.SemaphoreType.DMA((max(n, 1), 3)), pltpu.SemaphoreType.DMA((max(n, 1), 3))]
    if small is not None:
        shapes.append(jax.ShapeDtypeStruct((N_DEV, *small.shape), F32))
        sems += [pltpu.SemaphoreType.DMA((7,)), pltpu.SemaphoreType.DMA((7,)), pltpu.SemaphoreType.DMA]

    def copies(ins, outs, sem_refs):
        x, y, c, chips = _place()
        out = []
        for k in range(n):
            for j, (px, py) in enumerate(chips):
                out.append(pltpu.make_async_remote_copy(
                    src_ref=ins[k].at[j], dst_ref=outs[k].at[j],
                    send_sem=sem_refs[0].at[k, j], recv_sem=sem_refs[1].at[k, j],
                    device_id=(px, py, c), device_id_type=MESH))
        local = None
        if small is not None:
            me = 4 * x + 2 * y + c
            local = pltpu.make_async_copy(ins[n], outs[n].at[me], sem_refs[4])
            rel = 0
            for fx in (0, 1):
                for fy in (0, 1):
                    for fc in (0, 1):
                        if (fx, fy, fc) != (0, 0, 0):
                            out.append(pltpu.make_async_remote_copy(
                                src_ref=ins[n], dst_ref=outs[n].at[me],
                                send_sem=sem_refs[2].at[rel], recv_sem=sem_refs[3].at[rel],
                                device_id=(x ^ fx, y ^ fy, c ^ fc), device_id_type=MESH))
                            rel += 1
        return out, local

    def begin(ins, outs, sem_refs):
        remote, local = copies(ins, outs, sem_refs)
        if local is not None:
            local.start()
        for cp in remote:
            cp.start()

    def end(ins, outs, sem_refs):
        remote, local = copies(ins, outs, sem_refs)
        for cp in remote:
            cp.wait()
        if local is not None:
            local.wait()

    return _Rider(inputs, shapes, sems, begin, end)


def _exchange(name, rider):
    return _call(lambda: None, (), name=name, grid=(1,), in_specs=[], out_specs=[], out_shape=[], rider=rider)[1]


PEER_SETS = {"sibling": 1, "chips": 2, "sibling+chips": 3, "all": 4}


def _peers(pattern):
    x, y, c, chips = _place()
    sibling, others = [(x, y, 1 - c)], [(*chip, c) for chip in chips]
    if pattern == "all":
        return sibling + others + [(*chip, 1 - c) for chip in chips]
    return {"sibling": sibling, "chips": others, "sibling+chips": sibling + others}[pattern]


def _on_sequencer(name, pattern, rider):
    assert not rider.aliases
    n_in, n_out = len(rider.inputs), len(rider.out_shapes)

    def body(*refs):
        ins, outs, sems = refs[:n_in], refs[n_in:n_in + n_out], refs[n_in + n_out:]
        peers = _peers(pattern)
        barrier = pltpu.get_barrier_semaphore()
        for peer in peers:
            pl.semaphore_signal(barrier, inc=1, device_id=peer, device_id_type=MESH)
        pl.semaphore_wait(barrier, len(peers))
        rider.begin(ins, outs, sems)
        if rider.middle is not None:
            rider.middle(ins, outs, sems)
        rider.end(ins, outs, sems)

    outs = pl.kernel(
        body,
        name=name,
        out_type=tuple(rider.out_shapes),
        mesh=plsc.ScalarSubcoreMesh(axis_name="sequencer", num_cores=1),
        scratch_types=tuple(rider.sems),
        compiler_params=pltpu.CompilerParams(collective_id=PEER_SETS[pattern]),
    )(*rider.inputs)
    return list(outs)


def _merge_riders(*riders):
    riders = [r for r in riders if r is not None]
    if len(riders) == 1:
        return riders[0]
    assert not any(r.aliases for r in riders)

    def split(refs, counts):
        out, at = [], 0
        for n in counts:
            out.append(refs[at:at + n])
            at += n
        return out

    def run(which):
        def fn(ins, outs, sems):
            parts = zip(riders, split(ins, [len(r.inputs) for r in riders]),
                        split(outs, [len(r.out_shapes) for r in riders]), split(sems, [len(r.sems) for r in riders]))
            for r, i, o, s in parts:
                hook = getattr(r, which)
                if hook is not None:
                    hook(i, o, s)
        return fn

    middle = run("middle") if any(r.middle is not None for r in riders) else None
    return _Rider(sum((r.inputs for r in riders), []), sum((r.out_shapes for r in riders), []),
                  sum((r.sems for r in riders), []), run("begin"), run("end"), middle)


def _split_outputs(outs, *riders):
    res, at = [], 0
    for r in riders:
        res.append(outs[at:at + len(r.out_shapes)])
        at += len(r.out_shapes)
    return res


def _adamw(w, g, m, v):
    m = ADAM_B1 * m + (1.0 - ADAM_B1) * g
    v = ADAM_B2 * v + (1.0 - ADAM_B2) * jnp.square(g)
    m_hat = m / (1.0 - ADAM_B1 ** ADAM_STEP)
    v_hat = v / (1.0 - ADAM_B2 ** ADAM_STEP)
    delta = -ADAM_LR * (m_hat / (jnp.sqrt(v_hat) + ADAM_EPS) + ADAM_WD * w)
    return delta, m, v


def _adamw_big(ks, place, operands):
    n = len(ks)
    tiles = lambda i, place: (i, 0)
    in_specs, out_specs, out_shape = [], [], []
    for k in ks:
        _, _, r, c = BIG_WEIGHTS[k]
        tile = r // 2
        in_specs += [
            pl.BlockSpec((1, 1, tile, c), lambda i, place: (2 * place[0] + place[1], place[2], i, 0)),
            pl.BlockSpec((1, tile, c), lambda i, place: (2 * place[0] + place[1], i, 0)),
            pl.BlockSpec((3, tile, c), lambda i, place: (0, i, 0)),
        ] + [pl.BlockSpec((tile, c), tiles)] * 3
        out_specs += [pl.BlockSpec((tile, c), tiles)] * 4
        out_shape += [jax.ShapeDtypeStruct((r, c), F32)] * 4
    args, in_specs, _ = _after_last(sum((list(ops) for ops in operands), []), in_specs)

    def body(place_ref, *refs):
        ins, outs = refs[:6 * n], refs[len(args):]
        for i in range(n):
            mine_ref, sib_ref, land_ref, w_ref, m_ref, v_ref = ins[6 * i:6 * i + 6]
            g_ref, d_ref, nm_ref, nv_ref = outs[4 * i:4 * i + 4]
            g = mine_ref[0, 0] + sib_ref[0]
            g = ((g + land_ref[0].astype(F32)) + land_ref[1].astype(F32)) + land_ref[2].astype(F32)
            g_ref[...] = g
            d_ref[...], nm_ref[...], nv_ref[...] = _adamw(w_ref[...], g, m_ref[...], v_ref[...])

    outs = pl.pallas_call(
        body,
        name="adamw_" + "_".join(BIG_WEIGHTS[k][0] for k in ks),
        grid_spec=pltpu.PrefetchScalarGridSpec(
            num_scalar_prefetch=1, grid=(2,), in_specs=in_specs, out_specs=out_specs),
        out_shape=out_shape,
    )(place, *args)
    _mark_issued(outs[0])
    return [outs[4 * i:4 * i + 4] for i in range(n)]


def _sum_small(parts_list):
    n = len(parts_list)

    def body(*refs):
        for p_ref, out_ref in zip(refs[:n], refs[n:]):
            tot = p_ref[0]
            for j in range(1, N_DEV):
                tot = tot + p_ref[j]
            out_ref[...] = tot

    return pl.pallas_call(body, name="sum_small",
                          out_shape=[jax.ShapeDtypeStruct(p.shape[1:], F32) for p in parts_list])(*parts_list)


def _adamw_small(grads, ws, ms, vs):
    n = len(grads)

    def body(*refs):
        g_refs, w_refs, m_refs, v_refs = refs[:n], refs[n:2 * n], refs[2 * n:3 * n], refs[3 * n:4 * n]
        outs = refs[4 * n:]
        for i in range(n):
            d, nm, nv = _adamw(w_refs[i][...], g_refs[i][...], m_refs[i][...], v_refs[i][...])
            outs[i][...] = d
            outs[n + i][...] = nm
            outs[2 * n + i][...] = nv

    shapes = [jax.ShapeDtypeStruct(w.shape, F32) for w in ws]
    return pl.pallas_call(body, name="adamw_small", out_shape=shapes * 3)(*grads, *ws, *ms, *vs)


SMALL_NAMES = ("g_attn_norm", "g_q", "g_k", "attn_sinks", "rel_bias", "w_pool", "pool_scale", "g_ffn_norm", "g_ple_norm")


def _pack_small(arrays):
    rows, offsets = [], []
    at = 0
    for a in arrays:
        flat = a.reshape(-1)
        n_rows = -(-flat.shape[0] // (8 * SMALL_LANES)) * 8
        flat = jnp.pad(flat, (0, n_rows * SMALL_LANES - flat.shape[0]))
        rows.append(flat.reshape(n_rows, SMALL_LANES))
        offsets.append(at)
        at += n_rows
    return jnp.concatenate(rows, axis=0), offsets


def kernel(x, p, w_in, w_out, g_attn_norm, g_q, g_k, attn_sinks, rel_bias, w_pool, pool_scale, g_ffn_norm, w_gate, w_up, w_down, g_ple_norm, w_ple_gate, w_ple_proj, loss_target, m_w_in, m_w_out, m_g_attn_norm, m_g_q, m_g_k, m_attn_sinks, m_rel_bias, m_w_pool, m_pool_scale, m_g_ffn_norm, m_w_gate, m_w_up, m_w_down, m_g_ple_norm, m_w_ple_gate, m_w_ple_proj, v_w_in, v_w_out, v_g_attn_norm, v_g_q, v_g_k, v_attn_sinks, v_rel_bias, v_w_pool, v_pool_scale, v_g_ffn_norm, v_w_gate, v_w_up, v_w_down, v_g_ple_norm, v_w_ple_gate, v_w_ple_proj):
    weights = dict(w_in=w_in, w_out=w_out, g_attn_norm=g_attn_norm, g_q=g_q, g_k=g_k, attn_sinks=attn_sinks,
                   rel_bias=rel_bias, w_pool=w_pool, pool_scale=pool_scale, g_ffn_norm=g_ffn_norm, w_gate=w_gate,
                   w_up=w_up, w_down=w_down, g_ple_norm=g_ple_norm, w_ple_gate=w_ple_gate, w_ple_proj=w_ple_proj)
    m_in = dict(w_in=m_w_in, w_out=m_w_out, g_attn_norm=m_g_attn_norm, g_q=m_g_q, g_k=m_g_k, attn_sinks=m_attn_sinks,
                rel_bias=m_rel_bias, w_pool=m_w_pool, pool_scale=m_pool_scale, g_ffn_norm=m_g_ffn_norm, w_gate=m_w_gate,
                w_up=m_w_up, w_down=m_w_down, g_ple_norm=m_g_ple_norm, w_ple_gate=m_w_ple_gate, w_ple_proj=m_w_ple_proj)
    v_in = dict(w_in=v_w_in, w_out=v_w_out, g_attn_norm=v_g_attn_norm, g_q=v_g_q, g_k=v_g_k, attn_sinks=v_attn_sinks,
                rel_bias=v_rel_bias, w_pool=v_w_pool, pool_scale=v_pool_scale, g_ffn_norm=v_g_ffn_norm, w_gate=v_w_gate,
                w_up=v_w_up, w_down=v_w_down, g_ple_norm=v_g_ple_norm, w_ple_gate=v_w_ple_gate, w_ple_proj=v_w_ple_proj)

    _issued.clear()
    xs = x[0]
    ps = p[0, 0]
    target = loss_target[0]
    wp = w_pool[0]
    gq_t = jnp.tile(g_q, (1, ATTN_WIDTH // HEAD_DIM))
    gk_t = jnp.tile(g_k, (1, KV_WIDTH // HEAD_DIM))

    def to_blocks(k, arr):
        return jnp.swapaxes(arr[0], 0, 1) if BIG_WEIGHTS[k][1] else arr[0]

    def from_blocks(k, arr):
        return (jnp.swapaxes(arr, 0, 1) if BIG_WEIGHTS[k][1] else arr)[None]

    IN, OUT, GATE, UP, DOWN, PG, PP = range(N_BIG)
    full = lambda g: g.reshape(N_DEV * g.shape[1], g.shape[2])
    halves = lambda k, g: g.reshape(N_CHIPS, 2, *BIG_WEIGHTS[k][2:])
    place = jnp.stack([lax.axis_index("x"), lax.axis_index("y"), lax.axis_index("c")]).astype(jnp.int32)

    sh, w_in_g = _cast_and_gather_first([to_blocks(k, weights[name]) for k, (name, _, _, _) in enumerate(BIG_WEIGHTS)])
    w_in_t = full(w_in_g)

    (w_out_g,) = _on_sequencer("gather_out", "sibling+chips", _gather_rider([sh[OUT]]))
    wg_g, wu_g = _on_sequencer("gather_gate_up", "sibling+chips", _gather_rider([sh[GATE], sh[UP]]))
    wd_g, w_pg_g, w_pp_g = _on_sequencer("gather_down_ple", "sibling+chips", _gather_rider([sh[DOWN], sh[PG], sh[PP]]))
    tab = _bias_table(rel_bias.T)
    (zqk, qn, kn, v, u), _ = _in_proj(xs, g_attn_norm, w_in_t, gq_t, gk_t)
    (a,), _ = _attn_fwd(qn, kn, v, tab, attn_sinks)
    w_out_f = full(w_out_g)
    (h1, hn2, m_out), _ = _mix_out(u, a, xs, w_out_f, wp, pool_scale, g_ffn_norm)
    wg_t, wu_t = full(wg_g), full(wu_g)
    (gt, up), _ = _ffn_up(hn2, wg_t, wu_t)
    w_down_f = full(wd_g)

    partial, from_sibling, sums, landed = [None] * N_BIG, [None] * N_BIG, [None] * N_BIG, [None] * N_BIG

    def to_sibling(name, ks, grads):
        for k, g in zip(ks, grads):
            partial[k] = halves(k, g)
        got = _on_sequencer(name, "sibling", _sibling_rider([partial[k] for k in ks]))
        for k, g in zip(ks, got):
            from_sibling[k] = g

    def chip_sum(*ks):
        for k, s in zip(ks, _chip_sum(ks, place, [partial[k] for k in ks], [from_sibling[k] for k in ks])):
            sums[k] = s

    def to_chips(name, ks, small=None):
        got = _on_sequencer(name, "chips" if small is None else "all", _chips_rider([sums[k] for k in ks], small))
        for k, g in zip(ks, got):
            landed[k] = g
        return got[len(ks):]

    (loss_part, dh2, d_wpg, d_wpp, d_g_ple), _ = _ffn_down_ple(
        gt, up, h1, w_down_f, ps, target, g_ple_norm, full(w_pg_g), w_pp_g)
    to_sibling("sibling_ple", (PG, PP), (d_wpg, d_wpp))
    (dgt, dup, dh1, dh1b, d_g_ffn, d_wd), _ = _ffn_bwd_act(dh2, h1, gt, up, g_ffn_norm, wg_t, wu_t, w_down_f)
    to_sibling("sibling_down", (DOWN,), (d_wd,))
    chip_sum(PG, PP)
    to_chips("chips_ple", (PG, PP))
    (d_wo,), _ = _out_w_bwd(a, m_out, dh1b)
    to_sibling("sibling_out", (OUT,), (d_wo,))
    (d_wg_t, d_wu_t), _ = _ffn_bwd_w(dgt, dup, hn2)
    to_sibling("sibling_gate_up", (GATE, UP), (d_wg_t, d_wu_t))
    _complete_before_next([landed[PG], landed[PP]])
    chip_sum(DOWN, OUT)
    to_chips("chips_down_out", (DOWN, OUT))
    (da, du, d_wpool, d_scale), _ = _mix_bwd(dh1b, u, w_out_f, wp, pool_scale)
    chip_sum(GATE, UP)
    early, early_at = _pack_small([d_wpool, d_scale, d_g_ffn, d_g_ple, loss_part[:, :1]])
    (early_all,) = to_chips("chips_gate_up", (GATE, UP), early)
    (dqn, dkn, dv, dl_acc, d_sinks), _ = _attn_bwd(qn, kn, v, a, da, tab, attn_sinks)
    _complete_before_next([landed[DOWN], landed[OUT]])
    (grad_x, d_win_t, d_g_attn, d_gq, d_gk), _ = _in_proj_bwd(dqn, dkn, dv, du, zqk, xs, dh1, g_attn_norm, gq_t, gk_t, w_in_t)
    to_sibling("sibling_in", (IN,), (d_win_t,))
    _complete_before_next([landed[GATE], landed[UP], early_all])
    (d_rel_t,), _ = _bias_table_bwd(dl_acc)
    chip_sum(IN)
    late, late_at = _pack_small([d_g_attn, d_gq[:, :HEAD_DIM], d_gk[:, :HEAD_DIM], d_sinks[:, 0], d_rel_t[:, :N_BUCKETS]])
    (late_all,) = to_chips("chips_in", (IN,), late)

    out = {"grad": {}, "delta": {}, "new_m": {}, "new_v": {}}
    for ks in ((PG, PP, OUT), (DOWN, GATE, UP), (IN,)):
        names = [BIG_WEIGHTS[k][0] for k in ks]
        results = _adamw_big(ks, place, [
            (partial[k], from_sibling[k], landed[k], to_blocks(k, weights[n]), to_blocks(k, m_in[n]),
             to_blocks(k, v_in[n])) for k, n in zip(ks, names)])
        for k, name, res in zip(ks, names, results):
            for kind, r in zip(("grad", "delta", "new_m", "new_v"), res):
                out[kind][name] = from_blocks(k, r)
    early_sum, late_sum = _sum_small([early_all, late_all])

    def unpack(packed, at, shape):
        n = math.prod(shape)
        return packed[at:at + -(-n // SMALL_LANES)].reshape(-1)[:n].reshape(shape)

    small_grads = dict(
        w_pool=unpack(early_sum, early_at[0], w_pool.shape), pool_scale=unpack(early_sum, early_at[1], pool_scale.shape),
        g_ffn_norm=unpack(early_sum, early_at[2], g_ffn_norm.shape), g_ple_norm=unpack(early_sum, early_at[3], g_ple_norm.shape),
        g_attn_norm=unpack(late_sum, late_at[0], g_attn_norm.shape), g_q=unpack(late_sum, late_at[1], g_q.shape),
        g_k=unpack(late_sum, late_at[2], g_k.shape), attn_sinks=unpack(late_sum, late_at[3], attn_sinks.shape),
        rel_bias=unpack(late_sum, late_at[4], rel_bias.T.shape))
    loss = early_sum[early_at[4], 0]
    flip = lambda name, arr: arr.T if name == "rel_bias" else arr
    updates = _adamw_small([small_grads[n] for n in SMALL_NAMES], [flip(n, weights[n]) for n in SMALL_NAMES],
                           [flip(n, m_in[n]) for n in SMALL_NAMES], [flip(n, v_in[n]) for n in SMALL_NAMES])
    n_small = len(SMALL_NAMES)
    for i, name in enumerate(SMALL_NAMES):
        out["grad"][name] = flip(name, small_grads[name])
        out["delta"][name] = flip(name, updates[i])
        out["new_m"][name] = flip(name, updates[n_small + i])
        out["new_v"][name] = flip(name, updates[2 * n_small + i])

    _issued.clear()
    order = ("w_in", "w_out", "g_attn_norm", "g_q", "g_k", "attn_sinks", "rel_bias", "w_pool", "pool_scale",
             "g_ffn_norm", "w_gate", "w_up", "w_down", "g_ple_norm", "w_ple_gate", "w_ple_proj")
    return (loss, grad_x[None], *[out["grad"][n] for n in order], *[out["delta"][n] for n in order],
            *[out["new_m"][n] for n in order], *[out["new_v"][n] for n in order])
```

```python
import functools
import math

import jax
import jax.numpy as jnp
import numpy as np
from jax import lax
from jax.experimental import pallas as pl
from jax.experimental.pallas import tpu as pltpu
from jax.experimental.pallas import tpu_sc as plsc

F32 = jnp.float32
BF16 = jnp.bfloat16
MESH = pl.DeviceIdType.MESH

D_MODEL = 1024
HEAD_DIM = 64
ATTN_WIDTH = 512
KV_WIDTH = 128
POOL_WIDTH = 512
POOL_SIZES = (2, 4, 8, 16)
POOL_GROUP = 128
POOL_HALO = 16
IN_WIDTH = 1280
D_FF = 2816
PLE_DIM = 256
BLOCK = 128
N_BUCKETS = 32
MAX_DISTANCE = 128
EPS = 1e-6
N_DEV = 8
N_CHIPS = 4

ADAM_LR = 0.001
ADAM_B1 = 0.9
ADAM_B2 = 0.999
ADAM_EPS = 1e-08
ADAM_WD = 0.01
ADAM_STEP = 10

TOKEN_TILE = 512
FFN_BWD_TILE = 256
FF_CHUNK = 256
ATTN_STEP_BLOCKS = 4
GATE_ROWS_EARLY = 96
UP_ROWS_EARLY = 64
HEADS_A = (0, 2, 5, 7)
HEADS_B = (1, 3, 4, 6)
SMALL_LANES = 128


def _nn(a, b):
    return jnp.dot(a, b, preferred_element_type=F32)


def _nt(a, b):
    return lax.dot_general(a, b, (((1,), (1,)), ((), ())), preferred_element_type=F32)


def _tn(a, b):
    return lax.dot_general(a, b, (((0,), (0,)), ((), ())), preferred_element_type=F32)


def _resident(shape):
    nd = len(shape)
    return pl.BlockSpec(shape, lambda i, _nd=nd: (0,) * _nd, pipeline_mode=pl.Buffered(1))


def _rows(tile, width):
    return pl.BlockSpec((tile, width), lambda i: (i, 0))


def _acc(shape):
    nd = len(shape)
    return pl.BlockSpec(shape, lambda i, _nd=nd: (0,) * _nd)


def _head_mean_matrix(width):
    idx = np.arange(width) // HEAD_DIM
    return jnp.asarray((idx[:, None] == idx[None, :]).astype(np.float32) / HEAD_DIM, dtype=BF16)


def _seg_mean(v, bmat):
    hi = v.astype(BF16)
    lo = (v - hi.astype(F32)).astype(BF16)
    return _nn(hi, bmat) + _nn(lo, bmat)


def _rms(x):
    return lax.rsqrt(jnp.mean(x * x, axis=-1, keepdims=True) + EPS)


def _rms_bwd(d_y, x, r, g):
    gy = d_y * g
    d_x = r * gy - x * (r * r * r) * jnp.mean(gy * x, axis=-1, keepdims=True)
    d_g = jnp.sum(d_y * (x * r), axis=0, keepdims=True)
    return d_x, d_g


def _lane_lo(shape):
    return lax.broadcasted_iota(jnp.int32, shape, 1) < HEAD_DIM


class _Rider:
    def __init__(self, inputs, out_shapes, sems, begin, end, middle=None, aliases=None):
        self.inputs, self.out_shapes, self.sems = list(inputs), list(out_shapes), list(sems)
        self.begin, self.middle, self.end = begin, middle, end
        self.aliases = dict(aliases or {})


_issued = []


def _after_last(args, in_specs):
    extra = list(_issued)
    return list(args) + extra, list(in_specs) + [pl.BlockSpec(memory_space=pl.ANY)] * len(extra), len(extra)


def _mark_issued(out):
    _issued[:] = [out]


def _complete_before_next(arrays):
    _issued.extend(arrays)


def _call(body, args, *, name, grid, in_specs, out_specs, out_shape, scratch_shapes=(), rider=None):
    in_specs, out_specs, out_shape, scratch_shapes = list(in_specs), list(out_specs), list(out_shape), list(scratch_shapes)
    if rider is None:
        n_args = len(args)
        args, in_specs, _ = _after_last(args, in_specs)

        def ordered(*refs):
            body(*refs[:n_args], *refs[len(args):])

        outs = pl.pallas_call(ordered, name=name, grid=grid, in_specs=in_specs, out_specs=out_specs, out_shape=out_shape,
                              scratch_shapes=scratch_shapes)(*args)
        _mark_issued(outs[0])
        return list(outs), []
    n_in, n_out, n_scr = len(in_specs), len(out_shape), len(scratch_shapes)
    r_in, r_out = len(rider.inputs), len(rider.out_shapes)
    n_steps = grid[0]

    def hosted(*refs):
        ins, refs = refs[:n_in], refs[n_in:]
        r_ins, refs = refs[:r_in], refs[r_in:]
        outs, refs = refs[:n_out], refs[n_out:]
        r_outs, refs = refs[:r_out], refs[r_out:]
        scratch, r_sems = refs[:n_scr], refs[n_scr:]
        step = pl.program_id(0)

        @pl.when(step == 0)
        def _():
            rider.begin(r_ins, r_outs, r_sems)

        if rider.middle is not None:
            @pl.when(step == n_steps - 1)
            def _():
                rider.middle(r_ins, r_outs, r_sems)

        body(*ins, *outs, *scratch)

        @pl.when(step == n_steps - 1)
        def _():
            rider.end(r_ins, r_outs, r_sems)

    any_spec = pl.BlockSpec(memory_space=pl.ANY)
    outs = pl.pallas_call(
        hosted, name=name, grid=grid,
        in_specs=in_specs + [any_spec] * r_in,
        out_specs=out_specs + [any_spec] * r_out,
        out_shape=out_shape + rider.out_shapes,
        scratch_shapes=scratch_shapes + rider.sems,
        input_output_aliases={n_in + i: n_out + o for i, o in rider.aliases.items()},
    )(*args, *rider.inputs)
    return list(outs[:n_out]), list(outs[n_out:])


def _in_proj(x, g_attn, w_in_t, gq_t, gk_t, rider=None):
    s = x.shape[0]
    ts = min(TOKEN_TILE, s)

    def body(x_ref, g_ref, w_ref, gq_ref, gk_ref, bq_ref, bk_ref, zqk_ref, qn_ref, kn_ref, v_ref, u_ref):
        xf = x_ref[...]
        hn = ((xf * _rms(xf)) * g_ref[...]).astype(BF16)
        z = _nt(hn, w_ref[...])
        q = z[:, :ATTN_WIDTH]
        k = z[:, ATTN_WIDTH:ATTN_WIDTH + KV_WIDTH]
        zqk_ref[...] = z[:, :ATTN_WIDTH + KV_WIDTH]
        rq = lax.rsqrt(_seg_mean(q * q, bq_ref[...]) + EPS)
        qn_ref[...] = ((q * rq) * gq_ref[...]).astype(BF16)
        rk = lax.rsqrt(_seg_mean(k * k, bk_ref[...]) + EPS)
        kn_ref[...] = ((k * rk) * gk_ref[...]).astype(BF16)
        v_ref[...] = z[:, ATTN_WIDTH + KV_WIDTH:ATTN_WIDTH + 2 * KV_WIDTH].astype(BF16)
        u_ref[...] = z[:, ATTN_WIDTH + 2 * KV_WIDTH:]

    return _call(
        body,
        (x, g_attn, w_in_t, gq_t, gk_t, _head_mean_matrix(ATTN_WIDTH), _head_mean_matrix(KV_WIDTH)),
        name="in_proj",
        grid=(s // ts,),
        in_specs=[
            _rows(ts, D_MODEL),
            _resident((1, D_MODEL)),
            _resident((IN_WIDTH, D_MODEL)),
            _resident((1, ATTN_WIDTH)),
            _resident((1, KV_WIDTH)),
            _resident((ATTN_WIDTH, ATTN_WIDTH)),
            _resident((KV_WIDTH, KV_WIDTH)),
        ],
        out_specs=[
            _rows(ts, ATTN_WIDTH + KV_WIDTH),
            _rows(ts, ATTN_WIDTH),
            _rows(ts, KV_WIDTH),
            _rows(ts, KV_WIDTH),
            _rows(ts, POOL_WIDTH),
        ],
        out_shape=[
            jax.ShapeDtypeStruct((s, ATTN_WIDTH + KV_WIDTH), F32),
            jax.ShapeDtypeStruct((s, ATTN_WIDTH), BF16),
            jax.ShapeDtypeStruct((s, KV_WIDTH), BF16),
            jax.ShapeDtypeStruct((s, KV_WIDTH), BF16),
            jax.ShapeDtypeStruct((s, POOL_WIDTH), F32),
        ],
        rider=rider,
    )


def _bucket_ranges():
    n = np.arange(MAX_DISTANCE)
    max_exact = N_BUCKETS // 2
    nf = np.maximum(n, 1).astype(np.float64)
    large = max_exact + (np.log(nf / max_exact) / math.log(MAX_DISTANCE / max_exact) * (N_BUCKETS - max_exact)).astype(np.int64)
    bucket = np.where(n < max_exact, n, np.minimum(large, N_BUCKETS - 1))
    out = []
    for b in range(N_BUCKETS):
        idx = np.nonzero(bucket == b)[0]
        out.append((int(idx.min()), int(idx.max()) + 1))
    return out


def _band_distance():
    i = lax.broadcasted_iota(jnp.int32, (BLOCK, 2 * BLOCK), 0)
    j = lax.broadcasted_iota(jnp.int32, (BLOCK, 2 * BLOCK), 1)
    return BLOCK + i - j


def _bias_table(rel_bias_t):
    ranges = _bucket_ranges()

    def body(rb_ref, tab_ref):
        d = _band_distance()
        for half, heads in enumerate((HEADS_A, HEADS_B)):
            for slot, h in enumerate(heads):
                t = jnp.full((BLOCK, 2 * BLOCK), -jnp.inf, F32)
                for b, (lo, hi) in enumerate(ranges):
                    t = jnp.where((d >= lo) & (d < hi), rb_ref[h, b], t)
                tab_ref[half, slot * BLOCK:(slot + 1) * BLOCK, :] = t

    return pl.pallas_call(
        body,
        name="bias_table",
        in_specs=[pl.BlockSpec(memory_space=pltpu.SMEM)],
        out_shape=jax.ShapeDtypeStruct((2, 4 * BLOCK, 2 * BLOCK), F32),
    )(rel_bias_t)


def _bias_table_bwd(dl_acc, rider=None):
    ranges = _bucket_ranges()
    n_heads = len(HEADS_A) + len(HEADS_B)

    def body(dl_ref, out_ref):
        d = _band_distance()
        row = lax.broadcasted_iota(jnp.int32, (n_heads, SMALL_LANES), 0)
        lane = lax.broadcasted_iota(jnp.int32, (n_heads, SMALL_LANES), 1)
        out = jnp.zeros((n_heads, SMALL_LANES), F32)
        for b, (lo, hi) in enumerate(ranges):
            in_bucket = (d >= lo) & (d < hi)
            for half, heads in enumerate((HEADS_A, HEADS_B)):
                for slot, h in enumerate(heads):
                    g = dl_ref[half, slot * BLOCK:(slot + 1) * BLOCK, :]
                    part = jnp.sum(jnp.where(in_bucket, g, 0.0), axis=0, keepdims=True)
                    tot = jnp.sum(part, axis=1, keepdims=True)
                    out = jnp.where((row == h) & (lane == b), tot, out)
        out_ref[...] = out

    return _call(
        body,
        (dl_acc,),
        name="bias_table_bwd",
        grid=(1,),
        in_specs=[_acc((2, 4 * BLOCK, 2 * BLOCK))],
        out_specs=[_acc((n_heads, SMALL_LANES))],
        out_shape=[jax.ShapeDtypeStruct((n_heads, SMALL_LANES), F32)],
        rider=rider,
    )


def _stack_heads(pairs, lo_mask):
    zero = jnp.zeros_like(pairs[0])
    lo = [jnp.where(lo_mask, t, zero) for t in pairs]
    hi = [jnp.where(lo_mask, zero, t) for t in pairs]
    return (jnp.concatenate([lo[0], lo[1], hi[2], hi[3]], axis=0),
            jnp.concatenate([hi[0], hi[1], lo[2], lo[3]], axis=0))


def _unstack_heads(out_a, out_b, lo_mask):
    t = lambda x, r: x[r * BLOCK:(r + 1) * BLOCK, :]
    return [
        jnp.where(lo_mask, t(out_a, 0), t(out_b, 0)),
        jnp.where(lo_mask, t(out_a, 1), t(out_b, 1)),
        jnp.where(lo_mask, t(out_b, 2), t(out_a, 2)),
        jnp.where(lo_mask, t(out_b, 3), t(out_a, 3)),
    ]


def _sink_column(sink_ref, heads):
    row = lax.broadcasted_iota(jnp.int32, (4 * BLOCK, 1), 0)
    col = jnp.full((4 * BLOCK, 1), sink_ref[0, heads[3]], F32)
    for slot in (2, 1, 0):
        col = jnp.where(row < (slot + 1) * BLOCK, sink_ref[0, heads[slot]], col)
    return col


def _band_probs(q_stack, keys, tab, sink, first_block):
    s = _nt(q_stack, keys) * (HEAD_DIM ** -0.5) + tab
    if first_block is not None:
        col = lax.broadcasted_iota(jnp.int32, s.shape, 1)
        s = jnp.where(jnp.logical_and(first_block, col < BLOCK), -jnp.inf, s)
    m = jnp.maximum(jnp.max(s, axis=-1, keepdims=True), sink)
    e = jnp.exp(s - m)
    e_sink = jnp.exp(sink - m)
    den = jnp.sum(e, axis=-1, keepdims=True) + e_sink
    return e / den, e_sink / den


def _attn_specs(n_groups):
    group = lambda n: (jnp.minimum(n, n_groups - 1), 0)
    prev = lambda n: (jnp.maximum(jnp.minimum(n, n_groups - 1) * ATTN_STEP_BLOCKS - 1, 0), 0)
    return group, prev


def _band(prev_ref, group_ref, b):
    rows = lambda i: group_ref[i * BLOCK:(i + 1) * BLOCK, :]
    band = jnp.concatenate([prev_ref[...] if b == 0 else rows(b - 1), rows(b)], axis=0)
    return band, pltpu.roll(band, HEAD_DIM, 1)


def _attn_fwd(qn, kn, v, tab, sinks, rider=None):
    s = qn.shape[0]
    n_groups = s // (ATTN_STEP_BLOCKS * BLOCK)
    group, prev = _attn_specs(n_groups)
    rows = ATTN_STEP_BLOCKS * BLOCK

    def body(sink_ref, q_ref, kc_ref, kp_ref, vc_ref, vp_ref, tab_ref, o_ref):
        first = pl.program_id(0) == 0
        lo_mask = _lane_lo((BLOCK, BLOCK))
        for b in range(ATTN_STEP_BLOCKS):
            at = slice(b * BLOCK, (b + 1) * BLOCK)
            kk, kk_sw = _band(kp_ref, kc_ref, b)
            vv, vv_sw = _band(vp_ref, vc_ref, b)
            q_a, q_b = _stack_heads([q_ref[at, p * BLOCK:(p + 1) * BLOCK] for p in range(4)], lo_mask)
            no_prev = first if b == 0 else None
            p_a, _ = _band_probs(q_a, kk, tab_ref[0], _sink_column(sink_ref, HEADS_A), no_prev)
            p_b, _ = _band_probs(q_b, kk_sw, tab_ref[1], _sink_column(sink_ref, HEADS_B), no_prev)
            out = _unstack_heads(_nn(p_a.astype(BF16), vv), _nn(p_b.astype(BF16), vv_sw), lo_mask)
            for p in range(4):
                o_ref[at, p * BLOCK:(p + 1) * BLOCK] = out[p].astype(BF16)

    return _call(
        body,
        (sinks, qn, kn, kn, v, v, tab),
        name="attn_fwd",
        grid=(n_groups,),
        in_specs=[
            pl.BlockSpec(memory_space=pltpu.SMEM),
            pl.BlockSpec((rows, ATTN_WIDTH), group),
            pl.BlockSpec((rows, KV_WIDTH), group),
            pl.BlockSpec((BLOCK, KV_WIDTH), prev),
            pl.BlockSpec((rows, KV_WIDTH), group),
            pl.BlockSpec((BLOCK, KV_WIDTH), prev),
            _resident((2, 4 * BLOCK, 2 * BLOCK)),
        ],
        out_specs=[pl.BlockSpec((rows, ATTN_WIDTH), group)],
        out_shape=[jax.ShapeDtypeStruct((s, ATTN_WIDTH), BF16)],
        rider=rider,
    )


def _pooled(u_tile, u_halo, tile_index, tile_rows):
    halo = jnp.where(tile_index > 0, u_halo, 0.0)
    ext = jnp.concatenate([halo, u_tile], axis=0)
    sums = []
    acc = ext
    for shift in (1, 2, 4, 8):
        acc = acc + pltpu.roll(acc, shift, 0)
        sums.append(acc)
    t = tile_index * tile_rows + lax.broadcasted_iota(jnp.int32, (tile_rows, 1), 0)
    out = []
    for g, w in enumerate(POOL_SIZES):
        lanes = slice(g * POOL_GROUP, (g + 1) * POOL_GROUP)
        cnt = jnp.minimum(t + 1, w).astype(F32)
        out.append(sums[g][POOL_HALO:, lanes] / cnt - u_tile[:, lanes])
    return out


def _halo_before(tile):
    return lambda i: (jnp.maximum(i * (tile // POOL_HALO) - 1, 0), 0)


def _mix_out(u, a, x, w_out, w_pool, pool_scale, g_ffn, rider=None):
    s = x.shape[0]
    ts = min(TOKEN_TILE, s)

    def body(u_ref, uh_ref, a_ref, x_ref, wo_ref, wp_ref, sc_ref, g_ref, h1_ref, hn_ref, m_ref):
        i = pl.program_id(0)
        pooled = _pooled(u_ref[...], uh_ref[...], i, ts)
        for g in range(len(POOL_SIZES)):
            lanes = slice(g * POOL_GROUP, (g + 1) * POOL_GROUP)
            y = _nn(pooled[g].astype(BF16), wp_ref[g].astype(BF16))
            m_ref[:, lanes] = (y * sc_ref[:, lanes]).astype(BF16)
        h1 = x_ref[...] + _nn(a_ref[...], wo_ref[:ATTN_WIDTH, :]) + _nn(m_ref[...], wo_ref[ATTN_WIDTH:, :])
        h1_ref[...] = h1
        hn_ref[...] = ((h1 * _rms(h1)) * g_ref[...]).astype(BF16)

    return _call(
        body,
        (u, u, a, x, w_out, w_pool, pool_scale, g_ffn),
        name="mix_out",
        grid=(s // ts,),
        in_specs=[
            _rows(ts, POOL_WIDTH),
            pl.BlockSpec((POOL_HALO, POOL_WIDTH), _halo_before(ts)),
            _rows(ts, ATTN_WIDTH),
            _rows(ts, D_MODEL),
            _resident((D_MODEL, D_MODEL)),
            _resident((len(POOL_SIZES), POOL_GROUP, POOL_GROUP)),
            _resident((1, POOL_WIDTH)),
            _resident((1, D_MODEL)),
        ],
        out_specs=[_rows(ts, D_MODEL), _rows(ts, D_MODEL), _rows(ts, POOL_WIDTH)],
        out_shape=[
            jax.ShapeDtypeStruct((s, D_MODEL), F32),
            jax.ShapeDtypeStruct((s, D_MODEL), BF16),
            jax.ShapeDtypeStruct((s, POOL_WIDTH), BF16),
        ],
        rider=rider,
    )


def _ffn_up(hn2, wg_t, wu_t, rider=None):
    s = hn2.shape[0]
    ts = min(TOKEN_TILE, s)

    def body(hn_ref, wg_ref, wu_ref, gt_ref, up_ref):
        hn = hn_ref[...]
        for c in range(D_FF // FF_CHUNK):
            cols = slice(c * FF_CHUNK, (c + 1) * FF_CHUNK)
            gt_ref[:, cols] = _nt(hn, wg_ref[cols, :]).astype(BF16)
            up_ref[:, cols] = _nt(hn, wu_ref[cols, :]).astype(BF16)

    return _call(
        body,
        (hn2, wg_t, wu_t),
        name="ffn_up",
        grid=(s // ts,),
        in_specs=[_rows(ts, D_MODEL), _resident((D_FF, D_MODEL)), _resident((D_FF, D_MODEL))],
        out_specs=[_rows(ts, D_FF), _rows(ts, D_FF)],
        out_shape=[jax.ShapeDtypeStruct((s, D_FF), BF16), jax.ShapeDtypeStruct((s, D_FF), BF16)],
        rider=rider,
    )


def _silu_mul(gt, up):
    return (gt * jax.nn.sigmoid(gt)) * up


def _ffn_down_ple(gt, up, h1, w_down, p, target, g_ple, w_pg, w_pp, rider=None):
    s = h1.shape[0]
    ts = min(TOKEN_TILE, s)
    blk = D_MODEL // N_DEV

    def body(gt_ref, up_ref, h1_ref, wd_ref, p_ref, t_ref, g_ref, wpg_ref, wpp_ref,
             loss_ref, dh_ref, dwpg_ref, dwpp_ref, dg_ref, act_ref, pp_ref):
        @pl.when(pl.program_id(0) == 0)
        def _():
            loss_ref[...] = jnp.zeros_like(loss_ref)
            dwpg_ref[...] = jnp.zeros_like(dwpg_ref)
            dwpp_ref[...] = jnp.zeros_like(dwpp_ref)
            dg_ref[...] = jnp.zeros_like(dg_ref)

        h2v = h1_ref[...]
        for c in range(D_FF // FF_CHUNK):
            cols = slice(c * FF_CHUNK, (c + 1) * FF_CHUNK)
            act = _silu_mul(gt_ref[:, cols].astype(F32), up_ref[:, cols].astype(F32)).astype(BF16)
            h2v = _nn(act, wd_ref[cols, :]) + h2v
        r = _rms(h2v)
        hn = ((h2v * r) * g_ref[...]).astype(BF16)
        gate = jax.nn.sigmoid(_nn(hn, wpg_ref[...]))
        pb = p_ref[...].astype(BF16)
        for j in range(N_DEV):
            pp_ref[:, j * blk:(j + 1) * blk] = _nn(pb, wpp_ref[j])
        pp = pp_ref[...]
        diff = (h2v + gate * pp) - t_ref[...]
        loss_ref[...] += jnp.sum(jnp.sum(diff * diff, axis=0, keepdims=True), axis=1, keepdims=True) * (0.5 / D_MODEL)
        dy = diff * (1.0 / D_MODEL)
        d_pp = (dy * gate).astype(BF16)
        d_pre = ((dy * pp) * (gate * (1.0 - gate))).astype(BF16)
        for j in range(N_DEV):
            dwpp_ref[j] += _tn(pb, d_pp[:, j * blk:(j + 1) * blk])
        dwpg_ref[...] += _tn(hn, d_pre)
        d_x, d_g = _rms_bwd(_nt(d_pre, wpg_ref[...]), h2v, r, g_ref[...])
        dg_ref[...] += d_g
        dh_ref[...] = dy + d_x

    return _call(
        body,
        (gt, up, h1, w_down, p, target, g_ple, w_pg, w_pp),
        name="ffn_down_ple",
        grid=(s // ts,),
        in_specs=[
            _rows(ts, D_FF),
            _rows(ts, D_FF),
            _rows(ts, D_MODEL),
            _resident((D_FF, D_MODEL)),
            _rows(ts, PLE_DIM),
            _rows(ts, D_MODEL),
            _resident((1, D_MODEL)),
            _resident((D_MODEL, D_MODEL)),
            _resident((N_DEV, PLE_DIM, blk)),
        ],
        out_specs=[
            _acc((1, SMALL_LANES)),
            _rows(ts, D_MODEL),
            _acc((D_MODEL, D_MODEL)),
            _acc((N_DEV, PLE_DIM, blk)),
            _acc((1, D_MODEL)),
        ],
        out_shape=[
            jax.ShapeDtypeStruct((1, SMALL_LANES), F32),
            jax.ShapeDtypeStruct((s, D_MODEL), F32),
            jax.ShapeDtypeStruct((D_MODEL, D_MODEL), F32),
            jax.ShapeDtypeStruct((N_DEV, PLE_DIM, blk), F32),
            jax.ShapeDtypeStruct((1, D_MODEL), F32),
        ],
        scratch_shapes=[pltpu.VMEM((ts, D_FF), BF16), pltpu.VMEM((ts, D_MODEL), F32)],
        rider=rider,
    )


def _ffn_bwd_act(dh2, h1, gt, up, g_ffn, wg_t, wu_t, w_down, rider=None):
    s = h1.shape[0]
    ts = min(FFN_BWD_TILE, s)

    def body(dh_ref, h1_ref, gt_ref, up_ref, g_ref, wg_ref, wu_ref, wd_ref,
             dgt_ref, dup_ref, dh1_ref, dh1b_ref, dg_ref, dwd_ref, act_ref):
        @pl.when(pl.program_id(0) == 0)
        def _():
            dg_ref[...] = jnp.zeros_like(dg_ref)
            dwd_ref[...] = jnp.zeros_like(dwd_ref)

        dhb = dh_ref[...].astype(BF16)
        d_hn = jnp.zeros((ts, D_MODEL), F32)
        for c in range(D_FF // FF_CHUNK):
            cols = slice(c * FF_CHUNK, (c + 1) * FF_CHUNK)
            d_act = _nt(dhb, wd_ref[cols, :])
            gtv = gt_ref[:, cols].astype(F32)
            upv = up_ref[:, cols].astype(F32)
            sg = jax.nn.sigmoid(gtv)
            silu = gtv * sg
            act_ref[:, cols] = (silu * upv).astype(BF16)
            d_up = (d_act * silu).astype(BF16)
            d_gt = ((d_act * upv) * (sg * (1.0 + gtv * (1.0 - sg)))).astype(BF16)
            dup_ref[:, cols] = d_up
            dgt_ref[:, cols] = d_gt
            d_hn = (_nn(d_gt, wg_ref[cols, :]) + _nn(d_up, wu_ref[cols, :])) + d_hn
        dwd_ref[...] += _tn(act_ref[...], dhb)
        h1v = h1_ref[...]
        d_x, d_g = _rms_bwd(d_hn, h1v, _rms(h1v), g_ref[...])
        dg_ref[...] += d_g
        dh1 = dh_ref[...] + d_x
        dh1_ref[...] = dh1
        dh1b_ref[...] = dh1.astype(BF16)

    return _call(
        body,
        (dh2, h1, gt, up, g_ffn, wg_t, wu_t, w_down),
        name="ffn_bwd_act",
        grid=(s // ts,),
        in_specs=[
            _rows(ts, D_MODEL),
            _rows(ts, D_MODEL),
            _rows(ts, D_FF),
            _rows(ts, D_FF),
            _resident((1, D_MODEL)),
            _resident((D_FF, D_MODEL)),
            _resident((D_FF, D_MODEL)),
            _resident((D_FF, D_MODEL)),
        ],
        out_specs=[
            _rows(ts, D_FF), _rows(ts, D_FF),
            _rows(ts, D_MODEL), _rows(ts, D_MODEL), _acc((1, D_MODEL)), _acc((D_FF, D_MODEL)),
        ],
        out_shape=[
            jax.ShapeDtypeStruct((s, D_FF), BF16),
            jax.ShapeDtypeStruct((s, D_FF), BF16),
            jax.ShapeDtypeStruct((s, D_MODEL), F32),
            jax.ShapeDtypeStruct((s, D_MODEL), BF16),
            jax.ShapeDtypeStruct((1, D_MODEL), F32),
            jax.ShapeDtypeStruct((D_FF, D_MODEL), F32),
        ],
        scratch_shapes=[pltpu.VMEM((ts, D_FF), BF16)],
        rider=rider,
    )


def _ffn_bwd_w(dgt, dup, hn2, rider=None):
    s = hn2.shape[0]
    slab = pl.BlockSpec((s, FF_CHUNK), lambda i: (0, i))

    def body(dgt_ref, dup_ref, hn_ref, dwg_ref, dwu_ref):
        hn = hn_ref[...]
        dwg_ref[...] = _tn(dgt_ref[...], hn)
        dwu_ref[...] = _tn(dup_ref[...], hn)

    return _call(
        body,
        (dgt, dup, hn2),
        name="ffn_bwd_w",
        grid=(D_FF // FF_CHUNK,),
        in_specs=[slab, slab, _resident((s, D_MODEL))],
        out_specs=[_rows(FF_CHUNK, D_MODEL)] * 2,
        out_shape=[jax.ShapeDtypeStruct((D_FF, D_MODEL), F32)] * 2,
        rider=rider,
    )


def _mix_bwd(dh1b, u, w_out, w_pool, pool_scale, rider=None):
    s = u.shape[0]
    ts = min(TOKEN_TILE, s)
    nt = s // ts
    halo_after = lambda i: (jnp.minimum((i + 1) * (ts // POOL_HALO), s // POOL_HALO - 1), 0)
    n_groups = len(POOL_SIZES)

    def body(dh_ref, dhn_ref, u_ref, uh_ref, wo_ref, wp_ref, sc_ref, da_ref, du_ref, dwp_ref, dsc_ref):
        i = pl.program_id(0)

        @pl.when(i == 0)
        def _():
            dwp_ref[...] = jnp.zeros_like(dwp_ref)
            dsc_ref[...] = jnp.zeros_like(dsc_ref)

        dh = dh_ref[...]
        da_ref[...] = _nt(dh, wo_ref[:ATTN_WIDTH, :])
        dh_next = jnp.where(i < nt - 1, dhn_ref[...], jnp.zeros_like(dhn_ref))
        dm_ext = _nt(jnp.concatenate([dh, dh_next], axis=0), wo_ref[ATTN_WIDTH:, :])
        pooled = _pooled(u_ref[...], uh_ref[...], i, ts)
        t_ext = i * ts + lax.broadcasted_iota(jnp.int32, (ts + POOL_HALO, 1), 0)
        for g, w in enumerate(POOL_SIZES):
            lanes = slice(g * POOL_GROUP, (g + 1) * POOL_GROUP)
            wp = wp_ref[g].astype(BF16)
            pg = pooled[g].astype(BF16)
            dm_g = dm_ext[:, lanes]
            dsc_ref[:, lanes] += jnp.sum(dm_g[:ts, :] * _nn(pg, wp), axis=0, keepdims=True)
            dy = (dm_g * sc_ref[:, lanes]).astype(BF16)
            dwp_ref[g] += _tn(pg, dy[:ts, :])
            d_pool = _nt(dy, wp)
            acc = d_pool / jnp.minimum(t_ext + 1, w).astype(F32)
            shift = 1
            while shift < w:
                acc = acc + pltpu.roll(acc, ts + POOL_HALO - shift, 0)
                shift *= 2
            du_ref[:, lanes] = acc[:ts, :] - d_pool[:ts, :]

    return _call(
        body,
        (dh1b, dh1b, u, u, w_out, w_pool, pool_scale),
        name="mix_bwd",
        grid=(nt,),
        in_specs=[
            _rows(ts, D_MODEL),
            pl.BlockSpec((POOL_HALO, D_MODEL), halo_after),
            _rows(ts, POOL_WIDTH),
            pl.BlockSpec((POOL_HALO, POOL_WIDTH), _halo_before(ts)),
            _resident((D_MODEL, D_MODEL)),
            _resident((n_groups, POOL_GROUP, POOL_GROUP)),
            _resident((1, POOL_WIDTH)),
        ],
        out_specs=[
            _rows(ts, ATTN_WIDTH),
            _rows(ts, POOL_WIDTH),
            _acc((n_groups, POOL_GROUP, POOL_GROUP)),
            _acc((1, POOL_WIDTH)),
        ],
        out_shape=[
            jax.ShapeDtypeStruct((s, ATTN_WIDTH), F32),
            jax.ShapeDtypeStruct((s, POOL_WIDTH), F32),
            jax.ShapeDtypeStruct((n_groups, POOL_GROUP, POOL_GROUP), F32),
            jax.ShapeDtypeStruct((1, POOL_WIDTH), F32),
        ],
        rider=rider,
    )


def _out_w_bwd(a, m, dh1b, rider=None):
    s = dh1b.shape[0]

    def body(a_ref, m_ref, dh_ref, dw_ref):
        @pl.when(pl.program_id(0) == 0)
        def _():
            dw_ref[...] = _tn(a_ref[...], dh_ref[...])

        @pl.when(pl.program_id(0) == 1)
        def _():
            dw_ref[...] = _tn(m_ref[...], dh_ref[...])

    return _call(
        body,
        (a, m, dh1b),
        name="out_w_bwd",
        grid=(2,),
        in_specs=[_resident((s, ATTN_WIDTH)), _resident((s, POOL_WIDTH)), _resident((s, D_MODEL))],
        out_specs=[_rows(ATTN_WIDTH, D_MODEL)],
        out_shape=[jax.ShapeDtypeStruct((D_MODEL, D_MODEL), F32)],
        rider=rider,
    )


def _attn_bwd(qn, kn, v, a, da, tab, sinks, rider=None):
    s = qn.shape[0]
    qb = ATTN_STEP_BLOCKS
    rows = qb * BLOCK
    n_groups = s // rows
    group, prev = _attn_specs(n_groups)
    done = lambda n: (jnp.maximum(n - 1, 0), 0)

    def body(sink_ref, q_ref, kc_ref, kp_ref, vc_ref, vp_ref, o_ref, do_ref, tab_ref,
             dq_ref, dk_ref, dv_ref, dl_ref, ds_ref, k_carry, v_carry, sink_acc):
        n = pl.program_id(0)

        @pl.when(n == 0)
        def _():
            dl_ref[...] = jnp.zeros_like(dl_ref)
            k_carry[...] = jnp.zeros_like(k_carry)
            v_carry[...] = jnp.zeros_like(v_carry)
            sink_acc[...] = jnp.zeros_like(sink_acc)

        @pl.when(n < n_groups)
        def _():
            first = n == 0
            lo_mask = _lane_lo((BLOCK, BLOCK))
            dks, dvs = [], []
            for b in range(qb):
                at = slice(b * BLOCK, (b + 1) * BLOCK)
                keys = _band(kp_ref, kc_ref, b)
                vals = _band(vp_ref, vc_ref, b)
                q_st = _stack_heads([q_ref[at, p * BLOCK:(p + 1) * BLOCK] for p in range(4)], lo_mask)
                do_st = _stack_heads([do_ref[at, p * BLOCK:(p + 1) * BLOCK] for p in range(4)], lo_mask)
                o_st = _stack_heads([o_ref[at, p * BLOCK:(p + 1) * BLOCK].astype(F32) for p in range(4)], lo_mask)
                dq_st, dk_parts, dv_parts = [], [], []
                for half, heads in enumerate((HEADS_A, HEADS_B)):
                    probs, p_sink = _band_probs(q_st[half], keys[half], tab_ref[half], _sink_column(sink_ref, heads),
                                                first if b == 0 else None)
                    delta = jnp.sum(do_st[half] * o_st[half], axis=-1, keepdims=True)
                    dob = do_st[half].astype(BF16)
                    dl = probs * (_nt(dob, vals[half]) - delta)
                    dl_ref[half] += dl
                    sink_acc[half] += p_sink * delta
                    dsb = (dl * (HEAD_DIM ** -0.5)).astype(BF16)
                    dq_st.append(_nn(dsb, keys[half]))
                    dk_parts.append(_tn(dsb, q_st[half]))
                    dv_parts.append(_tn(probs.astype(BF16), dob))
                dq = _unstack_heads(dq_st[0], dq_st[1], lo_mask)
                for p in range(4):
                    dq_ref[at, p * BLOCK:(p + 1) * BLOCK] = dq[p]
                dks.append(dk_parts[0] + pltpu.roll(dk_parts[1], HEAD_DIM, 1))
                dvs.append(dv_parts[0] + pltpu.roll(dv_parts[1], HEAD_DIM, 1))
            last = slice((qb - 1) * BLOCK, qb * BLOCK)
            for parts, out_ref, carry in ((dks, dk_ref, k_carry), (dvs, dv_ref, v_carry)):
                out_ref[...] = carry[...]
                out_ref[last, :] += parts[0][:BLOCK, :]
                for b in range(qb):
                    own = parts[b][BLOCK:, :]
                    carry[b * BLOCK:(b + 1) * BLOCK, :] = own + parts[b + 1][:BLOCK, :] if b + 1 < qb else own

        @pl.when(n == n_groups)
        def _():
            dk_ref[...] = k_carry[...]
            dv_ref[...] = v_carry[...]
            for half, heads in enumerate((HEADS_A, HEADS_B)):
                for slot, h in enumerate(heads):
                    tot = jnp.sum(sink_acc[half, slot * BLOCK:(slot + 1) * BLOCK, :], axis=0, keepdims=True)
                    ds_ref[h:h + 1, :] = jnp.broadcast_to(-tot, (1, SMALL_LANES))

    return _call(
        body,
        (sinks, qn, kn, kn, v, v, a, da, tab),
        name="attn_bwd",
        grid=(n_groups + 1,),
        in_specs=[
            pl.BlockSpec(memory_space=pltpu.SMEM),
            pl.BlockSpec((rows, ATTN_WIDTH), group),
            pl.BlockSpec((rows, KV_WIDTH), group),
            pl.BlockSpec((BLOCK, KV_WIDTH), prev),
            pl.BlockSpec((rows, KV_WIDTH), group),
            pl.BlockSpec((BLOCK, KV_WIDTH), prev),
            pl.BlockSpec((rows, ATTN_WIDTH), group),
            pl.BlockSpec((rows, ATTN_WIDTH), group),
            _resident((2, 4 * BLOCK, 2 * BLOCK)),
        ],
        out_specs=[
            pl.BlockSpec((rows, ATTN_WIDTH), group),
            pl.BlockSpec((rows, KV_WIDTH), done),
            pl.BlockSpec((rows, KV_WIDTH), done),
            _acc((2, 4 * BLOCK, 2 * BLOCK)),
            _acc((N_DEV, SMALL_LANES)),
        ],
        out_shape=[
            jax.ShapeDtypeStruct((s, ATTN_WIDTH), F32),
            jax.ShapeDtypeStruct((s, KV_WIDTH), F32),
            jax.ShapeDtypeStruct((s, KV_WIDTH), F32),
            jax.ShapeDtypeStruct((2, 4 * BLOCK, 2 * BLOCK), F32),
            jax.ShapeDtypeStruct((N_DEV, SMALL_LANES), F32),
        ],
        scratch_shapes=[
            pltpu.VMEM((rows, KV_WIDTH), F32),
            pltpu.VMEM((rows, KV_WIDTH), F32),
            pltpu.VMEM((2, 4 * BLOCK, 1), F32),
        ],
        rider=rider,
    )


def _fold_heads(acc):
    t = acc + pltpu.roll(acc, HEAD_DIM, 1)
    out = t[:, :SMALL_LANES]
    for g in range(1, acc.shape[1] // SMALL_LANES):
        out = out + t[:, g * SMALL_LANES:(g + 1) * SMALL_LANES]
    return out


def _in_proj_bwd(dqn, dkn, dv, du, zqk, x, dh1, g_attn, gq_t, gk_t, w_in_t, rider=None):
    s = x.shape[0]
    ts = min(TOKEN_TILE, s)
    nt = s // ts

    def head_norm_bwd(d_n, raw, g_t, bmat):
        r = lax.rsqrt(_seg_mean(raw * raw, bmat) + EPS)
        gy = d_n * g_t
        d_raw = r * gy - raw * (r * r * r) * _seg_mean(gy * raw, bmat)
        return d_raw, jnp.sum(d_n * (raw * r), axis=0, keepdims=True)

    def body(dqn_ref, dkn_ref, dv_ref, du_ref, zqk_ref, x_ref, dh1_ref, g_ref, gq_ref, gk_ref, w_ref, bq_ref, bk_ref,
             gx_ref, dw_ref, dg_ref, dgq_ref, dgk_ref, dz_ref, gq_acc, gk_acc):
        i = pl.program_id(0)

        @pl.when(i == 0)
        def _():
            dw_ref[...] = jnp.zeros_like(dw_ref)
            dg_ref[...] = jnp.zeros_like(dg_ref)
            gq_acc[...] = jnp.zeros_like(gq_acc)
            gk_acc[...] = jnp.zeros_like(gk_acc)

        d_q, d_gq = head_norm_bwd(dqn_ref[...], zqk_ref[:, :ATTN_WIDTH], gq_ref[...], bq_ref[...])
        d_k, d_gk = head_norm_bwd(dkn_ref[...], zqk_ref[:, ATTN_WIDTH:], gk_ref[...], bk_ref[...])
        gq_acc[...] += d_gq
        gk_acc[...] += d_gk
        dz_ref[:, :ATTN_WIDTH] = d_q.astype(BF16)
        dz_ref[:, ATTN_WIDTH:ATTN_WIDTH + KV_WIDTH] = d_k.astype(BF16)
        dz_ref[:, ATTN_WIDTH + KV_WIDTH:ATTN_WIDTH + 2 * KV_WIDTH] = dv_ref[...].astype(BF16)
        dz_ref[:, ATTN_WIDTH + 2 * KV_WIDTH:] = du_ref[...].astype(BF16)
        dz = dz_ref[...]
        xf = x_ref[...]
        r = _rms(xf)
        hn = ((xf * r) * g_ref[...]).astype(BF16)
        dw_ref[...] += _tn(dz, hn)
        d_x, d_g = _rms_bwd(_nn(dz, w_ref[...]), xf, r, g_ref[...])
        dg_ref[...] += d_g
        gx_ref[...] = dh1_ref[...] + d_x

        @pl.when(i == nt - 1)
        def _():
            dgq_ref[...] = _fold_heads(gq_acc[...])
            dgk_ref[...] = _fold_heads(gk_acc[...])

    return _call(
        body,
        (dqn, dkn, dv, du, zqk, x, dh1, g_attn, gq_t, gk_t, w_in_t,
      _head_mean_matrix(ATTN_WIDTH), _head_mean_matrix(KV_WIDTH)),
        name="in_proj_bwd",
        grid=(nt,),
        in_specs=[
            _rows(ts, ATTN_WIDTH),
            _rows(ts, KV_WIDTH),
            _rows(ts, KV_WIDTH),
            _rows(ts, POOL_WIDTH),
            _rows(ts, ATTN_WIDTH + KV_WIDTH),
            _rows(ts, D_MODEL),
            _rows(ts, D_MODEL),
            _resident((1, D_MODEL)),
            _resident((1, ATTN_WIDTH)),
            _resident((1, KV_WIDTH)),
            _resident((IN_WIDTH, D_MODEL)),
            _resident((ATTN_WIDTH, ATTN_WIDTH)),
            _resident((KV_WIDTH, KV_WIDTH)),
        ],
        out_specs=[
            _rows(ts, D_MODEL),
            _acc((IN_WIDTH, D_MODEL)),
            _acc((1, D_MODEL)),
            _acc((1, SMALL_LANES)),
            _acc((1, SMALL_LANES)),
        ],
        out_shape=[
            jax.ShapeDtypeStruct((s, D_MODEL), F32),
            jax.ShapeDtypeStruct((IN_WIDTH, D_MODEL), F32),
            jax.ShapeDtypeStruct((1, D_MODEL), F32),
            jax.ShapeDtypeStruct((1, SMALL_LANES), F32),
            jax.ShapeDtypeStruct((1, SMALL_LANES), F32),
        ],
        scratch_shapes=[
            pltpu.VMEM((ts, IN_WIDTH), BF16),
            pltpu.VMEM((1, ATTN_WIDTH), F32),
            pltpu.VMEM((1, KV_WIDTH), F32),
        ],
        rider=rider,
    )


BIG_WEIGHTS = (
    ("w_in", True, IN_WIDTH // N_DEV, D_MODEL),
    ("w_out", False, D_MODEL // N_DEV, D_MODEL),
    ("w_gate", True, D_FF // N_DEV, D_MODEL),
    ("w_up", True, D_FF // N_DEV, D_MODEL),
    ("w_down", False, D_FF // N_DEV, D_MODEL),
    ("w_ple_gate", False, D_MODEL // N_DEV, D_MODEL),
    ("w_ple_proj", False, PLE_DIM, D_MODEL // N_DEV),
)
N_BIG = len(BIG_WEIGHTS)


def _place():
    x, y, c = lax.axis_index("x"), lax.axis_index("y"), lax.axis_index("c")
    chips = [(1 - x, y), (x, 1 - y), (1 - x, 1 - y)]
    return x, y, c, chips


class _Gather:
    def __init__(self, n, rows=None):
        self.n = n
        self.rows = rows or [None] * n
        self.sems = [pltpu.SemaphoreType.DMA((n, 7)), pltpu.SemaphoreType.DMA((n, 7)), pltpu.SemaphoreType.DMA((n,))]

    def _ctx(self, srcs, outs, sems):
        send_sems, recv_sems, local_sems = sems
        x, y, c, chips = _place()
        me, sibling = (x, y, c), (x, y, 1 - c)

        def part(k, ref):
            return ref if self.rows[k] is None else ref.at[pl.ds(*self.rows[k]), :]

        def block(k, owner):
            px, py, pc = owner
            return part(k, outs[k].at[4 * px + 2 * py + pc])

        def copy(k, idx, owner, to, mine=False):
            return pltpu.make_async_remote_copy(
                src_ref=part(k, srcs[k]) if mine else block(k, owner), dst_ref=block(k, owner),
                send_sem=send_sems.at[k, idx], recv_sem=recv_sems.at[k, idx], device_id=to, device_id_type=MESH)

        def local(k):
            return pltpu.make_async_copy(part(k, srcs[k]), block(k, me), local_sems.at[k])

        return c, chips, me, sibling, copy, local

    def begin(self, srcs, outs, sems):
        c, chips, me, sibling, copy, local = self._ctx(srcs, outs, sems)
        for k in range(self.n):
            local(k).start()
            copy(k, 0, me, sibling, mine=True).start()
            for j, chip in enumerate(chips):
                copy(k, 1 + j, me, (*chip, c), mine=True).start()

    def middle(self, srcs, outs, sems):
        c, chips, me, sibling, copy, local = self._ctx(srcs, outs, sems)
        for j, chip in enumerate(chips):
            for k in range(self.n):
                copy(k, 1 + j, (*chip, c), me).wait_recv()
                copy(k, 4 + j, (*chip, c), sibling).start()

    def end(self, srcs, outs, sems):
        c, chips, me, sibling, copy, local = self._ctx(srcs, outs, sems)
        for k in range(self.n):
            copy(k, 0, sibling, me).wait_recv()
            for j, chip in enumerate(chips):
                copy(k, 4 + j, (*chip, 1 - c), me).wait_recv()
        for k in range(self.n):
            copy(k, 0, me, sibling, mine=True).wait_send()
            for j, chip in enumerate(chips):
                copy(k, 1 + j, me, (*chip, c), mine=True).wait_send()
                copy(k, 4 + j, (*chip, c), sibling).wait_send()
            local(k).wait()


def _gather_rider(items):
    items = [it if isinstance(it, tuple) else (it, None, None, None) for it in items]
    n = len(items)
    g = _Gather(n, [None if r0 is None else (r0, nr) for _, r0, nr, _ in items])
    shapes = [jax.ShapeDtypeStruct((N_DEV, *sh.shape), sh.dtype) for sh, _, _, _ in items]
    stacks = [(k, st) for k, (_, _, _, st) in enumerate(items) if st is not None]
    aliases = {n + i: k for i, (k, _) in enumerate(stacks)}
    return _Rider([sh for sh, _, _, _ in items] + [st for _, st in stacks], shapes, g.sems, g.begin, g.end, g.middle,
                  aliases=aliases)


def _cast_and_gather_first(shards):
    g = _Gather(1)
    any_spec = pl.BlockSpec(memory_space=pl.ANY)
    vmem = pl.BlockSpec(memory_space=pltpu.VMEM)

    def body(*refs):
        ins, outs, gathered, sems = refs[:N_BIG], refs[N_BIG:2 * N_BIG], refs[2 * N_BIG], refs[2 * N_BIG + 1:]
        outs[0][...] = ins[0][...].astype(BF16)
        g.begin(outs[:1], [gathered], sems)
        for k in range(1, N_BIG):
            outs[k][...] = ins[k][...].astype(BF16)
        g.middle(outs[:1], [gathered], sems)
        g.end(outs[:1], [gathered], sems)

    res = pl.pallas_call(
        body,
        name="cast_and_gather_first",
        in_specs=[vmem] * N_BIG,
        out_specs=[vmem] * N_BIG + [any_spec],
        out_shape=[jax.ShapeDtypeStruct((r, c), BF16) for _, _, r, c in BIG_WEIGHTS]
        + [jax.ShapeDtypeStruct((N_DEV, *BIG_WEIGHTS[0][2:]), BF16)],
        scratch_shapes=g.sems,
    )(*shards)
    return list(res[:N_BIG]), res[N_BIG]


def _sibling_rider(grads):
    n = len(grads)

    def copies(gs, lands, sems):
        send_sems, recv_sems = sems
        x, y, c, _ = _place()
        return [
            pltpu.make_async_remote_copy(
                src_ref=gs[k].at[:, 1 - c], dst_ref=lands[k], send_sem=send_sems.at[k], recv_sem=recv_sems.at[k],
                device_id=(x, y, 1 - c), device_id_type=MESH)
            for k in range(n)
        ]

    def begin(gs, lands, sems):
        for cp in copies(gs, lands, sems):
            cp.start()

    def end(gs, lands, sems):
        for cp in copies(gs, lands, sems):
            cp.wait()

    shapes = [jax.ShapeDtypeStruct((N_CHIPS, *g.shape[2:]), F32) for g in grads]
    return _Rider(grads, shapes, [pltpu.SemaphoreType.DMA((n,)), pltpu.SemaphoreType.DMA((n,))], begin, end)


def _chip_of_relation(j, place):
    x, y = place[0], place[1]
    return jnp.where(j == 0, 2 * (1 - x) + y, jnp.where(j == 1, 2 * x + 1 - y, 2 * (1 - x) + 1 - y))


def _chip_sum(ks, place, grads, from_sibling):
    n = len(ks)
    shapes = [BIG_WEIGHTS[k][2:] for k in ks]
    operands, specs = [], []
    for (r, c), g, l in zip(shapes, grads, from_sibling):
        operands += [g, l]
        specs += [pl.BlockSpec((1, 1, r, c), lambda j, place: (_chip_of_relation(j, place), place[2], 0, 0)),
                  pl.BlockSpec((1, r, c), lambda j, place: (_chip_of_relation(j, place), 0, 0))]
    args, in_specs, _ = _after_last(operands, specs)

    def body(place_ref, *refs):
        ins, outs = refs[:2 * n], refs[len(args):]
        for i in range(n):
            outs[i][0] = (ins[2 * i][0, 0] + ins[2 * i + 1][0]).astype(BF16)

    outs = pl.pallas_call(
        body,
        name="chip_sum_" + "_".join(BIG_WEIGHTS[k][0] for k in ks),
        grid_spec=pltpu.PrefetchScalarGridSpec(
            num_scalar_prefetch=1,
            grid=(N_CHIPS - 1,),
            in_specs=in_specs,
            out_specs=[pl.BlockSpec((1, r, c), lambda j, place: (j, 0, 0)) for r, c in shapes],
        ),
        out_shape=[jax.ShapeDtypeStruct((N_CHIPS - 1, r, c), BF16) for r, c in shapes],
    )(place, *args)
    _mark_issued(outs[0])
    return list(outs)


def _chips_rider(to_send, small=None):
    n = len(to_send)
    inputs = list(to_send) + ([] if small is None else [small])
    shapes = [jax.ShapeDtypeStruct((3, *t.shape[1:]), BF16) for t in to_send]
    sems = [pltpu.SemaphoreType.DMA((max(n, 1), 3)), pltpu.SemaphoreType.DMA((max(n, 1), 3))]
    if small is not None:
        shapes.append(jax.ShapeDtypeStruct((N_DEV, *small.shape), F32))
        sems += [pltpu.SemaphoreType.DMA((7,)), pltpu.SemaphoreType.DMA((7,)), pltpu.SemaphoreType.DMA]

    def copies(ins, outs, sem_refs):
        x, y, c, chips = _place()
        out = []
        for k in range(n):
            for j, (px, py) in enumerate(chips):
                out.append(pltpu.make_async_remote_copy(
                    src_ref=ins[k].at[j], dst_ref=outs[k].at[j],
                    send_sem=sem_refs[0].at[k, j], recv_sem=sem_refs[1].at[k, j],
                    device_id=(px, py, c), device_id_type=MESH))
        local = None
        if small is not None:
            me = 4 * x + 2 * y + c
            local = pltpu.make_async_copy(ins[n], outs[n].at[me], sem_refs[4])
            rel = 0
            for fx in (0, 1):
                for fy in (0, 1):
                    for fc in (0, 1):
                        if (fx, fy, fc) != (0, 0, 0):
                            out.append(pltpu.make_async_remote_copy(
                                src_ref=ins[n], dst_ref=outs[n].at[me],
                                send_sem=sem_refs[2].at[rel], recv_sem=sem_refs[3].at[rel],
                                device_id=(x ^ fx, y ^ fy, c ^ fc), device_id_type=MESH))
                            rel += 1
        return out, local

    def begin(ins, outs, sem_refs):
        remote, local = copies(ins, outs, sem_refs)
        if local is not None:
            local.start()
        for cp in remote:
            cp.start()

    def end(ins, outs, sem_refs):
        remote, local = copies(ins, outs, sem_refs)
        for cp in remote:
            cp.wait()
        if local is not None:
            local.wait()

    return _Rider(inputs, shapes, sems, begin, end)


def _exchange(name, rider):
    return _call(lambda: None, (), name=name, grid=(1,), in_specs=[], out_specs=[], out_shape=[], rider=rider)[1]


PEER_SETS = {"sibling": 1, "chips": 2, "sibling+chips": 3, "all": 4}


def _peers(pattern):
    x, y, c, chips = _place()
    sibling, others = [(x, y, 1 - c)], [(*chip, c) for chip in chips]
    if pattern == "all":
        return sibling + others + [(*chip, 1 - c) for chip in chips]
    return {"sibling": sibling, "chips": others, "sibling+chips": sibling + others}[pattern]


def _on_sequencer(name, pattern, rider):
    assert not rider.aliases
    n_in, n_out = len(rider.inputs), len(rider.out_shapes)

    def body(*refs):
        ins, outs, sems = refs[:n_in], refs[n_in:n_in + n_out], refs[n_in + n_out:]
        peers = _peers(pattern)
        barrier = pltpu.get_barrier_semaphore()
        for peer in peers:
            pl.semaphore_signal(barrier, inc=1, device_id=peer, device_id_type=MESH)
        pl.semaphore_wait(barrier, len(peers))
        rider.begin(ins, outs, sems)
        if rider.middle is not None:
            rider.middle(ins, outs, sems)
        rider.end(ins, outs, sems)

    outs = pl.kernel(
        body,
        name=name,
        out_type=tuple(rider.out_shapes),
        mesh=plsc.ScalarSubcoreMesh(axis_name="sequencer", num_cores=1),
        scratch_types=tuple(rider.sems),
        compiler_params=pltpu.CompilerParams(collective_id=PEER_SETS[pattern]),
    )(*rider.inputs)
    return list(outs)


def _merge_riders(*riders):
    riders = [r for r in riders if r is not None]
    if len(riders) == 1:
        return riders[0]
    assert not any(r.aliases for r in riders)

    def split(refs, counts):
        out, at = [], 0
        for n in counts:
            out.append(refs[at:at + n])
            at += n
        return out

    def run(which):
        def fn(ins, outs, sems):
            parts = zip(riders, split(ins, [len(r.inputs) for r in riders]),
                        split(outs, [len(r.out_shapes) for r in riders]), split(sems, [len(r.sems) for r in riders]))
            for r, i, o, s in parts:
                hook = getattr(r, which)
                if hook is not None:
                    hook(i, o, s)
        return fn

    middle = run("middle") if any(r.middle is not None for r in riders) else None
    return _Rider(sum((r.inputs for r in riders), []), sum((r.out_shapes for r in riders), []),
                  sum((r.sems for r in riders), []), run("begin"), run("end"), middle)


def _split_outputs(outs, *riders):
    res, at = [], 0
    for r in riders:
        res.append(outs[at:at + len(r.out_shapes)])
        at += len(r.out_shapes)
    return res


def _adamw(w, g, m, v):
    m = ADAM_B1 * m + (1.0 - ADAM_B1) * g
    v = ADAM_B2 * v + (1.0 - ADAM_B2) * jnp.square(g)
    m_hat = m / (1.0 - ADAM_B1 ** ADAM_STEP)
    v_hat = v / (1.0 - ADAM_B2 ** ADAM_STEP)
    delta = -ADAM_LR * (m_hat / (jnp.sqrt(v_hat) + ADAM_EPS) + ADAM_WD * w)
    return delta, m, v


def _adamw_big(ks, place, operands):
    n = len(ks)
    tiles = lambda i, place: (i, 0)
    in_specs, out_specs, out_shape = [], [], []
    for k in ks:
        _, _, r, c = BIG_WEIGHTS[k]
        tile = r // 2
        in_specs += [
            pl.BlockSpec((1, 1, tile, c), lambda i, place: (2 * place[0] + place[1], place[2], i, 0)),
            pl.BlockSpec((1, tile, c), lambda i, place: (2 * place[0] + place[1], i, 0)),
            pl.BlockSpec((3, tile, c), lambda i, place: (0, i, 0)),
        ] + [pl.BlockSpec((tile, c), tiles)] * 3
        out_specs += [pl.BlockSpec((tile, c), tiles)] * 4
        out_shape += [jax.ShapeDtypeStruct((r, c), F32)] * 4
    args, in_specs, _ = _after_last(sum((list(ops) for ops in operands), []), in_specs)

    def body(place_ref, *refs):
        ins, outs = refs[:6 * n], refs[len(args):]
        for i in range(n):
            mine_ref, sib_ref, land_ref, w_ref, m_ref, v_ref = ins[6 * i:6 * i + 6]
            g_ref, d_ref, nm_ref, nv_ref = outs[4 * i:4 * i + 4]
            g = mine_ref[0, 0] + sib_ref[0]
            g = ((g + land_ref[0].astype(F32)) + land_ref[1].astype(F32)) + land_ref[2].astype(F32)
            g_ref[...] = g
            d_ref[...], nm_ref[...], nv_ref[...] = _adamw(w_ref[...], g, m_ref[...], v_ref[...])

    outs = pl.pallas_call(
        body,
        name="adamw_" + "_".join(BIG_WEIGHTS[k][0] for k in ks),
        grid_spec=pltpu.PrefetchScalarGridSpec(
            num_scalar_prefetch=1, grid=(2,), in_specs=in_specs, out_specs=out_specs),
        out_shape=out_shape,
    )(place, *args)
    _mark_issued(outs[0])
    return [outs[4 * i:4 * i + 4] for i in range(n)]


def _pack_small(arrays):
    rows, offsets = [], []
    at = 0
    for a in arrays:
        if a.ndim != 2 or a.shape[1] != SMALL_LANES or a.shape[0] % 8:
            flat = a.reshape(-1)
            n_rows = -(-flat.shape[0] // (8 * SMALL_LANES)) * 8
            a = jnp.pad(flat, (0, n_rows * SMALL_LANES - flat.shape[0])).reshape(n_rows, SMALL_LANES)
        rows.append(a)
        offsets.append(at)
        at += a.shape[0]
    return jnp.concatenate(rows, axis=0), offsets


def _unpack_small(tot, at, shape):
    r, c = shape
    if r % 8 == 0:
        return tot[at:at + r, :c]
    assert r == 1
    if c <= SMALL_LANES:
        return tot[at:at + 1, :c]
    return jnp.concatenate([tot[at + j:at + j + 1, :] for j in range(c // SMALL_LANES)], axis=1)


def _small_update(packs, loss_at, grads_at, ws, ms, vs):
    n, n_packs = len(ws), len(packs)

    def body(*refs):
        pack_refs, refs = refs[:n_packs], refs[n_packs:]
        w_refs, m_refs, v_refs, loss_ref, outs = refs[:n], refs[n:2 * n], refs[2 * n:3 * n], refs[3 * n], refs[3 * n + 1:]
        tots = []
        for p_ref in pack_refs:
            tot = p_ref[0]
            for j in range(1, N_DEV):
                tot = tot + p_ref[j]
            tots.append(tot)
        loss_ref[...] = _unpack_small(tots[loss_at[0]], loss_at[1], (1, 1))
        for i, (pack, at) in enumerate(grads_at):
            g = _unpack_small(tots[pack], at, w_refs[i].shape)
            outs[i][...] = g
            outs[n + i][...], outs[2 * n + i][...], outs[3 * n + i][...] = _adamw(
                w_refs[i][...], g, m_refs[i][...], v_refs[i][...])

    shapes = [jax.ShapeDtypeStruct(w.shape, F32) for w in ws]
    outs = pl.pallas_call(body, name="small_update", out_shape=[jax.ShapeDtypeStruct((1, 1), F32)] + shapes * 4)(
        *packs, *ws, *ms, *vs)
    return outs[0], outs[1:]


SMALL_NAMES = ("g_attn_norm", "g_q", "g_k", "attn_sinks", "rel_bias", "w_pool", "pool_scale", "g_ffn_norm", "g_ple_norm")


def kernel(x, p, w_in, w_out, g_attn_norm, g_q, g_k, attn_sinks, rel_bias, w_pool, pool_scale, g_ffn_norm, w_gate, w_up, w_down, g_ple_norm, w_ple_gate, w_ple_proj, loss_target, m_w_in, m_w_out, m_g_attn_norm, m_g_q, m_g_k, m_attn_sinks, m_rel_bias, m_w_pool, m_pool_scale, m_g_ffn_norm, m_w_gate, m_w_up, m_w_down, m_g_ple_norm, m_w_ple_gate, m_w_ple_proj, v_w_in, v_w_out, v_g_attn_norm, v_g_q, v_g_k, v_attn_sinks, v_rel_bias, v_w_pool, v_pool_scale, v_g_ffn_norm, v_w_gate, v_w_up, v_w_down, v_g_ple_norm, v_w_ple_gate, v_w_ple_proj):
    weights = dict(w_in=w_in, w_out=w_out, g_attn_norm=g_attn_norm, g_q=g_q, g_k=g_k, attn_sinks=attn_sinks,
                   rel_bias=rel_bias, w_pool=w_pool, pool_scale=pool_scale, g_ffn_norm=g_ffn_norm, w_gate=w_gate,
                   w_up=w_up, w_down=w_down, g_ple_norm=g_ple_norm, w_ple_gate=w_ple_gate, w_ple_proj=w_ple_proj)
    m_in = dict(w_in=m_w_in, w_out=m_w_out, g_attn_norm=m_g_attn_norm, g_q=m_g_q, g_k=m_g_k, attn_sinks=m_attn_sinks,
                rel_bias=m_rel_bias, w_pool=m_w_pool, pool_scale=m_pool_scale, g_ffn_norm=m_g_ffn_norm, w_gate=m_w_gate,
                w_up=m_w_up, w_down=m_w_down, g_ple_norm=m_g_ple_norm, w_ple_gate=m_w_ple_gate, w_ple_proj=m_w_ple_proj)
    v_in = dict(w_in=v_w_in, w_out=v_w_out, g_attn_norm=v_g_attn_norm, g_q=v_g_q, g_k=v_g_k, attn_sinks=v_attn_sinks,
                rel_bias=v_rel_bias, w_pool=v_w_pool, pool_scale=v_pool_scale, g_ffn_norm=v_g_ffn_norm, w_gate=v_w_gate,
                w_up=v_w_up, w_down=v_w_down, g_ple_norm=v_g_ple_norm, w_ple_gate=v_w_ple_gate, w_ple_proj=v_w_ple_proj)

    _issued.clear()
    xs = x[0]
    ps = p[0, 0]
    target = loss_target[0]
    wp = w_pool[0]
    gq_t = jnp.tile(g_q, (1, ATTN_WIDTH // HEAD_DIM))
    gk_t = jnp.tile(g_k, (1, KV_WIDTH // HEAD_DIM))

    def to_blocks(k, arr):
        return jnp.swapaxes(arr[0], 0, 1) if BIG_WEIGHTS[k][1] else arr[0]

    def from_blocks(k, arr):
        return (jnp.swapaxes(arr, 0, 1) if BIG_WEIGHTS[k][1] else arr)[None]

    IN, OUT, GATE, UP, DOWN, PG, PP = range(N_BIG)
    full = lambda g: g.reshape(N_DEV * g.shape[1], g.shape[2])
    halves = lambda k, g: g.reshape(N_CHIPS, 2, *BIG_WEIGHTS[k][2:])
    place = jnp.stack([lax.axis_index("x"), lax.axis_index("y"), lax.axis_index("c")]).astype(jnp.int32)

    sh, w_in_g = _cast_and_gather_first([to_blocks(k, weights[name]) for k, (name, _, _, _) in enumerate(BIG_WEIGHTS)])
    w_in_t = full(w_in_g)

    (w_out_g,) = _on_sequencer("gather_out", "sibling+chips", _gather_rider([sh[OUT]]))
    wg_g, wu_g = _on_sequencer("gather_gate_up", "sibling+chips", _gather_rider([sh[GATE], sh[UP]]))
    wd_g, w_pg_g, w_pp_g = _on_sequencer("gather_down_ple", "sibling+chips", _gather_rider([sh[DOWN], sh[PG], sh[PP]]))
    tab = _bias_table(rel_bias.T)
    (zqk, qn, kn, v, u), _ = _in_proj(xs, g_attn_norm, w_in_t, gq_t, gk_t)
    (a,), _ = _attn_fwd(qn, kn, v, tab, attn_sinks)
    w_out_f = full(w_out_g)
    (h1, hn2, m_out), _ = _mix_out(u, a, xs, w_out_f, wp, pool_scale, g_ffn_norm)
    wg_t, wu_t = full(wg_g), full(wu_g)
    (gt, up), _ = _ffn_up(hn2, wg_t, wu_t)
    w_down_f = full(wd_g)

    partial, from_sibling, sums, landed = [None] * N_BIG, [None] * N_BIG, [None] * N_BIG, [None] * N_BIG

    def to_sibling(name, ks, grads):
        for k, g in zip(ks, grads):
            partial[k] = halves(k, g)
        got = _on_sequencer(name, "sibling", _sibling_rider([partial[k] for k in ks]))
        for k, g in zip(ks, got):
            from_sibling[k] = g

    def chip_sum(*ks):
        for k, s in zip(ks, _chip_sum(ks, place, [partial[k] for k in ks], [from_sibling[k] for k in ks])):
            sums[k] = s

    def to_chips(name, ks, small=None):
        got = _on_sequencer(name, "chips" if small is None else "all", _chips_rider([sums[k] for k in ks], small))
        for k, g in zip(ks, got):
            landed[k] = g
        return got[len(ks):]

    (loss_part, dh2, d_wpg, d_wpp, d_g_ple), _ = _ffn_down_ple(
        gt, up, h1, w_down_f, ps, target, g_ple_norm, full(w_pg_g), w_pp_g)
    to_sibling("sibling_ple", (PG, PP), (d_wpg, d_wpp))
    (dgt, dup, dh1, dh1b, d_g_ffn, d_wd), _ = _ffn_bwd_act(dh2, h1, gt, up, g_ffn_norm, wg_t, wu_t, w_down_f)
    to_sibling("sibling_down", (DOWN,), (d_wd,))
    chip_sum(PG, PP)
    to_chips("chips_ple", (PG, PP))
    (d_wo,), _ = _out_w_bwd(a, m_out, dh1b)
    to_sibling("sibling_out", (OUT,), (d_wo,))
    chip_sum(DOWN)
    to_chips("chips_down", (DOWN,))
    (d_wg_t, d_wu_t), _ = _ffn_bwd_w(dgt, dup, hn2)
    to_sibling("sibling_gate_up", (GATE, UP), (d_wg_t, d_wu_t))
    _complete_before_next([landed[PG], landed[PP]])
    (da, du, d_wpool, d_scale), _ = _mix_bwd(dh1b, u, w_out_f, wp, pool_scale)
    chip_sum(OUT, GATE, UP)
    early, early_at = _pack_small([d_wpool.reshape(POOL_WIDTH, POOL_GROUP), d_scale, d_g_ffn, d_g_ple, loss_part[:, :1]])
    (early_all,) = to_chips("chips_out_gate_up", (OUT, GATE, UP), early)
    (dqn, dkn, dv, dl_acc, d_sinks), _ = _attn_bwd(qn, kn, v, a, da, tab, attn_sinks)
    _complete_before_next([landed[DOWN]])
    (grad_x, d_win_t, d_g_attn, d_gq, d_gk), _ = _in_proj_bwd(dqn, dkn, dv, du, zqk, xs, dh1, g_attn_norm, gq_t, gk_t, w_in_t)
    to_sibling("sibling_in", (IN,), (d_win_t,))
    _complete_before_next([landed[OUT], landed[GATE], landed[UP], early_all])
    (d_rel_t,), _ = _bias_table_bwd(dl_acc)
    chip_sum(IN)
    late, late_at = _pack_small([d_g_attn, d_gq[:, :HEAD_DIM], d_gk[:, :HEAD_DIM], d_sinks[:, 0], d_rel_t])
    (late_all,) = to_chips("chips_in", (IN,), late)

    out = {"grad": {}, "delta": {}, "new_m": {}, "new_v": {}}
    for ks in ((PG, PP, DOWN), (OUT, GATE, UP), (IN,)):
        names = [BIG_WEIGHTS[k][0] for k in ks]
        results = _adamw_big(ks, place, [
            (partial[k], from_sibling[k], landed[k], to_blocks(k, weights[n]), to_blocks(k, m_in[n]),
             to_blocks(k, v_in[n])) for k, n in zip(ks, names)])
        for k, name, res in zip(ks, names, results):
            for kind, r in zip(("grad", "delta", "new_m", "new_v"), res):
                out[kind][name] = from_blocks(k, r)
    def as_rows(name, arr):
        return arr.T if name == "rel_bias" else arr.reshape(POOL_WIDTH, POOL_GROUP) if name == "w_pool" else arr

    def from_rows(name, arr):
        return arr.T if name == "rel_bias" else arr.reshape(w_pool.shape) if name == "w_pool" else arr

    grads_at = dict(w_pool=(0, early_at[0]), pool_scale=(0, early_at[1]), g_ffn_norm=(0, early_at[2]),
                    g_ple_norm=(0, early_at[3]), g_attn_norm=(1, late_at[0]), g_q=(1, late_at[1]), g_k=(1, late_at[2]),
                    attn_sinks=(1, late_at[3]), rel_bias=(1, late_at[4]))
    loss, updates = _small_update(
        [early_all, late_all], (0, early_at[4]), [grads_at[n] for n in SMALL_NAMES],
        [as_rows(n, weights[n]) for n in SMALL_NAMES], [as_rows(n, m_in[n]) for n in SMALL_NAMES],
        [as_rows(n, v_in[n]) for n in SMALL_NAMES])
    loss = loss.reshape(())
    n_small = len(SMALL_NAMES)
    for j, kind in enumerate(("grad", "delta", "new_m", "new_v")):
        for i, name in enumerate(SMALL_NAMES):
            out[kind][name] = from_rows(name, updates[j * n_small + i])

    _issued.clear()
    order = ("w_in", "w_out", "g_attn_norm", "g_q", "g_k", "attn_sinks", "rel_bias", "w_pool", "pool_scale",
             "g_ffn_norm", "w_gate", "w_up", "w_down", "g_ple_norm", "w_ple_gate", "w_ple_proj")
    return (loss, grad_x[None], *[out["grad"][n] for n in order], *[out["delta"][n] for n in order],
            *[out["new_m"][n] for n in order], *[out["new_v"][n] for n in order])
```

```python
import functools
import math

import jax
import jax.numpy as jnp
import numpy as np
from jax import lax
from jax.experimental import pallas as pl
from jax.experimental.pallas import tpu as pltpu
from jax.experimental.pallas import tpu_sc as plsc

F32 = jnp.float32
BF16 = jnp.bfloat16
MESH = pl.DeviceIdType.MESH

D_MODEL = 1024
HEAD_DIM = 64
ATTN_WIDTH = 512
KV_WIDTH = 128
POOL_WIDTH = 512
POOL_SIZES = (2, 4, 8, 16)
POOL_GROUP = 128
POOL_HALO = 16
IN_WIDTH = 1280
D_FF = 2816
PLE_DIM = 256
BLOCK = 128
N_BUCKETS = 32
MAX_DISTANCE = 128
EPS = 1e-6
N_DEV = 8
N_CHIPS = 4

ADAM_LR = 0.001
ADAM_B1 = 0.9
ADAM_B2 = 0.999
ADAM_EPS = 1e-08
ADAM_WD = 0.01
ADAM_STEP = 10

TOKEN_TILE = 512
FFN_BWD_TILE = 256
FF_CHUNK = 256
ATTN_STEP_BLOCKS = 4
GATE_ROWS_EARLY = 96
UP_ROWS_EARLY = 64
HEADS_A = (0, 2, 5, 7)
HEADS_B = (1, 3, 4, 6)
SMALL_LANES = 128


def _nn(a, b):
    return jnp.dot(a, b, preferred_element_type=F32)


def _nt(a, b):
    return lax.dot_general(a, b, (((1,), (1,)), ((), ())), preferred_element_type=F32)


def _tn(a, b):
    return lax.dot_general(a, b, (((0,), (0,)), ((), ())), preferred_element_type=F32)


def _resident(shape):
    nd = len(shape)
    return pl.BlockSpec(shape, lambda i, _nd=nd: (0,) * _nd, pipeline_mode=pl.Buffered(1))


def _rows(tile, width):
    return pl.BlockSpec((tile, width), lambda i: (i, 0))


def _acc(shape):
    nd = len(shape)
    return pl.BlockSpec(shape, lambda i, _nd=nd: (0,) * _nd)


def _head_mean_matrix(width):
    idx = np.arange(width) // HEAD_DIM
    return jnp.asarray((idx[:, None] == idx[None, :]).astype(np.float32) / HEAD_DIM, dtype=BF16)


def _seg_mean(v, bmat):
    hi = v.astype(BF16)
    lo = (v - hi.astype(F32)).astype(BF16)
    return _nn(hi, bmat) + _nn(lo, bmat)


def _rms(x):
    return lax.rsqrt(jnp.mean(x * x, axis=-1, keepdims=True) + EPS)


def _rms_bwd(d_y, x, r, g):
    gy = d_y * g
    d_x = r * gy - x * (r * r * r) * jnp.mean(gy * x, axis=-1, keepdims=True)
    d_g = jnp.sum(d_y * (x * r), axis=0, keepdims=True)
    return d_x, d_g


def _lane_lo(shape):
    return lax.broadcasted_iota(jnp.int32, shape, 1) < HEAD_DIM


class _Rider:
    def __init__(self, inputs, out_shapes, sems, begin, end, middle=None, aliases=None):
        self.inputs, self.out_shapes, self.sems = list(inputs), list(out_shapes), list(sems)
        self.begin, self.middle, self.end = begin, middle, end
        self.aliases = dict(aliases or {})


_issued = []


def _after_last(args, in_specs):
    extra = list(_issued)
    return list(args) + extra, list(in_specs) + [pl.BlockSpec(memory_space=pl.ANY)] * len(extra), len(extra)


def _mark_issued(out):
    _issued[:] = [out]


def _complete_before_next(arrays):
    _issued.extend(arrays)


def _call(body, args, *, name, grid, in_specs, out_specs, out_shape, scratch_shapes=(), rider=None):
    in_specs, out_specs, out_shape, scratch_shapes = list(in_specs), list(out_specs), list(out_shape), list(scratch_shapes)
    if rider is None:
        n_args = len(args)
        args, in_specs, _ = _after_last(args, in_specs)

        def ordered(*refs):
            body(*refs[:n_args], *refs[len(args):])

        outs = pl.pallas_call(ordered, name=name, grid=grid, in_specs=in_specs, out_specs=out_specs, out_shape=out_shape,
                              scratch_shapes=scratch_shapes)(*args)
        _mark_issued(outs[0])
        return list(outs), []
    n_in, n_out, n_scr = len(in_specs), len(out_shape), len(scratch_shapes)
    r_in, r_out = len(rider.inputs), len(rider.out_shapes)
    n_steps = grid[0]

    def hosted(*refs):
        ins, refs = refs[:n_in], refs[n_in:]
        r_ins, refs = refs[:r_in], refs[r_in:]
        outs, refs = refs[:n_out], refs[n_out:]
        r_outs, refs = refs[:r_out], refs[r_out:]
        scratch, r_sems = refs[:n_scr], refs[n_scr:]
        step = pl.program_id(0)

        @pl.when(step == 0)
        def _():
            rider.begin(r_ins, r_outs, r_sems)

        if rider.middle is not None:
            @pl.when(step == n_steps - 1)
            def _():
                rider.middle(r_ins, r_outs, r_sems)

        body(*ins, *outs, *scratch)

        @pl.when(step == n_steps - 1)
        def _():
            rider.end(r_ins, r_outs, r_sems)

    any_spec = pl.BlockSpec(memory_space=pl.ANY)
    outs = pl.pallas_call(
        hosted, name=name, grid=grid,
        in_specs=in_specs + [any_spec] * r_in,
        out_specs=out_specs + [any_spec] * r_out,
        out_shape=out_shape + rider.out_shapes,
        scratch_shapes=scratch_shapes + rider.sems,
        input_output_aliases={n_in + i: n_out + o for i, o in rider.aliases.items()},
    )(*args, *rider.inputs)
    return list(outs[:n_out]), list(outs[n_out:])


def _in_proj(x, g_attn, w_in_t, gq_t, gk_t, rider=None):
    s = x.shape[0]
    ts = min(TOKEN_TILE, s)

    def body(x_ref, g_ref, w_ref, gq_ref, gk_ref, bq_ref, bk_ref, zqk_ref, qn_ref, kn_ref, v_ref, u_ref):
        xf = x_ref[...]
        hn = ((xf * _rms(xf)) * g_ref[...]).astype(BF16)
        z = _nt(hn, w_ref[...])
        q = z[:, :ATTN_WIDTH]
        k = z[:, ATTN_WIDTH:ATTN_WIDTH + KV_WIDTH]
        zqk_ref[...] = z[:, :ATTN_WIDTH + KV_WIDTH]
        rq = lax.rsqrt(_seg_mean(q * q, bq_ref[...]) + EPS)
        qn_ref[...] = ((q * rq) * gq_ref[...]).astype(BF16)
        rk = lax.rsqrt(_seg_mean(k * k, bk_ref[...]) + EPS)
        kn_ref[...] = ((k * rk) * gk_ref[...]).astype(BF16)
        v_ref[...] = z[:, ATTN_WIDTH + KV_WIDTH:ATTN_WIDTH + 2 * KV_WIDTH].astype(BF16)
        u_ref[...] = z[:, ATTN_WIDTH + 2 * KV_WIDTH:]

    return _call(
        body,
        (x, g_attn, w_in_t, gq_t, gk_t, _head_mean_matrix(ATTN_WIDTH), _head_mean_matrix(KV_WIDTH)),
        name="in_proj",
        grid=(s // ts,),
        in_specs=[
            _rows(ts, D_MODEL),
            _resident((1, D_MODEL)),
            _resident((IN_WIDTH, D_MODEL)),
            _resident((1, ATTN_WIDTH)),
            _resident((1, KV_WIDTH)),
            _resident((ATTN_WIDTH, ATTN_WIDTH)),
            _resident((KV_WIDTH, KV_WIDTH)),
        ],
        out_specs=[
            _rows(ts, ATTN_WIDTH + KV_WIDTH),
            _rows(ts, ATTN_WIDTH),
            _rows(ts, KV_WIDTH),
            _rows(ts, KV_WIDTH),
            _rows(ts, POOL_WIDTH),
        ],
        out_shape=[
            jax.ShapeDtypeStruct((s, ATTN_WIDTH + KV_WIDTH), F32),
            jax.ShapeDtypeStruct((s, ATTN_WIDTH), BF16),
            jax.ShapeDtypeStruct((s, KV_WIDTH), BF16),
            jax.ShapeDtypeStruct((s, KV_WIDTH), BF16),
            jax.ShapeDtypeStruct((s, POOL_WIDTH), F32),
        ],
        rider=rider,
    )


def _bucket_ranges():
    n = np.arange(MAX_DISTANCE)
    max_exact = N_BUCKETS // 2
    nf = np.maximum(n, 1).astype(np.float64)
    large = max_exact + (np.log(nf / max_exact) / math.log(MAX_DISTANCE / max_exact) * (N_BUCKETS - max_exact)).astype(np.int64)
    bucket = np.where(n < max_exact, n, np.minimum(large, N_BUCKETS - 1))
    out = []
    for b in range(N_BUCKETS):
        idx = np.nonzero(bucket == b)[0]
        out.append((int(idx.min()), int(idx.max()) + 1))
    return out


def _band_distance():
    i = lax.broadcasted_iota(jnp.int32, (BLOCK, 2 * BLOCK), 0)
    j = lax.broadcasted_iota(jnp.int32, (BLOCK, 2 * BLOCK), 1)
    return BLOCK + i - j


BIAS_TABLE_SHAPE = (2, 4 * BLOCK, 2 * BLOCK)


def _write_bias_table(rb_ref, tab_ref):
    d = _band_distance()
    for half, heads in enumerate((HEADS_A, HEADS_B)):
        for slot, h in enumerate(heads):
            t = jnp.full((BLOCK, 2 * BLOCK), -jnp.inf, F32)
            for b, (lo, hi) in enumerate(_bucket_ranges()):
                t = jnp.where((d >= lo) & (d < hi), rb_ref[h, b], t)
            tab_ref[half, slot * BLOCK:(slot + 1) * BLOCK, :] = t


def _bias_table_bwd(dl_acc, rider=None):
    ranges = _bucket_ranges()
    n_heads = len(HEADS_A) + len(HEADS_B)

    def body(dl_ref, out_ref):
        d = _band_distance()
        row = lax.broadcasted_iota(jnp.int32, (n_heads, SMALL_LANES), 0)
        lane = lax.broadcasted_iota(jnp.int32, (n_heads, SMALL_LANES), 1)
        out = jnp.zeros((n_heads, SMALL_LANES), F32)
        for b, (lo, hi) in enumerate(ranges):
            in_bucket = (d >= lo) & (d < hi)
            for half, heads in enumerate((HEADS_A, HEADS_B)):
                for slot, h in enumerate(heads):
                    g = dl_ref[half, slot * BLOCK:(slot + 1) * BLOCK, :]
                    part = jnp.sum(jnp.where(in_bucket, g, 0.0), axis=0, keepdims=True)
                    tot = jnp.sum(part, axis=1, keepdims=True)
                    out = jnp.where((row == h) & (lane == b), tot, out)
        out_ref[...] = out

    return _call(
        body,
        (dl_acc,),
        name="bias_table_bwd",
        grid=(1,),
        in_specs=[_acc((2, 4 * BLOCK, 2 * BLOCK))],
        out_specs=[_acc((n_heads, SMALL_LANES))],
        out_shape=[jax.ShapeDtypeStruct((n_heads, SMALL_LANES), F32)],
        rider=rider,
    )


def _stack_heads(pairs, lo_mask):
    zero = jnp.zeros_like(pairs[0])
    lo = [jnp.where(lo_mask, t, zero) for t in pairs]
    hi = [jnp.where(lo_mask, zero, t) for t in pairs]
    return (jnp.concatenate([lo[0], lo[1], hi[2], hi[3]], axis=0),
            jnp.concatenate([hi[0], hi[1], lo[2], lo[3]], axis=0))


def _unstack_heads(out_a, out_b, lo_mask):
    t = lambda x, r: x[r * BLOCK:(r + 1) * BLOCK, :]
    return [
        jnp.where(lo_mask, t(out_a, 0), t(out_b, 0)),
        jnp.where(lo_mask, t(out_a, 1), t(out_b, 1)),
        jnp.where(lo_mask, t(out_b, 2), t(out_a, 2)),
        jnp.where(lo_mask, t(out_b, 3), t(out_a, 3)),
    ]


def _sink_column(sink_ref, heads):
    row = lax.broadcasted_iota(jnp.int32, (4 * BLOCK, 1), 0)
    col = jnp.full((4 * BLOCK, 1), sink_ref[0, heads[3]], F32)
    for slot in (2, 1, 0):
        col = jnp.where(row < (slot + 1) * BLOCK, sink_ref[0, heads[slot]], col)
    return col


def _band_probs(q_stack, keys, tab, sink, first_block):
    s = _nt(q_stack, keys) * (HEAD_DIM ** -0.5) + tab
    if first_block is not None:
        col = lax.broadcasted_iota(jnp.int32, s.shape, 1)
        s = jnp.where(jnp.logical_and(first_block, col < BLOCK), -jnp.inf, s)
    m = jnp.maximum(jnp.max(s, axis=-1, keepdims=True), sink)
    e = jnp.exp(s - m)
    e_sink = jnp.exp(sink - m)
    den = jnp.sum(e, axis=-1, keepdims=True) + e_sink
    return e / den, e_sink / den


def _attn_specs(n_groups):
    group = lambda n: (jnp.minimum(n, n_groups - 1), 0)
    prev = lambda n: (jnp.maximum(jnp.minimum(n, n_groups - 1) * ATTN_STEP_BLOCKS - 1, 0), 0)
    return group, prev


def _band(prev_ref, group_ref, b):
    rows = lambda i: group_ref[i * BLOCK:(i + 1) * BLOCK, :]
    band = jnp.concatenate([prev_ref[...] if b == 0 else rows(b - 1), rows(b)], axis=0)
    return band, pltpu.roll(band, HEAD_DIM, 1)


def _attn_fwd(qn, kn, v, tab, sinks, rider=None):
    s = qn.shape[0]
    n_groups = s // (ATTN_STEP_BLOCKS * BLOCK)
    group, prev = _attn_specs(n_groups)
    rows = ATTN_STEP_BLOCKS * BLOCK

    def body(sink_ref, q_ref, kc_ref, kp_ref, vc_ref, vp_ref, tab_ref, o_ref):
        first = pl.program_id(0) == 0
        lo_mask = _lane_lo((BLOCK, BLOCK))
        for b in range(ATTN_STEP_BLOCKS):
            at = slice(b * BLOCK, (b + 1) * BLOCK)
            kk, kk_sw = _band(kp_ref, kc_ref, b)
            vv, vv_sw = _band(vp_ref, vc_ref, b)
            q_a, q_b = _stack_heads([q_ref[at, p * BLOCK:(p + 1) * BLOCK] for p in range(4)], lo_mask)
            no_prev = first if b == 0 else None
            p_a, _ = _band_probs(q_a, kk, tab_ref[0], _sink_column(sink_ref, HEADS_A), no_prev)
            p_b, _ = _band_probs(q_b, kk_sw, tab_ref[1], _sink_column(sink_ref, HEADS_B), no_prev)
            out = _unstack_heads(_nn(p_a.astype(BF16), vv), _nn(p_b.astype(BF16), vv_sw), lo_mask)
            for p in range(4):
                o_ref[at, p * BLOCK:(p + 1) * BLOCK] = out[p].astype(BF16)

    return _call(
        body,
        (sinks, qn, kn, kn, v, v, tab),
        name="attn_fwd",
        grid=(n_groups,),
        in_specs=[
            pl.BlockSpec(memory_space=pltpu.SMEM),
            pl.BlockSpec((rows, ATTN_WIDTH), group),
            pl.BlockSpec((rows, KV_WIDTH), group),
            pl.BlockSpec((BLOCK, KV_WIDTH), prev),
            pl.BlockSpec((rows, KV_WIDTH), group),
            pl.BlockSpec((BLOCK, KV_WIDTH), prev),
            _resident((2, 4 * BLOCK, 2 * BLOCK)),
        ],
        out_specs=[pl.BlockSpec((rows, ATTN_WIDTH), group)],
        out_shape=[jax.ShapeDtypeStruct((s, ATTN_WIDTH), BF16)],
        rider=rider,
    )


def _pooled(u_tile, u_halo, tile_index, tile_rows):
    halo = jnp.where(tile_index > 0, u_halo, 0.0)
    ext = jnp.concatenate([halo, u_tile], axis=0)
    sums = []
    acc = ext
    for shift in (1, 2, 4, 8):
        acc = acc + pltpu.roll(acc, shift, 0)
        sums.append(acc)
    t = tile_index * tile_rows + lax.broadcasted_iota(jnp.int32, (tile_rows, 1), 0)
    out = []
    for g, w in enumerate(POOL_SIZES):
        lanes = slice(g * POOL_GROUP, (g + 1) * POOL_GROUP)
        cnt = jnp.minimum(t + 1, w).astype(F32)
        out.append(sums[g][POOL_HALO:, lanes] / cnt - u_tile[:, lanes])
    return out


def _halo_before(tile):
    return lambda i: (jnp.maximum(i * (tile // POOL_HALO) - 1, 0), 0)


def _mix_out(u, a, x, w_out, w_pool, pool_scale, g_ffn, rider=None):
    s = x.shape[0]
    ts = min(TOKEN_TILE, s)

    def body(u_ref, uh_ref, a_ref, x_ref, wo_ref, wp_ref, sc_ref, g_ref, h1_ref, hn_ref, m_ref):
        i = pl.program_id(0)
        pooled = _pooled(u_ref[...], uh_ref[...], i, ts)
        for g in range(len(POOL_SIZES)):
            lanes = slice(g * POOL_GROUP, (g + 1) * POOL_GROUP)
            y = _nn(pooled[g].astype(BF16), wp_ref[g].astype(BF16))
            m_ref[:, lanes] = (y * sc_ref[:, lanes]).astype(BF16)
        h1 = x_ref[...] + _nn(a_ref[...], wo_ref[:ATTN_WIDTH, :]) + _nn(m_ref[...], wo_ref[ATTN_WIDTH:, :])
        h1_ref[...] = h1
        hn_ref[...] = ((h1 * _rms(h1)) * g_ref[...]).astype(BF16)

    return _call(
        body,
        (u, u, a, x, w_out, w_pool, pool_scale, g_ffn),
        name="mix_out",
        grid=(s // ts,),
        in_specs=[
            _rows(ts, POOL_WIDTH),
            pl.BlockSpec((POOL_HALO, POOL_WIDTH), _halo_before(ts)),
            _rows(ts, ATTN_WIDTH),
            _rows(ts, D_MODEL),
            _resident((D_MODEL, D_MODEL)),
            _resident((len(POOL_SIZES), POOL_GROUP, POOL_GROUP)),
            _resident((1, POOL_WIDTH)),
            _resident((1, D_MODEL)),
        ],
        out_specs=[_rows(ts, D_MODEL), _rows(ts, D_MODEL), _rows(ts, POOL_WIDTH)],
        out_shape=[
            jax.ShapeDtypeStruct((s, D_MODEL), F32),
            jax.ShapeDtypeStruct((s, D_MODEL), BF16),
            jax.ShapeDtypeStruct((s, POOL_WIDTH), BF16),
        ],
        rider=rider,
    )


def _ffn_up(hn2, wg_t, wu_t, rider=None):
    s = hn2.shape[0]
    ts = min(TOKEN_TILE, s)

    def body(hn_ref, wg_ref, wu_ref, gt_ref, up_ref):
        hn = hn_ref[...]
        for c in range(D_FF // FF_CHUNK):
            cols = slice(c * FF_CHUNK, (c + 1) * FF_CHUNK)
            gt_ref[:, cols] = _nt(hn, wg_ref[cols, :]).astype(BF16)
            up_ref[:, cols] = _nt(hn, wu_ref[cols, :]).astype(BF16)

    return _call(
        body,
        (hn2, wg_t, wu_t),
        name="ffn_up",
        grid=(s // ts,),
        in_specs=[_rows(ts, D_MODEL), _resident((D_FF, D_MODEL)), _resident((D_FF, D_MODEL))],
        out_specs=[_rows(ts, D_FF), _rows(ts, D_FF)],
        out_shape=[jax.ShapeDtypeStruct((s, D_FF), BF16), jax.ShapeDtypeStruct((s, D_FF), BF16)],
        rider=rider,
    )


def _silu_mul(gt, up):
    return (gt * jax.nn.sigmoid(gt)) * up


def _ffn_down_ple(gt, up, h1, w_down, p, target, g_ple, w_pg, w_pp, rider=None):
    s = h1.shape[0]
    ts = min(TOKEN_TILE, s)
    blk = D_MODEL // N_DEV

    def body(gt_ref, up_ref, h1_ref, wd_ref, p_ref, t_ref, g_ref, wpg_ref, wpp_ref,
             loss_ref, dh_ref, dwpg_ref, dwpp_ref, dg_ref, act_ref, pp_ref):
        @pl.when(pl.program_id(0) == 0)
        def _():
            loss_ref[...] = jnp.zeros_like(loss_ref)
            dwpg_ref[...] = jnp.zeros_like(dwpg_ref)
            dwpp_ref[...] = jnp.zeros_like(dwpp_ref)
            dg_ref[...] = jnp.zeros_like(dg_ref)

        h2v = h1_ref[...]
        for c in range(D_FF // FF_CHUNK):
            cols = slice(c * FF_CHUNK, (c + 1) * FF_CHUNK)
            act = _silu_mul(gt_ref[:, cols].astype(F32), up_ref[:, cols].astype(F32)).astype(BF16)
            h2v = _nn(act, wd_ref[cols, :]) + h2v
        r = _rms(h2v)
        hn = ((h2v * r) * g_ref[...]).astype(BF16)
        gate = jax.nn.sigmoid(_nn(hn, wpg_ref[...]))
        pb = p_ref[...].astype(BF16)
        for j in range(N_DEV):
            pp_ref[:, j * blk:(j + 1) * blk] = _nn(pb, wpp_ref[j])
        pp = pp_ref[...]
        diff = (h2v + gate * pp) - t_ref[...]
        loss_ref[...] += jnp.sum(jnp.sum(diff * diff, axis=0, keepdims=True), axis=1, keepdims=True) * (0.5 / D_MODEL)
        dy = diff * (1.0 / D_MODEL)
        d_pp = (dy * gate).astype(BF16)
        d_pre = ((dy * pp) * (gate * (1.0 - gate))).astype(BF16)
        for j in range(N_DEV):
            dwpp_ref[j] += _tn(pb, d_pp[:, j * blk:(j + 1) * blk])
        dwpg_ref[...] += _tn(hn, d_pre)
        d_x, d_g = _rms_bwd(_nt(d_pre, wpg_ref[...]), h2v, r, g_ref[...])
        dg_ref[...] += d_g
        dh_ref[...] = dy + d_x

    return _call(
        body,
        (gt, up, h1, w_down, p, target, g_ple, w_pg, w_pp),
        name="ffn_down_ple",
        grid=(s // ts,),
        in_specs=[
            _rows(ts, D_FF),
            _rows(ts, D_FF),
            _rows(ts, D_MODEL),
            _resident((D_FF, D_MODEL)),
            _rows(ts, PLE_DIM),
            _rows(ts, D_MODEL),
            _resident((1, D_MODEL)),
            _resident((D_MODEL, D_MODEL)),
            _resident((N_DEV, PLE_DIM, blk)),
        ],
        out_specs=[
            _acc((1, SMALL_LANES)),
            _rows(ts, D_MODEL),
            _acc((D_MODEL, D_MODEL)),
            _acc((N_DEV, PLE_DIM, blk)),
            _acc((1, D_MODEL)),
        ],
        out_shape=[
            jax.ShapeDtypeStruct((1, SMALL_LANES), F32),
            jax.ShapeDtypeStruct((s, D_MODEL), F32),
            jax.ShapeDtypeStruct((D_MODEL, D_MODEL), F32),
            jax.ShapeDtypeStruct((N_DEV, PLE_DIM, blk), F32),
            jax.ShapeDtypeStruct((1, D_MODEL), F32),
        ],
        scratch_shapes=[pltpu.VMEM((ts, D_FF), BF16), pltpu.VMEM((ts, D_MODEL), F32)],
        rider=rider,
    )


def _ffn_bwd_act(dh2, h1, gt, up, g_ffn, wg_t, wu_t, w_down, rider=None):
    s = h1.shape[0]
    ts = min(FFN_BWD_TILE, s)

    def body(dh_ref, h1_ref, gt_ref, up_ref, g_ref, wg_ref, wu_ref, wd_ref,
             dgt_ref, dup_ref, dh1_ref, dh1b_ref, dg_ref, dwd_ref, act_ref):
        @pl.when(pl.program_id(0) == 0)
        def _():
            dg_ref[...] = jnp.zeros_like(dg_ref)
            dwd_ref[...] = jnp.zeros_like(dwd_ref)

        dhb = dh_ref[...].astype(BF16)
        d_hn = jnp.zeros((ts, D_MODEL), F32)
        for c in range(D_FF // FF_CHUNK):
            cols = slice(c * FF_CHUNK, (c + 1) * FF_CHUNK)
            d_act = _nt(dhb, wd_ref[cols, :])
            gtv = gt_ref[:, cols].astype(F32)
            upv = up_ref[:, cols].astype(F32)
            sg = jax.nn.sigmoid(gtv)
            silu = gtv * sg
            act_ref[:, cols] = (silu * upv).astype(BF16)
            d_up = (d_act * silu).astype(BF16)
            d_gt = ((d_act * upv) * (sg * (1.0 + gtv * (1.0 - sg)))).astype(BF16)
            dup_ref[:, cols] = d_up
            dgt_ref[:, cols] = d_gt
            d_hn = (_nn(d_gt, wg_ref[cols, :]) + _nn(d_up, wu_ref[cols, :])) + d_hn
        dwd_ref[...] += _tn(act_ref[...], dhb)
        h1v = h1_ref[...]
        d_x, d_g = _rms_bwd(d_hn, h1v, _rms(h1v), g_ref[...])
        dg_ref[...] += d_g
        dh1 = dh_ref[...] + d_x
        dh1_ref[...] = dh1
        dh1b_ref[...] = dh1.astype(BF16)

    return _call(
        body,
        (dh2, h1, gt, up, g_ffn, wg_t, wu_t, w_down),
        name="ffn_bwd_act",
        grid=(s // ts,),
        in_specs=[
            _rows(ts, D_MODEL),
            _rows(ts, D_MODEL),
            _rows(ts, D_FF),
            _rows(ts, D_FF),
            _resident((1, D_MODEL)),
            _resident((D_FF, D_MODEL)),
            _resident((D_FF, D_MODEL)),
            _resident((D_FF, D_MODEL)),
        ],
        out_specs=[
            _rows(ts, D_FF), _rows(ts, D_FF),
            _rows(ts, D_MODEL), _rows(ts, D_MODEL), _acc((1, D_MODEL)), _acc((D_FF, D_MODEL)),
        ],
        out_shape=[
            jax.ShapeDtypeStruct((s, D_FF), BF16),
            jax.ShapeDtypeStruct((s, D_FF), BF16),
            jax.ShapeDtypeStruct((s, D_MODEL), F32),
            jax.ShapeDtypeStruct((s, D_MODEL), BF16),
            jax.ShapeDtypeStruct((1, D_MODEL), F32),
            jax.ShapeDtypeStruct((D_FF, D_MODEL), F32),
        ],
        scratch_shapes=[pltpu.VMEM((ts, D_FF), BF16)],
        rider=rider,
    )


def _ffn_bwd_w(dgt, dup, hn2, rider=None):
    s = hn2.shape[0]
    slab = pl.BlockSpec((s, FF_CHUNK), lambda i: (0, i))

    def body(dgt_ref, dup_ref, hn_ref, dwg_ref, dwu_ref):
        hn = hn_ref[...]
        dwg_ref[...] = _tn(dgt_ref[...], hn)
        dwu_ref[...] = _tn(dup_ref[...], hn)

    return _call(
        body,
        (dgt, dup, hn2),
        name="ffn_bwd_w",
        grid=(D_FF // FF_CHUNK,),
        in_specs=[slab, slab, _resident((s, D_MODEL))],
        out_specs=[_rows(FF_CHUNK, D_MODEL)] * 2,
        out_shape=[jax.ShapeDtypeStruct((D_FF, D_MODEL), F32)] * 2,
        rider=rider,
    )


def _mix_bwd(dh1b, u, w_out, w_pool, pool_scale, rider=None):
    s = u.shape[0]
    ts = min(TOKEN_TILE, s)
    nt = s // ts
    halo_after = lambda i: (jnp.minimum((i + 1) * (ts // POOL_HALO), s // POOL_HALO - 1), 0)
    n_groups = len(POOL_SIZES)

    def body(dh_ref, dhn_ref, u_ref, uh_ref, wo_ref, wp_ref, sc_ref, da_ref, du_ref, dwp_ref, dsc_ref):
        i = pl.program_id(0)

        @pl.when(i == 0)
        def _():
            dwp_ref[...] = jnp.zeros_like(dwp_ref)
            dsc_ref[...] = jnp.zeros_like(dsc_ref)

        dh = dh_ref[...]
        da_ref[...] = _nt(dh, wo_ref[:ATTN_WIDTH, :])
        dh_next = jnp.where(i < nt - 1, dhn_ref[...], jnp.zeros_like(dhn_ref))
        dm_ext = _nt(jnp.concatenate([dh, dh_next], axis=0), wo_ref[ATTN_WIDTH:, :])
        pooled = _pooled(u_ref[...], uh_ref[...], i, ts)
        t_ext = i * ts + lax.broadcasted_iota(jnp.int32, (ts + POOL_HALO, 1), 0)
        for g, w in enumerate(POOL_SIZES):
            lanes = slice(g * POOL_GROUP, (g + 1) * POOL_GROUP)
            wp = wp_ref[g].astype(BF16)
            pg = pooled[g].astype(BF16)
            dm_g = dm_ext[:, lanes]
            dsc_ref[:, lanes] += jnp.sum(dm_g[:ts, :] * _nn(pg, wp), axis=0, keepdims=True)
            dy = (dm_g * sc_ref[:, lanes]).astype(BF16)
            dwp_ref[g] += _tn(pg, dy[:ts, :])
            d_pool = _nt(dy, wp)
            acc = d_pool / jnp.minimum(t_ext + 1, w).astype(F32)
            shift = 1
            while shift < w:
                acc = acc + pltpu.roll(acc, ts + POOL_HALO - shift, 0)
                shift *= 2
            du_ref[:, lanes] = acc[:ts, :] - d_pool[:ts, :]

    return _call(
        body,
        (dh1b, dh1b, u, u, w_out, w_pool, pool_scale),
        name="mix_bwd",
        grid=(nt,),
        in_specs=[
            _rows(ts, D_MODEL),
            pl.BlockSpec((POOL_HALO, D_MODEL), halo_after),
            _rows(ts, POOL_WIDTH),
            pl.BlockSpec((POOL_HALO, POOL_WIDTH), _halo_before(ts)),
            _resident((D_MODEL, D_MODEL)),
            _resident((n_groups, POOL_GROUP, POOL_GROUP)),
            _resident((1, POOL_WIDTH)),
        ],
        out_specs=[
            _rows(ts, ATTN_WIDTH),
            _rows(ts, POOL_WIDTH),
            _acc((n_groups, POOL_GROUP, POOL_GROUP)),
            _acc((1, POOL_WIDTH)),
        ],
        out_shape=[
            jax.ShapeDtypeStruct((s, ATTN_WIDTH), F32),
            jax.ShapeDtypeStruct((s, POOL_WIDTH), F32),
            jax.ShapeDtypeStruct((n_groups, POOL_GROUP, POOL_GROUP), F32),
            jax.ShapeDtypeStruct((1, POOL_WIDTH), F32),
        ],
        rider=rider,
    )


def _out_w_bwd(a, m, dh1b, rider=None):
    s = dh1b.shape[0]

    def body(a_ref, m_ref, dh_ref, dw_ref):
        @pl.when(pl.program_id(0) == 0)
        def _():
            dw_ref[...] = _tn(a_ref[...], dh_ref[...])

        @pl.when(pl.program_id(0) == 1)
        def _():
            dw_ref[...] = _tn(m_ref[...], dh_ref[...])

    return _call(
        body,
        (a, m, dh1b),
        name="out_w_bwd",
        grid=(2,),
        in_specs=[_resident((s, ATTN_WIDTH)), _resident((s, POOL_WIDTH)), _resident((s, D_MODEL))],
        out_specs=[_rows(ATTN_WIDTH, D_MODEL)],
        out_shape=[jax.ShapeDtypeStruct((D_MODEL, D_MODEL), F32)],
        rider=rider,
    )


def _attn_bwd(qn, kn, v, a, da, tab, sinks, rider=None):
    s = qn.shape[0]
    qb = ATTN_STEP_BLOCKS
    rows = qb * BLOCK
    n_groups = s // rows
    group, prev = _attn_specs(n_groups)
    done = lambda n: (jnp.maximum(n - 1, 0), 0)

    def body(sink_ref, q_ref, kc_ref, kp_ref, vc_ref, vp_ref, o_ref, do_ref, tab_ref,
             dq_ref, dk_ref, dv_ref, dl_ref, ds_ref, k_carry, v_carry, sink_acc):
        n = pl.program_id(0)

        @pl.when(n == 0)
        def _():
            dl_ref[...] = jnp.zeros_like(dl_ref)
            k_carry[...] = jnp.zeros_like(k_carry)
            v_carry[...] = jnp.zeros_like(v_carry)
            sink_acc[...] = jnp.zeros_like(sink_acc)

        @pl.when(n < n_groups)
        def _():
            first = n == 0
            lo_mask = _lane_lo((BLOCK, BLOCK))
            dks, dvs = [], []
            for b in range(qb):
                at = slice(b * BLOCK, (b + 1) * BLOCK)
                keys = _band(kp_ref, kc_ref, b)
                vals = _band(vp_ref, vc_ref, b)
                q_st = _stack_heads([q_ref[at, p * BLOCK:(p + 1) * BLOCK] for p in range(4)], lo_mask)
                do_st = _stack_heads([do_ref[at, p * BLOCK:(p + 1) * BLOCK] for p in range(4)], lo_mask)
                o_st = _stack_heads([o_ref[at, p * BLOCK:(p + 1) * BLOCK].astype(F32) for p in range(4)], lo_mask)
                dq_st, dk_parts, dv_parts = [], [], []
                for half, heads in enumerate((HEADS_A, HEADS_B)):
                    probs, p_sink = _band_probs(q_st[half], keys[half], tab_ref[half], _sink_column(sink_ref, heads),
                                                first if b == 0 else None)
                    delta = jnp.sum(do_st[half] * o_st[half], axis=-1, keepdims=True)
                    dob = do_st[half].astype(BF16)
                    dl = probs * (_nt(dob, vals[half]) - delta)
                    dl_ref[half] += dl
                    sink_acc[half] += p_sink * delta
                    dsb = (dl * (HEAD_DIM ** -0.5)).astype(BF16)
                    dq_st.append(_nn(dsb, keys[half]))
                    dk_parts.append(_tn(dsb, q_st[half]))
                    dv_parts.append(_tn(probs.astype(BF16), dob))
                dq = _unstack_heads(dq_st[0], dq_st[1], lo_mask)
                for p in range(4):
                    dq_ref[at, p * BLOCK:(p + 1) * BLOCK] = dq[p]
                dks.append(dk_parts[0] + pltpu.roll(dk_parts[1], HEAD_DIM, 1))
                dvs.append(dv_parts[0] + pltpu.roll(dv_parts[1], HEAD_DIM, 1))
            last = slice((qb - 1) * BLOCK, qb * BLOCK)
            for parts, out_ref, carry in ((dks, dk_ref, k_carry), (dvs, dv_ref, v_carry)):
                out_ref[...] = carry[...]
                out_ref[last, :] += parts[0][:BLOCK, :]
                for b in range(qb):
                    own = parts[b][BLOCK:, :]
                    carry[b * BLOCK:(b + 1) * BLOCK, :] = own + parts[b + 1][:BLOCK, :] if b + 1 < qb else own

        @pl.when(n == n_groups)
        def _():
            dk_ref[...] = k_carry[...]
            dv_ref[...] = v_carry[...]
            for half, heads in enumerate((HEADS_A, HEADS_B)):
                for slot, h in enumerate(heads):
                    tot = jnp.sum(sink_acc[half, slot * BLOCK:(slot + 1) * BLOCK, :], axis=0, keepdims=True)
                    ds_ref[h:h + 1, :] = jnp.broadcast_to(-tot, (1, SMALL_LANES))

    return _call(
        body,
        (sinks, qn, kn, kn, v, v, a, da, tab),
        name="attn_bwd",
        grid=(n_groups + 1,),
        in_specs=[
            pl.BlockSpec(memory_space=pltpu.SMEM),
            pl.BlockSpec((rows, ATTN_WIDTH), group),
            pl.BlockSpec((rows, KV_WIDTH), group),
            pl.BlockSpec((BLOCK, KV_WIDTH), prev),
            pl.BlockSpec((rows, KV_WIDTH), group),
            pl.BlockSpec((BLOCK, KV_WIDTH), prev),
            pl.BlockSpec((rows, ATTN_WIDTH), group),
            pl.BlockSpec((rows, ATTN_WIDTH), group),
            _resident((2, 4 * BLOCK, 2 * BLOCK)),
        ],
        out_specs=[
            pl.BlockSpec((rows, ATTN_WIDTH), group),
            pl.BlockSpec((rows, KV_WIDTH), done),
            pl.BlockSpec((rows, KV_WIDTH), done),
            _acc((2, 4 * BLOCK, 2 * BLOCK)),
            _acc((N_DEV, SMALL_LANES)),
        ],
        out_shape=[
            jax.ShapeDtypeStruct((s, ATTN_WIDTH), F32),
            jax.ShapeDtypeStruct((s, KV_WIDTH), F32),
            jax.ShapeDtypeStruct((s, KV_WIDTH), F32),
            jax.ShapeDtypeStruct((2, 4 * BLOCK, 2 * BLOCK), F32),
            jax.ShapeDtypeStruct((N_DEV, SMALL_LANES), F32),
        ],
        scratch_shapes=[
            pltpu.VMEM((rows, KV_WIDTH), F32),
            pltpu.VMEM((rows, KV_WIDTH), F32),
            pltpu.VMEM((2, 4 * BLOCK, 1), F32),
        ],
        rider=rider,
    )


def _fold_heads(acc):
    t = acc + pltpu.roll(acc, HEAD_DIM, 1)
    out = t[:, :SMALL_LANES]
    for g in range(1, acc.shape[1] // SMALL_LANES):
        out = out + t[:, g * SMALL_LANES:(g + 1) * SMALL_LANES]
    return out


def _in_proj_bwd(dqn, dkn, dv, du, zqk, x, dh1, g_attn, gq_t, gk_t, w_in_t, rider=None):
    s = x.shape[0]
    ts = min(TOKEN_TILE, s)
    nt = s // ts

    def head_norm_bwd(d_n, raw, g_t, bmat):
        r = lax.rsqrt(_seg_mean(raw * raw, bmat) + EPS)
        gy = d_n * g_t
        d_raw = r * gy - raw * (r * r * r) * _seg_mean(gy * raw, bmat)
        return d_raw, jnp.sum(d_n * (raw * r), axis=0, keepdims=True)

    def body(dqn_ref, dkn_ref, dv_ref, du_ref, zqk_ref, x_ref, dh1_ref, g_ref, gq_ref, gk_ref, w_ref, bq_ref, bk_ref,
             gx_ref, dw_ref, dg_ref, dgq_ref, dgk_ref, dz_ref, gq_acc, gk_acc):
        i = pl.program_id(0)

        @pl.when(i == 0)
        def _():
            dw_ref[...] = jnp.zeros_like(dw_ref)
            dg_ref[...] = jnp.zeros_like(dg_ref)
            gq_acc[...] = jnp.zeros_like(gq_acc)
            gk_acc[...] = jnp.zeros_like(gk_acc)

        d_q, d_gq = head_norm_bwd(dqn_ref[...], zqk_ref[:, :ATTN_WIDTH], gq_ref[...], bq_ref[...])
        d_k, d_gk = head_norm_bwd(dkn_ref[...], zqk_ref[:, ATTN_WIDTH:], gk_ref[...], bk_ref[...])
        gq_acc[...] += d_gq
        gk_acc[...] += d_gk
        dz_ref[:, :ATTN_WIDTH] = d_q.astype(BF16)
        dz_ref[:, ATTN_WIDTH:ATTN_WIDTH + KV_WIDTH] = d_k.astype(BF16)
        dz_ref[:, ATTN_WIDTH + KV_WIDTH:ATTN_WIDTH + 2 * KV_WIDTH] = dv_ref[...].astype(BF16)
        dz_ref[:, ATTN_WIDTH + 2 * KV_WIDTH:] = du_ref[...].astype(BF16)
        dz = dz_ref[...]
        xf = x_ref[...]
        r = _rms(xf)
        hn = ((xf * r) * g_ref[...]).astype(BF16)
        dw_ref[...] += _tn(dz, hn)
        d_x, d_g = _rms_bwd(_nn(dz, w_ref[...]), xf, r, g_ref[...])
        dg_ref[...] += d_g
        gx_ref[...] = dh1_ref[...] + d_x

        @pl.when(i == nt - 1)
        def _():
            dgq_ref[...] = _fold_heads(gq_acc[...])
            dgk_ref[...] = _fold_heads(gk_acc[...])

    return _call(
        body,
        (dqn, dkn, dv, du, zqk, x, dh1, g_attn, gq_t, gk_t, w_in_t,
      _head_mean_matrix(ATTN_WIDTH), _head_mean_matrix(KV_WIDTH)),
        name="in_proj_bwd",
        grid=(nt,),
        in_specs=[
            _rows(ts, ATTN_WIDTH),
            _rows(ts, KV_WIDTH),
            _rows(ts, KV_WIDTH),
            _rows(ts, POOL_WIDTH),
            _rows(ts, ATTN_WIDTH + KV_WIDTH),
            _rows(ts, D_MODEL),
            _rows(ts, D_MODEL),
            _resident((1, D_MODEL)),
            _resident((1, ATTN_WIDTH)),
            _resident((1, KV_WIDTH)),
            _resident((IN_WIDTH, D_MODEL)),
            _resident((ATTN_WIDTH, ATTN_WIDTH)),
            _resident((KV_WIDTH, KV_WIDTH)),
        ],
        out_specs=[
            _rows(ts, D_MODEL),
            _acc((IN_WIDTH, D_MODEL)),
            _acc((1, D_MODEL)),
            _acc((1, SMALL_LANES)),
            _acc((1, SMALL_LANES)),
        ],
        out_shape=[
            jax.ShapeDtypeStruct((s, D_MODEL), F32),
            jax.ShapeDtypeStruct((IN_WIDTH, D_MODEL), F32),
            jax.ShapeDtypeStruct((1, D_MODEL), F32),
            jax.ShapeDtypeStruct((1, SMALL_LANES), F32),
            jax.ShapeDtypeStruct((1, SMALL_LANES), F32),
        ],
        scratch_shapes=[
            pltpu.VMEM((ts, IN_WIDTH), BF16),
            pltpu.VMEM((1, ATTN_WIDTH), F32),
            pltpu.VMEM((1, KV_WIDTH), F32),
        ],
        rider=rider,
    )


BIG_WEIGHTS = (
    ("w_in", True, IN_WIDTH // N_DEV, D_MODEL),
    ("w_out", False, D_MODEL // N_DEV, D_MODEL),
    ("w_gate", True, D_FF // N_DEV, D_MODEL),
    ("w_up", True, D_FF // N_DEV, D_MODEL),
    ("w_down", False, D_FF // N_DEV, D_MODEL),
    ("w_ple_gate", False, D_MODEL // N_DEV, D_MODEL),
    ("w_ple_proj", False, PLE_DIM, D_MODEL // N_DEV),
)
N_BIG = len(BIG_WEIGHTS)


def _place():
    x, y, c = lax.axis_index("x"), lax.axis_index("y"), lax.axis_index("c")
    chips = [(1 - x, y), (x, 1 - y), (1 - x, 1 - y)]
    return x, y, c, chips


class _Gather:
    def __init__(self, n, rows=None):
        self.n = n
        self.rows = rows or [None] * n
        self.sems = [pltpu.SemaphoreType.DMA((n, 7)), pltpu.SemaphoreType.DMA((n, 7)), pltpu.SemaphoreType.DMA((n,))]

    def _ctx(self, srcs, outs, sems):
        send_sems, recv_sems, local_sems = sems
        x, y, c, chips = _place()
        me, sibling = (x, y, c), (x, y, 1 - c)

        def part(k, ref):
            return ref if self.rows[k] is None else ref.at[pl.ds(*self.rows[k]), :]

        def block(k, owner):
            px, py, pc = owner
            return part(k, outs[k].at[4 * px + 2 * py + pc])

        def copy(k, idx, owner, to, mine=False):
            return pltpu.make_async_remote_copy(
                src_ref=part(k, srcs[k]) if mine else block(k, owner), dst_ref=block(k, owner),
                send_sem=send_sems.at[k, idx], recv_sem=recv_sems.at[k, idx], device_id=to, device_id_type=MESH)

        def local(k):
            return pltpu.make_async_copy(part(k, srcs[k]), block(k, me), local_sems.at[k])

        return c, chips, me, sibling, copy, local

    def begin(self, srcs, outs, sems):
        c, chips, me, sibling, copy, local = self._ctx(srcs, outs, sems)
        for k in range(self.n):
            local(k).start()
            copy(k, 0, me, sibling, mine=True).start()
            for j, chip in enumerate(chips):
                copy(k, 1 + j, me, (*chip, c), mine=True).start()

    def middle(self, srcs, outs, sems):
        c, chips, me, sibling, copy, local = self._ctx(srcs, outs, sems)
        for j, chip in enumerate(chips):
            for k in range(self.n):
                copy(k, 1 + j, (*chip, c), me).wait_recv()
                copy(k, 4 + j, (*chip, c), sibling).start()

    def end(self, srcs, outs, sems):
        c, chips, me, sibling, copy, local = self._ctx(srcs, outs, sems)
        for k in range(self.n):
            copy(k, 0, sibling, me).wait_recv()
            for j, chip in enumerate(chips):
                copy(k, 4 + j, (*chip, 1 - c), me).wait_recv()
        for k in range(self.n):
            copy(k, 0, me, sibling, mine=True).wait_send()
            for j, chip in enumerate(chips):
                copy(k, 1 + j, me, (*chip, c), mine=True).wait_send()
                copy(k, 4 + j, (*chip, c), sibling).wait_send()
            local(k).wait()


def _gather_rider(items):
    items = [it if isinstance(it, tuple) else (it, None, None, None) for it in items]
    n = len(items)
    g = _Gather(n, [None if r0 is None else (r0, nr) for _, r0, nr, _ in items])
    shapes = [jax.ShapeDtypeStruct((N_DEV, *sh.shape), sh.dtype) for sh, _, _, _ in items]
    stacks = [(k, st) for k, (_, _, _, st) in enumerate(items) if st is not None]
    aliases = {n + i: k for i, (k, _) in enumerate(stacks)}
    return _Rider([sh for sh, _, _, _ in items] + [st for _, st in stacks], shapes, g.sems, g.begin, g.end, g.middle,
                  aliases=aliases)


def _cast_and_gather_first(shards, rel_bias_t):
    g = _Gather(1)
    any_spec = pl.BlockSpec(memory_space=pl.ANY)
    vmem = pl.BlockSpec(memory_space=pltpu.VMEM)

    def body(*refs):
        ins, rb_ref, outs = refs[:N_BIG], refs[N_BIG], refs[N_BIG + 1:2 * N_BIG + 1]
        gathered, tab_ref, sems = refs[2 * N_BIG + 1], refs[2 * N_BIG + 2], refs[2 * N_BIG + 3:]
        outs[0][...] = ins[0][...].astype(BF16)
        g.begin(outs[:1], [gathered], sems)
        for k in range(1, N_BIG):
            outs[k][...] = ins[k][...].astype(BF16)
        _write_bias_table(rb_ref, tab_ref)
        g.middle(outs[:1], [gathered], sems)
        g.end(outs[:1], [gathered], sems)

    res = pl.pallas_call(
        body,
        name="cast_and_gather_first",
        in_specs=[vmem] * N_BIG + [pl.BlockSpec(memory_space=pltpu.SMEM)],
        out_specs=[vmem] * N_BIG + [any_spec, vmem],
        out_shape=[jax.ShapeDtypeStruct((r, c), BF16) for _, _, r, c in BIG_WEIGHTS]
        + [jax.ShapeDtypeStruct((N_DEV, *BIG_WEIGHTS[0][2:]), BF16), jax.ShapeDtypeStruct(BIAS_TABLE_SHAPE, F32)],
        scratch_shapes=g.sems,
    )(*shards, rel_bias_t)
    return list(res[:N_BIG]), res[N_BIG], res[N_BIG + 1]


def _sibling_rider(grads):
    n = len(grads)

    def copies(gs, lands, sems):
        send_sems, recv_sems = sems
        x, y, c, _ = _place()
        return [
            pltpu.make_async_remote_copy(
                src_ref=gs[k].at[:, 1 - c], dst_ref=lands[k], send_sem=send_sems.at[k], recv_sem=recv_sems.at[k],
                device_id=(x, y, 1 - c), device_id_type=MESH)
            for k in range(n)
        ]

    def begin(gs, lands, sems):
        for cp in copies(gs, lands, sems):
            cp.start()

    def end(gs, lands, sems):
        for cp in copies(gs, lands, sems):
            cp.wait()

    shapes = [jax.ShapeDtypeStruct((N_CHIPS, *g.shape[2:]), F32) for g in grads]
    return _Rider(grads, shapes, [pltpu.SemaphoreType.DMA((n,)), pltpu.SemaphoreType.DMA((n,))], begin, end)


def _chip_of_relation(j, place):
    x, y = place[0], place[1]
    return jnp.where(j == 0, 2 * (1 - x) + y, jnp.where(j == 1, 2 * x + 1 - y, 2 * (1 - x) + 1 - y))


def _chip_sum(ks, place, grads, from_sibling):
    n = len(ks)
    shapes = [BIG_WEIGHTS[k][2:] for k in ks]
    operands, specs = [], []
    for (r, c), g, l in zip(shapes, grads, from_sibling):
        operands += [g, l]
        specs += [pl.BlockSpec((1, 1, r, c), lambda j, place: (_chip_of_relation(j, place), place[2], 0, 0)),
                  pl.BlockSpec((1, r, c), lambda j, place: (_chip_of_relation(j, place), 0, 0))]
    args, in_specs, _ = _after_last(operands, specs)

    def body(place_ref, *refs):
        ins, outs = refs[:2 * n], refs[len(args):]
        for i in range(n):
            outs[i][0] = (ins[2 * i][0, 0] + ins[2 * i + 1][0]).astype(BF16)

    outs = pl.pallas_call(
        body,
        name="chip_sum_" + "_".join(BIG_WEIGHTS[k][0] for k in ks),
        grid_spec=pltpu.PrefetchScalarGridSpec(
            num_scalar_prefetch=1,
            grid=(N_CHIPS - 1,),
            in_specs=in_specs,
            out_specs=[pl.BlockSpec((1, r, c), lambda j, place: (j, 0, 0)) for r, c in shapes],
        ),
        out_shape=[jax.ShapeDtypeStruct((N_CHIPS - 1, r, c), BF16) for r, c in shapes],
    )(place, *args)
    _mark_issued(outs[0])
    return list(outs)


def _chips_rider(to_send, small=None):
    n = len(to_send)
    inputs = list(to_send) + ([] if small is None else [small])
    shapes = [jax.ShapeDtypeStruct((3, *t.shape[1:]), BF16) for t in to_send]
    sems = [pltpu.SemaphoreType.DMA((max(n, 1), 3)), pltpu.SemaphoreType.DMA((max(n, 1), 3))]
    if small is not None:
        shapes.append(jax.ShapeDtypeStruct((N_DEV, *small.shape), F32))
        sems += [pltpu.SemaphoreType.DMA((7,)), pltpu.SemaphoreType.DMA((7,)), pltpu.SemaphoreType.DMA]

    def copies(ins, outs, sem_refs):
        x, y, c, chips = _place()
        out = []
        for k in range(n):
            for j, (px, py) in enumerate(chips):
                out.append(pltpu.make_async_remote_copy(
                    src_ref=ins[k].at[j], dst_ref=outs[k].at[j],
                    send_sem=sem_refs[0].at[k, j], recv_sem=sem_refs[1].at[k, j],
                    device_id=(px, py, c), device_id_type=MESH))
        local = None
        if small is not None:
            me = 4 * x + 2 * y + c
            local = pltpu.make_async_copy(ins[n], outs[n].at[me], sem_refs[4])
            rel = 0
            for fx in (0, 1):
                for fy in (0, 1):
                    for fc in (0, 1):
                        if (fx, fy, fc) != (0, 0, 0):
                            out.append(pltpu.make_async_remote_copy(
                                src_ref=ins[n], dst_ref=outs[n].at[me],
                                send_sem=sem_refs[2].at[rel], recv_sem=sem_refs[3].at[rel],
                                device_id=(x ^ fx, y ^ fy, c ^ fc), device_id_type=MESH))
                            rel += 1
        return out, local

    def begin(ins, outs, sem_refs):
        remote, local = copies(ins, outs, sem_refs)
        if local is not None:
            local.start()
        for cp in remote:
            cp.start()

    def end(ins, outs, sem_refs):
        remote, local = copies(ins, outs, sem_refs)
        for cp in remote:
            cp.wait()
        if local is not None:
            local.wait()

    return _Rider(inputs, shapes, sems, begin, end)


def _exchange(name, rider):
    return _call(lambda: None, (), name=name, grid=(1,), in_specs=[], out_specs=[], out_shape=[], rider=rider)[1]


PEER_SETS = {"sibling": 1, "chips": 2, "sibling+chips": 3, "all": 4}


def _peers(pattern):
    x, y, c, chips = _place()
    sibling, others = [(x, y, 1 - c)], [(*chip, c) for chip in chips]
    if pattern == "all":
        return sibling + others + [(*chip, 1 - c) for chip in chips]
    return {"sibling": sibling, "chips": others, "sibling+chips": sibling + others}[pattern]


def _on_sequencer(name, pattern, rider):
    assert not rider.aliases
    n_in, n_out = len(rider.inputs), len(rider.out_shapes)

    def body(*refs):
        ins, outs, sems = refs[:n_in], refs[n_in:n_in + n_out], refs[n_in + n_out:]
        peers = _peers(pattern)
        barrier = pltpu.get_barrier_semaphore()
        for peer in peers:
            pl.semaphore_signal(barrier, inc=1, device_id=peer, device_id_type=MESH)
        pl.semaphore_wait(barrier, len(peers))
        rider.begin(ins, outs, sems)
        if rider.middle is not None:
            rider.middle(ins, outs, sems)
        rider.end(ins, outs, sems)

    outs = pl.kernel(
        body,
        name=name,
        out_type=tuple(rider.out_shapes),
        mesh=plsc.ScalarSubcoreMesh(axis_name="sequencer", num_cores=1),
        scratch_types=tuple(rider.sems),
        compiler_params=pltpu.CompilerParams(collective_id=PEER_SETS[pattern]),
    )(*rider.inputs)
    return list(outs)


def _merge_riders(*riders):
    riders = [r for r in riders if r is not None]
    if len(riders) == 1:
        return riders[0]
    assert not any(r.aliases for r in riders)

    def split(refs, counts):
        out, at = [], 0
        for n in counts:
            out.append(refs[at:at + n])
            at += n
        return out

    def run(which):
        def fn(ins, outs, sems):
            parts = zip(riders, split(ins, [len(r.inputs) for r in riders]),
                        split(outs, [len(r.out_shapes) for r in riders]), split(sems, [len(r.sems) for r in riders]))
            for r, i, o, s in parts:
                hook = getattr(r, which)
                if hook is not None:
                    hook(i, o, s)
        return fn

    middle = run("middle") if any(r.middle is not None for r in riders) else None
    return _Rider(sum((r.inputs for r in riders), []), sum((r.out_shapes for r in riders), []),
                  sum((r.sems for r in riders), []), run("begin"), run("end"), middle)


def _split_outputs(outs, *riders):
    res, at = [], 0
    for r in riders:
        res.append(outs[at:at + len(r.out_shapes)])
        at += len(r.out_shapes)
    return res


def _adamw(w, g, m, v):
    m = ADAM_B1 * m + (1.0 - ADAM_B1) * g
    v = ADAM_B2 * v + (1.0 - ADAM_B2) * jnp.square(g)
    m_hat = m / (1.0 - ADAM_B1 ** ADAM_STEP)
    v_hat = v / (1.0 - ADAM_B2 ** ADAM_STEP)
    delta = -ADAM_LR * (m_hat / (jnp.sqrt(v_hat) + ADAM_EPS) + ADAM_WD * w)
    return delta, m, v


def _adamw_big(ks, place, operands):
    n = len(ks)
    tiles = lambda i, place: (i, 0)
    in_specs, out_specs, out_shape = [], [], []
    for k in ks:
        _, _, r, c = BIG_WEIGHTS[k]
        tile = r // 2
        in_specs += [
            pl.BlockSpec((1, 1, tile, c), lambda i, place: (2 * place[0] + place[1], place[2], i, 0)),
            pl.BlockSpec((1, tile, c), lambda i, place: (2 * place[0] + place[1], i, 0)),
            pl.BlockSpec((3, tile, c), lambda i, place: (0, i, 0)),
        ] + [pl.BlockSpec((tile, c), tiles)] * 3
        out_specs += [pl.BlockSpec((tile, c), tiles)] * 4
        out_shape += [jax.ShapeDtypeStruct((r, c), F32)] * 4
    args, in_specs, _ = _after_last(sum((list(ops) for ops in operands), []), in_specs)

    def body(place_ref, *refs):
        ins, outs = refs[:6 * n], refs[len(args):]
        for i in range(n):
            mine_ref, sib_ref, land_ref, w_ref, m_ref, v_ref = ins[6 * i:6 * i + 6]
            g_ref, d_ref, nm_ref, nv_ref = outs[4 * i:4 * i + 4]
            g = mine_ref[0, 0] + sib_ref[0]
            g = ((g + land_ref[0].astype(F32)) + land_ref[1].astype(F32)) + land_ref[2].astype(F32)
            g_ref[...] = g
            d_ref[...], nm_ref[...], nv_ref[...] = _adamw(w_ref[...], g, m_ref[...], v_ref[...])

    outs = pl.pallas_call(
        body,
        name="adamw_" + "_".join(BIG_WEIGHTS[k][0] for k in ks),
        grid_spec=pltpu.PrefetchScalarGridSpec(
            num_scalar_prefetch=1, grid=(2,), in_specs=in_specs, out_specs=out_specs),
        out_shape=out_shape,
    )(place, *args)
    _mark_issued(outs[0])
    return [outs[4 * i:4 * i + 4] for i in range(n)]


def _pack_small(arrays):
    rows, offsets = [], []
    at = 0
    for a in arrays:
        if a.ndim != 2 or a.shape[1] != SMALL_LANES or a.shape[0] % 8:
            flat = a.reshape(-1)
            n_rows = -(-flat.shape[0] // (8 * SMALL_LANES)) * 8
            a = jnp.pad(flat, (0, n_rows * SMALL_LANES - flat.shape[0])).reshape(n_rows, SMALL_LANES)
        rows.append(a)
        offsets.append(at)
        at += a.shape[0]
    return jnp.concatenate(rows, axis=0), offsets


def _unpack_small(tot, at, shape):
    r, c = shape
    if r % 8 == 0:
        return tot[at:at + r, :c]
    assert r == 1
    if c <= SMALL_LANES:
        return tot[at:at + 1, :c]
    return jnp.concatenate([tot[at + j:at + j + 1, :] for j in range(c // SMALL_LANES)], axis=1)


def _small_update(packs, loss_at, grads_at, ws, ms, vs):
    n, n_packs = len(ws), len(packs)

    def body(*refs):
        pack_refs, refs = refs[:n_packs], refs[n_packs:]
        w_refs, m_refs, v_refs, loss_ref, outs = refs[:n], refs[n:2 * n], refs[2 * n:3 * n], refs[3 * n], refs[3 * n + 1:]
        tots = []
        for p_ref in pack_refs:
            tot = p_ref[0]
            for j in range(1, N_DEV):
                tot = tot + p_ref[j]
            tots.append(tot)
        loss_ref[...] = _unpack_small(tots[loss_at[0]], loss_at[1], (1, 1))
        for i, (pack, at) in enumerate(grads_at):
            g = _unpack_small(tots[pack], at, w_refs[i].shape)
            outs[i][...] = g
            outs[n + i][...], outs[2 * n + i][...], outs[3 * n + i][...] = _adamw(
                w_refs[i][...], g, m_refs[i][...], v_refs[i][...])

    shapes = [jax.ShapeDtypeStruct(w.shape, F32) for w in ws]
    outs = pl.pallas_call(body, name="small_update", out_shape=[jax.ShapeDtypeStruct((1, 1), F32)] + shapes * 4)(
        *packs, *ws, *ms, *vs)
    return outs[0], outs[1:]


SMALL_NAMES = ("g_attn_norm", "g_q", "g_k", "attn_sinks", "rel_bias", "w_pool", "pool_scale", "g_ffn_norm", "g_ple_norm")


def kernel(x, p, w_in, w_out, g_attn_norm, g_q, g_k, attn_sinks, rel_bias, w_pool, pool_scale, g_ffn_norm, w_gate, w_up, w_down, g_ple_norm, w_ple_gate, w_ple_proj, loss_target, m_w_in, m_w_out, m_g_attn_norm, m_g_q, m_g_k, m_attn_sinks, m_rel_bias, m_w_pool, m_pool_scale, m_g_ffn_norm, m_w_gate, m_w_up, m_w_down, m_g_ple_norm, m_w_ple_gate, m_w_ple_proj, v_w_in, v_w_out, v_g_attn_norm, v_g_q, v_g_k, v_attn_sinks, v_rel_bias, v_w_pool, v_pool_scale, v_g_ffn_norm, v_w_gate, v_w_up, v_w_down, v_g_ple_norm, v_w_ple_gate, v_w_ple_proj):
    weights = dict(w_in=w_in, w_out=w_out, g_attn_norm=g_attn_norm, g_q=g_q, g_k=g_k, attn_sinks=attn_sinks,
                   rel_bias=rel_bias, w_pool=w_pool, pool_scale=pool_scale, g_ffn_norm=g_ffn_norm, w_gate=w_gate,
                   w_up=w_up, w_down=w_down, g_ple_norm=g_ple_norm, w_ple_gate=w_ple_gate, w_ple_proj=w_ple_proj)
    m_in = dict(w_in=m_w_in, w_out=m_w_out, g_attn_norm=m_g_attn_norm, g_q=m_g_q, g_k=m_g_k, attn_sinks=m_attn_sinks,
                rel_bias=m_rel_bias, w_pool=m_w_pool, pool_scale=m_pool_scale, g_ffn_norm=m_g_ffn_norm, w_gate=m_w_gate,
                w_up=m_w_up, w_down=m_w_down, g_ple_norm=m_g_ple_norm, w_ple_gate=m_w_ple_gate, w_ple_proj=m_w_ple_proj)
    v_in = dict(w_in=v_w_in, w_out=v_w_out, g_attn_norm=v_g_attn_norm, g_q=v_g_q, g_k=v_g_k, attn_sinks=v_attn_sinks,
                rel_bias=v_rel_bias, w_pool=v_w_pool, pool_scale=v_pool_scale, g_ffn_norm=v_g_ffn_norm, w_gate=v_w_gate,
                w_up=v_w_up, w_down=v_w_down, g_ple_norm=v_g_ple_norm, w_ple_gate=v_w_ple_gate, w_ple_proj=v_w_ple_proj)

    _issued.clear()
    xs = x[0]
    ps = p[0, 0]
    target = loss_target[0]
    wp = w_pool[0]
    gq_t = jnp.tile(g_q, (1, ATTN_WIDTH // HEAD_DIM))
    gk_t = jnp.tile(g_k, (1, KV_WIDTH // HEAD_DIM))

    def to_blocks(k, arr):
        return jnp.swapaxes(arr[0], 0, 1) if BIG_WEIGHTS[k][1] else arr[0]

    def from_blocks(k, arr):
        return (jnp.swapaxes(arr, 0, 1) if BIG_WEIGHTS[k][1] else arr)[None]

    IN, OUT, GATE, UP, DOWN, PG, PP = range(N_BIG)
    full = lambda g: g.reshape(N_DEV * g.shape[1], g.shape[2])
    halves = lambda k, g: g.reshape(N_CHIPS, 2, *BIG_WEIGHTS[k][2:])
    place = jnp.stack([lax.axis_index("x"), lax.axis_index("y"), lax.axis_index("c")]).astype(jnp.int32)

    sh, w_in_g, tab = _cast_and_gather_first(
        [to_blocks(k, weights[name]) for k, (name, _, _, _) in enumerate(BIG_WEIGHTS)], rel_bias.T)
    w_in_t = full(w_in_g)

    (w_out_g,) = _on_sequencer("gather_out", "sibling+chips", _gather_rider([sh[OUT]]))
    wg_g, wu_g = _on_sequencer("gather_gate_up", "sibling+chips", _gather_rider([sh[GATE], sh[UP]]))
    wd_g, w_pg_g, w_pp_g = _on_sequencer("gather_down_ple", "sibling+chips", _gather_rider([sh[DOWN], sh[PG], sh[PP]]))
    (zqk, qn, kn, v, u), _ = _in_proj(xs, g_attn_norm, w_in_t, gq_t, gk_t)
    (a,), _ = _attn_fwd(qn, kn, v, tab, attn_sinks)
    w_out_f = full(w_out_g)
    (h1, hn2, m_out), _ = _mix_out(u, a, xs, w_out_f, wp, pool_scale, g_ffn_norm)
    wg_t, wu_t = full(wg_g), full(wu_g)
    (gt, up), _ = _ffn_up(hn2, wg_t, wu_t)
    w_down_f = full(wd_g)

    partial, from_sibling, sums, landed = [None] * N_BIG, [None] * N_BIG, [None] * N_BIG, [None] * N_BIG

    def to_sibling(name, ks, grads):
        for k, g in zip(ks, grads):
            partial[k] = halves(k, g)
        got = _on_sequencer(name, "sibling", _sibling_rider([partial[k] for k in ks]))
        for k, g in zip(ks, got):
            from_sibling[k] = g

    def chip_sum(*ks):
        for k, s in zip(ks, _chip_sum(ks, place, [partial[k] for k in ks], [from_sibling[k] for k in ks])):
            sums[k] = s

    def to_chips(name, ks, small=None):
        got = _on_sequencer(name, "chips" if small is None else "all", _chips_rider([sums[k] for k in ks], small))
        for k, g in zip(ks, got):
            landed[k] = g
        return got[len(ks):]

    (loss_part, dh2, d_wpg, d_wpp, d_g_ple), _ = _ffn_down_ple(
        gt, up, h1, w_down_f, ps, target, g_ple_norm, full(w_pg_g), w_pp_g)
    to_sibling("sibling_ple", (PG, PP), (d_wpg, d_wpp))
    (dgt, dup, dh1, dh1b, d_g_ffn, d_wd), _ = _ffn_bwd_act(dh2, h1, gt, up, g_ffn_norm, wg_t, wu_t, w_down_f)
    to_sibling("sibling_down", (DOWN,), (d_wd,))
    chip_sum(PG, PP)
    to_chips("chips_ple", (PG, PP))
    (d_wg_t, d_wu_t), _ = _ffn_bwd_w(dgt, dup, hn2)
    to_sibling("sibling_gate_up", (GATE, UP), (d_wg_t, d_wu_t))
    _complete_before_next([landed[PG], landed[PP]])
    chip_sum(DOWN)
    to_chips("chips_down", (DOWN,))
    (d_wo,), _ = _out_w_bwd(a, m_out, dh1b)
    to_sibling("sibling_out", (OUT,), (d_wo,))
    chip_sum(GATE, UP)
    to_chips("chips_gate_up", (GATE, UP))
    (da, du, d_wpool, d_scale), _ = _mix_bwd(dh1b, u, w_out_f, wp, pool_scale)
    chip_sum(OUT)
    early, early_at = _pack_small([d_wpool.reshape(POOL_WIDTH, POOL_GROUP), d_scale, d_g_ffn, d_g_ple, loss_part[:, :1]])
    (early_all,) = to_chips("chips_out", (OUT,), early)
    _complete_before_next([landed[DOWN]])
    (dqn, dkn, dv, dl_acc, d_sinks), _ = _attn_bwd(qn, kn, v, a, da, tab, attn_sinks)
    (grad_x, d_win_t, d_g_attn, d_gq, d_gk), _ = _in_proj_bwd(dqn, dkn, dv, du, zqk, xs, dh1, g_attn_norm, gq_t, gk_t, w_in_t)
    to_sibling("sibling_in", (IN,), (d_win_t,))
    _complete_before_next([landed[OUT], landed[GATE], landed[UP], early_all])
    (d_rel_t,), _ = _bias_table_bwd(dl_acc)
    chip_sum(IN)
    late, late_at = _pack_small([d_g_attn, d_gq[:, :HEAD_DIM], d_gk[:, :HEAD_DIM], d_sinks[:, 0], d_rel_t])
    (late_all,) = to_chips("chips_in", (IN,), late)

    out = {"grad": {}, "delta": {}, "new_m": {}, "new_v": {}}
    for ks in ((PG, PP, DOWN), (OUT, GATE, UP), (IN,)):
        names = [BIG_WEIGHTS[k][0] for k in ks]
        results = _adamw_big(ks, place, [
            (partial[k], from_sibling[k], landed[k], to_blocks(k, weights[n]), to_blocks(k, m_in[n]),
             to_blocks(k, v_in[n])) for k, n in zip(ks, names)])
        for k, name, res in zip(ks, names, results):
            for kind, r in zip(("grad", "delta", "new_m", "new_v"), res):
                out[kind][name] = from_blocks(k, r)
    def as_rows(name, arr):
        return arr.T if name == "rel_bias" else arr.reshape(POOL_WIDTH, POOL_GROUP) if name == "w_pool" else arr

    def from_rows(name, arr):
        return arr.T if name == "rel_bias" else arr.reshape(w_pool.shape) if name == "w_pool" else arr

    grads_at = dict(w_pool=(0, early_at[0]), pool_scale=(0, early_at[1]), g_ffn_norm=(0, early_at[2]),
                    g_ple_norm=(0, early_at[3]), g_attn_norm=(1, late_at[0]), g_q=(1, late_at[1]), g_k=(1, late_at[2]),
                    attn_sinks=(1, late_at[3]), rel_bias=(1, late_at[4]))
    loss, updates = _small_update(
        [early_all, late_all], (0, early_at[4]), [grads_at[n] for n in SMALL_NAMES],
        [as_rows(n, weights[n]) for n in SMALL_NAMES], [as_rows(n, m_in[n]) for n in SMALL_NAMES],
        [as_rows(n, v_in[n]) for n in SMALL_NAMES])
    loss = loss.reshape(())
    n_small = len(SMALL_NAMES)
    for j, kind in enumerate(("grad", "delta", "new_m", "new_v")):
        for i, name in enumerate(SMALL_NAMES):
            out[kind][name] = from_rows(name, updates[j * n_small + i])

    _issued.clear()
    order = ("w_in", "w_out", "g_attn_norm", "g_q", "g_k", "attn_sinks", "rel_bias", "w_pool", "pool_scale",
             "g_ffn_norm", "w_gate", "w_up", "w_down", "g_ple_norm", "w_ple_gate", "w_ple_proj")
    return (loss, grad_x[None], *[out["grad"][n] for n in order], *[out["delta"][n] for n in order],
            *[out["new_m"][n] for n in order], *[out["new_v"][n] for n in order])
```

```python
import functools
import math

import jax
import jax.numpy as jnp
import numpy as np
from jax import lax
from jax.experimental import pallas as pl
from jax.experimental.pallas import tpu as pltpu
from jax.experimental.pallas import tpu_sc as plsc

F32 = jnp.float32
BF16 = jnp.bfloat16
MESH = pl.DeviceIdType.MESH

D_MODEL = 1024
HEAD_DIM = 64
ATTN_WIDTH = 512
KV_WIDTH = 128
POOL_WIDTH = 512
POOL_SIZES = (2, 4, 8, 16)
POOL_GROUP = 128
POOL_HALO = 16
IN_WIDTH = 1280
D_FF = 2816
PLE_DIM = 256
BLOCK = 128
N_BUCKETS = 32
MAX_DISTANCE = 128
EPS = 1e-6
N_DEV = 8
N_CHIPS = 4

ADAM_LR = 0.001
ADAM_B1 = 0.9
ADAM_B2 = 0.999
ADAM_EPS = 1e-08
ADAM_WD = 0.01
ADAM_STEP = 10

TOKEN_TILE = 512
FFN_BWD_TILE = 256
FF_CHUNK = 256
ATTN_STEP_BLOCKS = 4
GATE_ROWS_EARLY = 96
UP_ROWS_EARLY = 64
HEADS_A = (0, 2, 5, 7)
HEADS_B = (1, 3, 4, 6)
SMALL_LANES = 128


def _nn(a, b):
    return jnp.dot(a, b, preferred_element_type=F32)


def _nt(a, b):
    return lax.dot_general(a, b, (((1,), (1,)), ((), ())), preferred_element_type=F32)


def _tn(a, b):
    return lax.dot_general(a, b, (((0,), (0,)), ((), ())), preferred_element_type=F32)


def _resident(shape):
    nd = len(shape)
    return pl.BlockSpec(shape, lambda i, _nd=nd: (0,) * _nd, pipeline_mode=pl.Buffered(1))


def _rows(tile, width):
    return pl.BlockSpec((tile, width), lambda i: (i, 0))


def _acc(shape):
    nd = len(shape)
    return pl.BlockSpec(shape, lambda i, _nd=nd: (0,) * _nd)


def _head_mean_matrix(width):
    idx = np.arange(width) // HEAD_DIM
    return jnp.asarray((idx[:, None] == idx[None, :]).astype(np.float32) / HEAD_DIM, dtype=BF16)


def _seg_mean(v, bmat):
    hi = v.astype(BF16)
    lo = (v - hi.astype(F32)).astype(BF16)
    return _nn(hi, bmat) + _nn(lo, bmat)


def _rms(x):
    return lax.rsqrt(jnp.mean(x * x, axis=-1, keepdims=True) + EPS)


def _rms_bwd(d_y, x, r, g):
    gy = d_y * g
    d_x = r * gy - x * (r * r * r) * jnp.mean(gy * x, axis=-1, keepdims=True)
    d_g = jnp.sum(d_y * (x * r), axis=0, keepdims=True)
    return d_x, d_g


def _lane_lo(shape):
    return lax.broadcasted_iota(jnp.int32, shape, 1) < HEAD_DIM


class _Rider:
    def __init__(self, inputs, out_shapes, sems, begin, end, middle=None, aliases=None):
        self.inputs, self.out_shapes, self.sems = list(inputs), list(out_shapes), list(sems)
        self.begin, self.middle, self.end = begin, middle, end
        self.aliases = dict(aliases or {})


_issued = []


def _after_last(args, in_specs):
    extra = list(_issued)
    return list(args) + extra, list(in_specs) + [pl.BlockSpec(memory_space=pl.ANY)] * len(extra), len(extra)


def _mark_issued(out):
    _issued[:] = [out]


def _complete_before_next(arrays):
    _issued.extend(arrays)


def _call(body, args, *, name, grid, in_specs, out_specs, out_shape, scratch_shapes=(), rider=None):
    in_specs, out_specs, out_shape, scratch_shapes = list(in_specs), list(out_specs), list(out_shape), list(scratch_shapes)
    if rider is None:
        n_args = len(args)
        args, in_specs, _ = _after_last(args, in_specs)

        def ordered(*refs):
            body(*refs[:n_args], *refs[len(args):])

        outs = pl.pallas_call(ordered, name=name, grid=grid, in_specs=in_specs, out_specs=out_specs, out_shape=out_shape,
                              scratch_shapes=scratch_shapes)(*args)
        _mark_issued(outs[0])
        return list(outs), []
    n_in, n_out, n_scr = len(in_specs), len(out_shape), len(scratch_shapes)
    r_in, r_out = len(rider.inputs), len(rider.out_shapes)
    n_steps = grid[0]

    def hosted(*refs):
        ins, refs = refs[:n_in], refs[n_in:]
        r_ins, refs = refs[:r_in], refs[r_in:]
        outs, refs = refs[:n_out], refs[n_out:]
        r_outs, refs = refs[:r_out], refs[r_out:]
        scratch, r_sems = refs[:n_scr], refs[n_scr:]
        step = pl.program_id(0)

        @pl.when(step == 0)
        def _():
            rider.begin(r_ins, r_outs, r_sems)

        if rider.middle is not None:
            @pl.when(step == n_steps - 1)
            def _():
                rider.middle(r_ins, r_outs, r_sems)

        body(*ins, *outs, *scratch)

        @pl.when(step == n_steps - 1)
        def _():
            rider.end(r_ins, r_outs, r_sems)

    any_spec = pl.BlockSpec(memory_space=pl.ANY)
    outs = pl.pallas_call(
        hosted, name=name, grid=grid,
        in_specs=in_specs + [any_spec] * r_in,
        out_specs=out_specs + [any_spec] * r_out,
        out_shape=out_shape + rider.out_shapes,
        scratch_shapes=scratch_shapes + rider.sems,
        input_output_aliases={n_in + i: n_out + o for i, o in rider.aliases.items()},
    )(*args, *rider.inputs)
    return list(outs[:n_out]), list(outs[n_out:])


def _in_proj(x, g_attn, w_in_t, gq_t, gk_t, rider=None):
    s = x.shape[0]
    ts = min(TOKEN_TILE, s)

    def body(x_ref, g_ref, w_ref, gq_ref, gk_ref, bq_ref, bk_ref, zqk_ref, qn_ref, kn_ref, v_ref, u_ref):
        xf = x_ref[...]
        hn = ((xf * _rms(xf)) * g_ref[...]).astype(BF16)
        z = _nt(hn, w_ref[...])
        q = z[:, :ATTN_WIDTH]
        k = z[:, ATTN_WIDTH:ATTN_WIDTH + KV_WIDTH]
        zqk_ref[...] = z[:, :ATTN_WIDTH + KV_WIDTH]
        rq = lax.rsqrt(_seg_mean(q * q, bq_ref[...]) + EPS)
        qn_ref[...] = ((q * rq) * gq_ref[...]).astype(BF16)
        rk = lax.rsqrt(_seg_mean(k * k, bk_ref[...]) + EPS)
        kn_ref[...] = ((k * rk) * gk_ref[...]).astype(BF16)
        v_ref[...] = z[:, ATTN_WIDTH + KV_WIDTH:ATTN_WIDTH + 2 * KV_WIDTH].astype(BF16)
        u_ref[...] = z[:, ATTN_WIDTH + 2 * KV_WIDTH:]

    return _call(
        body,
        (x, g_attn, w_in_t, gq_t, gk_t, _head_mean_matrix(ATTN_WIDTH), _head_mean_matrix(KV_WIDTH)),
        name="in_proj",
        grid=(s // ts,),
        in_specs=[
            _rows(ts, D_MODEL),
            _resident((1, D_MODEL)),
            _resident((IN_WIDTH, D_MODEL)),
            _resident((1, ATTN_WIDTH)),
            _resident((1, KV_WIDTH)),
            _resident((ATTN_WIDTH, ATTN_WIDTH)),
            _resident((KV_WIDTH, KV_WIDTH)),
        ],
        out_specs=[
            _rows(ts, ATTN_WIDTH + KV_WIDTH),
            _rows(ts, ATTN_WIDTH),
            _rows(ts, KV_WIDTH),
            _rows(ts, KV_WIDTH),
            _rows(ts, POOL_WIDTH),
        ],
        out_shape=[
            jax.ShapeDtypeStruct((s, ATTN_WIDTH + KV_WIDTH), F32),
            jax.ShapeDtypeStruct((s, ATTN_WIDTH), BF16),
            jax.ShapeDtypeStruct((s, KV_WIDTH), BF16),
            jax.ShapeDtypeStruct((s, KV_WIDTH), BF16),
            jax.ShapeDtypeStruct((s, POOL_WIDTH), F32),
        ],
        rider=rider,
    )


def _bucket_ranges():
    n = np.arange(MAX_DISTANCE)
    max_exact = N_BUCKETS // 2
    nf = np.maximum(n, 1).astype(np.float64)
    large = max_exact + (np.log(nf / max_exact) / math.log(MAX_DISTANCE / max_exact) * (N_BUCKETS - max_exact)).astype(np.int64)
    bucket = np.where(n < max_exact, n, np.minimum(large, N_BUCKETS - 1))
    out = []
    for b in range(N_BUCKETS):
        idx = np.nonzero(bucket == b)[0]
        out.append((int(idx.min()), int(idx.max()) + 1))
    return out


def _band_distance():
    i = lax.broadcasted_iota(jnp.int32, (BLOCK, 2 * BLOCK), 0)
    j = lax.broadcasted_iota(jnp.int32, (BLOCK, 2 * BLOCK), 1)
    return BLOCK + i - j


BIAS_TABLE_SHAPE = (2, 4 * BLOCK, 2 * BLOCK)


def _write_bias_table(rb_ref, tab_ref):
    d = _band_distance()
    for half, heads in enumerate((HEADS_A, HEADS_B)):
        for slot, h in enumerate(heads):
            t = jnp.full((BLOCK, 2 * BLOCK), -jnp.inf, F32)
            for b, (lo, hi) in enumerate(_bucket_ranges()):
                t = jnp.where((d >= lo) & (d < hi), rb_ref[h, b], t)
            tab_ref[half, slot * BLOCK:(slot + 1) * BLOCK, :] = t


def _bias_table_bwd(dl_acc, rider=None):
    ranges = _bucket_ranges()
    n_heads = len(HEADS_A) + len(HEADS_B)

    def body(dl_ref, out_ref):
        d = _band_distance()
        row = lax.broadcasted_iota(jnp.int32, (n_heads, SMALL_LANES), 0)
        lane = lax.broadcasted_iota(jnp.int32, (n_heads, SMALL_LANES), 1)
        out = jnp.zeros((n_heads, SMALL_LANES), F32)
        for b, (lo, hi) in enumerate(ranges):
            in_bucket = (d >= lo) & (d < hi)
            for half, heads in enumerate((HEADS_A, HEADS_B)):
                for slot, h in enumerate(heads):
                    g = dl_ref[half, slot * BLOCK:(slot + 1) * BLOCK, :]
                    part = jnp.sum(jnp.where(in_bucket, g, 0.0), axis=0, keepdims=True)
                    tot = jnp.sum(part, axis=1, keepdims=True)
                    out = jnp.where((row == h) & (lane == b), tot, out)
        out_ref[...] = out

    return _call(
        body,
        (dl_acc,),
        name="bias_table_bwd",
        grid=(1,),
        in_specs=[_acc((2, 4 * BLOCK, 2 * BLOCK))],
        out_specs=[_acc((n_heads, SMALL_LANES))],
        out_shape=[jax.ShapeDtypeStruct((n_heads, SMALL_LANES), F32)],
        rider=rider,
    )


def _stack_heads(pairs, lo_mask):
    zero = jnp.zeros_like(pairs[0])
    lo = [jnp.where(lo_mask, t, zero) for t in pairs]
    hi = [jnp.where(lo_mask, zero, t) for t in pairs]
    return (jnp.concatenate([lo[0], lo[1], hi[2], hi[3]], axis=0),
            jnp.concatenate([hi[0], hi[1], lo[2], lo[3]], axis=0))


def _unstack_heads(out_a, out_b, lo_mask):
    t = lambda x, r: x[r * BLOCK:(r + 1) * BLOCK, :]
    return [
        jnp.where(lo_mask, t(out_a, 0), t(out_b, 0)),
        jnp.where(lo_mask, t(out_a, 1), t(out_b, 1)),
        jnp.where(lo_mask, t(out_b, 2), t(out_a, 2)),
        jnp.where(lo_mask, t(out_b, 3), t(out_a, 3)),
    ]


def _sink_column(sink_ref, heads):
    row = lax.broadcasted_iota(jnp.int32, (4 * BLOCK, 1), 0)
    col = jnp.full((4 * BLOCK, 1), sink_ref[0, heads[3]], F32)
    for slot in (2, 1, 0):
        col = jnp.where(row < (slot + 1) * BLOCK, sink_ref[0, heads[slot]], col)
    return col


def _band_probs(q_stack, keys, tab, sink, first_block):
    s = _nt(q_stack, keys) * (HEAD_DIM ** -0.5) + tab
    if first_block is not None:
        col = lax.broadcasted_iota(jnp.int32, s.shape, 1)
        s = jnp.where(jnp.logical_and(first_block, col < BLOCK), -jnp.inf, s)
    m = jnp.maximum(jnp.max(s, axis=-1, keepdims=True), sink)
    e = jnp.exp(s - m)
    e_sink = jnp.exp(sink - m)
    den = jnp.sum(e, axis=-1, keepdims=True) + e_sink
    return e / den, e_sink / den


def _attn_specs(n_groups):
    group = lambda n: (jnp.minimum(n, n_groups - 1), 0)
    prev = lambda n: (jnp.maximum(jnp.minimum(n, n_groups - 1) * ATTN_STEP_BLOCKS - 1, 0), 0)
    return group, prev


def _band(prev_ref, group_ref, b):
    rows = lambda i: group_ref[i * BLOCK:(i + 1) * BLOCK, :]
    band = jnp.concatenate([prev_ref[...] if b == 0 else rows(b - 1), rows(b)], axis=0)
    return band, pltpu.roll(band, HEAD_DIM, 1)


def _attn_fwd(qn, kn, v, tab, sinks, rider=None):
    s = qn.shape[0]
    n_groups = s // (ATTN_STEP_BLOCKS * BLOCK)
    group, prev = _attn_specs(n_groups)
    rows = ATTN_STEP_BLOCKS * BLOCK

    def body(sink_ref, q_ref, kc_ref, kp_ref, vc_ref, vp_ref, tab_ref, o_ref):
        first = pl.program_id(0) == 0
        lo_mask = _lane_lo((BLOCK, BLOCK))
        for b in range(ATTN_STEP_BLOCKS):
            at = slice(b * BLOCK, (b + 1) * BLOCK)
            kk, kk_sw = _band(kp_ref, kc_ref, b)
            vv, vv_sw = _band(vp_ref, vc_ref, b)
            q_a, q_b = _stack_heads([q_ref[at, p * BLOCK:(p + 1) * BLOCK] for p in range(4)], lo_mask)
            no_prev = first if b == 0 else None
            p_a, _ = _band_probs(q_a, kk, tab_ref[0], _sink_column(sink_ref, HEADS_A), no_prev)
            p_b, _ = _band_probs(q_b, kk_sw, tab_ref[1], _sink_column(sink_ref, HEADS_B), no_prev)
            out = _unstack_heads(_nn(p_a.astype(BF16), vv), _nn(p_b.astype(BF16), vv_sw), lo_mask)
            for p in range(4):
                o_ref[at, p * BLOCK:(p + 1) * BLOCK] = out[p].astype(BF16)

    return _call(
        body,
        (sinks, qn, kn, kn, v, v, tab),
        name="attn_fwd",
        grid=(n_groups,),
        in_specs=[
            pl.BlockSpec(memory_space=pltpu.SMEM),
            pl.BlockSpec((rows, ATTN_WIDTH), group),
            pl.BlockSpec((rows, KV_WIDTH), group),
            pl.BlockSpec((BLOCK, KV_WIDTH), prev),
            pl.BlockSpec((rows, KV_WIDTH), group),
            pl.BlockSpec((BLOCK, KV_WIDTH), prev),
            _resident((2, 4 * BLOCK, 2 * BLOCK)),
        ],
        out_specs=[pl.BlockSpec((rows, ATTN_WIDTH), group)],
        out_shape=[jax.ShapeDtypeStruct((s, ATTN_WIDTH), BF16)],
        rider=rider,
    )


def _pooled(u_tile, u_halo, tile_index, tile_rows):
    halo = jnp.where(tile_index > 0, u_halo, 0.0)
    ext = jnp.concatenate([halo, u_tile], axis=0)
    sums = []
    acc = ext
    for shift in (1, 2, 4, 8):
        acc = acc + pltpu.roll(acc, shift, 0)
        sums.append(acc)
    t = tile_index * tile_rows + lax.broadcasted_iota(jnp.int32, (tile_rows, 1), 0)
    out = []
    for g, w in enumerate(POOL_SIZES):
        lanes = slice(g * POOL_GROUP, (g + 1) * POOL_GROUP)
        cnt = jnp.minimum(t + 1, w).astype(F32)
        out.append(sums[g][POOL_HALO:, lanes] / cnt - u_tile[:, lanes])
    return out


def _halo_before(tile):
    return lambda i: (jnp.maximum(i * (tile // POOL_HALO) - 1, 0), 0)


def _mix_out(u, a, x, w_out, w_pool, pool_scale, g_ffn, rider=None):
    s = x.shape[0]
    ts = min(TOKEN_TILE, s)

    def body(u_ref, uh_ref, a_ref, x_ref, wo_ref, wp_ref, sc_ref, g_ref, h1_ref, hn_ref, m_ref):
        i = pl.program_id(0)
        pooled = _pooled(u_ref[...], uh_ref[...], i, ts)
        for g in range(len(POOL_SIZES)):
            lanes = slice(g * POOL_GROUP, (g + 1) * POOL_GROUP)
            y = _nn(pooled[g].astype(BF16), wp_ref[g].astype(BF16))
            m_ref[:, lanes] = (y * sc_ref[:, lanes]).astype(BF16)
        h1 = x_ref[...] + _nn(a_ref[...], wo_ref[:ATTN_WIDTH, :]) + _nn(m_ref[...], wo_ref[ATTN_WIDTH:, :])
        h1_ref[...] = h1
        hn_ref[...] = ((h1 * _rms(h1)) * g_ref[...]).astype(BF16)

    return _call(
        body,
        (u, u, a, x, w_out, w_pool, pool_scale, g_ffn),
        name="mix_out",
        grid=(s // ts,),
        in_specs=[
            _rows(ts, POOL_WIDTH),
            pl.BlockSpec((POOL_HALO, POOL_WIDTH), _halo_before(ts)),
            _rows(ts, ATTN_WIDTH),
            _rows(ts, D_MODEL),
            _resident((D_MODEL, D_MODEL)),
            _resident((len(POOL_SIZES), POOL_GROUP, POOL_GROUP)),
            _resident((1, POOL_WIDTH)),
            _resident((1, D_MODEL)),
        ],
        out_specs=[_rows(ts, D_MODEL), _rows(ts, D_MODEL), _rows(ts, POOL_WIDTH)],
        out_shape=[
            jax.ShapeDtypeStruct((s, D_MODEL), F32),
            jax.ShapeDtypeStruct((s, D_MODEL), BF16),
            jax.ShapeDtypeStruct((s, POOL_WIDTH), BF16),
        ],
        rider=rider,
    )


def _ffn_up(hn2, wg_t, wu_t, rider=None):
    s = hn2.shape[0]
    ts = min(TOKEN_TILE, s)

    def body(hn_ref, wg_ref, wu_ref, gt_ref, up_ref):
        hn = hn_ref[...]
        for c in range(D_FF // FF_CHUNK):
            cols = slice(c * FF_CHUNK, (c + 1) * FF_CHUNK)
            gt_ref[:, cols] = _nt(hn, wg_ref[cols, :]).astype(BF16)
            up_ref[:, cols] = _nt(hn, wu_ref[cols, :]).astype(BF16)

    return _call(
        body,
        (hn2, wg_t, wu_t),
        name="ffn_up",
        grid=(s // ts,),
        in_specs=[_rows(ts, D_MODEL), _resident((D_FF, D_MODEL)), _resident((D_FF, D_MODEL))],
        out_specs=[_rows(ts, D_FF), _rows(ts, D_FF)],
        out_shape=[jax.ShapeDtypeStruct((s, D_FF), BF16), jax.ShapeDtypeStruct((s, D_FF), BF16)],
        rider=rider,
    )


def _silu_mul(gt, up):
    return (gt * jax.nn.sigmoid(gt)) * up


def _ffn_down_ple(gt, up, h1, w_down, p, target, g_ple, w_pg, w_pp, rider=None):
    s = h1.shape[0]
    ts = min(TOKEN_TILE, s)
    blk = D_MODEL // N_DEV

    def body(gt_ref, up_ref, h1_ref, wd_ref, p_ref, t_ref, g_ref, wpg_ref, wpp_ref,
             loss_ref, dh_ref, dwpg_ref, dwpp_ref, dg_ref, act_ref, pp_ref):
        @pl.when(pl.program_id(0) == 0)
        def _():
            loss_ref[...] = jnp.zeros_like(loss_ref)
            dwpg_ref[...] = jnp.zeros_like(dwpg_ref)
            dwpp_ref[...] = jnp.zeros_like(dwpp_ref)
            dg_ref[...] = jnp.zeros_like(dg_ref)

        h2v = h1_ref[...]
        for c in range(D_FF // FF_CHUNK):
            cols = slice(c * FF_CHUNK, (c + 1) * FF_CHUNK)
            act = _silu_mul(gt_ref[:, cols].astype(F32), up_ref[:, cols].astype(F32)).astype(BF16)
            h2v = _nn(act, wd_ref[cols, :]) + h2v
        r = _rms(h2v)
        hn = ((h2v * r) * g_ref[...]).astype(BF16)
        gate = jax.nn.sigmoid(_nn(hn, wpg_ref[...]))
        pb = p_ref[...].astype(BF16)
        for j in range(N_DEV):
            pp_ref[:, j * blk:(j + 1) * blk] = _nn(pb, wpp_ref[j])
        pp = pp_ref[...]
        diff = (h2v + gate * pp) - t_ref[...]
        loss_ref[...] += jnp.sum(jnp.sum(diff * diff, axis=0, keepdims=True), axis=1, keepdims=True) * (0.5 / D_MODEL)
        dy = diff * (1.0 / D_MODEL)
        d_pp = (dy * gate).astype(BF16)
        d_pre = ((dy * pp) * (gate * (1.0 - gate))).astype(BF16)
        for j in range(N_DEV):
            dwpp_ref[j] += _tn(pb, d_pp[:, j * blk:(j + 1) * blk])
        dwpg_ref[...] += _tn(hn, d_pre)
        d_x, d_g = _rms_bwd(_nt(d_pre, wpg_ref[...]), h2v, r, g_ref[...])
        dg_ref[...] += d_g
        dh_ref[...] = dy + d_x

    return _call(
        body,
        (gt, up, h1, w_down, p, target, g_ple, w_pg, w_pp),
        name="ffn_down_ple",
        grid=(s // ts,),
        in_specs=[
            _rows(ts, D_FF),
            _rows(ts, D_FF),
            _rows(ts, D_MODEL),
            _resident((D_FF, D_MODEL)),
            _rows(ts, PLE_DIM),
            _rows(ts, D_MODEL),
            _resident((1, D_MODEL)),
            _resident((D_MODEL, D_MODEL)),
            _resident((N_DEV, PLE_DIM, blk)),
        ],
        out_specs=[
            _acc((1, SMALL_LANES)),
            _rows(ts, D_MODEL),
            _acc((D_MODEL, D_MODEL)),
            _acc((N_DEV, PLE_DIM, blk)),
            _acc((1, D_MODEL)),
        ],
        out_shape=[
            jax.ShapeDtypeStruct((1, SMALL_LANES), F32),
            jax.ShapeDtypeStruct((s, D_MODEL), F32),
            jax.ShapeDtypeStruct((D_MODEL, D_MODEL), F32),
            jax.ShapeDtypeStruct((N_DEV, PLE_DIM, blk), F32),
            jax.ShapeDtypeStruct((1, D_MODEL), F32),
        ],
        scratch_shapes=[pltpu.VMEM((ts, D_FF), BF16), pltpu.VMEM((ts, D_MODEL), F32)],
        rider=rider,
    )


def _ffn_bwd_act(dh2, h1, gt, up, g_ffn, wg_t, wu_t, w_down, rider=None):
    s = h1.shape[0]
    ts = min(FFN_BWD_TILE, s)

    def body(dh_ref, h1_ref, gt_ref, up_ref, g_ref, wg_ref, wu_ref, wd_ref,
             dgt_ref, dup_ref, dh1_ref, dh1b_ref, dg_ref, dwd_ref, act_ref):
        @pl.when(pl.program_id(0) == 0)
        def _():
            dg_ref[...] = jnp.zeros_like(dg_ref)
            dwd_ref[...] = jnp.zeros_like(dwd_ref)

        dhb = dh_ref[...].astype(BF16)
        d_hn = jnp.zeros((ts, D_MODEL), F32)
        for c in range(D_FF // FF_CHUNK):
            cols = slice(c * FF_CHUNK, (c + 1) * FF_CHUNK)
            d_act = _nt(dhb, wd_ref[cols, :])
            gtv = gt_ref[:, cols].astype(F32)
            upv = up_ref[:, cols].astype(F32)
            sg = jax.nn.sigmoid(gtv)
            silu = gtv * sg
            act_ref[:, cols] = (silu * upv).astype(BF16)
            d_up = (d_act * silu).astype(BF16)
            d_gt = ((d_act * upv) * (sg * (1.0 + gtv * (1.0 - sg)))).astype(BF16)
            dup_ref[:, cols] = d_up
            dgt_ref[:, cols] = d_gt
            d_hn = (_nn(d_gt, wg_ref[cols, :]) + _nn(d_up, wu_ref[cols, :])) + d_hn
        dwd_ref[...] += _tn(act_ref[...], dhb)
        h1v = h1_ref[...]
        d_x, d_g = _rms_bwd(d_hn, h1v, _rms(h1v), g_ref[...])
        dg_ref[...] += d_g
        dh1 = dh_ref[...] + d_x
        dh1_ref[...] = dh1
        dh1b_ref[...] = dh1.astype(BF16)

    return _call(
        body,
        (dh2, h1, gt, up, g_ffn, wg_t, wu_t, w_down),
        name="ffn_bwd_act",
        grid=(s // ts,),
        in_specs=[
            _rows(ts, D_MODEL),
            _rows(ts, D_MODEL),
            _rows(ts, D_FF),
            _rows(ts, D_FF),
            _resident((1, D_MODEL)),
            _resident((D_FF, D_MODEL)),
            _resident((D_FF, D_MODEL)),
            _resident((D_FF, D_MODEL)),
        ],
        out_specs=[
            _rows(ts, D_FF), _rows(ts, D_FF),
            _rows(ts, D_MODEL), _rows(ts, D_MODEL), _acc((1, D_MODEL)), _acc((D_FF, D_MODEL)),
        ],
        out_shape=[
            jax.ShapeDtypeStruct((s, D_FF), BF16),
            jax.ShapeDtypeStruct((s, D_FF), BF16),
            jax.ShapeDtypeStruct((s, D_MODEL), F32),
            jax.ShapeDtypeStruct((s, D_MODEL), BF16),
            jax.ShapeDtypeStruct((1, D_MODEL), F32),
            jax.ShapeDtypeStruct((D_FF, D_MODEL), F32),
        ],
        scratch_shapes=[pltpu.VMEM((ts, D_FF), BF16)],
        rider=rider,
    )


def _ffn_bwd_w(dgt, dup, hn2, rider=None):
    s = hn2.shape[0]
    slab = pl.BlockSpec((s, FF_CHUNK), lambda i: (0, i))

    def body(dgt_ref, dup_ref, hn_ref, dwg_ref, dwu_ref):
        hn = hn_ref[...]
        dwg_ref[...] = _tn(dgt_ref[...], hn)
        dwu_ref[...] = _tn(dup_ref[...], hn)

    return _call(
        body,
        (dgt, dup, hn2),
        name="ffn_bwd_w",
        grid=(D_FF // FF_CHUNK,),
        in_specs=[slab, slab, _resident((s, D_MODEL))],
        out_specs=[_rows(FF_CHUNK, D_MODEL)] * 2,
        out_shape=[jax.ShapeDtypeStruct((D_FF, D_MODEL), F32)] * 2,
        rider=rider,
    )


def _mix_bwd(dh1b, u, a, m, w_out, w_pool, pool_scale, rider=None):
    s = u.shape[0]
    ts = min(TOKEN_TILE, s)
    nt = s // ts
    halo_after = lambda i: (jnp.minimum((i + 1) * (ts // POOL_HALO), s // POOL_HALO - 1), 0)
    n_groups = len(POOL_SIZES)

    def body(dh_ref, dhn_ref, u_ref, uh_ref, a_ref, m_ref, wo_ref, wp_ref, sc_ref,
             da_ref, du_ref, dwp_ref, dsc_ref, dwo_ref):
        i = pl.program_id(0)

        @pl.when(i == 0)
        def _():
            dwp_ref[...] = jnp.zeros_like(dwp_ref)
            dsc_ref[...] = jnp.zeros_like(dsc_ref)
            dwo_ref[...] = jnp.zeros_like(dwo_ref)

        dh = dh_ref[...]
        dwo_ref[:ATTN_WIDTH, :] += _tn(a_ref[...], dh)
        dwo_ref[ATTN_WIDTH:, :] += _tn(m_ref[...], dh)
        da_ref[...] = _nt(dh, wo_ref[:ATTN_WIDTH, :])
        dh_next = jnp.where(i < nt - 1, dhn_ref[...], jnp.zeros_like(dhn_ref))
        dm_ext = _nt(jnp.concatenate([dh, dh_next], axis=0), wo_ref[ATTN_WIDTH:, :])
        pooled = _pooled(u_ref[...], uh_ref[...], i, ts)
        t_ext = i * ts + lax.broadcasted_iota(jnp.int32, (ts + POOL_HALO, 1), 0)
        for g, w in enumerate(POOL_SIZES):
            lanes = slice(g * POOL_GROUP, (g + 1) * POOL_GROUP)
            wp = wp_ref[g].astype(BF16)
            pg = pooled[g].astype(BF16)
            dm_g = dm_ext[:, lanes]
            dsc_ref[:, lanes] += jnp.sum(dm_g[:ts, :] * _nn(pg, wp), axis=0, keepdims=True)
            dy = (dm_g * sc_ref[:, lanes]).astype(BF16)
            dwp_ref[g] += _tn(pg, dy[:ts, :])
            d_pool = _nt(dy, wp)
            acc = d_pool / jnp.minimum(t_ext + 1, w).astype(F32)
            shift = 1
            while shift < w:
                acc = acc + pltpu.roll(acc, ts + POOL_HALO - shift, 0)
                shift *= 2
            du_ref[:, lanes] = acc[:ts, :] - d_pool[:ts, :]

    return _call(
        body,
        (dh1b, dh1b, u, u, a, m, w_out, w_pool, pool_scale),
        name="mix_bwd",
        grid=(nt,),
        in_specs=[
            _rows(ts, D_MODEL),
            pl.BlockSpec((POOL_HALO, D_MODEL), halo_after),
            _rows(ts, POOL_WIDTH),
            pl.BlockSpec((POOL_HALO, POOL_WIDTH), _halo_before(ts)),
            _rows(ts, ATTN_WIDTH),
            _rows(ts, POOL_WIDTH),
            _resident((D_MODEL, D_MODEL)),
            _resident((n_groups, POOL_GROUP, POOL_GROUP)),
            _resident((1, POOL_WIDTH)),
        ],
        out_specs=[
            _rows(ts, ATTN_WIDTH),
            _rows(ts, POOL_WIDTH),
            _acc((n_groups, POOL_GROUP, POOL_GROUP)),
            _acc((1, POOL_WIDTH)),
            _acc((D_MODEL, D_MODEL)),
        ],
        out_shape=[
            jax.ShapeDtypeStruct((s, ATTN_WIDTH), F32),
            jax.ShapeDtypeStruct((s, POOL_WIDTH), F32),
            jax.ShapeDtypeStruct((n_groups, POOL_GROUP, POOL_GROUP), F32),
            jax.ShapeDtypeStruct((1, POOL_WIDTH), F32),
            jax.ShapeDtypeStruct((D_MODEL, D_MODEL), F32),
        ],
        rider=rider,
    )


def _attn_bwd(qn, kn, v, a, da, tab, sinks, rider=None):
    s = qn.shape[0]
    qb = ATTN_STEP_BLOCKS
    rows = qb * BLOCK
    n_groups = s // rows
    group, prev = _attn_specs(n_groups)
    done = lambda n: (jnp.maximum(n - 1, 0), 0)

    def body(sink_ref, q_ref, kc_ref, kp_ref, vc_ref, vp_ref, o_ref, do_ref, tab_ref,
             dq_ref, dk_ref, dv_ref, dl_ref, ds_ref, k_carry, v_carry, sink_acc):
        n = pl.program_id(0)

        @pl.when(n == 0)
        def _():
            dl_ref[...] = jnp.zeros_like(dl_ref)
            k_carry[...] = jnp.zeros_like(k_carry)
            v_carry[...] = jnp.zeros_like(v_carry)
            sink_acc[...] = jnp.zeros_like(sink_acc)

        @pl.when(n < n_groups)
        def _():
            first = n == 0
            lo_mask = _lane_lo((BLOCK, BLOCK))
            dks, dvs = [], []
            for b in range(qb):
                at = slice(b * BLOCK, (b + 1) * BLOCK)
                keys = _band(kp_ref, kc_ref, b)
                vals = _band(vp_ref, vc_ref, b)
                q_st = _stack_heads([q_ref[at, p * BLOCK:(p + 1) * BLOCK] for p in range(4)], lo_mask)
                do_st = _stack_heads([do_ref[at, p * BLOCK:(p + 1) * BLOCK] for p in range(4)], lo_mask)
                o_st = _stack_heads([o_ref[at, p * BLOCK:(p + 1) * BLOCK].astype(F32) for p in range(4)], lo_mask)
                dq_st, dk_parts, dv_parts = [], [], []
                for half, heads in enumerate((HEADS_A, HEADS_B)):
                    probs, p_sink = _band_probs(q_st[half], keys[half], tab_ref[half], _sink_column(sink_ref, heads),
                                                first if b == 0 else None)
                    delta = jnp.sum(do_st[half] * o_st[half], axis=-1, keepdims=True)
                    dob = do_st[half].astype(BF16)
                    dl = probs * (_nt(dob, vals[half]) - delta)
                    dl_ref[half] += dl
                    sink_acc[half] += p_sink * delta
                    dsb = (dl * (HEAD_DIM ** -0.5)).astype(BF16)
                    dq_st.append(_nn(dsb, keys[half]))
                    dk_parts.append(_tn(dsb, q_st[half]))
                    dv_parts.append(_tn(probs.astype(BF16), dob))
                dq = _unstack_heads(dq_st[0], dq_st[1], lo_mask)
                for p in range(4):
                    dq_ref[at, p * BLOCK:(p + 1) * BLOCK] = dq[p]
                dks.append(dk_parts[0] + pltpu.roll(dk_parts[1], HEAD_DIM, 1))
                dvs.append(dv_parts[0] + pltpu.roll(dv_parts[1], HEAD_DIM, 1))
            last = slice((qb - 1) * BLOCK, qb * BLOCK)
            for parts, out_ref, carry in ((dks, dk_ref, k_carry), (dvs, dv_ref, v_carry)):
                out_ref[...] = carry[...]
                out_ref[last, :] += parts[0][:BLOCK, :]
                for b in range(qb):
                    own = parts[b][BLOCK:, :]
                    carry[b * BLOCK:(b + 1) * BLOCK, :] = own + parts[b + 1][:BLOCK, :] if b + 1 < qb else own

        @pl.when(n == n_groups)
        def _():
            dk_ref[...] = k_carry[...]
            dv_ref[...] = v_carry[...]
            for half, heads in enumerate((HEADS_A, HEADS_B)):
                for slot, h in enumerate(heads):
                    tot = jnp.sum(sink_acc[half, slot * BLOCK:(slot + 1) * BLOCK, :], axis=0, keepdims=True)
                    ds_ref[h:h + 1, :] = jnp.broadcast_to(-tot, (1, SMALL_LANES))

    return _call(
        body,
        (sinks, qn, kn, kn, v, v, a, da, tab),
        name="attn_bwd",
        grid=(n_groups + 1,),
        in_specs=[
            pl.BlockSpec(memory_space=pltpu.SMEM),
            pl.BlockSpec((rows, ATTN_WIDTH), group),
            pl.BlockSpec((rows, KV_WIDTH), group),
            pl.BlockSpec((BLOCK, KV_WIDTH), prev),
            pl.BlockSpec((rows, KV_WIDTH), group),
            pl.BlockSpec((BLOCK, KV_WIDTH), prev),
            pl.BlockSpec((rows, ATTN_WIDTH), group),
            pl.BlockSpec((rows, ATTN_WIDTH), group),
            _resident((2, 4 * BLOCK, 2 * BLOCK)),
        ],
        out_specs=[
            pl.BlockSpec((rows, ATTN_WIDTH), group),
            pl.BlockSpec((rows, KV_WIDTH), done),
            pl.BlockSpec((rows, KV_WIDTH), done),
            _acc((2, 4 * BLOCK, 2 * BLOCK)),
            _acc((N_DEV, SMALL_LANES)),
        ],
        out_shape=[
            jax.ShapeDtypeStruct((s, ATTN_WIDTH), F32),
            jax.ShapeDtypeStruct((s, KV_WIDTH), F32),
            jax.ShapeDtypeStruct((s, KV_WIDTH), F32),
            jax.ShapeDtypeStruct((2, 4 * BLOCK, 2 * BLOCK), F32),
            jax.ShapeDtypeStruct((N_DEV, SMALL_LANES), F32),
        ],
        scratch_shapes=[
            pltpu.VMEM((rows, KV_WIDTH), F32),
            pltpu.VMEM((rows, KV_WIDTH), F32),
            pltpu.VMEM((2, 4 * BLOCK, 1), F32),
        ],
        rider=rider,
    )


def _fold_heads(acc):
    t = acc + pltpu.roll(acc, HEAD_DIM, 1)
    out = t[:, :SMALL_LANES]
    for g in range(1, acc.shape[1] // SMALL_LANES):
        out = out + t[:, g * SMALL_LANES:(g + 1) * SMALL_LANES]
    return out


def _in_proj_bwd(dqn, dkn, dv, du, zqk, x, dh1, g_attn, gq_t, gk_t, w_in_t, rider=None):
    s = x.shape[0]
    ts = min(TOKEN_TILE, s)
    nt = s // ts

    def head_norm_bwd(d_n, raw, g_t, bmat):
        r = lax.rsqrt(_seg_mean(raw * raw, bmat) + EPS)
        gy = d_n * g_t
        d_raw = r * gy - raw * (r * r * r) * _seg_mean(gy * raw, bmat)
        return d_raw, jnp.sum(d_n * (raw * r), axis=0, keepdims=True)

    def body(dqn_ref, dkn_ref, dv_ref, du_ref, zqk_ref, x_ref, dh1_ref, g_ref, gq_ref, gk_ref, w_ref, bq_ref, bk_ref,
             gx_ref, dw_ref, dg_ref, dgq_ref, dgk_ref, dz_ref, gq_acc, gk_acc):
        i = pl.program_id(0)

        @pl.when(i == 0)
        def _():
            dw_ref[...] = jnp.zeros_like(dw_ref)
            dg_ref[...] = jnp.zeros_like(dg_ref)
            gq_acc[...] = jnp.zeros_like(gq_acc)
            gk_acc[...] = jnp.zeros_like(gk_acc)

        d_q, d_gq = head_norm_bwd(dqn_ref[...], zqk_ref[:, :ATTN_WIDTH], gq_ref[...], bq_ref[...])
        d_k, d_gk = head_norm_bwd(dkn_ref[...], zqk_ref[:, ATTN_WIDTH:], gk_ref[...], bk_ref[...])
        gq_acc[...] += d_gq
        gk_acc[...] += d_gk
        dz_ref[:, :ATTN_WIDTH] = d_q.astype(BF16)
        dz_ref[:, ATTN_WIDTH:ATTN_WIDTH + KV_WIDTH] = d_k.astype(BF16)
        dz_ref[:, ATTN_WIDTH + KV_WIDTH:ATTN_WIDTH + 2 * KV_WIDTH] = dv_ref[...].astype(BF16)
        dz_ref[:, ATTN_WIDTH + 2 * KV_WIDTH:] = du_ref[...].astype(BF16)
        dz = dz_ref[...]
        xf = x_ref[...]
        r = _rms(xf)
        hn = ((xf * r) * g_ref[...]).astype(BF16)
        dw_ref[...] += _tn(dz, hn)
        d_x, d_g = _rms_bwd(_nn(dz, w_ref[...]), xf, r, g_ref[...])
        dg_ref[...] += d_g
        gx_ref[...] = dh1_ref[...] + d_x

        @pl.when(i == nt - 1)
        def _():
            dgq_ref[...] = _fold_heads(gq_acc[...])
            dgk_ref[...] = _fold_heads(gk_acc[...])

    return _call(
        body,
        (dqn, dkn, dv, du, zqk, x, dh1, g_attn, gq_t, gk_t, w_in_t,
      _head_mean_matrix(ATTN_WIDTH), _head_mean_matrix(KV_WIDTH)),
        name="in_proj_bwd",
        grid=(nt,),
        in_specs=[
            _rows(ts, ATTN_WIDTH),
            _rows(ts, KV_WIDTH),
            _rows(ts, KV_WIDTH),
            _rows(ts, POOL_WIDTH),
            _rows(ts, ATTN_WIDTH + KV_WIDTH),
            _rows(ts, D_MODEL),
            _rows(ts, D_MODEL),
            _resident((1, D_MODEL)),
            _resident((1, ATTN_WIDTH)),
            _resident((1, KV_WIDTH)),
            _resident((IN_WIDTH, D_MODEL)),
            _resident((ATTN_WIDTH, ATTN_WIDTH)),
            _resident((KV_WIDTH, KV_WIDTH)),
        ],
        out_specs=[
            _rows(ts, D_MODEL),
            _acc((IN_WIDTH, D_MODEL)),
            _acc((1, D_MODEL)),
            _acc((1, SMALL_LANES)),
            _acc((1, SMALL_LANES)),
        ],
        out_shape=[
            jax.ShapeDtypeStruct((s, D_MODEL), F32),
            jax.ShapeDtypeStruct((IN_WIDTH, D_MODEL), F32),
            jax.ShapeDtypeStruct((1, D_MODEL), F32),
            jax.ShapeDtypeStruct((1, SMALL_LANES), F32),
            jax.ShapeDtypeStruct((1, SMALL_LANES), F32),
        ],
        scratch_shapes=[
            pltpu.VMEM((ts, IN_WIDTH), BF16),
            pltpu.VMEM((1, ATTN_WIDTH), F32),
            pltpu.VMEM((1, KV_WIDTH), F32),
        ],
        rider=rider,
    )


BIG_WEIGHTS = (
    ("w_in", True, IN_WIDTH // N_DEV, D_MODEL),
    ("w_out", False, D_MODEL // N_DEV, D_MODEL),
    ("w_gate", True, D_FF // N_DEV, D_MODEL),
    ("w_up", True, D_FF // N_DEV, D_MODEL),
    ("w_down", False, D_FF // N_DEV, D_MODEL),
    ("w_ple_gate", False, D_MODEL // N_DEV, D_MODEL),
    ("w_ple_proj", False, PLE_DIM, D_MODEL // N_DEV),
)
N_BIG = len(BIG_WEIGHTS)


def _place():
    x, y, c = lax.axis_index("x"), lax.axis_index("y"), lax.axis_index("c")
    chips = [(1 - x, y), (x, 1 - y), (1 - x, 1 - y)]
    return x, y, c, chips


class _Gather:
    def __init__(self, n, rows=None):
        self.n = n
        self.rows = rows or [None] * n
        self.sems = [pltpu.SemaphoreType.DMA((n, 7)), pltpu.SemaphoreType.DMA((n, 7)), pltpu.SemaphoreType.DMA((n,))]

    def _ctx(self, srcs, outs, sems):
        send_sems, recv_sems, local_sems = sems
        x, y, c, chips = _place()
        me, sibling = (x, y, c), (x, y, 1 - c)

        def part(k, ref):
            return ref if self.rows[k] is None else ref.at[pl.ds(*self.rows[k]), :]

        def block(k, owner):
            px, py, pc = owner
            return part(k, outs[k].at[4 * px + 2 * py + pc])

        def copy(k, idx, owner, to, mine=False):
            return pltpu.make_async_remote_copy(
                src_ref=part(k, srcs[k]) if mine else block(k, owner), dst_ref=block(k, owner),
                send_sem=send_sems.at[k, idx], recv_sem=recv_sems.at[k, idx], device_id=to, device_id_type=MESH)

        def local(k):
            return pltpu.make_async_copy(part(k, srcs[k]), block(k, me), local_sems.at[k])

        return c, chips, me, sibling, copy, local

    def begin(self, srcs, outs, sems):
        c, chips, me, sibling, copy, local = self._ctx(srcs, outs, sems)
        for k in range(self.n):
            local(k).start()
            copy(k, 0, me, sibling, mine=True).start()
            for j, chip in enumerate(chips):
                copy(k, 1 + j, me, (*chip, c), mine=True).start()

    def middle(self, srcs, outs, sems):
        c, chips, me, sibling, copy, local = self._ctx(srcs, outs, sems)
        for j, chip in enumerate(chips):
            for k in range(self.n):
                copy(k, 1 + j, (*chip, c), me).wait_recv()
                copy(k, 4 + j, (*chip, c), sibling).start()

    def end(self, srcs, outs, sems):
        c, chips, me, sibling, copy, local = self._ctx(srcs, outs, sems)
        for k in range(self.n):
            copy(k, 0, sibling, me).wait_recv()
            for j, chip in enumerate(chips):
                copy(k, 4 + j, (*chip, 1 - c), me).wait_recv()
        for k in range(self.n):
            copy(k, 0, me, sibling, mine=True).wait_send()
            for j, chip in enumerate(chips):
                copy(k, 1 + j, me, (*chip, c), mine=True).wait_send()
                copy(k, 4 + j, (*chip, c), sibling).wait_send()
            local(k).wait()


def _gather_rider(items):
    items = [it if isinstance(it, tuple) else (it, None, None, None) for it in items]
    n = len(items)
    g = _Gather(n, [None if r0 is None else (r0, nr) for _, r0, nr, _ in items])
    shapes = [jax.ShapeDtypeStruct((N_DEV, *sh.shape), sh.dtype) for sh, _, _, _ in items]
    stacks = [(k, st) for k, (_, _, _, st) in enumerate(items) if st is not None]
    aliases = {n + i: k for i, (k, _) in enumerate(stacks)}
    return _Rider([sh for sh, _, _, _ in items] + [st for _, st in stacks], shapes, g.sems, g.begin, g.end, g.middle,
                  aliases=aliases)


def _cast_and_gather_first(shards, rel_bias_t):
    g = _Gather(1)
    any_spec = pl.BlockSpec(memory_space=pl.ANY)
    vmem = pl.BlockSpec(memory_space=pltpu.VMEM)

    def body(*refs):
        ins, rb_ref, outs = refs[:N_BIG], refs[N_BIG], refs[N_BIG + 1:2 * N_BIG + 1]
        gathered, tab_ref, sems = refs[2 * N_BIG + 1], refs[2 * N_BIG + 2], refs[2 * N_BIG + 3:]
        outs[0][...] = ins[0][...].astype(BF16)
        g.begin(outs[:1], [gathered], sems)
        for k in range(1, N_BIG):
            outs[k][...] = ins[k][...].astype(BF16)
        _write_bias_table(rb_ref, tab_ref)
        g.middle(outs[:1], [gathered], sems)
        g.end(outs[:1], [gathered], sems)

    res = pl.pallas_call(
        body,
        name="cast_and_gather_first",
        in_specs=[vmem] * N_BIG + [pl.BlockSpec(memory_space=pltpu.SMEM)],
        out_specs=[vmem] * N_BIG + [any_spec, vmem],
        out_shape=[jax.ShapeDtypeStruct((r, c), BF16) for _, _, r, c in BIG_WEIGHTS]
        + [jax.ShapeDtypeStruct((N_DEV, *BIG_WEIGHTS[0][2:]), BF16), jax.ShapeDtypeStruct(BIAS_TABLE_SHAPE, F32)],
        scratch_shapes=g.sems,
    )(*shards, rel_bias_t)
    return list(res[:N_BIG]), res[N_BIG], res[N_BIG + 1]


def _sibling_rider(grads):
    n = len(grads)

    def copies(gs, lands, sems):
        send_sems, recv_sems = sems
        x, y, c, _ = _place()
        return [
            pltpu.make_async_remote_copy(
                src_ref=gs[k].at[:, 1 - c], dst_ref=lands[k], send_sem=send_sems.at[k], recv_sem=recv_sems.at[k],
                device_id=(x, y, 1 - c), device_id_type=MESH)
            for k in range(n)
        ]

    def begin(gs, lands, sems):
        for cp in copies(gs, lands, sems):
            cp.start()

    def end(gs, lands, sems):
        for cp in copies(gs, lands, sems):
            cp.wait()

    shapes = [jax.ShapeDtypeStruct((N_CHIPS, *g.shape[2:]), F32) for g in grads]
    return _Rider(grads, shapes, [pltpu.SemaphoreType.DMA((n,)), pltpu.SemaphoreType.DMA((n,))], begin, end)


def _chip_of_relation(j, place):
    x, y = place[0], place[1]
    return jnp.where(j == 0, 2 * (1 - x) + y, jnp.where(j == 1, 2 * x + 1 - y, 2 * (1 - x) + 1 - y))


def _chip_sum(ks, place, grads, from_sibling):
    n = len(ks)
    shapes = [BIG_WEIGHTS[k][2:] for k in ks]
    operands, specs = [], []
    for (r, c), g, l in zip(shapes, grads, from_sibling):
        operands += [g, l]
        specs += [pl.BlockSpec((1, 1, r, c), lambda j, place: (_chip_of_relation(j, place), place[2], 0, 0)),
                  pl.BlockSpec((1, r, c), lambda j, place: (_chip_of_relation(j, place), 0, 0))]
    args, in_specs, _ = _after_last(operands, specs)

    def body(place_ref, *refs):
        ins, outs = refs[:2 * n], refs[len(args):]
        for i in range(n):
            outs[i][0] = (ins[2 * i][0, 0] + ins[2 * i + 1][0]).astype(BF16)

    outs = pl.pallas_call(
        body,
        name="chip_sum_" + "_".join(BIG_WEIGHTS[k][0] for k in ks),
        grid_spec=pltpu.PrefetchScalarGridSpec(
            num_scalar_prefetch=1,
            grid=(N_CHIPS - 1,),
            in_specs=in_specs,
            out_specs=[pl.BlockSpec((1, r, c), lambda j, place: (j, 0, 0)) for r, c in shapes],
        ),
        out_shape=[jax.ShapeDtypeStruct((N_CHIPS - 1, r, c), BF16) for r, c in shapes],
    )(place, *args)
    _mark_issued(outs[0])
    return list(outs)


def _chips_rider(to_send, small=None):
    n = len(to_send)
    inputs = list(to_send) + ([] if small is None else [small])
    shapes = [jax.ShapeDtypeStruct((3, *t.shape[1:]), BF16) for t in to_send]
    sems = [pltpu.SemaphoreType.DMA((max(n, 1), 3)), pltpu.SemaphoreType.DMA((max(n, 1), 3))]
    if small is not None:
        shapes.append(jax.ShapeDtypeStruct((N_DEV, *small.shape), F32))
        sems += [pltpu.SemaphoreType.DMA((7,)), pltpu.SemaphoreType.DMA((7,)), pltpu.SemaphoreType.DMA]

    def copies(ins, outs, sem_refs):
        x, y, c, chips = _place()
        out = []
        for k in range(n):
            for j, (px, py) in enumerate(chips):
                out.append(pltpu.make_async_remote_copy(
                    src_ref=ins[k].at[j], dst_ref=outs[k].at[j],
                    send_sem=sem_refs[0].at[k, j], recv_sem=sem_refs[1].at[k, j],
                    device_id=(px, py, c), device_id_type=MESH))
        local = None
        if small is not None:
            me = 4 * x + 2 * y + c
            local = pltpu.make_async_copy(ins[n], outs[n].at[me], sem_refs[4])
            rel = 0
            for fx in (0, 1):
                for fy in (0, 1):
                    for fc in (0, 1):
                        if (fx, fy, fc) != (0, 0, 0):
                            out.append(pltpu.make_async_remote_copy(
                                src_ref=ins[n], dst_ref=outs[n].at[me],
                                send_sem=sem_refs[2].at[rel], recv_sem=sem_refs[3].at[rel],
                                device_id=(x ^ fx, y ^ fy, c ^ fc), device_id_type=MESH))
                            rel += 1
        return out, local

    def begin(ins, outs, sem_refs):
        remote, local = copies(ins, outs, sem_refs)
        if local is not None:
            local.start()
        for cp in remote:
            cp.start()

    def end(ins, outs, sem_refs):
        remote, local = copies(ins, outs, sem_refs)
        for cp in remote:
            cp.wait()
        if local is not None:
            local.wait()

    return _Rider(inputs, shapes, sems, begin, end)


def _exchange(name, rider):
    return _call(lambda: None, (), name=name, grid=(1,), in_specs=[], out_specs=[], out_shape=[], rider=rider)[1]


PEER_SETS = {"sibling": 1, "chips": 2, "sibling+chips": 3, "all": 4}


def _peers(pattern):
    x, y, c, chips = _place()
    sibling, others = [(x, y, 1 - c)], [(*chip, c) for chip in chips]
    if pattern == "all":
        return sibling + others + [(*chip, 1 - c) for chip in chips]
    return {"sibling": sibling, "chips": others, "sibling+chips": sibling + others}[pattern]


def _on_sequencer(name, pattern, rider):
    assert not rider.aliases
    n_in, n_out = len(rider.inputs), len(rider.out_shapes)

    def body(*refs):
        ins, outs, sems = refs[:n_in], refs[n_in:n_in + n_out], refs[n_in + n_out:]
        peers = _peers(pattern)
        barrier = pltpu.get_barrier_semaphore()
        for peer in peers:
            pl.semaphore_signal(barrier, inc=1, device_id=peer, device_id_type=MESH)
        pl.semaphore_wait(barrier, len(peers))
        rider.begin(ins, outs, sems)
        if rider.middle is not None:
            rider.middle(ins, outs, sems)
        rider.end(ins, outs, sems)

    outs = pl.kernel(
        body,
        name=name,
        out_type=tuple(rider.out_shapes),
        mesh=plsc.ScalarSubcoreMesh(axis_name="sequencer", num_cores=1),
        scratch_types=tuple(rider.sems),
        compiler_params=pltpu.CompilerParams(collective_id=PEER_SETS[pattern]),
    )(*rider.inputs)
    return list(outs)


def _merge_riders(*riders):
    riders = [r for r in riders if r is not None]
    if len(riders) == 1:
        return riders[0]
    assert not any(r.aliases for r in riders)

    def split(refs, counts):
        out, at = [], 0
        for n in counts:
            out.append(refs[at:at + n])
            at += n
        return out

    def run(which):
        def fn(ins, outs, sems):
            parts = zip(riders, split(ins, [len(r.inputs) for r in riders]),
                        split(outs, [len(r.out_shapes) for r in riders]), split(sems, [len(r.sems) for r in riders]))
            for r, i, o, s in parts:
                hook = getattr(r, which)
                if hook is not None:
                    hook(i, o, s)
        return fn

    middle = run("middle") if any(r.middle is not None for r in riders) else None
    return _Rider(sum((r.inputs for r in riders), []), sum((r.out_shapes for r in riders), []),
                  sum((r.sems for r in riders), []), run("begin"), run("end"), middle)


def _split_outputs(outs, *riders):
    res, at = [], 0
    for r in riders:
        res.append(outs[at:at + len(r.out_shapes)])
        at += len(r.out_shapes)
    return res


def _adamw(w, g, m, v):
    m = ADAM_B1 * m + (1.0 - ADAM_B1) * g
    v = ADAM_B2 * v + (1.0 - ADAM_B2) * jnp.square(g)
    m_hat = m / (1.0 - ADAM_B1 ** ADAM_STEP)
    v_hat = v / (1.0 - ADAM_B2 ** ADAM_STEP)
    delta = -ADAM_LR * (m_hat / (jnp.sqrt(v_hat) + ADAM_EPS) + ADAM_WD * w)
    return delta, m, v


def _adamw_big(ks, place, operands):
    n = len(ks)
    tiles = lambda i, place: (i, 0)
    in_specs, out_specs, out_shape = [], [], []
    for k in ks:
        _, _, r, c = BIG_WEIGHTS[k]
        tile = r // 2
        in_specs += [
            pl.BlockSpec((1, 1, tile, c), lambda i, place: (2 * place[0] + place[1], place[2], i, 0)),
            pl.BlockSpec((1, tile, c), lambda i, place: (2 * place[0] + place[1], i, 0)),
            pl.BlockSpec((3, tile, c), lambda i, place: (0, i, 0)),
        ] + [pl.BlockSpec((tile, c), tiles)] * 3
        out_specs += [pl.BlockSpec((tile, c), tiles)] * 4
        out_shape += [jax.ShapeDtypeStruct((r, c), F32)] * 4
    args, in_specs, _ = _after_last(sum((list(ops) for ops in operands), []), in_specs)

    def body(place_ref, *refs):
        ins, outs = refs[:6 * n], refs[len(args):]
        for i in range(n):
            mine_ref, sib_ref, land_ref, w_ref, m_ref, v_ref = ins[6 * i:6 * i + 6]
            g_ref, d_ref, nm_ref, nv_ref = outs[4 * i:4 * i + 4]
            g = mine_ref[0, 0] + sib_ref[0]
            g = ((g + land_ref[0].astype(F32)) + land_ref[1].astype(F32)) + land_ref[2].astype(F32)
            g_ref[...] = g
            d_ref[...], nm_ref[...], nv_ref[...] = _adamw(w_ref[...], g, m_ref[...], v_ref[...])

    outs = pl.pallas_call(
        body,
        name="adamw_" + "_".join(BIG_WEIGHTS[k][0] for k in ks),
        grid_spec=pltpu.PrefetchScalarGridSpec(
            num_scalar_prefetch=1, grid=(2,), in_specs=in_specs, out_specs=out_specs),
        out_shape=out_shape,
    )(place, *args)
    _mark_issued(outs[0])
    return [outs[4 * i:4 * i + 4] for i in range(n)]


def _pack_small(arrays):
    rows, offsets = [], []
    at = 0
    for a in arrays:
        if a.ndim != 2 or a.shape[1] != SMALL_LANES or a.shape[0] % 8:
            flat = a.reshape(-1)
            n_rows = -(-flat.shape[0] // (8 * SMALL_LANES)) * 8
            a = jnp.pad(flat, (0, n_rows * SMALL_LANES - flat.shape[0])).reshape(n_rows, SMALL_LANES)
        rows.append(a)
        offsets.append(at)
        at += a.shape[0]
    return jnp.concatenate(rows, axis=0), offsets


def _unpack_small(tot, at, shape):
    r, c = shape
    if r % 8 == 0:
        return tot[at:at + r, :c]
    assert r == 1
    if c <= SMALL_LANES:
        return tot[at:at + 1, :c]
    return jnp.concatenate([tot[at + j:at + j + 1, :] for j in range(c // SMALL_LANES)], axis=1)


def _small_update(packs, loss_at, grads_at, ws, ms, vs):
    n, n_packs = len(ws), len(packs)

    def body(*refs):
        pack_refs, refs = refs[:n_packs], refs[n_packs:]
        w_refs, m_refs, v_refs, loss_ref, outs = refs[:n], refs[n:2 * n], refs[2 * n:3 * n], refs[3 * n], refs[3 * n + 1:]
        tots = []
        for p_ref in pack_refs:
            tot = p_ref[0]
            for j in range(1, N_DEV):
                tot = tot + p_ref[j]
            tots.append(tot)
        loss_ref[...] = _unpack_small(tots[loss_at[0]], loss_at[1], (1, 1))
        for i, (pack, at) in enumerate(grads_at):
            g = _unpack_small(tots[pack], at, w_refs[i].shape)
            outs[i][...] = g
            outs[n + i][...], outs[2 * n + i][...], outs[3 * n + i][...] = _adamw(
                w_refs[i][...], g, m_refs[i][...], v_refs[i][...])

    shapes = [jax.ShapeDtypeStruct(w.shape, F32) for w in ws]
    outs = pl.pallas_call(body, name="small_update", out_shape=[jax.ShapeDtypeStruct((1, 1), F32)] + shapes * 4)(
        *packs, *ws, *ms, *vs)
    return outs[0], outs[1:]


SMALL_NAMES = ("g_attn_norm", "g_q", "g_k", "attn_sinks", "rel_bias", "w_pool", "pool_scale", "g_ffn_norm", "g_ple_norm")


def kernel(x, p, w_in, w_out, g_attn_norm, g_q, g_k, attn_sinks, rel_bias, w_pool, pool_scale, g_ffn_norm, w_gate, w_up, w_down, g_ple_norm, w_ple_gate, w_ple_proj, loss_target, m_w_in, m_w_out, m_g_attn_norm, m_g_q, m_g_k, m_attn_sinks, m_rel_bias, m_w_pool, m_pool_scale, m_g_ffn_norm, m_w_gate, m_w_up, m_w_down, m_g_ple_norm, m_w_ple_gate, m_w_ple_proj, v_w_in, v_w_out, v_g_attn_norm, v_g_q, v_g_k, v_attn_sinks, v_rel_bias, v_w_pool, v_pool_scale, v_g_ffn_norm, v_w_gate, v_w_up, v_w_down, v_g_ple_norm, v_w_ple_gate, v_w_ple_proj):
    weights = dict(w_in=w_in, w_out=w_out, g_attn_norm=g_attn_norm, g_q=g_q, g_k=g_k, attn_sinks=attn_sinks,
                   rel_bias=rel_bias, w_pool=w_pool, pool_scale=pool_scale, g_ffn_norm=g_ffn_norm, w_gate=w_gate,
                   w_up=w_up, w_down=w_down, g_ple_norm=g_ple_norm, w_ple_gate=w_ple_gate, w_ple_proj=w_ple_proj)
    m_in = dict(w_in=m_w_in, w_out=m_w_out, g_attn_norm=m_g_attn_norm, g_q=m_g_q, g_k=m_g_k, attn_sinks=m_attn_sinks,
                rel_bias=m_rel_bias, w_pool=m_w_pool, pool_scale=m_pool_scale, g_ffn_norm=m_g_ffn_norm, w_gate=m_w_gate,
                w_up=m_w_up, w_down=m_w_down, g_ple_norm=m_g_ple_norm, w_ple_gate=m_w_ple_gate, w_ple_proj=m_w_ple_proj)
    v_in = dict(w_in=v_w_in, w_out=v_w_out, g_attn_norm=v_g_attn_norm, g_q=v_g_q, g_k=v_g_k, attn_sinks=v_attn_sinks,
                rel_bias=v_rel_bias, w_pool=v_w_pool, pool_scale=v_pool_scale, g_ffn_norm=v_g_ffn_norm, w_gate=v_w_gate,
                w_up=v_w_up, w_down=v_w_down, g_ple_norm=v_g_ple_norm, w_ple_gate=v_w_ple_gate, w_ple_proj=v_w_ple_proj)

    _issued.clear()
    xs = x[0]
    ps = p[0, 0]
    target = loss_target[0]
    wp = w_pool[0]
    gq_t = jnp.tile(g_q, (1, ATTN_WIDTH // HEAD_DIM))
    gk_t = jnp.tile(g_k, (1, KV_WIDTH // HEAD_DIM))

    def to_blocks(k, arr):
        return jnp.swapaxes(arr[0], 0, 1) if BIG_WEIGHTS[k][1] else arr[0]

    def from_blocks(k, arr):
        return (jnp.swapaxes(arr, 0, 1) if BIG_WEIGHTS[k][1] else arr)[None]

    IN, OUT, GATE, UP, DOWN, PG, PP = range(N_BIG)
    full = lambda g: g.reshape(N_DEV * g.shape[1], g.shape[2])
    halves = lambda k, g: g.reshape(N_CHIPS, 2, *BIG_WEIGHTS[k][2:])
    place = jnp.stack([lax.axis_index("x"), lax.axis_index("y"), lax.axis_index("c")]).astype(jnp.int32)

    sh, w_in_g, tab = _cast_and_gather_first(
        [to_blocks(k, weights[name]) for k, (name, _, _, _) in enumerate(BIG_WEIGHTS)], rel_bias.T)
    w_in_t = full(w_in_g)

    (w_out_g,) = _on_sequencer("gather_out", "sibling+chips", _gather_rider([sh[OUT]]))
    wg_g, wu_g = _on_sequencer("gather_gate_up", "sibling+chips", _gather_rider([sh[GATE], sh[UP]]))
    wd_g, w_pg_g, w_pp_g = _on_sequencer("gather_down_ple", "sibling+chips", _gather_rider([sh[DOWN], sh[PG], sh[PP]]))
    (zqk, qn, kn, v, u), _ = _in_proj(xs, g_attn_norm, w_in_t, gq_t, gk_t)
    (a,), _ = _attn_fwd(qn, kn, v, tab, attn_sinks)
    w_out_f = full(w_out_g)
    (h1, hn2, m_out), _ = _mix_out(u, a, xs, w_out_f, wp, pool_scale, g_ffn_norm)
    wg_t, wu_t = full(wg_g), full(wu_g)
    (gt, up), _ = _ffn_up(hn2, wg_t, wu_t)
    w_down_f = full(wd_g)

    partial, from_sibling, sums, landed = [None] * N_BIG, [None] * N_BIG, [None] * N_BIG, [None] * N_BIG

    def to_sibling(name, ks, grads):
        for k, g in zip(ks, grads):
            partial[k] = halves(k, g)
        got = _on_sequencer(name, "sibling", _sibling_rider([partial[k] for k in ks]))
        for k, g in zip(ks, got):
            from_sibling[k] = g

    def chip_sum(*ks):
        for k, s in zip(ks, _chip_sum(ks, place, [partial[k] for k in ks], [from_sibling[k] for k in ks])):
            sums[k] = s

    def to_chips(name, ks, small=None):
        got = _on_sequencer(name, "chips" if small is None else "all", _chips_rider([sums[k] for k in ks], small))
        for k, g in zip(ks, got):
            landed[k] = g
        return got[len(ks):]

    (loss_part, dh2, d_wpg, d_wpp, d_g_ple), _ = _ffn_down_ple(
        gt, up, h1, w_down_f, ps, target, g_ple_norm, full(w_pg_g), w_pp_g)
    to_sibling("sibling_ple", (PG, PP), (d_wpg, d_wpp))
    (dgt, dup, dh1, dh1b, d_g_ffn, d_wd), _ = _ffn_bwd_act(dh2, h1, gt, up, g_ffn_norm, wg_t, wu_t, w_down_f)
    to_sibling("sibling_down", (DOWN,), (d_wd,))
    chip_sum(PG, PP)
    to_chips("chips_ple", (PG, PP))
    (d_wg_t, d_wu_t), _ = _ffn_bwd_w(dgt, dup, hn2)
    to_sibling("sibling_gate_up", (GATE, UP), (d_wg_t, d_wu_t))
    _complete_before_next([landed[PG], landed[PP]])
    chip_sum(DOWN)
    to_chips("chips_down", (DOWN,))
    (da, du, d_wpool, d_scale, d_wo), _ = _mix_bwd(dh1b, u, a, m_out, w_out_f, wp, pool_scale)
    to_sibling("sibling_out", (OUT,), (d_wo,))
    chip_sum(GATE, UP)
    to_chips("chips_gate_up", (GATE, UP))
    (dqn, dkn, dv, dl_acc, d_sinks), _ = _attn_bwd(qn, kn, v, a, da, tab, attn_sinks)
    _complete_before_next([landed[DOWN]])
    chip_sum(OUT)
    early, early_at = _pack_small([d_wpool.reshape(POOL_WIDTH, POOL_GROUP), d_scale, d_g_ffn, d_g_ple, loss_part[:, :1]])
    (early_all,) = to_chips("chips_out", (OUT,), early)
    (grad_x, d_win_t, d_g_attn, d_gq, d_gk), _ = _in_proj_bwd(dqn, dkn, dv, du, zqk, xs, dh1, g_attn_norm, gq_t, gk_t, w_in_t)
    to_sibling("sibling_in", (IN,), (d_win_t,))
    _complete_before_next([landed[OUT], landed[GATE], landed[UP], early_all])
    (d_rel_t,), _ = _bias_table_bwd(dl_acc)
    chip_sum(IN)
    late, late_at = _pack_small([d_g_attn, d_gq[:, :HEAD_DIM], d_gk[:, :HEAD_DIM], d_sinks[:, 0], d_rel_t])
    (late_all,) = to_chips("chips_in", (IN,), late)

    out = {"grad": {}, "delta": {}, "new_m": {}, "new_v": {}}
    for ks in ((PG, PP, DOWN), (OUT, GATE, UP), (IN,)):
        names = [BIG_WEIGHTS[k][0] for k in ks]
        results = _adamw_big(ks, place, [
            (partial[k], from_sibling[k], landed[k], to_blocks(k, weights[n]), to_blocks(k, m_in[n]),
             to_blocks(k, v_in[n])) for k, n in zip(ks, names)])
        for k, name, res in zip(ks, names, results):
            for kind, r in zip(("grad", "delta", "new_m", "new_v"), res):
                out[kind][name] = from_blocks(k, r)
    def as_rows(name, arr):
        return arr.T if name == "rel_bias" else arr.reshape(POOL_WIDTH, POOL_GROUP) if name == "w_pool" else arr

    def from_rows(name, arr):
        return arr.T if name == "rel_bias" else arr.reshape(w_pool.shape) if name == "w_pool" else arr

    grads_at = dict(w_pool=(0, early_at[0]), pool_scale=(0, early_at[1]), g_ffn_norm=(0, early_at[2]),
                    g_ple_norm=(0, early_at[3]), g_attn_norm=(1, late_at[0]), g_q=(1, late_at[1]), g_k=(1, late_at[2]),
                    attn_sinks=(1, late_at[3]), rel_bias=(1, late_at[4]))
    loss, updates = _small_update(
        [early_all, late_all], (0, early_at[4]), [grads_at[n] for n in SMALL_NAMES],
        [as_rows(n, weights[n]) for n in SMALL_NAMES], [as_rows(n, m_in[n]) for n in SMALL_NAMES],
        [as_rows(n, v_in[n]) for n in SMALL_NAMES])
    loss = loss.reshape(())
    n_small = len(SMALL_NAMES)
    for j, kind in enumerate(("grad", "delta", "new_m", "new_v")):
        for i, name in enumerate(SMALL_NAMES):
            out[kind][name] = from_rows(name, updates[j * n_small + i])

    _issued.clear()
    order = ("w_in", "w_out", "g_attn_norm", "g_q", "g_k", "attn_sinks", "rel_bias", "w_pool", "pool_scale",
             "g_ffn_norm", "w_gate", "w_up", "w_down", "g_ple_norm", "w_ple_gate", "w_ple_proj")
    return (loss, grad_x[None], *[out["grad"][n] for n in order], *[out["delta"][n] for n in order],
            *[out["new_m"][n] for n in order], *[out["new_v"][n] for n in order])
```

```python
import functools
import math

import jax
import jax.numpy as jnp
import numpy as np
from jax import lax
from jax.experimental import pallas as pl
from jax.experimental.pallas import tpu as pltpu
from jax.experimental.pallas import tpu_sc as plsc

F32 = jnp.float32
BF16 = jnp.bfloat16
MESH = pl.DeviceIdType.MESH

D_MODEL = 1024
HEAD_DIM = 64
ATTN_WIDTH = 512
KV_WIDTH = 128
POOL_WIDTH = 512
POOL_SIZES = (2, 4, 8, 16)
POOL_GROUP = 128
POOL_HALO = 16
IN_WIDTH = 1280
D_FF = 2816
PLE_DIM = 256
BLOCK = 128
N_BUCKETS = 32
MAX_DISTANCE = 128
EPS = 1e-6
N_DEV = 8
N_CHIPS = 4

ADAM_LR = 0.001
ADAM_B1 = 0.9
ADAM_B2 = 0.999
ADAM_EPS = 1e-08
ADAM_WD = 0.01
ADAM_STEP = 10

TOKEN_TILE = 512
FFN_BWD_TILE = 256
FF_CHUNK = 256
ATTN_STEP_BLOCKS = 4
GATE_ROWS_EARLY = 96
UP_ROWS_EARLY = 64
HEADS_A = (0, 2, 5, 7)
HEADS_B = (1, 3, 4, 6)
SMALL_LANES = 128


def _nn(a, b):
    return jnp.dot(a, b, preferred_element_type=F32)


def _nt(a, b):
    return lax.dot_general(a, b, (((1,), (1,)), ((), ())), preferred_element_type=F32)


def _tn(a, b):
    return lax.dot_general(a, b, (((0,), (0,)), ((), ())), preferred_element_type=F32)


def _resident(shape):
    nd = len(shape)
    return pl.BlockSpec(shape, lambda i, _nd=nd: (0,) * _nd, pipeline_mode=pl.Buffered(1))


def _rows(tile, width):
    return pl.BlockSpec((tile, width), lambda i: (i, 0))


def _acc(shape):
    nd = len(shape)
    return pl.BlockSpec(shape, lambda i, _nd=nd: (0,) * _nd)


def _head_mean_matrix(width):
    idx = np.arange(width) // HEAD_DIM
    return jnp.asarray((idx[:, None] == idx[None, :]).astype(np.float32) / HEAD_DIM, dtype=BF16)


def _seg_mean(v, bmat):
    hi = v.astype(BF16)
    lo = (v - hi.astype(F32)).astype(BF16)
    return _nn(hi, bmat) + _nn(lo, bmat)


def _rms(x):
    return lax.rsqrt(jnp.mean(x * x, axis=-1, keepdims=True) + EPS)


def _rms_bwd(d_y, x, r, g):
    gy = d_y * g
    d_x = r * gy - x * (r * r * r) * jnp.mean(gy * x, axis=-1, keepdims=True)
    d_g = jnp.sum(d_y * (x * r), axis=0, keepdims=True)
    return d_x, d_g


def _lane_lo(shape):
    return lax.broadcasted_iota(jnp.int32, shape, 1) < HEAD_DIM


class _Rider:
    def __init__(self, inputs, out_shapes, sems, begin, end, middle=None, aliases=None):
        self.inputs, self.out_shapes, self.sems = list(inputs), list(out_shapes), list(sems)
        self.begin, self.middle, self.end = begin, middle, end
        self.aliases = dict(aliases or {})


_issued = []


def _after_last(args, in_specs):
    extra = list(_issued)
    return list(args) + extra, list(in_specs) + [pl.BlockSpec(memory_space=pl.ANY)] * len(extra), len(extra)


def _mark_issued(out):
    _issued[:] = [out]


def _complete_before_next(arrays):
    _issued.extend(arrays)


def _call(body, args, *, name, grid, in_specs, out_specs, out_shape, scratch_shapes=(), rider=None):
    in_specs, out_specs, out_shape, scratch_shapes = list(in_specs), list(out_specs), list(out_shape), list(scratch_shapes)
    if rider is None:
        n_args = len(args)
        args, in_specs, _ = _after_last(args, in_specs)

        def ordered(*refs):
            body(*refs[:n_args], *refs[len(args):])

        outs = pl.pallas_call(ordered, name=name, grid=grid, in_specs=in_specs, out_specs=out_specs, out_shape=out_shape,
                              scratch_shapes=scratch_shapes)(*args)
        _mark_issued(outs[0])
        return list(outs), []
    n_in, n_out, n_scr = len(in_specs), len(out_shape), len(scratch_shapes)
    r_in, r_out = len(rider.inputs), len(rider.out_shapes)
    n_steps = grid[0]

    def hosted(*refs):
        ins, refs = refs[:n_in], refs[n_in:]
        r_ins, refs = refs[:r_in], refs[r_in:]
        outs, refs = refs[:n_out], refs[n_out:]
        r_outs, refs = refs[:r_out], refs[r_out:]
        scratch, r_sems = refs[:n_scr], refs[n_scr:]
        step = pl.program_id(0)

        @pl.when(step == 0)
        def _():
            rider.begin(r_ins, r_outs, r_sems)

        if rider.middle is not None:
            @pl.when(step == n_steps - 1)
            def _():
                rider.middle(r_ins, r_outs, r_sems)

        body(*ins, *outs, *scratch)

        @pl.when(step == n_steps - 1)
        def _():
            rider.end(r_ins, r_outs, r_sems)

    any_spec = pl.BlockSpec(memory_space=pl.ANY)
    outs = pl.pallas_call(
        hosted, name=name, grid=grid,
        in_specs=in_specs + [any_spec] * r_in,
        out_specs=out_specs + [any_spec] * r_out,
        out_shape=out_shape + rider.out_shapes,
        scratch_shapes=scratch_shapes + rider.sems,
        input_output_aliases={n_in + i: n_out + o for i, o in rider.aliases.items()},
    )(*args, *rider.inputs)
    return list(outs[:n_out]), list(outs[n_out:])


def _in_proj(x, g_attn, w_in_t, gq_t, gk_t, rider=None):
    s = x.shape[0]
    ts = min(TOKEN_TILE, s)

    def body(x_ref, g_ref, w_ref, gq_ref, gk_ref, bq_ref, bk_ref, zqk_ref, qn_ref, kn_ref, v_ref, u_ref):
        xf = x_ref[...]
        hn = ((xf * _rms(xf)) * g_ref[...]).astype(BF16)
        z = _nt(hn, w_ref[...])
        q = z[:, :ATTN_WIDTH]
        k = z[:, ATTN_WIDTH:ATTN_WIDTH + KV_WIDTH]
        zqk_ref[...] = z[:, :ATTN_WIDTH + KV_WIDTH]
        rq = lax.rsqrt(_seg_mean(q * q, bq_ref[...]) + EPS)
        qn_ref[...] = ((q * rq) * gq_ref[...]).astype(BF16)
        rk = lax.rsqrt(_seg_mean(k * k, bk_ref[...]) + EPS)
        kn_ref[...] = ((k * rk) * gk_ref[...]).astype(BF16)
        v_ref[...] = z[:, ATTN_WIDTH + KV_WIDTH:ATTN_WIDTH + 2 * KV_WIDTH].astype(BF16)
        u_ref[...] = z[:, ATTN_WIDTH + 2 * KV_WIDTH:]

    return _call(
        body,
        (x, g_attn, w_in_t, gq_t, gk_t, _head_mean_matrix(ATTN_WIDTH), _head_mean_matrix(KV_WIDTH)),
        name="in_proj",
        grid=(s // ts,),
        in_specs=[
            _rows(ts, D_MODEL),
            _resident((1, D_MODEL)),
            _resident((IN_WIDTH, D_MODEL)),
            _resident((1, ATTN_WIDTH)),
            _resident((1, KV_WIDTH)),
            _resident((ATTN_WIDTH, ATTN_WIDTH)),
            _resident((KV_WIDTH, KV_WIDTH)),
        ],
        out_specs=[
            _rows(ts, ATTN_WIDTH + KV_WIDTH),
            _rows(ts, ATTN_WIDTH),
            _rows(ts, KV_WIDTH),
            _rows(ts, KV_WIDTH),
            _rows(ts, POOL_WIDTH),
        ],
        out_shape=[
            jax.ShapeDtypeStruct((s, ATTN_WIDTH + KV_WIDTH), F32),
            jax.ShapeDtypeStruct((s, ATTN_WIDTH), BF16),
            jax.ShapeDtypeStruct((s, KV_WIDTH), BF16),
            jax.ShapeDtypeStruct((s, KV_WIDTH), BF16),
            jax.ShapeDtypeStruct((s, POOL_WIDTH), F32),
        ],
        rider=rider,
    )


def _bucket_ranges():
    n = np.arange(MAX_DISTANCE)
    max_exact = N_BUCKETS // 2
    nf = np.maximum(n, 1).astype(np.float64)
    large = max_exact + (np.log(nf / max_exact) / math.log(MAX_DISTANCE / max_exact) * (N_BUCKETS - max_exact)).astype(np.int64)
    bucket = np.where(n < max_exact, n, np.minimum(large, N_BUCKETS - 1))
    out = []
    for b in range(N_BUCKETS):
        idx = np.nonzero(bucket == b)[0]
        out.append((int(idx.min()), int(idx.max()) + 1))
    return out


def _band_distance():
    i = lax.broadcasted_iota(jnp.int32, (BLOCK, 2 * BLOCK), 0)
    j = lax.broadcasted_iota(jnp.int32, (BLOCK, 2 * BLOCK), 1)
    return BLOCK + i - j


BIAS_TABLE_SHAPE = (2, 4 * BLOCK, 2 * BLOCK)


def _write_bias_table(rb_ref, tab_ref):
    d = _band_distance()
    for half, heads in enumerate((HEADS_A, HEADS_B)):
        for slot, h in enumerate(heads):
            t = jnp.full((BLOCK, 2 * BLOCK), -jnp.inf, F32)
            for b, (lo, hi) in enumerate(_bucket_ranges()):
                t = jnp.where((d >= lo) & (d < hi), rb_ref[h, b], t)
            tab_ref[half, slot * BLOCK:(slot + 1) * BLOCK, :] = t


def _bias_table_bwd(dl_acc, rider=None):
    ranges = _bucket_ranges()
    n_heads = len(HEADS_A) + len(HEADS_B)

    def body(dl_ref, out_ref):
        d = _band_distance()
        row = lax.broadcasted_iota(jnp.int32, (n_heads, SMALL_LANES), 0)
        lane = lax.broadcasted_iota(jnp.int32, (n_heads, SMALL_LANES), 1)
        out = jnp.zeros((n_heads, SMALL_LANES), F32)
        for b, (lo, hi) in enumerate(ranges):
            in_bucket = (d >= lo) & (d < hi)
            for half, heads in enumerate((HEADS_A, HEADS_B)):
                for slot, h in enumerate(heads):
                    g = dl_ref[half, slot * BLOCK:(slot + 1) * BLOCK, :]
                    part = jnp.sum(jnp.where(in_bucket, g, 0.0), axis=0, keepdims=True)
                    tot = jnp.sum(part, axis=1, keepdims=True)
                    out = jnp.where((row == h) & (lane == b), tot, out)
        out_ref[...] = out

    return _call(
        body,
        (dl_acc,),
        name="bias_table_bwd",
        grid=(1,),
        in_specs=[_acc((2, 4 * BLOCK, 2 * BLOCK))],
        out_specs=[_acc((n_heads, SMALL_LANES))],
        out_shape=[jax.ShapeDtypeStruct((n_heads, SMALL_LANES), F32)],
        rider=rider,
    )


def _stack_heads(pairs, lo_mask):
    zero = jnp.zeros_like(pairs[0])
    lo = [jnp.where(lo_mask, t, zero) for t in pairs]
    hi = [jnp.where(lo_mask, zero, t) for t in pairs]
    return (jnp.concatenate([lo[0], lo[1], hi[2], hi[3]], axis=0),
            jnp.concatenate([hi[0], hi[1], lo[2], lo[3]], axis=0))


def _unstack_heads(out_a, out_b, lo_mask):
    t = lambda x, r: x[r * BLOCK:(r + 1) * BLOCK, :]
    return [
        jnp.where(lo_mask, t(out_a, 0), t(out_b, 0)),
        jnp.where(lo_mask, t(out_a, 1), t(out_b, 1)),
        jnp.where(lo_mask, t(out_b, 2), t(out_a, 2)),
        jnp.where(lo_mask, t(out_b, 3), t(out_a, 3)),
    ]


def _sink_column(sink_ref, heads):
    row = lax.broadcasted_iota(jnp.int32, (4 * BLOCK, 1), 0)
    col = jnp.full((4 * BLOCK, 1), sink_ref[0, heads[3]], F32)
    for slot in (2, 1, 0):
        col = jnp.where(row < (slot + 1) * BLOCK, sink_ref[0, heads[slot]], col)
    return col


def _band_probs(q_stack, keys, tab, sink, first_block):
    s = _nt(q_stack, keys) * (HEAD_DIM ** -0.5) + tab
    if first_block is not None:
        col = lax.broadcasted_iota(jnp.int32, s.shape, 1)
        s = jnp.where(jnp.logical_and(first_block, col < BLOCK), -jnp.inf, s)
    m = jnp.maximum(jnp.max(s, axis=-1, keepdims=True), sink)
    e = jnp.exp(s - m)
    e_sink = jnp.exp(sink - m)
    den = jnp.sum(e, axis=-1, keepdims=True) + e_sink
    return e / den, e_sink / den


def _attn_specs(n_groups):
    group = lambda n: (jnp.minimum(n, n_groups - 1), 0)
    prev = lambda n: (jnp.maximum(jnp.minimum(n, n_groups - 1) * ATTN_STEP_BLOCKS - 1, 0), 0)
    return group, prev


def _band(prev_ref, group_ref, b):
    rows = lambda i: group_ref[i * BLOCK:(i + 1) * BLOCK, :]
    band = jnp.concatenate([prev_ref[...] if b == 0 else rows(b - 1), rows(b)], axis=0)
    return band, pltpu.roll(band, HEAD_DIM, 1)


def _attn_fwd(qn, kn, v, tab, sinks, rider=None):
    s = qn.shape[0]
    n_groups = s // (ATTN_STEP_BLOCKS * BLOCK)
    group, prev = _attn_specs(n_groups)
    rows = ATTN_STEP_BLOCKS * BLOCK

    def body(sink_ref, q_ref, kc_ref, kp_ref, vc_ref, vp_ref, tab_ref, o_ref):
        first = pl.program_id(0) == 0
        lo_mask = _lane_lo((BLOCK, BLOCK))
        for b in range(ATTN_STEP_BLOCKS):
            at = slice(b * BLOCK, (b + 1) * BLOCK)
            kk, kk_sw = _band(kp_ref, kc_ref, b)
            vv, vv_sw = _band(vp_ref, vc_ref, b)
            q_a, q_b = _stack_heads([q_ref[at, p * BLOCK:(p + 1) * BLOCK] for p in range(4)], lo_mask)
            no_prev = first if b == 0 else None
            p_a, _ = _band_probs(q_a, kk, tab_ref[0], _sink_column(sink_ref, HEADS_A), no_prev)
            p_b, _ = _band_probs(q_b, kk_sw, tab_ref[1], _sink_column(sink_ref, HEADS_B), no_prev)
            out = _unstack_heads(_nn(p_a.astype(BF16), vv), _nn(p_b.astype(BF16), vv_sw), lo_mask)
            for p in range(4):
                o_ref[at, p * BLOCK:(p + 1) * BLOCK] = out[p].astype(BF16)

    return _call(
        body,
        (sinks, qn, kn, kn, v, v, tab),
        name="attn_fwd",
        grid=(n_groups,),
        in_specs=[
            pl.BlockSpec(memory_space=pltpu.SMEM),
            pl.BlockSpec((rows, ATTN_WIDTH), group),
            pl.BlockSpec((rows, KV_WIDTH), group),
            pl.BlockSpec((BLOCK, KV_WIDTH), prev),
            pl.BlockSpec((rows, KV_WIDTH), group),
            pl.BlockSpec((BLOCK, KV_WIDTH), prev),
            _resident((2, 4 * BLOCK, 2 * BLOCK)),
        ],
        out_specs=[pl.BlockSpec((rows, ATTN_WIDTH), group)],
        out_shape=[jax.ShapeDtypeStruct((s, ATTN_WIDTH), BF16)],
        rider=rider,
    )


def _pooled(u_tile, u_halo, tile_index, tile_rows):
    halo = jnp.where(tile_index > 0, u_halo, 0.0)
    ext = jnp.concatenate([halo, u_tile], axis=0)
    sums = []
    acc = ext
    for shift in (1, 2, 4, 8):
        acc = acc + pltpu.roll(acc, shift, 0)
        sums.append(acc)
    t = tile_index * tile_rows + lax.broadcasted_iota(jnp.int32, (tile_rows, 1), 0)
    out = []
    for g, w in enumerate(POOL_SIZES):
        lanes = slice(g * POOL_GROUP, (g + 1) * POOL_GROUP)
        cnt = jnp.minimum(t + 1, w).astype(F32)
        out.append(sums[g][POOL_HALO:, lanes] / cnt - u_tile[:, lanes])
    return out


def _halo_before(tile):
    return lambda i: (jnp.maximum(i * (tile // POOL_HALO) - 1, 0), 0)


def _mix_out(u, a, x, w_out, w_pool, pool_scale, g_ffn, rider=None):
    s = x.shape[0]
    ts = min(TOKEN_TILE, s)

    def body(u_ref, uh_ref, a_ref, x_ref, wo_ref, wp_ref, sc_ref, g_ref, h1_ref, hn_ref, m_ref):
        i = pl.program_id(0)
        pooled = _pooled(u_ref[...], uh_ref[...], i, ts)
        for g in range(len(POOL_SIZES)):
            lanes = slice(g * POOL_GROUP, (g + 1) * POOL_GROUP)
            y = _nn(pooled[g].astype(BF16), wp_ref[g].astype(BF16))
            m_ref[:, lanes] = (y * sc_ref[:, lanes]).astype(BF16)
        h1 = x_ref[...] + _nn(a_ref[...], wo_ref[:ATTN_WIDTH, :]) + _nn(m_ref[...], wo_ref[ATTN_WIDTH:, :])
        h1_ref[...] = h1
        hn_ref[...] = ((h1 * _rms(h1)) * g_ref[...]).astype(BF16)

    return _call(
        body,
        (u, u, a, x, w_out, w_pool, pool_scale, g_ffn),
        name="mix_out",
        grid=(s // ts,),
        in_specs=[
            _rows(ts, POOL_WIDTH),
            pl.BlockSpec((POOL_HALO, POOL_WIDTH), _halo_before(ts)),
            _rows(ts, ATTN_WIDTH),
            _rows(ts, D_MODEL),
            _resident((D_MODEL, D_MODEL)),
            _resident((len(POOL_SIZES), POOL_GROUP, POOL_GROUP)),
            _resident((1, POOL_WIDTH)),
            _resident((1, D_MODEL)),
        ],
        out_specs=[_rows(ts, D_MODEL), _rows(ts, D_MODEL), _rows(ts, POOL_WIDTH)],
        out_shape=[
            jax.ShapeDtypeStruct((s, D_MODEL), F32),
            jax.ShapeDtypeStruct((s, D_MODEL), BF16),
            jax.ShapeDtypeStruct((s, POOL_WIDTH), BF16),
        ],
        rider=rider,
    )


def _ffn_up(hn2, wg_t, wu_t, rider=None):
    s = hn2.shape[0]
    ts = min(TOKEN_TILE, s)

    def body(hn_ref, wg_ref, wu_ref, gt_ref, up_ref):
        hn = hn_ref[...]
        for c in range(D_FF // FF_CHUNK):
            cols = slice(c * FF_CHUNK, (c + 1) * FF_CHUNK)
            gt_ref[:, cols] = _nt(hn, wg_ref[cols, :]).astype(BF16)
            up_ref[:, cols] = _nt(hn, wu_ref[cols, :]).astype(BF16)

    return _call(
        body,
        (hn2, wg_t, wu_t),
        name="ffn_up",
        grid=(s // ts,),
        in_specs=[_rows(ts, D_MODEL), _resident((D_FF, D_MODEL)), _resident((D_FF, D_MODEL))],
        out_specs=[_rows(ts, D_FF), _rows(ts, D_FF)],
        out_shape=[jax.ShapeDtypeStruct((s, D_FF), BF16), jax.ShapeDtypeStruct((s, D_FF), BF16)],
        rider=rider,
    )


def _silu_mul(gt, up):
    return (gt * jax.nn.sigmoid(gt)) * up


def _ffn_down_ple(gt, up, h1, w_down, p, target, g_ple, w_pg, w_pp, rider=None):
    s = h1.shape[0]
    ts = min(TOKEN_TILE, s)
    blk = D_MODEL // N_DEV

    def body(gt_ref, up_ref, h1_ref, wd_ref, p_ref, t_ref, g_ref, wpg_ref, wpp_ref,
             loss_ref, dh_ref, dwpg_ref, dwpp_ref, dg_ref, act_ref, pp_ref):
        @pl.when(pl.program_id(0) == 0)
        def _():
            loss_ref[...] = jnp.zeros_like(loss_ref)
            dwpg_ref[...] = jnp.zeros_like(dwpg_ref)
            dwpp_ref[...] = jnp.zeros_like(dwpp_ref)
            dg_ref[...] = jnp.zeros_like(dg_ref)

        h2v = h1_ref[...]
        for c in range(D_FF // FF_CHUNK):
            cols = slice(c * FF_CHUNK, (c + 1) * FF_CHUNK)
            act = _silu_mul(gt_ref[:, cols].astype(F32), up_ref[:, cols].astype(F32)).astype(BF16)
            h2v = _nn(act, wd_ref[cols, :]) + h2v
        r = _rms(h2v)
        hn = ((h2v * r) * g_ref[...]).astype(BF16)
        gate = jax.nn.sigmoid(_nn(hn, wpg_ref[...]))
        pb = p_ref[...].astype(BF16)
        for j in range(N_DEV):
            pp_ref[:, j * blk:(j + 1) * blk] = _nn(pb, wpp_ref[j])
        pp = pp_ref[...]
        diff = (h2v + gate * pp) - t_ref[...]
        loss_ref[...] += jnp.sum(jnp.sum(diff * diff, axis=0, keepdims=True), axis=1, keepdims=True) * (0.5 / D_MODEL)
        dy = diff * (1.0 / D_MODEL)
        d_pp = (dy * gate).astype(BF16)
        d_pre = ((dy * pp) * (gate * (1.0 - gate))).astype(BF16)
        for j in range(N_DEV):
            dwpp_ref[j] += _tn(pb, d_pp[:, j * blk:(j + 1) * blk])
        dwpg_ref[...] += _tn(hn, d_pre)
        d_x, d_g = _rms_bwd(_nt(d_pre, wpg_ref[...]), h2v, r, g_ref[...])
        dg_ref[...] += d_g
        dh_ref[...] = dy + d_x

    return _call(
        body,
        (gt, up, h1, w_down, p, target, g_ple, w_pg, w_pp),
        name="ffn_down_ple",
        grid=(s // ts,),
        in_specs=[
            _rows(ts, D_FF),
            _rows(ts, D_FF),
            _rows(ts, D_MODEL),
            _resident((D_FF, D_MODEL)),
            _rows(ts, PLE_DIM),
            _rows(ts, D_MODEL),
            _resident((1, D_MODEL)),
            _resident((D_MODEL, D_MODEL)),
            _resident((N_DEV, PLE_DIM, blk)),
        ],
        out_specs=[
            _acc((1, SMALL_LANES)),
            _rows(ts, D_MODEL),
            _acc((D_MODEL, D_MODEL)),
            _acc((N_DEV, PLE_DIM, blk)),
            _acc((1, D_MODEL)),
        ],
        out_shape=[
            jax.ShapeDtypeStruct((1, SMALL_LANES), F32),
            jax.ShapeDtypeStruct((s, D_MODEL), F32),
            jax.ShapeDtypeStruct((D_MODEL, D_MODEL), F32),
            jax.ShapeDtypeStruct((N_DEV, PLE_DIM, blk), F32),
            jax.ShapeDtypeStruct((1, D_MODEL), F32),
        ],
        scratch_shapes=[pltpu.VMEM((ts, D_FF), BF16), pltpu.VMEM((ts, D_MODEL), F32)],
        rider=rider,
    )


def _ffn_bwd_act(dh2, h1, gt, up, g_ffn, wg_t, wu_t, w_down, rider=None):
    s = h1.shape[0]
    ts = min(FFN_BWD_TILE, s)

    def body(dh_ref, h1_ref, gt_ref, up_ref, g_ref, wg_ref, wu_ref, wd_ref,
             dgt_ref, dup_ref, dh1_ref, dh1b_ref, dg_ref, dwd_ref, act_ref):
        @pl.when(pl.program_id(0) == 0)
        def _():
            dg_ref[...] = jnp.zeros_like(dg_ref)
            dwd_ref[...] = jnp.zeros_like(dwd_ref)

        dhb = dh_ref[...].astype(BF16)
        d_hn = jnp.zeros((ts, D_MODEL), F32)
        for c in range(D_FF // FF_CHUNK):
            cols = slice(c * FF_CHUNK, (c + 1) * FF_CHUNK)
            d_act = _nt(dhb, wd_ref[cols, :])
            gtv = gt_ref[:, cols].astype(F32)
            upv = up_ref[:, cols].astype(F32)
            sg = jax.nn.sigmoid(gtv)
            silu = gtv * sg
            act_ref[:, cols] = (silu * upv).astype(BF16)
            d_up = (d_act * silu).astype(BF16)
            d_gt = ((d_act * upv) * (sg * (1.0 + gtv * (1.0 - sg)))).astype(BF16)
            dup_ref[:, cols] = d_up
            dgt_ref[:, cols] = d_gt
            d_hn = (_nn(d_gt, wg_ref[cols, :]) + _nn(d_up, wu_ref[cols, :])) + d_hn
        dwd_ref[...] += _tn(act_ref[...], dhb)
        h1v = h1_ref[...]
        d_x, d_g = _rms_bwd(d_hn, h1v, _rms(h1v), g_ref[...])
        dg_ref[...] += d_g
        dh1 = dh_ref[...] + d_x
        dh1_ref[...] = dh1
        dh1b_ref[...] = dh1.astype(BF16)

    return _call(
        body,
        (dh2, h1, gt, up, g_ffn, wg_t, wu_t, w_down),
        name="ffn_bwd_act",
        grid=(s // ts,),
        in_specs=[
            _rows(ts, D_MODEL),
            _rows(ts, D_MODEL),
            _rows(ts, D_FF),
            _rows(ts, D_FF),
            _resident((1, D_MODEL)),
            _resident((D_FF, D_MODEL)),
            _resident((D_FF, D_MODEL)),
            _resident((D_FF, D_MODEL)),
        ],
        out_specs=[
            _rows(ts, D_FF), _rows(ts, D_FF),
            _rows(ts, D_MODEL), _rows(ts, D_MODEL), _acc((1, D_MODEL)), _acc((D_FF, D_MODEL)),
        ],
        out_shape=[
            jax.ShapeDtypeStruct((s, D_FF), BF16),
            jax.ShapeDtypeStruct((s, D_FF), BF16),
            jax.ShapeDtypeStruct((s, D_MODEL), F32),
            jax.ShapeDtypeStruct((s, D_MODEL), BF16),
            jax.ShapeDtypeStruct((1, D_MODEL), F32),
            jax.ShapeDtypeStruct((D_FF, D_MODEL), F32),
        ],
        scratch_shapes=[pltpu.VMEM((ts, D_FF), BF16)],
        rider=rider,
    )


def _ffn_bwd_w(dgt, dup, hn2, rider=None):
    s = hn2.shape[0]
    slab = pl.BlockSpec((s, FF_CHUNK), lambda i: (0, i))

    def body(dgt_ref, dup_ref, hn_ref, dwg_ref, dwu_ref):
        hn = hn_ref[...]
        dwg_ref[...] = _tn(dgt_ref[...], hn)
        dwu_ref[...] = _tn(dup_ref[...], hn)

    return _call(
        body,
        (dgt, dup, hn2),
        name="ffn_bwd_w",
        grid=(D_FF // FF_CHUNK,),
        in_specs=[slab, slab, _resident((s, D_MODEL))],
        out_specs=[_rows(FF_CHUNK, D_MODEL)] * 2,
        out_shape=[jax.ShapeDtypeStruct((D_FF, D_MODEL), F32)] * 2,
        rider=rider,
    )


def _mix_bwd(dh1b, u, a, m, w_out, w_pool, pool_scale, rider=None):
    s = u.shape[0]
    ts = min(TOKEN_TILE, s)
    nt = s // ts
    halo_after = lambda i: (jnp.minimum((i + 1) * (ts // POOL_HALO), s // POOL_HALO - 1), 0)
    n_groups = len(POOL_SIZES)

    def body(dh_ref, dhn_ref, u_ref, uh_ref, a_ref, m_ref, wo_ref, wp_ref, sc_ref,
             da_ref, du_ref, dwp_ref, dsc_ref, dwo_ref):
        i = pl.program_id(0)

        @pl.when(i == 0)
        def _():
            dwp_ref[...] = jnp.zeros_like(dwp_ref)
            dsc_ref[...] = jnp.zeros_like(dsc_ref)
            dwo_ref[...] = jnp.zeros_like(dwo_ref)

        dh = dh_ref[...]
        dwo_ref[:ATTN_WIDTH, :] += _tn(a_ref[...], dh)
        dwo_ref[ATTN_WIDTH:, :] += _tn(m_ref[...], dh)
        da_ref[...] = _nt(dh, wo_ref[:ATTN_WIDTH, :])
        dh_next = jnp.where(i < nt - 1, dhn_ref[...], jnp.zeros_like(dhn_ref))
        dm_ext = _nt(jnp.concatenate([dh, dh_next], axis=0), wo_ref[ATTN_WIDTH:, :])
        pooled = _pooled(u_ref[...], uh_ref[...], i, ts)
        t_ext = i * ts + lax.broadcasted_iota(jnp.int32, (ts + POOL_HALO, 1), 0)
        for g, w in enumerate(POOL_SIZES):
            lanes = slice(g * POOL_GROUP, (g + 1) * POOL_GROUP)
            wp = wp_ref[g].astype(BF16)
            pg = pooled[g].astype(BF16)
            dm_g = dm_ext[:, lanes]
            dsc_ref[:, lanes] += jnp.sum(dm_g[:ts, :] * _nn(pg, wp), axis=0, keepdims=True)
            dy = (dm_g * sc_ref[:, lanes]).astype(BF16)
            dwp_ref[g] += _tn(pg, dy[:ts, :])
            d_pool = _nt(dy, wp)
            acc = d_pool / jnp.minimum(t_ext + 1, w).astype(F32)
            shift = 1
            while shift < w:
                acc = acc + pltpu.roll(acc, ts + POOL_HALO - shift, 0)
                shift *= 2
            du_ref[:, lanes] = acc[:ts, :] - d_pool[:ts, :]

    return _call(
        body,
        (dh1b, dh1b, u, u, a, m, w_out, w_pool, pool_scale),
        name="mix_bwd",
        grid=(nt,),
        in_specs=[
            _rows(ts, D_MODEL),
            pl.BlockSpec((POOL_HALO, D_MODEL), halo_after),
            _rows(ts, POOL_WIDTH),
            pl.BlockSpec((POOL_HALO, POOL_WIDTH), _halo_before(ts)),
            _rows(ts, ATTN_WIDTH),
            _rows(ts, POOL_WIDTH),
            _resident((D_MODEL, D_MODEL)),
            _resident((n_groups, POOL_GROUP, POOL_GROUP)),
            _resident((1, POOL_WIDTH)),
        ],
        out_specs=[
            _rows(ts, ATTN_WIDTH),
            _rows(ts, POOL_WIDTH),
            _acc((n_groups, POOL_GROUP, POOL_GROUP)),
            _acc((1, POOL_WIDTH)),
            _acc((D_MODEL, D_MODEL)),
        ],
        out_shape=[
            jax.ShapeDtypeStruct((s, ATTN_WIDTH), F32),
            jax.ShapeDtypeStruct((s, POOL_WIDTH), F32),
            jax.ShapeDtypeStruct((n_groups, POOL_GROUP, POOL_GROUP), F32),
            jax.ShapeDtypeStruct((1, POOL_WIDTH), F32),
            jax.ShapeDtypeStruct((D_MODEL, D_MODEL), F32),
        ],
        rider=rider,
    )


def _attn_bwd(qn, kn, v, a, da, tab, sinks, rider=None):
    s = qn.shape[0]
    qb = ATTN_STEP_BLOCKS
    rows = qb * BLOCK
    n_groups = s // rows
    group, prev = _attn_specs(n_groups)
    done = lambda n: (jnp.maximum(n - 1, 0), 0)

    def body(sink_ref, q_ref, kc_ref, kp_ref, vc_ref, vp_ref, o_ref, do_ref, tab_ref,
             dq_ref, dk_ref, dv_ref, dl_ref, ds_ref, k_carry, v_carry, sink_acc):
        n = pl.program_id(0)

        @pl.when(n == 0)
        def _():
            dl_ref[...] = jnp.zeros_like(dl_ref)
            k_carry[...] = jnp.zeros_like(k_carry)
            v_carry[...] = jnp.zeros_like(v_carry)
            sink_acc[...] = jnp.zeros_like(sink_acc)

        @pl.when(n < n_groups)
        def _():
            first = n == 0
            lo_mask = _lane_lo((BLOCK, BLOCK))
            dks, dvs = [], []
            for b in range(qb):
                at = slice(b * BLOCK, (b + 1) * BLOCK)
                keys = _band(kp_ref, kc_ref, b)
                vals = _band(vp_ref, vc_ref, b)
                q_st = _stack_heads([q_ref[at, p * BLOCK:(p + 1) * BLOCK] for p in range(4)], lo_mask)
                do_st = _stack_heads([do_ref[at, p * BLOCK:(p + 1) * BLOCK] for p in range(4)], lo_mask)
                o_st = _stack_heads([o_ref[at, p * BLOCK:(p + 1) * BLOCK].astype(F32) for p in range(4)], lo_mask)
                dq_st, dk_parts, dv_parts = [], [], []
                for half, heads in enumerate((HEADS_A, HEADS_B)):
                    probs, p_sink = _band_probs(q_st[half], keys[half], tab_ref[half], _sink_column(sink_ref, heads),
                                                first if b == 0 else None)
                    delta = jnp.sum(do_st[half] * o_st[half], axis=-1, keepdims=True)
                    dob = do_st[half].astype(BF16)
                    dl = probs * (_nt(dob, vals[half]) - delta)
                    dl_ref[half] += dl
                    sink_acc[half] += p_sink * delta
                    dsb = (dl * (HEAD_DIM ** -0.5)).astype(BF16)
                    dq_st.append(_nn(dsb, keys[half]))
                    dk_parts.append(_tn(dsb, q_st[half]))
                    dv_parts.append(_tn(probs.astype(BF16), dob))
                dq = _unstack_heads(dq_st[0], dq_st[1], lo_mask)
                for p in range(4):
                    dq_ref[at, p * BLOCK:(p + 1) * BLOCK] = dq[p]
                dks.append(dk_parts[0] + pltpu.roll(dk_parts[1], HEAD_DIM, 1))
                dvs.append(dv_parts[0] + pltpu.roll(dv_parts[1], HEAD_DIM, 1))
            last = slice((qb - 1) * BLOCK, qb * BLOCK)
            for parts, out_ref, carry in ((dks, dk_ref, k_carry), (dvs, dv_ref, v_carry)):
                out_ref[...] = carry[...]
                out_ref[last, :] += parts[0][:BLOCK, :]
                for b in range(qb):
                    own = parts[b][BLOCK:, :]
                    carry[b * BLOCK:(b + 1) * BLOCK, :] = own + parts[b + 1][:BLOCK, :] if b + 1 < qb else own

        @pl.when(n == n_groups)
        def _():
            dk_ref[...] = k_carry[...]
            dv_ref[...] = v_carry[...]
            for half, heads in enumerate((HEADS_A, HEADS_B)):
                for slot, h in enumerate(heads):
                    tot = jnp.sum(sink_acc[half, slot * BLOCK:(slot + 1) * BLOCK, :], axis=0, keepdims=True)
                    ds_ref[h:h + 1, :] = jnp.broadcast_to(-tot, (1, SMALL_LANES))

    return _call(
        body,
        (sinks, qn, kn, kn, v, v, a, da, tab),
        name="attn_bwd",
        grid=(n_groups + 1,),
        in_specs=[
            pl.BlockSpec(memory_space=pltpu.SMEM),
            pl.BlockSpec((rows, ATTN_WIDTH), group),
            pl.BlockSpec((rows, KV_WIDTH), group),
            pl.BlockSpec((BLOCK, KV_WIDTH), prev),
            pl.BlockSpec((rows, KV_WIDTH), group),
            pl.BlockSpec((BLOCK, KV_WIDTH), prev),
            pl.BlockSpec((rows, ATTN_WIDTH), group),
            pl.BlockSpec((rows, ATTN_WIDTH), group),
            _resident((2, 4 * BLOCK, 2 * BLOCK)),
        ],
        out_specs=[
            pl.BlockSpec((rows, ATTN_WIDTH), group),
            pl.BlockSpec((rows, KV_WIDTH), done),
            pl.BlockSpec((rows, KV_WIDTH), done),
            _acc((2, 4 * BLOCK, 2 * BLOCK)),
            _acc((N_DEV, SMALL_LANES)),
        ],
        out_shape=[
            jax.ShapeDtypeStruct((s, ATTN_WIDTH), F32),
            jax.ShapeDtypeStruct((s, KV_WIDTH), F32),
            jax.ShapeDtypeStruct((s, KV_WIDTH), F32),
            jax.ShapeDtypeStruct((2, 4 * BLOCK, 2 * BLOCK), F32),
            jax.ShapeDtypeStruct((N_DEV, SMALL_LANES), F32),
        ],
        scratch_shapes=[
            pltpu.VMEM((rows, KV_WIDTH), F32),
            pltpu.VMEM((rows, KV_WIDTH), F32),
            pltpu.VMEM((2, 4 * BLOCK, 1), F32),
        ],
        rider=rider,
    )


def _fold_heads(acc):
    t = acc + pltpu.roll(acc, HEAD_DIM, 1)
    out = t[:, :SMALL_LANES]
    for g in range(1, acc.shape[1] // SMALL_LANES):
        out = out + t[:, g * SMALL_LANES:(g + 1) * SMALL_LANES]
    return out


def _in_proj_bwd(dqn, dkn, dv, du, zqk, x, dh1, g_attn, gq_t, gk_t, w_in_t, rider=None):
    s = x.shape[0]
    ts = min(TOKEN_TILE, s)
    nt = s // ts

    def head_norm_bwd(d_n, raw, g_t, bmat):
        r = lax.rsqrt(_seg_mean(raw * raw, bmat) + EPS)
        gy = d_n * g_t
        d_raw = r * gy - raw * (r * r * r) * _seg_mean(gy * raw, bmat)
        return d_raw, jnp.sum(d_n * (raw * r), axis=0, keepdims=True)

    def body(dqn_ref, dkn_ref, dv_ref, du_ref, zqk_ref, x_ref, dh1_ref, g_ref, gq_ref, gk_ref, w_ref, bq_ref, bk_ref,
             gx_ref, dw_ref, dg_ref, dgq_ref, dgk_ref, dz_ref, gq_acc, gk_acc):
        i = pl.program_id(0)

        @pl.when(i == 0)
        def _():
            dw_ref[...] = jnp.zeros_like(dw_ref)
            dg_ref[...] = jnp.zeros_like(dg_ref)
            gq_acc[...] = jnp.zeros_like(gq_acc)
            gk_acc[...] = jnp.zeros_like(gk_acc)

        d_q, d_gq = head_norm_bwd(dqn_ref[...], zqk_ref[:, :ATTN_WIDTH], gq_ref[...], bq_ref[...])
        d_k, d_gk = head_norm_bwd(dkn_ref[...], zqk_ref[:, ATTN_WIDTH:], gk_ref[...], bk_ref[...])
        gq_acc[...] += d_gq
        gk_acc[...] += d_gk
        dz_ref[:, :ATTN_WIDTH] = d_q.astype(BF16)
        dz_ref[:, ATTN_WIDTH:ATTN_WIDTH + KV_WIDTH] = d_k.astype(BF16)
        dz_ref[:, ATTN_WIDTH + KV_WIDTH:ATTN_WIDTH + 2 * KV_WIDTH] = dv_ref[...].astype(BF16)
        dz_ref[:, ATTN_WIDTH + 2 * KV_WIDTH:] = du_ref[...].astype(BF16)
        dz = dz_ref[...]
        xf = x_ref[...]
        r = _rms(xf)
        hn = ((xf * r) * g_ref[...]).astype(BF16)
        dw_ref[...] += _tn(dz, hn)
        d_x, d_g = _rms_bwd(_nn(dz, w_ref[...]), xf, r, g_ref[...])
        dg_ref[...] += d_g
        gx_ref[...] = dh1_ref[...] + d_x

        @pl.when(i == nt - 1)
        def _():
            dgq_ref[...] = _fold_heads(gq_acc[...])
            dgk_ref[...] = _fold_heads(gk_acc[...])

    return _call(
        body,
        (dqn, dkn, dv, du, zqk, x, dh1, g_attn, gq_t, gk_t, w_in_t,
      _head_mean_matrix(ATTN_WIDTH), _head_mean_matrix(KV_WIDTH)),
        name="in_proj_bwd",
        grid=(nt,),
        in_specs=[
            _rows(ts, ATTN_WIDTH),
            _rows(ts, KV_WIDTH),
            _rows(ts, KV_WIDTH),
            _rows(ts, POOL_WIDTH),
            _rows(ts, ATTN_WIDTH + KV_WIDTH),
            _rows(ts, D_MODEL),
            _rows(ts, D_MODEL),
            _resident((1, D_MODEL)),
            _resident((1, ATTN_WIDTH)),
            _resident((1, KV_WIDTH)),
            _resident((IN_WIDTH, D_MODEL)),
            _resident((ATTN_WIDTH, ATTN_WIDTH)),
            _resident((KV_WIDTH, KV_WIDTH)),
        ],
        out_specs=[
            _rows(ts, D_MODEL),
            _acc((IN_WIDTH, D_MODEL)),
            _acc((1, D_MODEL)),
            _acc((1, SMALL_LANES)),
            _acc((1, SMALL_LANES)),
        ],
        out_shape=[
            jax.ShapeDtypeStruct((s, D_MODEL), F32),
            jax.ShapeDtypeStruct((IN_WIDTH, D_MODEL), F32),
            jax.ShapeDtypeStruct((1, D_MODEL), F32),
            jax.ShapeDtypeStruct((1, SMALL_LANES), F32),
            jax.ShapeDtypeStruct((1, SMALL_LANES), F32),
        ],
        scratch_shapes=[
            pltpu.VMEM((ts, IN_WIDTH), BF16),
            pltpu.VMEM((1, ATTN_WIDTH), F32),
            pltpu.VMEM((1, KV_WIDTH), F32),
        ],
        rider=rider,
    )


BIG_WEIGHTS = (
    ("w_in", True, IN_WIDTH // N_DEV, D_MODEL),
    ("w_out", False, D_MODEL // N_DEV, D_MODEL),
    ("w_gate", True, D_FF // N_DEV, D_MODEL),
    ("w_up", True, D_FF // N_DEV, D_MODEL),
    ("w_down", False, D_FF // N_DEV, D_MODEL),
    ("w_ple_gate", False, D_MODEL // N_DEV, D_MODEL),
    ("w_ple_proj", False, PLE_DIM, D_MODEL // N_DEV),
)
N_BIG = len(BIG_WEIGHTS)


def _place():
    x, y, c = lax.axis_index("x"), lax.axis_index("y"), lax.axis_index("c")
    chips = [(1 - x, y), (x, 1 - y), (1 - x, 1 - y)]
    return x, y, c, chips


class _Gather:
    def __init__(self, n, rows=None):
        self.n = n
        self.rows = rows or [None] * n
        self.sems = [pltpu.SemaphoreType.DMA((n, 7)), pltpu.SemaphoreType.DMA((n, 7)), pltpu.SemaphoreType.DMA((n,))]

    def _ctx(self, srcs, outs, sems):
        send_sems, recv_sems, local_sems = sems
        x, y, c, chips = _place()
        me, sibling = (x, y, c), (x, y, 1 - c)

        def part(k, ref):
            return ref if self.rows[k] is None else ref.at[pl.ds(*self.rows[k]), :]

        def block(k, owner):
            px, py, pc = owner
            return part(k, outs[k].at[4 * px + 2 * py + pc])

        def copy(k, idx, owner, to, mine=False):
            return pltpu.make_async_remote_copy(
                src_ref=part(k, srcs[k]) if mine else block(k, owner), dst_ref=block(k, owner),
                send_sem=send_sems.at[k, idx], recv_sem=recv_sems.at[k, idx], device_id=to, device_id_type=MESH)

        def local(k):
            return pltpu.make_async_copy(part(k, srcs[k]), block(k, me), local_sems.at[k])

        return c, chips, me, sibling, copy, local

    def begin(self, srcs, outs, sems):
        c, chips, me, sibling, copy, local = self._ctx(srcs, outs, sems)
        for k in range(self.n):
            local(k).start()
            copy(k, 0, me, sibling, mine=True).start()
            for j, chip in enumerate(chips):
                copy(k, 1 + j, me, (*chip, c), mine=True).start()

    def middle(self, srcs, outs, sems):
        c, chips, me, sibling, copy, local = self._ctx(srcs, outs, sems)
        for j, chip in enumerate(chips):
            for k in range(self.n):
                copy(k, 1 + j, (*chip, c), me).wait_recv()
                copy(k, 4 + j, (*chip, c), sibling).start()

    def end(self, srcs, outs, sems):
        c, chips, me, sibling, copy, local = self._ctx(srcs, outs, sems)
        for k in range(self.n):
            copy(k, 0, sibling, me).wait_recv()
            for j, chip in enumerate(chips):
                copy(k, 4 + j, (*chip, 1 - c), me).wait_recv()
        for k in range(self.n):
            copy(k, 0, me, sibling, mine=True).wait_send()
            for j, chip in enumerate(chips):
                copy(k, 1 + j, me, (*chip, c), mine=True).wait_send()
                copy(k, 4 + j, (*chip, c), sibling).wait_send()
            local(k).wait()


def _gather_rider(items):
    items = [it if isinstance(it, tuple) else (it, None, None, None) for it in items]
    n = len(items)
    g = _Gather(n, [None if r0 is None else (r0, nr) for _, r0, nr, _ in items])
    shapes = [jax.ShapeDtypeStruct((N_DEV, *sh.shape), sh.dtype) for sh, _, _, _ in items]
    stacks = [(k, st) for k, (_, _, _, st) in enumerate(items) if st is not None]
    aliases = {n + i: k for i, (k, _) in enumerate(stacks)}
    return _Rider([sh for sh, _, _, _ in items] + [st for _, st in stacks], shapes, g.sems, g.begin, g.end, g.middle,
                  aliases=aliases)


def _cast_and_gather_first(shards, rel_bias_t):
    g = _Gather(1)
    any_spec = pl.BlockSpec(memory_space=pl.ANY)
    vmem = pl.BlockSpec(memory_space=pltpu.VMEM)

    def body(*refs):
        ins, rb_ref, outs = refs[:N_BIG], refs[N_BIG], refs[N_BIG + 1:2 * N_BIG + 1]
        gathered, tab_ref, sems = refs[2 * N_BIG + 1], refs[2 * N_BIG + 2], refs[2 * N_BIG + 3:]
        outs[0][...] = ins[0][...].astype(BF16)
        g.begin(outs[:1], [gathered], sems)
        for k in range(1, N_BIG):
            outs[k][...] = ins[k][...].astype(BF16)
        _write_bias_table(rb_ref, tab_ref)
        g.middle(outs[:1], [gathered], sems)
        g.end(outs[:1], [gathered], sems)

    res = pl.pallas_call(
        body,
        name="cast_and_gather_first",
        in_specs=[vmem] * N_BIG + [pl.BlockSpec(memory_space=pltpu.SMEM)],
        out_specs=[vmem] * N_BIG + [any_spec, vmem],
        out_shape=[jax.ShapeDtypeStruct((r, c), BF16) for _, _, r, c in BIG_WEIGHTS]
        + [jax.ShapeDtypeStruct((N_DEV, *BIG_WEIGHTS[0][2:]), BF16), jax.ShapeDtypeStruct(BIAS_TABLE_SHAPE, F32)],
        scratch_shapes=g.sems,
    )(*shards, rel_bias_t)
    return list(res[:N_BIG]), res[N_BIG], res[N_BIG + 1]


def _sibling_rider(grads):
    n = len(grads)

    def copies(gs, lands, sems):
        send_sems, recv_sems = sems
        x, y, c, _ = _place()
        return [
            pltpu.make_async_remote_copy(
                src_ref=gs[k].at[:, 1 - c], dst_ref=lands[k], send_sem=send_sems.at[k], recv_sem=recv_sems.at[k],
                device_id=(x, y, 1 - c), device_id_type=MESH)
            for k in range(n)
        ]

    def begin(gs, lands, sems):
        for cp in copies(gs, lands, sems):
            cp.start()

    def end(gs, lands, sems):
        for cp in copies(gs, lands, sems):
            cp.wait()

    shapes = [jax.ShapeDtypeStruct((N_CHIPS, *g.shape[2:]), F32) for g in grads]
    return _Rider(grads, shapes, [pltpu.SemaphoreType.DMA((n,)), pltpu.SemaphoreType.DMA((n,))], begin, end)


def _chip_of_relation(j, place):
    x, y = place[0], place[1]
    return jnp.where(j == 0, 2 * (1 - x) + y, jnp.where(j == 1, 2 * x + 1 - y, 2 * (1 - x) + 1 - y))


def _chip_sum(ks, place, grads, from_sibling):
    n = len(ks)
    shapes = [BIG_WEIGHTS[k][2:] for k in ks]
    operands, specs = [], []
    for (r, c), g, l in zip(shapes, grads, from_sibling):
        operands += [g, l]
        specs += [pl.BlockSpec((1, 1, r, c), lambda j, place: (_chip_of_relation(j, place), place[2], 0, 0)),
                  pl.BlockSpec((1, r, c), lambda j, place: (_chip_of_relation(j, place), 0, 0))]
    args, in_specs, _ = _after_last(operands, specs)

    def body(place_ref, *refs):
        ins, outs = refs[:2 * n], refs[len(args):]
        for i in range(n):
            outs[i][0] = (ins[2 * i][0, 0] + ins[2 * i + 1][0]).astype(BF16)

    outs = pl.pallas_call(
        body,
        name="chip_sum_" + "_".join(BIG_WEIGHTS[k][0] for k in ks),
        grid_spec=pltpu.PrefetchScalarGridSpec(
            num_scalar_prefetch=1,
            grid=(N_CHIPS - 1,),
            in_specs=in_specs,
            out_specs=[pl.BlockSpec((1, r, c), lambda j, place: (j, 0, 0)) for r, c in shapes],
        ),
        out_shape=[jax.ShapeDtypeStruct((N_CHIPS - 1, r, c), BF16) for r, c in shapes],
    )(place, *args)
    _mark_issued(outs[0])
    return list(outs)


def _chips_rider(to_send, small=None):
    n = len(to_send)
    inputs = list(to_send) + ([] if small is None else [small])
    shapes = [jax.ShapeDtypeStruct((3, *t.shape[1:]), BF16) for t in to_send]
    sems = [pltpu.SemaphoreType.DMA((max(n, 1), 3)), pltpu.SemaphoreType.DMA((max(n, 1), 3))]
    if small is not None:
        shapes.append(jax.ShapeDtypeStruct((N_DEV, *small.shape), F32))
        sems += [pltpu.SemaphoreType.DMA((7,)), pltpu.SemaphoreType.DMA((7,)), pltpu.SemaphoreType.DMA]

    def copies(ins, outs, sem_refs):
        x, y, c, chips = _place()
        out = []
        for k in range(n):
            for j, (px, py) in enumerate(chips):
                out.append(pltpu.make_async_remote_copy(
                    src_ref=ins[k].at[j], dst_ref=outs[k].at[j],
                    send_sem=sem_refs[0].at[k, j], recv_sem=sem_refs[1].at[k, j],
                    device_id=(px, py, c), device_id_type=MESH))
        local = None
        if small is not None:
            me = 4 * x + 2 * y + c
            local = pltpu.make_async_copy(ins[n], outs[n].at[me], sem_refs[4])
            rel = 0
            for fx in (0, 1):
                for fy in (0, 1):
                    for fc in (0, 1):
                        if (fx, fy, fc) != (0, 0, 0):
                            out.append(pltpu.make_async_remote_copy(
                                src_ref=ins[n], dst_ref=outs[n].at[me],
                                send_sem=sem_refs[2].at[rel], recv_sem=sem_refs[3].at[rel],
                                device_id=(x ^ fx, y ^ fy, c ^ fc), device_id_type=MESH))
                            rel += 1
        return out, local

    def begin(ins, outs, sem_refs):
        remote, local = copies(ins, outs, sem_refs)
        if local is not None:
            local.start()
        for cp in remote:
            cp.start()

    def end(ins, outs, sem_refs):
        remote, local = copies(ins, outs, sem_refs)
        for cp in remote:
            cp.wait()
        if local is not None:
            local.wait()

    return _Rider(inputs, shapes, sems, begin, end)


def _exchange(name, rider):
    return _call(lambda: None, (), name=name, grid=(1,), in_specs=[], out_specs=[], out_shape=[], rider=rider)[1]


PEER_SETS = {"sibling": 1, "chips": 2, "sibling+chips": 3, "all": 4}


def _peers(pattern):
    x, y, c, chips = _place()
    sibling, others = [(x, y, 1 - c)], [(*chip, c) for chip in chips]
    if pattern == "all":
        return sibling + others + [(*chip, 1 - c) for chip in chips]
    return {"sibling": sibling, "chips": others, "sibling+chips": sibling + others}[pattern]


def _on_sequencer(name, pattern, rider):
    assert not rider.aliases
    n_in, n_out = len(rider.inputs), len(rider.out_shapes)

    def body(*refs):
        ins, outs, sems = refs[:n_in], refs[n_in:n_in + n_out], refs[n_in + n_out:]
        peers = _peers(pattern)
        barrier = pltpu.get_barrier_semaphore()
        for peer in peers:
            pl.semaphore_signal(barrier, inc=1, device_id=peer, device_id_type=MESH)
        pl.semaphore_wait(barrier, len(peers))
        rider.begin(ins, outs, sems)
        if rider.middle is not None:
            rider.middle(ins, outs, sems)
        rider.end(ins, outs, sems)

    outs = pl.kernel(
        body,
        name=name,
        out_type=tuple(rider.out_shapes),
        mesh=plsc.ScalarSubcoreMesh(axis_name="sequencer", num_cores=1),
        scratch_types=tuple(rider.sems),
        compiler_params=pltpu.CompilerParams(collective_id=PEER_SETS[pattern]),
    )(*rider.inputs)
    return list(outs)


def _merge_riders(*riders):
    riders = [r for r in riders if r is not None]
    if len(riders) == 1:
        return riders[0]
    assert not any(r.aliases for r in riders)

    def split(refs, counts):
        out, at = [], 0
        for n in counts:
            out.append(refs[at:at + n])
            at += n
        return out

    def run(which):
        def fn(ins, outs, sems):
            parts = zip(riders, split(ins, [len(r.inputs) for r in riders]),
                        split(outs, [len(r.out_shapes) for r in riders]), split(sems, [len(r.sems) for r in riders]))
            for r, i, o, s in parts:
                hook = getattr(r, which)
                if hook is not None:
                    hook(i, o, s)
        return fn

    middle = run("middle") if any(r.middle is not None for r in riders) else None
    return _Rider(sum((r.inputs for r in riders), []), sum((r.out_shapes for r in riders), []),
                  sum((r.sems for r in riders), []), run("begin"), run("end"), middle)


def _split_outputs(outs, *riders):
    res, at = [], 0
    for r in riders:
        res.append(outs[at:at + len(r.out_shapes)])
        at += len(r.out_shapes)
    return res


def _adamw(w, g, m, v):
    m = ADAM_B1 * m + (1.0 - ADAM_B1) * g
    v = ADAM_B2 * v + (1.0 - ADAM_B2) * jnp.square(g)
    m_hat = m / (1.0 - ADAM_B1 ** ADAM_STEP)
    v_hat = v / (1.0 - ADAM_B2 ** ADAM_STEP)
    delta = -ADAM_LR * (m_hat / (jnp.sqrt(v_hat) + ADAM_EPS) + ADAM_WD * w)
    return delta, m, v


def _adamw_big(ks, place, operands):
    n = len(ks)
    tiles = lambda i, place: (i, 0)
    in_specs, out_specs, out_shape = [], [], []
    for k in ks:
        _, _, r, c = BIG_WEIGHTS[k]
        tile = r // 2
        in_specs += [
            pl.BlockSpec((1, 1, tile, c), lambda i, place: (2 * place[0] + place[1], place[2], i, 0)),
            pl.BlockSpec((1, tile, c), lambda i, place: (2 * place[0] + place[1], i, 0)),
            pl.BlockSpec((3, tile, c), lambda i, place: (0, i, 0)),
        ] + [pl.BlockSpec((tile, c), tiles)] * 3
        out_specs += [pl.BlockSpec((tile, c), tiles)] * 4
        out_shape += [jax.ShapeDtypeStruct((r, c), F32)] * 4
    args, in_specs, _ = _after_last(sum((list(ops) for ops in operands), []), in_specs)

    def body(place_ref, *refs):
        ins, outs = refs[:6 * n], refs[len(args):]
        for i in range(n):
            mine_ref, sib_ref, land_ref, w_ref, m_ref, v_ref = ins[6 * i:6 * i + 6]
            g_ref, d_ref, nm_ref, nv_ref = outs[4 * i:4 * i + 4]
            g = mine_ref[0, 0] + sib_ref[0]
            g = ((g + land_ref[0].astype(F32)) + land_ref[1].astype(F32)) + land_ref[2].astype(F32)
            g_ref[...] = g
            d_ref[...], nm_ref[...], nv_ref[...] = _adamw(w_ref[...], g, m_ref[...], v_ref[...])

    outs = pl.pallas_call(
        body,
        name="adamw_" + "_".join(BIG_WEIGHTS[k][0] for k in ks),
        grid_spec=pltpu.PrefetchScalarGridSpec(
            num_scalar_prefetch=1, grid=(2,), in_specs=in_specs, out_specs=out_specs),
        out_shape=out_shape,
    )(place, *args)
    _mark_issued(outs[0])
    return [outs[4 * i:4 * i + 4] for i in range(n)]


def _pack_small(arrays):
    rows, offsets = [], []
    at = 0
    for a in arrays:
        if a.ndim != 2 or a.shape[1] != SMALL_LANES or a.shape[0] % 8:
            flat = a.reshape(-1)
            n_rows = -(-flat.shape[0] // (8 * SMALL_LANES)) * 8
            a = jnp.pad(flat, (0, n_rows * SMALL_LANES - flat.shape[0])).reshape(n_rows, SMALL_LANES)
        rows.append(a)
        offsets.append(at)
        at += a.shape[0]
    return jnp.concatenate(rows, axis=0), offsets


def _unpack_small(tot, at, shape):
    r, c = shape
    if r % 8 == 0:
        return tot[at:at + r, :c]
    assert r == 1
    if c <= SMALL_LANES:
        return tot[at:at + 1, :c]
    return jnp.concatenate([tot[at + j:at + j + 1, :] for j in range(c // SMALL_LANES)], axis=1)


def _small_update(packs, loss_at, grads_at, ws, ms, vs):
    n, n_packs = len(ws), len(packs)

    def body(*refs):
        pack_refs, refs = refs[:n_packs], refs[n_packs:]
        w_refs, m_refs, v_refs, loss_ref, outs = refs[:n], refs[n:2 * n], refs[2 * n:3 * n], refs[3 * n], refs[3 * n + 1:]
        tots = []
        for p_ref in pack_refs:
            tot = p_ref[0]
            for j in range(1, N_DEV):
                tot = tot + p_ref[j]
            tots.append(tot)
        loss_ref[...] = _unpack_small(tots[loss_at[0]], loss_at[1], (1, 1))
        for i, (pack, at) in enumerate(grads_at):
            g = _unpack_small(tots[pack], at, w_refs[i].shape)
            outs[i][...] = g
            outs[n + i][...], outs[2 * n + i][...], outs[3 * n + i][...] = _adamw(
                w_refs[i][...], g, m_refs[i][...], v_refs[i][...])

    shapes = [jax.ShapeDtypeStruct(w.shape, F32) for w in ws]
    outs = pl.pallas_call(body, name="small_update", out_shape=[jax.ShapeDtypeStruct((1, 1), F32)] + shapes * 4)(
        *packs, *ws, *ms, *vs)
    return outs[0], outs[1:]


SMALL_NAMES = ("g_attn_norm", "g_q", "g_k", "attn_sinks", "rel_bias", "w_pool", "pool_scale", "g_ffn_norm", "g_ple_norm")


def kernel(x, p, w_in, w_out, g_attn_norm, g_q, g_k, attn_sinks, rel_bias, w_pool, pool_scale, g_ffn_norm, w_gate, w_up, w_down, g_ple_norm, w_ple_gate, w_ple_proj, loss_target, m_w_in, m_w_out, m_g_attn_norm, m_g_q, m_g_k, m_attn_sinks, m_rel_bias, m_w_pool, m_pool_scale, m_g_ffn_norm, m_w_gate, m_w_up, m_w_down, m_g_ple_norm, m_w_ple_gate, m_w_ple_proj, v_w_in, v_w_out, v_g_attn_norm, v_g_q, v_g_k, v_attn_sinks, v_rel_bias, v_w_pool, v_pool_scale, v_g_ffn_norm, v_w_gate, v_w_up, v_w_down, v_g_ple_norm, v_w_ple_gate, v_w_ple_proj):
    weights = dict(w_in=w_in, w_out=w_out, g_attn_norm=g_attn_norm, g_q=g_q, g_k=g_k, attn_sinks=attn_sinks,
                   rel_bias=rel_bias, w_pool=w_pool, pool_scale=pool_scale, g_ffn_norm=g_ffn_norm, w_gate=w_gate,
                   w_up=w_up, w_down=w_down, g_ple_norm=g_ple_norm, w_ple_gate=w_ple_gate, w_ple_proj=w_ple_proj)
    m_in = dict(w_in=m_w_in, w_out=m_w_out, g_attn_norm=m_g_attn_norm, g_q=m_g_q, g_k=m_g_k, attn_sinks=m_attn_sinks,
                rel_bias=m_rel_bias, w_pool=m_w_pool, pool_scale=m_pool_scale, g_ffn_norm=m_g_ffn_norm, w_gate=m_w_gate,
                w_up=m_w_up, w_down=m_w_down, g_ple_norm=m_g_ple_norm, w_ple_gate=m_w_ple_gate, w_ple_proj=m_w_ple_proj)
    v_in = dict(w_in=v_w_in, w_out=v_w_out, g_attn_norm=v_g_attn_norm, g_q=v_g_q, g_k=v_g_k, attn_sinks=v_attn_sinks,
                rel_bias=v_rel_bias, w_pool=v_w_pool, pool_scale=v_pool_scale, g_ffn_norm=v_g_ffn_norm, w_gate=v_w_gate,
                w_up=v_w_up, w_down=v_w_down, g_ple_norm=v_g_ple_norm, w_ple_gate=v_w_ple_gate, w_ple_proj=v_w_ple_proj)

    _issued.clear()
    xs = x[0]
    ps = p[0, 0]
    target = loss_target[0]
    wp = w_pool[0]
    gq_t = jnp.tile(g_q, (1, ATTN_WIDTH // HEAD_DIM))
    gk_t = jnp.tile(g_k, (1, KV_WIDTH // HEAD_DIM))

    def to_blocks(k, arr):
        return jnp.swapaxes(arr[0], 0, 1) if BIG_WEIGHTS[k][1] else arr[0]

    def from_blocks(k, arr):
        return (jnp.swapaxes(arr, 0, 1) if BIG_WEIGHTS[k][1] else arr)[None]

    IN, OUT, GATE, UP, DOWN, PG, PP = range(N_BIG)
    full = lambda g: g.reshape(N_DEV * g.shape[1], g.shape[2])
    halves = lambda k, g: g.reshape(N_CHIPS, 2, *BIG_WEIGHTS[k][2:])
    place = jnp.stack([lax.axis_index("x"), lax.axis_index("y"), lax.axis_index("c")]).astype(jnp.int32)

    sh, w_in_g, tab = _cast_and_gather_first(
        [to_blocks(k, weights[name]) for k, (name, _, _, _) in enumerate(BIG_WEIGHTS)], rel_bias.T)
    w_in_t = full(w_in_g)

    (w_out_g,) = _on_sequencer("gather_out", "sibling+chips", _gather_rider([sh[OUT]]))
    wg_g, wu_g = _on_sequencer("gather_gate_up", "sibling+chips", _gather_rider([sh[GATE], sh[UP]]))
    wd_g, w_pg_g, w_pp_g = _on_sequencer("gather_down_ple", "sibling+chips", _gather_rider([sh[DOWN], sh[PG], sh[PP]]))
    (zqk, qn, kn, v, u), _ = _in_proj(xs, g_attn_norm, w_in_t, gq_t, gk_t)
    (a,), _ = _attn_fwd(qn, kn, v, tab, attn_sinks)
    w_out_f = full(w_out_g)
    (h1, hn2, m_out), _ = _mix_out(u, a, xs, w_out_f, wp, pool_scale, g_ffn_norm)
    wg_t, wu_t = full(wg_g), full(wu_g)
    (gt, up), _ = _ffn_up(hn2, wg_t, wu_t)
    w_down_f = full(wd_g)

    partial, from_sibling, sums, landed = [None] * N_BIG, [None] * N_BIG, [None] * N_BIG, [None] * N_BIG

    def to_sibling(name, ks, grads):
        for k, g in zip(ks, grads):
            partial[k] = halves(k, g)
        got = _on_sequencer(name, "sibling", _sibling_rider([partial[k] for k in ks]))
        for k, g in zip(ks, got):
            from_sibling[k] = g

    def chip_sum(*ks):
        for k, s in zip(ks, _chip_sum(ks, place, [partial[k] for k in ks], [from_sibling[k] for k in ks])):
            sums[k] = s

    def to_chips(name, ks, small=None):
        got = _on_sequencer(name, "chips" if small is None else "all", _chips_rider([sums[k] for k in ks], small))
        for k, g in zip(ks, got):
            landed[k] = g
        return got[len(ks):]

    (loss_part, dh2, d_wpg, d_wpp, d_g_ple), _ = _ffn_down_ple(
        gt, up, h1, w_down_f, ps, target, g_ple_norm, full(w_pg_g), w_pp_g)
    to_sibling("sibling_ple", (PG, PP), (d_wpg, d_wpp))
    (dgt, dup, dh1, dh1b, d_g_ffn, d_wd), _ = _ffn_bwd_act(dh2, h1, gt, up, g_ffn_norm, wg_t, wu_t, w_down_f)
    to_sibling("sibling_down", (DOWN,), (d_wd,))
    chip_sum(PG, PP)
    to_chips("chips_ple", (PG, PP))
    (d_wg_t, d_wu_t), _ = _ffn_bwd_w(dgt, dup, hn2)
    to_sibling("sibling_gate_up", (GATE, UP), (d_wg_t, d_wu_t))
    _complete_before_next([landed[PG], landed[PP]])
    chip_sum(DOWN)
    to_chips("chips_down", (DOWN,))
    (da, du, d_wpool, d_scale, d_wo), _ = _mix_bwd(dh1b, u, a, m_out, w_out_f, wp, pool_scale)
    to_sibling("sibling_out", (OUT,), (d_wo,))
    chip_sum(GATE, UP)
    to_chips("chips_gate_up", (GATE, UP))
    _complete_before_next([landed[DOWN]])
    (dqn, dkn, dv, dl_acc, d_sinks), _ = _attn_bwd(qn, kn, v, a, da, tab, attn_sinks)
    chip_sum(OUT)
    early, early_at = _pack_small([d_wpool.reshape(POOL_WIDTH, POOL_GROUP), d_scale, d_g_ffn, d_g_ple, loss_part[:, :1]])
    (early_all,) = to_chips("chips_out", (OUT,), early)
    (grad_x, d_win_t, d_g_attn, d_gq, d_gk), _ = _in_proj_bwd(dqn, dkn, dv, du, zqk, xs, dh1, g_attn_norm, gq_t, gk_t, w_in_t)
    to_sibling("sibling_in", (IN,), (d_win_t,))
    _complete_before_next([landed[OUT], landed[GATE], landed[UP], early_all])
    (d_rel_t,), _ = _bias_table_bwd(dl_acc)
    chip_sum(IN)
    late, late_at = _pack_small([d_g_attn, d_gq[:, :HEAD_DIM], d_gk[:, :HEAD_DIM], d_sinks[:, 0], d_rel_t])
    (late_all,) = to_chips("chips_in", (IN,), late)

    out = {"grad": {}, "delta": {}, "new_m": {}, "new_v": {}}
    for ks in ((PG, PP, DOWN), (OUT, GATE, UP), (IN,)):
        names = [BIG_WEIGHTS[k][0] for k in ks]
        results = _adamw_big(ks, place, [
            (partial[k], from_sibling[k], landed[k], to_blocks(k, weights[n]), to_blocks(k, m_in[n]),
             to_blocks(k, v_in[n])) for k, n in zip(ks, names)])
        for k, name, res in zip(ks, names, results):
            for kind, r in zip(("grad", "delta", "new_m", "new_v"), res):
                out[kind][name] = from_blocks(k, r)
    def as_rows(name, arr):
        return arr.T if name == "rel_bias" else arr.reshape(POOL_WIDTH, POOL_GROUP) if name == "w_pool" else arr

    def from_rows(name, arr):
        return arr.T if name == "rel_bias" else arr.reshape(w_pool.shape) if name == "w_pool" else arr

    grads_at = dict(w_pool=(0, early_at[0]), pool_scale=(0, early_at[1]), g_ffn_norm=(0, early_at[2]),
                    g_ple_norm=(0, early_at[3]), g_attn_norm=(1, late_at[0]), g_q=(1, late_at[1]), g_k=(1, late_at[2]),
                    attn_sinks=(1, late_at[3]), rel_bias=(1, late_at[4]))
    loss, updates = _small_update(
        [early_all, late_all], (0, early_at[4]), [grads_at[n] for n in SMALL_NAMES],
        [as_rows(n, weights[n]) for n in SMALL_NAMES], [as_rows(n, m_in[n]) for n in SMALL_NAMES],
        [as_rows(n, v_in[n]) for n in SMALL_NAMES])
    loss = loss.reshape(())
    n_small = len(SMALL_NAMES)
    for j, kind in enumerate(("grad", "delta", "new_m", "new_v")):
        for i, name in enumerate(SMALL_NAMES):
            out[kind][name] = from_rows(name, updates[j * n_small + i])

    _issued.clear()
    order = ("w_in", "w_out", "g_attn_norm", "g_q", "g_k", "attn_sinks", "rel_bias", "w_pool", "pool_scale",
             "g_ffn_norm", "w_gate", "w_up", "w_down", "g_ple_norm", "w_ple_gate", "w_ple_proj")
    return (loss, grad_x[None], *[out["grad"][n] for n in order], *[out["delta"][n] for n in order],
            *[out["new_m"][n] for n in order], *[out["new_v"][n] for n in order])
```

```python
import math

import jax
import jax.numpy as jnp
import numpy as np
from jax import lax
from jax.experimental import pallas as pl
from jax.experimental.pallas import tpu as pltpu
from jax.experimental.pallas import tpu_sc as plsc

F32 = jnp.float32
BF16 = jnp.bfloat16
MESH = pl.DeviceIdType.MESH

D_MODEL = 1024
HEAD_DIM = 64
ATTN_WIDTH = 512
KV_WIDTH = 128
POOL_WIDTH = 512
POOL_SIZES = (2, 4, 8, 16)
POOL_GROUP = 128
POOL_HALO = 16
IN_WIDTH = 1280
D_FF = 2816
PLE_DIM = 256
BLOCK = 128
N_BUCKETS = 32
MAX_DISTANCE = 128
EPS = 1e-6
N_DEV = 8
N_CHIPS = 4

ADAM_LR = 0.001
ADAM_B1 = 0.9
ADAM_B2 = 0.999
ADAM_EPS = 1e-08
ADAM_WD = 0.01
ADAM_STEP = 10

TOKEN_TILE = 512
FFN_BWD_TILE = 256
FF_CHUNK = 256
FFN_W_SLAB = 256
ATTN_STEP_BLOCKS = 4
HEADS_A = (0, 2, 5, 7)
HEADS_B = (1, 3, 4, 6)
SMALL_LANES = 128


def _nn(a, b):
    return jnp.dot(a, b, preferred_element_type=F32)


def _nt(a, b):
    return lax.dot_general(a, b, (((1,), (1,)), ((), ())), preferred_element_type=F32)


def _tn(a, b):
    return lax.dot_general(a, b, (((0,), (0,)), ((), ())), preferred_element_type=F32)


def _resident(shape):
    nd = len(shape)
    return pl.BlockSpec(shape, lambda i, _nd=nd: (0,) * _nd, pipeline_mode=pl.Buffered(1))


def _rows(tile, width):
    return pl.BlockSpec((tile, width), lambda i: (i, 0))


def _acc(shape):
    nd = len(shape)
    return pl.BlockSpec(shape, lambda i, _nd=nd: (0,) * _nd)


def _head_mean_matrix(width):
    idx = np.arange(width) // HEAD_DIM
    return jnp.asarray((idx[:, None] == idx[None, :]).astype(np.float32) / HEAD_DIM, dtype=BF16)


def _seg_mean(v, bmat):
    hi = v.astype(BF16)
    lo = (v - hi.astype(F32)).astype(BF16)
    return _nn(hi, bmat) + _nn(lo, bmat)


def _rms(x):
    return lax.rsqrt(jnp.mean(x * x, axis=-1, keepdims=True) + EPS)


def _rms_bwd(d_y, x, r, g):
    gy = d_y * g
    d_x = r * gy - x * (r * r * r) * jnp.mean(gy * x, axis=-1, keepdims=True)
    d_g = jnp.sum(d_y * (x * r), axis=0, keepdims=True)
    return d_x, d_g


def _lane_lo(shape):
    return lax.broadcasted_iota(jnp.int32, shape, 1) < HEAD_DIM


class _Rider:
    def __init__(self, inputs, out_shapes, sems, begin, end, middle=None):
        self.inputs, self.out_shapes, self.sems = list(inputs), list(out_shapes), list(sems)
        self.begin, self.middle, self.end = begin, middle, end


_issued = []


def _after_last(args, in_specs):
    extra = list(_issued)
    return list(args) + extra, list(in_specs) + [pl.BlockSpec(memory_space=pl.ANY)] * len(extra), len(extra)


def _mark_issued(out):
    _issued[:] = [out]


def _complete_before_next(arrays):
    _issued.extend(arrays)


def _call(body, args, *, name, grid, in_specs, out_specs, out_shape, scratch_shapes=()):
    n_args = len(args)
    args, in_specs, _ = _after_last(args, in_specs)

    def ordered(*refs):
        body(*refs[:n_args], *refs[len(args):])

    outs = pl.pallas_call(ordered, name=name, grid=grid, in_specs=in_specs, out_specs=list(out_specs),
                          out_shape=list(out_shape), scratch_shapes=list(scratch_shapes))(*args)
    _mark_issued(outs[0])
    return list(outs)


def _in_proj(x, g_attn, w_in_t, gq_t, gk_t):
    s = x.shape[0]
    ts = min(TOKEN_TILE, s)

    def body(x_ref, g_ref, w_ref, gq_ref, gk_ref, bq_ref, bk_ref, zqk_ref, qn_ref, kn_ref, v_ref, u_ref):
        xf = x_ref[...]
        hn = ((xf * _rms(xf)) * g_ref[...]).astype(BF16)
        z = _nt(hn, w_ref[...])
        q = z[:, :ATTN_WIDTH]
        k = z[:, ATTN_WIDTH:ATTN_WIDTH + KV_WIDTH]
        zqk_ref[...] = z[:, :ATTN_WIDTH + KV_WIDTH]
        rq = lax.rsqrt(_seg_mean(q * q, bq_ref[...]) + EPS)
        qn_ref[...] = ((q * rq) * gq_ref[...]).astype(BF16)
        rk = lax.rsqrt(_seg_mean(k * k, bk_ref[...]) + EPS)
        kn_ref[...] = ((k * rk) * gk_ref[...]).astype(BF16)
        v_ref[...] = z[:, ATTN_WIDTH + KV_WIDTH:ATTN_WIDTH + 2 * KV_WIDTH].astype(BF16)
        u_ref[...] = z[:, ATTN_WIDTH + 2 * KV_WIDTH:]

    return _call(
        body,
        (x, g_attn, w_in_t, gq_t, gk_t, _head_mean_matrix(ATTN_WIDTH), _head_mean_matrix(KV_WIDTH)),
        name="in_proj",
        grid=(s // ts,),
        in_specs=[
            _rows(ts, D_MODEL),
            _resident((1, D_MODEL)),
            _resident((IN_WIDTH, D_MODEL)),
            _resident((1, ATTN_WIDTH)),
            _resident((1, KV_WIDTH)),
            _resident((ATTN_WIDTH, ATTN_WIDTH)),
            _resident((KV_WIDTH, KV_WIDTH)),
        ],
        out_specs=[
            _rows(ts, ATTN_WIDTH + KV_WIDTH),
            _rows(ts, ATTN_WIDTH),
            _rows(ts, KV_WIDTH),
            _rows(ts, KV_WIDTH),
            _rows(ts, POOL_WIDTH),
        ],
        out_shape=[
            jax.ShapeDtypeStruct((s, ATTN_WIDTH + KV_WIDTH), F32),
            jax.ShapeDtypeStruct((s, ATTN_WIDTH), BF16),
            jax.ShapeDtypeStruct((s, KV_WIDTH), BF16),
            jax.ShapeDtypeStruct((s, KV_WIDTH), BF16),
            jax.ShapeDtypeStruct((s, POOL_WIDTH), F32),
        ],
    )


def _bucket_ranges():
    n = np.arange(MAX_DISTANCE)
    max_exact = N_BUCKETS // 2
    nf = np.maximum(n, 1).astype(np.float64)
    large = max_exact + (np.log(nf / max_exact) / math.log(MAX_DISTANCE / max_exact) * (N_BUCKETS - max_exact)).astype(np.int64)
    bucket = np.where(n < max_exact, n, np.minimum(large, N_BUCKETS - 1))
    out = []
    for b in range(N_BUCKETS):
        idx = np.nonzero(bucket == b)[0]
        out.append((int(idx.min()), int(idx.max()) + 1))
    return out


def _band_distance():
    i = lax.broadcasted_iota(jnp.int32, (BLOCK, 2 * BLOCK), 0)
    j = lax.broadcasted_iota(jnp.int32, (BLOCK, 2 * BLOCK), 1)
    return BLOCK + i - j


BIAS_TABLE_SHAPE = (2, 4 * BLOCK, 2 * BLOCK)


def _write_bias_table(rb_ref, tab_ref):
    d = _band_distance()
    for half, heads in enumerate((HEADS_A, HEADS_B)):
        for slot, h in enumerate(heads):
            t = jnp.full((BLOCK, 2 * BLOCK), -jnp.inf, F32)
            for b, (lo, hi) in enumerate(_bucket_ranges()):
                t = jnp.where((d >= lo) & (d < hi), rb_ref[h, b], t)
            tab_ref[half, slot * BLOCK:(slot + 1) * BLOCK, :] = t


def _bias_table_bwd(dl_acc):
    ranges = _bucket_ranges()
    n_heads = len(HEADS_A) + len(HEADS_B)

    def body(dl_ref, out_ref):
        d = _band_distance()
        row = lax.broadcasted_iota(jnp.int32, (n_heads, SMALL_LANES), 0)
        lane = lax.broadcasted_iota(jnp.int32, (n_heads, SMALL_LANES), 1)
        out = jnp.zeros((n_heads, SMALL_LANES), F32)
        for b, (lo, hi) in enumerate(ranges):
            in_bucket = (d >= lo) & (d < hi)
            for half, heads in enumerate((HEADS_A, HEADS_B)):
                for slot, h in enumerate(heads):
                    g = dl_ref[half, slot * BLOCK:(slot + 1) * BLOCK, :]
                    part = jnp.sum(jnp.where(in_bucket, g, 0.0), axis=0, keepdims=True)
                    tot = jnp.sum(part, axis=1, keepdims=True)
                    out = jnp.where((row == h) & (lane == b), tot, out)
        out_ref[...] = out

    return _call(
        body,
        (dl_acc,),
        name="bias_table_bwd",
        grid=(1,),
        in_specs=[_acc((2, 4 * BLOCK, 2 * BLOCK))],
        out_specs=[_acc((n_heads, SMALL_LANES))],
        out_shape=[jax.ShapeDtypeStruct((n_heads, SMALL_LANES), F32)],
    )


def _stack_heads(pairs, lo_mask):
    zero = jnp.zeros_like(pairs[0])
    lo = [jnp.where(lo_mask, t, zero) for t in pairs]
    hi = [jnp.where(lo_mask, zero, t) for t in pairs]
    return (jnp.concatenate([lo[0], lo[1], hi[2], hi[3]], axis=0),
            jnp.concatenate([hi[0], hi[1], lo[2], lo[3]], axis=0))


def _unstack_heads(out_a, out_b, lo_mask):
    t = lambda x, r: x[r * BLOCK:(r + 1) * BLOCK, :]
    return [
        jnp.where(lo_mask, t(out_a, 0), t(out_b, 0)),
        jnp.where(lo_mask, t(out_a, 1), t(out_b, 1)),
        jnp.where(lo_mask, t(out_b, 2), t(out_a, 2)),
        jnp.where(lo_mask, t(out_b, 3), t(out_a, 3)),
    ]


def _sink_column(sink_ref, heads):
    row = lax.broadcasted_iota(jnp.int32, (4 * BLOCK, 1), 0)
    col = jnp.full((4 * BLOCK, 1), sink_ref[0, heads[3]], F32)
    for slot in (2, 1, 0):
        col = jnp.where(row < (slot + 1) * BLOCK, sink_ref[0, heads[slot]], col)
    return col


def _band_probs(q_stack, keys, tab, sink, first_block):
    s = _nt(q_stack, keys) * (HEAD_DIM ** -0.5) + tab
    if first_block is not None:
        col = lax.broadcasted_iota(jnp.int32, s.shape, 1)
        s = jnp.where(jnp.logical_and(first_block, col < BLOCK), -jnp.inf, s)
    m = jnp.maximum(jnp.max(s, axis=-1, keepdims=True), sink)
    e = jnp.exp(s - m)
    e_sink = jnp.exp(sink - m)
    den = jnp.sum(e, axis=-1, keepdims=True) + e_sink
    return e / den, e_sink / den


def _attn_specs(n_groups):
    group = lambda n: (jnp.minimum(n, n_groups - 1), 0)
    prev = lambda n: (jnp.maximum(jnp.minimum(n, n_groups - 1) * ATTN_STEP_BLOCKS - 1, 0), 0)
    return group, prev


def _band(prev_ref, group_ref, b):
    rows = lambda i: group_ref[i * BLOCK:(i + 1) * BLOCK, :]
    band = jnp.concatenate([prev_ref[...] if b == 0 else rows(b - 1), rows(b)], axis=0)
    return band, pltpu.roll(band, HEAD_DIM, 1)


def _attn_fwd(qn, kn, v, tab, sinks):
    s = qn.shape[0]
    n_groups = s // (ATTN_STEP_BLOCKS * BLOCK)
    group, prev = _attn_specs(n_groups)
    rows = ATTN_STEP_BLOCKS * BLOCK

    def body(sink_ref, q_ref, kc_ref, kp_ref, vc_ref, vp_ref, tab_ref, o_ref):
        first = pl.program_id(0) == 0
        lo_mask = _lane_lo((BLOCK, BLOCK))
        for b in range(ATTN_STEP_BLOCKS):
            at = slice(b * BLOCK, (b + 1) * BLOCK)
            kk, kk_sw = _band(kp_ref, kc_ref, b)
            vv, vv_sw = _band(vp_ref, vc_ref, b)
            q_a, q_b = _stack_heads([q_ref[at, p * BLOCK:(p + 1) * BLOCK] for p in range(4)], lo_mask)
            no_prev = first if b == 0 else None
            p_a, _ = _band_probs(q_a, kk, tab_ref[0], _sink_column(sink_ref, HEADS_A), no_prev)
            p_b, _ = _band_probs(q_b, kk_sw, tab_ref[1], _sink_column(sink_ref, HEADS_B), no_prev)
            out = _unstack_heads(_nn(p_a.astype(BF16), vv), _nn(p_b.astype(BF16), vv_sw), lo_mask)
            for p in range(4):
                o_ref[at, p * BLOCK:(p + 1) * BLOCK] = out[p].astype(BF16)

    return _call(
        body,
        (sinks, qn, kn, kn, v, v, tab),
        name="attn_fwd",
        grid=(n_groups,),
        in_specs=[
            pl.BlockSpec(memory_space=pltpu.SMEM),
            pl.BlockSpec((rows, ATTN_WIDTH), group),
            pl.BlockSpec((rows, KV_WIDTH), group),
            pl.BlockSpec((BLOCK, KV_WIDTH), prev),
            pl.BlockSpec((rows, KV_WIDTH), group),
            pl.BlockSpec((BLOCK, KV_WIDTH), prev),
            _resident((2, 4 * BLOCK, 2 * BLOCK)),
        ],
        out_specs=[pl.BlockSpec((rows, ATTN_WIDTH), group)],
        out_shape=[jax.ShapeDtypeStruct((s, ATTN_WIDTH), BF16)],
    )


def _pooled(u_tile, u_halo, tile_index, tile_rows):
    halo = jnp.where(tile_index > 0, u_halo, 0.0)
    ext = jnp.concatenate([halo, u_tile], axis=0)
    sums = []
    acc = ext
    for shift in (1, 2, 4, 8):
        acc = acc + pltpu.roll(acc, shift, 0)
        sums.append(acc)
    t = tile_index * tile_rows + lax.broadcasted_iota(jnp.int32, (tile_rows, 1), 0)
    out = []
    for g, w in enumerate(POOL_SIZES):
        lanes = slice(g * POOL_GROUP, (g + 1) * POOL_GROUP)
        cnt = jnp.minimum(t + 1, w).astype(F32)
        out.append(sums[g][POOL_HALO:, lanes] / cnt - u_tile[:, lanes])
    return out


def _halo_before(tile):
    return lambda i: (jnp.maximum(i * (tile // POOL_HALO) - 1, 0), 0)


def _mix_out(u, a, x, w_out, w_pool, pool_scale, g_ffn):
    s = x.shape[0]
    ts = min(TOKEN_TILE, s)

    def body(u_ref, uh_ref, a_ref, x_ref, wo_ref, wp_ref, sc_ref, g_ref, h1_ref, hn_ref, m_ref):
        i = pl.program_id(0)
        pooled = _pooled(u_ref[...], uh_ref[...], i, ts)
        for g in range(len(POOL_SIZES)):
            lanes = slice(g * POOL_GROUP, (g + 1) * POOL_GROUP)
            y = _nn(pooled[g].astype(BF16), wp_ref[g].astype(BF16))
            m_ref[:, lanes] = (y * sc_ref[:, lanes]).astype(BF16)
        h1 = x_ref[...] + _nn(a_ref[...], wo_ref[:ATTN_WIDTH, :]) + _nn(m_ref[...], wo_ref[ATTN_WIDTH:, :])
        h1_ref[...] = h1
        hn_ref[...] = ((h1 * _rms(h1)) * g_ref[...]).astype(BF16)

    return _call(
        body,
        (u, u, a, x, w_out, w_pool, pool_scale, g_ffn),
        name="mix_out",
        grid=(s // ts,),
        in_specs=[
            _rows(ts, POOL_WIDTH),
            pl.BlockSpec((POOL_HALO, POOL_WIDTH), _halo_before(ts)),
            _rows(ts, ATTN_WIDTH),
            _rows(ts, D_MODEL),
            _resident((D_MODEL, D_MODEL)),
            _resident((len(POOL_SIZES), POOL_GROUP, POOL_GROUP)),
            _resident((1, POOL_WIDTH)),
            _resident((1, D_MODEL)),
        ],
        out_specs=[_rows(ts, D_MODEL), _rows(ts, D_MODEL), _rows(ts, POOL_WIDTH)],
        out_shape=[
            jax.ShapeDtypeStruct((s, D_MODEL), F32),
            jax.ShapeDtypeStruct((s, D_MODEL), BF16),
            jax.ShapeDtypeStruct((s, POOL_WIDTH), BF16),
        ],
    )


def _ffn_up(hn2, wg_t, wu_t):
    s = hn2.shape[0]
    ts = min(TOKEN_TILE, s)

    def body(hn_ref, wg_ref, wu_ref, gt_ref, up_ref):
        hn = hn_ref[...]
        for c in range(D_FF // FF_CHUNK):
            cols = slice(c * FF_CHUNK, (c + 1) * FF_CHUNK)
            gt_ref[:, cols] = _nt(hn, wg_ref[cols, :]).astype(BF16)
            up_ref[:, cols] = _nt(hn, wu_ref[cols, :]).astype(BF16)

    return _call(
        body,
        (hn2, wg_t, wu_t),
        name="ffn_up",
        grid=(s // ts,),
        in_specs=[_rows(ts, D_MODEL), _resident((D_FF, D_MODEL)), _resident((D_FF, D_MODEL))],
        out_specs=[_rows(ts, D_FF), _rows(ts, D_FF)],
        out_shape=[jax.ShapeDtypeStruct((s, D_FF), BF16), jax.ShapeDtypeStruct((s, D_FF), BF16)],
    )


def _silu_mul(gt, up):
    return (gt * jax.nn.sigmoid(gt)) * up


def _ffn_down_ple(gt, up, h1, w_down, p, target, g_ple, w_pg, w_pp):
    s = h1.shape[0]
    ts = min(TOKEN_TILE, s)
    blk = D_MODEL // N_DEV

    def body(gt_ref, up_ref, h1_ref, wd_ref, p_ref, t_ref, g_ref, wpg_ref, wpp_ref,
             loss_ref, dh_ref, dwpg_ref, dwpp_ref, dg_ref, pp_ref):
        @pl.when(pl.program_id(0) == 0)
        def _():
            loss_ref[...] = jnp.zeros_like(loss_ref)
            dwpg_ref[...] = jnp.zeros_like(dwpg_ref)
            dwpp_ref[...] = jnp.zeros_like(dwpp_ref)
            dg_ref[...] = jnp.zeros_like(dg_ref)

        h2v = h1_ref[...]
        for c in range(D_FF // FF_CHUNK):
            cols = slice(c * FF_CHUNK, (c + 1) * FF_CHUNK)
            act = _silu_mul(gt_ref[:, cols].astype(F32), up_ref[:, cols].astype(F32)).astype(BF16)
            h2v = _nn(act, wd_ref[cols, :]) + h2v
        r = _rms(h2v)
        hn = ((h2v * r) * g_ref[...]).astype(BF16)
        gate = jax.nn.sigmoid(_nn(hn, wpg_ref[...]))
        pb = p_ref[...].astype(BF16)
        for j in range(N_DEV):
            pp_ref[:, j * blk:(j + 1) * blk] = _nn(pb, wpp_ref[j])
        pp = pp_ref[...]
        diff = (h2v + gate * pp) - t_ref[...]
        loss_ref[...] += jnp.sum(jnp.sum(diff * diff, axis=0, keepdims=True), axis=1, keepdims=True) * (0.5 / D_MODEL)
        dy = diff * (1.0 / D_MODEL)
        d_pp = (dy * gate).astype(BF16)
        d_pre = ((dy * pp) * (gate * (1.0 - gate))).astype(BF16)
        for j in range(N_DEV):
            dwpp_ref[j] += _tn(pb, d_pp[:, j * blk:(j + 1) * blk])
        dwpg_ref[...] += _tn(hn, d_pre)
        d_x, d_g = _rms_bwd(_nt(d_pre, wpg_ref[...]), h2v, r, g_ref[...])
        dg_ref[...] += d_g
        dh_ref[...] = dy + d_x

    return _call(
        body,
        (gt, up, h1, w_down, p, target, g_ple, w_pg, w_pp),
        name="ffn_down_ple",
        grid=(s // ts,),
        in_specs=[
            _rows(ts, D_FF),
            _rows(ts, D_FF),
            _rows(ts, D_MODEL),
            _resident((D_FF, D_MODEL)),
            _rows(ts, PLE_DIM),
            _rows(ts, D_MODEL),
            _resident((1, D_MODEL)),
            _resident((D_MODEL, D_MODEL)),
            _resident((N_DEV, PLE_DIM, blk)),
        ],
        out_specs=[
            _acc((1, SMALL_LANES)),
            _rows(ts, D_MODEL),
            _acc((D_MODEL, D_MODEL)),
            _acc((N_DEV, PLE_DIM, blk)),
            _acc((1, D_MODEL)),
        ],
        out_shape=[
            jax.ShapeDtypeStruct((1, SMALL_LANES), F32),
            jax.ShapeDtypeStruct((s, D_MODEL), F32),
            jax.ShapeDtypeStruct((D_MODEL, D_MODEL), F32),
            jax.ShapeDtypeStruct((N_DEV, PLE_DIM, blk), F32),
            jax.ShapeDtypeStruct((1, D_MODEL), F32),
        ],
        scratch_shapes=[pltpu.VMEM((ts, D_MODEL), F32)],
    )


def _ffn_bwd_act(dh2, h1, gt, up, g_ffn, wg_t, wu_t, w_down):
    s = h1.shape[0]
    ts = min(FFN_BWD_TILE, s)

    def body(dh_ref, h1_ref, gt_ref, up_ref, g_ref, wg_ref, wu_ref, wd_ref,
             dgt_ref, dup_ref, dh1_ref, dh1b_ref, dg_ref, dwd_ref, act_ref):
        @pl.when(pl.program_id(0) == 0)
        def _():
            dg_ref[...] = jnp.zeros_like(dg_ref)
            dwd_ref[...] = jnp.zeros_like(dwd_ref)

        dhb = dh_ref[...].astype(BF16)
        d_hn = jnp.zeros((ts, D_MODEL), F32)
        for c in range(D_FF // FF_CHUNK):
            cols = slice(c * FF_CHUNK, (c + 1) * FF_CHUNK)
            d_act = _nt(dhb, wd_ref[cols, :])
            gtv = gt_ref[:, cols].astype(F32)
            upv = up_ref[:, cols].astype(F32)
            sg = jax.nn.sigmoid(gtv)
            silu = gtv * sg
            act_ref[:, cols] = (silu * upv).astype(BF16)
            d_up = (d_act * silu).astype(BF16)
            d_gt = ((d_act * upv) * (sg * (1.0 + gtv * (1.0 - sg)))).astype(BF16)
            dup_ref[:, cols] = d_up
            dgt_ref[:, cols] = d_gt
            d_hn = (_nn(d_gt, wg_ref[cols, :]) + _nn(d_up, wu_ref[cols, :])) + d_hn
        dwd_ref[...] += _tn(act_ref[...], dhb)
        h1v = h1_ref[...]
        d_x, d_g = _rms_bwd(d_hn, h1v, _rms(h1v), g_ref[...])
        dg_ref[...] += d_g
        dh1 = dh_ref[...] + d_x
        dh1_ref[...] = dh1
        dh1b_ref[...] = dh1.astype(BF16)

    return _call(
        body,
        (dh2, h1, gt, up, g_ffn, wg_t, wu_t, w_down),
        name="ffn_bwd_act",
        grid=(s // ts,),
        in_specs=[
            _rows(ts, D_MODEL),
            _rows(ts, D_MODEL),
            _rows(ts, D_FF),
            _rows(ts, D_FF),
            _resident((1, D_MODEL)),
            _resident((D_FF, D_MODEL)),
            _resident((D_FF, D_MODEL)),
            _resident((D_FF, D_MODEL)),
        ],
        out_specs=[
            _rows(ts, D_FF), _rows(ts, D_FF),
            _rows(ts, D_MODEL), _rows(ts, D_MODEL), _acc((1, D_MODEL)), _acc((D_FF, D_MODEL)),
        ],
        out_shape=[
            jax.ShapeDtypeStruct((s, D_FF), BF16),
            jax.ShapeDtypeStruct((s, D_FF), BF16),
            jax.ShapeDtypeStruct((s, D_MODEL), F32),
            jax.ShapeDtypeStruct((s, D_MODEL), BF16),
            jax.ShapeDtypeStruct((1, D_MODEL), F32),
            jax.ShapeDtypeStruct((D_FF, D_MODEL), F32),
        ],
        scratch_shapes=[pltpu.VMEM((ts, D_FF), BF16)],
    )


def _ffn_bwd_w(dgt, dup, hn2):
    s = hn2.shape[0]
    slab = pl.BlockSpec((s, FFN_W_SLAB), lambda i: (0, i))

    def body(dgt_ref, dup_ref, hn_ref, dwg_ref, dwu_ref):
        hn = hn_ref[...]
        dwg_ref[...] = _tn(dgt_ref[...], hn)
        dwu_ref[...] = _tn(dup_ref[...], hn)

    return _call(
        body,
        (dgt, dup, hn2),
        name="ffn_bwd_w",
        grid=(D_FF // FFN_W_SLAB,),
        in_specs=[slab, slab, _resident((s, D_MODEL))],
        out_specs=[_rows(FFN_W_SLAB, D_MODEL)] * 2,
        out_shape=[jax.ShapeDtypeStruct((D_FF, D_MODEL), F32)] * 2,
    )


def _mix_bwd(dh1b, u, a, m, w_out, w_pool, pool_scale):
    s = u.shape[0]
    ts = min(TOKEN_TILE, s)
    nt = s // ts
    halo_after = lambda i: (jnp.minimum((i + 1) * (ts // POOL_HALO), s // POOL_HALO - 1), 0)
    n_groups = len(POOL_SIZES)

    def body(dh_ref, dhn_ref, u_ref, uh_ref, a_ref, m_ref, wo_ref, wp_ref, sc_ref,
             da_ref, du_ref, dwp_ref, dsc_ref, dwo_ref):
        i = pl.program_id(0)

        @pl.when(i == 0)
        def _():
            dwp_ref[...] = jnp.zeros_like(dwp_ref)
            dsc_ref[...] = jnp.zeros_like(dsc_ref)
            dwo_ref[...] = jnp.zeros_like(dwo_ref)

        dh = dh_ref[...]
        dwo_ref[:ATTN_WIDTH, :] += _tn(a_ref[...], dh)
        dwo_ref[ATTN_WIDTH:, :] += _tn(m_ref[...], dh)
        da_ref[...] = _nt(dh, wo_ref[:ATTN_WIDTH, :])
        dh_next = jnp.where(i < nt - 1, dhn_ref[...], jnp.zeros_like(dhn_ref))
        dm_ext = _nt(jnp.concatenate([dh, dh_next], axis=0), wo_ref[ATTN_WIDTH:, :])
        pooled = _pooled(u_ref[...], uh_ref[...], i, ts)
        t_ext = i * ts + lax.broadcasted_iota(jnp.int32, (ts + POOL_HALO, 1), 0)
        for g, w in enumerate(POOL_SIZES):
            lanes = slice(g * POOL_GROUP, (g + 1) * POOL_GROUP)
            wp = wp_ref[g].astype(BF16)
            pg = pooled[g].astype(BF16)
            dm_g = dm_ext[:, lanes]
            dsc_ref[:, lanes] += jnp.sum(dm_g[:ts, :] * _nn(pg, wp), axis=0, keepdims=True)
            dy = (dm_g * sc_ref[:, lanes]).astype(BF16)
            dwp_ref[g] += _tn(pg, dy[:ts, :])
            d_pool = _nt(dy, wp)
            acc = d_pool / jnp.minimum(t_ext + 1, w).astype(F32)
            shift = 1
            while shift < w:
                acc = acc + pltpu.roll(acc, ts + POOL_HALO - shift, 0)
                shift *= 2
            du_ref[:, lanes] = acc[:ts, :] - d_pool[:ts, :]

    return _call(
        body,
        (dh1b, dh1b, u, u, a, m, w_out, w_pool, pool_scale),
        name="mix_bwd",
        grid=(nt,),
        in_specs=[
            _rows(ts, D_MODEL),
            pl.BlockSpec((POOL_HALO, D_MODEL), halo_after),
            _rows(ts, POOL_WIDTH),
            pl.BlockSpec((POOL_HALO, POOL_WIDTH), _halo_before(ts)),
            _rows(ts, ATTN_WIDTH),
            _rows(ts, POOL_WIDTH),
            _resident((D_MODEL, D_MODEL)),
            _resident((n_groups, POOL_GROUP, POOL_GROUP)),
            _resident((1, POOL_WIDTH)),
        ],
        out_specs=[
            _rows(ts, ATTN_WIDTH),
            _rows(ts, POOL_WIDTH),
            _acc((n_groups, POOL_GROUP, POOL_GROUP)),
            _acc((1, POOL_WIDTH)),
            _acc((D_MODEL, D_MODEL)),
        ],
        out_shape=[
            jax.ShapeDtypeStruct((s, ATTN_WIDTH), F32),
            jax.ShapeDtypeStruct((s, POOL_WIDTH), F32),
            jax.ShapeDtypeStruct((n_groups, POOL_GROUP, POOL_GROUP), F32),
            jax.ShapeDtypeStruct((1, POOL_WIDTH), F32),
            jax.ShapeDtypeStruct((D_MODEL, D_MODEL), F32),
        ],
    )


def _attn_bwd(qn, kn, v, a, da, tab, sinks):
    s = qn.shape[0]
    qb = ATTN_STEP_BLOCKS
    rows = qb * BLOCK
    n_groups = s // rows
    group, prev = _attn_specs(n_groups)
    done = lambda n: (jnp.maximum(n - 1, 0), 0)

    def body(sink_ref, q_ref, kc_ref, kp_ref, vc_ref, vp_ref, o_ref, do_ref, tab_ref,
             dq_ref, dk_ref, dv_ref, dl_ref, ds_ref, k_carry, v_carry, sink_acc):
        n = pl.program_id(0)

        @pl.when(n == 0)
        def _():
            dl_ref[...] = jnp.zeros_like(dl_ref)
            k_carry[...] = jnp.zeros_like(k_carry)
            v_carry[...] = jnp.zeros_like(v_carry)
            sink_acc[...] = jnp.zeros_like(sink_acc)

        @pl.when(n < n_groups)
        def _():
            first = n == 0
            lo_mask = _lane_lo((BLOCK, BLOCK))
            dks, dvs = [], []
            for b in range(qb):
                at = slice(b * BLOCK, (b + 1) * BLOCK)
                keys = _band(kp_ref, kc_ref, b)
                vals = _band(vp_ref, vc_ref, b)
                q_st = _stack_heads([q_ref[at, p * BLOCK:(p + 1) * BLOCK] for p in range(4)], lo_mask)
                do_st = _stack_heads([do_ref[at, p * BLOCK:(p + 1) * BLOCK] for p in range(4)], lo_mask)
                o_st = _stack_heads([o_ref[at, p * BLOCK:(p + 1) * BLOCK].astype(F32) for p in range(4)], lo_mask)
                dq_st, dk_parts, dv_parts = [], [], []
                for half, heads in enumerate((HEADS_A, HEADS_B)):
                    probs, p_sink = _band_probs(q_st[half], keys[half], tab_ref[half], _sink_column(sink_ref, heads),
                                                first if b == 0 else None)
                    delta = jnp.sum(do_st[half] * o_st[half], axis=-1, keepdims=True)
                    dob = do_st[half].astype(BF16)
                    dl = probs * (_nt(dob, vals[half]) - delta)
                    dl_ref[half] += dl
                    sink_acc[half] += p_sink * delta
                    dsb = (dl * (HEAD_DIM ** -0.5)).astype(BF16)
                    dq_st.append(_nn(dsb, keys[half]))
                    dk_parts.append(_tn(dsb, q_st[half]))
                    dv_parts.append(_tn(probs.astype(BF16), dob))
                dq = _unstack_heads(dq_st[0], dq_st[1], lo_mask)
                for p in range(4):
                    dq_ref[at, p * BLOCK:(p + 1) * BLOCK] = dq[p]
                dks.append(dk_parts[0] + pltpu.roll(dk_parts[1], HEAD_DIM, 1))
                dvs.append(dv_parts[0] + pltpu.roll(dv_parts[1], HEAD_DIM, 1))
            last = slice((qb - 1) * BLOCK, qb * BLOCK)
            for parts, out_ref, carry in ((dks, dk_ref, k_carry), (dvs, dv_ref, v_carry)):
                out_ref[...] = carry[...]
                out_ref[last, :] += parts[0][:BLOCK, :]
                for b in range(qb):
                    own = parts[b][BLOCK:, :]
                    carry[b * BLOCK:(b + 1) * BLOCK, :] = own + parts[b + 1][:BLOCK, :] if b + 1 < qb else own

        @pl.when(n == n_groups)
        def _():
            dk_ref[...] = k_carry[...]
            dv_ref[...] = v_carry[...]
            for half, heads in enumerate((HEADS_A, HEADS_B)):
                for slot, h in enumerate(heads):
                    tot = jnp.sum(sink_acc[half, slot * BLOCK:(slot + 1) * BLOCK, :], axis=0, keepdims=True)
                    ds_ref[h:h + 1, :] = jnp.broadcast_to(-tot, (1, SMALL_LANES))

    return _call(
        body,
        (sinks, qn, kn, kn, v, v, a, da, tab),
        name="attn_bwd",
        grid=(n_groups + 1,),
        in_specs=[
            pl.BlockSpec(memory_space=pltpu.SMEM),
            pl.BlockSpec((rows, ATTN_WIDTH), group),
            pl.BlockSpec((rows, KV_WIDTH), group),
            pl.BlockSpec((BLOCK, KV_WIDTH), prev),
            pl.BlockSpec((rows, KV_WIDTH), group),
            pl.BlockSpec((BLOCK, KV_WIDTH), prev),
            pl.BlockSpec((rows, ATTN_WIDTH), group),
            pl.BlockSpec((rows, ATTN_WIDTH), group),
            _resident((2, 4 * BLOCK, 2 * BLOCK)),
        ],
        out_specs=[
            pl.BlockSpec((rows, ATTN_WIDTH), group),
            pl.BlockSpec((rows, KV_WIDTH), done),
            pl.BlockSpec((rows, KV_WIDTH), done),
            _acc((2, 4 * BLOCK, 2 * BLOCK)),
            _acc((N_DEV, SMALL_LANES)),
        ],
        out_shape=[
            jax.ShapeDtypeStruct((s, ATTN_WIDTH), F32),
            jax.ShapeDtypeStruct((s, KV_WIDTH), F32),
            jax.ShapeDtypeStruct((s, KV_WIDTH), F32),
            jax.ShapeDtypeStruct((2, 4 * BLOCK, 2 * BLOCK), F32),
            jax.ShapeDtypeStruct((N_DEV, SMALL_LANES), F32),
        ],
        scratch_shapes=[
            pltpu.VMEM((rows, KV_WIDTH), F32),
            pltpu.VMEM((rows, KV_WIDTH), F32),
            pltpu.VMEM((2, 4 * BLOCK, 1), F32),
        ],
    )


def _fold_heads(acc):
    t = acc + pltpu.roll(acc, HEAD_DIM, 1)
    out = t[:, :SMALL_LANES]
    for g in range(1, acc.shape[1] // SMALL_LANES):
        out = out + t[:, g * SMALL_LANES:(g + 1) * SMALL_LANES]
    return out


def _in_proj_bwd(dqn, dkn, dv, du, zqk, x, dh1, g_attn, gq_t, gk_t, w_in_t):
    s = x.shape[0]
    ts = min(TOKEN_TILE, s)
    nt = s // ts

    def head_norm_bwd(d_n, raw, g_t, bmat):
        r = lax.rsqrt(_seg_mean(raw * raw, bmat) + EPS)
        gy = d_n * g_t
        d_raw = r * gy - raw * (r * r * r) * _seg_mean(gy * raw, bmat)
        return d_raw, jnp.sum(d_n * (raw * r), axis=0, keepdims=True)

    def body(dqn_ref, dkn_ref, dv_ref, du_ref, zqk_ref, x_ref, dh1_ref, g_ref, gq_ref, gk_ref, w_ref, bq_ref, bk_ref,
             gx_ref, dw_ref, dg_ref, dgq_ref, dgk_ref, dz_ref, gq_acc, gk_acc):
        i = pl.program_id(0)

        @pl.when(i == 0)
        def _():
            dw_ref[...] = jnp.zeros_like(dw_ref)
            dg_ref[...] = jnp.zeros_like(dg_ref)
            gq_acc[...] = jnp.zeros_like(gq_acc)
            gk_acc[...] = jnp.zeros_like(gk_acc)

        d_q, d_gq = head_norm_bwd(dqn_ref[...], zqk_ref[:, :ATTN_WIDTH], gq_ref[...], bq_ref[...])
        d_k, d_gk = head_norm_bwd(dkn_ref[...], zqk_ref[:, ATTN_WIDTH:], gk_ref[...], bk_ref[...])
        gq_acc[...] += d_gq
        gk_acc[...] += d_gk
        dz_ref[:, :ATTN_WIDTH] = d_q.astype(BF16)
        dz_ref[:, ATTN_WIDTH:ATTN_WIDTH + KV_WIDTH] = d_k.astype(BF16)
        dz_ref[:, ATTN_WIDTH + KV_WIDTH:ATTN_WIDTH + 2 * KV_WIDTH] = dv_ref[...].astype(BF16)
        dz_ref[:, ATTN_WIDTH + 2 * KV_WIDTH:] = du_ref[...].astype(BF16)
        dz = dz_ref[...]
        xf = x_ref[...]
        r = _rms(xf)
        hn = ((xf * r) * g_ref[...]).astype(BF16)
        dw_ref[...] += _tn(dz, hn)
        d_x, d_g = _rms_bwd(_nn(dz, w_ref[...]), xf, r, g_ref[...])
        dg_ref[...] += d_g
        gx_ref[...] = dh1_ref[...] + d_x

        @pl.when(i == nt - 1)
        def _():
            dgq_ref[...] = _fold_heads(gq_acc[...])
            dgk_ref[...] = _fold_heads(gk_acc[...])

    return _call(
        body,
        (dqn, dkn, dv, du, zqk, x, dh1, g_attn, gq_t, gk_t, w_in_t,
      _head_mean_matrix(ATTN_WIDTH), _head_mean_matrix(KV_WIDTH)),
        name="in_proj_bwd",
        grid=(nt,),
        in_specs=[
            _rows(ts, ATTN_WIDTH),
            _rows(ts, KV_WIDTH),
            _rows(ts, KV_WIDTH),
            _rows(ts, POOL_WIDTH),
            _rows(ts, ATTN_WIDTH + KV_WIDTH),
            _rows(ts, D_MODEL),
            _rows(ts, D_MODEL),
            _resident((1, D_MODEL)),
            _resident((1, ATTN_WIDTH)),
            _resident((1, KV_WIDTH)),
            _resident((IN_WIDTH, D_MODEL)),
            _resident((ATTN_WIDTH, ATTN_WIDTH)),
            _resident((KV_WIDTH, KV_WIDTH)),
        ],
        out_specs=[
            _rows(ts, D_MODEL),
            _acc((IN_WIDTH, D_MODEL)),
            _acc((1, D_MODEL)),
            _acc((1, SMALL_LANES)),
            _acc((1, SMALL_LANES)),
        ],
        out_shape=[
            jax.ShapeDtypeStruct((s, D_MODEL), F32),
            jax.ShapeDtypeStruct((IN_WIDTH, D_MODEL), F32),
            jax.ShapeDtypeStruct((1, D_MODEL), F32),
            jax.ShapeDtypeStruct((1, SMALL_LANES), F32),
            jax.ShapeDtypeStruct((1, SMALL_LANES), F32),
        ],
        scratch_shapes=[
            pltpu.VMEM((ts, IN_WIDTH), BF16),
            pltpu.VMEM((1, ATTN_WIDTH), F32),
            pltpu.VMEM((1, KV_WIDTH), F32),
        ],
    )


BIG_WEIGHTS = (
    ("w_in", True, IN_WIDTH // N_DEV, D_MODEL),
    ("w_out", False, D_MODEL // N_DEV, D_MODEL),
    ("w_gate", True, D_FF // N_DEV, D_MODEL),
    ("w_up", True, D_FF // N_DEV, D_MODEL),
    ("w_down", False, D_FF // N_DEV, D_MODEL),
    ("w_ple_gate", False, D_MODEL // N_DEV, D_MODEL),
    ("w_ple_proj", False, PLE_DIM, D_MODEL // N_DEV),
)
N_BIG = len(BIG_WEIGHTS)


def _place():
    x, y, c = lax.axis_index("x"), lax.axis_index("y"), lax.axis_index("c")
    chips = [(1 - x, y), (x, 1 - y), (1 - x, 1 - y)]
    return x, y, c, chips


class _Gather:
    def __init__(self, n):
        self.n = n
        self.sems = [pltpu.SemaphoreType.DMA((n, 7)), pltpu.SemaphoreType.DMA((n, 7)), pltpu.SemaphoreType.DMA((n,))]

    def _ctx(self, srcs, outs, sems):
        send_sems, recv_sems, local_sems = sems
        x, y, c, chips = _place()
        me, sibling = (x, y, c), (x, y, 1 - c)

        def block(k, owner):
            px, py, pc = owner
            return outs[k].at[4 * px + 2 * py + pc]

        def copy(k, idx, owner, to, mine=False):
            return pltpu.make_async_remote_copy(
                src_ref=srcs[k] if mine else block(k, owner), dst_ref=block(k, owner),
                send_sem=send_sems.at[k, idx], recv_sem=recv_sems.at[k, idx], device_id=to, device_id_type=MESH)

        def local(k):
            return pltpu.make_async_copy(srcs[k], block(k, me), local_sems.at[k])

        return c, chips, me, sibling, copy, local

    def begin(self, srcs, outs, sems):
        c, chips, me, sibling, copy, local = self._ctx(srcs, outs, sems)
        for k in range(self.n):
            local(k).start()
            copy(k, 0, me, sibling, mine=True).start()
            for j, chip in enumerate(chips):
                copy(k, 1 + j, me, (*chip, c), mine=True).start()

    def middle(self, srcs, outs, sems):
        c, chips, me, sibling, copy, local = self._ctx(srcs, outs, sems)
        for j, chip in enumerate(chips):
            for k in range(self.n):
                copy(k, 1 + j, (*chip, c), me).wait_recv()
                copy(k, 4 + j, (*chip, c), sibling).start()

    def end(self, srcs, outs, sems):
        c, chips, me, sibling, copy, local = self._ctx(srcs, outs, sems)
        for k in range(self.n):
            copy(k, 0, sibling, me).wait_recv()
            for j, chip in enumerate(chips):
                copy(k, 4 + j, (*chip, 1 - c), me).wait_recv()
        for k in range(self.n):
            copy(k, 0, me, sibling, mine=True).wait_send()
            for j, chip in enumerate(chips):
                copy(k, 1 + j, me, (*chip, c), mine=True).wait_send()
                copy(k, 4 + j, (*chip, c), sibling).wait_send()
            local(k).wait()


def _gather_rider(shards):
    g = _Gather(len(shards))
    shapes = [jax.ShapeDtypeStruct((N_DEV, *sh.shape), sh.dtype) for sh in shards]
    return _Rider(shards, shapes, g.sems, g.begin, g.end, g.middle)


def _cast_and_gather_first(shards, rel_bias_t):
    g = _Gather(1)
    any_spec = pl.BlockSpec(memory_space=pl.ANY)
    vmem = pl.BlockSpec(memory_space=pltpu.VMEM)

    def body(*refs):
        ins, rb_ref, outs = refs[:N_BIG], refs[N_BIG], refs[N_BIG + 1:2 * N_BIG + 1]
        gathered, tab_ref, sems = refs[2 * N_BIG + 1], refs[2 * N_BIG + 2], refs[2 * N_BIG + 3:]
        outs[0][...] = ins[0][...].astype(BF16)
        g.begin(outs[:1], [gathered], sems)
        for k in range(1, N_BIG):
            outs[k][...] = ins[k][...].astype(BF16)
        _write_bias_table(rb_ref, tab_ref)
        g.middle(outs[:1], [gathered], sems)
        g.end(outs[:1], [gathered], sems)

    res = pl.pallas_call(
        body,
        name="cast_and_gather_first",
        in_specs=[vmem] * N_BIG + [pl.BlockSpec(memory_space=pltpu.SMEM)],
        out_specs=[vmem] * N_BIG + [any_spec, vmem],
        out_shape=[jax.ShapeDtypeStruct((r, c), BF16) for _, _, r, c in BIG_WEIGHTS]
        + [jax.ShapeDtypeStruct((N_DEV, *BIG_WEIGHTS[0][2:]), BF16), jax.ShapeDtypeStruct(BIAS_TABLE_SHAPE, F32)],
        scratch_shapes=g.sems,
    )(*shards, rel_bias_t)
    return list(res[:N_BIG]), res[N_BIG], res[N_BIG + 1]


def _sibling_rider(grads):
    n = len(grads)

    def copies(gs, lands, sems):
        send_sems, recv_sems = sems
        x, y, c, _ = _place()
        return [
            pltpu.make_async_remote_copy(
                src_ref=gs[k].at[:, 1 - c], dst_ref=lands[k], send_sem=send_sems.at[k], recv_sem=recv_sems.at[k],
                device_id=(x, y, 1 - c), device_id_type=MESH)
            for k in range(n)
        ]

    def begin(gs, lands, sems):
        for cp in copies(gs, lands, sems):
            cp.start()

    def end(gs, lands, sems):
        for cp in copies(gs, lands, sems):
            cp.wait()

    shapes = [jax.ShapeDtypeStruct((N_CHIPS, *g.shape[2:]), F32) for g in grads]
    return _Rider(grads, shapes, [pltpu.SemaphoreType.DMA((n,)), pltpu.SemaphoreType.DMA((n,))], begin, end)


def _chip_of_relation(j, place):
    x, y = place[0], place[1]
    return jnp.where(j == 0, 2 * (1 - x) + y, jnp.where(j == 1, 2 * x + 1 - y, 2 * (1 - x) + 1 - y))


def _chip_sum(ks, place, grads, from_sibling):
    n = len(ks)
    shapes = [BIG_WEIGHTS[k][2:] for k in ks]
    operands, specs = [], []
    for (r, c), g, l in zip(shapes, grads, from_sibling):
        operands += [g, l]
        specs += [pl.BlockSpec((1, 1, r, c), lambda j, place: (_chip_of_relation(j, place), place[2], 0, 0)),
                  pl.BlockSpec((1, r, c), lambda j, place: (_chip_of_relation(j, place), 0, 0))]
    args, in_specs, _ = _after_last(operands, specs)

    def body(place_ref, *refs):
        ins, outs = refs[:2 * n], refs[len(args):]
        for i in range(n):
            outs[i][0] = (ins[2 * i][0, 0] + ins[2 * i + 1][0]).astype(BF16)

    outs = pl.pallas_call(
        body,
        name="chip_sum_" + "_".join(BIG_WEIGHTS[k][0] for k in ks),
        grid_spec=pltpu.PrefetchScalarGridSpec(
            num_scalar_prefetch=1,
            grid=(N_CHIPS - 1,),
            in_specs=in_specs,
            out_specs=[pl.BlockSpec((1, r, c), lambda j, place: (j, 0, 0)) for r, c in shapes],
        ),
        out_shape=[jax.ShapeDtypeStruct((N_CHIPS - 1, r, c), BF16) for r, c in shapes],
    )(place, *args)
    _mark_issued(outs[0])
    return list(outs)


def _chips_rider(to_send, small=None):
    n = len(to_send)
    inputs = list(to_send) + ([] if small is None else [small])
    shapes = [jax.ShapeDtypeStruct((3, *t.shape[1:]), BF16) for t in to_send]
    sems = [pltpu.SemaphoreType.DMA((max(n, 1), 3)), pltpu.SemaphoreType.DMA((max(n, 1), 3))]
    if small is not None:
        shapes.append(jax.ShapeDtypeStruct((N_DEV, *small.shape), F32))
        sems += [pltpu.SemaphoreType.DMA((7,)), pltpu.SemaphoreType.DMA((7,)), pltpu.SemaphoreType.DMA]

    def copies(ins, outs, sem_refs):
        x, y, c, chips = _place()
        out = []
        for k in range(n):
            for j, (px, py) in enumerate(chips):
                out.append(pltpu.make_async_remote_copy(
                    src_ref=ins[k].at[j], dst_ref=outs[k].at[j],
                    send_sem=sem_refs[0].at[k, j], recv_sem=sem_refs[1].at[k, j],
                    device_id=(px, py, c), device_id_type=MESH))
        local = None
        if small is not None:
            me = 4 * x + 2 * y + c
            local = pltpu.make_async_copy(ins[n], outs[n].at[me], sem_refs[4])
            rel = 0
            for fx in (0, 1):
                for fy in (0, 1):
                    for fc in (0, 1):
                        if (fx, fy, fc) != (0, 0, 0):
                            out.append(pltpu.make_async_remote_copy(
                                src_ref=ins[n], dst_ref=outs[n].at[me],
                                send_sem=sem_refs[2].at[rel], recv_sem=sem_refs[3].at[rel],
                                device_id=(x ^ fx, y ^ fy, c ^ fc), device_id_type=MESH))
                            rel += 1
        return out, local

    def begin(ins, outs, sem_refs):
        remote, local = copies(ins, outs, sem_refs)
        if local is not None:
            local.start()
        for cp in remote:
            cp.start()

    def end(ins, outs, sem_refs):
        remote, local = copies(ins, outs, sem_refs)
        for cp in remote:
            cp.wait()
        if local is not None:
            local.wait()

    return _Rider(inputs, shapes, sems, begin, end)


PEER_SETS = {"sibling": 1, "chips": 2, "sibling+chips": 3, "all": 4}


def _peers(pattern):
    x, y, c, chips = _place()
    sibling, others = [(x, y, 1 - c)], [(*chip, c) for chip in chips]
    if pattern == "all":
        return sibling + others + [(*chip, 1 - c) for chip in chips]
    return {"sibling": sibling, "chips": others, "sibling+chips": sibling + others}[pattern]


def _on_sequencer(name, pattern, rider):
    n_in, n_out = len(rider.inputs), len(rider.out_shapes)

    def body(*refs):
        ins, outs, sems = refs[:n_in], refs[n_in:n_in + n_out], refs[n_in + n_out:]
        peers = _peers(pattern)
        barrier = pltpu.get_barrier_semaphore()
        for peer in peers:
            pl.semaphore_signal(barrier, inc=1, device_id=peer, device_id_type=MESH)
        pl.semaphore_wait(barrier, len(peers))
        rider.begin(ins, outs, sems)
        if rider.middle is not None:
            rider.middle(ins, outs, sems)
        rider.end(ins, outs, sems)

    outs = pl.kernel(
        body,
        name=name,
        out_type=tuple(rider.out_shapes),
        mesh=plsc.ScalarSubcoreMesh(axis_name="sequencer", num_cores=1),
        scratch_types=tuple(rider.sems),
        compiler_params=pltpu.CompilerParams(collective_id=PEER_SETS[pattern]),
    )(*rider.inputs)
    return list(outs)


def _adamw(w, g, m, v):
    m = ADAM_B1 * m + (1.0 - ADAM_B1) * g
    v = ADAM_B2 * v + (1.0 - ADAM_B2) * jnp.square(g)
    m_hat = m / (1.0 - ADAM_B1 ** ADAM_STEP)
    v_hat = v / (1.0 - ADAM_B2 ** ADAM_STEP)
    delta = -ADAM_LR * (m_hat / (jnp.sqrt(v_hat) + ADAM_EPS) + ADAM_WD * w)
    return delta, m, v


def _adamw_big(ks, place, operands):
    n = len(ks)
    tiles = lambda i, place: (i, 0)
    in_specs, out_specs, out_shape = [], [], []
    for k in ks:
        _, _, r, c = BIG_WEIGHTS[k]
        tile = r // 2
        in_specs += [
            pl.BlockSpec((1, 1, tile, c), lambda i, place: (2 * place[0] + place[1], place[2], i, 0)),
            pl.BlockSpec((1, tile, c), lambda i, place: (2 * place[0] + place[1], i, 0)),
            pl.BlockSpec((3, tile, c), lambda i, place: (0, i, 0)),
        ] + [pl.BlockSpec((tile, c), tiles)] * 3
        out_specs += [pl.BlockSpec((tile, c), tiles)] * 4
        out_shape += [jax.ShapeDtypeStruct((r, c), F32)] * 4
    args, in_specs, _ = _after_last(sum((list(ops) for ops in operands), []), in_specs)

    def body(place_ref, *refs):
        ins, outs = refs[:6 * n], refs[len(args):]
        for i in range(n):
            mine_ref, sib_ref, land_ref, w_ref, m_ref, v_ref = ins[6 * i:6 * i + 6]
            g_ref, d_ref, nm_ref, nv_ref = outs[4 * i:4 * i + 4]
            g = mine_ref[0, 0] + sib_ref[0]
            g = ((g + land_ref[0].astype(F32)) + land_ref[1].astype(F32)) + land_ref[2].astype(F32)
            g_ref[...] = g
            d_ref[...], nm_ref[...], nv_ref[...] = _adamw(w_ref[...], g, m_ref[...], v_ref[...])

    outs = pl.pallas_call(
        body,
        name="adamw_" + "_".join(BIG_WEIGHTS[k][0] for k in ks),
        grid_spec=pltpu.PrefetchScalarGridSpec(
            num_scalar_prefetch=1, grid=(2,), in_specs=in_specs, out_specs=out_specs),
        out_shape=out_shape,
    )(place, *args)
    _mark_issued(outs[0])
    return [outs[4 * i:4 * i + 4] for i in range(n)]


def _pack_small(arrays):
    rows, offsets = [], []
    at = 0
    for a in arrays:
        if a.ndim != 2 or a.shape[1] != SMALL_LANES or a.shape[0] % 8:
            flat = a.reshape(-1)
            n_rows = -(-flat.shape[0] // (8 * SMALL_LANES)) * 8
            a = jnp.pad(flat, (0, n_rows * SMALL_LANES - flat.shape[0])).reshape(n_rows, SMALL_LANES)
        rows.append(a)
        offsets.append(at)
        at += a.shape[0]
    return jnp.concatenate(rows, axis=0), offsets


def _unpack_small(tot, at, shape):
    r, c = shape
    if r % 8 == 0:
        return tot[at:at + r, :c]
    assert r == 1
    if c <= SMALL_LANES:
        return tot[at:at + 1, :c]
    return jnp.concatenate([tot[at + j:at + j + 1, :] for j in range(c // SMALL_LANES)], axis=1)


def _small_update(packs, loss_at, grads_at, ws, ms, vs):
    n, n_packs = len(ws), len(packs)

    def body(*refs):
        pack_refs, refs = refs[:n_packs], refs[n_packs:]
        w_refs, m_refs, v_refs, loss_ref, outs = refs[:n], refs[n:2 * n], refs[2 * n:3 * n], refs[3 * n], refs[3 * n + 1:]
        tots = []
        for p_ref in pack_refs:
            tot = p_ref[0]
            for j in range(1, N_DEV):
                tot = tot + p_ref[j]
            tots.append(tot)
        loss_ref[...] = _unpack_small(tots[loss_at[0]], loss_at[1], (1, 1))
        for i, (pack, at) in enumerate(grads_at):
            g = _unpack_small(tots[pack], at, w_refs[i].shape)
            outs[i][...] = g
            outs[n + i][...], outs[2 * n + i][...], outs[3 * n + i][...] = _adamw(
                w_refs[i][...], g, m_refs[i][...], v_refs[i][...])

    shapes = [jax.ShapeDtypeStruct(w.shape, F32) for w in ws]
    outs = pl.pallas_call(body, name="small_update", out_shape=[jax.ShapeDtypeStruct((1, 1), F32)] + shapes * 4)(
        *packs, *ws, *ms, *vs)
    return outs[0], outs[1:]


SMALL_NAMES = ("g_attn_norm", "g_q", "g_k", "attn_sinks", "rel_bias", "w_pool", "pool_scale", "g_ffn_norm", "g_ple_norm")


def kernel(x, p, w_in, w_out, g_attn_norm, g_q, g_k, attn_sinks, rel_bias, w_pool, pool_scale, g_ffn_norm, w_gate, w_up, w_down, g_ple_norm, w_ple_gate, w_ple_proj, loss_target, m_w_in, m_w_out, m_g_attn_norm, m_g_q, m_g_k, m_attn_sinks, m_rel_bias, m_w_pool, m_pool_scale, m_g_ffn_norm, m_w_gate, m_w_up, m_w_down, m_g_ple_norm, m_w_ple_gate, m_w_ple_proj, v_w_in, v_w_out, v_g_attn_norm, v_g_q, v_g_k, v_attn_sinks, v_rel_bias, v_w_pool, v_pool_scale, v_g_ffn_norm, v_w_gate, v_w_up, v_w_down, v_g_ple_norm, v_w_ple_gate, v_w_ple_proj):
    weights = dict(w_in=w_in, w_out=w_out, g_attn_norm=g_attn_norm, g_q=g_q, g_k=g_k, attn_sinks=attn_sinks,
                   rel_bias=rel_bias, w_pool=w_pool, pool_scale=pool_scale, g_ffn_norm=g_ffn_norm, w_gate=w_gate,
                   w_up=w_up, w_down=w_down, g_ple_norm=g_ple_norm, w_ple_gate=w_ple_gate, w_ple_proj=w_ple_proj)
    m_in = dict(w_in=m_w_in, w_out=m_w_out, g_attn_norm=m_g_attn_norm, g_q=m_g_q, g_k=m_g_k, attn_sinks=m_attn_sinks,
                rel_bias=m_rel_bias, w_pool=m_w_pool, pool_scale=m_pool_scale, g_ffn_norm=m_g_ffn_norm, w_gate=m_w_gate,
                w_up=m_w_up, w_down=m_w_down, g_ple_norm=m_g_ple_norm, w_ple_gate=m_w_ple_gate, w_ple_proj=m_w_ple_proj)
    v_in = dict(w_in=v_w_in, w_out=v_w_out, g_attn_norm=v_g_attn_norm, g_q=v_g_q, g_k=v_g_k, attn_sinks=v_attn_sinks,
                rel_bias=v_rel_bias, w_pool=v_w_pool, pool_scale=v_pool_scale, g_ffn_norm=v_g_ffn_norm, w_gate=v_w_gate,
                w_up=v_w_up, w_down=v_w_down, g_ple_norm=v_g_ple_norm, w_ple_gate=v_w_ple_gate, w_ple_proj=v_w_ple_proj)

    _issued.clear()
    xs = x[0]
    ps = p[0, 0]
    target = loss_target[0]
    wp = w_pool[0]
    gq_t = jnp.tile(g_q, (1, ATTN_WIDTH // HEAD_DIM))
    gk_t = jnp.tile(g_k, (1, KV_WIDTH // HEAD_DIM))

    def to_blocks(k, arr):
        return jnp.swapaxes(arr[0], 0, 1) if BIG_WEIGHTS[k][1] else arr[0]

    def from_blocks(k, arr):
        return (jnp.swapaxes(arr, 0, 1) if BIG_WEIGHTS[k][1] else arr)[None]

    IN, OUT, GATE, UP, DOWN, PG, PP = range(N_BIG)
    full = lambda g: g.reshape(N_DEV * g.shape[1], g.shape[2])
    halves = lambda k, g: g.reshape(N_CHIPS, 2, *BIG_WEIGHTS[k][2:])
    place = jnp.stack([lax.axis_index("x"), lax.axis_index("y"), lax.axis_index("c")]).astype(jnp.int32)

    sh, w_in_g, tab = _cast_and_gather_first(
        [to_blocks(k, weights[name]) for k, (name, _, _, _) in enumerate(BIG_WEIGHTS)], rel_bias.T)
    w_in_t = full(w_in_g)

    (w_out_g,) = _on_sequencer("gather_out", "sibling+chips", _gather_rider([sh[OUT]]))
    wg_g, wu_g = _on_sequencer("gather_gate_up", "sibling+chips", _gather_rider([sh[GATE], sh[UP]]))
    wd_g, w_pg_g, w_pp_g = _on_sequencer("gather_down_ple", "sibling+chips", _gather_rider([sh[DOWN], sh[PG], sh[PP]]))
    (zqk, qn, kn, v, u) = _in_proj(xs, g_attn_norm, w_in_t, gq_t, gk_t)
    (a,) = _attn_fwd(qn, kn, v, tab, attn_sinks)
    w_out_f = full(w_out_g)
    (h1, hn2, m_out) = _mix_out(u, a, xs, w_out_f, wp, pool_scale, g_ffn_norm)
    wg_t, wu_t = full(wg_g), full(wu_g)
    (gt, up) = _ffn_up(hn2, wg_t, wu_t)
    w_down_f = full(wd_g)

    partial, from_sibling, sums, landed = [None] * N_BIG, [None] * N_BIG, [None] * N_BIG, [None] * N_BIG

    def to_sibling(name, ks, grads):
        for k, g in zip(ks, grads):
            partial[k] = halves(k, g)
        got = _on_sequencer(name, "sibling", _sibling_rider([partial[k] for k in ks]))
        for k, g in zip(ks, got):
            from_sibling[k] = g

    def chip_sum(*ks):
        for k, s in zip(ks, _chip_sum(ks, place, [partial[k] for k in ks], [from_sibling[k] for k in ks])):
            sums[k] = s

    def to_chips(name, ks, small=None):
        got = _on_sequencer(name, "chips" if small is None else "all", _chips_rider([sums[k] for k in ks], small))
        for k, g in zip(ks, got):
            landed[k] = g
        return got[len(ks):]

    (loss_part, dh2, d_wpg, d_wpp, d_g_ple) = _ffn_down_ple(
        gt, up, h1, w_down_f, ps, target, g_ple_norm, full(w_pg_g), w_pp_g)
    to_sibling("sibling_ple", (PG, PP), (d_wpg, d_wpp))
    (dgt, dup, dh1, dh1b, d_g_ffn, d_wd) = _ffn_bwd_act(dh2, h1, gt, up, g_ffn_norm, wg_t, wu_t, w_down_f)
    to_sibling("sibling_down", (DOWN,), (d_wd,))
    chip_sum(PG, PP)
    to_chips("chips_ple", (PG, PP))
    (d_wg_t, d_wu_t) = _ffn_bwd_w(dgt, dup, hn2)
    to_sibling("sibling_gate_up", (GATE, UP), (d_wg_t, d_wu_t))
    _complete_before_next([landed[PG], landed[PP]])
    chip_sum(DOWN)
    to_chips("chips_down", (DOWN,))
    (da, du, d_wpool, d_scale, d_wo) = _mix_bwd(dh1b, u, a, m_out, w_out_f, wp, pool_scale)
    to_sibling("sibling_out", (OUT,), (d_wo,))
    chip_sum(GATE, UP)
    to_chips("chips_gate_up", (GATE, UP))
    _complete_before_next([landed[DOWN]])
    (dqn, dkn, dv, dl_acc, d_sinks) = _attn_bwd(qn, kn, v, a, da, tab, attn_sinks)
    chip_sum(OUT)
    early, early_at = _pack_small([d_wpool.reshape(POOL_WIDTH, POOL_GROUP), d_scale, d_g_ffn, d_g_ple, loss_part[:, :1]])
    (early_all,) = to_chips("chips_out", (OUT,), early)
    (grad_x, d_win_t, d_g_attn, d_gq, d_gk) = _in_proj_bwd(dqn, dkn, dv, du, zqk, xs, dh1, g_attn_norm, gq_t, gk_t, w_in_t)
    to_sibling("sibling_in", (IN,), (d_win_t,))
    _complete_before_next([landed[OUT], landed[GATE], landed[UP], early_all])
    (d_rel_t,) = _bias_table_bwd(dl_acc)
    chip_sum(IN)
    late, late_at = _pack_small([d_g_attn, d_gq[:, :HEAD_DIM], d_gk[:, :HEAD_DIM], d_sinks[:, 0], d_rel_t])
    (late_all,) = to_chips("chips_in", (IN,), late)

    out = {"grad": {}, "delta": {}, "new_m": {}, "new_v": {}}
    for ks in ((PG, PP, DOWN), (OUT, GATE, UP), (IN,)):
        names = [BIG_WEIGHTS[k][0] for k in ks]
        results = _adamw_big(ks, place, [
            (partial[k], from_sibling[k], landed[k], to_blocks(k, weights[n]), to_blocks(k, m_in[n]),
             to_blocks(k, v_in[n])) for k, n in zip(ks, names)])
        for k, name, res in zip(ks, names, results):
            for kind, r in zip(("grad", "delta", "new_m", "new_v"), res):
                out[kind][name] = from_blocks(k, r)
    def as_rows(name, arr):
        return arr.T if name == "rel_bias" else arr.reshape(POOL_WIDTH, POOL_GROUP) if name == "w_pool" else arr

    def from_rows(name, arr):
        return arr.T if name == "rel_bias" else arr.reshape(w_pool.shape) if name == "w_pool" else arr

    grads_at = dict(w_pool=(0, early_at[0]), pool_scale=(0, early_at[1]), g_ffn_norm=(0, early_at[2]),
                    g_ple_norm=(0, early_at[3]), g_attn_norm=(1, late_at[0]), g_q=(1, late_at[1]), g_k=(1, late_at[2]),
                    attn_sinks=(1, late_at[3]), rel_bias=(1, late_at[4]))
    loss, updates = _small_update(
        [early_all, late_all], (0, early_at[4]), [grads_at[n] for n in SMALL_NAMES],
        [as_rows(n, weights[n]) for n in SMALL_NAMES], [as_rows(n, m_in[n]) for n in SMALL_NAMES],
        [as_rows(n, v_in[n]) for n in SMALL_NAMES])
    loss = loss.reshape(())
    n_small = len(SMALL_NAMES)
    for j, kind in enumerate(("grad", "delta", "new_m", "new_v")):
        for i, name in enumerate(SMALL_NAMES):
            out[kind][name] = from_rows(name, updates[j * n_small + i])

    _issued.clear()
    order = ("w_in", "w_out", "g_attn_norm", "g_q", "g_k", "attn_sinks", "rel_bias", "w_pool", "pool_scale",
             "g_ffn_norm", "w_gate", "w_up", "w_down", "g_ple_norm", "w_ple_gate", "w_ple_proj")
    return (loss, grad_x[None], *[out["grad"][n] for n in order], *[out["delta"][n] for n in order],
            *[out["new_m"][n] for n in order], *[out["new_v"][n] for n in order])
```

```python
import math

import jax
import jax.numpy as jnp
import numpy as np
from jax import lax
from jax.experimental import pallas as pl
from jax.experimental.pallas import tpu as pltpu
from jax.experimental.pallas import tpu_sc as plsc

F32 = jnp.float32
BF16 = jnp.bfloat16
MESH = pl.DeviceIdType.MESH

D_MODEL = 1024
HEAD_DIM = 64
ATTN_WIDTH = 512
KV_WIDTH = 128
POOL_WIDTH = 512
POOL_SIZES = (2, 4, 8, 16)
POOL_GROUP = 128
POOL_HALO = 16
IN_WIDTH = 1280
D_FF = 2816
PLE_DIM = 256
BLOCK = 128
N_BUCKETS = 32
MAX_DISTANCE = 128
EPS = 1e-6
N_DEV = 8
N_CHIPS = 4

ADAM_LR = 0.001
ADAM_B1 = 0.9
ADAM_B2 = 0.999
ADAM_EPS = 1e-08
ADAM_WD = 0.01
ADAM_STEP = 10

TOKEN_TILE = 512
FFN_BWD_TILE = 256
FF_CHUNK = 256
FFN_W_SLAB = 256
ATTN_STEP_BLOCKS = 4
HEADS_A = (0, 2, 5, 7)
HEADS_B = (1, 3, 4, 6)
SMALL_LANES = 128


def _nn(a, b):
    return jnp.dot(a, b, preferred_element_type=F32)


def _nt(a, b):
    return lax.dot_general(a, b, (((1,), (1,)), ((), ())), preferred_element_type=F32)


def _tn(a, b):
    return lax.dot_general(a, b, (((0,), (0,)), ((), ())), preferred_element_type=F32)


def _resident(shape):
    nd = len(shape)
    return pl.BlockSpec(shape, lambda i, _nd=nd: (0,) * _nd, pipeline_mode=pl.Buffered(1))


def _rows(tile, width):
    return pl.BlockSpec((tile, width), lambda i: (i, 0))


def _acc(shape):
    nd = len(shape)
    return pl.BlockSpec(shape, lambda i, _nd=nd: (0,) * _nd)


def _head_mean_matrix(width):
    idx = np.arange(width) // HEAD_DIM
    return jnp.asarray((idx[:, None] == idx[None, :]).astype(np.float32) / HEAD_DIM, dtype=BF16)


def _seg_mean(v, bmat):
    hi = v.astype(BF16)
    lo = (v - hi.astype(F32)).astype(BF16)
    return _nn(hi, bmat) + _nn(lo, bmat)


def _rms(x):
    return lax.rsqrt(jnp.mean(x * x, axis=-1, keepdims=True) + EPS)


def _rms_bwd(d_y, x, r, g):
    gy = d_y * g
    d_x = r * gy - x * (r * r * r) * jnp.mean(gy * x, axis=-1, keepdims=True)
    d_g = jnp.sum(d_y * (x * r), axis=0, keepdims=True)
    return d_x, d_g


def _lane_lo(shape):
    return lax.broadcasted_iota(jnp.int32, shape, 1) < HEAD_DIM


class _Rider:
    def __init__(self, inputs, out_shapes, sems, begin, end, middle=None):
        self.inputs, self.out_shapes, self.sems = list(inputs), list(out_shapes), list(sems)
        self.begin, self.middle, self.end = begin, middle, end


_issued = []


def _after_last(args, in_specs):
    extra = list(_issued)
    return list(args) + extra, list(in_specs) + [pl.BlockSpec(memory_space=pl.ANY)] * len(extra), len(extra)


def _mark_issued(out):
    _issued[:] = [out]


def _complete_before_next(arrays):
    _issued.extend(arrays)


def _call(body, args, *, name, grid, in_specs, out_specs, out_shape, scratch_shapes=()):
    n_args = len(args)
    args, in_specs, _ = _after_last(args, in_specs)

    def ordered(*refs):
        body(*refs[:n_args], *refs[len(args):])

    outs = pl.pallas_call(ordered, name=name, grid=grid, in_specs=in_specs, out_specs=list(out_specs),
                          out_shape=list(out_shape), scratch_shapes=list(scratch_shapes))(*args)
    _mark_issued(outs[0])
    return list(outs)


def _in_proj(x, g_attn, w_in_t, gq_t, gk_t):
    s = x.shape[0]
    ts = min(TOKEN_TILE, s)

    def body(x_ref, g_ref, w_ref, gq_ref, gk_ref, bq_ref, bk_ref, zqk_ref, qn_ref, kn_ref, v_ref, u_ref):
        xf = x_ref[...]
        hn = ((xf * _rms(xf)) * g_ref[...]).astype(BF16)
        z = _nt(hn, w_ref[...])
        q = z[:, :ATTN_WIDTH]
        k = z[:, ATTN_WIDTH:ATTN_WIDTH + KV_WIDTH]
        zqk_ref[...] = z[:, :ATTN_WIDTH + KV_WIDTH]
        rq = lax.rsqrt(_seg_mean(q * q, bq_ref[...]) + EPS)
        qn_ref[...] = ((q * rq) * gq_ref[...]).astype(BF16)
        rk = lax.rsqrt(_seg_mean(k * k, bk_ref[...]) + EPS)
        kn_ref[...] = ((k * rk) * gk_ref[...]).astype(BF16)
        v_ref[...] = z[:, ATTN_WIDTH + KV_WIDTH:ATTN_WIDTH + 2 * KV_WIDTH].astype(BF16)
        u_ref[...] = z[:, ATTN_WIDTH + 2 * KV_WIDTH:]

    return _call(
        body,
        (x, g_attn, w_in_t, gq_t, gk_t, _head_mean_matrix(ATTN_WIDTH), _head_mean_matrix(KV_WIDTH)),
        name="in_proj",
        grid=(s // ts,),
        in_specs=[
            _rows(ts, D_MODEL),
            _resident((1, D_MODEL)),
            _resident((IN_WIDTH, D_MODEL)),
            _resident((1, ATTN_WIDTH)),
            _resident((1, KV_WIDTH)),
            _resident((ATTN_WIDTH, ATTN_WIDTH)),
            _resident((KV_WIDTH, KV_WIDTH)),
        ],
        out_specs=[
            _rows(ts, ATTN_WIDTH + KV_WIDTH),
            _rows(ts, ATTN_WIDTH),
            _rows(ts, KV_WIDTH),
            _rows(ts, KV_WIDTH),
            _rows(ts, POOL_WIDTH),
        ],
        out_shape=[
            jax.ShapeDtypeStruct((s, ATTN_WIDTH + KV_WIDTH), F32),
            jax.ShapeDtypeStruct((s, ATTN_WIDTH), BF16),
            jax.ShapeDtypeStruct((s, KV_WIDTH), BF16),
            jax.ShapeDtypeStruct((s, KV_WIDTH), BF16),
            jax.ShapeDtypeStruct((s, POOL_WIDTH), F32),
        ],
    )


def _bucket_ranges():
    n = np.arange(MAX_DISTANCE)
    max_exact = N_BUCKETS // 2
    nf = np.maximum(n, 1).astype(np.float64)
    large = max_exact + (np.log(nf / max_exact) / math.log(MAX_DISTANCE / max_exact) * (N_BUCKETS - max_exact)).astype(np.int64)
    bucket = np.where(n < max_exact, n, np.minimum(large, N_BUCKETS - 1))
    out = []
    for b in range(N_BUCKETS):
        idx = np.nonzero(bucket == b)[0]
        out.append((int(idx.min()), int(idx.max()) + 1))
    return out


def _band_distance():
    i = lax.broadcasted_iota(jnp.int32, (BLOCK, 2 * BLOCK), 0)
    j = lax.broadcasted_iota(jnp.int32, (BLOCK, 2 * BLOCK), 1)
    return BLOCK + i - j


BIAS_TABLE_SHAPE = (2, 4 * BLOCK, 2 * BLOCK)


def _write_bias_table(rb_ref, tab_ref):
    d = _band_distance()
    for half, heads in enumerate((HEADS_A, HEADS_B)):
        for slot, h in enumerate(heads):
            t = jnp.full((BLOCK, 2 * BLOCK), -jnp.inf, F32)
            for b, (lo, hi) in enumerate(_bucket_ranges()):
                t = jnp.where((d >= lo) & (d < hi), rb_ref[h, b], t)
            tab_ref[half, slot * BLOCK:(slot + 1) * BLOCK, :] = t


def _bias_table_bwd(dl_acc):
    ranges = _bucket_ranges()
    n_heads = len(HEADS_A) + len(HEADS_B)

    def body(dl_ref, out_ref):
        d = _band_distance()
        row = lax.broadcasted_iota(jnp.int32, (n_heads, SMALL_LANES), 0)
        lane = lax.broadcasted_iota(jnp.int32, (n_heads, SMALL_LANES), 1)
        out = jnp.zeros((n_heads, SMALL_LANES), F32)
        for b, (lo, hi) in enumerate(ranges):
            in_bucket = (d >= lo) & (d < hi)
            for half, heads in enumerate((HEADS_A, HEADS_B)):
                for slot, h in enumerate(heads):
                    g = dl_ref[half, slot * BLOCK:(slot + 1) * BLOCK, :]
                    part = jnp.sum(jnp.where(in_bucket, g, 0.0), axis=0, keepdims=True)
                    tot = jnp.sum(part, axis=1, keepdims=True)
                    out = jnp.where((row == h) & (lane == b), tot, out)
        out_ref[...] = out

    return _call(
        body,
        (dl_acc,),
        name="bias_table_bwd",
        grid=(1,),
        in_specs=[_acc((2, 4 * BLOCK, 2 * BLOCK))],
        out_specs=[_acc((n_heads, SMALL_LANES))],
        out_shape=[jax.ShapeDtypeStruct((n_heads, SMALL_LANES), F32)],
    )


def _stack_heads(pairs, lo_mask):
    zero = jnp.zeros_like(pairs[0])
    lo = [jnp.where(lo_mask, t, zero) for t in pairs]
    hi = [jnp.where(lo_mask, zero, t) for t in pairs]
    return (jnp.concatenate([lo[0], lo[1], hi[2], hi[3]], axis=0),
            jnp.concatenate([hi[0], hi[1], lo[2], lo[3]], axis=0))


def _unstack_heads(out_a, out_b, lo_mask):
    t = lambda x, r: x[r * BLOCK:(r + 1) * BLOCK, :]
    return [
        jnp.where(lo_mask, t(out_a, 0), t(out_b, 0)),
        jnp.where(lo_mask, t(out_a, 1), t(out_b, 1)),
        jnp.where(lo_mask, t(out_b, 2), t(out_a, 2)),
        jnp.where(lo_mask, t(out_b, 3), t(out_a, 3)),
    ]


def _sink_column(sink_ref, heads):
    row = lax.broadcasted_iota(jnp.int32, (4 * BLOCK, 1), 0)
    col = jnp.full((4 * BLOCK, 1), sink_ref[0, heads[3]], F32)
    for slot in (2, 1, 0):
        col = jnp.where(row < (slot + 1) * BLOCK, sink_ref[0, heads[slot]], col)
    return col


def _band_probs(q_stack, keys, tab, sink, first_block):
    s = _nt(q_stack, keys) * (HEAD_DIM ** -0.5) + tab
    if first_block is not None:
        col = lax.broadcasted_iota(jnp.int32, s.shape, 1)
        s = jnp.where(jnp.logical_and(first_block, col < BLOCK), -jnp.inf, s)
    m = jnp.maximum(jnp.max(s, axis=-1, keepdims=True), sink)
    e = jnp.exp(s - m)
    e_sink = jnp.exp(sink - m)
    den = jnp.sum(e, axis=-1, keepdims=True) + e_sink
    return e / den, e_sink / den


def _attn_specs(n_groups):
    group = lambda n: (jnp.minimum(n, n_groups - 1), 0)
    prev = lambda n: (jnp.maximum(jnp.minimum(n, n_groups - 1) * ATTN_STEP_BLOCKS - 1, 0), 0)
    return group, prev


def _band(prev_ref, group_ref, b):
    rows = lambda i: group_ref[i * BLOCK:(i + 1) * BLOCK, :]
    band = jnp.concatenate([prev_ref[...] if b == 0 else rows(b - 1), rows(b)], axis=0)
    return band, pltpu.roll(band, HEAD_DIM, 1)


def _attn_fwd(qn, kn, v, tab, sinks):
    s = qn.shape[0]
    n_groups = s // (ATTN_STEP_BLOCKS * BLOCK)
    group, prev = _attn_specs(n_groups)
    rows = ATTN_STEP_BLOCKS * BLOCK

    def body(sink_ref, q_ref, kc_ref, kp_ref, vc_ref, vp_ref, tab_ref, o_ref):
        first = pl.program_id(0) == 0
        lo_mask = _lane_lo((BLOCK, BLOCK))
        for b in range(ATTN_STEP_BLOCKS):
            at = slice(b * BLOCK, (b + 1) * BLOCK)
            kk, kk_sw = _band(kp_ref, kc_ref, b)
            vv, vv_sw = _band(vp_ref, vc_ref, b)
            q_a, q_b = _stack_heads([q_ref[at, p * BLOCK:(p + 1) * BLOCK] for p in range(4)], lo_mask)
            no_prev = first if b == 0 else None
            p_a, _ = _band_probs(q_a, kk, tab_ref[0], _sink_column(sink_ref, HEADS_A), no_prev)
            p_b, _ = _band_probs(q_b, kk_sw, tab_ref[1], _sink_column(sink_ref, HEADS_B), no_prev)
            out = _unstack_heads(_nn(p_a.astype(BF16), vv), _nn(p_b.astype(BF16), vv_sw), lo_mask)
            for p in range(4):
                o_ref[at, p * BLOCK:(p + 1) * BLOCK] = out[p].astype(BF16)

    return _call(
        body,
        (sinks, qn, kn, kn, v, v, tab),
        name="attn_fwd",
        grid=(n_groups,),
        in_specs=[
            pl.BlockSpec(memory_space=pltpu.SMEM),
            pl.BlockSpec((rows, ATTN_WIDTH), group),
            pl.BlockSpec((rows, KV_WIDTH), group),
            pl.BlockSpec((BLOCK, KV_WIDTH), prev),
            pl.BlockSpec((rows, KV_WIDTH), group),
            pl.BlockSpec((BLOCK, KV_WIDTH), prev),
            _resident((2, 4 * BLOCK, 2 * BLOCK)),
        ],
        out_specs=[pl.BlockSpec((rows, ATTN_WIDTH), group)],
        out_shape=[jax.ShapeDtypeStruct((s, ATTN_WIDTH), BF16)],
    )


def _pooled(u_tile, u_halo, tile_index, tile_rows):
    halo = jnp.where(tile_index > 0, u_halo, 0.0)
    ext = jnp.concatenate([halo, u_tile], axis=0)
    sums = []
    acc = ext
    for shift in (1, 2, 4, 8):
        acc = acc + pltpu.roll(acc, shift, 0)
        sums.append(acc)
    t = tile_index * tile_rows + lax.broadcasted_iota(jnp.int32, (tile_rows, 1), 0)
    out = []
    for g, w in enumerate(POOL_SIZES):
        lanes = slice(g * POOL_GROUP, (g + 1) * POOL_GROUP)
        cnt = jnp.minimum(t + 1, w).astype(F32)
        out.append(sums[g][POOL_HALO:, lanes] / cnt - u_tile[:, lanes])
    return out


def _halo_before(tile):
    return lambda i: (jnp.maximum(i * (tile // POOL_HALO) - 1, 0), 0)


def _mix_out(u, a, x, w_out, w_pool, pool_scale, g_ffn):
    s = x.shape[0]
    ts = min(TOKEN_TILE, s)

    def body(u_ref, uh_ref, a_ref, x_ref, wo_ref, wp_ref, sc_ref, g_ref, h1_ref, hn_ref, m_ref):
        i = pl.program_id(0)
        pooled = _pooled(u_ref[...], uh_ref[...], i, ts)
        for g in range(len(POOL_SIZES)):
            lanes = slice(g * POOL_GROUP, (g + 1) * POOL_GROUP)
            y = _nn(pooled[g].astype(BF16), wp_ref[g].astype(BF16))
            m_ref[:, lanes] = (y * sc_ref[:, lanes]).astype(BF16)
        h1 = x_ref[...] + _nn(a_ref[...], wo_ref[:ATTN_WIDTH, :]) + _nn(m_ref[...], wo_ref[ATTN_WIDTH:, :])
        h1_ref[...] = h1
        hn_ref[...] = ((h1 * _rms(h1)) * g_ref[...]).astype(BF16)

    return _call(
        body,
        (u, u, a, x, w_out, w_pool, pool_scale, g_ffn),
        name="mix_out",
        grid=(s // ts,),
        in_specs=[
            _rows(ts, POOL_WIDTH),
            pl.BlockSpec((POOL_HALO, POOL_WIDTH), _halo_before(ts)),
            _rows(ts, ATTN_WIDTH),
            _rows(ts, D_MODEL),
            _resident((D_MODEL, D_MODEL)),
            _resident((len(POOL_SIZES), POOL_GROUP, POOL_GROUP)),
            _resident((1, POOL_WIDTH)),
            _resident((1, D_MODEL)),
        ],
        out_specs=[_rows(ts, D_MODEL), _rows(ts, D_MODEL), _rows(ts, POOL_WIDTH)],
        out_shape=[
            jax.ShapeDtypeStruct((s, D_MODEL), F32),
            jax.ShapeDtypeStruct((s, D_MODEL), BF16),
            jax.ShapeDtypeStruct((s, POOL_WIDTH), BF16),
        ],
    )


def _ffn_up(hn2, wg_t, wu_t):
    s = hn2.shape[0]
    ts = min(TOKEN_TILE, s)

    def body(hn_ref, wg_ref, wu_ref, gt_ref, up_ref):
        hn = hn_ref[...]
        for c in range(D_FF // FF_CHUNK):
            cols = slice(c * FF_CHUNK, (c + 1) * FF_CHUNK)
            gt_ref[:, cols] = _nt(hn, wg_ref[cols, :]).astype(BF16)
            up_ref[:, cols] = _nt(hn, wu_ref[cols, :]).astype(BF16)

    return _call(
        body,
        (hn2, wg_t, wu_t),
        name="ffn_up",
        grid=(s // ts,),
        in_specs=[_rows(ts, D_MODEL), _resident((D_FF, D_MODEL)), _resident((D_FF, D_MODEL))],
        out_specs=[_rows(ts, D_FF), _rows(ts, D_FF)],
        out_shape=[jax.ShapeDtypeStruct((s, D_FF), BF16), jax.ShapeDtypeStruct((s, D_FF), BF16)],
    )


def _silu_mul(gt, up):
    return (gt * jax.nn.sigmoid(gt)) * up


def _ffn_down_ple(gt, up, h1, w_down, p, target, g_ple, w_pg, w_pp):
    s = h1.shape[0]
    ts = min(TOKEN_TILE, s)
    blk = D_MODEL // N_DEV

    def body(gt_ref, up_ref, h1_ref, wd_ref, p_ref, t_ref, g_ref, wpg_ref, wpp_ref,
             loss_ref, dh_ref, dwpg_ref, dwpp_ref, dg_ref):
        @pl.when(pl.program_id(0) == 0)
        def _():
            loss_ref[...] = jnp.zeros_like(loss_ref)
            dwpg_ref[...] = jnp.zeros_like(dwpg_ref)
            dwpp_ref[...] = jnp.zeros_like(dwpp_ref)
            dg_ref[...] = jnp.zeros_like(dg_ref)

        h2v = h1_ref[...]
        for c in range(D_FF // FF_CHUNK):
            cols = slice(c * FF_CHUNK, (c + 1) * FF_CHUNK)
            act = _silu_mul(gt_ref[:, cols].astype(F32), up_ref[:, cols].astype(F32)).astype(BF16)
            h2v = _nn(act, wd_ref[cols, :]) + h2v
        r = _rms(h2v)
        hn = ((h2v * r) * g_ref[...]).astype(BF16)
        gate = jax.nn.sigmoid(_nn(hn, wpg_ref[...]))
        pb = p_ref[...].astype(BF16)
        pp = _nn(pb, jnp.concatenate([wpp_ref[j] for j in range(N_DEV)], axis=1))
        diff = (h2v + gate * pp) - t_ref[...]
        loss_ref[...] += jnp.sum(jnp.sum(diff * diff, axis=0, keepdims=True), axis=1, keepdims=True) * (0.5 / D_MODEL)
        dy = diff * (1.0 / D_MODEL)
        d_pp = (dy * gate).astype(BF16)
        d_pre = ((dy * pp) * (gate * (1.0 - gate))).astype(BF16)
        d_x, d_g = _rms_bwd(_nt(d_pre, wpg_ref[...]), h2v, r, g_ref[...])
        dg_ref[...] += d_g
        dh_ref[...] = dy + d_x
        d_wpp = _tn(pb, d_pp)
        for j in range(N_DEV):
            dwpp_ref[j] += d_wpp[:, j * blk:(j + 1) * blk]
        dwpg_ref[...] += _tn(hn, d_pre)

    return _call(
        body,
        (gt, up, h1, w_down, p, target, g_ple, w_pg, w_pp),
        name="ffn_down_ple",
        grid=(s // ts,),
        in_specs=[
            _rows(ts, D_FF),
            _rows(ts, D_FF),
            _rows(ts, D_MODEL),
            _resident((D_FF, D_MODEL)),
            _rows(ts, PLE_DIM),
            _rows(ts, D_MODEL),
            _resident((1, D_MODEL)),
            _resident((D_MODEL, D_MODEL)),
            _resident((N_DEV, PLE_DIM, blk)),
        ],
        out_specs=[
            _acc((1, SMALL_LANES)),
            _rows(ts, D_MODEL),
            _acc((D_MODEL, D_MODEL)),
            _acc((N_DEV, PLE_DIM, blk)),
            _acc((1, D_MODEL)),
        ],
        out_shape=[
            jax.ShapeDtypeStruct((1, SMALL_LANES), F32),
            jax.ShapeDtypeStruct((s, D_MODEL), F32),
            jax.ShapeDtypeStruct((D_MODEL, D_MODEL), F32),
            jax.ShapeDtypeStruct((N_DEV, PLE_DIM, blk), F32),
            jax.ShapeDtypeStruct((1, D_MODEL), F32),
        ],
    )


def _ffn_bwd_act(dh2, h1, gt, up, g_ffn, wg_t, wu_t, w_down):
    s = h1.shape[0]
    ts = min(FFN_BWD_TILE, s)

    def body(dh_ref, h1_ref, gt_ref, up_ref, g_ref, wg_ref, wu_ref, wd_ref,
             dgt_ref, dup_ref, dh1_ref, dh1b_ref, dg_ref, dwd_ref, act_ref):
        @pl.when(pl.program_id(0) == 0)
        def _():
            dg_ref[...] = jnp.zeros_like(dg_ref)
            dwd_ref[...] = jnp.zeros_like(dwd_ref)

        dhb = dh_ref[...].astype(BF16)
        d_hn = jnp.zeros((ts, D_MODEL), F32)
        for c in range(D_FF // FF_CHUNK):
            cols = slice(c * FF_CHUNK, (c + 1) * FF_CHUNK)
            d_act = _nt(dhb, wd_ref[cols, :])
            gtv = gt_ref[:, cols].astype(F32)
            upv = up_ref[:, cols].astype(F32)
            sg = jax.nn.sigmoid(gtv)
            silu = gtv * sg
            act_ref[:, cols] = (silu * upv).astype(BF16)
            d_up = (d_act * silu).astype(BF16)
            d_gt = ((d_act * upv) * (sg * (1.0 + gtv * (1.0 - sg)))).astype(BF16)
            dup_ref[:, cols] = d_up
            dgt_ref[:, cols] = d_gt
            d_hn = (_nn(d_gt, wg_ref[cols, :]) + _nn(d_up, wu_ref[cols, :])) + d_hn
        dwd_ref[...] += _tn(act_ref[...], dhb)
        h1v = h1_ref[...]
        d_x, d_g = _rms_bwd(d_hn, h1v, _rms(h1v), g_ref[...])
        dg_ref[...] += d_g
        dh1 = dh_ref[...] + d_x
        dh1_ref[...] = dh1
        dh1b_ref[...] = dh1.astype(BF16)

    return _call(
        body,
        (dh2, h1, gt, up, g_ffn, wg_t, wu_t, w_down),
        name="ffn_bwd_act",
        grid=(s // ts,),
        in_specs=[
            _rows(ts, D_MODEL),
            _rows(ts, D_MODEL),
            _rows(ts, D_FF),
            _rows(ts, D_FF),
            _resident((1, D_MODEL)),
            _resident((D_FF, D_MODEL)),
            _resident((D_FF, D_MODEL)),
            _resident((D_FF, D_MODEL)),
        ],
        out_specs=[
            _rows(ts, D_FF), _rows(ts, D_FF),
            _rows(ts, D_MODEL), _rows(ts, D_MODEL), _acc((1, D_MODEL)), _acc((D_FF, D_MODEL)),
        ],
        out_shape=[
            jax.ShapeDtypeStruct((s, D_FF), BF16),
            jax.ShapeDtypeStruct((s, D_FF), BF16),
            jax.ShapeDtypeStruct((s, D_MODEL), F32),
            jax.ShapeDtypeStruct((s, D_MODEL), BF16),
            jax.ShapeDtypeStruct((1, D_MODEL), F32),
            jax.ShapeDtypeStruct((D_FF, D_MODEL), F32),
        ],
        scratch_shapes=[pltpu.VMEM((ts, D_FF), BF16)],
    )


def _ffn_bwd_w(dgt, dup, hn2):
    s = hn2.shape[0]
    slab = pl.BlockSpec((s, FFN_W_SLAB), lambda i: (0, i))

    def body(dgt_ref, dup_ref, hn_ref, dwg_ref, dwu_ref):
        hn = hn_ref[...]
        dwg_ref[...] = _tn(dgt_ref[...], hn)
        dwu_ref[...] = _tn(dup_ref[...], hn)

    return _call(
        body,
        (dgt, dup, hn2),
        name="ffn_bwd_w",
        grid=(D_FF // FFN_W_SLAB,),
        in_specs=[slab, slab, _resident((s, D_MODEL))],
        out_specs=[_rows(FFN_W_SLAB, D_MODEL)] * 2,
        out_shape=[jax.ShapeDtypeStruct((D_FF, D_MODEL), F32)] * 2,
    )


def _mix_bwd(dh1b, u, a, m, w_out, w_pool, pool_scale):
    s = u.shape[0]
    ts = min(TOKEN_TILE, s)
    nt = s // ts
    halo_after = lambda i: (jnp.minimum((i + 1) * (ts // POOL_HALO), s // POOL_HALO - 1), 0)
    n_groups = len(POOL_SIZES)

    def body(dh_ref, dhn_ref, u_ref, uh_ref, a_ref, m_ref, wo_ref, wp_ref, sc_ref,
             da_ref, du_ref, dwp_ref, dsc_ref, dwo_ref):
        i = pl.program_id(0)

        @pl.when(i == 0)
        def _():
            dwp_ref[...] = jnp.zeros_like(dwp_ref)
            dsc_ref[...] = jnp.zeros_like(dsc_ref)
            dwo_ref[...] = jnp.zeros_like(dwo_ref)

        dh = dh_ref[...]
        dwo_ref[:ATTN_WIDTH, :] += _tn(a_ref[...], dh)
        dwo_ref[ATTN_WIDTH:, :] += _tn(m_ref[...], dh)
        da_ref[...] = _nt(dh, wo_ref[:ATTN_WIDTH, :])
        dh_next = jnp.where(i < nt - 1, dhn_ref[...], jnp.zeros_like(dhn_ref))
        dm_ext = _nt(jnp.concatenate([dh, dh_next], axis=0), wo_ref[ATTN_WIDTH:, :])
        pooled = _pooled(u_ref[...], uh_ref[...], i, ts)
        t_ext = i * ts + lax.broadcasted_iota(jnp.int32, (ts + POOL_HALO, 1), 0)
        for g, w in enumerate(POOL_SIZES):
            lanes = slice(g * POOL_GROUP, (g + 1) * POOL_GROUP)
            wp = wp_ref[g].astype(BF16)
            pg = pooled[g].astype(BF16)
            dm_g = dm_ext[:, lanes]
            dsc_ref[:, lanes] += jnp.sum(dm_g[:ts, :] * _nn(pg, wp), axis=0, keepdims=True)
            dy = (dm_g * sc_ref[:, lanes]).astype(BF16)
            dwp_ref[g] += _tn(pg, dy[:ts, :])
            d_pool = _nt(dy, wp)
            acc = d_pool / jnp.minimum(t_ext + 1, w).astype(F32)
            shift = 1
            while shift < w:
                acc = acc + pltpu.roll(acc, ts + POOL_HALO - shift, 0)
                shift *= 2
            du_ref[:, lanes] = acc[:ts, :] - d_pool[:ts, :]

    return _call(
        body,
        (dh1b, dh1b, u, u, a, m, w_out, w_pool, pool_scale),
        name="mix_bwd",
        grid=(nt,),
        in_specs=[
            _rows(ts, D_MODEL),
            pl.BlockSpec((POOL_HALO, D_MODEL), halo_after),
            _rows(ts, POOL_WIDTH),
            pl.BlockSpec((POOL_HALO, POOL_WIDTH), _halo_before(ts)),
            _rows(ts, ATTN_WIDTH),
            _rows(ts, POOL_WIDTH),
            _resident((D_MODEL, D_MODEL)),
            _resident((n_groups, POOL_GROUP, POOL_GROUP)),
            _resident((1, POOL_WIDTH)),
        ],
        out_specs=[
            _rows(ts, ATTN_WIDTH),
            _rows(ts, POOL_WIDTH),
            _acc((n_groups, POOL_GROUP, POOL_GROUP)),
            _acc((1, POOL_WIDTH)),
            _acc((D_MODEL, D_MODEL)),
        ],
        out_shape=[
            jax.ShapeDtypeStruct((s, ATTN_WIDTH), F32),
            jax.ShapeDtypeStruct((s, POOL_WIDTH), F32),
            jax.ShapeDtypeStruct((n_groups, POOL_GROUP, POOL_GROUP), F32),
            jax.ShapeDtypeStruct((1, POOL_WIDTH), F32),
            jax.ShapeDtypeStruct((D_MODEL, D_MODEL), F32),
        ],
    )


def _attn_bwd(qn, kn, v, a, da, tab, sinks):
    s = qn.shape[0]
    qb = ATTN_STEP_BLOCKS
    rows = qb * BLOCK
    n_groups = s // rows
    group, prev = _attn_specs(n_groups)
    done = lambda n: (jnp.maximum(n - 1, 0), 0)

    def body(sink_ref, q_ref, kc_ref, kp_ref, vc_ref, vp_ref, o_ref, do_ref, tab_ref,
             dq_ref, dk_ref, dv_ref, dl_ref, ds_ref, k_carry, v_carry, sink_acc):
        n = pl.program_id(0)

        @pl.when(n == 0)
        def _():
            dl_ref[...] = jnp.zeros_like(dl_ref)
            k_carry[...] = jnp.zeros_like(k_carry)
            v_carry[...] = jnp.zeros_like(v_carry)
            sink_acc[...] = jnp.zeros_like(sink_acc)

        @pl.when(n < n_groups)
        def _():
            first = n == 0
            lo_mask = _lane_lo((BLOCK, BLOCK))
            dks, dvs = [], []
            for b in range(qb):
                at = slice(b * BLOCK, (b + 1) * BLOCK)
                keys = _band(kp_ref, kc_ref, b)
                vals = _band(vp_ref, vc_ref, b)
                q_st = _stack_heads([q_ref[at, p * BLOCK:(p + 1) * BLOCK] for p in range(4)], lo_mask)
                do_st = _stack_heads([do_ref[at, p * BLOCK:(p + 1) * BLOCK] for p in range(4)], lo_mask)
                o_st = _stack_heads([o_ref[at, p * BLOCK:(p + 1) * BLOCK].astype(F32) for p in range(4)], lo_mask)
                dq_st, dk_parts, dv_parts = [], [], []
                for half, heads in enumerate((HEADS_A, HEADS_B)):
                    probs, p_sink = _band_probs(q_st[half], keys[half], tab_ref[half], _sink_column(sink_ref, heads),
                                                first if b == 0 else None)
                    delta = jnp.sum(do_st[half] * o_st[half], axis=-1, keepdims=True)
                    dob = do_st[half].astype(BF16)
                    dl = probs * (_nt(dob, vals[half]) - delta)
                    dl_ref[half] += dl
                    sink_acc[half] += p_sink * delta
                    dsb = (dl * (HEAD_DIM ** -0.5)).astype(BF16)
                    dq_st.append(_nn(dsb, keys[half]))
                    dk_parts.append(_tn(dsb, q_st[half]))
                    dv_parts.append(_tn(probs.astype(BF16), dob))
                dq = _unstack_heads(dq_st[0], dq_st[1], lo_mask)
                for p in range(4):
                    dq_ref[at, p * BLOCK:(p + 1) * BLOCK] = dq[p]
                dks.append(dk_parts[0] + pltpu.roll(dk_parts[1], HEAD_DIM, 1))
                dvs.append(dv_parts[0] + pltpu.roll(dv_parts[1], HEAD_DIM, 1))
            last = slice((qb - 1) * BLOCK, qb * BLOCK)
            for parts, out_ref, carry in ((dks, dk_ref, k_carry), (dvs, dv_ref, v_carry)):
                out_ref[...] = carry[...]
                out_ref[last, :] += parts[0][:BLOCK, :]
                for b in range(qb):
                    own = parts[b][BLOCK:, :]
                    carry[b * BLOCK:(b + 1) * BLOCK, :] = own + parts[b + 1][:BLOCK, :] if b + 1 < qb else own

        @pl.when(n == n_groups)
        def _():
            dk_ref[...] = k_carry[...]
            dv_ref[...] = v_carry[...]
            for half, heads in enumerate((HEADS_A, HEADS_B)):
                for slot, h in enumerate(heads):
                    tot = jnp.sum(sink_acc[half, slot * BLOCK:(slot + 1) * BLOCK, :], axis=0, keepdims=True)
                    ds_ref[h:h + 1, :] = jnp.broadcast_to(-tot, (1, SMALL_LANES))

    return _call(
        body,
        (sinks, qn, kn, kn, v, v, a, da, tab),
        name="attn_bwd",
        grid=(n_groups + 1,),
        in_specs=[
            pl.BlockSpec(memory_space=pltpu.SMEM),
            pl.BlockSpec((rows, ATTN_WIDTH), group),
            pl.BlockSpec((rows, KV_WIDTH), group),
            pl.BlockSpec((BLOCK, KV_WIDTH), prev),
            pl.BlockSpec((rows, KV_WIDTH), group),
            pl.BlockSpec((BLOCK, KV_WIDTH), prev),
            pl.BlockSpec((rows, ATTN_WIDTH), group),
            pl.BlockSpec((rows, ATTN_WIDTH), group),
            _resident((2, 4 * BLOCK, 2 * BLOCK)),
        ],
        out_specs=[
            pl.BlockSpec((rows, ATTN_WIDTH), group),
            pl.BlockSpec((rows, KV_WIDTH), done),
            pl.BlockSpec((rows, KV_WIDTH), done),
            _acc((2, 4 * BLOCK, 2 * BLOCK)),
            _acc((N_DEV, SMALL_LANES)),
        ],
        out_shape=[
            jax.ShapeDtypeStruct((s, ATTN_WIDTH), F32),
            jax.ShapeDtypeStruct((s, KV_WIDTH), F32),
            jax.ShapeDtypeStruct((s, KV_WIDTH), F32),
            jax.ShapeDtypeStruct((2, 4 * BLOCK, 2 * BLOCK), F32),
            jax.ShapeDtypeStruct((N_DEV, SMALL_LANES), F32),
        ],
        scratch_shapes=[
            pltpu.VMEM((rows, KV_WIDTH), F32),
            pltpu.VMEM((rows, KV_WIDTH), F32),
            pltpu.VMEM((2, 4 * BLOCK, 1), F32),
        ],
    )


def _fold_heads(acc):
    t = acc + pltpu.roll(acc, HEAD_DIM, 1)
    out = t[:, :SMALL_LANES]
    for g in range(1, acc.shape[1] // SMALL_LANES):
        out = out + t[:, g * SMALL_LANES:(g + 1) * SMALL_LANES]
    return out


def _in_proj_bwd(dqn, dkn, dv, du, zqk, x, dh1, g_attn, gq_t, gk_t, w_in_t):
    s = x.shape[0]
    ts = min(TOKEN_TILE, s)
    nt = s // ts

    def head_norm_bwd(d_n, raw, g_t, bmat):
        r = lax.rsqrt(_seg_mean(raw * raw, bmat) + EPS)
        gy = d_n * g_t
        d_raw = r * gy - raw * (r * r * r) * _seg_mean(gy * raw, bmat)
        return d_raw, jnp.sum(d_n * (raw * r), axis=0, keepdims=True)

    def body(dqn_ref, dkn_ref, dv_ref, du_ref, zqk_ref, x_ref, dh1_ref, g_ref, gq_ref, gk_ref, w_ref, bq_ref, bk_ref,
             gx_ref, dw_ref, dg_ref, dgq_ref, dgk_ref, dz_ref, gq_acc, gk_acc):
        i = pl.program_id(0)

        @pl.when(i == 0)
        def _():
            dw_ref[...] = jnp.zeros_like(dw_ref)
            dg_ref[...] = jnp.zeros_like(dg_ref)
            gq_acc[...] = jnp.zeros_like(gq_acc)
            gk_acc[...] = jnp.zeros_like(gk_acc)

        d_q, d_gq = head_norm_bwd(dqn_ref[...], zqk_ref[:, :ATTN_WIDTH], gq_ref[...], bq_ref[...])
        d_k, d_gk = head_norm_bwd(dkn_ref[...], zqk_ref[:, ATTN_WIDTH:], gk_ref[...], bk_ref[...])
        gq_acc[...] += d_gq
        gk_acc[...] += d_gk
        dz_ref[:, :ATTN_WIDTH] = d_q.astype(BF16)
        dz_ref[:, ATTN_WIDTH:ATTN_WIDTH + KV_WIDTH] = d_k.astype(BF16)
        dz_ref[:, ATTN_WIDTH + KV_WIDTH:ATTN_WIDTH + 2 * KV_WIDTH] = dv_ref[...].astype(BF16)
        dz_ref[:, ATTN_WIDTH + 2 * KV_WIDTH:] = du_ref[...].astype(BF16)
        dz = dz_ref[...]
        xf = x_ref[...]
        r = _rms(xf)
        hn = ((xf * r) * g_ref[...]).astype(BF16)
        d_x, d_g = _rms_bwd(_nn(dz, w_ref[...]), xf, r, g_ref[...])
        dg_ref[...] += d_g
        gx_ref[...] = dh1_ref[...] + d_x
        dw_ref[...] += _tn(dz, hn)

        @pl.when(i == nt - 1)
        def _():
            dgq_ref[...] = _fold_heads(gq_acc[...])
            dgk_ref[...] = _fold_heads(gk_acc[...])

    return _call(
        body,
        (dqn, dkn, dv, du, zqk, x, dh1, g_attn, gq_t, gk_t, w_in_t,
      _head_mean_matrix(ATTN_WIDTH), _head_mean_matrix(KV_WIDTH)),
        name="in_proj_bwd",
        grid=(nt,),
        in_specs=[
            _rows(ts, ATTN_WIDTH),
            _rows(ts, KV_WIDTH),
            _rows(ts, KV_WIDTH),
            _rows(ts, POOL_WIDTH),
            _rows(ts, ATTN_WIDTH + KV_WIDTH),
            _rows(ts, D_MODEL),
            _rows(ts, D_MODEL),
            _resident((1, D_MODEL)),
            _resident((1, ATTN_WIDTH)),
            _resident((1, KV_WIDTH)),
            _resident((IN_WIDTH, D_MODEL)),
            _resident((ATTN_WIDTH, ATTN_WIDTH)),
            _resident((KV_WIDTH, KV_WIDTH)),
        ],
        out_specs=[
            _rows(ts, D_MODEL),
            _acc((IN_WIDTH, D_MODEL)),
            _acc((1, D_MODEL)),
            _acc((1, SMALL_LANES)),
            _acc((1, SMALL_LANES)),
        ],
        out_shape=[
            jax.ShapeDtypeStruct((s, D_MODEL), F32),
            jax.ShapeDtypeStruct((IN_WIDTH, D_MODEL), F32),
            jax.ShapeDtypeStruct((1, D_MODEL), F32),
            jax.ShapeDtypeStruct((1, SMALL_LANES), F32),
            jax.ShapeDtypeStruct((1, SMALL_LANES), F32),
        ],
        scratch_shapes=[
            pltpu.VMEM((ts, IN_WIDTH), BF16),
            pltpu.VMEM((1, ATTN_WIDTH), F32),
            pltpu.VMEM((1, KV_WIDTH), F32),
        ],
    )


BIG_WEIGHTS = (
    ("w_in", True, IN_WIDTH // N_DEV, D_MODEL),
    ("w_out", False, D_MODEL // N_DEV, D_MODEL),
    ("w_gate", True, D_FF // N_DEV, D_MODEL),
    ("w_up", True, D_FF // N_DEV, D_MODEL),
    ("w_down", False, D_FF // N_DEV, D_MODEL),
    ("w_ple_gate", False, D_MODEL // N_DEV, D_MODEL),
    ("w_ple_proj", False, PLE_DIM, D_MODEL // N_DEV),
)
N_BIG = len(BIG_WEIGHTS)


def _place():
    x, y, c = lax.axis_index("x"), lax.axis_index("y"), lax.axis_index("c")
    chips = [(1 - x, y), (x, 1 - y), (1 - x, 1 - y)]
    return x, y, c, chips


class _Gather:
    def __init__(self, n):
        self.n = n
        self.sems = [pltpu.SemaphoreType.DMA((n, 7)), pltpu.SemaphoreType.DMA((n, 7)), pltpu.SemaphoreType.DMA((n,))]

    def _ctx(self, srcs, outs, sems):
        send_sems, recv_sems, local_sems = sems
        x, y, c, chips = _place()
        me, sibling = (x, y, c), (x, y, 1 - c)

        def block(k, owner):
            px, py, pc = owner
            return outs[k].at[4 * px + 2 * py + pc]

        def copy(k, idx, owner, to, mine=False):
            return pltpu.make_async_remote_copy(
                src_ref=srcs[k] if mine else block(k, owner), dst_ref=block(k, owner),
                send_sem=send_sems.at[k, idx], recv_sem=recv_sems.at[k, idx], device_id=to, device_id_type=MESH)

        def local(k):
            return pltpu.make_async_copy(srcs[k], block(k, me), local_sems.at[k])

        return c, chips, me, sibling, copy, local

    def begin(self, srcs, outs, sems):
        c, chips, me, sibling, copy, local = self._ctx(srcs, outs, sems)
        for k in range(self.n):
            local(k).start()
            copy(k, 0, me, sibling, mine=True).start()
            for j, chip in enumerate(chips):
                copy(k, 1 + j, me, (*chip, c), mine=True).start()

    def middle(self, srcs, outs, sems):
        c, chips, me, sibling, copy, local = self._ctx(srcs, outs, sems)
        for j, chip in enumerate(chips):
            for k in range(self.n):
                copy(k, 1 + j, (*chip, c), me).wait_recv()
                copy(k, 4 + j, (*chip, c), sibling).start()

    def end(self, srcs, outs, sems):
        c, chips, me, sibling, copy, local = self._ctx(srcs, outs, sems)
        for k in range(self.n):
            copy(k, 0, sibling, me).wait_recv()
            for j, chip in enumerate(chips):
                copy(k, 4 + j, (*chip, 1 - c), me).wait_recv()
        for k in range(self.n):
            copy(k, 0, me, sibling, mine=True).wait_send()
            for j, chip in enumerate(chips):
                copy(k, 1 + j, me, (*chip, c), mine=True).wait_send()
                copy(k, 4 + j, (*chip, c), sibling).wait_send()
            local(k).wait()


def _gather_rider(shards):
    g = _Gather(len(shards))
    shapes = [jax.ShapeDtypeStruct((N_DEV, *sh.shape), sh.dtype) for sh in shards]
    return _Rider(shards, shapes, g.sems, g.begin, g.end, g.middle)


def _cast_and_gather_first(shards, rel_bias_t):
    g = _Gather(1)
    any_spec = pl.BlockSpec(memory_space=pl.ANY)
    vmem = pl.BlockSpec(memory_space=pltpu.VMEM)

    def body(*refs):
        ins, rb_ref, outs = refs[:N_BIG], refs[N_BIG], refs[N_BIG + 1:2 * N_BIG + 1]
        gathered, tab_ref, sems = refs[2 * N_BIG + 1], refs[2 * N_BIG + 2], refs[2 * N_BIG + 3:]
        outs[0][...] = ins[0][...].astype(BF16)
        g.begin(outs[:1], [gathered], sems)
        for k in range(1, N_BIG):
            outs[k][...] = ins[k][...].astype(BF16)
        _write_bias_table(rb_ref, tab_ref)
        g.middle(outs[:1], [gathered], sems)
        g.end(outs[:1], [gathered], sems)

    res = pl.pallas_call(
        body,
        name="cast_and_gather_first",
        in_specs=[vmem] * N_BIG + [pl.BlockSpec(memory_space=pltpu.SMEM)],
        out_specs=[vmem] * N_BIG + [any_spec, vmem],
        out_shape=[jax.ShapeDtypeStruct((r, c), BF16) for _, _, r, c in BIG_WEIGHTS]
        + [jax.ShapeDtypeStruct((N_DEV, *BIG_WEIGHTS[0][2:]), BF16), jax.ShapeDtypeStruct(BIAS_TABLE_SHAPE, F32)],
        scratch_shapes=g.sems,
    )(*shards, rel_bias_t)
    return list(res[:N_BIG]), res[N_BIG], res[N_BIG + 1]


def _sibling_rider(grads):
    n = len(grads)

    def copies(gs, lands, sems):
        send_sems, recv_sems = sems
        x, y, c, _ = _place()
        return [
            pltpu.make_async_remote_copy(
                src_ref=gs[k].at[:, 1 - c], dst_ref=lands[k], send_sem=send_sems.at[k], recv_sem=recv_sems.at[k],
                device_id=(x, y, 1 - c), device_id_type=MESH)
            for k in range(n)
        ]

    def begin(gs, lands, sems):
        for cp in copies(gs, lands, sems):
            cp.start()

    def end(gs, lands, sems):
        for cp in copies(gs, lands, sems):
            cp.wait()

    shapes = [jax.ShapeDtypeStruct((N_CHIPS, *g.shape[2:]), F32) for g in grads]
    return _Rider(grads, shapes, [pltpu.SemaphoreType.DMA((n,)), pltpu.SemaphoreType.DMA((n,))], begin, end)


def _chip_of_relation(j, place):
    x, y = place[0], place[1]
    return jnp.where(j == 0, 2 * (1 - x) + y, jnp.where(j == 1, 2 * x + 1 - y, 2 * (1 - x) + 1 - y))


def _chip_sum(ks, place, grads, from_sibling):
    n = len(ks)
    shapes = [BIG_WEIGHTS[k][2:] for k in ks]
    operands, specs = [], []
    for (r, c), g, l in zip(shapes, grads, from_sibling):
        operands += [g, l]
        specs += [pl.BlockSpec((1, 1, r, c), lambda j, place: (_chip_of_relation(j, place), place[2], 0, 0)),
                  pl.BlockSpec((1, r, c), lambda j, place: (_chip_of_relation(j, place), 0, 0))]
    args, in_specs, _ = _after_last(operands, specs)

    def body(place_ref, *refs):
        ins, outs = refs[:2 * n], refs[len(args):]
        for i in range(n):
            outs[i][0] = (ins[2 * i][0, 0] + ins[2 * i + 1][0]).astype(BF16)

    outs = pl.pallas_call(
        body,
        name="chip_sum_" + "_".join(BIG_WEIGHTS[k][0] for k in ks),
        grid_spec=pltpu.PrefetchScalarGridSpec(
            num_scalar_prefetch=1,
            grid=(N_CHIPS - 1,),
            in_specs=in_specs,
            out_specs=[pl.BlockSpec((1, r, c), lambda j, place: (j, 0, 0)) for r, c in shapes],
        ),
        out_shape=[jax.ShapeDtypeStruct((N_CHIPS - 1, r, c), BF16) for r, c in shapes],
    )(place, *args)
    _mark_issued(outs[0])
    return list(outs)


def _chips_rider(to_send, small=None):
    n = len(to_send)
    inputs = list(to_send) + ([] if small is None else [small])
    shapes = [jax.ShapeDtypeStruct((3, *t.shape[1:]), BF16) for t in to_send]
    sems = [pltpu.SemaphoreType.DMA((max(n, 1), 3)), pltpu.SemaphoreType.DMA((max(n, 1), 3))]
    if small is not None:
        shapes.append(jax.ShapeDtypeStruct((N_DEV, *small.shape), F32))
        sems += [pltpu.SemaphoreType.DMA((7,)), pltpu.SemaphoreType.DMA((7,)), pltpu.SemaphoreType.DMA]

    def copies(ins, outs, sem_refs):
        x, y, c, chips = _place()
        out = []
        for k in range(n):
            for j, (px, py) in enumerate(chips):
                out.append(pltpu.make_async_remote_copy(
                    src_ref=ins[k].at[j], dst_ref=outs[k].at[j],
                    send_sem=sem_refs[0].at[k, j], recv_sem=sem_refs[1].at[k, j],
                    device_id=(px, py, c), device_id_type=MESH))
        local = None
        if small is not None:
            me = 4 * x + 2 * y + c
            local = pltpu.make_async_copy(ins[n], outs[n].at[me], sem_refs[4])
            rel = 0
            for fx in (0, 1):
                for fy in (0, 1):
                    for fc in (0, 1):
                        if (fx, fy, fc) != (0, 0, 0):
                            out.append(pltpu.make_async_remote_copy(
                                src_ref=ins[n], dst_ref=outs[n].at[me],
                                send_sem=sem_refs[2].at[rel], recv_sem=sem_refs[3].at[rel],
                                device_id=(x ^ fx, y ^ fy, c ^ fc), device_id_type=MESH))
                            rel += 1
        return out, local

    def begin(ins, outs, sem_refs):
        remote, local = copies(ins, outs, sem_refs)
        if local is not None:
            local.start()
        for cp in remote:
            cp.start()

    def end(ins, outs, sem_refs):
        remote, local = copies(ins, outs, sem_refs)
        for cp in remote:
            cp.wait()
        if local is not None:
            local.wait()

    return _Rider(inputs, shapes, sems, begin, end)


PEER_SETS = {"sibling": 1, "chips": 2, "sibling+chips": 3, "all": 4}


def _peers(pattern):
    x, y, c, chips = _place()
    sibling, others = [(x, y, 1 - c)], [(*chip, c) for chip in chips]
    if pattern == "all":
        return sibling + others + [(*chip, 1 - c) for chip in chips]
    return {"sibling": sibling, "chips": others, "sibling+chips": sibling + others}[pattern]


def _on_sequencer(name, pattern, rider):
    n_in, n_out = len(rider.inputs), len(rider.out_shapes)

    def body(*refs):
        ins, outs, sems = refs[:n_in], refs[n_in:n_in + n_out], refs[n_in + n_out:]
        peers = _peers(pattern)
        barrier = pltpu.get_barrier_semaphore()
        for peer in peers:
            pl.semaphore_signal(barrier, inc=1, device_id=peer, device_id_type=MESH)
        pl.semaphore_wait(barrier, len(peers))
        rider.begin(ins, outs, sems)
        if rider.middle is not None:
            rider.middle(ins, outs, sems)
        rider.end(ins, outs, sems)

    outs = pl.kernel(
        body,
        name=name,
        out_type=tuple(rider.out_shapes),
        mesh=plsc.ScalarSubcoreMesh(axis_name="sequencer", num_cores=1),
        scratch_types=tuple(rider.sems),
        compiler_params=pltpu.CompilerParams(collective_id=PEER_SETS[pattern]),
    )(*rider.inputs)
    return list(outs)


def _adamw(w, g, m, v):
    m = ADAM_B1 * m + (1.0 - ADAM_B1) * g
    v = ADAM_B2 * v + (1.0 - ADAM_B2) * jnp.square(g)
    m_hat = m / (1.0 - ADAM_B1 ** ADAM_STEP)
    v_hat = v / (1.0 - ADAM_B2 ** ADAM_STEP)
    delta = -ADAM_LR * (m_hat / (jnp.sqrt(v_hat) + ADAM_EPS) + ADAM_WD * w)
    return delta, m, v


def _adamw_big(ks, place, operands):
    n = len(ks)
    tiles = lambda i, place: (i, 0)
    in_specs, out_specs, out_shape = [], [], []
    for k in ks:
        _, _, r, c = BIG_WEIGHTS[k]
        tile = r // 2
        in_specs += [
            pl.BlockSpec((1, 1, tile, c), lambda i, place: (2 * place[0] + place[1], place[2], i, 0)),
            pl.BlockSpec((1, tile, c), lambda i, place: (2 * place[0] + place[1], i, 0)),
            pl.BlockSpec((3, tile, c), lambda i, place: (0, i, 0)),
        ] + [pl.BlockSpec((tile, c), tiles)] * 3
        out_specs += [pl.BlockSpec((tile, c), tiles)] * 4
        out_shape += [jax.ShapeDtypeStruct((r, c), F32)] * 4
    args, in_specs, _ = _after_last(sum((list(ops) for ops in operands), []), in_specs)

    def body(place_ref, *refs):
        ins, outs = refs[:6 * n], refs[len(args):]
        for i in range(n):
            mine_ref, sib_ref, land_ref, w_ref, m_ref, v_ref = ins[6 * i:6 * i + 6]
            g_ref, d_ref, nm_ref, nv_ref = outs[4 * i:4 * i + 4]
            g = mine_ref[0, 0] + sib_ref[0]
            g = ((g + land_ref[0].astype(F32)) + land_ref[1].astype(F32)) + land_ref[2].astype(F32)
            g_ref[...] = g
            d_ref[...], nm_ref[...], nv_ref[...] = _adamw(w_ref[...], g, m_ref[...], v_ref[...])

    outs = pl.pallas_call(
        body,
        name="adamw_" + "_".join(BIG_WEIGHTS[k][0] for k in ks),
        grid_spec=pltpu.PrefetchScalarGridSpec(
            num_scalar_prefetch=1, grid=(2,), in_specs=in_specs, out_specs=out_specs),
        out_shape=out_shape,
    )(place, *args)
    _mark_issued(outs[0])
    return [outs[4 * i:4 * i + 4] for i in range(n)]


def _pack_small(arrays):
    rows, offsets = [], []
    at = 0
    for a in arrays:
        if a.ndim != 2 or a.shape[1] != SMALL_LANES or a.shape[0] % 8:
            flat = a.reshape(-1)
            n_rows = -(-flat.shape[0] // (8 * SMALL_LANES)) * 8
            a = jnp.pad(flat, (0, n_rows * SMALL_LANES - flat.shape[0])).reshape(n_rows, SMALL_LANES)
        rows.append(a)
        offsets.append(at)
        at += a.shape[0]
    return jnp.concatenate(rows, axis=0), offsets


def _unpack_small(tot, at, shape):
    r, c = shape
    if r % 8 == 0:
        return tot[at:at + r, :c]
    assert r == 1
    if c <= SMALL_LANES:
        return tot[at:at + 1, :c]
    return jnp.concatenate([tot[at + j:at + j + 1, :] for j in range(c // SMALL_LANES)], axis=1)


def _small_update(packs, loss_at, grads_at, ws, ms, vs):
    n, n_packs = len(ws), len(packs)

    def body(*refs):
        pack_refs, refs = refs[:n_packs], refs[n_packs:]
        w_refs, m_refs, v_refs, loss_ref, outs = refs[:n], refs[n:2 * n], refs[2 * n:3 * n], refs[3 * n], refs[3 * n + 1:]
        tots = []
        for p_ref in pack_refs:
            tot = p_ref[0]
            for j in range(1, N_DEV):
                tot = tot + p_ref[j]
            tots.append(tot)
        loss_ref[...] = _unpack_small(tots[loss_at[0]], loss_at[1], (1, 1))
        for i, (pack, at) in enumerate(grads_at):
            g = _unpack_small(tots[pack], at, w_refs[i].shape)
            outs[i][...] = g
            outs[n + i][...], outs[2 * n + i][...], outs[3 * n + i][...] = _adamw(
                w_refs[i][...], g, m_refs[i][...], v_refs[i][...])

    shapes = [jax.ShapeDtypeStruct(w.shape, F32) for w in ws]
    outs = pl.pallas_call(body, name="small_update", out_shape=[jax.ShapeDtypeStruct((1, 1), F32)] + shapes * 4)(
        *packs, *ws, *ms, *vs)
    return outs[0], outs[1:]


SMALL_NAMES = ("g_attn_norm", "g_q", "g_k", "attn_sinks", "rel_bias", "w_pool", "pool_scale", "g_ffn_norm", "g_ple_norm")


def kernel(x, p, w_in, w_out, g_attn_norm, g_q, g_k, attn_sinks, rel_bias, w_pool, pool_scale, g_ffn_norm, w_gate, w_up, w_down, g_ple_norm, w_ple_gate, w_ple_proj, loss_target, m_w_in, m_w_out, m_g_attn_norm, m_g_q, m_g_k, m_attn_sinks, m_rel_bias, m_w_pool, m_pool_scale, m_g_ffn_norm, m_w_gate, m_w_up, m_w_down, m_g_ple_norm, m_w_ple_gate, m_w_ple_proj, v_w_in, v_w_out, v_g_attn_norm, v_g_q, v_g_k, v_attn_sinks, v_rel_bias, v_w_pool, v_pool_scale, v_g_ffn_norm, v_w_gate, v_w_up, v_w_down, v_g_ple_norm, v_w_ple_gate, v_w_ple_proj):
    weights = dict(w_in=w_in, w_out=w_out, g_attn_norm=g_attn_norm, g_q=g_q, g_k=g_k, attn_sinks=attn_sinks,
                   rel_bias=rel_bias, w_pool=w_pool, pool_scale=pool_scale, g_ffn_norm=g_ffn_norm, w_gate=w_gate,
                   w_up=w_up, w_down=w_down, g_ple_norm=g_ple_norm, w_ple_gate=w_ple_gate, w_ple_proj=w_ple_proj)
    m_in = dict(w_in=m_w_in, w_out=m_w_out, g_attn_norm=m_g_attn_norm, g_q=m_g_q, g_k=m_g_k, attn_sinks=m_attn_sinks,
                rel_bias=m_rel_bias, w_pool=m_w_pool, pool_scale=m_pool_scale, g_ffn_norm=m_g_ffn_norm, w_gate=m_w_gate,
                w_up=m_w_up, w_down=m_w_down, g_ple_norm=m_g_ple_norm, w_ple_gate=m_w_ple_gate, w_ple_proj=m_w_ple_proj)
    v_in = dict(w_in=v_w_in, w_out=v_w_out, g_attn_norm=v_g_attn_norm, g_q=v_g_q, g_k=v_g_k, attn_sinks=v_attn_sinks,
                rel_bias=v_rel_bias, w_pool=v_w_pool, pool_scale=v_pool_scale, g_ffn_norm=v_g_ffn_norm, w_gate=v_w_gate,
                w_up=v_w_up, w_down=v_w_down, g_ple_norm=v_g_ple_norm, w_ple_gate=v_w_ple_gate, w_ple_proj=v_w_ple_proj)

    _issued.clear()
    xs = x[0]
    ps = p[0, 0]
    target = loss_target[0]
    wp = w_pool[0]
    gq_t = jnp.tile(g_q, (1, ATTN_WIDTH // HEAD_DIM))
    gk_t = jnp.tile(g_k, (1, KV_WIDTH // HEAD_DIM))

    def to_blocks(k, arr):
        return jnp.swapaxes(arr[0], 0, 1) if BIG_WEIGHTS[k][1] else arr[0]

    def from_blocks(k, arr):
        return (jnp.swapaxes(arr, 0, 1) if BIG_WEIGHTS[k][1] else arr)[None]

    IN, OUT, GATE, UP, DOWN, PG, PP = range(N_BIG)
    full = lambda g: g.reshape(N_DEV * g.shape[1], g.shape[2])
    halves = lambda k, g: g.reshape(N_CHIPS, 2, *BIG_WEIGHTS[k][2:])
    place = jnp.stack([lax.axis_index("x"), lax.axis_index("y"), lax.axis_index("c")]).astype(jnp.int32)

    sh, w_in_g, tab = _cast_and_gather_first(
        [to_blocks(k, weights[name]) for k, (name, _, _, _) in enumerate(BIG_WEIGHTS)], rel_bias.T)
    w_in_t = full(w_in_g)

    (w_out_g,) = _on_sequencer("gather_out", "sibling+chips", _gather_rider([sh[OUT]]))
    wg_g, wu_g = _on_sequencer("gather_gate_up", "sibling+chips", _gather_rider([sh[GATE], sh[UP]]))
    wd_g, w_pg_g, w_pp_g = _on_sequencer("gather_down_ple", "sibling+chips", _gather_rider([sh[DOWN], sh[PG], sh[PP]]))
    (zqk, qn, kn, v, u) = _in_proj(xs, g_attn_norm, w_in_t, gq_t, gk_t)
    (a,) = _attn_fwd(qn, kn, v, tab, attn_sinks)
    w_out_f = full(w_out_g)
    (h1, hn2, m_out) = _mix_out(u, a, xs, w_out_f, wp, pool_scale, g_ffn_norm)
    wg_t, wu_t = full(wg_g), full(wu_g)
    (gt, up) = _ffn_up(hn2, wg_t, wu_t)
    w_down_f = full(wd_g)

    partial, from_sibling, sums, landed = [None] * N_BIG, [None] * N_BIG, [None] * N_BIG, [None] * N_BIG

    def to_sibling(name, ks, grads):
        for k, g in zip(ks, grads):
            partial[k] = halves(k, g)
        got = _on_sequencer(name, "sibling", _sibling_rider([partial[k] for k in ks]))
        for k, g in zip(ks, got):
            from_sibling[k] = g

    def chip_sum(*ks):
        for k, s in zip(ks, _chip_sum(ks, place, [partial[k] for k in ks], [from_sibling[k] for k in ks])):
            sums[k] = s

    def to_chips(name, ks, small=None):
        got = _on_sequencer(name, "chips" if small is None else "all", _chips_rider([sums[k] for k in ks], small))
        for k, g in zip(ks, got):
            landed[k] = g
        return got[len(ks):]

    (loss_part, dh2, d_wpg, d_wpp, d_g_ple) = _ffn_down_ple(
        gt, up, h1, w_down_f, ps, target, g_ple_norm, full(w_pg_g), w_pp_g)
    to_sibling("sibling_ple", (PG, PP), (d_wpg, d_wpp))
    (dgt, dup, dh1, dh1b, d_g_ffn, d_wd) = _ffn_bwd_act(dh2, h1, gt, up, g_ffn_norm, wg_t, wu_t, w_down_f)
    to_sibling("sibling_down", (DOWN,), (d_wd,))
    chip_sum(PG, PP)
    to_chips("chips_ple", (PG, PP))
    (d_wg_t, d_wu_t) = _ffn_bwd_w(dgt, dup, hn2)
    to_sibling("sibling_gate_up", (GATE, UP), (d_wg_t, d_wu_t))
    _complete_before_next([landed[PG], landed[PP]])
    chip_sum(DOWN)
    to_chips("chips_down", (DOWN,))
    (da, du, d_wpool, d_scale, d_wo) = _mix_bwd(dh1b, u, a, m_out, w_out_f, wp, pool_scale)
    to_sibling("sibling_out", (OUT,), (d_wo,))
    chip_sum(GATE, UP)
    to_chips("chips_gate_up", (GATE, UP))
    _complete_before_next([landed[DOWN]])
    (dqn, dkn, dv, dl_acc, d_sinks) = _attn_bwd(qn, kn, v, a, da, tab, attn_sinks)
    chip_sum(OUT)
    early, early_at = _pack_small([d_wpool.reshape(POOL_WIDTH, POOL_GROUP), d_scale, d_g_ffn, d_g_ple, loss_part[:, :1]])
    (early_all,) = to_chips("chips_out", (OUT,), early)
    (grad_x, d_win_t, d_g_attn, d_gq, d_gk) = _in_proj_bwd(dqn, dkn, dv, du, zqk, xs, dh1, g_attn_norm, gq_t, gk_t, w_in_t)
    to_sibling("sibling_in", (IN,), (d_win_t,))
    _complete_before_next([landed[OUT], landed[GATE], landed[UP], early_all])
    (d_rel_t,) = _bias_table_bwd(dl_acc)
    chip_sum(IN)
    late, late_at = _pack_small([d_g_attn, d_gq[:, :HEAD_DIM], d_gk[:, :HEAD_DIM], d_sinks[:, 0], d_rel_t])
    (late_all,) = to_chips("chips_in", (IN,), late)

    out = {"grad": {}, "delta": {}, "new_m": {}, "new_v": {}}
    for ks in ((PG, PP, DOWN), (OUT, GATE, UP), (IN,)):
        names = [BIG_WEIGHTS[k][0] for k in ks]
        results = _adamw_big(ks, place, [
            (partial[k], from_sibling[k], landed[k], to_blocks(k, weights[n]), to_blocks(k, m_in[n]),
             to_blocks(k, v_in[n])) for k, n in zip(ks, names)])
        for k, name, res in zip(ks, names, results):
            for kind, r in zip(("grad", "delta", "new_m", "new_v"), res):
                out[kind][name] = from_blocks(k, r)
    def as_rows(name, arr):
        return arr.T if name == "rel_bias" else arr.reshape(POOL_WIDTH, POOL_GROUP) if name == "w_pool" else arr

    def from_rows(name, arr):
        return arr.T if name == "rel_bias" else arr.reshape(w_pool.shape) if name == "w_pool" else arr

    grads_at = dict(w_pool=(0, early_at[0]), pool_scale=(0, early_at[1]), g_ffn_norm=(0, early_at[2]),
                    g_ple_norm=(0, early_at[3]), g_attn_norm=(1, late_at[0]), g_q=(1, late_at[1]), g_k=(1, late_at[2]),
                    attn_sinks=(1, late_at[3]), rel_bias=(1, late_at[4]))
    loss, updates = _small_update(
        [early_all, late_all], (0, early_at[4]), [grads_at[n] for n in SMALL_NAMES],
        [as_rows(n, weights[n]) for n in SMALL_NAMES], [as_rows(n, m_in[n]) for n in SMALL_NAMES],
        [as_rows(n, v_in[n]) for n in SMALL_NAMES])
    loss = loss.reshape(())
    n_small = len(SMALL_NAMES)
    for j, kind in enumerate(("grad", "delta", "new_m", "new_v")):
        for i, name in enumerate(SMALL_NAMES):
            out[kind][name] = from_rows(name, updates[j * n_small + i])

    _issued.clear()
    order = ("w_in", "w_out", "g_attn_norm", "g_q", "g_k", "attn_sinks", "rel_bias", "w_pool", "pool_scale",
             "g_ffn_norm", "w_gate", "w_up", "w_down", "g_ple_norm", "w_ple_gate", "w_ple_proj")
    return (loss, grad_x[None], *[out["grad"][n] for n in order], *[out["delta"][n] for n in order],
            *[out["new_m"][n] for n in order], *[out["new_v"][n] for n in order])
```

```python
import math

import jax
import jax.numpy as jnp
import numpy as np
from jax import lax
from jax.experimental import pallas as pl
from jax.experimental.pallas import tpu as pltpu
from jax.experimental.pallas import tpu_sc as plsc

F32 = jnp.float32
BF16 = jnp.bfloat16
MESH = pl.DeviceIdType.MESH

D_MODEL = 1024
HEAD_DIM = 64
ATTN_WIDTH = 512
KV_WIDTH = 128
POOL_WIDTH = 512
POOL_SIZES = (2, 4, 8, 16)
POOL_GROUP = 128
POOL_HALO = 16
IN_WIDTH = 1280
D_FF = 2816
PLE_DIM = 256
BLOCK = 128
N_BUCKETS = 32
MAX_DISTANCE = 128
EPS = 1e-6
N_DEV = 8
N_CHIPS = 4

ADAM_LR = 0.001
ADAM_B1 = 0.9
ADAM_B2 = 0.999
ADAM_EPS = 1e-08
ADAM_WD = 0.01
ADAM_STEP = 10

TOKEN_TILE = 512
FFN_BWD_TILE = 256
FF_CHUNK = 256
FFN_W_SLAB = 256
ATTN_STEP_BLOCKS = 4
HEADS_A = (0, 2, 5, 7)
HEADS_B = (1, 3, 4, 6)
SMALL_LANES = 128


def _nn(a, b):
    return jnp.dot(a, b, preferred_element_type=F32)


def _nt(a, b):
    return lax.dot_general(a, b, (((1,), (1,)), ((), ())), preferred_element_type=F32)


def _tn(a, b):
    return lax.dot_general(a, b, (((0,), (0,)), ((), ())), preferred_element_type=F32)


def _resident(shape):
    nd = len(shape)
    return pl.BlockSpec(shape, lambda i, _nd=nd: (0,) * _nd, pipeline_mode=pl.Buffered(1))


def _rows(tile, width):
    return pl.BlockSpec((tile, width), lambda i: (i, 0))


def _acc(shape):
    nd = len(shape)
    return pl.BlockSpec(shape, lambda i, _nd=nd: (0,) * _nd)


def _head_mean_matrix(width):
    idx = np.arange(width) // HEAD_DIM
    return jnp.asarray((idx[:, None] == idx[None, :]).astype(np.float32) / HEAD_DIM, dtype=BF16)


def _seg_mean(v, bmat):
    hi = v.astype(BF16)
    lo = (v - hi.astype(F32)).astype(BF16)
    return _nn(hi, bmat) + _nn(lo, bmat)


def _rms(x):
    return lax.rsqrt(jnp.mean(x * x, axis=-1, keepdims=True) + EPS)


def _rms_bwd(d_y, x, r, g):
    gy = d_y * g
    d_x = r * gy - x * (r * r * r) * jnp.mean(gy * x, axis=-1, keepdims=True)
    d_g = jnp.sum(d_y * (x * r), axis=0, keepdims=True)
    return d_x, d_g


def _lane_lo(shape):
    return lax.broadcasted_iota(jnp.int32, shape, 1) < HEAD_DIM


class _Rider:
    def __init__(self, inputs, out_shapes, sems, begin, end, middle=None):
        self.inputs, self.out_shapes, self.sems = list(inputs), list(out_shapes), list(sems)
        self.begin, self.middle, self.end = begin, middle, end


_issued = []


def _after_last(args, in_specs):
    extra = list(_issued)
    return list(args) + extra, list(in_specs) + [pl.BlockSpec(memory_space=pl.ANY)] * len(extra), len(extra)


def _mark_issued(out):
    _issued[:] = [out]


def _complete_before_next(arrays):
    _issued.extend(arrays)


def _call(body, args, *, name, grid, in_specs, out_specs, out_shape, scratch_shapes=()):
    n_args = len(args)
    args, in_specs, _ = _after_last(args, in_specs)

    def ordered(*refs):
        body(*refs[:n_args], *refs[len(args):])

    outs = pl.pallas_call(ordered, name=name, grid=grid, in_specs=in_specs, out_specs=list(out_specs),
                          out_shape=list(out_shape), scratch_shapes=list(scratch_shapes))(*args)
    _mark_issued(outs[0])
    return list(outs)


def _in_proj(x, g_attn, w_in_t, gq_t, gk_t):
    s = x.shape[0]
    ts = min(TOKEN_TILE, s)

    def body(x_ref, g_ref, w_ref, gq_ref, gk_ref, bq_ref, bk_ref, zqk_ref, qn_ref, kn_ref, v_ref, u_ref):
        xf = x_ref[...]
        hn = ((xf * _rms(xf)) * g_ref[...]).astype(BF16)
        z = _nt(hn, w_ref[...])
        q = z[:, :ATTN_WIDTH]
        k = z[:, ATTN_WIDTH:ATTN_WIDTH + KV_WIDTH]
        zqk_ref[...] = z[:, :ATTN_WIDTH + KV_WIDTH]
        rq = lax.rsqrt(_seg_mean(q * q, bq_ref[...]) + EPS)
        qn_ref[...] = ((q * rq) * gq_ref[...]).astype(BF16)
        rk = lax.rsqrt(_seg_mean(k * k, bk_ref[...]) + EPS)
        kn_ref[...] = ((k * rk) * gk_ref[...]).astype(BF16)
        v_ref[...] = z[:, ATTN_WIDTH + KV_WIDTH:ATTN_WIDTH + 2 * KV_WIDTH].astype(BF16)
        u_ref[...] = z[:, ATTN_WIDTH + 2 * KV_WIDTH:]

    return _call(
        body,
        (x, g_attn, w_in_t, gq_t, gk_t, _head_mean_matrix(ATTN_WIDTH), _head_mean_matrix(KV_WIDTH)),
        name="in_proj",
        grid=(s // ts,),
        in_specs=[
            _rows(ts, D_MODEL),
            _resident((1, D_MODEL)),
            _resident((IN_WIDTH, D_MODEL)),
            _resident((1, ATTN_WIDTH)),
            _resident((1, KV_WIDTH)),
            _resident((ATTN_WIDTH, ATTN_WIDTH)),
            _resident((KV_WIDTH, KV_WIDTH)),
        ],
        out_specs=[
            _rows(ts, ATTN_WIDTH + KV_WIDTH),
            _rows(ts, ATTN_WIDTH),
            _rows(ts, KV_WIDTH),
            _rows(ts, KV_WIDTH),
            _rows(ts, POOL_WIDTH),
        ],
        out_shape=[
            jax.ShapeDtypeStruct((s, ATTN_WIDTH + KV_WIDTH), F32),
            jax.ShapeDtypeStruct((s, ATTN_WIDTH), BF16),
            jax.ShapeDtypeStruct((s, KV_WIDTH), BF16),
            jax.ShapeDtypeStruct((s, KV_WIDTH), BF16),
            jax.ShapeDtypeStruct((s, POOL_WIDTH), F32),
        ],
    )


def _bucket_ranges():
    n = np.arange(MAX_DISTANCE)
    max_exact = N_BUCKETS // 2
    nf = np.maximum(n, 1).astype(np.float64)
    large = max_exact + (np.log(nf / max_exact) / math.log(MAX_DISTANCE / max_exact) * (N_BUCKETS - max_exact)).astype(np.int64)
    bucket = np.where(n < max_exact, n, np.minimum(large, N_BUCKETS - 1))
    out = []
    for b in range(N_BUCKETS):
        idx = np.nonzero(bucket == b)[0]
        out.append((int(idx.min()), int(idx.max()) + 1))
    return out


def _band_distance():
    i = lax.broadcasted_iota(jnp.int32, (BLOCK, 2 * BLOCK), 0)
    j = lax.broadcasted_iota(jnp.int32, (BLOCK, 2 * BLOCK), 1)
    return BLOCK + i - j


BIAS_TABLE_SHAPE = (2, 4 * BLOCK, 2 * BLOCK)


def _write_bias_table(rb_ref, tab_ref):
    d = _band_distance()
    for half, heads in enumerate((HEADS_A, HEADS_B)):
        for slot, h in enumerate(heads):
            t = jnp.full((BLOCK, 2 * BLOCK), -jnp.inf, F32)
            for b, (lo, hi) in enumerate(_bucket_ranges()):
                t = jnp.where((d >= lo) & (d < hi), rb_ref[h, b], t)
            tab_ref[half, slot * BLOCK:(slot + 1) * BLOCK, :] = t


def _bias_table_bwd(dl_acc):
    ranges = _bucket_ranges()
    n_heads = len(HEADS_A) + len(HEADS_B)

    def body(dl_ref, out_ref):
        d = _band_distance()
        row = lax.broadcasted_iota(jnp.int32, (n_heads, SMALL_LANES), 0)
        lane = lax.broadcasted_iota(jnp.int32, (n_heads, SMALL_LANES), 1)
        out = jnp.zeros((n_heads, SMALL_LANES), F32)
        for b, (lo, hi) in enumerate(ranges):
            in_bucket = (d >= lo) & (d < hi)
            for half, heads in enumerate((HEADS_A, HEADS_B)):
                for slot, h in enumerate(heads):
                    g = dl_ref[half, slot * BLOCK:(slot + 1) * BLOCK, :]
                    part = jnp.sum(jnp.where(in_bucket, g, 0.0), axis=0, keepdims=True)
                    tot = jnp.sum(part, axis=1, keepdims=True)
                    out = jnp.where((row == h) & (lane == b), tot, out)
        out_ref[...] = out

    return _call(
        body,
        (dl_acc,),
        name="bias_table_bwd",
        grid=(1,),
        in_specs=[_acc((2, 4 * BLOCK, 2 * BLOCK))],
        out_specs=[_acc((n_heads, SMALL_LANES))],
        out_shape=[jax.ShapeDtypeStruct((n_heads, SMALL_LANES), F32)],
    )


def _stack_heads(pairs, lo_mask):
    zero = jnp.zeros_like(pairs[0])
    lo = [jnp.where(lo_mask, t, zero) for t in pairs]
    hi = [jnp.where(lo_mask, zero, t) for t in pairs]
    return (jnp.concatenate([lo[0], lo[1], hi[2], hi[3]], axis=0),
            jnp.concatenate([hi[0], hi[1], lo[2], lo[3]], axis=0))


def _unstack_heads(out_a, out_b, lo_mask):
    t = lambda x, r: x[r * BLOCK:(r + 1) * BLOCK, :]
    return [
        jnp.where(lo_mask, t(out_a, 0), t(out_b, 0)),
        jnp.where(lo_mask, t(out_a, 1), t(out_b, 1)),
        jnp.where(lo_mask, t(out_b, 2), t(out_a, 2)),
        jnp.where(lo_mask, t(out_b, 3), t(out_a, 3)),
    ]


def _sink_column(sink_ref, heads):
    row = lax.broadcasted_iota(jnp.int32, (4 * BLOCK, 1), 0)
    col = jnp.full((4 * BLOCK, 1), sink_ref[0, heads[3]], F32)
    for slot in (2, 1, 0):
        col = jnp.where(row < (slot + 1) * BLOCK, sink_ref[0, heads[slot]], col)
    return col


def _band_probs(q_stack, keys, tab, sink, first_block):
    s = _nt(q_stack, keys) * (HEAD_DIM ** -0.5) + tab
    if first_block is not None:
        col = lax.broadcasted_iota(jnp.int32, s.shape, 1)
        s = jnp.where(jnp.logical_and(first_block, col < BLOCK), -jnp.inf, s)
    m = jnp.maximum(jnp.max(s, axis=-1, keepdims=True), sink)
    e = jnp.exp(s - m)
    e_sink = jnp.exp(sink - m)
    den = jnp.sum(e, axis=-1, keepdims=True) + e_sink
    return e / den, e_sink / den


def _attn_specs(n_groups):
    group = lambda n: (jnp.minimum(n, n_groups - 1), 0)
    prev = lambda n: (jnp.maximum(jnp.minimum(n, n_groups - 1) * ATTN_STEP_BLOCKS - 1, 0), 0)
    return group, prev


def _band(prev_ref, group_ref, b):
    rows = lambda i: group_ref[i * BLOCK:(i + 1) * BLOCK, :]
    band = jnp.concatenate([prev_ref[...] if b == 0 else rows(b - 1), rows(b)], axis=0)
    return band, pltpu.roll(band, HEAD_DIM, 1)


def _attn_fwd(qn, kn, v, tab, sinks):
    s = qn.shape[0]
    n_groups = s // (ATTN_STEP_BLOCKS * BLOCK)
    group, prev = _attn_specs(n_groups)
    rows = ATTN_STEP_BLOCKS * BLOCK

    def body(sink_ref, q_ref, kc_ref, kp_ref, vc_ref, vp_ref, tab_ref, o_ref):
        first = pl.program_id(0) == 0
        lo_mask = _lane_lo((BLOCK, BLOCK))
        for b in range(ATTN_STEP_BLOCKS):
            at = slice(b * BLOCK, (b + 1) * BLOCK)
            kk, kk_sw = _band(kp_ref, kc_ref, b)
            vv, vv_sw = _band(vp_ref, vc_ref, b)
            q_a, q_b = _stack_heads([q_ref[at, p * BLOCK:(p + 1) * BLOCK] for p in range(4)], lo_mask)
            no_prev = first if b == 0 else None
            p_a, _ = _band_probs(q_a, kk, tab_ref[0], _sink_column(sink_ref, HEADS_A), no_prev)
            p_b, _ = _band_probs(q_b, kk_sw, tab_ref[1], _sink_column(sink_ref, HEADS_B), no_prev)
            out = _unstack_heads(_nn(p_a.astype(BF16), vv), _nn(p_b.astype(BF16), vv_sw), lo_mask)
            for p in range(4):
                o_ref[at, p * BLOCK:(p + 1) * BLOCK] = out[p].astype(BF16)

    return _call(
        body,
        (sinks, qn, kn, kn, v, v, tab),
        name="attn_fwd",
        grid=(n_groups,),
        in_specs=[
            pl.BlockSpec(memory_space=pltpu.SMEM),
            pl.BlockSpec((rows, ATTN_WIDTH), group),
            pl.BlockSpec((rows, KV_WIDTH), group),
            pl.BlockSpec((BLOCK, KV_WIDTH), prev),
            pl.BlockSpec((rows, KV_WIDTH), group),
            pl.BlockSpec((BLOCK, KV_WIDTH), prev),
            _resident((2, 4 * BLOCK, 2 * BLOCK)),
        ],
        out_specs=[pl.BlockSpec((rows, ATTN_WIDTH), group)],
        out_shape=[jax.ShapeDtypeStruct((s, ATTN_WIDTH), BF16)],
    )


def _pooled(u_tile, u_halo, tile_index, tile_rows):
    halo = jnp.where(tile_index > 0, u_halo, 0.0)
    ext = jnp.concatenate([halo, u_tile], axis=0)
    sums = []
    acc = ext
    for shift in (1, 2, 4, 8):
        acc = acc + pltpu.roll(acc, shift, 0)
        sums.append(acc)
    t = tile_index * tile_rows + lax.broadcasted_iota(jnp.int32, (tile_rows, 1), 0)
    out = []
    for g, w in enumerate(POOL_SIZES):
        lanes = slice(g * POOL_GROUP, (g + 1) * POOL_GROUP)
        cnt = jnp.minimum(t + 1, w).astype(F32)
        out.append(sums[g][POOL_HALO:, lanes] / cnt - u_tile[:, lanes])
    return out


def _halo_before(tile):
    return lambda i: (jnp.maximum(i * (tile // POOL_HALO) - 1, 0), 0)


def _mix_out(u, a, x, w_out, w_pool, pool_scale, g_ffn):
    s = x.shape[0]
    ts = min(TOKEN_TILE, s)

    def body(u_ref, uh_ref, a_ref, x_ref, wo_ref, wp_ref, sc_ref, g_ref, h1_ref, hn_ref, m_ref):
        i = pl.program_id(0)
        pooled = _pooled(u_ref[...], uh_ref[...], i, ts)
        for g in range(len(POOL_SIZES)):
            lanes = slice(g * POOL_GROUP, (g + 1) * POOL_GROUP)
            y = _nn(pooled[g].astype(BF16), wp_ref[g].astype(BF16))
            m_ref[:, lanes] = (y * sc_ref[:, lanes]).astype(BF16)
        h1 = x_ref[...] + _nn(a_ref[...], wo_ref[:ATTN_WIDTH, :]) + _nn(m_ref[...], wo_ref[ATTN_WIDTH:, :])
        h1_ref[...] = h1
        hn_ref[...] = ((h1 * _rms(h1)) * g_ref[...]).astype(BF16)

    return _call(
        body,
        (u, u, a, x, w_out, w_pool, pool_scale, g_ffn),
        name="mix_out",
        grid=(s // ts,),
        in_specs=[
            _rows(ts, POOL_WIDTH),
            pl.BlockSpec((POOL_HALO, POOL_WIDTH), _halo_before(ts)),
            _rows(ts, ATTN_WIDTH),
            _rows(ts, D_MODEL),
            _resident((D_MODEL, D_MODEL)),
            _resident((len(POOL_SIZES), POOL_GROUP, POOL_GROUP)),
            _resident((1, POOL_WIDTH)),
            _resident((1, D_MODEL)),
        ],
        out_specs=[_rows(ts, D_MODEL), _rows(ts, D_MODEL), _rows(ts, POOL_WIDTH)],
        out_shape=[
            jax.ShapeDtypeStruct((s, D_MODEL), F32),
            jax.ShapeDtypeStruct((s, D_MODEL), BF16),
            jax.ShapeDtypeStruct((s, POOL_WIDTH), BF16),
        ],
    )


def _ffn_up(hn2, wg_t, wu_t):
    s = hn2.shape[0]
    ts = min(TOKEN_TILE, s)

    def body(hn_ref, wg_ref, wu_ref, gt_ref, up_ref):
        hn = hn_ref[...]
        for c in range(D_FF // FF_CHUNK):
            cols = slice(c * FF_CHUNK, (c + 1) * FF_CHUNK)
            gt_ref[:, cols] = _nt(hn, wg_ref[cols, :]).astype(BF16)
            up_ref[:, cols] = _nt(hn, wu_ref[cols, :]).astype(BF16)

    return _call(
        body,
        (hn2, wg_t, wu_t),
        name="ffn_up",
        grid=(s // ts,),
        in_specs=[_rows(ts, D_MODEL), _resident((D_FF, D_MODEL)), _resident((D_FF, D_MODEL))],
        out_specs=[_rows(ts, D_FF), _rows(ts, D_FF)],
        out_shape=[jax.ShapeDtypeStruct((s, D_FF), BF16), jax.ShapeDtypeStruct((s, D_FF), BF16)],
    )


def _silu_mul(gt, up):
    return (gt * jax.nn.sigmoid(gt)) * up


def _ffn_down_ple(gt, up, h1, w_down, p, target, g_ple, w_pg, w_pp):
    s = h1.shape[0]
    ts = min(TOKEN_TILE, s)
    blk = D_MODEL // N_DEV

    def body(gt_ref, up_ref, h1_ref, wd_ref, p_ref, t_ref, g_ref, wpg_ref, wpp_ref,
             loss_ref, dh_ref, dwpg_ref, dwpp_ref, dg_ref):
        @pl.when(pl.program_id(0) == 0)
        def _():
            loss_ref[...] = jnp.zeros_like(loss_ref)
            dwpg_ref[...] = jnp.zeros_like(dwpg_ref)
            dwpp_ref[...] = jnp.zeros_like(dwpp_ref)
            dg_ref[...] = jnp.zeros_like(dg_ref)

        h2v = h1_ref[...]
        for c in range(D_FF // FF_CHUNK):
            cols = slice(c * FF_CHUNK, (c + 1) * FF_CHUNK)
            act = _silu_mul(gt_ref[:, cols].astype(F32), up_ref[:, cols].astype(F32)).astype(BF16)
            h2v = _nn(act, wd_ref[cols, :]) + h2v
        r = _rms(h2v)
        hn = ((h2v * r) * g_ref[...]).astype(BF16)
        gate = jax.nn.sigmoid(_nn(hn, wpg_ref[...]))
        pb = p_ref[...].astype(BF16)
        pp = _nn(pb, jnp.concatenate([wpp_ref[j] for j in range(N_DEV)], axis=1))
        diff = (h2v + gate * pp) - t_ref[...]
        loss_ref[...] += jnp.sum(jnp.sum(diff * diff, axis=0, keepdims=True), axis=1, keepdims=True) * (0.5 / D_MODEL)
        dy = diff * (1.0 / D_MODEL)
        d_pp = (dy * gate).astype(BF16)
        d_pre = ((dy * pp) * (gate * (1.0 - gate))).astype(BF16)
        d_x, d_g = _rms_bwd(_nt(d_pre, wpg_ref[...]), h2v, r, g_ref[...])
        dg_ref[...] += d_g
        dh_ref[...] = dy + d_x
        d_wpp = _tn(pb, d_pp)
        for j in range(N_DEV):
            dwpp_ref[j] += d_wpp[:, j * blk:(j + 1) * blk]
        dwpg_ref[...] += _tn(hn, d_pre)

    return _call(
        body,
        (gt, up, h1, w_down, p, target, g_ple, w_pg, w_pp),
        name="ffn_down_ple",
        grid=(s // ts,),
        in_specs=[
            _rows(ts, D_FF),
            _rows(ts, D_FF),
            _rows(ts, D_MODEL),
            _resident((D_FF, D_MODEL)),
            _rows(ts, PLE_DIM),
            _rows(ts, D_MODEL),
            _resident((1, D_MODEL)),
            _resident((D_MODEL, D_MODEL)),
            _resident((N_DEV, PLE_DIM, blk)),
        ],
        out_specs=[
            _acc((1, SMALL_LANES)),
            _rows(ts, D_MODEL),
            _acc((D_MODEL, D_MODEL)),
            _acc((N_DEV, PLE_DIM, blk)),
            _acc((1, D_MODEL)),
        ],
        out_shape=[
            jax.ShapeDtypeStruct((1, SMALL_LANES), F32),
            jax.ShapeDtypeStruct((s, D_MODEL), F32),
            jax.ShapeDtypeStruct((D_MODEL, D_MODEL), F32),
            jax.ShapeDtypeStruct((N_DEV, PLE_DIM, blk), F32),
            jax.ShapeDtypeStruct((1, D_MODEL), F32),
        ],
    )


def _ffn_bwd_act(dh2, h1, gt, up, g_ffn, wg_t, wu_t, w_down):
    s = h1.shape[0]
    ts = min(FFN_BWD_TILE, s)

    def body(dh_ref, h1_ref, gt_ref, up_ref, g_ref, wg_ref, wu_ref, wd_ref,
             dgt_ref, dup_ref, dh1_ref, dh1b_ref, dg_ref, dwd_ref, act_ref):
        @pl.when(pl.program_id(0) == 0)
        def _():
            dg_ref[...] = jnp.zeros_like(dg_ref)
            dwd_ref[...] = jnp.zeros_like(dwd_ref)

        dhb = dh_ref[...].astype(BF16)
        d_hn = jnp.zeros((ts, D_MODEL), F32)
        for c in range(D_FF // FF_CHUNK):
            cols = slice(c * FF_CHUNK, (c + 1) * FF_CHUNK)
            d_act = _nt(dhb, wd_ref[cols, :])
            gtv = gt_ref[:, cols].astype(F32)
            upv = up_ref[:, cols].astype(F32)
            sg = jax.nn.sigmoid(gtv)
            silu = gtv * sg
            act_ref[:, cols] = (silu * upv).astype(BF16)
            d_up = (d_act * silu).astype(BF16)
            d_gt = ((d_act * upv) * (sg * (1.0 + gtv * (1.0 - sg)))).astype(BF16)
            dup_ref[:, cols] = d_up
            dgt_ref[:, cols] = d_gt
            d_hn = (_nn(d_gt, wg_ref[cols, :]) + _nn(d_up, wu_ref[cols, :])) + d_hn
        dwd_ref[...] += _tn(act_ref[...], dhb)
        h1v = h1_ref[...]
        d_x, d_g = _rms_bwd(d_hn, h1v, _rms(h1v), g_ref[...])
        dg_ref[...] += d_g
        dh1 = dh_ref[...] + d_x
        dh1_ref[...] = dh1
        dh1b_ref[...] = dh1.astype(BF16)

    return _call(
        body,
        (dh2, h1, gt, up, g_ffn, wg_t, wu_t, w_down),
        name="ffn_bwd_act",
        grid=(s // ts,),
        in_specs=[
            _rows(ts, D_MODEL),
            _rows(ts, D_MODEL),
            _rows(ts, D_FF),
            _rows(ts, D_FF),
            _resident((1, D_MODEL)),
            _resident((D_FF, D_MODEL)),
            _resident((D_FF, D_MODEL)),
            _resident((D_FF, D_MODEL)),
        ],
        out_specs=[
            _rows(ts, D_FF), _rows(ts, D_FF),
            _rows(ts, D_MODEL), _rows(ts, D_MODEL), _acc((1, D_MODEL)), _acc((D_FF, D_MODEL)),
        ],
        out_shape=[
            jax.ShapeDtypeStruct((s, D_FF), BF16),
            jax.ShapeDtypeStruct((s, D_FF), BF16),
            jax.ShapeDtypeStruct((s, D_MODEL), F32),
            jax.ShapeDtypeStruct((s, D_MODEL), BF16),
            jax.ShapeDtypeStruct((1, D_MODEL), F32),
            jax.ShapeDtypeStruct((D_FF, D_MODEL), F32),
        ],
        scratch_shapes=[pltpu.VMEM((ts, D_FF), BF16)],
    )


def _ffn_bwd_w(dgt, dup, hn2):
    s = hn2.shape[0]
    slab = pl.BlockSpec((s, FFN_W_SLAB), lambda i: (0, i))

    def body(dgt_ref, dup_ref, hn_ref, dwg_ref, dwu_ref):
        hn = hn_ref[...]
        dwg_ref[...] = _tn(dgt_ref[...], hn)
        dwu_ref[...] = _tn(dup_ref[...], hn)

    return _call(
        body,
        (dgt, dup, hn2),
        name="ffn_bwd_w",
        grid=(D_FF // FFN_W_SLAB,),
        in_specs=[slab, slab, _resident((s, D_MODEL))],
        out_specs=[_rows(FFN_W_SLAB, D_MODEL)] * 2,
        out_shape=[jax.ShapeDtypeStruct((D_FF, D_MODEL), F32)] * 2,
    )


def _mix_bwd(dh1b, u, a, m, w_out, w_pool, pool_scale):
    s = u.shape[0]
    ts = min(TOKEN_TILE, s)
    nt = s // ts
    halo_after = lambda i: (jnp.minimum((i + 1) * (ts // POOL_HALO), s // POOL_HALO - 1), 0)
    n_groups = len(POOL_SIZES)

    def body(dh_ref, dhn_ref, u_ref, uh_ref, a_ref, m_ref, wo_ref, wp_ref, sc_ref,
             da_ref, du_ref, dwp_ref, dsc_ref, dwo_ref):
        i = pl.program_id(0)

        @pl.when(i == 0)
        def _():
            dwp_ref[...] = jnp.zeros_like(dwp_ref)
            dsc_ref[...] = jnp.zeros_like(dsc_ref)
            dwo_ref[...] = jnp.zeros_like(dwo_ref)

        dh = dh_ref[...]
        dwo_ref[:ATTN_WIDTH, :] += _tn(a_ref[...], dh)
        dwo_ref[ATTN_WIDTH:, :] += _tn(m_ref[...], dh)
        da_ref[...] = _nt(dh, wo_ref[:ATTN_WIDTH, :])
        dh_next = jnp.where(i < nt - 1, dhn_ref[...], jnp.zeros_like(dhn_ref))
        dm_ext = _nt(jnp.concatenate([dh, dh_next], axis=0), wo_ref[ATTN_WIDTH:, :])
        pooled = _pooled(u_ref[...], uh_ref[...], i, ts)
        t_ext = i * ts + lax.broadcasted_iota(jnp.int32, (ts + POOL_HALO, 1), 0)
        for g, w in enumerate(POOL_SIZES):
            lanes = slice(g * POOL_GROUP, (g + 1) * POOL_GROUP)
            wp = wp_ref[g].astype(BF16)
            pg = pooled[g].astype(BF16)
            dm_g = dm_ext[:, lanes]
            dsc_ref[:, lanes] += jnp.sum(dm_g[:ts, :] * _nn(pg, wp), axis=0, keepdims=True)
            dy = (dm_g * sc_ref[:, lanes]).astype(BF16)
            dwp_ref[g] += _tn(pg, dy[:ts, :])
            d_pool = _nt(dy, wp)
            acc = d_pool / jnp.minimum(t_ext + 1, w).astype(F32)
            shift = 1
            while shift < w:
                acc = acc + pltpu.roll(acc, ts + POOL_HALO - shift, 0)
                shift *= 2
            du_ref[:, lanes] = (acc[:ts, :] - d_pool[:ts, :]).astype(BF16)

    return _call(
        body,
        (dh1b, dh1b, u, u, a, m, w_out, w_pool, pool_scale),
        name="mix_bwd",
        grid=(nt,),
        in_specs=[
            _rows(ts, D_MODEL),
            pl.BlockSpec((POOL_HALO, D_MODEL), halo_after),
            _rows(ts, POOL_WIDTH),
            pl.BlockSpec((POOL_HALO, POOL_WIDTH), _halo_before(ts)),
            _rows(ts, ATTN_WIDTH),
            _rows(ts, POOL_WIDTH),
            _resident((D_MODEL, D_MODEL)),
            _resident((n_groups, POOL_GROUP, POOL_GROUP)),
            _resident((1, POOL_WIDTH)),
        ],
        out_specs=[
            _rows(ts, ATTN_WIDTH),
            _rows(ts, POOL_WIDTH),
            _acc((n_groups, POOL_GROUP, POOL_GROUP)),
            _acc((1, POOL_WIDTH)),
            _acc((D_MODEL, D_MODEL)),
        ],
        out_shape=[
            jax.ShapeDtypeStruct((s, ATTN_WIDTH), F32),
            jax.ShapeDtypeStruct((s, POOL_WIDTH), BF16),
            jax.ShapeDtypeStruct((n_groups, POOL_GROUP, POOL_GROUP), F32),
            jax.ShapeDtypeStruct((1, POOL_WIDTH), F32),
            jax.ShapeDtypeStruct((D_MODEL, D_MODEL), F32),
        ],
    )


def _attn_bwd(qn, kn, v, a, da, tab, sinks):
    s = qn.shape[0]
    qb = ATTN_STEP_BLOCKS
    rows = qb * BLOCK
    n_groups = s // rows
    group, prev = _attn_specs(n_groups)
    done = lambda n: (jnp.maximum(n - 1, 0), 0)

    def body(sink_ref, q_ref, kc_ref, kp_ref, vc_ref, vp_ref, o_ref, do_ref, tab_ref,
             dq_ref, dk_ref, dv_ref, dl_ref, ds_ref, k_carry, v_carry, sink_acc):
        n = pl.program_id(0)

        @pl.when(n == 0)
        def _():
            dl_ref[...] = jnp.zeros_like(dl_ref)
            k_carry[...] = jnp.zeros_like(k_carry)
            v_carry[...] = jnp.zeros_like(v_carry)
            sink_acc[...] = jnp.zeros_like(sink_acc)

        @pl.when(n < n_groups)
        def _():
            first = n == 0
            lo_mask = _lane_lo((BLOCK, BLOCK))
            dks, dvs = [], []
            for b in range(qb):
                at = slice(b * BLOCK, (b + 1) * BLOCK)
                keys = _band(kp_ref, kc_ref, b)
                vals = _band(vp_ref, vc_ref, b)
                q_st = _stack_heads([q_ref[at, p * BLOCK:(p + 1) * BLOCK] for p in range(4)], lo_mask)
                do_st = _stack_heads([do_ref[at, p * BLOCK:(p + 1) * BLOCK] for p in range(4)], lo_mask)
                o_st = _stack_heads([o_ref[at, p * BLOCK:(p + 1) * BLOCK].astype(F32) for p in range(4)], lo_mask)
                dq_st, dk_parts, dv_parts = [], [], []
                for half, heads in enumerate((HEADS_A, HEADS_B)):
                    probs, p_sink = _band_probs(q_st[half], keys[half], tab_ref[half], _sink_column(sink_ref, heads),
                                                first if b == 0 else None)
                    delta = jnp.sum(do_st[half] * o_st[half], axis=-1, keepdims=True)
                    dob = do_st[half].astype(BF16)
                    dl = probs * (_nt(dob, vals[half]) - delta)
                    dl_ref[half] += dl
                    sink_acc[half] += p_sink * delta
                    dsb = (dl * (HEAD_DIM ** -0.5)).astype(BF16)
                    dq_st.append(_nn(dsb, keys[half]))
                    dk_parts.append(_tn(dsb, q_st[half]))
                    dv_parts.append(_tn(probs.astype(BF16), dob))
                dq = _unstack_heads(dq_st[0], dq_st[1], lo_mask)
                for p in range(4):
                    dq_ref[at, p * BLOCK:(p + 1) * BLOCK] = dq[p]
                dks.append(dk_parts[0] + pltpu.roll(dk_parts[1], HEAD_DIM, 1))
                dvs.append(dv_parts[0] + pltpu.roll(dv_parts[1], HEAD_DIM, 1))
            last = slice((qb - 1) * BLOCK, qb * BLOCK)
            for parts, out_ref, carry in ((dks, dk_ref, k_carry), (dvs, dv_ref, v_carry)):
                out_ref[...] = carry[...]
                out_ref[last, :] += parts[0][:BLOCK, :]
                for b in range(qb):
                    own = parts[b][BLOCK:, :]
                    carry[b * BLOCK:(b + 1) * BLOCK, :] = own + parts[b + 1][:BLOCK, :] if b + 1 < qb else own

        @pl.when(n == n_groups)
        def _():
            dk_ref[...] = k_carry[...]
            dv_ref[...] = v_carry[...]
            for half, heads in enumerate((HEADS_A, HEADS_B)):
                for slot, h in enumerate(heads):
                    tot = jnp.sum(sink_acc[half, slot * BLOCK:(slot + 1) * BLOCK, :], axis=0, keepdims=True)
                    ds_ref[h:h + 1, :] = jnp.broadcast_to(-tot, (1, SMALL_LANES))

    return _call(
        body,
        (sinks, qn, kn, kn, v, v, a, da, tab),
        name="attn_bwd",
        grid=(n_groups + 1,),
        in_specs=[
            pl.BlockSpec(memory_space=pltpu.SMEM),
            pl.BlockSpec((rows, ATTN_WIDTH), group),
            pl.BlockSpec((rows, KV_WIDTH), group),
            pl.BlockSpec((BLOCK, KV_WIDTH), prev),
            pl.BlockSpec((rows, KV_WIDTH), group),
            pl.BlockSpec((BLOCK, KV_WIDTH), prev),
            pl.BlockSpec((rows, ATTN_WIDTH), group),
            pl.BlockSpec((rows, ATTN_WIDTH), group),
            _resident((2, 4 * BLOCK, 2 * BLOCK)),
        ],
        out_specs=[
            pl.BlockSpec((rows, ATTN_WIDTH), group),
            pl.BlockSpec((rows, KV_WIDTH), done),
            pl.BlockSpec((rows, KV_WIDTH), done),
            _acc((2, 4 * BLOCK, 2 * BLOCK)),
            _acc((N_DEV, SMALL_LANES)),
        ],
        out_shape=[
            jax.ShapeDtypeStruct((s, ATTN_WIDTH), F32),
            jax.ShapeDtypeStruct((s, KV_WIDTH), F32),
            jax.ShapeDtypeStruct((s, KV_WIDTH), F32),
            jax.ShapeDtypeStruct((2, 4 * BLOCK, 2 * BLOCK), F32),
            jax.ShapeDtypeStruct((N_DEV, SMALL_LANES), F32),
        ],
        scratch_shapes=[
            pltpu.VMEM((rows, KV_WIDTH), F32),
            pltpu.VMEM((rows, KV_WIDTH), F32),
            pltpu.VMEM((2, 4 * BLOCK, 1), F32),
        ],
    )


def _fold_heads(acc):
    t = acc + pltpu.roll(acc, HEAD_DIM, 1)
    out = t[:, :SMALL_LANES]
    for g in range(1, acc.shape[1] // SMALL_LANES):
        out = out + t[:, g * SMALL_LANES:(g + 1) * SMALL_LANES]
    return out


def _in_proj_bwd(dqn, dkn, dv, du, zqk, x, dh1, g_attn, gq_t, gk_t, w_in_t):
    s = x.shape[0]
    ts = min(TOKEN_TILE, s)
    nt = s // ts

    def head_norm_bwd(d_n, raw, g_t, bmat):
        r = lax.rsqrt(_seg_mean(raw * raw, bmat) + EPS)
        gy = d_n * g_t
        d_raw = r * gy - raw * (r * r * r) * _seg_mean(gy * raw, bmat)
        return d_raw, jnp.sum(d_n * (raw * r), axis=0, keepdims=True)

    def body(dqn_ref, dkn_ref, dv_ref, du_ref, zqk_ref, x_ref, dh1_ref, g_ref, gq_ref, gk_ref, w_ref, bq_ref, bk_ref,
             gx_ref, dw_ref, dg_ref, dgq_ref, dgk_ref, dz_ref, gq_acc, gk_acc):
        i = pl.program_id(0)

        @pl.when(i == 0)
        def _():
            dw_ref[...] = jnp.zeros_like(dw_ref)
            dg_ref[...] = jnp.zeros_like(dg_ref)
            gq_acc[...] = jnp.zeros_like(gq_acc)
            gk_acc[...] = jnp.zeros_like(gk_acc)

        d_q, d_gq = head_norm_bwd(dqn_ref[...], zqk_ref[:, :ATTN_WIDTH], gq_ref[...], bq_ref[...])
        d_k, d_gk = head_norm_bwd(dkn_ref[...], zqk_ref[:, ATTN_WIDTH:], gk_ref[...], bk_ref[...])
        gq_acc[...] += d_gq
        gk_acc[...] += d_gk
        dz_ref[:, :ATTN_WIDTH] = d_q.astype(BF16)
        dz_ref[:, ATTN_WIDTH:ATTN_WIDTH + KV_WIDTH] = d_k.astype(BF16)
        dz_ref[:, ATTN_WIDTH + KV_WIDTH:ATTN_WIDTH + 2 * KV_WIDTH] = dv_ref[...].astype(BF16)
        dz_ref[:, ATTN_WIDTH + 2 * KV_WIDTH:] = du_ref[...]
        dz = dz_ref[...]
        xf = x_ref[...]
        r = _rms(xf)
        hn = ((xf * r) * g_ref[...]).astype(BF16)
        d_x, d_g = _rms_bwd(_nn(dz, w_ref[...]), xf, r, g_ref[...])
        dg_ref[...] += d_g
        gx_ref[...] = dh1_ref[...] + d_x
        dw_ref[...] += _tn(dz, hn)

        @pl.when(i == nt - 1)
        def _():
            dgq_ref[...] = _fold_heads(gq_acc[...])
            dgk_ref[...] = _fold_heads(gk_acc[...])

    return _call(
        body,
        (dqn, dkn, dv, du, zqk, x, dh1, g_attn, gq_t, gk_t, w_in_t,
      _head_mean_matrix(ATTN_WIDTH), _head_mean_matrix(KV_WIDTH)),
        name="in_proj_bwd",
        grid=(nt,),
        in_specs=[
            _rows(ts, ATTN_WIDTH),
            _rows(ts, KV_WIDTH),
            _rows(ts, KV_WIDTH),
            _rows(ts, POOL_WIDTH),
            _rows(ts, ATTN_WIDTH + KV_WIDTH),
            _rows(ts, D_MODEL),
            _rows(ts, D_MODEL),
            _resident((1, D_MODEL)),
            _resident((1, ATTN_WIDTH)),
            _resident((1, KV_WIDTH)),
            _resident((IN_WIDTH, D_MODEL)),
            _resident((ATTN_WIDTH, ATTN_WIDTH)),
            _resident((KV_WIDTH, KV_WIDTH)),
        ],
        out_specs=[
            _rows(ts, D_MODEL),
            _acc((IN_WIDTH, D_MODEL)),
            _acc((1, D_MODEL)),
            _acc((1, SMALL_LANES)),
            _acc((1, SMALL_LANES)),
        ],
        out_shape=[
            jax.ShapeDtypeStruct((s, D_MODEL), F32),
            jax.ShapeDtypeStruct((IN_WIDTH, D_MODEL), F32),
            jax.ShapeDtypeStruct((1, D_MODEL), F32),
            jax.ShapeDtypeStruct((1, SMALL_LANES), F32),
            jax.ShapeDtypeStruct((1, SMALL_LANES), F32),
        ],
        scratch_shapes=[
            pltpu.VMEM((ts, IN_WIDTH), BF16),
            pltpu.VMEM((1, ATTN_WIDTH), F32),
            pltpu.VMEM((1, KV_WIDTH), F32),
        ],
    )


BIG_WEIGHTS = (
    ("w_in", True, IN_WIDTH // N_DEV, D_MODEL),
    ("w_out", False, D_MODEL // N_DEV, D_MODEL),
    ("w_gate", True, D_FF // N_DEV, D_MODEL),
    ("w_up", True, D_FF // N_DEV, D_MODEL),
    ("w_down", False, D_FF // N_DEV, D_MODEL),
    ("w_ple_gate", False, D_MODEL // N_DEV, D_MODEL),
    ("w_ple_proj", False, PLE_DIM, D_MODEL // N_DEV),
)
N_BIG = len(BIG_WEIGHTS)


def _place():
    x, y, c = lax.axis_index("x"), lax.axis_index("y"), lax.axis_index("c")
    chips = [(1 - x, y), (x, 1 - y), (1 - x, 1 - y)]
    return x, y, c, chips


class _Gather:
    def __init__(self, n):
        self.n = n
        self.sems = [pltpu.SemaphoreType.DMA((n, 7)), pltpu.SemaphoreType.DMA((n, 7)), pltpu.SemaphoreType.DMA((n,))]

    def _ctx(self, srcs, outs, sems):
        send_sems, recv_sems, local_sems = sems
        x, y, c, chips = _place()
        me, sibling = (x, y, c), (x, y, 1 - c)

        def block(k, owner):
            px, py, pc = owner
            return outs[k].at[4 * px + 2 * py + pc]

        def copy(k, idx, owner, to, mine=False):
            return pltpu.make_async_remote_copy(
                src_ref=srcs[k] if mine else block(k, owner), dst_ref=block(k, owner),
                send_sem=send_sems.at[k, idx], recv_sem=recv_sems.at[k, idx], device_id=to, device_id_type=MESH)

        def local(k):
            return pltpu.make_async_copy(srcs[k], block(k, me), local_sems.at[k])

        return c, chips, me, sibling, copy, local

    def begin(self, srcs, outs, sems):
        c, chips, me, sibling, copy, local = self._ctx(srcs, outs, sems)
        for k in range(self.n):
            local(k).start()
            copy(k, 0, me, sibling, mine=True).start()
            for j, chip in enumerate(chips):
                copy(k, 1 + j, me, (*chip, c), mine=True).start()

    def middle(self, srcs, outs, sems):
        c, chips, me, sibling, copy, local = self._ctx(srcs, outs, sems)
        for j, chip in enumerate(chips):
            for k in range(self.n):
                copy(k, 1 + j, (*chip, c), me).wait_recv()
                copy(k, 4 + j, (*chip, c), sibling).start()

    def end(self, srcs, outs, sems):
        c, chips, me, sibling, copy, local = self._ctx(srcs, outs, sems)
        for k in range(self.n):
            copy(k, 0, sibling, me).wait_recv()
            for j, chip in enumerate(chips):
                copy(k, 4 + j, (*chip, 1 - c), me).wait_recv()
        for k in range(self.n):
            copy(k, 0, me, sibling, mine=True).wait_send()
            for j, chip in enumerate(chips):
                copy(k, 1 + j, me, (*chip, c), mine=True).wait_send()
                copy(k, 4 + j, (*chip, c), sibling).wait_send()
            local(k).wait()


def _gather_rider(shards):
    g = _Gather(len(shards))
    shapes = [jax.ShapeDtypeStruct((N_DEV, *sh.shape), sh.dtype) for sh in shards]
    return _Rider(shards, shapes, g.sems, g.begin, g.end, g.middle)


def _cast_and_gather_first(shards, rel_bias_t):
    g = _Gather(1)
    any_spec = pl.BlockSpec(memory_space=pl.ANY)
    vmem = pl.BlockSpec(memory_space=pltpu.VMEM)

    def body(*refs):
        ins, rb_ref, outs = refs[:N_BIG], refs[N_BIG], refs[N_BIG + 1:2 * N_BIG + 1]
        gathered, tab_ref, sems = refs[2 * N_BIG + 1], refs[2 * N_BIG + 2], refs[2 * N_BIG + 3:]
        outs[0][...] = ins[0][...].astype(BF16)
        g.begin(outs[:1], [gathered], sems)
        for k in range(1, N_BIG):
            outs[k][...] = ins[k][...].astype(BF16)
        _write_bias_table(rb_ref, tab_ref)
        g.middle(outs[:1], [gathered], sems)
        g.end(outs[:1], [gathered], sems)

    res = pl.pallas_call(
        body,
        name="cast_and_gather_first",
        in_specs=[vmem] * N_BIG + [pl.BlockSpec(memory_space=pltpu.SMEM)],
        out_specs=[vmem] * N_BIG + [any_spec, vmem],
        out_shape=[jax.ShapeDtypeStruct((r, c), BF16) for _, _, r, c in BIG_WEIGHTS]
        + [jax.ShapeDtypeStruct((N_DEV, *BIG_WEIGHTS[0][2:]), BF16), jax.ShapeDtypeStruct(BIAS_TABLE_SHAPE, F32)],
        scratch_shapes=g.sems,
    )(*shards, rel_bias_t)
    return list(res[:N_BIG]), res[N_BIG], res[N_BIG + 1]


def _sibling_rider(grads):
    n = len(grads)

    def copies(gs, lands, sems):
        send_sems, recv_sems = sems
        x, y, c, _ = _place()
        return [
            pltpu.make_async_remote_copy(
                src_ref=gs[k].at[:, 1 - c], dst_ref=lands[k], send_sem=send_sems.at[k], recv_sem=recv_sems.at[k],
                device_id=(x, y, 1 - c), device_id_type=MESH)
            for k in range(n)
        ]

    def begin(gs, lands, sems):
        for cp in copies(gs, lands, sems):
            cp.start()

    def end(gs, lands, sems):
        for cp in copies(gs, lands, sems):
            cp.wait()

    shapes = [jax.ShapeDtypeStruct((N_CHIPS, *g.shape[2:]), F32) for g in grads]
    return _Rider(grads, shapes, [pltpu.SemaphoreType.DMA((n,)), pltpu.SemaphoreType.DMA((n,))], begin, end)


def _chip_of_relation(j, place):
    x, y = place[0], place[1]
    return jnp.where(j == 0, 2 * (1 - x) + y, jnp.where(j == 1, 2 * x + 1 - y, 2 * (1 - x) + 1 - y))


def _chip_sum(ks, place, grads, from_sibling):
    n = len(ks)
    shapes = [BIG_WEIGHTS[k][2:] for k in ks]
    operands, specs = [], []
    for (r, c), g, l in zip(shapes, grads, from_sibling):
        operands += [g, l]
        specs += [pl.BlockSpec((1, 1, r, c), lambda j, place: (_chip_of_relation(j, place), place[2], 0, 0)),
                  pl.BlockSpec((1, r, c), lambda j, place: (_chip_of_relation(j, place), 0, 0))]
    args, in_specs, _ = _after_last(operands, specs)

    def body(place_ref, *refs):
        ins, outs = refs[:2 * n], refs[len(args):]
        for i in range(n):
            outs[i][0] = (ins[2 * i][0, 0] + ins[2 * i + 1][0]).astype(BF16)

    outs = pl.pallas_call(
        body,
        name="chip_sum_" + "_".join(BIG_WEIGHTS[k][0] for k in ks),
        grid_spec=pltpu.PrefetchScalarGridSpec(
            num_scalar_prefetch=1,
            grid=(N_CHIPS - 1,),
            in_specs=in_specs,
            out_specs=[pl.BlockSpec((1, r, c), lambda j, place: (j, 0, 0)) for r, c in shapes],
        ),
        out_shape=[jax.ShapeDtypeStruct((N_CHIPS - 1, r, c), BF16) for r, c in shapes],
    )(place, *args)
    _mark_issued(outs[0])
    return list(outs)


def _chips_rider(to_send, small=None):
    n = len(to_send)
    inputs = list(to_send) + ([] if small is None else [small])
    shapes = [jax.ShapeDtypeStruct((3, *t.shape[1:]), BF16) for t in to_send]
    sems = [pltpu.SemaphoreType.DMA((max(n, 1), 3)), pltpu.SemaphoreType.DMA((max(n, 1), 3))]
    if small is not None:
        shapes.append(jax.ShapeDtypeStruct((N_DEV, *small.shape), F32))
        sems += [pltpu.SemaphoreType.DMA((7,)), pltpu.SemaphoreType.DMA((7,)), pltpu.SemaphoreType.DMA]

    def copies(ins, outs, sem_refs):
        x, y, c, chips = _place()
        out = []
        for k in range(n):
            for j, (px, py) in enumerate(chips):
                out.append(pltpu.make_async_remote_copy(
                    src_ref=ins[k].at[j], dst_ref=outs[k].at[j],
                    send_sem=sem_refs[0].at[k, j], recv_sem=sem_refs[1].at[k, j],
                    device_id=(px, py, c), device_id_type=MESH))
        local = None
        if small is not None:
            me = 4 * x + 2 * y + c
            local = pltpu.make_async_copy(ins[n], outs[n].at[me], sem_refs[4])
            rel = 0
            for fx in (0, 1):
                for fy in (0, 1):
                    for fc in (0, 1):
                        if (fx, fy, fc) != (0, 0, 0):
                            out.append(pltpu.make_async_remote_copy(
                                src_ref=ins[n], dst_ref=outs[n].at[me],
                                send_sem=sem_refs[2].at[rel], recv_sem=sem_refs[3].at[rel],
                                device_id=(x ^ fx, y ^ fy, c ^ fc), device_id_type=MESH))
                            rel += 1
        return out, local

    def begin(ins, outs, sem_refs):
        remote, local = copies(ins, outs, sem_refs)
        if local is not None:
            local.start()
        for cp in remote:
            cp.start()

    def end(ins, outs, sem_refs):
        remote, local = copies(ins, outs, sem_refs)
        for cp in remote:
            cp.wait()
        if local is not None:
            local.wait()

    return _Rider(inputs, shapes, sems, begin, end)


PEER_SETS = {"sibling": 1, "chips": 2, "sibling+chips": 3, "all": 4}


def _peers(pattern):
    x, y, c, chips = _place()
    sibling, others = [(x, y, 1 - c)], [(*chip, c) for chip in chips]
    if pattern == "all":
        return sibling + others + [(*chip, 1 - c) for chip in chips]
    return {"sibling": sibling, "chips": others, "sibling+chips": sibling + others}[pattern]


def _on_sequencer(name, pattern, rider):
    n_in, n_out = len(rider.inputs), len(rider.out_shapes)

    def body(*refs):
        ins, outs, sems = refs[:n_in], refs[n_in:n_in + n_out], refs[n_in + n_out:]
        peers = _peers(pattern)
        barrier = pltpu.get_barrier_semaphore()
        for peer in peers:
            pl.semaphore_signal(barrier, inc=1, device_id=peer, device_id_type=MESH)
        pl.semaphore_wait(barrier, len(peers))
        rider.begin(ins, outs, sems)
        if rider.middle is not None:
            rider.middle(ins, outs, sems)
        rider.end(ins, outs, sems)

    outs = pl.kernel(
        body,
        name=name,
        out_type=tuple(rider.out_shapes),
        mesh=plsc.ScalarSubcoreMesh(axis_name="sequencer", num_cores=1),
        scratch_types=tuple(rider.sems),
        compiler_params=pltpu.CompilerParams(collective_id=PEER_SETS[pattern]),
    )(*rider.inputs)
    return list(outs)


def _adamw(w, g, m, v):
    m = ADAM_B1 * m + (1.0 - ADAM_B1) * g
    v = ADAM_B2 * v + (1.0 - ADAM_B2) * jnp.square(g)
    m_hat = m / (1.0 - ADAM_B1 ** ADAM_STEP)
    v_hat = v / (1.0 - ADAM_B2 ** ADAM_STEP)
    delta = -ADAM_LR * (m_hat / (jnp.sqrt(v_hat) + ADAM_EPS) + ADAM_WD * w)
    return delta, m, v


def _adamw_big(ks, place, operands):
    n = len(ks)
    tiles = lambda i, place: (i, 0)
    in_specs, out_specs, out_shape = [], [], []
    for k in ks:
        _, _, r, c = BIG_WEIGHTS[k]
        tile = r // 2
        in_specs += [
            pl.BlockSpec((1, 1, tile, c), lambda i, place: (2 * place[0] + place[1], place[2], i, 0)),
            pl.BlockSpec((1, tile, c), lambda i, place: (2 * place[0] + place[1], i, 0)),
            pl.BlockSpec((3, tile, c), lambda i, place: (0, i, 0)),
        ] + [pl.BlockSpec((tile, c), tiles)] * 3
        out_specs += [pl.BlockSpec((tile, c), tiles)] * 4
        out_shape += [jax.ShapeDtypeStruct((r, c), F32)] * 4
    args, in_specs, _ = _after_last(sum((list(ops) for ops in operands), []), in_specs)

    def body(place_ref, *refs):
        ins, outs = refs[:6 * n], refs[len(args):]
        for i in range(n):
            mine_ref, sib_ref, land_ref, w_ref, m_ref, v_ref = ins[6 * i:6 * i + 6]
            g_ref, d_ref, nm_ref, nv_ref = outs[4 * i:4 * i + 4]
            g = mine_ref[0, 0] + sib_ref[0]
            g = ((g + land_ref[0].astype(F32)) + land_ref[1].astype(F32)) + land_ref[2].astype(F32)
            g_ref[...] = g
            d_ref[...], nm_ref[...], nv_ref[...] = _adamw(w_ref[...], g, m_ref[...], v_ref[...])

    outs = pl.pallas_call(
        body,
        name="adamw_" + "_".join(BIG_WEIGHTS[k][0] for k in ks),
        grid_spec=pltpu.PrefetchScalarGridSpec(
            num_scalar_prefetch=1, grid=(2,), in_specs=in_specs, out_specs=out_specs),
        out_shape=out_shape,
    )(place, *args)
    _mark_issued(outs[0])
    return [outs[4 * i:4 * i + 4] for i in range(n)]


def _pack_small(arrays):
    rows, offsets = [], []
    at = 0
    for a in arrays:
        if a.ndim != 2 or a.shape[1] != SMALL_LANES or a.shape[0] % 8:
            flat = a.reshape(-1)
            n_rows = -(-flat.shape[0] // (8 * SMALL_LANES)) * 8
            a = jnp.pad(flat, (0, n_rows * SMALL_LANES - flat.shape[0])).reshape(n_rows, SMALL_LANES)
        rows.append(a)
        offsets.append(at)
        at += a.shape[0]
    return jnp.concatenate(rows, axis=0), offsets


def _unpack_small(tot, at, shape):
    r, c = shape
    if r % 8 == 0:
        return tot[at:at + r, :c]
    assert r == 1
    if c <= SMALL_LANES:
        return tot[at:at + 1, :c]
    return jnp.concatenate([tot[at + j:at + j + 1, :] for j in range(c // SMALL_LANES)], axis=1)


def _small_update(packs, loss_at, grads_at, ws, ms, vs):
    n, n_packs = len(ws), len(packs)

    def body(*refs):
        pack_refs, refs = refs[:n_packs], refs[n_packs:]
        w_refs, m_refs, v_refs, loss_ref, outs = refs[:n], refs[n:2 * n], refs[2 * n:3 * n], refs[3 * n], refs[3 * n + 1:]
        tots = []
        for p_ref in pack_refs:
            tot = p_ref[0]
            for j in range(1, N_DEV):
                tot = tot + p_ref[j]
            tots.append(tot)
        loss_ref[...] = _unpack_small(tots[loss_at[0]], loss_at[1], (1, 1))
        for i, (pack, at) in enumerate(grads_at):
            g = _unpack_small(tots[pack], at, w_refs[i].shape)
            outs[i][...] = g
            outs[n + i][...], outs[2 * n + i][...], outs[3 * n + i][...] = _adamw(
                w_refs[i][...], g, m_refs[i][...], v_refs[i][...])

    shapes = [jax.ShapeDtypeStruct(w.shape, F32) for w in ws]
    outs = pl.pallas_call(body, name="small_update", out_shape=[jax.ShapeDtypeStruct((1, 1), F32)] + shapes * 4)(
        *packs, *ws, *ms, *vs)
    return outs[0], outs[1:]


SMALL_NAMES = ("g_attn_norm", "g_q", "g_k", "attn_sinks", "rel_bias", "w_pool", "pool_scale", "g_ffn_norm", "g_ple_norm")


def kernel(x, p, w_in, w_out, g_attn_norm, g_q, g_k, attn_sinks, rel_bias, w_pool, pool_scale, g_ffn_norm, w_gate, w_up, w_down, g_ple_norm, w_ple_gate, w_ple_proj, loss_target, m_w_in, m_w_out, m_g_attn_norm, m_g_q, m_g_k, m_attn_sinks, m_rel_bias, m_w_pool, m_pool_scale, m_g_ffn_norm, m_w_gate, m_w_up, m_w_down, m_g_ple_norm, m_w_ple_gate, m_w_ple_proj, v_w_in, v_w_out, v_g_attn_norm, v_g_q, v_g_k, v_attn_sinks, v_rel_bias, v_w_pool, v_pool_scale, v_g_ffn_norm, v_w_gate, v_w_up, v_w_down, v_g_ple_norm, v_w_ple_gate, v_w_ple_proj):
    weights = dict(w_in=w_in, w_out=w_out, g_attn_norm=g_attn_norm, g_q=g_q, g_k=g_k, attn_sinks=attn_sinks,
                   rel_bias=rel_bias, w_pool=w_pool, pool_scale=pool_scale, g_ffn_norm=g_ffn_norm, w_gate=w_gate,
                   w_up=w_up, w_down=w_down, g_ple_norm=g_ple_norm, w_ple_gate=w_ple_gate, w_ple_proj=w_ple_proj)
    m_in = dict(w_in=m_w_in, w_out=m_w_out, g_attn_norm=m_g_attn_norm, g_q=m_g_q, g_k=m_g_k, attn_sinks=m_attn_sinks,
                rel_bias=m_rel_bias, w_pool=m_w_pool, pool_scale=m_pool_scale, g_ffn_norm=m_g_ffn_norm, w_gate=m_w_gate,
                w_up=m_w_up, w_down=m_w_down, g_ple_norm=m_g_ple_norm, w_ple_gate=m_w_ple_gate, w_ple_proj=m_w_ple_proj)
    v_in = dict(w_in=v_w_in, w_out=v_w_out, g_attn_norm=v_g_attn_norm, g_q=v_g_q, g_k=v_g_k, attn_sinks=v_attn_sinks,
                rel_bias=v_rel_bias, w_pool=v_w_pool, pool_scale=v_pool_scale, g_ffn_norm=v_g_ffn_norm, w_gate=v_w_gate,
                w_up=v_w_up, w_down=v_w_down, g_ple_norm=v_g_ple_norm, w_ple_gate=v_w_ple_gate, w_ple_proj=v_w_ple_proj)

    _issued.clear()
    xs = x[0]
    ps = p[0, 0]
    target = loss_target[0]
    wp = w_pool[0]
    gq_t = jnp.tile(g_q, (1, ATTN_WIDTH // HEAD_DIM))
    gk_t = jnp.tile(g_k, (1, KV_WIDTH // HEAD_DIM))

    def to_blocks(k, arr):
        return jnp.swapaxes(arr[0], 0, 1) if BIG_WEIGHTS[k][1] else arr[0]

    def from_blocks(k, arr):
        return (jnp.swapaxes(arr, 0, 1) if BIG_WEIGHTS[k][1] else arr)[None]

    IN, OUT, GATE, UP, DOWN, PG, PP = range(N_BIG)
    full = lambda g: g.reshape(N_DEV * g.shape[1], g.shape[2])
    halves = lambda k, g: g.reshape(N_CHIPS, 2, *BIG_WEIGHTS[k][2:])
    place = jnp.stack([lax.axis_index("x"), lax.axis_index("y"), lax.axis_index("c")]).astype(jnp.int32)

    sh, w_in_g, tab = _cast_and_gather_first(
        [to_blocks(k, weights[name]) for k, (name, _, _, _) in enumerate(BIG_WEIGHTS)], rel_bias.T)
    w_in_t = full(w_in_g)

    (w_out_g,) = _on_sequencer("gather_out", "sibling+chips", _gather_rider([sh[OUT]]))
    wg_g, wu_g = _on_sequencer("gather_gate_up", "sibling+chips", _gather_rider([sh[GATE], sh[UP]]))
    wd_g, w_pg_g, w_pp_g = _on_sequencer("gather_down_ple", "sibling+chips", _gather_rider([sh[DOWN], sh[PG], sh[PP]]))
    (zqk, qn, kn, v, u) = _in_proj(xs, g_attn_norm, w_in_t, gq_t, gk_t)
    (a,) = _attn_fwd(qn, kn, v, tab, attn_sinks)
    w_out_f = full(w_out_g)
    (h1, hn2, m_out) = _mix_out(u, a, xs, w_out_f, wp, pool_scale, g_ffn_norm)
    wg_t, wu_t = full(wg_g), full(wu_g)
    (gt, up) = _ffn_up(hn2, wg_t, wu_t)
    w_down_f = full(wd_g)

    partial, from_sibling, sums, landed = [None] * N_BIG, [None] * N_BIG, [None] * N_BIG, [None] * N_BIG

    def to_sibling(name, ks, grads):
        for k, g in zip(ks, grads):
            partial[k] = halves(k, g)
        got = _on_sequencer(name, "sibling", _sibling_rider([partial[k] for k in ks]))
        for k, g in zip(ks, got):
            from_sibling[k] = g

    def chip_sum(*ks):
        for k, s in zip(ks, _chip_sum(ks, place, [partial[k] for k in ks], [from_sibling[k] for k in ks])):
            sums[k] = s

    def to_chips(name, ks, small=None):
        got = _on_sequencer(name, "chips" if small is None else "all", _chips_rider([sums[k] for k in ks], small))
        for k, g in zip(ks, got):
            landed[k] = g
        return got[len(ks):]

    (loss_part, dh2, d_wpg, d_wpp, d_g_ple) = _ffn_down_ple(
        gt, up, h1, w_down_f, ps, target, g_ple_norm, full(w_pg_g), w_pp_g)
    to_sibling("sibling_ple", (PG, PP), (d_wpg, d_wpp))
    (dgt, dup, dh1, dh1b, d_g_ffn, d_wd) = _ffn_bwd_act(dh2, h1, gt, up, g_ffn_norm, wg_t, wu_t, w_down_f)
    to_sibling("sibling_down", (DOWN,), (d_wd,))
    chip_sum(PG, PP)
    to_chips("chips_ple", (PG, PP))
    (d_wg_t, d_wu_t) = _ffn_bwd_w(dgt, dup, hn2)
    to_sibling("sibling_gate_up", (GATE, UP), (d_wg_t, d_wu_t))
    _complete_before_next([landed[PG], landed[PP]])
    chip_sum(DOWN)
    to_chips("chips_down", (DOWN,))
    (da, du, d_wpool, d_scale, d_wo) = _mix_bwd(dh1b, u, a, m_out, w_out_f, wp, pool_scale)
    to_sibling("sibling_out", (OUT,), (d_wo,))
    chip_sum(GATE, UP)
    to_chips("chips_gate_up", (GATE, UP))
    _complete_before_next([landed[DOWN]])
    (dqn, dkn, dv, dl_acc, d_sinks) = _attn_bwd(qn, kn, v, a, da, tab, attn_sinks)
    chip_sum(OUT)
    early, early_at = _pack_small([d_wpool.reshape(POOL_WIDTH, POOL_GROUP), d_scale, d_g_ffn, d_g_ple, loss_part[:, :1]])
    (early_all,) = to_chips("chips_out", (OUT,), early)
    (grad_x, d_win_t, d_g_attn, d_gq, d_gk) = _in_proj_bwd(dqn, dkn, dv, du, zqk, xs, dh1, g_attn_norm, gq_t, gk_t, w_in_t)
    to_sibling("sibling_in", (IN,), (d_win_t,))
    _complete_before_next([landed[OUT], landed[GATE], landed[UP], early_all])
    (d_rel_t,) = _bias_table_bwd(dl_acc)
    chip_sum(IN)
    late, late_at = _pack_small([d_g_attn, d_gq[:, :HEAD_DIM], d_gk[:, :HEAD_DIM], d_sinks[:, 0], d_rel_t])
    (late_all,) = to_chips("chips_in", (IN,), late)

    out = {"grad": {}, "delta": {}, "new_m": {}, "new_v": {}}
    for ks in ((PG, PP, DOWN), (OUT, GATE, UP), (IN,)):
        names = [BIG_WEIGHTS[k][0] for k in ks]
        results = _adamw_big(ks, place, [
            (partial[k], from_sibling[k], landed[k], to_blocks(k, weights[n]), to_blocks(k, m_in[n]),
             to_blocks(k, v_in[n])) for k, n in zip(ks, names)])
        for k, name, res in zip(ks, names, results):
            for kind, r in zip(("grad", "delta", "new_m", "new_v"), res):
                out[kind][name] = from_blocks(k, r)
    def as_rows(name, arr):
        return arr.T if name == "rel_bias" else arr.reshape(POOL_WIDTH, POOL_GROUP) if name == "w_pool" else arr

    def from_rows(name, arr):
        return arr.T if name == "rel_bias" else arr.reshape(w_pool.shape) if name == "w_pool" else arr

    grads_at = dict(w_pool=(0, early_at[0]), pool_scale=(0, early_at[1]), g_ffn_norm=(0, early_at[2]),
                    g_ple_norm=(0, early_at[3]), g_attn_norm=(1, late_at[0]), g_q=(1, late_at[1]), g_k=(1, late_at[2]),
                    attn_sinks=(1, late_at[3]), rel_bias=(1, late_at[4]))
    loss, updates = _small_update(
        [early_all, late_all], (0, early_at[4]), [grads_at[n] for n in SMALL_NAMES],
        [as_rows(n, weights[n]) for n in SMALL_NAMES], [as_rows(n, m_in[n]) for n in SMALL_NAMES],
        [as_rows(n, v_in[n]) for n in SMALL_NAMES])
    loss = loss.reshape(())
    n_small = len(SMALL_NAMES)
    for j, kind in enumerate(("grad", "delta", "new_m", "new_v")):
        for i, name in enumerate(SMALL_NAMES):
            out[kind][name] = from_rows(name, updates[j * n_small + i])

    _issued.clear()
    order = ("w_in", "w_out", "g_attn_norm", "g_q", "g_k", "attn_sinks", "rel_bias", "w_pool", "pool_scale",
             "g_ffn_norm", "w_gate", "w_up", "w_down", "g_ple_norm", "w_ple_gate", "w_ple_proj")
    return (loss, grad_x[None], *[out["grad"][n] for n in order], *[out["delta"][n] for n in order],
            *[out["new_m"][n] for n in order], *[out["new_v"][n] for n in order])
```

```python
import math

import jax
import jax.numpy as jnp
import numpy as np
from jax import lax
from jax.experimental import pallas as pl
from jax.experimental.pallas import tpu as pltpu
from jax.experimental.pallas import tpu_sc as plsc

F32 = jnp.float32
BF16 = jnp.bfloat16
MESH = pl.DeviceIdType.MESH

D_MODEL = 1024
HEAD_DIM = 64
ATTN_WIDTH = 512
KV_WIDTH = 128
POOL_WIDTH = 512
POOL_SIZES = (2, 4, 8, 16)
POOL_GROUP = 128
POOL_HALO = 16
IN_WIDTH = 1280
D_FF = 2816
PLE_DIM = 256
BLOCK = 128
N_BUCKETS = 32
MAX_DISTANCE = 128
EPS = 1e-6
N_DEV = 8
N_CHIPS = 4

ADAM_LR = 0.001
ADAM_B1 = 0.9
ADAM_B2 = 0.999
ADAM_EPS = 1e-08
ADAM_WD = 0.01
ADAM_STEP = 10

TOKEN_TILE = 512
FFN_BWD_TILE = 256
FF_CHUNK = 256
FFN_W_SLAB = 256
ATTN_STEP_BLOCKS = 4
HEADS_A = (0, 2, 5, 7)
HEADS_B = (1, 3, 4, 6)
SMALL_LANES = 128


def _nn(a, b):
    return jnp.dot(a, b, preferred_element_type=F32)


def _nt(a, b):
    return lax.dot_general(a, b, (((1,), (1,)), ((), ())), preferred_element_type=F32)


def _tn(a, b):
    return lax.dot_general(a, b, (((0,), (0,)), ((), ())), preferred_element_type=F32)


def _resident(shape):
    nd = len(shape)
    return pl.BlockSpec(shape, lambda i, _nd=nd: (0,) * _nd, pipeline_mode=pl.Buffered(1))


def _rows(tile, width):
    return pl.BlockSpec((tile, width), lambda i: (i, 0))


def _acc(shape):
    nd = len(shape)
    return pl.BlockSpec(shape, lambda i, _nd=nd: (0,) * _nd)


def _head_mean_matrix(width):
    idx = np.arange(width) // HEAD_DIM
    return jnp.asarray((idx[:, None] == idx[None, :]).astype(np.float32) / HEAD_DIM, dtype=BF16)


def _seg_mean(v, bmat):
    hi = v.astype(BF16)
    lo = (v - hi.astype(F32)).astype(BF16)
    return _nn(hi, bmat) + _nn(lo, bmat)


def _rms(x):
    return lax.rsqrt(jnp.mean(x * x, axis=-1, keepdims=True) + EPS)


def _rms_bwd(d_y, x, r, g):
    gy = d_y * g
    d_x = r * gy - x * (r * r * r) * jnp.mean(gy * x, axis=-1, keepdims=True)
    d_g = jnp.sum(d_y * (x * r), axis=0, keepdims=True)
    return d_x, d_g


def _lane_lo(shape):
    return lax.broadcasted_iota(jnp.int32, shape, 1) < HEAD_DIM


class _Rider:
    def __init__(self, inputs, out_shapes, sems, begin, end, middle=None):
        self.inputs, self.out_shapes, self.sems = list(inputs), list(out_shapes), list(sems)
        self.begin, self.middle, self.end = begin, middle, end


_issued = []


def _after_last(args, in_specs):
    extra = list(_issued)
    return list(args) + extra, list(in_specs) + [pl.BlockSpec(memory_space=pl.ANY)] * len(extra), len(extra)


def _mark_issued(out):
    _issued[:] = [out]


def _complete_before_next(arrays):
    _issued.extend(arrays)


def _call(body, args, *, name, grid, in_specs, out_specs, out_shape, scratch_shapes=()):
    n_args = len(args)
    args, in_specs, _ = _after_last(args, in_specs)

    def ordered(*refs):
        body(*refs[:n_args], *refs[len(args):])

    outs = pl.pallas_call(ordered, name=name, grid=grid, in_specs=in_specs, out_specs=list(out_specs),
                          out_shape=list(out_shape), scratch_shapes=list(scratch_shapes))(*args)
    _mark_issued(outs[0])
    return list(outs)


def _in_proj(x, g_attn, w_in_t, gq_t, gk_t):
    s = x.shape[0]
    ts = min(TOKEN_TILE, s)

    def body(x_ref, g_ref, w_ref, gq_ref, gk_ref, bq_ref, bk_ref, zqk_ref, qn_ref, kn_ref, v_ref, u_ref):
        xf = x_ref[...]
        hn = ((xf * _rms(xf)) * g_ref[...]).astype(BF16)
        z = _nt(hn, w_ref[...])
        q = z[:, :ATTN_WIDTH]
        k = z[:, ATTN_WIDTH:ATTN_WIDTH + KV_WIDTH]
        zqk_ref[...] = z[:, :ATTN_WIDTH + KV_WIDTH]
        rq = lax.rsqrt(_seg_mean(q * q, bq_ref[...]) + EPS)
        qn_ref[...] = ((q * rq) * gq_ref[...]).astype(BF16)
        rk = lax.rsqrt(_seg_mean(k * k, bk_ref[...]) + EPS)
        kn_ref[...] = ((k * rk) * gk_ref[...]).astype(BF16)
        v_ref[...] = z[:, ATTN_WIDTH + KV_WIDTH:ATTN_WIDTH + 2 * KV_WIDTH].astype(BF16)
        u_ref[...] = z[:, ATTN_WIDTH + 2 * KV_WIDTH:]

    return _call(
        body,
        (x, g_attn, w_in_t, gq_t, gk_t, _head_mean_matrix(ATTN_WIDTH), _head_mean_matrix(KV_WIDTH)),
        name="in_proj",
        grid=(s // ts,),
        in_specs=[
            _rows(ts, D_MODEL),
            _resident((1, D_MODEL)),
            _resident((IN_WIDTH, D_MODEL)),
            _resident((1, ATTN_WIDTH)),
            _resident((1, KV_WIDTH)),
            _resident((ATTN_WIDTH, ATTN_WIDTH)),
            _resident((KV_WIDTH, KV_WIDTH)),
        ],
        out_specs=[
            _rows(ts, ATTN_WIDTH + KV_WIDTH),
            _rows(ts, ATTN_WIDTH),
            _rows(ts, KV_WIDTH),
            _rows(ts, KV_WIDTH),
            _rows(ts, POOL_WIDTH),
        ],
        out_shape=[
            jax.ShapeDtypeStruct((s, ATTN_WIDTH + KV_WIDTH), F32),
            jax.ShapeDtypeStruct((s, ATTN_WIDTH), BF16),
            jax.ShapeDtypeStruct((s, KV_WIDTH), BF16),
            jax.ShapeDtypeStruct((s, KV_WIDTH), BF16),
            jax.ShapeDtypeStruct((s, POOL_WIDTH), F32),
        ],
    )


def _bucket_ranges():
    n = np.arange(MAX_DISTANCE)
    max_exact = N_BUCKETS // 2
    nf = np.maximum(n, 1).astype(np.float64)
    large = max_exact + (np.log(nf / max_exact) / math.log(MAX_DISTANCE / max_exact) * (N_BUCKETS - max_exact)).astype(np.int64)
    bucket = np.where(n < max_exact, n, np.minimum(large, N_BUCKETS - 1))
    out = []
    for b in range(N_BUCKETS):
        idx = np.nonzero(bucket == b)[0]
        out.append((int(idx.min()), int(idx.max()) + 1))
    return out


def _band_distance():
    i = lax.broadcasted_iota(jnp.int32, (BLOCK, 2 * BLOCK), 0)
    j = lax.broadcasted_iota(jnp.int32, (BLOCK, 2 * BLOCK), 1)
    return BLOCK + i - j


BIAS_TABLE_SHAPE = (2, 4 * BLOCK, 2 * BLOCK)


def _write_bias_table(rb_ref, tab_ref):
    d = _band_distance()
    for half, heads in enumerate((HEADS_A, HEADS_B)):
        for slot, h in enumerate(heads):
            t = jnp.full((BLOCK, 2 * BLOCK), -jnp.inf, F32)
            for b, (lo, hi) in enumerate(_bucket_ranges()):
                t = jnp.where((d >= lo) & (d < hi), rb_ref[h, b], t)
            tab_ref[half, slot * BLOCK:(slot + 1) * BLOCK, :] = t


def _bias_table_bwd(dl_acc):
    ranges = _bucket_ranges()
    n_heads = len(HEADS_A) + len(HEADS_B)

    def body(dl_ref, out_ref):
        d = _band_distance()
        row = lax.broadcasted_iota(jnp.int32, (n_heads, SMALL_LANES), 0)
        lane = lax.broadcasted_iota(jnp.int32, (n_heads, SMALL_LANES), 1)
        out = jnp.zeros((n_heads, SMALL_LANES), F32)
        for b, (lo, hi) in enumerate(ranges):
            in_bucket = (d >= lo) & (d < hi)
            for half, heads in enumerate((HEADS_A, HEADS_B)):
                for slot, h in enumerate(heads):
                    g = dl_ref[half, slot * BLOCK:(slot + 1) * BLOCK, :]
                    part = jnp.sum(jnp.where(in_bucket, g, 0.0), axis=0, keepdims=True)
                    tot = jnp.sum(part, axis=1, keepdims=True)
                    out = jnp.where((row == h) & (lane == b), tot, out)
        out_ref[...] = out

    return _call(
        body,
        (dl_acc,),
        name="bias_table_bwd",
        grid=(1,),
        in_specs=[_acc((2, 4 * BLOCK, 2 * BLOCK))],
        out_specs=[_acc((n_heads, SMALL_LANES))],
        out_shape=[jax.ShapeDtypeStruct((n_heads, SMALL_LANES), F32)],
    )


def _stack_heads(pairs, lo_mask):
    zero = jnp.zeros_like(pairs[0])
    lo = [jnp.where(lo_mask, t, zero) for t in pairs]
    hi = [jnp.where(lo_mask, zero, t) for t in pairs]
    return (jnp.concatenate([lo[0], lo[1], hi[2], hi[3]], axis=0),
            jnp.concatenate([hi[0], hi[1], lo[2], lo[3]], axis=0))


def _unstack_heads(out_a, out_b, lo_mask):
    t = lambda x, r: x[r * BLOCK:(r + 1) * BLOCK, :]
    return [
        jnp.where(lo_mask, t(out_a, 0), t(out_b, 0)),
        jnp.where(lo_mask, t(out_a, 1), t(out_b, 1)),
        jnp.where(lo_mask, t(out_b, 2), t(out_a, 2)),
        jnp.where(lo_mask, t(out_b, 3), t(out_a, 3)),
    ]


def _sink_column(sink_ref, heads):
    row = lax.broadcasted_iota(jnp.int32, (4 * BLOCK, 1), 0)
    col = jnp.full((4 * BLOCK, 1), sink_ref[0, heads[3]], F32)
    for slot in (2, 1, 0):
        col = jnp.where(row < (slot + 1) * BLOCK, sink_ref[0, heads[slot]], col)
    return col


def _band_scores(q_stack, keys, tab, first_block):
    s = _nt(q_stack, keys) * (HEAD_DIM ** -0.5) + tab
    if first_block is not None:
        col = lax.broadcasted_iota(jnp.int32, s.shape, 1)
        s = jnp.where(jnp.logical_and(first_block, col < BLOCK), -jnp.inf, s)
    return s


def _softmax_with_sink(s, sink):
    m = jnp.maximum(jnp.max(s, axis=-1, keepdims=True), sink)
    e = jnp.exp(s - m)
    e_sink = jnp.exp(sink - m)
    den = jnp.sum(e, axis=-1, keepdims=True) + e_sink
    return e / den, e_sink / den


def _band_probs(q_stack, keys, tab, sink, first_block):
    return _softmax_with_sink(_band_scores(q_stack, keys, tab, first_block), sink)


def _attn_specs(n_groups):
    group = lambda n: (jnp.minimum(n, n_groups - 1), 0)
    prev = lambda n: (jnp.maximum(jnp.minimum(n, n_groups - 1) * ATTN_STEP_BLOCKS - 1, 0), 0)
    return group, prev


def _band(prev_ref, group_ref, b):
    rows = lambda i: group_ref[i * BLOCK:(i + 1) * BLOCK, :]
    band = jnp.concatenate([prev_ref[...] if b == 0 else rows(b - 1), rows(b)], axis=0)
    return band, pltpu.roll(band, HEAD_DIM, 1)


def _attn_fwd(qn, kn, v, tab, sinks):
    s = qn.shape[0]
    n_groups = s // (ATTN_STEP_BLOCKS * BLOCK)
    group, prev = _attn_specs(n_groups)
    rows = ATTN_STEP_BLOCKS * BLOCK

    def body(sink_ref, q_ref, kc_ref, kp_ref, vc_ref, vp_ref, tab_ref, o_ref):
        first = pl.program_id(0) == 0
        lo_mask = _lane_lo((BLOCK, BLOCK))
        for b in range(ATTN_STEP_BLOCKS):
            at = slice(b * BLOCK, (b + 1) * BLOCK)
            kk, kk_sw = _band(kp_ref, kc_ref, b)
            vv, vv_sw = _band(vp_ref, vc_ref, b)
            q_a, q_b = _stack_heads([q_ref[at, p * BLOCK:(p + 1) * BLOCK] for p in range(4)], lo_mask)
            no_prev = first if b == 0 else None
            p_a, _ = _band_probs(q_a, kk, tab_ref[0], _sink_column(sink_ref, HEADS_A), no_prev)
            p_b, _ = _band_probs(q_b, kk_sw, tab_ref[1], _sink_column(sink_ref, HEADS_B), no_prev)
            out = _unstack_heads(_nn(p_a.astype(BF16), vv), _nn(p_b.astype(BF16), vv_sw), lo_mask)
            for p in range(4):
                o_ref[at, p * BLOCK:(p + 1) * BLOCK] = out[p].astype(BF16)

    return _call(
        body,
        (sinks, qn, kn, kn, v, v, tab),
        name="attn_fwd",
        grid=(n_groups,),
        in_specs=[
            pl.BlockSpec(memory_space=pltpu.SMEM),
            pl.BlockSpec((rows, ATTN_WIDTH), group),
            pl.BlockSpec((rows, KV_WIDTH), group),
            pl.BlockSpec((BLOCK, KV_WIDTH), prev),
            pl.BlockSpec((rows, KV_WIDTH), group),
            pl.BlockSpec((BLOCK, KV_WIDTH), prev),
            _resident((2, 4 * BLOCK, 2 * BLOCK)),
        ],
        out_specs=[pl.BlockSpec((rows, ATTN_WIDTH), group)],
        out_shape=[jax.ShapeDtypeStruct((s, ATTN_WIDTH), BF16)],
    )


def _pooled(u_tile, u_halo, tile_index, tile_rows):
    halo = jnp.where(tile_index > 0, u_halo, 0.0)
    ext = jnp.concatenate([halo, u_tile], axis=0)
    sums = []
    acc = ext
    for shift in (1, 2, 4, 8):
        acc = acc + pltpu.roll(acc, shift, 0)
        sums.append(acc)
    t = tile_index * tile_rows + lax.broadcasted_iota(jnp.int32, (tile_rows, 1), 0)
    out = []
    for g, w in enumerate(POOL_SIZES):
        lanes = slice(g * POOL_GROUP, (g + 1) * POOL_GROUP)
        cnt = jnp.minimum(t + 1, w).astype(F32)
        out.append(sums[g][POOL_HALO:, lanes] / cnt - u_tile[:, lanes])
    return out


def _halo_before(tile):
    return lambda i: (jnp.maximum(i * (tile // POOL_HALO) - 1, 0), 0)


def _mix_out(u, a, x, w_out, w_pool, pool_scale, g_ffn):
    s = x.shape[0]
    ts = min(TOKEN_TILE, s)

    def body(u_ref, uh_ref, a_ref, x_ref, wo_ref, wp_ref, sc_ref, g_ref, h1_ref, hn_ref, m_ref):
        i = pl.program_id(0)
        pooled = _pooled(u_ref[...], uh_ref[...], i, ts)
        for g in range(len(POOL_SIZES)):
            lanes = slice(g * POOL_GROUP, (g + 1) * POOL_GROUP)
            y = _nn(pooled[g].astype(BF16), wp_ref[g].astype(BF16))
            m_ref[:, lanes] = (y * sc_ref[:, lanes]).astype(BF16)
        h1 = x_ref[...] + _nn(a_ref[...], wo_ref[:ATTN_WIDTH, :]) + _nn(m_ref[...], wo_ref[ATTN_WIDTH:, :])
        h1_ref[...] = h1
        hn_ref[...] = ((h1 * _rms(h1)) * g_ref[...]).astype(BF16)

    return _call(
        body,
        (u, u, a, x, w_out, w_pool, pool_scale, g_ffn),
        name="mix_out",
        grid=(s // ts,),
        in_specs=[
            _rows(ts, POOL_WIDTH),
            pl.BlockSpec((POOL_HALO, POOL_WIDTH), _halo_before(ts)),
            _rows(ts, ATTN_WIDTH),
            _rows(ts, D_MODEL),
            _resident((D_MODEL, D_MODEL)),
            _resident((len(POOL_SIZES), POOL_GROUP, POOL_GROUP)),
            _resident((1, POOL_WIDTH)),
            _resident((1, D_MODEL)),
        ],
        out_specs=[_rows(ts, D_MODEL), _rows(ts, D_MODEL), _rows(ts, POOL_WIDTH)],
        out_shape=[
            jax.ShapeDtypeStruct((s, D_MODEL), F32),
            jax.ShapeDtypeStruct((s, D_MODEL), BF16),
            jax.ShapeDtypeStruct((s, POOL_WIDTH), BF16),
        ],
    )


def _ffn_up(hn2, wg_t, wu_t):
    s = hn2.shape[0]
    ts = min(TOKEN_TILE, s)

    def body(hn_ref, wg_ref, wu_ref, gt_ref, up_ref):
        hn = hn_ref[...]
        for c in range(D_FF // FF_CHUNK):
            cols = slice(c * FF_CHUNK, (c + 1) * FF_CHUNK)
            gt_ref[:, cols] = _nt(hn, wg_ref[cols, :]).astype(BF16)
            up_ref[:, cols] = _nt(hn, wu_ref[cols, :]).astype(BF16)

    return _call(
        body,
        (hn2, wg_t, wu_t),
        name="ffn_up",
        grid=(s // ts,),
        in_specs=[_rows(ts, D_MODEL), _resident((D_FF, D_MODEL)), _resident((D_FF, D_MODEL))],
        out_specs=[_rows(ts, D_FF), _rows(ts, D_FF)],
        out_shape=[jax.ShapeDtypeStruct((s, D_FF), BF16), jax.ShapeDtypeStruct((s, D_FF), BF16)],
    )


def _silu_mul(gt, up):
    return (gt * jax.nn.sigmoid(gt)) * up


def _ffn_down_ple(gt, up, h1, w_down, p, target, g_ple, w_pg, w_pp):
    s = h1.shape[0]
    ts = min(TOKEN_TILE, s)
    blk = D_MODEL // N_DEV

    def body(gt_ref, up_ref, h1_ref, wd_ref, p_ref, t_ref, g_ref, wpg_ref, wpp_ref,
             loss_ref, dh_ref, dwpg_ref, dwpp_ref, dg_ref):
        @pl.when(pl.program_id(0) == 0)
        def _():
            loss_ref[...] = jnp.zeros_like(loss_ref)
            dwpg_ref[...] = jnp.zeros_like(dwpg_ref)
            dwpp_ref[...] = jnp.zeros_like(dwpp_ref)
            dg_ref[...] = jnp.zeros_like(dg_ref)

        h2v = h1_ref[...]
        for c in range(D_FF // FF_CHUNK):
            cols = slice(c * FF_CHUNK, (c + 1) * FF_CHUNK)
            act = _silu_mul(gt_ref[:, cols].astype(F32), up_ref[:, cols].astype(F32)).astype(BF16)
            h2v = _nn(act, wd_ref[cols, :]) + h2v
        r = _rms(h2v)
        hn = ((h2v * r) * g_ref[...]).astype(BF16)
        gate = jax.nn.sigmoid(_nn(hn, wpg_ref[...]))
        pb = p_ref[...].astype(BF16)
        pp = _nn(pb, jnp.concatenate([wpp_ref[j] for j in range(N_DEV)], axis=1))
        diff = (h2v + gate * pp) - t_ref[...]
        loss_ref[...] += jnp.sum(jnp.sum(diff * diff, axis=0, keepdims=True), axis=1, keepdims=True) * (0.5 / D_MODEL)
        dy = diff * (1.0 / D_MODEL)
        d_pp = (dy * gate).astype(BF16)
        d_pre = ((dy * pp) * (gate * (1.0 - gate))).astype(BF16)
        d_x, d_g = _rms_bwd(_nt(d_pre, wpg_ref[...]), h2v, r, g_ref[...])
        dg_ref[...] += d_g
        dh_ref[...] = dy + d_x
        d_wpp = _tn(pb, d_pp)
        for j in range(N_DEV):
            dwpp_ref[j] += d_wpp[:, j * blk:(j + 1) * blk]
        dwpg_ref[...] += _tn(hn, d_pre)

    return _call(
        body,
        (gt, up, h1, w_down, p, target, g_ple, w_pg, w_pp),
        name="ffn_down_ple",
        grid=(s // ts,),
        in_specs=[
            _rows(ts, D_FF),
            _rows(ts, D_FF),
            _rows(ts, D_MODEL),
            _resident((D_FF, D_MODEL)),
            _rows(ts, PLE_DIM),
            _rows(ts, D_MODEL),
            _resident((1, D_MODEL)),
            _resident((D_MODEL, D_MODEL)),
            _resident((N_DEV, PLE_DIM, blk)),
        ],
        out_specs=[
            _acc((1, SMALL_LANES)),
            _rows(ts, D_MODEL),
            _acc((D_MODEL, D_MODEL)),
            _acc((N_DEV, PLE_DIM, blk)),
            _acc((1, D_MODEL)),
        ],
        out_shape=[
            jax.ShapeDtypeStruct((1, SMALL_LANES), F32),
            jax.ShapeDtypeStruct((s, D_MODEL), F32),
            jax.ShapeDtypeStruct((D_MODEL, D_MODEL), F32),
            jax.ShapeDtypeStruct((N_DEV, PLE_DIM, blk), F32),
            jax.ShapeDtypeStruct((1, D_MODEL), F32),
        ],
    )


def _ffn_bwd_act(dh2, h1, gt, up, g_ffn, wg_t, wu_t, w_down):
    s = h1.shape[0]
    ts = min(FFN_BWD_TILE, s)

    def body(dh_ref, h1_ref, gt_ref, up_ref, g_ref, wg_ref, wu_ref, wd_ref,
             dgt_ref, dup_ref, dh1_ref, dh1b_ref, dg_ref, dwd_ref, act_ref):
        @pl.when(pl.program_id(0) == 0)
        def _():
            dg_ref[...] = jnp.zeros_like(dg_ref)
            dwd_ref[...] = jnp.zeros_like(dwd_ref)

        dhb = dh_ref[...].astype(BF16)
        d_hn = jnp.zeros((ts, D_MODEL), F32)
        for c in range(D_FF // FF_CHUNK):
            cols = slice(c * FF_CHUNK, (c + 1) * FF_CHUNK)
            d_act = _nt(dhb, wd_ref[cols, :])
            gtv = gt_ref[:, cols].astype(F32)
            upv = up_ref[:, cols].astype(F32)
            sg = jax.nn.sigmoid(gtv)
            silu = gtv * sg
            act_ref[:, cols] = (silu * upv).astype(BF16)
            d_up = (d_act * silu).astype(BF16)
            d_gt = ((d_act * upv) * (sg * (1.0 + gtv * (1.0 - sg)))).astype(BF16)
            dup_ref[:, cols] = d_up
            dgt_ref[:, cols] = d_gt
            d_hn = (_nn(d_gt, wg_ref[cols, :]) + _nn(d_up, wu_ref[cols, :])) + d_hn
        dwd_ref[...] += _tn(act_ref[...], dhb)
        h1v = h1_ref[...]
        d_x, d_g = _rms_bwd(d_hn, h1v, _rms(h1v), g_ref[...])
        dg_ref[...] += d_g
        dh1 = dh_ref[...] + d_x
        dh1_ref[...] = dh1
        dh1b_ref[...] = dh1.astype(BF16)

    return _call(
        body,
        (dh2, h1, gt, up, g_ffn, wg_t, wu_t, w_down),
        name="ffn_bwd_act",
        grid=(s // ts,),
        in_specs=[
            _rows(ts, D_MODEL),
            _rows(ts, D_MODEL),
            _rows(ts, D_FF),
            _rows(ts, D_FF),
            _resident((1, D_MODEL)),
            _resident((D_FF, D_MODEL)),
            _resident((D_FF, D_MODEL)),
            _resident((D_FF, D_MODEL)),
        ],
        out_specs=[
            _rows(ts, D_FF), _rows(ts, D_FF),
            _rows(ts, D_MODEL), _rows(ts, D_MODEL), _acc((1, D_MODEL)), _acc((D_FF, D_MODEL)),
        ],
        out_shape=[
            jax.ShapeDtypeStruct((s, D_FF), BF16),
            jax.ShapeDtypeStruct((s, D_FF), BF16),
            jax.ShapeDtypeStruct((s, D_MODEL), F32),
            jax.ShapeDtypeStruct((s, D_MODEL), BF16),
            jax.ShapeDtypeStruct((1, D_MODEL), F32),
            jax.ShapeDtypeStruct((D_FF, D_MODEL), F32),
        ],
        scratch_shapes=[pltpu.VMEM((ts, D_FF), BF16)],
    )


def _ffn_bwd_w(dgt, dup, hn2):
    s = hn2.shape[0]
    slab = pl.BlockSpec((s, FFN_W_SLAB), lambda i: (0, i))

    def body(dgt_ref, dup_ref, hn_ref, dwg_ref, dwu_ref):
        hn = hn_ref[...]
        dwg_ref[...] = _tn(dgt_ref[...], hn)
        dwu_ref[...] = _tn(dup_ref[...], hn)

    return _call(
        body,
        (dgt, dup, hn2),
        name="ffn_bwd_w",
        grid=(D_FF // FFN_W_SLAB,),
        in_specs=[slab, slab, _resident((s, D_MODEL))],
        out_specs=[_rows(FFN_W_SLAB, D_MODEL)] * 2,
        out_shape=[jax.ShapeDtypeStruct((D_FF, D_MODEL), F32)] * 2,
    )


def _mix_bwd(dh1b, u, a, m, w_out, w_pool, pool_scale):
    s = u.shape[0]
    ts = min(TOKEN_TILE, s)
    nt = s // ts
    halo_after = lambda i: (jnp.minimum((i + 1) * (ts // POOL_HALO), s // POOL_HALO - 1), 0)
    n_groups = len(POOL_SIZES)

    def body(dh_ref, dhn_ref, u_ref, uh_ref, a_ref, m_ref, wo_ref, wp_ref, sc_ref,
             da_ref, du_ref, dwp_ref, dsc_ref, dwo_ref):
        i = pl.program_id(0)

        @pl.when(i == 0)
        def _():
            dwp_ref[...] = jnp.zeros_like(dwp_ref)
            dsc_ref[...] = jnp.zeros_like(dsc_ref)
            dwo_ref[...] = jnp.zeros_like(dwo_ref)

        dh = dh_ref[...]
        dwo_ref[:ATTN_WIDTH, :] += _tn(a_ref[...], dh)
        dwo_ref[ATTN_WIDTH:, :] += _tn(m_ref[...], dh)
        da_ref[...] = _nt(dh, wo_ref[:ATTN_WIDTH, :])
        dh_next = jnp.where(i < nt - 1, dhn_ref[...], jnp.zeros_like(dhn_ref))
        dm_ext = _nt(jnp.concatenate([dh, dh_next], axis=0), wo_ref[ATTN_WIDTH:, :])
        pooled = _pooled(u_ref[...], uh_ref[...], i, ts)
        t_ext = i * ts + lax.broadcasted_iota(jnp.int32, (ts + POOL_HALO, 1), 0)
        for g, w in enumerate(POOL_SIZES):
            lanes = slice(g * POOL_GROUP, (g + 1) * POOL_GROUP)
            wp = wp_ref[g].astype(BF16)
            pg = pooled[g].astype(BF16)
            dm_g = dm_ext[:, lanes]
            dsc_ref[:, lanes] += jnp.sum(dm_g[:ts, :] * _nn(pg, wp), axis=0, keepdims=True)
            dy = (dm_g * sc_ref[:, lanes]).astype(BF16)
            dwp_ref[g] += _tn(pg, dy[:ts, :])
            d_pool = _nt(dy, wp)
            acc = d_pool / jnp.minimum(t_ext + 1, w).astype(F32)
            shift = 1
            while shift < w:
                acc = acc + pltpu.roll(acc, ts + POOL_HALO - shift, 0)
                shift *= 2
            du_ref[:, lanes] = (acc[:ts, :] - d_pool[:ts, :]).astype(BF16)

    return _call(
        body,
        (dh1b, dh1b, u, u, a, m, w_out, w_pool, pool_scale),
        name="mix_bwd",
        grid=(nt,),
        in_specs=[
            _rows(ts, D_MODEL),
            pl.BlockSpec((POOL_HALO, D_MODEL), halo_after),
            _rows(ts, POOL_WIDTH),
            pl.BlockSpec((POOL_HALO, POOL_WIDTH), _halo_before(ts)),
            _rows(ts, ATTN_WIDTH),
            _rows(ts, POOL_WIDTH),
            _resident((D_MODEL, D_MODEL)),
            _resident((n_groups, POOL_GROUP, POOL_GROUP)),
            _resident((1, POOL_WIDTH)),
        ],
        out_specs=[
            _rows(ts, ATTN_WIDTH),
            _rows(ts, POOL_WIDTH),
            _acc((n_groups, POOL_GROUP, POOL_GROUP)),
            _acc((1, POOL_WIDTH)),
            _acc((D_MODEL, D_MODEL)),
        ],
        out_shape=[
            jax.ShapeDtypeStruct((s, ATTN_WIDTH), F32),
            jax.ShapeDtypeStruct((s, POOL_WIDTH), BF16),
            jax.ShapeDtypeStruct((n_groups, POOL_GROUP, POOL_GROUP), F32),
            jax.ShapeDtypeStruct((1, POOL_WIDTH), F32),
            jax.ShapeDtypeStruct((D_MODEL, D_MODEL), F32),
        ],
    )


def _attn_bwd(qn, kn, v, a, da, tab, sinks):
    s = qn.shape[0]
    qb = ATTN_STEP_BLOCKS
    rows = qb * BLOCK
    n_groups = s // rows
    group, prev = _attn_specs(n_groups)
    done = lambda n: (jnp.maximum(n - 1, 0), 0)

    def body(sink_ref, q_ref, kc_ref, kp_ref, vc_ref, vp_ref, o_ref, do_ref, tab_ref,
             dq_ref, dk_ref, dv_ref, dl_ref, ds_ref, k_carry, v_carry, sink_acc):
        n = pl.program_id(0)

        @pl.when(n == 0)
        def _():
            dl_ref[...] = jnp.zeros_like(dl_ref)
            k_carry[...] = jnp.zeros_like(k_carry)
            v_carry[...] = jnp.zeros_like(v_carry)
            sink_acc[...] = jnp.zeros_like(sink_acc)

        @pl.when(n < n_groups)
        def _():
            first = n == 0
            lo_mask = _lane_lo((BLOCK, BLOCK))
            chains = [(b, half) for b in range(qb) for half in range(2)]
            tile = lambda ref, b, p: ref[b * BLOCK:(b + 1) * BLOCK, p * BLOCK:(p + 1) * BLOCK]
            keys = [_band(kp_ref, kc_ref, b) for b in range(qb)]
            vals = [_band(vp_ref, vc_ref, b) for b in range(qb)]
            q_st = [_stack_heads([tile(q_ref, b, p) for p in range(4)], lo_mask) for b in range(qb)]
            do_st = [_stack_heads([tile(do_ref, b, p) for p in range(4)], lo_mask) for b in range(qb)]
            o_st = [_stack_heads([tile(o_ref, b, p).astype(F32) for p in range(4)], lo_mask) for b in range(qb)]
            sink_col = [_sink_column(sink_ref, heads) for heads in (HEADS_A, HEADS_B)]
            scores = {(b, h): _band_scores(q_st[b][h], keys[b][h], tab_ref[h], first if b == 0 else None)
                      for b, h in chains}
            dob = {(b, h): do_st[b][h].astype(BF16) for b, h in chains}
            d_probs = {(b, h): _nt(dob[b, h], vals[b][h]) for b, h in chains}
            delta = {(b, h): jnp.sum(do_st[b][h] * o_st[b][h], axis=-1, keepdims=True) for b, h in chains}
            soft = {(b, h): _softmax_with_sink(scores[b, h], sink_col[h]) for b, h in chains}
            dl = {(b, h): soft[b, h][0] * (d_probs[b, h] - delta[b, h]) for b, h in chains}
            for b, h in chains:
                dl_ref[h] += dl[b, h]
                sink_acc[h] += soft[b, h][1] * delta[b, h]
            dsb = {(b, h): (dl[b, h] * (HEAD_DIM ** -0.5)).astype(BF16) for b, h in chains}
            dq_st = {(b, h): _nn(dsb[b, h], keys[b][h]) for b, h in chains}
            dk_parts = {(b, h): _tn(dsb[b, h], q_st[b][h]) for b, h in chains}
            dv_parts = {(b, h): _tn(soft[b, h][0].astype(BF16), dob[b, h]) for b, h in chains}
            for b in range(qb):
                dq = _unstack_heads(dq_st[b, 0], dq_st[b, 1], lo_mask)
                for p in range(4):
                    dq_ref[b * BLOCK:(b + 1) * BLOCK, p * BLOCK:(p + 1) * BLOCK] = dq[p]
            dks = [dk_parts[b, 0] + pltpu.roll(dk_parts[b, 1], HEAD_DIM, 1) for b in range(qb)]
            dvs = [dv_parts[b, 0] + pltpu.roll(dv_parts[b, 1], HEAD_DIM, 1) for b in range(qb)]
            last = slice((qb - 1) * BLOCK, qb * BLOCK)
            for parts, out_ref, carry in ((dks, dk_ref, k_carry), (dvs, dv_ref, v_carry)):
                out_ref[...] = carry[...]
                out_ref[last, :] += parts[0][:BLOCK, :]
                for b in range(qb):
                    own = parts[b][BLOCK:, :]
                    carry[b * BLOCK:(b + 1) * BLOCK, :] = own + parts[b + 1][:BLOCK, :] if b + 1 < qb else own

        @pl.when(n == n_groups)
        def _():
            dk_ref[...] = k_carry[...]
            dv_ref[...] = v_carry[...]
            for half, heads in enumerate((HEADS_A, HEADS_B)):
                for slot, h in enumerate(heads):
                    tot = jnp.sum(sink_acc[half, slot * BLOCK:(slot + 1) * BLOCK, :], axis=0, keepdims=True)
                    ds_ref[h:h + 1, :] = jnp.broadcast_to(-tot, (1, SMALL_LANES))

    return _call(
        body,
        (sinks, qn, kn, kn, v, v, a, da, tab),
        name="attn_bwd",
        grid=(n_groups + 1,),
        in_specs=[
            pl.BlockSpec(memory_space=pltpu.SMEM),
            pl.BlockSpec((rows, ATTN_WIDTH), group),
            pl.BlockSpec((rows, KV_WIDTH), group),
            pl.BlockSpec((BLOCK, KV_WIDTH), prev),
            pl.BlockSpec((rows, KV_WIDTH), group),
            pl.BlockSpec((BLOCK, KV_WIDTH), prev),
            pl.BlockSpec((rows, ATTN_WIDTH), group),
            pl.BlockSpec((rows, ATTN_WIDTH), group),
            _resident((2, 4 * BLOCK, 2 * BLOCK)),
        ],
        out_specs=[
            pl.BlockSpec((rows, ATTN_WIDTH), group),
            pl.BlockSpec((rows, KV_WIDTH), done),
            pl.BlockSpec((rows, KV_WIDTH), done),
            _acc((2, 4 * BLOCK, 2 * BLOCK)),
            _acc((N_DEV, SMALL_LANES)),
        ],
        out_shape=[
            jax.ShapeDtypeStruct((s, ATTN_WIDTH), F32),
            jax.ShapeDtypeStruct((s, KV_WIDTH), F32),
            jax.ShapeDtypeStruct((s, KV_WIDTH), F32),
            jax.ShapeDtypeStruct((2, 4 * BLOCK, 2 * BLOCK), F32),
            jax.ShapeDtypeStruct((N_DEV, SMALL_LANES), F32),
        ],
        scratch_shapes=[
            pltpu.VMEM((rows, KV_WIDTH), F32),
            pltpu.VMEM((rows, KV_WIDTH), F32),
            pltpu.VMEM((2, 4 * BLOCK, 1), F32),
        ],
    )


def _fold_heads(acc):
    t = acc + pltpu.roll(acc, HEAD_DIM, 1)
    out = t[:, :SMALL_LANES]
    for g in range(1, acc.shape[1] // SMALL_LANES):
        out = out + t[:, g * SMALL_LANES:(g + 1) * SMALL_LANES]
    return out


def _in_proj_bwd(dqn, dkn, dv, du, zqk, x, dh1, g_attn, gq_t, gk_t, w_in_t):
    s = x.shape[0]
    ts = min(TOKEN_TILE, s)
    nt = s // ts

    def head_norm_bwd(d_n, raw, g_t, bmat):
        r = lax.rsqrt(_seg_mean(raw * raw, bmat) + EPS)
        gy = d_n * g_t
        d_raw = r * gy - raw * (r * r * r) * _seg_mean(gy * raw, bmat)
        return d_raw, jnp.sum(d_n * (raw * r), axis=0, keepdims=True)

    def body(dqn_ref, dkn_ref, dv_ref, du_ref, zqk_ref, x_ref, dh1_ref, g_ref, gq_ref, gk_ref, w_ref, bq_ref, bk_ref,
             gx_ref, dw_ref, dg_ref, dgq_ref, dgk_ref, dz_ref, gq_acc, gk_acc):
        i = pl.program_id(0)

        @pl.when(i == 0)
        def _():
            dw_ref[...] = jnp.zeros_like(dw_ref)
            dg_ref[...] = jnp.zeros_like(dg_ref)
            gq_acc[...] = jnp.zeros_like(gq_acc)
            gk_acc[...] = jnp.zeros_like(gk_acc)

        d_q, d_gq = head_norm_bwd(dqn_ref[...], zqk_ref[:, :ATTN_WIDTH], gq_ref[...], bq_ref[...])
        d_k, d_gk = head_norm_bwd(dkn_ref[...], zqk_ref[:, ATTN_WIDTH:], gk_ref[...], bk_ref[...])
        gq_acc[...] += d_gq
        gk_acc[...] += d_gk
        dz_ref[:, :ATTN_WIDTH] = d_q.astype(BF16)
        dz_ref[:, ATTN_WIDTH:ATTN_WIDTH + KV_WIDTH] = d_k.astype(BF16)
        dz_ref[:, ATTN_WIDTH + KV_WIDTH:ATTN_WIDTH + 2 * KV_WIDTH] = dv_ref[...].astype(BF16)
        dz_ref[:, ATTN_WIDTH + 2 * KV_WIDTH:] = du_ref[...]
        dz = dz_ref[...]
        xf = x_ref[...]
        r = _rms(xf)
        hn = ((xf * r) * g_ref[...]).astype(BF16)
        d_x, d_g = _rms_bwd(_nn(dz, w_ref[...]), xf, r, g_ref[...])
        dg_ref[...] += d_g
        gx_ref[...] = dh1_ref[...] + d_x
        dw_ref[...] += _tn(dz, hn)

        @pl.when(i == nt - 1)
        def _():
            dgq_ref[...] = _fold_heads(gq_acc[...])
            dgk_ref[...] = _fold_heads(gk_acc[...])

    return _call(
        body,
        (dqn, dkn, dv, du, zqk, x, dh1, g_attn, gq_t, gk_t, w_in_t,
      _head_mean_matrix(ATTN_WIDTH), _head_mean_matrix(KV_WIDTH)),
        name="in_proj_bwd",
        grid=(nt,),
        in_specs=[
            _rows(ts, ATTN_WIDTH),
            _rows(ts, KV_WIDTH),
            _rows(ts, KV_WIDTH),
            _rows(ts, POOL_WIDTH),
            _rows(ts, ATTN_WIDTH + KV_WIDTH),
            _rows(ts, D_MODEL),
            _rows(ts, D_MODEL),
            _resident((1, D_MODEL)),
            _resident((1, ATTN_WIDTH)),
            _resident((1, KV_WIDTH)),
            _resident((IN_WIDTH, D_MODEL)),
            _resident((ATTN_WIDTH, ATTN_WIDTH)),
            _resident((KV_WIDTH, KV_WIDTH)),
        ],
        out_specs=[
            _rows(ts, D_MODEL),
            _acc((IN_WIDTH, D_MODEL)),
            _acc((1, D_MODEL)),
            _acc((1, SMALL_LANES)),
            _acc((1, SMALL_LANES)),
        ],
        out_shape=[
            jax.ShapeDtypeStruct((s, D_MODEL), F32),
            jax.ShapeDtypeStruct((IN_WIDTH, D_MODEL), F32),
            jax.ShapeDtypeStruct((1, D_MODEL), F32),
            jax.ShapeDtypeStruct((1, SMALL_LANES), F32),
            jax.ShapeDtypeStruct((1, SMALL_LANES), F32),
        ],
        scratch_shapes=[
            pltpu.VMEM((ts, IN_WIDTH), BF16),
            pltpu.VMEM((1, ATTN_WIDTH), F32),
            pltpu.VMEM((1, KV_WIDTH), F32),
        ],
    )


BIG_WEIGHTS = (
    ("w_in", True, IN_WIDTH // N_DEV, D_MODEL),
    ("w_out", False, D_MODEL // N_DEV, D_MODEL),
    ("w_gate", True, D_FF // N_DEV, D_MODEL),
    ("w_up", True, D_FF // N_DEV, D_MODEL),
    ("w_down", False, D_FF // N_DEV, D_MODEL),
    ("w_ple_gate", False, D_MODEL // N_DEV, D_MODEL),
    ("w_ple_proj", False, PLE_DIM, D_MODEL // N_DEV),
)
N_BIG = len(BIG_WEIGHTS)


def _place():
    x, y, c = lax.axis_index("x"), lax.axis_index("y"), lax.axis_index("c")
    chips = [(1 - x, y), (x, 1 - y), (1 - x, 1 - y)]
    return x, y, c, chips


class _Gather:
    def __init__(self, n):
        self.n = n
        self.sems = [pltpu.SemaphoreType.DMA((n, 7)), pltpu.SemaphoreType.DMA((n, 7)), pltpu.SemaphoreType.DMA((n,))]

    def _ctx(self, srcs, outs, sems):
        send_sems, recv_sems, local_sems = sems
        x, y, c, chips = _place()
        me, sibling = (x, y, c), (x, y, 1 - c)

        def block(k, owner):
            px, py, pc = owner
            return outs[k].at[4 * px + 2 * py + pc]

        def copy(k, idx, owner, to, mine=False):
            return pltpu.make_async_remote_copy(
                src_ref=srcs[k] if mine else block(k, owner), dst_ref=block(k, owner),
                send_sem=send_sems.at[k, idx], recv_sem=recv_sems.at[k, idx], device_id=to, device_id_type=MESH)

        def local(k):
            return pltpu.make_async_copy(srcs[k], block(k, me), local_sems.at[k])

        return c, chips, me, sibling, copy, local

    def begin(self, srcs, outs, sems):
        c, chips, me, sibling, copy, local = self._ctx(srcs, outs, sems)
        for k in range(self.n):
            local(k).start()
            copy(k, 0, me, sibling, mine=True).start()
            for j, chip in enumerate(chips):
                copy(k, 1 + j, me, (*chip, c), mine=True).start()

    def middle(self, srcs, outs, sems):
        c, chips, me, sibling, copy, local = self._ctx(srcs, outs, sems)
        for j, chip in enumerate(chips):
            for k in range(self.n):
                copy(k, 1 + j, (*chip, c), me).wait_recv()
                copy(k, 4 + j, (*chip, c), sibling).start()

    def end(self, srcs, outs, sems):
        c, chips, me, sibling, copy, local = self._ctx(srcs, outs, sems)
        for k in range(self.n):
            copy(k, 0, sibling, me).wait_recv()
            for j, chip in enumerate(chips):
                copy(k, 4 + j, (*chip, 1 - c), me).wait_recv()
        for k in range(self.n):
            copy(k, 0, me, sibling, mine=True).wait_send()
            for j, chip in enumerate(chips):
                copy(k, 1 + j, me, (*chip, c), mine=True).wait_send()
                copy(k, 4 + j, (*chip, c), sibling).wait_send()
            local(k).wait()


def _gather_rider(shards):
    g = _Gather(len(shards))
    shapes = [jax.ShapeDtypeStruct((N_DEV, *sh.shape), sh.dtype) for sh in shards]
    return _Rider(shards, shapes, g.sems, g.begin, g.end, g.middle)


def _cast_and_gather_first(shards, rel_bias_t):
    g = _Gather(1)
    any_spec = pl.BlockSpec(memory_space=pl.ANY)
    vmem = pl.BlockSpec(memory_space=pltpu.VMEM)

    def body(*refs):
        ins, rb_ref, outs = refs[:N_BIG], refs[N_BIG], refs[N_BIG + 1:2 * N_BIG + 1]
        gathered, tab_ref, sems = refs[2 * N_BIG + 1], refs[2 * N_BIG + 2], refs[2 * N_BIG + 3:]
        outs[0][...] = ins[0][...].astype(BF16)
        g.begin(outs[:1], [gathered], sems)
        for k in range(1, N_BIG):
            outs[k][...] = ins[k][...].astype(BF16)
        _write_bias_table(rb_ref, tab_ref)
        g.middle(outs[:1], [gathered], sems)
        g.end(outs[:1], [gathered], sems)

    res = pl.pallas_call(
        body,
        name="cast_and_gather_first",
        in_specs=[vmem] * N_BIG + [pl.BlockSpec(memory_space=pltpu.SMEM)],
        out_specs=[vmem] * N_BIG + [any_spec, vmem],
        out_shape=[jax.ShapeDtypeStruct((r, c), BF16) for _, _, r, c in BIG_WEIGHTS]
        + [jax.ShapeDtypeStruct((N_DEV, *BIG_WEIGHTS[0][2:]), BF16), jax.ShapeDtypeStruct(BIAS_TABLE_SHAPE, F32)],
        scratch_shapes=g.sems,
    )(*shards, rel_bias_t)
    return list(res[:N_BIG]), res[N_BIG], res[N_BIG + 1]


def _sibling_rider(grads):
    n = len(grads)

    def copies(gs, lands, sems):
        send_sems, recv_sems = sems
        x, y, c, _ = _place()
        return [
            pltpu.make_async_remote_copy(
                src_ref=gs[k].at[:, 1 - c], dst_ref=lands[k], send_sem=send_sems.at[k], recv_sem=recv_sems.at[k],
                device_id=(x, y, 1 - c), device_id_type=MESH)
            for k in range(n)
        ]

    def begin(gs, lands, sems):
        for cp in copies(gs, lands, sems):
            cp.start()

    def end(gs, lands, sems):
        for cp in copies(gs, lands, sems):
            cp.wait()

    shapes = [jax.ShapeDtypeStruct((N_CHIPS, *g.shape[2:]), F32) for g in grads]
    return _Rider(grads, shapes, [pltpu.SemaphoreType.DMA((n,)), pltpu.SemaphoreType.DMA((n,))], begin, end)


def _chip_of_relation(j, place):
    x, y = place[0], place[1]
    return jnp.where(j == 0, 2 * (1 - x) + y, jnp.where(j == 1, 2 * x + 1 - y, 2 * (1 - x) + 1 - y))


def _chip_sum(ks, place, grads, from_sibling):
    n = len(ks)
    shapes = [BIG_WEIGHTS[k][2:] for k in ks]
    operands, specs = [], []
    for (r, c), g, l in zip(shapes, grads, from_sibling):
        operands += [g, l]
        specs += [pl.BlockSpec((1, 1, r, c), lambda j, place: (_chip_of_relation(j, place), place[2], 0, 0)),
                  pl.BlockSpec((1, r, c), lambda j, place: (_chip_of_relation(j, place), 0, 0))]
    args, in_specs, _ = _after_last(operands, specs)

    def body(place_ref, *refs):
        ins, outs = refs[:2 * n], refs[len(args):]
        for i in range(n):
            outs[i][0] = (ins[2 * i][0, 0] + ins[2 * i + 1][0]).astype(BF16)

    outs = pl.pallas_call(
        body,
        name="chip_sum_" + "_".join(BIG_WEIGHTS[k][0] for k in ks),
        grid_spec=pltpu.PrefetchScalarGridSpec(
            num_scalar_prefetch=1,
            grid=(N_CHIPS - 1,),
            in_specs=in_specs,
            out_specs=[pl.BlockSpec((1, r, c), lambda j, place: (j, 0, 0)) for r, c in shapes],
        ),
        out_shape=[jax.ShapeDtypeStruct((N_CHIPS - 1, r, c), BF16) for r, c in shapes],
    )(place, *args)
    _mark_issued(outs[0])
    return list(outs)


def _chips_rider(to_send, small=None):
    n = len(to_send)
    inputs = list(to_send) + ([] if small is None else [small])
    shapes = [jax.ShapeDtypeStruct((3, *t.shape[1:]), BF16) for t in to_send]
    sems = [pltpu.SemaphoreType.DMA((max(n, 1), 3)), pltpu.SemaphoreType.DMA((max(n, 1), 3))]
    if small is not None:
        shapes.append(jax.ShapeDtypeStruct((N_DEV, *small.shape), F32))
        sems += [pltpu.SemaphoreType.DMA((7,)), pltpu.SemaphoreType.DMA((7,)), pltpu.SemaphoreType.DMA]

    def copies(ins, outs, sem_refs):
        x, y, c, chips = _place()
        out = []
        for k in range(n):
            for j, (px, py) in enumerate(chips):
                out.append(pltpu.make_async_remote_copy(
                    src_ref=ins[k].at[j], dst_ref=outs[k].at[j],
                    send_sem=sem_refs[0].at[k, j], recv_sem=sem_refs[1].at[k, j],
                    device_id=(px, py, c), device_id_type=MESH))
        local = None
        if small is not None:
            me = 4 * x + 2 * y + c
            local = pltpu.make_async_copy(ins[n], outs[n].at[me], sem_refs[4])
            rel = 0
            for fx in (0, 1):
                for fy in (0, 1):
                    for fc in (0, 1):
                        if (fx, fy, fc) != (0, 0, 0):
                            out.append(pltpu.make_async_remote_copy(
                                src_ref=ins[n], dst_ref=outs[n].at[me],
                                send_sem=sem_refs[2].at[rel], recv_sem=sem_refs[3].at[rel],
                                device_id=(x ^ fx, y ^ fy, c ^ fc), device_id_type=MESH))
                            rel += 1
        return out, local

    def begin(ins, outs, sem_refs):
        remote, local = copies(ins, outs, sem_refs)
        if local is not None:
            local.start()
        for cp in remote:
            cp.start()

    def end(ins, outs, sem_refs):
        remote, local = copies(ins, outs, sem_refs)
        for cp in remote:
            cp.wait()
        if local is not None:
            local.wait()

    return _Rider(inputs, shapes, sems, begin, end)


PEER_SETS = {"sibling": 1, "chips": 2, "sibling+chips": 3, "all": 4}


def _peers(pattern):
    x, y, c, chips = _place()
    sibling, others = [(x, y, 1 - c)], [(*chip, c) for chip in chips]
    if pattern == "all":
        return sibling + others + [(*chip, 1 - c) for chip in chips]
    return {"sibling": sibling, "chips": others, "sibling+chips": sibling + others}[pattern]


def _on_sequencer(name, pattern, rider):
    n_in, n_out = len(rider.inputs), len(rider.out_shapes)

    def body(*refs):
        ins, outs, sems = refs[:n_in], refs[n_in:n_in + n_out], refs[n_in + n_out:]
        peers = _peers(pattern)
        barrier = pltpu.get_barrier_semaphore()
        for peer in peers:
            pl.semaphore_signal(barrier, inc=1, device_id=peer, device_id_type=MESH)
        pl.semaphore_wait(barrier, len(peers))
        rider.begin(ins, outs, sems)
        if rider.middle is not None:
            rider.middle(ins, outs, sems)
        rider.end(ins, outs, sems)

    outs = pl.kernel(
        body,
        name=name,
        out_type=tuple(rider.out_shapes),
        mesh=plsc.ScalarSubcoreMesh(axis_name="sequencer", num_cores=1),
        scratch_types=tuple(rider.sems),
        compiler_params=pltpu.CompilerParams(collective_id=PEER_SETS[pattern]),
    )(*rider.inputs)
    return list(outs)


def _adamw(w, g, m, v):
    m = ADAM_B1 * m + (1.0 - ADAM_B1) * g
    v = ADAM_B2 * v + (1.0 - ADAM_B2) * jnp.square(g)
    m_hat = m / (1.0 - ADAM_B1 ** ADAM_STEP)
    v_hat = v / (1.0 - ADAM_B2 ** ADAM_STEP)
    delta = -ADAM_LR * (m_hat / (jnp.sqrt(v_hat) + ADAM_EPS) + ADAM_WD * w)
    return delta, m, v


def _adamw_big(ks, place, operands):
    n = len(ks)
    tiles = lambda i, place: (i, 0)
    in_specs, out_specs, out_shape = [], [], []
    for k in ks:
        _, _, r, c = BIG_WEIGHTS[k]
        tile = r // 2
        in_specs += [
            pl.BlockSpec((1, 1, tile, c), lambda i, place: (2 * place[0] + place[1], place[2], i, 0)),
            pl.BlockSpec((1, tile, c), lambda i, place: (2 * place[0] + place[1], i, 0)),
            pl.BlockSpec((3, tile, c), lambda i, place: (0, i, 0)),
        ] + [pl.BlockSpec((tile, c), tiles)] * 3
        out_specs += [pl.BlockSpec((tile, c), tiles)] * 4
        out_shape += [jax.ShapeDtypeStruct((r, c), F32)] * 4
    args, in_specs, _ = _after_last(sum((list(ops) for ops in operands), []), in_specs)

    def body(place_ref, *refs):
        ins, outs = refs[:6 * n], refs[len(args):]
        for i in range(n):
            mine_ref, sib_ref, land_ref, w_ref, m_ref, v_ref = ins[6 * i:6 * i + 6]
            g_ref, d_ref, nm_ref, nv_ref = outs[4 * i:4 * i + 4]
            g = mine_ref[0, 0] + sib_ref[0]
            g = ((g + land_ref[0].astype(F32)) + land_ref[1].astype(F32)) + land_ref[2].astype(F32)
            g_ref[...] = g
            d_ref[...], nm_ref[...], nv_ref[...] = _adamw(w_ref[...], g, m_ref[...], v_ref[...])

    outs = pl.pallas_call(
        body,
        name="adamw_" + "_".join(BIG_WEIGHTS[k][0] for k in ks),
        grid_spec=pltpu.PrefetchScalarGridSpec(
            num_scalar_prefetch=1, grid=(2,), in_specs=in_specs, out_specs=out_specs),
        out_shape=out_shape,
    )(place, *args)
    _mark_issued(outs[0])
    return [outs[4 * i:4 * i + 4] for i in range(n)]


def _pack_small(arrays):
    rows, offsets = [], []
    at = 0
    for a in arrays:
        if a.ndim != 2 or a.shape[1] != SMALL_LANES or a.shape[0] % 8:
            flat = a.reshape(-1)
            n_rows = -(-flat.shape[0] // (8 * SMALL_LANES)) * 8
            a = jnp.pad(flat, (0, n_rows * SMALL_LANES - flat.shape[0])).reshape(n_rows, SMALL_LANES)
        rows.append(a)
        offsets.append(at)
        at += a.shape[0]
    return jnp.concatenate(rows, axis=0), offsets


def _unpack_small(tot, at, shape):
    r, c = shape
    if r % 8 == 0:
        return tot[at:at + r, :c]
    assert r == 1
    if c <= SMALL_LANES:
        return tot[at:at + 1, :c]
    return jnp.concatenate([tot[at + j:at + j + 1, :] for j in range(c // SMALL_LANES)], axis=1)


def _small_update(packs, loss_at, grads_at, ws, ms, vs):
    n, n_packs = len(ws), len(packs)

    def body(*refs):
        pack_refs, refs = refs[:n_packs], refs[n_packs:]
        w_refs, m_refs, v_refs, loss_ref, outs = refs[:n], refs[n:2 * n], refs[2 * n:3 * n], refs[3 * n], refs[3 * n + 1:]
        tots = []
        for p_ref in pack_refs:
            tot = p_ref[0]
            for j in range(1, N_DEV):
                tot = tot + p_ref[j]
            tots.append(tot)
        loss_ref[...] = _unpack_small(tots[loss_at[0]], loss_at[1], (1, 1))
        for i, (pack, at) in enumerate(grads_at):
            g = _unpack_small(tots[pack], at, w_refs[i].shape)
            outs[i][...] = g
            outs[n + i][...], outs[2 * n + i][...], outs[3 * n + i][...] = _adamw(
                w_refs[i][...], g, m_refs[i][...], v_refs[i][...])

    shapes = [jax.ShapeDtypeStruct(w.shape, F32) for w in ws]
    outs = pl.pallas_call(body, name="small_update", out_shape=[jax.ShapeDtypeStruct((1, 1), F32)] + shapes * 4)(
        *packs, *ws, *ms, *vs)
    return outs[0], outs[1:]


SMALL_NAMES = ("g_attn_norm", "g_q", "g_k", "attn_sinks", "rel_bias", "w_pool", "pool_scale", "g_ffn_norm", "g_ple_norm")


def kernel(x, p, w_in, w_out, g_attn_norm, g_q, g_k, attn_sinks, rel_bias, w_pool, pool_scale, g_ffn_norm, w_gate, w_up, w_down, g_ple_norm, w_ple_gate, w_ple_proj, loss_target, m_w_in, m_w_out, m_g_attn_norm, m_g_q, m_g_k, m_attn_sinks, m_rel_bias, m_w_pool, m_pool_scale, m_g_ffn_norm, m_w_gate, m_w_up, m_w_down, m_g_ple_norm, m_w_ple_gate, m_w_ple_proj, v_w_in, v_w_out, v_g_attn_norm, v_g_q, v_g_k, v_attn_sinks, v_rel_bias, v_w_pool, v_pool_scale, v_g_ffn_norm, v_w_gate, v_w_up, v_w_down, v_g_ple_norm, v_w_ple_gate, v_w_ple_proj):
    weights = dict(w_in=w_in, w_out=w_out, g_attn_norm=g_attn_norm, g_q=g_q, g_k=g_k, attn_sinks=attn_sinks,
                   rel_bias=rel_bias, w_pool=w_pool, pool_scale=pool_scale, g_ffn_norm=g_ffn_norm, w_gate=w_gate,
                   w_up=w_up, w_down=w_down, g_ple_norm=g_ple_norm, w_ple_gate=w_ple_gate, w_ple_proj=w_ple_proj)
    m_in = dict(w_in=m_w_in, w_out=m_w_out, g_attn_norm=m_g_attn_norm, g_q=m_g_q, g_k=m_g_k, attn_sinks=m_attn_sinks,
                rel_bias=m_rel_bias, w_pool=m_w_pool, pool_scale=m_pool_scale, g_ffn_norm=m_g_ffn_norm, w_gate=m_w_gate,
                w_up=m_w_up, w_down=m_w_down, g_ple_norm=m_g_ple_norm, w_ple_gate=m_w_ple_gate, w_ple_proj=m_w_ple_proj)
    v_in = dict(w_in=v_w_in, w_out=v_w_out, g_attn_norm=v_g_attn_norm, g_q=v_g_q, g_k=v_g_k, attn_sinks=v_attn_sinks,
                rel_bias=v_rel_bias, w_pool=v_w_pool, pool_scale=v_pool_scale, g_ffn_norm=v_g_ffn_norm, w_gate=v_w_gate,
                w_up=v_w_up, w_down=v_w_down, g_ple_norm=v_g_ple_norm, w_ple_gate=v_w_ple_gate, w_ple_proj=v_w_ple_proj)

    _issued.clear()
    xs = x[0]
    ps = p[0, 0]
    target = loss_target[0]
    wp = w_pool[0]
    gq_t = jnp.tile(g_q, (1, ATTN_WIDTH // HEAD_DIM))
    gk_t = jnp.tile(g_k, (1, KV_WIDTH // HEAD_DIM))

    def to_blocks(k, arr):
        return jnp.swapaxes(arr[0], 0, 1) if BIG_WEIGHTS[k][1] else arr[0]

    def from_blocks(k, arr):
        return (jnp.swapaxes(arr, 0, 1) if BIG_WEIGHTS[k][1] else arr)[None]

    IN, OUT, GATE, UP, DOWN, PG, PP = range(N_BIG)
    full = lambda g: g.reshape(N_DEV * g.shape[1], g.shape[2])
    halves = lambda k, g: g.reshape(N_CHIPS, 2, *BIG_WEIGHTS[k][2:])
    place = jnp.stack([lax.axis_index("x"), lax.axis_index("y"), lax.axis_index("c")]).astype(jnp.int32)

    sh, w_in_g, tab = _cast_and_gather_first(
        [to_blocks(k, weights[name]) for k, (name, _, _, _) in enumerate(BIG_WEIGHTS)], rel_bias.T)
    w_in_t = full(w_in_g)

    (w_out_g,) = _on_sequencer("gather_out", "sibling+chips", _gather_rider([sh[OUT]]))
    wg_g, wu_g = _on_sequencer("gather_gate_up", "sibling+chips", _gather_rider([sh[GATE], sh[UP]]))
    wd_g, w_pg_g, w_pp_g = _on_sequencer("gather_down_ple", "sibling+chips", _gather_rider([sh[DOWN], sh[PG], sh[PP]]))
    (zqk, qn, kn, v, u) = _in_proj(xs, g_attn_norm, w_in_t, gq_t, gk_t)
    (a,) = _attn_fwd(qn, kn, v, tab, attn_sinks)
    w_out_f = full(w_out_g)
    (h1, hn2, m_out) = _mix_out(u, a, xs, w_out_f, wp, pool_scale, g_ffn_norm)
    wg_t, wu_t = full(wg_g), full(wu_g)
    (gt, up) = _ffn_up(hn2, wg_t, wu_t)
    w_down_f = full(wd_g)

    partial, from_sibling, sums, landed = [None] * N_BIG, [None] * N_BIG, [None] * N_BIG, [None] * N_BIG

    def to_sibling(name, ks, grads):
        for k, g in zip(ks, grads):
            partial[k] = halves(k, g)
        got = _on_sequencer(name, "sibling", _sibling_rider([partial[k] for k in ks]))
        for k, g in zip(ks, got):
            from_sibling[k] = g

    def chip_sum(*ks):
        for k, s in zip(ks, _chip_sum(ks, place, [partial[k] for k in ks], [from_sibling[k] for k in ks])):
            sums[k] = s

    def to_chips(name, ks, small=None):
        got = _on_sequencer(name, "chips" if small is None else "all", _chips_rider([sums[k] for k in ks], small))
        for k, g in zip(ks, got):
            landed[k] = g
        return got[len(ks):]

    (loss_part, dh2, d_wpg, d_wpp, d_g_ple) = _ffn_down_ple(
        gt, up, h1, w_down_f, ps, target, g_ple_norm, full(w_pg_g), w_pp_g)
    to_sibling("sibling_ple", (PG, PP), (d_wpg, d_wpp))
    (dgt, dup, dh1, dh1b, d_g_ffn, d_wd) = _ffn_bwd_act(dh2, h1, gt, up, g_ffn_norm, wg_t, wu_t, w_down_f)
    to_sibling("sibling_down", (DOWN,), (d_wd,))
    chip_sum(PG, PP)
    to_chips("chips_ple", (PG, PP))
    (d_wg_t, d_wu_t) = _ffn_bwd_w(dgt, dup, hn2)
    to_sibling("sibling_gate_up", (GATE, UP), (d_wg_t, d_wu_t))
    _complete_before_next([landed[PG], landed[PP]])
    chip_sum(DOWN)
    to_chips("chips_down", (DOWN,))
    (da, du, d_wpool, d_scale, d_wo) = _mix_bwd(dh1b, u, a, m_out, w_out_f, wp, pool_scale)
    to_sibling("sibling_out", (OUT,), (d_wo,))
    chip_sum(GATE, UP)
    to_chips("chips_gate_up", (GATE, UP))
    _complete_before_next([landed[DOWN]])
    (dqn, dkn, dv, dl_acc, d_sinks) = _attn_bwd(qn, kn, v, a, da, tab, attn_sinks)
    chip_sum(OUT)
    early, early_at = _pack_small([d_wpool.reshape(POOL_WIDTH, POOL_GROUP), d_scale, d_g_ffn, d_g_ple, loss_part[:, :1]])
    (early_all,) = to_chips("chips_out", (OUT,), early)
    (grad_x, d_win_t, d_g_attn, d_gq, d_gk) = _in_proj_bwd(dqn, dkn, dv, du, zqk, xs, dh1, g_attn_norm, gq_t, gk_t, w_in_t)
    to_sibling("sibling_in", (IN,), (d_win_t,))
    _complete_before_next([landed[OUT], landed[GATE], landed[UP], early_all])
    (d_rel_t,) = _bias_table_bwd(dl_acc)
    chip_sum(IN)
    late, late_at = _pack_small([d_g_attn, d_gq[:, :HEAD_DIM], d_gk[:, :HEAD_DIM], d_sinks[:, 0], d_rel_t])
    (late_all,) = to_chips("chips_in", (IN,), late)

    out = {"grad": {}, "delta": {}, "new_m": {}, "new_v": {}}
    for ks in ((PG, PP, DOWN), (OUT, GATE, UP), (IN,)):
        names = [BIG_WEIGHTS[k][0] for k in ks]
        results = _adamw_big(ks, place, [
            (partial[k], from_sibling[k], landed[k], to_blocks(k, weights[n]), to_blocks(k, m_in[n]),
             to_blocks(k, v_in[n])) for k, n in zip(ks, names)])
        for k, name, res in zip(ks, names, results):
            for kind, r in zip(("grad", "delta", "new_m", "new_v"), res):
                out[kind][name] = from_blocks(k, r)
    def as_rows(name, arr):
        return arr.T if name == "rel_bias" else arr.reshape(POOL_WIDTH, POOL_GROUP) if name == "w_pool" else arr

    def from_rows(name, arr):
        return arr.T if name == "rel_bias" else arr.reshape(w_pool.shape) if name == "w_pool" else arr

    grads_at = dict(w_pool=(0, early_at[0]), pool_scale=(0, early_at[1]), g_ffn_norm=(0, early_at[2]),
                    g_ple_norm=(0, early_at[3]), g_attn_norm=(1, late_at[0]), g_q=(1, late_at[1]), g_k=(1, late_at[2]),
                    attn_sinks=(1, late_at[3]), rel_bias=(1, late_at[4]))
    loss, updates = _small_update(
        [early_all, late_all], (0, early_at[4]), [grads_at[n] for n in SMALL_NAMES],
        [as_rows(n, weights[n]) for n in SMALL_NAMES], [as_rows(n, m_in[n]) for n in SMALL_NAMES],
        [as_rows(n, v_in[n]) for n in SMALL_NAMES])
    loss = loss.reshape(())
    n_small = len(SMALL_NAMES)
    for j, kind in enumerate(("grad", "delta", "new_m", "new_v")):
        for i, name in enumerate(SMALL_NAMES):
            out[kind][name] = from_rows(name, updates[j * n_small + i])

    _issued.clear()
    order = ("w_in", "w_out", "g_attn_norm", "g_q", "g_k", "attn_sinks", "rel_bias", "w_pool", "pool_scale",
             "g_ffn_norm", "w_gate", "w_up", "w_down", "g_ple_norm", "w_ple_gate", "w_ple_proj")
    return (loss, grad_x[None], *[out["grad"][n] for n in order], *[out["delta"][n] for n in order],
            *[out["new_m"][n] for n in order], *[out["new_v"][n] for n in order])
```

```python
import math

import jax
import jax.numpy as jnp
import numpy as np
from jax import lax
from jax.experimental import pallas as pl
from jax.experimental.pallas import tpu as pltpu
from jax.experimental.pallas import tpu_sc as plsc

F32 = jnp.float32
BF16 = jnp.bfloat16
MESH = pl.DeviceIdType.MESH

D_MODEL = 1024
HEAD_DIM = 64
ATTN_WIDTH = 512
KV_WIDTH = 128
POOL_WIDTH = 512
POOL_SIZES = (2, 4, 8, 16)
POOL_GROUP = 128
POOL_HALO = 16
IN_WIDTH = 1280
D_FF = 2816
PLE_DIM = 256
BLOCK = 128
N_BUCKETS = 32
MAX_DISTANCE = 128
EPS = 1e-6
N_DEV = 8
N_CHIPS = 4

ADAM_LR = 0.001
ADAM_B1 = 0.9
ADAM_B2 = 0.999
ADAM_EPS = 1e-08
ADAM_WD = 0.01
ADAM_STEP = 10

TOKEN_TILE = 512
FFN_BWD_TILE = 256
FF_CHUNK = 256
FFN_W_SLAB = 256
ATTN_STEP_BLOCKS = 4
HEADS_A = (0, 2, 5, 7)
HEADS_B = (1, 3, 4, 6)
SMALL_LANES = 128


def _nn(a, b):
    return jnp.dot(a, b, preferred_element_type=F32)


def _nt(a, b):
    return lax.dot_general(a, b, (((1,), (1,)), ((), ())), preferred_element_type=F32)


def _tn(a, b):
    return lax.dot_general(a, b, (((0,), (0,)), ((), ())), preferred_element_type=F32)


def _resident(shape):
    nd = len(shape)
    return pl.BlockSpec(shape, lambda i, _nd=nd: (0,) * _nd, pipeline_mode=pl.Buffered(1))


def _rows(tile, width):
    return pl.BlockSpec((tile, width), lambda i: (i, 0))


def _acc(shape):
    nd = len(shape)
    return pl.BlockSpec(shape, lambda i, _nd=nd: (0,) * _nd)


def _head_mean_matrix(width):
    idx = np.arange(width) // HEAD_DIM
    return jnp.asarray((idx[:, None] == idx[None, :]).astype(np.float32) / HEAD_DIM, dtype=BF16)


def _seg_mean(v, bmat):
    hi = v.astype(BF16)
    lo = (v - hi.astype(F32)).astype(BF16)
    return _nn(hi, bmat) + _nn(lo, bmat)


def _rms(x):
    return lax.rsqrt(jnp.mean(x * x, axis=-1, keepdims=True) + EPS)


def _rms_bwd(d_y, x, r, g):
    gy = d_y * g
    d_x = r * gy - x * (r * r * r) * jnp.mean(gy * x, axis=-1, keepdims=True)
    d_g = jnp.sum(d_y * (x * r), axis=0, keepdims=True)
    return d_x, d_g


def _lane_lo(shape):
    return lax.broadcasted_iota(jnp.int32, shape, 1) < HEAD_DIM


class _Rider:
    def __init__(self, inputs, out_shapes, sems, begin, end, middle=None):
        self.inputs, self.out_shapes, self.sems = list(inputs), list(out_shapes), list(sems)
        self.begin, self.middle, self.end = begin, middle, end


_issued = []


def _after_last(args, in_specs):
    extra = list(_issued)
    return list(args) + extra, list(in_specs) + [pl.BlockSpec(memory_space=pl.ANY)] * len(extra), len(extra)


def _mark_issued(out):
    _issued[:] = [out]


def _complete_before_next(arrays):
    _issued.extend(arrays)


def _call(body, args, *, name, grid, in_specs, out_specs, out_shape, scratch_shapes=()):
    n_args = len(args)
    args, in_specs, _ = _after_last(args, in_specs)

    def ordered(*refs):
        body(*refs[:n_args], *refs[len(args):])

    outs = pl.pallas_call(ordered, name=name, grid=grid, in_specs=in_specs, out_specs=list(out_specs),
                          out_shape=list(out_shape), scratch_shapes=list(scratch_shapes))(*args)
    _mark_issued(outs[0])
    return list(outs)


def _in_proj(x, g_attn, w_in_t, gq_t, gk_t):
    s = x.shape[0]
    ts = min(TOKEN_TILE, s)

    def body(x_ref, g_ref, w_ref, gq_ref, gk_ref, bq_ref, bk_ref, zqk_ref, qn_ref, kn_ref, v_ref, u_ref):
        xf = x_ref[...]
        hn = ((xf * _rms(xf)) * g_ref[...]).astype(BF16)
        z = _nt(hn, w_ref[...])
        q = z[:, :ATTN_WIDTH]
        k = z[:, ATTN_WIDTH:ATTN_WIDTH + KV_WIDTH]
        zqk_ref[...] = z[:, :ATTN_WIDTH + KV_WIDTH]
        rq = lax.rsqrt(_seg_mean(q * q, bq_ref[...]) + EPS)
        qn_ref[...] = ((q * rq) * gq_ref[...]).astype(BF16)
        rk = lax.rsqrt(_seg_mean(k * k, bk_ref[...]) + EPS)
        kn_ref[...] = ((k * rk) * gk_ref[...]).astype(BF16)
        v_ref[...] = z[:, ATTN_WIDTH + KV_WIDTH:ATTN_WIDTH + 2 * KV_WIDTH].astype(BF16)
        u_ref[...] = z[:, ATTN_WIDTH + 2 * KV_WIDTH:]

    return _call(
        body,
        (x, g_attn, w_in_t, gq_t, gk_t, _head_mean_matrix(ATTN_WIDTH), _head_mean_matrix(KV_WIDTH)),
        name="in_proj",
        grid=(s // ts,),
        in_specs=[
            _rows(ts, D_MODEL),
            _resident((1, D_MODEL)),
            _resident((IN_WIDTH, D_MODEL)),
            _resident((1, ATTN_WIDTH)),
            _resident((1, KV_WIDTH)),
            _resident((ATTN_WIDTH, ATTN_WIDTH)),
            _resident((KV_WIDTH, KV_WIDTH)),
        ],
        out_specs=[
            _rows(ts, ATTN_WIDTH + KV_WIDTH),
            _rows(ts, ATTN_WIDTH),
            _rows(ts, KV_WIDTH),
            _rows(ts, KV_WIDTH),
            _rows(ts, POOL_WIDTH),
        ],
        out_shape=[
            jax.ShapeDtypeStruct((s, ATTN_WIDTH + KV_WIDTH), F32),
            jax.ShapeDtypeStruct((s, ATTN_WIDTH), BF16),
            jax.ShapeDtypeStruct((s, KV_WIDTH), BF16),
            jax.ShapeDtypeStruct((s, KV_WIDTH), BF16),
            jax.ShapeDtypeStruct((s, POOL_WIDTH), F32),
        ],
    )


def _bucket_ranges():
    n = np.arange(MAX_DISTANCE)
    max_exact = N_BUCKETS // 2
    nf = np.maximum(n, 1).astype(np.float64)
    large = max_exact + (np.log(nf / max_exact) / math.log(MAX_DISTANCE / max_exact) * (N_BUCKETS - max_exact)).astype(np.int64)
    bucket = np.where(n < max_exact, n, np.minimum(large, N_BUCKETS - 1))
    out = []
    for b in range(N_BUCKETS):
        idx = np.nonzero(bucket == b)[0]
        out.append((int(idx.min()), int(idx.max()) + 1))
    return out


def _band_distance():
    i = lax.broadcasted_iota(jnp.int32, (BLOCK, 2 * BLOCK), 0)
    j = lax.broadcasted_iota(jnp.int32, (BLOCK, 2 * BLOCK), 1)
    return BLOCK + i - j


BIAS_TABLE_SHAPE = (2, 4 * BLOCK, 2 * BLOCK)


def _write_bias_table(rb_ref, tab_ref):
    d = _band_distance()
    for half, heads in enumerate((HEADS_A, HEADS_B)):
        for slot, h in enumerate(heads):
            t = jnp.full((BLOCK, 2 * BLOCK), -jnp.inf, F32)
            for b, (lo, hi) in enumerate(_bucket_ranges()):
                t = jnp.where((d >= lo) & (d < hi), rb_ref[h, b], t)
            tab_ref[half, slot * BLOCK:(slot + 1) * BLOCK, :] = t


def _bias_table_bwd(dl_acc):
    ranges = _bucket_ranges()
    n_heads = len(HEADS_A) + len(HEADS_B)

    def body(dl_ref, out_ref):
        d = _band_distance()
        row = lax.broadcasted_iota(jnp.int32, (n_heads, SMALL_LANES), 0)
        lane = lax.broadcasted_iota(jnp.int32, (n_heads, SMALL_LANES), 1)
        out = jnp.zeros((n_heads, SMALL_LANES), F32)
        for b, (lo, hi) in enumerate(ranges):
            in_bucket = (d >= lo) & (d < hi)
            for half, heads in enumerate((HEADS_A, HEADS_B)):
                for slot, h in enumerate(heads):
                    g = dl_ref[half, slot * BLOCK:(slot + 1) * BLOCK, :]
                    part = jnp.sum(jnp.where(in_bucket, g, 0.0), axis=0, keepdims=True)
                    tot = jnp.sum(part, axis=1, keepdims=True)
                    out = jnp.where((row == h) & (lane == b), tot, out)
        out_ref[...] = out

    return _call(
        body,
        (dl_acc,),
        name="bias_table_bwd",
        grid=(1,),
        in_specs=[_acc((2, 4 * BLOCK, 2 * BLOCK))],
        out_specs=[_acc((n_heads, SMALL_LANES))],
        out_shape=[jax.ShapeDtypeStruct((n_heads, SMALL_LANES), F32)],
    )


def _stack_heads(pairs, lo_mask):
    zero = jnp.zeros_like(pairs[0])
    lo = [jnp.where(lo_mask, t, zero) for t in pairs]
    hi = [jnp.where(lo_mask, zero, t) for t in pairs]
    return (jnp.concatenate([lo[0], lo[1], hi[2], hi[3]], axis=0),
            jnp.concatenate([hi[0], hi[1], lo[2], lo[3]], axis=0))


def _unstack_heads(out_a, out_b, lo_mask):
    t = lambda x, r: x[r * BLOCK:(r + 1) * BLOCK, :]
    return [
        jnp.where(lo_mask, t(out_a, 0), t(out_b, 0)),
        jnp.where(lo_mask, t(out_a, 1), t(out_b, 1)),
        jnp.where(lo_mask, t(out_b, 2), t(out_a, 2)),
        jnp.where(lo_mask, t(out_b, 3), t(out_a, 3)),
    ]


def _sink_column(sink_ref, heads):
    row = lax.broadcasted_iota(jnp.int32, (4 * BLOCK, 1), 0)
    col = jnp.full((4 * BLOCK, 1), sink_ref[0, heads[3]], F32)
    for slot in (2, 1, 0):
        col = jnp.where(row < (slot + 1) * BLOCK, sink_ref[0, heads[slot]], col)
    return col


def _band_scores(q_stack, keys, tab, first_block):
    s = _nt(q_stack, keys) * (HEAD_DIM ** -0.5) + tab
    if first_block is not None:
        col = lax.broadcasted_iota(jnp.int32, s.shape, 1)
        s = jnp.where(jnp.logical_and(first_block, col < BLOCK), -jnp.inf, s)
    return s


def _softmax_with_sink(s, sink):
    m = jnp.maximum(jnp.max(s, axis=-1, keepdims=True), sink)
    e = jnp.exp(s - m)
    e_sink = jnp.exp(sink - m)
    den = jnp.sum(e, axis=-1, keepdims=True) + e_sink
    return e / den, e_sink / den


def _band_probs(q_stack, keys, tab, sink, first_block):
    return _softmax_with_sink(_band_scores(q_stack, keys, tab, first_block), sink)


def _attn_specs(n_groups):
    group = lambda n: (jnp.minimum(n, n_groups - 1), 0)
    prev = lambda n: (jnp.maximum(jnp.minimum(n, n_groups - 1) * ATTN_STEP_BLOCKS - 1, 0), 0)
    return group, prev


def _band(prev_ref, group_ref, b):
    rows = lambda i: group_ref[i * BLOCK:(i + 1) * BLOCK, :]
    band = jnp.concatenate([prev_ref[...] if b == 0 else rows(b - 1), rows(b)], axis=0)
    return band, pltpu.roll(band, HEAD_DIM, 1)


def _attn_fwd(qn, kn, v, tab, sinks):
    s = qn.shape[0]
    n_groups = s // (ATTN_STEP_BLOCKS * BLOCK)
    group, prev = _attn_specs(n_groups)
    rows = ATTN_STEP_BLOCKS * BLOCK

    def body(sink_ref, q_ref, kc_ref, kp_ref, vc_ref, vp_ref, tab_ref, o_ref):
        first = pl.program_id(0) == 0
        lo_mask = _lane_lo((BLOCK, BLOCK))
        for b in range(ATTN_STEP_BLOCKS):
            at = slice(b * BLOCK, (b + 1) * BLOCK)
            kk, kk_sw = _band(kp_ref, kc_ref, b)
            vv, vv_sw = _band(vp_ref, vc_ref, b)
            q_a, q_b = _stack_heads([q_ref[at, p * BLOCK:(p + 1) * BLOCK] for p in range(4)], lo_mask)
            no_prev = first if b == 0 else None
            p_a, _ = _band_probs(q_a, kk, tab_ref[0], _sink_column(sink_ref, HEADS_A), no_prev)
            p_b, _ = _band_probs(q_b, kk_sw, tab_ref[1], _sink_column(sink_ref, HEADS_B), no_prev)
            out = _unstack_heads(_nn(p_a.astype(BF16), vv), _nn(p_b.astype(BF16), vv_sw), lo_mask)
            for p in range(4):
                o_ref[at, p * BLOCK:(p + 1) * BLOCK] = out[p].astype(BF16)

    return _call(
        body,
        (sinks, qn, kn, kn, v, v, tab),
        name="attn_fwd",
        grid=(n_groups,),
        in_specs=[
            pl.BlockSpec(memory_space=pltpu.SMEM),
            pl.BlockSpec((rows, ATTN_WIDTH), group),
            pl.BlockSpec((rows, KV_WIDTH), group),
            pl.BlockSpec((BLOCK, KV_WIDTH), prev),
            pl.BlockSpec((rows, KV_WIDTH), group),
            pl.BlockSpec((BLOCK, KV_WIDTH), prev),
            _resident((2, 4 * BLOCK, 2 * BLOCK)),
        ],
        out_specs=[pl.BlockSpec((rows, ATTN_WIDTH), group)],
        out_shape=[jax.ShapeDtypeStruct((s, ATTN_WIDTH), BF16)],
    )


def _pooled(u_tile, u_halo, tile_index, tile_rows):
    halo = jnp.where(tile_index > 0, u_halo, 0.0)
    ext = jnp.concatenate([halo, u_tile], axis=0)
    sums = []
    acc = ext
    for shift in (1, 2, 4, 8):
        acc = acc + pltpu.roll(acc, shift, 0)
        sums.append(acc)
    t = tile_index * tile_rows + lax.broadcasted_iota(jnp.int32, (tile_rows, 1), 0)
    out = []
    for g, w in enumerate(POOL_SIZES):
        lanes = slice(g * POOL_GROUP, (g + 1) * POOL_GROUP)
        cnt = jnp.minimum(t + 1, w).astype(F32)
        out.append(sums[g][POOL_HALO:, lanes] / cnt - u_tile[:, lanes])
    return out


def _halo_before(tile):
    return lambda i: (jnp.maximum(i * (tile // POOL_HALO) - 1, 0), 0)


def _mix_out(u, a, x, w_out, w_pool, pool_scale, g_ffn):
    s = x.shape[0]
    ts = min(TOKEN_TILE, s)

    def body(u_ref, uh_ref, a_ref, x_ref, wo_ref, wp_ref, sc_ref, g_ref, h1_ref, hn_ref, m_ref):
        i = pl.program_id(0)
        pooled = _pooled(u_ref[...], uh_ref[...], i, ts)
        for g in range(len(POOL_SIZES)):
            lanes = slice(g * POOL_GROUP, (g + 1) * POOL_GROUP)
            y = _nn(pooled[g].astype(BF16), wp_ref[g].astype(BF16))
            m_ref[:, lanes] = (y * sc_ref[:, lanes]).astype(BF16)
        h1 = x_ref[...] + _nn(a_ref[...], wo_ref[:ATTN_WIDTH, :]) + _nn(m_ref[...], wo_ref[ATTN_WIDTH:, :])
        h1_ref[...] = h1
        hn_ref[...] = ((h1 * _rms(h1)) * g_ref[...]).astype(BF16)

    return _call(
        body,
        (u, u, a, x, w_out, w_pool, pool_scale, g_ffn),
        name="mix_out",
        grid=(s // ts,),
        in_specs=[
            _rows(ts, POOL_WIDTH),
            pl.BlockSpec((POOL_HALO, POOL_WIDTH), _halo_before(ts)),
            _rows(ts, ATTN_WIDTH),
            _rows(ts, D_MODEL),
            _resident((D_MODEL, D_MODEL)),
            _resident((len(POOL_SIZES), POOL_GROUP, POOL_GROUP)),
            _resident((1, POOL_WIDTH)),
            _resident((1, D_MODEL)),
        ],
        out_specs=[_rows(ts, D_MODEL), _rows(ts, D_MODEL), _rows(ts, POOL_WIDTH)],
        out_shape=[
            jax.ShapeDtypeStruct((s, D_MODEL), F32),
            jax.ShapeDtypeStruct((s, D_MODEL), BF16),
            jax.ShapeDtypeStruct((s, POOL_WIDTH), BF16),
        ],
    )


def _ffn_up(hn2, wg_t, wu_t):
    s = hn2.shape[0]
    ts = min(TOKEN_TILE, s)

    def body(hn_ref, wg_ref, wu_ref, gt_ref, up_ref):
        hn = hn_ref[...]
        for c in range(D_FF // FF_CHUNK):
            cols = slice(c * FF_CHUNK, (c + 1) * FF_CHUNK)
            gt_ref[:, cols] = _nt(hn, wg_ref[cols, :]).astype(BF16)
            up_ref[:, cols] = _nt(hn, wu_ref[cols, :]).astype(BF16)

    return _call(
        body,
        (hn2, wg_t, wu_t),
        name="ffn_up",
        grid=(s // ts,),
        in_specs=[_rows(ts, D_MODEL), _resident((D_FF, D_MODEL)), _resident((D_FF, D_MODEL))],
        out_specs=[_rows(ts, D_FF), _rows(ts, D_FF)],
        out_shape=[jax.ShapeDtypeStruct((s, D_FF), BF16), jax.ShapeDtypeStruct((s, D_FF), BF16)],
    )


def _silu_mul(gt, up):
    return (gt * jax.nn.sigmoid(gt)) * up


def _ffn_down_ple(gt, up, h1, w_down, p, target, g_ple, w_pg, w_pp):
    s = h1.shape[0]
    ts = min(TOKEN_TILE, s)
    blk = D_MODEL // N_DEV

    def body(gt_ref, up_ref, h1_ref, wd_ref, p_ref, t_ref, g_ref, wpg_ref, wpp_ref,
             loss_ref, dh_ref, dwpg_ref, dwpp_ref, dg_ref):
        @pl.when(pl.program_id(0) == 0)
        def _():
            loss_ref[...] = jnp.zeros_like(loss_ref)
            dwpg_ref[...] = jnp.zeros_like(dwpg_ref)
            dwpp_ref[...] = jnp.zeros_like(dwpp_ref)
            dg_ref[...] = jnp.zeros_like(dg_ref)

        h2v = h1_ref[...]
        for c in range(D_FF // FF_CHUNK):
            cols = slice(c * FF_CHUNK, (c + 1) * FF_CHUNK)
            act = _silu_mul(gt_ref[:, cols].astype(F32), up_ref[:, cols].astype(F32)).astype(BF16)
            h2v = _nn(act, wd_ref[cols, :]) + h2v
        r = _rms(h2v)
        hn = ((h2v * r) * g_ref[...]).astype(BF16)
        gate = jax.nn.sigmoid(_nn(hn, wpg_ref[...]))
        pb = p_ref[...].astype(BF16)
        pp = _nn(pb, jnp.concatenate([wpp_ref[j] for j in range(N_DEV)], axis=1))
        diff = (h2v + gate * pp) - t_ref[...]
        loss_ref[...] += jnp.sum(jnp.sum(diff * diff, axis=0, keepdims=True), axis=1, keepdims=True) * (0.5 / D_MODEL)
        dy = diff * (1.0 / D_MODEL)
        d_pp = (dy * gate).astype(BF16)
        d_pre = ((dy * pp) * (gate * (1.0 - gate))).astype(BF16)
        d_x, d_g = _rms_bwd(_nt(d_pre, wpg_ref[...]), h2v, r, g_ref[...])
        dg_ref[...] += d_g
        dh_ref[...] = dy + d_x
        d_wpp = _tn(pb, d_pp)
        for j in range(N_DEV):
            dwpp_ref[j] += d_wpp[:, j * blk:(j + 1) * blk]
        dwpg_ref[...] += _tn(hn, d_pre)

    return _call(
        body,
        (gt, up, h1, w_down, p, target, g_ple, w_pg, w_pp),
        name="ffn_down_ple",
        grid=(s // ts,),
        in_specs=[
            _rows(ts, D_FF),
            _rows(ts, D_FF),
            _rows(ts, D_MODEL),
            _resident((D_FF, D_MODEL)),
            _rows(ts, PLE_DIM),
            _rows(ts, D_MODEL),
            _resident((1, D_MODEL)),
            _resident((D_MODEL, D_MODEL)),
            _resident((N_DEV, PLE_DIM, blk)),
        ],
        out_specs=[
            _acc((1, SMALL_LANES)),
            _rows(ts, D_MODEL),
            _acc((D_MODEL, D_MODEL)),
            _acc((N_DEV, PLE_DIM, blk)),
            _acc((1, D_MODEL)),
        ],
        out_shape=[
            jax.ShapeDtypeStruct((1, SMALL_LANES), F32),
            jax.ShapeDtypeStruct((s, D_MODEL), F32),
            jax.ShapeDtypeStruct((D_MODEL, D_MODEL), F32),
            jax.ShapeDtypeStruct((N_DEV, PLE_DIM, blk), F32),
            jax.ShapeDtypeStruct((1, D_MODEL), F32),
        ],
    )


def _ffn_bwd_act(dh2, h1, gt, up, g_ffn, wg_t, wu_t, w_down):
    s = h1.shape[0]
    ts = min(FFN_BWD_TILE, s)

    def body(dh_ref, h1_ref, gt_ref, up_ref, g_ref, wg_ref, wu_ref, wd_ref,
             dgt_ref, dup_ref, dh1_ref, dh1b_ref, dg_ref, dwd_ref, act_ref):
        @pl.when(pl.program_id(0) == 0)
        def _():
            dg_ref[...] = jnp.zeros_like(dg_ref)
            dwd_ref[...] = jnp.zeros_like(dwd_ref)

        dhb = dh_ref[...].astype(BF16)
        d_hn = jnp.zeros((ts, D_MODEL), F32)
        for c in range(D_FF // FF_CHUNK):
            cols = slice(c * FF_CHUNK, (c + 1) * FF_CHUNK)
            d_act = _nt(dhb, wd_ref[cols, :])
            gtv = gt_ref[:, cols].astype(F32)
            upv = up_ref[:, cols].astype(F32)
            sg = jax.nn.sigmoid(gtv)
            silu = gtv * sg
            act_ref[:, cols] = (silu * upv).astype(BF16)
            d_up = (d_act * silu).astype(BF16)
            d_gt = ((d_act * upv) * (sg * (1.0 + gtv * (1.0 - sg)))).astype(BF16)
            dup_ref[:, cols] = d_up
            dgt_ref[:, cols] = d_gt
            d_hn = (_nn(d_gt, wg_ref[cols, :]) + _nn(d_up, wu_ref[cols, :])) + d_hn
        dwd_ref[...] += _tn(act_ref[...], dhb)
        h1v = h1_ref[...]
        d_x, d_g = _rms_bwd(d_hn, h1v, _rms(h1v), g_ref[...])
        dg_ref[...] += d_g
        dh1 = dh_ref[...] + d_x
        dh1_ref[...] = dh1
        dh1b_ref[...] = dh1.astype(BF16)

    return _call(
        body,
        (dh2, h1, gt, up, g_ffn, wg_t, wu_t, w_down),
        name="ffn_bwd_act",
        grid=(s // ts,),
        in_specs=[
            _rows(ts, D_MODEL),
            _rows(ts, D_MODEL),
            _rows(ts, D_FF),
            _rows(ts, D_FF),
            _resident((1, D_MODEL)),
            _resident((D_FF, D_MODEL)),
            _resident((D_FF, D_MODEL)),
            _resident((D_FF, D_MODEL)),
        ],
        out_specs=[
            _rows(ts, D_FF), _rows(ts, D_FF),
            _rows(ts, D_MODEL), _rows(ts, D_MODEL), _acc((1, D_MODEL)), _acc((D_FF, D_MODEL)),
        ],
        out_shape=[
            jax.ShapeDtypeStruct((s, D_FF), BF16),
            jax.ShapeDtypeStruct((s, D_FF), BF16),
            jax.ShapeDtypeStruct((s, D_MODEL), F32),
            jax.ShapeDtypeStruct((s, D_MODEL), BF16),
            jax.ShapeDtypeStruct((1, D_MODEL), F32),
            jax.ShapeDtypeStruct((D_FF, D_MODEL), F32),
        ],
        scratch_shapes=[pltpu.VMEM((ts, D_FF), BF16)],
    )


def _ffn_bwd_w(dgt, dup, hn2):
    s = hn2.shape[0]
    slab = pl.BlockSpec((s, FFN_W_SLAB), lambda i: (0, i))

    def body(dgt_ref, dup_ref, hn_ref, dwg_ref, dwu_ref):
        hn = hn_ref[...]
        dwg_ref[...] = _tn(dgt_ref[...], hn)
        dwu_ref[...] = _tn(dup_ref[...], hn)

    return _call(
        body,
        (dgt, dup, hn2),
        name="ffn_bwd_w",
        grid=(D_FF // FFN_W_SLAB,),
        in_specs=[slab, slab, _resident((s, D_MODEL))],
        out_specs=[_rows(FFN_W_SLAB, D_MODEL)] * 2,
        out_shape=[jax.ShapeDtypeStruct((D_FF, D_MODEL), F32)] * 2,
    )


def _mix_bwd(dh1b, u, a, m, w_out, w_pool, pool_scale):
    s = u.shape[0]
    ts = min(TOKEN_TILE, s)
    nt = s // ts
    halo_after = lambda i: (jnp.minimum((i + 1) * (ts // POOL_HALO), s // POOL_HALO - 1), 0)
    n_groups = len(POOL_SIZES)

    def body(dh_ref, dhn_ref, u_ref, uh_ref, a_ref, m_ref, wo_ref, wp_ref, sc_ref,
             da_ref, du_ref, dwp_ref, dsc_ref, dwo_ref):
        i = pl.program_id(0)

        @pl.when(i == 0)
        def _():
            dwp_ref[...] = jnp.zeros_like(dwp_ref)
            dsc_ref[...] = jnp.zeros_like(dsc_ref)
            dwo_ref[...] = jnp.zeros_like(dwo_ref)

        dh = dh_ref[...]
        dwo_ref[:ATTN_WIDTH, :] += _tn(a_ref[...], dh)
        dwo_ref[ATTN_WIDTH:, :] += _tn(m_ref[...], dh)
        da_ref[...] = _nt(dh, wo_ref[:ATTN_WIDTH, :])
        dh_next = jnp.where(i < nt - 1, dhn_ref[...], jnp.zeros_like(dhn_ref))
        dm_ext = _nt(jnp.concatenate([dh, dh_next], axis=0), wo_ref[ATTN_WIDTH:, :])
        pooled = _pooled(u_ref[...], uh_ref[...], i, ts)
        t_ext = i * ts + lax.broadcasted_iota(jnp.int32, (ts + POOL_HALO, 1), 0)
        for g, w in enumerate(POOL_SIZES):
            lanes = slice(g * POOL_GROUP, (g + 1) * POOL_GROUP)
            wp = wp_ref[g].astype(BF16)
            pg = pooled[g].astype(BF16)
            dm_g = dm_ext[:, lanes]
            dsc_ref[:, lanes] += jnp.sum(dm_g[:ts, :] * _nn(pg, wp), axis=0, keepdims=True)
            dy = (dm_g * sc_ref[:, lanes]).astype(BF16)
            dwp_ref[g] += _tn(pg, dy[:ts, :])
            d_pool = _nt(dy, wp)
            acc = d_pool / jnp.minimum(t_ext + 1, w).astype(F32)
            shift = 1
            while shift < w:
                acc = acc + pltpu.roll(acc, ts + POOL_HALO - shift, 0)
                shift *= 2
            du_ref[:, lanes] = (acc[:ts, :] - d_pool[:ts, :]).astype(BF16)

    return _call(
        body,
        (dh1b, dh1b, u, u, a, m, w_out, w_pool, pool_scale),
        name="mix_bwd",
        grid=(nt,),
        in_specs=[
            _rows(ts, D_MODEL),
            pl.BlockSpec((POOL_HALO, D_MODEL), halo_after),
            _rows(ts, POOL_WIDTH),
            pl.BlockSpec((POOL_HALO, POOL_WIDTH), _halo_before(ts)),
            _rows(ts, ATTN_WIDTH),
            _rows(ts, POOL_WIDTH),
            _resident((D_MODEL, D_MODEL)),
            _resident((n_groups, POOL_GROUP, POOL_GROUP)),
            _resident((1, POOL_WIDTH)),
        ],
        out_specs=[
            _rows(ts, ATTN_WIDTH),
            _rows(ts, POOL_WIDTH),
            _acc((n_groups, POOL_GROUP, POOL_GROUP)),
            _acc((1, POOL_WIDTH)),
            _acc((D_MODEL, D_MODEL)),
        ],
        out_shape=[
            jax.ShapeDtypeStruct((s, ATTN_WIDTH), F32),
            jax.ShapeDtypeStruct((s, POOL_WIDTH), BF16),
            jax.ShapeDtypeStruct((n_groups, POOL_GROUP, POOL_GROUP), F32),
            jax.ShapeDtypeStruct((1, POOL_WIDTH), F32),
            jax.ShapeDtypeStruct((D_MODEL, D_MODEL), F32),
        ],
    )


def _attn_bwd(qn, kn, v, a, da, tab, sinks):
    s = qn.shape[0]
    qb = ATTN_STEP_BLOCKS
    rows = qb * BLOCK
    n_groups = s // rows
    group, prev = _attn_specs(n_groups)
    done = lambda n: (jnp.maximum(n - 1, 0), 0)

    def body(sink_ref, q_ref, kc_ref, kp_ref, vc_ref, vp_ref, o_ref, do_ref, tab_ref,
             dq_ref, dk_ref, dv_ref, dl_ref, ds_ref, k_carry, v_carry, sink_acc):
        n = pl.program_id(0)

        @pl.when(n == 0)
        def _():
            dl_ref[...] = jnp.zeros_like(dl_ref)
            k_carry[...] = jnp.zeros_like(k_carry)
            v_carry[...] = jnp.zeros_like(v_carry)
            sink_acc[...] = jnp.zeros_like(sink_acc)

        @pl.when(n < n_groups)
        def _():
            first = n == 0
            lo_mask = _lane_lo((BLOCK, BLOCK))
            chains = [(b, half) for b in range(qb) for half in range(2)]
            tile = lambda ref, b, p: ref[b * BLOCK:(b + 1) * BLOCK, p * BLOCK:(p + 1) * BLOCK]
            keys = [_band(kp_ref, kc_ref, b) for b in range(qb)]
            vals = [_band(vp_ref, vc_ref, b) for b in range(qb)]
            q_st = [_stack_heads([tile(q_ref, b, p) for p in range(4)], lo_mask) for b in range(qb)]
            do_st = [_stack_heads([tile(do_ref, b, p) for p in range(4)], lo_mask) for b in range(qb)]
            o_st = [_stack_heads([tile(o_ref, b, p).astype(F32) for p in range(4)], lo_mask) for b in range(qb)]
            sink_col = [_sink_column(sink_ref, heads) for heads in (HEADS_A, HEADS_B)]
            scores = {(b, h): _band_scores(q_st[b][h], keys[b][h], tab_ref[h], first if b == 0 else None)
                      for b, h in chains}
            dob = {(b, h): do_st[b][h].astype(BF16) for b, h in chains}
            d_probs = {(b, h): _nt(dob[b, h], vals[b][h]) for b, h in chains}
            delta = {(b, h): jnp.sum(do_st[b][h] * o_st[b][h], axis=-1, keepdims=True) for b, h in chains}
            soft = {(b, h): _softmax_with_sink(scores[b, h], sink_col[h]) for b, h in chains}
            dl = {(b, h): soft[b, h][0] * (d_probs[b, h] - delta[b, h]) for b, h in chains}
            for b, h in chains:
                dl_ref[h] += dl[b, h]
                sink_acc[h] += soft[b, h][1] * delta[b, h]
            dsb = {(b, h): (dl[b, h] * (HEAD_DIM ** -0.5)).astype(BF16) for b, h in chains}
            dq_st = {(b, h): _nn(dsb[b, h], keys[b][h]) for b, h in chains}
            dk_parts = {(b, h): _tn(dsb[b, h], q_st[b][h]) for b, h in chains}
            dv_parts = {(b, h): _tn(soft[b, h][0].astype(BF16), dob[b, h]) for b, h in chains}
            for b in range(qb):
                dq = _unstack_heads(dq_st[b, 0], dq_st[b, 1], lo_mask)
                for p in range(4):
                    dq_ref[b * BLOCK:(b + 1) * BLOCK, p * BLOCK:(p + 1) * BLOCK] = dq[p]
            dks = [dk_parts[b, 0] + pltpu.roll(dk_parts[b, 1], HEAD_DIM, 1) for b in range(qb)]
            dvs = [dv_parts[b, 0] + pltpu.roll(dv_parts[b, 1], HEAD_DIM, 1) for b in range(qb)]
            last = slice((qb - 1) * BLOCK, qb * BLOCK)
            for parts, out_ref, carry in ((dks, dk_ref, k_carry), (dvs, dv_ref, v_carry)):
                out_ref[...] = carry[...]
                out_ref[last, :] += parts[0][:BLOCK, :]
                for b in range(qb):
                    own = parts[b][BLOCK:, :]
                    carry[b * BLOCK:(b + 1) * BLOCK, :] = own + parts[b + 1][:BLOCK, :] if b + 1 < qb else own

        @pl.when(n == n_groups)
        def _():
            dk_ref[...] = k_carry[...]
            dv_ref[...] = v_carry[...]
            for half, heads in enumerate((HEADS_A, HEADS_B)):
                for slot, h in enumerate(heads):
                    tot = jnp.sum(sink_acc[half, slot * BLOCK:(slot + 1) * BLOCK, :], axis=0, keepdims=True)
                    ds_ref[h:h + 1, :] = jnp.broadcast_to(-tot, (1, SMALL_LANES))

    return _call(
        body,
        (sinks, qn, kn, kn, v, v, a, da, tab),
        name="attn_bwd",
        grid=(n_groups + 1,),
        in_specs=[
            pl.BlockSpec(memory_space=pltpu.SMEM),
            pl.BlockSpec((rows, ATTN_WIDTH), group),
            pl.BlockSpec((rows, KV_WIDTH), group),
            pl.BlockSpec((BLOCK, KV_WIDTH), prev),
            pl.BlockSpec((rows, KV_WIDTH), group),
            pl.BlockSpec((BLOCK, KV_WIDTH), prev),
            pl.BlockSpec((rows, ATTN_WIDTH), group),
            pl.BlockSpec((rows, ATTN_WIDTH), group),
            _resident((2, 4 * BLOCK, 2 * BLOCK)),
        ],
        out_specs=[
            pl.BlockSpec((rows, ATTN_WIDTH), group),
            pl.BlockSpec((rows, KV_WIDTH), done),
            pl.BlockSpec((rows, KV_WIDTH), done),
            _acc((2, 4 * BLOCK, 2 * BLOCK)),
            _acc((N_DEV, SMALL_LANES)),
        ],
        out_shape=[
            jax.ShapeDtypeStruct((s, ATTN_WIDTH), F32),
            jax.ShapeDtypeStruct((s, KV_WIDTH), F32),
            jax.ShapeDtypeStruct((s, KV_WIDTH), F32),
            jax.ShapeDtypeStruct((2, 4 * BLOCK, 2 * BLOCK), F32),
            jax.ShapeDtypeStruct((N_DEV, SMALL_LANES), F32),
        ],
        scratch_shapes=[
            pltpu.VMEM((rows, KV_WIDTH), F32),
            pltpu.VMEM((rows, KV_WIDTH), F32),
            pltpu.VMEM((2, 4 * BLOCK, 1), F32),
        ],
    )


def _fold_heads(acc):
    t = acc + pltpu.roll(acc, HEAD_DIM, 1)
    out = t[:, :SMALL_LANES]
    for g in range(1, acc.shape[1] // SMALL_LANES):
        out = out + t[:, g * SMALL_LANES:(g + 1) * SMALL_LANES]
    return out


def _in_proj_bwd(dqn, dkn, dv, du, zqk, x, dh1, g_attn, gq_t, gk_t, w_in_t):
    s = x.shape[0]
    ts = min(TOKEN_TILE, s)
    nt = s // ts

    def head_norm_bwd(d_n, raw, g_t, bmat):
        r = lax.rsqrt(_seg_mean(raw * raw, bmat) + EPS)
        gy = d_n * g_t
        d_raw = r * gy - raw * (r * r * r) * _seg_mean(gy * raw, bmat)
        return d_raw, jnp.sum(d_n * (raw * r), axis=0, keepdims=True)

    def body(dqn_ref, dkn_ref, dv_ref, du_ref, zqk_ref, x_ref, dh1_ref, g_ref, gq_ref, gk_ref, w_ref, bq_ref, bk_ref,
             gx_ref, dw_ref, dg_ref, dgq_ref, dgk_ref, dz_ref, gq_acc, gk_acc):
        i = pl.program_id(0)

        @pl.when(i == 0)
        def _():
            dw_ref[...] = jnp.zeros_like(dw_ref)
            dg_ref[...] = jnp.zeros_like(dg_ref)
            gq_acc[...] = jnp.zeros_like(gq_acc)
            gk_acc[...] = jnp.zeros_like(gk_acc)

        d_q, d_gq = head_norm_bwd(dqn_ref[...], zqk_ref[:, :ATTN_WIDTH], gq_ref[...], bq_ref[...])
        d_k, d_gk = head_norm_bwd(dkn_ref[...], zqk_ref[:, ATTN_WIDTH:], gk_ref[...], bk_ref[...])
        gq_acc[...] += d_gq
        gk_acc[...] += d_gk
        dz_ref[:, :ATTN_WIDTH] = d_q.astype(BF16)
        dz_ref[:, ATTN_WIDTH:ATTN_WIDTH + KV_WIDTH] = d_k.astype(BF16)
        dz_ref[:, ATTN_WIDTH + KV_WIDTH:ATTN_WIDTH + 2 * KV_WIDTH] = dv_ref[...].astype(BF16)
        dz_ref[:, ATTN_WIDTH + 2 * KV_WIDTH:] = du_ref[...]
        dz = dz_ref[...]
        xf = x_ref[...]
        r = _rms(xf)
        hn = ((xf * r) * g_ref[...]).astype(BF16)
        d_x, d_g = _rms_bwd(_nn(dz, w_ref[...]), xf, r, g_ref[...])
        dg_ref[...] += d_g
        gx_ref[...] = dh1_ref[...] + d_x
        dw_ref[...] += _tn(dz, hn)

        @pl.when(i == nt - 1)
        def _():
            dgq_ref[...] = _fold_heads(gq_acc[...])
            dgk_ref[...] = _fold_heads(gk_acc[...])

    return _call(
        body,
        (dqn, dkn, dv, du, zqk, x, dh1, g_attn, gq_t, gk_t, w_in_t,
      _head_mean_matrix(ATTN_WIDTH), _head_mean_matrix(KV_WIDTH)),
        name="in_proj_bwd",
        grid=(nt,),
        in_specs=[
            _rows(ts, ATTN_WIDTH),
            _rows(ts, KV_WIDTH),
            _rows(ts, KV_WIDTH),
            _rows(ts, POOL_WIDTH),
            _rows(ts, ATTN_WIDTH + KV_WIDTH),
            _rows(ts, D_MODEL),
            _rows(ts, D_MODEL),
            _resident((1, D_MODEL)),
            _resident((1, ATTN_WIDTH)),
            _resident((1, KV_WIDTH)),
            _resident((IN_WIDTH, D_MODEL)),
            _resident((ATTN_WIDTH, ATTN_WIDTH)),
            _resident((KV_WIDTH, KV_WIDTH)),
        ],
        out_specs=[
            _rows(ts, D_MODEL),
            _acc((IN_WIDTH, D_MODEL)),
            _acc((1, D_MODEL)),
            _acc((1, SMALL_LANES)),
            _acc((1, SMALL_LANES)),
        ],
        out_shape=[
            jax.ShapeDtypeStruct((s, D_MODEL), F32),
            jax.ShapeDtypeStruct((IN_WIDTH, D_MODEL), F32),
            jax.ShapeDtypeStruct((1, D_MODEL), F32),
            jax.ShapeDtypeStruct((1, SMALL_LANES), F32),
            jax.ShapeDtypeStruct((1, SMALL_LANES), F32),
        ],
        scratch_shapes=[
            pltpu.VMEM((ts, IN_WIDTH), BF16),
            pltpu.VMEM((1, ATTN_WIDTH), F32),
            pltpu.VMEM((1, KV_WIDTH), F32),
        ],
    )


BIG_WEIGHTS = (
    ("w_in", True, IN_WIDTH // N_DEV, D_MODEL),
    ("w_out", False, D_MODEL // N_DEV, D_MODEL),
    ("w_gate", True, D_FF // N_DEV, D_MODEL),
    ("w_up", True, D_FF // N_DEV, D_MODEL),
    ("w_down", False, D_FF // N_DEV, D_MODEL),
    ("w_ple_gate", False, D_MODEL // N_DEV, D_MODEL),
    ("w_ple_proj", False, PLE_DIM, D_MODEL // N_DEV),
)
N_BIG = len(BIG_WEIGHTS)


def _place():
    x, y, c = lax.axis_index("x"), lax.axis_index("y"), lax.axis_index("c")
    chips = [(1 - x, y), (x, 1 - y), (1 - x, 1 - y)]
    return x, y, c, chips


class _Gather:
    def __init__(self, n):
        self.n = n
        self.sems = [pltpu.SemaphoreType.DMA((n, 7)), pltpu.SemaphoreType.DMA((n, 7)), pltpu.SemaphoreType.DMA((n,))]

    def _ctx(self, srcs, outs, sems):
        send_sems, recv_sems, local_sems = sems
        x, y, c, chips = _place()
        me, sibling = (x, y, c), (x, y, 1 - c)

        def block(k, owner):
            px, py, pc = owner
            return outs[k].at[4 * px + 2 * py + pc]

        def copy(k, idx, owner, to, mine=False):
            return pltpu.make_async_remote_copy(
                src_ref=srcs[k] if mine else block(k, owner), dst_ref=block(k, owner),
                send_sem=send_sems.at[k, idx], recv_sem=recv_sems.at[k, idx], device_id=to, device_id_type=MESH)

        def local(k):
            return pltpu.make_async_copy(srcs[k], block(k, me), local_sems.at[k])

        return c, chips, me, sibling, copy, local

    def begin(self, srcs, outs, sems):
        c, chips, me, sibling, copy, local = self._ctx(srcs, outs, sems)
        for k in range(self.n):
            local(k).start()
            copy(k, 0, me, sibling, mine=True).start()
            for j, chip in enumerate(chips):
                copy(k, 1 + j, me, (*chip, c), mine=True).start()

    def middle(self, srcs, outs, sems):
        c, chips, me, sibling, copy, local = self._ctx(srcs, outs, sems)
        for j, chip in enumerate(chips):
            for k in range(self.n):
                copy(k, 1 + j, (*chip, c), me).wait_recv()
                copy(k, 4 + j, (*chip, c), sibling).start()

    def end(self, srcs, outs, sems):
        c, chips, me, sibling, copy, local = self._ctx(srcs, outs, sems)
        for k in range(self.n):
            copy(k, 0, sibling, me).wait_recv()
            for j, chip in enumerate(chips):
                copy(k, 4 + j, (*chip, 1 - c), me).wait_recv()
        for k in range(self.n):
            copy(k, 0, me, sibling, mine=True).wait_send()
            for j, chip in enumerate(chips):
                copy(k, 1 + j, me, (*chip, c), mine=True).wait_send()
                copy(k, 4 + j, (*chip, c), sibling).wait_send()
            local(k).wait()


def _gather_rider(shards):
    g = _Gather(len(shards))
    shapes = [jax.ShapeDtypeStruct((N_DEV, *sh.shape), sh.dtype) for sh in shards]
    return _Rider(shards, shapes, g.sems, g.begin, g.end, g.middle)


def _cast_and_gather_first(shards, rel_bias_t):
    g = _Gather(1)
    any_spec = pl.BlockSpec(memory_space=pl.ANY)
    vmem = pl.BlockSpec(memory_space=pltpu.VMEM)

    def body(*refs):
        ins, rb_ref, outs = refs[:N_BIG], refs[N_BIG], refs[N_BIG + 1:2 * N_BIG + 1]
        gathered, tab_ref, sems = refs[2 * N_BIG + 1], refs[2 * N_BIG + 2], refs[2 * N_BIG + 3:]
        outs[0][...] = ins[0][...].astype(BF16)
        g.begin(outs[:1], [gathered], sems)
        for k in range(1, N_BIG):
            outs[k][...] = ins[k][...].astype(BF16)
        _write_bias_table(rb_ref, tab_ref)
        g.middle(outs[:1], [gathered], sems)
        g.end(outs[:1], [gathered], sems)

    res = pl.pallas_call(
        body,
        name="cast_and_gather_first",
        in_specs=[vmem] * N_BIG + [pl.BlockSpec(memory_space=pltpu.SMEM)],
        out_specs=[vmem] * N_BIG + [any_spec, vmem],
        out_shape=[jax.ShapeDtypeStruct((r, c), BF16) for _, _, r, c in BIG_WEIGHTS]
        + [jax.ShapeDtypeStruct((N_DEV, *BIG_WEIGHTS[0][2:]), BF16), jax.ShapeDtypeStruct(BIAS_TABLE_SHAPE, F32)],
        scratch_shapes=g.sems,
    )(*shards, rel_bias_t)
    return list(res[:N_BIG]), res[N_BIG], res[N_BIG + 1]


def _sibling_rider(grads):
    n = len(grads)

    def copies(gs, lands, sems):
        send_sems, recv_sems = sems
        x, y, c, _ = _place()
        return [
            pltpu.make_async_remote_copy(
                src_ref=gs[k].at[:, 1 - c], dst_ref=lands[k], send_sem=send_sems.at[k], recv_sem=recv_sems.at[k],
                device_id=(x, y, 1 - c), device_id_type=MESH)
            for k in range(n)
        ]

    def begin(gs, lands, sems):
        for cp in copies(gs, lands, sems):
            cp.start()

    def end(gs, lands, sems):
        for cp in copies(gs, lands, sems):
            cp.wait()

    shapes = [jax.ShapeDtypeStruct((N_CHIPS, *g.shape[2:]), F32) for g in grads]
    return _Rider(grads, shapes, [pltpu.SemaphoreType.DMA((n,)), pltpu.SemaphoreType.DMA((n,))], begin, end)


def _chip_of_relation(j, place):
    x, y = place[0], place[1]
    return jnp.where(j == 0, 2 * (1 - x) + y, jnp.where(j == 1, 2 * x + 1 - y, 2 * (1 - x) + 1 - y))


def _chip_sum(ks, place, grads, from_sibling):
    n = len(ks)
    shapes = [BIG_WEIGHTS[k][2:] for k in ks]
    operands, specs = [], []
    for (r, c), g, l in zip(shapes, grads, from_sibling):
        operands += [g, l]
        specs += [pl.BlockSpec((1, 1, r, c), lambda j, place: (_chip_of_relation(j, place), place[2], 0, 0)),
                  pl.BlockSpec((1, r, c), lambda j, place: (_chip_of_relation(j, place), 0, 0))]
    args, in_specs, _ = _after_last(operands, specs)

    def body(place_ref, *refs):
        ins, outs = refs[:2 * n], refs[len(args):]
        for i in range(n):
            outs[i][0] = (ins[2 * i][0, 0] + ins[2 * i + 1][0]).astype(BF16)

    outs = pl.pallas_call(
        body,
        name="chip_sum_" + "_".join(BIG_WEIGHTS[k][0] for k in ks),
        grid_spec=pltpu.PrefetchScalarGridSpec(
            num_scalar_prefetch=1,
            grid=(N_CHIPS - 1,),
            in_specs=in_specs,
            out_specs=[pl.BlockSpec((1, r, c), lambda j, place: (j, 0, 0)) for r, c in shapes],
        ),
        out_shape=[jax.ShapeDtypeStruct((N_CHIPS - 1, r, c), BF16) for r, c in shapes],
    )(place, *args)
    _mark_issued(outs[0])
    return list(outs)


def _chips_rider(to_send, small=None):
    n = len(to_send)
    inputs = list(to_send) + ([] if small is None else [small])
    shapes = [jax.ShapeDtypeStruct((3, *t.shape[1:]), BF16) for t in to_send]
    sems = [pltpu.SemaphoreType.DMA((max(n, 1), 3)), pltpu.SemaphoreType.DMA((max(n, 1), 3))]
    if small is not None:
        shapes.append(jax.ShapeDtypeStruct((N_DEV, *small.shape), F32))
        sems += [pltpu.SemaphoreType.DMA((7,)), pltpu.SemaphoreType.DMA((7,)), pltpu.SemaphoreType.DMA]

    def copies(ins, outs, sem_refs):
        x, y, c, chips = _place()
        out = []
        for k in range(n):
            for j, (px, py) in enumerate(chips):
                out.append(pltpu.make_async_remote_copy(
                    src_ref=ins[k].at[j], dst_ref=outs[k].at[j],
                    send_sem=sem_refs[0].at[k, j], recv_sem=sem_refs[1].at[k, j],
                    device_id=(px, py, c), device_id_type=MESH))
        local = None
        if small is not None:
            me = 4 * x + 2 * y + c
            local = pltpu.make_async_copy(ins[n], outs[n].at[me], sem_refs[4])
            rel = 0
            for fx in (0, 1):
                for fy in (0, 1):
                    for fc in (0, 1):
                        if (fx, fy, fc) != (0, 0, 0):
                            out.append(pltpu.make_async_remote_copy(
                                src_ref=ins[n], dst_ref=outs[n].at[me],
                                send_sem=sem_refs[2].at[rel], recv_sem=sem_refs[3].at[rel],
                                device_id=(x ^ fx, y ^ fy, c ^ fc), device_id_type=MESH))
                            rel += 1
        return out, local

    def begin(ins, outs, sem_refs):
        remote, local = copies(ins, outs, sem_refs)
        if local is not None:
            local.start()
        for cp in remote:
            cp.start()

    def end(ins, outs, sem_refs):
        remote, local = copies(ins, outs, sem_refs)
        for cp in remote:
            cp.wait()
        if local is not None:
            local.wait()

    return _Rider(inputs, shapes, sems, begin, end)


PEER_SETS = {"sibling": 1, "chips": 2, "sibling+chips": 3, "all": 4}


def _peers(pattern):
    x, y, c, chips = _place()
    sibling, others = [(x, y, 1 - c)], [(*chip, c) for chip in chips]
    if pattern == "all":
        return sibling + others + [(*chip, 1 - c) for chip in chips]
    return {"sibling": sibling, "chips": others, "sibling+chips": sibling + others}[pattern]


def _on_sequencer(name, pattern, rider):
    n_in, n_out = len(rider.inputs), len(rider.out_shapes)

    def body(*refs):
        ins, outs, sems = refs[:n_in], refs[n_in:n_in + n_out], refs[n_in + n_out:]
        peers = _peers(pattern)
        barrier = pltpu.get_barrier_semaphore()
        for peer in peers:
            pl.semaphore_signal(barrier, inc=1, device_id=peer, device_id_type=MESH)
        pl.semaphore_wait(barrier, len(peers))
        rider.begin(ins, outs, sems)
        if rider.middle is not None:
            rider.middle(ins, outs, sems)
        rider.end(ins, outs, sems)

    outs = pl.kernel(
        body,
        name=name,
        out_type=tuple(rider.out_shapes),
        mesh=plsc.ScalarSubcoreMesh(axis_name="sequencer", num_cores=1),
        scratch_types=tuple(rider.sems),
        compiler_params=pltpu.CompilerParams(collective_id=PEER_SETS[pattern]),
    )(*rider.inputs)
    return list(outs)


def _adamw(w, g, m, v):
    m = ADAM_B1 * m + (1.0 - ADAM_B1) * g
    v = ADAM_B2 * v + (1.0 - ADAM_B2) * jnp.square(g)
    m_hat = m / (1.0 - ADAM_B1 ** ADAM_STEP)
    v_hat = v / (1.0 - ADAM_B2 ** ADAM_STEP)
    delta = -ADAM_LR * (m_hat / (jnp.sqrt(v_hat) + ADAM_EPS) + ADAM_WD * w)
    return delta, m, v


def _adamw_big(ks, place, operands):
    n = len(ks)
    tiles = lambda i, place: (i, 0)
    in_specs, out_specs, out_shape = [], [], []
    for k in ks:
        _, _, r, c = BIG_WEIGHTS[k]
        tile = r // 2
        in_specs += [
            pl.BlockSpec((1, 1, tile, c), lambda i, place: (2 * place[0] + place[1], place[2], i, 0)),
            pl.BlockSpec((1, tile, c), lambda i, place: (2 * place[0] + place[1], i, 0)),
            pl.BlockSpec((3, tile, c), lambda i, place: (0, i, 0)),
        ] + [pl.BlockSpec((tile, c), tiles)] * 3
        out_specs += [pl.BlockSpec((tile, c), tiles)] * 4
        out_shape += [jax.ShapeDtypeStruct((r, c), F32)] * 4
    args, in_specs, _ = _after_last(sum((list(ops) for ops in operands), []), in_specs)

    def body(place_ref, *refs):
        ins, outs = refs[:6 * n], refs[len(args):]
        for i in range(n):
            mine_ref, sib_ref, land_ref, w_ref, m_ref, v_ref = ins[6 * i:6 * i + 6]
            g_ref, d_ref, nm_ref, nv_ref = outs[4 * i:4 * i + 4]
            g = mine_ref[0, 0] + sib_ref[0]
            g = ((g + land_ref[0].astype(F32)) + land_ref[1].astype(F32)) + land_ref[2].astype(F32)
            g_ref[...] = g
            d_ref[...], nm_ref[...], nv_ref[...] = _adamw(w_ref[...], g, m_ref[...], v_ref[...])

    outs = pl.pallas_call(
        body,
        name="adamw_" + "_".join(BIG_WEIGHTS[k][0] for k in ks),
        grid_spec=pltpu.PrefetchScalarGridSpec(
            num_scalar_prefetch=1, grid=(2,), in_specs=in_specs, out_specs=out_specs),
        out_shape=out_shape,
    )(place, *args)
    _mark_issued(outs[0])
    return [outs[4 * i:4 * i + 4] for i in range(n)]


def _pack_small(arrays):
    rows, offsets = [], []
    at = 0
    for a in arrays:
        if a.ndim != 2 or a.shape[1] != SMALL_LANES or a.shape[0] % 8:
            flat = a.reshape(-1)
            n_rows = -(-flat.shape[0] // (8 * SMALL_LANES)) * 8
            a = jnp.pad(flat, (0, n_rows * SMALL_LANES - flat.shape[0])).reshape(n_rows, SMALL_LANES)
        rows.append(a)
        offsets.append(at)
        at += a.shape[0]
    return jnp.concatenate(rows, axis=0), offsets


def _unpack_small(tot, at, shape):
    r, c = shape
    if r % 8 == 0:
        return tot[at:at + r, :c]
    assert r == 1
    if c <= SMALL_LANES:
        return tot[at:at + 1, :c]
    return jnp.concatenate([tot[at + j:at + j + 1, :] for j in range(c // SMALL_LANES)], axis=1)


def _small_update(packs, loss_at, grads_at, ws, ms, vs):
    n, n_packs = len(ws), len(packs)

    def body(*refs):
        pack_refs, refs = refs[:n_packs], refs[n_packs:]
        w_refs, m_refs, v_refs, loss_ref, outs = refs[:n], refs[n:2 * n], refs[2 * n:3 * n], refs[3 * n], refs[3 * n + 1:]
        tots = []
        for p_ref in pack_refs:
            tot = p_ref[0]
            for j in range(1, N_DEV):
                tot = tot + p_ref[j]
            tots.append(tot)
        loss_ref[...] = _unpack_small(tots[loss_at[0]], loss_at[1], (1, 1))
        for i, (pack, at) in enumerate(grads_at):
            g = _unpack_small(tots[pack], at, w_refs[i].shape)
            outs[i][...] = g
            outs[n + i][...], outs[2 * n + i][...], outs[3 * n + i][...] = _adamw(
                w_refs[i][...], g, m_refs[i][...], v_refs[i][...])

    shapes = [jax.ShapeDtypeStruct(w.shape, F32) for w in ws]
    outs = pl.pallas_call(body, name="small_update", out_shape=[jax.ShapeDtypeStruct((1, 1), F32)] + shapes * 4)(
        *packs, *ws, *ms, *vs)
    return outs[0], outs[1:]


SMALL_NAMES = ("g_attn_norm", "g_q", "g_k", "attn_sinks", "rel_bias", "w_pool", "pool_scale", "g_ffn_norm", "g_ple_norm")


def kernel(x, p, w_in, w_out, g_attn_norm, g_q, g_k, attn_sinks, rel_bias, w_pool, pool_scale, g_ffn_norm, w_gate, w_up, w_down, g_ple_norm, w_ple_gate, w_ple_proj, loss_target, m_w_in, m_w_out, m_g_attn_norm, m_g_q, m_g_k, m_attn_sinks, m_rel_bias, m_w_pool, m_pool_scale, m_g_ffn_norm, m_w_gate, m_w_up, m_w_down, m_g_ple_norm, m_w_ple_gate, m_w_ple_proj, v_w_in, v_w_out, v_g_attn_norm, v_g_q, v_g_k, v_attn_sinks, v_rel_bias, v_w_pool, v_pool_scale, v_g_ffn_norm, v_w_gate, v_w_up, v_w_down, v_g_ple_norm, v_w_ple_gate, v_w_ple_proj):
    weights = dict(w_in=w_in, w_out=w_out, g_attn_norm=g_attn_norm, g_q=g_q, g_k=g_k, attn_sinks=attn_sinks,
                   rel_bias=rel_bias, w_pool=w_pool, pool_scale=pool_scale, g_ffn_norm=g_ffn_norm, w_gate=w_gate,
                   w_up=w_up, w_down=w_down, g_ple_norm=g_ple_norm, w_ple_gate=w_ple_gate, w_ple_proj=w_ple_proj)
    m_in = dict(w_in=m_w_in, w_out=m_w_out, g_attn_norm=m_g_attn_norm, g_q=m_g_q, g_k=m_g_k, attn_sinks=m_attn_sinks,
                rel_bias=m_rel_bias, w_pool=m_w_pool, pool_scale=m_pool_scale, g_ffn_norm=m_g_ffn_norm, w_gate=m_w_gate,
                w_up=m_w_up, w_down=m_w_down, g_ple_norm=m_g_ple_norm, w_ple_gate=m_w_ple_gate, w_ple_proj=m_w_ple_proj)
    v_in = dict(w_in=v_w_in, w_out=v_w_out, g_attn_norm=v_g_attn_norm, g_q=v_g_q, g_k=v_g_k, attn_sinks=v_attn_sinks,
                rel_bias=v_rel_bias, w_pool=v_w_pool, pool_scale=v_pool_scale, g_ffn_norm=v_g_ffn_norm, w_gate=v_w_gate,
                w_up=v_w_up, w_down=v_w_down, g_ple_norm=v_g_ple_norm, w_ple_gate=v_w_ple_gate, w_ple_proj=v_w_ple_proj)

    _issued.clear()
    xs = x[0]
    ps = p[0, 0]
    target = loss_target[0]
    wp = w_pool[0]
    gq_t = jnp.tile(g_q, (1, ATTN_WIDTH // HEAD_DIM))
    gk_t = jnp.tile(g_k, (1, KV_WIDTH // HEAD_DIM))

    def to_blocks(k, arr):
        return jnp.swapaxes(arr[0], 0, 1) if BIG_WEIGHTS[k][1] else arr[0]

    def from_blocks(k, arr):
        return (jnp.swapaxes(arr, 0, 1) if BIG_WEIGHTS[k][1] else arr)[None]

    IN, OUT, GATE, UP, DOWN, PG, PP = range(N_BIG)
    full = lambda g: g.reshape(N_DEV * g.shape[1], g.shape[2])
    halves = lambda k, g: g.reshape(N_CHIPS, 2, *BIG_WEIGHTS[k][2:])
    place = jnp.stack([lax.axis_index("x"), lax.axis_index("y"), lax.axis_index("c")]).astype(jnp.int32)

    sh, w_in_g, tab = _cast_and_gather_first(
        [to_blocks(k, weights[name]) for k, (name, _, _, _) in enumerate(BIG_WEIGHTS)], rel_bias.T)
    w_in_t = full(w_in_g)

    (w_out_g,) = _on_sequencer("gather_out", "sibling+chips", _gather_rider([sh[OUT]]))
    wg_g, wu_g = _on_sequencer("gather_gate_up", "sibling+chips", _gather_rider([sh[GATE], sh[UP]]))
    wd_g, w_pg_g, w_pp_g = _on_sequencer("gather_down_ple", "sibling+chips", _gather_rider([sh[DOWN], sh[PG], sh[PP]]))
    (zqk, qn, kn, v, u) = _in_proj(xs, g_attn_norm, w_in_t, gq_t, gk_t)
    (a,) = _attn_fwd(qn, kn, v, tab, attn_sinks)
    w_out_f = full(w_out_g)
    (h1, hn2, m_out) = _mix_out(u, a, xs, w_out_f, wp, pool_scale, g_ffn_norm)
    wg_t, wu_t = full(wg_g), full(wu_g)
    (gt, up) = _ffn_up(hn2, wg_t, wu_t)
    w_down_f = full(wd_g)

    partial, from_sibling, sums, landed = [None] * N_BIG, [None] * N_BIG, [None] * N_BIG, [None] * N_BIG

    def to_sibling(name, ks, grads):
        for k, g in zip(ks, grads):
            partial[k] = halves(k, g)
        got = _on_sequencer(name, "sibling", _sibling_rider([partial[k] for k in ks]))
        for k, g in zip(ks, got):
            from_sibling[k] = g

    def chip_sum(*ks):
        for k, s in zip(ks, _chip_sum(ks, place, [partial[k] for k in ks], [from_sibling[k] for k in ks])):
            sums[k] = s

    def to_chips(name, ks, small=None):
        got = _on_sequencer(name, "chips" if small is None else "all", _chips_rider([sums[k] for k in ks], small))
        for k, g in zip(ks, got):
            landed[k] = g
        return got[len(ks):]

    (loss_part, dh2, d_wpg, d_wpp, d_g_ple) = _ffn_down_ple(
        gt, up, h1, w_down_f, ps, target, g_ple_norm, full(w_pg_g), w_pp_g)
    to_sibling("sibling_ple", (PG, PP), (d_wpg, d_wpp))
    (dgt, dup, dh1, dh1b, d_g_ffn, d_wd) = _ffn_bwd_act(dh2, h1, gt, up, g_ffn_norm, wg_t, wu_t, w_down_f)
    to_sibling("sibling_down", (DOWN,), (d_wd,))
    chip_sum(PG, PP)
    to_chips("chips_ple", (PG, PP))
    chip_sum(DOWN)
    to_chips("chips_down", (DOWN,))
    (d_wg_t, d_wu_t) = _ffn_bwd_w(dgt, dup, hn2)
    to_sibling("sibling_gate_up", (GATE, UP), (d_wg_t, d_wu_t))
    _complete_before_next([landed[PG], landed[PP], landed[DOWN]])
    (da, du, d_wpool, d_scale, d_wo) = _mix_bwd(dh1b, u, a, m_out, w_out_f, wp, pool_scale)
    to_sibling("sibling_out", (OUT,), (d_wo,))
    chip_sum(GATE, UP)
    to_chips("chips_gate_up", (GATE, UP))
    (dqn, dkn, dv, dl_acc, d_sinks) = _attn_bwd(qn, kn, v, a, da, tab, attn_sinks)
    chip_sum(OUT)
    early, early_at = _pack_small([d_wpool.reshape(POOL_WIDTH, POOL_GROUP), d_scale, d_g_ffn, d_g_ple, loss_part[:, :1]])
    (early_all,) = to_chips("chips_out", (OUT,), early)
    (grad_x, d_win_t, d_g_attn, d_gq, d_gk) = _in_proj_bwd(dqn, dkn, dv, du, zqk, xs, dh1, g_attn_norm, gq_t, gk_t, w_in_t)
    to_sibling("sibling_in", (IN,), (d_win_t,))
    _complete_before_next([landed[OUT], landed[GATE], landed[UP], early_all])
    (d_rel_t,) = _bias_table_bwd(dl_acc)
    chip_sum(IN)
    late, late_at = _pack_small([d_g_attn, d_gq[:, :HEAD_DIM], d_gk[:, :HEAD_DIM], d_sinks[:, 0], d_rel_t])
    (late_all,) = to_chips("chips_in", (IN,), late)

    out = {"grad": {}, "delta": {}, "new_m": {}, "new_v": {}}
    for ks in ((PG, PP, DOWN), (OUT, GATE, UP), (IN,)):
        names = [BIG_WEIGHTS[k][0] for k in ks]
        results = _adamw_big(ks, place, [
            (partial[k], from_sibling[k], landed[k], to_blocks(k, weights[n]), to_blocks(k, m_in[n]),
             to_blocks(k, v_in[n])) for k, n in zip(ks, names)])
        for k, name, res in zip(ks, names, results):
            for kind, r in zip(("grad", "delta", "new_m", "new_v"), res):
                out[kind][name] = from_blocks(k, r)
    def as_rows(name, arr):
        return arr.T if name == "rel_bias" else arr.reshape(POOL_WIDTH, POOL_GROUP) if name == "w_pool" else arr

    def from_rows(name, arr):
        return arr.T if name == "rel_bias" else arr.reshape(w_pool.shape) if name == "w_pool" else arr

    grads_at = dict(w_pool=(0, early_at[0]), pool_scale=(0, early_at[1]), g_ffn_norm=(0, early_at[2]),
                    g_ple_norm=(0, early_at[3]), g_attn_norm=(1, late_at[0]), g_q=(1, late_at[1]), g_k=(1, late_at[2]),
                    attn_sinks=(1, late_at[3]), rel_bias=(1, late_at[4]))
    loss, updates = _small_update(
        [early_all, late_all], (0, early_at[4]), [grads_at[n] for n in SMALL_NAMES],
        [as_rows(n, weights[n]) for n in SMALL_NAMES], [as_rows(n, m_in[n]) for n in SMALL_NAMES],
        [as_rows(n, v_in[n]) for n in SMALL_NAMES])
    loss = loss.reshape(())
    n_small = len(SMALL_NAMES)
    for j, kind in enumerate(("grad", "delta", "new_m", "new_v")):
        for i, name in enumerate(SMALL_NAMES):
            out[kind][name] = from_rows(name, updates[j * n_small + i])

    _issued.clear()
    order = ("w_in", "w_out", "g_attn_norm", "g_q", "g_k", "attn_sinks", "rel_bias", "w_pool", "pool_scale",
             "g_ffn_norm", "w_gate", "w_up", "w_down", "g_ple_norm", "w_ple_gate", "w_ple_proj")
    return (loss, grad_x[None], *[out["grad"][n] for n in order], *[out["delta"][n] for n in order],
            *[out["new_m"][n] for n in order], *[out["new_v"][n] for n in order])
```

```python
import math

import jax
import jax.numpy as jnp
import numpy as np
from jax import lax
from jax.experimental import pallas as pl
from jax.experimental.pallas import tpu as pltpu
from jax.experimental.pallas import tpu_sc as plsc

F32 = jnp.float32
BF16 = jnp.bfloat16
MESH = pl.DeviceIdType.MESH

D_MODEL = 1024
HEAD_DIM = 64
ATTN_WIDTH = 512
KV_WIDTH = 128
POOL_WIDTH = 512
POOL_SIZES = (2, 4, 8, 16)
POOL_GROUP = 128
POOL_HALO = 16
IN_WIDTH = 1280
D_FF = 2816
PLE_DIM = 256
BLOCK = 128
N_BUCKETS = 32
MAX_DISTANCE = 128
EPS = 1e-6
N_DEV = 8
N_CHIPS = 4

ADAM_LR = 0.001
ADAM_B1 = 0.9
ADAM_B2 = 0.999
ADAM_EPS = 1e-08
ADAM_WD = 0.01
ADAM_STEP = 10

TOKEN_TILE = 512
FFN_BWD_TILE = 256
FF_CHUNK = 256
FFN_W_SLAB = 256
ATTN_STEP_BLOCKS = 4
HEADS_A = (0, 2, 5, 7)
HEADS_B = (1, 3, 4, 6)
SMALL_LANES = 128


def _nn(a, b):
    return jnp.dot(a, b, preferred_element_type=F32)


def _nt(a, b):
    return lax.dot_general(a, b, (((1,), (1,)), ((), ())), preferred_element_type=F32)


def _tn(a, b):
    return lax.dot_general(a, b, (((0,), (0,)), ((), ())), preferred_element_type=F32)


def _resident(shape):
    nd = len(shape)
    return pl.BlockSpec(shape, lambda i, _nd=nd: (0,) * _nd, pipeline_mode=pl.Buffered(1))


def _rows(tile, width):
    return pl.BlockSpec((tile, width), lambda i: (i, 0))


def _acc(shape):
    nd = len(shape)
    return pl.BlockSpec(shape, lambda i, _nd=nd: (0,) * _nd)


def _head_mean_matrix(width):
    idx = np.arange(width) // HEAD_DIM
    return jnp.asarray((idx[:, None] == idx[None, :]).astype(np.float32) / HEAD_DIM, dtype=BF16)


def _seg_mean(v, bmat):
    hi = v.astype(BF16)
    lo = (v - hi.astype(F32)).astype(BF16)
    return _nn(hi, bmat) + _nn(lo, bmat)


def _rms(x):
    return lax.rsqrt(jnp.mean(x * x, axis=-1, keepdims=True) + EPS)


def _rms_bwd(d_y, x, r, g):
    gy = d_y * g
    d_x = r * gy - x * (r * r * r) * jnp.mean(gy * x, axis=-1, keepdims=True)
    d_g = jnp.sum(d_y * (x * r), axis=0, keepdims=True)
    return d_x, d_g


def _lane_lo(shape):
    return lax.broadcasted_iota(jnp.int32, shape, 1) < HEAD_DIM


class _Rider:
    def __init__(self, inputs, out_shapes, sems, begin, end, middle=None):
        self.inputs, self.out_shapes, self.sems = list(inputs), list(out_shapes), list(sems)
        self.begin, self.middle, self.end = begin, middle, end


_issued = []


def _after_last(args, in_specs):
    extra = list(_issued)
    return list(args) + extra, list(in_specs) + [pl.BlockSpec(memory_space=pl.ANY)] * len(extra), len(extra)


def _mark_issued(out):
    _issued[:] = [out]


def _complete_before_next(arrays):
    _issued.extend(arrays)


def _call(body, args, *, name, grid, in_specs, out_specs, out_shape, scratch_shapes=()):
    n_args = len(args)
    args, in_specs, _ = _after_last(args, in_specs)

    def ordered(*refs):
        body(*refs[:n_args], *refs[len(args):])

    outs = pl.pallas_call(ordered, name=name, grid=grid, in_specs=in_specs, out_specs=list(out_specs),
                          out_shape=list(out_shape), scratch_shapes=list(scratch_shapes))(*args)
    _mark_issued(outs[0])
    return list(outs)


def _in_proj(x, g_attn, w_in_t, gq_t, gk_t):
    s = x.shape[0]
    ts = min(TOKEN_TILE, s)

    def body(x_ref, g_ref, w_ref, gq_ref, gk_ref, bq_ref, bk_ref, zqk_ref, qn_ref, kn_ref, v_ref, u_ref):
        xf = x_ref[...]
        hn = ((xf * _rms(xf)) * g_ref[...]).astype(BF16)
        z = _nt(hn, w_ref[...])
        q = z[:, :ATTN_WIDTH]
        k = z[:, ATTN_WIDTH:ATTN_WIDTH + KV_WIDTH]
        zqk_ref[...] = z[:, :ATTN_WIDTH + KV_WIDTH]
        rq = lax.rsqrt(_seg_mean(q * q, bq_ref[...]) + EPS)
        qn_ref[...] = ((q * rq) * gq_ref[...]).astype(BF16)
        rk = lax.rsqrt(_seg_mean(k * k, bk_ref[...]) + EPS)
        kn_ref[...] = ((k * rk) * gk_ref[...]).astype(BF16)
        v_ref[...] = z[:, ATTN_WIDTH + KV_WIDTH:ATTN_WIDTH + 2 * KV_WIDTH].astype(BF16)
        u_ref[...] = z[:, ATTN_WIDTH + 2 * KV_WIDTH:]

    return _call(
        body,
        (x, g_attn, w_in_t, gq_t, gk_t, _head_mean_matrix(ATTN_WIDTH), _head_mean_matrix(KV_WIDTH)),
        name="in_proj",
        grid=(s // ts,),
        in_specs=[
            _rows(ts, D_MODEL),
            _resident((1, D_MODEL)),
            _resident((IN_WIDTH, D_MODEL)),
            _resident((1, ATTN_WIDTH)),
            _resident((1, KV_WIDTH)),
            _resident((ATTN_WIDTH, ATTN_WIDTH)),
            _resident((KV_WIDTH, KV_WIDTH)),
        ],
        out_specs=[
            _rows(ts, ATTN_WIDTH + KV_WIDTH),
            _rows(ts, ATTN_WIDTH),
            _rows(ts, KV_WIDTH),
            _rows(ts, KV_WIDTH),
            _rows(ts, POOL_WIDTH),
        ],
        out_shape=[
            jax.ShapeDtypeStruct((s, ATTN_WIDTH + KV_WIDTH), F32),
            jax.ShapeDtypeStruct((s, ATTN_WIDTH), BF16),
            jax.ShapeDtypeStruct((s, KV_WIDTH), BF16),
            jax.ShapeDtypeStruct((s, KV_WIDTH), BF16),
            jax.ShapeDtypeStruct((s, POOL_WIDTH), F32),
        ],
    )


def _bucket_ranges():
    n = np.arange(MAX_DISTANCE)
    max_exact = N_BUCKETS // 2
    nf = np.maximum(n, 1).astype(np.float64)
    large = max_exact + (np.log(nf / max_exact) / math.log(MAX_DISTANCE / max_exact) * (N_BUCKETS - max_exact)).astype(np.int64)
    bucket = np.where(n < max_exact, n, np.minimum(large, N_BUCKETS - 1))
    out = []
    for b in range(N_BUCKETS):
        idx = np.nonzero(bucket == b)[0]
        out.append((int(idx.min()), int(idx.max()) + 1))
    return out


def _band_distance():
    i = lax.broadcasted_iota(jnp.int32, (BLOCK, 2 * BLOCK), 0)
    j = lax.broadcasted_iota(jnp.int32, (BLOCK, 2 * BLOCK), 1)
    return BLOCK + i - j


BIAS_TABLE_SHAPE = (2, 4 * BLOCK, 2 * BLOCK)


def _write_bias_table(rb_ref, tab_ref):
    d = _band_distance()
    for half, heads in enumerate((HEADS_A, HEADS_B)):
        for slot, h in enumerate(heads):
            t = jnp.full((BLOCK, 2 * BLOCK), -jnp.inf, F32)
            for b, (lo, hi) in enumerate(_bucket_ranges()):
                t = jnp.where((d >= lo) & (d < hi), rb_ref[h, b], t)
            tab_ref[half, slot * BLOCK:(slot + 1) * BLOCK, :] = t


def _bias_table_bwd(dl_acc):
    ranges = _bucket_ranges()
    n_heads = len(HEADS_A) + len(HEADS_B)

    def body(dl_ref, out_ref):
        d = _band_distance()
        row = lax.broadcasted_iota(jnp.int32, (n_heads, SMALL_LANES), 0)
        lane = lax.broadcasted_iota(jnp.int32, (n_heads, SMALL_LANES), 1)
        out = jnp.zeros((n_heads, SMALL_LANES), F32)
        for b, (lo, hi) in enumerate(ranges):
            in_bucket = (d >= lo) & (d < hi)
            for half, heads in enumerate((HEADS_A, HEADS_B)):
                for slot, h in enumerate(heads):
                    g = dl_ref[half, slot * BLOCK:(slot + 1) * BLOCK, :]
                    part = jnp.sum(jnp.where(in_bucket, g, 0.0), axis=0, keepdims=True)
                    tot = jnp.sum(part, axis=1, keepdims=True)
                    out = jnp.where((row == h) & (lane == b), tot, out)
        out_ref[...] = out

    return _call(
        body,
        (dl_acc,),
        name="bias_table_bwd",
        grid=(1,),
        in_specs=[_acc((2, 4 * BLOCK, 2 * BLOCK))],
        out_specs=[_acc((n_heads, SMALL_LANES))],
        out_shape=[jax.ShapeDtypeStruct((n_heads, SMALL_LANES), F32)],
    )


def _stack_heads(pairs, lo_mask):
    zero = jnp.zeros_like(pairs[0])
    lo = [jnp.where(lo_mask, t, zero) for t in pairs]
    hi = [jnp.where(lo_mask, zero, t) for t in pairs]
    return (jnp.concatenate([lo[0], lo[1], hi[2], hi[3]], axis=0),
            jnp.concatenate([hi[0], hi[1], lo[2], lo[3]], axis=0))


def _unstack_heads(out_a, out_b, lo_mask):
    t = lambda x, r: x[r * BLOCK:(r + 1) * BLOCK, :]
    return [
        jnp.where(lo_mask, t(out_a, 0), t(out_b, 0)),
        jnp.where(lo_mask, t(out_a, 1), t(out_b, 1)),
        jnp.where(lo_mask, t(out_b, 2), t(out_a, 2)),
        jnp.where(lo_mask, t(out_b, 3), t(out_a, 3)),
    ]


def _sink_column(sink_ref, heads):
    row = lax.broadcasted_iota(jnp.int32, (4 * BLOCK, 1), 0)
    col = jnp.full((4 * BLOCK, 1), sink_ref[0, heads[3]], F32)
    for slot in (2, 1, 0):
        col = jnp.where(row < (slot + 1) * BLOCK, sink_ref[0, heads[slot]], col)
    return col


def _band_scores(q_stack, keys, tab, first_block):
    s = _nt(q_stack, keys) * (HEAD_DIM ** -0.5) + tab
    if first_block is not None:
        col = lax.broadcasted_iota(jnp.int32, s.shape, 1)
        s = jnp.where(jnp.logical_and(first_block, col < BLOCK), -jnp.inf, s)
    return s


def _softmax_with_sink(s, sink):
    m = jnp.maximum(jnp.max(s, axis=-1, keepdims=True), sink)
    e = jnp.exp(s - m)
    e_sink = jnp.exp(sink - m)
    den = jnp.sum(e, axis=-1, keepdims=True) + e_sink
    return e / den, e_sink / den


def _band_probs(q_stack, keys, tab, sink, first_block):
    return _softmax_with_sink(_band_scores(q_stack, keys, tab, first_block), sink)


def _attn_specs(n_groups):
    group = lambda n: (jnp.minimum(n, n_groups - 1), 0)
    prev = lambda n: (jnp.maximum(jnp.minimum(n, n_groups - 1) * ATTN_STEP_BLOCKS - 1, 0), 0)
    return group, prev


def _band(prev_ref, group_ref, b):
    rows = lambda i: group_ref[i * BLOCK:(i + 1) * BLOCK, :]
    band = jnp.concatenate([prev_ref[...] if b == 0 else rows(b - 1), rows(b)], axis=0)
    return band, pltpu.roll(band, HEAD_DIM, 1)


def _attn_fwd(qn, kn, v, tab, sinks):
    s = qn.shape[0]
    n_groups = s // (ATTN_STEP_BLOCKS * BLOCK)
    group, prev = _attn_specs(n_groups)
    rows = ATTN_STEP_BLOCKS * BLOCK

    def body(sink_ref, q_ref, kc_ref, kp_ref, vc_ref, vp_ref, tab_ref, o_ref):
        first = pl.program_id(0) == 0
        lo_mask = _lane_lo((BLOCK, BLOCK))
        for b in range(ATTN_STEP_BLOCKS):
            at = slice(b * BLOCK, (b + 1) * BLOCK)
            kk, kk_sw = _band(kp_ref, kc_ref, b)
            vv, vv_sw = _band(vp_ref, vc_ref, b)
            q_a, q_b = _stack_heads([q_ref[at, p * BLOCK:(p + 1) * BLOCK] for p in range(4)], lo_mask)
            no_prev = first if b == 0 else None
            p_a, _ = _band_probs(q_a, kk, tab_ref[0], _sink_column(sink_ref, HEADS_A), no_prev)
            p_b, _ = _band_probs(q_b, kk_sw, tab_ref[1], _sink_column(sink_ref, HEADS_B), no_prev)
            out = _unstack_heads(_nn(p_a.astype(BF16), vv), _nn(p_b.astype(BF16), vv_sw), lo_mask)
            for p in range(4):
                o_ref[at, p * BLOCK:(p + 1) * BLOCK] = out[p].astype(BF16)

    return _call(
        body,
        (sinks, qn, kn, kn, v, v, tab),
        name="attn_fwd",
        grid=(n_groups,),
        in_specs=[
            pl.BlockSpec(memory_space=pltpu.SMEM),
            pl.BlockSpec((rows, ATTN_WIDTH), group),
            pl.BlockSpec((rows, KV_WIDTH), group),
            pl.BlockSpec((BLOCK, KV_WIDTH), prev),
            pl.BlockSpec((rows, KV_WIDTH), group),
            pl.BlockSpec((BLOCK, KV_WIDTH), prev),
            _resident((2, 4 * BLOCK, 2 * BLOCK)),
        ],
        out_specs=[pl.BlockSpec((rows, ATTN_WIDTH), group)],
        out_shape=[jax.ShapeDtypeStruct((s, ATTN_WIDTH), BF16)],
    )


def _pooled(u_tile, u_halo, tile_index, tile_rows):
    halo = jnp.where(tile_index > 0, u_halo, 0.0)
    ext = jnp.concatenate([halo, u_tile], axis=0)
    sums = []
    acc = ext
    for shift in (1, 2, 4, 8):
        acc = acc + pltpu.roll(acc, shift, 0)
        sums.append(acc)
    t = tile_index * tile_rows + lax.broadcasted_iota(jnp.int32, (tile_rows, 1), 0)
    out = []
    for g, w in enumerate(POOL_SIZES):
        lanes = slice(g * POOL_GROUP, (g + 1) * POOL_GROUP)
        cnt = jnp.minimum(t + 1, w).astype(F32)
        out.append(sums[g][POOL_HALO:, lanes] / cnt - u_tile[:, lanes])
    return out


def _halo_before(tile):
    return lambda i: (jnp.maximum(i * (tile // POOL_HALO) - 1, 0), 0)


def _mix_out(u, a, x, w_out, w_pool, pool_scale, g_ffn):
    s = x.shape[0]
    ts = min(TOKEN_TILE, s)

    def body(u_ref, uh_ref, a_ref, x_ref, wo_ref, wp_ref, sc_ref, g_ref, h1_ref, hn_ref, m_ref):
        i = pl.program_id(0)
        pooled = _pooled(u_ref[...], uh_ref[...], i, ts)
        for g in range(len(POOL_SIZES)):
            lanes = slice(g * POOL_GROUP, (g + 1) * POOL_GROUP)
            y = _nn(pooled[g].astype(BF16), wp_ref[g].astype(BF16))
            m_ref[:, lanes] = (y * sc_ref[:, lanes]).astype(BF16)
        h1 = x_ref[...] + _nn(a_ref[...], wo_ref[:ATTN_WIDTH, :]) + _nn(m_ref[...], wo_ref[ATTN_WIDTH:, :])
        h1_ref[...] = h1
        hn_ref[...] = ((h1 * _rms(h1)) * g_ref[...]).astype(BF16)

    return _call(
        body,
        (u, u, a, x, w_out, w_pool, pool_scale, g_ffn),
        name="mix_out",
        grid=(s // ts,),
        in_specs=[
            _rows(ts, POOL_WIDTH),
            pl.BlockSpec((POOL_HALO, POOL_WIDTH), _halo_before(ts)),
            _rows(ts, ATTN_WIDTH),
            _rows(ts, D_MODEL),
            _resident((D_MODEL, D_MODEL)),
            _resident((len(POOL_SIZES), POOL_GROUP, POOL_GROUP)),
            _resident((1, POOL_WIDTH)),
            _resident((1, D_MODEL)),
        ],
        out_specs=[_rows(ts, D_MODEL), _rows(ts, D_MODEL), _rows(ts, POOL_WIDTH)],
        out_shape=[
            jax.ShapeDtypeStruct((s, D_MODEL), F32),
            jax.ShapeDtypeStruct((s, D_MODEL), BF16),
            jax.ShapeDtypeStruct((s, POOL_WIDTH), BF16),
        ],
    )


def _ffn_up(hn2, wg_t, wu_t):
    s = hn2.shape[0]
    ts = min(TOKEN_TILE, s)

    def body(hn_ref, wg_ref, wu_ref, gt_ref, up_ref):
        hn = hn_ref[...]
        for c in range(D_FF // FF_CHUNK):
            cols = slice(c * FF_CHUNK, (c + 1) * FF_CHUNK)
            gt_ref[:, cols] = _nt(hn, wg_ref[cols, :]).astype(BF16)
            up_ref[:, cols] = _nt(hn, wu_ref[cols, :]).astype(BF16)

    return _call(
        body,
        (hn2, wg_t, wu_t),
        name="ffn_up",
        grid=(s // ts,),
        in_specs=[_rows(ts, D_MODEL), _resident((D_FF, D_MODEL)), _resident((D_FF, D_MODEL))],
        out_specs=[_rows(ts, D_FF), _rows(ts, D_FF)],
        out_shape=[jax.ShapeDtypeStruct((s, D_FF), BF16), jax.ShapeDtypeStruct((s, D_FF), BF16)],
    )


def _silu_mul(gt, up):
    return (gt * jax.nn.sigmoid(gt)) * up


def _ffn_down_ple(gt, up, h1, w_down, p, target, g_ple, w_pg, w_pp):
    s = h1.shape[0]
    ts = min(TOKEN_TILE, s)
    blk = D_MODEL // N_DEV

    def body(gt_ref, up_ref, h1_ref, wd_ref, p_ref, t_ref, g_ref, wpg_ref, wpp_ref,
             loss_ref, dh_ref, dwpg_ref, dwpp_ref, dg_ref):
        @pl.when(pl.program_id(0) == 0)
        def _():
            loss_ref[...] = jnp.zeros_like(loss_ref)
            dwpg_ref[...] = jnp.zeros_like(dwpg_ref)
            dwpp_ref[...] = jnp.zeros_like(dwpp_ref)
            dg_ref[...] = jnp.zeros_like(dg_ref)

        h2v = h1_ref[...]
        for c in range(D_FF // FF_CHUNK):
            cols = slice(c * FF_CHUNK, (c + 1) * FF_CHUNK)
            act = _silu_mul(gt_ref[:, cols].astype(F32), up_ref[:, cols].astype(F32)).astype(BF16)
            h2v = _nn(act, wd_ref[cols, :]) + h2v
        r = _rms(h2v)
        hn = ((h2v * r) * g_ref[...]).astype(BF16)
        gate = jax.nn.sigmoid(_nn(hn, wpg_ref[...]))
        pb = p_ref[...].astype(BF16)
        pp = _nn(pb, jnp.concatenate([wpp_ref[j] for j in range(N_DEV)], axis=1))
        diff = (h2v + gate * pp) - t_ref[...]
        loss_ref[...] += jnp.sum(jnp.sum(diff * diff, axis=0, keepdims=True), axis=1, keepdims=True) * (0.5 / D_MODEL)
        dy = diff * (1.0 / D_MODEL)
        d_pp = (dy * gate).astype(BF16)
        d_pre = ((dy * pp) * (gate * (1.0 - gate))).astype(BF16)
        d_x, d_g = _rms_bwd(_nt(d_pre, wpg_ref[...]), h2v, r, g_ref[...])
        dg_ref[...] += d_g
        dh_ref[...] = dy + d_x
        d_wpp = _tn(pb, d_pp)
        for j in range(N_DEV):
            dwpp_ref[j] += d_wpp[:, j * blk:(j + 1) * blk]
        dwpg_ref[...] += _tn(hn, d_pre)

    return _call(
        body,
        (gt, up, h1, w_down, p, target, g_ple, w_pg, w_pp),
        name="ffn_down_ple",
        grid=(s // ts,),
        in_specs=[
            _rows(ts, D_FF),
            _rows(ts, D_FF),
            _rows(ts, D_MODEL),
            _resident((D_FF, D_MODEL)),
            _rows(ts, PLE_DIM),
            _rows(ts, D_MODEL),
            _resident((1, D_MODEL)),
            _resident((D_MODEL, D_MODEL)),
            _resident((N_DEV, PLE_DIM, blk)),
        ],
        out_specs=[
            _acc((1, SMALL_LANES)),
            _rows(ts, D_MODEL),
            _acc((D_MODEL, D_MODEL)),
            _acc((N_DEV, PLE_DIM, blk)),
            _acc((1, D_MODEL)),
        ],
        out_shape=[
            jax.ShapeDtypeStruct((1, SMALL_LANES), F32),
            jax.ShapeDtypeStruct((s, D_MODEL), F32),
            jax.ShapeDtypeStruct((D_MODEL, D_MODEL), F32),
            jax.ShapeDtypeStruct((N_DEV, PLE_DIM, blk), F32),
            jax.ShapeDtypeStruct((1, D_MODEL), F32),
        ],
    )


def _ffn_bwd_act(dh2, h1, gt, up, g_ffn, wg_t, wu_t, w_down):
    s = h1.shape[0]
    ts = min(FFN_BWD_TILE, s)

    def body(dh_ref, h1_ref, gt_ref, up_ref, g_ref, wg_ref, wu_ref, wd_ref,
             dgt_ref, dup_ref, dh1_ref, dh1b_ref, dg_ref, dwd_ref, act_ref):
        @pl.when(pl.program_id(0) == 0)
        def _():
            dg_ref[...] = jnp.zeros_like(dg_ref)
            dwd_ref[...] = jnp.zeros_like(dwd_ref)

        dhb = dh_ref[...].astype(BF16)
        d_hn = jnp.zeros((ts, D_MODEL), F32)
        for c in range(D_FF // FF_CHUNK):
            cols = slice(c * FF_CHUNK, (c + 1) * FF_CHUNK)
            d_act = _nt(dhb, wd_ref[cols, :])
            gtv = gt_ref[:, cols].astype(F32)
            upv = up_ref[:, cols].astype(F32)
            sg = jax.nn.sigmoid(gtv)
            silu = gtv * sg
            act_ref[:, cols] = (silu * upv).astype(BF16)
            d_up = (d_act * silu).astype(BF16)
            d_gt = ((d_act * upv) * (sg * (1.0 + gtv * (1.0 - sg)))).astype(BF16)
            dup_ref[:, cols] = d_up
            dgt_ref[:, cols] = d_gt
            d_hn = (_nn(d_gt, wg_ref[cols, :]) + _nn(d_up, wu_ref[cols, :])) + d_hn
        dwd_ref[...] += _tn(act_ref[...], dhb)
        h1v = h1_ref[...]
        d_x, d_g = _rms_bwd(d_hn, h1v, _rms(h1v), g_ref[...])
        dg_ref[...] += d_g
        dh1 = dh_ref[...] + d_x
        dh1_ref[...] = dh1
        dh1b_ref[...] = dh1.astype(BF16)

    return _call(
        body,
        (dh2, h1, gt, up, g_ffn, wg_t, wu_t, w_down),
        name="ffn_bwd_act",
        grid=(s // ts,),
        in_specs=[
            _rows(ts, D_MODEL),
            _rows(ts, D_MODEL),
            _rows(ts, D_FF),
            _rows(ts, D_FF),
            _resident((1, D_MODEL)),
            _resident((D_FF, D_MODEL)),
            _resident((D_FF, D_MODEL)),
            _resident((D_FF, D_MODEL)),
        ],
        out_specs=[
            _rows(ts, D_FF), _rows(ts, D_FF),
            _rows(ts, D_MODEL), _rows(ts, D_MODEL), _acc((1, D_MODEL)), _acc((D_FF, D_MODEL)),
        ],
        out_shape=[
            jax.ShapeDtypeStruct((s, D_FF), BF16),
            jax.ShapeDtypeStruct((s, D_FF), BF16),
            jax.ShapeDtypeStruct((s, D_MODEL), F32),
            jax.ShapeDtypeStruct((s, D_MODEL), BF16),
            jax.ShapeDtypeStruct((1, D_MODEL), F32),
            jax.ShapeDtypeStruct((D_FF, D_MODEL), F32),
        ],
        scratch_shapes=[pltpu.VMEM((ts, D_FF), BF16)],
    )


def _ffn_bwd_w(dgt, dup, hn2):
    s = hn2.shape[0]
    slab = pl.BlockSpec((s, FFN_W_SLAB), lambda i: (0, i))

    def body(dgt_ref, dup_ref, hn_ref, dwg_ref, dwu_ref):
        hn = hn_ref[...]
        dwg_ref[...] = _tn(dgt_ref[...], hn)
        dwu_ref[...] = _tn(dup_ref[...], hn)

    return _call(
        body,
        (dgt, dup, hn2),
        name="ffn_bwd_w",
        grid=(D_FF // FFN_W_SLAB,),
        in_specs=[slab, slab, _resident((s, D_MODEL))],
        out_specs=[_rows(FFN_W_SLAB, D_MODEL)] * 2,
        out_shape=[jax.ShapeDtypeStruct((D_FF, D_MODEL), F32)] * 2,
    )


def _mix_bwd(dh1b, u, a, m, w_out, w_pool, pool_scale):
    s = u.shape[0]
    ts = min(TOKEN_TILE, s)
    nt = s // ts
    halo_after = lambda i: (jnp.minimum((i + 1) * (ts // POOL_HALO), s // POOL_HALO - 1), 0)
    n_groups = len(POOL_SIZES)

    def body(dh_ref, dhn_ref, u_ref, uh_ref, a_ref, m_ref, wo_ref, wp_ref, sc_ref,
             da_ref, du_ref, dwp_ref, dsc_ref, dwo_ref):
        i = pl.program_id(0)

        @pl.when(i == 0)
        def _():
            dwp_ref[...] = jnp.zeros_like(dwp_ref)
            dsc_ref[...] = jnp.zeros_like(dsc_ref)
            dwo_ref[...] = jnp.zeros_like(dwo_ref)

        dh = dh_ref[...]
        dwo_ref[:ATTN_WIDTH, :] += _tn(a_ref[...], dh)
        dwo_ref[ATTN_WIDTH:, :] += _tn(m_ref[...], dh)
        da_ref[...] = _nt(dh, wo_ref[:ATTN_WIDTH, :])
        dh_next = jnp.where(i < nt - 1, dhn_ref[...], jnp.zeros_like(dhn_ref))
        dm_ext = _nt(jnp.concatenate([dh, dh_next], axis=0), wo_ref[ATTN_WIDTH:, :])
        pooled = _pooled(u_ref[...], uh_ref[...], i, ts)
        t_ext = i * ts + lax.broadcasted_iota(jnp.int32, (ts + POOL_HALO, 1), 0)
        for g, w in enumerate(POOL_SIZES):
            lanes = slice(g * POOL_GROUP, (g + 1) * POOL_GROUP)
            wp = wp_ref[g].astype(BF16)
            pg = pooled[g].astype(BF16)
            dm_g = dm_ext[:, lanes]
            dsc_ref[:, lanes] += jnp.sum(dm_g[:ts, :] * _nn(pg, wp), axis=0, keepdims=True)
            dy = (dm_g * sc_ref[:, lanes]).astype(BF16)
            dwp_ref[g] += _tn(pg, dy[:ts, :])
            d_pool = _nt(dy, wp)
            acc = d_pool / jnp.minimum(t_ext + 1, w).astype(F32)
            shift = 1
            while shift < w:
                acc = acc + pltpu.roll(acc, ts + POOL_HALO - shift, 0)
                shift *= 2
            du_ref[:, lanes] = (acc[:ts, :] - d_pool[:ts, :]).astype(BF16)

    return _call(
        body,
        (dh1b, dh1b, u, u, a, m, w_out, w_pool, pool_scale),
        name="mix_bwd",
        grid=(nt,),
        in_specs=[
            _rows(ts, D_MODEL),
            pl.BlockSpec((POOL_HALO, D_MODEL), halo_after),
            _rows(ts, POOL_WIDTH),
            pl.BlockSpec((POOL_HALO, POOL_WIDTH), _halo_before(ts)),
            _rows(ts, ATTN_WIDTH),
            _rows(ts, POOL_WIDTH),
            _resident((D_MODEL, D_MODEL)),
            _resident((n_groups, POOL_GROUP, POOL_GROUP)),
            _resident((1, POOL_WIDTH)),
        ],
        out_specs=[
            _rows(ts, ATTN_WIDTH),
            _rows(ts, POOL_WIDTH),
            _acc((n_groups, POOL_GROUP, POOL_GROUP)),
            _acc((1, POOL_WIDTH)),
            _acc((D_MODEL, D_MODEL)),
        ],
        out_shape=[
            jax.ShapeDtypeStruct((s, ATTN_WIDTH), F32),
            jax.ShapeDtypeStruct((s, POOL_WIDTH), BF16),
            jax.ShapeDtypeStruct((n_groups, POOL_GROUP, POOL_GROUP), F32),
            jax.ShapeDtypeStruct((1, POOL_WIDTH), F32),
            jax.ShapeDtypeStruct((D_MODEL, D_MODEL), F32),
        ],
    )


def _attn_bwd(qn, kn, v, a, da, tab, sinks):
    s = qn.shape[0]
    qb = ATTN_STEP_BLOCKS
    rows = qb * BLOCK
    n_groups = s // rows
    group, prev = _attn_specs(n_groups)
    done = lambda n: (jnp.maximum(n - 1, 0), 0)

    def body(sink_ref, q_ref, kc_ref, kp_ref, vc_ref, vp_ref, o_ref, do_ref, tab_ref,
             dq_ref, dk_ref, dv_ref, dl_ref, ds_ref, k_carry, v_carry, sink_acc):
        n = pl.program_id(0)

        @pl.when(n == 0)
        def _():
            dl_ref[...] = jnp.zeros_like(dl_ref)
            k_carry[...] = jnp.zeros_like(k_carry)
            v_carry[...] = jnp.zeros_like(v_carry)
            sink_acc[...] = jnp.zeros_like(sink_acc)

        @pl.when(n < n_groups)
        def _():
            first = n == 0
            lo_mask = _lane_lo((BLOCK, BLOCK))
            chains = [(b, half) for b in range(qb) for half in range(2)]
            tile = lambda ref, b, p: ref[b * BLOCK:(b + 1) * BLOCK, p * BLOCK:(p + 1) * BLOCK]
            keys = [_band(kp_ref, kc_ref, b) for b in range(qb)]
            vals = [_band(vp_ref, vc_ref, b) for b in range(qb)]
            q_st = [_stack_heads([tile(q_ref, b, p) for p in range(4)], lo_mask) for b in range(qb)]
            do_st = [_stack_heads([tile(do_ref, b, p) for p in range(4)], lo_mask) for b in range(qb)]
            o_st = [_stack_heads([tile(o_ref, b, p).astype(F32) for p in range(4)], lo_mask) for b in range(qb)]
            sink_col = [_sink_column(sink_ref, heads) for heads in (HEADS_A, HEADS_B)]
            scores = {(b, h): _band_scores(q_st[b][h], keys[b][h], tab_ref[h], first if b == 0 else None)
                      for b, h in chains}
            dob = {(b, h): do_st[b][h].astype(BF16) for b, h in chains}
            d_probs = {(b, h): _nt(dob[b, h], vals[b][h]) for b, h in chains}
            delta = {(b, h): jnp.sum(do_st[b][h] * o_st[b][h], axis=-1, keepdims=True) for b, h in chains}
            soft = {(b, h): _softmax_with_sink(scores[b, h], sink_col[h]) for b, h in chains}
            dl = {(b, h): soft[b, h][0] * (d_probs[b, h] - delta[b, h]) for b, h in chains}
            for b, h in chains:
                dl_ref[h] += dl[b, h]
                sink_acc[h] += soft[b, h][1] * delta[b, h]
            dsb = {(b, h): (dl[b, h] * (HEAD_DIM ** -0.5)).astype(BF16) for b, h in chains}
            dq_st = {(b, h): _nn(dsb[b, h], keys[b][h]) for b, h in chains}
            dk_parts = {(b, h): _tn(dsb[b, h], q_st[b][h]) for b, h in chains}
            dv_parts = {(b, h): _tn(soft[b, h][0].astype(BF16), dob[b, h]) for b, h in chains}
            for b in range(qb):
                dq = _unstack_heads(dq_st[b, 0], dq_st[b, 1], lo_mask)
                for p in range(4):
                    dq_ref[b * BLOCK:(b + 1) * BLOCK, p * BLOCK:(p + 1) * BLOCK] = dq[p]
            dks = [dk_parts[b, 0] + pltpu.roll(dk_parts[b, 1], HEAD_DIM, 1) for b in range(qb)]
            dvs = [dv_parts[b, 0] + pltpu.roll(dv_parts[b, 1], HEAD_DIM, 1) for b in range(qb)]
            last = slice((qb - 1) * BLOCK, qb * BLOCK)
            for parts, out_ref, carry in ((dks, dk_ref, k_carry), (dvs, dv_ref, v_carry)):
                out_ref[...] = carry[...]
                out_ref[last, :] += parts[0][:BLOCK, :]
                for b in range(qb):
                    own = parts[b][BLOCK:, :]
                    carry[b * BLOCK:(b + 1) * BLOCK, :] = own + parts[b + 1][:BLOCK, :] if b + 1 < qb else own

        @pl.when(n == n_groups)
        def _():
            dk_ref[...] = k_carry[...]
            dv_ref[...] = v_carry[...]
            for half, heads in enumerate((HEADS_A, HEADS_B)):
                for slot, h in enumerate(heads):
                    tot = jnp.sum(sink_acc[half, slot * BLOCK:(slot + 1) * BLOCK, :], axis=0, keepdims=True)
                    ds_ref[h:h + 1, :] = jnp.broadcast_to(-tot, (1, SMALL_LANES))

    return _call(
        body,
        (sinks, qn, kn, kn, v, v, a, da, tab),
        name="attn_bwd",
        grid=(n_groups + 1,),
        in_specs=[
            pl.BlockSpec(memory_space=pltpu.SMEM),
            pl.BlockSpec((rows, ATTN_WIDTH), group),
            pl.BlockSpec((rows, KV_WIDTH), group),
            pl.BlockSpec((BLOCK, KV_WIDTH), prev),
            pl.BlockSpec((rows, KV_WIDTH), group),
            pl.BlockSpec((BLOCK, KV_WIDTH), prev),
            pl.BlockSpec((rows, ATTN_WIDTH), group),
            pl.BlockSpec((rows, ATTN_WIDTH), group),
            _resident((2, 4 * BLOCK, 2 * BLOCK)),
        ],
        out_specs=[
            pl.BlockSpec((rows, ATTN_WIDTH), group),
            pl.BlockSpec((rows, KV_WIDTH), done),
            pl.BlockSpec((rows, KV_WIDTH), done),
            _acc((2, 4 * BLOCK, 2 * BLOCK)),
            _acc((N_DEV, SMALL_LANES)),
        ],
        out_shape=[
            jax.ShapeDtypeStruct((s, ATTN_WIDTH), F32),
            jax.ShapeDtypeStruct((s, KV_WIDTH), F32),
            jax.ShapeDtypeStruct((s, KV_WIDTH), F32),
            jax.ShapeDtypeStruct((2, 4 * BLOCK, 2 * BLOCK), F32),
            jax.ShapeDtypeStruct((N_DEV, SMALL_LANES), F32),
        ],
        scratch_shapes=[
            pltpu.VMEM((rows, KV_WIDTH), F32),
            pltpu.VMEM((rows, KV_WIDTH), F32),
            pltpu.VMEM((2, 4 * BLOCK, 1), F32),
        ],
    )


def _fold_heads(acc):
    t = acc + pltpu.roll(acc, HEAD_DIM, 1)
    out = t[:, :SMALL_LANES]
    for g in range(1, acc.shape[1] // SMALL_LANES):
        out = out + t[:, g * SMALL_LANES:(g + 1) * SMALL_LANES]
    return out


def _in_proj_bwd(dqn, dkn, dv, du, zqk, x, dh1, g_attn, gq_t, gk_t, w_in_t):
    s = x.shape[0]
    ts = min(TOKEN_TILE, s)
    nt = s // ts

    def head_norm_bwd(d_n, raw, g_t, bmat):
        r = lax.rsqrt(_seg_mean(raw * raw, bmat) + EPS)
        gy = d_n * g_t
        d_raw = r * gy - raw * (r * r * r) * _seg_mean(gy * raw, bmat)
        return d_raw, jnp.sum(d_n * (raw * r), axis=0, keepdims=True)

    def body(dqn_ref, dkn_ref, dv_ref, du_ref, zqk_ref, x_ref, dh1_ref, g_ref, gq_ref, gk_ref, w_ref, bq_ref, bk_ref,
             gx_ref, dw_ref, dg_ref, dgq_ref, dgk_ref, dz_ref, gq_acc, gk_acc):
        i = pl.program_id(0)

        @pl.when(i == 0)
        def _():
            dw_ref[...] = jnp.zeros_like(dw_ref)
            dg_ref[...] = jnp.zeros_like(dg_ref)
            gq_acc[...] = jnp.zeros_like(gq_acc)
            gk_acc[...] = jnp.zeros_like(gk_acc)

        d_q, d_gq = head_norm_bwd(dqn_ref[...], zqk_ref[:, :ATTN_WIDTH], gq_ref[...], bq_ref[...])
        d_k, d_gk = head_norm_bwd(dkn_ref[...], zqk_ref[:, ATTN_WIDTH:], gk_ref[...], bk_ref[...])
        gq_acc[...] += d_gq
        gk_acc[...] += d_gk
        dz_ref[:, :ATTN_WIDTH] = d_q.astype(BF16)
        dz_ref[:, ATTN_WIDTH:ATTN_WIDTH + KV_WIDTH] = d_k.astype(BF16)
        dz_ref[:, ATTN_WIDTH + KV_WIDTH:ATTN_WIDTH + 2 * KV_WIDTH] = dv_ref[...].astype(BF16)
        dz_ref[:, ATTN_WIDTH + 2 * KV_WIDTH:] = du_ref[...]
        dz = dz_ref[...]
        xf = x_ref[...]
        r = _rms(xf)
        hn = ((xf * r) * g_ref[...]).astype(BF16)
        d_x, d_g = _rms_bwd(_nn(dz, w_ref[...]), xf, r, g_ref[...])
        dg_ref[...] += d_g
        gx_ref[...] = dh1_ref[...] + d_x
        dw_ref[...] += _tn(dz, hn)

        @pl.when(i == nt - 1)
        def _():
            dgq_ref[...] = _fold_heads(gq_acc[...])
            dgk_ref[...] = _fold_heads(gk_acc[...])

    return _call(
        body,
        (dqn, dkn, dv, du, zqk, x, dh1, g_attn, gq_t, gk_t, w_in_t,
      _head_mean_matrix(ATTN_WIDTH), _head_mean_matrix(KV_WIDTH)),
        name="in_proj_bwd",
        grid=(nt,),
        in_specs=[
            _rows(ts, ATTN_WIDTH),
            _rows(ts, KV_WIDTH),
            _rows(ts, KV_WIDTH),
            _rows(ts, POOL_WIDTH),
            _rows(ts, ATTN_WIDTH + KV_WIDTH),
            _rows(ts, D_MODEL),
            _rows(ts, D_MODEL),
            _resident((1, D_MODEL)),
            _resident((1, ATTN_WIDTH)),
            _resident((1, KV_WIDTH)),
            _resident((IN_WIDTH, D_MODEL)),
            _resident((ATTN_WIDTH, ATTN_WIDTH)),
            _resident((KV_WIDTH, KV_WIDTH)),
        ],
        out_specs=[
            _rows(ts, D_MODEL),
            _acc((IN_WIDTH, D_MODEL)),
            _acc((1, D_MODEL)),
            _acc((1, SMALL_LANES)),
            _acc((1, SMALL_LANES)),
        ],
        out_shape=[
            jax.ShapeDtypeStruct((s, D_MODEL), F32),
            jax.ShapeDtypeStruct((IN_WIDTH, D_MODEL), F32),
            jax.ShapeDtypeStruct((1, D_MODEL), F32),
            jax.ShapeDtypeStruct((1, SMALL_LANES), F32),
            jax.ShapeDtypeStruct((1, SMALL_LANES), F32),
        ],
        scratch_shapes=[
            pltpu.VMEM((ts, IN_WIDTH), BF16),
            pltpu.VMEM((1, ATTN_WIDTH), F32),
            pltpu.VMEM((1, KV_WIDTH), F32),
        ],
    )


BIG_WEIGHTS = (
    ("w_in", True, IN_WIDTH // N_DEV, D_MODEL),
    ("w_out", False, D_MODEL // N_DEV, D_MODEL),
    ("w_gate", True, D_FF // N_DEV, D_MODEL),
    ("w_up", True, D_FF // N_DEV, D_MODEL),
    ("w_down", False, D_FF // N_DEV, D_MODEL),
    ("w_ple_gate", False, D_MODEL // N_DEV, D_MODEL),
    ("w_ple_proj", False, PLE_DIM, D_MODEL // N_DEV),
)
N_BIG = len(BIG_WEIGHTS)


def _place():
    x, y, c = lax.axis_index("x"), lax.axis_index("y"), lax.axis_index("c")
    chips = [(1 - x, y), (x, 1 - y), (1 - x, 1 - y)]
    return x, y, c, chips


class _Gather:
    def __init__(self, n):
        self.n = n
        self.sems = [pltpu.SemaphoreType.DMA((n, 7)), pltpu.SemaphoreType.DMA((n, 7)), pltpu.SemaphoreType.DMA((n,))]

    def _ctx(self, srcs, outs, sems):
        send_sems, recv_sems, local_sems = sems
        x, y, c, chips = _place()
        me, sibling = (x, y, c), (x, y, 1 - c)

        def block(k, owner):
            px, py, pc = owner
            return outs[k].at[4 * px + 2 * py + pc]

        def copy(k, idx, owner, to, mine=False):
            return pltpu.make_async_remote_copy(
                src_ref=srcs[k] if mine else block(k, owner), dst_ref=block(k, owner),
                send_sem=send_sems.at[k, idx], recv_sem=recv_sems.at[k, idx], device_id=to, device_id_type=MESH)

        def local(k):
            return pltpu.make_async_copy(srcs[k], block(k, me), local_sems.at[k])

        return c, chips, me, sibling, copy, local

    def begin(self, srcs, outs, sems):
        c, chips, me, sibling, copy, local = self._ctx(srcs, outs, sems)
        for k in range(self.n):
            local(k).start()
            copy(k, 0, me, sibling, mine=True).start()
            for j, chip in enumerate(chips):
                copy(k, 1 + j, me, (*chip, c), mine=True).start()

    def middle(self, srcs, outs, sems):
        c, chips, me, sibling, copy, local = self._ctx(srcs, outs, sems)
        for j, chip in enumerate(chips):
            for k in range(self.n):
                copy(k, 1 + j, (*chip, c), me).wait_recv()
                copy(k, 4 + j, (*chip, c), sibling).start()

    def end(self, srcs, outs, sems):
        c, chips, me, sibling, copy, local = self._ctx(srcs, outs, sems)
        for k in range(self.n):
            copy(k, 0, sibling, me).wait_recv()
            for j, chip in enumerate(chips):
                copy(k, 4 + j, (*chip, 1 - c), me).wait_recv()
        for k in range(self.n):
            copy(k, 0, me, sibling, mine=True).wait_send()
            for j, chip in enumerate(chips):
                copy(k, 1 + j, me, (*chip, c), mine=True).wait_send()
                copy(k, 4 + j, (*chip, c), sibling).wait_send()
            local(k).wait()


def _gather_rider(shards):
    g = _Gather(len(shards))
    shapes = [jax.ShapeDtypeStruct((N_DEV, *sh.shape), sh.dtype) for sh in shards]
    return _Rider(shards, shapes, g.sems, g.begin, g.end, g.middle)


def _cast_and_gather_first(shards, rel_bias_t):
    g = _Gather(1)
    any_spec = pl.BlockSpec(memory_space=pl.ANY)
    vmem = pl.BlockSpec(memory_space=pltpu.VMEM)

    def body(*refs):
        ins, rb_ref, outs = refs[:N_BIG], refs[N_BIG], refs[N_BIG + 1:2 * N_BIG + 1]
        gathered, tab_ref, sems = refs[2 * N_BIG + 1], refs[2 * N_BIG + 2], refs[2 * N_BIG + 3:]
        outs[0][...] = ins[0][...].astype(BF16)
        g.begin(outs[:1], [gathered], sems)
        for k in range(1, N_BIG):
            outs[k][...] = ins[k][...].astype(BF16)
        _write_bias_table(rb_ref, tab_ref)
        g.middle(outs[:1], [gathered], sems)
        g.end(outs[:1], [gathered], sems)

    res = pl.pallas_call(
        body,
        name="cast_and_gather_first",
        in_specs=[vmem] * N_BIG + [pl.BlockSpec(memory_space=pltpu.SMEM)],
        out_specs=[vmem] * N_BIG + [any_spec, vmem],
        out_shape=[jax.ShapeDtypeStruct((r, c), BF16) for _, _, r, c in BIG_WEIGHTS]
        + [jax.ShapeDtypeStruct((N_DEV, *BIG_WEIGHTS[0][2:]), BF16), jax.ShapeDtypeStruct(BIAS_TABLE_SHAPE, F32)],
        scratch_shapes=g.sems,
    )(*shards, rel_bias_t)
    return list(res[:N_BIG]), res[N_BIG], res[N_BIG + 1]


def _sibling_rider(grads):
    n = len(grads)

    def copies(gs, lands, sems):
        send_sems, recv_sems = sems
        x, y, c, _ = _place()
        return [
            pltpu.make_async_remote_copy(
                src_ref=gs[k].at[:, 1 - c], dst_ref=lands[k], send_sem=send_sems.at[k], recv_sem=recv_sems.at[k],
                device_id=(x, y, 1 - c), device_id_type=MESH)
            for k in range(n)
        ]

    def begin(gs, lands, sems):
        for cp in copies(gs, lands, sems):
            cp.start()

    def end(gs, lands, sems):
        for cp in copies(gs, lands, sems):
            cp.wait()

    shapes = [jax.ShapeDtypeStruct((N_CHIPS, *g.shape[2:]), F32) for g in grads]
    return _Rider(grads, shapes, [pltpu.SemaphoreType.DMA((n,)), pltpu.SemaphoreType.DMA((n,))], begin, end)


def _chip_of_relation(j, place):
    x, y = place[0], place[1]
    return jnp.where(j == 0, 2 * (1 - x) + y, jnp.where(j == 1, 2 * x + 1 - y, 2 * (1 - x) + 1 - y))


def _chip_sum(ks, place, grads, from_sibling):
    n = len(ks)
    shapes = [BIG_WEIGHTS[k][2:] for k in ks]
    operands, specs = [], []
    for (r, c), g, l in zip(shapes, grads, from_sibling):
        operands += [g, l]
        specs += [pl.BlockSpec((1, 1, r, c), lambda j, place: (_chip_of_relation(j, place), place[2], 0, 0)),
                  pl.BlockSpec((1, r, c), lambda j, place: (_chip_of_relation(j, place), 0, 0))]
    args, in_specs, _ = _after_last(operands, specs)

    def body(place_ref, *refs):
        ins, outs = refs[:2 * n], refs[len(args):]
        for i in range(n):
            outs[i][0] = (ins[2 * i][0, 0] + ins[2 * i + 1][0]).astype(BF16)

    outs = pl.pallas_call(
        body,
        name="chip_sum_" + "_".join(BIG_WEIGHTS[k][0] for k in ks),
        grid_spec=pltpu.PrefetchScalarGridSpec(
            num_scalar_prefetch=1,
            grid=(N_CHIPS - 1,),
            in_specs=in_specs,
            out_specs=[pl.BlockSpec((1, r, c), lambda j, place: (j, 0, 0)) for r, c in shapes],
        ),
        out_shape=[jax.ShapeDtypeStruct((N_CHIPS - 1, r, c), BF16) for r, c in shapes],
    )(place, *args)
    _mark_issued(outs[0])
    return list(outs)


def _chips_rider(to_send, small=None):
    n = len(to_send)
    inputs = list(to_send) + ([] if small is None else [small])
    shapes = [jax.ShapeDtypeStruct((3, *t.shape[1:]), BF16) for t in to_send]
    sems = [pltpu.SemaphoreType.DMA((max(n, 1), 3)), pltpu.SemaphoreType.DMA((max(n, 1), 3))]
    if small is not None:
        shapes.append(jax.ShapeDtypeStruct((N_DEV, *small.shape), F32))
        sems += [pltpu.SemaphoreType.DMA((7,)), pltpu.SemaphoreType.DMA((7,)), pltpu.SemaphoreType.DMA]

    def copies(ins, outs, sem_refs):
        x, y, c, chips = _place()
        out = []
        for k in range(n):
            for j, (px, py) in enumerate(chips):
                out.append(pltpu.make_async_remote_copy(
                    src_ref=ins[k].at[j], dst_ref=outs[k].at[j],
                    send_sem=sem_refs[0].at[k, j], recv_sem=sem_refs[1].at[k, j],
                    device_id=(px, py, c), device_id_type=MESH))
        local = None
        if small is not None:
            me = 4 * x + 2 * y + c
            local = pltpu.make_async_copy(ins[n], outs[n].at[me], sem_refs[4])
            rel = 0
            for fx in (0, 1):
                for fy in (0, 1):
                    for fc in (0, 1):
                        if (fx, fy, fc) != (0, 0, 0):
                            out.append(pltpu.make_async_remote_copy(
                                src_ref=ins[n], dst_ref=outs[n].at[me],
                                send_sem=sem_refs[2].at[rel], recv_sem=sem_refs[3].at[rel],
                                device_id=(x ^ fx, y ^ fy, c ^ fc), device_id_type=MESH))
                            rel += 1
        return out, local

    def begin(ins, outs, sem_refs):
        remote, local = copies(ins, outs, sem_refs)
        if local is not None:
            local.start()
        for cp in remote:
            cp.start()

    def end(ins, outs, sem_refs):
        remote, local = copies(ins, outs, sem_refs)
        for cp in remote:
            cp.wait()
        if local is not None:
            local.wait()

    return _Rider(inputs, shapes, sems, begin, end)


PEER_SETS = {"sibling": 1, "chips": 2, "sibling+chips": 3, "all": 4}


def _peers(pattern):
    x, y, c, chips = _place()
    sibling, others = [(x, y, 1 - c)], [(*chip, c) for chip in chips]
    if pattern == "all":
        return sibling + others + [(*chip, 1 - c) for chip in chips]
    return {"sibling": sibling, "chips": others, "sibling+chips": sibling + others}[pattern]


def _on_sequencer(name, pattern, rider):
    n_in, n_out = len(rider.inputs), len(rider.out_shapes)

    def body(*refs):
        ins, outs, sems = refs[:n_in], refs[n_in:n_in + n_out], refs[n_in + n_out:]
        peers = _peers(pattern)
        barrier = pltpu.get_barrier_semaphore()
        for peer in peers:
            pl.semaphore_signal(barrier, inc=1, device_id=peer, device_id_type=MESH)
        pl.semaphore_wait(barrier, len(peers))
        rider.begin(ins, outs, sems)
        if rider.middle is not None:
            rider.middle(ins, outs, sems)
        rider.end(ins, outs, sems)

    outs = pl.kernel(
        body,
        name=name,
        out_type=tuple(rider.out_shapes),
        mesh=plsc.ScalarSubcoreMesh(axis_name="sequencer", num_cores=1),
        scratch_types=tuple(rider.sems),
        compiler_params=pltpu.CompilerParams(collective_id=PEER_SETS[pattern]),
    )(*rider.inputs)
    return list(outs)


def _adamw(w, g, m, v):
    m = ADAM_B1 * m + (1.0 - ADAM_B1) * g
    v = ADAM_B2 * v + (1.0 - ADAM_B2) * jnp.square(g)
    m_hat = m / (1.0 - ADAM_B1 ** ADAM_STEP)
    v_hat = v / (1.0 - ADAM_B2 ** ADAM_STEP)
    delta = -ADAM_LR * (m_hat / (jnp.sqrt(v_hat) + ADAM_EPS) + ADAM_WD * w)
    return delta, m, v


def _adamw_big(ks, place, operands):
    n = len(ks)
    tiles = lambda i, place: (i, 0)
    in_specs, out_specs, out_shape = [], [], []
    for k in ks:
        _, _, r, c = BIG_WEIGHTS[k]
        tile = r // 2
        in_specs += [
            pl.BlockSpec((1, 1, tile, c), lambda i, place: (2 * place[0] + place[1], place[2], i, 0)),
            pl.BlockSpec((1, tile, c), lambda i, place: (2 * place[0] + place[1], i, 0)),
            pl.BlockSpec((3, tile, c), lambda i, place: (0, i, 0)),
        ] + [pl.BlockSpec((tile, c), tiles)] * 3
        out_specs += [pl.BlockSpec((tile, c), tiles)] * 4
        out_shape += [jax.ShapeDtypeStruct((r, c), F32)] * 4
    args, in_specs, _ = _after_last(sum((list(ops) for ops in operands), []), in_specs)

    def body(place_ref, *refs):
        ins, outs = refs[:6 * n], refs[len(args):]
        for i in range(n):
            mine_ref, sib_ref, land_ref, w_ref, m_ref, v_ref = ins[6 * i:6 * i + 6]
            g_ref, d_ref, nm_ref, nv_ref = outs[4 * i:4 * i + 4]
            g = mine_ref[0, 0] + sib_ref[0]
            g = ((g + land_ref[0].astype(F32)) + land_ref[1].astype(F32)) + land_ref[2].astype(F32)
            g_ref[...] = g
            d_ref[...], nm_ref[...], nv_ref[...] = _adamw(w_ref[...], g, m_ref[...], v_ref[...])

    outs = pl.pallas_call(
        body,
        name="adamw_" + "_".join(BIG_WEIGHTS[k][0] for k in ks),
        grid_spec=pltpu.PrefetchScalarGridSpec(
            num_scalar_prefetch=1, grid=(2,), in_specs=in_specs, out_specs=out_specs),
        out_shape=out_shape,
    )(place, *args)
    _mark_issued(outs[0])
    return [outs[4 * i:4 * i + 4] for i in range(n)]


def _pack_small(arrays):
    rows, offsets = [], []
    at = 0
    for a in arrays:
        if a.ndim != 2 or a.shape[1] != SMALL_LANES or a.shape[0] % 8:
            flat = a.reshape(-1)
            n_rows = -(-flat.shape[0] // (8 * SMALL_LANES)) * 8
            a = jnp.pad(flat, (0, n_rows * SMALL_LANES - flat.shape[0])).reshape(n_rows, SMALL_LANES)
        rows.append(a)
        offsets.append(at)
        at += a.shape[0]
    return jnp.concatenate(rows, axis=0), offsets


def _unpack_small(tot, at, shape):
    r, c = shape
    if r % 8 == 0:
        return tot[at:at + r, :c]
    assert r == 1
    if c <= SMALL_LANES:
        return tot[at:at + 1, :c]
    return jnp.concatenate([tot[at + j:at + j + 1, :] for j in range(c // SMALL_LANES)], axis=1)


def _small_update(packs, loss_at, grads_at, ws, ms, vs):
    n, n_packs = len(ws), len(packs)

    def body(*refs):
        pack_refs, refs = refs[:n_packs], refs[n_packs:]
        w_refs, m_refs, v_refs, loss_ref, outs = refs[:n], refs[n:2 * n], refs[2 * n:3 * n], refs[3 * n], refs[3 * n + 1:]
        tots = []
        for p_ref in pack_refs:
            tot = p_ref[0]
            for j in range(1, N_DEV):
                tot = tot + p_ref[j]
            tots.append(tot)
        loss_ref[...] = _unpack_small(tots[loss_at[0]], loss_at[1], (1, 1))
        for i, (pack, at) in enumerate(grads_at):
            g = _unpack_small(tots[pack], at, w_refs[i].shape)
            outs[i][...] = g
            outs[n + i][...], outs[2 * n + i][...], outs[3 * n + i][...] = _adamw(
                w_refs[i][...], g, m_refs[i][...], v_refs[i][...])

    shapes = [jax.ShapeDtypeStruct(w.shape, F32) for w in ws]
    outs = pl.pallas_call(body, name="small_update", out_shape=[jax.ShapeDtypeStruct((1, 1), F32)] + shapes * 4)(
        *packs, *ws, *ms, *vs)
    return outs[0], outs[1:]


SMALL_NAMES = ("g_attn_norm", "g_q", "g_k", "attn_sinks", "rel_bias", "w_pool", "pool_scale", "g_ffn_norm", "g_ple_norm")


def kernel(x, p, w_in, w_out, g_attn_norm, g_q, g_k, attn_sinks, rel_bias, w_pool, pool_scale, g_ffn_norm, w_gate, w_up, w_down, g_ple_norm, w_ple_gate, w_ple_proj, loss_target, m_w_in, m_w_out, m_g_attn_norm, m_g_q, m_g_k, m_attn_sinks, m_rel_bias, m_w_pool, m_pool_scale, m_g_ffn_norm, m_w_gate, m_w_up, m_w_down, m_g_ple_norm, m_w_ple_gate, m_w_ple_proj, v_w_in, v_w_out, v_g_attn_norm, v_g_q, v_g_k, v_attn_sinks, v_rel_bias, v_w_pool, v_pool_scale, v_g_ffn_norm, v_w_gate, v_w_up, v_w_down, v_g_ple_norm, v_w_ple_gate, v_w_ple_proj):
    weights = dict(w_in=w_in, w_out=w_out, g_attn_norm=g_attn_norm, g_q=g_q, g_k=g_k, attn_sinks=attn_sinks,
                   rel_bias=rel_bias, w_pool=w_pool, pool_scale=pool_scale, g_ffn_norm=g_ffn_norm, w_gate=w_gate,
                   w_up=w_up, w_down=w_down, g_ple_norm=g_ple_norm, w_ple_gate=w_ple_gate, w_ple_proj=w_ple_proj)
    m_in = dict(w_in=m_w_in, w_out=m_w_out, g_attn_norm=m_g_attn_norm, g_q=m_g_q, g_k=m_g_k, attn_sinks=m_attn_sinks,
                rel_bias=m_rel_bias, w_pool=m_w_pool, pool_scale=m_pool_scale, g_ffn_norm=m_g_ffn_norm, w_gate=m_w_gate,
                w_up=m_w_up, w_down=m_w_down, g_ple_norm=m_g_ple_norm, w_ple_gate=m_w_ple_gate, w_ple_proj=m_w_ple_proj)
    v_in = dict(w_in=v_w_in, w_out=v_w_out, g_attn_norm=v_g_attn_norm, g_q=v_g_q, g_k=v_g_k, attn_sinks=v_attn_sinks,
                rel_bias=v_rel_bias, w_pool=v_w_pool, pool_scale=v_pool_scale, g_ffn_norm=v_g_ffn_norm, w_gate=v_w_gate,
                w_up=v_w_up, w_down=v_w_down, g_ple_norm=v_g_ple_norm, w_ple_gate=v_w_ple_gate, w_ple_proj=v_w_ple_proj)

    _issued.clear()
    xs = x[0]
    ps = p[0, 0]
    target = loss_target[0]
    wp = w_pool[0]
    gq_t = jnp.tile(g_q, (1, ATTN_WIDTH // HEAD_DIM))
    gk_t = jnp.tile(g_k, (1, KV_WIDTH // HEAD_DIM))

    def to_blocks(k, arr):
        return jnp.swapaxes(arr[0], 0, 1) if BIG_WEIGHTS[k][1] else arr[0]

    def from_blocks(k, arr):
        return (jnp.swapaxes(arr, 0, 1) if BIG_WEIGHTS[k][1] else arr)[None]

    IN, OUT, GATE, UP, DOWN, PG, PP = range(N_BIG)
    full = lambda g: g.reshape(N_DEV * g.shape[1], g.shape[2])
    halves = lambda k, g: g.reshape(N_CHIPS, 2, *BIG_WEIGHTS[k][2:])
    place = jnp.stack([lax.axis_index("x"), lax.axis_index("y"), lax.axis_index("c")]).astype(jnp.int32)

    sh, w_in_g, tab = _cast_and_gather_first(
        [to_blocks(k, weights[name]) for k, (name, _, _, _) in enumerate(BIG_WEIGHTS)], rel_bias.T)
    w_in_t = full(w_in_g)

    (w_out_g,) = _on_sequencer("gather_out", "sibling+chips", _gather_rider([sh[OUT]]))
    wg_g, wu_g = _on_sequencer("gather_gate_up", "sibling+chips", _gather_rider([sh[GATE], sh[UP]]))
    wd_g, w_pg_g, w_pp_g = _on_sequencer("gather_down_ple", "sibling+chips", _gather_rider([sh[DOWN], sh[PG], sh[PP]]))
    (zqk, qn, kn, v, u) = _in_proj(xs, g_attn_norm, w_in_t, gq_t, gk_t)
    (a,) = _attn_fwd(qn, kn, v, tab, attn_sinks)
    w_out_f = full(w_out_g)
    (h1, hn2, m_out) = _mix_out(u, a, xs, w_out_f, wp, pool_scale, g_ffn_norm)
    wg_t, wu_t = full(wg_g), full(wu_g)
    (gt, up) = _ffn_up(hn2, wg_t, wu_t)
    w_down_f = full(wd_g)

    partial, from_sibling, sums, landed = [None] * N_BIG, [None] * N_BIG, [None] * N_BIG, [None] * N_BIG

    def to_sibling(name, ks, grads):
        for k, g in zip(ks, grads):
            partial[k] = halves(k, g)
        got = _on_sequencer(name, "sibling", _sibling_rider([partial[k] for k in ks]))
        for k, g in zip(ks, got):
            from_sibling[k] = g

    def chip_sum(*ks):
        for k, s in zip(ks, _chip_sum(ks, place, [partial[k] for k in ks], [from_sibling[k] for k in ks])):
            sums[k] = s

    def to_chips(name, ks, small=None):
        got = _on_sequencer(name, "chips" if small is None else "all", _chips_rider([sums[k] for k in ks], small))
        for k, g in zip(ks, got):
            landed[k] = g
        return got[len(ks):]

    (loss_part, dh2, d_wpg, d_wpp, d_g_ple) = _ffn_down_ple(
        gt, up, h1, w_down_f, ps, target, g_ple_norm, full(w_pg_g), w_pp_g)
    to_sibling("sibling_ple", (PG, PP), (d_wpg, d_wpp))
    (dgt, dup, dh1, dh1b, d_g_ffn, d_wd) = _ffn_bwd_act(dh2, h1, gt, up, g_ffn_norm, wg_t, wu_t, w_down_f)
    to_sibling("sibling_down", (DOWN,), (d_wd,))
    chip_sum(PG, PP)
    to_chips("chips_ple", (PG, PP))
    chip_sum(DOWN)
    to_chips("chips_down", (DOWN,))
    (d_wg_t, d_wu_t) = _ffn_bwd_w(dgt, dup, hn2)
    to_sibling("sibling_gate_up", (GATE, UP), (d_wg_t, d_wu_t))
    _complete_before_next([landed[PG], landed[PP], landed[DOWN]])
    (da, du, d_wpool, d_scale, d_wo) = _mix_bwd(dh1b, u, a, m_out, w_out_f, wp, pool_scale)
    to_sibling("sibling_out", (OUT,), (d_wo,))
    chip_sum(GATE, UP)
    to_chips("chips_gate_up", (GATE, UP))
    (dqn, dkn, dv, dl_acc, d_sinks) = _attn_bwd(qn, kn, v, a, da, tab, attn_sinks)
    chip_sum(OUT)
    to_chips("chips_out", (OUT,))
    early, early_at = _pack_small([d_wpool.reshape(POOL_WIDTH, POOL_GROUP), d_scale, d_g_ffn, d_g_ple, loss_part[:, :1]])
    (early_all,) = _on_sequencer("gather_early", "sibling+chips", _gather_rider([early]))
    (grad_x, d_win_t, d_g_attn, d_gq, d_gk) = _in_proj_bwd(dqn, dkn, dv, du, zqk, xs, dh1, g_attn_norm, gq_t, gk_t, w_in_t)
    to_sibling("sibling_in", (IN,), (d_win_t,))
    _complete_before_next([landed[OUT], landed[GATE], landed[UP], early_all])
    (d_rel_t,) = _bias_table_bwd(dl_acc)
    chip_sum(IN)
    late, late_at = _pack_small([d_g_attn, d_gq[:, :HEAD_DIM], d_gk[:, :HEAD_DIM], d_sinks[:, 0], d_rel_t])
    (late_all,) = to_chips("chips_in", (IN,), late)

    out = {"grad": {}, "delta": {}, "new_m": {}, "new_v": {}}
    for ks in ((PG, PP, DOWN), (OUT, GATE, UP), (IN,)):
        names = [BIG_WEIGHTS[k][0] for k in ks]
        results = _adamw_big(ks, place, [
            (partial[k], from_sibling[k], landed[k], to_blocks(k, weights[n]), to_blocks(k, m_in[n]),
             to_blocks(k, v_in[n])) for k, n in zip(ks, names)])
        for k, name, res in zip(ks, names, results):
            for kind, r in zip(("grad", "delta", "new_m", "new_v"), res):
                out[kind][name] = from_blocks(k, r)
    def as_rows(name, arr):
        return arr.T if name == "rel_bias" else arr.reshape(POOL_WIDTH, POOL_GROUP) if name == "w_pool" else arr

    def from_rows(name, arr):
        return arr.T if name == "rel_bias" else arr.reshape(w_pool.shape) if name == "w_pool" else arr

    grads_at = dict(w_pool=(0, early_at[0]), pool_scale=(0, early_at[1]), g_ffn_norm=(0, early_at[2]),
                    g_ple_norm=(0, early_at[3]), g_attn_norm=(1, late_at[0]), g_q=(1, late_at[1]), g_k=(1, late_at[2]),
                    attn_sinks=(1, late_at[3]), rel_bias=(1, late_at[4]))
    loss, updates = _small_update(
        [early_all, late_all], (0, early_at[4]), [grads_at[n] for n in SMALL_NAMES],
        [as_rows(n, weights[n]) for n in SMALL_NAMES], [as_rows(n, m_in[n]) for n in SMALL_NAMES],
        [as_rows(n, v_in[n]) for n in SMALL_NAMES])
    loss = loss.reshape(())
    n_small = len(SMALL_NAMES)
    for j, kind in enumerate(("grad", "delta", "new_m", "new_v")):
        for i, name in enumerate(SMALL_NAMES):
            out[kind][name] = from_rows(name, updates[j * n_small + i])

    _issued.clear()
    order = ("w_in", "w_out", "g_attn_norm", "g_q", "g_k", "attn_sinks", "rel_bias", "w_pool", "pool_scale",
             "g_ffn_norm", "w_gate", "w_up", "w_down", "g_ple_norm", "w_ple_gate", "w_ple_proj")
    return (loss, grad_x[None], *[out["grad"][n] for n in order], *[out["delta"][n] for n in order],
            *[out["new_m"][n] for n in order], *[out["new_v"][n] for n in order])
```

```python
import math

import jax
import jax.numpy as jnp
import numpy as np
from jax import lax
from jax.experimental import pallas as pl
from jax.experimental.pallas import tpu as pltpu
from jax.experimental.pallas import tpu_sc as plsc

F32 = jnp.float32
BF16 = jnp.bfloat16
MESH = pl.DeviceIdType.MESH

D_MODEL = 1024
HEAD_DIM = 64
ATTN_WIDTH = 512
KV_WIDTH = 128
POOL_WIDTH = 512
POOL_SIZES = (2, 4, 8, 16)
POOL_GROUP = 128
POOL_HALO = 16
IN_WIDTH = 1280
D_FF = 2816
PLE_DIM = 256
BLOCK = 128
N_BUCKETS = 32
MAX_DISTANCE = 128
EPS = 1e-6
N_DEV = 8
N_CHIPS = 4

ADAM_LR = 0.001
ADAM_B1 = 0.9
ADAM_B2 = 0.999
ADAM_EPS = 1e-08
ADAM_WD = 0.01
ADAM_STEP = 10

TOKEN_TILE = 512
FFN_BWD_TILE = 256
FF_CHUNK = 256
FFN_W_SLAB = 256
ATTN_STEP_BLOCKS = 4
HEADS_A = (0, 2, 5, 7)
HEADS_B = (1, 3, 4, 6)
SMALL_LANES = 128


def _nn(a, b):
    return jnp.dot(a, b, preferred_element_type=F32)


def _nt(a, b):
    return lax.dot_general(a, b, (((1,), (1,)), ((), ())), preferred_element_type=F32)


def _tn(a, b):
    return lax.dot_general(a, b, (((0,), (0,)), ((), ())), preferred_element_type=F32)


def _resident(shape):
    nd = len(shape)
    return pl.BlockSpec(shape, lambda i, _nd=nd: (0,) * _nd, pipeline_mode=pl.Buffered(1))


def _rows(tile, width):
    return pl.BlockSpec((tile, width), lambda i: (i, 0))


def _acc(shape):
    nd = len(shape)
    return pl.BlockSpec(shape, lambda i, _nd=nd: (0,) * _nd)


def _head_mean_matrix(width):
    idx = np.arange(width) // HEAD_DIM
    return jnp.asarray((idx[:, None] == idx[None, :]).astype(np.float32) / HEAD_DIM, dtype=BF16)


def _seg_mean(v, bmat):
    hi = v.astype(BF16)
    lo = (v - hi.astype(F32)).astype(BF16)
    return _nn(hi, bmat) + _nn(lo, bmat)


def _rms(x):
    return lax.rsqrt(jnp.mean(x * x, axis=-1, keepdims=True) + EPS)


def _rms_bwd(d_y, x, r, g):
    gy = d_y * g
    d_x = r * gy - x * (r * r * r) * jnp.mean(gy * x, axis=-1, keepdims=True)
    d_g = jnp.sum(d_y * (x * r), axis=0, keepdims=True)
    return d_x, d_g


def _lane_lo(shape):
    return lax.broadcasted_iota(jnp.int32, shape, 1) < HEAD_DIM


class _Rider:
    def __init__(self, inputs, out_shapes, sems, begin, end, middle=None):
        self.inputs, self.out_shapes, self.sems = list(inputs), list(out_shapes), list(sems)
        self.begin, self.middle, self.end = begin, middle, end


_issued = []


def _after_last(args, in_specs):
    extra = list(_issued)
    return list(args) + extra, list(in_specs) + [pl.BlockSpec(memory_space=pl.ANY)] * len(extra), len(extra)


def _mark_issued(out):
    _issued[:] = [out]


def _complete_before_next(arrays):
    _issued.extend(arrays)


def _call(body, args, *, name, grid, in_specs, out_specs, out_shape, scratch_shapes=()):
    n_args = len(args)
    args, in_specs, _ = _after_last(args, in_specs)

    def ordered(*refs):
        body(*refs[:n_args], *refs[len(args):])

    outs = pl.pallas_call(ordered, name=name, grid=grid, in_specs=in_specs, out_specs=list(out_specs),
                          out_shape=list(out_shape), scratch_shapes=list(scratch_shapes))(*args)
    _mark_issued(outs[0])
    return list(outs)


def _in_proj(x, g_attn, w_in_t, gq_t, gk_t):
    s = x.shape[0]
    ts = min(TOKEN_TILE, s)

    def body(x_ref, g_ref, w_ref, gq_ref, gk_ref, bq_ref, bk_ref, zqk_ref, qn_ref, kn_ref, v_ref, u_ref):
        xf = x_ref[...]
        hn = ((xf * _rms(xf)) * g_ref[...]).astype(BF16)
        z = _nt(hn, w_ref[...])
        q = z[:, :ATTN_WIDTH]
        k = z[:, ATTN_WIDTH:ATTN_WIDTH + KV_WIDTH]
        zqk_ref[...] = z[:, :ATTN_WIDTH + KV_WIDTH]
        rq = lax.rsqrt(_seg_mean(q * q, bq_ref[...]) + EPS)
        qn_ref[...] = ((q * rq) * gq_ref[...]).astype(BF16)
        rk = lax.rsqrt(_seg_mean(k * k, bk_ref[...]) + EPS)
        kn_ref[...] = ((k * rk) * gk_ref[...]).astype(BF16)
        v_ref[...] = z[:, ATTN_WIDTH + KV_WIDTH:ATTN_WIDTH + 2 * KV_WIDTH].astype(BF16)
        u_ref[...] = z[:, ATTN_WIDTH + 2 * KV_WIDTH:]

    return _call(
        body,
        (x, g_attn, w_in_t, gq_t, gk_t, _head_mean_matrix(ATTN_WIDTH), _head_mean_matrix(KV_WIDTH)),
        name="in_proj",
        grid=(s // ts,),
        in_specs=[
            _rows(ts, D_MODEL),
            _resident((1, D_MODEL)),
            _resident((IN_WIDTH, D_MODEL)),
            _resident((1, ATTN_WIDTH)),
            _resident((1, KV_WIDTH)),
            _resident((ATTN_WIDTH, ATTN_WIDTH)),
            _resident((KV_WIDTH, KV_WIDTH)),
        ],
        out_specs=[
            _rows(ts, ATTN_WIDTH + KV_WIDTH),
            _rows(ts, ATTN_WIDTH),
            _rows(ts, KV_WIDTH),
            _rows(ts, KV_WIDTH),
            _rows(ts, POOL_WIDTH),
        ],
        out_shape=[
            jax.ShapeDtypeStruct((s, ATTN_WIDTH + KV_WIDTH), F32),
            jax.ShapeDtypeStruct((s, ATTN_WIDTH), BF16),
            jax.ShapeDtypeStruct((s, KV_WIDTH), BF16),
            jax.ShapeDtypeStruct((s, KV_WIDTH), BF16),
            jax.ShapeDtypeStruct((s, POOL_WIDTH), F32),
        ],
    )


def _bucket_ranges():
    n = np.arange(MAX_DISTANCE)
    max_exact = N_BUCKETS // 2
    nf = np.maximum(n, 1).astype(np.float64)
    large = max_exact + (np.log(nf / max_exact) / math.log(MAX_DISTANCE / max_exact) * (N_BUCKETS - max_exact)).astype(np.int64)
    bucket = np.where(n < max_exact, n, np.minimum(large, N_BUCKETS - 1))
    out = []
    for b in range(N_BUCKETS):
        idx = np.nonzero(bucket == b)[0]
        out.append((int(idx.min()), int(idx.max()) + 1))
    return out


def _band_distance():
    i = lax.broadcasted_iota(jnp.int32, (BLOCK, 2 * BLOCK), 0)
    j = lax.broadcasted_iota(jnp.int32, (BLOCK, 2 * BLOCK), 1)
    return BLOCK + i - j


BIAS_TABLE_SHAPE = (2, 4 * BLOCK, 2 * BLOCK)


def _write_bias_table(rb_ref, tab_ref):
    d = _band_distance()
    for half, heads in enumerate((HEADS_A, HEADS_B)):
        for slot, h in enumerate(heads):
            t = jnp.full((BLOCK, 2 * BLOCK), -jnp.inf, F32)
            for b, (lo, hi) in enumerate(_bucket_ranges()):
                t = jnp.where((d >= lo) & (d < hi), rb_ref[h, b], t)
            tab_ref[half, slot * BLOCK:(slot + 1) * BLOCK, :] = t


def _bias_table_bwd(dl_acc):
    ranges = _bucket_ranges()
    n_heads = len(HEADS_A) + len(HEADS_B)

    def body(dl_ref, out_ref):
        d = _band_distance()
        row = lax.broadcasted_iota(jnp.int32, (n_heads, SMALL_LANES), 0)
        lane = lax.broadcasted_iota(jnp.int32, (n_heads, SMALL_LANES), 1)
        out = jnp.zeros((n_heads, SMALL_LANES), F32)
        for b, (lo, hi) in enumerate(ranges):
            in_bucket = (d >= lo) & (d < hi)
            for half, heads in enumerate((HEADS_A, HEADS_B)):
                for slot, h in enumerate(heads):
                    g = dl_ref[half, slot * BLOCK:(slot + 1) * BLOCK, :]
                    part = jnp.sum(jnp.where(in_bucket, g, 0.0), axis=0, keepdims=True)
                    tot = jnp.sum(part, axis=1, keepdims=True)
                    out = jnp.where((row == h) & (lane == b), tot, out)
        out_ref[...] = out

    return _call(
        body,
        (dl_acc,),
        name="bias_table_bwd",
        grid=(1,),
        in_specs=[_acc((2, 4 * BLOCK, 2 * BLOCK))],
        out_specs=[_acc((n_heads, SMALL_LANES))],
        out_shape=[jax.ShapeDtypeStruct((n_heads, SMALL_LANES), F32)],
    )


def _stack_heads(pairs, lo_mask):
    zero = jnp.zeros_like(pairs[0])
    lo = [jnp.where(lo_mask, t, zero) for t in pairs]
    hi = [jnp.where(lo_mask, zero, t) for t in pairs]
    return (jnp.concatenate([lo[0], lo[1], hi[2], hi[3]], axis=0),
            jnp.concatenate([hi[0], hi[1], lo[2], lo[3]], axis=0))


def _unstack_heads(out_a, out_b, lo_mask):
    t = lambda x, r: x[r * BLOCK:(r + 1) * BLOCK, :]
    return [
        jnp.where(lo_mask, t(out_a, 0), t(out_b, 0)),
        jnp.where(lo_mask, t(out_a, 1), t(out_b, 1)),
        jnp.where(lo_mask, t(out_b, 2), t(out_a, 2)),
        jnp.where(lo_mask, t(out_b, 3), t(out_a, 3)),
    ]


def _sink_column(sink_ref, heads):
    row = lax.broadcasted_iota(jnp.int32, (4 * BLOCK, 1), 0)
    col = jnp.full((4 * BLOCK, 1), sink_ref[0, heads[3]], F32)
    for slot in (2, 1, 0):
        col = jnp.where(row < (slot + 1) * BLOCK, sink_ref[0, heads[slot]], col)
    return col


def _band_scores(q_stack, keys, tab, first_block):
    s = _nt(q_stack, keys) * (HEAD_DIM ** -0.5) + tab
    if first_block is not None:
        col = lax.broadcasted_iota(jnp.int32, s.shape, 1)
        s = jnp.where(jnp.logical_and(first_block, col < BLOCK), -jnp.inf, s)
    return s


def _softmax_with_sink(s, sink):
    m = jnp.maximum(jnp.max(s, axis=-1, keepdims=True), sink)
    e = jnp.exp(s - m)
    e_sink = jnp.exp(sink - m)
    den = jnp.sum(e, axis=-1, keepdims=True) + e_sink
    return e / den, e_sink / den


def _band_probs(q_stack, keys, tab, sink, first_block):
    return _softmax_with_sink(_band_scores(q_stack, keys, tab, first_block), sink)


def _attn_specs(n_groups):
    group = lambda n: (jnp.minimum(n, n_groups - 1), 0)
    prev = lambda n: (jnp.maximum(jnp.minimum(n, n_groups - 1) * ATTN_STEP_BLOCKS - 1, 0), 0)
    return group, prev


def _band(prev_ref, group_ref, b):
    rows = lambda i: group_ref[i * BLOCK:(i + 1) * BLOCK, :]
    band = jnp.concatenate([prev_ref[...] if b == 0 else rows(b - 1), rows(b)], axis=0)
    return band, pltpu.roll(band, HEAD_DIM, 1)


def _attn_fwd(qn, kn, v, tab, sinks):
    s = qn.shape[0]
    n_groups = s // (ATTN_STEP_BLOCKS * BLOCK)
    group, prev = _attn_specs(n_groups)
    rows = ATTN_STEP_BLOCKS * BLOCK

    def body(sink_ref, q_ref, kc_ref, kp_ref, vc_ref, vp_ref, tab_ref, o_ref):
        first = pl.program_id(0) == 0
        lo_mask = _lane_lo((BLOCK, BLOCK))
        for b in range(ATTN_STEP_BLOCKS):
            at = slice(b * BLOCK, (b + 1) * BLOCK)
            kk, kk_sw = _band(kp_ref, kc_ref, b)
            vv, vv_sw = _band(vp_ref, vc_ref, b)
            q_a, q_b = _stack_heads([q_ref[at, p * BLOCK:(p + 1) * BLOCK] for p in range(4)], lo_mask)
            no_prev = first if b == 0 else None
            p_a, _ = _band_probs(q_a, kk, tab_ref[0], _sink_column(sink_ref, HEADS_A), no_prev)
            p_b, _ = _band_probs(q_b, kk_sw, tab_ref[1], _sink_column(sink_ref, HEADS_B), no_prev)
            out = _unstack_heads(_nn(p_a.astype(BF16), vv), _nn(p_b.astype(BF16), vv_sw), lo_mask)
            for p in range(4):
                o_ref[at, p * BLOCK:(p + 1) * BLOCK] = out[p].astype(BF16)

    return _call(
        body,
        (sinks, qn, kn, kn, v, v, tab),
        name="attn_fwd",
        grid=(n_groups,),
        in_specs=[
            pl.BlockSpec(memory_space=pltpu.SMEM),
            pl.BlockSpec((rows, ATTN_WIDTH), group),
            pl.BlockSpec((rows, KV_WIDTH), group),
            pl.BlockSpec((BLOCK, KV_WIDTH), prev),
            pl.BlockSpec((rows, KV_WIDTH), group),
            pl.BlockSpec((BLOCK, KV_WIDTH), prev),
            _resident((2, 4 * BLOCK, 2 * BLOCK)),
        ],
        out_specs=[pl.BlockSpec((rows, ATTN_WIDTH), group)],
        out_shape=[jax.ShapeDtypeStruct((s, ATTN_WIDTH), BF16)],
    )


def _pooled(u_tile, u_halo, tile_index, tile_rows):
    halo = jnp.where(tile_index > 0, u_halo, 0.0)
    ext = jnp.concatenate([halo, u_tile], axis=0)
    sums = []
    acc = ext
    for shift in (1, 2, 4, 8):
        acc = acc + pltpu.roll(acc, shift, 0)
        sums.append(acc)
    t = tile_index * tile_rows + lax.broadcasted_iota(jnp.int32, (tile_rows, 1), 0)
    out = []
    for g, w in enumerate(POOL_SIZES):
        lanes = slice(g * POOL_GROUP, (g + 1) * POOL_GROUP)
        cnt = jnp.minimum(t + 1, w).astype(F32)
        out.append(sums[g][POOL_HALO:, lanes] / cnt - u_tile[:, lanes])
    return out


def _halo_before(tile):
    return lambda i: (jnp.maximum(i * (tile // POOL_HALO) - 1, 0), 0)


def _mix_out(u, a, x, w_out, w_pool, pool_scale, g_ffn):
    s = x.shape[0]
    ts = min(TOKEN_TILE, s)

    def body(u_ref, uh_ref, a_ref, x_ref, wo_ref, wp_ref, sc_ref, g_ref, h1_ref, hn_ref, m_ref):
        i = pl.program_id(0)
        pooled = _pooled(u_ref[...], uh_ref[...], i, ts)
        for g in range(len(POOL_SIZES)):
            lanes = slice(g * POOL_GROUP, (g + 1) * POOL_GROUP)
            y = _nn(pooled[g].astype(BF16), wp_ref[g].astype(BF16))
            m_ref[:, lanes] = (y * sc_ref[:, lanes]).astype(BF16)
        h1 = x_ref[...] + _nn(a_ref[...], wo_ref[:ATTN_WIDTH, :]) + _nn(m_ref[...], wo_ref[ATTN_WIDTH:, :])
        h1_ref[...] = h1
        hn_ref[...] = ((h1 * _rms(h1)) * g_ref[...]).astype(BF16)

    return _call(
        body,
        (u, u, a, x, w_out, w_pool, pool_scale, g_ffn),
        name="mix_out",
        grid=(s // ts,),
        in_specs=[
            _rows(ts, POOL_WIDTH),
            pl.BlockSpec((POOL_HALO, POOL_WIDTH), _halo_before(ts)),
            _rows(ts, ATTN_WIDTH),
            _rows(ts, D_MODEL),
            _resident((D_MODEL, D_MODEL)),
            _resident((len(POOL_SIZES), POOL_GROUP, POOL_GROUP)),
            _resident((1, POOL_WIDTH)),
            _resident((1, D_MODEL)),
        ],
        out_specs=[_rows(ts, D_MODEL), _rows(ts, D_MODEL), _rows(ts, POOL_WIDTH)],
        out_shape=[
            jax.ShapeDtypeStruct((s, D_MODEL), F32),
            jax.ShapeDtypeStruct((s, D_MODEL), BF16),
            jax.ShapeDtypeStruct((s, POOL_WIDTH), BF16),
        ],
    )


def _ffn_up(hn2, wg_t, wu_t):
    s = hn2.shape[0]
    ts = min(TOKEN_TILE, s)

    def body(hn_ref, wg_ref, wu_ref, gt_ref, up_ref):
        hn = hn_ref[...]
        for c in range(D_FF // FF_CHUNK):
            cols = slice(c * FF_CHUNK, (c + 1) * FF_CHUNK)
            gt_ref[:, cols] = _nt(hn, wg_ref[cols, :]).astype(BF16)
            up_ref[:, cols] = _nt(hn, wu_ref[cols, :]).astype(BF16)

    return _call(
        body,
        (hn2, wg_t, wu_t),
        name="ffn_up",
        grid=(s // ts,),
        in_specs=[_rows(ts, D_MODEL), _resident((D_FF, D_MODEL)), _resident((D_FF, D_MODEL))],
        out_specs=[_rows(ts, D_FF), _rows(ts, D_FF)],
        out_shape=[jax.ShapeDtypeStruct((s, D_FF), BF16), jax.ShapeDtypeStruct((s, D_FF), BF16)],
    )


def _silu_mul(gt, up):
    return (gt * jax.nn.sigmoid(gt)) * up


def _ffn_down_ple(gt, up, h1, w_down, p, target, g_ple, w_pg, w_pp):
    s = h1.shape[0]
    ts = min(TOKEN_TILE, s)
    blk = D_MODEL // N_DEV

    def body(gt_ref, up_ref, h1_ref, wd_ref, p_ref, t_ref, g_ref, wpg_ref, wpp_ref,
             loss_ref, dh_ref, dwpg_ref, dwpp_ref, dg_ref):
        @pl.when(pl.program_id(0) == 0)
        def _():
            loss_ref[...] = jnp.zeros_like(loss_ref)
            dwpg_ref[...] = jnp.zeros_like(dwpg_ref)
            dwpp_ref[...] = jnp.zeros_like(dwpp_ref)
            dg_ref[...] = jnp.zeros_like(dg_ref)

        h2v = h1_ref[...]
        for c in range(D_FF // FF_CHUNK):
            cols = slice(c * FF_CHUNK, (c + 1) * FF_CHUNK)
            act = _silu_mul(gt_ref[:, cols].astype(F32), up_ref[:, cols].astype(F32)).astype(BF16)
            h2v = _nn(act, wd_ref[cols, :]) + h2v
        r = _rms(h2v)
        hn = ((h2v * r) * g_ref[...]).astype(BF16)
        gate = jax.nn.sigmoid(_nn(hn, wpg_ref[...]))
        pb = p_ref[...].astype(BF16)
        pp = _nn(pb, jnp.concatenate([wpp_ref[j] for j in range(N_DEV)], axis=1))
        diff = (h2v + gate * pp) - t_ref[...]
        loss_ref[...] += jnp.sum(jnp.sum(diff * diff, axis=0, keepdims=True), axis=1, keepdims=True) * (0.5 / D_MODEL)
        dy = diff * (1.0 / D_MODEL)
        d_pp = (dy * gate).astype(BF16)
        d_pre = ((dy * pp) * (gate * (1.0 - gate))).astype(BF16)
        d_x, d_g = _rms_bwd(_nt(d_pre, wpg_ref[...]), h2v, r, g_ref[...])
        dg_ref[...] += d_g
        dh_ref[...] = dy + d_x
        d_wpp = _tn(pb, d_pp)
        for j in range(N_DEV):
            dwpp_ref[j] += d_wpp[:, j * blk:(j + 1) * blk]
        dwpg_ref[...] += _tn(hn, d_pre)

    return _call(
        body,
        (gt, up, h1, w_down, p, target, g_ple, w_pg, w_pp),
        name="ffn_down_ple",
        grid=(s // ts,),
        in_specs=[
            _rows(ts, D_FF),
            _rows(ts, D_FF),
            _rows(ts, D_MODEL),
            _resident((D_FF, D_MODEL)),
            _rows(ts, PLE_DIM),
            _rows(ts, D_MODEL),
            _resident((1, D_MODEL)),
            _resident((D_MODEL, D_MODEL)),
            _resident((N_DEV, PLE_DIM, blk)),
        ],
        out_specs=[
            _acc((1, SMALL_LANES)),
            _rows(ts, D_MODEL),
            _acc((D_MODEL, D_MODEL)),
            _acc((N_DEV, PLE_DIM, blk)),
            _acc((1, D_MODEL)),
        ],
        out_shape=[
            jax.ShapeDtypeStruct((1, SMALL_LANES), F32),
            jax.ShapeDtypeStruct((s, D_MODEL), F32),
            jax.ShapeDtypeStruct((D_MODEL, D_MODEL), F32),
            jax.ShapeDtypeStruct((N_DEV, PLE_DIM, blk), F32),
            jax.ShapeDtypeStruct((1, D_MODEL), F32),
        ],
    )


def _ffn_bwd_act(dh2, h1, gt, up, g_ffn, wg_t, wu_t, w_down):
    s = h1.shape[0]
    ts = min(FFN_BWD_TILE, s)

    def body(dh_ref, h1_ref, gt_ref, up_ref, g_ref, wg_ref, wu_ref, wd_ref,
             dgt_ref, dup_ref, dh1_ref, dh1b_ref, dg_ref, dwd_ref, act_ref):
        @pl.when(pl.program_id(0) == 0)
        def _():
            dg_ref[...] = jnp.zeros_like(dg_ref)
            dwd_ref[...] = jnp.zeros_like(dwd_ref)

        dhb = dh_ref[...].astype(BF16)
        d_hn = jnp.zeros((ts, D_MODEL), F32)
        n_chunks = D_FF // FF_CHUNK
        chunk = lambda c: slice(c * FF_CHUNK, (c + 1) * FF_CHUNK)
        d_act_next = _nt(dhb, wd_ref[chunk(0), :])
        for c in range(n_chunks):
            cols = chunk(c)
            d_act = d_act_next
            if c + 1 < n_chunks:
                d_act_next = _nt(dhb, wd_ref[chunk(c + 1), :])
            gtv = gt_ref[:, cols].astype(F32)
            upv = up_ref[:, cols].astype(F32)
            sg = jax.nn.sigmoid(gtv)
            silu = gtv * sg
            act_ref[:, cols] = (silu * upv).astype(BF16)
            d_up = (d_act * silu).astype(BF16)
            d_gt = ((d_act * upv) * (sg * (1.0 + gtv * (1.0 - sg)))).astype(BF16)
            dup_ref[:, cols] = d_up
            dgt_ref[:, cols] = d_gt
            d_hn = (_nn(d_gt, wg_ref[cols, :]) + _nn(d_up, wu_ref[cols, :])) + d_hn
        dwd_ref[...] += _tn(act_ref[...], dhb)
        h1v = h1_ref[...]
        d_x, d_g = _rms_bwd(d_hn, h1v, _rms(h1v), g_ref[...])
        dg_ref[...] += d_g
        dh1 = dh_ref[...] + d_x
        dh1_ref[...] = dh1
        dh1b_ref[...] = dh1.astype(BF16)

    return _call(
        body,
        (dh2, h1, gt, up, g_ffn, wg_t, wu_t, w_down),
        name="ffn_bwd_act",
        grid=(s // ts,),
        in_specs=[
            _rows(ts, D_MODEL),
            _rows(ts, D_MODEL),
            _rows(ts, D_FF),
            _rows(ts, D_FF),
            _resident((1, D_MODEL)),
            _resident((D_FF, D_MODEL)),
            _resident((D_FF, D_MODEL)),
            _resident((D_FF, D_MODEL)),
        ],
        out_specs=[
            _rows(ts, D_FF), _rows(ts, D_FF),
            _rows(ts, D_MODEL), _rows(ts, D_MODEL), _acc((1, D_MODEL)), _acc((D_FF, D_MODEL)),
        ],
        out_shape=[
            jax.ShapeDtypeStruct((s, D_FF), BF16),
            jax.ShapeDtypeStruct((s, D_FF), BF16),
            jax.ShapeDtypeStruct((s, D_MODEL), F32),
            jax.ShapeDtypeStruct((s, D_MODEL), BF16),
            jax.ShapeDtypeStruct((1, D_MODEL), F32),
            jax.ShapeDtypeStruct((D_FF, D_MODEL), F32),
        ],
        scratch_shapes=[pltpu.VMEM((ts, D_FF), BF16)],
    )


def _ffn_bwd_w(dgt, dup, hn2):
    s = hn2.shape[0]
    slab = pl.BlockSpec((s, FFN_W_SLAB), lambda i: (0, i))

    def body(dgt_ref, dup_ref, hn_ref, dwg_ref, dwu_ref):
        hn = hn_ref[...]
        dwg_ref[...] = _tn(dgt_ref[...], hn)
        dwu_ref[...] = _tn(dup_ref[...], hn)

    return _call(
        body,
        (dgt, dup, hn2),
        name="ffn_bwd_w",
        grid=(D_FF // FFN_W_SLAB,),
        in_specs=[slab, slab, _resident((s, D_MODEL))],
        out_specs=[_rows(FFN_W_SLAB, D_MODEL)] * 2,
        out_shape=[jax.ShapeDtypeStruct((D_FF, D_MODEL), F32)] * 2,
    )


def _mix_bwd(dh1b, u, a, m, w_out, w_pool, pool_scale):
    s = u.shape[0]
    ts = min(TOKEN_TILE, s)
    nt = s // ts
    halo_after = lambda i: (jnp.minimum((i + 1) * (ts // POOL_HALO), s // POOL_HALO - 1), 0)
    n_groups = len(POOL_SIZES)

    def body(dh_ref, dhn_ref, u_ref, uh_ref, a_ref, m_ref, wo_ref, wp_ref, sc_ref,
             da_ref, du_ref, dwp_ref, dsc_ref, dwo_ref):
        i = pl.program_id(0)

        @pl.when(i == 0)
        def _():
            dwp_ref[...] = jnp.zeros_like(dwp_ref)
            dsc_ref[...] = jnp.zeros_like(dsc_ref)
            dwo_ref[...] = jnp.zeros_like(dwo_ref)

        dh = dh_ref[...]
        dwo_ref[:ATTN_WIDTH, :] += _tn(a_ref[...], dh)
        dwo_ref[ATTN_WIDTH:, :] += _tn(m_ref[...], dh)
        da_ref[...] = _nt(dh, wo_ref[:ATTN_WIDTH, :])
        dh_next = jnp.where(i < nt - 1, dhn_ref[...], jnp.zeros_like(dhn_ref))
        dm_ext = _nt(jnp.concatenate([dh, dh_next], axis=0), wo_ref[ATTN_WIDTH:, :])
        pooled = _pooled(u_ref[...], uh_ref[...], i, ts)
        t_ext = i * ts + lax.broadcasted_iota(jnp.int32, (ts + POOL_HALO, 1), 0)
        for g, w in enumerate(POOL_SIZES):
            lanes = slice(g * POOL_GROUP, (g + 1) * POOL_GROUP)
            wp = wp_ref[g].astype(BF16)
            pg = pooled[g].astype(BF16)
            dm_g = dm_ext[:, lanes]
            dsc_ref[:, lanes] += jnp.sum(dm_g[:ts, :] * _nn(pg, wp), axis=0, keepdims=True)
            dy = (dm_g * sc_ref[:, lanes]).astype(BF16)
            dwp_ref[g] += _tn(pg, dy[:ts, :])
            d_pool = _nt(dy, wp)
            acc = d_pool / jnp.minimum(t_ext + 1, w).astype(F32)
            shift = 1
            while shift < w:
                acc = acc + pltpu.roll(acc, ts + POOL_HALO - shift, 0)
                shift *= 2
            du_ref[:, lanes] = (acc[:ts, :] - d_pool[:ts, :]).astype(BF16)

    return _call(
        body,
        (dh1b, dh1b, u, u, a, m, w_out, w_pool, pool_scale),
        name="mix_bwd",
        grid=(nt,),
        in_specs=[
            _rows(ts, D_MODEL),
            pl.BlockSpec((POOL_HALO, D_MODEL), halo_after),
            _rows(ts, POOL_WIDTH),
            pl.BlockSpec((POOL_HALO, POOL_WIDTH), _halo_before(ts)),
            _rows(ts, ATTN_WIDTH),
            _rows(ts, POOL_WIDTH),
            _resident((D_MODEL, D_MODEL)),
            _resident((n_groups, POOL_GROUP, POOL_GROUP)),
            _resident((1, POOL_WIDTH)),
        ],
        out_specs=[
            _rows(ts, ATTN_WIDTH),
            _rows(ts, POOL_WIDTH),
            _acc((n_groups, POOL_GROUP, POOL_GROUP)),
            _acc((1, POOL_WIDTH)),
            _acc((D_MODEL, D_MODEL)),
        ],
        out_shape=[
            jax.ShapeDtypeStruct((s, ATTN_WIDTH), F32),
            jax.ShapeDtypeStruct((s, POOL_WIDTH), BF16),
            jax.ShapeDtypeStruct((n_groups, POOL_GROUP, POOL_GROUP), F32),
            jax.ShapeDtypeStruct((1, POOL_WIDTH), F32),
            jax.ShapeDtypeStruct((D_MODEL, D_MODEL), F32),
        ],
    )


def _attn_bwd(qn, kn, v, a, da, tab, sinks):
    s = qn.shape[0]
    qb = ATTN_STEP_BLOCKS
    rows = qb * BLOCK
    n_groups = s // rows
    group, prev = _attn_specs(n_groups)
    done = lambda n: (jnp.maximum(n - 1, 0), 0)

    def body(sink_ref, q_ref, kc_ref, kp_ref, vc_ref, vp_ref, o_ref, do_ref, tab_ref,
             dq_ref, dk_ref, dv_ref, dl_ref, ds_ref, k_carry, v_carry, sink_acc):
        n = pl.program_id(0)

        @pl.when(n == 0)
        def _():
            dl_ref[...] = jnp.zeros_like(dl_ref)
            k_carry[...] = jnp.zeros_like(k_carry)
            v_carry[...] = jnp.zeros_like(v_carry)
            sink_acc[...] = jnp.zeros_like(sink_acc)

        @pl.when(n < n_groups)
        def _():
            first = n == 0
            lo_mask = _lane_lo((BLOCK, BLOCK))
            chains = [(b, half) for b in range(qb) for half in range(2)]
            tile = lambda ref, b, p: ref[b * BLOCK:(b + 1) * BLOCK, p * BLOCK:(p + 1) * BLOCK]
            keys = [_band(kp_ref, kc_ref, b) for b in range(qb)]
            vals = [_band(vp_ref, vc_ref, b) for b in range(qb)]
            q_st = [_stack_heads([tile(q_ref, b, p) for p in range(4)], lo_mask) for b in range(qb)]
            do_st = [_stack_heads([tile(do_ref, b, p) for p in range(4)], lo_mask) for b in range(qb)]
            o_st = [_stack_heads([tile(o_ref, b, p).astype(F32) for p in range(4)], lo_mask) for b in range(qb)]
            sink_col = [_sink_column(sink_ref, heads) for heads in (HEADS_A, HEADS_B)]
            scores = {(b, h): _band_scores(q_st[b][h], keys[b][h], tab_ref[h], first if b == 0 else None)
                      for b, h in chains}
            dob = {(b, h): do_st[b][h].astype(BF16) for b, h in chains}
            d_probs = {(b, h): _nt(dob[b, h], vals[b][h]) for b, h in chains}
            delta = {(b, h): jnp.sum(do_st[b][h] * o_st[b][h], axis=-1, keepdims=True) for b, h in chains}
            soft = {(b, h): _softmax_with_sink(scores[b, h], sink_col[h]) for b, h in chains}
            dl = {(b, h): soft[b, h][0] * (d_probs[b, h] - delta[b, h]) for b, h in chains}
            for b, h in chains:
                dl_ref[h] += dl[b, h]
                sink_acc[h] += soft[b, h][1] * delta[b, h]
            dsb = {(b, h): (dl[b, h] * (HEAD_DIM ** -0.5)).astype(BF16) for b, h in chains}
            dq_st = {(b, h): _nn(dsb[b, h], keys[b][h]) for b, h in chains}
            dk_parts = {(b, h): _tn(dsb[b, h], q_st[b][h]) for b, h in chains}
            dv_parts = {(b, h): _tn(soft[b, h][0].astype(BF16), dob[b, h]) for b, h in chains}
            for b in range(qb):
                dq = _unstack_heads(dq_st[b, 0], dq_st[b, 1], lo_mask)
                for p in range(4):
                    dq_ref[b * BLOCK:(b + 1) * BLOCK, p * BLOCK:(p + 1) * BLOCK] = dq[p]
            dks = [dk_parts[b, 0] + pltpu.roll(dk_parts[b, 1], HEAD_DIM, 1) for b in range(qb)]
            dvs = [dv_parts[b, 0] + pltpu.roll(dv_parts[b, 1], HEAD_DIM, 1) for b in range(qb)]
            last = slice((qb - 1) * BLOCK, qb * BLOCK)
            for parts, out_ref, carry in ((dks, dk_ref, k_carry), (dvs, dv_ref, v_carry)):
                out_ref[...] = carry[...]
                out_ref[last, :] += parts[0][:BLOCK, :]
                for b in range(qb):
                    own = parts[b][BLOCK:, :]
                    carry[b * BLOCK:(b + 1) * BLOCK, :] = own + parts[b + 1][:BLOCK, :] if b + 1 < qb else own

        @pl.when(n == n_groups)
        def _():
            dk_ref[...] = k_carry[...]
            dv_ref[...] = v_carry[...]
            for half, heads in enumerate((HEADS_A, HEADS_B)):
                for slot, h in enumerate(heads):
                    tot = jnp.sum(sink_acc[half, slot * BLOCK:(slot + 1) * BLOCK, :], axis=0, keepdims=True)
                    ds_ref[h:h + 1, :] = jnp.broadcast_to(-tot, (1, SMALL_LANES))

    return _call(
        body,
        (sinks, qn, kn, kn, v, v, a, da, tab),
        name="attn_bwd",
        grid=(n_groups + 1,),
        in_specs=[
            pl.BlockSpec(memory_space=pltpu.SMEM),
            pl.BlockSpec((rows, ATTN_WIDTH), group),
            pl.BlockSpec((rows, KV_WIDTH), group),
            pl.BlockSpec((BLOCK, KV_WIDTH), prev),
            pl.BlockSpec((rows, KV_WIDTH), group),
            pl.BlockSpec((BLOCK, KV_WIDTH), prev),
            pl.BlockSpec((rows, ATTN_WIDTH), group),
            pl.BlockSpec((rows, ATTN_WIDTH), group),
            _resident((2, 4 * BLOCK, 2 * BLOCK)),
        ],
        out_specs=[
            pl.BlockSpec((rows, ATTN_WIDTH), group),
            pl.BlockSpec((rows, KV_WIDTH), done),
            pl.BlockSpec((rows, KV_WIDTH), done),
            _acc((2, 4 * BLOCK, 2 * BLOCK)),
            _acc((N_DEV, SMALL_LANES)),
        ],
        out_shape=[
            jax.ShapeDtypeStruct((s, ATTN_WIDTH), F32),
            jax.ShapeDtypeStruct((s, KV_WIDTH), F32),
            jax.ShapeDtypeStruct((s, KV_WIDTH), F32),
            jax.ShapeDtypeStruct((2, 4 * BLOCK, 2 * BLOCK), F32),
            jax.ShapeDtypeStruct((N_DEV, SMALL_LANES), F32),
        ],
        scratch_shapes=[
            pltpu.VMEM((rows, KV_WIDTH), F32),
            pltpu.VMEM((rows, KV_WIDTH), F32),
            pltpu.VMEM((2, 4 * BLOCK, 1), F32),
        ],
    )


def _fold_heads(acc):
    t = acc + pltpu.roll(acc, HEAD_DIM, 1)
    out = t[:, :SMALL_LANES]
    for g in range(1, acc.shape[1] // SMALL_LANES):
        out = out + t[:, g * SMALL_LANES:(g + 1) * SMALL_LANES]
    return out


def _in_proj_bwd(dqn, dkn, dv, du, zqk, x, dh1, g_attn, gq_t, gk_t, w_in_t):
    s = x.shape[0]
    ts = min(TOKEN_TILE, s)
    nt = s // ts

    def head_norm_bwd(d_n, raw, g_t, bmat):
        r = lax.rsqrt(_seg_mean(raw * raw, bmat) + EPS)
        gy = d_n * g_t
        d_raw = r * gy - raw * (r * r * r) * _seg_mean(gy * raw, bmat)
        return d_raw, jnp.sum(d_n * (raw * r), axis=0, keepdims=True)

    def body(dqn_ref, dkn_ref, dv_ref, du_ref, zqk_ref, x_ref, dh1_ref, g_ref, gq_ref, gk_ref, w_ref, bq_ref, bk_ref,
             gx_ref, dw_ref, dg_ref, dgq_ref, dgk_ref, dz_ref, gq_acc, gk_acc):
        i = pl.program_id(0)

        @pl.when(i == 0)
        def _():
            dw_ref[...] = jnp.zeros_like(dw_ref)
            dg_ref[...] = jnp.zeros_like(dg_ref)
            gq_acc[...] = jnp.zeros_like(gq_acc)
            gk_acc[...] = jnp.zeros_like(gk_acc)

        d_q, d_gq = head_norm_bwd(dqn_ref[...], zqk_ref[:, :ATTN_WIDTH], gq_ref[...], bq_ref[...])
        d_k, d_gk = head_norm_bwd(dkn_ref[...], zqk_ref[:, ATTN_WIDTH:], gk_ref[...], bk_ref[...])
        gq_acc[...] += d_gq
        gk_acc[...] += d_gk
        dz_ref[:, :ATTN_WIDTH] = d_q.astype(BF16)
        dz_ref[:, ATTN_WIDTH:ATTN_WIDTH + KV_WIDTH] = d_k.astype(BF16)
        dz_ref[:, ATTN_WIDTH + KV_WIDTH:ATTN_WIDTH + 2 * KV_WIDTH] = dv_ref[...].astype(BF16)
        dz_ref[:, ATTN_WIDTH + 2 * KV_WIDTH:] = du_ref[...]
        dz = dz_ref[...]
        xf = x_ref[...]
        r = _rms(xf)
        hn = ((xf * r) * g_ref[...]).astype(BF16)
        d_x, d_g = _rms_bwd(_nn(dz, w_ref[...]), xf, r, g_ref[...])
        dg_ref[...] += d_g
        gx_ref[...] = dh1_ref[...] + d_x
        dw_ref[...] += _tn(dz, hn)

        @pl.when(i == nt - 1)
        def _():
            dgq_ref[...] = _fold_heads(gq_acc[...])
            dgk_ref[...] = _fold_heads(gk_acc[...])

    return _call(
        body,
        (dqn, dkn, dv, du, zqk, x, dh1, g_attn, gq_t, gk_t, w_in_t,
      _head_mean_matrix(ATTN_WIDTH), _head_mean_matrix(KV_WIDTH)),
        name="in_proj_bwd",
        grid=(nt,),
        in_specs=[
            _rows(ts, ATTN_WIDTH),
            _rows(ts, KV_WIDTH),
            _rows(ts, KV_WIDTH),
            _rows(ts, POOL_WIDTH),
            _rows(ts, ATTN_WIDTH + KV_WIDTH),
            _rows(ts, D_MODEL),
            _rows(ts, D_MODEL),
            _resident((1, D_MODEL)),
            _resident((1, ATTN_WIDTH)),
            _resident((1, KV_WIDTH)),
            _resident((IN_WIDTH, D_MODEL)),
            _resident((ATTN_WIDTH, ATTN_WIDTH)),
            _resident((KV_WIDTH, KV_WIDTH)),
        ],
        out_specs=[
            _rows(ts, D_MODEL),
            _acc((IN_WIDTH, D_MODEL)),
            _acc((1, D_MODEL)),
            _acc((1, SMALL_LANES)),
            _acc((1, SMALL_LANES)),
        ],
        out_shape=[
            jax.ShapeDtypeStruct((s, D_MODEL), F32),
            jax.ShapeDtypeStruct((IN_WIDTH, D_MODEL), F32),
            jax.ShapeDtypeStruct((1, D_MODEL), F32),
            jax.ShapeDtypeStruct((1, SMALL_LANES), F32),
            jax.ShapeDtypeStruct((1, SMALL_LANES), F32),
        ],
        scratch_shapes=[
            pltpu.VMEM((ts, IN_WIDTH), BF16),
            pltpu.VMEM((1, ATTN_WIDTH), F32),
            pltpu.VMEM((1, KV_WIDTH), F32),
        ],
    )


BIG_WEIGHTS = (
    ("w_in", True, IN_WIDTH // N_DEV, D_MODEL),
    ("w_out", False, D_MODEL // N_DEV, D_MODEL),
    ("w_gate", True, D_FF // N_DEV, D_MODEL),
    ("w_up", True, D_FF // N_DEV, D_MODEL),
    ("w_down", False, D_FF // N_DEV, D_MODEL),
    ("w_ple_gate", False, D_MODEL // N_DEV, D_MODEL),
    ("w_ple_proj", False, PLE_DIM, D_MODEL // N_DEV),
)
N_BIG = len(BIG_WEIGHTS)


def _place():
    x, y, c = lax.axis_index("x"), lax.axis_index("y"), lax.axis_index("c")
    chips = [(1 - x, y), (x, 1 - y), (1 - x, 1 - y)]
    return x, y, c, chips


class _Gather:
    def __init__(self, n):
        self.n = n
        self.sems = [pltpu.SemaphoreType.DMA((n, 7)), pltpu.SemaphoreType.DMA((n, 7)), pltpu.SemaphoreType.DMA((n,))]

    def _ctx(self, srcs, outs, sems):
        send_sems, recv_sems, local_sems = sems
        x, y, c, chips = _place()
        me, sibling = (x, y, c), (x, y, 1 - c)

        def block(k, owner):
            px, py, pc = owner
            return outs[k].at[4 * px + 2 * py + pc]

        def copy(k, idx, owner, to, mine=False):
            return pltpu.make_async_remote_copy(
                src_ref=srcs[k] if mine else block(k, owner), dst_ref=block(k, owner),
                send_sem=send_sems.at[k, idx], recv_sem=recv_sems.at[k, idx], device_id=to, device_id_type=MESH)

        def local(k):
            return pltpu.make_async_copy(srcs[k], block(k, me), local_sems.at[k])

        return c, chips, me, sibling, copy, local

    def begin(self, srcs, outs, sems):
        c, chips, me, sibling, copy, local = self._ctx(srcs, outs, sems)
        for k in range(self.n):
            local(k).start()
            copy(k, 0, me, sibling, mine=True).start()
            for j, chip in enumerate(chips):
                copy(k, 1 + j, me, (*chip, c), mine=True).start()

    def middle(self, srcs, outs, sems):
        c, chips, me, sibling, copy, local = self._ctx(srcs, outs, sems)
        for j, chip in enumerate(chips):
            for k in range(self.n):
                copy(k, 1 + j, (*chip, c), me).wait_recv()
                copy(k, 4 + j, (*chip, c), sibling).start()

    def end(self, srcs, outs, sems):
        c, chips, me, sibling, copy, local = self._ctx(srcs, outs, sems)
        for k in range(self.n):
            copy(k, 0, sibling, me).wait_recv()
            for j, chip in enumerate(chips):
                copy(k, 4 + j, (*chip, 1 - c), me).wait_recv()
        for k in range(self.n):
            copy(k, 0, me, sibling, mine=True).wait_send()
            for j, chip in enumerate(chips):
                copy(k, 1 + j, me, (*chip, c), mine=True).wait_send()
                copy(k, 4 + j, (*chip, c), sibling).wait_send()
            local(k).wait()


def _gather_rider(shards):
    g = _Gather(len(shards))
    shapes = [jax.ShapeDtypeStruct((N_DEV, *sh.shape), sh.dtype) for sh in shards]
    return _Rider(shards, shapes, g.sems, g.begin, g.end, g.middle)


def _cast_and_gather_first(shards, rel_bias_t):
    g = _Gather(1)
    any_spec = pl.BlockSpec(memory_space=pl.ANY)
    vmem = pl.BlockSpec(memory_space=pltpu.VMEM)

    def body(*refs):
        ins, rb_ref, outs = refs[:N_BIG], refs[N_BIG], refs[N_BIG + 1:2 * N_BIG + 1]
        gathered, tab_ref, sems = refs[2 * N_BIG + 1], refs[2 * N_BIG + 2], refs[2 * N_BIG + 3:]
        outs[0][...] = ins[0][...].astype(BF16)
        g.begin(outs[:1], [gathered], sems)
        for k in range(1, N_BIG):
            outs[k][...] = ins[k][...].astype(BF16)
        _write_bias_table(rb_ref, tab_ref)
        g.middle(outs[:1], [gathered], sems)
        g.end(outs[:1], [gathered], sems)

    res = pl.pallas_call(
        body,
        name="cast_and_gather_first",
        in_specs=[vmem] * N_BIG + [pl.BlockSpec(memory_space=pltpu.SMEM)],
        out_specs=[vmem] * N_BIG + [any_spec, vmem],
        out_shape=[jax.ShapeDtypeStruct((r, c), BF16) for _, _, r, c in BIG_WEIGHTS]
        + [jax.ShapeDtypeStruct((N_DEV, *BIG_WEIGHTS[0][2:]), BF16), jax.ShapeDtypeStruct(BIAS_TABLE_SHAPE, F32)],
        scratch_shapes=g.sems,
    )(*shards, rel_bias_t)
    return list(res[:N_BIG]), res[N_BIG], res[N_BIG + 1]


def _sibling_rider(grads):
    n = len(grads)

    def copies(gs, lands, sems):
        send_sems, recv_sems = sems
        x, y, c, _ = _place()
        return [
            pltpu.make_async_remote_copy(
                src_ref=gs[k].at[:, 1 - c], dst_ref=lands[k], send_sem=send_sems.at[k], recv_sem=recv_sems.at[k],
                device_id=(x, y, 1 - c), device_id_type=MESH)
            for k in range(n)
        ]

    def begin(gs, lands, sems):
        for cp in copies(gs, lands, sems):
            cp.start()

    def end(gs, lands, sems):
        for cp in copies(gs, lands, sems):
            cp.wait()

    shapes = [jax.ShapeDtypeStruct((N_CHIPS, *g.shape[2:]), F32) for g in grads]
    return _Rider(grads, shapes, [pltpu.SemaphoreType.DMA((n,)), pltpu.SemaphoreType.DMA((n,))], begin, end)


def _chip_of_relation(j, place):
    x, y = place[0], place[1]
    return jnp.where(j == 0, 2 * (1 - x) + y, jnp.where(j == 1, 2 * x + 1 - y, 2 * (1 - x) + 1 - y))


def _chip_sum(ks, place, grads, from_sibling):
    n = len(ks)
    shapes = [BIG_WEIGHTS[k][2:] for k in ks]
    operands, specs = [], []
    for (r, c), g, l in zip(shapes, grads, from_sibling):
        operands += [g, l]
        specs += [pl.BlockSpec((1, 1, r, c), lambda j, place: (_chip_of_relation(j, place), place[2], 0, 0)),
                  pl.BlockSpec((1, r, c), lambda j, place: (_chip_of_relation(j, place), 0, 0))]
    args, in_specs, _ = _after_last(operands, specs)

    def body(place_ref, *refs):
        ins, outs = refs[:2 * n], refs[len(args):]
        for i in range(n):
            outs[i][0] = (ins[2 * i][0, 0] + ins[2 * i + 1][0]).astype(BF16)

    outs = pl.pallas_call(
        body,
        name="chip_sum_" + "_".join(BIG_WEIGHTS[k][0] for k in ks),
        grid_spec=pltpu.PrefetchScalarGridSpec(
            num_scalar_prefetch=1,
            grid=(N_CHIPS - 1,),
            in_specs=in_specs,
            out_specs=[pl.BlockSpec((1, r, c), lambda j, place: (j, 0, 0)) for r, c in shapes],
        ),
        out_shape=[jax.ShapeDtypeStruct((N_CHIPS - 1, r, c), BF16) for r, c in shapes],
    )(place, *args)
    _mark_issued(outs[0])
    return list(outs)


def _chips_rider(to_send, small=None):
    n = len(to_send)
    inputs = list(to_send) + ([] if small is None else [small])
    shapes = [jax.ShapeDtypeStruct((3, *t.shape[1:]), BF16) for t in to_send]
    sems = [pltpu.SemaphoreType.DMA((max(n, 1), 3)), pltpu.SemaphoreType.DMA((max(n, 1), 3))]
    if small is not None:
        shapes.append(jax.ShapeDtypeStruct((N_DEV, *small.shape), F32))
        sems += [pltpu.SemaphoreType.DMA((7,)), pltpu.SemaphoreType.DMA((7,)), pltpu.SemaphoreType.DMA]

    def copies(ins, outs, sem_refs):
        x, y, c, chips = _place()
        out = []
        for k in range(n):
            for j, (px, py) in enumerate(chips):
                out.append(pltpu.make_async_remote_copy(
                    src_ref=ins[k].at[j], dst_ref=outs[k].at[j],
                    send_sem=sem_refs[0].at[k, j], recv_sem=sem_refs[1].at[k, j],
                    device_id=(px, py, c), device_id_type=MESH))
        local = None
        if small is not None:
            me = 4 * x + 2 * y + c
            local = pltpu.make_async_copy(ins[n], outs[n].at[me], sem_refs[4])
            rel = 0
            for fx in (0, 1):
                for fy in (0, 1):
                    for fc in (0, 1):
                        if (fx, fy, fc) != (0, 0, 0):
                            out.append(pltpu.make_async_remote_copy(
                                src_ref=ins[n], dst_ref=outs[n].at[me],
                                send_sem=sem_refs[2].at[rel], recv_sem=sem_refs[3].at[rel],
                                device_id=(x ^ fx, y ^ fy, c ^ fc), device_id_type=MESH))
                            rel += 1
        return out, local

    def begin(ins, outs, sem_refs):
        remote, local = copies(ins, outs, sem_refs)
        if local is not None:
            local.start()
        for cp in remote:
            cp.start()

    def end(ins, outs, sem_refs):
        remote, local = copies(ins, outs, sem_refs)
        for cp in remote:
            cp.wait()
        if local is not None:
            local.wait()

    return _Rider(inputs, shapes, sems, begin, end)


PEER_SETS = {"sibling": 1, "chips": 2, "sibling+chips": 3, "all": 4}


def _peers(pattern):
    x, y, c, chips = _place()
    sibling, others = [(x, y, 1 - c)], [(*chip, c) for chip in chips]
    if pattern == "all":
        return sibling + others + [(*chip, 1 - c) for chip in chips]
    return {"sibling": sibling, "chips": others, "sibling+chips": sibling + others}[pattern]


def _on_sequencer(name, pattern, rider):
    n_in, n_out = len(rider.inputs), len(rider.out_shapes)

    def body(*refs):
        ins, outs, sems = refs[:n_in], refs[n_in:n_in + n_out], refs[n_in + n_out:]
        peers = _peers(pattern)
        barrier = pltpu.get_barrier_semaphore()
        for peer in peers:
            pl.semaphore_signal(barrier, inc=1, device_id=peer, device_id_type=MESH)
        pl.semaphore_wait(barrier, len(peers))
        rider.begin(ins, outs, sems)
        if rider.middle is not None:
            rider.middle(ins, outs, sems)
        rider.end(ins, outs, sems)

    outs = pl.kernel(
        body,
        name=name,
        out_type=tuple(rider.out_shapes),
        mesh=plsc.ScalarSubcoreMesh(axis_name="sequencer", num_cores=1),
        scratch_types=tuple(rider.sems),
        compiler_params=pltpu.CompilerParams(collective_id=PEER_SETS[pattern]),
    )(*rider.inputs)
    return list(outs)


def _adamw(w, g, m, v):
    m = ADAM_B1 * m + (1.0 - ADAM_B1) * g
    v = ADAM_B2 * v + (1.0 - ADAM_B2) * jnp.square(g)
    m_hat = m / (1.0 - ADAM_B1 ** ADAM_STEP)
    v_hat = v / (1.0 - ADAM_B2 ** ADAM_STEP)
    delta = -ADAM_LR * (m_hat / (jnp.sqrt(v_hat) + ADAM_EPS) + ADAM_WD * w)
    return delta, m, v


def _adamw_big(ks, place, operands):
    n = len(ks)
    tiles = lambda i, place: (i, 0)
    in_specs, out_specs, out_shape = [], [], []
    for k in ks:
        _, _, r, c = BIG_WEIGHTS[k]
        tile = r // 2
        in_specs += [
            pl.BlockSpec((1, 1, tile, c), lambda i, place: (2 * place[0] + place[1], place[2], i, 0)),
            pl.BlockSpec((1, tile, c), lambda i, place: (2 * place[0] + place[1], i, 0)),
            pl.BlockSpec((3, tile, c), lambda i, place: (0, i, 0)),
        ] + [pl.BlockSpec((tile, c), tiles)] * 3
        out_specs += [pl.BlockSpec((tile, c), tiles)] * 4
        out_shape += [jax.ShapeDtypeStruct((r, c), F32)] * 4
    args, in_specs, _ = _after_last(sum((list(ops) for ops in operands), []), in_specs)

    def body(place_ref, *refs):
        ins, outs = refs[:6 * n], refs[len(args):]
        for i in range(n):
            mine_ref, sib_ref, land_ref, w_ref, m_ref, v_ref = ins[6 * i:6 * i + 6]
            g_ref, d_ref, nm_ref, nv_ref = outs[4 * i:4 * i + 4]
            g = mine_ref[0, 0] + sib_ref[0]
            g = ((g + land_ref[0].astype(F32)) + land_ref[1].astype(F32)) + land_ref[2].astype(F32)
            g_ref[...] = g
            d_ref[...], nm_ref[...], nv_ref[...] = _adamw(w_ref[...], g, m_ref[...], v_ref[...])

    outs = pl.pallas_call(
        body,
        name="adamw_" + "_".join(BIG_WEIGHTS[k][0] for k in ks),
        grid_spec=pltpu.PrefetchScalarGridSpec(
            num_scalar_prefetch=1, grid=(2,), in_specs=in_specs, out_specs=out_specs),
        out_shape=out_shape,
    )(place, *args)
    _mark_issued(outs[0])
    return [outs[4 * i:4 * i + 4] for i in range(n)]


def _pack_small(arrays):
    rows, offsets = [], []
    at = 0
    for a in arrays:
        if a.ndim != 2 or a.shape[1] != SMALL_LANES or a.shape[0] % 8:
            flat = a.reshape(-1)
            n_rows = -(-flat.shape[0] // (8 * SMALL_LANES)) * 8
            a = jnp.pad(flat, (0, n_rows * SMALL_LANES - flat.shape[0])).reshape(n_rows, SMALL_LANES)
        rows.append(a)
        offsets.append(at)
        at += a.shape[0]
    return jnp.concatenate(rows, axis=0), offsets


def _unpack_small(tot, at, shape):
    r, c = shape
    if r % 8 == 0:
        return tot[at:at + r, :c]
    assert r == 1
    if c <= SMALL_LANES:
        return tot[at:at + 1, :c]
    return jnp.concatenate([tot[at + j:at + j + 1, :] for j in range(c // SMALL_LANES)], axis=1)


def _small_update(packs, loss_at, grads_at, ws, ms, vs):
    n, n_packs = len(ws), len(packs)

    def body(*refs):
        pack_refs, refs = refs[:n_packs], refs[n_packs:]
        w_refs, m_refs, v_refs, loss_ref, outs = refs[:n], refs[n:2 * n], refs[2 * n:3 * n], refs[3 * n], refs[3 * n + 1:]
        tots = []
        for p_ref in pack_refs:
            tot = p_ref[0]
            for j in range(1, N_DEV):
                tot = tot + p_ref[j]
            tots.append(tot)
        loss_ref[...] = _unpack_small(tots[loss_at[0]], loss_at[1], (1, 1))
        for i, (pack, at) in enumerate(grads_at):
            g = _unpack_small(tots[pack], at, w_refs[i].shape)
            outs[i][...] = g
            outs[n + i][...], outs[2 * n + i][...], outs[3 * n + i][...] = _adamw(
                w_refs[i][...], g, m_refs[i][...], v_refs[i][...])

    shapes = [jax.ShapeDtypeStruct(w.shape, F32) for w in ws]
    outs = pl.pallas_call(body, name="small_update", out_shape=[jax.ShapeDtypeStruct((1, 1), F32)] + shapes * 4)(
        *packs, *ws, *ms, *vs)
    return outs[0], outs[1:]


SMALL_NAMES = ("g_attn_norm", "g_q", "g_k", "attn_sinks", "rel_bias", "w_pool", "pool_scale", "g_ffn_norm", "g_ple_norm")


def kernel(x, p, w_in, w_out, g_attn_norm, g_q, g_k, attn_sinks, rel_bias, w_pool, pool_scale, g_ffn_norm, w_gate, w_up, w_down, g_ple_norm, w_ple_gate, w_ple_proj, loss_target, m_w_in, m_w_out, m_g_attn_norm, m_g_q, m_g_k, m_attn_sinks, m_rel_bias, m_w_pool, m_pool_scale, m_g_ffn_norm, m_w_gate, m_w_up, m_w_down, m_g_ple_norm, m_w_ple_gate, m_w_ple_proj, v_w_in, v_w_out, v_g_attn_norm, v_g_q, v_g_k, v_attn_sinks, v_rel_bias, v_w_pool, v_pool_scale, v_g_ffn_norm, v_w_gate, v_w_up, v_w_down, v_g_ple_norm, v_w_ple_gate, v_w_ple_proj):
    weights = dict(w_in=w_in, w_out=w_out, g_attn_norm=g_attn_norm, g_q=g_q, g_k=g_k, attn_sinks=attn_sinks,
                   rel_bias=rel_bias, w_pool=w_pool, pool_scale=pool_scale, g_ffn_norm=g_ffn_norm, w_gate=w_gate,
                   w_up=w_up, w_down=w_down, g_ple_norm=g_ple_norm, w_ple_gate=w_ple_gate, w_ple_proj=w_ple_proj)
    m_in = dict(w_in=m_w_in, w_out=m_w_out, g_attn_norm=m_g_attn_norm, g_q=m_g_q, g_k=m_g_k, attn_sinks=m_attn_sinks,
                rel_bias=m_rel_bias, w_pool=m_w_pool, pool_scale=m_pool_scale, g_ffn_norm=m_g_ffn_norm, w_gate=m_w_gate,
                w_up=m_w_up, w_down=m_w_down, g_ple_norm=m_g_ple_norm, w_ple_gate=m_w_ple_gate, w_ple_proj=m_w_ple_proj)
    v_in = dict(w_in=v_w_in, w_out=v_w_out, g_attn_norm=v_g_attn_norm, g_q=v_g_q, g_k=v_g_k, attn_sinks=v_attn_sinks,
                rel_bias=v_rel_bias, w_pool=v_w_pool, pool_scale=v_pool_scale, g_ffn_norm=v_g_ffn_norm, w_gate=v_w_gate,
                w_up=v_w_up, w_down=v_w_down, g_ple_norm=v_g_ple_norm, w_ple_gate=v_w_ple_gate, w_ple_proj=v_w_ple_proj)

    _issued.clear()
    xs = x[0]
    ps = p[0, 0]
    target = loss_target[0]
    wp = w_pool[0]
    gq_t = jnp.tile(g_q, (1, ATTN_WIDTH // HEAD_DIM))
    gk_t = jnp.tile(g_k, (1, KV_WIDTH // HEAD_DIM))

    def to_blocks(k, arr):
        return jnp.swapaxes(arr[0], 0, 1) if BIG_WEIGHTS[k][1] else arr[0]

    def from_blocks(k, arr):
        return (jnp.swapaxes(arr, 0, 1) if BIG_WEIGHTS[k][1] else arr)[None]

    IN, OUT, GATE, UP, DOWN, PG, PP = range(N_BIG)
    full = lambda g: g.reshape(N_DEV * g.shape[1], g.shape[2])
    halves = lambda k, g: g.reshape(N_CHIPS, 2, *BIG_WEIGHTS[k][2:])
    place = jnp.stack([lax.axis_index("x"), lax.axis_index("y"), lax.axis_index("c")]).astype(jnp.int32)

    sh, w_in_g, tab = _cast_and_gather_first(
        [to_blocks(k, weights[name]) for k, (name, _, _, _) in enumerate(BIG_WEIGHTS)], rel_bias.T)
    w_in_t = full(w_in_g)

    (w_out_g,) = _on_sequencer("gather_out", "sibling+chips", _gather_rider([sh[OUT]]))
    wg_g, wu_g = _on_sequencer("gather_gate_up", "sibling+chips", _gather_rider([sh[GATE], sh[UP]]))
    wd_g, w_pg_g, w_pp_g = _on_sequencer("gather_down_ple", "sibling+chips", _gather_rider([sh[DOWN], sh[PG], sh[PP]]))
    (zqk, qn, kn, v, u) = _in_proj(xs, g_attn_norm, w_in_t, gq_t, gk_t)
    (a,) = _attn_fwd(qn, kn, v, tab, attn_sinks)
    w_out_f = full(w_out_g)
    (h1, hn2, m_out) = _mix_out(u, a, xs, w_out_f, wp, pool_scale, g_ffn_norm)
    wg_t, wu_t = full(wg_g), full(wu_g)
    (gt, up) = _ffn_up(hn2, wg_t, wu_t)
    w_down_f = full(wd_g)

    partial, from_sibling, sums, landed = [None] * N_BIG, [None] * N_BIG, [None] * N_BIG, [None] * N_BIG

    def to_sibling(name, ks, grads):
        for k, g in zip(ks, grads):
            partial[k] = halves(k, g)
        got = _on_sequencer(name, "sibling", _sibling_rider([partial[k] for k in ks]))
        for k, g in zip(ks, got):
            from_sibling[k] = g

    def chip_sum(*ks):
        for k, s in zip(ks, _chip_sum(ks, place, [partial[k] for k in ks], [from_sibling[k] for k in ks])):
            sums[k] = s

    def to_chips(name, ks, small=None):
        got = _on_sequencer(name, "chips" if small is None else "all", _chips_rider([sums[k] for k in ks], small))
        for k, g in zip(ks, got):
            landed[k] = g
        return got[len(ks):]

    (loss_part, dh2, d_wpg, d_wpp, d_g_ple) = _ffn_down_ple(
        gt, up, h1, w_down_f, ps, target, g_ple_norm, full(w_pg_g), w_pp_g)
    to_sibling("sibling_ple", (PG, PP), (d_wpg, d_wpp))
    (dgt, dup, dh1, dh1b, d_g_ffn, d_wd) = _ffn_bwd_act(dh2, h1, gt, up, g_ffn_norm, wg_t, wu_t, w_down_f)
    to_sibling("sibling_down", (DOWN,), (d_wd,))
    chip_sum(PG, PP)
    to_chips("chips_ple", (PG, PP))
    chip_sum(DOWN)
    to_chips("chips_down", (DOWN,))
    (d_wg_t, d_wu_t) = _ffn_bwd_w(dgt, dup, hn2)
    to_sibling("sibling_gate_up", (GATE, UP), (d_wg_t, d_wu_t))
    _complete_before_next([landed[PG], landed[PP], landed[DOWN]])
    (da, du, d_wpool, d_scale, d_wo) = _mix_bwd(dh1b, u, a, m_out, w_out_f, wp, pool_scale)
    to_sibling("sibling_out", (OUT,), (d_wo,))
    chip_sum(GATE, UP)
    to_chips("chips_gate_up", (GATE, UP))
    (dqn, dkn, dv, dl_acc, d_sinks) = _attn_bwd(qn, kn, v, a, da, tab, attn_sinks)
    chip_sum(OUT)
    to_chips("chips_out", (OUT,))
    early, early_at = _pack_small([d_wpool.reshape(POOL_WIDTH, POOL_GROUP), d_scale, d_g_ffn, d_g_ple, loss_part[:, :1]])
    (early_all,) = _on_sequencer("gather_early", "sibling+chips", _gather_rider([early]))
    (grad_x, d_win_t, d_g_attn, d_gq, d_gk) = _in_proj_bwd(dqn, dkn, dv, du, zqk, xs, dh1, g_attn_norm, gq_t, gk_t, w_in_t)
    to_sibling("sibling_in", (IN,), (d_win_t,))
    _complete_before_next([landed[OUT], landed[GATE], landed[UP], early_all])
    (d_rel_t,) = _bias_table_bwd(dl_acc)
    chip_sum(IN)
    late, late_at = _pack_small([d_g_attn, d_gq[:, :HEAD_DIM], d_gk[:, :HEAD_DIM], d_sinks[:, 0], d_rel_t])
    (late_all,) = to_chips("chips_in", (IN,), late)

    out = {"grad": {}, "delta": {}, "new_m": {}, "new_v": {}}
    for ks in ((PG, PP, DOWN), (OUT, GATE, UP), (IN,)):
        names = [BIG_WEIGHTS[k][0] for k in ks]
        results = _adamw_big(ks, place, [
            (partial[k], from_sibling[k], landed[k], to_blocks(k, weights[n]), to_blocks(k, m_in[n]),
             to_blocks(k, v_in[n])) for k, n in zip(ks, names)])
        for k, name, res in zip(ks, names, results):
            for kind, r in zip(("grad", "delta", "new_m", "new_v"), res):
                out[kind][name] = from_blocks(k, r)
    def as_rows(name, arr):
        return arr.T if name == "rel_bias" else arr.reshape(POOL_WIDTH, POOL_GROUP) if name == "w_pool" else arr

    def from_rows(name, arr):
        return arr.T if name == "rel_bias" else arr.reshape(w_pool.shape) if name == "w_pool" else arr

    grads_at = dict(w_pool=(0, early_at[0]), pool_scale=(0, early_at[1]), g_ffn_norm=(0, early_at[2]),
                    g_ple_norm=(0, early_at[3]), g_attn_norm=(1, late_at[0]), g_q=(1, late_at[1]), g_k=(1, late_at[2]),
                    attn_sinks=(1, late_at[3]), rel_bias=(1, late_at[4]))
    loss, updates = _small_update(
        [early_all, late_all], (0, early_at[4]), [grads_at[n] for n in SMALL_NAMES],
        [as_rows(n, weights[n]) for n in SMALL_NAMES], [as_rows(n, m_in[n]) for n in SMALL_NAMES],
        [as_rows(n, v_in[n]) for n in SMALL_NAMES])
    loss = loss.reshape(())
    n_small = len(SMALL_NAMES)
    for j, kind in enumerate(("grad", "delta", "new_m", "new_v")):
        for i, name in enumerate(SMALL_NAMES):
            out[kind][name] = from_rows(name, updates[j * n_small + i])

    _issued.clear()
    order = ("w_in", "w_out", "g_attn_norm", "g_q", "g_k", "attn_sinks", "rel_bias", "w_pool", "pool_scale",
             "g_ffn_norm", "w_gate", "w_up", "w_down", "g_ple_norm", "w_ple_gate", "w_ple_proj")
    return (loss, grad_x[None], *[out["grad"][n] for n in order], *[out["delta"][n] for n in order],
            *[out["new_m"][n] for n in order], *[out["new_v"][n] for n in order])
```

```python
import math

import jax
import jax.numpy as jnp
import numpy as np
from jax import lax
from jax.experimental import pallas as pl
from jax.experimental.pallas import tpu as pltpu
from jax.experimental.pallas import tpu_sc as plsc

F32 = jnp.float32
BF16 = jnp.bfloat16
MESH = pl.DeviceIdType.MESH

D_MODEL = 1024
HEAD_DIM = 64
ATTN_WIDTH = 512
KV_WIDTH = 128
POOL_WIDTH = 512
POOL_SIZES = (2, 4, 8, 16)
POOL_GROUP = 128
POOL_HALO = 16
IN_WIDTH = 1280
D_FF = 2816
PLE_DIM = 256
BLOCK = 128
N_BUCKETS = 32
MAX_DISTANCE = 128
EPS = 1e-6
N_DEV = 8
N_CHIPS = 4

ADAM_LR = 0.001
ADAM_B1 = 0.9
ADAM_B2 = 0.999
ADAM_EPS = 1e-08
ADAM_WD = 0.01
ADAM_STEP = 10

TOKEN_TILE = 512
FFN_BWD_TILE = 256
FF_CHUNK = 256
CHIP_SUM_ONE_STEP_BYTES = 4 * 2 ** 20
FFN_W_SLAB = 256
ATTN_STEP_BLOCKS = 4
HEADS_A = (0, 2, 5, 7)
HEADS_B = (1, 3, 4, 6)
SMALL_LANES = 128


def _nn(a, b):
    return jnp.dot(a, b, preferred_element_type=F32)


def _nt(a, b):
    return lax.dot_general(a, b, (((1,), (1,)), ((), ())), preferred_element_type=F32)


def _tn(a, b):
    return lax.dot_general(a, b, (((0,), (0,)), ((), ())), preferred_element_type=F32)


def _resident(shape):
    nd = len(shape)
    return pl.BlockSpec(shape, lambda i, _nd=nd: (0,) * _nd, pipeline_mode=pl.Buffered(1))


def _rows(tile, width):
    return pl.BlockSpec((tile, width), lambda i: (i, 0))


def _acc(shape):
    nd = len(shape)
    return pl.BlockSpec(shape, lambda i, _nd=nd: (0,) * _nd)


def _head_mean_matrix(width):
    idx = np.arange(width) // HEAD_DIM
    return jnp.asarray((idx[:, None] == idx[None, :]).astype(np.float32) / HEAD_DIM, dtype=BF16)


def _seg_mean(v, bmat):
    hi = v.astype(BF16)
    lo = (v - hi.astype(F32)).astype(BF16)
    return _nn(hi, bmat) + _nn(lo, bmat)


def _rms(x):
    return lax.rsqrt(jnp.mean(x * x, axis=-1, keepdims=True) + EPS)


def _rms_bwd(d_y, x, r, g):
    gy = d_y * g
    d_x = r * gy - x * (r * r * r) * jnp.mean(gy * x, axis=-1, keepdims=True)
    d_g = jnp.sum(d_y * (x * r), axis=0, keepdims=True)
    return d_x, d_g


def _lane_lo(shape):
    return lax.broadcasted_iota(jnp.int32, shape, 1) < HEAD_DIM


class _Rider:
    def __init__(self, inputs, out_shapes, sems, begin, end, middle=None):
        self.inputs, self.out_shapes, self.sems = list(inputs), list(out_shapes), list(sems)
        self.begin, self.middle, self.end = begin, middle, end


_issued = []


def _after_last(args, in_specs):
    extra = list(_issued)
    return list(args) + extra, list(in_specs) + [pl.BlockSpec(memory_space=pl.ANY)] * len(extra), len(extra)


def _mark_issued(out):
    _issued[:] = [out]


def _complete_before_next(arrays):
    _issued.extend(arrays)


def _call(body, args, *, name, grid, in_specs, out_specs, out_shape, scratch_shapes=()):
    n_args = len(args)
    args, in_specs, _ = _after_last(args, in_specs)

    def ordered(*refs):
        body(*refs[:n_args], *refs[len(args):])

    outs = pl.pallas_call(ordered, name=name, grid=grid, in_specs=in_specs, out_specs=list(out_specs),
                          out_shape=list(out_shape), scratch_shapes=list(scratch_shapes))(*args)
    _mark_issued(outs[0])
    return list(outs)


def _in_proj(x, g_attn, w_in_t, gq_t, gk_t):
    s = x.shape[0]
    ts = min(TOKEN_TILE, s)

    def body(x_ref, g_ref, w_ref, gq_ref, gk_ref, bq_ref, bk_ref, zqk_ref, qn_ref, kn_ref, v_ref, u_ref):
        xf = x_ref[...]
        hn = ((xf * _rms(xf)) * g_ref[...]).astype(BF16)
        z = _nt(hn, w_ref[...])
        q = z[:, :ATTN_WIDTH]
        k = z[:, ATTN_WIDTH:ATTN_WIDTH + KV_WIDTH]
        zqk_ref[...] = z[:, :ATTN_WIDTH + KV_WIDTH]
        rq = lax.rsqrt(_seg_mean(q * q, bq_ref[...]) + EPS)
        qn_ref[...] = ((q * rq) * gq_ref[...]).astype(BF16)
        rk = lax.rsqrt(_seg_mean(k * k, bk_ref[...]) + EPS)
        kn_ref[...] = ((k * rk) * gk_ref[...]).astype(BF16)
        v_ref[...] = z[:, ATTN_WIDTH + KV_WIDTH:ATTN_WIDTH + 2 * KV_WIDTH].astype(BF16)
        u_ref[...] = z[:, ATTN_WIDTH + 2 * KV_WIDTH:]

    return _call(
        body,
        (x, g_attn, w_in_t, gq_t, gk_t, _head_mean_matrix(ATTN_WIDTH), _head_mean_matrix(KV_WIDTH)),
        name="in_proj",
        grid=(s // ts,),
        in_specs=[
            _rows(ts, D_MODEL),
            _resident((1, D_MODEL)),
            _resident((IN_WIDTH, D_MODEL)),
            _resident((1, ATTN_WIDTH)),
            _resident((1, KV_WIDTH)),
            _resident((ATTN_WIDTH, ATTN_WIDTH)),
            _resident((KV_WIDTH, KV_WIDTH)),
        ],
        out_specs=[
            _rows(ts, ATTN_WIDTH + KV_WIDTH),
            _rows(ts, ATTN_WIDTH),
            _rows(ts, KV_WIDTH),
            _rows(ts, KV_WIDTH),
            _rows(ts, POOL_WIDTH),
        ],
        out_shape=[
            jax.ShapeDtypeStruct((s, ATTN_WIDTH + KV_WIDTH), F32),
            jax.ShapeDtypeStruct((s, ATTN_WIDTH), BF16),
            jax.ShapeDtypeStruct((s, KV_WIDTH), BF16),
            jax.ShapeDtypeStruct((s, KV_WIDTH), BF16),
            jax.ShapeDtypeStruct((s, POOL_WIDTH), F32),
        ],
    )


def _bucket_ranges():
    n = np.arange(MAX_DISTANCE)
    max_exact = N_BUCKETS // 2
    nf = np.maximum(n, 1).astype(np.float64)
    large = max_exact + (np.log(nf / max_exact) / math.log(MAX_DISTANCE / max_exact) * (N_BUCKETS - max_exact)).astype(np.int64)
    bucket = np.where(n < max_exact, n, np.minimum(large, N_BUCKETS - 1))
    out = []
    for b in range(N_BUCKETS):
        idx = np.nonzero(bucket == b)[0]
        out.append((int(idx.min()), int(idx.max()) + 1))
    return out


def _band_distance():
    i = lax.broadcasted_iota(jnp.int32, (BLOCK, 2 * BLOCK), 0)
    j = lax.broadcasted_iota(jnp.int32, (BLOCK, 2 * BLOCK), 1)
    return BLOCK + i - j


BIAS_TABLE_SHAPE = (2, 4 * BLOCK, 2 * BLOCK)


def _write_bias_table(rb_ref, tab_ref):
    d = _band_distance()
    for half, heads in enumerate((HEADS_A, HEADS_B)):
        for slot, h in enumerate(heads):
            t = jnp.full((BLOCK, 2 * BLOCK), -jnp.inf, F32)
            for b, (lo, hi) in enumerate(_bucket_ranges()):
                t = jnp.where((d >= lo) & (d < hi), rb_ref[h, b], t)
            tab_ref[half, slot * BLOCK:(slot + 1) * BLOCK, :] = t


def _bias_table_bwd(dl_acc):
    ranges = _bucket_ranges()
    n_heads = len(HEADS_A) + len(HEADS_B)

    def body(dl_ref, out_ref):
        d = _band_distance()
        row = lax.broadcasted_iota(jnp.int32, (n_heads, SMALL_LANES), 0)
        lane = lax.broadcasted_iota(jnp.int32, (n_heads, SMALL_LANES), 1)
        out = jnp.zeros((n_heads, SMALL_LANES), F32)
        for b, (lo, hi) in enumerate(ranges):
            in_bucket = (d >= lo) & (d < hi)
            for half, heads in enumerate((HEADS_A, HEADS_B)):
                for slot, h in enumerate(heads):
                    g = dl_ref[half, slot * BLOCK:(slot + 1) * BLOCK, :]
                    part = jnp.sum(jnp.where(in_bucket, g, 0.0), axis=0, keepdims=True)
                    tot = jnp.sum(part, axis=1, keepdims=True)
                    out = jnp.where((row == h) & (lane == b), tot, out)
        out_ref[...] = out

    return _call(
        body,
        (dl_acc,),
        name="bias_table_bwd",
        grid=(1,),
        in_specs=[_acc((2, 4 * BLOCK, 2 * BLOCK))],
        out_specs=[_acc((n_heads, SMALL_LANES))],
        out_shape=[jax.ShapeDtypeStruct((n_heads, SMALL_LANES), F32)],
    )


def _stack_heads(pairs, lo_mask):
    zero = jnp.zeros_like(pairs[0])
    lo = [jnp.where(lo_mask, t, zero) for t in pairs]
    hi = [jnp.where(lo_mask, zero, t) for t in pairs]
    return (jnp.concatenate([lo[0], lo[1], hi[2], hi[3]], axis=0),
            jnp.concatenate([hi[0], hi[1], lo[2], lo[3]], axis=0))


def _unstack_heads(out_a, out_b, lo_mask):
    t = lambda x, r: x[r * BLOCK:(r + 1) * BLOCK, :]
    return [
        jnp.where(lo_mask, t(out_a, 0), t(out_b, 0)),
        jnp.where(lo_mask, t(out_a, 1), t(out_b, 1)),
        jnp.where(lo_mask, t(out_b, 2), t(out_a, 2)),
        jnp.where(lo_mask, t(out_b, 3), t(out_a, 3)),
    ]


def _sink_column(sink_ref, heads):
    row = lax.broadcasted_iota(jnp.int32, (4 * BLOCK, 1), 0)
    col = jnp.full((4 * BLOCK, 1), sink_ref[0, heads[3]], F32)
    for slot in (2, 1, 0):
        col = jnp.where(row < (slot + 1) * BLOCK, sink_ref[0, heads[slot]], col)
    return col


def _band_scores(q_stack, keys, tab, first_block):
    s = _nt(q_stack, keys) * (HEAD_DIM ** -0.5) + tab
    if first_block is not None:
        col = lax.broadcasted_iota(jnp.int32, s.shape, 1)
        s = jnp.where(jnp.logical_and(first_block, col < BLOCK), -jnp.inf, s)
    return s


def _softmax_with_sink(s, sink):
    m = jnp.maximum(jnp.max(s, axis=-1, keepdims=True), sink)
    e = jnp.exp(s - m)
    e_sink = jnp.exp(sink - m)
    den = jnp.sum(e, axis=-1, keepdims=True) + e_sink
    return e / den, e_sink / den


def _band_probs(q_stack, keys, tab, sink, first_block):
    return _softmax_with_sink(_band_scores(q_stack, keys, tab, first_block), sink)


def _attn_specs(n_groups):
    group = lambda n: (jnp.minimum(n, n_groups - 1), 0)
    prev = lambda n: (jnp.maximum(jnp.minimum(n, n_groups - 1) * ATTN_STEP_BLOCKS - 1, 0), 0)
    return group, prev


def _band(prev_ref, group_ref, b):
    rows = lambda i: group_ref[i * BLOCK:(i + 1) * BLOCK, :]
    band = jnp.concatenate([prev_ref[...] if b == 0 else rows(b - 1), rows(b)], axis=0)
    return band, pltpu.roll(band, HEAD_DIM, 1)


def _attn_fwd(qn, kn, v, tab, sinks):
    s = qn.shape[0]
    n_groups = s // (ATTN_STEP_BLOCKS * BLOCK)
    group, prev = _attn_specs(n_groups)
    rows = ATTN_STEP_BLOCKS * BLOCK

    def body(sink_ref, q_ref, kc_ref, kp_ref, vc_ref, vp_ref, tab_ref, o_ref):
        first = pl.program_id(0) == 0
        lo_mask = _lane_lo((BLOCK, BLOCK))
        for b in range(ATTN_STEP_BLOCKS):
            at = slice(b * BLOCK, (b + 1) * BLOCK)
            kk, kk_sw = _band(kp_ref, kc_ref, b)
            vv, vv_sw = _band(vp_ref, vc_ref, b)
            q_a, q_b = _stack_heads([q_ref[at, p * BLOCK:(p + 1) * BLOCK] for p in range(4)], lo_mask)
            no_prev = first if b == 0 else None
            p_a, _ = _band_probs(q_a, kk, tab_ref[0], _sink_column(sink_ref, HEADS_A), no_prev)
            p_b, _ = _band_probs(q_b, kk_sw, tab_ref[1], _sink_column(sink_ref, HEADS_B), no_prev)
            out = _unstack_heads(_nn(p_a.astype(BF16), vv), _nn(p_b.astype(BF16), vv_sw), lo_mask)
            for p in range(4):
                o_ref[at, p * BLOCK:(p + 1) * BLOCK] = out[p].astype(BF16)

    return _call(
        body,
        (sinks, qn, kn, kn, v, v, tab),
        name="attn_fwd",
        grid=(n_groups,),
        in_specs=[
            pl.BlockSpec(memory_space=pltpu.SMEM),
            pl.BlockSpec((rows, ATTN_WIDTH), group),
            pl.BlockSpec((rows, KV_WIDTH), group),
            pl.BlockSpec((BLOCK, KV_WIDTH), prev),
            pl.BlockSpec((rows, KV_WIDTH), group),
            pl.BlockSpec((BLOCK, KV_WIDTH), prev),
            _resident((2, 4 * BLOCK, 2 * BLOCK)),
        ],
        out_specs=[pl.BlockSpec((rows, ATTN_WIDTH), group)],
        out_shape=[jax.ShapeDtypeStruct((s, ATTN_WIDTH), BF16)],
    )


def _pooled(u_tile, u_halo, tile_index, tile_rows):
    halo = jnp.where(tile_index > 0, u_halo, 0.0)
    ext = jnp.concatenate([halo, u_tile], axis=0)
    sums = []
    acc = ext
    for shift in (1, 2, 4, 8):
        acc = acc + pltpu.roll(acc, shift, 0)
        sums.append(acc)
    t = tile_index * tile_rows + lax.broadcasted_iota(jnp.int32, (tile_rows, 1), 0)
    out = []
    for g, w in enumerate(POOL_SIZES):
        lanes = slice(g * POOL_GROUP, (g + 1) * POOL_GROUP)
        cnt = jnp.minimum(t + 1, w).astype(F32)
        out.append(sums[g][POOL_HALO:, lanes] / cnt - u_tile[:, lanes])
    return out


def _halo_before(tile):
    return lambda i: (jnp.maximum(i * (tile // POOL_HALO) - 1, 0), 0)


def _mix_out(u, a, x, w_out, w_pool, pool_scale, g_ffn):
    s = x.shape[0]
    ts = min(TOKEN_TILE, s)

    def body(u_ref, uh_ref, a_ref, x_ref, wo_ref, wp_ref, sc_ref, g_ref, h1_ref, hn_ref, m_ref):
        i = pl.program_id(0)
        pooled = _pooled(u_ref[...], uh_ref[...], i, ts)
        for g in range(len(POOL_SIZES)):
            lanes = slice(g * POOL_GROUP, (g + 1) * POOL_GROUP)
            y = _nn(pooled[g].astype(BF16), wp_ref[g].astype(BF16))
            m_ref[:, lanes] = (y * sc_ref[:, lanes]).astype(BF16)
        h1 = x_ref[...] + _nn(a_ref[...], wo_ref[:ATTN_WIDTH, :]) + _nn(m_ref[...], wo_ref[ATTN_WIDTH:, :])
        h1_ref[...] = h1
        hn_ref[...] = ((h1 * _rms(h1)) * g_ref[...]).astype(BF16)

    return _call(
        body,
        (u, u, a, x, w_out, w_pool, pool_scale, g_ffn),
        name="mix_out",
        grid=(s // ts,),
        in_specs=[
            _rows(ts, POOL_WIDTH),
            pl.BlockSpec((POOL_HALO, POOL_WIDTH), _halo_before(ts)),
            _rows(ts, ATTN_WIDTH),
            _rows(ts, D_MODEL),
            _resident((D_MODEL, D_MODEL)),
            _resident((len(POOL_SIZES), POOL_GROUP, POOL_GROUP)),
            _resident((1, POOL_WIDTH)),
            _resident((1, D_MODEL)),
        ],
        out_specs=[_rows(ts, D_MODEL), _rows(ts, D_MODEL), _rows(ts, POOL_WIDTH)],
        out_shape=[
            jax.ShapeDtypeStruct((s, D_MODEL), F32),
            jax.ShapeDtypeStruct((s, D_MODEL), BF16),
            jax.ShapeDtypeStruct((s, POOL_WIDTH), BF16),
        ],
    )


def _ffn_up(hn2, wg_t, wu_t):
    s = hn2.shape[0]
    ts = min(TOKEN_TILE, s)

    def body(hn_ref, wg_ref, wu_ref, gt_ref, up_ref):
        hn = hn_ref[...]
        for c in range(D_FF // FF_CHUNK):
            cols = slice(c * FF_CHUNK, (c + 1) * FF_CHUNK)
            gt_ref[:, cols] = _nt(hn, wg_ref[cols, :]).astype(BF16)
            up_ref[:, cols] = _nt(hn, wu_ref[cols, :]).astype(BF16)

    return _call(
        body,
        (hn2, wg_t, wu_t),
        name="ffn_up",
        grid=(s // ts,),
        in_specs=[_rows(ts, D_MODEL), _resident((D_FF, D_MODEL)), _resident((D_FF, D_MODEL))],
        out_specs=[_rows(ts, D_FF), _rows(ts, D_FF)],
        out_shape=[jax.ShapeDtypeStruct((s, D_FF), BF16), jax.ShapeDtypeStruct((s, D_FF), BF16)],
    )


def _silu_mul(gt, up):
    return (gt * jax.nn.sigmoid(gt)) * up


def _ffn_down_ple(gt, up, h1, w_down, p, target, g_ple, w_pg, w_pp):
    s = h1.shape[0]
    ts = min(TOKEN_TILE, s)
    blk = D_MODEL // N_DEV

    def body(gt_ref, up_ref, h1_ref, wd_ref, p_ref, t_ref, g_ref, wpg_ref, wpp_ref,
             loss_ref, dh_ref, dwpg_ref, dwpp_ref, dg_ref):
        @pl.when(pl.program_id(0) == 0)
        def _():
            loss_ref[...] = jnp.zeros_like(loss_ref)
            dwpg_ref[...] = jnp.zeros_like(dwpg_ref)
            dwpp_ref[...] = jnp.zeros_like(dwpp_ref)
            dg_ref[...] = jnp.zeros_like(dg_ref)

        h2v = h1_ref[...]
        for c in range(D_FF // FF_CHUNK):
            cols = slice(c * FF_CHUNK, (c + 1) * FF_CHUNK)
            act = _silu_mul(gt_ref[:, cols].astype(F32), up_ref[:, cols].astype(F32)).astype(BF16)
            h2v = _nn(act, wd_ref[cols, :]) + h2v
        r = _rms(h2v)
        hn = ((h2v * r) * g_ref[...]).astype(BF16)
        gate = jax.nn.sigmoid(_nn(hn, wpg_ref[...]))
        pb = p_ref[...].astype(BF16)
        pp = _nn(pb, jnp.concatenate([wpp_ref[j] for j in range(N_DEV)], axis=1))
        diff = (h2v + gate * pp) - t_ref[...]
        loss_ref[...] += jnp.sum(jnp.sum(diff * diff, axis=0, keepdims=True), axis=1, keepdims=True) * (0.5 / D_MODEL)
        dy = diff * (1.0 / D_MODEL)
        d_pp = (dy * gate).astype(BF16)
        d_pre = ((dy * pp) * (gate * (1.0 - gate))).astype(BF16)
        d_x, d_g = _rms_bwd(_nt(d_pre, wpg_ref[...]), h2v, r, g_ref[...])
        dg_ref[...] += d_g
        dh_ref[...] = dy + d_x
        d_wpp = _tn(pb, d_pp)
        for j in range(N_DEV):
            dwpp_ref[j] += d_wpp[:, j * blk:(j + 1) * blk]
        dwpg_ref[...] += _tn(hn, d_pre)

    return _call(
        body,
        (gt, up, h1, w_down, p, target, g_ple, w_pg, w_pp),
        name="ffn_down_ple",
        grid=(s // ts,),
        in_specs=[
            _rows(ts, D_FF),
            _rows(ts, D_FF),
            _rows(ts, D_MODEL),
            _resident((D_FF, D_MODEL)),
            _rows(ts, PLE_DIM),
            _rows(ts, D_MODEL),
            _resident((1, D_MODEL)),
            _resident((D_MODEL, D_MODEL)),
            _resident((N_DEV, PLE_DIM, blk)),
        ],
        out_specs=[
            _acc((1, SMALL_LANES)),
            _rows(ts, D_MODEL),
            _acc((D_MODEL, D_MODEL)),
            _acc((N_DEV, PLE_DIM, blk)),
            _acc((1, D_MODEL)),
        ],
        out_shape=[
            jax.ShapeDtypeStruct((1, SMALL_LANES), F32),
            jax.ShapeDtypeStruct((s, D_MODEL), F32),
            jax.ShapeDtypeStruct((D_MODEL, D_MODEL), F32),
            jax.ShapeDtypeStruct((N_DEV, PLE_DIM, blk), F32),
            jax.ShapeDtypeStruct((1, D_MODEL), F32),
        ],
    )


def _ffn_bwd_act(dh2, h1, gt, up, g_ffn, wg_t, wu_t, w_down):
    s = h1.shape[0]
    ts = min(FFN_BWD_TILE, s)

    def body(dh_ref, h1_ref, gt_ref, up_ref, g_ref, wg_ref, wu_ref, wd_ref,
             dgt_ref, dup_ref, dh1_ref, dh1b_ref, dg_ref, dwd_ref, act_ref):
        @pl.when(pl.program_id(0) == 0)
        def _():
            dg_ref[...] = jnp.zeros_like(dg_ref)
            dwd_ref[...] = jnp.zeros_like(dwd_ref)

        dhb = dh_ref[...].astype(BF16)
        d_hn = jnp.zeros((ts, D_MODEL), F32)
        for c in range(D_FF // FF_CHUNK):
            cols = slice(c * FF_CHUNK, (c + 1) * FF_CHUNK)
            d_act = _nt(dhb, wd_ref[cols, :])
            gtv = gt_ref[:, cols].astype(F32)
            upv = up_ref[:, cols].astype(F32)
            sg = jax.nn.sigmoid(gtv)
            silu = gtv * sg
            act_ref[:, cols] = (silu * upv).astype(BF16)
            d_up = (d_act * silu).astype(BF16)
            d_gt = ((d_act * upv) * (sg * (1.0 + gtv * (1.0 - sg)))).astype(BF16)
            dup_ref[:, cols] = d_up
            dgt_ref[:, cols] = d_gt
            d_hn = (_nn(d_gt, wg_ref[cols, :]) + _nn(d_up, wu_ref[cols, :])) + d_hn
        dwd_ref[...] += _tn(act_ref[...], dhb)
        h1v = h1_ref[...]
        d_x, d_g = _rms_bwd(d_hn, h1v, _rms(h1v), g_ref[...])
        dg_ref[...] += d_g
        dh1 = dh_ref[...] + d_x
        dh1_ref[...] = dh1
        dh1b_ref[...] = dh1.astype(BF16)

    return _call(
        body,
        (dh2, h1, gt, up, g_ffn, wg_t, wu_t, w_down),
        name="ffn_bwd_act",
        grid=(s // ts,),
        in_specs=[
            _rows(ts, D_MODEL),
            _rows(ts, D_MODEL),
            _rows(ts, D_FF),
            _rows(ts, D_FF),
            _resident((1, D_MODEL)),
            _resident((D_FF, D_MODEL)),
            _resident((D_FF, D_MODEL)),
            _resident((D_FF, D_MODEL)),
        ],
        out_specs=[
            _rows(ts, D_FF), _rows(ts, D_FF),
            _rows(ts, D_MODEL), _rows(ts, D_MODEL), _acc((1, D_MODEL)), _acc((D_FF, D_MODEL)),
        ],
        out_shape=[
            jax.ShapeDtypeStruct((s, D_FF), BF16),
            jax.ShapeDtypeStruct((s, D_FF), BF16),
            jax.ShapeDtypeStruct((s, D_MODEL), F32),
            jax.ShapeDtypeStruct((s, D_MODEL), BF16),
            jax.ShapeDtypeStruct((1, D_MODEL), F32),
            jax.ShapeDtypeStruct((D_FF, D_MODEL), F32),
        ],
        scratch_shapes=[pltpu.VMEM((ts, D_FF), BF16)],
    )


def _ffn_bwd_w(dgt, dup, hn2):
    s = hn2.shape[0]
    slab = pl.BlockSpec((s, FFN_W_SLAB), lambda i: (0, i))

    def body(dgt_ref, dup_ref, hn_ref, dwg_ref, dwu_ref):
        hn = hn_ref[...]
        dwg_ref[...] = _tn(dgt_ref[...], hn)
        dwu_ref[...] = _tn(dup_ref[...], hn)

    return _call(
        body,
        (dgt, dup, hn2),
        name="ffn_bwd_w",
        grid=(D_FF // FFN_W_SLAB,),
        in_specs=[slab, slab, _resident((s, D_MODEL))],
        out_specs=[_rows(FFN_W_SLAB, D_MODEL)] * 2,
        out_shape=[jax.ShapeDtypeStruct((D_FF, D_MODEL), F32)] * 2,
    )


def _mix_bwd(dh1b, u, a, m, w_out, w_pool, pool_scale):
    s = u.shape[0]
    ts = min(TOKEN_TILE, s)
    nt = s // ts
    halo_after = lambda i: (jnp.minimum((i + 1) * (ts // POOL_HALO), s // POOL_HALO - 1), 0)
    n_groups = len(POOL_SIZES)

    def body(dh_ref, dhn_ref, u_ref, uh_ref, a_ref, m_ref, wo_ref, wp_ref, sc_ref,
             da_ref, du_ref, dwp_ref, dsc_ref, dwo_ref):
        i = pl.program_id(0)

        @pl.when(i == 0)
        def _():
            dwp_ref[...] = jnp.zeros_like(dwp_ref)
            dsc_ref[...] = jnp.zeros_like(dsc_ref)
            dwo_ref[...] = jnp.zeros_like(dwo_ref)

        dh = dh_ref[...]
        dwo_ref[:ATTN_WIDTH, :] += _tn(a_ref[...], dh)
        dwo_ref[ATTN_WIDTH:, :] += _tn(m_ref[...], dh)
        da_ref[...] = _nt(dh, wo_ref[:ATTN_WIDTH, :])
        dh_next = jnp.where(i < nt - 1, dhn_ref[...], jnp.zeros_like(dhn_ref))
        dm_ext = _nt(jnp.concatenate([dh, dh_next], axis=0), wo_ref[ATTN_WIDTH:, :])
        pooled = _pooled(u_ref[...], uh_ref[...], i, ts)
        t_ext = i * ts + lax.broadcasted_iota(jnp.int32, (ts + POOL_HALO, 1), 0)
        for g, w in enumerate(POOL_SIZES):
            lanes = slice(g * POOL_GROUP, (g + 1) * POOL_GROUP)
            wp = wp_ref[g].astype(BF16)
            pg = pooled[g].astype(BF16)
            dm_g = dm_ext[:, lanes]
            dsc_ref[:, lanes] += jnp.sum(dm_g[:ts, :] * _nn(pg, wp), axis=0, keepdims=True)
            dy = (dm_g * sc_ref[:, lanes]).astype(BF16)
            dwp_ref[g] += _tn(pg, dy[:ts, :])
            d_pool = _nt(dy, wp)
            acc = d_pool / jnp.minimum(t_ext + 1, w).astype(F32)
            shift = 1
            while shift < w:
                acc = acc + pltpu.roll(acc, ts + POOL_HALO - shift, 0)
                shift *= 2
            du_ref[:, lanes] = (acc[:ts, :] - d_pool[:ts, :]).astype(BF16)

    return _call(
        body,
        (dh1b, dh1b, u, u, a, m, w_out, w_pool, pool_scale),
        name="mix_bwd",
        grid=(nt,),
        in_specs=[
            _rows(ts, D_MODEL),
            pl.BlockSpec((POOL_HALO, D_MODEL), halo_after),
            _rows(ts, POOL_WIDTH),
            pl.BlockSpec((POOL_HALO, POOL_WIDTH), _halo_before(ts)),
            _rows(ts, ATTN_WIDTH),
            _rows(ts, POOL_WIDTH),
            _resident((D_MODEL, D_MODEL)),
            _resident((n_groups, POOL_GROUP, POOL_GROUP)),
            _resident((1, POOL_WIDTH)),
        ],
        out_specs=[
            _rows(ts, ATTN_WIDTH),
            _rows(ts, POOL_WIDTH),
            _acc((n_groups, POOL_GROUP, POOL_GROUP)),
            _acc((1, POOL_WIDTH)),
            _acc((D_MODEL, D_MODEL)),
        ],
        out_shape=[
            jax.ShapeDtypeStruct((s, ATTN_WIDTH), F32),
            jax.ShapeDtypeStruct((s, POOL_WIDTH), BF16),
            jax.ShapeDtypeStruct((n_groups, POOL_GROUP, POOL_GROUP), F32),
            jax.ShapeDtypeStruct((1, POOL_WIDTH), F32),
            jax.ShapeDtypeStruct((D_MODEL, D_MODEL), F32),
        ],
    )


def _attn_bwd(qn, kn, v, a, da, tab, sinks):
    s = qn.shape[0]
    qb = ATTN_STEP_BLOCKS
    rows = qb * BLOCK
    n_groups = s // rows
    group, prev = _attn_specs(n_groups)
    done = lambda n: (jnp.maximum(n - 1, 0), 0)

    def body(sink_ref, q_ref, kc_ref, kp_ref, vc_ref, vp_ref, o_ref, do_ref, tab_ref,
             dq_ref, dk_ref, dv_ref, dl_ref, ds_ref, k_carry, v_carry, sink_acc):
        n = pl.program_id(0)

        @pl.when(n == 0)
        def _():
            dl_ref[...] = jnp.zeros_like(dl_ref)
            k_carry[...] = jnp.zeros_like(k_carry)
            v_carry[...] = jnp.zeros_like(v_carry)
            sink_acc[...] = jnp.zeros_like(sink_acc)

        @pl.when(n < n_groups)
        def _():
            first = n == 0
            lo_mask = _lane_lo((BLOCK, BLOCK))
            chains = [(b, half) for b in range(qb) for half in range(2)]
            tile = lambda ref, b, p: ref[b * BLOCK:(b + 1) * BLOCK, p * BLOCK:(p + 1) * BLOCK]
            keys = [_band(kp_ref, kc_ref, b) for b in range(qb)]
            vals = [_band(vp_ref, vc_ref, b) for b in range(qb)]
            q_st = [_stack_heads([tile(q_ref, b, p) for p in range(4)], lo_mask) for b in range(qb)]
            do_st = [_stack_heads([tile(do_ref, b, p) for p in range(4)], lo_mask) for b in range(qb)]
            o_st = [_stack_heads([tile(o_ref, b, p).astype(F32) for p in range(4)], lo_mask) for b in range(qb)]
            sink_col = [_sink_column(sink_ref, heads) for heads in (HEADS_A, HEADS_B)]
            scores = {(b, h): _band_scores(q_st[b][h], keys[b][h], tab_ref[h], first if b == 0 else None)
                      for b, h in chains}
            dob = {(b, h): do_st[b][h].astype(BF16) for b, h in chains}
            d_probs = {(b, h): _nt(dob[b, h], vals[b][h]) for b, h in chains}
            delta = {(b, h): jnp.sum(do_st[b][h] * o_st[b][h], axis=-1, keepdims=True) for b, h in chains}
            soft = {(b, h): _softmax_with_sink(scores[b, h], sink_col[h]) for b, h in chains}
            dl = {(b, h): soft[b, h][0] * (d_probs[b, h] - delta[b, h]) for b, h in chains}
            for b, h in chains:
                dl_ref[h] += dl[b, h]
                sink_acc[h] += soft[b, h][1] * delta[b, h]
            dsb = {(b, h): (dl[b, h] * (HEAD_DIM ** -0.5)).astype(BF16) for b, h in chains}
            dq_st = {(b, h): _nn(dsb[b, h], keys[b][h]) for b, h in chains}
            dk_parts = {(b, h): _tn(dsb[b, h], q_st[b][h]) for b, h in chains}
            dv_parts = {(b, h): _tn(soft[b, h][0].astype(BF16), dob[b, h]) for b, h in chains}
            for b in range(qb):
                dq = _unstack_heads(dq_st[b, 0], dq_st[b, 1], lo_mask)
                for p in range(4):
                    dq_ref[b * BLOCK:(b + 1) * BLOCK, p * BLOCK:(p + 1) * BLOCK] = dq[p]
            dks = [dk_parts[b, 0] + pltpu.roll(dk_parts[b, 1], HEAD_DIM, 1) for b in range(qb)]
            dvs = [dv_parts[b, 0] + pltpu.roll(dv_parts[b, 1], HEAD_DIM, 1) for b in range(qb)]
            last = slice((qb - 1) * BLOCK, qb * BLOCK)
            for parts, out_ref, carry in ((dks, dk_ref, k_carry), (dvs, dv_ref, v_carry)):
                out_ref[...] = carry[...]
                out_ref[last, :] += parts[0][:BLOCK, :]
                for b in range(qb):
                    own = parts[b][BLOCK:, :]
                    carry[b * BLOCK:(b + 1) * BLOCK, :] = own + parts[b + 1][:BLOCK, :] if b + 1 < qb else own

        @pl.when(n == n_groups)
        def _():
            dk_ref[...] = k_carry[...]
            dv_ref[...] = v_carry[...]
            for half, heads in enumerate((HEADS_A, HEADS_B)):
                for slot, h in enumerate(heads):
                    tot = jnp.sum(sink_acc[half, slot * BLOCK:(slot + 1) * BLOCK, :], axis=0, keepdims=True)
                    ds_ref[h:h + 1, :] = jnp.broadcast_to(-tot, (1, SMALL_LANES))

    return _call(
        body,
        (sinks, qn, kn, kn, v, v, a, da, tab),
        name="attn_bwd",
        grid=(n_groups + 1,),
        in_specs=[
            pl.BlockSpec(memory_space=pltpu.SMEM),
            pl.BlockSpec((rows, ATTN_WIDTH), group),
            pl.BlockSpec((rows, KV_WIDTH), group),
            pl.BlockSpec((BLOCK, KV_WIDTH), prev),
            pl.BlockSpec((rows, KV_WIDTH), group),
            pl.BlockSpec((BLOCK, KV_WIDTH), prev),
            pl.BlockSpec((rows, ATTN_WIDTH), group),
            pl.BlockSpec((rows, ATTN_WIDTH), group),
            _resident((2, 4 * BLOCK, 2 * BLOCK)),
        ],
        out_specs=[
            pl.BlockSpec((rows, ATTN_WIDTH), group),
            pl.BlockSpec((rows, KV_WIDTH), done),
            pl.BlockSpec((rows, KV_WIDTH), done),
            _acc((2, 4 * BLOCK, 2 * BLOCK)),
            _acc((N_DEV, SMALL_LANES)),
        ],
        out_shape=[
            jax.ShapeDtypeStruct((s, ATTN_WIDTH), F32),
            jax.ShapeDtypeStruct((s, KV_WIDTH), F32),
            jax.ShapeDtypeStruct((s, KV_WIDTH), F32),
            jax.ShapeDtypeStruct((2, 4 * BLOCK, 2 * BLOCK), F32),
            jax.ShapeDtypeStruct((N_DEV, SMALL_LANES), F32),
        ],
        scratch_shapes=[
            pltpu.VMEM((rows, KV_WIDTH), F32),
            pltpu.VMEM((rows, KV_WIDTH), F32),
            pltpu.VMEM((2, 4 * BLOCK, 1), F32),
        ],
    )


def _fold_heads(acc):
    t = acc + pltpu.roll(acc, HEAD_DIM, 1)
    out = t[:, :SMALL_LANES]
    for g in range(1, acc.shape[1] // SMALL_LANES):
        out = out + t[:, g * SMALL_LANES:(g + 1) * SMALL_LANES]
    return out


def _in_proj_bwd(dqn, dkn, dv, du, zqk, x, dh1, g_attn, gq_t, gk_t, w_in_t):
    s = x.shape[0]
    ts = min(TOKEN_TILE, s)
    nt = s // ts

    def head_norm_bwd(d_n, raw, g_t, bmat):
        r = lax.rsqrt(_seg_mean(raw * raw, bmat) + EPS)
        gy = d_n * g_t
        d_raw = r * gy - raw * (r * r * r) * _seg_mean(gy * raw, bmat)
        return d_raw, jnp.sum(d_n * (raw * r), axis=0, keepdims=True)

    def body(dqn_ref, dkn_ref, dv_ref, du_ref, zqk_ref, x_ref, dh1_ref, g_ref, gq_ref, gk_ref, w_ref, bq_ref, bk_ref,
             gx_ref, dw_ref, dg_ref, dgq_ref, dgk_ref, dz_ref, gq_acc, gk_acc):
        i = pl.program_id(0)

        @pl.when(i == 0)
        def _():
            dw_ref[...] = jnp.zeros_like(dw_ref)
            dg_ref[...] = jnp.zeros_like(dg_ref)
            gq_acc[...] = jnp.zeros_like(gq_acc)
            gk_acc[...] = jnp.zeros_like(gk_acc)

        d_q, d_gq = head_norm_bwd(dqn_ref[...], zqk_ref[:, :ATTN_WIDTH], gq_ref[...], bq_ref[...])
        d_k, d_gk = head_norm_bwd(dkn_ref[...], zqk_ref[:, ATTN_WIDTH:], gk_ref[...], bk_ref[...])
        gq_acc[...] += d_gq
        gk_acc[...] += d_gk
        dz_ref[:, :ATTN_WIDTH] = d_q.astype(BF16)
        dz_ref[:, ATTN_WIDTH:ATTN_WIDTH + KV_WIDTH] = d_k.astype(BF16)
        dz_ref[:, ATTN_WIDTH + KV_WIDTH:ATTN_WIDTH + 2 * KV_WIDTH] = dv_ref[...].astype(BF16)
        dz_ref[:, ATTN_WIDTH + 2 * KV_WIDTH:] = du_ref[...]
        dz = dz_ref[...]
        xf = x_ref[...]
        r = _rms(xf)
        hn = ((xf * r) * g_ref[...]).astype(BF16)
        d_x, d_g = _rms_bwd(_nn(dz, w_ref[...]), xf, r, g_ref[...])
        dg_ref[...] += d_g
        gx_ref[...] = dh1_ref[...] + d_x
        dw_ref[...] += _tn(dz, hn)

        @pl.when(i == nt - 1)
        def _():
            dgq_ref[...] = _fold_heads(gq_acc[...])
            dgk_ref[...] = _fold_heads(gk_acc[...])

    return _call(
        body,
        (dqn, dkn, dv, du, zqk, x, dh1, g_attn, gq_t, gk_t, w_in_t,
      _head_mean_matrix(ATTN_WIDTH), _head_mean_matrix(KV_WIDTH)),
        name="in_proj_bwd",
        grid=(nt,),
        in_specs=[
            _rows(ts, ATTN_WIDTH),
            _rows(ts, KV_WIDTH),
            _rows(ts, KV_WIDTH),
            _rows(ts, POOL_WIDTH),
            _rows(ts, ATTN_WIDTH + KV_WIDTH),
            _rows(ts, D_MODEL),
            _rows(ts, D_MODEL),
            _resident((1, D_MODEL)),
            _resident((1, ATTN_WIDTH)),
            _resident((1, KV_WIDTH)),
            _resident((IN_WIDTH, D_MODEL)),
            _resident((ATTN_WIDTH, ATTN_WIDTH)),
            _resident((KV_WIDTH, KV_WIDTH)),
        ],
        out_specs=[
            _rows(ts, D_MODEL),
            _acc((IN_WIDTH, D_MODEL)),
            _acc((1, D_MODEL)),
            _acc((1, SMALL_LANES)),
            _acc((1, SMALL_LANES)),
        ],
        out_shape=[
            jax.ShapeDtypeStruct((s, D_MODEL), F32),
            jax.ShapeDtypeStruct((IN_WIDTH, D_MODEL), F32),
            jax.ShapeDtypeStruct((1, D_MODEL), F32),
            jax.ShapeDtypeStruct((1, SMALL_LANES), F32),
            jax.ShapeDtypeStruct((1, SMALL_LANES), F32),
        ],
        scratch_shapes=[
            pltpu.VMEM((ts, IN_WIDTH), BF16),
            pltpu.VMEM((1, ATTN_WIDTH), F32),
            pltpu.VMEM((1, KV_WIDTH), F32),
        ],
    )


BIG_WEIGHTS = (
    ("w_in", True, IN_WIDTH // N_DEV, D_MODEL),
    ("w_out", False, D_MODEL // N_DEV, D_MODEL),
    ("w_gate", True, D_FF // N_DEV, D_MODEL),
    ("w_up", True, D_FF // N_DEV, D_MODEL),
    ("w_down", False, D_FF // N_DEV, D_MODEL),
    ("w_ple_gate", False, D_MODEL // N_DEV, D_MODEL),
    ("w_ple_proj", False, PLE_DIM, D_MODEL // N_DEV),
)
N_BIG = len(BIG_WEIGHTS)


def _place():
    x, y, c = lax.axis_index("x"), lax.axis_index("y"), lax.axis_index("c")
    chips = [(1 - x, y), (x, 1 - y), (1 - x, 1 - y)]
    return x, y, c, chips


class _Gather:
    def __init__(self, n):
        self.n = n
        self.sems = [pltpu.SemaphoreType.DMA((n, 7)), pltpu.SemaphoreType.DMA((n, 7)), pltpu.SemaphoreType.DMA((n,))]

    def _ctx(self, srcs, outs, sems):
        send_sems, recv_sems, local_sems = sems
        x, y, c, chips = _place()
        me, sibling = (x, y, c), (x, y, 1 - c)

        def block(k, owner):
            px, py, pc = owner
            return outs[k].at[4 * px + 2 * py + pc]

        def copy(k, idx, owner, to, mine=False):
            return pltpu.make_async_remote_copy(
                src_ref=srcs[k] if mine else block(k, owner), dst_ref=block(k, owner),
                send_sem=send_sems.at[k, idx], recv_sem=recv_sems.at[k, idx], device_id=to, device_id_type=MESH)

        def local(k):
            return pltpu.make_async_copy(srcs[k], block(k, me), local_sems.at[k])

        return c, chips, me, sibling, copy, local

    def begin(self, srcs, outs, sems):
        c, chips, me, sibling, copy, local = self._ctx(srcs, outs, sems)
        for k in range(self.n):
            local(k).start()
            copy(k, 0, me, sibling, mine=True).start()
            for j, chip in enumerate(chips):
                copy(k, 1 + j, me, (*chip, c), mine=True).start()

    def middle(self, srcs, outs, sems):
        c, chips, me, sibling, copy, local = self._ctx(srcs, outs, sems)
        for j, chip in enumerate(chips):
            for k in range(self.n):
                copy(k, 1 + j, (*chip, c), me).wait_recv()
                copy(k, 4 + j, (*chip, c), sibling).start()

    def end(self, srcs, outs, sems):
        c, chips, me, sibling, copy, local = self._ctx(srcs, outs, sems)
        for k in range(self.n):
            copy(k, 0, sibling, me).wait_recv()
            for j, chip in enumerate(chips):
                copy(k, 4 + j, (*chip, 1 - c), me).wait_recv()
        for k in range(self.n):
            copy(k, 0, me, sibling, mine=True).wait_send()
            for j, chip in enumerate(chips):
                copy(k, 1 + j, me, (*chip, c), mine=True).wait_send()
                copy(k, 4 + j, (*chip, c), sibling).wait_send()
            local(k).wait()


def _gather_rider(shards):
    g = _Gather(len(shards))
    shapes = [jax.ShapeDtypeStruct((N_DEV, *sh.shape), sh.dtype) for sh in shards]
    return _Rider(shards, shapes, g.sems, g.begin, g.end, g.middle)


def _cast_and_gather_first(shards, rel_bias_t):
    g = _Gather(1)
    any_spec = pl.BlockSpec(memory_space=pl.ANY)
    vmem = pl.BlockSpec(memory_space=pltpu.VMEM)

    def body(*refs):
        ins, rb_ref, outs = refs[:N_BIG], refs[N_BIG], refs[N_BIG + 1:2 * N_BIG + 1]
        gathered, tab_ref, sems = refs[2 * N_BIG + 1], refs[2 * N_BIG + 2], refs[2 * N_BIG + 3:]
        outs[0][...] = ins[0][...].astype(BF16)
        g.begin(outs[:1], [gathered], sems)
        for k in range(1, N_BIG):
            outs[k][...] = ins[k][...].astype(BF16)
        _write_bias_table(rb_ref, tab_ref)
        g.middle(outs[:1], [gathered], sems)
        g.end(outs[:1], [gathered], sems)

    res = pl.pallas_call(
        body,
        name="cast_and_gather_first",
        in_specs=[vmem] * N_BIG + [pl.BlockSpec(memory_space=pltpu.SMEM)],
        out_specs=[vmem] * N_BIG + [any_spec, vmem],
        out_shape=[jax.ShapeDtypeStruct((r, c), BF16) for _, _, r, c in BIG_WEIGHTS]
        + [jax.ShapeDtypeStruct((N_DEV, *BIG_WEIGHTS[0][2:]), BF16), jax.ShapeDtypeStruct(BIAS_TABLE_SHAPE, F32)],
        scratch_shapes=g.sems,
    )(*shards, rel_bias_t)
    return list(res[:N_BIG]), res[N_BIG], res[N_BIG + 1]


def _sibling_rider(grads):
    n = len(grads)

    def copies(gs, lands, sems):
        send_sems, recv_sems = sems
        x, y, c, _ = _place()
        return [
            pltpu.make_async_remote_copy(
                src_ref=gs[k].at[:, 1 - c], dst_ref=lands[k], send_sem=send_sems.at[k], recv_sem=recv_sems.at[k],
                device_id=(x, y, 1 - c), device_id_type=MESH)
            for k in range(n)
        ]

    def begin(gs, lands, sems):
        for cp in copies(gs, lands, sems):
            cp.start()

    def end(gs, lands, sems):
        for cp in copies(gs, lands, sems):
            cp.wait()

    shapes = [jax.ShapeDtypeStruct((N_CHIPS, *g.shape[2:]), F32) for g in grads]
    return _Rider(grads, shapes, [pltpu.SemaphoreType.DMA((n,)), pltpu.SemaphoreType.DMA((n,))], begin, end)


def _chip_of_relation(j, place):
    x, y = place[0], place[1]
    return jnp.where(j == 0, 2 * (1 - x) + y, jnp.where(j == 1, 2 * x + 1 - y, 2 * (1 - x) + 1 - y))


def _chip_sum(ks, place, grads, from_sibling):
    shapes = [BIG_WEIGHTS[k][2:] for k in ks]
    n_rel = N_CHIPS - 1
    per_step = n_rel if 2 * n_rel * sum(4 * r * c for r, c in shapes) <= CHIP_SUM_ONE_STEP_BYTES else 1
    operands, specs = [], []
    for (r, c), g, l in zip(shapes, grads, from_sibling):
        for q in range(per_step):
            chip = lambda j, place, q=q: _chip_of_relation(j * per_step + q, place)
            operands += [g, l]
            specs += [pl.BlockSpec((1, 1, r, c), lambda j, place, chip=chip: (chip(j, place), place[2], 0, 0)),
                      pl.BlockSpec((1, r, c), lambda j, place, chip=chip: (chip(j, place), 0, 0))]
    args, in_specs, _ = _after_last(operands, specs)

    def body(place_ref, *refs):
        ins, outs = refs[:len(operands)], refs[len(args):]
        for i in range(len(ks)):
            for q in range(per_step):
                mine_ref, sib_ref = ins[2 * (i * per_step + q):2 * (i * per_step + q) + 2]
                outs[i][q] = (mine_ref[0, 0] + sib_ref[0]).astype(BF16)

    outs = pl.pallas_call(
        body,
        name="chip_sum_" + "_".join(BIG_WEIGHTS[k][0] for k in ks),
        grid_spec=pltpu.PrefetchScalarGridSpec(
            num_scalar_prefetch=1,
            grid=(n_rel // per_step,),
            in_specs=in_specs,
            out_specs=[pl.BlockSpec((per_step, r, c), lambda j, place: (j, 0, 0)) for r, c in shapes],
        ),
        out_shape=[jax.ShapeDtypeStruct((n_rel, r, c), BF16) for r, c in shapes],
    )(place, *args)
    _mark_issued(outs[0])
    return list(outs)


def _chips_rider(to_send, small=None):
    n = len(to_send)
    inputs = list(to_send) + ([] if small is None else [small])
    shapes = [jax.ShapeDtypeStruct((3, *t.shape[1:]), BF16) for t in to_send]
    sems = [pltpu.SemaphoreType.DMA((max(n, 1), 3)), pltpu.SemaphoreType.DMA((max(n, 1), 3))]
    if small is not None:
        shapes.append(jax.ShapeDtypeStruct((N_DEV, *small.shape), F32))
        sems += [pltpu.SemaphoreType.DMA((7,)), pltpu.SemaphoreType.DMA((7,)), pltpu.SemaphoreType.DMA]

    def copies(ins, outs, sem_refs):
        x, y, c, chips = _place()
        out = []
        for k in range(n):
            for j, (px, py) in enumerate(chips):
                out.append(pltpu.make_async_remote_copy(
                    src_ref=ins[k].at[j], dst_ref=outs[k].at[j],
                    send_sem=sem_refs[0].at[k, j], recv_sem=sem_refs[1].at[k, j],
                    device_id=(px, py, c), device_id_type=MESH))
        local = None
        if small is not None:
            me = 4 * x + 2 * y + c
            local = pltpu.make_async_copy(ins[n], outs[n].at[me], sem_refs[4])
            rel = 0
            for fx in (0, 1):
                for fy in (0, 1):
                    for fc in (0, 1):
                        if (fx, fy, fc) != (0, 0, 0):
                            out.append(pltpu.make_async_remote_copy(
                                src_ref=ins[n], dst_ref=outs[n].at[me],
                                send_sem=sem_refs[2].at[rel], recv_sem=sem_refs[3].at[rel],
                                device_id=(x ^ fx, y ^ fy, c ^ fc), device_id_type=MESH))
                            rel += 1
        return out, local

    def begin(ins, outs, sem_refs):
        remote, local = copies(ins, outs, sem_refs)
        if local is not None:
            local.start()
        for cp in remote:
            cp.start()

    def end(ins, outs, sem_refs):
        remote, local = copies(ins, outs, sem_refs)
        for cp in remote:
            cp.wait()
        if local is not None:
            local.wait()

    return _Rider(inputs, shapes, sems, begin, end)


PEER_SETS = {"sibling": 1, "chips": 2, "sibling+chips": 3, "all": 4}


def _peers(pattern):
    x, y, c, chips = _place()
    sibling, others = [(x, y, 1 - c)], [(*chip, c) for chip in chips]
    if pattern == "all":
        return sibling + others + [(*chip, 1 - c) for chip in chips]
    return {"sibling": sibling, "chips": others, "sibling+chips": sibling + others}[pattern]


def _on_sequencer(name, pattern, rider):
    n_in, n_out = len(rider.inputs), len(rider.out_shapes)

    def body(*refs):
        ins, outs, sems = refs[:n_in], refs[n_in:n_in + n_out], refs[n_in + n_out:]
        peers = _peers(pattern)
        barrier = pltpu.get_barrier_semaphore()
        for peer in peers:
            pl.semaphore_signal(barrier, inc=1, device_id=peer, device_id_type=MESH)
        pl.semaphore_wait(barrier, len(peers))
        rider.begin(ins, outs, sems)
        if rider.middle is not None:
            rider.middle(ins, outs, sems)
        rider.end(ins, outs, sems)

    outs = pl.kernel(
        body,
        name=name,
        out_type=tuple(rider.out_shapes),
        mesh=plsc.ScalarSubcoreMesh(axis_name="sequencer", num_cores=1),
        scratch_types=tuple(rider.sems),
        compiler_params=pltpu.CompilerParams(collective_id=PEER_SETS[pattern]),
    )(*rider.inputs)
    return list(outs)


def _adamw(w, g, m, v):
    m = ADAM_B1 * m + (1.0 - ADAM_B1) * g
    v = ADAM_B2 * v + (1.0 - ADAM_B2) * jnp.square(g)
    m_hat = m / (1.0 - ADAM_B1 ** ADAM_STEP)
    v_hat = v / (1.0 - ADAM_B2 ** ADAM_STEP)
    delta = -ADAM_LR * (m_hat / (jnp.sqrt(v_hat) + ADAM_EPS) + ADAM_WD * w)
    return delta, m, v


def _adamw_big(ks, place, operands):
    n = len(ks)
    tiles = lambda i, place: (i, 0)
    in_specs, out_specs, out_shape = [], [], []
    for k in ks:
        _, _, r, c = BIG_WEIGHTS[k]
        tile = r // 2
        in_specs += [
            pl.BlockSpec((1, 1, tile, c), lambda i, place: (2 * place[0] + place[1], place[2], i, 0)),
            pl.BlockSpec((1, tile, c), lambda i, place: (2 * place[0] + place[1], i, 0)),
            pl.BlockSpec((3, tile, c), lambda i, place: (0, i, 0)),
        ] + [pl.BlockSpec((tile, c), tiles)] * 3
        out_specs += [pl.BlockSpec((tile, c), tiles)] * 4
        out_shape += [jax.ShapeDtypeStruct((r, c), F32)] * 4
    args, in_specs, _ = _after_last(sum((list(ops) for ops in operands), []), in_specs)

    def body(place_ref, *refs):
        ins, outs = refs[:6 * n], refs[len(args):]
        for i in range(n):
            mine_ref, sib_ref, land_ref, w_ref, m_ref, v_ref = ins[6 * i:6 * i + 6]
            g_ref, d_ref, nm_ref, nv_ref = outs[4 * i:4 * i + 4]
            g = mine_ref[0, 0] + sib_ref[0]
            g = ((g + land_ref[0].astype(F32)) + land_ref[1].astype(F32)) + land_ref[2].astype(F32)
            g_ref[...] = g
            d_ref[...], nm_ref[...], nv_ref[...] = _adamw(w_ref[...], g, m_ref[...], v_ref[...])

    outs = pl.pallas_call(
        body,
        name="adamw_" + "_".join(BIG_WEIGHTS[k][0] for k in ks),
        grid_spec=pltpu.PrefetchScalarGridSpec(
            num_scalar_prefetch=1, grid=(2,), in_specs=in_specs, out_specs=out_specs),
        out_shape=out_shape,
    )(place, *args)
    _mark_issued(outs[0])
    return [outs[4 * i:4 * i + 4] for i in range(n)]


def _pack_small(arrays):
    rows, offsets = [], []
    at = 0
    for a in arrays:
        if a.ndim != 2 or a.shape[1] != SMALL_LANES or a.shape[0] % 8:
            flat = a.reshape(-1)
            n_rows = -(-flat.shape[0] // (8 * SMALL_LANES)) * 8
            a = jnp.pad(flat, (0, n_rows * SMALL_LANES - flat.shape[0])).reshape(n_rows, SMALL_LANES)
        rows.append(a)
        offsets.append(at)
        at += a.shape[0]
    return jnp.concatenate(rows, axis=0), offsets


def _unpack_small(tot, at, shape):
    r, c = shape
    if r % 8 == 0:
        return tot[at:at + r, :c]
    assert r == 1
    if c <= SMALL_LANES:
        return tot[at:at + 1, :c]
    return jnp.concatenate([tot[at + j:at + j + 1, :] for j in range(c // SMALL_LANES)], axis=1)


def _small_update(packs, loss_at, grads_at, ws, ms, vs):
    n, n_packs = len(ws), len(packs)

    def body(*refs):
        pack_refs, refs = refs[:n_packs], refs[n_packs:]
        w_refs, m_refs, v_refs, loss_ref, outs = refs[:n], refs[n:2 * n], refs[2 * n:3 * n], refs[3 * n], refs[3 * n + 1:]
        tots = []
        for p_ref in pack_refs:
            tot = p_ref[0]
            for j in range(1, N_DEV):
                tot = tot + p_ref[j]
            tots.append(tot)
        loss_ref[...] = _unpack_small(tots[loss_at[0]], loss_at[1], (1, 1))
        for i, (pack, at) in enumerate(grads_at):
            g = _unpack_small(tots[pack], at, w_refs[i].shape)
            outs[i][...] = g
            outs[n + i][...], outs[2 * n + i][...], outs[3 * n + i][...] = _adamw(
                w_refs[i][...], g, m_refs[i][...], v_refs[i][...])

    shapes = [jax.ShapeDtypeStruct(w.shape, F32) for w in ws]
    outs = pl.pallas_call(body, name="small_update", out_shape=[jax.ShapeDtypeStruct((1, 1), F32)] + shapes * 4)(
        *packs, *ws, *ms, *vs)
    return outs[0], outs[1:]


SMALL_NAMES = ("g_attn_norm", "g_q", "g_k", "attn_sinks", "rel_bias", "w_pool", "pool_scale", "g_ffn_norm", "g_ple_norm")


def kernel(x, p, w_in, w_out, g_attn_norm, g_q, g_k, attn_sinks, rel_bias, w_pool, pool_scale, g_ffn_norm, w_gate, w_up, w_down, g_ple_norm, w_ple_gate, w_ple_proj, loss_target, m_w_in, m_w_out, m_g_attn_norm, m_g_q, m_g_k, m_attn_sinks, m_rel_bias, m_w_pool, m_pool_scale, m_g_ffn_norm, m_w_gate, m_w_up, m_w_down, m_g_ple_norm, m_w_ple_gate, m_w_ple_proj, v_w_in, v_w_out, v_g_attn_norm, v_g_q, v_g_k, v_attn_sinks, v_rel_bias, v_w_pool, v_pool_scale, v_g_ffn_norm, v_w_gate, v_w_up, v_w_down, v_g_ple_norm, v_w_ple_gate, v_w_ple_proj):
    weights = dict(w_in=w_in, w_out=w_out, g_attn_norm=g_attn_norm, g_q=g_q, g_k=g_k, attn_sinks=attn_sinks,
                   rel_bias=rel_bias, w_pool=w_pool, pool_scale=pool_scale, g_ffn_norm=g_ffn_norm, w_gate=w_gate,
                   w_up=w_up, w_down=w_down, g_ple_norm=g_ple_norm, w_ple_gate=w_ple_gate, w_ple_proj=w_ple_proj)
    m_in = dict(w_in=m_w_in, w_out=m_w_out, g_attn_norm=m_g_attn_norm, g_q=m_g_q, g_k=m_g_k, attn_sinks=m_attn_sinks,
                rel_bias=m_rel_bias, w_pool=m_w_pool, pool_scale=m_pool_scale, g_ffn_norm=m_g_ffn_norm, w_gate=m_w_gate,
                w_up=m_w_up, w_down=m_w_down, g_ple_norm=m_g_ple_norm, w_ple_gate=m_w_ple_gate, w_ple_proj=m_w_ple_proj)
    v_in = dict(w_in=v_w_in, w_out=v_w_out, g_attn_norm=v_g_attn_norm, g_q=v_g_q, g_k=v_g_k, attn_sinks=v_attn_sinks,
                rel_bias=v_rel_bias, w_pool=v_w_pool, pool_scale=v_pool_scale, g_ffn_norm=v_g_ffn_norm, w_gate=v_w_gate,
                w_up=v_w_up, w_down=v_w_down, g_ple_norm=v_g_ple_norm, w_ple_gate=v_w_ple_gate, w_ple_proj=v_w_ple_proj)

    _issued.clear()
    xs = x[0]
    ps = p[0, 0]
    target = loss_target[0]
    wp = w_pool[0]
    gq_t = jnp.tile(g_q, (1, ATTN_WIDTH // HEAD_DIM))
    gk_t = jnp.tile(g_k, (1, KV_WIDTH // HEAD_DIM))

    def to_blocks(k, arr):
        return jnp.swapaxes(arr[0], 0, 1) if BIG_WEIGHTS[k][1] else arr[0]

    def from_blocks(k, arr):
        return (jnp.swapaxes(arr, 0, 1) if BIG_WEIGHTS[k][1] else arr)[None]

    IN, OUT, GATE, UP, DOWN, PG, PP = range(N_BIG)
    full = lambda g: g.reshape(N_DEV * g.shape[1], g.shape[2])
    halves = lambda k, g: g.reshape(N_CHIPS, 2, *BIG_WEIGHTS[k][2:])
    place = jnp.stack([lax.axis_index("x"), lax.axis_index("y"), lax.axis_index("c")]).astype(jnp.int32)

    sh, w_in_g, tab = _cast_and_gather_first(
        [to_blocks(k, weights[name]) for k, (name, _, _, _) in enumerate(BIG_WEIGHTS)], rel_bias.T)
    w_in_t = full(w_in_g)

    (w_out_g,) = _on_sequencer("gather_out", "sibling+chips", _gather_rider([sh[OUT]]))
    wg_g, wu_g = _on_sequencer("gather_gate_up", "sibling+chips", _gather_rider([sh[GATE], sh[UP]]))
    wd_g, w_pg_g, w_pp_g = _on_sequencer("gather_down_ple", "sibling+chips", _gather_rider([sh[DOWN], sh[PG], sh[PP]]))
    (zqk, qn, kn, v, u) = _in_proj(xs, g_attn_norm, w_in_t, gq_t, gk_t)
    (a,) = _attn_fwd(qn, kn, v, tab, attn_sinks)
    w_out_f = full(w_out_g)
    (h1, hn2, m_out) = _mix_out(u, a, xs, w_out_f, wp, pool_scale, g_ffn_norm)
    wg_t, wu_t = full(wg_g), full(wu_g)
    (gt, up) = _ffn_up(hn2, wg_t, wu_t)
    w_down_f = full(wd_g)

    partial, from_sibling, sums, landed = [None] * N_BIG, [None] * N_BIG, [None] * N_BIG, [None] * N_BIG

    def to_sibling(name, ks, grads):
        for k, g in zip(ks, grads):
            partial[k] = halves(k, g)
        got = _on_sequencer(name, "sibling", _sibling_rider([partial[k] for k in ks]))
        for k, g in zip(ks, got):
            from_sibling[k] = g

    def chip_sum(*ks):
        for k, s in zip(ks, _chip_sum(ks, place, [partial[k] for k in ks], [from_sibling[k] for k in ks])):
            sums[k] = s

    def to_chips(name, ks, small=None):
        got = _on_sequencer(name, "chips" if small is None else "all", _chips_rider([sums[k] for k in ks], small))
        for k, g in zip(ks, got):
            landed[k] = g
        return got[len(ks):]

    (loss_part, dh2, d_wpg, d_wpp, d_g_ple) = _ffn_down_ple(
        gt, up, h1, w_down_f, ps, target, g_ple_norm, full(w_pg_g), w_pp_g)
    to_sibling("sibling_ple", (PG, PP), (d_wpg, d_wpp))
    (dgt, dup, dh1, dh1b, d_g_ffn, d_wd) = _ffn_bwd_act(dh2, h1, gt, up, g_ffn_norm, wg_t, wu_t, w_down_f)
    to_sibling("sibling_down", (DOWN,), (d_wd,))
    chip_sum(PG, PP)
    to_chips("chips_ple", (PG, PP))
    chip_sum(DOWN)
    to_chips("chips_down", (DOWN,))
    (d_wg_t, d_wu_t) = _ffn_bwd_w(dgt, dup, hn2)
    to_sibling("sibling_gate_up", (GATE, UP), (d_wg_t, d_wu_t))
    _complete_before_next([landed[PG], landed[PP], landed[DOWN]])
    (da, du, d_wpool, d_scale, d_wo) = _mix_bwd(dh1b, u, a, m_out, w_out_f, wp, pool_scale)
    to_sibling("sibling_out", (OUT,), (d_wo,))
    chip_sum(GATE, UP)
    to_chips("chips_gate_up", (GATE, UP))
    (dqn, dkn, dv, dl_acc, d_sinks) = _attn_bwd(qn, kn, v, a, da, tab, attn_sinks)
    chip_sum(OUT)
    to_chips("chips_out", (OUT,))
    early, early_at = _pack_small([d_wpool.reshape(POOL_WIDTH, POOL_GROUP), d_scale, d_g_ffn, d_g_ple, loss_part[:, :1]])
    (early_all,) = _on_sequencer("gather_early", "sibling+chips", _gather_rider([early]))
    (grad_x, d_win_t, d_g_attn, d_gq, d_gk) = _in_proj_bwd(dqn, dkn, dv, du, zqk, xs, dh1, g_attn_norm, gq_t, gk_t, w_in_t)
    to_sibling("sibling_in", (IN,), (d_win_t,))
    _complete_before_next([landed[OUT], landed[GATE], landed[UP], early_all])
    (d_rel_t,) = _bias_table_bwd(dl_acc)
    chip_sum(IN)
    late, late_at = _pack_small([d_g_attn, d_gq[:, :HEAD_DIM], d_gk[:, :HEAD_DIM], d_sinks[:, 0], d_rel_t])
    (late_all,) = to_chips("chips_in", (IN,), late)

    out = {"grad": {}, "delta": {}, "new_m": {}, "new_v": {}}
    for ks in ((PG, PP, DOWN), (OUT, GATE, UP), (IN,)):
        names = [BIG_WEIGHTS[k][0] for k in ks]
        results = _adamw_big(ks, place, [
            (partial[k], from_sibling[k], landed[k], to_blocks(k, weights[n]), to_blocks(k, m_in[n]),
             to_blocks(k, v_in[n])) for k, n in zip(ks, names)])
        for k, name, res in zip(ks, names, results):
            for kind, r in zip(("grad", "delta", "new_m", "new_v"), res):
                out[kind][name] = from_blocks(k, r)
    def as_rows(name, arr):
        return arr.T if name == "rel_bias" else arr.reshape(POOL_WIDTH, POOL_GROUP) if name == "w_pool" else arr

    def from_rows(name, arr):
        return arr.T if name == "rel_bias" else arr.reshape(w_pool.shape) if name == "w_pool" else arr

    grads_at = dict(w_pool=(0, early_at[0]), pool_scale=(0, early_at[1]), g_ffn_norm=(0, early_at[2]),
                    g_ple_norm=(0, early_at[3]), g_attn_norm=(1, late_at[0]), g_q=(1, late_at[1]), g_k=(1, late_at[2]),
                    attn_sinks=(1, late_at[3]), rel_bias=(1, late_at[4]))
    loss, updates = _small_update(
        [early_all, late_all], (0, early_at[4]), [grads_at[n] for n in SMALL_NAMES],
        [as_rows(n, weights[n]) for n in SMALL_NAMES], [as_rows(n, m_in[n]) for n in SMALL_NAMES],
        [as_rows(n, v_in[n]) for n in SMALL_NAMES])
    loss = loss.reshape(())
    n_small = len(SMALL_NAMES)
    for j, kind in enumerate(("grad", "delta", "new_m", "new_v")):
        for i, name in enumerate(SMALL_NAMES):
            out[kind][name] = from_rows(name, updates[j * n_small + i])

    _issued.clear()
    order = ("w_in", "w_out", "g_attn_norm", "g_q", "g_k", "attn_sinks", "rel_bias", "w_pool", "pool_scale",
             "g_ffn_norm", "w_gate", "w_up", "w_down", "g_ple_norm", "w_ple_gate", "w_ple_proj")
    return (loss, grad_x[None], *[out["grad"][n] for n in order], *[out["delta"][n] for n in order],
            *[out["new_m"][n] for n in order], *[out["new_v"][n] for n in order])
```

```python
import math

import jax
import jax.numpy as jnp
import numpy as np
from jax import lax
from jax.experimental import pallas as pl
from jax.experimental.pallas import tpu as pltpu
from jax.experimental.pallas import tpu_sc as plsc

F32 = jnp.float32
BF16 = jnp.bfloat16
MESH = pl.DeviceIdType.MESH

D_MODEL = 1024
HEAD_DIM = 64
ATTN_WIDTH = 512
KV_WIDTH = 128
POOL_WIDTH = 512
POOL_SIZES = (2, 4, 8, 16)
POOL_GROUP = 128
POOL_HALO = 16
IN_WIDTH = 1280
D_FF = 2816
PLE_DIM = 256
BLOCK = 128
N_BUCKETS = 32
MAX_DISTANCE = 128
EPS = 1e-6
N_DEV = 8
N_CHIPS = 4

ADAM_LR = 0.001
ADAM_B1 = 0.9
ADAM_B2 = 0.999
ADAM_EPS = 1e-08
ADAM_WD = 0.01
ADAM_STEP = 10

TOKEN_TILE = 512
FFN_BWD_TILE = 256
FF_CHUNK = 256
CHIP_SUM_ONE_STEP_BYTES = 4 * 2 ** 20
FFN_W_SLAB = 256
ATTN_STEP_BLOCKS = 4
HEADS_A = (0, 2, 5, 7)
HEADS_B = (1, 3, 4, 6)
SMALL_LANES = 128


def _nn(a, b):
    return jnp.dot(a, b, preferred_element_type=F32)


def _nt(a, b):
    return lax.dot_general(a, b, (((1,), (1,)), ((), ())), preferred_element_type=F32)


def _tn(a, b):
    return lax.dot_general(a, b, (((0,), (0,)), ((), ())), preferred_element_type=F32)


def _resident(shape):
    nd = len(shape)
    return pl.BlockSpec(shape, lambda i, _nd=nd: (0,) * _nd, pipeline_mode=pl.Buffered(1))


def _rows(tile, width):
    return pl.BlockSpec((tile, width), lambda i: (i, 0))


def _acc(shape):
    nd = len(shape)
    return pl.BlockSpec(shape, lambda i, _nd=nd: (0,) * _nd)


def _head_mean_matrix(width):
    idx = np.arange(width) // HEAD_DIM
    return jnp.asarray((idx[:, None] == idx[None, :]).astype(np.float32) / HEAD_DIM, dtype=BF16)


def _seg_mean(v, bmat):
    hi = v.astype(BF16)
    lo = (v - hi.astype(F32)).astype(BF16)
    return _nn(hi, bmat) + _nn(lo, bmat)


def _rms(x):
    return lax.rsqrt(jnp.mean(x * x, axis=-1, keepdims=True) + EPS)


def _rms_bwd(d_y, x, r, g):
    gy = d_y * g
    d_x = r * gy - x * (r * r * r) * jnp.mean(gy * x, axis=-1, keepdims=True)
    d_g = jnp.sum(d_y * (x * r), axis=0, keepdims=True)
    return d_x, d_g


def _lane_lo(shape):
    return lax.broadcasted_iota(jnp.int32, shape, 1) < HEAD_DIM


class _Rider:
    def __init__(self, inputs, out_shapes, sems, begin, end, middle=None):
        self.inputs, self.out_shapes, self.sems = list(inputs), list(out_shapes), list(sems)
        self.begin, self.middle, self.end = begin, middle, end


_issued = []


def _after_last(args, in_specs):
    extra = list(_issued)
    return list(args) + extra, list(in_specs) + [pl.BlockSpec(memory_space=pl.ANY)] * len(extra), len(extra)


def _mark_issued(out):
    _issued[:] = [out]


def _complete_before_next(arrays):
    _issued.extend(arrays)


def _call(body, args, *, name, grid, in_specs, out_specs, out_shape, scratch_shapes=()):
    n_args = len(args)
    args, in_specs, _ = _after_last(args, in_specs)

    def ordered(*refs):
        body(*refs[:n_args], *refs[len(args):])

    outs = pl.pallas_call(ordered, name=name, grid=grid, in_specs=in_specs, out_specs=list(out_specs),
                          out_shape=list(out_shape), scratch_shapes=list(scratch_shapes))(*args)
    _mark_issued(outs[0])
    return list(outs)


def _in_proj(x, g_attn, w_in_t, gq_t, gk_t):
    s = x.shape[0]
    ts = min(TOKEN_TILE, s)

    def body(x_ref, g_ref, w_ref, gq_ref, gk_ref, bq_ref, bk_ref, zqk_ref, qn_ref, kn_ref, v_ref, u_ref):
        xf = x_ref[...]
        hn = ((xf * _rms(xf)) * g_ref[...]).astype(BF16)
        z = _nt(hn, w_ref[...])
        q = z[:, :ATTN_WIDTH]
        k = z[:, ATTN_WIDTH:ATTN_WIDTH + KV_WIDTH]
        zqk_ref[...] = z[:, :ATTN_WIDTH + KV_WIDTH]
        rq = lax.rsqrt(_seg_mean(q * q, bq_ref[...]) + EPS)
        qn_ref[...] = ((q * rq) * gq_ref[...]).astype(BF16)
        rk = lax.rsqrt(_seg_mean(k * k, bk_ref[...]) + EPS)
        kn_ref[...] = ((k * rk) * gk_ref[...]).astype(BF16)
        v_ref[...] = z[:, ATTN_WIDTH + KV_WIDTH:ATTN_WIDTH + 2 * KV_WIDTH].astype(BF16)
        u_ref[...] = z[:, ATTN_WIDTH + 2 * KV_WIDTH:]

    return _call(
        body,
        (x, g_attn, w_in_t, gq_t, gk_t, _head_mean_matrix(ATTN_WIDTH), _head_mean_matrix(KV_WIDTH)),
        name="in_proj",
        grid=(s // ts,),
        in_specs=[
            _rows(ts, D_MODEL),
            _resident((1, D_MODEL)),
            _resident((IN_WIDTH, D_MODEL)),
            _resident((1, ATTN_WIDTH)),
            _resident((1, KV_WIDTH)),
            _resident((ATTN_WIDTH, ATTN_WIDTH)),
            _resident((KV_WIDTH, KV_WIDTH)),
        ],
        out_specs=[
            _rows(ts, ATTN_WIDTH + KV_WIDTH),
            _rows(ts, ATTN_WIDTH),
            _rows(ts, KV_WIDTH),
            _rows(ts, KV_WIDTH),
            _rows(ts, POOL_WIDTH),
        ],
        out_shape=[
            jax.ShapeDtypeStruct((s, ATTN_WIDTH + KV_WIDTH), F32),
            jax.ShapeDtypeStruct((s, ATTN_WIDTH), BF16),
            jax.ShapeDtypeStruct((s, KV_WIDTH), BF16),
            jax.ShapeDtypeStruct((s, KV_WIDTH), BF16),
            jax.ShapeDtypeStruct((s, POOL_WIDTH), F32),
        ],
    )


def _bucket_ranges():
    n = np.arange(MAX_DISTANCE)
    max_exact = N_BUCKETS // 2
    nf = np.maximum(n, 1).astype(np.float64)
    large = max_exact + (np.log(nf / max_exact) / math.log(MAX_DISTANCE / max_exact) * (N_BUCKETS - max_exact)).astype(np.int64)
    bucket = np.where(n < max_exact, n, np.minimum(large, N_BUCKETS - 1))
    out = []
    for b in range(N_BUCKETS):
        idx = np.nonzero(bucket == b)[0]
        out.append((int(idx.min()), int(idx.max()) + 1))
    return out


def _band_distance():
    i = lax.broadcasted_iota(jnp.int32, (BLOCK, 2 * BLOCK), 0)
    j = lax.broadcasted_iota(jnp.int32, (BLOCK, 2 * BLOCK), 1)
    return BLOCK + i - j


BIAS_TABLE_SHAPE = (2, 4 * BLOCK, 2 * BLOCK)


def _write_bias_table(rb_ref, tab_ref):
    d = _band_distance()
    for half, heads in enumerate((HEADS_A, HEADS_B)):
        for slot, h in enumerate(heads):
            t = jnp.full((BLOCK, 2 * BLOCK), -jnp.inf, F32)
            for b, (lo, hi) in enumerate(_bucket_ranges()):
                t = jnp.where((d >= lo) & (d < hi), rb_ref[h, b], t)
            tab_ref[half, slot * BLOCK:(slot + 1) * BLOCK, :] = t


def _bias_table_bwd(dl_acc):
    ranges = _bucket_ranges()
    n_heads = len(HEADS_A) + len(HEADS_B)

    def body(dl_ref, out_ref):
        d = _band_distance()
        row = lax.broadcasted_iota(jnp.int32, (n_heads, SMALL_LANES), 0)
        lane = lax.broadcasted_iota(jnp.int32, (n_heads, SMALL_LANES), 1)
        out = jnp.zeros((n_heads, SMALL_LANES), F32)
        for b, (lo, hi) in enumerate(ranges):
            in_bucket = (d >= lo) & (d < hi)
            for half, heads in enumerate((HEADS_A, HEADS_B)):
                for slot, h in enumerate(heads):
                    g = dl_ref[half, slot * BLOCK:(slot + 1) * BLOCK, :]
                    part = jnp.sum(jnp.where(in_bucket, g, 0.0), axis=0, keepdims=True)
                    tot = jnp.sum(part, axis=1, keepdims=True)
                    out = jnp.where((row == h) & (lane == b), tot, out)
        out_ref[...] = out

    return _call(
        body,
        (dl_acc,),
        name="bias_table_bwd",
        grid=(1,),
        in_specs=[_acc((2, 4 * BLOCK, 2 * BLOCK))],
        out_specs=[_acc((n_heads, SMALL_LANES))],
        out_shape=[jax.ShapeDtypeStruct((n_heads, SMALL_LANES), F32)],
    )


def _stack_heads(pairs, lo_mask):
    zero = jnp.zeros_like(pairs[0])
    lo = [jnp.where(lo_mask, t, zero) for t in pairs]
    hi = [jnp.where(lo_mask, zero, t) for t in pairs]
    return (jnp.concatenate([lo[0], lo[1], hi[2], hi[3]], axis=0),
            jnp.concatenate([hi[0], hi[1], lo[2], lo[3]], axis=0))


def _unstack_heads(out_a, out_b, lo_mask):
    t = lambda x, r: x[r * BLOCK:(r + 1) * BLOCK, :]
    return [
        jnp.where(lo_mask, t(out_a, 0), t(out_b, 0)),
        jnp.where(lo_mask, t(out_a, 1), t(out_b, 1)),
        jnp.where(lo_mask, t(out_b, 2), t(out_a, 2)),
        jnp.where(lo_mask, t(out_b, 3), t(out_a, 3)),
    ]


def _sink_column(sink_ref, heads):
    row = lax.broadcasted_iota(jnp.int32, (4 * BLOCK, 1), 0)
    col = jnp.full((4 * BLOCK, 1), sink_ref[0, heads[3]], F32)
    for slot in (2, 1, 0):
        col = jnp.where(row < (slot + 1) * BLOCK, sink_ref[0, heads[slot]], col)
    return col


def _band_scores(q_stack, keys, tab, first_block):
    s = _nt(q_stack, keys) * (HEAD_DIM ** -0.5) + tab
    if first_block is not None:
        col = lax.broadcasted_iota(jnp.int32, s.shape, 1)
        s = jnp.where(jnp.logical_and(first_block, col < BLOCK), -jnp.inf, s)
    return s


def _softmax_with_sink(s, sink):
    m = jnp.maximum(jnp.max(s, axis=-1, keepdims=True), sink)
    e = jnp.exp(s - m)
    e_sink = jnp.exp(sink - m)
    den = jnp.sum(e, axis=-1, keepdims=True) + e_sink
    return e / den, e_sink / den


def _band_probs(q_stack, keys, tab, sink, first_block):
    return _softmax_with_sink(_band_scores(q_stack, keys, tab, first_block), sink)


def _attn_specs(n_groups):
    group = lambda n: (jnp.minimum(n, n_groups - 1), 0)
    prev = lambda n: (jnp.maximum(jnp.minimum(n, n_groups - 1) * ATTN_STEP_BLOCKS - 1, 0), 0)
    return group, prev


def _band(prev_ref, group_ref, b):
    rows = lambda i: group_ref[i * BLOCK:(i + 1) * BLOCK, :]
    band = jnp.concatenate([prev_ref[...] if b == 0 else rows(b - 1), rows(b)], axis=0)
    return band, pltpu.roll(band, HEAD_DIM, 1)


def _attn_fwd(qn, kn, v, tab, sinks):
    s = qn.shape[0]
    n_groups = s // (ATTN_STEP_BLOCKS * BLOCK)
    group, prev = _attn_specs(n_groups)
    rows = ATTN_STEP_BLOCKS * BLOCK

    def body(sink_ref, q_ref, kc_ref, kp_ref, vc_ref, vp_ref, tab_ref, o_ref):
        first = pl.program_id(0) == 0
        lo_mask = _lane_lo((BLOCK, BLOCK))
        for b in range(ATTN_STEP_BLOCKS):
            at = slice(b * BLOCK, (b + 1) * BLOCK)
            kk, kk_sw = _band(kp_ref, kc_ref, b)
            vv, vv_sw = _band(vp_ref, vc_ref, b)
            q_a, q_b = _stack_heads([q_ref[at, p * BLOCK:(p + 1) * BLOCK] for p in range(4)], lo_mask)
            no_prev = first if b == 0 else None
            p_a, _ = _band_probs(q_a, kk, tab_ref[0], _sink_column(sink_ref, HEADS_A), no_prev)
            p_b, _ = _band_probs(q_b, kk_sw, tab_ref[1], _sink_column(sink_ref, HEADS_B), no_prev)
            out = _unstack_heads(_nn(p_a.astype(BF16), vv), _nn(p_b.astype(BF16), vv_sw), lo_mask)
            for p in range(4):
                o_ref[at, p * BLOCK:(p + 1) * BLOCK] = out[p].astype(BF16)

    return _call(
        body,
        (sinks, qn, kn, kn, v, v, tab),
        name="attn_fwd",
        grid=(n_groups,),
        in_specs=[
            pl.BlockSpec(memory_space=pltpu.SMEM),
            pl.BlockSpec((rows, ATTN_WIDTH), group),
            pl.BlockSpec((rows, KV_WIDTH), group),
            pl.BlockSpec((BLOCK, KV_WIDTH), prev),
            pl.BlockSpec((rows, KV_WIDTH), group),
            pl.BlockSpec((BLOCK, KV_WIDTH), prev),
            _resident((2, 4 * BLOCK, 2 * BLOCK)),
        ],
        out_specs=[pl.BlockSpec((rows, ATTN_WIDTH), group)],
        out_shape=[jax.ShapeDtypeStruct((s, ATTN_WIDTH), BF16)],
    )


def _pooled(u_tile, u_halo, tile_index, tile_rows):
    halo = jnp.where(tile_index > 0, u_halo, 0.0)
    ext = jnp.concatenate([halo, u_tile], axis=0)
    sums = []
    acc = ext
    for shift in (1, 2, 4, 8):
        acc = acc + pltpu.roll(acc, shift, 0)
        sums.append(acc)
    t = tile_index * tile_rows + lax.broadcasted_iota(jnp.int32, (tile_rows, 1), 0)
    out = []
    for g, w in enumerate(POOL_SIZES):
        lanes = slice(g * POOL_GROUP, (g + 1) * POOL_GROUP)
        cnt = jnp.minimum(t + 1, w).astype(F32)
        out.append(sums[g][POOL_HALO:, lanes] / cnt - u_tile[:, lanes])
    return out


def _halo_before(tile):
    return lambda i: (jnp.maximum(i * (tile // POOL_HALO) - 1, 0), 0)


def _mix_out(u, a, x, w_out, w_pool, pool_scale, g_ffn):
    s = x.shape[0]
    ts = min(TOKEN_TILE, s)

    def body(u_ref, uh_ref, a_ref, x_ref, wo_ref, wp_ref, sc_ref, g_ref, h1_ref, hn_ref, m_ref):
        i = pl.program_id(0)
        pooled = _pooled(u_ref[...], uh_ref[...], i, ts)
        for g in range(len(POOL_SIZES)):
            lanes = slice(g * POOL_GROUP, (g + 1) * POOL_GROUP)
            y = _nn(pooled[g].astype(BF16), wp_ref[g].astype(BF16))
            m_ref[:, lanes] = (y * sc_ref[:, lanes]).astype(BF16)
        h1 = x_ref[...] + _nn(a_ref[...], wo_ref[:ATTN_WIDTH, :]) + _nn(m_ref[...], wo_ref[ATTN_WIDTH:, :])
        h1_ref[...] = h1
        hn_ref[...] = ((h1 * _rms(h1)) * g_ref[...]).astype(BF16)

    return _call(
        body,
        (u, u, a, x, w_out, w_pool, pool_scale, g_ffn),
        name="mix_out",
        grid=(s // ts,),
        in_specs=[
            _rows(ts, POOL_WIDTH),
            pl.BlockSpec((POOL_HALO, POOL_WIDTH), _halo_before(ts)),
            _rows(ts, ATTN_WIDTH),
            _rows(ts, D_MODEL),
            _resident((D_MODEL, D_MODEL)),
            _resident((len(POOL_SIZES), POOL_GROUP, POOL_GROUP)),
            _resident((1, POOL_WIDTH)),
            _resident((1, D_MODEL)),
        ],
        out_specs=[_rows(ts, D_MODEL), _rows(ts, D_MODEL), _rows(ts, POOL_WIDTH)],
        out_shape=[
            jax.ShapeDtypeStruct((s, D_MODEL), F32),
            jax.ShapeDtypeStruct((s, D_MODEL), BF16),
            jax.ShapeDtypeStruct((s, POOL_WIDTH), BF16),
        ],
    )


def _ffn_up(hn2, wg_t, wu_t):
    s = hn2.shape[0]
    ts = min(TOKEN_TILE, s)

    def body(hn_ref, wg_ref, wu_ref, gt_ref, up_ref):
        hn = hn_ref[...]
        for c in range(D_FF // FF_CHUNK):
            cols = slice(c * FF_CHUNK, (c + 1) * FF_CHUNK)
            gt_ref[:, cols] = _nt(hn, wg_ref[cols, :]).astype(BF16)
            up_ref[:, cols] = _nt(hn, wu_ref[cols, :]).astype(BF16)

    return _call(
        body,
        (hn2, wg_t, wu_t),
        name="ffn_up",
        grid=(s // ts,),
        in_specs=[_rows(ts, D_MODEL), _resident((D_FF, D_MODEL)), _resident((D_FF, D_MODEL))],
        out_specs=[_rows(ts, D_FF), _rows(ts, D_FF)],
        out_shape=[jax.ShapeDtypeStruct((s, D_FF), BF16), jax.ShapeDtypeStruct((s, D_FF), BF16)],
    )


def _silu_mul(gt, up):
    return (gt * jax.nn.sigmoid(gt)) * up


def _ffn_down_ple(gt, up, h1, w_down, p, target, g_ple, w_pg, w_pp):
    s = h1.shape[0]
    ts = min(TOKEN_TILE, s)
    blk = D_MODEL // N_DEV

    def body(gt_ref, up_ref, h1_ref, wd_ref, p_ref, t_ref, g_ref, wpg_ref, wpp_ref,
             loss_ref, dh_ref, dwpg_ref, dwpp_ref, dg_ref):
        @pl.when(pl.program_id(0) == 0)
        def _():
            loss_ref[...] = jnp.zeros_like(loss_ref)
            dwpg_ref[...] = jnp.zeros_like(dwpg_ref)
            dwpp_ref[...] = jnp.zeros_like(dwpp_ref)
            dg_ref[...] = jnp.zeros_like(dg_ref)

        h2v = h1_ref[...]
        for c in range(D_FF // FF_CHUNK):
            cols = slice(c * FF_CHUNK, (c + 1) * FF_CHUNK)
            act = _silu_mul(gt_ref[:, cols].astype(F32), up_ref[:, cols].astype(F32)).astype(BF16)
            h2v = _nn(act, wd_ref[cols, :]) + h2v
        r = _rms(h2v)
        hn = ((h2v * r) * g_ref[...]).astype(BF16)
        gate = jax.nn.sigmoid(_nn(hn, wpg_ref[...]))
        pb = p_ref[...].astype(BF16)
        pp = _nn(pb, jnp.concatenate([wpp_ref[j] for j in range(N_DEV)], axis=1))
        diff = (h2v + gate * pp) - t_ref[...]
        loss_ref[...] += jnp.sum(jnp.sum(diff * diff, axis=0, keepdims=True), axis=1, keepdims=True) * (0.5 / D_MODEL)
        dy = diff * (1.0 / D_MODEL)
        d_pp = (dy * gate).astype(BF16)
        d_pre = ((dy * pp) * (gate * (1.0 - gate))).astype(BF16)
        d_x, d_g = _rms_bwd(_nt(d_pre, wpg_ref[...]), h2v, r, g_ref[...])
        dg_ref[...] += d_g
        dh_ref[...] = dy + d_x
        d_wpp = _tn(pb, d_pp)
        for j in range(N_DEV):
            dwpp_ref[j] += d_wpp[:, j * blk:(j + 1) * blk]
        dwpg_ref[...] += _tn(hn, d_pre)

    return _call(
        body,
        (gt, up, h1, w_down, p, target, g_ple, w_pg, w_pp),
        name="ffn_down_ple",
        grid=(s // ts,),
        in_specs=[
            _rows(ts, D_FF),
            _rows(ts, D_FF),
            _rows(ts, D_MODEL),
            _resident((D_FF, D_MODEL)),
            _rows(ts, PLE_DIM),
            _rows(ts, D_MODEL),
            _resident((1, D_MODEL)),
            _resident((D_MODEL, D_MODEL)),
            _resident((N_DEV, PLE_DIM, blk)),
        ],
        out_specs=[
            _acc((1, SMALL_LANES)),
            _rows(ts, D_MODEL),
            _acc((D_MODEL, D_MODEL)),
            _acc((N_DEV, PLE_DIM, blk)),
            _acc((1, D_MODEL)),
        ],
        out_shape=[
            jax.ShapeDtypeStruct((1, SMALL_LANES), F32),
            jax.ShapeDtypeStruct((s, D_MODEL), F32),
            jax.ShapeDtypeStruct((D_MODEL, D_MODEL), F32),
            jax.ShapeDtypeStruct((N_DEV, PLE_DIM, blk), F32),
            jax.ShapeDtypeStruct((1, D_MODEL), F32),
        ],
    )


def _ffn_bwd_act(dh2, h1, gt, up, g_ffn, wg_t, wu_t, w_down):
    s = h1.shape[0]
    ts = min(FFN_BWD_TILE, s)

    def body(dh_ref, h1_ref, gt_ref, up_ref, g_ref, wg_ref, wu_ref, wd_ref,
             dgt_ref, dup_ref, dh1_ref, dh1b_ref, dg_ref, dwd_ref, act_ref):
        @pl.when(pl.program_id(0) == 0)
        def _():
            dg_ref[...] = jnp.zeros_like(dg_ref)
            dwd_ref[...] = jnp.zeros_like(dwd_ref)

        dhb = dh_ref[...].astype(BF16)
        d_hn = jnp.zeros((ts, D_MODEL), F32)
        for c in range(D_FF // FF_CHUNK):
            cols = slice(c * FF_CHUNK, (c + 1) * FF_CHUNK)
            d_act = _nt(dhb, wd_ref[cols, :])
            gtv = gt_ref[:, cols].astype(F32)
            upv = up_ref[:, cols].astype(F32)
            sg = jax.nn.sigmoid(gtv)
            silu = gtv * sg
            act_ref[:, cols] = (silu * upv).astype(BF16)
            d_up = (d_act * silu).astype(BF16)
            d_gt = ((d_act * upv) * (sg * (1.0 + gtv * (1.0 - sg)))).astype(BF16)
            dup_ref[:, cols] = d_up
            dgt_ref[:, cols] = d_gt
            d_hn = (_nn(d_gt, wg_ref[cols, :]) + _nn(d_up, wu_ref[cols, :])) + d_hn
        dwd_ref[...] += _tn(act_ref[...], dhb)
        h1v = h1_ref[...]
        d_x, d_g = _rms_bwd(d_hn, h1v, _rms(h1v), g_ref[...])
        dg_ref[...] += d_g
        dh1 = dh_ref[...] + d_x
        dh1_ref[...] = dh1
        dh1b_ref[...] = dh1.astype(BF16)

    return _call(
        body,
        (dh2, h1, gt, up, g_ffn, wg_t, wu_t, w_down),
        name="ffn_bwd_act",
        grid=(s // ts,),
        in_specs=[
            _rows(ts, D_MODEL),
            _rows(ts, D_MODEL),
            _rows(ts, D_FF),
            _rows(ts, D_FF),
            _resident((1, D_MODEL)),
            _resident((D_FF, D_MODEL)),
            _resident((D_FF, D_MODEL)),
            _resident((D_FF, D_MODEL)),
        ],
        out_specs=[
            _rows(ts, D_FF), _rows(ts, D_FF),
            _rows(ts, D_MODEL), _rows(ts, D_MODEL), _acc((1, D_MODEL)), _acc((D_FF, D_MODEL)),
        ],
        out_shape=[
            jax.ShapeDtypeStruct((s, D_FF), BF16),
            jax.ShapeDtypeStruct((s, D_FF), BF16),
            jax.ShapeDtypeStruct((s, D_MODEL), F32),
            jax.ShapeDtypeStruct((s, D_MODEL), BF16),
            jax.ShapeDtypeStruct((1, D_MODEL), F32),
            jax.ShapeDtypeStruct((D_FF, D_MODEL), F32),
        ],
        scratch_shapes=[pltpu.VMEM((ts, D_FF), BF16)],
    )


def _ffn_bwd_w(dgt, dup, hn2):
    s = hn2.shape[0]
    slab = pl.BlockSpec((s, FFN_W_SLAB), lambda i: (0, i))

    def body(dgt_ref, dup_ref, hn_ref, dwg_ref, dwu_ref):
        hn = hn_ref[...]
        dwg_ref[...] = _tn(dgt_ref[...], hn)
        dwu_ref[...] = _tn(dup_ref[...], hn)

    return _call(
        body,
        (dgt, dup, hn2),
        name="ffn_bwd_w",
        grid=(D_FF // FFN_W_SLAB,),
        in_specs=[slab, slab, _resident((s, D_MODEL))],
        out_specs=[_rows(FFN_W_SLAB, D_MODEL)] * 2,
        out_shape=[jax.ShapeDtypeStruct((D_FF, D_MODEL), F32)] * 2,
    )


def _mix_bwd(dh1b, u, a, m, w_out, w_pool, pool_scale):
    s = u.shape[0]
    ts = min(TOKEN_TILE, s)
    nt = s // ts
    halo_after = lambda i: (jnp.minimum((i + 1) * (ts // POOL_HALO), s // POOL_HALO - 1), 0)
    n_groups = len(POOL_SIZES)

    def body(dh_ref, dhn_ref, u_ref, uh_ref, a_ref, m_ref, wo_ref, wp_ref, sc_ref,
             da_ref, du_ref, dwp_ref, dsc_ref, dwo_ref):
        i = pl.program_id(0)

        @pl.when(i == 0)
        def _():
            dwp_ref[...] = jnp.zeros_like(dwp_ref)
            dsc_ref[...] = jnp.zeros_like(dsc_ref)
            dwo_ref[...] = jnp.zeros_like(dwo_ref)

        dh = dh_ref[...]
        dwo_ref[:ATTN_WIDTH, :] += _tn(a_ref[...], dh)
        dwo_ref[ATTN_WIDTH:, :] += _tn(m_ref[...], dh)
        da_ref[...] = _nt(dh, wo_ref[:ATTN_WIDTH, :])
        dh_next = jnp.where(i < nt - 1, dhn_ref[...], jnp.zeros_like(dhn_ref))
        dm_ext = _nt(jnp.concatenate([dh, dh_next], axis=0), wo_ref[ATTN_WIDTH:, :])
        pooled = _pooled(u_ref[...], uh_ref[...], i, ts)
        t_ext = i * ts + lax.broadcasted_iota(jnp.int32, (ts + POOL_HALO, 1), 0)
        for g, w in enumerate(POOL_SIZES):
            lanes = slice(g * POOL_GROUP, (g + 1) * POOL_GROUP)
            wp = wp_ref[g].astype(BF16)
            pg = pooled[g].astype(BF16)
            dm_g = dm_ext[:, lanes]
            dsc_ref[:, lanes] += jnp.sum(dm_g[:ts, :] * _nn(pg, wp), axis=0, keepdims=True)
            dy = (dm_g * sc_ref[:, lanes]).astype(BF16)
            dwp_ref[g] += _tn(pg, dy[:ts, :])
            d_pool = _nt(dy, wp)
            acc = d_pool / jnp.minimum(t_ext + 1, w).astype(F32)
            shift = 1
            while shift < w:
                acc = acc + pltpu.roll(acc, ts + POOL_HALO - shift, 0)
                shift *= 2
            du_ref[:, lanes] = (acc[:ts, :] - d_pool[:ts, :]).astype(BF16)

    return _call(
        body,
        (dh1b, dh1b, u, u, a, m, w_out, w_pool, pool_scale),
        name="mix_bwd",
        grid=(nt,),
        in_specs=[
            _rows(ts, D_MODEL),
            pl.BlockSpec((POOL_HALO, D_MODEL), halo_after),
            _rows(ts, POOL_WIDTH),
            pl.BlockSpec((POOL_HALO, POOL_WIDTH), _halo_before(ts)),
            _rows(ts, ATTN_WIDTH),
            _rows(ts, POOL_WIDTH),
            _resident((D_MODEL, D_MODEL)),
            _resident((n_groups, POOL_GROUP, POOL_GROUP)),
            _resident((1, POOL_WIDTH)),
        ],
        out_specs=[
            _rows(ts, ATTN_WIDTH),
            _rows(ts, POOL_WIDTH),
            _acc((n_groups, POOL_GROUP, POOL_GROUP)),
            _acc((1, POOL_WIDTH)),
            _acc((D_MODEL, D_MODEL)),
        ],
        out_shape=[
            jax.ShapeDtypeStruct((s, ATTN_WIDTH), F32),
            jax.ShapeDtypeStruct((s, POOL_WIDTH), BF16),
            jax.ShapeDtypeStruct((n_groups, POOL_GROUP, POOL_GROUP), F32),
            jax.ShapeDtypeStruct((1, POOL_WIDTH), F32),
            jax.ShapeDtypeStruct((D_MODEL, D_MODEL), F32),
        ],
    )


def _attn_bwd(qn, kn, v, a, da, tab, sinks):
    s = qn.shape[0]
    qb = ATTN_STEP_BLOCKS
    rows = qb * BLOCK
    n_groups = s // rows
    group, prev = _attn_specs(n_groups)
    done = lambda n: (jnp.maximum(n - 1, 0), 0)

    def body(sink_ref, q_ref, kc_ref, kp_ref, vc_ref, vp_ref, o_ref, do_ref, tab_ref,
             dq_ref, dk_ref, dv_ref, dl_ref, ds_ref, k_carry, v_carry, sink_acc):
        n = pl.program_id(0)

        @pl.when(n == 0)
        def _():
            dl_ref[...] = jnp.zeros_like(dl_ref)
            k_carry[...] = jnp.zeros_like(k_carry)
            v_carry[...] = jnp.zeros_like(v_carry)
            sink_acc[...] = jnp.zeros_like(sink_acc)

        @pl.when(n < n_groups)
        def _():
            first = n == 0
            lo_mask = _lane_lo((BLOCK, BLOCK))
            chains = [(b, half) for b in range(qb) for half in range(2)]
            tile = lambda ref, b, p: ref[b * BLOCK:(b + 1) * BLOCK, p * BLOCK:(p + 1) * BLOCK]
            keys = [_band(kp_ref, kc_ref, b) for b in range(qb)]
            vals = [_band(vp_ref, vc_ref, b) for b in range(qb)]
            q_st = [_stack_heads([tile(q_ref, b, p) for p in range(4)], lo_mask) for b in range(qb)]
            do_st = [_stack_heads([tile(do_ref, b, p) for p in range(4)], lo_mask) for b in range(qb)]
            o_st = [_stack_heads([tile(o_ref, b, p).astype(F32) for p in range(4)], lo_mask) for b in range(qb)]
            sink_col = [_sink_column(sink_ref, heads) for heads in (HEADS_A, HEADS_B)]
            scores = {(b, h): _band_scores(q_st[b][h], keys[b][h], tab_ref[h], first if b == 0 else None)
                      for b, h in chains}
            dob = {(b, h): do_st[b][h].astype(BF16) for b, h in chains}
            d_probs = {(b, h): _nt(dob[b, h], vals[b][h]) for b, h in chains}
            delta = {(b, h): jnp.sum(do_st[b][h] * o_st[b][h], axis=-1, keepdims=True) for b, h in chains}
            soft = {(b, h): _softmax_with_sink(scores[b, h], sink_col[h]) for b, h in chains}
            dl = {(b, h): soft[b, h][0] * (d_probs[b, h] - delta[b, h]) for b, h in chains}
            for b, h in chains:
                dl_ref[h] += dl[b, h]
                sink_acc[h] += soft[b, h][1] * delta[b, h]
            dsb = {(b, h): (dl[b, h] * (HEAD_DIM ** -0.5)).astype(BF16) for b, h in chains}
            dq_st = {(b, h): _nn(dsb[b, h], keys[b][h]) for b, h in chains}
            dk_parts = {(b, h): _tn(dsb[b, h], q_st[b][h]) for b, h in chains}
            dv_parts = {(b, h): _tn(soft[b, h][0].astype(BF16), dob[b, h]) for b, h in chains}
            for b in range(qb):
                dq = _unstack_heads(dq_st[b, 0], dq_st[b, 1], lo_mask)
                for p in range(4):
                    dq_ref[b * BLOCK:(b + 1) * BLOCK, p * BLOCK:(p + 1) * BLOCK] = dq[p]
            dks = [dk_parts[b, 0] + pltpu.roll(dk_parts[b, 1], HEAD_DIM, 1) for b in range(qb)]
            dvs = [dv_parts[b, 0] + pltpu.roll(dv_parts[b, 1], HEAD_DIM, 1) for b in range(qb)]
            last = slice((qb - 1) * BLOCK, qb * BLOCK)
            for parts, out_ref, carry in ((dks, dk_ref, k_carry), (dvs, dv_ref, v_carry)):
                out_ref[...] = carry[...]
                out_ref[last, :] += parts[0][:BLOCK, :]
                for b in range(qb):
                    own = parts[b][BLOCK:, :]
                    carry[b * BLOCK:(b + 1) * BLOCK, :] = own + parts[b + 1][:BLOCK, :] if b + 1 < qb else own

        @pl.when(n == n_groups)
        def _():
            dk_ref[...] = k_carry[...]
            dv_ref[...] = v_carry[...]
            for half, heads in enumerate((HEADS_A, HEADS_B)):
                for slot, h in enumerate(heads):
                    tot = jnp.sum(sink_acc[half, slot * BLOCK:(slot + 1) * BLOCK, :], axis=0, keepdims=True)
                    ds_ref[h:h + 1, :] = jnp.broadcast_to(-tot, (1, SMALL_LANES))

    return _call(
        body,
        (sinks, qn, kn, kn, v, v, a, da, tab),
        name="attn_bwd",
        grid=(n_groups + 1,),
        in_specs=[
            pl.BlockSpec(memory_space=pltpu.SMEM),
            pl.BlockSpec((rows, ATTN_WIDTH), group),
            pl.BlockSpec((rows, KV_WIDTH), group),
            pl.BlockSpec((BLOCK, KV_WIDTH), prev),
            pl.BlockSpec((rows, KV_WIDTH), group),
            pl.BlockSpec((BLOCK, KV_WIDTH), prev),
            pl.BlockSpec((rows, ATTN_WIDTH), group),
            pl.BlockSpec((rows, ATTN_WIDTH), group),
            _resident((2, 4 * BLOCK, 2 * BLOCK)),
        ],
        out_specs=[
            pl.BlockSpec((rows, ATTN_WIDTH), group),
            pl.BlockSpec((rows, KV_WIDTH), done),
            pl.BlockSpec((rows, KV_WIDTH), done),
            _acc((2, 4 * BLOCK, 2 * BLOCK)),
            _acc((N_DEV, SMALL_LANES)),
        ],
        out_shape=[
            jax.ShapeDtypeStruct((s, ATTN_WIDTH), F32),
            jax.ShapeDtypeStruct((s, KV_WIDTH), F32),
            jax.ShapeDtypeStruct((s, KV_WIDTH), F32),
            jax.ShapeDtypeStruct((2, 4 * BLOCK, 2 * BLOCK), F32),
            jax.ShapeDtypeStruct((N_DEV, SMALL_LANES), F32),
        ],
        scratch_shapes=[
            pltpu.VMEM((rows, KV_WIDTH), F32),
            pltpu.VMEM((rows, KV_WIDTH), F32),
            pltpu.VMEM((2, 4 * BLOCK, 1), F32),
        ],
    )


def _fold_heads(acc):
    t = acc + pltpu.roll(acc, HEAD_DIM, 1)
    out = t[:, :SMALL_LANES]
    for g in range(1, acc.shape[1] // SMALL_LANES):
        out = out + t[:, g * SMALL_LANES:(g + 1) * SMALL_LANES]
    return out


def _in_proj_bwd(dqn, dkn, dv, du, zqk, x, dh1, g_attn, gq_t, gk_t, w_in_t):
    s = x.shape[0]
    ts = min(TOKEN_TILE, s)
    nt = s // ts

    def head_norm_bwd(d_n, raw, g_t, bmat):
        r = lax.rsqrt(_seg_mean(raw * raw, bmat) + EPS)
        gy = d_n * g_t
        d_raw = r * gy - raw * (r * r * r) * _seg_mean(gy * raw, bmat)
        return d_raw, jnp.sum(d_n * (raw * r), axis=0, keepdims=True)

    def body(dqn_ref, dkn_ref, dv_ref, du_ref, zqk_ref, x_ref, dh1_ref, g_ref, gq_ref, gk_ref, w_ref, bq_ref, bk_ref,
             gx_ref, dw_ref, dg_ref, dgq_ref, dgk_ref, dz_ref, gq_acc, gk_acc):
        i = pl.program_id(0)

        @pl.when(i == 0)
        def _():
            dw_ref[...] = jnp.zeros_like(dw_ref)
            dg_ref[...] = jnp.zeros_like(dg_ref)
            gq_acc[...] = jnp.zeros_like(gq_acc)
            gk_acc[...] = jnp.zeros_like(gk_acc)

        d_q, d_gq = head_norm_bwd(dqn_ref[...], zqk_ref[:, :ATTN_WIDTH], gq_ref[...], bq_ref[...])
        d_k, d_gk = head_norm_bwd(dkn_ref[...], zqk_ref[:, ATTN_WIDTH:], gk_ref[...], bk_ref[...])
        gq_acc[...] += d_gq
        gk_acc[...] += d_gk
        dz_ref[:, :ATTN_WIDTH] = d_q.astype(BF16)
        dz_ref[:, ATTN_WIDTH:ATTN_WIDTH + KV_WIDTH] = d_k.astype(BF16)
        dz_ref[:, ATTN_WIDTH + KV_WIDTH:ATTN_WIDTH + 2 * KV_WIDTH] = dv_ref[...].astype(BF16)
        dz_ref[:, ATTN_WIDTH + 2 * KV_WIDTH:] = du_ref[...]
        dz = dz_ref[...]
        xf = x_ref[...]
        r = _rms(xf)
        hn = ((xf * r) * g_ref[...]).astype(BF16)
        d_x, d_g = _rms_bwd(_nn(dz, w_ref[...]), xf, r, g_ref[...])
        dg_ref[...] += d_g
        gx_ref[...] = dh1_ref[...] + d_x
        dw_ref[...] += _tn(dz, hn)

        @pl.when(i == nt - 1)
        def _():
            dgq_ref[...] = _fold_heads(gq_acc[...])
            dgk_ref[...] = _fold_heads(gk_acc[...])

    return _call(
        body,
        (dqn, dkn, dv, du, zqk, x, dh1, g_attn, gq_t, gk_t, w_in_t,
      _head_mean_matrix(ATTN_WIDTH), _head_mean_matrix(KV_WIDTH)),
        name="in_proj_bwd",
        grid=(nt,),
        in_specs=[
            _rows(ts, ATTN_WIDTH),
            _rows(ts, KV_WIDTH),
            _rows(ts, KV_WIDTH),
            _rows(ts, POOL_WIDTH),
            _rows(ts, ATTN_WIDTH + KV_WIDTH),
            _rows(ts, D_MODEL),
            _rows(ts, D_MODEL),
            _resident((1, D_MODEL)),
            _resident((1, ATTN_WIDTH)),
            _resident((1, KV_WIDTH)),
            _resident((IN_WIDTH, D_MODEL)),
            _resident((ATTN_WIDTH, ATTN_WIDTH)),
            _resident((KV_WIDTH, KV_WIDTH)),
        ],
        out_specs=[
            _rows(ts, D_MODEL),
            _acc((IN_WIDTH, D_MODEL)),
            _acc((1, D_MODEL)),
            _acc((1, SMALL_LANES)),
            _acc((1, SMALL_LANES)),
        ],
        out_shape=[
            jax.ShapeDtypeStruct((s, D_MODEL), F32),
            jax.ShapeDtypeStruct((IN_WIDTH, D_MODEL), F32),
            jax.ShapeDtypeStruct((1, D_MODEL), F32),
            jax.ShapeDtypeStruct((1, SMALL_LANES), F32),
            jax.ShapeDtypeStruct((1, SMALL_LANES), F32),
        ],
        scratch_shapes=[
            pltpu.VMEM((ts, IN_WIDTH), BF16),
            pltpu.VMEM((1, ATTN_WIDTH), F32),
            pltpu.VMEM((1, KV_WIDTH), F32),
        ],
    )


BIG_WEIGHTS = (
    ("w_in", True, IN_WIDTH // N_DEV, D_MODEL),
    ("w_out", False, D_MODEL // N_DEV, D_MODEL),
    ("w_gate", True, D_FF // N_DEV, D_MODEL),
    ("w_up", True, D_FF // N_DEV, D_MODEL),
    ("w_down", False, D_FF // N_DEV, D_MODEL),
    ("w_ple_gate", False, D_MODEL // N_DEV, D_MODEL),
    ("w_ple_proj", False, PLE_DIM, D_MODEL // N_DEV),
)
N_BIG = len(BIG_WEIGHTS)


def _place():
    x, y, c = lax.axis_index("x"), lax.axis_index("y"), lax.axis_index("c")
    chips = [(1 - x, y), (x, 1 - y), (1 - x, 1 - y)]
    return x, y, c, chips


class _Gather:
    def __init__(self, n):
        self.n = n
        self.sems = [pltpu.SemaphoreType.DMA((n, 7)), pltpu.SemaphoreType.DMA((n, 7)), pltpu.SemaphoreType.DMA((n,))]

    def _ctx(self, srcs, outs, sems):
        send_sems, recv_sems, local_sems = sems
        x, y, c, chips = _place()
        me, sibling = (x, y, c), (x, y, 1 - c)

        def block(k, owner):
            px, py, pc = owner
            return outs[k].at[4 * px + 2 * py + pc]

        def copy(k, idx, owner, to, mine=False):
            return pltpu.make_async_remote_copy(
                src_ref=srcs[k] if mine else block(k, owner), dst_ref=block(k, owner),
                send_sem=send_sems.at[k, idx], recv_sem=recv_sems.at[k, idx], device_id=to, device_id_type=MESH)

        def local(k):
            return pltpu.make_async_copy(srcs[k], block(k, me), local_sems.at[k])

        return c, chips, me, sibling, copy, local

    def begin(self, srcs, outs, sems):
        c, chips, me, sibling, copy, local = self._ctx(srcs, outs, sems)
        for k in range(self.n):
            local(k).start()
            copy(k, 0, me, sibling, mine=True).start()
            for j, chip in enumerate(chips):
                copy(k, 1 + j, me, (*chip, c), mine=True).start()

    def middle(self, srcs, outs, sems):
        c, chips, me, sibling, copy, local = self._ctx(srcs, outs, sems)
        for j, chip in enumerate(chips):
            for k in range(self.n):
                copy(k, 1 + j, (*chip, c), me).wait_recv()
                copy(k, 4 + j, (*chip, c), sibling).start()

    def end(self, srcs, outs, sems):
        c, chips, me, sibling, copy, local = self._ctx(srcs, outs, sems)
        for k in range(self.n):
            copy(k, 0, sibling, me).wait_recv()
            for j, chip in enumerate(chips):
                copy(k, 4 + j, (*chip, 1 - c), me).wait_recv()
        for k in range(self.n):
            copy(k, 0, me, sibling, mine=True).wait_send()
            for j, chip in enumerate(chips):
                copy(k, 1 + j, me, (*chip, c), mine=True).wait_send()
                copy(k, 4 + j, (*chip, c), sibling).wait_send()
            local(k).wait()


def _gather_rider(shards):
    g = _Gather(len(shards))
    shapes = [jax.ShapeDtypeStruct((N_DEV, *sh.shape), sh.dtype) for sh in shards]
    return _Rider(shards, shapes, g.sems, g.begin, g.end, g.middle)


def _cast_and_gather_first(shards, rel_bias_t):
    g = _Gather(1)
    any_spec = pl.BlockSpec(memory_space=pl.ANY)
    vmem = pl.BlockSpec(memory_space=pltpu.VMEM)

    def body(*refs):
        ins, rb_ref, outs = refs[:N_BIG], refs[N_BIG], refs[N_BIG + 1:2 * N_BIG + 1]
        gathered, tab_ref, sems = refs[2 * N_BIG + 1], refs[2 * N_BIG + 2], refs[2 * N_BIG + 3:]
        outs[0][...] = ins[0][...].astype(BF16)
        g.begin(outs[:1], [gathered], sems)
        for k in range(1, N_BIG):
            outs[k][...] = ins[k][...].astype(BF16)
        _write_bias_table(rb_ref, tab_ref)
        g.middle(outs[:1], [gathered], sems)
        g.end(outs[:1], [gathered], sems)

    res = pl.pallas_call(
        body,
        name="cast_and_gather_first",
        in_specs=[vmem] * N_BIG + [pl.BlockSpec(memory_space=pltpu.SMEM)],
        out_specs=[vmem] * N_BIG + [any_spec, vmem],
        out_shape=[jax.ShapeDtypeStruct((r, c), BF16) for _, _, r, c in BIG_WEIGHTS]
        + [jax.ShapeDtypeStruct((N_DEV, *BIG_WEIGHTS[0][2:]), BF16), jax.ShapeDtypeStruct(BIAS_TABLE_SHAPE, F32)],
        scratch_shapes=g.sems,
    )(*shards, rel_bias_t)
    return list(res[:N_BIG]), res[N_BIG], res[N_BIG + 1]


def _sibling_rider(grads):
    n = len(grads)

    def copies(gs, lands, sems):
        send_sems, recv_sems = sems
        x, y, c, _ = _place()
        return [
            pltpu.make_async_remote_copy(
                src_ref=gs[k].at[:, 1 - c], dst_ref=lands[k], send_sem=send_sems.at[k], recv_sem=recv_sems.at[k],
                device_id=(x, y, 1 - c), device_id_type=MESH)
            for k in range(n)
        ]

    def begin(gs, lands, sems):
        for cp in copies(gs, lands, sems):
            cp.start()

    def end(gs, lands, sems):
        for cp in copies(gs, lands, sems):
            cp.wait()

    shapes = [jax.ShapeDtypeStruct((N_CHIPS, *g.shape[2:]), F32) for g in grads]
    return _Rider(grads, shapes, [pltpu.SemaphoreType.DMA((n,)), pltpu.SemaphoreType.DMA((n,))], begin, end)


def _chip_of_relation(j, place):
    x, y = place[0], place[1]
    return jnp.where(j == 0, 2 * (1 - x) + y, jnp.where(j == 1, 2 * x + 1 - y, 2 * (1 - x) + 1 - y))


def _chip_sum(ks, place, grads, from_sibling):
    shapes = [BIG_WEIGHTS[k][2:] for k in ks]
    n_rel = N_CHIPS - 1
    per_step = n_rel if 2 * n_rel * sum(4 * r * c for r, c in shapes) <= CHIP_SUM_ONE_STEP_BYTES else 1
    operands, specs = [], []
    for (r, c), g, l in zip(shapes, grads, from_sibling):
        for q in range(per_step):
            chip = lambda j, place, q=q: _chip_of_relation(j * per_step + q, place)
            operands += [g, l]
            specs += [pl.BlockSpec((1, 1, r, c), lambda j, place, chip=chip: (chip(j, place), place[2], 0, 0)),
                      pl.BlockSpec((1, r, c), lambda j, place, chip=chip: (chip(j, place), 0, 0))]
    args, in_specs, _ = _after_last(operands, specs)

    def body(place_ref, *refs):
        ins, outs = refs[:len(operands)], refs[len(args):]
        for i in range(len(ks)):
            for q in range(per_step):
                mine_ref, sib_ref = ins[2 * (i * per_step + q):2 * (i * per_step + q) + 2]
                outs[i][q] = (mine_ref[0, 0] + sib_ref[0]).astype(BF16)

    outs = pl.pallas_call(
        body,
        name="chip_sum_" + "_".join(BIG_WEIGHTS[k][0] for k in ks),
        grid_spec=pltpu.PrefetchScalarGridSpec(
            num_scalar_prefetch=1,
            grid=(n_rel // per_step,),
            in_specs=in_specs,
            out_specs=[pl.BlockSpec((per_step, r, c), lambda j, place: (j, 0, 0)) for r, c in shapes],
        ),
        out_shape=[jax.ShapeDtypeStruct((n_rel, r, c), BF16) for r, c in shapes],
    )(place, *args)
    _mark_issued(outs[0])
    return list(outs)


def _chips_rider(to_send, small=None):
    n = len(to_send)
    inputs = list(to_send) + ([] if small is None else [small])
    shapes = [jax.ShapeDtypeStruct((3, *t.shape[1:]), BF16) for t in to_send]
    sems = [pltpu.SemaphoreType.DMA((max(n, 1), 3)), pltpu.SemaphoreType.DMA((max(n, 1), 3))]
    if small is not None:
        shapes.append(jax.ShapeDtypeStruct((N_DEV, *small.shape), F32))
        sems += [pltpu.SemaphoreType.DMA((7,)), pltpu.SemaphoreType.DMA((7,)), pltpu.SemaphoreType.DMA]

    def copies(ins, outs, sem_refs):
        x, y, c, chips = _place()
        out = []
        for k in range(n):
            for j, (px, py) in enumerate(chips):
                out.append(pltpu.make_async_remote_copy(
                    src_ref=ins[k].at[j], dst_ref=outs[k].at[j],
                    send_sem=sem_refs[0].at[k, j], recv_sem=sem_refs[1].at[k, j],
                    device_id=(px, py, c), device_id_type=MESH))
        local = None
        if small is not None:
            me = 4 * x + 2 * y + c
            local = pltpu.make_async_copy(ins[n], outs[n].at[me], sem_refs[4])
            rel = 0
            for fx in (0, 1):
                for fy in (0, 1):
                    for fc in (0, 1):
                        if (fx, fy, fc) != (0, 0, 0):
                            out.append(pltpu.make_async_remote_copy(
                                src_ref=ins[n], dst_ref=outs[n].at[me],
                                send_sem=sem_refs[2].at[rel], recv_sem=sem_refs[3].at[rel],
                                device_id=(x ^ fx, y ^ fy, c ^ fc), device_id_type=MESH))
                            rel += 1
        return out, local

    def begin(ins, outs, sem_refs):
        remote, local = copies(ins, outs, sem_refs)
        if local is not None:
            local.start()
        for cp in remote:
            cp.start()

    def end(ins, outs, sem_refs):
        remote, local = copies(ins, outs, sem_refs)
        for cp in remote:
            cp.wait()
        if local is not None:
            local.wait()

    return _Rider(inputs, shapes, sems, begin, end)


def _together(first, second):
    cut = lambda refs, a, b: (refs[:len(a)], refs[len(a):len(a) + len(b)])

    def run(which):
        def fn(ins, outs, sems):
            parts = zip((first, second), cut(ins, first.inputs, second.inputs),
                        cut(outs, first.out_shapes, second.out_shapes), cut(sems, first.sems, second.sems))
            for rider, i, o, s in parts:
                hook = getattr(rider, which)
                if hook is not None:
                    hook(i, o, s)
        return fn

    return _Rider(first.inputs + second.inputs, first.out_shapes + second.out_shapes, first.sems + second.sems,
                  run("begin"), run("end"), run("middle"))


PEER_SETS = {"sibling": 1, "chips": 2, "sibling+chips": 3, "all": 4}


def _peers(pattern):
    x, y, c, chips = _place()
    sibling, others = [(x, y, 1 - c)], [(*chip, c) for chip in chips]
    if pattern == "all":
        return sibling + others + [(*chip, 1 - c) for chip in chips]
    return {"sibling": sibling, "chips": others, "sibling+chips": sibling + others}[pattern]


def _on_sequencer(name, pattern, rider):
    n_in, n_out = len(rider.inputs), len(rider.out_shapes)

    def body(*refs):
        ins, outs, sems = refs[:n_in], refs[n_in:n_in + n_out], refs[n_in + n_out:]
        peers = _peers(pattern)
        barrier = pltpu.get_barrier_semaphore()
        for peer in peers:
            pl.semaphore_signal(barrier, inc=1, device_id=peer, device_id_type=MESH)
        pl.semaphore_wait(barrier, len(peers))
        rider.begin(ins, outs, sems)
        if rider.middle is not None:
            rider.middle(ins, outs, sems)
        rider.end(ins, outs, sems)

    outs = pl.kernel(
        body,
        name=name,
        out_type=tuple(rider.out_shapes),
        mesh=plsc.ScalarSubcoreMesh(axis_name="sequencer", num_cores=1),
        scratch_types=tuple(rider.sems),
        compiler_params=pltpu.CompilerParams(collective_id=PEER_SETS[pattern]),
    )(*rider.inputs)
    return list(outs)


def _adamw(w, g, m, v):
    m = ADAM_B1 * m + (1.0 - ADAM_B1) * g
    v = ADAM_B2 * v + (1.0 - ADAM_B2) * jnp.square(g)
    m_hat = m / (1.0 - ADAM_B1 ** ADAM_STEP)
    v_hat = v / (1.0 - ADAM_B2 ** ADAM_STEP)
    delta = -ADAM_LR * (m_hat / (jnp.sqrt(v_hat) + ADAM_EPS) + ADAM_WD * w)
    return delta, m, v


def _adamw_big(ks, place, operands):
    n = len(ks)
    tiles = lambda i, place: (i, 0)
    in_specs, out_specs, out_shape = [], [], []
    for k in ks:
        _, _, r, c = BIG_WEIGHTS[k]
        tile = r // 2
        in_specs += [
            pl.BlockSpec((1, 1, tile, c), lambda i, place: (2 * place[0] + place[1], place[2], i, 0)),
            pl.BlockSpec((1, tile, c), lambda i, place: (2 * place[0] + place[1], i, 0)),
            pl.BlockSpec((3, tile, c), lambda i, place: (0, i, 0)),
        ] + [pl.BlockSpec((tile, c), tiles)] * 3
        out_specs += [pl.BlockSpec((tile, c), tiles)] * 4
        out_shape += [jax.ShapeDtypeStruct((r, c), F32)] * 4
    args, in_specs, _ = _after_last(sum((list(ops) for ops in operands), []), in_specs)

    def body(place_ref, *refs):
        ins, outs = refs[:6 * n], refs[len(args):]
        for i in range(n):
            mine_ref, sib_ref, land_ref, w_ref, m_ref, v_ref = ins[6 * i:6 * i + 6]
            g_ref, d_ref, nm_ref, nv_ref = outs[4 * i:4 * i + 4]
            g = mine_ref[0, 0] + sib_ref[0]
            g = ((g + land_ref[0].astype(F32)) + land_ref[1].astype(F32)) + land_ref[2].astype(F32)
            g_ref[...] = g
            d_ref[...], nm_ref[...], nv_ref[...] = _adamw(w_ref[...], g, m_ref[...], v_ref[...])

    outs = pl.pallas_call(
        body,
        name="adamw_" + "_".join(BIG_WEIGHTS[k][0] for k in ks),
        grid_spec=pltpu.PrefetchScalarGridSpec(
            num_scalar_prefetch=1, grid=(2,), in_specs=in_specs, out_specs=out_specs),
        out_shape=out_shape,
    )(place, *args)
    _mark_issued(outs[0])
    return [outs[4 * i:4 * i + 4] for i in range(n)]


def _pack_small(arrays):
    rows, offsets = [], []
    at = 0
    for a in arrays:
        if a.ndim != 2 or a.shape[1] != SMALL_LANES or a.shape[0] % 8:
            flat = a.reshape(-1)
            n_rows = -(-flat.shape[0] // (8 * SMALL_LANES)) * 8
            a = jnp.pad(flat, (0, n_rows * SMALL_LANES - flat.shape[0])).reshape(n_rows, SMALL_LANES)
        rows.append(a)
        offsets.append(at)
        at += a.shape[0]
    return jnp.concatenate(rows, axis=0), offsets


def _unpack_small(tot, at, shape):
    r, c = shape
    if r % 8 == 0:
        return tot[at:at + r, :c]
    assert r == 1
    if c <= SMALL_LANES:
        return tot[at:at + 1, :c]
    return jnp.concatenate([tot[at + j:at + j + 1, :] for j in range(c // SMALL_LANES)], axis=1)


def _small_update(packs, loss_at, grads_at, ws, ms, vs):
    n, n_packs = len(ws), len(packs)

    def body(*refs):
        pack_refs, refs = refs[:n_packs], refs[n_packs:]
        w_refs, m_refs, v_refs, loss_ref, outs = refs[:n], refs[n:2 * n], refs[2 * n:3 * n], refs[3 * n], refs[3 * n + 1:]
        tots = []
        for p_ref in pack_refs:
            tot = p_ref[0]
            for j in range(1, N_DEV):
                tot = tot + p_ref[j]
            tots.append(tot)
        loss_ref[...] = _unpack_small(tots[loss_at[0]], loss_at[1], (1, 1))
        for i, (pack, at) in enumerate(grads_at):
            g = _unpack_small(tots[pack], at, w_refs[i].shape)
            outs[i][...] = g
            outs[n + i][...], outs[2 * n + i][...], outs[3 * n + i][...] = _adamw(
                w_refs[i][...], g, m_refs[i][...], v_refs[i][...])

    shapes = [jax.ShapeDtypeStruct(w.shape, F32) for w in ws]
    outs = pl.pallas_call(body, name="small_update", out_shape=[jax.ShapeDtypeStruct((1, 1), F32)] + shapes * 4)(
        *packs, *ws, *ms, *vs)
    return outs[0], outs[1:]


SMALL_NAMES = ("g_attn_norm", "g_q", "g_k", "attn_sinks", "rel_bias", "w_pool", "pool_scale", "g_ffn_norm", "g_ple_norm")


def kernel(x, p, w_in, w_out, g_attn_norm, g_q, g_k, attn_sinks, rel_bias, w_pool, pool_scale, g_ffn_norm, w_gate, w_up, w_down, g_ple_norm, w_ple_gate, w_ple_proj, loss_target, m_w_in, m_w_out, m_g_attn_norm, m_g_q, m_g_k, m_attn_sinks, m_rel_bias, m_w_pool, m_pool_scale, m_g_ffn_norm, m_w_gate, m_w_up, m_w_down, m_g_ple_norm, m_w_ple_gate, m_w_ple_proj, v_w_in, v_w_out, v_g_attn_norm, v_g_q, v_g_k, v_attn_sinks, v_rel_bias, v_w_pool, v_pool_scale, v_g_ffn_norm, v_w_gate, v_w_up, v_w_down, v_g_ple_norm, v_w_ple_gate, v_w_ple_proj):
    weights = dict(w_in=w_in, w_out=w_out, g_attn_norm=g_attn_norm, g_q=g_q, g_k=g_k, attn_sinks=attn_sinks,
                   rel_bias=rel_bias, w_pool=w_pool, pool_scale=pool_scale, g_ffn_norm=g_ffn_norm, w_gate=w_gate,
                   w_up=w_up, w_down=w_down, g_ple_norm=g_ple_norm, w_ple_gate=w_ple_gate, w_ple_proj=w_ple_proj)
    m_in = dict(w_in=m_w_in, w_out=m_w_out, g_attn_norm=m_g_attn_norm, g_q=m_g_q, g_k=m_g_k, attn_sinks=m_attn_sinks,
                rel_bias=m_rel_bias, w_pool=m_w_pool, pool_scale=m_pool_scale, g_ffn_norm=m_g_ffn_norm, w_gate=m_w_gate,
                w_up=m_w_up, w_down=m_w_down, g_ple_norm=m_g_ple_norm, w_ple_gate=m_w_ple_gate, w_ple_proj=m_w_ple_proj)
    v_in = dict(w_in=v_w_in, w_out=v_w_out, g_attn_norm=v_g_attn_norm, g_q=v_g_q, g_k=v_g_k, attn_sinks=v_attn_sinks,
                rel_bias=v_rel_bias, w_pool=v_w_pool, pool_scale=v_pool_scale, g_ffn_norm=v_g_ffn_norm, w_gate=v_w_gate,
                w_up=v_w_up, w_down=v_w_down, g_ple_norm=v_g_ple_norm, w_ple_gate=v_w_ple_gate, w_ple_proj=v_w_ple_proj)

    _issued.clear()
    xs = x[0]
    ps = p[0, 0]
    target = loss_target[0]
    wp = w_pool[0]
    gq_t = jnp.tile(g_q, (1, ATTN_WIDTH // HEAD_DIM))
    gk_t = jnp.tile(g_k, (1, KV_WIDTH // HEAD_DIM))

    def to_blocks(k, arr):
        return jnp.swapaxes(arr[0], 0, 1) if BIG_WEIGHTS[k][1] else arr[0]

    def from_blocks(k, arr):
        return (jnp.swapaxes(arr, 0, 1) if BIG_WEIGHTS[k][1] else arr)[None]

    IN, OUT, GATE, UP, DOWN, PG, PP = range(N_BIG)
    full = lambda g: g.reshape(N_DEV * g.shape[1], g.shape[2])
    halves = lambda k, g: g.reshape(N_CHIPS, 2, *BIG_WEIGHTS[k][2:])
    place = jnp.stack([lax.axis_index("x"), lax.axis_index("y"), lax.axis_index("c")]).astype(jnp.int32)

    sh, w_in_g, tab = _cast_and_gather_first(
        [to_blocks(k, weights[name]) for k, (name, _, _, _) in enumerate(BIG_WEIGHTS)], rel_bias.T)
    w_in_t = full(w_in_g)

    (w_out_g,) = _on_sequencer("gather_out", "sibling+chips", _gather_rider([sh[OUT]]))
    wg_g, wu_g = _on_sequencer("gather_gate_up", "sibling+chips", _gather_rider([sh[GATE], sh[UP]]))
    wd_g, w_pg_g, w_pp_g = _on_sequencer("gather_down_ple", "sibling+chips", _gather_rider([sh[DOWN], sh[PG], sh[PP]]))
    (zqk, qn, kn, v, u) = _in_proj(xs, g_attn_norm, w_in_t, gq_t, gk_t)
    (a,) = _attn_fwd(qn, kn, v, tab, attn_sinks)
    w_out_f = full(w_out_g)
    (h1, hn2, m_out) = _mix_out(u, a, xs, w_out_f, wp, pool_scale, g_ffn_norm)
    wg_t, wu_t = full(wg_g), full(wu_g)
    (gt, up) = _ffn_up(hn2, wg_t, wu_t)
    w_down_f = full(wd_g)

    partial, from_sibling, sums, landed = [None] * N_BIG, [None] * N_BIG, [None] * N_BIG, [None] * N_BIG

    def to_sibling(name, ks, grads):
        for k, g in zip(ks, grads):
            partial[k] = halves(k, g)
        got = _on_sequencer(name, "sibling", _sibling_rider([partial[k] for k in ks]))
        for k, g in zip(ks, got):
            from_sibling[k] = g

    def chip_sum(*ks):
        for k, s in zip(ks, _chip_sum(ks, place, [partial[k] for k in ks], [from_sibling[k] for k in ks])):
            sums[k] = s

    def to_chips(name, ks, small=None):
        got = _on_sequencer(name, "chips" if small is None else "all", _chips_rider([sums[k] for k in ks], small))
        for k, g in zip(ks, got):
            landed[k] = g
        return got[len(ks):]

    (loss_part, dh2, d_wpg, d_wpp, d_g_ple) = _ffn_down_ple(
        gt, up, h1, w_down_f, ps, target, g_ple_norm, full(w_pg_g), w_pp_g)
    to_sibling("sibling_ple", (PG, PP), (d_wpg, d_wpp))
    (dgt, dup, dh1, dh1b, d_g_ffn, d_wd) = _ffn_bwd_act(dh2, h1, gt, up, g_ffn_norm, wg_t, wu_t, w_down_f)
    to_sibling("sibling_down", (DOWN,), (d_wd,))
    chip_sum(PG, PP)
    to_chips("chips_ple", (PG, PP))
    chip_sum(DOWN)
    to_chips("chips_down", (DOWN,))
    (d_wg_t, d_wu_t) = _ffn_bwd_w(dgt, dup, hn2)
    to_sibling("sibling_gate_up", (GATE, UP), (d_wg_t, d_wu_t))
    _complete_before_next([landed[PG], landed[PP], landed[DOWN]])
    (da, du, d_wpool, d_scale, d_wo) = _mix_bwd(dh1b, u, a, m_out, w_out_f, wp, pool_scale)
    to_sibling("sibling_out", (OUT,), (d_wo,))
    chip_sum(GATE, UP)
    to_chips("chips_gate_up", (GATE, UP))
    (dqn, dkn, dv, dl_acc, d_sinks) = _attn_bwd(qn, kn, v, a, da, tab, attn_sinks)
    chip_sum(OUT)
    early, early_at = _pack_small([d_wpool.reshape(POOL_WIDTH, POOL_GROUP), d_scale, d_g_ffn, d_g_ple, loss_part[:, :1]])
    landed[OUT], early_all = _on_sequencer(
        "chips_out", "sibling+chips", _together(_chips_rider([sums[OUT]]), _gather_rider([early])))
    (grad_x, d_win_t, d_g_attn, d_gq, d_gk) = _in_proj_bwd(dqn, dkn, dv, du, zqk, xs, dh1, g_attn_norm, gq_t, gk_t, w_in_t)
    to_sibling("sibling_in", (IN,), (d_win_t,))
    _complete_before_next([landed[OUT], landed[GATE], landed[UP], early_all])
    (d_rel_t,) = _bias_table_bwd(dl_acc)
    chip_sum(IN)
    late, late_at = _pack_small([d_g_attn, d_gq[:, :HEAD_DIM], d_gk[:, :HEAD_DIM], d_sinks[:, 0], d_rel_t])
    (late_all,) = to_chips("chips_in", (IN,), late)

    out = {"grad": {}, "delta": {}, "new_m": {}, "new_v": {}}
    for ks in ((PG, PP, DOWN), (OUT, GATE, UP), (IN,)):
        names = [BIG_WEIGHTS[k][0] for k in ks]
        results = _adamw_big(ks, place, [
            (partial[k], from_sibling[k], landed[k], to_blocks(k, weights[n]), to_blocks(k, m_in[n]),
             to_blocks(k, v_in[n])) for k, n in zip(ks, names)])
        for k, name, res in zip(ks, names, results):
            for kind, r in zip(("grad", "delta", "new_m", "new_v"), res):
                out[kind][name] = from_blocks(k, r)
    def as_rows(name, arr):
        return arr.T if name == "rel_bias" else arr.reshape(POOL_WIDTH, POOL_GROUP) if name == "w_pool" else arr

    def from_rows(name, arr):
        return arr.T if name == "rel_bias" else arr.reshape(w_pool.shape) if name == "w_pool" else arr

    grads_at = dict(w_pool=(0, early_at[0]), pool_scale=(0, early_at[1]), g_ffn_norm=(0, early_at[2]),
                    g_ple_norm=(0, early_at[3]), g_attn_norm=(1, late_at[0]), g_q=(1, late_at[1]), g_k=(1, late_at[2]),
                    attn_sinks=(1, late_at[3]), rel_bias=(1, late_at[4]))
    loss, updates = _small_update(
        [early_all, late_all], (0, early_at[4]), [grads_at[n] for n in SMALL_NAMES],
        [as_rows(n, weights[n]) for n in SMALL_NAMES], [as_rows(n, m_in[n]) for n in SMALL_NAMES],
        [as_rows(n, v_in[n]) for n in SMALL_NAMES])
    loss = loss.reshape(())
    n_small = len(SMALL_NAMES)
    for j, kind in enumerate(("grad", "delta", "new_m", "new_v")):
        for i, name in enumerate(SMALL_NAMES):
            out[kind][name] = from_rows(name, updates[j * n_small + i])

    _issued.clear()
    order = ("w_in", "w_out", "g_attn_norm", "g_q", "g_k", "attn_sinks", "rel_bias", "w_pool", "pool_scale",
             "g_ffn_norm", "w_gate", "w_up", "w_down", "g_ple_norm", "w_ple_gate", "w_ple_proj")
    return (loss, grad_x[None], *[out["grad"][n] for n in order], *[out["delta"][n] for n in order],
            *[out["new_m"][n] for n in order], *[out["new_v"][n] for n in order])
```

```python
import math

import jax
import jax.numpy as jnp
import numpy as np
from jax import lax
from jax.experimental import pallas as pl
from jax.experimental.pallas import tpu as pltpu
from jax.experimental.pallas import tpu_sc as plsc

F32 = jnp.float32
BF16 = jnp.bfloat16
MESH = pl.DeviceIdType.MESH

D_MODEL = 1024
HEAD_DIM = 64
ATTN_WIDTH = 512
KV_WIDTH = 128
POOL_WIDTH = 512
POOL_SIZES = (2, 4, 8, 16)
POOL_GROUP = 128
POOL_HALO = 16
IN_WIDTH = 1280
D_FF = 2816
PLE_DIM = 256
BLOCK = 128
N_BUCKETS = 32
MAX_DISTANCE = 128
EPS = 1e-6
N_DEV = 8
N_CHIPS = 4

ADAM_LR = 0.001
ADAM_B1 = 0.9
ADAM_B2 = 0.999
ADAM_EPS = 1e-08
ADAM_WD = 0.01
ADAM_STEP = 10

TOKEN_TILE = 512
FFN_BWD_TILE = 256
FF_CHUNK = 256
CHIP_SUM_ONE_STEP_BYTES = 9 * 2 ** 20
FFN_W_SLAB = 256
ATTN_STEP_BLOCKS = 4
HEADS_A = (0, 2, 5, 7)
HEADS_B = (1, 3, 4, 6)
SMALL_LANES = 128


def _nn(a, b):
    return jnp.dot(a, b, preferred_element_type=F32)


def _nt(a, b):
    return lax.dot_general(a, b, (((1,), (1,)), ((), ())), preferred_element_type=F32)


def _tn(a, b):
    return lax.dot_general(a, b, (((0,), (0,)), ((), ())), preferred_element_type=F32)


def _resident(shape):
    nd = len(shape)
    return pl.BlockSpec(shape, lambda i, _nd=nd: (0,) * _nd, pipeline_mode=pl.Buffered(1))


def _rows(tile, width):
    return pl.BlockSpec((tile, width), lambda i: (i, 0))


def _acc(shape):
    nd = len(shape)
    return pl.BlockSpec(shape, lambda i, _nd=nd: (0,) * _nd)


def _head_mean_matrix(width):
    idx = np.arange(width) // HEAD_DIM
    return jnp.asarray((idx[:, None] == idx[None, :]).astype(np.float32) / HEAD_DIM, dtype=BF16)


def _seg_mean(v, bmat):
    hi = v.astype(BF16)
    lo = (v - hi.astype(F32)).astype(BF16)
    return _nn(hi, bmat) + _nn(lo, bmat)


def _rms(x):
    return lax.rsqrt(jnp.mean(x * x, axis=-1, keepdims=True) + EPS)


def _rms_bwd(d_y, x, r, g):
    gy = d_y * g
    d_x = r * gy - x * (r * r * r) * jnp.mean(gy * x, axis=-1, keepdims=True)
    d_g = jnp.sum(d_y * (x * r), axis=0, keepdims=True)
    return d_x, d_g


def _lane_lo(shape):
    return lax.broadcasted_iota(jnp.int32, shape, 1) < HEAD_DIM


class _Rider:
    def __init__(self, inputs, out_shapes, sems, begin, end, middle=None):
        self.inputs, self.out_shapes, self.sems = list(inputs), list(out_shapes), list(sems)
        self.begin, self.middle, self.end = begin, middle, end


_issued = []


def _after_last(args, in_specs):
    extra = list(_issued)
    return list(args) + extra, list(in_specs) + [pl.BlockSpec(memory_space=pl.ANY)] * len(extra), len(extra)


def _mark_issued(out):
    _issued[:] = [out]


def _complete_before_next(arrays):
    _issued.extend(arrays)


def _call(body, args, *, name, grid, in_specs, out_specs, out_shape, scratch_shapes=()):
    n_args = len(args)
    args, in_specs, _ = _after_last(args, in_specs)

    def ordered(*refs):
        body(*refs[:n_args], *refs[len(args):])

    outs = pl.pallas_call(ordered, name=name, grid=grid, in_specs=in_specs, out_specs=list(out_specs),
                          out_shape=list(out_shape), scratch_shapes=list(scratch_shapes))(*args)
    _mark_issued(outs[0])
    return list(outs)


def _in_proj(x, g_attn, w_in_t, gq_t, gk_t):
    s = x.shape[0]
    ts = min(TOKEN_TILE, s)

    def body(x_ref, g_ref, w_ref, gq_ref, gk_ref, bq_ref, bk_ref, zqk_ref, qn_ref, kn_ref, v_ref, u_ref):
        xf = x_ref[...]
        hn = ((xf * _rms(xf)) * g_ref[...]).astype(BF16)
        z = _nt(hn, w_ref[...])
        q = z[:, :ATTN_WIDTH]
        k = z[:, ATTN_WIDTH:ATTN_WIDTH + KV_WIDTH]
        zqk_ref[...] = z[:, :ATTN_WIDTH + KV_WIDTH]
        rq = lax.rsqrt(_seg_mean(q * q, bq_ref[...]) + EPS)
        qn_ref[...] = ((q * rq) * gq_ref[...]).astype(BF16)
        rk = lax.rsqrt(_seg_mean(k * k, bk_ref[...]) + EPS)
        kn_ref[...] = ((k * rk) * gk_ref[...]).astype(BF16)
        v_ref[...] = z[:, ATTN_WIDTH + KV_WIDTH:ATTN_WIDTH + 2 * KV_WIDTH].astype(BF16)
        u_ref[...] = z[:, ATTN_WIDTH + 2 * KV_WIDTH:]

    return _call(
        body,
        (x, g_attn, w_in_t, gq_t, gk_t, _head_mean_matrix(ATTN_WIDTH), _head_mean_matrix(KV_WIDTH)),
        name="in_proj",
        grid=(s // ts,),
        in_specs=[
            _rows(ts, D_MODEL),
            _resident((1, D_MODEL)),
            _resident((IN_WIDTH, D_MODEL)),
            _resident((1, ATTN_WIDTH)),
            _resident((1, KV_WIDTH)),
            _resident((ATTN_WIDTH, ATTN_WIDTH)),
            _resident((KV_WIDTH, KV_WIDTH)),
        ],
        out_specs=[
            _rows(ts, ATTN_WIDTH + KV_WIDTH),
            _rows(ts, ATTN_WIDTH),
            _rows(ts, KV_WIDTH),
            _rows(ts, KV_WIDTH),
            _rows(ts, POOL_WIDTH),
        ],
        out_shape=[
            jax.ShapeDtypeStruct((s, ATTN_WIDTH + KV_WIDTH), F32),
            jax.ShapeDtypeStruct((s, ATTN_WIDTH), BF16),
            jax.ShapeDtypeStruct((s, KV_WIDTH), BF16),
            jax.ShapeDtypeStruct((s, KV_WIDTH), BF16),
            jax.ShapeDtypeStruct((s, POOL_WIDTH), F32),
        ],
    )


def _bucket_ranges():
    n = np.arange(MAX_DISTANCE)
    max_exact = N_BUCKETS // 2
    nf = np.maximum(n, 1).astype(np.float64)
    large = max_exact + (np.log(nf / max_exact) / math.log(MAX_DISTANCE / max_exact) * (N_BUCKETS - max_exact)).astype(np.int64)
    bucket = np.where(n < max_exact, n, np.minimum(large, N_BUCKETS - 1))
    out = []
    for b in range(N_BUCKETS):
        idx = np.nonzero(bucket == b)[0]
        out.append((int(idx.min()), int(idx.max()) + 1))
    return out


def _band_distance():
    i = lax.broadcasted_iota(jnp.int32, (BLOCK, 2 * BLOCK), 0)
    j = lax.broadcasted_iota(jnp.int32, (BLOCK, 2 * BLOCK), 1)
    return BLOCK + i - j


BIAS_TABLE_SHAPE = (2, 4 * BLOCK, 2 * BLOCK)


def _write_bias_table(rb_ref, tab_ref):
    d = _band_distance()
    for half, heads in enumerate((HEADS_A, HEADS_B)):
        for slot, h in enumerate(heads):
            t = jnp.full((BLOCK, 2 * BLOCK), -jnp.inf, F32)
            for b, (lo, hi) in enumerate(_bucket_ranges()):
                t = jnp.where((d >= lo) & (d < hi), rb_ref[h, b], t)
            tab_ref[half, slot * BLOCK:(slot + 1) * BLOCK, :] = t


def _bias_table_bwd(dl_acc):
    ranges = _bucket_ranges()
    n_heads = len(HEADS_A) + len(HEADS_B)

    def body(dl_ref, out_ref):
        d = _band_distance()
        row = lax.broadcasted_iota(jnp.int32, (n_heads, SMALL_LANES), 0)
        lane = lax.broadcasted_iota(jnp.int32, (n_heads, SMALL_LANES), 1)
        out = jnp.zeros((n_heads, SMALL_LANES), F32)
        for b, (lo, hi) in enumerate(ranges):
            in_bucket = (d >= lo) & (d < hi)
            for half, heads in enumerate((HEADS_A, HEADS_B)):
                for slot, h in enumerate(heads):
                    g = dl_ref[half, slot * BLOCK:(slot + 1) * BLOCK, :]
                    part = jnp.sum(jnp.where(in_bucket, g, 0.0), axis=0, keepdims=True)
                    tot = jnp.sum(part, axis=1, keepdims=True)
                    out = jnp.where((row == h) & (lane == b), tot, out)
        out_ref[...] = out

    return _call(
        body,
        (dl_acc,),
        name="bias_table_bwd",
        grid=(1,),
        in_specs=[_acc((2, 4 * BLOCK, 2 * BLOCK))],
        out_specs=[_acc((n_heads, SMALL_LANES))],
        out_shape=[jax.ShapeDtypeStruct((n_heads, SMALL_LANES), F32)],
    )


def _stack_heads(pairs, lo_mask):
    zero = jnp.zeros_like(pairs[0])
    lo = [jnp.where(lo_mask, t, zero) for t in pairs]
    hi = [jnp.where(lo_mask, zero, t) for t in pairs]
    return (jnp.concatenate([lo[0], lo[1], hi[2], hi[3]], axis=0),
            jnp.concatenate([hi[0], hi[1], lo[2], lo[3]], axis=0))


def _unstack_heads(out_a, out_b, lo_mask):
    t = lambda x, r: x[r * BLOCK:(r + 1) * BLOCK, :]
    return [
        jnp.where(lo_mask, t(out_a, 0), t(out_b, 0)),
        jnp.where(lo_mask, t(out_a, 1), t(out_b, 1)),
        jnp.where(lo_mask, t(out_b, 2), t(out_a, 2)),
        jnp.where(lo_mask, t(out_b, 3), t(out_a, 3)),
    ]


def _sink_column(sink_ref, heads):
    row = lax.broadcasted_iota(jnp.int32, (4 * BLOCK, 1), 0)
    col = jnp.full((4 * BLOCK, 1), sink_ref[0, heads[3]], F32)
    for slot in (2, 1, 0):
        col = jnp.where(row < (slot + 1) * BLOCK, sink_ref[0, heads[slot]], col)
    return col


def _band_scores(q_stack, keys, tab, first_block):
    s = _nt(q_stack, keys) * (HEAD_DIM ** -0.5) + tab
    if first_block is not None:
        col = lax.broadcasted_iota(jnp.int32, s.shape, 1)
        s = jnp.where(jnp.logical_and(first_block, col < BLOCK), -jnp.inf, s)
    return s


def _softmax_with_sink(s, sink):
    m = jnp.maximum(jnp.max(s, axis=-1, keepdims=True), sink)
    e = jnp.exp(s - m)
    e_sink = jnp.exp(sink - m)
    den = jnp.sum(e, axis=-1, keepdims=True) + e_sink
    return e / den, e_sink / den


def _band_probs(q_stack, keys, tab, sink, first_block):
    return _softmax_with_sink(_band_scores(q_stack, keys, tab, first_block), sink)


def _attn_specs(n_groups):
    group = lambda n: (jnp.minimum(n, n_groups - 1), 0)
    prev = lambda n: (jnp.maximum(jnp.minimum(n, n_groups - 1) * ATTN_STEP_BLOCKS - 1, 0), 0)
    return group, prev


def _band(prev_ref, group_ref, b):
    rows = lambda i: group_ref[i * BLOCK:(i + 1) * BLOCK, :]
    band = jnp.concatenate([prev_ref[...] if b == 0 else rows(b - 1), rows(b)], axis=0)
    return band, pltpu.roll(band, HEAD_DIM, 1)


def _attn_fwd(qn, kn, v, tab, sinks):
    s = qn.shape[0]
    n_groups = s // (ATTN_STEP_BLOCKS * BLOCK)
    group, prev = _attn_specs(n_groups)
    rows = ATTN_STEP_BLOCKS * BLOCK

    def body(sink_ref, q_ref, kc_ref, kp_ref, vc_ref, vp_ref, tab_ref, o_ref):
        first = pl.program_id(0) == 0
        lo_mask = _lane_lo((BLOCK, BLOCK))
        for b in range(ATTN_STEP_BLOCKS):
            at = slice(b * BLOCK, (b + 1) * BLOCK)
            kk, kk_sw = _band(kp_ref, kc_ref, b)
            vv, vv_sw = _band(vp_ref, vc_ref, b)
            q_a, q_b = _stack_heads([q_ref[at, p * BLOCK:(p + 1) * BLOCK] for p in range(4)], lo_mask)
            no_prev = first if b == 0 else None
            p_a, _ = _band_probs(q_a, kk, tab_ref[0], _sink_column(sink_ref, HEADS_A), no_prev)
            p_b, _ = _band_probs(q_b, kk_sw, tab_ref[1], _sink_column(sink_ref, HEADS_B), no_prev)
            out = _unstack_heads(_nn(p_a.astype(BF16), vv), _nn(p_b.astype(BF16), vv_sw), lo_mask)
            for p in range(4):
                o_ref[at, p * BLOCK:(p + 1) * BLOCK] = out[p].astype(BF16)

    return _call(
        body,
        (sinks, qn, kn, kn, v, v, tab),
        name="attn_fwd",
        grid=(n_groups,),
        in_specs=[
            pl.BlockSpec(memory_space=pltpu.SMEM),
            pl.BlockSpec((rows, ATTN_WIDTH), group),
            pl.BlockSpec((rows, KV_WIDTH), group),
            pl.BlockSpec((BLOCK, KV_WIDTH), prev),
            pl.BlockSpec((rows, KV_WIDTH), group),
            pl.BlockSpec((BLOCK, KV_WIDTH), prev),
            _resident((2, 4 * BLOCK, 2 * BLOCK)),
        ],
        out_specs=[pl.BlockSpec((rows, ATTN_WIDTH), group)],
        out_shape=[jax.ShapeDtypeStruct((s, ATTN_WIDTH), BF16)],
    )


def _pooled(u_tile, u_halo, tile_index, tile_rows):
    halo = jnp.where(tile_index > 0, u_halo, 0.0)
    ext = jnp.concatenate([halo, u_tile], axis=0)
    sums = []
    acc = ext
    for shift in (1, 2, 4, 8):
        acc = acc + pltpu.roll(acc, shift, 0)
        sums.append(acc)
    t = tile_index * tile_rows + lax.broadcasted_iota(jnp.int32, (tile_rows, 1), 0)
    out = []
    for g, w in enumerate(POOL_SIZES):
        lanes = slice(g * POOL_GROUP, (g + 1) * POOL_GROUP)
        cnt = jnp.minimum(t + 1, w).astype(F32)
        out.append(sums[g][POOL_HALO:, lanes] / cnt - u_tile[:, lanes])
    return out


def _halo_before(tile):
    return lambda i: (jnp.maximum(i * (tile // POOL_HALO) - 1, 0), 0)


def _mix_out(u, a, x, w_out, w_pool, pool_scale, g_ffn):
    s = x.shape[0]
    ts = min(TOKEN_TILE, s)

    def body(u_ref, uh_ref, a_ref, x_ref, wo_ref, wp_ref, sc_ref, g_ref, h1_ref, hn_ref, m_ref):
        i = pl.program_id(0)
        pooled = _pooled(u_ref[...], uh_ref[...], i, ts)
        for g in range(len(POOL_SIZES)):
            lanes = slice(g * POOL_GROUP, (g + 1) * POOL_GROUP)
            y = _nn(pooled[g].astype(BF16), wp_ref[g].astype(BF16))
            m_ref[:, lanes] = (y * sc_ref[:, lanes]).astype(BF16)
        h1 = x_ref[...] + _nn(a_ref[...], wo_ref[:ATTN_WIDTH, :]) + _nn(m_ref[...], wo_ref[ATTN_WIDTH:, :])
        h1_ref[...] = h1
        hn_ref[...] = ((h1 * _rms(h1)) * g_ref[...]).astype(BF16)

    return _call(
        body,
        (u, u, a, x, w_out, w_pool, pool_scale, g_ffn),
        name="mix_out",
        grid=(s // ts,),
        in_specs=[
            _rows(ts, POOL_WIDTH),
            pl.BlockSpec((POOL_HALO, POOL_WIDTH), _halo_before(ts)),
            _rows(ts, ATTN_WIDTH),
            _rows(ts, D_MODEL),
            _resident((D_MODEL, D_MODEL)),
            _resident((len(POOL_SIZES), POOL_GROUP, POOL_GROUP)),
            _resident((1, POOL_WIDTH)),
            _resident((1, D_MODEL)),
        ],
        out_specs=[_rows(ts, D_MODEL), _rows(ts, D_MODEL), _rows(ts, POOL_WIDTH)],
        out_shape=[
            jax.ShapeDtypeStruct((s, D_MODEL), F32),
            jax.ShapeDtypeStruct((s, D_MODEL), BF16),
            jax.ShapeDtypeStruct((s, POOL_WIDTH), BF16),
        ],
    )


def _ffn_up(hn2, wg_t, wu_t):
    s = hn2.shape[0]
    ts = min(TOKEN_TILE, s)

    def body(hn_ref, wg_ref, wu_ref, gt_ref, up_ref):
        hn = hn_ref[...]
        for c in range(D_FF // FF_CHUNK):
            cols = slice(c * FF_CHUNK, (c + 1) * FF_CHUNK)
            gt_ref[:, cols] = _nt(hn, wg_ref[cols, :]).astype(BF16)
            up_ref[:, cols] = _nt(hn, wu_ref[cols, :]).astype(BF16)

    return _call(
        body,
        (hn2, wg_t, wu_t),
        name="ffn_up",
        grid=(s // ts,),
        in_specs=[_rows(ts, D_MODEL), _resident((D_FF, D_MODEL)), _resident((D_FF, D_MODEL))],
        out_specs=[_rows(ts, D_FF), _rows(ts, D_FF)],
        out_shape=[jax.ShapeDtypeStruct((s, D_FF), BF16), jax.ShapeDtypeStruct((s, D_FF), BF16)],
    )


def _silu_mul(gt, up):
    return (gt * jax.nn.sigmoid(gt)) * up


def _ffn_down_ple(gt, up, h1, w_down, p, target, g_ple, w_pg, w_pp):
    s = h1.shape[0]
    ts = min(TOKEN_TILE, s)
    blk = D_MODEL // N_DEV

    def body(gt_ref, up_ref, h1_ref, wd_ref, p_ref, t_ref, g_ref, wpg_ref, wpp_ref,
             loss_ref, dh_ref, dwpg_ref, dwpp_ref, dg_ref):
        @pl.when(pl.program_id(0) == 0)
        def _():
            loss_ref[...] = jnp.zeros_like(loss_ref)
            dwpg_ref[...] = jnp.zeros_like(dwpg_ref)
            dwpp_ref[...] = jnp.zeros_like(dwpp_ref)
            dg_ref[...] = jnp.zeros_like(dg_ref)

        h2v = h1_ref[...]
        for c in range(D_FF // FF_CHUNK):
            cols = slice(c * FF_CHUNK, (c + 1) * FF_CHUNK)
            act = _silu_mul(gt_ref[:, cols].astype(F32), up_ref[:, cols].astype(F32)).astype(BF16)
            h2v = _nn(act, wd_ref[cols, :]) + h2v
        r = _rms(h2v)
        hn = ((h2v * r) * g_ref[...]).astype(BF16)
        gate = jax.nn.sigmoid(_nn(hn, wpg_ref[...]))
        pb = p_ref[...].astype(BF16)
        pp = _nn(pb, jnp.concatenate([wpp_ref[j] for j in range(N_DEV)], axis=1))
        diff = (h2v + gate * pp) - t_ref[...]
        loss_ref[...] += jnp.sum(jnp.sum(diff * diff, axis=0, keepdims=True), axis=1, keepdims=True) * (0.5 / D_MODEL)
        dy = diff * (1.0 / D_MODEL)
        d_pp = (dy * gate).astype(BF16)
        d_pre = ((dy * pp) * (gate * (1.0 - gate))).astype(BF16)
        d_x, d_g = _rms_bwd(_nt(d_pre, wpg_ref[...]), h2v, r, g_ref[...])
        dg_ref[...] += d_g
        dh_ref[...] = dy + d_x
        d_wpp = _tn(pb, d_pp)
        for j in range(N_DEV):
            dwpp_ref[j] += d_wpp[:, j * blk:(j + 1) * blk]
        dwpg_ref[...] += _tn(hn, d_pre)

    return _call(
        body,
        (gt, up, h1, w_down, p, target, g_ple, w_pg, w_pp),
        name="ffn_down_ple",
        grid=(s // ts,),
        in_specs=[
            _rows(ts, D_FF),
            _rows(ts, D_FF),
            _rows(ts, D_MODEL),
            _resident((D_FF, D_MODEL)),
            _rows(ts, PLE_DIM),
            _rows(ts, D_MODEL),
            _resident((1, D_MODEL)),
            _resident((D_MODEL, D_MODEL)),
            _resident((N_DEV, PLE_DIM, blk)),
        ],
        out_specs=[
            _acc((1, SMALL_LANES)),
            _rows(ts, D_MODEL),
            _acc((D_MODEL, D_MODEL)),
            _acc((N_DEV, PLE_DIM, blk)),
            _acc((1, D_MODEL)),
        ],
        out_shape=[
            jax.ShapeDtypeStruct((1, SMALL_LANES), F32),
            jax.ShapeDtypeStruct((s, D_MODEL), F32),
            jax.ShapeDtypeStruct((D_MODEL, D_MODEL), F32),
            jax.ShapeDtypeStruct((N_DEV, PLE_DIM, blk), F32),
            jax.ShapeDtypeStruct((1, D_MODEL), F32),
        ],
    )


def _ffn_bwd_act(dh2, h1, gt, up, g_ffn, wg_t, wu_t, w_down):
    s = h1.shape[0]
    ts = min(FFN_BWD_TILE, s)

    def body(dh_ref, h1_ref, gt_ref, up_ref, g_ref, wg_ref, wu_ref, wd_ref,
             dgt_ref, dup_ref, dh1_ref, dh1b_ref, dg_ref, dwd_ref, act_ref):
        @pl.when(pl.program_id(0) == 0)
        def _():
            dg_ref[...] = jnp.zeros_like(dg_ref)
            dwd_ref[...] = jnp.zeros_like(dwd_ref)

        dhb = dh_ref[...].astype(BF16)
        d_hn = jnp.zeros((ts, D_MODEL), F32)
        for c in range(D_FF // FF_CHUNK):
            cols = slice(c * FF_CHUNK, (c + 1) * FF_CHUNK)
            d_act = _nt(dhb, wd_ref[cols, :])
            gtv = gt_ref[:, cols].astype(F32)
            upv = up_ref[:, cols].astype(F32)
            sg = jax.nn.sigmoid(gtv)
            silu = gtv * sg
            act_ref[:, cols] = (silu * upv).astype(BF16)
            d_up = (d_act * silu).astype(BF16)
            d_gt = ((d_act * upv) * (sg * (1.0 + gtv * (1.0 - sg)))).astype(BF16)
            dup_ref[:, cols] = d_up
            dgt_ref[:, cols] = d_gt
            d_hn = (_nn(d_gt, wg_ref[cols, :]) + _nn(d_up, wu_ref[cols, :])) + d_hn
        dwd_ref[...] += _tn(act_ref[...], dhb)
        h1v = h1_ref[...]
        d_x, d_g = _rms_bwd(d_hn, h1v, _rms(h1v), g_ref[...])
        dg_ref[...] += d_g
        dh1 = dh_ref[...] + d_x
        dh1_ref[...] = dh1
        dh1b_ref[...] = dh1.astype(BF16)

    return _call(
        body,
        (dh2, h1, gt, up, g_ffn, wg_t, wu_t, w_down),
        name="ffn_bwd_act",
        grid=(s // ts,),
        in_specs=[
            _rows(ts, D_MODEL),
            _rows(ts, D_MODEL),
            _rows(ts, D_FF),
            _rows(ts, D_FF),
            _resident((1, D_MODEL)),
            _resident((D_FF, D_MODEL)),
            _resident((D_FF, D_MODEL)),
            _resident((D_FF, D_MODEL)),
        ],
        out_specs=[
            _rows(ts, D_FF), _rows(ts, D_FF),
            _rows(ts, D_MODEL), _rows(ts, D_MODEL), _acc((1, D_MODEL)), _acc((D_FF, D_MODEL)),
        ],
        out_shape=[
            jax.ShapeDtypeStruct((s, D_FF), BF16),
            jax.ShapeDtypeStruct((s, D_FF), BF16),
            jax.ShapeDtypeStruct((s, D_MODEL), F32),
            jax.ShapeDtypeStruct((s, D_MODEL), BF16),
            jax.ShapeDtypeStruct((1, D_MODEL), F32),
            jax.ShapeDtypeStruct((D_FF, D_MODEL), F32),
        ],
        scratch_shapes=[pltpu.VMEM((ts, D_FF), BF16)],
    )


def _ffn_bwd_w(dgt, dup, hn2):
    s = hn2.shape[0]
    slab = pl.BlockSpec((s, FFN_W_SLAB), lambda i: (0, i))

    def body(dgt_ref, dup_ref, hn_ref, dwg_ref, dwu_ref):
        hn = hn_ref[...]
        dwg_ref[...] = _tn(dgt_ref[...], hn)
        dwu_ref[...] = _tn(dup_ref[...], hn)

    return _call(
        body,
        (dgt, dup, hn2),
        name="ffn_bwd_w",
        grid=(D_FF // FFN_W_SLAB,),
        in_specs=[slab, slab, _resident((s, D_MODEL))],
        out_specs=[_rows(FFN_W_SLAB, D_MODEL)] * 2,
        out_shape=[jax.ShapeDtypeStruct((D_FF, D_MODEL), F32)] * 2,
    )


def _mix_bwd(dh1b, u, a, m, w_out, w_pool, pool_scale):
    s = u.shape[0]
    ts = min(TOKEN_TILE, s)
    nt = s // ts
    halo_after = lambda i: (jnp.minimum((i + 1) * (ts // POOL_HALO), s // POOL_HALO - 1), 0)
    n_groups = len(POOL_SIZES)

    def body(dh_ref, dhn_ref, u_ref, uh_ref, a_ref, m_ref, wo_ref, wp_ref, sc_ref,
             da_ref, du_ref, dwp_ref, dsc_ref, dwo_ref):
        i = pl.program_id(0)

        @pl.when(i == 0)
        def _():
            dwp_ref[...] = jnp.zeros_like(dwp_ref)
            dsc_ref[...] = jnp.zeros_like(dsc_ref)
            dwo_ref[...] = jnp.zeros_like(dwo_ref)

        dh = dh_ref[...]
        dwo_ref[:ATTN_WIDTH, :] += _tn(a_ref[...], dh)
        dwo_ref[ATTN_WIDTH:, :] += _tn(m_ref[...], dh)
        da_ref[...] = _nt(dh, wo_ref[:ATTN_WIDTH, :])
        dh_next = jnp.where(i < nt - 1, dhn_ref[...], jnp.zeros_like(dhn_ref))
        dm_ext = _nt(jnp.concatenate([dh, dh_next], axis=0), wo_ref[ATTN_WIDTH:, :])
        pooled = _pooled(u_ref[...], uh_ref[...], i, ts)
        t_ext = i * ts + lax.broadcasted_iota(jnp.int32, (ts + POOL_HALO, 1), 0)
        for g, w in enumerate(POOL_SIZES):
            lanes = slice(g * POOL_GROUP, (g + 1) * POOL_GROUP)
            wp = wp_ref[g].astype(BF16)
            pg = pooled[g].astype(BF16)
            dm_g = dm_ext[:, lanes]
            dsc_ref[:, lanes] += jnp.sum(dm_g[:ts, :] * _nn(pg, wp), axis=0, keepdims=True)
            dy = (dm_g * sc_ref[:, lanes]).astype(BF16)
            dwp_ref[g] += _tn(pg, dy[:ts, :])
            d_pool = _nt(dy, wp)
            acc = d_pool / jnp.minimum(t_ext + 1, w).astype(F32)
            shift = 1
            while shift < w:
                acc = acc + pltpu.roll(acc, ts + POOL_HALO - shift, 0)
                shift *= 2
            du_ref[:, lanes] = (acc[:ts, :] - d_pool[:ts, :]).astype(BF16)

    return _call(
        body,
        (dh1b, dh1b, u, u, a, m, w_out, w_pool, pool_scale),
        name="mix_bwd",
        grid=(nt,),
        in_specs=[
            _rows(ts, D_MODEL),
            pl.BlockSpec((POOL_HALO, D_MODEL), halo_after),
            _rows(ts, POOL_WIDTH),
            pl.BlockSpec((POOL_HALO, POOL_WIDTH), _halo_before(ts)),
            _rows(ts, ATTN_WIDTH),
            _rows(ts, POOL_WIDTH),
            _resident((D_MODEL, D_MODEL)),
            _resident((n_groups, POOL_GROUP, POOL_GROUP)),
            _resident((1, POOL_WIDTH)),
        ],
        out_specs=[
            _rows(ts, ATTN_WIDTH),
            _rows(ts, POOL_WIDTH),
            _acc((n_groups, POOL_GROUP, POOL_GROUP)),
            _acc((1, POOL_WIDTH)),
            _acc((D_MODEL, D_MODEL)),
        ],
        out_shape=[
            jax.ShapeDtypeStruct((s, ATTN_WIDTH), F32),
            jax.ShapeDtypeStruct((s, POOL_WIDTH), BF16),
            jax.ShapeDtypeStruct((n_groups, POOL_GROUP, POOL_GROUP), F32),
            jax.ShapeDtypeStruct((1, POOL_WIDTH), F32),
            jax.ShapeDtypeStruct((D_MODEL, D_MODEL), F32),
        ],
    )


def _attn_bwd(qn, kn, v, a, da, tab, sinks):
    s = qn.shape[0]
    qb = ATTN_STEP_BLOCKS
    rows = qb * BLOCK
    n_groups = s // rows
    group, prev = _attn_specs(n_groups)
    done = lambda n: (jnp.maximum(n - 1, 0), 0)

    def body(sink_ref, q_ref, kc_ref, kp_ref, vc_ref, vp_ref, o_ref, do_ref, tab_ref,
             dq_ref, dk_ref, dv_ref, dl_ref, ds_ref, k_carry, v_carry, sink_acc):
        n = pl.program_id(0)

        @pl.when(n == 0)
        def _():
            dl_ref[...] = jnp.zeros_like(dl_ref)
            k_carry[...] = jnp.zeros_like(k_carry)
            v_carry[...] = jnp.zeros_like(v_carry)
            sink_acc[...] = jnp.zeros_like(sink_acc)

        @pl.when(n < n_groups)
        def _():
            first = n == 0
            lo_mask = _lane_lo((BLOCK, BLOCK))
            chains = [(b, half) for b in range(qb) for half in range(2)]
            tile = lambda ref, b, p: ref[b * BLOCK:(b + 1) * BLOCK, p * BLOCK:(p + 1) * BLOCK]
            keys = [_band(kp_ref, kc_ref, b) for b in range(qb)]
            vals = [_band(vp_ref, vc_ref, b) for b in range(qb)]
            q_st = [_stack_heads([tile(q_ref, b, p) for p in range(4)], lo_mask) for b in range(qb)]
            do_st = [_stack_heads([tile(do_ref, b, p) for p in range(4)], lo_mask) for b in range(qb)]
            o_st = [_stack_heads([tile(o_ref, b, p).astype(F32) for p in range(4)], lo_mask) for b in range(qb)]
            sink_col = [_sink_column(sink_ref, heads) for heads in (HEADS_A, HEADS_B)]
            scores = {(b, h): _band_scores(q_st[b][h], keys[b][h], tab_ref[h], first if b == 0 else None)
                      for b, h in chains}
            dob = {(b, h): do_st[b][h].astype(BF16) for b, h in chains}
            d_probs = {(b, h): _nt(dob[b, h], vals[b][h]) for b, h in chains}
            delta = {(b, h): jnp.sum(do_st[b][h] * o_st[b][h], axis=-1, keepdims=True) for b, h in chains}
            soft = {(b, h): _softmax_with_sink(scores[b, h], sink_col[h]) for b, h in chains}
            dl = {(b, h): soft[b, h][0] * (d_probs[b, h] - delta[b, h]) for b, h in chains}
            for b, h in chains:
                dl_ref[h] += dl[b, h]
                sink_acc[h] += soft[b, h][1] * delta[b, h]
            dsb = {(b, h): (dl[b, h] * (HEAD_DIM ** -0.5)).astype(BF16) for b, h in chains}
            dq_st = {(b, h): _nn(dsb[b, h], keys[b][h]) for b, h in chains}
            dk_parts = {(b, h): _tn(dsb[b, h], q_st[b][h]) for b, h in chains}
            dv_parts = {(b, h): _tn(soft[b, h][0].astype(BF16), dob[b, h]) for b, h in chains}
            for b in range(qb):
                dq = _unstack_heads(dq_st[b, 0], dq_st[b, 1], lo_mask)
                for p in range(4):
                    dq_ref[b * BLOCK:(b + 1) * BLOCK, p * BLOCK:(p + 1) * BLOCK] = dq[p]
            dks = [dk_parts[b, 0] + pltpu.roll(dk_parts[b, 1], HEAD_DIM, 1) for b in range(qb)]
            dvs = [dv_parts[b, 0] + pltpu.roll(dv_parts[b, 1], HEAD_DIM, 1) for b in range(qb)]
            last = slice((qb - 1) * BLOCK, qb * BLOCK)
            for parts, out_ref, carry in ((dks, dk_ref, k_carry), (dvs, dv_ref, v_carry)):
                out_ref[...] = carry[...]
                out_ref[last, :] += parts[0][:BLOCK, :]
                for b in range(qb):
                    own = parts[b][BLOCK:, :]
                    carry[b * BLOCK:(b + 1) * BLOCK, :] = own + parts[b + 1][:BLOCK, :] if b + 1 < qb else own

        @pl.when(n == n_groups)
        def _():
            dk_ref[...] = k_carry[...]
            dv_ref[...] = v_carry[...]
            for half, heads in enumerate((HEADS_A, HEADS_B)):
                for slot, h in enumerate(heads):
                    tot = jnp.sum(sink_acc[half, slot * BLOCK:(slot + 1) * BLOCK, :], axis=0, keepdims=True)
                    ds_ref[h:h + 1, :] = jnp.broadcast_to(-tot, (1, SMALL_LANES))

    return _call(
        body,
        (sinks, qn, kn, kn, v, v, a, da, tab),
        name="attn_bwd",
        grid=(n_groups + 1,),
        in_specs=[
            pl.BlockSpec(memory_space=pltpu.SMEM),
            pl.BlockSpec((rows, ATTN_WIDTH), group),
            pl.BlockSpec((rows, KV_WIDTH), group),
            pl.BlockSpec((BLOCK, KV_WIDTH), prev),
            pl.BlockSpec((rows, KV_WIDTH), group),
            pl.BlockSpec((BLOCK, KV_WIDTH), prev),
            pl.BlockSpec((rows, ATTN_WIDTH), group),
            pl.BlockSpec((rows, ATTN_WIDTH), group),
            _resident((2, 4 * BLOCK, 2 * BLOCK)),
        ],
        out_specs=[
            pl.BlockSpec((rows, ATTN_WIDTH), group),
            pl.BlockSpec((rows, KV_WIDTH), done),
            pl.BlockSpec((rows, KV_WIDTH), done),
            _acc((2, 4 * BLOCK, 2 * BLOCK)),
            _acc((N_DEV, SMALL_LANES)),
        ],
        out_shape=[
            jax.ShapeDtypeStruct((s, ATTN_WIDTH), F32),
            jax.ShapeDtypeStruct((s, KV_WIDTH), F32),
            jax.ShapeDtypeStruct((s, KV_WIDTH), F32),
            jax.ShapeDtypeStruct((2, 4 * BLOCK, 2 * BLOCK), F32),
            jax.ShapeDtypeStruct((N_DEV, SMALL_LANES), F32),
        ],
        scratch_shapes=[
            pltpu.VMEM((rows, KV_WIDTH), F32),
            pltpu.VMEM((rows, KV_WIDTH), F32),
            pltpu.VMEM((2, 4 * BLOCK, 1), F32),
        ],
    )


def _fold_heads(acc):
    t = acc + pltpu.roll(acc, HEAD_DIM, 1)
    out = t[:, :SMALL_LANES]
    for g in range(1, acc.shape[1] // SMALL_LANES):
        out = out + t[:, g * SMALL_LANES:(g + 1) * SMALL_LANES]
    return out


def _in_proj_bwd(dqn, dkn, dv, du, zqk, x, dh1, g_attn, gq_t, gk_t, w_in_t):
    s = x.shape[0]
    ts = min(TOKEN_TILE, s)
    nt = s // ts

    def head_norm_bwd(d_n, raw, g_t, bmat):
        r = lax.rsqrt(_seg_mean(raw * raw, bmat) + EPS)
        gy = d_n * g_t
        d_raw = r * gy - raw * (r * r * r) * _seg_mean(gy * raw, bmat)
        return d_raw, jnp.sum(d_n * (raw * r), axis=0, keepdims=True)

    def body(dqn_ref, dkn_ref, dv_ref, du_ref, zqk_ref, x_ref, dh1_ref, g_ref, gq_ref, gk_ref, w_ref, bq_ref, bk_ref,
             gx_ref, dw_ref, dg_ref, dgq_ref, dgk_ref, dz_ref, gq_acc, gk_acc):
        i = pl.program_id(0)

        @pl.when(i == 0)
        def _():
            dw_ref[...] = jnp.zeros_like(dw_ref)
            dg_ref[...] = jnp.zeros_like(dg_ref)
            gq_acc[...] = jnp.zeros_like(gq_acc)
            gk_acc[...] = jnp.zeros_like(gk_acc)

        d_q, d_gq = head_norm_bwd(dqn_ref[...], zqk_ref[:, :ATTN_WIDTH], gq_ref[...], bq_ref[...])
        d_k, d_gk = head_norm_bwd(dkn_ref[...], zqk_ref[:, ATTN_WIDTH:], gk_ref[...], bk_ref[...])
        gq_acc[...] += d_gq
        gk_acc[...] += d_gk
        dz_ref[:, :ATTN_WIDTH] = d_q.astype(BF16)
        dz_ref[:, ATTN_WIDTH:ATTN_WIDTH + KV_WIDTH] = d_k.astype(BF16)
        dz_ref[:, ATTN_WIDTH + KV_WIDTH:ATTN_WIDTH + 2 * KV_WIDTH] = dv_ref[...].astype(BF16)
        dz_ref[:, ATTN_WIDTH + 2 * KV_WIDTH:] = du_ref[...]
        dz = dz_ref[...]
        xf = x_ref[...]
        r = _rms(xf)
        hn = ((xf * r) * g_ref[...]).astype(BF16)
        d_x, d_g = _rms_bwd(_nn(dz, w_ref[...]), xf, r, g_ref[...])
        dg_ref[...] += d_g
        gx_ref[...] = dh1_ref[...] + d_x
        dw_ref[...] += _tn(dz, hn)

        @pl.when(i == nt - 1)
        def _():
            dgq_ref[...] = _fold_heads(gq_acc[...])
            dgk_ref[...] = _fold_heads(gk_acc[...])

    return _call(
        body,
        (dqn, dkn, dv, du, zqk, x, dh1, g_attn, gq_t, gk_t, w_in_t,
      _head_mean_matrix(ATTN_WIDTH), _head_mean_matrix(KV_WIDTH)),
        name="in_proj_bwd",
        grid=(nt,),
        in_specs=[
            _rows(ts, ATTN_WIDTH),
            _rows(ts, KV_WIDTH),
            _rows(ts, KV_WIDTH),
            _rows(ts, POOL_WIDTH),
            _rows(ts, ATTN_WIDTH + KV_WIDTH),
            _rows(ts, D_MODEL),
            _rows(ts, D_MODEL),
            _resident((1, D_MODEL)),
            _resident((1, ATTN_WIDTH)),
            _resident((1, KV_WIDTH)),
            _resident((IN_WIDTH, D_MODEL)),
            _resident((ATTN_WIDTH, ATTN_WIDTH)),
            _resident((KV_WIDTH, KV_WIDTH)),
        ],
        out_specs=[
            _rows(ts, D_MODEL),
            _acc((IN_WIDTH, D_MODEL)),
            _acc((1, D_MODEL)),
            _acc((1, SMALL_LANES)),
            _acc((1, SMALL_LANES)),
        ],
        out_shape=[
            jax.ShapeDtypeStruct((s, D_MODEL), F32),
            jax.ShapeDtypeStruct((IN_WIDTH, D_MODEL), F32),
            jax.ShapeDtypeStruct((1, D_MODEL), F32),
            jax.ShapeDtypeStruct((1, SMALL_LANES), F32),
            jax.ShapeDtypeStruct((1, SMALL_LANES), F32),
        ],
        scratch_shapes=[
            pltpu.VMEM((ts, IN_WIDTH), BF16),
            pltpu.VMEM((1, ATTN_WIDTH), F32),
            pltpu.VMEM((1, KV_WIDTH), F32),
        ],
    )


BIG_WEIGHTS = (
    ("w_in", True, IN_WIDTH // N_DEV, D_MODEL),
    ("w_out", False, D_MODEL // N_DEV, D_MODEL),
    ("w_gate", True, D_FF // N_DEV, D_MODEL),
    ("w_up", True, D_FF // N_DEV, D_MODEL),
    ("w_down", False, D_FF // N_DEV, D_MODEL),
    ("w_ple_gate", False, D_MODEL // N_DEV, D_MODEL),
    ("w_ple_proj", False, PLE_DIM, D_MODEL // N_DEV),
)
N_BIG = len(BIG_WEIGHTS)


def _place():
    x, y, c = lax.axis_index("x"), lax.axis_index("y"), lax.axis_index("c")
    chips = [(1 - x, y), (x, 1 - y), (1 - x, 1 - y)]
    return x, y, c, chips


class _Gather:
    def __init__(self, n):
        self.n = n
        self.sems = [pltpu.SemaphoreType.DMA((n, 7)), pltpu.SemaphoreType.DMA((n, 7)), pltpu.SemaphoreType.DMA((n,))]

    def _ctx(self, srcs, outs, sems):
        send_sems, recv_sems, local_sems = sems
        x, y, c, chips = _place()
        me, sibling = (x, y, c), (x, y, 1 - c)

        def block(k, owner):
            px, py, pc = owner
            return outs[k].at[4 * px + 2 * py + pc]

        def copy(k, idx, owner, to, mine=False):
            return pltpu.make_async_remote_copy(
                src_ref=srcs[k] if mine else block(k, owner), dst_ref=block(k, owner),
                send_sem=send_sems.at[k, idx], recv_sem=recv_sems.at[k, idx], device_id=to, device_id_type=MESH)

        def local(k):
            return pltpu.make_async_copy(srcs[k], block(k, me), local_sems.at[k])

        return c, chips, me, sibling, copy, local

    def begin(self, srcs, outs, sems):
        c, chips, me, sibling, copy, local = self._ctx(srcs, outs, sems)
        for k in range(self.n):
            local(k).start()
            copy(k, 0, me, sibling, mine=True).start()
            for j, chip in enumerate(chips):
                copy(k, 1 + j, me, (*chip, c), mine=True).start()

    def middle(self, srcs, outs, sems):
        c, chips, me, sibling, copy, local = self._ctx(srcs, outs, sems)
        for j, chip in enumerate(chips):
            for k in range(self.n):
                copy(k, 1 + j, (*chip, c), me).wait_recv()
                copy(k, 4 + j, (*chip, c), sibling).start()

    def end(self, srcs, outs, sems):
        c, chips, me, sibling, copy, local = self._ctx(srcs, outs, sems)
        for k in range(self.n):
            copy(k, 0, sibling, me).wait_recv()
            for j, chip in enumerate(chips):
                copy(k, 4 + j, (*chip, 1 - c), me).wait_recv()
        for k in range(self.n):
            copy(k, 0, me, sibling, mine=True).wait_send()
            for j, chip in enumerate(chips):
                copy(k, 1 + j, me, (*chip, c), mine=True).wait_send()
                copy(k, 4 + j, (*chip, c), sibling).wait_send()
            local(k).wait()


def _gather_rider(shards):
    g = _Gather(len(shards))
    shapes = [jax.ShapeDtypeStruct((N_DEV, *sh.shape), sh.dtype) for sh in shards]
    return _Rider(shards, shapes, g.sems, g.begin, g.end, g.middle)


def _cast_and_gather_first(shards, rel_bias_t):
    g = _Gather(1)
    any_spec = pl.BlockSpec(memory_space=pl.ANY)
    vmem = pl.BlockSpec(memory_space=pltpu.VMEM)

    def body(*refs):
        ins, rb_ref, outs = refs[:N_BIG], refs[N_BIG], refs[N_BIG + 1:2 * N_BIG + 1]
        gathered, tab_ref, sems = refs[2 * N_BIG + 1], refs[2 * N_BIG + 2], refs[2 * N_BIG + 3:]
        outs[0][...] = ins[0][...].astype(BF16)
        g.begin(outs[:1], [gathered], sems)
        for k in range(1, N_BIG):
            outs[k][...] = ins[k][...].astype(BF16)
        _write_bias_table(rb_ref, tab_ref)
        g.middle(outs[:1], [gathered], sems)
        g.end(outs[:1], [gathered], sems)

    res = pl.pallas_call(
        body,
        name="cast_and_gather_first",
        in_specs=[vmem] * N_BIG + [pl.BlockSpec(memory_space=pltpu.SMEM)],
        out_specs=[vmem] * N_BIG + [any_spec, vmem],
        out_shape=[jax.ShapeDtypeStruct((r, c), BF16) for _, _, r, c in BIG_WEIGHTS]
        + [jax.ShapeDtypeStruct((N_DEV, *BIG_WEIGHTS[0][2:]), BF16), jax.ShapeDtypeStruct(BIAS_TABLE_SHAPE, F32)],
        scratch_shapes=g.sems,
    )(*shards, rel_bias_t)
    return list(res[:N_BIG]), res[N_BIG], res[N_BIG + 1]


def _sibling_rider(grads):
    n = len(grads)

    def copies(gs, lands, sems):
        send_sems, recv_sems = sems
        x, y, c, _ = _place()
        return [
            pltpu.make_async_remote_copy(
                src_ref=gs[k].at[:, 1 - c], dst_ref=lands[k], send_sem=send_sems.at[k], recv_sem=recv_sems.at[k],
                device_id=(x, y, 1 - c), device_id_type=MESH)
            for k in range(n)
        ]

    def begin(gs, lands, sems):
        for cp in copies(gs, lands, sems):
            cp.start()

    def end(gs, lands, sems):
        for cp in copies(gs, lands, sems):
            cp.wait()

    shapes = [jax.ShapeDtypeStruct((N_CHIPS, *g.shape[2:]), F32) for g in grads]
    return _Rider(grads, shapes, [pltpu.SemaphoreType.DMA((n,)), pltpu.SemaphoreType.DMA((n,))], begin, end)


def _chip_of_relation(j, place):
    x, y = place[0], place[1]
    return jnp.where(j == 0, 2 * (1 - x) + y, jnp.where(j == 1, 2 * x + 1 - y, 2 * (1 - x) + 1 - y))


def _chip_sum(ks, place, grads, from_sibling):
    shapes = [BIG_WEIGHTS[k][2:] for k in ks]
    n_rel = N_CHIPS - 1
    per_step = n_rel if 2 * n_rel * sum(4 * r * c for r, c in shapes) <= CHIP_SUM_ONE_STEP_BYTES else 1
    operands, specs = [], []
    for (r, c), g, l in zip(shapes, grads, from_sibling):
        for q in range(per_step):
            chip = lambda j, place, q=q: _chip_of_relation(j * per_step + q, place)
            operands += [g, l]
            specs += [pl.BlockSpec((1, 1, r, c), lambda j, place, chip=chip: (chip(j, place), place[2], 0, 0)),
                      pl.BlockSpec((1, r, c), lambda j, place, chip=chip: (chip(j, place), 0, 0))]
    args, in_specs, _ = _after_last(operands, specs)

    def body(place_ref, *refs):
        ins, outs = refs[:len(operands)], refs[len(args):]
        for i in range(len(ks)):
            for q in range(per_step):
                mine_ref, sib_ref = ins[2 * (i * per_step + q):2 * (i * per_step + q) + 2]
                outs[i][q] = (mine_ref[0, 0] + sib_ref[0]).astype(BF16)

    outs = pl.pallas_call(
        body,
        name="chip_sum_" + "_".join(BIG_WEIGHTS[k][0] for k in ks),
        grid_spec=pltpu.PrefetchScalarGridSpec(
            num_scalar_prefetch=1,
            grid=(n_rel // per_step,),
            in_specs=in_specs,
            out_specs=[pl.BlockSpec((per_step, r, c), lambda j, place: (j, 0, 0)) for r, c in shapes],
        ),
        out_shape=[jax.ShapeDtypeStruct((n_rel, r, c), BF16) for r, c in shapes],
    )(place, *args)
    _mark_issued(outs[0])
    return list(outs)


def _chips_rider(to_send, small=None):
    n = len(to_send)
    inputs = list(to_send) + ([] if small is None else [small])
    shapes = [jax.ShapeDtypeStruct((3, *t.shape[1:]), BF16) for t in to_send]
    sems = [pltpu.SemaphoreType.DMA((max(n, 1), 3)), pltpu.SemaphoreType.DMA((max(n, 1), 3))]
    if small is not None:
        shapes.append(jax.ShapeDtypeStruct((N_DEV, *small.shape), F32))
        sems += [pltpu.SemaphoreType.DMA((7,)), pltpu.SemaphoreType.DMA((7,)), pltpu.SemaphoreType.DMA]

    def copies(ins, outs, sem_refs):
        x, y, c, chips = _place()
        out = []
        for k in range(n):
            for j, (px, py) in enumerate(chips):
                out.append(pltpu.make_async_remote_copy(
                    src_ref=ins[k].at[j], dst_ref=outs[k].at[j],
                    send_sem=sem_refs[0].at[k, j], recv_sem=sem_refs[1].at[k, j],
                    device_id=(px, py, c), device_id_type=MESH))
        local = None
        if small is not None:
            me = 4 * x + 2 * y + c
            local = pltpu.make_async_copy(ins[n], outs[n].at[me], sem_refs[4])
            rel = 0
            for fx in (0, 1):
                for fy in (0, 1):
                    for fc in (0, 1):
                        if (fx, fy, fc) != (0, 0, 0):
                            out.append(pltpu.make_async_remote_copy(
                                src_ref=ins[n], dst_ref=outs[n].at[me],
                                send_sem=sem_refs[2].at[rel], recv_sem=sem_refs[3].at[rel],
                                device_id=(x ^ fx, y ^ fy, c ^ fc), device_id_type=MESH))
                            rel += 1
        return out, local

    def begin(ins, outs, sem_refs):
        remote, local = copies(ins, outs, sem_refs)
        if local is not None:
            local.start()
        for cp in remote:
            cp.start()

    def end(ins, outs, sem_refs):
        remote, local = copies(ins, outs, sem_refs)
        for cp in remote:
            cp.wait()
        if local is not None:
            local.wait()

    return _Rider(inputs, shapes, sems, begin, end)


def _together(first, second):
    cut = lambda refs, a, b: (refs[:len(a)], refs[len(a):len(a) + len(b)])

    def run(which):
        def fn(ins, outs, sems):
            parts = zip((first, second), cut(ins, first.inputs, second.inputs),
                        cut(outs, first.out_shapes, second.out_shapes), cut(sems, first.sems, second.sems))
            for rider, i, o, s in parts:
                hook = getattr(rider, which)
                if hook is not None:
                    hook(i, o, s)
        return fn

    return _Rider(first.inputs + second.inputs, first.out_shapes + second.out_shapes, first.sems + second.sems,
                  run("begin"), run("end"), run("middle"))


PEER_SETS = {"sibling": 1, "chips": 2, "sibling+chips": 3, "all": 4}


def _peers(pattern):
    x, y, c, chips = _place()
    sibling, others = [(x, y, 1 - c)], [(*chip, c) for chip in chips]
    if pattern == "all":
        return sibling + others + [(*chip, 1 - c) for chip in chips]
    return {"sibling": sibling, "chips": others, "sibling+chips": sibling + others}[pattern]


def _on_sequencer(name, pattern, rider):
    n_in, n_out = len(rider.inputs), len(rider.out_shapes)

    def body(*refs):
        ins, outs, sems = refs[:n_in], refs[n_in:n_in + n_out], refs[n_in + n_out:]
        peers = _peers(pattern)
        barrier = pltpu.get_barrier_semaphore()
        for peer in peers:
            pl.semaphore_signal(barrier, inc=1, device_id=peer, device_id_type=MESH)
        pl.semaphore_wait(barrier, len(peers))
        rider.begin(ins, outs, sems)
        if rider.middle is not None:
            rider.middle(ins, outs, sems)
        rider.end(ins, outs, sems)

    outs = pl.kernel(
        body,
        name=name,
        out_type=tuple(rider.out_shapes),
        mesh=plsc.ScalarSubcoreMesh(axis_name="sequencer", num_cores=1),
        scratch_types=tuple(rider.sems),
        compiler_params=pltpu.CompilerParams(collective_id=PEER_SETS[pattern]),
    )(*rider.inputs)
    return list(outs)


def _adamw(w, g, m, v):
    m = ADAM_B1 * m + (1.0 - ADAM_B1) * g
    v = ADAM_B2 * v + (1.0 - ADAM_B2) * jnp.square(g)
    m_hat = m / (1.0 - ADAM_B1 ** ADAM_STEP)
    v_hat = v / (1.0 - ADAM_B2 ** ADAM_STEP)
    delta = -ADAM_LR * (m_hat / (jnp.sqrt(v_hat) + ADAM_EPS) + ADAM_WD * w)
    return delta, m, v


def _adamw_big(ks, place, operands):
    n = len(ks)
    tiles = lambda i, place: (i, 0)
    in_specs, out_specs, out_shape = [], [], []
    for k in ks:
        _, _, r, c = BIG_WEIGHTS[k]
        tile = r // 2
        in_specs += [
            pl.BlockSpec((1, 1, tile, c), lambda i, place: (2 * place[0] + place[1], place[2], i, 0)),
            pl.BlockSpec((1, tile, c), lambda i, place: (2 * place[0] + place[1], i, 0)),
            pl.BlockSpec((3, tile, c), lambda i, place: (0, i, 0)),
        ] + [pl.BlockSpec((tile, c), tiles)] * 3
        out_specs += [pl.BlockSpec((tile, c), tiles)] * 4
        out_shape += [jax.ShapeDtypeStruct((r, c), F32)] * 4
    args, in_specs, _ = _after_last(sum((list(ops) for ops in operands), []), in_specs)

    def body(place_ref, *refs):
        ins, outs = refs[:6 * n], refs[len(args):]
        for i in range(n):
            mine_ref, sib_ref, land_ref, w_ref, m_ref, v_ref = ins[6 * i:6 * i + 6]
            g_ref, d_ref, nm_ref, nv_ref = outs[4 * i:4 * i + 4]
            g = mine_ref[0, 0] + sib_ref[0]
            g = ((g + land_ref[0].astype(F32)) + land_ref[1].astype(F32)) + land_ref[2].astype(F32)
            g_ref[...] = g
            d_ref[...], nm_ref[...], nv_ref[...] = _adamw(w_ref[...], g, m_ref[...], v_ref[...])

    outs = pl.pallas_call(
        body,
        name="adamw_" + "_".join(BIG_WEIGHTS[k][0] for k in ks),
        grid_spec=pltpu.PrefetchScalarGridSpec(
            num_scalar_prefetch=1, grid=(2,), in_specs=in_specs, out_specs=out_specs),
        out_shape=out_shape,
    )(place, *args)
    _mark_issued(outs[0])
    return [outs[4 * i:4 * i + 4] for i in range(n)]


def _pack_small(arrays):
    rows, offsets = [], []
    at = 0
    for a in arrays:
        if a.ndim != 2 or a.shape[1] != SMALL_LANES or a.shape[0] % 8:
            flat = a.reshape(-1)
            n_rows = -(-flat.shape[0] // (8 * SMALL_LANES)) * 8
            a = jnp.pad(flat, (0, n_rows * SMALL_LANES - flat.shape[0])).reshape(n_rows, SMALL_LANES)
        rows.append(a)
        offsets.append(at)
        at += a.shape[0]
    return jnp.concatenate(rows, axis=0), offsets


def _unpack_small(tot, at, shape):
    r, c = shape
    if r % 8 == 0:
        return tot[at:at + r, :c]
    assert r == 1
    if c <= SMALL_LANES:
        return tot[at:at + 1, :c]
    return jnp.concatenate([tot[at + j:at + j + 1, :] for j in range(c // SMALL_LANES)], axis=1)


def _small_update(packs, loss_at, grads_at, ws, ms, vs):
    n, n_packs = len(ws), len(packs)

    def body(*refs):
        pack_refs, refs = refs[:n_packs], refs[n_packs:]
        w_refs, m_refs, v_refs, loss_ref, outs = refs[:n], refs[n:2 * n], refs[2 * n:3 * n], refs[3 * n], refs[3 * n + 1:]
        tots = []
        for p_ref in pack_refs:
            tot = p_ref[0]
            for j in range(1, N_DEV):
                tot = tot + p_ref[j]
            tots.append(tot)
        loss_ref[...] = _unpack_small(tots[loss_at[0]], loss_at[1], (1, 1))
        for i, (pack, at) in enumerate(grads_at):
            g = _unpack_small(tots[pack], at, w_refs[i].shape)
            outs[i][...] = g
            outs[n + i][...], outs[2 * n + i][...], outs[3 * n + i][...] = _adamw(
                w_refs[i][...], g, m_refs[i][...], v_refs[i][...])

    shapes = [jax.ShapeDtypeStruct(w.shape, F32) for w in ws]
    outs = pl.pallas_call(body, name="small_update", out_shape=[jax.ShapeDtypeStruct((1, 1), F32)] + shapes * 4)(
        *packs, *ws, *ms, *vs)
    return outs[0], outs[1:]


SMALL_NAMES = ("g_attn_norm", "g_q", "g_k", "attn_sinks", "rel_bias", "w_pool", "pool_scale", "g_ffn_norm", "g_ple_norm")


def kernel(x, p, w_in, w_out, g_attn_norm, g_q, g_k, attn_sinks, rel_bias, w_pool, pool_scale, g_ffn_norm, w_gate, w_up, w_down, g_ple_norm, w_ple_gate, w_ple_proj, loss_target, m_w_in, m_w_out, m_g_attn_norm, m_g_q, m_g_k, m_attn_sinks, m_rel_bias, m_w_pool, m_pool_scale, m_g_ffn_norm, m_w_gate, m_w_up, m_w_down, m_g_ple_norm, m_w_ple_gate, m_w_ple_proj, v_w_in, v_w_out, v_g_attn_norm, v_g_q, v_g_k, v_attn_sinks, v_rel_bias, v_w_pool, v_pool_scale, v_g_ffn_norm, v_w_gate, v_w_up, v_w_down, v_g_ple_norm, v_w_ple_gate, v_w_ple_proj):
    weights = dict(w_in=w_in, w_out=w_out, g_attn_norm=g_attn_norm, g_q=g_q, g_k=g_k, attn_sinks=attn_sinks,
                   rel_bias=rel_bias, w_pool=w_pool, pool_scale=pool_scale, g_ffn_norm=g_ffn_norm, w_gate=w_gate,
                   w_up=w_up, w_down=w_down, g_ple_norm=g_ple_norm, w_ple_gate=w_ple_gate, w_ple_proj=w_ple_proj)
    m_in = dict(w_in=m_w_in, w_out=m_w_out, g_attn_norm=m_g_attn_norm, g_q=m_g_q, g_k=m_g_k, attn_sinks=m_attn_sinks,
                rel_bias=m_rel_bias, w_pool=m_w_pool, pool_scale=m_pool_scale, g_ffn_norm=m_g_ffn_norm, w_gate=m_w_gate,
                w_up=m_w_up, w_down=m_w_down, g_ple_norm=m_g_ple_norm, w_ple_gate=m_w_ple_gate, w_ple_proj=m_w_ple_proj)
    v_in = dict(w_in=v_w_in, w_out=v_w_out, g_attn_norm=v_g_attn_norm, g_q=v_g_q, g_k=v_g_k, attn_sinks=v_attn_sinks,
                rel_bias=v_rel_bias, w_pool=v_w_pool, pool_scale=v_pool_scale, g_ffn_norm=v_g_ffn_norm, w_gate=v_w_gate,
                w_up=v_w_up, w_down=v_w_down, g_ple_norm=v_g_ple_norm, w_ple_gate=v_w_ple_gate, w_ple_proj=v_w_ple_proj)

    _issued.clear()
    xs = x[0]
    ps = p[0, 0]
    target = loss_target[0]
    wp = w_pool[0]
    gq_t = jnp.tile(g_q, (1, ATTN_WIDTH // HEAD_DIM))
    gk_t = jnp.tile(g_k, (1, KV_WIDTH // HEAD_DIM))

    def to_blocks(k, arr):
        return jnp.swapaxes(arr[0], 0, 1) if BIG_WEIGHTS[k][1] else arr[0]

    def from_blocks(k, arr):
        return (jnp.swapaxes(arr, 0, 1) if BIG_WEIGHTS[k][1] else arr)[None]

    IN, OUT, GATE, UP, DOWN, PG, PP = range(N_BIG)
    full = lambda g: g.reshape(N_DEV * g.shape[1], g.shape[2])
    halves = lambda k, g: g.reshape(N_CHIPS, 2, *BIG_WEIGHTS[k][2:])
    place = jnp.stack([lax.axis_index("x"), lax.axis_index("y"), lax.axis_index("c")]).astype(jnp.int32)

    sh, w_in_g, tab = _cast_and_gather_first(
        [to_blocks(k, weights[name]) for k, (name, _, _, _) in enumerate(BIG_WEIGHTS)], rel_bias.T)
    w_in_t = full(w_in_g)

    (w_out_g,) = _on_sequencer("gather_out", "sibling+chips", _gather_rider([sh[OUT]]))
    wg_g, wu_g = _on_sequencer("gather_gate_up", "sibling+chips", _gather_rider([sh[GATE], sh[UP]]))
    wd_g, w_pg_g, w_pp_g = _on_sequencer("gather_down_ple", "sibling+chips", _gather_rider([sh[DOWN], sh[PG], sh[PP]]))
    (zqk, qn, kn, v, u) = _in_proj(xs, g_attn_norm, w_in_t, gq_t, gk_t)
    (a,) = _attn_fwd(qn, kn, v, tab, attn_sinks)
    w_out_f = full(w_out_g)
    (h1, hn2, m_out) = _mix_out(u, a, xs, w_out_f, wp, pool_scale, g_ffn_norm)
    wg_t, wu_t = full(wg_g), full(wu_g)
    (gt, up) = _ffn_up(hn2, wg_t, wu_t)
    w_down_f = full(wd_g)

    partial, from_sibling, sums, landed = [None] * N_BIG, [None] * N_BIG, [None] * N_BIG, [None] * N_BIG

    def to_sibling(name, ks, grads):
        for k, g in zip(ks, grads):
            partial[k] = halves(k, g)
        got = _on_sequencer(name, "sibling", _sibling_rider([partial[k] for k in ks]))
        for k, g in zip(ks, got):
            from_sibling[k] = g

    def chip_sum(*ks):
        for k, s in zip(ks, _chip_sum(ks, place, [partial[k] for k in ks], [from_sibling[k] for k in ks])):
            sums[k] = s

    def to_chips(name, ks, small=None):
        got = _on_sequencer(name, "chips" if small is None else "all", _chips_rider([sums[k] for k in ks], small))
        for k, g in zip(ks, got):
            landed[k] = g
        return got[len(ks):]

    (loss_part, dh2, d_wpg, d_wpp, d_g_ple) = _ffn_down_ple(
        gt, up, h1, w_down_f, ps, target, g_ple_norm, full(w_pg_g), w_pp_g)
    to_sibling("sibling_ple", (PG, PP), (d_wpg, d_wpp))
    (dgt, dup, dh1, dh1b, d_g_ffn, d_wd) = _ffn_bwd_act(dh2, h1, gt, up, g_ffn_norm, wg_t, wu_t, w_down_f)
    to_sibling("sibling_down", (DOWN,), (d_wd,))
    chip_sum(PG, PP)
    to_chips("chips_ple", (PG, PP))
    chip_sum(DOWN)
    to_chips("chips_down", (DOWN,))
    (d_wg_t, d_wu_t) = _ffn_bwd_w(dgt, dup, hn2)
    to_sibling("sibling_gate_up", (GATE, UP), (d_wg_t, d_wu_t))
    _complete_before_next([landed[PG], landed[PP], landed[DOWN]])
    (da, du, d_wpool, d_scale, d_wo) = _mix_bwd(dh1b, u, a, m_out, w_out_f, wp, pool_scale)
    to_sibling("sibling_out", (OUT,), (d_wo,))
    chip_sum(GATE, UP)
    to_chips("chips_gate_up", (GATE, UP))
    (dqn, dkn, dv, dl_acc, d_sinks) = _attn_bwd(qn, kn, v, a, da, tab, attn_sinks)
    chip_sum(OUT)
    early, early_at = _pack_small([d_wpool.reshape(POOL_WIDTH, POOL_GROUP), d_scale, d_g_ffn, d_g_ple, loss_part[:, :1]])
    landed[OUT], early_all = _on_sequencer(
        "chips_out", "sibling+chips", _together(_chips_rider([sums[OUT]]), _gather_rider([early])))
    (grad_x, d_win_t, d_g_attn, d_gq, d_gk) = _in_proj_bwd(dqn, dkn, dv, du, zqk, xs, dh1, g_attn_norm, gq_t, gk_t, w_in_t)
    to_sibling("sibling_in", (IN,), (d_win_t,))
    _complete_before_next([landed[OUT], landed[GATE], landed[UP], early_all])
    (d_rel_t,) = _bias_table_bwd(dl_acc)
    chip_sum(IN)
    late, late_at = _pack_small([d_g_attn, d_gq[:, :HEAD_DIM], d_gk[:, :HEAD_DIM], d_sinks[:, 0], d_rel_t])
    (late_all,) = to_chips("chips_in", (IN,), late)

    out = {"grad": {}, "delta": {}, "new_m": {}, "new_v": {}}
    for ks in ((PG, PP, DOWN), (OUT, GATE, UP), (IN,)):
        names = [BIG_WEIGHTS[k][0] for k in ks]
        results = _adamw_big(ks, place, [
            (partial[k], from_sibling[k], landed[k], to_blocks(k, weights[n]), to_blocks(k, m_in[n]),
             to_blocks(k, v_in[n])) for k, n in zip(ks, names)])
        for k, name, res in zip(ks, names, results):
            for kind, r in zip(("grad", "delta", "new_m", "new_v"), res):
                out[kind][name] = from_blocks(k, r)
    def as_rows(name, arr):
        return arr.T if name == "rel_bias" else arr.reshape(POOL_WIDTH, POOL_GROUP) if name == "w_pool" else arr

    def from_rows(name, arr):
        return arr.T if name == "rel_bias" else arr.reshape(w_pool.shape) if name == "w_pool" else arr

    grads_at = dict(w_pool=(0, early_at[0]), pool_scale=(0, early_at[1]), g_ffn_norm=(0, early_at[2]),
                    g_ple_norm=(0, early_at[3]), g_attn_norm=(1, late_at[0]), g_q=(1, late_at[1]), g_k=(1, late_at[2]),
                    attn_sinks=(1, late_at[3]), rel_bias=(1, late_at[4]))
    loss, updates = _small_update(
        [early_all, late_all], (0, early_at[4]), [grads_at[n] for n in SMALL_NAMES],
        [as_rows(n, weights[n]) for n in SMALL_NAMES], [as_rows(n, m_in[n]) for n in SMALL_NAMES],
        [as_rows(n, v_in[n]) for n in SMALL_NAMES])
    loss = loss.reshape(())
    n_small = len(SMALL_NAMES)
    for j, kind in enumerate(("grad", "delta", "new_m", "new_v")):
        for i, name in enumerate(SMALL_NAMES):
            out[kind][name] = from_rows(name, updates[j * n_small + i])

    _issued.clear()
    order = ("w_in", "w_out", "g_attn_norm", "g_q", "g_k", "attn_sinks", "rel_bias", "w_pool", "pool_scale",
             "g_ffn_norm", "w_gate", "w_up", "w_down", "g_ple_norm", "w_ple_gate", "w_ple_proj")
    return (loss, grad_x[None], *[out["grad"][n] for n in order], *[out["delta"][n] for n in order],
            *[out["new_m"][n] for n in order], *[out["new_v"][n] for n in order])
```

```python
import math

import jax
import jax.numpy as jnp
import numpy as np
from jax import lax
from jax.experimental import pallas as pl
from jax.experimental.pallas import tpu as pltpu
from jax.experimental.pallas import tpu_sc as plsc

F32 = jnp.float32
BF16 = jnp.bfloat16
MESH = pl.DeviceIdType.MESH

D_MODEL = 1024
HEAD_DIM = 64
ATTN_WIDTH = 512
KV_WIDTH = 128
POOL_WIDTH = 512
POOL_SIZES = (2, 4, 8, 16)
POOL_GROUP = 128
POOL_HALO = 16
IN_WIDTH = 1280
D_FF = 2816
PLE_DIM = 256
BLOCK = 128
N_BUCKETS = 32
MAX_DISTANCE = 128
EPS = 1e-6
N_DEV = 8
N_CHIPS = 4

ADAM_LR = 0.001
ADAM_B1 = 0.9
ADAM_B2 = 0.999
ADAM_EPS = 1e-08
ADAM_WD = 0.01
ADAM_STEP = 10

TOKEN_TILE = 512
FFN_BWD_TILE = 256
FF_CHUNK = 256
CHIP_SUM_ONE_STEP_BYTES = 9 * 2 ** 20
FFN_W_SLAB = 256
ATTN_STEP_BLOCKS = 4
HEADS_A = (0, 2, 5, 7)
HEADS_B = (1, 3, 4, 6)
SMALL_LANES = 128


def _nn(a, b):
    return jnp.dot(a, b, preferred_element_type=F32)


def _nt(a, b):
    return lax.dot_general(a, b, (((1,), (1,)), ((), ())), preferred_element_type=F32)


def _tn(a, b):
    return lax.dot_general(a, b, (((0,), (0,)), ((), ())), preferred_element_type=F32)


def _resident(shape):
    nd = len(shape)
    return pl.BlockSpec(shape, lambda i, _nd=nd: (0,) * _nd, pipeline_mode=pl.Buffered(1))


def _rows(tile, width):
    return pl.BlockSpec((tile, width), lambda i: (i, 0))


def _acc(shape):
    nd = len(shape)
    return pl.BlockSpec(shape, lambda i, _nd=nd: (0,) * _nd)


def _head_mean_matrix(width):
    idx = np.arange(width) // HEAD_DIM
    return jnp.asarray((idx[:, None] == idx[None, :]).astype(np.float32) / HEAD_DIM, dtype=BF16)


def _seg_mean(v, bmat):
    hi = v.astype(BF16)
    lo = (v - hi.astype(F32)).astype(BF16)
    return _nn(hi, bmat) + _nn(lo, bmat)


def _rms(x):
    return lax.rsqrt(jnp.mean(x * x, axis=-1, keepdims=True) + EPS)


def _rms_bwd(d_y, x, r, g):
    gy = d_y * g
    d_x = r * gy - x * (r * r * r) * jnp.mean(gy * x, axis=-1, keepdims=True)
    d_g = jnp.sum(d_y * (x * r), axis=0, keepdims=True)
    return d_x, d_g


def _lane_lo(shape):
    return lax.broadcasted_iota(jnp.int32, shape, 1) < HEAD_DIM


class _Rider:
    def __init__(self, inputs, out_shapes, sems, begin, end, middle=None):
        self.inputs, self.out_shapes, self.sems = list(inputs), list(out_shapes), list(sems)
        self.begin, self.middle, self.end = begin, middle, end


_issued = []


def _after_last(args, in_specs):
    extra = list(_issued)
    return list(args) + extra, list(in_specs) + [pl.BlockSpec(memory_space=pl.ANY)] * len(extra), len(extra)


def _mark_issued(out):
    _issued[:] = [out]


def _complete_before_next(arrays):
    _issued.extend(arrays)


def _call(body, args, *, name, grid, in_specs, out_specs, out_shape, scratch_shapes=()):
    n_args = len(args)
    args, in_specs, _ = _after_last(args, in_specs)

    def ordered(*refs):
        body(*refs[:n_args], *refs[len(args):])

    outs = pl.pallas_call(ordered, name=name, grid=grid, in_specs=in_specs, out_specs=list(out_specs),
                          out_shape=list(out_shape), scratch_shapes=list(scratch_shapes))(*args)
    _mark_issued(outs[0])
    return list(outs)


def _in_proj(x, g_attn, w_in_t, gq_t, gk_t):
    s = x.shape[0]
    ts = min(TOKEN_TILE, s)

    def body(x_ref, g_ref, w_ref, gq_ref, gk_ref, bq_ref, bk_ref, zqk_ref, qn_ref, kn_ref, v_ref, u_ref):
        xf = x_ref[...]
        hn = ((xf * _rms(xf)) * g_ref[...]).astype(BF16)
        z = _nt(hn, w_ref[...])
        q = z[:, :ATTN_WIDTH]
        k = z[:, ATTN_WIDTH:ATTN_WIDTH + KV_WIDTH]
        zqk_ref[...] = z[:, :ATTN_WIDTH + KV_WIDTH]
        rq = lax.rsqrt(_seg_mean(q * q, bq_ref[...]) + EPS)
        qn_ref[...] = ((q * rq) * gq_ref[...]).astype(BF16)
        rk = lax.rsqrt(_seg_mean(k * k, bk_ref[...]) + EPS)
        kn_ref[...] = ((k * rk) * gk_ref[...]).astype(BF16)
        v_ref[...] = z[:, ATTN_WIDTH + KV_WIDTH:ATTN_WIDTH + 2 * KV_WIDTH].astype(BF16)
        u_ref[...] = z[:, ATTN_WIDTH + 2 * KV_WIDTH:]

    return _call(
        body,
        (x, g_attn, w_in_t, gq_t, gk_t, _head_mean_matrix(ATTN_WIDTH), _head_mean_matrix(KV_WIDTH)),
        name="in_proj",
        grid=(s // ts,),
        in_specs=[
            _rows(ts, D_MODEL),
            _resident((1, D_MODEL)),
            _resident((IN_WIDTH, D_MODEL)),
            _resident((1, ATTN_WIDTH)),
            _resident((1, KV_WIDTH)),
            _resident((ATTN_WIDTH, ATTN_WIDTH)),
            _resident((KV_WIDTH, KV_WIDTH)),
        ],
        out_specs=[
            _rows(ts, ATTN_WIDTH + KV_WIDTH),
            _rows(ts, ATTN_WIDTH),
            _rows(ts, KV_WIDTH),
            _rows(ts, KV_WIDTH),
            _rows(ts, POOL_WIDTH),
        ],
        out_shape=[
            jax.ShapeDtypeStruct((s, ATTN_WIDTH + KV_WIDTH), F32),
            jax.ShapeDtypeStruct((s, ATTN_WIDTH), BF16),
            jax.ShapeDtypeStruct((s, KV_WIDTH), BF16),
            jax.ShapeDtypeStruct((s, KV_WIDTH), BF16),
            jax.ShapeDtypeStruct((s, POOL_WIDTH), F32),
        ],
    )


def _bucket_ranges():
    n = np.arange(MAX_DISTANCE)
    max_exact = N_BUCKETS // 2
    nf = np.maximum(n, 1).astype(np.float64)
    large = max_exact + (np.log(nf / max_exact) / math.log(MAX_DISTANCE / max_exact) * (N_BUCKETS - max_exact)).astype(np.int64)
    bucket = np.where(n < max_exact, n, np.minimum(large, N_BUCKETS - 1))
    out = []
    for b in range(N_BUCKETS):
        idx = np.nonzero(bucket == b)[0]
        out.append((int(idx.min()), int(idx.max()) + 1))
    return out


def _band_distance():
    i = lax.broadcasted_iota(jnp.int32, (BLOCK, 2 * BLOCK), 0)
    j = lax.broadcasted_iota(jnp.int32, (BLOCK, 2 * BLOCK), 1)
    return BLOCK + i - j


BIAS_TABLE_SHAPE = (2, 4 * BLOCK, 2 * BLOCK)


def _write_bias_table(rb_ref, tab_ref):
    d = _band_distance()
    for half, heads in enumerate((HEADS_A, HEADS_B)):
        for slot, h in enumerate(heads):
            t = jnp.full((BLOCK, 2 * BLOCK), -jnp.inf, F32)
            for b, (lo, hi) in enumerate(_bucket_ranges()):
                t = jnp.where((d >= lo) & (d < hi), rb_ref[h, b], t)
            tab_ref[half, slot * BLOCK:(slot + 1) * BLOCK, :] = t


def _bias_table_bwd(dl_acc):
    ranges = _bucket_ranges()
    n_heads = len(HEADS_A) + len(HEADS_B)

    def body(dl_ref, out_ref):
        d = _band_distance()
        row = lax.broadcasted_iota(jnp.int32, (n_heads, SMALL_LANES), 0)
        lane = lax.broadcasted_iota(jnp.int32, (n_heads, SMALL_LANES), 1)
        out = jnp.zeros((n_heads, SMALL_LANES), F32)
        for b, (lo, hi) in enumerate(ranges):
            in_bucket = (d >= lo) & (d < hi)
            for half, heads in enumerate((HEADS_A, HEADS_B)):
                for slot, h in enumerate(heads):
                    g = dl_ref[half, slot * BLOCK:(slot + 1) * BLOCK, :]
                    part = jnp.sum(jnp.where(in_bucket, g, 0.0), axis=0, keepdims=True)
                    tot = jnp.sum(part, axis=1, keepdims=True)
                    out = jnp.where((row == h) & (lane == b), tot, out)
        out_ref[...] = out

    return _call(
        body,
        (dl_acc,),
        name="bias_table_bwd",
        grid=(1,),
        in_specs=[_acc((2, 4 * BLOCK, 2 * BLOCK))],
        out_specs=[_acc((n_heads, SMALL_LANES))],
        out_shape=[jax.ShapeDtypeStruct((n_heads, SMALL_LANES), F32)],
    )


def _stack_heads(pairs, lo_mask):
    zero = jnp.zeros_like(pairs[0])
    lo = [jnp.where(lo_mask, t, zero) for t in pairs]
    hi = [jnp.where(lo_mask, zero, t) for t in pairs]
    return (jnp.concatenate([lo[0], lo[1], hi[2], hi[3]], axis=0),
            jnp.concatenate([hi[0], hi[1], lo[2], lo[3]], axis=0))


def _unstack_heads(out_a, out_b, lo_mask):
    t = lambda x, r: x[r * BLOCK:(r + 1) * BLOCK, :]
    return [
        jnp.where(lo_mask, t(out_a, 0), t(out_b, 0)),
        jnp.where(lo_mask, t(out_a, 1), t(out_b, 1)),
        jnp.where(lo_mask, t(out_b, 2), t(out_a, 2)),
        jnp.where(lo_mask, t(out_b, 3), t(out_a, 3)),
    ]


def _sink_column(sink_ref, heads):
    row = lax.broadcasted_iota(jnp.int32, (4 * BLOCK, 1), 0)
    col = jnp.full((4 * BLOCK, 1), sink_ref[0, heads[3]], F32)
    for slot in (2, 1, 0):
        col = jnp.where(row < (slot + 1) * BLOCK, sink_ref[0, heads[slot]], col)
    return col


def _band_scores(q_stack, keys, tab, first_block):
    s = _nt(q_stack, keys) * (HEAD_DIM ** -0.5) + tab
    if first_block is not None:
        col = lax.broadcasted_iota(jnp.int32, s.shape, 1)
        s = jnp.where(jnp.logical_and(first_block, col < BLOCK), -jnp.inf, s)
    return s


def _softmax_with_sink(s, sink):
    m = jnp.maximum(jnp.max(s, axis=-1, keepdims=True), sink)
    e = jnp.exp(s - m)
    e_sink = jnp.exp(sink - m)
    den = jnp.sum(e, axis=-1, keepdims=True) + e_sink
    return e / den, e_sink / den


def _band_probs(q_stack, keys, tab, sink, first_block):
    return _softmax_with_sink(_band_scores(q_stack, keys, tab, first_block), sink)


def _attn_specs(n_groups):
    group = lambda n: (jnp.minimum(n, n_groups - 1), 0)
    prev = lambda n: (jnp.maximum(jnp.minimum(n, n_groups - 1) * ATTN_STEP_BLOCKS - 1, 0), 0)
    return group, prev


def _band(prev_ref, group_ref, b):
    rows = lambda i: group_ref[i * BLOCK:(i + 1) * BLOCK, :]
    band = jnp.concatenate([prev_ref[...] if b == 0 else rows(b - 1), rows(b)], axis=0)
    return band, pltpu.roll(band, HEAD_DIM, 1)


def _attn_fwd(qn, kn, v, tab, sinks):
    s = qn.shape[0]
    n_groups = s // (ATTN_STEP_BLOCKS * BLOCK)
    group, prev = _attn_specs(n_groups)
    rows = ATTN_STEP_BLOCKS * BLOCK

    def body(sink_ref, q_ref, kc_ref, kp_ref, vc_ref, vp_ref, tab_ref, o_ref):
        first = pl.program_id(0) == 0
        lo_mask = _lane_lo((BLOCK, BLOCK))
        for b in range(ATTN_STEP_BLOCKS):
            at = slice(b * BLOCK, (b + 1) * BLOCK)
            kk, kk_sw = _band(kp_ref, kc_ref, b)
            vv, vv_sw = _band(vp_ref, vc_ref, b)
            q_a, q_b = _stack_heads([q_ref[at, p * BLOCK:(p + 1) * BLOCK] for p in range(4)], lo_mask)
            no_prev = first if b == 0 else None
            p_a, _ = _band_probs(q_a, kk, tab_ref[0], _sink_column(sink_ref, HEADS_A), no_prev)
            p_b, _ = _band_probs(q_b, kk_sw, tab_ref[1], _sink_column(sink_ref, HEADS_B), no_prev)
            out = _unstack_heads(_nn(p_a.astype(BF16), vv), _nn(p_b.astype(BF16), vv_sw), lo_mask)
            for p in range(4):
                o_ref[at, p * BLOCK:(p + 1) * BLOCK] = out[p].astype(BF16)

    return _call(
        body,
        (sinks, qn, kn, kn, v, v, tab),
        name="attn_fwd",
        grid=(n_groups,),
        in_specs=[
            pl.BlockSpec(memory_space=pltpu.SMEM),
            pl.BlockSpec((rows, ATTN_WIDTH), group),
            pl.BlockSpec((rows, KV_WIDTH), group),
            pl.BlockSpec((BLOCK, KV_WIDTH), prev),
            pl.BlockSpec((rows, KV_WIDTH), group),
            pl.BlockSpec((BLOCK, KV_WIDTH), prev),
            _resident((2, 4 * BLOCK, 2 * BLOCK)),
        ],
        out_specs=[pl.BlockSpec((rows, ATTN_WIDTH), group)],
        out_shape=[jax.ShapeDtypeStruct((s, ATTN_WIDTH), BF16)],
    )


def _pooled(u_tile, u_halo, tile_index, tile_rows):
    halo = jnp.where(tile_index > 0, u_halo, 0.0)
    ext = jnp.concatenate([halo, u_tile], axis=0)
    sums = []
    acc = ext
    for shift in (1, 2, 4, 8):
        acc = acc + pltpu.roll(acc, shift, 0)
        sums.append(acc)
    t = tile_index * tile_rows + lax.broadcasted_iota(jnp.int32, (tile_rows, 1), 0)
    out = []
    for g, w in enumerate(POOL_SIZES):
        lanes = slice(g * POOL_GROUP, (g + 1) * POOL_GROUP)
        cnt = jnp.minimum(t + 1, w).astype(F32)
        out.append(sums[g][POOL_HALO:, lanes] / cnt - u_tile[:, lanes])
    return out


def _halo_before(tile):
    return lambda i: (jnp.maximum(i * (tile // POOL_HALO) - 1, 0), 0)


def _mix_out(u, a, x, w_out, w_pool, pool_scale, g_ffn):
    s = x.shape[0]
    ts = min(TOKEN_TILE, s)

    def body(u_ref, uh_ref, a_ref, x_ref, wo_ref, wp_ref, sc_ref, g_ref, h1_ref, hn_ref, m_ref):
        i = pl.program_id(0)
        pooled = _pooled(u_ref[...], uh_ref[...], i, ts)
        for g in range(len(POOL_SIZES)):
            lanes = slice(g * POOL_GROUP, (g + 1) * POOL_GROUP)
            y = _nn(pooled[g].astype(BF16), wp_ref[g].astype(BF16))
            m_ref[:, lanes] = (y * sc_ref[:, lanes]).astype(BF16)
        h1 = x_ref[...] + _nn(a_ref[...], wo_ref[:ATTN_WIDTH, :]) + _nn(m_ref[...], wo_ref[ATTN_WIDTH:, :])
        h1_ref[...] = h1
        hn_ref[...] = ((h1 * _rms(h1)) * g_ref[...]).astype(BF16)

    return _call(
        body,
        (u, u, a, x, w_out, w_pool, pool_scale, g_ffn),
        name="mix_out",
        grid=(s // ts,),
        in_specs=[
            _rows(ts, POOL_WIDTH),
            pl.BlockSpec((POOL_HALO, POOL_WIDTH), _halo_before(ts)),
            _rows(ts, ATTN_WIDTH),
            _rows(ts, D_MODEL),
            _resident((D_MODEL, D_MODEL)),
            _resident((len(POOL_SIZES), POOL_GROUP, POOL_GROUP)),
            _resident((1, POOL_WIDTH)),
            _resident((1, D_MODEL)),
        ],
        out_specs=[_rows(ts, D_MODEL), _rows(ts, D_MODEL), _rows(ts, POOL_WIDTH)],
        out_shape=[
            jax.ShapeDtypeStruct((s, D_MODEL), F32),
            jax.ShapeDtypeStruct((s, D_MODEL), BF16),
            jax.ShapeDtypeStruct((s, POOL_WIDTH), BF16),
        ],
    )


def _ffn_up(hn2, wg_t, wu_t):
    s = hn2.shape[0]
    ts = min(TOKEN_TILE, s)

    def body(hn_ref, wg_ref, wu_ref, gt_ref, up_ref):
        hn = hn_ref[...]
        for c in range(D_FF // FF_CHUNK):
            cols = slice(c * FF_CHUNK, (c + 1) * FF_CHUNK)
            gt_ref[:, cols] = _nt(hn, wg_ref[cols, :]).astype(BF16)
            up_ref[:, cols] = _nt(hn, wu_ref[cols, :]).astype(BF16)

    return _call(
        body,
        (hn2, wg_t, wu_t),
        name="ffn_up",
        grid=(s // ts,),
        in_specs=[_rows(ts, D_MODEL), _resident((D_FF, D_MODEL)), _resident((D_FF, D_MODEL))],
        out_specs=[_rows(ts, D_FF), _rows(ts, D_FF)],
        out_shape=[jax.ShapeDtypeStruct((s, D_FF), BF16), jax.ShapeDtypeStruct((s, D_FF), BF16)],
    )


def _silu_mul(gt, up):
    return (gt * jax.nn.sigmoid(gt)) * up


def _ffn_down_ple(gt, up, h1, w_down, p, target, g_ple, w_pg, w_pp):
    s = h1.shape[0]
    ts = min(TOKEN_TILE, s)
    blk = D_MODEL // N_DEV

    def body(gt_ref, up_ref, h1_ref, wd_ref, p_ref, t_ref, g_ref, wpg_ref, wpp_ref,
             loss_ref, dh_ref, dwpg_ref, dwpp_ref, dg_ref):
        @pl.when(pl.program_id(0) == 0)
        def _():
            loss_ref[...] = jnp.zeros_like(loss_ref)
            dwpg_ref[...] = jnp.zeros_like(dwpg_ref)
            dwpp_ref[...] = jnp.zeros_like(dwpp_ref)
            dg_ref[...] = jnp.zeros_like(dg_ref)

        h2v = h1_ref[...]
        for c in range(D_FF // FF_CHUNK):
            cols = slice(c * FF_CHUNK, (c + 1) * FF_CHUNK)
            act = _silu_mul(gt_ref[:, cols].astype(F32), up_ref[:, cols].astype(F32)).astype(BF16)
            h2v = _nn(act, wd_ref[cols, :]) + h2v
        r = _rms(h2v)
        hn = ((h2v * r) * g_ref[...]).astype(BF16)
        gate = jax.nn.sigmoid(_nn(hn, wpg_ref[...]))
        pb = p_ref[...].astype(BF16)
        pp = _nn(pb, jnp.concatenate([wpp_ref[j] for j in range(N_DEV)], axis=1))
        diff = (h2v + gate * pp) - t_ref[...]
        loss_ref[...] += jnp.sum(jnp.sum(diff * diff, axis=0, keepdims=True), axis=1, keepdims=True) * (0.5 / D_MODEL)
        dy = diff * (1.0 / D_MODEL)
        d_pp = (dy * gate).astype(BF16)
        d_pre = ((dy * pp) * (gate * (1.0 - gate))).astype(BF16)
        d_x, d_g = _rms_bwd(_nt(d_pre, wpg_ref[...]), h2v, r, g_ref[...])
        dg_ref[...] += d_g
        dh_ref[...] = dy + d_x
        d_wpp = _tn(pb, d_pp)
        for j in range(N_DEV):
            dwpp_ref[j] += d_wpp[:, j * blk:(j + 1) * blk]
        dwpg_ref[...] += _tn(hn, d_pre)

    return _call(
        body,
        (gt, up, h1, w_down, p, target, g_ple, w_pg, w_pp),
        name="ffn_down_ple",
        grid=(s // ts,),
        in_specs=[
            _rows(ts, D_FF),
            _rows(ts, D_FF),
            _rows(ts, D_MODEL),
            _resident((D_FF, D_MODEL)),
            _rows(ts, PLE_DIM),
            _rows(ts, D_MODEL),
            _resident((1, D_MODEL)),
            _resident((D_MODEL, D_MODEL)),
            _resident((N_DEV, PLE_DIM, blk)),
        ],
        out_specs=[
            _acc((1, SMALL_LANES)),
            _rows(ts, D_MODEL),
            _acc((D_MODEL, D_MODEL)),
            _acc((N_DEV, PLE_DIM, blk)),
            _acc((1, D_MODEL)),
        ],
        out_shape=[
            jax.ShapeDtypeStruct((1, SMALL_LANES), F32),
            jax.ShapeDtypeStruct((s, D_MODEL), F32),
            jax.ShapeDtypeStruct((D_MODEL, D_MODEL), F32),
            jax.ShapeDtypeStruct((N_DEV, PLE_DIM, blk), F32),
            jax.ShapeDtypeStruct((1, D_MODEL), F32),
        ],
    )


def _ffn_bwd_act(dh2, h1, gt, up, g_ffn, wg_t, wu_t, w_down):
    s = h1.shape[0]
    ts = min(FFN_BWD_TILE, s)

    def body(dh_ref, h1_ref, gt_ref, up_ref, g_ref, wg_ref, wu_ref, wd_ref,
             dgt_ref, dup_ref, dh1_ref, dh1b_ref, dg_ref, dwd_ref, act_ref):
        @pl.when(pl.program_id(0) == 0)
        def _():
            dg_ref[...] = jnp.zeros_like(dg_ref)
            dwd_ref[...] = jnp.zeros_like(dwd_ref)

        dhb = dh_ref[...].astype(BF16)
        d_hn = jnp.zeros((ts, D_MODEL), F32)
        for c in range(D_FF // FF_CHUNK):
            cols = slice(c * FF_CHUNK, (c + 1) * FF_CHUNK)
            d_act = _nt(dhb, wd_ref[cols, :])
            gtv = gt_ref[:, cols].astype(F32)
            upv = up_ref[:, cols].astype(F32)
            sg = jax.nn.sigmoid(gtv)
            silu = gtv * sg
            act_ref[:, cols] = (silu * upv).astype(BF16)
            d_up = (d_act * silu).astype(BF16)
            d_gt = ((d_act * upv) * (sg * (1.0 + gtv * (1.0 - sg)))).astype(BF16)
            dup_ref[:, cols] = d_up
            dgt_ref[:, cols] = d_gt
            d_hn = (_nn(d_gt, wg_ref[cols, :]) + _nn(d_up, wu_ref[cols, :])) + d_hn
        dwd_ref[...] += _tn(act_ref[...], dhb)
        h1v = h1_ref[...]
        d_x, d_g = _rms_bwd(d_hn, h1v, _rms(h1v), g_ref[...])
        dg_ref[...] += d_g
        dh1 = dh_ref[...] + d_x
        dh1_ref[...] = dh1
        dh1b_ref[...] = dh1.astype(BF16)

    return _call(
        body,
        (dh2, h1, gt, up, g_ffn, wg_t, wu_t, w_down),
        name="ffn_bwd_act",
        grid=(s // ts,),
        in_specs=[
            _rows(ts, D_MODEL),
            _rows(ts, D_MODEL),
            _rows(ts, D_FF),
            _rows(ts, D_FF),
            _resident((1, D_MODEL)),
            _resident((D_FF, D_MODEL)),
            _resident((D_FF, D_MODEL)),
            _resident((D_FF, D_MODEL)),
        ],
        out_specs=[
            _rows(ts, D_FF), _rows(ts, D_FF),
            _rows(ts, D_MODEL), _rows(ts, D_MODEL), _acc((1, D_MODEL)), _acc((D_FF, D_MODEL)),
        ],
        out_shape=[
            jax.ShapeDtypeStruct((s, D_FF), BF16),
            jax.ShapeDtypeStruct((s, D_FF), BF16),
            jax.ShapeDtypeStruct((s, D_MODEL), F32),
            jax.ShapeDtypeStruct((s, D_MODEL), BF16),
            jax.ShapeDtypeStruct((1, D_MODEL), F32),
            jax.ShapeDtypeStruct((D_FF, D_MODEL), F32),
        ],
        scratch_shapes=[pltpu.VMEM((ts, D_FF), BF16)],
    )


def _ffn_bwd_w(dgt, dup, hn2):
    s = hn2.shape[0]
    slab = pl.BlockSpec((s, FFN_W_SLAB), lambda i: (0, i))

    def body(dgt_ref, dup_ref, hn_ref, dwg_ref, dwu_ref):
        hn = hn_ref[...]
        dwg_ref[...] = _tn(dgt_ref[...], hn)
        dwu_ref[...] = _tn(dup_ref[...], hn)

    return _call(
        body,
        (dgt, dup, hn2),
        name="ffn_bwd_w",
        grid=(D_FF // FFN_W_SLAB,),
        in_specs=[slab, slab, _resident((s, D_MODEL))],
        out_specs=[_rows(FFN_W_SLAB, D_MODEL)] * 2,
        out_shape=[jax.ShapeDtypeStruct((D_FF, D_MODEL), F32)] * 2,
    )


def _mix_bwd(dh1b, u, a, m, w_out, w_pool, pool_scale):
    s = u.shape[0]
    ts = min(TOKEN_TILE, s)
    nt = s // ts
    halo_after = lambda i: (jnp.minimum((i + 1) * (ts // POOL_HALO), s // POOL_HALO - 1), 0)
    n_groups = len(POOL_SIZES)

    def body(dh_ref, dhn_ref, u_ref, uh_ref, a_ref, m_ref, wo_ref, wp_ref, sc_ref,
             da_ref, du_ref, dwp_ref, dsc_ref, dwo_ref):
        i = pl.program_id(0)

        @pl.when(i == 0)
        def _():
            dwp_ref[...] = jnp.zeros_like(dwp_ref)
            dsc_ref[...] = jnp.zeros_like(dsc_ref)
            dwo_ref[...] = jnp.zeros_like(dwo_ref)

        dh = dh_ref[...]
        dwo_ref[:ATTN_WIDTH, :] += _tn(a_ref[...], dh)
        dwo_ref[ATTN_WIDTH:, :] += _tn(m_ref[...], dh)
        da_ref[...] = _nt(dh, wo_ref[:ATTN_WIDTH, :])
        dh_next = jnp.where(i < nt - 1, dhn_ref[...], jnp.zeros_like(dhn_ref))
        dm_ext = _nt(jnp.concatenate([dh, dh_next], axis=0), wo_ref[ATTN_WIDTH:, :])
        pooled = _pooled(u_ref[...], uh_ref[...], i, ts)
        t_ext = i * ts + lax.broadcasted_iota(jnp.int32, (ts + POOL_HALO, 1), 0)
        for g, w in enumerate(POOL_SIZES):
            lanes = slice(g * POOL_GROUP, (g + 1) * POOL_GROUP)
            wp = wp_ref[g].astype(BF16)
            pg = pooled[g].astype(BF16)
            dm_g = dm_ext[:, lanes]
            dsc_ref[:, lanes] += jnp.sum(dm_g[:ts, :] * _nn(pg, wp), axis=0, keepdims=True)
            dy = (dm_g * sc_ref[:, lanes]).astype(BF16)
            dwp_ref[g] += _tn(pg, dy[:ts, :])
            d_pool = _nt(dy, wp)
            acc = d_pool / jnp.minimum(t_ext + 1, w).astype(F32)
            shift = 1
            while shift < w:
                acc = acc + pltpu.roll(acc, ts + POOL_HALO - shift, 0)
                shift *= 2
            du_ref[:, lanes] = (acc[:ts, :] - d_pool[:ts, :]).astype(BF16)

    return _call(
        body,
        (dh1b, dh1b, u, u, a, m, w_out, w_pool, pool_scale),
        name="mix_bwd",
        grid=(nt,),
        in_specs=[
            _rows(ts, D_MODEL),
            pl.BlockSpec((POOL_HALO, D_MODEL), halo_after),
            _rows(ts, POOL_WIDTH),
            pl.BlockSpec((POOL_HALO, POOL_WIDTH), _halo_before(ts)),
            _rows(ts, ATTN_WIDTH),
            _rows(ts, POOL_WIDTH),
            _resident((D_MODEL, D_MODEL)),
            _resident((n_groups, POOL_GROUP, POOL_GROUP)),
            _resident((1, POOL_WIDTH)),
        ],
        out_specs=[
            _rows(ts, ATTN_WIDTH),
            _rows(ts, POOL_WIDTH),
            _acc((n_groups, POOL_GROUP, POOL_GROUP)),
            _acc((1, POOL_WIDTH)),
            _acc((D_MODEL, D_MODEL)),
        ],
        out_shape=[
            jax.ShapeDtypeStruct((s, ATTN_WIDTH), F32),
            jax.ShapeDtypeStruct((s, POOL_WIDTH), BF16),
            jax.ShapeDtypeStruct((n_groups, POOL_GROUP, POOL_GROUP), F32),
            jax.ShapeDtypeStruct((1, POOL_WIDTH), F32),
            jax.ShapeDtypeStruct((D_MODEL, D_MODEL), F32),
        ],
    )


def _attn_bwd(qn, kn, v, a, da, tab, sinks):
    s = qn.shape[0]
    qb = ATTN_STEP_BLOCKS
    rows = qb * BLOCK
    n_groups = s // rows
    group, prev = _attn_specs(n_groups)
    done = lambda n: (jnp.maximum(n - 1, 0), 0)

    def body(sink_ref, q_ref, kc_ref, kp_ref, vc_ref, vp_ref, o_ref, do_ref, tab_ref,
             dq_ref, dk_ref, dv_ref, dl_ref, ds_ref, k_carry, v_carry, sink_acc):
        n = pl.program_id(0)

        @pl.when(n == 0)
        def _():
            dl_ref[...] = jnp.zeros_like(dl_ref)
            k_carry[...] = jnp.zeros_like(k_carry)
            v_carry[...] = jnp.zeros_like(v_carry)
            sink_acc[...] = jnp.zeros_like(sink_acc)

        @pl.when(n < n_groups)
        def _():
            first = n == 0
            lo_mask = _lane_lo((BLOCK, BLOCK))
            chains = [(b, half) for b in range(qb) for half in range(2)]
            tile = lambda ref, b, p: ref[b * BLOCK:(b + 1) * BLOCK, p * BLOCK:(p + 1) * BLOCK]
            keys = [_band(kp_ref, kc_ref, b) for b in range(qb)]
            vals = [_band(vp_ref, vc_ref, b) for b in range(qb)]
            q_st = [_stack_heads([tile(q_ref, b, p) for p in range(4)], lo_mask) for b in range(qb)]
            do_st = [_stack_heads([tile(do_ref, b, p) for p in range(4)], lo_mask) for b in range(qb)]
            o_st = [_stack_heads([tile(o_ref, b, p).astype(F32) for p in range(4)], lo_mask) for b in range(qb)]
            sink_col = [_sink_column(sink_ref, heads) for heads in (HEADS_A, HEADS_B)]
            scores = {(b, h): _band_scores(q_st[b][h], keys[b][h], tab_ref[h], first if b == 0 else None)
                      for b, h in chains}
            dob = {(b, h): do_st[b][h].astype(BF16) for b, h in chains}
            d_probs = {(b, h): _nt(dob[b, h], vals[b][h]) for b, h in chains}
            delta = {(b, h): jnp.sum(do_st[b][h] * o_st[b][h], axis=-1, keepdims=True) for b, h in chains}
            soft = {(b, h): _softmax_with_sink(scores[b, h], sink_col[h]) for b, h in chains}
            dl = {(b, h): soft[b, h][0] * (d_probs[b, h] - delta[b, h]) for b, h in chains}
            for b, h in chains:
                dl_ref[h] += dl[b, h]
                sink_acc[h] += soft[b, h][1] * delta[b, h]
            dsb = {(b, h): (dl[b, h] * (HEAD_DIM ** -0.5)).astype(BF16) for b, h in chains}
            dq_st = {(b, h): _nn(dsb[b, h], keys[b][h]) for b, h in chains}
            dk_parts = {(b, h): _tn(dsb[b, h], q_st[b][h]) for b, h in chains}
            dv_parts = {(b, h): _tn(soft[b, h][0].astype(BF16), dob[b, h]) for b, h in chains}
            for b in range(qb):
                dq = _unstack_heads(dq_st[b, 0], dq_st[b, 1], lo_mask)
                for p in range(4):
                    dq_ref[b * BLOCK:(b + 1) * BLOCK, p * BLOCK:(p + 1) * BLOCK] = dq[p]
            dks = [dk_parts[b, 0] + pltpu.roll(dk_parts[b, 1], HEAD_DIM, 1) for b in range(qb)]
            dvs = [dv_parts[b, 0] + pltpu.roll(dv_parts[b, 1], HEAD_DIM, 1) for b in range(qb)]
            last = slice((qb - 1) * BLOCK, qb * BLOCK)
            for parts, out_ref, carry in ((dks, dk_ref, k_carry), (dvs, dv_ref, v_carry)):
                out_ref[...] = carry[...]
                out_ref[last, :] += parts[0][:BLOCK, :]
                for b in range(qb):
                    own = parts[b][BLOCK:, :]
                    carry[b * BLOCK:(b + 1) * BLOCK, :] = own + parts[b + 1][:BLOCK, :] if b + 1 < qb else own

        @pl.when(n == n_groups)
        def _():
            dk_ref[...] = k_carry[...]
            dv_ref[...] = v_carry[...]
            for half, heads in enumerate((HEADS_A, HEADS_B)):
                for slot, h in enumerate(heads):
                    tot = jnp.sum(sink_acc[half, slot * BLOCK:(slot + 1) * BLOCK, :], axis=0, keepdims=True)
                    ds_ref[h:h + 1, :] = jnp.broadcast_to(-tot, (1, SMALL_LANES))

    return _call(
        body,
        (sinks, qn, kn, kn, v, v, a, da, tab),
        name="attn_bwd",
        grid=(n_groups + 1,),
        in_specs=[
            pl.BlockSpec(memory_space=pltpu.SMEM),
            pl.BlockSpec((rows, ATTN_WIDTH), group),
            pl.BlockSpec((rows, KV_WIDTH), group),
            pl.BlockSpec((BLOCK, KV_WIDTH), prev),
            pl.BlockSpec((rows, KV_WIDTH), group),
            pl.BlockSpec((BLOCK, KV_WIDTH), prev),
            pl.BlockSpec((rows, ATTN_WIDTH), group),
            pl.BlockSpec((rows, ATTN_WIDTH), group),
            _resident((2, 4 * BLOCK, 2 * BLOCK)),
        ],
        out_specs=[
            pl.BlockSpec((rows, ATTN_WIDTH), group),
            pl.BlockSpec((rows, KV_WIDTH), done),
            pl.BlockSpec((rows, KV_WIDTH), done),
            _acc((2, 4 * BLOCK, 2 * BLOCK)),
            _acc((N_DEV, SMALL_LANES)),
        ],
        out_shape=[
            jax.ShapeDtypeStruct((s, ATTN_WIDTH), F32),
            jax.ShapeDtypeStruct((s, KV_WIDTH), F32),
            jax.ShapeDtypeStruct((s, KV_WIDTH), F32),
            jax.ShapeDtypeStruct((2, 4 * BLOCK, 2 * BLOCK), F32),
            jax.ShapeDtypeStruct((N_DEV, SMALL_LANES), F32),
        ],
        scratch_shapes=[
            pltpu.VMEM((rows, KV_WIDTH), F32),
            pltpu.VMEM((rows, KV_WIDTH), F32),
            pltpu.VMEM((2, 4 * BLOCK, 1), F32),
        ],
    )


def _fold_heads(acc):
    t = acc + pltpu.roll(acc, HEAD_DIM, 1)
    out = t[:, :SMALL_LANES]
    for g in range(1, acc.shape[1] // SMALL_LANES):
        out = out + t[:, g * SMALL_LANES:(g + 1) * SMALL_LANES]
    return out


def _in_proj_bwd(dqn, dkn, dv, du, zqk, x, dh1, g_attn, gq_t, gk_t, w_in_t):
    s = x.shape[0]
    ts = min(TOKEN_TILE, s)
    nt = s // ts

    def head_norm_bwd(d_n, raw, g_t, bmat):
        r = lax.rsqrt(_seg_mean(raw * raw, bmat) + EPS)
        gy = d_n * g_t
        d_raw = r * gy - raw * (r * r * r) * _seg_mean(gy * raw, bmat)
        return d_raw, jnp.sum(d_n * (raw * r), axis=0, keepdims=True)

    def body(dqn_ref, dkn_ref, dv_ref, du_ref, zqk_ref, x_ref, dh1_ref, g_ref, gq_ref, gk_ref, w_ref, bq_ref, bk_ref,
             gx_ref, dw_ref, dg_ref, dgq_ref, dgk_ref, dz_ref, gq_acc, gk_acc):
        i = pl.program_id(0)

        @pl.when(i == 0)
        def _():
            dw_ref[...] = jnp.zeros_like(dw_ref)
            dg_ref[...] = jnp.zeros_like(dg_ref)
            gq_acc[...] = jnp.zeros_like(gq_acc)
            gk_acc[...] = jnp.zeros_like(gk_acc)

        d_q, d_gq = head_norm_bwd(dqn_ref[...], zqk_ref[:, :ATTN_WIDTH], gq_ref[...], bq_ref[...])
        d_k, d_gk = head_norm_bwd(dkn_ref[...], zqk_ref[:, ATTN_WIDTH:], gk_ref[...], bk_ref[...])
        gq_acc[...] += d_gq
        gk_acc[...] += d_gk
        dz_ref[:, :ATTN_WIDTH] = d_q.astype(BF16)
        dz_ref[:, ATTN_WIDTH:ATTN_WIDTH + KV_WIDTH] = d_k.astype(BF16)
        dz_ref[:, ATTN_WIDTH + KV_WIDTH:ATTN_WIDTH + 2 * KV_WIDTH] = dv_ref[...].astype(BF16)
        dz_ref[:, ATTN_WIDTH + 2 * KV_WIDTH:] = du_ref[...]
        dz = dz_ref[...]
        xf = x_ref[...]
        r = _rms(xf)
        hn = ((xf * r) * g_ref[...]).astype(BF16)
        d_x, d_g = _rms_bwd(_nn(dz, w_ref[...]), xf, r, g_ref[...])
        dg_ref[...] += d_g
        gx_ref[...] = dh1_ref[...] + d_x
        dw_ref[...] += _tn(dz, hn)

        @pl.when(i == nt - 1)
        def _():
            dgq_ref[...] = _fold_heads(gq_acc[...])
            dgk_ref[...] = _fold_heads(gk_acc[...])

    return _call(
        body,
        (dqn, dkn, dv, du, zqk, x, dh1, g_attn, gq_t, gk_t, w_in_t,
      _head_mean_matrix(ATTN_WIDTH), _head_mean_matrix(KV_WIDTH)),
        name="in_proj_bwd",
        grid=(nt,),
        in_specs=[
            _rows(ts, ATTN_WIDTH),
            _rows(ts, KV_WIDTH),
            _rows(ts, KV_WIDTH),
            _rows(ts, POOL_WIDTH),
            _rows(ts, ATTN_WIDTH + KV_WIDTH),
            _rows(ts, D_MODEL),
            _rows(ts, D_MODEL),
            _resident((1, D_MODEL)),
            _resident((1, ATTN_WIDTH)),
            _resident((1, KV_WIDTH)),
            _resident((IN_WIDTH, D_MODEL)),
            _resident((ATTN_WIDTH, ATTN_WIDTH)),
            _resident((KV_WIDTH, KV_WIDTH)),
        ],
        out_specs=[
            _rows(ts, D_MODEL),
            _acc((IN_WIDTH, D_MODEL)),
            _acc((1, D_MODEL)),
            _acc((1, SMALL_LANES)),
            _acc((1, SMALL_LANES)),
        ],
        out_shape=[
            jax.ShapeDtypeStruct((s, D_MODEL), F32),
            jax.ShapeDtypeStruct((IN_WIDTH, D_MODEL), F32),
            jax.ShapeDtypeStruct((1, D_MODEL), F32),
            jax.ShapeDtypeStruct((1, SMALL_LANES), F32),
            jax.ShapeDtypeStruct((1, SMALL_LANES), F32),
        ],
        scratch_shapes=[
            pltpu.VMEM((ts, IN_WIDTH), BF16),
            pltpu.VMEM((1, ATTN_WIDTH), F32),
            pltpu.VMEM((1, KV_WIDTH), F32),
        ],
    )


BIG_WEIGHTS = (
    ("w_in", True, IN_WIDTH // N_DEV, D_MODEL),
    ("w_out", False, D_MODEL // N_DEV, D_MODEL),
    ("w_gate", True, D_FF // N_DEV, D_MODEL),
    ("w_up", True, D_FF // N_DEV, D_MODEL),
    ("w_down", False, D_FF // N_DEV, D_MODEL),
    ("w_ple_gate", False, D_MODEL // N_DEV, D_MODEL),
    ("w_ple_proj", False, PLE_DIM, D_MODEL // N_DEV),
)
N_BIG = len(BIG_WEIGHTS)


def _place():
    x, y, c = lax.axis_index("x"), lax.axis_index("y"), lax.axis_index("c")
    chips = [(1 - x, y), (x, 1 - y), (1 - x, 1 - y)]
    return x, y, c, chips


class _Gather:
    def __init__(self, n):
        self.n = n
        self.sems = [pltpu.SemaphoreType.DMA((n, 7)), pltpu.SemaphoreType.DMA((n, 7)), pltpu.SemaphoreType.DMA((n,))]

    def _ctx(self, srcs, outs, sems):
        send_sems, recv_sems, local_sems = sems
        x, y, c, chips = _place()
        me, sibling = (x, y, c), (x, y, 1 - c)

        def block(k, owner):
            px, py, pc = owner
            return outs[k].at[4 * px + 2 * py + pc]

        def copy(k, idx, owner, to, mine=False):
            return pltpu.make_async_remote_copy(
                src_ref=srcs[k] if mine else block(k, owner), dst_ref=block(k, owner),
                send_sem=send_sems.at[k, idx], recv_sem=recv_sems.at[k, idx], device_id=to, device_id_type=MESH)

        def local(k):
            return pltpu.make_async_copy(srcs[k], block(k, me), local_sems.at[k])

        return c, chips, me, sibling, copy, local

    def begin(self, srcs, outs, sems):
        c, chips, me, sibling, copy, local = self._ctx(srcs, outs, sems)
        for k in range(self.n):
            local(k).start()
            copy(k, 0, me, sibling, mine=True).start()
            for j, chip in enumerate(chips):
                copy(k, 1 + j, me, (*chip, c), mine=True).start()

    def middle(self, srcs, outs, sems):
        c, chips, me, sibling, copy, local = self._ctx(srcs, outs, sems)
        for j, chip in enumerate(chips):
            for k in range(self.n):
                copy(k, 1 + j, (*chip, c), me).wait_recv()
                copy(k, 4 + j, (*chip, c), sibling).start()

    def end(self, srcs, outs, sems):
        c, chips, me, sibling, copy, local = self._ctx(srcs, outs, sems)
        for k in range(self.n):
            copy(k, 0, sibling, me).wait_recv()
            for j, chip in enumerate(chips):
                copy(k, 4 + j, (*chip, 1 - c), me).wait_recv()
        for k in range(self.n):
            copy(k, 0, me, sibling, mine=True).wait_send()
            for j, chip in enumerate(chips):
                copy(k, 1 + j, me, (*chip, c), mine=True).wait_send()
                copy(k, 4 + j, (*chip, c), sibling).wait_send()
            local(k).wait()


def _gather_rider(shards):
    g = _Gather(len(shards))
    shapes = [jax.ShapeDtypeStruct((N_DEV, *sh.shape), sh.dtype) for sh in shards]
    return _Rider(shards, shapes, g.sems, g.begin, g.end, g.middle)


def _cast_and_gather_first(shards, rel_bias_t):
    g = _Gather(1)
    any_spec = pl.BlockSpec(memory_space=pl.ANY)
    vmem = pl.BlockSpec(memory_space=pltpu.VMEM)

    def body(*refs):
        ins, rb_ref, outs = refs[:N_BIG], refs[N_BIG], refs[N_BIG + 1:2 * N_BIG + 1]
        gathered, tab_ref, sems = refs[2 * N_BIG + 1], refs[2 * N_BIG + 2], refs[2 * N_BIG + 3:]
        outs[0][...] = ins[0][...].astype(BF16)
        g.begin(outs[:1], [gathered], sems)
        for k in range(1, N_BIG):
            outs[k][...] = ins[k][...].astype(BF16)
        _write_bias_table(rb_ref, tab_ref)
        g.middle(outs[:1], [gathered], sems)
        g.end(outs[:1], [gathered], sems)

    res = pl.pallas_call(
        body,
        name="cast_and_gather_first",
        in_specs=[vmem] * N_BIG + [pl.BlockSpec(memory_space=pltpu.SMEM)],
        out_specs=[vmem] * N_BIG + [any_spec, vmem],
        out_shape=[jax.ShapeDtypeStruct((r, c), BF16) for _, _, r, c in BIG_WEIGHTS]
        + [jax.ShapeDtypeStruct((N_DEV, *BIG_WEIGHTS[0][2:]), BF16), jax.ShapeDtypeStruct(BIAS_TABLE_SHAPE, F32)],
        scratch_shapes=g.sems,
    )(*shards, rel_bias_t)
    return list(res[:N_BIG]), res[N_BIG], res[N_BIG + 1]


def _sibling_rider(grads):
    n = len(grads)

    def copies(gs, lands, sems):
        send_sems, recv_sems = sems
        x, y, c, _ = _place()
        return [
            pltpu.make_async_remote_copy(
                src_ref=gs[k].at[j, 1 - c], dst_ref=lands[k].at[j], send_sem=send_sems.at[k, j],
                recv_sem=recv_sems.at[k, j], device_id=(x, y, 1 - c), device_id_type=MESH)
            for k in range(n) for j in range(N_CHIPS)
        ]

    def begin(gs, lands, sems):
        for cp in copies(gs, lands, sems):
            cp.start()

    def end(gs, lands, sems):
        for cp in copies(gs, lands, sems):
            cp.wait()

    shapes = [jax.ShapeDtypeStruct((N_CHIPS, *g.shape[2:]), F32) for g in grads]
    return _Rider(grads, shapes, [pltpu.SemaphoreType.DMA((n, N_CHIPS)), pltpu.SemaphoreType.DMA((n, N_CHIPS))], begin, end)


def _chip_of_relation(j, place):
    x, y = place[0], place[1]
    return jnp.where(j == 0, 2 * (1 - x) + y, jnp.where(j == 1, 2 * x + 1 - y, 2 * (1 - x) + 1 - y))


def _chip_sum(ks, place, grads, from_sibling):
    shapes = [BIG_WEIGHTS[k][2:] for k in ks]
    n_rel = N_CHIPS - 1
    per_step = n_rel if 2 * n_rel * sum(4 * r * c for r, c in shapes) <= CHIP_SUM_ONE_STEP_BYTES else 1
    operands, specs = [], []
    for (r, c), g, l in zip(shapes, grads, from_sibling):
        for q in range(per_step):
            chip = lambda j, place, q=q: _chip_of_relation(j * per_step + q, place)
            operands += [g, l]
            specs += [pl.BlockSpec((1, 1, r, c), lambda j, place, chip=chip: (chip(j, place), place[2], 0, 0)),
                      pl.BlockSpec((1, r, c), lambda j, place, chip=chip: (chip(j, place), 0, 0))]
    args, in_specs, _ = _after_last(operands, specs)

    def body(place_ref, *refs):
        ins, outs = refs[:len(operands)], refs[len(args):]
        for i in range(len(ks)):
            for q in range(per_step):
                mine_ref, sib_ref = ins[2 * (i * per_step + q):2 * (i * per_step + q) + 2]
                outs[i][q] = (mine_ref[0, 0] + sib_ref[0]).astype(BF16)

    outs = pl.pallas_call(
        body,
        name="chip_sum_" + "_".join(BIG_WEIGHTS[k][0] for k in ks),
        grid_spec=pltpu.PrefetchScalarGridSpec(
            num_scalar_prefetch=1,
            grid=(n_rel // per_step,),
            in_specs=in_specs,
            out_specs=[pl.BlockSpec((per_step, r, c), lambda j, place: (j, 0, 0)) for r, c in shapes],
        ),
        out_shape=[jax.ShapeDtypeStruct((n_rel, r, c), BF16) for r, c in shapes],
    )(place, *args)
    _mark_issued(outs[0])
    return list(outs)


def _chips_rider(to_send, small=None):
    n = len(to_send)
    inputs = list(to_send) + ([] if small is None else [small])
    shapes = [jax.ShapeDtypeStruct((3, *t.shape[1:]), BF16) for t in to_send]
    sems = [pltpu.SemaphoreType.DMA((max(n, 1), 3)), pltpu.SemaphoreType.DMA((max(n, 1), 3))]
    if small is not None:
        shapes.append(jax.ShapeDtypeStruct((N_DEV, *small.shape), F32))
        sems += [pltpu.SemaphoreType.DMA((7,)), pltpu.SemaphoreType.DMA((7,)), pltpu.SemaphoreType.DMA]

    def copies(ins, outs, sem_refs):
        x, y, c, chips = _place()
        out = []
        for k in range(n):
            for j, (px, py) in enumerate(chips):
                out.append(pltpu.make_async_remote_copy(
                    src_ref=ins[k].at[j], dst_ref=outs[k].at[j],
                    send_sem=sem_refs[0].at[k, j], recv_sem=sem_refs[1].at[k, j],
                    device_id=(px, py, c), device_id_type=MESH))
        local = None
        if small is not None:
            me = 4 * x + 2 * y + c
            local = pltpu.make_async_copy(ins[n], outs[n].at[me], sem_refs[4])
            rel = 0
            for fx in (0, 1):
                for fy in (0, 1):
                    for fc in (0, 1):
                        if (fx, fy, fc) != (0, 0, 0):
                            out.append(pltpu.make_async_remote_copy(
                                src_ref=ins[n], dst_ref=outs[n].at[me],
                                send_sem=sem_refs[2].at[rel], recv_sem=sem_refs[3].at[rel],
                                device_id=(x ^ fx, y ^ fy, c ^ fc), device_id_type=MESH))
                            rel += 1
        return out, local

    def begin(ins, outs, sem_refs):
        remote, local = copies(ins, outs, sem_refs)
        if local is not None:
            local.start()
        for cp in remote:
            cp.start()

    def end(ins, outs, sem_refs):
        remote, local = copies(ins, outs, sem_refs)
        for cp in remote:
            cp.wait()
        if local is not None:
            local.wait()

    return _Rider(inputs, shapes, sems, begin, end)


def _together(first, second):
    cut = lambda refs, a, b: (refs[:len(a)], refs[len(a):len(a) + len(b)])

    def run(which):
        def fn(ins, outs, sems):
            parts = zip((first, second), cut(ins, first.inputs, second.inputs),
                        cut(outs, first.out_shapes, second.out_shapes), cut(sems, first.sems, second.sems))
            for rider, i, o, s in parts:
                hook = getattr(rider, which)
                if hook is not None:
                    hook(i, o, s)
        return fn

    return _Rider(first.inputs + second.inputs, first.out_shapes + second.out_shapes, first.sems + second.sems,
                  run("begin"), run("end"), run("middle"))


PEER_SETS = {"sibling": 1, "chips": 2, "sibling+chips": 3, "all": 4}


def _peers(pattern):
    x, y, c, chips = _place()
    sibling, others = [(x, y, 1 - c)], [(*chip, c) for chip in chips]
    if pattern == "all":
        return sibling + others + [(*chip, 1 - c) for chip in chips]
    return {"sibling": sibling, "chips": others, "sibling+chips": sibling + others}[pattern]


def _on_sequencer(name, pattern, rider):
    n_in, n_out = len(rider.inputs), len(rider.out_shapes)

    def body(*refs):
        ins, outs, sems = refs[:n_in], refs[n_in:n_in + n_out], refs[n_in + n_out:]
        peers = _peers(pattern)
        barrier = pltpu.get_barrier_semaphore()
        for peer in peers:
            pl.semaphore_signal(barrier, inc=1, device_id=peer, device_id_type=MESH)
        pl.semaphore_wait(barrier, len(peers))
        rider.begin(ins, outs, sems)
        if rider.middle is not None:
            rider.middle(ins, outs, sems)
        rider.end(ins, outs, sems)

    outs = pl.kernel(
        body,
        name=name,
        out_type=tuple(rider.out_shapes),
        mesh=plsc.ScalarSubcoreMesh(axis_name="sequencer", num_cores=1),
        scratch_types=tuple(rider.sems),
        compiler_params=pltpu.CompilerParams(collective_id=PEER_SETS[pattern]),
    )(*rider.inputs)
    return list(outs)


def _adamw(w, g, m, v):
    m = ADAM_B1 * m + (1.0 - ADAM_B1) * g
    v = ADAM_B2 * v + (1.0 - ADAM_B2) * jnp.square(g)
    m_hat = m / (1.0 - ADAM_B1 ** ADAM_STEP)
    v_hat = v / (1.0 - ADAM_B2 ** ADAM_STEP)
    delta = -ADAM_LR * (m_hat / (jnp.sqrt(v_hat) + ADAM_EPS) + ADAM_WD * w)
    return delta, m, v


def _adamw_big(ks, place, operands):
    n = len(ks)
    tiles = lambda i, place: (i, 0)
    in_specs, out_specs, out_shape = [], [], []
    for k in ks:
        _, _, r, c = BIG_WEIGHTS[k]
        tile = r // 2
        in_specs += [
            pl.BlockSpec((1, 1, tile, c), lambda i, place: (2 * place[0] + place[1], place[2], i, 0)),
            pl.BlockSpec((1, tile, c), lambda i, place: (2 * place[0] + place[1], i, 0)),
            pl.BlockSpec((3, tile, c), lambda i, place: (0, i, 0)),
        ] + [pl.BlockSpec((tile, c), tiles)] * 3
        out_specs += [pl.BlockSpec((tile, c), tiles)] * 4
        out_shape += [jax.ShapeDtypeStruct((r, c), F32)] * 4
    args, in_specs, _ = _after_last(sum((list(ops) for ops in operands), []), in_specs)

    def body(place_ref, *refs):
        ins, outs = refs[:6 * n], refs[len(args):]
        for i in range(n):
            mine_ref, sib_ref, land_ref, w_ref, m_ref, v_ref = ins[6 * i:6 * i + 6]
            g_ref, d_ref, nm_ref, nv_ref = outs[4 * i:4 * i + 4]
            g = mine_ref[0, 0] + sib_ref[0]
            g = ((g + land_ref[0].astype(F32)) + land_ref[1].astype(F32)) + land_ref[2].astype(F32)
            g_ref[...] = g
            d_ref[...], nm_ref[...], nv_ref[...] = _adamw(w_ref[...], g, m_ref[...], v_ref[...])

    outs = pl.pallas_call(
        body,
        name="adamw_" + "_".join(BIG_WEIGHTS[k][0] for k in ks),
        grid_spec=pltpu.PrefetchScalarGridSpec(
            num_scalar_prefetch=1, grid=(2,), in_specs=in_specs, out_specs=out_specs),
        out_shape=out_shape,
    )(place, *args)
    _mark_issued(outs[0])
    return [outs[4 * i:4 * i + 4] for i in range(n)]


def _pack_small(arrays):
    rows, offsets = [], []
    at = 0
    for a in arrays:
        if a.ndim != 2 or a.shape[1] != SMALL_LANES or a.shape[0] % 8:
            flat = a.reshape(-1)
            n_rows = -(-flat.shape[0] // (8 * SMALL_LANES)) * 8
            a = jnp.pad(flat, (0, n_rows * SMALL_LANES - flat.shape[0])).reshape(n_rows, SMALL_LANES)
        rows.append(a)
        offsets.append(at)
        at += a.shape[0]
    return jnp.concatenate(rows, axis=0), offsets


def _unpack_small(tot, at, shape):
    r, c = shape
    if r % 8 == 0:
        return tot[at:at + r, :c]
    assert r == 1
    if c <= SMALL_LANES:
        return tot[at:at + 1, :c]
    return jnp.concatenate([tot[at + j:at + j + 1, :] for j in range(c // SMALL_LANES)], axis=1)


def _small_update(packs, loss_at, grads_at, ws, ms, vs):
    n, n_packs = len(ws), len(packs)

    def body(*refs):
        pack_refs, refs = refs[:n_packs], refs[n_packs:]
        w_refs, m_refs, v_refs, loss_ref, outs = refs[:n], refs[n:2 * n], refs[2 * n:3 * n], refs[3 * n], refs[3 * n + 1:]
        tots = []
        for p_ref in pack_refs:
            tot = p_ref[0]
            for j in range(1, N_DEV):
                tot = tot + p_ref[j]
            tots.append(tot)
        loss_ref[...] = _unpack_small(tots[loss_at[0]], loss_at[1], (1, 1))
        for i, (pack, at) in enumerate(grads_at):
            g = _unpack_small(tots[pack], at, w_refs[i].shape)
            outs[i][...] = g
            outs[n + i][...], outs[2 * n + i][...], outs[3 * n + i][...] = _adamw(
                w_refs[i][...], g, m_refs[i][...], v_refs[i][...])

    shapes = [jax.ShapeDtypeStruct(w.shape, F32) for w in ws]
    outs = pl.pallas_call(body, name="small_update", out_shape=[jax.ShapeDtypeStruct((1, 1), F32)] + shapes * 4)(
        *packs, *ws, *ms, *vs)
    return outs[0], outs[1:]


SMALL_NAMES = ("g_attn_norm", "g_q", "g_k", "attn_sinks", "rel_bias", "w_pool", "pool_scale", "g_ffn_norm", "g_ple_norm")


def kernel(x, p, w_in, w_out, g_attn_norm, g_q, g_k, attn_sinks, rel_bias, w_pool, pool_scale, g_ffn_norm, w_gate, w_up, w_down, g_ple_norm, w_ple_gate, w_ple_proj, loss_target, m_w_in, m_w_out, m_g_attn_norm, m_g_q, m_g_k, m_attn_sinks, m_rel_bias, m_w_pool, m_pool_scale, m_g_ffn_norm, m_w_gate, m_w_up, m_w_down, m_g_ple_norm, m_w_ple_gate, m_w_ple_proj, v_w_in, v_w_out, v_g_attn_norm, v_g_q, v_g_k, v_attn_sinks, v_rel_bias, v_w_pool, v_pool_scale, v_g_ffn_norm, v_w_gate, v_w_up, v_w_down, v_g_ple_norm, v_w_ple_gate, v_w_ple_proj):
    weights = dict(w_in=w_in, w_out=w_out, g_attn_norm=g_attn_norm, g_q=g_q, g_k=g_k, attn_sinks=attn_sinks,
                   rel_bias=rel_bias, w_pool=w_pool, pool_scale=pool_scale, g_ffn_norm=g_ffn_norm, w_gate=w_gate,
                   w_up=w_up, w_down=w_down, g_ple_norm=g_ple_norm, w_ple_gate=w_ple_gate, w_ple_proj=w_ple_proj)
    m_in = dict(w_in=m_w_in, w_out=m_w_out, g_attn_norm=m_g_attn_norm, g_q=m_g_q, g_k=m_g_k, attn_sinks=m_attn_sinks,
                rel_bias=m_rel_bias, w_pool=m_w_pool, pool_scale=m_pool_scale, g_ffn_norm=m_g_ffn_norm, w_gate=m_w_gate,
                w_up=m_w_up, w_down=m_w_down, g_ple_norm=m_g_ple_norm, w_ple_gate=m_w_ple_gate, w_ple_proj=m_w_ple_proj)
    v_in = dict(w_in=v_w_in, w_out=v_w_out, g_attn_norm=v_g_attn_norm, g_q=v_g_q, g_k=v_g_k, attn_sinks=v_attn_sinks,
                rel_bias=v_rel_bias, w_pool=v_w_pool, pool_scale=v_pool_scale, g_ffn_norm=v_g_ffn_norm, w_gate=v_w_gate,
                w_up=v_w_up, w_down=v_w_down, g_ple_norm=v_g_ple_norm, w_ple_gate=v_w_ple_gate, w_ple_proj=v_w_ple_proj)

    _issued.clear()
    xs = x[0]
    ps = p[0, 0]
    target = loss_target[0]
    wp = w_pool[0]
    gq_t = jnp.tile(g_q, (1, ATTN_WIDTH // HEAD_DIM))
    gk_t = jnp.tile(g_k, (1, KV_WIDTH // HEAD_DIM))

    def to_blocks(k, arr):
        return jnp.swapaxes(arr[0], 0, 1) if BIG_WEIGHTS[k][1] else arr[0]

    def from_blocks(k, arr):
        return (jnp.swapaxes(arr, 0, 1) if BIG_WEIGHTS[k][1] else arr)[None]

    IN, OUT, GATE, UP, DOWN, PG, PP = range(N_BIG)
    full = lambda g: g.reshape(N_DEV * g.shape[1], g.shape[2])
    halves = lambda k, g: g.reshape(N_CHIPS, 2, *BIG_WEIGHTS[k][2:])
    place = jnp.stack([lax.axis_index("x"), lax.axis_index("y"), lax.axis_index("c")]).astype(jnp.int32)

    sh, w_in_g, tab = _cast_and_gather_first(
        [to_blocks(k, weights[name]) for k, (name, _, _, _) in enumerate(BIG_WEIGHTS)], rel_bias.T)
    w_in_t = full(w_in_g)

    (w_out_g,) = _on_sequencer("gather_out", "sibling+chips", _gather_rider([sh[OUT]]))
    wg_g, wu_g = _on_sequencer("gather_gate_up", "sibling+chips", _gather_rider([sh[GATE], sh[UP]]))
    wd_g, w_pg_g, w_pp_g = _on_sequencer("gather_down_ple", "sibling+chips", _gather_rider([sh[DOWN], sh[PG], sh[PP]]))
    (zqk, qn, kn, v, u) = _in_proj(xs, g_attn_norm, w_in_t, gq_t, gk_t)
    (a,) = _attn_fwd(qn, kn, v, tab, attn_sinks)
    w_out_f = full(w_out_g)
    (h1, hn2, m_out) = _mix_out(u, a, xs, w_out_f, wp, pool_scale, g_ffn_norm)
    wg_t, wu_t = full(wg_g), full(wu_g)
    (gt, up) = _ffn_up(hn2, wg_t, wu_t)
    w_down_f = full(wd_g)

    partial, from_sibling, sums, landed = [None] * N_BIG, [None] * N_BIG, [None] * N_BIG, [None] * N_BIG

    def to_sibling(name, ks, grads):
        for k, g in zip(ks, grads):
            partial[k] = halves(k, g)
        got = _on_sequencer(name, "sibling", _sibling_rider([partial[k] for k in ks]))
        for k, g in zip(ks, got):
            from_sibling[k] = g

    def chip_sum(*ks):
        for k, s in zip(ks, _chip_sum(ks, place, [partial[k] for k in ks], [from_sibling[k] for k in ks])):
            sums[k] = s

    def to_chips(name, ks, small=None):
        got = _on_sequencer(name, "chips" if small is None else "all", _chips_rider([sums[k] for k in ks], small))
        for k, g in zip(ks, got):
            landed[k] = g
        return got[len(ks):]

    (loss_part, dh2, d_wpg, d_wpp, d_g_ple) = _ffn_down_ple(
        gt, up, h1, w_down_f, ps, target, g_ple_norm, full(w_pg_g), w_pp_g)
    to_sibling("sibling_ple", (PG, PP), (d_wpg, d_wpp))
    (dgt, dup, dh1, dh1b, d_g_ffn, d_wd) = _ffn_bwd_act(dh2, h1, gt, up, g_ffn_norm, wg_t, wu_t, w_down_f)
    to_sibling("sibling_down", (DOWN,), (d_wd,))
    chip_sum(PG, PP)
    to_chips("chips_ple", (PG, PP))
    chip_sum(DOWN)
    to_chips("chips_down", (DOWN,))
    (d_wg_t, d_wu_t) = _ffn_bwd_w(dgt, dup, hn2)
    to_sibling("sibling_gate_up", (GATE, UP), (d_wg_t, d_wu_t))
    _complete_before_next([landed[PG], landed[PP], landed[DOWN]])
    (da, du, d_wpool, d_scale, d_wo) = _mix_bwd(dh1b, u, a, m_out, w_out_f, wp, pool_scale)
    to_sibling("sibling_out", (OUT,), (d_wo,))
    chip_sum(GATE, UP)
    to_chips("chips_gate_up", (GATE, UP))
    (dqn, dkn, dv, dl_acc, d_sinks) = _attn_bwd(qn, kn, v, a, da, tab, attn_sinks)
    chip_sum(OUT)
    early, early_at = _pack_small([d_wpool.reshape(POOL_WIDTH, POOL_GROUP), d_scale, d_g_ffn, d_g_ple, loss_part[:, :1]])
    landed[OUT], early_all = _on_sequencer(
        "chips_out", "sibling+chips", _together(_chips_rider([sums[OUT]]), _gather_rider([early])))
    (grad_x, d_win_t, d_g_attn, d_gq, d_gk) = _in_proj_bwd(dqn, dkn, dv, du, zqk, xs, dh1, g_attn_norm, gq_t, gk_t, w_in_t)
    to_sibling("sibling_in", (IN,), (d_win_t,))
    _complete_before_next([landed[OUT], landed[GATE], landed[UP], early_all])
    (d_rel_t,) = _bias_table_bwd(dl_acc)
    chip_sum(IN)
    late, late_at = _pack_small([d_g_attn, d_gq[:, :HEAD_DIM], d_gk[:, :HEAD_DIM], d_sinks[:, 0], d_rel_t])
    (late_all,) = to_chips("chips_in", (IN,), late)

    out = {"grad": {}, "delta": {}, "new_m": {}, "new_v": {}}
    for ks in ((PG, PP, DOWN), (OUT, GATE, UP), (IN,)):
        names = [BIG_WEIGHTS[k][0] for k in ks]
        results = _adamw_big(ks, place, [
            (partial[k], from_sibling[k], landed[k], to_blocks(k, weights[n]), to_blocks(k, m_in[n]),
             to_blocks(k, v_in[n])) for k, n in zip(ks, names)])
        for k, name, res in zip(ks, names, results):
            for kind, r in zip(("grad", "delta", "new_m", "new_v"), res):
                out[kind][name] = from_blocks(k, r)
    def as_rows(name, arr):
        return arr.T if name == "rel_bias" else arr.reshape(POOL_WIDTH, POOL_GROUP) if name == "w_pool" else arr

    def from_rows(name, arr):
        return arr.T if name == "rel_bias" else arr.reshape(w_pool.shape) if name == "w_pool" else arr

    grads_at = dict(w_pool=(0, early_at[0]), pool_scale=(0, early_at[1]), g_ffn_norm=(0, early_at[2]),
                    g_ple_norm=(0, early_at[3]), g_attn_norm=(1, late_at[0]), g_q=(1, late_at[1]), g_k=(1, late_at[2]),
                    attn_sinks=(1, late_at[3]), rel_bias=(1, late_at[4]))
    loss, updates = _small_update(
        [early_all, late_all], (0, early_at[4]), [grads_at[n] for n in SMALL_NAMES],
        [as_rows(n, weights[n]) for n in SMALL_NAMES], [as_rows(n, m_in[n]) for n in SMALL_NAMES],
        [as_rows(n, v_in[n]) for n in SMALL_NAMES])
    loss = loss.reshape(())
    n_small = len(SMALL_NAMES)
    for j, kind in enumerate(("grad", "delta", "new_m", "new_v")):
        for i, name in enumerate(SMALL_NAMES):
            out[kind][name] = from_rows(name, updates[j * n_small + i])

    _issued.clear()
    order = ("w_in", "w_out", "g_attn_norm", "g_q", "g_k", "attn_sinks", "rel_bias", "w_pool", "pool_scale",
             "g_ffn_norm", "w_gate", "w_up", "w_down", "g_ple_norm", "w_ple_gate", "w_ple_proj")
    return (loss, grad_x[None], *[out["grad"][n] for n in order], *[out["delta"][n] for n in order],
            *[out["new_m"][n] for n in order], *[out["new_v"][n] for n in order])
```

```python
import math

import jax
import jax.numpy as jnp
import numpy as np
from jax import lax
from jax.experimental import pallas as pl
from jax.experimental.pallas import tpu as pltpu
from jax.experimental.pallas import tpu_sc as plsc

F32 = jnp.float32
BF16 = jnp.bfloat16
MESH = pl.DeviceIdType.MESH

D_MODEL = 1024
HEAD_DIM = 64
ATTN_WIDTH = 512
KV_WIDTH = 128
POOL_WIDTH = 512
POOL_SIZES = (2, 4, 8, 16)
POOL_GROUP = 128
POOL_HALO = 16
IN_WIDTH = 1280
D_FF = 2816
PLE_DIM = 256
BLOCK = 128
N_BUCKETS = 32
MAX_DISTANCE = 128
EPS = 1e-6
N_DEV = 8
N_CHIPS = 4

ADAM_LR = 0.001
ADAM_B1 = 0.9
ADAM_B2 = 0.999
ADAM_EPS = 1e-08
ADAM_WD = 0.01
ADAM_STEP = 10

TOKEN_TILE = 512
FFN_BWD_TILE = 256
FF_CHUNK = 256
CHIP_SUM_ONE_STEP_BYTES = 9 * 2 ** 20
FFN_W_SLAB = 256
ATTN_STEP_BLOCKS = 4
HEADS_A = (0, 2, 5, 7)
HEADS_B = (1, 3, 4, 6)
SMALL_LANES = 128


def _nn(a, b):
    return jnp.dot(a, b, preferred_element_type=F32)


def _nt(a, b):
    return lax.dot_general(a, b, (((1,), (1,)), ((), ())), preferred_element_type=F32)


def _tn(a, b):
    return lax.dot_general(a, b, (((0,), (0,)), ((), ())), preferred_element_type=F32)


def _resident(shape):
    nd = len(shape)
    return pl.BlockSpec(shape, lambda i, _nd=nd: (0,) * _nd, pipeline_mode=pl.Buffered(1))


def _rows(tile, width):
    return pl.BlockSpec((tile, width), lambda i: (i, 0))


def _acc(shape):
    nd = len(shape)
    return pl.BlockSpec(shape, lambda i, _nd=nd: (0,) * _nd)


def _head_mean_matrix(width):
    idx = np.arange(width) // HEAD_DIM
    return jnp.asarray((idx[:, None] == idx[None, :]).astype(np.float32) / HEAD_DIM, dtype=BF16)


def _seg_mean(v, bmat):
    hi = v.astype(BF16)
    lo = (v - hi.astype(F32)).astype(BF16)
    return _nn(hi, bmat) + _nn(lo, bmat)


def _rms(x):
    return lax.rsqrt(jnp.mean(x * x, axis=-1, keepdims=True) + EPS)


def _rms_bwd(d_y, x, r, g):
    gy = d_y * g
    d_x = r * gy - x * (r * r * r) * jnp.mean(gy * x, axis=-1, keepdims=True)
    d_g = jnp.sum(d_y * (x * r), axis=0, keepdims=True)
    return d_x, d_g


def _lane_lo(shape):
    return lax.broadcasted_iota(jnp.int32, shape, 1) < HEAD_DIM


class _Rider:
    def __init__(self, inputs, out_shapes, sems, begin, end, middle=None):
        self.inputs, self.out_shapes, self.sems = list(inputs), list(out_shapes), list(sems)
        self.begin, self.middle, self.end = begin, middle, end


_issued = []


def _after_last(args, in_specs):
    extra = list(_issued)
    return list(args) + extra, list(in_specs) + [pl.BlockSpec(memory_space=pl.ANY)] * len(extra), len(extra)


def _mark_issued(out):
    _issued[:] = [out]


def _complete_before_next(arrays):
    _issued.extend(arrays)


def _call(body, args, *, name, grid, in_specs, out_specs, out_shape, scratch_shapes=()):
    n_args = len(args)
    args, in_specs, _ = _after_last(args, in_specs)

    def ordered(*refs):
        body(*refs[:n_args], *refs[len(args):])

    outs = pl.pallas_call(ordered, name=name, grid=grid, in_specs=in_specs, out_specs=list(out_specs),
                          out_shape=list(out_shape), scratch_shapes=list(scratch_shapes))(*args)
    _mark_issued(outs[0])
    return list(outs)


def _in_proj(x, g_attn, w_in_t, gq_t, gk_t):
    s = x.shape[0]
    ts = min(TOKEN_TILE, s)

    def body(x_ref, g_ref, w_ref, gq_ref, gk_ref, bq_ref, bk_ref, zqk_ref, qn_ref, kn_ref, v_ref, u_ref):
        xf = x_ref[...]
        hn = ((xf * _rms(xf)) * g_ref[...]).astype(BF16)
        z = _nt(hn, w_ref[...])
        q = z[:, :ATTN_WIDTH]
        k = z[:, ATTN_WIDTH:ATTN_WIDTH + KV_WIDTH]
        zqk_ref[...] = z[:, :ATTN_WIDTH + KV_WIDTH]
        rq = lax.rsqrt(_seg_mean(q * q, bq_ref[...]) + EPS)
        qn_ref[...] = ((q * rq) * gq_ref[...]).astype(BF16)
        rk = lax.rsqrt(_seg_mean(k * k, bk_ref[...]) + EPS)
        kn_ref[...] = ((k * rk) * gk_ref[...]).astype(BF16)
        v_ref[...] = z[:, ATTN_WIDTH + KV_WIDTH:ATTN_WIDTH + 2 * KV_WIDTH].astype(BF16)
        u_ref[...] = z[:, ATTN_WIDTH + 2 * KV_WIDTH:]

    return _call(
        body,
        (x, g_attn, w_in_t, gq_t, gk_t, _head_mean_matrix(ATTN_WIDTH), _head_mean_matrix(KV_WIDTH)),
        name="in_proj",
        grid=(s // ts,),
        in_specs=[
            _rows(ts, D_MODEL),
            _resident((1, D_MODEL)),
            _resident((IN_WIDTH, D_MODEL)),
            _resident((1, ATTN_WIDTH)),
            _resident((1, KV_WIDTH)),
            _resident((ATTN_WIDTH, ATTN_WIDTH)),
            _resident((KV_WIDTH, KV_WIDTH)),
        ],
        out_specs=[
            _rows(ts, ATTN_WIDTH + KV_WIDTH),
            _rows(ts, ATTN_WIDTH),
            _rows(ts, KV_WIDTH),
            _rows(ts, KV_WIDTH),
            _rows(ts, POOL_WIDTH),
        ],
        out_shape=[
            jax.ShapeDtypeStruct((s, ATTN_WIDTH + KV_WIDTH), F32),
            jax.ShapeDtypeStruct((s, ATTN_WIDTH), BF16),
            jax.ShapeDtypeStruct((s, KV_WIDTH), BF16),
            jax.ShapeDtypeStruct((s, KV_WIDTH), BF16),
            jax.ShapeDtypeStruct((s, POOL_WIDTH), F32),
        ],
    )


def _bucket_ranges():
    n = np.arange(MAX_DISTANCE)
    max_exact = N_BUCKETS // 2
    nf = np.maximum(n, 1).astype(np.float64)
    large = max_exact + (np.log(nf / max_exact) / math.log(MAX_DISTANCE / max_exact) * (N_BUCKETS - max_exact)).astype(np.int64)
    bucket = np.where(n < max_exact, n, np.minimum(large, N_BUCKETS - 1))
    out = []
    for b in range(N_BUCKETS):
        idx = np.nonzero(bucket == b)[0]
        out.append((int(idx.min()), int(idx.max()) + 1))
    return out


def _band_distance():
    i = lax.broadcasted_iota(jnp.int32, (BLOCK, 2 * BLOCK), 0)
    j = lax.broadcasted_iota(jnp.int32, (BLOCK, 2 * BLOCK), 1)
    return BLOCK + i - j


BIAS_TABLE_SHAPE = (2, 4 * BLOCK, 2 * BLOCK)


def _write_bias_table(rb_ref, tab_ref):
    d = _band_distance()
    for half, heads in enumerate((HEADS_A, HEADS_B)):
        for slot, h in enumerate(heads):
            t = jnp.full((BLOCK, 2 * BLOCK), -jnp.inf, F32)
            for b, (lo, hi) in enumerate(_bucket_ranges()):
                t = jnp.where((d >= lo) & (d < hi), rb_ref[h, b], t)
            tab_ref[half, slot * BLOCK:(slot + 1) * BLOCK, :] = t


def _bias_table_bwd(dl_acc):
    ranges = _bucket_ranges()
    n_heads = len(HEADS_A) + len(HEADS_B)

    def body(dl_ref, out_ref):
        d = _band_distance()
        row = lax.broadcasted_iota(jnp.int32, (n_heads, SMALL_LANES), 0)
        lane = lax.broadcasted_iota(jnp.int32, (n_heads, SMALL_LANES), 1)
        out = jnp.zeros((n_heads, SMALL_LANES), F32)
        for b, (lo, hi) in enumerate(ranges):
            in_bucket = (d >= lo) & (d < hi)
            for half, heads in enumerate((HEADS_A, HEADS_B)):
                for slot, h in enumerate(heads):
                    g = dl_ref[half, slot * BLOCK:(slot + 1) * BLOCK, :]
                    part = jnp.sum(jnp.where(in_bucket, g, 0.0), axis=0, keepdims=True)
                    tot = jnp.sum(part, axis=1, keepdims=True)
                    out = jnp.where((row == h) & (lane == b), tot, out)
        out_ref[...] = out

    return _call(
        body,
        (dl_acc,),
        name="bias_table_bwd",
        grid=(1,),
        in_specs=[_acc((2, 4 * BLOCK, 2 * BLOCK))],
        out_specs=[_acc((n_heads, SMALL_LANES))],
        out_shape=[jax.ShapeDtypeStruct((n_heads, SMALL_LANES), F32)],
    )


def _stack_heads(pairs, lo_mask):
    zero = jnp.zeros_like(pairs[0])
    lo = [jnp.where(lo_mask, t, zero) for t in pairs]
    hi = [jnp.where(lo_mask, zero, t) for t in pairs]
    return (jnp.concatenate([lo[0], lo[1], hi[2], hi[3]], axis=0),
            jnp.concatenate([hi[0], hi[1], lo[2], lo[3]], axis=0))


def _unstack_heads(out_a, out_b, lo_mask):
    t = lambda x, r: x[r * BLOCK:(r + 1) * BLOCK, :]
    return [
        jnp.where(lo_mask, t(out_a, 0), t(out_b, 0)),
        jnp.where(lo_mask, t(out_a, 1), t(out_b, 1)),
        jnp.where(lo_mask, t(out_b, 2), t(out_a, 2)),
        jnp.where(lo_mask, t(out_b, 3), t(out_a, 3)),
    ]


def _sink_column(sink_ref, heads):
    row = lax.broadcasted_iota(jnp.int32, (4 * BLOCK, 1), 0)
    col = jnp.full((4 * BLOCK, 1), sink_ref[0, heads[3]], F32)
    for slot in (2, 1, 0):
        col = jnp.where(row < (slot + 1) * BLOCK, sink_ref[0, heads[slot]], col)
    return col


def _band_scores(q_stack, keys, tab, first_block):
    s = _nt(q_stack, keys) * (HEAD_DIM ** -0.5) + tab
    if first_block is not None:
        col = lax.broadcasted_iota(jnp.int32, s.shape, 1)
        s = jnp.where(jnp.logical_and(first_block, col < BLOCK), -jnp.inf, s)
    return s


def _softmax_with_sink(s, sink):
    m = jnp.maximum(jnp.max(s, axis=-1, keepdims=True), sink)
    e = jnp.exp(s - m)
    e_sink = jnp.exp(sink - m)
    den = jnp.sum(e, axis=-1, keepdims=True) + e_sink
    return e / den, e_sink / den


def _band_probs(q_stack, keys, tab, sink, first_block):
    return _softmax_with_sink(_band_scores(q_stack, keys, tab, first_block), sink)


def _attn_specs(n_groups):
    group = lambda n: (jnp.minimum(n, n_groups - 1), 0)
    prev = lambda n: (jnp.maximum(jnp.minimum(n, n_groups - 1) * ATTN_STEP_BLOCKS - 1, 0), 0)
    return group, prev


def _band(prev_ref, group_ref, b):
    rows = lambda i: group_ref[i * BLOCK:(i + 1) * BLOCK, :]
    band = jnp.concatenate([prev_ref[...] if b == 0 else rows(b - 1), rows(b)], axis=0)
    return band, pltpu.roll(band, HEAD_DIM, 1)


def _attn_fwd(qn, kn, v, tab, sinks):
    s = qn.shape[0]
    n_groups = s // (ATTN_STEP_BLOCKS * BLOCK)
    group, prev = _attn_specs(n_groups)
    rows = ATTN_STEP_BLOCKS * BLOCK

    def body(sink_ref, q_ref, kc_ref, kp_ref, vc_ref, vp_ref, tab_ref, o_ref):
        first = pl.program_id(0) == 0
        lo_mask = _lane_lo((BLOCK, BLOCK))
        for b in range(ATTN_STEP_BLOCKS):
            at = slice(b * BLOCK, (b + 1) * BLOCK)
            kk, kk_sw = _band(kp_ref, kc_ref, b)
            vv, vv_sw = _band(vp_ref, vc_ref, b)
            q_a, q_b = _stack_heads([q_ref[at, p * BLOCK:(p + 1) * BLOCK] for p in range(4)], lo_mask)
            no_prev = first if b == 0 else None
            p_a, _ = _band_probs(q_a, kk, tab_ref[0], _sink_column(sink_ref, HEADS_A), no_prev)
            p_b, _ = _band_probs(q_b, kk_sw, tab_ref[1], _sink_column(sink_ref, HEADS_B), no_prev)
            out = _unstack_heads(_nn(p_a.astype(BF16), vv), _nn(p_b.astype(BF16), vv_sw), lo_mask)
            for p in range(4):
                o_ref[at, p * BLOCK:(p + 1) * BLOCK] = out[p].astype(BF16)

    return _call(
        body,
        (sinks, qn, kn, kn, v, v, tab),
        name="attn_fwd",
        grid=(n_groups,),
        in_specs=[
            pl.BlockSpec(memory_space=pltpu.SMEM),
            pl.BlockSpec((rows, ATTN_WIDTH), group),
            pl.BlockSpec((rows, KV_WIDTH), group),
            pl.BlockSpec((BLOCK, KV_WIDTH), prev),
            pl.BlockSpec((rows, KV_WIDTH), group),
            pl.BlockSpec((BLOCK, KV_WIDTH), prev),
            _resident((2, 4 * BLOCK, 2 * BLOCK)),
        ],
        out_specs=[pl.BlockSpec((rows, ATTN_WIDTH), group)],
        out_shape=[jax.ShapeDtypeStruct((s, ATTN_WIDTH), BF16)],
    )


def _pooled(u_tile, u_halo, tile_index, tile_rows):
    halo = jnp.where(tile_index > 0, u_halo, 0.0)
    ext = jnp.concatenate([halo, u_tile], axis=0)
    sums = []
    acc = ext
    for shift in (1, 2, 4, 8):
        acc = acc + pltpu.roll(acc, shift, 0)
        sums.append(acc)
    t = tile_index * tile_rows + lax.broadcasted_iota(jnp.int32, (tile_rows, 1), 0)
    out = []
    for g, w in enumerate(POOL_SIZES):
        lanes = slice(g * POOL_GROUP, (g + 1) * POOL_GROUP)
        cnt = jnp.minimum(t + 1, w).astype(F32)
        out.append(sums[g][POOL_HALO:, lanes] / cnt - u_tile[:, lanes])
    return out


def _halo_before(tile):
    return lambda i: (jnp.maximum(i * (tile // POOL_HALO) - 1, 0), 0)


def _mix_out(u, a, x, w_out, w_pool, pool_scale, g_ffn):
    s = x.shape[0]
    ts = min(TOKEN_TILE, s)

    def body(u_ref, uh_ref, a_ref, x_ref, wo_ref, wp_ref, sc_ref, g_ref, h1_ref, hn_ref, m_ref):
        i = pl.program_id(0)
        pooled = _pooled(u_ref[...], uh_ref[...], i, ts)
        for g in range(len(POOL_SIZES)):
            lanes = slice(g * POOL_GROUP, (g + 1) * POOL_GROUP)
            y = _nn(pooled[g].astype(BF16), wp_ref[g].astype(BF16))
            m_ref[:, lanes] = (y * sc_ref[:, lanes]).astype(BF16)
        h1 = x_ref[...] + _nn(a_ref[...], wo_ref[:ATTN_WIDTH, :]) + _nn(m_ref[...], wo_ref[ATTN_WIDTH:, :])
        h1_ref[...] = h1
        hn_ref[...] = ((h1 * _rms(h1)) * g_ref[...]).astype(BF16)

    return _call(
        body,
        (u, u, a, x, w_out, w_pool, pool_scale, g_ffn),
        name="mix_out",
        grid=(s // ts,),
        in_specs=[
            _rows(ts, POOL_WIDTH),
            pl.BlockSpec((POOL_HALO, POOL_WIDTH), _halo_before(ts)),
            _rows(ts, ATTN_WIDTH),
            _rows(ts, D_MODEL),
            _resident((D_MODEL, D_MODEL)),
            _resident((len(POOL_SIZES), POOL_GROUP, POOL_GROUP)),
            _resident((1, POOL_WIDTH)),
            _resident((1, D_MODEL)),
        ],
        out_specs=[_rows(ts, D_MODEL), _rows(ts, D_MODEL), _rows(ts, POOL_WIDTH)],
        out_shape=[
            jax.ShapeDtypeStruct((s, D_MODEL), F32),
            jax.ShapeDtypeStruct((s, D_MODEL), BF16),
            jax.ShapeDtypeStruct((s, POOL_WIDTH), BF16),
        ],
    )


def _ffn_up(hn2, wg_t, wu_t):
    s = hn2.shape[0]
    ts = min(TOKEN_TILE, s)

    def body(hn_ref, wg_ref, wu_ref, gt_ref, up_ref):
        hn = hn_ref[...]
        for c in range(D_FF // FF_CHUNK):
            cols = slice(c * FF_CHUNK, (c + 1) * FF_CHUNK)
            gt_ref[:, cols] = _nt(hn, wg_ref[cols, :]).astype(BF16)
            up_ref[:, cols] = _nt(hn, wu_ref[cols, :]).astype(BF16)

    return _call(
        body,
        (hn2, wg_t, wu_t),
        name="ffn_up",
        grid=(s // ts,),
        in_specs=[_rows(ts, D_MODEL), _resident((D_FF, D_MODEL)), _resident((D_FF, D_MODEL))],
        out_specs=[_rows(ts, D_FF), _rows(ts, D_FF)],
        out_shape=[jax.ShapeDtypeStruct((s, D_FF), BF16), jax.ShapeDtypeStruct((s, D_FF), BF16)],
    )


def _silu_mul(gt, up):
    return (gt * jax.nn.sigmoid(gt)) * up


def _ffn_down_ple(gt, up, h1, w_down, p, target, g_ple, w_pg, w_pp):
    s = h1.shape[0]
    ts = min(TOKEN_TILE, s)
    blk = D_MODEL // N_DEV

    n_steps = s // ts
    ring = 3

    def body(gt_hbm, up_hbm, h1_ref, wd_ref, p_ref, t_ref, g_ref, wpg_ref, wpp_ref,
             loss_ref, dh_ref, dwpg_ref, dwpp_ref, dg_ref, gt_buf, up_buf, sems):
        step = pl.program_id(0)

        def fetch(tile, slot):
            rows = pl.ds(pl.multiple_of(tile * ts, ts), ts)
            return [pltpu.make_async_copy(src.at[rows, :], buf.at[slot], sems.at[k, slot])
                    for k, (src, buf) in enumerate(((gt_hbm, gt_buf), (up_hbm, up_buf)))]

        @pl.when(step == 0)
        def _():
            loss_ref[...] = jnp.zeros_like(loss_ref)
            dwpg_ref[...] = jnp.zeros_like(dwpg_ref)
            dwpp_ref[...] = jnp.zeros_like(dwpp_ref)
            dg_ref[...] = jnp.zeros_like(dg_ref)
            for tile in range(min(ring - 1, n_steps)):
                for cp in fetch(tile, tile):
                    cp.start()

        @pl.when(step + ring - 1 < n_steps)
        def _():
            for cp in fetch(step + ring - 1, (step + ring - 1) % ring):
                cp.start()

        slot = step % ring
        for cp in fetch(step, slot):
            cp.wait()
        gt_ref, up_ref = gt_buf.at[slot], up_buf.at[slot]
        h2v = h1_ref[...]
        for c in range(D_FF // FF_CHUNK):
            cols = slice(c * FF_CHUNK, (c + 1) * FF_CHUNK)
            act = _silu_mul(gt_ref[:, cols].astype(F32), up_ref[:, cols].astype(F32)).astype(BF16)
            h2v = _nn(act, wd_ref[cols, :]) + h2v
        r = _rms(h2v)
        hn = ((h2v * r) * g_ref[...]).astype(BF16)
        gate = jax.nn.sigmoid(_nn(hn, wpg_ref[...]))
        pb = p_ref[...].astype(BF16)
        pp = _nn(pb, jnp.concatenate([wpp_ref[j] for j in range(N_DEV)], axis=1))
        diff = (h2v + gate * pp) - t_ref[...]
        loss_ref[...] += jnp.sum(jnp.sum(diff * diff, axis=0, keepdims=True), axis=1, keepdims=True) * (0.5 / D_MODEL)
        dy = diff * (1.0 / D_MODEL)
        d_pp = (dy * gate).astype(BF16)
        d_pre = ((dy * pp) * (gate * (1.0 - gate))).astype(BF16)
        d_x, d_g = _rms_bwd(_nt(d_pre, wpg_ref[...]), h2v, r, g_ref[...])
        dg_ref[...] += d_g
        dh_ref[...] = dy + d_x
        d_wpp = _tn(pb, d_pp)
        for j in range(N_DEV):
            dwpp_ref[j] += d_wpp[:, j * blk:(j + 1) * blk]
        dwpg_ref[...] += _tn(hn, d_pre)

    return _call(
        body,
        (gt, up, h1, w_down, p, target, g_ple, w_pg, w_pp),
        name="ffn_down_ple",
        grid=(s // ts,),
        in_specs=[
            pl.BlockSpec(memory_space=pl.ANY),
            pl.BlockSpec(memory_space=pl.ANY),
            _rows(ts, D_MODEL),
            _resident((D_FF, D_MODEL)),
            _rows(ts, PLE_DIM),
            _rows(ts, D_MODEL),
            _resident((1, D_MODEL)),
            _resident((D_MODEL, D_MODEL)),
            _resident((N_DEV, PLE_DIM, blk)),
        ],
        out_specs=[
            _acc((1, SMALL_LANES)),
            _rows(ts, D_MODEL),
            _acc((D_MODEL, D_MODEL)),
            _acc((N_DEV, PLE_DIM, blk)),
            _acc((1, D_MODEL)),
        ],
        out_shape=[
            jax.ShapeDtypeStruct((1, SMALL_LANES), F32),
            jax.ShapeDtypeStruct((s, D_MODEL), F32),
            jax.ShapeDtypeStruct((D_MODEL, D_MODEL), F32),
            jax.ShapeDtypeStruct((N_DEV, PLE_DIM, blk), F32),
            jax.ShapeDtypeStruct((1, D_MODEL), F32),
        ],
        scratch_shapes=[pltpu.VMEM((ring, ts, D_FF), BF16), pltpu.VMEM((ring, ts, D_FF), BF16),
                        pltpu.SemaphoreType.DMA((2, ring))],
    )


def _ffn_bwd_act(dh2, h1, gt, up, g_ffn, wg_t, wu_t, w_down):
    s = h1.shape[0]
    ts = min(FFN_BWD_TILE, s)

    def body(dh_ref, h1_ref, gt_ref, up_ref, g_ref, wg_ref, wu_ref, wd_ref,
             dgt_ref, dup_ref, dh1_ref, dh1b_ref, dg_ref, dwd_ref, act_ref):
        @pl.when(pl.program_id(0) == 0)
        def _():
            dg_ref[...] = jnp.zeros_like(dg_ref)
            dwd_ref[...] = jnp.zeros_like(dwd_ref)

        dhb = dh_ref[...].astype(BF16)
        d_hn = jnp.zeros((ts, D_MODEL), F32)
        for c in range(D_FF // FF_CHUNK):
            cols = slice(c * FF_CHUNK, (c + 1) * FF_CHUNK)
            d_act = _nt(dhb, wd_ref[cols, :])
            gtv = gt_ref[:, cols].astype(F32)
            upv = up_ref[:, cols].astype(F32)
            sg = jax.nn.sigmoid(gtv)
            silu = gtv * sg
            act_ref[:, cols] = (silu * upv).astype(BF16)
            d_up = (d_act * silu).astype(BF16)
            d_gt = ((d_act * upv) * (sg * (1.0 + gtv * (1.0 - sg)))).astype(BF16)
            dup_ref[:, cols] = d_up
            dgt_ref[:, cols] = d_gt
            d_hn = (_nn(d_gt, wg_ref[cols, :]) + _nn(d_up, wu_ref[cols, :])) + d_hn
        dwd_ref[...] += _tn(act_ref[...], dhb)
        h1v = h1_ref[...]
        d_x, d_g = _rms_bwd(d_hn, h1v, _rms(h1v), g_ref[...])
        dg_ref[...] += d_g
        dh1 = dh_ref[...] + d_x
        dh1_ref[...] = dh1
        dh1b_ref[...] = dh1.astype(BF16)

    return _call(
        body,
        (dh2, h1, gt, up, g_ffn, wg_t, wu_t, w_down),
        name="ffn_bwd_act",
        grid=(s // ts,),
        in_specs=[
            _rows(ts, D_MODEL),
            _rows(ts, D_MODEL),
            _rows(ts, D_FF),
            _rows(ts, D_FF),
            _resident((1, D_MODEL)),
            _resident((D_FF, D_MODEL)),
            _resident((D_FF, D_MODEL)),
            _resident((D_FF, D_MODEL)),
        ],
        out_specs=[
            _rows(ts, D_FF), _rows(ts, D_FF),
            _rows(ts, D_MODEL), _rows(ts, D_MODEL), _acc((1, D_MODEL)), _acc((D_FF, D_MODEL)),
        ],
        out_shape=[
            jax.ShapeDtypeStruct((s, D_FF), BF16),
            jax.ShapeDtypeStruct((s, D_FF), BF16),
            jax.ShapeDtypeStruct((s, D_MODEL), F32),
            jax.ShapeDtypeStruct((s, D_MODEL), BF16),
            jax.ShapeDtypeStruct((1, D_MODEL), F32),
            jax.ShapeDtypeStruct((D_FF, D_MODEL), F32),
        ],
        scratch_shapes=[pltpu.VMEM((ts, D_FF), BF16)],
    )


def _ffn_bwd_w(dgt, dup, hn2):
    s = hn2.shape[0]
    slab = pl.BlockSpec((s, FFN_W_SLAB), lambda i: (0, i))

    def body(dgt_ref, dup_ref, hn_ref, dwg_ref, dwu_ref):
        hn = hn_ref[...]
        dwg_ref[...] = _tn(dgt_ref[...], hn)
        dwu_ref[...] = _tn(dup_ref[...], hn)

    return _call(
        body,
        (dgt, dup, hn2),
        name="ffn_bwd_w",
        grid=(D_FF // FFN_W_SLAB,),
        in_specs=[slab, slab, _resident((s, D_MODEL))],
        out_specs=[_rows(FFN_W_SLAB, D_MODEL)] * 2,
        out_shape=[jax.ShapeDtypeStruct((D_FF, D_MODEL), F32)] * 2,
    )


def _mix_bwd(dh1b, u, a, m, w_out, w_pool, pool_scale):
    s = u.shape[0]
    ts = min(TOKEN_TILE, s)
    nt = s // ts
    halo_after = lambda i: (jnp.minimum((i + 1) * (ts // POOL_HALO), s // POOL_HALO - 1), 0)
    n_groups = len(POOL_SIZES)

    def body(dh_ref, dhn_ref, u_ref, uh_ref, a_ref, m_ref, wo_ref, wp_ref, sc_ref,
             da_ref, du_ref, dwp_ref, dsc_ref, dwo_ref):
        i = pl.program_id(0)

        @pl.when(i == 0)
        def _():
            dwp_ref[...] = jnp.zeros_like(dwp_ref)
            dsc_ref[...] = jnp.zeros_like(dsc_ref)
            dwo_ref[...] = jnp.zeros_like(dwo_ref)

        dh = dh_ref[...]
        dwo_ref[:ATTN_WIDTH, :] += _tn(a_ref[...], dh)
        dwo_ref[ATTN_WIDTH:, :] += _tn(m_ref[...], dh)
        da_ref[...] = _nt(dh, wo_ref[:ATTN_WIDTH, :])
        dh_next = jnp.where(i < nt - 1, dhn_ref[...], jnp.zeros_like(dhn_ref))
        dm_ext = _nt(jnp.concatenate([dh, dh_next], axis=0), wo_ref[ATTN_WIDTH:, :])
        pooled = _pooled(u_ref[...], uh_ref[...], i, ts)
        t_ext = i * ts + lax.broadcasted_iota(jnp.int32, (ts + POOL_HALO, 1), 0)
        for g, w in enumerate(POOL_SIZES):
            lanes = slice(g * POOL_GROUP, (g + 1) * POOL_GROUP)
            wp = wp_ref[g].astype(BF16)
            pg = pooled[g].astype(BF16)
            dm_g = dm_ext[:, lanes]
            dsc_ref[:, lanes] += jnp.sum(dm_g[:ts, :] * _nn(pg, wp), axis=0, keepdims=True)
            dy = (dm_g * sc_ref[:, lanes]).astype(BF16)
            dwp_ref[g] += _tn(pg, dy[:ts, :])
            d_pool = _nt(dy, wp)
            acc = d_pool / jnp.minimum(t_ext + 1, w).astype(F32)
            shift = 1
            while shift < w:
                acc = acc + pltpu.roll(acc, ts + POOL_HALO - shift, 0)
                shift *= 2
            du_ref[:, lanes] = (acc[:ts, :] - d_pool[:ts, :]).astype(BF16)

    return _call(
        body,
        (dh1b, dh1b, u, u, a, m, w_out, w_pool, pool_scale),
        name="mix_bwd",
        grid=(nt,),
        in_specs=[
            _rows(ts, D_MODEL),
            pl.BlockSpec((POOL_HALO, D_MODEL), halo_after),
            _rows(ts, POOL_WIDTH),
            pl.BlockSpec((POOL_HALO, POOL_WIDTH), _halo_before(ts)),
            _rows(ts, ATTN_WIDTH),
            _rows(ts, POOL_WIDTH),
            _resident((D_MODEL, D_MODEL)),
            _resident((n_groups, POOL_GROUP, POOL_GROUP)),
            _resident((1, POOL_WIDTH)),
        ],
        out_specs=[
            _rows(ts, ATTN_WIDTH),
            _rows(ts, POOL_WIDTH),
            _acc((n_groups, POOL_GROUP, POOL_GROUP)),
            _acc((1, POOL_WIDTH)),
            _acc((D_MODEL, D_MODEL)),
        ],
        out_shape=[
            jax.ShapeDtypeStruct((s, ATTN_WIDTH), F32),
            jax.ShapeDtypeStruct((s, POOL_WIDTH), BF16),
            jax.ShapeDtypeStruct((n_groups, POOL_GROUP, POOL_GROUP), F32),
            jax.ShapeDtypeStruct((1, POOL_WIDTH), F32),
            jax.ShapeDtypeStruct((D_MODEL, D_MODEL), F32),
        ],
    )


def _attn_bwd(qn, kn, v, a, da, tab, sinks):
    s = qn.shape[0]
    qb = ATTN_STEP_BLOCKS
    rows = qb * BLOCK
    n_groups = s // rows
    group, prev = _attn_specs(n_groups)
    done = lambda n: (jnp.maximum(n - 1, 0), 0)

    def body(sink_ref, q_ref, kc_ref, kp_ref, vc_ref, vp_ref, o_ref, do_ref, tab_ref,
             dq_ref, dk_ref, dv_ref, dl_ref, ds_ref, k_carry, v_carry, sink_acc):
        n = pl.program_id(0)

        @pl.when(n == 0)
        def _():
            dl_ref[...] = jnp.zeros_like(dl_ref)
            k_carry[...] = jnp.zeros_like(k_carry)
            v_carry[...] = jnp.zeros_like(v_carry)
            sink_acc[...] = jnp.zeros_like(sink_acc)

        @pl.when(n < n_groups)
        def _():
            first = n == 0
            lo_mask = _lane_lo((BLOCK, BLOCK))
            chains = [(b, half) for b in range(qb) for half in range(2)]
            tile = lambda ref, b, p: ref[b * BLOCK:(b + 1) * BLOCK, p * BLOCK:(p + 1) * BLOCK]
            keys = [_band(kp_ref, kc_ref, b) for b in range(qb)]
            vals = [_band(vp_ref, vc_ref, b) for b in range(qb)]
            q_st = [_stack_heads([tile(q_ref, b, p) for p in range(4)], lo_mask) for b in range(qb)]
            do_st = [_stack_heads([tile(do_ref, b, p) for p in range(4)], lo_mask) for b in range(qb)]
            o_st = [_stack_heads([tile(o_ref, b, p).astype(F32) for p in range(4)], lo_mask) for b in range(qb)]
            sink_col = [_sink_column(sink_ref, heads) for heads in (HEADS_A, HEADS_B)]
            scores = {(b, h): _band_scores(q_st[b][h], keys[b][h], tab_ref[h], first if b == 0 else None)
                      for b, h in chains}
            dob = {(b, h): do_st[b][h].astype(BF16) for b, h in chains}
            d_probs = {(b, h): _nt(dob[b, h], vals[b][h]) for b, h in chains}
            delta = {(b, h): jnp.sum(do_st[b][h] * o_st[b][h], axis=-1, keepdims=True) for b, h in chains}
            soft = {(b, h): _softmax_with_sink(scores[b, h], sink_col[h]) for b, h in chains}
            dl = {(b, h): soft[b, h][0] * (d_probs[b, h] - delta[b, h]) for b, h in chains}
            for b, h in chains:
                dl_ref[h] += dl[b, h]
                sink_acc[h] += soft[b, h][1] * delta[b, h]
            dsb = {(b, h): (dl[b, h] * (HEAD_DIM ** -0.5)).astype(BF16) for b, h in chains}
            dq_st = {(b, h): _nn(dsb[b, h], keys[b][h]) for b, h in chains}
            dk_parts = {(b, h): _tn(dsb[b, h], q_st[b][h]) for b, h in chains}
            dv_parts = {(b, h): _tn(soft[b, h][0].astype(BF16), dob[b, h]) for b, h in chains}
            for b in range(qb):
                dq = _unstack_heads(dq_st[b, 0], dq_st[b, 1], lo_mask)
                for p in range(4):
                    dq_ref[b * BLOCK:(b + 1) * BLOCK, p * BLOCK:(p + 1) * BLOCK] = dq[p]
            dks = [dk_parts[b, 0] + pltpu.roll(dk_parts[b, 1], HEAD_DIM, 1) for b in range(qb)]
            dvs = [dv_parts[b, 0] + pltpu.roll(dv_parts[b, 1], HEAD_DIM, 1) for b in range(qb)]
            last = slice((qb - 1) * BLOCK, qb * BLOCK)
            for parts, out_ref, carry in ((dks, dk_ref, k_carry), (dvs, dv_ref, v_carry)):
                out_ref[...] = carry[...]
                out_ref[last, :] += parts[0][:BLOCK, :]
                for b in range(qb):
                    own = parts[b][BLOCK:, :]
                    carry[b * BLOCK:(b + 1) * BLOCK, :] = own + parts[b + 1][:BLOCK, :] if b + 1 < qb else own

        @pl.when(n == n_groups)
        def _():
            dk_ref[...] = k_carry[...]
            dv_ref[...] = v_carry[...]
            for half, heads in enumerate((HEADS_A, HEADS_B)):
                for slot, h in enumerate(heads):
                    tot = jnp.sum(sink_acc[half, slot * BLOCK:(slot + 1) * BLOCK, :], axis=0, keepdims=True)
                    ds_ref[h:h + 1, :] = jnp.broadcast_to(-tot, (1, SMALL_LANES))

    return _call(
        body,
        (sinks, qn, kn, kn, v, v, a, da, tab),
        name="attn_bwd",
        grid=(n_groups + 1,),
        in_specs=[
            pl.BlockSpec(memory_space=pltpu.SMEM),
            pl.BlockSpec((rows, ATTN_WIDTH), group),
            pl.BlockSpec((rows, KV_WIDTH), group),
            pl.BlockSpec((BLOCK, KV_WIDTH), prev),
            pl.BlockSpec((rows, KV_WIDTH), group),
            pl.BlockSpec((BLOCK, KV_WIDTH), prev),
            pl.BlockSpec((rows, ATTN_WIDTH), group),
            pl.BlockSpec((rows, ATTN_WIDTH), group),
            _resident((2, 4 * BLOCK, 2 * BLOCK)),
        ],
        out_specs=[
            pl.BlockSpec((rows, ATTN_WIDTH), group),
            pl.BlockSpec((rows, KV_WIDTH), done),
            pl.BlockSpec((rows, KV_WIDTH), done),
            _acc((2, 4 * BLOCK, 2 * BLOCK)),
            _acc((N_DEV, SMALL_LANES)),
        ],
        out_shape=[
            jax.ShapeDtypeStruct((s, ATTN_WIDTH), F32),
            jax.ShapeDtypeStruct((s, KV_WIDTH), F32),
            jax.ShapeDtypeStruct((s, KV_WIDTH), F32),
            jax.ShapeDtypeStruct((2, 4 * BLOCK, 2 * BLOCK), F32),
            jax.ShapeDtypeStruct((N_DEV, SMALL_LANES), F32),
        ],
        scratch_shapes=[
            pltpu.VMEM((rows, KV_WIDTH), F32),
            pltpu.VMEM((rows, KV_WIDTH), F32),
            pltpu.VMEM((2, 4 * BLOCK, 1), F32),
        ],
    )


def _fold_heads(acc):
    t = acc + pltpu.roll(acc, HEAD_DIM, 1)
    out = t[:, :SMALL_LANES]
    for g in range(1, acc.shape[1] // SMALL_LANES):
        out = out + t[:, g * SMALL_LANES:(g + 1) * SMALL_LANES]
    return out


def _in_proj_bwd(dqn, dkn, dv, du, zqk, x, dh1, g_attn, gq_t, gk_t, w_in_t):
    s = x.shape[0]
    ts = min(TOKEN_TILE, s)
    nt = s // ts

    def head_norm_bwd(d_n, raw, g_t, bmat):
        r = lax.rsqrt(_seg_mean(raw * raw, bmat) + EPS)
        gy = d_n * g_t
        d_raw = r * gy - raw * (r * r * r) * _seg_mean(gy * raw, bmat)
        return d_raw, jnp.sum(d_n * (raw * r), axis=0, keepdims=True)

    def body(dqn_ref, dkn_ref, dv_ref, du_ref, zqk_ref, x_ref, dh1_ref, g_ref, gq_ref, gk_ref, w_ref, bq_ref, bk_ref,
             gx_ref, dw_ref, dg_ref, dgq_ref, dgk_ref, dz_ref, gq_acc, gk_acc):
        i = pl.program_id(0)

        @pl.when(i == 0)
        def _():
            dw_ref[...] = jnp.zeros_like(dw_ref)
            dg_ref[...] = jnp.zeros_like(dg_ref)
            gq_acc[...] = jnp.zeros_like(gq_acc)
            gk_acc[...] = jnp.zeros_like(gk_acc)

        d_q, d_gq = head_norm_bwd(dqn_ref[...], zqk_ref[:, :ATTN_WIDTH], gq_ref[...], bq_ref[...])
        d_k, d_gk = head_norm_bwd(dkn_ref[...], zqk_ref[:, ATTN_WIDTH:], gk_ref[...], bk_ref[...])
        gq_acc[...] += d_gq
        gk_acc[...] += d_gk
        dz_ref[:, :ATTN_WIDTH] = d_q.astype(BF16)
        dz_ref[:, ATTN_WIDTH:ATTN_WIDTH + KV_WIDTH] = d_k.astype(BF16)
        dz_ref[:, ATTN_WIDTH + KV_WIDTH:ATTN_WIDTH + 2 * KV_WIDTH] = dv_ref[...].astype(BF16)
        dz_ref[:, ATTN_WIDTH + 2 * KV_WIDTH:] = du_ref[...]
        dz = dz_ref[...]
        xf = x_ref[...]
        r = _rms(xf)
        hn = ((xf * r) * g_ref[...]).astype(BF16)
        d_x, d_g = _rms_bwd(_nn(dz, w_ref[...]), xf, r, g_ref[...])
        dg_ref[...] += d_g
        gx_ref[...] = dh1_ref[...] + d_x
        dw_ref[...] += _tn(dz, hn)

        @pl.when(i == nt - 1)
        def _():
            dgq_ref[...] = _fold_heads(gq_acc[...])
            dgk_ref[...] = _fold_heads(gk_acc[...])

    return _call(
        body,
        (dqn, dkn, dv, du, zqk, x, dh1, g_attn, gq_t, gk_t, w_in_t,
      _head_mean_matrix(ATTN_WIDTH), _head_mean_matrix(KV_WIDTH)),
        name="in_proj_bwd",
        grid=(nt,),
        in_specs=[
            _rows(ts, ATTN_WIDTH),
            _rows(ts, KV_WIDTH),
            _rows(ts, KV_WIDTH),
            _rows(ts, POOL_WIDTH),
            _rows(ts, ATTN_WIDTH + KV_WIDTH),
            _rows(ts, D_MODEL),
            _rows(ts, D_MODEL),
            _resident((1, D_MODEL)),
            _resident((1, ATTN_WIDTH)),
            _resident((1, KV_WIDTH)),
            _resident((IN_WIDTH, D_MODEL)),
            _resident((ATTN_WIDTH, ATTN_WIDTH)),
            _resident((KV_WIDTH, KV_WIDTH)),
        ],
        out_specs=[
            _rows(ts, D_MODEL),
            _acc((IN_WIDTH, D_MODEL)),
            _acc((1, D_MODEL)),
            _acc((1, SMALL_LANES)),
            _acc((1, SMALL_LANES)),
        ],
        out_shape=[
            jax.ShapeDtypeStruct((s, D_MODEL), F32),
            jax.ShapeDtypeStruct((IN_WIDTH, D_MODEL), F32),
            jax.ShapeDtypeStruct((1, D_MODEL), F32),
            jax.ShapeDtypeStruct((1, SMALL_LANES), F32),
            jax.ShapeDtypeStruct((1, SMALL_LANES), F32),
        ],
        scratch_shapes=[
            pltpu.VMEM((ts, IN_WIDTH), BF16),
            pltpu.VMEM((1, ATTN_WIDTH), F32),
            pltpu.VMEM((1, KV_WIDTH), F32),
        ],
    )


BIG_WEIGHTS = (
    ("w_in", True, IN_WIDTH // N_DEV, D_MODEL),
    ("w_out", False, D_MODEL // N_DEV, D_MODEL),
    ("w_gate", True, D_FF // N_DEV, D_MODEL),
    ("w_up", True, D_FF // N_DEV, D_MODEL),
    ("w_down", False, D_FF // N_DEV, D_MODEL),
    ("w_ple_gate", False, D_MODEL // N_DEV, D_MODEL),
    ("w_ple_proj", False, PLE_DIM, D_MODEL // N_DEV),
)
N_BIG = len(BIG_WEIGHTS)


def _place():
    x, y, c = lax.axis_index("x"), lax.axis_index("y"), lax.axis_index("c")
    chips = [(1 - x, y), (x, 1 - y), (1 - x, 1 - y)]
    return x, y, c, chips


class _Gather:
    def __init__(self, n):
        self.n = n
        self.sems = [pltpu.SemaphoreType.DMA((n, 7)), pltpu.SemaphoreType.DMA((n, 7)), pltpu.SemaphoreType.DMA((n,))]

    def _ctx(self, srcs, outs, sems):
        send_sems, recv_sems, local_sems = sems
        x, y, c, chips = _place()
        me, sibling = (x, y, c), (x, y, 1 - c)

        def block(k, owner):
            px, py, pc = owner
            return outs[k].at[4 * px + 2 * py + pc]

        def copy(k, idx, owner, to, mine=False):
            return pltpu.make_async_remote_copy(
                src_ref=srcs[k] if mine else block(k, owner), dst_ref=block(k, owner),
                send_sem=send_sems.at[k, idx], recv_sem=recv_sems.at[k, idx], device_id=to, device_id_type=MESH)

        def local(k):
            return pltpu.make_async_copy(srcs[k], block(k, me), local_sems.at[k])

        return c, chips, me, sibling, copy, local

    def begin(self, srcs, outs, sems):
        c, chips, me, sibling, copy, local = self._ctx(srcs, outs, sems)
        for k in range(self.n):
            local(k).start()
            copy(k, 0, me, sibling, mine=True).start()
            for j, chip in enumerate(chips):
                copy(k, 1 + j, me, (*chip, c), mine=True).start()

    def middle(self, srcs, outs, sems):
        c, chips, me, sibling, copy, local = self._ctx(srcs, outs, sems)
        for j, chip in enumerate(chips):
            for k in range(self.n):
                copy(k, 1 + j, (*chip, c), me).wait_recv()
                copy(k, 4 + j, (*chip, c), sibling).start()

    def end(self, srcs, outs, sems):
        c, chips, me, sibling, copy, local = self._ctx(srcs, outs, sems)
        for k in range(self.n):
            copy(k, 0, sibling, me).wait_recv()
            for j, chip in enumerate(chips):
                copy(k, 4 + j, (*chip, 1 - c), me).wait_recv()
        for k in range(self.n):
            copy(k, 0, me, sibling, mine=True).wait_send()
            for j, chip in enumerate(chips):
                copy(k, 1 + j, me, (*chip, c), mine=True).wait_send()
                copy(k, 4 + j, (*chip, c), sibling).wait_send()
            local(k).wait()


def _gather_rider(shards):
    g = _Gather(len(shards))
    shapes = [jax.ShapeDtypeStruct((N_DEV, *sh.shape), sh.dtype) for sh in shards]
    return _Rider(shards, shapes, g.sems, g.begin, g.end, g.middle)


def _cast_and_gather_first(shards, rel_bias_t):
    g = _Gather(1)
    any_spec = pl.BlockSpec(memory_space=pl.ANY)
    vmem = pl.BlockSpec(memory_space=pltpu.VMEM)

    def body(*refs):
        ins, rb_ref, outs = refs[:N_BIG], refs[N_BIG], refs[N_BIG + 1:2 * N_BIG + 1]
        gathered, tab_ref, sems = refs[2 * N_BIG + 1], refs[2 * N_BIG + 2], refs[2 * N_BIG + 3:]
        outs[0][...] = ins[0][...].astype(BF16)
        g.begin(outs[:1], [gathered], sems)
        for k in range(1, N_BIG):
            outs[k][...] = ins[k][...].astype(BF16)
        _write_bias_table(rb_ref, tab_ref)
        g.middle(outs[:1], [gathered], sems)
        g.end(outs[:1], [gathered], sems)

    res = pl.pallas_call(
        body,
        name="cast_and_gather_first",
        in_specs=[vmem] * N_BIG + [pl.BlockSpec(memory_space=pltpu.SMEM)],
        out_specs=[vmem] * N_BIG + [any_spec, vmem],
        out_shape=[jax.ShapeDtypeStruct((r, c), BF16) for _, _, r, c in BIG_WEIGHTS]
        + [jax.ShapeDtypeStruct((N_DEV, *BIG_WEIGHTS[0][2:]), BF16), jax.ShapeDtypeStruct(BIAS_TABLE_SHAPE, F32)],
        scratch_shapes=g.sems,
    )(*shards, rel_bias_t)
    return list(res[:N_BIG]), res[N_BIG], res[N_BIG + 1]


def _sibling_rider(grads):
    n = len(grads)

    def copies(gs, lands, sems):
        send_sems, recv_sems = sems
        x, y, c, _ = _place()
        return [
            pltpu.make_async_remote_copy(
                src_ref=gs[k].at[:, 1 - c], dst_ref=lands[k], send_sem=send_sems.at[k], recv_sem=recv_sems.at[k],
                device_id=(x, y, 1 - c), device_id_type=MESH)
            for k in range(n)
        ]

    def begin(gs, lands, sems):
        for cp in copies(gs, lands, sems):
            cp.start()

    def end(gs, lands, sems):
        for cp in copies(gs, lands, sems):
            cp.wait()

    shapes = [jax.ShapeDtypeStruct((N_CHIPS, *g.shape[2:]), F32) for g in grads]
    return _Rider(grads, shapes, [pltpu.SemaphoreType.DMA((n,)), pltpu.SemaphoreType.DMA((n,))], begin, end)


def _chip_of_relation(j, place):
    x, y = place[0], place[1]
    return jnp.where(j == 0, 2 * (1 - x) + y, jnp.where(j == 1, 2 * x + 1 - y, 2 * (1 - x) + 1 - y))


def _chip_sum(ks, place, grads, from_sibling):
    shapes = [BIG_WEIGHTS[k][2:] for k in ks]
    n_rel = N_CHIPS - 1
    per_step = n_rel if 2 * n_rel * sum(4 * r * c for r, c in shapes) <= CHIP_SUM_ONE_STEP_BYTES else 1
    operands, specs = [], []
    for (r, c), g, l in zip(shapes, grads, from_sibling):
        for q in range(per_step):
            chip = lambda j, place, q=q: _chip_of_relation(j * per_step + q, place)
            operands += [g, l]
            specs += [pl.BlockSpec((1, 1, r, c), lambda j, place, chip=chip: (chip(j, place), place[2], 0, 0)),
                      pl.BlockSpec((1, r, c), lambda j, place, chip=chip: (chip(j, place), 0, 0))]
    args, in_specs, _ = _after_last(operands, specs)

    def body(place_ref, *refs):
        ins, outs = refs[:len(operands)], refs[len(args):]
        for i in range(len(ks)):
            for q in range(per_step):
                mine_ref, sib_ref = ins[2 * (i * per_step + q):2 * (i * per_step + q) + 2]
                outs[i][q] = (mine_ref[0, 0] + sib_ref[0]).astype(BF16)

    outs = pl.pallas_call(
        body,
        name="chip_sum_" + "_".join(BIG_WEIGHTS[k][0] for k in ks),
        grid_spec=pltpu.PrefetchScalarGridSpec(
            num_scalar_prefetch=1,
            grid=(n_rel // per_step,),
            in_specs=in_specs,
            out_specs=[pl.BlockSpec((per_step, r, c), lambda j, place: (j, 0, 0)) for r, c in shapes],
        ),
        out_shape=[jax.ShapeDtypeStruct((n_rel, r, c), BF16) for r, c in shapes],
    )(place, *args)
    _mark_issued(outs[0])
    return list(outs)


def _chips_rider(to_send, small=None):
    n = len(to_send)
    inputs = list(to_send) + ([] if small is None else [small])
    shapes = [jax.ShapeDtypeStruct((3, *t.shape[1:]), BF16) for t in to_send]
    sems = [pltpu.SemaphoreType.DMA((max(n, 1), 3)), pltpu.SemaphoreType.DMA((max(n, 1), 3))]
    if small is not None:
        shapes.append(jax.ShapeDtypeStruct((N_DEV, *small.shape), F32))
        sems += [pltpu.SemaphoreType.DMA((7,)), pltpu.SemaphoreType.DMA((7,)), pltpu.SemaphoreType.DMA]

    def copies(ins, outs, sem_refs):
        x, y, c, chips = _place()
        out = []
        for k in range(n):
            for j, (px, py) in enumerate(chips):
                out.append(pltpu.make_async_remote_copy(
                    src_ref=ins[k].at[j], dst_ref=outs[k].at[j],
                    send_sem=sem_refs[0].at[k, j], recv_sem=sem_refs[1].at[k, j],
                    device_id=(px, py, c), device_id_type=MESH))
        local = None
        if small is not None:
            me = 4 * x + 2 * y + c
            local = pltpu.make_async_copy(ins[n], outs[n].at[me], sem_refs[4])
            rel = 0
            for fx in (0, 1):
                for fy in (0, 1):
                    for fc in (0, 1):
                        if (fx, fy, fc) != (0, 0, 0):
                            out.append(pltpu.make_async_remote_copy(
                                src_ref=ins[n], dst_ref=outs[n].at[me],
                                send_sem=sem_refs[2].at[rel], recv_sem=sem_refs[3].at[rel],
                                device_id=(x ^ fx, y ^ fy, c ^ fc), device_id_type=MESH))
                            rel += 1
        return out, local

    def begin(ins, outs, sem_refs):
        remote, local = copies(ins, outs, sem_refs)
        if local is not None:
            local.start()
        for cp in remote:
            cp.start()

    def end(ins, outs, sem_refs):
        remote, local = copies(ins, outs, sem_refs)
        for cp in remote:
            cp.wait()
        if local is not None:
            local.wait()

    return _Rider(inputs, shapes, sems, begin, end)


def _together(first, second):
    cut = lambda refs, a, b: (refs[:len(a)], refs[len(a):len(a) + len(b)])

    def run(which):
        def fn(ins, outs, sems):
            parts = zip((first, second), cut(ins, first.inputs, second.inputs),
                        cut(outs, first.out_shapes, second.out_shapes), cut(sems, first.sems, second.sems))
            for rider, i, o, s in parts:
                hook = getattr(rider, which)
                if hook is not None:
                    hook(i, o, s)
        return fn

    return _Rider(first.inputs + second.inputs, first.out_shapes + second.out_shapes, first.sems + second.sems,
                  run("begin"), run("end"), run("middle"))


PEER_SETS = {"sibling": 1, "chips": 2, "sibling+chips": 3, "all": 4}


def _peers(pattern):
    x, y, c, chips = _place()
    sibling, others = [(x, y, 1 - c)], [(*chip, c) for chip in chips]
    if pattern == "all":
        return sibling + others + [(*chip, 1 - c) for chip in chips]
    return {"sibling": sibling, "chips": others, "sibling+chips": sibling + others}[pattern]


def _on_sequencer(name, pattern, rider):
    n_in, n_out = len(rider.inputs), len(rider.out_shapes)

    def body(*refs):
        ins, outs, sems = refs[:n_in], refs[n_in:n_in + n_out], refs[n_in + n_out:]
        peers = _peers(pattern)
        barrier = pltpu.get_barrier_semaphore()
        for peer in peers:
            pl.semaphore_signal(barrier, inc=1, device_id=peer, device_id_type=MESH)
        pl.semaphore_wait(barrier, len(peers))
        rider.begin(ins, outs, sems)
        if rider.middle is not None:
            rider.middle(ins, outs, sems)
        rider.end(ins, outs, sems)

    outs = pl.kernel(
        body,
        name=name,
        out_type=tuple(rider.out_shapes),
        mesh=plsc.ScalarSubcoreMesh(axis_name="sequencer", num_cores=1),
        scratch_types=tuple(rider.sems),
        compiler_params=pltpu.CompilerParams(collective_id=PEER_SETS[pattern]),
    )(*rider.inputs)
    return list(outs)


def _adamw(w, g, m, v):
    m = ADAM_B1 * m + (1.0 - ADAM_B1) * g
    v = ADAM_B2 * v + (1.0 - ADAM_B2) * jnp.square(g)
    m_hat = m / (1.0 - ADAM_B1 ** ADAM_STEP)
    v_hat = v / (1.0 - ADAM_B2 ** ADAM_STEP)
    delta = -ADAM_LR * (m_hat / (jnp.sqrt(v_hat) + ADAM_EPS) + ADAM_WD * w)
    return delta, m, v


def _adamw_big(ks, place, operands):
    n = len(ks)
    tiles = lambda i, place: (i, 0)
    in_specs, out_specs, out_shape = [], [], []
    for k in ks:
        _, _, r, c = BIG_WEIGHTS[k]
        tile = r // 2
        in_specs += [
            pl.BlockSpec((1, 1, tile, c), lambda i, place: (2 * place[0] + place[1], place[2], i, 0)),
            pl.BlockSpec((1, tile, c), lambda i, place: (2 * place[0] + place[1], i, 0)),
            pl.BlockSpec((3, tile, c), lambda i, place: (0, i, 0)),
        ] + [pl.BlockSpec((tile, c), tiles)] * 3
        out_specs += [pl.BlockSpec((tile, c), tiles)] * 4
        out_shape += [jax.ShapeDtypeStruct((r, c), F32)] * 4
    args, in_specs, _ = _after_last(sum((list(ops) for ops in operands), []), in_specs)

    def body(place_ref, *refs):
        ins, outs = refs[:6 * n], refs[len(args):]
        for i in range(n):
            mine_ref, sib_ref, land_ref, w_ref, m_ref, v_ref = ins[6 * i:6 * i + 6]
            g_ref, d_ref, nm_ref, nv_ref = outs[4 * i:4 * i + 4]
            g = mine_ref[0, 0] + sib_ref[0]
            g = ((g + land_ref[0].astype(F32)) + land_ref[1].astype(F32)) + land_ref[2].astype(F32)
            g_ref[...] = g
            d_ref[...], nm_ref[...], nv_ref[...] = _adamw(w_ref[...], g, m_ref[...], v_ref[...])

    outs = pl.pallas_call(
        body,
        name="adamw_" + "_".join(BIG_WEIGHTS[k][0] for k in ks),
        grid_spec=pltpu.PrefetchScalarGridSpec(
            num_scalar_prefetch=1, grid=(2,), in_specs=in_specs, out_specs=out_specs),
        out_shape=out_shape,
    )(place, *args)
    _mark_issued(outs[0])
    return [outs[4 * i:4 * i + 4] for i in range(n)]


def _pack_small(arrays):
    rows, offsets = [], []
    at = 0
    for a in arrays:
        if a.ndim != 2 or a.shape[1] != SMALL_LANES or a.shape[0] % 8:
            flat = a.reshape(-1)
            n_rows = -(-flat.shape[0] // (8 * SMALL_LANES)) * 8
            a = jnp.pad(flat, (0, n_rows * SMALL_LANES - flat.shape[0])).reshape(n_rows, SMALL_LANES)
        rows.append(a)
        offsets.append(at)
        at += a.shape[0]
    return jnp.concatenate(rows, axis=0), offsets


def _unpack_small(tot, at, shape):
    r, c = shape
    if r % 8 == 0:
        return tot[at:at + r, :c]
    assert r == 1
    if c <= SMALL_LANES:
        return tot[at:at + 1, :c]
    return jnp.concatenate([tot[at + j:at + j + 1, :] for j in range(c // SMALL_LANES)], axis=1)


def _small_update(packs, loss_at, grads_at, ws, ms, vs):
    n, n_packs = len(ws), len(packs)

    def body(*refs):
        pack_refs, refs = refs[:n_packs], refs[n_packs:]
        w_refs, m_refs, v_refs, loss_ref, outs = refs[:n], refs[n:2 * n], refs[2 * n:3 * n], refs[3 * n], refs[3 * n + 1:]
        tots = []
        for p_ref in pack_refs:
            tot = p_ref[0]
            for j in range(1, N_DEV):
                tot = tot + p_ref[j]
            tots.append(tot)
        loss_ref[...] = _unpack_small(tots[loss_at[0]], loss_at[1], (1, 1))
        for i, (pack, at) in enumerate(grads_at):
            g = _unpack_small(tots[pack], at, w_refs[i].shape)
            outs[i][...] = g
            outs[n + i][...], outs[2 * n + i][...], outs[3 * n + i][...] = _adamw(
                w_refs[i][...], g, m_refs[i][...], v_refs[i][...])

    shapes = [jax.ShapeDtypeStruct(w.shape, F32) for w in ws]
    outs = pl.pallas_call(body, name="small_update", out_shape=[jax.ShapeDtypeStruct((1, 1), F32)] + shapes * 4)(
        *packs, *ws, *ms, *vs)
    return outs[0], outs[1:]


SMALL_NAMES = ("g_attn_norm", "g_q", "g_k", "attn_sinks", "rel_bias", "w_pool", "pool_scale", "g_ffn_norm", "g_ple_norm")


def kernel(x, p, w_in, w_out, g_attn_norm, g_q, g_k, attn_sinks, rel_bias, w_pool, pool_scale, g_ffn_norm, w_gate, w_up, w_down, g_ple_norm, w_ple_gate, w_ple_proj, loss_target, m_w_in, m_w_out, m_g_attn_norm, m_g_q, m_g_k, m_attn_sinks, m_rel_bias, m_w_pool, m_pool_scale, m_g_ffn_norm, m_w_gate, m_w_up, m_w_down, m_g_ple_norm, m_w_ple_gate, m_w_ple_proj, v_w_in, v_w_out, v_g_attn_norm, v_g_q, v_g_k, v_attn_sinks, v_rel_bias, v_w_pool, v_pool_scale, v_g_ffn_norm, v_w_gate, v_w_up, v_w_down, v_g_ple_norm, v_w_ple_gate, v_w_ple_proj):
    weights = dict(w_in=w_in, w_out=w_out, g_attn_norm=g_attn_norm, g_q=g_q, g_k=g_k, attn_sinks=attn_sinks,
                   rel_bias=rel_bias, w_pool=w_pool, pool_scale=pool_scale, g_ffn_norm=g_ffn_norm, w_gate=w_gate,
                   w_up=w_up, w_down=w_down, g_ple_norm=g_ple_norm, w_ple_gate=w_ple_gate, w_ple_proj=w_ple_proj)
    m_in = dict(w_in=m_w_in, w_out=m_w_out, g_attn_norm=m_g_attn_norm, g_q=m_g_q, g_k=m_g_k, attn_sinks=m_attn_sinks,
                rel_bias=m_rel_bias, w_pool=m_w_pool, pool_scale=m_pool_scale, g_ffn_norm=m_g_ffn_norm, w_gate=m_w_gate,
                w_up=m_w_up, w_down=m_w_down, g_ple_norm=m_g_ple_norm, w_ple_gate=m_w_ple_gate, w_ple_proj=m_w_ple_proj)
    v_in = dict(w_in=v_w_in, w_out=v_w_out, g_attn_norm=v_g_attn_norm, g_q=v_g_q, g_k=v_g_k, attn_sinks=v_attn_sinks,
                rel_bias=v_rel_bias, w_pool=v_w_pool, pool_scale=v_pool_scale, g_ffn_norm=v_g_ffn_norm, w_gate=v_w_gate,
                w_up=v_w_up, w_down=v_w_down, g_ple_norm=v_g_ple_norm, w_ple_gate=v_w_ple_gate, w_ple_proj=v_w_ple_proj)

    _issued.clear()
    xs = x[0]
    ps = p[0, 0]
    target = loss_target[0]
    wp = w_pool[0]
    gq_t = jnp.tile(g_q, (1, ATTN_WIDTH // HEAD_DIM))
    gk_t = jnp.tile(g_k, (1, KV_WIDTH // HEAD_DIM))

    def to_blocks(k, arr):
        return jnp.swapaxes(arr[0], 0, 1) if BIG_WEIGHTS[k][1] else arr[0]

    def from_blocks(k, arr):
        return (jnp.swapaxes(arr, 0, 1) if BIG_WEIGHTS[k][1] else arr)[None]

    IN, OUT, GATE, UP, DOWN, PG, PP = range(N_BIG)
    full = lambda g: g.reshape(N_DEV * g.shape[1], g.shape[2])
    halves = lambda k, g: g.reshape(N_CHIPS, 2, *BIG_WEIGHTS[k][2:])
    place = jnp.stack([lax.axis_index("x"), lax.axis_index("y"), lax.axis_index("c")]).astype(jnp.int32)

    sh, w_in_g, tab = _cast_and_gather_first(
        [to_blocks(k, weights[name]) for k, (name, _, _, _) in enumerate(BIG_WEIGHTS)], rel_bias.T)
    w_in_t = full(w_in_g)

    (w_out_g,) = _on_sequencer("gather_out", "sibling+chips", _gather_rider([sh[OUT]]))
    wg_g, wu_g = _on_sequencer("gather_gate_up", "sibling+chips", _gather_rider([sh[GATE], sh[UP]]))
    wd_g, w_pg_g, w_pp_g = _on_sequencer("gather_down_ple", "sibling+chips", _gather_rider([sh[DOWN], sh[PG], sh[PP]]))
    (zqk, qn, kn, v, u) = _in_proj(xs, g_attn_norm, w_in_t, gq_t, gk_t)
    (a,) = _attn_fwd(qn, kn, v, tab, attn_sinks)
    w_out_f = full(w_out_g)
    (h1, hn2, m_out) = _mix_out(u, a, xs, w_out_f, wp, pool_scale, g_ffn_norm)
    wg_t, wu_t = full(wg_g), full(wu_g)
    (gt, up) = _ffn_up(hn2, wg_t, wu_t)
    w_down_f = full(wd_g)

    partial, from_sibling, sums, landed = [None] * N_BIG, [None] * N_BIG, [None] * N_BIG, [None] * N_BIG

    def to_sibling(name, ks, grads):
        for k, g in zip(ks, grads):
            partial[k] = halves(k, g)
        got = _on_sequencer(name, "sibling", _sibling_rider([partial[k] for k in ks]))
        for k, g in zip(ks, got):
            from_sibling[k] = g

    def chip_sum(*ks):
        for k, s in zip(ks, _chip_sum(ks, place, [partial[k] for k in ks], [from_sibling[k] for k in ks])):
            sums[k] = s

    def to_chips(name, ks, small=None):
        got = _on_sequencer(name, "chips" if small is None else "all", _chips_rider([sums[k] for k in ks], small))
        for k, g in zip(ks, got):
            landed[k] = g
        return got[len(ks):]

    (loss_part, dh2, d_wpg, d_wpp, d_g_ple) = _ffn_down_ple(
        gt, up, h1, w_down_f, ps, target, g_ple_norm, full(w_pg_g), w_pp_g)
    to_sibling("sibling_ple", (PG, PP), (d_wpg, d_wpp))
    (dgt, dup, dh1, dh1b, d_g_ffn, d_wd) = _ffn_bwd_act(dh2, h1, gt, up, g_ffn_norm, wg_t, wu_t, w_down_f)
    to_sibling("sibling_down", (DOWN,), (d_wd,))
    chip_sum(PG, PP)
    to_chips("chips_ple", (PG, PP))
    chip_sum(DOWN)
    to_chips("chips_down", (DOWN,))
    (d_wg_t, d_wu_t) = _ffn_bwd_w(dgt, dup, hn2)
    to_sibling("sibling_gate_up", (GATE, UP), (d_wg_t, d_wu_t))
    _complete_before_next([landed[PG], landed[PP], landed[DOWN]])
    (da, du, d_wpool, d_scale, d_wo) = _mix_bwd(dh1b, u, a, m_out, w_out_f, wp, pool_scale)
    to_sibling("sibling_out", (OUT,), (d_wo,))
    chip_sum(GATE, UP)
    to_chips("chips_gate_up", (GATE, UP))
    (dqn, dkn, dv, dl_acc, d_sinks) = _attn_bwd(qn, kn, v, a, da, tab, attn_sinks)
    chip_sum(OUT)
    early, early_at = _pack_small([d_wpool.reshape(POOL_WIDTH, POOL_GROUP), d_scale, d_g_ffn, d_g_ple, loss_part[:, :1]])
    landed[OUT], early_all = _on_sequencer(
        "chips_out", "sibling+chips", _together(_chips_rider([sums[OUT]]), _gather_rider([early])))
    (grad_x, d_win_t, d_g_attn, d_gq, d_gk) = _in_proj_bwd(dqn, dkn, dv, du, zqk, xs, dh1, g_attn_norm, gq_t, gk_t, w_in_t)
    to_sibling("sibling_in", (IN,), (d_win_t,))
    _complete_before_next([landed[OUT], landed[GATE], landed[UP], early_all])
    (d_rel_t,) = _bias_table_bwd(dl_acc)
    chip_sum(IN)
    late, late_at = _pack_small([d_g_attn, d_gq[:, :HEAD_DIM], d_gk[:, :HEAD_DIM], d_sinks[:, 0], d_rel_t])
    (late_all,) = to_chips("chips_in", (IN,), late)

    out = {"grad": {}, "delta": {}, "new_m": {}, "new_v": {}}
    for ks in ((PG, PP, DOWN), (OUT, GATE, UP), (IN,)):
        names = [BIG_WEIGHTS[k][0] for k in ks]
        results = _adamw_big(ks, place, [
            (partial[k], from_sibling[k], landed[k], to_blocks(k, weights[n]), to_blocks(k, m_in[n]),
             to_blocks(k, v_in[n])) for k, n in zip(ks, names)])
        for k, name, res in zip(ks, names, results):
            for kind, r in zip(("grad", "delta", "new_m", "new_v"), res):
                out[kind][name] = from_blocks(k, r)
    def as_rows(name, arr):
        return arr.T if name == "rel_bias" else arr.reshape(POOL_WIDTH, POOL_GROUP) if name == "w_pool" else arr

    def from_rows(name, arr):
        return arr.T if name == "rel_bias" else arr.reshape(w_pool.shape) if name == "w_pool" else arr

    grads_at = dict(w_pool=(0, early_at[0]), pool_scale=(0, early_at[1]), g_ffn_norm=(0, early_at[2]),
                    g_ple_norm=(0, early_at[3]), g_attn_norm=(1, late_at[0]), g_q=(1, late_at[1]), g_k=(1, late_at[2]),
                    attn_sinks=(1, late_at[3]), rel_bias=(1, late_at[4]))
    loss, updates = _small_update(
        [early_all, late_all], (0, early_at[4]), [grads_at[n] for n in SMALL_NAMES],
        [as_rows(n, weights[n]) for n in SMALL_NAMES], [as_rows(n, m_in[n]) for n in SMALL_NAMES],
        [as_rows(n, v_in[n]) for n in SMALL_NAMES])
    loss = loss.reshape(())
    n_small = len(SMALL_NAMES)
    for j, kind in enumerate(("grad", "delta", "new_m", "new_v")):
        for i, name in enumerate(SMALL_NAMES):
            out[kind][name] = from_rows(name, updates[j * n_small + i])

    _issued.clear()
    order = ("w_in", "w_out", "g_attn_norm", "g_q", "g_k", "attn_sinks", "rel_bias", "w_pool", "pool_scale",
             "g_ffn_norm", "w_gate", "w_up", "w_down", "g_ple_norm", "w_ple_gate", "w_ple_proj")
    return (loss, grad_x[None], *[out["grad"][n] for n in order], *[out["delta"][n] for n in order],
            *[out["new_m"][n] for n in order], *[out["new_v"][n] for n in order])
```

```python
import math

import jax
import jax.numpy as jnp
import numpy as np
from jax import lax
from jax.experimental import pallas as pl
from jax.experimental.pallas import tpu as pltpu
from jax.experimental.pallas import tpu_sc as plsc

F32 = jnp.float32
BF16 = jnp.bfloat16
MESH = pl.DeviceIdType.MESH

D_MODEL = 1024
HEAD_DIM = 64
ATTN_WIDTH = 512
KV_WIDTH = 128
POOL_WIDTH = 512
POOL_SIZES = (2, 4, 8, 16)
POOL_GROUP = 128
POOL_HALO = 16
IN_WIDTH = 1280
D_FF = 2816
PLE_DIM = 256
BLOCK = 128
N_BUCKETS = 32
MAX_DISTANCE = 128
EPS = 1e-6
N_DEV = 8
N_CHIPS = 4

ADAM_LR = 0.001
ADAM_B1 = 0.9
ADAM_B2 = 0.999
ADAM_EPS = 1e-08
ADAM_WD = 0.01
ADAM_STEP = 10

TOKEN_TILE = 512
FFN_BWD_TILE = 256
FF_CHUNK = 256
CHIP_SUM_ONE_STEP_BYTES = 9 * 2 ** 20
FFN_W_SLAB = 256
ATTN_STEP_BLOCKS = 4
HEADS_A = (0, 2, 5, 7)
HEADS_B = (1, 3, 4, 6)
SMALL_LANES = 128


def _nn(a, b):
    return jnp.dot(a, b, preferred_element_type=F32)


def _nt(a, b):
    return lax.dot_general(a, b, (((1,), (1,)), ((), ())), preferred_element_type=F32)


def _tn(a, b):
    return lax.dot_general(a, b, (((0,), (0,)), ((), ())), preferred_element_type=F32)


def _resident(shape):
    nd = len(shape)
    return pl.BlockSpec(shape, lambda i, _nd=nd: (0,) * _nd, pipeline_mode=pl.Buffered(1))


def _rows(tile, width):
    return pl.BlockSpec((tile, width), lambda i: (i, 0))


def _acc(shape):
    nd = len(shape)
    return pl.BlockSpec(shape, lambda i, _nd=nd: (0,) * _nd)


def _head_mean_matrix(width):
    idx = np.arange(width) // HEAD_DIM
    return jnp.asarray((idx[:, None] == idx[None, :]).astype(np.float32) / HEAD_DIM, dtype=BF16)


def _seg_mean(v, bmat):
    hi = v.astype(BF16)
    lo = (v - hi.astype(F32)).astype(BF16)
    return _nn(hi, bmat) + _nn(lo, bmat)


def _rms(x):
    return lax.rsqrt(jnp.mean(x * x, axis=-1, keepdims=True) + EPS)


def _rms_bwd(d_y, x, r, g):
    gy = d_y * g
    d_x = r * gy - x * (r * r * r) * jnp.mean(gy * x, axis=-1, keepdims=True)
    d_g = jnp.sum(d_y * (x * r), axis=0, keepdims=True)
    return d_x, d_g


def _lane_lo(shape):
    return lax.broadcasted_iota(jnp.int32, shape, 1) < HEAD_DIM


class _Rider:
    def __init__(self, inputs, out_shapes, sems, begin, end, middle=None):
        self.inputs, self.out_shapes, self.sems = list(inputs), list(out_shapes), list(sems)
        self.begin, self.middle, self.end = begin, middle, end


_issued = []


def _after_last(args, in_specs):
    extra = list(_issued)
    return list(args) + extra, list(in_specs) + [pl.BlockSpec(memory_space=pl.ANY)] * len(extra), len(extra)


def _mark_issued(out):
    _issued[:] = [out]


def _complete_before_next(arrays):
    _issued.extend(arrays)


def _call(body, args, *, name, grid, in_specs, out_specs, out_shape, scratch_shapes=()):
    n_args = len(args)
    args, in_specs, _ = _after_last(args, in_specs)

    def ordered(*refs):
        body(*refs[:n_args], *refs[len(args):])

    outs = pl.pallas_call(ordered, name=name, grid=grid, in_specs=in_specs, out_specs=list(out_specs),
                          out_shape=list(out_shape), scratch_shapes=list(scratch_shapes))(*args)
    _mark_issued(outs[0])
    return list(outs)


def _in_proj(x, g_attn, w_in_t, gq_t, gk_t):
    s = x.shape[0]
    ts = min(TOKEN_TILE, s)

    def body(x_ref, g_ref, w_ref, gq_ref, gk_ref, bq_ref, bk_ref, zqk_ref, qn_ref, kn_ref, v_ref, u_ref):
        xf = x_ref[...]
        hn = ((xf * _rms(xf)) * g_ref[...]).astype(BF16)
        z = _nt(hn, w_ref[...])
        q = z[:, :ATTN_WIDTH]
        k = z[:, ATTN_WIDTH:ATTN_WIDTH + KV_WIDTH]
        zqk_ref[...] = z[:, :ATTN_WIDTH + KV_WIDTH]
        rq = lax.rsqrt(_seg_mean(q * q, bq_ref[...]) + EPS)
        qn_ref[...] = ((q * rq) * gq_ref[...]).astype(BF16)
        rk = lax.rsqrt(_seg_mean(k * k, bk_ref[...]) + EPS)
        kn_ref[...] = ((k * rk) * gk_ref[...]).astype(BF16)
        v_ref[...] = z[:, ATTN_WIDTH + KV_WIDTH:ATTN_WIDTH + 2 * KV_WIDTH].astype(BF16)
        u_ref[...] = z[:, ATTN_WIDTH + 2 * KV_WIDTH:]

    return _call(
        body,
        (x, g_attn, w_in_t, gq_t, gk_t, _head_mean_matrix(ATTN_WIDTH), _head_mean_matrix(KV_WIDTH)),
        name="in_proj",
        grid=(s // ts,),
        in_specs=[
            _rows(ts, D_MODEL),
            _resident((1, D_MODEL)),
            _resident((IN_WIDTH, D_MODEL)),
            _resident((1, ATTN_WIDTH)),
            _resident((1, KV_WIDTH)),
            _resident((ATTN_WIDTH, ATTN_WIDTH)),
            _resident((KV_WIDTH, KV_WIDTH)),
        ],
        out_specs=[
            _rows(ts, ATTN_WIDTH + KV_WIDTH),
            _rows(ts, ATTN_WIDTH),
            _rows(ts, KV_WIDTH),
            _rows(ts, KV_WIDTH),
            _rows(ts, POOL_WIDTH),
        ],
        out_shape=[
            jax.ShapeDtypeStruct((s, ATTN_WIDTH + KV_WIDTH), F32),
            jax.ShapeDtypeStruct((s, ATTN_WIDTH), BF16),
            jax.ShapeDtypeStruct((s, KV_WIDTH), BF16),
            jax.ShapeDtypeStruct((s, KV_WIDTH), BF16),
            jax.ShapeDtypeStruct((s, POOL_WIDTH), F32),
        ],
    )


def _bucket_ranges():
    n = np.arange(MAX_DISTANCE)
    max_exact = N_BUCKETS // 2
    nf = np.maximum(n, 1).astype(np.float64)
    large = max_exact + (np.log(nf / max_exact) / math.log(MAX_DISTANCE / max_exact) * (N_BUCKETS - max_exact)).astype(np.int64)
    bucket = np.where(n < max_exact, n, np.minimum(large, N_BUCKETS - 1))
    out = []
    for b in range(N_BUCKETS):
        idx = np.nonzero(bucket == b)[0]
        out.append((int(idx.min()), int(idx.max()) + 1))
    return out


def _band_distance():
    i = lax.broadcasted_iota(jnp.int32, (BLOCK, 2 * BLOCK), 0)
    j = lax.broadcasted_iota(jnp.int32, (BLOCK, 2 * BLOCK), 1)
    return BLOCK + i - j


BIAS_TABLE_SHAPE = (2, 4 * BLOCK, 2 * BLOCK)


def _write_bias_table(rb_ref, tab_ref):
    d = _band_distance()
    for half, heads in enumerate((HEADS_A, HEADS_B)):
        for slot, h in enumerate(heads):
            t = jnp.full((BLOCK, 2 * BLOCK), -jnp.inf, F32)
            for b, (lo, hi) in enumerate(_bucket_ranges()):
                t = jnp.where((d >= lo) & (d < hi), rb_ref[h, b], t)
            tab_ref[half, slot * BLOCK:(slot + 1) * BLOCK, :] = t


def _bias_table_bwd(dl_acc):
    ranges = _bucket_ranges()
    n_heads = len(HEADS_A) + len(HEADS_B)

    def body(dl_ref, out_ref):
        d = _band_distance()
        row = lax.broadcasted_iota(jnp.int32, (n_heads, SMALL_LANES), 0)
        lane = lax.broadcasted_iota(jnp.int32, (n_heads, SMALL_LANES), 1)
        out = jnp.zeros((n_heads, SMALL_LANES), F32)
        for b, (lo, hi) in enumerate(ranges):
            in_bucket = (d >= lo) & (d < hi)
            for half, heads in enumerate((HEADS_A, HEADS_B)):
                for slot, h in enumerate(heads):
                    g = dl_ref[half, slot * BLOCK:(slot + 1) * BLOCK, :]
                    part = jnp.sum(jnp.where(in_bucket, g, 0.0), axis=0, keepdims=True)
                    tot = jnp.sum(part, axis=1, keepdims=True)
                    out = jnp.where((row == h) & (lane == b), tot, out)
        out_ref[...] = out

    return _call(
        body,
        (dl_acc,),
        name="bias_table_bwd",
        grid=(1,),
        in_specs=[_acc((2, 4 * BLOCK, 2 * BLOCK))],
        out_specs=[_acc((n_heads, SMALL_LANES))],
        out_shape=[jax.ShapeDtypeStruct((n_heads, SMALL_LANES), F32)],
    )


def _stack_heads(pairs, lo_mask):
    zero = jnp.zeros_like(pairs[0])
    lo = [jnp.where(lo_mask, t, zero) for t in pairs]
    hi = [jnp.where(lo_mask, zero, t) for t in pairs]
    return (jnp.concatenate([lo[0], lo[1], hi[2], hi[3]], axis=0),
            jnp.concatenate([hi[0], hi[1], lo[2], lo[3]], axis=0))


def _unstack_heads(out_a, out_b, lo_mask):
    t = lambda x, r: x[r * BLOCK:(r + 1) * BLOCK, :]
    return [
        jnp.where(lo_mask, t(out_a, 0), t(out_b, 0)),
        jnp.where(lo_mask, t(out_a, 1), t(out_b, 1)),
        jnp.where(lo_mask, t(out_b, 2), t(out_a, 2)),
        jnp.where(lo_mask, t(out_b, 3), t(out_a, 3)),
    ]


def _sink_column(sink_ref, heads):
    row = lax.broadcasted_iota(jnp.int32, (4 * BLOCK, 1), 0)
    col = jnp.full((4 * BLOCK, 1), sink_ref[0, heads[3]], F32)
    for slot in (2, 1, 0):
        col = jnp.where(row < (slot + 1) * BLOCK, sink_ref[0, heads[slot]], col)
    return col


def _band_scores(q_stack, keys, tab, first_block):
    s = _nt(q_stack, keys) * (HEAD_DIM ** -0.5) + tab
    if first_block is not None:
        col = lax.broadcasted_iota(jnp.int32, s.shape, 1)
        s = jnp.where(jnp.logical_and(first_block, col < BLOCK), -jnp.inf, s)
    return s


def _softmax_with_sink(s, sink):
    m = jnp.maximum(jnp.max(s, axis=-1, keepdims=True), sink)
    e = jnp.exp(s - m)
    e_sink = jnp.exp(sink - m)
    den = jnp.sum(e, axis=-1, keepdims=True) + e_sink
    return e / den, e_sink / den


def _band_probs(q_stack, keys, tab, sink, first_block):
    return _softmax_with_sink(_band_scores(q_stack, keys, tab, first_block), sink)


def _attn_specs(n_groups):
    group = lambda n: (jnp.minimum(n, n_groups - 1), 0)
    prev = lambda n: (jnp.maximum(jnp.minimum(n, n_groups - 1) * ATTN_STEP_BLOCKS - 1, 0), 0)
    return group, prev


def _band(prev_ref, group_ref, b):
    rows = lambda i: group_ref[i * BLOCK:(i + 1) * BLOCK, :]
    band = jnp.concatenate([prev_ref[...] if b == 0 else rows(b - 1), rows(b)], axis=0)
    return band, pltpu.roll(band, HEAD_DIM, 1)


def _attn_fwd(qn, kn, v, tab, sinks):
    s = qn.shape[0]
    n_groups = s // (ATTN_STEP_BLOCKS * BLOCK)
    group, prev = _attn_specs(n_groups)
    rows = ATTN_STEP_BLOCKS * BLOCK

    def body(sink_ref, q_ref, kc_ref, kp_ref, vc_ref, vp_ref, tab_ref, o_ref):
        first = pl.program_id(0) == 0
        lo_mask = _lane_lo((BLOCK, BLOCK))
        for b in range(ATTN_STEP_BLOCKS):
            at = slice(b * BLOCK, (b + 1) * BLOCK)
            kk, kk_sw = _band(kp_ref, kc_ref, b)
            vv, vv_sw = _band(vp_ref, vc_ref, b)
            q_a, q_b = _stack_heads([q_ref[at, p * BLOCK:(p + 1) * BLOCK] for p in range(4)], lo_mask)
            no_prev = first if b == 0 else None
            p_a, _ = _band_probs(q_a, kk, tab_ref[0], _sink_column(sink_ref, HEADS_A), no_prev)
            p_b, _ = _band_probs(q_b, kk_sw, tab_ref[1], _sink_column(sink_ref, HEADS_B), no_prev)
            out = _unstack_heads(_nn(p_a.astype(BF16), vv), _nn(p_b.astype(BF16), vv_sw), lo_mask)
            for p in range(4):
                o_ref[at, p * BLOCK:(p + 1) * BLOCK] = out[p].astype(BF16)

    return _call(
        body,
        (sinks, qn, kn, kn, v, v, tab),
        name="attn_fwd",
        grid=(n_groups,),
        in_specs=[
            pl.BlockSpec(memory_space=pltpu.SMEM),
            pl.BlockSpec((rows, ATTN_WIDTH), group),
            pl.BlockSpec((rows, KV_WIDTH), group),
            pl.BlockSpec((BLOCK, KV_WIDTH), prev),
            pl.BlockSpec((rows, KV_WIDTH), group),
            pl.BlockSpec((BLOCK, KV_WIDTH), prev),
            _resident((2, 4 * BLOCK, 2 * BLOCK)),
        ],
        out_specs=[pl.BlockSpec((rows, ATTN_WIDTH), group)],
        out_shape=[jax.ShapeDtypeStruct((s, ATTN_WIDTH), BF16)],
    )


def _pooled(u_tile, u_halo, tile_index, tile_rows):
    halo = jnp.where(tile_index > 0, u_halo, 0.0)
    ext = jnp.concatenate([halo, u_tile], axis=0)
    sums = []
    acc = ext
    for shift in (1, 2, 4, 8):
        acc = acc + pltpu.roll(acc, shift, 0)
        sums.append(acc)
    t = tile_index * tile_rows + lax.broadcasted_iota(jnp.int32, (tile_rows, 1), 0)
    out = []
    for g, w in enumerate(POOL_SIZES):
        lanes = slice(g * POOL_GROUP, (g + 1) * POOL_GROUP)
        cnt = jnp.minimum(t + 1, w).astype(F32)
        out.append(sums[g][POOL_HALO:, lanes] / cnt - u_tile[:, lanes])
    return out


def _halo_before(tile):
    return lambda i: (jnp.maximum(i * (tile // POOL_HALO) - 1, 0), 0)


def _mix_out(u, a, x, w_out, w_pool, pool_scale, g_ffn):
    s = x.shape[0]
    ts = min(TOKEN_TILE, s)

    def body(u_ref, uh_ref, a_ref, x_ref, wo_ref, wp_ref, sc_ref, g_ref, h1_ref, hn_ref, m_ref):
        i = pl.program_id(0)
        pooled = _pooled(u_ref[...], uh_ref[...], i, ts)
        for g in range(len(POOL_SIZES)):
            lanes = slice(g * POOL_GROUP, (g + 1) * POOL_GROUP)
            y = _nn(pooled[g].astype(BF16), wp_ref[g].astype(BF16))
            m_ref[:, lanes] = (y * sc_ref[:, lanes]).astype(BF16)
        h1 = x_ref[...] + _nn(a_ref[...], wo_ref[:ATTN_WIDTH, :]) + _nn(m_ref[...], wo_ref[ATTN_WIDTH:, :])
        h1_ref[...] = h1
        hn_ref[...] = ((h1 * _rms(h1)) * g_ref[...]).astype(BF16)

    return _call(
        body,
        (u, u, a, x, w_out, w_pool, pool_scale, g_ffn),
        name="mix_out",
        grid=(s // ts,),
        in_specs=[
            _rows(ts, POOL_WIDTH),
            pl.BlockSpec((POOL_HALO, POOL_WIDTH), _halo_before(ts)),
            _rows(ts, ATTN_WIDTH),
            _rows(ts, D_MODEL),
            _resident((D_MODEL, D_MODEL)),
            _resident((len(POOL_SIZES), POOL_GROUP, POOL_GROUP)),
            _resident((1, POOL_WIDTH)),
            _resident((1, D_MODEL)),
        ],
        out_specs=[_rows(ts, D_MODEL), _rows(ts, D_MODEL), _rows(ts, POOL_WIDTH)],
        out_shape=[
            jax.ShapeDtypeStruct((s, D_MODEL), F32),
            jax.ShapeDtypeStruct((s, D_MODEL), BF16),
            jax.ShapeDtypeStruct((s, POOL_WIDTH), BF16),
        ],
    )


def _ffn_up(hn2, wg_t, wu_t):
    s = hn2.shape[0]
    ts = min(TOKEN_TILE, s)

    def body(hn_ref, wg_ref, wu_ref, gt_ref, up_ref):
        hn = hn_ref[...]
        for c in range(D_FF // FF_CHUNK):
            cols = slice(c * FF_CHUNK, (c + 1) * FF_CHUNK)
            gt_ref[:, cols] = _nt(hn, wg_ref[cols, :]).astype(BF16)
            up_ref[:, cols] = _nt(hn, wu_ref[cols, :]).astype(BF16)

    return _call(
        body,
        (hn2, wg_t, wu_t),
        name="ffn_up",
        grid=(s // ts,),
        in_specs=[_rows(ts, D_MODEL), _resident((D_FF, D_MODEL)), _resident((D_FF, D_MODEL))],
        out_specs=[_rows(ts, D_FF), _rows(ts, D_FF)],
        out_shape=[jax.ShapeDtypeStruct((s, D_FF), BF16), jax.ShapeDtypeStruct((s, D_FF), BF16)],
    )


def _silu_mul(gt, up):
    return (gt * jax.nn.sigmoid(gt)) * up


def _ffn_down_ple(gt, up, h1, w_down, p, target, g_ple, w_pg, w_pp):
    s = h1.shape[0]
    ts = min(TOKEN_TILE, s)
    blk = D_MODEL // N_DEV

    n_steps = s // ts
    ring = 3

    def body(gt_hbm, up_hbm, h1_ref, wd_ref, p_ref, t_ref, g_ref, wpg_ref, wpp_ref,
             loss_ref, dh_ref, dwpg_ref, dwpp_ref, dg_ref, gt_buf, up_buf, sems):
        step = pl.program_id(0)

        def fetch(tile, slot):
            rows = pl.ds(pl.multiple_of(tile * ts, ts), ts)
            return [pltpu.make_async_copy(src.at[rows, :], buf.at[slot], sems.at[k, slot])
                    for k, (src, buf) in enumerate(((gt_hbm, gt_buf), (up_hbm, up_buf)))]

        @pl.when(step == 0)
        def _():
            loss_ref[...] = jnp.zeros_like(loss_ref)
            dwpg_ref[...] = jnp.zeros_like(dwpg_ref)
            dwpp_ref[...] = jnp.zeros_like(dwpp_ref)
            dg_ref[...] = jnp.zeros_like(dg_ref)
            for tile in range(min(ring - 1, n_steps)):
                for cp in fetch(tile, tile):
                    cp.start()

        @pl.when(step + ring - 1 < n_steps)
        def _():
            for cp in fetch(step + ring - 1, (step + ring - 1) % ring):
                cp.start()

        slot = step % ring
        for cp in fetch(step, slot):
            cp.wait()
        gt_ref, up_ref = gt_buf.at[slot], up_buf.at[slot]
        h2v = h1_ref[...]
        for c in range(D_FF // FF_CHUNK):
            cols = slice(c * FF_CHUNK, (c + 1) * FF_CHUNK)
            act = _silu_mul(gt_ref[:, cols].astype(F32), up_ref[:, cols].astype(F32)).astype(BF16)
            h2v = _nn(act, wd_ref[cols, :]) + h2v
        r = _rms(h2v)
        hn = ((h2v * r) * g_ref[...]).astype(BF16)
        gate = jax.nn.sigmoid(_nn(hn, wpg_ref[...]))
        pb = p_ref[...].astype(BF16)
        pp = _nn(pb, jnp.concatenate([wpp_ref[j] for j in range(N_DEV)], axis=1))
        diff = (h2v + gate * pp) - t_ref[...]
        loss_ref[...] += jnp.sum(jnp.sum(diff * diff, axis=0, keepdims=True), axis=1, keepdims=True) * (0.5 / D_MODEL)
        dy = diff * (1.0 / D_MODEL)
        d_pp = (dy * gate).astype(BF16)
        d_pre = ((dy * pp) * (gate * (1.0 - gate))).astype(BF16)
        d_x, d_g = _rms_bwd(_nt(d_pre, wpg_ref[...]), h2v, r, g_ref[...])
        dg_ref[...] += d_g
        dh_ref[...] = dy + d_x
        d_wpp = _tn(pb, d_pp)
        for j in range(N_DEV):
            dwpp_ref[j] += d_wpp[:, j * blk:(j + 1) * blk]
        dwpg_ref[...] += _tn(hn, d_pre)

    return _call(
        body,
        (gt, up, h1, w_down, p, target, g_ple, w_pg, w_pp),
        name="ffn_down_ple",
        grid=(s // ts,),
        in_specs=[
            pl.BlockSpec(memory_space=pl.ANY),
            pl.BlockSpec(memory_space=pl.ANY),
            _rows(ts, D_MODEL),
            _resident((D_FF, D_MODEL)),
            _rows(ts, PLE_DIM),
            _rows(ts, D_MODEL),
            _resident((1, D_MODEL)),
            _resident((D_MODEL, D_MODEL)),
            _resident((N_DEV, PLE_DIM, blk)),
        ],
        out_specs=[
            _acc((1, SMALL_LANES)),
            _rows(ts, D_MODEL),
            _acc((D_MODEL, D_MODEL)),
            _acc((N_DEV, PLE_DIM, blk)),
            _acc((1, D_MODEL)),
        ],
        out_shape=[
            jax.ShapeDtypeStruct((1, SMALL_LANES), F32),
            jax.ShapeDtypeStruct((s, D_MODEL), F32),
            jax.ShapeDtypeStruct((D_MODEL, D_MODEL), F32),
            jax.ShapeDtypeStruct((N_DEV, PLE_DIM, blk), F32),
            jax.ShapeDtypeStruct((1, D_MODEL), F32),
        ],
        scratch_shapes=[pltpu.VMEM((ring, ts, D_FF), BF16), pltpu.VMEM((ring, ts, D_FF), BF16),
                        pltpu.SemaphoreType.DMA((2, ring))],
    )


def _ffn_bwd_act(dh2, h1, gt, up, g_ffn, wg_t, wu_t, w_down):
    s = h1.shape[0]
    ts = min(FFN_BWD_TILE, s)

    n_steps = s // ts
    ring = 3

    def body(dh_ref, h1_ref, gt_hbm, up_hbm, g_ref, wg_ref, wu_ref, wd_ref,
             dgt_ref, dup_ref, dh1_ref, dh1b_ref, dg_ref, dwd_ref, act_ref, gt_buf, up_buf, sems):
        step = pl.program_id(0)

        def fetch(tile, slot):
            rows = pl.ds(pl.multiple_of(tile * ts, ts), ts)
            return [pltpu.make_async_copy(src.at[rows, :], buf.at[slot], sems.at[k, slot])
                    for k, (src, buf) in enumerate(((gt_hbm, gt_buf), (up_hbm, up_buf)))]

        @pl.when(step == 0)
        def _():
            dg_ref[...] = jnp.zeros_like(dg_ref)
            dwd_ref[...] = jnp.zeros_like(dwd_ref)
            for tile in range(min(ring - 1, n_steps)):
                for cp in fetch(tile, tile):
                    cp.start()

        @pl.when(step + ring - 1 < n_steps)
        def _():
            for cp in fetch(step + ring - 1, (step + ring - 1) % ring):
                cp.start()

        slot = step % ring
        for cp in fetch(step, slot):
            cp.wait()
        gt_ref, up_ref = gt_buf.at[slot], up_buf.at[slot]
        dhb = dh_ref[...].astype(BF16)
        d_hn = jnp.zeros((ts, D_MODEL), F32)
        for c in range(D_FF // FF_CHUNK):
            cols = slice(c * FF_CHUNK, (c + 1) * FF_CHUNK)
            d_act = _nt(dhb, wd_ref[cols, :])
            gtv = gt_ref[:, cols].astype(F32)
            upv = up_ref[:, cols].astype(F32)
            sg = jax.nn.sigmoid(gtv)
            silu = gtv * sg
            act_ref[:, cols] = (silu * upv).astype(BF16)
            d_up = (d_act * silu).astype(BF16)
            d_gt = ((d_act * upv) * (sg * (1.0 + gtv * (1.0 - sg)))).astype(BF16)
            dup_ref[:, cols] = d_up
            dgt_ref[:, cols] = d_gt
            d_hn = (_nn(d_gt, wg_ref[cols, :]) + _nn(d_up, wu_ref[cols, :])) + d_hn
        dwd_ref[...] += _tn(act_ref[...], dhb)
        h1v = h1_ref[...]
        d_x, d_g = _rms_bwd(d_hn, h1v, _rms(h1v), g_ref[...])
        dg_ref[...] += d_g
        dh1 = dh_ref[...] + d_x
        dh1_ref[...] = dh1
        dh1b_ref[...] = dh1.astype(BF16)

    return _call(
        body,
        (dh2, h1, gt, up, g_ffn, wg_t, wu_t, w_down),
        name="ffn_bwd_act",
        grid=(s // ts,),
        in_specs=[
            _rows(ts, D_MODEL),
            _rows(ts, D_MODEL),
            pl.BlockSpec(memory_space=pl.ANY),
            pl.BlockSpec(memory_space=pl.ANY),
            _resident((1, D_MODEL)),
            _resident((D_FF, D_MODEL)),
            _resident((D_FF, D_MODEL)),
            _resident((D_FF, D_MODEL)),
        ],
        out_specs=[
            _rows(ts, D_FF), _rows(ts, D_FF),
            _rows(ts, D_MODEL), _rows(ts, D_MODEL), _acc((1, D_MODEL)), _acc((D_FF, D_MODEL)),
        ],
        out_shape=[
            jax.ShapeDtypeStruct((s, D_FF), BF16),
            jax.ShapeDtypeStruct((s, D_FF), BF16),
            jax.ShapeDtypeStruct((s, D_MODEL), F32),
            jax.ShapeDtypeStruct((s, D_MODEL), BF16),
            jax.ShapeDtypeStruct((1, D_MODEL), F32),
            jax.ShapeDtypeStruct((D_FF, D_MODEL), F32),
        ],
        scratch_shapes=[pltpu.VMEM((ts, D_FF), BF16), pltpu.VMEM((ring, ts, D_FF), BF16),
                        pltpu.VMEM((ring, ts, D_FF), BF16), pltpu.SemaphoreType.DMA((2, ring))],
    )


def _ffn_bwd_w(dgt, dup, hn2):
    s = hn2.shape[0]
    slab = pl.BlockSpec((s, FFN_W_SLAB), lambda i: (0, i))

    def body(dgt_ref, dup_ref, hn_ref, dwg_ref, dwu_ref):
        hn = hn_ref[...]
        dwg_ref[...] = _tn(dgt_ref[...], hn)
        dwu_ref[...] = _tn(dup_ref[...], hn)

    return _call(
        body,
        (dgt, dup, hn2),
        name="ffn_bwd_w",
        grid=(D_FF // FFN_W_SLAB,),
        in_specs=[slab, slab, _resident((s, D_MODEL))],
        out_specs=[_rows(FFN_W_SLAB, D_MODEL)] * 2,
        out_shape=[jax.ShapeDtypeStruct((D_FF, D_MODEL), F32)] * 2,
    )


def _mix_bwd(dh1b, u, a, m, w_out, w_pool, pool_scale):
    s = u.shape[0]
    ts = min(TOKEN_TILE, s)
    nt = s // ts
    halo_after = lambda i: (jnp.minimum((i + 1) * (ts // POOL_HALO), s // POOL_HALO - 1), 0)
    n_groups = len(POOL_SIZES)

    def body(dh_ref, dhn_ref, u_ref, uh_ref, a_ref, m_ref, wo_ref, wp_ref, sc_ref,
             da_ref, du_ref, dwp_ref, dsc_ref, dwo_ref):
        i = pl.program_id(0)

        @pl.when(i == 0)
        def _():
            dwp_ref[...] = jnp.zeros_like(dwp_ref)
            dsc_ref[...] = jnp.zeros_like(dsc_ref)
            dwo_ref[...] = jnp.zeros_like(dwo_ref)

        dh = dh_ref[...]
        dwo_ref[:ATTN_WIDTH, :] += _tn(a_ref[...], dh)
        dwo_ref[ATTN_WIDTH:, :] += _tn(m_ref[...], dh)
        da_ref[...] = _nt(dh, wo_ref[:ATTN_WIDTH, :])
        dh_next = jnp.where(i < nt - 1, dhn_ref[...], jnp.zeros_like(dhn_ref))
        dm_ext = _nt(jnp.concatenate([dh, dh_next], axis=0), wo_ref[ATTN_WIDTH:, :])
        pooled = _pooled(u_ref[...], uh_ref[...], i, ts)
        t_ext = i * ts + lax.broadcasted_iota(jnp.int32, (ts + POOL_HALO, 1), 0)
        for g, w in enumerate(POOL_SIZES):
            lanes = slice(g * POOL_GROUP, (g + 1) * POOL_GROUP)
            wp = wp_ref[g].astype(BF16)
            pg = pooled[g].astype(BF16)
            dm_g = dm_ext[:, lanes]
            dsc_ref[:, lanes] += jnp.sum(dm_g[:ts, :] * _nn(pg, wp), axis=0, keepdims=True)
            dy = (dm_g * sc_ref[:, lanes]).astype(BF16)
            dwp_ref[g] += _tn(pg, dy[:ts, :])
            d_pool = _nt(dy, wp)
            acc = d_pool / jnp.minimum(t_ext + 1, w).astype(F32)
            shift = 1
            while shift < w:
                acc = acc + pltpu.roll(acc, ts + POOL_HALO - shift, 0)
                shift *= 2
            du_ref[:, lanes] = (acc[:ts, :] - d_pool[:ts, :]).astype(BF16)

    return _call(
        body,
        (dh1b, dh1b, u, u, a, m, w_out, w_pool, pool_scale),
        name="mix_bwd",
        grid=(nt,),
        in_specs=[
            _rows(ts, D_MODEL),
            pl.BlockSpec((POOL_HALO, D_MODEL), halo_after),
            _rows(ts, POOL_WIDTH),
            pl.BlockSpec((POOL_HALO, POOL_WIDTH), _halo_before(ts)),
            _rows(ts, ATTN_WIDTH),
            _rows(ts, POOL_WIDTH),
            _resident((D_MODEL, D_MODEL)),
            _resident((n_groups, POOL_GROUP, POOL_GROUP)),
            _resident((1, POOL_WIDTH)),
        ],
        out_specs=[
            _rows(ts, ATTN_WIDTH),
            _rows(ts, POOL_WIDTH),
            _acc((n_groups, POOL_GROUP, POOL_GROUP)),
            _acc((1, POOL_WIDTH)),
            _acc((D_MODEL, D_MODEL)),
        ],
        out_shape=[
            jax.ShapeDtypeStruct((s, ATTN_WIDTH), F32),
            jax.ShapeDtypeStruct((s, POOL_WIDTH), BF16),
            jax.ShapeDtypeStruct((n_groups, POOL_GROUP, POOL_GROUP), F32),
            jax.ShapeDtypeStruct((1, POOL_WIDTH), F32),
            jax.ShapeDtypeStruct((D_MODEL, D_MODEL), F32),
        ],
    )


def _attn_bwd(qn, kn, v, a, da, tab, sinks):
    s = qn.shape[0]
    qb = ATTN_STEP_BLOCKS
    rows = qb * BLOCK
    n_groups = s // rows
    group, prev = _attn_specs(n_groups)
    done = lambda n: (jnp.maximum(n - 1, 0), 0)

    def body(sink_ref, q_ref, kc_ref, kp_ref, vc_ref, vp_ref, o_ref, do_ref, tab_ref,
             dq_ref, dk_ref, dv_ref, dl_ref, ds_ref, k_carry, v_carry, sink_acc):
        n = pl.program_id(0)

        @pl.when(n == 0)
        def _():
            dl_ref[...] = jnp.zeros_like(dl_ref)
            k_carry[...] = jnp.zeros_like(k_carry)
            v_carry[...] = jnp.zeros_like(v_carry)
            sink_acc[...] = jnp.zeros_like(sink_acc)

        @pl.when(n < n_groups)
        def _():
            first = n == 0
            lo_mask = _lane_lo((BLOCK, BLOCK))
            chains = [(b, half) for b in range(qb) for half in range(2)]
            tile = lambda ref, b, p: ref[b * BLOCK:(b + 1) * BLOCK, p * BLOCK:(p + 1) * BLOCK]
            keys = [_band(kp_ref, kc_ref, b) for b in range(qb)]
            vals = [_band(vp_ref, vc_ref, b) for b in range(qb)]
            q_st = [_stack_heads([tile(q_ref, b, p) for p in range(4)], lo_mask) for b in range(qb)]
            do_st = [_stack_heads([tile(do_ref, b, p) for p in range(4)], lo_mask) for b in range(qb)]
            o_st = [_stack_heads([tile(o_ref, b, p).astype(F32) for p in range(4)], lo_mask) for b in range(qb)]
            sink_col = [_sink_column(sink_ref, heads) for heads in (HEADS_A, HEADS_B)]
            scores = {(b, h): _band_scores(q_st[b][h], keys[b][h], tab_ref[h], first if b == 0 else None)
                      for b, h in chains}
            dob = {(b, h): do_st[b][h].astype(BF16) for b, h in chains}
            d_probs = {(b, h): _nt(dob[b, h], vals[b][h]) for b, h in chains}
            delta = {(b, h): jnp.sum(do_st[b][h] * o_st[b][h], axis=-1, keepdims=True) for b, h in chains}
            soft = {(b, h): _softmax_with_sink(scores[b, h], sink_col[h]) for b, h in chains}
            dl = {(b, h): soft[b, h][0] * (d_probs[b, h] - delta[b, h]) for b, h in chains}
            for b, h in chains:
                dl_ref[h] += dl[b, h]
                sink_acc[h] += soft[b, h][1] * delta[b, h]
            dsb = {(b, h): (dl[b, h] * (HEAD_DIM ** -0.5)).astype(BF16) for b, h in chains}
            dq_st = {(b, h): _nn(dsb[b, h], keys[b][h]) for b, h in chains}
            dk_parts = {(b, h): _tn(dsb[b, h], q_st[b][h]) for b, h in chains}
            dv_parts = {(b, h): _tn(soft[b, h][0].astype(BF16), dob[b, h]) for b, h in chains}
            for b in range(qb):
                dq = _unstack_heads(dq_st[b, 0], dq_st[b, 1], lo_mask)
                for p in range(4):
                    dq_ref[b * BLOCK:(b + 1) * BLOCK, p * BLOCK:(p + 1) * BLOCK] = dq[p]
            dks = [dk_parts[b, 0] + pltpu.roll(dk_parts[b, 1], HEAD_DIM, 1) for b in range(qb)]
            dvs = [dv_parts[b, 0] + pltpu.roll(dv_parts[b, 1], HEAD_DIM, 1) for b in range(qb)]
            last = slice((qb - 1) * BLOCK, qb * BLOCK)
            for parts, out_ref, carry in ((dks, dk_ref, k_carry), (dvs, dv_ref, v_carry)):
                out_ref[...] = carry[...]
                out_ref[last, :] += parts[0][:BLOCK, :]
                for b in range(qb):
                    own = parts[b][BLOCK:, :]
                    carry[b * BLOCK:(b + 1) * BLOCK, :] = own + parts[b + 1][:BLOCK, :] if b + 1 < qb else own

        @pl.when(n == n_groups)
        def _():
            dk_ref[...] = k_carry[...]
            dv_ref[...] = v_carry[...]
            for half, heads in enumerate((HEADS_A, HEADS_B)):
                for slot, h in enumerate(heads):
                    tot = jnp.sum(sink_acc[half, slot * BLOCK:(slot + 1) * BLOCK, :], axis=0, keepdims=True)
                    ds_ref[h:h + 1, :] = jnp.broadcast_to(-tot, (1, SMALL_LANES))

    return _call(
        body,
        (sinks, qn, kn, kn, v, v, a, da, tab),
        name="attn_bwd",
        grid=(n_groups + 1,),
        in_specs=[
            pl.BlockSpec(memory_space=pltpu.SMEM),
            pl.BlockSpec((rows, ATTN_WIDTH), group),
            pl.BlockSpec((rows, KV_WIDTH), group),
            pl.BlockSpec((BLOCK, KV_WIDTH), prev),
            pl.BlockSpec((rows, KV_WIDTH), group),
            pl.BlockSpec((BLOCK, KV_WIDTH), prev),
            pl.BlockSpec((rows, ATTN_WIDTH), group),
            pl.BlockSpec((rows, ATTN_WIDTH), group),
            _resident((2, 4 * BLOCK, 2 * BLOCK)),
        ],
        out_specs=[
            pl.BlockSpec((rows, ATTN_WIDTH), group),
            pl.BlockSpec((rows, KV_WIDTH), done),
            pl.BlockSpec((rows, KV_WIDTH), done),
            _acc((2, 4 * BLOCK, 2 * BLOCK)),
            _acc((N_DEV, SMALL_LANES)),
        ],
        out_shape=[
            jax.ShapeDtypeStruct((s, ATTN_WIDTH), F32),
            jax.ShapeDtypeStruct((s, KV_WIDTH), F32),
            jax.ShapeDtypeStruct((s, KV_WIDTH), F32),
            jax.ShapeDtypeStruct((2, 4 * BLOCK, 2 * BLOCK), F32),
            jax.ShapeDtypeStruct((N_DEV, SMALL_LANES), F32),
        ],
        scratch_shapes=[
            pltpu.VMEM((rows, KV_WIDTH), F32),
            pltpu.VMEM((rows, KV_WIDTH), F32),
            pltpu.VMEM((2, 4 * BLOCK, 1), F32),
        ],
    )


def _fold_heads(acc):
    t = acc + pltpu.roll(acc, HEAD_DIM, 1)
    out = t[:, :SMALL_LANES]
    for g in range(1, acc.shape[1] // SMALL_LANES):
        out = out + t[:, g * SMALL_LANES:(g + 1) * SMALL_LANES]
    return out


def _in_proj_bwd(dqn, dkn, dv, du, zqk, x, dh1, g_attn, gq_t, gk_t, w_in_t):
    s = x.shape[0]
    ts = min(TOKEN_TILE, s)
    nt = s // ts

    def head_norm_bwd(d_n, raw, g_t, bmat):
        r = lax.rsqrt(_seg_mean(raw * raw, bmat) + EPS)
        gy = d_n * g_t
        d_raw = r * gy - raw * (r * r * r) * _seg_mean(gy * raw, bmat)
        return d_raw, jnp.sum(d_n * (raw * r), axis=0, keepdims=True)

    def body(dqn_ref, dkn_ref, dv_ref, du_ref, zqk_ref, x_ref, dh1_ref, g_ref, gq_ref, gk_ref, w_ref, bq_ref, bk_ref,
             gx_ref, dw_ref, dg_ref, dgq_ref, dgk_ref, dz_ref, gq_acc, gk_acc):
        i = pl.program_id(0)

        @pl.when(i == 0)
        def _():
            dw_ref[...] = jnp.zeros_like(dw_ref)
            dg_ref[...] = jnp.zeros_like(dg_ref)
            gq_acc[...] = jnp.zeros_like(gq_acc)
            gk_acc[...] = jnp.zeros_like(gk_acc)

        d_q, d_gq = head_norm_bwd(dqn_ref[...], zqk_ref[:, :ATTN_WIDTH], gq_ref[...], bq_ref[...])
        d_k, d_gk = head_norm_bwd(dkn_ref[...], zqk_ref[:, ATTN_WIDTH:], gk_ref[...], bk_ref[...])
        gq_acc[...] += d_gq
        gk_acc[...] += d_gk
        dz_ref[:, :ATTN_WIDTH] = d_q.astype(BF16)
        dz_ref[:, ATTN_WIDTH:ATTN_WIDTH + KV_WIDTH] = d_k.astype(BF16)
        dz_ref[:, ATTN_WIDTH + KV_WIDTH:ATTN_WIDTH + 2 * KV_WIDTH] = dv_ref[...].astype(BF16)
        dz_ref[:, ATTN_WIDTH + 2 * KV_WIDTH:] = du_ref[...]
        dz = dz_ref[...]
        xf = x_ref[...]
        r = _rms(xf)
        hn = ((xf * r) * g_ref[...]).astype(BF16)
        d_x, d_g = _rms_bwd(_nn(dz, w_ref[...]), xf, r, g_ref[...])
        dg_ref[...] += d_g
        gx_ref[...] = dh1_ref[...] + d_x
        dw_ref[...] += _tn(dz, hn)

        @pl.when(i == nt - 1)
        def _():
            dgq_ref[...] = _fold_heads(gq_acc[...])
            dgk_ref[...] = _fold_heads(gk_acc[...])

    return _call(
        body,
        (dqn, dkn, dv, du, zqk, x, dh1, g_attn, gq_t, gk_t, w_in_t,
      _head_mean_matrix(ATTN_WIDTH), _head_mean_matrix(KV_WIDTH)),
        name="in_proj_bwd",
        grid=(nt,),
        in_specs=[
            _rows(ts, ATTN_WIDTH),
            _rows(ts, KV_WIDTH),
            _rows(ts, KV_WIDTH),
            _rows(ts, POOL_WIDTH),
            _rows(ts, ATTN_WIDTH + KV_WIDTH),
            _rows(ts, D_MODEL),
            _rows(ts, D_MODEL),
            _resident((1, D_MODEL)),
            _resident((1, ATTN_WIDTH)),
            _resident((1, KV_WIDTH)),
            _resident((IN_WIDTH, D_MODEL)),
            _resident((ATTN_WIDTH, ATTN_WIDTH)),
            _resident((KV_WIDTH, KV_WIDTH)),
        ],
        out_specs=[
            _rows(ts, D_MODEL),
            _acc((IN_WIDTH, D_MODEL)),
            _acc((1, D_MODEL)),
            _acc((1, SMALL_LANES)),
            _acc((1, SMALL_LANES)),
        ],
        out_shape=[
            jax.ShapeDtypeStruct((s, D_MODEL), F32),
            jax.ShapeDtypeStruct((IN_WIDTH, D_MODEL), F32),
            jax.ShapeDtypeStruct((1, D_MODEL), F32),
            jax.ShapeDtypeStruct((1, SMALL_LANES), F32),
            jax.ShapeDtypeStruct((1, SMALL_LANES), F32),
        ],
        scratch_shapes=[
            pltpu.VMEM((ts, IN_WIDTH), BF16),
            pltpu.VMEM((1, ATTN_WIDTH), F32),
            pltpu.VMEM((1, KV_WIDTH), F32),
        ],
    )


BIG_WEIGHTS = (
    ("w_in", True, IN_WIDTH // N_DEV, D_MODEL),
    ("w_out", False, D_MODEL // N_DEV, D_MODEL),
    ("w_gate", True, D_FF // N_DEV, D_MODEL),
    ("w_up", True, D_FF // N_DEV, D_MODEL),
    ("w_down", False, D_FF // N_DEV, D_MODEL),
    ("w_ple_gate", False, D_MODEL // N_DEV, D_MODEL),
    ("w_ple_proj", False, PLE_DIM, D_MODEL // N_DEV),
)
N_BIG = len(BIG_WEIGHTS)


def _place():
    x, y, c = lax.axis_index("x"), lax.axis_index("y"), lax.axis_index("c")
    chips = [(1 - x, y), (x, 1 - y), (1 - x, 1 - y)]
    return x, y, c, chips


class _Gather:
    def __init__(self, n):
        self.n = n
        self.sems = [pltpu.SemaphoreType.DMA((n, 7)), pltpu.SemaphoreType.DMA((n, 7)), pltpu.SemaphoreType.DMA((n,))]

    def _ctx(self, srcs, outs, sems):
        send_sems, recv_sems, local_sems = sems
        x, y, c, chips = _place()
        me, sibling = (x, y, c), (x, y, 1 - c)

        def block(k, owner):
            px, py, pc = owner
            return outs[k].at[4 * px + 2 * py + pc]

        def copy(k, idx, owner, to, mine=False):
            return pltpu.make_async_remote_copy(
                src_ref=srcs[k] if mine else block(k, owner), dst_ref=block(k, owner),
                send_sem=send_sems.at[k, idx], recv_sem=recv_sems.at[k, idx], device_id=to, device_id_type=MESH)

        def local(k):
            return pltpu.make_async_copy(srcs[k], block(k, me), local_sems.at[k])

        return c, chips, me, sibling, copy, local

    def begin(self, srcs, outs, sems):
        c, chips, me, sibling, copy, local = self._ctx(srcs, outs, sems)
        for k in range(self.n):
            local(k).start()
            copy(k, 0, me, sibling, mine=True).start()
            for j, chip in enumerate(chips):
                copy(k, 1 + j, me, (*chip, c), mine=True).start()

    def middle(self, srcs, outs, sems):
        c, chips, me, sibling, copy, local = self._ctx(srcs, outs, sems)
        for j, chip in enumerate(chips):
            for k in range(self.n):
                copy(k, 1 + j, (*chip, c), me).wait_recv()
                copy(k, 4 + j, (*chip, c), sibling).start()

    def end(self, srcs, outs, sems):
        c, chips, me, sibling, copy, local = self._ctx(srcs, outs, sems)
        for k in range(self.n):
            copy(k, 0, sibling, me).wait_recv()
            for j, chip in enumerate(chips):
                copy(k, 4 + j, (*chip, 1 - c), me).wait_recv()
        for k in range(self.n):
            copy(k, 0, me, sibling, mine=True).wait_send()
            for j, chip in enumerate(chips):
                copy(k, 1 + j, me, (*chip, c), mine=True).wait_send()
                copy(k, 4 + j, (*chip, c), sibling).wait_send()
            local(k).wait()


def _gather_rider(shards):
    g = _Gather(len(shards))
    shapes = [jax.ShapeDtypeStruct((N_DEV, *sh.shape), sh.dtype) for sh in shards]
    return _Rider(shards, shapes, g.sems, g.begin, g.end, g.middle)


def _cast_and_gather_first(shards, rel_bias_t):
    g = _Gather(1)
    any_spec = pl.BlockSpec(memory_space=pl.ANY)
    vmem = pl.BlockSpec(memory_space=pltpu.VMEM)

    def body(*refs):
        ins, rb_ref, outs = refs[:N_BIG], refs[N_BIG], refs[N_BIG + 1:2 * N_BIG + 1]
        gathered, tab_ref, sems = refs[2 * N_BIG + 1], refs[2 * N_BIG + 2], refs[2 * N_BIG + 3:]
        outs[0][...] = ins[0][...].astype(BF16)
        g.begin(outs[:1], [gathered], sems)
        for k in range(1, N_BIG):
            outs[k][...] = ins[k][...].astype(BF16)
        _write_bias_table(rb_ref, tab_ref)
        g.middle(outs[:1], [gathered], sems)
        g.end(outs[:1], [gathered], sems)

    res = pl.pallas_call(
        body,
        name="cast_and_gather_first",
        in_specs=[vmem] * N_BIG + [pl.BlockSpec(memory_space=pltpu.SMEM)],
        out_specs=[vmem] * N_BIG + [any_spec, vmem],
        out_shape=[jax.ShapeDtypeStruct((r, c), BF16) for _, _, r, c in BIG_WEIGHTS]
        + [jax.ShapeDtypeStruct((N_DEV, *BIG_WEIGHTS[0][2:]), BF16), jax.ShapeDtypeStruct(BIAS_TABLE_SHAPE, F32)],
        scratch_shapes=g.sems,
    )(*shards, rel_bias_t)
    return list(res[:N_BIG]), res[N_BIG], res[N_BIG + 1]


def _sibling_rider(grads):
    n = len(grads)

    def copies(gs, lands, sems):
        send_sems, recv_sems = sems
        x, y, c, _ = _place()
        return [
            pltpu.make_async_remote_copy(
                src_ref=gs[k].at[:, 1 - c], dst_ref=lands[k], send_sem=send_sems.at[k], recv_sem=recv_sems.at[k],
                device_id=(x, y, 1 - c), device_id_type=MESH)
            for k in range(n)
        ]

    def begin(gs, lands, sems):
        for cp in copies(gs, lands, sems):
            cp.start()

    def end(gs, lands, sems):
        for cp in copies(gs, lands, sems):
            cp.wait()

    shapes = [jax.ShapeDtypeStruct((N_CHIPS, *g.shape[2:]), F32) for g in grads]
    return _Rider(grads, shapes, [pltpu.SemaphoreType.DMA((n,)), pltpu.SemaphoreType.DMA((n,))], begin, end)


def _chip_of_relation(j, place):
    x, y = place[0], place[1]
    return jnp.where(j == 0, 2 * (1 - x) + y, jnp.where(j == 1, 2 * x + 1 - y, 2 * (1 - x) + 1 - y))


def _chip_sum(ks, place, grads, from_sibling):
    shapes = [BIG_WEIGHTS[k][2:] for k in ks]
    n_rel = N_CHIPS - 1
    per_step = n_rel if 2 * n_rel * sum(4 * r * c for r, c in shapes) <= CHIP_SUM_ONE_STEP_BYTES else 1
    operands, specs = [], []
    for (r, c), g, l in zip(shapes, grads, from_sibling):
        for q in range(per_step):
            chip = lambda j, place, q=q: _chip_of_relation(j * per_step + q, place)
            operands += [g, l]
            specs += [pl.BlockSpec((1, 1, r, c), lambda j, place, chip=chip: (chip(j, place), place[2], 0, 0)),
                      pl.BlockSpec((1, r, c), lambda j, place, chip=chip: (chip(j, place), 0, 0))]
    args, in_specs, _ = _after_last(operands, specs)

    def body(place_ref, *refs):
        ins, outs = refs[:len(operands)], refs[len(args):]
        for i in range(len(ks)):
            for q in range(per_step):
                mine_ref, sib_ref = ins[2 * (i * per_step + q):2 * (i * per_step + q) + 2]
                outs[i][q] = (mine_ref[0, 0] + sib_ref[0]).astype(BF16)

    outs = pl.pallas_call(
        body,
        name="chip_sum_" + "_".join(BIG_WEIGHTS[k][0] for k in ks),
        grid_spec=pltpu.PrefetchScalarGridSpec(
            num_scalar_prefetch=1,
            grid=(n_rel // per_step,),
            in_specs=in_specs,
            out_specs=[pl.BlockSpec((per_step, r, c), lambda j, place: (j, 0, 0)) for r, c in shapes],
        ),
        out_shape=[jax.ShapeDtypeStruct((n_rel, r, c), BF16) for r, c in shapes],
    )(place, *args)
    _mark_issued(outs[0])
    return list(outs)


def _chips_rider(to_send, small=None):
    n = len(to_send)
    inputs = list(to_send) + ([] if small is None else [small])
    shapes = [jax.ShapeDtypeStruct((3, *t.shape[1:]), BF16) for t in to_send]
    sems = [pltpu.SemaphoreType.DMA((max(n, 1), 3)), pltpu.SemaphoreType.DMA((max(n, 1), 3))]
    if small is not None:
        shapes.append(jax.ShapeDtypeStruct((N_DEV, *small.shape), F32))
        sems += [pltpu.SemaphoreType.DMA((7,)), pltpu.SemaphoreType.DMA((7,)), pltpu.SemaphoreType.DMA]

    def copies(ins, outs, sem_refs):
        x, y, c, chips = _place()
        out = []
        for k in range(n):
            for j, (px, py) in enumerate(chips):
                out.append(pltpu.make_async_remote_copy(
                    src_ref=ins[k].at[j], dst_ref=outs[k].at[j],
                    send_sem=sem_refs[0].at[k, j], recv_sem=sem_refs[1].at[k, j],
                    device_id=(px, py, c), device_id_type=MESH))
        local = None
        if small is not None:
            me = 4 * x + 2 * y + c
            local = pltpu.make_async_copy(ins[n], outs[n].at[me], sem_refs[4])
            rel = 0
            for fx in (0, 1):
                for fy in (0, 1):
                    for fc in (0, 1):
                        if (fx, fy, fc) != (0, 0, 0):
                            out.append(pltpu.make_async_remote_copy(
                                src_ref=ins[n], dst_ref=outs[n].at[me],
                                send_sem=sem_refs[2].at[rel], recv_sem=sem_refs[3].at[rel],
                                device_id=(x ^ fx, y ^ fy, c ^ fc), device_id_type=MESH))
                            rel += 1
        return out, local

    def begin(ins, outs, sem_refs):
        remote, local = copies(ins, outs, sem_refs)
        if local is not None:
            local.start()
        for cp in remote:
            cp.start()

    def end(ins, outs, sem_refs):
        remote, local = copies(ins, outs, sem_refs)
        for cp in remote:
            cp.wait()
        if local is not None:
            local.wait()

    return _Rider(inputs, shapes, sems, begin, end)


def _together(first, second):
    cut = lambda refs, a, b: (refs[:len(a)], refs[len(a):len(a) + len(b)])

    def run(which):
        def fn(ins, outs, sems):
            parts = zip((first, second), cut(ins, first.inputs, second.inputs),
                        cut(outs, first.out_shapes, second.out_shapes), cut(sems, first.sems, second.sems))
            for rider, i, o, s in parts:
                hook = getattr(rider, which)
                if hook is not None:
                    hook(i, o, s)
        return fn

    return _Rider(first.inputs + second.inputs, first.out_shapes + second.out_shapes, first.sems + second.sems,
                  run("begin"), run("end"), run("middle"))


PEER_SETS = {"sibling": 1, "chips": 2, "sibling+chips": 3, "all": 4}


def _peers(pattern):
    x, y, c, chips = _place()
    sibling, others = [(x, y, 1 - c)], [(*chip, c) for chip in chips]
    if pattern == "all":
        return sibling + others + [(*chip, 1 - c) for chip in chips]
    return {"sibling": sibling, "chips": others, "sibling+chips": sibling + others}[pattern]


def _on_sequencer(name, pattern, rider):
    n_in, n_out = len(rider.inputs), len(rider.out_shapes)

    def body(*refs):
        ins, outs, sems = refs[:n_in], refs[n_in:n_in + n_out], refs[n_in + n_out:]
        peers = _peers(pattern)
        barrier = pltpu.get_barrier_semaphore()
        for peer in peers:
            pl.semaphore_signal(barrier, inc=1, device_id=peer, device_id_type=MESH)
        pl.semaphore_wait(barrier, len(peers))
        rider.begin(ins, outs, sems)
        if rider.middle is not None:
            rider.middle(ins, outs, sems)
        rider.end(ins, outs, sems)

    outs = pl.kernel(
        body,
        name=name,
        out_type=tuple(rider.out_shapes),
        mesh=plsc.ScalarSubcoreMesh(axis_name="sequencer", num_cores=1),
        scratch_types=tuple(rider.sems),
        compiler_params=pltpu.CompilerParams(collective_id=PEER_SETS[pattern]),
    )(*rider.inputs)
    return list(outs)


def _adamw(w, g, m, v):
    m = ADAM_B1 * m + (1.0 - ADAM_B1) * g
    v = ADAM_B2 * v + (1.0 - ADAM_B2) * jnp.square(g)
    m_hat = m / (1.0 - ADAM_B1 ** ADAM_STEP)
    v_hat = v / (1.0 - ADAM_B2 ** ADAM_STEP)
    delta = -ADAM_LR * (m_hat / (jnp.sqrt(v_hat) + ADAM_EPS) + ADAM_WD * w)
    return delta, m, v


def _adamw_big(ks, place, operands):
    n = len(ks)
    tiles = lambda i, place: (i, 0)
    in_specs, out_specs, out_shape = [], [], []
    for k in ks:
        _, _, r, c = BIG_WEIGHTS[k]
        tile = r // 2
        in_specs += [
            pl.BlockSpec((1, 1, tile, c), lambda i, place: (2 * place[0] + place[1], place[2], i, 0)),
            pl.BlockSpec((1, tile, c), lambda i, place: (2 * place[0] + place[1], i, 0)),
            pl.BlockSpec((3, tile, c), lambda i, place: (0, i, 0)),
        ] + [pl.BlockSpec((tile, c), tiles)] * 3
        out_specs += [pl.BlockSpec((tile, c), tiles)] * 4
        out_shape += [jax.ShapeDtypeStruct((r, c), F32)] * 4
    args, in_specs, _ = _after_last(sum((list(ops) for ops in operands), []), in_specs)

    def body(place_ref, *refs):
        ins, outs = refs[:6 * n], refs[len(args):]
        for i in range(n):
            mine_ref, sib_ref, land_ref, w_ref, m_ref, v_ref = ins[6 * i:6 * i + 6]
            g_ref, d_ref, nm_ref, nv_ref = outs[4 * i:4 * i + 4]
            g = mine_ref[0, 0] + sib_ref[0]
            g = ((g + land_ref[0].astype(F32)) + land_ref[1].astype(F32)) + land_ref[2].astype(F32)
            g_ref[...] = g
            d_ref[...], nm_ref[...], nv_ref[...] = _adamw(w_ref[...], g, m_ref[...], v_ref[...])

    outs = pl.pallas_call(
        body,
        name="adamw_" + "_".join(BIG_WEIGHTS[k][0] for k in ks),
        grid_spec=pltpu.PrefetchScalarGridSpec(
            num_scalar_prefetch=1, grid=(2,), in_specs=in_specs, out_specs=out_specs),
        out_shape=out_shape,
    )(place, *args)
    _mark_issued(outs[0])
    return [outs[4 * i:4 * i + 4] for i in range(n)]


def _pack_small(arrays):
    rows, offsets = [], []
    at = 0
    for a in arrays:
        if a.ndim != 2 or a.shape[1] != SMALL_LANES or a.shape[0] % 8:
            flat = a.reshape(-1)
            n_rows = -(-flat.shape[0] // (8 * SMALL_LANES)) * 8
            a = jnp.pad(flat, (0, n_rows * SMALL_LANES - flat.shape[0])).reshape(n_rows, SMALL_LANES)
        rows.append(a)
        offsets.append(at)
        at += a.shape[0]
    return jnp.concatenate(rows, axis=0), offsets


def _unpack_small(tot, at, shape):
    r, c = shape
    if r % 8 == 0:
        return tot[at:at + r, :c]
    assert r == 1
    if c <= SMALL_LANES:
        return tot[at:at + 1, :c]
    return jnp.concatenate([tot[at + j:at + j + 1, :] for j in range(c // SMALL_LANES)], axis=1)


def _small_update(packs, loss_at, grads_at, ws, ms, vs):
    n, n_packs = len(ws), len(packs)

    def body(*refs):
        pack_refs, refs = refs[:n_packs], refs[n_packs:]
        w_refs, m_refs, v_refs, loss_ref, outs = refs[:n], refs[n:2 * n], refs[2 * n:3 * n], refs[3 * n], refs[3 * n + 1:]
        tots = []
        for p_ref in pack_refs:
            tot = p_ref[0]
            for j in range(1, N_DEV):
                tot = tot + p_ref[j]
            tots.append(tot)
        loss_ref[...] = _unpack_small(tots[loss_at[0]], loss_at[1], (1, 1))
        for i, (pack, at) in enumerate(grads_at):
            g = _unpack_small(tots[pack], at, w_refs[i].shape)
            outs[i][...] = g
            outs[n + i][...], outs[2 * n + i][...], outs[3 * n + i][...] = _adamw(
                w_refs[i][...], g, m_refs[i][...], v_refs[i][...])

    shapes = [jax.ShapeDtypeStruct(w.shape, F32) for w in ws]
    outs = pl.pallas_call(body, name="small_update", out_shape=[jax.ShapeDtypeStruct((1, 1), F32)] + shapes * 4)(
        *packs, *ws, *ms, *vs)
    return outs[0], outs[1:]


SMALL_NAMES = ("g_attn_norm", "g_q", "g_k", "attn_sinks", "rel_bias", "w_pool", "pool_scale", "g_ffn_norm", "g_ple_norm")


def kernel(x, p, w_in, w_out, g_attn_norm, g_q, g_k, attn_sinks, rel_bias, w_pool, pool_scale, g_ffn_norm, w_gate, w_up, w_down, g_ple_norm, w_ple_gate, w_ple_proj, loss_target, m_w_in, m_w_out, m_g_attn_norm, m_g_q, m_g_k, m_attn_sinks, m_rel_bias, m_w_pool, m_pool_scale, m_g_ffn_norm, m_w_gate, m_w_up, m_w_down, m_g_ple_norm, m_w_ple_gate, m_w_ple_proj, v_w_in, v_w_out, v_g_attn_norm, v_g_q, v_g_k, v_attn_sinks, v_rel_bias, v_w_pool, v_pool_scale, v_g_ffn_norm, v_w_gate, v_w_up, v_w_down, v_g_ple_norm, v_w_ple_gate, v_w_ple_proj):
    weights = dict(w_in=w_in, w_out=w_out, g_attn_norm=g_attn_norm, g_q=g_q, g_k=g_k, attn_sinks=attn_sinks,
                   rel_bias=rel_bias, w_pool=w_pool, pool_scale=pool_scale, g_ffn_norm=g_ffn_norm, w_gate=w_gate,
                   w_up=w_up, w_down=w_down, g_ple_norm=g_ple_norm, w_ple_gate=w_ple_gate, w_ple_proj=w_ple_proj)
    m_in = dict(w_in=m_w_in, w_out=m_w_out, g_attn_norm=m_g_attn_norm, g_q=m_g_q, g_k=m_g_k, attn_sinks=m_attn_sinks,
                rel_bias=m_rel_bias, w_pool=m_w_pool, pool_scale=m_pool_scale, g_ffn_norm=m_g_ffn_norm, w_gate=m_w_gate,
                w_up=m_w_up, w_down=m_w_down, g_ple_norm=m_g_ple_norm, w_ple_gate=m_w_ple_gate, w_ple_proj=m_w_ple_proj)
    v_in = dict(w_in=v_w_in, w_out=v_w_out, g_attn_norm=v_g_attn_norm, g_q=v_g_q, g_k=v_g_k, attn_sinks=v_attn_sinks,
                rel_bias=v_rel_bias, w_pool=v_w_pool, pool_scale=v_pool_scale, g_ffn_norm=v_g_ffn_norm, w_gate=v_w_gate,
                w_up=v_w_up, w_down=v_w_down, g_ple_norm=v_g_ple_norm, w_ple_gate=v_w_ple_gate, w_ple_proj=v_w_ple_proj)

    _issued.clear()
    xs = x[0]
    ps = p[0, 0]
    target = loss_target[0]
    wp = w_pool[0]
    gq_t = jnp.tile(g_q, (1, ATTN_WIDTH // HEAD_DIM))
    gk_t = jnp.tile(g_k, (1, KV_WIDTH // HEAD_DIM))

    def to_blocks(k, arr):
        return jnp.swapaxes(arr[0], 0, 1) if BIG_WEIGHTS[k][1] else arr[0]

    def from_blocks(k, arr):
        return (jnp.swapaxes(arr, 0, 1) if BIG_WEIGHTS[k][1] else arr)[None]

    IN, OUT, GATE, UP, DOWN, PG, PP = range(N_BIG)
    full = lambda g: g.reshape(N_DEV * g.shape[1], g.shape[2])
    halves = lambda k, g: g.reshape(N_CHIPS, 2, *BIG_WEIGHTS[k][2:])
    place = jnp.stack([lax.axis_index("x"), lax.axis_index("y"), lax.axis_index("c")]).astype(jnp.int32)

    sh, w_in_g, tab = _cast_and_gather_first(
        [to_blocks(k, weights[name]) for k, (name, _, _, _) in enumerate(BIG_WEIGHTS)], rel_bias.T)
    w_in_t = full(w_in_g)

    (w_out_g,) = _on_sequencer("gather_out", "sibling+chips", _gather_rider([sh[OUT]]))
    wg_g, wu_g = _on_sequencer("gather_gate_up", "sibling+chips", _gather_rider([sh[GATE], sh[UP]]))
    wd_g, w_pg_g, w_pp_g = _on_sequencer("gather_down_ple", "sibling+chips", _gather_rider([sh[DOWN], sh[PG], sh[PP]]))
    (zqk, qn, kn, v, u) = _in_proj(xs, g_attn_norm, w_in_t, gq_t, gk_t)
    (a,) = _attn_fwd(qn, kn, v, tab, attn_sinks)
    w_out_f = full(w_out_g)
    (h1, hn2, m_out) = _mix_out(u, a, xs, w_out_f, wp, pool_scale, g_ffn_norm)
    wg_t, wu_t = full(wg_g), full(wu_g)
    (gt, up) = _ffn_up(hn2, wg_t, wu_t)
    w_down_f = full(wd_g)

    partial, from_sibling, sums, landed = [None] * N_BIG, [None] * N_BIG, [None] * N_BIG, [None] * N_BIG

    def to_sibling(name, ks, grads):
        for k, g in zip(ks, grads):
            partial[k] = halves(k, g)
        got = _on_sequencer(name, "sibling", _sibling_rider([partial[k] for k in ks]))
        for k, g in zip(ks, got):
            from_sibling[k] = g

    def chip_sum(*ks):
        for k, s in zip(ks, _chip_sum(ks, place, [partial[k] for k in ks], [from_sibling[k] for k in ks])):
            sums[k] = s

    def to_chips(name, ks, small=None):
        got = _on_sequencer(name, "chips" if small is None else "all", _chips_rider([sums[k] for k in ks], small))
        for k, g in zip(ks, got):
            landed[k] = g
        return got[len(ks):]

    (loss_part, dh2, d_wpg, d_wpp, d_g_ple) = _ffn_down_ple(
        gt, up, h1, w_down_f, ps, target, g_ple_norm, full(w_pg_g), w_pp_g)
    to_sibling("sibling_ple", (PG, PP), (d_wpg, d_wpp))
    (dgt, dup, dh1, dh1b, d_g_ffn, d_wd) = _ffn_bwd_act(dh2, h1, gt, up, g_ffn_norm, wg_t, wu_t, w_down_f)
    to_sibling("sibling_down", (DOWN,), (d_wd,))
    chip_sum(PG, PP)
    to_chips("chips_ple", (PG, PP))
    chip_sum(DOWN)
    to_chips("chips_down", (DOWN,))
    (d_wg_t, d_wu_t) = _ffn_bwd_w(dgt, dup, hn2)
    to_sibling("sibling_gate_up", (GATE, UP), (d_wg_t, d_wu_t))
    _complete_before_next([landed[PG], landed[PP], landed[DOWN]])
    (da, du, d_wpool, d_scale, d_wo) = _mix_bwd(dh1b, u, a, m_out, w_out_f, wp, pool_scale)
    to_sibling("sibling_out", (OUT,), (d_wo,))
    chip_sum(GATE, UP)
    to_chips("chips_gate_up", (GATE, UP))
    (dqn, dkn, dv, dl_acc, d_sinks) = _attn_bwd(qn, kn, v, a, da, tab, attn_sinks)
    chip_sum(OUT)
    early, early_at = _pack_small([d_wpool.reshape(POOL_WIDTH, POOL_GROUP), d_scale, d_g_ffn, d_g_ple, loss_part[:, :1]])
    landed[OUT], early_all = _on_sequencer(
        "chips_out", "sibling+chips", _together(_chips_rider([sums[OUT]]), _gather_rider([early])))
    (grad_x, d_win_t, d_g_attn, d_gq, d_gk) = _in_proj_bwd(dqn, dkn, dv, du, zqk, xs, dh1, g_attn_norm, gq_t, gk_t, w_in_t)
    to_sibling("sibling_in", (IN,), (d_win_t,))
    _complete_before_next([landed[OUT], landed[GATE], landed[UP], early_all])
    (d_rel_t,) = _bias_table_bwd(dl_acc)
    chip_sum(IN)
    late, late_at = _pack_small([d_g_attn, d_gq[:, :HEAD_DIM], d_gk[:, :HEAD_DIM], d_sinks[:, 0], d_rel_t])
    (late_all,) = to_chips("chips_in", (IN,), late)

    out = {"grad": {}, "delta": {}, "new_m": {}, "new_v": {}}
    for ks in ((PG, PP, DOWN), (OUT, GATE, UP), (IN,)):
        names = [BIG_WEIGHTS[k][0] for k in ks]
        results = _adamw_big(ks, place, [
            (partial[k], from_sibling[k], landed[k], to_blocks(k, weights[n]), to_blocks(k, m_in[n]),
             to_blocks(k, v_in[n])) for k, n in zip(ks, names)])
        for k, name, res in zip(ks, names, results):
            for kind, r in zip(("grad", "delta", "new_m", "new_v"), res):
                out[kind][name] = from_blocks(k, r)
    def as_rows(name, arr):
        return arr.T if name == "rel_bias" else arr.reshape(POOL_WIDTH, POOL_GROUP) if name == "w_pool" else arr

    def from_rows(name, arr):
        return arr.T if name == "rel_bias" else arr.reshape(w_pool.shape) if name == "w_pool" else arr

    grads_at = dict(w_pool=(0, early_at[0]), pool_scale=(0, early_at[1]), g_ffn_norm=(0, early_at[2]),
                    g_ple_norm=(0, early_at[3]), g_attn_norm=(1, late_at[0]), g_q=(1, late_at[1]), g_k=(1, late_at[2]),
                    attn_sinks=(1, late_at[3]), rel_bias=(1, late_at[4]))
    loss, updates = _small_update(
        [early_all, late_all], (0, early_at[4]), [grads_at[n] for n in SMALL_NAMES],
        [as_rows(n, weights[n]) for n in SMALL_NAMES], [as_rows(n, m_in[n]) for n in SMALL_NAMES],
        [as_rows(n, v_in[n]) for n in SMALL_NAMES])
    loss = loss.reshape(())
    n_small = len(SMALL_NAMES)
    for j, kind in enumerate(("grad", "delta", "new_m", "new_v")):
        for i, name in enumerate(SMALL_NAMES):
            out[kind][name] = from_rows(name, updates[j * n_small + i])

    _issued.clear()
    order = ("w_in", "w_out", "g_attn_norm", "g_q", "g_k", "attn_sinks", "rel_bias", "w_pool", "pool_scale",
             "g_ffn_norm", "w_gate", "w_up", "w_down", "g_ple_norm", "w_ple_gate", "w_ple_proj")
    return (loss, grad_x[None], *[out["grad"][n] for n in order], *[out["delta"][n] for n in order],
            *[out["new_m"][n] for n in order], *[out["new_v"][n] for n in order])
```
